```python
import jax, jax.numpy as jnp
from jax import lax
import numpy as np

D_MODEL = 1024
BATCH = 8
SEQ = 8192
DEPTH = 1

EPS = 1e-6
CONV_WIDTH = D_MODEL
CONV_K = 3
GDN_HEADS = 8
GDN_DK = 128
GDN_DV = 128
GDN_CONV_K = 4
GDN_CHUNK = 64
GDN_QK = GDN_HEADS * GDN_DK
GDN_VW = GDN_HEADS * GDN_DV
D_FF = 256 * ((8 * D_MODEL // 3 + 255) // 256)
FFN_CONV_K = 3
IN_SPLITS = (CONV_WIDTH, CONV_WIDTH, CONV_WIDTH, GDN_QK, GDN_QK, GDN_VW, GDN_VW, GDN_HEADS, GDN_HEADS, D_MODEL, D_MODEL)
IN_WIDTH = sum(IN_SPLITS)

kernel_name = "hybrid_shortconv_gdn_gated_merge_convffn"


def _split_points():
    pts, acc = [], 0
    for w in IN_SPLITS[:-1]:
        acc += w
        pts.append(acc)
    return pts


def rmsnorm(x, g):
    xf = x.astype(jnp.float32)
    y = xf * lax.rsqrt(jnp.mean(xf * xf, axis=-1, keepdims=True) + EPS)
    return (y * g.astype(jnp.float32)).astype(x.dtype)


def l2norm(x):
    return x * lax.rsqrt(jnp.sum(x * x, axis=-1, keepdims=True) + EPS)


def causal_dwconv(x, w):
    K = w.shape[0]
    S = x.shape[1]
    xp = jnp.pad(x, ((0, 0), (K - 1, 0), (0, 0)))
    return sum(xp[:, j:j + S] * w[j] for j in range(K))


def short_conv_branch(bg, cg, xv, conv_w):
    return bg * causal_dwconv(cg * xv, conv_w)


def gated_delta_rule_chunked(q, k, v, g, beta):
    Bsz, S, H, dk = q.shape
    dv = v.shape[-1]
    C = GDN_CHUNK
    N = S // C

    def chunks(t):
        return t.reshape(Bsz, N, C, H, *t.shape[3:]).swapaxes(2, 3)

    q, k, v, g, beta = (chunks(t) for t in (q, k, v, g, beta))
    G = jnp.cumsum(g, axis=-1)
    tril = jnp.tril(jnp.ones((C, C), dtype=bool))
    strict = jnp.tril(jnp.ones((C, C), dtype=bool), -1)
    decay = jnp.exp(jnp.where(tril, G[..., :, None] - G[..., None, :], -jnp.inf))
    k_beta = k * beta[..., None]
    v_beta = v * beta[..., None]
    L = jnp.where(strict, jnp.einsum('bnhid,bnhjd->bnhij', k_beta, k) * decay, 0.0)
    eye = jnp.eye(C, dtype=q.dtype)
    rhs = jnp.concatenate([v_beta, k_beta * jnp.exp(G)[..., None]], axis=-1)
    sol = lax.linalg.triangular_solve(eye + L, rhs, left_side=True, lower=True)
    u, w = sol[..., :dv], sol[..., dv:]
    A_qk = jnp.einsum('bnhid,bnhjd->bnhij', q, k) * decay
    q_dec = q * jnp.exp(G)[..., None]
    G_last = G[..., -1:]
    k_dec = k * jnp.exp(G_last - G)[..., None]

    def step(state, inp):
        q_c, k_c, u_c, w_c, A_c, gl = inp
        v_new = u_c - jnp.einsum('bhcd,bhde->bhce', w_c, state)
        o = jnp.einsum('bhcd,bhde->bhce', q_c, state) + jnp.einsum('bhij,bhje->bhie', A_c, v_new)
        state = state * jnp.exp(gl)[..., None] + jnp.einsum('bhcd,bhce->bhde', k_c, v_new)
        return state, o

    xs = tuple(jnp.moveaxis(t, 1, 0) for t in (q_dec, k_dec, u, w, A_qk, G_last))
    state0 = jnp.zeros((Bsz, H, dk, dv), dtype=q.dtype)
    _, o = lax.scan(step, state0, xs)
    return o.transpose(1, 0, 3, 2, 4).reshape(Bsz, S, H, dv)


def gdn_branch(q, k, v, z, a, b, conv_w, A_log, dt_bias, norm_g):
    Bsz, S, _ = q.shape
    dtype = q.dtype
    f32 = jnp.float32
    qkv = jax.nn.silu(causal_dwconv(jnp.concatenate([q, k, v], axis=-1), conv_w))
    q, k, v = jnp.split(qkv, [GDN_QK, 2 * GDN_QK], axis=-1)
    q = l2norm(q.reshape(Bsz, S, GDN_HEADS, GDN_DK).astype(f32)) * (GDN_DK ** -0.5)
    k = l2norm(k.reshape(Bsz, S, GDN_HEADS, GDN_DK).astype(f32))
    v = v.reshape(Bsz, S, GDN_HEADS, GDN_DV).astype(f32)
    beta = jax.nn.sigmoid(b.astype(f32))
    g = -jnp.exp(A_log.astype(f32)) * jax.nn.softplus(a.astype(f32) + dt_bias.astype(f32))
    o = gated_delta_rule_chunked(q, k, v, g, beta)
    o = o * lax.rsqrt(jnp.mean(o * o, axis=-1, keepdims=True) + EPS) * norm_g.astype(f32)
    o = o * jax.nn.silu(z.reshape(Bsz, S, GDN_HEADS, GDN_DV).astype(f32))
    return o.reshape(Bsz, S, GDN_VW).astype(dtype)


def conv_ffn(h, w_up, conv_w, w_down):
    up = causal_dwconv(h @ w_up, conv_w)
    gate, val = jnp.split(up, [D_FF], axis=-1)
    return (jax.nn.silu(gate) * val) @ w_down


def _fwd_setup_inputs(seed: int = 0) -> dict:
    key = jax.random.key(seed)
    ks = jax.random.split(key, 16)
    f32 = jnp.float32
    nrm = lambda k, shape, scale: jax.random.normal(k, shape, f32) * scale
    x = jax.random.normal(ks[0], (BATCH, SEQ, D_MODEL), f32)
    norm_mix_g = 1.0 + nrm(ks[1], (DEPTH, D_MODEL), 0.02)
    w_in = nrm(ks[2], (DEPTH, D_MODEL, IN_WIDTH), D_MODEL ** -0.5)
    conv_a_w = nrm(ks[3], (DEPTH, CONV_K, CONV_WIDTH), CONV_K ** -0.5)
    gdn_conv_w = nrm(ks[4], (DEPTH, GDN_CONV_K, 2 * GDN_QK + GDN_VW), GDN_CONV_K ** -0.5)
    gdn_A_log = jnp.log(jax.random.uniform(ks[5], (DEPTH, GDN_HEADS), f32, 1.0, 16.0))
    dt = jnp.exp(jax.random.uniform(ks[6], (DEPTH, GDN_HEADS), f32, np.log(1e-3), np.log(1e-1)))
    gdn_dt_bias = dt + jnp.log(-jnp.expm1(-dt))
    gdn_norm_g = 1.0 + nrm(ks[7], (DEPTH, GDN_DV), 0.02)
    w_a_out = nrm(ks[8], (DEPTH, CONV_WIDTH, D_MODEL), CONV_WIDTH ** -0.5)
    w_b_out = nrm(ks[9], (DEPTH, GDN_VW, D_MODEL), GDN_VW ** -0.5)
    w_o = nrm(ks[10], (DEPTH, D_MODEL, D_MODEL), D_MODEL ** -0.5)
    norm_ffn_g = 1.0 + nrm(ks[11], (DEPTH, D_MODEL), 0.02)
    w_up = nrm(ks[12], (DEPTH, D_MODEL, 2 * D_FF), D_MODEL ** -0.5)
    ffn_conv_w = nrm(ks[13], (DEPTH, FFN_CONV_K, 2 * D_FF), FFN_CONV_K ** -0.5)
    w_down = nrm(ks[14], (DEPTH, D_FF, D_MODEL), D_FF ** -0.5)
    norm_final_g = 1.0 + nrm(ks[15], (D_MODEL,), 0.02)
    return {"x": x, "norm_mix_g": norm_mix_g, "w_in": w_in, "conv_a_w": conv_a_w,
            "gdn_conv_w": gdn_conv_w, "gdn_A_log": gdn_A_log, "gdn_dt_bias": gdn_dt_bias,
            "gdn_norm_g": gdn_norm_g, "w_a_out": w_a_out, "w_b_out": w_b_out, "w_o": w_o,
            "norm_ffn_g": norm_ffn_g, "w_up": w_up, "ffn_conv_w": ffn_conv_w, "w_down": w_down,
            "norm_final_g": norm_final_g}


def _fwd_reference(x, norm_mix_g, w_in, conv_a_w, gdn_conv_w, gdn_A_log, gdn_dt_bias, gdn_norm_g,
              w_a_out, w_b_out, w_o, norm_ffn_g, w_up, ffn_conv_w, w_down, norm_final_g):
    pts = _split_points()
    for l in range(DEPTH):
        h = rmsnorm(x, norm_mix_g[l])
        proj = h @ w_in[l]
        bg, cg, xv, q, k, v, z, a, b, ga, gb = jnp.split(proj, pts, axis=-1)
        y_a = short_conv_branch(bg, cg, xv, conv_a_w[l]) @ w_a_out[l]
        y_b = gdn_branch(q, k, v, z, a, b, gdn_conv_w[l], gdn_A_log[l], gdn_dt_bias[l],
                         gdn_norm_g[l]) @ w_b_out[l]
        mix = jax.nn.sigmoid(ga) * y_a + jax.nn.sigmoid(gb) * y_b
        x = x + mix @ w_o[l]
        h = rmsnorm(x, norm_ffn_g[l])
        x = x + conv_ffn(h, w_up[l], ffn_conv_w[l], w_down[l])
    return rmsnorm(x, norm_final_g)


import jax as _jax
import jax.numpy as _jnp

TWIN_FORMAT = 'train_step'
FWD_PARAMS = ['x', 'norm_mix_g', 'w_in', 'conv_a_w', 'gdn_conv_w', 'gdn_A_log', 'gdn_dt_bias', 'gdn_norm_g', 'w_a_out', 'w_b_out', 'w_o', 'norm_ffn_g', 'w_up', 'ffn_conv_w', 'w_down', 'norm_final_g']
TWIN_WEIGHTS = ['norm_mix_g', 'w_in', 'conv_a_w', 'gdn_conv_w', 'gdn_A_log', 'gdn_dt_bias', 'gdn_norm_g', 'w_a_out', 'w_b_out', 'w_o', 'norm_ffn_g', 'w_up', 'ffn_conv_w', 'w_down', 'norm_final_g']
TWIN_DIFF_INPUT = 'x'
TWIN_INPUTS = ['x', 'norm_mix_g', 'w_in', 'conv_a_w', 'gdn_conv_w', 'gdn_A_log', 'gdn_dt_bias', 'gdn_norm_g', 'w_a_out', 'w_b_out', 'w_o', 'norm_ffn_g', 'w_up', 'ffn_conv_w', 'w_down', 'norm_final_g', 'loss_target', 'm_norm_mix_g', 'm_w_in', 'm_conv_a_w', 'm_gdn_conv_w', 'm_gdn_A_log', 'm_gdn_dt_bias', 'm_gdn_norm_g', 'm_w_a_out', 'm_w_b_out', 'm_w_o', 'm_norm_ffn_g', 'm_w_up', 'm_ffn_conv_w', 'm_w_down', 'm_norm_final_g', 'v_norm_mix_g', 'v_w_in', 'v_conv_a_w', 'v_gdn_conv_w', 'v_gdn_A_log', 'v_gdn_dt_bias', 'v_gdn_norm_g', 'v_w_a_out', 'v_w_b_out', 'v_w_o', 'v_norm_ffn_g', 'v_w_up', 'v_ffn_conv_w', 'v_w_down', 'v_norm_final_g']
TWIN_OUTPUTS = ['loss', 'grad_x', 'grad_norm_mix_g', 'grad_w_in', 'grad_conv_a_w', 'grad_gdn_conv_w', 'grad_gdn_A_log', 'grad_gdn_dt_bias', 'grad_gdn_norm_g', 'grad_w_a_out', 'grad_w_b_out', 'grad_w_o', 'grad_norm_ffn_g', 'grad_w_up', 'grad_ffn_conv_w', 'grad_w_down', 'grad_norm_final_g', 'delta_norm_mix_g', 'delta_w_in', 'delta_conv_a_w', 'delta_gdn_conv_w', 'delta_gdn_A_log', 'delta_gdn_dt_bias', 'delta_gdn_norm_g', 'delta_w_a_out', 'delta_w_b_out', 'delta_w_o', 'delta_norm_ffn_g', 'delta_w_up', 'delta_ffn_conv_w', 'delta_w_down', 'delta_norm_final_g', 'new_m_norm_mix_g', 'new_m_w_in', 'new_m_conv_a_w', 'new_m_gdn_conv_w', 'new_m_gdn_A_log', 'new_m_gdn_dt_bias', 'new_m_gdn_norm_g', 'new_m_w_a_out', 'new_m_w_b_out', 'new_m_w_o', 'new_m_norm_ffn_g', 'new_m_w_up', 'new_m_ffn_conv_w', 'new_m_w_down', 'new_m_norm_final_g', 'new_v_norm_mix_g', 'new_v_w_in', 'new_v_conv_a_w', 'new_v_gdn_conv_w', 'new_v_gdn_A_log', 'new_v_gdn_dt_bias', 'new_v_gdn_norm_g', 'new_v_w_a_out', 'new_v_w_b_out', 'new_v_w_o', 'new_v_norm_ffn_g', 'new_v_w_up', 'new_v_ffn_conv_w', 'new_v_w_down', 'new_v_norm_final_g']
TWIN_LEAF_KINDS = {'loss': 'loss', 'grad_x': 'grad_x', 'grad_norm_mix_g': 'grad_w', 'grad_w_in': 'grad_w', 'grad_conv_a_w': 'grad_w', 'grad_gdn_conv_w': 'grad_w', 'grad_gdn_A_log': 'grad_w', 'grad_gdn_dt_bias': 'grad_w', 'grad_gdn_norm_g': 'grad_w', 'grad_w_a_out': 'grad_w', 'grad_w_b_out': 'grad_w', 'grad_w_o': 'grad_w', 'grad_norm_ffn_g': 'grad_w', 'grad_w_up': 'grad_w', 'grad_ffn_conv_w': 'grad_w', 'grad_w_down': 'grad_w', 'grad_norm_final_g': 'grad_w', 'delta_norm_mix_g': 'delta_w', 'delta_w_in': 'delta_w', 'delta_conv_a_w': 'delta_w', 'delta_gdn_conv_w': 'delta_w', 'delta_gdn_A_log': 'delta_w', 'delta_gdn_dt_bias': 'delta_w', 'delta_gdn_norm_g': 'delta_w', 'delta_w_a_out': 'delta_w', 'delta_w_b_out': 'delta_w', 'delta_w_o': 'delta_w', 'delta_norm_ffn_g': 'delta_w', 'delta_w_up': 'delta_w', 'delta_ffn_conv_w': 'delta_w', 'delta_w_down': 'delta_w', 'delta_norm_final_g': 'delta_w', 'new_m_norm_mix_g': 'new_m', 'new_m_w_in': 'new_m', 'new_m_conv_a_w': 'new_m', 'new_m_gdn_conv_w': 'new_m', 'new_m_gdn_A_log': 'new_m', 'new_m_gdn_dt_bias': 'new_m', 'new_m_gdn_norm_g': 'new_m', 'new_m_w_a_out': 'new_m', 'new_m_w_b_out': 'new_m', 'new_m_w_o': 'new_m', 'new_m_norm_ffn_g': 'new_m', 'new_m_w_up': 'new_m', 'new_m_ffn_conv_w': 'new_m', 'new_m_w_down': 'new_m', 'new_m_norm_final_g': 'new_m', 'new_v_norm_mix_g': 'new_v', 'new_v_w_in': 'new_v', 'new_v_conv_a_w': 'new_v', 'new_v_gdn_conv_w': 'new_v', 'new_v_gdn_A_log': 'new_v', 'new_v_gdn_dt_bias': 'new_v', 'new_v_gdn_norm_g': 'new_v', 'new_v_w_a_out': 'new_v', 'new_v_w_b_out': 'new_v', 'new_v_w_o': 'new_v', 'new_v_norm_ffn_g': 'new_v', 'new_v_w_up': 'new_v', 'new_v_ffn_conv_w': 'new_v', 'new_v_w_down': 'new_v', 'new_v_norm_final_g': 'new_v'}


def _forward(args):
    return _fwd_reference(*[args[k] for k in FWD_PARAMS])


def _output_shape():
    def fwd():
        inp = _fwd_setup_inputs(0)
        return _fwd_reference(*[inp[k] for k in FWD_PARAMS])
    out = _jax.eval_shape(fwd)
    return out.shape, out.dtype

N_MICROBATCH = 1
ADAM_LR = 0.001
ADAM_B1 = 0.9
ADAM_B2 = 0.999
ADAM_EPS = 1e-08
ADAM_WD = 0.01
ADAM_STEP = 10
PER_EXAMPLE_BATCH_AXIS = {'x': 0, 'loss_target': 0}
SHARED_INPUTS = []
_WEIGHT_DTYPES = {'norm_mix_g': _jnp.float32, 'w_in': _jnp.float32, 'conv_a_w': _jnp.float32, 'gdn_conv_w': _jnp.float32, 'gdn_A_log': _jnp.float32, 'gdn_dt_bias': _jnp.float32, 'gdn_norm_g': _jnp.float32, 'w_a_out': _jnp.float32, 'w_b_out': _jnp.float32, 'w_o': _jnp.float32, 'norm_ffn_g': _jnp.float32, 'w_up': _jnp.float32, 'ffn_conv_w': _jnp.float32, 'w_down': _jnp.float32, 'norm_final_g': _jnp.float32}
MOMENT_SCALE = {'norm_mix_g': 2.594675e-01, 'w_in': 8.615651e-02, 'conv_a_w': 1.384411e-01, 'gdn_conv_w': 5.723982e-02, 'gdn_A_log': 5.864981e-01, 'gdn_dt_bias': 5.615495e-01, 'gdn_norm_g': 2.181437e-01, 'w_a_out': 1.267591e-01, 'w_b_out': 7.344618e-02, 'w_o': 1.468034e-01, 'norm_ffn_g': 1.779773e-01, 'w_up': 7.263230e-02, 'ffn_conv_w': 7.150956e-02, 'w_down': 1.186892e-01, 'norm_final_g': 6.396961e+01}


def _to_microbatches(a, axis):
    t = _jnp.moveaxis(a, axis, 0)
    t = t.reshape((N_MICROBATCH, t.shape[0] // N_MICROBATCH) + t.shape[1:])
    return _jnp.moveaxis(t, 1, axis + 1)


def setup_inputs(seed: int = 0) -> dict:
    inp = _fwd_setup_inputs(seed)
    key = _jax.random.fold_in(_jax.random.key(seed), 7919)
    shape, _ = _output_shape()
    out = dict(inp)
    out["loss_target"] = _jax.random.normal(_jax.random.fold_in(key, 0), shape, _jnp.float32)
    for i, name in enumerate(TWIN_WEIGHTS):
        w = inp[name].astype(_jnp.float32)
        if MOMENT_SCALE is None:
            s = _jnp.sqrt(_jnp.mean(_jnp.square(w)) + 1e-30)
        else:
            s = MOMENT_SCALE[name]
        km, kv = _jax.random.split(_jax.random.fold_in(key, i + 1))
        out[name] = w
        out["m_" + name] = s * _jax.random.normal(km, w.shape, _jnp.float32)
        out["v_" + name] = (s * s) * _jax.random.uniform(kv, w.shape, _jnp.float32, 0.5, 1.5)
    if N_MICROBATCH > 1:
        for name, axis in PER_EXAMPLE_BATCH_AXIS.items():
            out[name] = _to_microbatches(out[name], axis)
    return {'x': out['x'], 'norm_mix_g': out['norm_mix_g'], 'w_in': out['w_in'], 'conv_a_w': out['conv_a_w'], 'gdn_conv_w': out['gdn_conv_w'], 'gdn_A_log': out['gdn_A_log'], 'gdn_dt_bias': out['gdn_dt_bias'], 'gdn_norm_g': out['gdn_norm_g'], 'w_a_out': out['w_a_out'], 'w_b_out': out['w_b_out'], 'w_o': out['w_o'], 'norm_ffn_g': out['norm_ffn_g'], 'w_up': out['w_up'], 'ffn_conv_w': out['ffn_conv_w'], 'w_down': out['w_down'], 'norm_final_g': out['norm_final_g'], 'loss_target': out['loss_target'], 'm_norm_mix_g': out['m_norm_mix_g'], 'm_w_in': out['m_w_in'], 'm_conv_a_w': out['m_conv_a_w'], 'm_gdn_conv_w': out['m_gdn_conv_w'], 'm_gdn_A_log': out['m_gdn_A_log'], 'm_gdn_dt_bias': out['m_gdn_dt_bias'], 'm_gdn_norm_g': out['m_gdn_norm_g'], 'm_w_a_out': out['m_w_a_out'], 'm_w_b_out': out['m_w_b_out'], 'm_w_o': out['m_w_o'], 'm_norm_ffn_g': out['m_norm_ffn_g'], 'm_w_up': out['m_w_up'], 'm_ffn_conv_w': out['m_ffn_conv_w'], 'm_w_down': out['m_w_down'], 'm_norm_final_g': out['m_norm_final_g'], 'v_norm_mix_g': out['v_norm_mix_g'], 'v_w_in': out['v_w_in'], 'v_conv_a_w': out['v_conv_a_w'], 'v_gdn_conv_w': out['v_gdn_conv_w'], 'v_gdn_A_log': out['v_gdn_A_log'], 'v_gdn_dt_bias': out['v_gdn_dt_bias'], 'v_gdn_norm_g': out['v_gdn_norm_g'], 'v_w_a_out': out['v_w_a_out'], 'v_w_b_out': out['v_w_b_out'], 'v_w_o': out['v_w_o'], 'v_norm_ffn_g': out['v_norm_ffn_g'], 'v_w_up': out['v_w_up'], 'v_ffn_conv_w': out['v_ffn_conv_w'], 'v_w_down': out['v_w_down'], 'v_norm_final_g': out['v_norm_final_g']}


def _loss(weights, diff, rest, loss_target):
    with _jax.named_scope("forward"):
        args = {**rest, TWIN_DIFF_INPUT: diff, **{k: w.astype(_WEIGHT_DTYPES[k]) for k, w in weights.items()}}
        y = _forward(args)
    with _jax.named_scope("loss_head"):
        err = _jnp.square(y.astype(_jnp.float32) - loss_target)
        return 0.5 * _jnp.sum(_jnp.mean(err, axis=-1)) if err.ndim else 0.5 * err


def _adamw(w, g, m, v):
    m = ADAM_B1 * m + (1.0 - ADAM_B1) * g
    v = ADAM_B2 * v + (1.0 - ADAM_B2) * _jnp.square(g)
    m_hat = m / (1.0 - ADAM_B1 ** ADAM_STEP)
    v_hat = v / (1.0 - ADAM_B2 ** ADAM_STEP)
    delta = -ADAM_LR * (m_hat / (_jnp.sqrt(v_hat) + ADAM_EPS) + ADAM_WD * w)
    return delta, m, v


def reference(x, norm_mix_g, w_in, conv_a_w, gdn_conv_w, gdn_A_log, gdn_dt_bias, gdn_norm_g, w_a_out, w_b_out, w_o, norm_ffn_g, w_up, ffn_conv_w, w_down, norm_final_g, loss_target, m_norm_mix_g, m_w_in, m_conv_a_w, m_gdn_conv_w, m_gdn_A_log, m_gdn_dt_bias, m_gdn_norm_g, m_w_a_out, m_w_b_out, m_w_o, m_norm_ffn_g, m_w_up, m_ffn_conv_w, m_w_down, m_norm_final_g, v_norm_mix_g, v_w_in, v_conv_a_w, v_gdn_conv_w, v_gdn_A_log, v_gdn_dt_bias, v_gdn_norm_g, v_w_a_out, v_w_b_out, v_w_o, v_norm_ffn_g, v_w_up, v_ffn_conv_w, v_w_down, v_norm_final_g):
    given = dict(x=x, norm_mix_g=norm_mix_g, w_in=w_in, conv_a_w=conv_a_w, gdn_conv_w=gdn_conv_w, gdn_A_log=gdn_A_log, gdn_dt_bias=gdn_dt_bias, gdn_norm_g=gdn_norm_g, w_a_out=w_a_out, w_b_out=w_b_out, w_o=w_o, norm_ffn_g=norm_ffn_g, w_up=w_up, ffn_conv_w=ffn_conv_w, w_down=w_down, norm_final_g=norm_final_g, loss_target=loss_target, m_norm_mix_g=m_norm_mix_g, m_w_in=m_w_in, m_conv_a_w=m_conv_a_w, m_gdn_conv_w=m_gdn_conv_w, m_gdn_A_log=m_gdn_A_log, m_gdn_dt_bias=m_gdn_dt_bias, m_gdn_norm_g=m_gdn_norm_g, m_w_a_out=m_w_a_out, m_w_b_out=m_w_b_out, m_w_o=m_w_o, m_norm_ffn_g=m_norm_ffn_g, m_w_up=m_w_up, m_ffn_conv_w=m_ffn_conv_w, m_w_down=m_w_down, m_norm_final_g=m_norm_final_g, v_norm_mix_g=v_norm_mix_g, v_w_in=v_w_in, v_conv_a_w=v_conv_a_w, v_gdn_conv_w=v_gdn_conv_w, v_gdn_A_log=v_gdn_A_log, v_gdn_dt_bias=v_gdn_dt_bias, v_gdn_norm_g=v_gdn_norm_g, v_w_a_out=v_w_a_out, v_w_b_out=v_w_b_out, v_w_o=v_w_o, v_norm_ffn_g=v_norm_ffn_g, v_w_up=v_w_up, v_ffn_conv_w=v_ffn_conv_w, v_w_down=v_w_down, v_norm_final_g=v_norm_final_g)
    weights = {n: given[n] for n in TWIN_WEIGHTS}
    shared = {n: given[n] for n in SHARED_INPUTS}
    per_example = {n: given[n] for n in ['x']}
    grad_fn = _jax.value_and_grad(_loss, argnums=(0, 1))

    def one_microbatch(ex, loss_target):
        ex = dict(ex)
        diff = ex.pop(TWIN_DIFF_INPUT)
        return grad_fn(weights, diff, {**shared, **ex}, loss_target)

    if N_MICROBATCH == 1:
        loss, (grad_w, grad_x) = one_microbatch(per_example, given["loss_target"])
    else:
        def body(carry, xs):
            loss_sum, grad_sum = carry
            l_k, (gw_k, gx_k) = one_microbatch(xs[0], xs[1])
            with _jax.named_scope("update"):
                return (loss_sum + l_k, _jax.tree.map(_jnp.add, grad_sum, gw_k)), gx_k

        init = (_jnp.zeros((), _jnp.float32), _jax.tree.map(_jnp.zeros_like, weights))
        (loss, grad_w), grad_x = _jax.lax.scan(body, init, (per_example, given["loss_target"]))
    with _jax.named_scope("update"):
        delta_w, new_m, new_v = {}, {}, {}
        for n in TWIN_WEIGHTS:
            delta_w[n], new_m[n], new_v[n] = _adamw(weights[n], grad_w[n], given["m_" + n], given["v_" + n])
    return (loss, grad_x, *[grad_w[n] for n in TWIN_WEIGHTS], *[delta_w[n] for n in TWIN_WEIGHTS],
            *[new_m[n] for n in TWIN_WEIGHTS], *[new_v[n] for n in TWIN_WEIGHTS])
```

```python
import functools

import jax
import jax.numpy as jnp
from jax import lax
from jax.experimental import pallas as pl
from jax.experimental.pallas import tpu as pltpu

f32 = jnp.float32
bf16 = jnp.bfloat16

D = 1024
H = 8
DH = 128
CH = 64
DFF = 2816
NW1 = 9216
EPS = 1e-6
N_DEV = 8

ADAM_LR = 0.001
ADAM_B1 = 0.9
ADAM_B2 = 0.999
ADAM_EPS = 1e-08
ADAM_WD = 0.01
ADAM_STEP = 10

VMEM_LIMIT_BYTES = 48 * 1024 * 1024

R_IN, R_UP, R_SQ, R_DOWN = 1154, 704, 128, 352
R_MM = R_IN + R_UP + 3 * R_SQ + R_DOWN
R_MM_PAD = 2608
R_ALL = 2600
N_CONV = 3 * 128 + 4 * 384 + 3 * 704

_HI = lax.Precision.HIGHEST
MESH = pl.DeviceIdType.MESH


def _params(n_grid):
    return pltpu.CompilerParams(dimension_semantics=("arbitrary",) * n_grid, vmem_limit_bytes=VMEM_LIMIT_BYTES)


def _bdot(a, b):
    return jnp.dot(a.astype(bf16), b.astype(bf16), preferred_element_type=f32)


def _bdot_nt(a, b):
    return lax.dot_general(a.astype(bf16), b.astype(bf16), (((1,), (1,)), ((), ())), preferred_element_type=f32)


def _bdot_tn(a, b):
    return lax.dot_general(a.astype(bf16), b.astype(bf16), (((0,), (0,)), ((), ())), preferred_element_type=f32)


def _hdot(a, b):
    return jnp.dot(a, b, preferred_element_type=f32, precision=_HI)


def _sigmoid(x):
    return 1.0 / (1.0 + jnp.exp(-x))


def _softplus(x):
    return jnp.maximum(x, 0.0) + jnp.log(1.0 + jnp.exp(-jnp.abs(x)))


def _shift_down(x, halo, j):
    if j == 0:
        return x
    xr = pltpu.roll(x, j, 0)
    hr = pltpu.roll(halo, j, 0)
    r8 = lax.broadcasted_iota(jnp.int32, hr.shape, 0)
    top = jnp.where(r8 < j, hr, xr[:8])
    return jnp.concatenate([top, xr[8:]], axis=0)


def _shift_up(x, halo, j):
    if j == 0:
        return x
    n = x.shape[0]
    xr = pltpu.roll(x, n - j, 0)
    hr = pltpu.roll(halo, 8 - j, 0)
    r8 = lax.broadcasted_iota(jnp.int32, hr.shape, 0)
    bot = jnp.where(r8 >= 8 - j, hr, xr[n - 8:])
    return jnp.concatenate([xr[:n - 8], bot], axis=0)


def _conv_down(x, halo, w_ref, k):
    out = w_ref[k - 1:k, :] * x
    for j in range(k - 1):
        out = out + w_ref[j:j + 1, :] * _shift_down(x, halo, k - 1 - j)
    return out


def _conv_up(dy, halo, w_ref, k):
    out = w_ref[k - 1:k, :] * dy
    for j in range(k - 1):
        out = out + w_ref[j:j + 1, :] * _shift_up(dy, halo, k - 1 - j)
    return out


def _row(tb, w, col=0):
    return pl.BlockSpec((tb, w), lambda i: (i, col))


def _prev(tb, w, col=0):
    return pl.BlockSpec((8, w), lambda i: (jnp.maximum(i * (tb // 8) - 1, 0), col))


def _next(tb, w, n_rows, col=0):
    last = n_rows // 8 - 1
    return pl.BlockSpec((8, w), lambda i: (jnp.minimum((i + 1) * (tb // 8), last), col))


def _fixed(shape):
    return pl.BlockSpec(shape, lambda i: (0,) * len(shape))


def _first_zero(halo_ref):
    return jnp.where(pl.program_id(0) == 0, 0.0, halo_ref[...])


def _last_zero(halo_ref, n_blocks):
    return jnp.where(pl.program_id(0) == n_blocks - 1, 0.0, halo_ref[...])


def _pick(n, prefs):
    for p in prefs:
        if n % p == 0:
            return p
    return n


def _matmul(a, b, *, name, out_dtype=f32, add=None):
    m, kd = a.shape
    _, n = b.shape
    tm = _pick(m, (1024, 512, 256))
    tn = _pick(n, (1024, 1408, 512, 128))
    tk = _pick(kd, (1024, 1408, 512, 128)) if kd > 1408 else kd
    nk = kd // tk

    def body(*refs):
        if add is None:
            a_ref, b_ref, o_ref = refs[:3]
            add_ref = None
        else:
            a_ref, b_ref, add_ref, o_ref = refs[:4]
        part = jnp.dot(a_ref[...].astype(bf16), b_ref[...].astype(bf16), preferred_element_type=f32)
        if nk == 1:
            if add_ref is not None:
                part = part + add_ref[...]
            o_ref[...] = part.astype(out_dtype)
            return
        acc_ref = refs[-1]
        k = pl.program_id(2)

        @pl.when(k == 0)
        def _():
            acc_ref[...] = part

        @pl.when(k > 0)
        def _():
            acc_ref[...] += part

        @pl.when(k == nk - 1)
        def _():
            res = acc_ref[...]
            if add_ref is not None:
                res = res + add_ref[...]
            o_ref[...] = res.astype(out_dtype)

    in_specs = [pl.BlockSpec((tm, tk), lambda i, j, k: (i, k)), pl.BlockSpec((tk, tn), lambda i, j, k: (k, j))]
    args = [a, b]
    if add is not None:
        in_specs.append(pl.BlockSpec((tm, tn), lambda i, j, k: (i, j)))
        args.append(add)
    return pl.pallas_call(
        body, name=name, grid=(m // tm, n // tn, nk), in_specs=in_specs,
        out_specs=pl.BlockSpec((tm, tn), lambda i, j, k: (i, j)),
        out_shape=jax.ShapeDtypeStruct((m, n), out_dtype),
        scratch_shapes=[pltpu.VMEM((tm, tn), f32)] if nk > 1 else [],
        compiler_params=_params(3),
    )(*args)


def _matmul_tn(a, b, *, name):
    t, m = a.shape
    _, n = b.shape
    tm = _pick(m, (1024, 1408, 512, 128))
    tn = _pick(n, (1024, 1408, 512, 128))
    tt = _pick(t, (1024, 512, 256))
    nt = t // tt

    def body(a_ref, b_ref, o_ref):
        k = pl.program_id(2)
        part = lax.dot_general(a_ref[...].astype(bf16), b_ref[...].astype(bf16), (((0,), (0,)), ((), ())),
                               preferred_element_type=f32)

        @pl.when(k == 0)
        def _():
            o_ref[...] = part

        @pl.when(k > 0)
        def _():
            o_ref[...] += part

    return pl.pallas_call(
        body, name=name, grid=(m // tm, n // tn, nt),
        in_specs=[pl.BlockSpec((tt, tm), lambda i, j, k: (k, i)), pl.BlockSpec((tt, tn), lambda i, j, k: (k, j))],
        out_specs=pl.BlockSpec((tm, tn), lambda i, j, k: (i, j)),
        out_shape=jax.ShapeDtypeStruct((m, n), f32),
        compiler_params=_params(3),
    )(a, b)


def _rms_fwd(x, g, *, name):
    t = x.shape[0]
    tb = _pick(t, (256, 128))

    def body(x_ref, g_ref, h_ref):
        xv = x_ref[...]
        r = lax.rsqrt(jnp.mean(xv * xv, axis=-1, keepdims=True) + EPS)
        h_ref[...] = (xv * r * g_ref[...]).astype(bf16)

    return pl.pallas_call(
        body, name=name, grid=(t // tb,), in_specs=[_row(tb, D), _fixed((1, D))], out_specs=_row(tb, D),
        out_shape=jax.ShapeDtypeStruct((t, D), bf16), compiler_params=_params(1),
    )(x, g)


def _rms_bwd(dh, x, g, dres, *, name):
    t = x.shape[0]
    tb = _pick(t, (256, 128))

    def body(dh_ref, x_ref, g_ref, dres_ref, dx_ref, dxb_ref, dg_ref):
        xv = x_ref[...]
        r = lax.rsqrt(jnp.mean(xv * xv, axis=-1, keepdims=True) + EPS)
        xh = xv * r
        dy = dh_ref[...]
        dyg = dy * g_ref[...]
        dx = dres_ref[...] + r * (dyg - xh * jnp.mean(dyg * xh, axis=-1, keepdims=True))
        dx_ref[...] = dx
        dxb_ref[...] = dx.astype(bf16)

        @pl.when(pl.program_id(0) == 0)
        def _():
            dg_ref[...] = jnp.zeros_like(dg_ref)

        dg_ref[...] += jnp.sum((dy * xh).reshape(tb // 8, 8, D), axis=0)

    return pl.pallas_call(
        body, name=name, grid=(t // tb,),
        in_specs=[_row(tb, D), _row(tb, D), _fixed((1, D)), _row(tb, D)],
        out_specs=[_row(tb, D), _row(tb, D), _fixed((8, D))],
        out_shape=[jax.ShapeDtypeStruct((t, D), f32), jax.ShapeDtypeStruct((t, D), bf16),
                   jax.ShapeDtypeStruct((8, D), f32)],
        compiler_params=_params(1),
    )(dh, x, g, dres)


def _gdn_gates(ab, alog, dtb):
    lane = lax.broadcasted_iota(jnp.int32, ab.shape, 1)
    g = -jnp.exp(alog) * _softplus(ab + dtb)
    beta = _sigmoid(ab)
    return jnp.where(lane < H, g, jnp.where(lane < 2 * H, beta, 0.0))


def _pre_fwd(p1, p2, wa, wg, alog, dtb):
    t = p1.shape[0]
    tb = 128

    def body(p0_ref, p0h_ref, pq_ref, pqh_ref, p2_ref, wa_ref, wg_ref, alog_ref, dtb_ref,
             ya_ref, qn_ref, kn_ref, vc_ref, gb_ref):
        p0 = p0_ref[...]
        h0 = _first_zero(p0h_ref)
        u = p0[:, D:2 * D] * p0[:, 2 * D:]
        uh = h0[:, D:2 * D] * h0[:, 2 * D:]
        ya_ref[...] = (p0[:, :D] * _conv_down(u, uh, wa_ref, 3)).astype(bf16)
        s = _conv_down(pq_ref[...], _first_zero(pqh_ref), wg_ref, 4)
        s = s * _sigmoid(s)
        for h in range(H):
            q = s[:, h * DH:(h + 1) * DH]
            k = s[:, D + h * DH:D + (h + 1) * DH]
            qn_ref[:, h * DH:(h + 1) * DH] = q * (lax.rsqrt(jnp.sum(q * q, axis=-1, keepdims=True) + EPS) * DH ** -0.5)
            kn_ref[:, h * DH:(h + 1) * DH] = k * lax.rsqrt(jnp.sum(k * k, axis=-1, keepdims=True) + EPS)
        vc_ref[...] = s[:, 2 * D:]
        gb_ref[...] = _gdn_gates(p2_ref[...], alog_ref[...], dtb_ref[...])

    return pl.pallas_call(
        body, name="pre_fwd", grid=(t // tb,),
        in_specs=[_row(tb, 3 * D, 0), _prev(tb, 3 * D, 0), _row(tb, 3 * D, 1), _prev(tb, 3 * D, 1), _row(tb, 128),
                  _fixed((8, D)), _fixed((8, 3 * D)), _fixed((1, 128)), _fixed((1, 128))],
        out_specs=[_row(tb, D), _row(tb, D), _row(tb, D), _row(tb, D), _row(tb, 128)],
        out_shape=[jax.ShapeDtypeStruct((t, D), bf16), jax.ShapeDtypeStruct((t, D), f32),
                   jax.ShapeDtypeStruct((t, D), f32), jax.ShapeDtypeStruct((t, D), f32),
                   jax.ShapeDtypeStruct((t, 128), f32)],
        compiler_params=_params(1),
    )(p1, p1, p1, p1, p2, wa, wg, alog, dtb)


def _post_fwd(o, p1, gn):
    t = o.shape[0]
    tb = _pick(t, (256, 128))

    def body(o_ref, z_ref, gn_ref, yb_ref):
        for h in range(H):
            sl = slice(h * DH, (h + 1) * DH)
            oh = o_ref[:, sl]
            z = z_ref[:, sl]
            r = lax.rsqrt(jnp.mean(oh * oh, axis=-1, keepdims=True) + EPS)
            yb_ref[:, sl] = (oh * r * gn_ref[...] * (z * _sigmoid(z))).astype(bf16)

    return pl.pallas_call(
        body, name="post_fwd", grid=(t // tb,), in_specs=[_row(tb, D), _row(tb, D, 6), _fixed((1, DH))],
        out_specs=_row(tb, D), out_shape=jax.ShapeDtypeStruct((t, D), bf16), compiler_params=_params(1),
    )(o, p1, gn)


def _post_bwd(dyb, o, p1, gn):
    t = o.shape[0]
    tb = _pick(t, (256, 128))

    def body(dyb_ref, o_ref, z_ref, gn_ref, do_ref, dz_ref, dgn_ref):
        @pl.when(pl.program_id(0) == 0)
        def _():
            dgn_ref[...] = jnp.zeros_like(dgn_ref)

        gn_v = gn_ref[...]
        acc = jnp.zeros((8, DH), f32)
        for h in range(H):
            sl = slice(h * DH, (h + 1) * DH)
            oh = o_ref[:, sl]
            z = z_ref[:, sl]
            dy = dyb_ref[:, sl]
            r = lax.rsqrt(jnp.mean(oh * oh, axis=-1, keepdims=True) + EPS)
            on = oh * r
            sg = _sigmoid(z)
            sz = z * sg
            don = dy * sz
            dz_ref[:, sl] = (dy * on * gn_v * (sg * (1.0 + z * (1.0 - sg)))).astype(bf16)
            acc = acc + jnp.sum((don * on).reshape(tb // 8, 8, DH), axis=0)
            doh = don * gn_v
            do_ref[:, sl] = r * (doh - on * jnp.mean(doh * on, axis=-1, keepdims=True))
        dgn_ref[...] += acc

    return pl.pallas_call(
        body, name="post_bwd", grid=(t // tb,),
        in_specs=[_row(tb, D), _row(tb, D), _row(tb, D, 6), _fixed((1, DH))],
        out_specs=[_row(tb, D), _row(tb, D), _fixed((8, DH))],
        out_shape=[jax.ShapeDtypeStruct((t, D), f32), jax.ShapeDtypeStruct((t, D), bf16),
                   jax.ShapeDtypeStruct((8, DH), f32)],
        compiler_params=_params(1),
    )(dyb, o, p1, gn)


def _mix_fwd(ya, yb, p1):
    t = ya.shape[0]
    tb = _pick(t, (256, 128))

    def body(ya_ref, yb_ref, ga_ref, gb_ref, mix_ref):
        mix_ref[...] = (_sigmoid(ga_ref[...]) * ya_ref[...] + _sigmoid(gb_ref[...]) * yb_ref[...]).astype(bf16)

    return pl.pallas_call(
        body, name="mix_fwd", grid=(t // tb,), in_specs=[_row(tb, D), _row(tb, D), _row(tb, D, 7), _row(tb, D, 8)],
        out_specs=_row(tb, D), out_shape=jax.ShapeDtypeStruct((t, D), bf16), compiler_params=_params(1),
    )(ya, yb, p1, p1)


def _mix_bwd(dmix, ya, yb, p1):
    t = ya.shape[0]
    tb = _pick(t, (256, 128))

    def body(dm_ref, ya_ref, yb_ref, ga_ref, gb_ref, dya_ref, dyb_ref, dg_ref):
        dm = dm_ref[...]
        sa = _sigmoid(ga_ref[...])
        sb = _sigmoid(gb_ref[...])
        dya_ref[...] = (dm * sa).astype(bf16)
        dyb_ref[...] = (dm * sb).astype(bf16)
        dg_ref[:, :D] = (dm * ya_ref[...] * sa * (1.0 - sa)).astype(bf16)
        dg_ref[:, D:] = (dm * yb_ref[...] * sb * (1.0 - sb)).astype(bf16)

    return pl.pallas_call(
        body, name="mix_bwd", grid=(t // tb,),
        in_specs=[_row(tb, D), _row(tb, D), _row(tb, D), _row(tb, D, 7), _row(tb, D, 8)],
        out_specs=[_row(tb, D), _row(tb, D), _row(tb, 2 * D)],
        out_shape=[jax.ShapeDtypeStruct((t, D), bf16), jax.ShapeDtypeStruct((t, D), bf16),
                   jax.ShapeDtypeStruct((t, 2 * D), bf16)],
        compiler_params=_params(1),
    )(dmix, ya, yb, p1, p1)


def _ffn_fwd(up, wf):
    t = up.shape[0]
    tb = 128

    def body(up_ref, uph_ref, wf_ref, act_ref):
        c = _conv_down(up_ref[...], _first_zero(uph_ref), wf_ref, 3)
        gate = c[:, :DFF]
        act_ref[...] = (gate * _sigmoid(gate) * c[:, DFF:]).astype(bf16)

    return pl.pallas_call(
        body, name="ffn_fwd", grid=(t // tb,), in_specs=[_row(tb, 2 * DFF), _prev(tb, 2 * DFF), _fixed((8, 2 * DFF))],
        out_specs=_row(tb, DFF), out_shape=jax.ShapeDtypeStruct((t, DFF), bf16), compiler_params=_params(1),
    )(up, up, wf)


def _ffn_bwd1(dact, up, wf):
    t = up.shape[0]
    tb = 128

    def body(da_ref, up_ref, uph_ref, wf_ref, dc_ref, dw_ref):
        @pl.when(pl.program_id(0) == 0)
        def _():
            dw_ref[...] = jnp.zeros_like(dw_ref)

        upv = up_ref[...]
        uph = _first_zero(uph_ref)
        c = _conv_down(upv, uph, wf_ref, 3)
        gate = c[:, :DFF]
        val = c[:, DFF:]
        sg = _sigmoid(gate)
        da = da_ref[...]
        dgate = da * val * (sg * (1.0 + gate * (1.0 - sg)))
        dval = da * (gate * sg)
        dc_ref[:, :DFF] = dgate
        dc_ref[:, DFF:] = dval
        dc = jnp.concatenate([dgate, dval], axis=1)
        for j in range(3):
            dw_ref[j:j + 1, :] += jnp.sum(dc * _shift_down(upv, uph, 2 - j), axis=0, keepdims=True)

    return pl.pallas_call(
        body, name="ffn_bwd1", grid=(t // tb,),
        in_specs=[_row(tb, DFF), _row(tb, 2 * DFF), _prev(tb, 2 * DFF), _fixed((8, 2 * DFF))],
        out_specs=[_row(tb, 2 * DFF), _fixed((8, 2 * DFF))],
        out_shape=[jax.ShapeDtypeStruct((t, 2 * DFF), f32), jax.ShapeDtypeStruct((8, 2 * DFF), f32)],
        compiler_params=_params(1),
    )(dact, up, up, wf)


def _ffn_bwd2(dc, wf):
    t = dc.shape[0]
    tb = 128
    nb = t // tb

    def body(dc_ref, dch_ref, wf_ref, dup_ref):
        dup_ref[...] = _conv_up(dc_ref[...], _last_zero(dch_ref, nb), wf_ref, 3).astype(bf16)

    return pl.pallas_call(
        body, name="ffn_bwd2", grid=(nb,), in_specs=[_row(tb, 2 * DFF), _next(tb, 2 * DFF, t), _fixed((8, 2 * DFF))],
        out_specs=_row(tb, 2 * DFF), out_shape=jax.ShapeDtypeStruct((t, 2 * DFF), bf16), compiler_params=_params(1),
    )(dc, dc, wf)


def _final(x3, tgt, g):
    t = x3.shape[0]
    tb = _pick(t, (256, 128))

    def body(x_ref, t_ref, g_ref, loss_ref, dx_ref, dxb_ref, dg_ref):
        @pl.when(pl.program_id(0) == 0)
        def _():
            loss_ref[...] = jnp.zeros_like(loss_ref)
            dg_ref[...] = jnp.zeros_like(dg_ref)

        xv = x_ref[...]
        r = lax.rsqrt(jnp.mean(xv * xv, axis=-1, keepdims=True) + EPS)
        xh = xv * r
        gv = g_ref[...]
        e = xh * gv - t_ref[...]
        lrow = 0.5 * jnp.mean(e * e, axis=-1, keepdims=True)
        loss_ref[...] += jnp.sum(jnp.broadcast_to(lrow, (tb, 128)).reshape(tb // 8, 8, 128), axis=0)
        dy = e * (1.0 / D)
        dyg = dy * gv
        dx = r * (dyg - xh * jnp.mean(dyg * xh, axis=-1, keepdims=True))
        dx_ref[...] = dx
        dxb_ref[...] = dx.astype(bf16)
        dg_ref[...] += jnp.sum((dy * xh).reshape(tb // 8, 8, D), axis=0)

    return pl.pallas_call(
        body, name="final", grid=(t // tb,), in_specs=[_row(tb, D), _row(tb, D), _fixed((1, D))],
        out_specs=[_fixed((8, 128)), _row(tb, D), _row(tb, D), _fixed((8, D))],
        out_shape=[jax.ShapeDtypeStruct((8, 128), f32), jax.ShapeDtypeStruct((t, D), f32),
                   jax.ShapeDtypeStruct((t, D), bf16), jax.ShapeDtypeStruct((8, D), f32)],
        compiler_params=_params(1),
    )(x3, tgt, g)


def _pre_bwd1(p1, p2, dya_in, dqn, dkn, dvc, dgb, gbeta, wa, wg, alog, dtb):
    t = p1.shape[0]
    tb = 128

    def body(p0_ref, p0h_ref, pq_ref, pqh_ref, p2_ref, dya_ref, dqn_ref, dkn_ref, dvc_ref, dgb_ref, gb_ref,
             wa_ref, wg_ref, alog_ref, dtb_ref,
             dbg_ref, dca_ref, dc4_ref, dp2_ref, dwa_ref, dwg_ref, dal_ref, ddt_ref):
        @pl.when(pl.program_id(0) == 0)
        def _():
            dwa_ref[...] = jnp.zeros_like(dwa_ref)
            dwg_ref[...] = jnp.zeros_like(dwg_ref)
            dal_ref[...] = jnp.zeros_like(dal_ref)
            ddt_ref[...] = jnp.zeros_like(ddt_ref)

        p0 = p0_ref[...]
        h0 = _first_zero(p0h_ref)
        u = p0[:, D:2 * D] * p0[:, 2 * D:]
        uh = h0[:, D:2 * D] * h0[:, 2 * D:]
        dya = dya_ref[...]
        dbg_ref[...] = (dya * _conv_down(u, uh, wa_ref, 3)).astype(bf16)
        dca = dya * p0[:, :D]
        dca_ref[...] = dca
        for j in range(3):
            dwa_ref[j:j + 1, :] += jnp.sum(dca * _shift_down(u, uh, 2 - j), axis=0, keepdims=True)

        pq = pq_ref[...]
        pqh = _first_zero(pqh_ref)
        c4 = _conv_down(pq, pqh, wg_ref, 4)
        sg = _sigmoid(c4)
        s = c4 * sg
        dsilu = sg * (1.0 + c4 * (1.0 - sg))
        for h in range(H):
            for base, d_ref, scale in ((0, dqn_ref, DH ** -0.5), (D, dkn_ref, 1.0)):
                sl = slice(base + h * DH, base + (h + 1) * DH)
                a = s[:, sl]
                r = lax.rsqrt(jnp.sum(a * a, axis=-1, keepdims=True) + EPS)
                an = a * r
                dn = d_ref[:, h * DH:(h + 1) * DH] * scale
                dc4_ref[:, sl] = r * (dn - an * jnp.sum(dn * an, axis=-1, keepdims=True)) * dsilu[:, sl]
        dc4_ref[:, 2 * D:] = dvc_ref[...] * dsilu[:, 2 * D:]
        dc4 = dc4_ref[...]
        for j in range(4):
            dwg_ref[j:j + 1, :] += jnp.sum(dc4 * _shift_down(pq, pqh, 3 - j), axis=0, keepdims=True)

        ab = p2_ref[...]
        lane = lax.broadcasted_iota(jnp.int32, ab.shape, 1)
        dgbv = dgb_ref[...]
        gbv = gb_ref[...]
        da = dgbv * (-jnp.exp(alog_ref[...])) * _sigmoid(ab + dtb_ref[...])
        db = dgbv * gbv * (1.0 - gbv)
        dp2_ref[...] = jnp.where(lane < H, da, jnp.where(lane < 2 * H, db, 0.0)).astype(bf16)
        dal = jnp.where(lane < H, dgbv * gbv, 0.0)
        ddt = jnp.where(lane < H, da, 0.0)
        dal_ref[...] += jnp.sum(dal.reshape(tb // 8, 8, 128), axis=0)
        ddt_ref[...] += jnp.sum(ddt.reshape(tb // 8, 8, 128), axis=0)

    return pl.pallas_call(
        body, name="pre_bwd1", grid=(t // tb,),
        in_specs=[_row(tb, 3 * D, 0), _prev(tb, 3 * D, 0), _row(tb, 3 * D, 1), _prev(tb, 3 * D, 1), _row(tb, 128),
                  _row(tb, D), _row(tb, D), _row(tb, D), _row(tb, D), _row(tb, 128), _row(tb, 128),
                  _fixed((8, D)), _fixed((8, 3 * D)), _fixed((1, 128)), _fixed((1, 128))],
        out_specs=[_row(tb, D), _row(tb, D), _row(tb, 3 * D), _row(tb, 128),
                   _fixed((8, D)), _fixed((8, 3 * D)), _fixed((8, 128)), _fixed((8, 128))],
        out_shape=[jax.ShapeDtypeStruct((t, D), bf16), jax.ShapeDtypeStruct((t, D), f32),
                   jax.ShapeDtypeStruct((t, 3 * D), f32), jax.ShapeDtypeStruct((t, 128), bf16),
                   jax.ShapeDtypeStruct((8, D), f32), jax.ShapeDtypeStruct((8, 3 * D), f32),
                   jax.ShapeDtypeStruct((8, 128), f32), jax.ShapeDtypeStruct((8, 128), f32)],
        compiler_params=_params(1),
    )(p1, p1, p1, p1, p2, dya_in, dqn, dkn, dvc, dgb, gbeta, wa, wg, alog, dtb)


def _pre_bwd2(dca, dc4, p1, dbg, dz, dgates, wa, wg):
    t = p1.shape[0]
    tb = 128
    nb = t // tb

    def body(dca_ref, dcah_ref, dc4_ref, dc4h_ref, p0_ref, dbg_ref, dz_ref, dgt_ref, wa_ref, wg_ref, dp_ref):
        du = _conv_up(dca_ref[...], _last_zero(dcah_ref, nb), wa_ref, 3)
        dp_ref[:, :D] = dbg_ref[...]
        dp_ref[:, D:2 * D] = (du * p0_ref[:, 2 * D:]).astype(bf16)
        dp_ref[:, 2 * D:3 * D] = (du * p0_ref[:, D:2 * D]).astype(bf16)
        dp_ref[:, 3 * D:6 * D] = _conv_up(dc4_ref[...], _last_zero(dc4h_ref, nb), wg_ref, 4).astype(bf16)
        dp_ref[:, 6 * D:7 * D] = dz_ref[...]
        dp_ref[:, 7 * D:] = dgt_ref[...]

    return pl.pallas_call(
        body, name="pre_bwd2", grid=(nb,),
        in_specs=[_row(tb, D), _next(tb, D, t), _row(tb, 3 * D), _next(tb, 3 * D, t), _row(tb, 3 * D, 0),
                  _row(tb, D), _row(tb, D), _row(tb, 2 * D), _fixed((8, D)), _fixed((8, 3 * D))],
        out_specs=_row(tb, NW1), out_shape=jax.ShapeDtypeStruct((t, NW1), bf16), compiler_params=_params(1),
    )(dca, dca, dc4, dc4, p1, dbg, dz, dgates, wa, wg)


def _chunk_consts():
    r = lax.broadcasted_iota(jnp.int32, (CH, CH), 0)
    c = lax.broadcasted_iota(jnp.int32, (CH, CH), 1)
    return r, c, (r == c).astype(f32)


def _tri_inverse(low, eye):
    xp = -low
    inv = eye + xp
    for _ in range(5):
        xp = _hdot(xp, xp)
        inv = inv + _hdot(inv, xp)
    return inv


def _chunk_common(q, k, v, gcol, bcol, r, c, eye):
    grow = jnp.sum(eye * gcol, axis=0, keepdims=True)
    dec = jnp.exp(jnp.where(r >= c, gcol - grow, -jnp.inf))
    rcol = lax.broadcasted_iota(jnp.int32, (CH, 1), 0)
    glast = jnp.sum(jnp.where(rcol == CH - 1, gcol, 0.0), axis=0, keepdims=True)
    eg = jnp.exp(gcol)
    el = jnp.exp(glast - gcol)
    kb = k * bcol
    vb = v * bcol
    kk = _bdot_nt(kb, k)
    low = jnp.where(r > c, kk * dec, 0.0)
    qk = _bdot_nt(q, k)
    att = qk * dec
    return grow, dec, glast, eg, el, kb, vb, kk, low, qk, att, rcol


def _gdn_fwd(qn, kn, vc, gbeta):
    t = qn.shape[0]
    n_chunks = t // CH

    def body(q_ref, k_ref, v_ref, gb_ref, o_ref, s_ref, t_ref, state):
        @pl.when(pl.program_id(0) == 0)
        def _():
            state[...] = jnp.zeros_like(state)

        r, c, eye = _chunk_consts()
        gb = gb_ref[...]
        gall = _hdot((r >= c).astype(f32), gb)
        for h in range(H):
            sl = slice(h * DH, (h + 1) * DH)
            q, k, v = q_ref[:, sl], k_ref[:, sl], v_ref[:, sl]
            gcol = gall[:, h:h + 1]
            bcol = gb[:, H + h:H + h + 1]
            _, _, glast, eg, el, kb, vb, _, low, _, att, _ = _chunk_common(q, k, v, gcol, bcol, r, c, eye)
            inv = _tri_inverse(low, eye)
            uw = _bdot(inv, jnp.concatenate([vb, kb * eg], axis=1))
            st = state[h]
            s_ref[0, h] = st.astype(bf16)
            t_ref[0, h] = inv
            vn = uw[:, :DH] - _bdot(uw[:, DH:], st)
            o_ref[:, sl] = _bdot(q * eg, st) + _bdot(att, vn)
            state[h] = st * jnp.exp(glast) + _bdot_tn(k * el, vn)

    return pl.pallas_call(
        body, name="gdn_fwd", grid=(n_chunks,),
        in_specs=[_row(CH, D), _row(CH, D), _row(CH, D), _row(CH, 128)],
        out_specs=[_row(CH, D), pl.BlockSpec((1, H, DH, DH), lambda i: (i, 0, 0, 0)),
                   pl.BlockSpec((1, H, CH, CH), lambda i: (i, 0, 0, 0))],
        out_shape=[jax.ShapeDtypeStruct((t, D), f32), jax.ShapeDtypeStruct((n_chunks, H, DH, DH), bf16),
                   jax.ShapeDtypeStruct((n_chunks, H, CH, CH), f32)],
        scratch_shapes=[pltpu.VMEM((H, DH, DH), f32)],
        compiler_params=_params(1),
    )(qn, kn, vc, gbeta)


def _gdn_bwd(qn, kn, vc, gbeta, do, s_all, t_all):
    t = qn.shape[0]
    n_chunks = t // CH

    def body(q_ref, k_ref, v_ref, gb_ref, do_ref, s_ref, t_ref, dq_ref, dk_ref, dv_ref, dgb_ref, dstate):
        @pl.when(pl.program_id(0) == 0)
        def _():
            dstate[...] = jnp.zeros_like(dstate)

        r, c, eye = _chunk_consts()
        tril = r >= c
        gb = gb_ref[...]
        gall = _hdot(tril.astype(f32), gb)
        lane = lax.broadcasted_iota(jnp.int32, (1, 128), 1)
        dg_acc = jnp.zeros((CH, 128), f32)
        db_acc = jnp.zeros((CH, 128), f32)
        for h in range(H):
            sl = slice(h * DH, (h + 1) * DH)
            q, k, v, dout = q_ref[:, sl], k_ref[:, sl], v_ref[:, sl], do_ref[:, sl]
            gcol = gall[:, h:h + 1]
            bcol = gb[:, H + h:H + h + 1]
            _, dec, glast, eg, el, kb, vb, kk, low, qk, att, rcol = _chunk_common(q, k, v, gcol, bcol, r, c, eye)
            inv = t_ref[0, h]
            st = s_ref[0, h]
            ds = dstate[h]
            elast = jnp.exp(glast)
            kbg = kb * eg
            uw = _bdot(inv, jnp.concatenate([vb, kbg], axis=1))
            u, w = uw[:, :DH], uw[:, DH:]
            vn = u - _bdot(w, st)
            qd = q * eg
            kd = k * el
            dvn = _bdot_tn(att, dout) + _bdot(kd, ds)
            dqd = _bdot_nt(dout, st)
            datt = jnp.where(tril, _bdot_nt(dout, vn), 0.0)
            dkd = _bdot_nt(vn, ds)
            dw = -_bdot_nt(dvn, st)
            dstate[h] = ds * elast + _bdot_tn(qd, dout) - _bdot_tn(w, dvn)
            stf = st.astype(f32)
            dglast = elast * jnp.sum(jnp.sum(stf * ds, axis=1, keepdims=True), axis=0, keepdims=True)
            dr = _bdot_tn(inv, jnp.concatenate([dvn, dw], axis=1))
            dvb, dkbg = dr[:, :DH], dr[:, DH:]
            dlow = -jnp.where(r > c, _bdot_nt(dvb, u) + _bdot_nt(dkbg, w), 0.0)
            dkk = dlow * dec
            dqk = datt * dec
            mm = dlow * low + datt * att
            dkb = _bdot(dkk, k) + dkbg * eg
            dk_ref[:, sl] = _bdot_tn(dkk, kb) + _bdot_tn(dqk, q) + dkd * el + dkb * bcol
            dq_ref[:, sl] = _bdot(dqk, k) + dqd * eg
            dv_ref[:, sl] = dvb * bcol
            dbeta = jnp.sum(dkb * k, axis=1, keepdims=True) + jnp.sum(dvb * v, axis=1, keepdims=True)
            deg = jnp.sum(dkbg * kb, axis=1, keepdims=True) + jnp.sum(dqd * q, axis=1, keepdims=True)
            delc = jnp.sum(dkd * k, axis=1, keepdims=True) * el
            colsum = jnp.sum(eye * jnp.sum(mm, axis=0, keepdims=True), axis=1, keepdims=True)
            dgc = jnp.sum(mm, axis=1, keepdims=True) - colsum + deg * eg - delc
            dgc = dgc + jnp.where(rcol == CH - 1, jnp.sum(delc, axis=0, keepdims=True) + dglast, 0.0)
            dg_acc = dg_acc + dgc * (lane == h).astype(f32)
            db_acc = db_acc + dbeta * (lane == H + h).astype(f32)
        dgb_ref[...] = _hdot((r <= c).astype(f32), dg_acc) + db_acc

    rev = lambda i: (n_chunks - 1 - i, 0)
    rev4 = lambda i: (n_chunks - 1 - i, 0, 0, 0)
    return pl.pallas_call(
        body, name="gdn_bwd", grid=(n_chunks,),
        in_specs=[pl.BlockSpec((CH, D), rev), pl.BlockSpec((CH, D), rev), pl.BlockSpec((CH, D), rev),
                  pl.BlockSpec((CH, 128), rev), pl.BlockSpec((CH, D), rev),
                  pl.BlockSpec((1, H, DH, DH), rev4), pl.BlockSpec((1, H, CH, CH), rev4)],
        out_specs=[pl.BlockSpec((CH, D), rev), pl.BlockSpec((CH, D), rev), pl.BlockSpec((CH, D), rev),
                   pl.BlockSpec((CH, 128), rev)],
        out_shape=[jax.ShapeDtypeStruct((t, D), f32)] * 3 + [jax.ShapeDtypeStruct((t, 128), f32)],
        scratch_shapes=[pltpu.VMEM((H, DH, DH), f32)],
        compiler_params=_params(1),
    )(qn, kn, vc, gbeta, do, s_all, t_all)


def _pad_rows(w, rows=8):
    return jnp.pad(w, ((0, rows - w.shape[0]), (0, 0)))


def _local_step(x, tgt, w):
    w1 = jnp.concatenate([w["w_in"][:, :7 * D], w["w_in"][:, 7 * D + 16:]], axis=1)
    w2 = jnp.pad(w["w_in"][:, 7 * D:7 * D + 16], ((0, 0), (0, 112)))
    wa = _pad_rows(w["conv_a_w"])
    wg = _pad_rows(w["gdn_conv_w"])
    wf = _pad_rows(w["ffn_conv_w"])
    alog = jnp.pad(w["gdn_A_log"].reshape(1, H), ((0, 0), (0, 128 - H)))
    dtb = jnp.pad(w["gdn_dt_bias"].reshape(1, H), ((0, 0), (0, 128 - H)))
    g1 = w["norm_mix_g"].reshape(1, D)
    g2 = w["norm_ffn_g"].reshape(1, D)
    g3 = w["norm_final_g"].reshape(1, D)
    gn = w["gdn_norm_g"].reshape(1, DH)

    h1 = _rms_fwd(x, g1, name="rms1_fwd")
    p1 = _matmul(h1, w1, name="mm_in")
    p2 = _matmul(h1, w2, name="mm_in_ab")
    ya_in, qn, kn, vc, gbeta = _pre_fwd(p1, p2, wa, wg, alog, dtb)
    o, s_all, t_all = _gdn_fwd(qn, kn, vc, gbeta)
    yb_in = _post_fwd(o, p1, gn)
    ya = _matmul(ya_in, w["w_a_out"], name="mm_a")
    yb = _matmul(yb_in, w["w_b_out"], name="mm_b")
    mix = _mix_fwd(ya, yb, p1)
    x2 = _matmul(mix, w["w_o"], name="mm_o", add=x)
    h2 = _rms_fwd(x2, g2, name="rms2_fwd")
    up = _matmul(h2, w["w_up"], name="mm_up")
    act = _ffn_fwd(up, wf)
    x3 = _matmul(act, w["w_down"], name="mm_down", add=x2)
    loss_p, dx3, dx3b, dg3 = _final(x3, tgt, g3)

    grads = {"norm_final_g": dg3}
    dact = _matmul(dx3b, w["w_down"].T, name="mm_down_dx")
    grads["w_down"] = _matmul_tn(act, dx3b, name="mm_down_dw")
    dc, dwf = _ffn_bwd1(dact, up, wf)
    grads["ffn_conv_w"] = dwf
    dup = _ffn_bwd2(dc, wf)
    dh2 = _matmul(dup, w["w_up"].T, name="mm_up_dx")
    grads["w_up"] = _matmul_tn(h2, dup, name="mm_up_dw")
    dx2, dx2b, dg2 = _rms_bwd(dh2, x2, g2, dx3, name="rms2_bwd")
    grads["norm_ffn_g"] = dg2
    dmix = _matmul(dx2b, w["w_o"].T, name="mm_o_dx")
    grads["w_o"] = _matmul_tn(mix, dx2b, name="mm_o_dw")
    dya, dyb, dgates = _mix_bwd(dmix, ya, yb, p1)
    dya_in = _matmul(dya, w["w_a_out"].T, name="mm_a_dx")
    grads["w_a_out"] = _matmul_tn(ya_in, dya, name="mm_a_dw")
    dyb_in = _matmul(dyb, w["w_b_out"].T, name="mm_b_dx")
    grads["w_b_out"] = _matmul_tn(yb_in, dyb, name="mm_b_dw")
    do, dz, dgn = _post_bwd(dyb_in, o, p1, gn)
    grads["gdn_norm_g"] = dgn
    dqn, dkn, dvc, dgb = _gdn_bwd(qn, kn, vc, gbeta, do, s_all, t_all)
    dbg, dca, dc4, dp2, dwa, dwg, dal, ddt = _pre_bwd1(p1, p2, dya_in, dqn, dkn, dvc, dgb, gbeta, wa, wg, alog, dtb)
    grads["conv_a_w"] = dwa
    grads["gdn_conv_w"] = dwg
    grads["gdn_A_log"] = dal
    grads["gdn_dt_bias"] = ddt
    dp1 = _pre_bwd2(dca, dc4, p1, dbg, dz, dgates, wa, wg)
    dh1 = _matmul(dp1, w1.T, name="mm_in_dx")
    dh1 = _matmul(dp2, w2.T, name="mm_in_ab_dx", add=dh1)
    dw1 = _matmul_tn(h1, dp1, name="mm_in_dw")
    dw2 = _matmul_tn(h1, dp2, name="mm_in_ab_dw")
    grads["w_in"] = jnp.concatenate([dw1[:, :7 * D], dw2[:, :16], dw1[:, 7 * D:]], axis=1)
    dx, _, dg1 = _rms_bwd(dh1, x, g1, dx2, name="rms1_bwd")
    grads["norm_mix_g"] = dg1
    return loss_p, dx, grads


_ANY = pl.BlockSpec(memory_space=pl.ANY)


def _all_gather(shard, *, name):
    rows, width = shard.shape

    def body(x_ref, out_ref, send_sems, recv_sems, local_sem):
        x, y, c = lax.axis_index("x"), lax.axis_index("y"), lax.axis_index("c")
        me, sibling = (x, y, c), (x, y, 1 - c)
        chips = [(1 - x, y), (x, 1 - y), (1 - x, 1 - y)]

        def block(px, py, pc):
            return out_ref.at[4 * px + 2 * py + pc]

        def copy(k, blk, to, src=None):
            return pltpu.make_async_remote_copy(
                src_ref=block(*blk) if src is None else src, dst_ref=block(*blk),
                send_sem=send_sems.at[k], recv_sem=recv_sems.at[k], device_id=to, device_id_type=MESH)

        mine = pltpu.make_async_copy(x_ref, block(*me), local_sem)
        mine.start()
        first = [copy(0, me, sibling, src=x_ref)]
        first += [copy(1 + j, me, (*chip, c), src=x_ref) for j, chip in enumerate(chips)]
        for cp in first:
            cp.start()
        passed = [copy(4 + j, (*chip, c), sibling) for j, chip in enumerate(chips)]
        for j, chip in enumerate(chips):
            copy(1 + j, (*chip, c), me).wait_recv()
            passed[j].start()
        copy(0, sibling, me).wait_recv()
        for j, chip in enumerate(chips):
            copy(4 + j, (*chip, 1 - c), me).wait_recv()
        for cp in first + passed:
            cp.wait_send()
        mine.wait()

    return pl.pallas_call(
        body, name=name, out_shape=jax.ShapeDtypeStruct((N_DEV, rows, width), shard.dtype),
        in_specs=[_ANY], out_specs=_ANY,
        scratch_shapes=[pltpu.SemaphoreType.DMA((7,)), pltpu.SemaphoreType.DMA((7,)), pltpu.SemaphoreType.DMA(())],
    )(shard)


def _exchange_sibling(pack):
    _, _, rows, width = pack.shape

    def body(p_ref, out_ref, send_sem, recv_sem):
        x, y, c = lax.axis_index("x"), lax.axis_index("y"), lax.axis_index("c")
        cp = pltpu.make_async_remote_copy(src_ref=p_ref.at[1 - c], dst_ref=out_ref, send_sem=send_sem,
                                          recv_sem=recv_sem, device_id=(x, y, 1 - c), device_id_type=MESH)
        cp.start()
        cp.wait()

    return pl.pallas_call(
        body, name="rs_sibling", out_shape=jax.ShapeDtypeStruct((4, rows, width), pack.dtype),
        in_specs=[_ANY], out_specs=_ANY,
        scratch_shapes=[pltpu.SemaphoreType.DMA(()), pltpu.SemaphoreType.DMA(())],
    )(pack)


def _exchange_chips(hsum):
    _, rows, width = hsum.shape

    def body(h_ref, out_ref, send_sems, recv_sems):
        x, y, c = lax.axis_index("x"), lax.axis_index("y"), lax.axis_index("c")
        chips = [(1 - x, y), (x, 1 - y), (1 - x, 1 - y)]
        cps = [pltpu.make_async_remote_copy(src_ref=h_ref.at[2 * px + py], dst_ref=out_ref.at[k],
                                            send_sem=send_sems.at[k], recv_sem=recv_sems.at[k],
                                            device_id=(px, py, c), device_id_type=MESH)
               for k, (px, py) in enumerate(chips)]
        for cp in cps:
            cp.start()
        for cp in cps:
            cp.wait()

    return pl.pallas_call(
        body, name="rs_chips", out_shape=jax.ShapeDtypeStruct((3, rows, width), hsum.dtype),
        in_specs=[_ANY], out_specs=_ANY,
        scratch_shapes=[pltpu.SemaphoreType.DMA((3,)), pltpu.SemaphoreType.DMA((3,))],
    )(hsum)


def _pair_sum(pack, recv, c):
    _, _, rows, width = pack.shape
    tb = _pick(rows, (200, 8))

    def body(c_ref, p_ref, r_ref, o_ref):
        o_ref[...] = p_ref[0] + r_ref[...]

    grid_spec = pltpu.PrefetchScalarGridSpec(
        num_scalar_prefetch=1, grid=(4, rows // tb),
        in_specs=[pl.BlockSpec((1, 1, tb, width), lambda j, i, c_ref: (c_ref[0], j, i, 0)),
                  pl.BlockSpec((1, tb, width), lambda j, i, c_ref: (j, i, 0))],
        out_specs=pl.BlockSpec((1, tb, width), lambda j, i, c_ref: (j, i, 0)))
    return pl.pallas_call(
        body, name="rs_pair_sum", grid_spec=grid_spec, out_shape=jax.ShapeDtypeStruct((4, rows, width), f32),
        compiler_params=_params(2),
    )(c, pack, recv)


def _adam_math(w, g, m, v):
    m = ADAM_B1 * m + (1.0 - ADAM_B1) * g
    v = ADAM_B2 * v + (1.0 - ADAM_B2) * jnp.square(g)
    m_hat = m / (1.0 - ADAM_B1 ** ADAM_STEP)
    v_hat = v / (1.0 - ADAM_B2 ** ADAM_STEP)
    delta = -ADAM_LR * (m_hat / (jnp.sqrt(v_hat) + ADAM_EPS) + ADAM_WD * w)
    return delta, m, v


def _adam_sharded(hsum, recv, chip, w, m, v):
    rows, width = w.shape
    tb = _pick(rows, (200, 8))

    def body(j_ref, h_ref, r_ref, w_ref, m_ref, v_ref, g_out, d_out, m_out, v_out):
        g = ((h_ref[0] + r_ref[0]) + r_ref[1]) + r_ref[2]
        delta, mn, vn = _adam_math(w_ref[...], g, m_ref[...], v_ref[...])
        g_out[...] = g
        d_out[...] = delta
        m_out[...] = mn
        v_out[...] = vn

    blk = pl.BlockSpec((tb, width), lambda i, j_ref: (i, 0))
    grid_spec = pltpu.PrefetchScalarGridSpec(
        num_scalar_prefetch=1, grid=(rows // tb,),
        in_specs=[pl.BlockSpec((1, tb, width), lambda i, j_ref: (j_ref[0], i, 0)),
                  pl.BlockSpec((3, tb, width), lambda i, j_ref: (0, i, 0)), blk, blk, blk],
        out_specs=[blk, blk, blk, blk])
    return pl.pallas_call(
        body, name="adam_sharded", grid_spec=grid_spec, out_shape=[jax.ShapeDtypeStruct((rows, width), f32)] * 4,
        compiler_params=_params(1),
    )(chip, hsum, recv, w, m, v)


def _adam_replicated(gath, w, m, v):
    def body(ga_ref, w_ref, m_ref, v_ref, g_out, d_out, m_out, v_out):
        g = ga_ref[0]
        for d in range(1, N_DEV):
            g = g + ga_ref[d]
        delta, mn, vn = _adam_math(w_ref[...], g, m_ref[...], v_ref[...])
        g_out[...] = g
        d_out[...] = delta
        m_out[...] = mn
        v_out[...] = vn

    return pl.pallas_call(body, name="adam_replicated", out_shape=[jax.ShapeDtypeStruct((8, D), f32)] * 4)(gath, w, m, v)


_SHARDED = ("w_in", "w_up", "w_a_out", "w_b_out", "w_o", "w_down")
_CONVS = (("conv_a_w", 3, 128), ("gdn_conv_w", 4, 384), ("ffn_conv_w", 3, 704))


def _pack_shard(t):
    conv = jnp.concatenate([t[n].reshape(-1) for n, _, _ in _CONVS])
    conv = jnp.pad(conv, (0, 4096 - N_CONV)).reshape(4, D)
    parts = [t[n].reshape(-1, D) for n in _SHARDED] + [conv, jnp.zeros((R_ALL - R_MM - 4, D), f32)]
    return jnp.concatenate(parts, axis=0)


def _unpack_shard(p):
    out = {}
    r = 0
    for n, rows, shape in (("w_in", R_IN, (D, R_IN)), ("w_up", R_UP, (D, R_UP)), ("w_a_out", R_SQ, (R_SQ, D)),
                           ("w_b_out", R_SQ, (R_SQ, D)), ("w_o", R_SQ, (R_SQ, D)), ("w_down", R_DOWN, (R_DOWN, D))):
        out[n] = p[r:r + rows].reshape(shape)
        r += rows
    conv = p[R_MM:R_MM + 4].reshape(-1)
    off = 0
    for n, k, wd in _CONVS:
        out[n] = conv[off:off + k * wd].reshape(k, wd)
        off += k * wd
    return out


def _cols_from_devices(g, rows, width):
    return g.reshape(N_DEV, D, width).transpose(1, 0, 2).reshape(D, N_DEV * width)


def _unpack_gathered(gm, gc):
    w = {}
    r = 0
    w["w_in"] = _cols_from_devices(gm[:, r:r + R_IN], R_IN, R_IN)
    r += R_IN
    w["w_up"] = _cols_from_devices(gm[:, r:r + R_UP], R_UP, R_UP)
    r += R_UP
    for n in ("w_a_out", "w_b_out", "w_o"):
        w[n] = gm[:, r:r + R_SQ].reshape(D, D)
        r += R_SQ
    w["w_down"] = gm[:, r:r + R_DOWN].reshape(DFF, D)
    conv = gc[:, :4].reshape(N_DEV, 4096)
    off = 0
    for n, k, wd in _CONVS:
        w[n] = conv[:, off:off + k * wd].reshape(N_DEV, k, wd).transpose(1, 0, 2).reshape(k, N_DEV * wd)
        off += k * wd
    return w


def _pack_grads(g):
    def cols(a, width):
        return a.reshape(D, N_DEV, width).transpose(1, 0, 2).reshape(N_DEV, width, D)

    conv = jnp.concatenate(
        [g[n][:k].reshape(k, N_DEV, wd).transpose(1, 0, 2).reshape(N_DEV, k * wd) for n, k, wd in _CONVS], axis=1)
    conv = jnp.pad(conv, ((0, 0), (0, 4096 - N_CONV))).reshape(N_DEV, 4, D)
    parts = [cols(g["w_in"], R_IN), cols(g["w_up"], R_UP), g["w_a_out"].reshape(N_DEV, R_SQ, D),
             g["w_b_out"].reshape(N_DEV, R_SQ, D), g["w_o"].reshape(N_DEV, R_SQ, D),
             g["w_down"].reshape(N_DEV, R_DOWN, D), conv, jnp.zeros((N_DEV, R_ALL - R_MM - 4, D), f32)]
    pack = jnp.concatenate(parts, axis=1)
    return pack.reshape(4, 2, R_ALL, D).transpose(1, 0, 2, 3)


_REPL = ("norm_mix_g", "norm_ffn_g", "norm_final_g", "gdn_norm_g", "gdn_A_log", "gdn_dt_bias")
_LOSS_LANE = 256


def _pack_replicated(t, loss=None):
    row3 = jnp.concatenate([t["gdn_norm_g"].reshape(-1), t["gdn_A_log"].reshape(-1), t["gdn_dt_bias"].reshape(-1)])
    row3 = jnp.pad(row3, (0, D - row3.shape[0]))
    if loss is not None:
        row3 = row3.at[_LOSS_LANE].set(loss)
    rows = [t["norm_mix_g"].reshape(1, D), t["norm_ffn_g"].reshape(1, D), t["norm_final_g"].reshape(1, D),
            row3.reshape(1, D), jnp.zeros((4, D), f32)]
    return jnp.concatenate(rows, axis=0)


def _unpack_replicated(p, shapes):
    out = {"norm_mix_g": p[0], "norm_ffn_g": p[1], "norm_final_g": p[2], "gdn_norm_g": p[3, :DH],
           "gdn_A_log": p[3, DH:DH + H], "gdn_dt_bias": p[3, DH + H:DH + 2 * H]}
    return {n: out[n].reshape(shapes[n]) for n in _REPL}


_WEIGHTS = ("norm_mix_g", "w_in", "conv_a_w", "gdn_conv_w", "gdn_A_log", "gdn_dt_bias", "gdn_norm_g", "w_a_out",
            "w_b_out", "w_o", "norm_ffn_g", "w_up", "ffn_conv_w", "w_down", "norm_final_g")


def kernel(x, norm_mix_g, w_in, conv_a_w, gdn_conv_w, gdn_A_log, gdn_dt_bias, gdn_norm_g, w_a_out, w_b_out, w_o, norm_ffn_g, w_up, ffn_conv_w, w_down, norm_final_g, loss_target, m_norm_mix_g, m_w_in, m_conv_a_w, m_gdn_conv_w, m_gdn_A_log, m_gdn_dt_bias, m_gdn_norm_g, m_w_a_out, m_w_b_out, m_w_o, m_norm_ffn_g, m_w_up, m_ffn_conv_w, m_w_down, m_norm_final_g, v_norm_mix_g, v_w_in, v_conv_a_w, v_gdn_conv_w, v_gdn_A_log, v_gdn_dt_bias, v_gdn_norm_g, v_w_a_out, v_w_b_out, v_w_o, v_norm_ffn_g, v_w_up, v_ffn_conv_w, v_w_down, v_norm_final_g):
    wts = dict(zip(_WEIGHTS, (norm_mix_g, w_in, conv_a_w, gdn_conv_w, gdn_A_log, gdn_dt_bias, gdn_norm_g, w_a_out,
                              w_b_out, w_o, norm_ffn_g, w_up, ffn_conv_w, w_down, norm_final_g)))
    mom = dict(zip(_WEIGHTS, (m_norm_mix_g, m_w_in, m_conv_a_w, m_gdn_conv_w, m_gdn_A_log, m_gdn_dt_bias,
                              m_gdn_norm_g, m_w_a_out, m_w_b_out, m_w_o, m_norm_ffn_g, m_w_up, m_ffn_conv_w,
                              m_w_down, m_norm_final_g)))
    var = dict(zip(_WEIGHTS, (v_norm_mix_g, v_w_in, v_conv_a_w, v_gdn_conv_w, v_gdn_A_log, v_gdn_dt_bias,
                              v_gdn_norm_g, v_w_a_out, v_w_b_out, v_w_o, v_norm_ffn_g, v_w_up, v_ffn_conv_w,
                              v_w_down, v_norm_final_g)))
    shapes = {n: wts[n].shape for n in _WEIGHTS}
    sharded = tuple(n for n in _WEIGHTS if n not in _REPL)

    def drop_depth(t):
        return {n: t[n][0] for n in sharded}

    w_pack = _pack_shard(drop_depth(wts))
    m_pack = _pack_shard(drop_depth(mom))
    v_pack = _pack_shard(drop_depth(var))

    mm_shard = jnp.pad(w_pack[:R_MM].astype(bf16), ((0, R_MM_PAD - R_MM), (0, 0)))
    conv_shard = jnp.pad(w_pack[R_MM:R_MM + 4], ((0, 4), (0, 0)))
    full = _unpack_gathered(_all_gather(mm_shard, name="ag_weights"), _all_gather(conv_shard, name="ag_conv"))
    for n in _REPL:
        full[n] = wts[n]

    loss_p, dx, grads = _local_step(x[0], loss_target[0], full)

    cx, cy, cc = lax.axis_index("x"), lax.axis_index("y"), lax.axis_index("c")
    pack = _pack_grads(grads)
    hsum = _pair_sum(pack, _exchange_sibling(pack), jnp.reshape(cc, (1,)).astype(jnp.int32))
    recv = _exchange_chips(hsum)
    g_sh, d_sh, m_sh, v_sh = _adam_sharded(hsum, recv, jnp.reshape(2 * cx + cy, (1,)).astype(jnp.int32),
                                           w_pack, m_pack, v_pack)

    small = {"norm_mix_g": jnp.sum(grads["norm_mix_g"], axis=0), "norm_ffn_g": jnp.sum(grads["norm_ffn_g"], axis=0),
             "norm_final_g": jnp.sum(grads["norm_final_g"], axis=0), "gdn_norm_g": jnp.sum(grads["gdn_norm_g"], axis=0),
             "gdn_A_log": jnp.sum(grads["gdn_A_log"], axis=0)[:H], "gdn_dt_bias": jnp.sum(grads["gdn_dt_bias"], axis=0)[:H]}
    g_small = _pack_replicated(small, loss=jnp.sum(loss_p[:, 0]))
    g_rp, d_rp, m_rp, v_rp = _adam_replicated(_all_gather(g_small, name="ag_small"), _pack_replicated(wts),
                                              _pack_replicated(mom), _pack_replicated(var))

    outs = []
    for packed_sh, packed_rp in ((g_sh, g_rp), (d_sh, d_rp), (m_sh, m_rp), (v_sh, v_rp)):
        t = _unpack_shard(packed_sh)
        t = {n: t[n].reshape(shapes[n]) for n in sharded}
        t.update(_unpack_replicated(packed_rp, shapes))
        outs.append([t[n] for n in _WEIGHTS])
    loss = g_rp[3, _LOSS_LANE]
    return (loss, dx[None], *outs[0], *outs[1], *outs[2], *outs[3])
```

```python
import functools

import jax
import jax.numpy as jnp
from jax import lax
from jax.experimental import pallas as pl
from jax.experimental.pallas import tpu as pltpu

f32 = jnp.float32
bf16 = jnp.bfloat16

D = 1024
H = 8
DH = 128
CH = 64
DFF = 2816
NW1 = 9216
EPS = 1e-6
N_DEV = 8

ADAM_LR = 0.001
ADAM_B1 = 0.9
ADAM_B2 = 0.999
ADAM_EPS = 1e-08
ADAM_WD = 0.01
ADAM_STEP = 10

VMEM_LIMIT_BYTES = 48 * 1024 * 1024

R_IN, R_UP, R_SQ, R_DOWN = 1154, 704, 128, 352
R_MM = R_IN + R_UP + 3 * R_SQ + R_DOWN
R_MM_PAD = 2608
R_ALL = 2640
R_BLOCK = 240
N_CONV = 3 * 128 + 4 * 384 + 3 * 704

_HI = lax.Precision.HIGHEST
MESH = pl.DeviceIdType.MESH


def _params(n_grid):
    return pltpu.CompilerParams(dimension_semantics=("arbitrary",) * n_grid, vmem_limit_bytes=VMEM_LIMIT_BYTES)


def _bdot(a, b):
    return jnp.dot(a.astype(bf16), b.astype(bf16), preferred_element_type=f32)


def _bdot_nt(a, b):
    return lax.dot_general(a.astype(bf16), b.astype(bf16), (((1,), (1,)), ((), ())), preferred_element_type=f32)


def _bdot_tn(a, b):
    return lax.dot_general(a.astype(bf16), b.astype(bf16), (((0,), (0,)), ((), ())), preferred_element_type=f32)


def _hdot(a, b):
    return jnp.dot(a, b, preferred_element_type=f32, precision=_HI)


def _idot(a, b):
    return jnp.dot(a, b, preferred_element_type=f32, precision=lax.Precision.HIGH)


def _sigmoid(x):
    return 1.0 / (1.0 + jnp.exp(-x))


def _softplus(x):
    return jnp.maximum(x, 0.0) + jnp.log(1.0 + jnp.exp(-jnp.abs(x)))


def _shift_down(x, halo, j):
    if j == 0:
        return x
    xr = pltpu.roll(x, j, 0)
    hr = pltpu.roll(halo, j, 0)
    r8 = lax.broadcasted_iota(jnp.int32, hr.shape, 0)
    top = jnp.where(r8 < j, hr, xr[:8])
    return jnp.concatenate([top, xr[8:]], axis=0)


def _shift_up(x, halo, j):
    if j == 0:
        return x
    n = x.shape[0]
    xr = pltpu.roll(x, n - j, 0)
    hr = pltpu.roll(halo, 8 - j, 0)
    r8 = lax.broadcasted_iota(jnp.int32, hr.shape, 0)
    bot = jnp.where(r8 >= 8 - j, hr, xr[n - 8:])
    return jnp.concatenate([xr[:n - 8], bot], axis=0)


def _conv_down(x, halo, w_ref, k):
    out = w_ref[k - 1:k, :] * x
    for j in range(k - 1):
        out = out + w_ref[j:j + 1, :] * _shift_down(x, halo, k - 1 - j)
    return out


def _conv_up(dy, halo, w_ref, k):
    out = w_ref[k - 1:k, :] * dy
    for j in range(k - 1):
        out = out + w_ref[j:j + 1, :] * _shift_up(dy, halo, k - 1 - j)
    return out


def _row(tb, w, col=0):
    return pl.BlockSpec((tb, w), lambda i: (i, col))


def _prev(tb, w, col=0):
    return pl.BlockSpec((8, w), lambda i: (jnp.maximum(i * (tb // 8) - 1, 0), col))


def _next(tb, w, n_rows, col=0):
    last = n_rows // 8 - 1
    return pl.BlockSpec((8, w), lambda i: (jnp.minimum((i + 1) * (tb // 8), last), col))


def _fixed(shape):
    return pl.BlockSpec(shape, lambda i: (0,) * len(shape))


def _first_zero(halo_ref):
    return jnp.where(pl.program_id(0) == 0, 0.0, halo_ref[...])


def _last_zero(halo_ref, n_blocks):
    return jnp.where(pl.program_id(0) == n_blocks - 1, 0.0, halo_ref[...])


def _pick(n, prefs):
    for p in prefs:
        if n % p == 0:
            return p
    return n


def _matmul(a, b, *, name, out_dtype=f32, add=None):
    m, kd = a.shape
    _, n = b.shape
    tm = _pick(m, (1024, 512, 256))
    tn = _pick(n, (1024, 1408, 512, 128))
    tk = _pick(kd, (1024, 1408, 512, 128)) if kd > 1408 else kd
    nk = kd // tk

    def body(*refs):
        if add is None:
            a_ref, b_ref, o_ref = refs[:3]
            add_ref = None
        else:
            a_ref, b_ref, add_ref, o_ref = refs[:4]
        part = jnp.dot(a_ref[...].astype(bf16), b_ref[...].astype(bf16), preferred_element_type=f32)
        if nk == 1:
            if add_ref is not None:
                part = part + add_ref[...]
            o_ref[...] = part.astype(out_dtype)
            return
        acc_ref = refs[-1]
        k = pl.program_id(2)

        @pl.when(k == 0)
        def _():
            acc_ref[...] = part

        @pl.when(k > 0)
        def _():
            acc_ref[...] += part

        @pl.when(k == nk - 1)
        def _():
            res = acc_ref[...]
            if add_ref is not None:
                res = res + add_ref[...]
            o_ref[...] = res.astype(out_dtype)

    in_specs = [pl.BlockSpec((tm, tk), lambda i, j, k: (i, k)), pl.BlockSpec((tk, tn), lambda i, j, k: (k, j))]
    args = [a, b]
    if add is not None:
        in_specs.append(pl.BlockSpec((tm, tn), lambda i, j, k: (i, j)))
        args.append(add)
    return pl.pallas_call(
        body, name=name, grid=(m // tm, n // tn, nk), in_specs=in_specs,
        out_specs=pl.BlockSpec((tm, tn), lambda i, j, k: (i, j)),
        out_shape=jax.ShapeDtypeStruct((m, n), out_dtype),
        scratch_shapes=[pltpu.VMEM((tm, tn), f32)] if nk > 1 else [],
        compiler_params=_params(3),
    )(*args)


def _matmul_tn(a, b, *, name):
    t, m = a.shape
    _, n = b.shape
    tm = _pick(m, (1024, 1408, 512, 128))
    tn = _pick(n, (1024, 1408, 512, 128))
    tt = _pick(t, (1024, 512, 256))
    nt = t // tt

    def body(a_ref, b_ref, o_ref):
        k = pl.program_id(2)
        part = lax.dot_general(a_ref[...].astype(bf16), b_ref[...].astype(bf16), (((0,), (0,)), ((), ())),
                               preferred_element_type=f32)

        @pl.when(k == 0)
        def _():
            o_ref[...] = part

        @pl.when(k > 0)
        def _():
            o_ref[...] += part

    return pl.pallas_call(
        body, name=name, grid=(m // tm, n // tn, nt),
        in_specs=[pl.BlockSpec((tt, tm), lambda i, j, k: (k, i)), pl.BlockSpec((tt, tn), lambda i, j, k: (k, j))],
        out_specs=pl.BlockSpec((tm, tn), lambda i, j, k: (i, j)),
        out_shape=jax.ShapeDtypeStruct((m, n), f32),
        compiler_params=_params(3),
    )(a, b)


def _rms_fwd(x, g, *, name):
    t = x.shape[0]
    tb = _pick(t, (256, 128))

    def body(x_ref, g_ref, h_ref):
        xv = x_ref[...]
        r = lax.rsqrt(jnp.mean(xv * xv, axis=-1, keepdims=True) + EPS)
        h_ref[...] = (xv * r * g_ref[...]).astype(bf16)

    return pl.pallas_call(
        body, name=name, grid=(t // tb,), in_specs=[_row(tb, D), _fixed((1, D))], out_specs=_row(tb, D),
        out_shape=jax.ShapeDtypeStruct((t, D), bf16), compiler_params=_params(1),
    )(x, g)


def _rms_bwd(dh, x, g, dres, *, name):
    t = x.shape[0]
    tb = _pick(t, (256, 128))

    def body(dh_ref, x_ref, g_ref, dres_ref, dx_ref, dxb_ref, dg_ref):
        xv = x_ref[...]
        r = lax.rsqrt(jnp.mean(xv * xv, axis=-1, keepdims=True) + EPS)
        xh = xv * r
        dy = dh_ref[...]
        dyg = dy * g_ref[...]
        dx = dres_ref[...] + r * (dyg - xh * jnp.mean(dyg * xh, axis=-1, keepdims=True))
        dx_ref[...] = dx
        dxb_ref[...] = dx.astype(bf16)

        @pl.when(pl.program_id(0) == 0)
        def _():
            dg_ref[...] = jnp.zeros_like(dg_ref)

        dg_ref[...] += jnp.sum((dy * xh).reshape(tb // 8, 8, D), axis=0)

    return pl.pallas_call(
        body, name=name, grid=(t // tb,),
        in_specs=[_row(tb, D), _row(tb, D), _fixed((1, D)), _row(tb, D)],
        out_specs=[_row(tb, D), _row(tb, D), _fixed((8, D))],
        out_shape=[jax.ShapeDtypeStruct((t, D), f32), jax.ShapeDtypeStruct((t, D), bf16),
                   jax.ShapeDtypeStruct((8, D), f32)],
        compiler_params=_params(1),
    )(dh, x, g, dres)


def _gdn_gates(ab, alog, dtb):
    lane = lax.broadcasted_iota(jnp.int32, ab.shape, 1)
    g = -jnp.exp(alog) * _softplus(ab + dtb)
    beta = _sigmoid(ab)
    return jnp.where(lane < H, g, jnp.where(lane < 2 * H, beta, 0.0))


def _pre_fwd(p1, p2, wa, wg, alog, dtb):
    t = p1.shape[0]
    tb = 128

    def body(p0_ref, p0h_ref, pq_ref, pqh_ref, p2_ref, wa_ref, wg_ref, alog_ref, dtb_ref,
             ya_ref, qn_ref, kn_ref, vc_ref, gb_ref):
        p0 = p0_ref[...]
        h0 = _first_zero(p0h_ref)
        u = p0[:, D:2 * D] * p0[:, 2 * D:]
        uh = h0[:, D:2 * D] * h0[:, 2 * D:]
        ya_ref[...] = (p0[:, :D] * _conv_down(u, uh, wa_ref, 3)).astype(bf16)
        s = _conv_down(pq_ref[...], _first_zero(pqh_ref), wg_ref, 4)
        s = s * _sigmoid(s)
        for h in range(H):
            q = s[:, h * DH:(h + 1) * DH]
            k = s[:, D + h * DH:D + (h + 1) * DH]
            qn_ref[:, h * DH:(h + 1) * DH] = q * (lax.rsqrt(jnp.sum(q * q, axis=-1, keepdims=True) + EPS) * DH ** -0.5)
            kn_ref[:, h * DH:(h + 1) * DH] = k * lax.rsqrt(jnp.sum(k * k, axis=-1, keepdims=True) + EPS)
        vc_ref[...] = s[:, 2 * D:]
        gb_ref[...] = _gdn_gates(p2_ref[...], alog_ref[...], dtb_ref[...])

    return pl.pallas_call(
        body, name="pre_fwd", grid=(t // tb,),
        in_specs=[_row(tb, 3 * D, 0), _prev(tb, 3 * D, 0), _row(tb, 3 * D, 1), _prev(tb, 3 * D, 1), _row(tb, 128),
                  _fixed((8, D)), _fixed((8, 3 * D)), _fixed((1, 128)), _fixed((1, 128))],
        out_specs=[_row(tb, D), _row(tb, D), _row(tb, D), _row(tb, D), _row(tb, 128)],
        out_shape=[jax.ShapeDtypeStruct((t, D), bf16), jax.ShapeDtypeStruct((t, D), f32),
                   jax.ShapeDtypeStruct((t, D), f32), jax.ShapeDtypeStruct((t, D), f32),
                   jax.ShapeDtypeStruct((t, 128), f32)],
        compiler_params=_params(1),
    )(p1, p1, p1, p1, p2, wa, wg, alog, dtb)


def _post_fwd(o, p1, gn):
    t = o.shape[0]
    tb = _pick(t, (256, 128))

    def body(o_ref, z_ref, gn_ref, yb_ref):
        for h in range(H):
            sl = slice(h * DH, (h + 1) * DH)
            oh = o_ref[:, sl]
            z = z_ref[:, sl]
            r = lax.rsqrt(jnp.mean(oh * oh, axis=-1, keepdims=True) + EPS)
            yb_ref[:, sl] = (oh * r * gn_ref[...] * (z * _sigmoid(z))).astype(bf16)

    return pl.pallas_call(
        body, name="post_fwd", grid=(t // tb,), in_specs=[_row(tb, D), _row(tb, D, 6), _fixed((1, DH))],
        out_specs=_row(tb, D), out_shape=jax.ShapeDtypeStruct((t, D), bf16), compiler_params=_params(1),
    )(o, p1, gn)


def _post_bwd(dyb, o, p1, gn):
    t = o.shape[0]
    tb = _pick(t, (256, 128))

    def body(dyb_ref, o_ref, z_ref, gn_ref, do_ref, dz_ref, dgn_ref):
        @pl.when(pl.program_id(0) == 0)
        def _():
            dgn_ref[...] = jnp.zeros_like(dgn_ref)

        gn_v = gn_ref[...]
        acc = jnp.zeros((8, DH), f32)
        for h in range(H):
            sl = slice(h * DH, (h + 1) * DH)
            oh = o_ref[:, sl]
            z = z_ref[:, sl]
            dy = dyb_ref[:, sl]
            r = lax.rsqrt(jnp.mean(oh * oh, axis=-1, keepdims=True) + EPS)
            on = oh * r
            sg = _sigmoid(z)
            sz = z * sg
            don = dy * sz
            dz_ref[:, sl] = (dy * on * gn_v * (sg * (1.0 + z * (1.0 - sg)))).astype(bf16)
            acc = acc + jnp.sum((don * on).reshape(tb // 8, 8, DH), axis=0)
            doh = don * gn_v
            do_ref[:, sl] = r * (doh - on * jnp.mean(doh * on, axis=-1, keepdims=True))
        dgn_ref[...] += acc

    return pl.pallas_call(
        body, name="post_bwd", grid=(t // tb,),
        in_specs=[_row(tb, D), _row(tb, D), _row(tb, D, 6), _fixed((1, DH))],
        out_specs=[_row(tb, D), _row(tb, D), _fixed((8, DH))],
        out_shape=[jax.ShapeDtypeStruct((t, D), f32), jax.ShapeDtypeStruct((t, D), bf16),
                   jax.ShapeDtypeStruct((8, DH), f32)],
        compiler_params=_params(1),
    )(dyb, o, p1, gn)


def _mix_fwd(ya, yb, p1):
    t = ya.shape[0]
    tb = _pick(t, (256, 128))

    def body(ya_ref, yb_ref, ga_ref, gb_ref, mix_ref):
        mix_ref[...] = (_sigmoid(ga_ref[...]) * ya_ref[...] + _sigmoid(gb_ref[...]) * yb_ref[...]).astype(bf16)

    return pl.pallas_call(
        body, name="mix_fwd", grid=(t // tb,), in_specs=[_row(tb, D), _row(tb, D), _row(tb, D, 7), _row(tb, D, 8)],
        out_specs=_row(tb, D), out_shape=jax.ShapeDtypeStruct((t, D), bf16), compiler_params=_params(1),
    )(ya, yb, p1, p1)


def _mix_bwd(dmix, ya, yb, p1):
    t = ya.shape[0]
    tb = _pick(t, (256, 128))

    def body(dm_ref, ya_ref, yb_ref, ga_ref, gb_ref, dya_ref, dyb_ref, dg_ref):
        dm = dm_ref[...]
        sa = _sigmoid(ga_ref[...])
        sb = _sigmoid(gb_ref[...])
        dya_ref[...] = (dm * sa).astype(bf16)
        dyb_ref[...] = (dm * sb).astype(bf16)
        dg_ref[:, :D] = (dm * ya_ref[...] * sa * (1.0 - sa)).astype(bf16)
        dg_ref[:, D:] = (dm * yb_ref[...] * sb * (1.0 - sb)).astype(bf16)

    return pl.pallas_call(
        body, name="mix_bwd", grid=(t // tb,),
        in_specs=[_row(tb, D), _row(tb, D), _row(tb, D), _row(tb, D, 7), _row(tb, D, 8)],
        out_specs=[_row(tb, D), _row(tb, D), _row(tb, 2 * D)],
        out_shape=[jax.ShapeDtypeStruct((t, D), bf16), jax.ShapeDtypeStruct((t, D), bf16),
                   jax.ShapeDtypeStruct((t, 2 * D), bf16)],
        compiler_params=_params(1),
    )(dmix, ya, yb, p1, p1)


def _ffn_fwd(up, wf):
    t = up.shape[0]
    tb = 128

    def body(up_ref, uph_ref, wf_ref, act_ref):
        c = _conv_down(up_ref[...], _first_zero(uph_ref), wf_ref, 3)
        gate = c[:, :DFF]
        act_ref[...] = (gate * _sigmoid(gate) * c[:, DFF:]).astype(bf16)

    return pl.pallas_call(
        body, name="ffn_fwd", grid=(t // tb,), in_specs=[_row(tb, 2 * DFF), _prev(tb, 2 * DFF), _fixed((8, 2 * DFF))],
        out_specs=_row(tb, DFF), out_shape=jax.ShapeDtypeStruct((t, DFF), bf16), compiler_params=_params(1),
    )(up, up, wf)


def _ffn_bwd1(dact, up, wf):
    t = up.shape[0]
    tb = 128

    def body(da_ref, up_ref, uph_ref, wf_ref, dc_ref, dw_ref):
        @pl.when(pl.program_id(0) == 0)
        def _():
            dw_ref[...] = jnp.zeros_like(dw_ref)

        upv = up_ref[...]
        uph = _first_zero(uph_ref)
        c = _conv_down(upv, uph, wf_ref, 3)
        gate = c[:, :DFF]
        val = c[:, DFF:]
        sg = _sigmoid(gate)
        da = da_ref[...]
        dgate = da * val * (sg * (1.0 + gate * (1.0 - sg)))
        dval = da * (gate * sg)
        dc_ref[:, :DFF] = dgate
        dc_ref[:, DFF:] = dval
        dc = jnp.concatenate([dgate, dval], axis=1)
        for j in range(3):
            dw_ref[j:j + 1, :] += jnp.sum(dc * _shift_down(upv, uph, 2 - j), axis=0, keepdims=True)

    return pl.pallas_call(
        body, name="ffn_bwd1", grid=(t // tb,),
        in_specs=[_row(tb, DFF), _row(tb, 2 * DFF), _prev(tb, 2 * DFF), _fixed((8, 2 * DFF))],
        out_specs=[_row(tb, 2 * DFF), _fixed((8, 2 * DFF))],
        out_shape=[jax.ShapeDtypeStruct((t, 2 * DFF), f32), jax.ShapeDtypeStruct((8, 2 * DFF), f32)],
        compiler_params=_params(1),
    )(dact, up, up, wf)


def _ffn_bwd2(dc, wf):
    t = dc.shape[0]
    tb = 128
    nb = t // tb

    def body(dc_ref, dch_ref, wf_ref, dup_ref):
        dup_ref[...] = _conv_up(dc_ref[...], _last_zero(dch_ref, nb), wf_ref, 3).astype(bf16)

    return pl.pallas_call(
        body, name="ffn_bwd2", grid=(nb,), in_specs=[_row(tb, 2 * DFF), _next(tb, 2 * DFF, t), _fixed((8, 2 * DFF))],
        out_specs=_row(tb, 2 * DFF), out_shape=jax.ShapeDtypeStruct((t, 2 * DFF), bf16), compiler_params=_params(1),
    )(dc, dc, wf)


def _final(x3, tgt, g):
    t = x3.shape[0]
    tb = _pick(t, (256, 128))

    def body(x_ref, t_ref, g_ref, loss_ref, dx_ref, dxb_ref, dg_ref):
        @pl.when(pl.program_id(0) == 0)
        def _():
            loss_ref[...] = jnp.zeros_like(loss_ref)
            dg_ref[...] = jnp.zeros_like(dg_ref)

        xv = x_ref[...]
        r = lax.rsqrt(jnp.mean(xv * xv, axis=-1, keepdims=True) + EPS)
        xh = xv * r
        gv = g_ref[...]
        e = xh * gv - t_ref[...]
        lrow = 0.5 * jnp.mean(e * e, axis=-1, keepdims=True)
        loss_ref[...] += jnp.sum(jnp.broadcast_to(lrow, (tb, 128)).reshape(tb // 8, 8, 128), axis=0)
        dy = e * (1.0 / D)
        dyg = dy * gv
        dx = r * (dyg - xh * jnp.mean(dyg * xh, axis=-1, keepdims=True))
        dx_ref[...] = dx
        dxb_ref[...] = dx.astype(bf16)
        dg_ref[...] += jnp.sum((dy * xh).reshape(tb // 8, 8, D), axis=0)

    return pl.pallas_call(
        body, name="final", grid=(t // tb,), in_specs=[_row(tb, D), _row(tb, D), _fixed((1, D))],
        out_specs=[_fixed((8, 128)), _row(tb, D), _row(tb, D), _fixed((8, D))],
        out_shape=[jax.ShapeDtypeStruct((8, 128), f32), jax.ShapeDtypeStruct((t, D), f32),
                   jax.ShapeDtypeStruct((t, D), bf16), jax.ShapeDtypeStruct((8, D), f32)],
        compiler_params=_params(1),
    )(x3, tgt, g)


def _pre_bwd1(p1, p2, dya_in, dqn, dkn, dvc, dgb, gbeta, wa, wg, alog, dtb):
    t = p1.shape[0]
    tb = 128

    def body(p0_ref, p0h_ref, pq_ref, pqh_ref, p2_ref, dya_ref, dqn_ref, dkn_ref, dvc_ref, dgb_ref, gb_ref,
             wa_ref, wg_ref, alog_ref, dtb_ref,
             dbg_ref, dca_ref, dc4_ref, dp2_ref, dwa_ref, dwg_ref, dal_ref, ddt_ref):
        @pl.when(pl.program_id(0) == 0)
        def _():
            dwa_ref[...] = jnp.zeros_like(dwa_ref)
            dwg_ref[...] = jnp.zeros_like(dwg_ref)
            dal_ref[...] = jnp.zeros_like(dal_ref)
            ddt_ref[...] = jnp.zeros_like(ddt_ref)

        p0 = p0_ref[...]
        h0 = _first_zero(p0h_ref)
        u = p0[:, D:2 * D] * p0[:, 2 * D:]
        uh = h0[:, D:2 * D] * h0[:, 2 * D:]
        dya = dya_ref[...]
        dbg_ref[...] = (dya * _conv_down(u, uh, wa_ref, 3)).astype(bf16)
        dca = dya * p0[:, :D]
        dca_ref[...] = dca
        for j in range(3):
            dwa_ref[j:j + 1, :] += jnp.sum(dca * _shift_down(u, uh, 2 - j), axis=0, keepdims=True)

        pq = pq_ref[...]
        pqh = _first_zero(pqh_ref)
        c4 = _conv_down(pq, pqh, wg_ref, 4)
        sg = _sigmoid(c4)
        s = c4 * sg
        dsilu = sg * (1.0 + c4 * (1.0 - sg))
        for h in range(H):
            for base, d_ref, scale in ((0, dqn_ref, DH ** -0.5), (D, dkn_ref, 1.0)):
                sl = slice(base + h * DH, base + (h + 1) * DH)
                a = s[:, sl]
                r = lax.rsqrt(jnp.sum(a * a, axis=-1, keepdims=True) + EPS)
                an = a * r
                dn = d_ref[:, h * DH:(h + 1) * DH] * scale
                dc4_ref[:, sl] = r * (dn - an * jnp.sum(dn * an, axis=-1, keepdims=True)) * dsilu[:, sl]
        dc4_ref[:, 2 * D:] = dvc_ref[...] * dsilu[:, 2 * D:]
        dc4 = dc4_ref[...]
        for j in range(4):
            dwg_ref[j:j + 1, :] += jnp.sum(dc4 * _shift_down(pq, pqh, 3 - j), axis=0, keepdims=True)

        ab = p2_ref[...]
        lane = lax.broadcasted_iota(jnp.int32, ab.shape, 1)
        dgbv = dgb_ref[...]
        gbv = gb_ref[...]
        da = dgbv * (-jnp.exp(alog_ref[...])) * _sigmoid(ab + dtb_ref[...])
        db = dgbv * gbv * (1.0 - gbv)
        dp2_ref[...] = jnp.where(lane < H, da, jnp.where(lane < 2 * H, db, 0.0)).astype(bf16)
        dal = jnp.where(lane < H, dgbv * gbv, 0.0)
        ddt = jnp.where(lane < H, da, 0.0)
        dal_ref[...] += jnp.sum(dal.reshape(tb // 8, 8, 128), axis=0)
        ddt_ref[...] += jnp.sum(ddt.reshape(tb // 8, 8, 128), axis=0)

    return pl.pallas_call(
        body, name="pre_bwd1", grid=(t // tb,),
        in_specs=[_row(tb, 3 * D, 0), _prev(tb, 3 * D, 0), _row(tb, 3 * D, 1), _prev(tb, 3 * D, 1), _row(tb, 128),
                  _row(tb, D), _row(tb, D), _row(tb, D), _row(tb, D), _row(tb, 128), _row(tb, 128),
                  _fixed((8, D)), _fixed((8, 3 * D)), _fixed((1, 128)), _fixed((1, 128))],
        out_specs=[_row(tb, D), _row(tb, D), _row(tb, 3 * D), _row(tb, 128),
                   _fixed((8, D)), _fixed((8, 3 * D)), _fixed((8, 128)), _fixed((8, 128))],
        out_shape=[jax.ShapeDtypeStruct((t, D), bf16), jax.ShapeDtypeStruct((t, D), f32),
                   jax.ShapeDtypeStruct((t, 3 * D), f32), jax.ShapeDtypeStruct((t, 128), bf16),
                   jax.ShapeDtypeStruct((8, D), f32), jax.ShapeDtypeStruct((8, 3 * D), f32),
                   jax.ShapeDtypeStruct((8, 128), f32), jax.ShapeDtypeStruct((8, 128), f32)],
        compiler_params=_params(1),
    )(p1, p1, p1, p1, p2, dya_in, dqn, dkn, dvc, dgb, gbeta, wa, wg, alog, dtb)


def _pre_bwd2(dca, dc4, p1, dbg, dz, dgates, wa, wg):
    t = p1.shape[0]
    tb = 128
    nb = t // tb

    def body(dca_ref, dcah_ref, dc4_ref, dc4h_ref, p0_ref, dbg_ref, dz_ref, dgt_ref, wa_ref, wg_ref, dp_ref):
        du = _conv_up(dca_ref[...], _last_zero(dcah_ref, nb), wa_ref, 3)
        dp_ref[:, :D] = dbg_ref[...]
        dp_ref[:, D:2 * D] = (du * p0_ref[:, 2 * D:]).astype(bf16)
        dp_ref[:, 2 * D:3 * D] = (du * p0_ref[:, D:2 * D]).astype(bf16)
        dp_ref[:, 3 * D:6 * D] = _conv_up(dc4_ref[...], _last_zero(dc4h_ref, nb), wg_ref, 4).astype(bf16)
        dp_ref[:, 6 * D:7 * D] = dz_ref[...]
        dp_ref[:, 7 * D:] = dgt_ref[...]

    return pl.pallas_call(
        body, name="pre_bwd2", grid=(nb,),
        in_specs=[_row(tb, D), _next(tb, D, t), _row(tb, 3 * D), _next(tb, 3 * D, t), _row(tb, 3 * D, 0),
                  _row(tb, D), _row(tb, D), _row(tb, 2 * D), _fixed((8, D)), _fixed((8, 3 * D))],
        out_specs=_row(tb, NW1), out_shape=jax.ShapeDtypeStruct((t, NW1), bf16), compiler_params=_params(1),
    )(dca, dca, dc4, dc4, p1, dbg, dz, dgates, wa, wg)


def _chunk_consts():
    r = lax.broadcasted_iota(jnp.int32, (CH, CH), 0)
    c = lax.broadcasted_iota(jnp.int32, (CH, CH), 1)
    return r, c, (r == c).astype(f32)


def _tri_inverse(lows, eye):
    xps = [-low for low in lows]
    invs = [eye + xp for xp in xps]
    for _ in range(5):
        xps = [_idot(xp, xp) for xp in xps]
        invs = [inv + _idot(inv, xp) for inv, xp in zip(invs, xps)]
    return invs


def _chunk_common(q, k, v, gcol, bcol, r, c, eye):
    grow = jnp.sum(eye * gcol, axis=0, keepdims=True)
    dec = jnp.exp(jnp.where(r >= c, gcol - grow, -jnp.inf))
    rcol = lax.broadcasted_iota(jnp.int32, (CH, 1), 0)
    glast = jnp.sum(jnp.where(rcol == CH - 1, gcol, 0.0), axis=0, keepdims=True)
    eg = jnp.exp(gcol)
    el = jnp.exp(glast - gcol)
    kb = k * bcol
    vb = v * bcol
    kk = _bdot_nt(kb, k)
    low = jnp.where(r > c, kk * dec, 0.0)
    qk = _bdot_nt(q, k)
    att = qk * dec
    return grow, dec, glast, eg, el, kb, vb, kk, low, qk, att, rcol


def _gdn_fwd(qn, kn, vc, gbeta):
    t = qn.shape[0]
    n_chunks = t // CH

    def body(q_ref, k_ref, v_ref, gb_ref, o_ref, s_ref, t_ref, state):
        @pl.when(pl.program_id(0) == 0)
        def _():
            state[...] = jnp.zeros_like(state)

        r, c, eye = _chunk_consts()
        gb = gb_ref[...]
        gall = _hdot((r >= c).astype(f32), gb)
        heads = range(H)
        qs = [q_ref[:, h * DH:(h + 1) * DH] for h in heads]
        ks = [k_ref[:, h * DH:(h + 1) * DH] for h in heads]
        vs = [v_ref[:, h * DH:(h + 1) * DH] for h in heads]
        sts = [state[h] for h in heads]
        cm = [_chunk_common(qs[h], ks[h], vs[h], gall[:, h:h + 1], gb[:, H + h:H + h + 1], r, c, eye) for h in heads]
        invs = _tri_inverse([m[8] for m in cm], eye)
        uws = [_bdot(invs[h], jnp.concatenate([cm[h][6], cm[h][5] * cm[h][3]], axis=1)) for h in heads]
        vns = [uws[h][:, :DH] - _bdot(uws[h][:, DH:], sts[h]) for h in heads]
        outs = [_bdot(qs[h] * cm[h][3], sts[h]) + _bdot(cm[h][10], vns[h]) for h in heads]
        news = [sts[h] * jnp.exp(cm[h][2]) + _bdot_tn(ks[h] * cm[h][4], vns[h]) for h in heads]
        for h in heads:
            s_ref[0, h] = sts[h].astype(bf16)
            t_ref[0, h] = invs[h]
            o_ref[:, h * DH:(h + 1) * DH] = outs[h]
            state[h] = news[h]

    return pl.pallas_call(
        body, name="gdn_fwd", grid=(n_chunks,),
        in_specs=[_row(CH, D), _row(CH, D), _row(CH, D), _row(CH, 128)],
        out_specs=[_row(CH, D), pl.BlockSpec((1, H, DH, DH), lambda i: (i, 0, 0, 0)),
                   pl.BlockSpec((1, H, CH, CH), lambda i: (i, 0, 0, 0))],
        out_shape=[jax.ShapeDtypeStruct((t, D), f32), jax.ShapeDtypeStruct((n_chunks, H, DH, DH), bf16),
                   jax.ShapeDtypeStruct((n_chunks, H, CH, CH), f32)],
        scratch_shapes=[pltpu.VMEM((H, DH, DH), f32)],
        compiler_params=_params(1),
    )(qn, kn, vc, gbeta)


def _gdn_bwd(qn, kn, vc, gbeta, do, s_all, t_all):
    t = qn.shape[0]
    n_chunks = t // CH

    def body(q_ref, k_ref, v_ref, gb_ref, do_ref, s_ref, t_ref, dq_ref, dk_ref, dv_ref, dgb_ref, dstate):
        @pl.when(pl.program_id(0) == 0)
        def _():
            dstate[...] = jnp.zeros_like(dstate)

        r, c, eye = _chunk_consts()
        tril = r >= c
        gb = gb_ref[...]
        gall = _hdot(tril.astype(f32), gb)
        lane = lax.broadcasted_iota(jnp.int32, (1, 128), 1)
        hs = range(H)

        def each(fn, *lists):
            return [fn(*args) for args in zip(*lists)]

        def rsum(a):
            return jnp.sum(a, axis=1, keepdims=True)

        q = [q_ref[:, h * DH:(h + 1) * DH] for h in hs]
        k = [k_ref[:, h * DH:(h + 1) * DH] for h in hs]
        v = [v_ref[:, h * DH:(h + 1) * DH] for h in hs]
        dout = [do_ref[:, h * DH:(h + 1) * DH] for h in hs]
        inv = [t_ref[0, h] for h in hs]
        st = [s_ref[0, h] for h in hs]
        ds = [dstate[h] for h in hs]
        bcol = [gb[:, H + h:H + h + 1] for h in hs]
        cm = [_chunk_common(q[h], k[h], v[h], gall[:, h:h + 1], bcol[h], r, c, eye) for h in hs]
        dec, glast, eg, el, kb, vb, low, att = ([m[i] for m in cm] for i in (1, 2, 3, 4, 5, 6, 8, 10))
        rcol = cm[0][11]
        elast = each(jnp.exp, glast)
        kbg = each(jnp.multiply, kb, eg)
        uw = each(lambda i, a, b: _bdot(i, jnp.concatenate([a, b], axis=1)), inv, vb, kbg)
        u = [a[:, :DH] for a in uw]
        w = [a[:, DH:] for a in uw]
        vn = each(lambda a, b, s: a - _bdot(b, s), u, w, st)
        qd = each(jnp.multiply, q, eg)
        kd = each(jnp.multiply, k, el)
        dvn = each(lambda a, d, kk, s: _bdot_tn(a, d) + _bdot(kk, s), att, dout, kd, ds)
        dqd = each(_bdot_nt, dout, st)
        datt = each(lambda d, x: jnp.where(tril, _bdot_nt(d, x), 0.0), dout, vn)
        dkd = each(_bdot_nt, vn, ds)
        dw = each(lambda a, s: -_bdot_nt(a, s), dvn, st)
        new_ds = each(lambda s, e, a, d, ww, dv_: s * e + _bdot_tn(a, d) - _bdot_tn(ww, dv_), ds, elast, qd, dout, w, dvn)
        dglast = each(lambda e, s, d: e * jnp.sum(rsum(s.astype(f32) * d), axis=0, keepdims=True), elast, st, ds)
        dr = each(lambda i, a, b: _bdot_tn(i, jnp.concatenate([a, b], axis=1)), inv, dvn, dw)
        dvb = [a[:, :DH] for a in dr]
        dkbg = [a[:, DH:] for a in dr]
        dlow = each(lambda a, b, x, y: -jnp.where(r > c, _bdot_nt(a, b) + _bdot_nt(x, y), 0.0), dvb, u, dkbg, w)
        dkk = each(jnp.multiply, dlow, dec)
        dqk = each(jnp.multiply, datt, dec)
        mm = each(lambda a, b, x, y: a * b + x * y, dlow, low, datt, att)
        dkb = each(lambda a, kk, b, e: _bdot(a, kk) + b * e, dkk, k, dkbg, eg)
        dk = each(lambda a, b, x, y, d, e, f, g: _bdot_tn(a, b) + _bdot_tn(x, y) + d * e + f * g,
                  dkk, kb, dqk, q, dkd, el, dkb, bcol)
        dq = each(lambda a, kk, d, e: _bdot(a, kk) + d * e, dqk, k, dqd, eg)
        dv = each(jnp.multiply, dvb, bcol)
        dbeta = each(lambda a, b, x, y: rsum(a * b) + rsum(x * y), dkb, k, dvb, v)
        deg = each(lambda a, b, x, y: rsum(a * b) + rsum(x * y), dkbg, kb, dqd, q)
        delc = each(lambda a, b, e: rsum(a * b) * e, dkd, k, el)
        dgc = each(lambda m, a, e, d: rsum(m) - rsum(eye * jnp.sum(m, axis=0, keepdims=True)) + a * e - d, mm, deg, eg, delc)
        dgc = each(lambda g, d, l: g + jnp.where(rcol == CH - 1, jnp.sum(d, axis=0, keepdims=True) + l, 0.0),
                   dgc, delc, dglast)
        dg_acc = jnp.zeros((CH, 128), f32)
        db_acc = jnp.zeros((CH, 128), f32)
        for h in hs:
            dq_ref[:, h * DH:(h + 1) * DH] = dq[h]
            dk_ref[:, h * DH:(h + 1) * DH] = dk[h]
            dv_ref[:, h * DH:(h + 1) * DH] = dv[h]
            dstate[h] = new_ds[h]
            dg_acc = dg_acc + dgc[h] * (lane == h).astype(f32)
            db_acc = db_acc + dbeta[h] * (lane == H + h).astype(f32)
        dgb_ref[...] = _hdot((r <= c).astype(f32), dg_acc) + db_acc

    rev = lambda i: (n_chunks - 1 - i, 0)
    rev4 = lambda i: (n_chunks - 1 - i, 0, 0, 0)
    return pl.pallas_call(
        body, name="gdn_bwd", grid=(n_chunks,),
        in_specs=[pl.BlockSpec((CH, D), rev), pl.BlockSpec((CH, D), rev), pl.BlockSpec((CH, D), rev),
                  pl.BlockSpec((CH, 128), rev), pl.BlockSpec((CH, D), rev),
                  pl.BlockSpec((1, H, DH, DH), rev4), pl.BlockSpec((1, H, CH, CH), rev4)],
        out_specs=[pl.BlockSpec((CH, D), rev), pl.BlockSpec((CH, D), rev), pl.BlockSpec((CH, D), rev),
                   pl.BlockSpec((CH, 128), rev)],
        out_shape=[jax.ShapeDtypeStruct((t, D), f32)] * 3 + [jax.ShapeDtypeStruct((t, 128), f32)],
        scratch_shapes=[pltpu.VMEM((H, DH, DH), f32)],
        compiler_params=_params(1),
    )(qn, kn, vc, gbeta, do, s_all, t_all)


def _pad_rows(w, rows=8):
    return jnp.pad(w, ((0, rows - w.shape[0]), (0, 0)))


def _local_step(x, tgt, w):
    w1 = jnp.concatenate([w["w_in"][:, :7 * D], w["w_in"][:, 7 * D + 16:]], axis=1)
    w2 = jnp.pad(w["w_in"][:, 7 * D:7 * D + 16], ((0, 0), (0, 112)))
    wa = _pad_rows(w["conv_a_w"])
    wg = _pad_rows(w["gdn_conv_w"])
    wf = _pad_rows(w["ffn_conv_w"])
    alog = jnp.pad(w["gdn_A_log"].reshape(1, H), ((0, 0), (0, 128 - H)))
    dtb = jnp.pad(w["gdn_dt_bias"].reshape(1, H), ((0, 0), (0, 128 - H)))
    g1 = w["norm_mix_g"].reshape(1, D)
    g2 = w["norm_ffn_g"].reshape(1, D)
    g3 = w["norm_final_g"].reshape(1, D)
    gn = w["gdn_norm_g"].reshape(1, DH)

    h1 = _rms_fwd(x, g1, name="rms1_fwd")
    p1 = _matmul(h1, w1, name="mm_in")
    p2 = _matmul(h1, w2, name="mm_in_ab")
    ya_in, qn, kn, vc, gbeta = _pre_fwd(p1, p2, wa, wg, alog, dtb)
    o, s_all, t_all = _gdn_fwd(qn, kn, vc, gbeta)
    yb_in = _post_fwd(o, p1, gn)
    ya = _matmul(ya_in, w["w_a_out"], name="mm_a")
    yb = _matmul(yb_in, w["w_b_out"], name="mm_b")
    mix = _mix_fwd(ya, yb, p1)
    x2 = _matmul(mix, w["w_o"], name="mm_o", add=x)
    h2 = _rms_fwd(x2, g2, name="rms2_fwd")
    up = _matmul(h2, w["w_up"], name="mm_up")
    act = _ffn_fwd(up, wf)
    x3 = _matmul(act, w["w_down"], name="mm_down", add=x2)
    loss_p, dx3, dx3b, dg3 = _final(x3, tgt, g3)

    grads = {"norm_final_g": dg3}
    dact = _matmul(dx3b, w["w_down"].T, name="mm_down_dx")
    grads["w_down"] = _matmul_tn(act, dx3b, name="mm_down_dw")
    dc, dwf = _ffn_bwd1(dact, up, wf)
    grads["ffn_conv_w"] = dwf
    dup = _ffn_bwd2(dc, wf)
    dh2 = _matmul(dup, w["w_up"].T, name="mm_up_dx")
    grads["w_up"] = _matmul_tn(h2, dup, name="mm_up_dw")
    dx2, dx2b, dg2 = _rms_bwd(dh2, x2, g2, dx3, name="rms2_bwd")
    grads["norm_ffn_g"] = dg2
    dmix = _matmul(dx2b, w["w_o"].T, name="mm_o_dx")
    grads["w_o"] = _matmul_tn(mix, dx2b, name="mm_o_dw")
    dya, dyb, dgates = _mix_bwd(dmix, ya, yb, p1)
    dya_in = _matmul(dya, w["w_a_out"].T, name="mm_a_dx")
    grads["w_a_out"] = _matmul_tn(ya_in, dya, name="mm_a_dw")
    dyb_in = _matmul(dyb, w["w_b_out"].T, name="mm_b_dx")
    grads["w_b_out"] = _matmul_tn(yb_in, dyb, name="mm_b_dw")
    do, dz, dgn = _post_bwd(dyb_in, o, p1, gn)
    grads["gdn_norm_g"] = dgn
    dqn, dkn, dvc, dgb = _gdn_bwd(qn, kn, vc, gbeta, do, s_all, t_all)
    dbg, dca, dc4, dp2, dwa, dwg, dal, ddt = _pre_bwd1(p1, p2, dya_in, dqn, dkn, dvc, dgb, gbeta, wa, wg, alog, dtb)
    grads["conv_a_w"] = dwa
    grads["gdn_conv_w"] = dwg
    grads["gdn_A_log"] = dal
    grads["gdn_dt_bias"] = ddt
    dp1 = _pre_bwd2(dca, dc4, p1, dbg, dz, dgates, wa, wg)
    dh1 = _matmul(dp1, w1.T, name="mm_in_dx")
    dh1 = _matmul(dp2, w2.T, name="mm_in_ab_dx", add=dh1)
    dw1 = _matmul_tn(h1, dp1, name="mm_in_dw")
    dw2 = _matmul_tn(h1, dp2, name="mm_in_ab_dw")
    grads["w_in"] = jnp.concatenate([dw1[:, :7 * D], dw2[:, :16], dw1[:, 7 * D:]], axis=1)
    dx, _, dg1 = _rms_bwd(dh1, x, g1, dx2, name="rms1_bwd")
    grads["norm_mix_g"] = dg1
    return loss_p, dx, grads


_ANY = pl.BlockSpec(memory_space=pl.ANY)


def _all_gather(shard, *, name):
    rows, width = shard.shape

    def body(x_ref, out_ref, send_sems, recv_sems, local_sem):
        x, y, c = lax.axis_index("x"), lax.axis_index("y"), lax.axis_index("c")
        me, sibling = (x, y, c), (x, y, 1 - c)
        chips = [(1 - x, y), (x, 1 - y), (1 - x, 1 - y)]

        def block(px, py, pc):
            return out_ref.at[4 * px + 2 * py + pc]

        def copy(k, blk, to, src=None):
            return pltpu.make_async_remote_copy(
                src_ref=block(*blk) if src is None else src, dst_ref=block(*blk),
                send_sem=send_sems.at[k], recv_sem=recv_sems.at[k], device_id=to, device_id_type=MESH)

        mine = pltpu.make_async_copy(x_ref, block(*me), local_sem)
        mine.start()
        first = [copy(0, me, sibling, src=x_ref)]
        first += [copy(1 + j, me, (*chip, c), src=x_ref) for j, chip in enumerate(chips)]
        for cp in first:
            cp.start()
        passed = [copy(4 + j, (*chip, c), sibling) for j, chip in enumerate(chips)]
        for j, chip in enumerate(chips):
            copy(1 + j, (*chip, c), me).wait_recv()
            passed[j].start()
        copy(0, sibling, me).wait_recv()
        for j, chip in enumerate(chips):
            copy(4 + j, (*chip, 1 - c), me).wait_recv()
        for cp in first + passed:
            cp.wait_send()
        mine.wait()

    return pl.pallas_call(
        body, name=name, out_shape=jax.ShapeDtypeStruct((N_DEV, rows, width), shard.dtype),
        in_specs=[_ANY], out_specs=_ANY,
        scratch_shapes=[pltpu.SemaphoreType.DMA((7,)), pltpu.SemaphoreType.DMA((7,)), pltpu.SemaphoreType.DMA(())],
    )(shard)


def _half_to_bf16(pack, c_other):
    _, _, rows, width = pack.shape
    tb = R_BLOCK

    def body(c_ref, p_ref, o_ref):
        o_ref[...] = p_ref[0].astype(bf16)

    grid_spec = pltpu.PrefetchScalarGridSpec(
        num_scalar_prefetch=1, grid=(4, rows // tb),
        in_specs=[pl.BlockSpec((1, 1, tb, width), lambda j, i, c_ref: (c_ref[0], j, i, 0))],
        out_specs=pl.BlockSpec((1, tb, width), lambda j, i, c_ref: (j, i, 0)))
    return pl.pallas_call(
        body, name="rs_half_bf16", grid_spec=grid_spec, out_shape=jax.ShapeDtypeStruct((4, rows, width), bf16),
        compiler_params=_params(2),
    )(c_other, pack)


def _exchange_sibling(half):
    def body(p_ref, out_ref, send_sem, recv_sem):
        x, y, c = lax.axis_index("x"), lax.axis_index("y"), lax.axis_index("c")
        cp = pltpu.make_async_remote_copy(src_ref=p_ref, dst_ref=out_ref, send_sem=send_sem,
                                          recv_sem=recv_sem, device_id=(x, y, 1 - c), device_id_type=MESH)
        cp.start()
        cp.wait()

    return pl.pallas_call(
        body, name="rs_sibling", out_shape=jax.ShapeDtypeStruct(half.shape, half.dtype),
        in_specs=[_ANY], out_specs=_ANY,
        scratch_shapes=[pltpu.SemaphoreType.DMA(()), pltpu.SemaphoreType.DMA(())],
    )(half)


def _exchange_chips(hsum):
    _, rows, width = hsum.shape

    def body(h_ref, out_ref, send_sems, recv_sems):
        x, y, c = lax.axis_index("x"), lax.axis_index("y"), lax.axis_index("c")
        chips = [(1 - x, y), (x, 1 - y), (1 - x, 1 - y)]
        cps = [pltpu.make_async_remote_copy(src_ref=h_ref.at[2 * px + py], dst_ref=out_ref.at[k],
                                            send_sem=send_sems.at[k], recv_sem=recv_sems.at[k],
                                            device_id=(px, py, c), device_id_type=MESH)
               for k, (px, py) in enumerate(chips)]
        for cp in cps:
            cp.start()
        for cp in cps:
            cp.wait()

    return pl.pallas_call(
        body, name="rs_chips", out_shape=jax.ShapeDtypeStruct((3, rows, width), hsum.dtype),
        in_specs=[_ANY], out_specs=_ANY,
        scratch_shapes=[pltpu.SemaphoreType.DMA((3,)), pltpu.SemaphoreType.DMA((3,))],
    )(hsum)


def _pair_sum(pack, recv, c):
    _, _, rows, width = pack.shape
    tb = R_BLOCK

    def body(c_ref, p_ref, r_ref, o_ref, ob_ref):
        s = p_ref[0] + r_ref[...].astype(f32)
        o_ref[...] = s
        ob_ref[...] = s.astype(bf16)

    out_blk = pl.BlockSpec((1, tb, width), lambda j, i, c_ref: (j, i, 0))
    grid_spec = pltpu.PrefetchScalarGridSpec(
        num_scalar_prefetch=1, grid=(4, rows // tb),
        in_specs=[pl.BlockSpec((1, 1, tb, width), lambda j, i, c_ref: (c_ref[0], j, i, 0)), out_blk],
        out_specs=[out_blk, out_blk])
    return pl.pallas_call(
        body, name="rs_pair_sum", grid_spec=grid_spec,
        out_shape=[jax.ShapeDtypeStruct((4, rows, width), f32), jax.ShapeDtypeStruct((4, rows, width), bf16)],
        compiler_params=_params(2),
    )(c, pack, recv)


def _adam_math(w, g, m, v):
    m = ADAM_B1 * m + (1.0 - ADAM_B1) * g
    v = ADAM_B2 * v + (1.0 - ADAM_B2) * jnp.square(g)
    m_hat = m / (1.0 - ADAM_B1 ** ADAM_STEP)
    v_hat = v / (1.0 - ADAM_B2 ** ADAM_STEP)
    delta = -ADAM_LR * (m_hat / (jnp.sqrt(v_hat) + ADAM_EPS) + ADAM_WD * w)
    return delta, m, v


def _adam_sharded(hsum, recv, chip, w, m, v):
    rows, width = w.shape
    tb = R_BLOCK

    def body(j_ref, h_ref, r_ref, w_ref, m_ref, v_ref, g_out, d_out, m_out, v_out):
        g = ((h_ref[0] + r_ref[0].astype(f32)) + r_ref[1].astype(f32)) + r_ref[2].astype(f32)
        delta, mn, vn = _adam_math(w_ref[...], g, m_ref[...], v_ref[...])
        g_out[...] = g
        d_out[...] = delta
        m_out[...] = mn
        v_out[...] = vn

    blk = pl.BlockSpec((tb, width), lambda i, j_ref: (i, 0))
    grid_spec = pltpu.PrefetchScalarGridSpec(
        num_scalar_prefetch=1, grid=(rows // tb,),
        in_specs=[pl.BlockSpec((1, tb, width), lambda i, j_ref: (j_ref[0], i, 0)),
                  pl.BlockSpec((3, tb, width), lambda i, j_ref: (0, i, 0)), blk, blk, blk],
        out_specs=[blk, blk, blk, blk])
    return pl.pallas_call(
        body, name="adam_sharded", grid_spec=grid_spec, out_shape=[jax.ShapeDtypeStruct((rows, width), f32)] * 4,
        compiler_params=_params(1),
    )(chip, hsum, recv, w, m, v)


def _adam_replicated(gath, w, m, v):
    def body(ga_ref, w_ref, m_ref, v_ref, g_out, d_out, m_out, v_out):
        g = ga_ref[0]
        for d in range(1, N_DEV):
            g = g + ga_ref[d]
        delta, mn, vn = _adam_math(w_ref[...], g, m_ref[...], v_ref[...])
        g_out[...] = g
        d_out[...] = delta
        m_out[...] = mn
        v_out[...] = vn

    return pl.pallas_call(body, name="adam_replicated", out_shape=[jax.ShapeDtypeStruct((8, D), f32)] * 4)(gath, w, m, v)


_SHARDED = ("w_in", "w_up", "w_a_out", "w_b_out", "w_o", "w_down")
_CONVS = (("conv_a_w", 3, 128), ("gdn_conv_w", 4, 384), ("ffn_conv_w", 3, 704))


def _pack_shard(t):
    conv = jnp.concatenate([t[n].reshape(-1) for n, _, _ in _CONVS])
    conv = jnp.pad(conv, (0, 4096 - N_CONV)).reshape(4, D)
    parts = [t[n].reshape(-1, D) for n in _SHARDED] + [conv, jnp.zeros((R_ALL - R_MM - 4, D), f32)]
    return jnp.concatenate(parts, axis=0)


def _unpack_shard(p):
    out = {}
    r = 0
    for n, rows, shape in (("w_in", R_IN, (D, R_IN)), ("w_up", R_UP, (D, R_UP)), ("w_a_out", R_SQ, (R_SQ, D)),
                           ("w_b_out", R_SQ, (R_SQ, D)), ("w_o", R_SQ, (R_SQ, D)), ("w_down", R_DOWN, (R_DOWN, D))):
        out[n] = p[r:r + rows].reshape(shape)
        r += rows
    conv = p[R_MM:R_MM + 4].reshape(-1)
    off = 0
    for n, k, wd in _CONVS:
        out[n] = conv[off:off + k * wd].reshape(k, wd)
        off += k * wd
    return out


def _cols_from_devices(g, rows, width):
    return g.reshape(N_DEV, D, width).transpose(1, 0, 2).reshape(D, N_DEV * width)


def _unpack_gathered(gm, gc):
    w = {}
    r = 0
    w["w_in"] = _cols_from_devices(gm[:, r:r + R_IN], R_IN, R_IN)
    r += R_IN
    w["w_up"] = _cols_from_devices(gm[:, r:r + R_UP], R_UP, R_UP)
    r += R_UP
    for n in ("w_a_out", "w_b_out", "w_o"):
        w[n] = gm[:, r:r + R_SQ].reshape(D, D)
        r += R_SQ
    w["w_down"] = gm[:, r:r + R_DOWN].reshape(DFF, D)
    conv = gc[:, :4].reshape(N_DEV, 4096)
    off = 0
    for n, k, wd in _CONVS:
        w[n] = conv[:, off:off + k * wd].reshape(N_DEV, k, wd).transpose(1, 0, 2).reshape(k, N_DEV * wd)
        off += k * wd
    return w


def _pack_grads(g):
    def cols(a, width):
        return a.reshape(D, N_DEV, width).transpose(1, 0, 2).reshape(N_DEV, width, D)

    conv = jnp.concatenate(
        [g[n][:k].reshape(k, N_DEV, wd).transpose(1, 0, 2).reshape(N_DEV, k * wd) for n, k, wd in _CONVS], axis=1)
    conv = jnp.pad(conv, ((0, 0), (0, 4096 - N_CONV))).reshape(N_DEV, 4, D)
    parts = [cols(g["w_in"], R_IN), cols(g["w_up"], R_UP), g["w_a_out"].reshape(N_DEV, R_SQ, D),
             g["w_b_out"].reshape(N_DEV, R_SQ, D), g["w_o"].reshape(N_DEV, R_SQ, D),
             g["w_down"].reshape(N_DEV, R_DOWN, D), conv, jnp.zeros((N_DEV, R_ALL - R_MM - 4, D), f32)]
    pack = jnp.concatenate(parts, axis=1)
    return pack.reshape(4, 2, R_ALL, D).transpose(1, 0, 2, 3)


_REPL = ("norm_mix_g", "norm_ffn_g", "norm_final_g", "gdn_norm_g", "gdn_A_log", "gdn_dt_bias")
_LOSS_LANE = 256


def _pack_replicated(t, loss=None):
    row3 = jnp.concatenate([t["gdn_norm_g"].reshape(-1), t["gdn_A_log"].reshape(-1), t["gdn_dt_bias"].reshape(-1)])
    row3 = jnp.pad(row3, (0, D - row3.shape[0]))
    if loss is not None:
        row3 = row3.at[_LOSS_LANE].set(loss)
    rows = [t["norm_mix_g"].reshape(1, D), t["norm_ffn_g"].reshape(1, D), t["norm_final_g"].reshape(1, D),
            row3.reshape(1, D), jnp.zeros((4, D), f32)]
    return jnp.concatenate(rows, axis=0)


def _unpack_replicated(p, shapes):
    out = {"norm_mix_g": p[0], "norm_ffn_g": p[1], "norm_final_g": p[2], "gdn_norm_g": p[3, :DH],
           "gdn_A_log": p[3, DH:DH + H], "gdn_dt_bias": p[3, DH + H:DH + 2 * H]}
    return {n: out[n].reshape(shapes[n]) for n in _REPL}


_WEIGHTS = ("norm_mix_g", "w_in", "conv_a_w", "gdn_conv_w", "gdn_A_log", "gdn_dt_bias", "gdn_norm_g", "w_a_out",
            "w_b_out", "w_o", "norm_ffn_g", "w_up", "ffn_conv_w", "w_down", "norm_final_g")


def kernel(x, norm_mix_g, w_in, conv_a_w, gdn_conv_w, gdn_A_log, gdn_dt_bias, gdn_norm_g, w_a_out, w_b_out, w_o, norm_ffn_g, w_up, ffn_conv_w, w_down, norm_final_g, loss_target, m_norm_mix_g, m_w_in, m_conv_a_w, m_gdn_conv_w, m_gdn_A_log, m_gdn_dt_bias, m_gdn_norm_g, m_w_a_out, m_w_b_out, m_w_o, m_norm_ffn_g, m_w_up, m_ffn_conv_w, m_w_down, m_norm_final_g, v_norm_mix_g, v_w_in, v_conv_a_w, v_gdn_conv_w, v_gdn_A_log, v_gdn_dt_bias, v_gdn_norm_g, v_w_a_out, v_w_b_out, v_w_o, v_norm_ffn_g, v_w_up, v_ffn_conv_w, v_w_down, v_norm_final_g):
    wts = dict(zip(_WEIGHTS, (norm_mix_g, w_in, conv_a_w, gdn_conv_w, gdn_A_log, gdn_dt_bias, gdn_norm_g, w_a_out,
                              w_b_out, w_o, norm_ffn_g, w_up, ffn_conv_w, w_down, norm_final_g)))
    mom = dict(zip(_WEIGHTS, (m_norm_mix_g, m_w_in, m_conv_a_w, m_gdn_conv_w, m_gdn_A_log, m_gdn_dt_bias,
                              m_gdn_norm_g, m_w_a_out, m_w_b_out, m_w_o, m_norm_ffn_g, m_w_up, m_ffn_conv_w,
                              m_w_down, m_norm_final_g)))
    var = dict(zip(_WEIGHTS, (v_norm_mix_g, v_w_in, v_conv_a_w, v_gdn_conv_w, v_gdn_A_log, v_gdn_dt_bias,
                              v_gdn_norm_g, v_w_a_out, v_w_b_out, v_w_o, v_norm_ffn_g, v_w_up, v_ffn_conv_w,
                              v_w_down, v_norm_final_g)))
    shapes = {n: wts[n].shape for n in _WEIGHTS}
    sharded = tuple(n for n in _WEIGHTS if n not in _REPL)

    def drop_depth(t):
        return {n: t[n][0] for n in sharded}

    w_pack = _pack_shard(drop_depth(wts))
    m_pack = _pack_shard(drop_depth(mom))
    v_pack = _pack_shard(drop_depth(var))

    mm_shard = jnp.pad(w_pack[:R_MM].astype(bf16), ((0, R_MM_PAD - R_MM), (0, 0)))
    conv_shard = jnp.pad(w_pack[R_MM:R_MM + 4], ((0, 4), (0, 0)))
    full = _unpack_gathered(_all_gather(mm_shard, name="ag_weights"), _all_gather(conv_shard, name="ag_conv"))
    for n in _REPL:
        full[n] = wts[n]

    loss_p, dx, grads = _local_step(x[0], loss_target[0], full)

    cx, cy, cc = lax.axis_index("x"), lax.axis_index("y"), lax.axis_index("c")
    pack = _pack_grads(grads)
    c_me = jnp.reshape(cc, (1,)).astype(jnp.int32)
    hsum, hsum_b = _pair_sum(pack, _exchange_sibling(_half_to_bf16(pack, 1 - c_me)), c_me)
    recv = _exchange_chips(hsum_b)
    g_sh, d_sh, m_sh, v_sh = _adam_sharded(hsum, recv, jnp.reshape(2 * cx + cy, (1,)).astype(jnp.int32),
                                           w_pack, m_pack, v_pack)

    small = {"norm_mix_g": jnp.sum(grads["norm_mix_g"], axis=0), "norm_ffn_g": jnp.sum(grads["norm_ffn_g"], axis=0),
             "norm_final_g": jnp.sum(grads["norm_final_g"], axis=0), "gdn_norm_g": jnp.sum(grads["gdn_norm_g"], axis=0),
             "gdn_A_log": jnp.sum(grads["gdn_A_log"], axis=0)[:H], "gdn_dt_bias": jnp.sum(grads["gdn_dt_bias"], axis=0)[:H]}
    g_small = _pack_replicated(small, loss=jnp.sum(loss_p[:, 0]))
    g_rp, d_rp, m_rp, v_rp = _adam_replicated(_all_gather(g_small, name="ag_small"), _pack_replicated(wts),
                                              _pack_replicated(mom), _pack_replicated(var))

    outs = []
    for packed_sh, packed_rp in ((g_sh, g_rp), (d_sh, d_rp), (m_sh, m_rp), (v_sh, v_rp)):
        t = _unpack_shard(packed_sh)
        t = {n: t[n].reshape(shapes[n]) for n in sharded}
        t.update(_unpack_replicated(packed_rp, shapes))
        outs.append([t[n] for n in _WEIGHTS])
    loss = g_rp[3, _LOSS_LANE]
    return (loss, dx[None], *outs[0], *outs[1], *outs[2], *outs[3])
```

```python
import jax
import jax.numpy as jnp
from jax import lax
from jax.experimental import pallas as pl
from jax.experimental.pallas import tpu as pltpu

f32 = jnp.float32
bf16 = jnp.bfloat16

D = 1024
H = 8
DH = 128
CH = 64
DFF = 2816
NW1 = 9216
EPS = 1e-6
N_DEV = 8

ADAM_LR = 0.001
ADAM_B1 = 0.9
ADAM_B2 = 0.999
ADAM_EPS = 1e-08
ADAM_WD = 0.01
ADAM_STEP = 10

VMEM_LIMIT_BYTES = 48 * 1024 * 1024

R_IN, R_UP = 1154, 704

_HI = lax.Precision.HIGHEST
MESH = pl.DeviceIdType.MESH


def _params(n_grid):
    return pltpu.CompilerParams(dimension_semantics=("arbitrary",) * n_grid, vmem_limit_bytes=VMEM_LIMIT_BYTES)


def _bdot(a, b):
    return jnp.dot(a.astype(bf16), b.astype(bf16), preferred_element_type=f32)


def _bdot_nt(a, b):
    return lax.dot_general(a.astype(bf16), b.astype(bf16), (((1,), (1,)), ((), ())), preferred_element_type=f32)


def _bdot_tn(a, b):
    return lax.dot_general(a.astype(bf16), b.astype(bf16), (((0,), (0,)), ((), ())), preferred_element_type=f32)


def _hdot(a, b):
    return jnp.dot(a, b, preferred_element_type=f32, precision=_HI)


def _idot(a, b):
    return jnp.dot(a, b, preferred_element_type=f32, precision=lax.Precision.HIGH)


def _sigmoid(x):
    return 1.0 / (1.0 + jnp.exp(-x))


def _softplus(x):
    return jnp.maximum(x, 0.0) + jnp.log(1.0 + jnp.exp(-jnp.abs(x)))


def _shift_down(x, halo, j):
    if j == 0:
        return x
    xr = pltpu.roll(x, j, 0)
    hr = pltpu.roll(halo, j, 0)
    r8 = lax.broadcasted_iota(jnp.int32, hr.shape, 0)
    top = jnp.where(r8 < j, hr, xr[:8])
    return jnp.concatenate([top, xr[8:]], axis=0)


def _shift_up(x, halo, j):
    if j == 0:
        return x
    n = x.shape[0]
    xr = pltpu.roll(x, n - j, 0)
    hr = pltpu.roll(halo, 8 - j, 0)
    r8 = lax.broadcasted_iota(jnp.int32, hr.shape, 0)
    bot = jnp.where(r8 >= 8 - j, hr, xr[n - 8:])
    return jnp.concatenate([xr[:n - 8], bot], axis=0)


def _conv_down(x, halo, w_ref, k):
    out = w_ref[k - 1:k, :] * x
    for j in range(k - 1):
        out = out + w_ref[j:j + 1, :] * _shift_down(x, halo, k - 1 - j)
    return out


def _conv_up(dy, halo, w_ref, k):
    out = w_ref[k - 1:k, :] * dy
    for j in range(k - 1):
        out = out + w_ref[j:j + 1, :] * _shift_up(dy, halo, k - 1 - j)
    return out


def _row(tb, w, col=0):
    return pl.BlockSpec((tb, w), lambda i: (i, col))


def _prev(tb, w, col=0):
    return pl.BlockSpec((8, w), lambda i: (jnp.maximum(i * (tb // 8) - 1, 0), col))


def _next(tb, w, n_rows, col=0):
    last = n_rows // 8 - 1
    return pl.BlockSpec((8, w), lambda i: (jnp.minimum((i + 1) * (tb // 8), last), col))


def _fixed(shape):
    return pl.BlockSpec(shape, lambda i: (0,) * len(shape))


def _first_zero(halo_ref):
    return jnp.where(pl.program_id(0) == 0, 0.0, halo_ref[...])


def _last_zero(halo_ref, n_blocks):
    return jnp.where(pl.program_id(0) == n_blocks - 1, 0.0, halo_ref[...])


def _pick(n, prefs):
    for p in prefs:
        if n % p == 0:
            return p
    return n


def _matmul(a, b, *, name, out_dtype=f32, add=None):
    m, kd = a.shape
    _, n = b.shape
    tm = _pick(m, (1024, 512, 256))
    tn = _pick(n, (1024, 1408, 512, 128))
    tk = _pick(kd, (1024, 1408, 512, 128)) if kd > 1408 else kd
    nk = kd // tk

    def body(*refs):
        if add is None:
            a_ref, b_ref, o_ref = refs[:3]
            add_ref = None
        else:
            a_ref, b_ref, add_ref, o_ref = refs[:4]
        part = jnp.dot(a_ref[...].astype(bf16), b_ref[...].astype(bf16), preferred_element_type=f32)
        if nk == 1:
            if add_ref is not None:
                part = part + add_ref[...]
            o_ref[...] = part.astype(out_dtype)
            return
        acc_ref = refs[-1]
        k = pl.program_id(2)

        @pl.when(k == 0)
        def _():
            acc_ref[...] = part

        @pl.when(k > 0)
        def _():
            acc_ref[...] += part

        @pl.when(k == nk - 1)
        def _():
            res = acc_ref[...]
            if add_ref is not None:
                res = res + add_ref[...]
            o_ref[...] = res.astype(out_dtype)

    in_specs = [pl.BlockSpec((tm, tk), lambda i, j, k: (i, k)), pl.BlockSpec((tk, tn), lambda i, j, k: (k, j))]
    args = [a, b]
    if add is not None:
        in_specs.append(pl.BlockSpec((tm, tn), lambda i, j, k: (i, j)))
        args.append(add)
    return pl.pallas_call(
        body, name=name, grid=(m // tm, n // tn, nk), in_specs=in_specs,
        out_specs=pl.BlockSpec((tm, tn), lambda i, j, k: (i, j)),
        out_shape=jax.ShapeDtypeStruct((m, n), out_dtype),
        scratch_shapes=[pltpu.VMEM((tm, tn), f32)] if nk > 1 else [],
        compiler_params=_params(3),
    )(*args)


def _matmul_tn(a, b, *, name):
    t, m = a.shape
    _, n = b.shape
    tm = _pick(m, (1024, 1408, 512, 128))
    tn = _pick(n, (1024, 1408, 512, 128))
    tt = _pick(t, (1024, 512, 256))
    nt = t // tt

    def body(a_ref, b_ref, o_ref):
        k = pl.program_id(2)
        part = lax.dot_general(a_ref[...].astype(bf16), b_ref[...].astype(bf16), (((0,), (0,)), ((), ())),
                               preferred_element_type=f32)

        @pl.when(k == 0)
        def _():
            o_ref[...] = part

        @pl.when(k > 0)
        def _():
            o_ref[...] += part

    return pl.pallas_call(
        body, name=name, grid=(m // tm, n // tn, nt),
        in_specs=[pl.BlockSpec((tt, tm), lambda i, j, k: (k, i)), pl.BlockSpec((tt, tn), lambda i, j, k: (k, j))],
        out_specs=pl.BlockSpec((tm, tn), lambda i, j, k: (i, j)),
        out_shape=jax.ShapeDtypeStruct((m, n), f32),
        compiler_params=_params(3),
    )(a, b)


def _rms_fwd(x, g, *, name):
    t = x.shape[0]
    tb = _pick(t, (256, 128))

    def body(x_ref, g_ref, h_ref):
        xv = x_ref[...]
        r = lax.rsqrt(jnp.mean(xv * xv, axis=-1, keepdims=True) + EPS)
        h_ref[...] = (xv * r * g_ref[...]).astype(bf16)

    return pl.pallas_call(
        body, name=name, grid=(t // tb,), in_specs=[_row(tb, D), _fixed((1, D))], out_specs=_row(tb, D),
        out_shape=jax.ShapeDtypeStruct((t, D), bf16), compiler_params=_params(1),
    )(x, g)


def _rms_bwd(dh, x, g, dres, *, name):
    t = x.shape[0]
    tb = _pick(t, (256, 128))

    def body(dh_ref, x_ref, g_ref, dres_ref, dx_ref, dxb_ref, dg_ref):
        xv = x_ref[...]
        r = lax.rsqrt(jnp.mean(xv * xv, axis=-1, keepdims=True) + EPS)
        xh = xv * r
        dy = dh_ref[...]
        dyg = dy * g_ref[...]
        dx = dres_ref[...] + r * (dyg - xh * jnp.mean(dyg * xh, axis=-1, keepdims=True))
        dx_ref[...] = dx
        dxb_ref[...] = dx.astype(bf16)

        @pl.when(pl.program_id(0) == 0)
        def _():
            dg_ref[...] = jnp.zeros_like(dg_ref)

        dg_ref[...] += jnp.sum((dy * xh).reshape(tb // 8, 8, D), axis=0)

    return pl.pallas_call(
        body, name=name, grid=(t // tb,),
        in_specs=[_row(tb, D), _row(tb, D), _fixed((1, D)), _row(tb, D)],
        out_specs=[_row(tb, D), _row(tb, D), _fixed((8, D))],
        out_shape=[jax.ShapeDtypeStruct((t, D), f32), jax.ShapeDtypeStruct((t, D), bf16),
                   jax.ShapeDtypeStruct((8, D), f32)],
        compiler_params=_params(1),
    )(dh, x, g, dres)


def _gdn_gates(ab, alog, dtb):
    lane = lax.broadcasted_iota(jnp.int32, ab.shape, 1)
    g = -jnp.exp(alog) * _softplus(ab + dtb)
    beta = _sigmoid(ab)
    return jnp.where(lane < H, g, jnp.where(lane < 2 * H, beta, 0.0))


def _pre_fwd(p1, p2, wa, wg, alog, dtb):
    t = p1.shape[0]
    tb = 128

    def body(p0_ref, p0h_ref, pq_ref, pqh_ref, p2_ref, wa_ref, wg_ref, alog_ref, dtb_ref,
             ya_ref, qn_ref, kn_ref, vc_ref, gb_ref):
        p0 = p0_ref[...]
        h0 = _first_zero(p0h_ref)
        u = p0[:, D:2 * D] * p0[:, 2 * D:]
        uh = h0[:, D:2 * D] * h0[:, 2 * D:]
        ya_ref[...] = (p0[:, :D] * _conv_down(u, uh, wa_ref, 3)).astype(bf16)
        s = _conv_down(pq_ref[...], _first_zero(pqh_ref), wg_ref, 4)
        s = s * _sigmoid(s)
        for h in range(H):
            q = s[:, h * DH:(h + 1) * DH]
            k = s[:, D + h * DH:D + (h + 1) * DH]
            qn_ref[:, h * DH:(h + 1) * DH] = q * (lax.rsqrt(jnp.sum(q * q, axis=-1, keepdims=True) + EPS) * DH ** -0.5)
            kn_ref[:, h * DH:(h + 1) * DH] = k * lax.rsqrt(jnp.sum(k * k, axis=-1, keepdims=True) + EPS)
        vc_ref[...] = s[:, 2 * D:]
        gb_ref[...] = _gdn_gates(p2_ref[...], alog_ref[...], dtb_ref[...])

    return pl.pallas_call(
        body, name="pre_fwd", grid=(t // tb,),
        in_specs=[_row(tb, 3 * D, 0), _prev(tb, 3 * D, 0), _row(tb, 3 * D, 1), _prev(tb, 3 * D, 1), _row(tb, 128),
                  _fixed((8, D)), _fixed((8, 3 * D)), _fixed((1, 128)), _fixed((1, 128))],
        out_specs=[_row(tb, D), _row(tb, D), _row(tb, D), _row(tb, D), _row(tb, 128)],
        out_shape=[jax.ShapeDtypeStruct((t, D), bf16), jax.ShapeDtypeStruct((t, D), f32),
                   jax.ShapeDtypeStruct((t, D), f32), jax.ShapeDtypeStruct((t, D), f32),
                   jax.ShapeDtypeStruct((t, 128), f32)],
        compiler_params=_params(1),
    )(p1, p1, p1, p1, p2, wa, wg, alog, dtb)


def _post_fwd(o, p1, gn):
    t = o.shape[0]
    tb = _pick(t, (256, 128))

    def body(o_ref, z_ref, gn_ref, yb_ref):
        for h in range(H):
            sl = slice(h * DH, (h + 1) * DH)
            oh = o_ref[:, sl]
            z = z_ref[:, sl]
            r = lax.rsqrt(jnp.mean(oh * oh, axis=-1, keepdims=True) + EPS)
            yb_ref[:, sl] = (oh * r * gn_ref[...] * (z * _sigmoid(z))).astype(bf16)

    return pl.pallas_call(
        body, name="post_fwd", grid=(t // tb,), in_specs=[_row(tb, D), _row(tb, D, 6), _fixed((1, DH))],
        out_specs=_row(tb, D), out_shape=jax.ShapeDtypeStruct((t, D), bf16), compiler_params=_params(1),
    )(o, p1, gn)


def _post_bwd(dyb, o, p1, gn):
    t = o.shape[0]
    tb = _pick(t, (256, 128))

    def body(dyb_ref, o_ref, z_ref, gn_ref, do_ref, dz_ref, dgn_ref):
        @pl.when(pl.program_id(0) == 0)
        def _():
            dgn_ref[...] = jnp.zeros_like(dgn_ref)

        gn_v = gn_ref[...]
        acc = jnp.zeros((8, DH), f32)
        for h in range(H):
            sl = slice(h * DH, (h + 1) * DH)
            oh = o_ref[:, sl]
            z = z_ref[:, sl]
            dy = dyb_ref[:, sl]
            r = lax.rsqrt(jnp.mean(oh * oh, axis=-1, keepdims=True) + EPS)
            on = oh * r
            sg = _sigmoid(z)
            sz = z * sg
            don = dy * sz
            dz_ref[:, sl] = (dy * on * gn_v * (sg * (1.0 + z * (1.0 - sg)))).astype(bf16)
            acc = acc + jnp.sum((don * on).reshape(tb // 8, 8, DH), axis=0)
            doh = don * gn_v
            do_ref[:, sl] = r * (doh - on * jnp.mean(doh * on, axis=-1, keepdims=True))
        dgn_ref[...] += acc

    return pl.pallas_call(
        body, name="post_bwd", grid=(t // tb,),
        in_specs=[_row(tb, D), _row(tb, D), _row(tb, D, 6), _fixed((1, DH))],
        out_specs=[_row(tb, D), _row(tb, D), _fixed((8, DH))],
        out_shape=[jax.ShapeDtypeStruct((t, D), f32), jax.ShapeDtypeStruct((t, D), bf16),
                   jax.ShapeDtypeStruct((8, DH), f32)],
        compiler_params=_params(1),
    )(dyb, o, p1, gn)


def _mix_fwd(ya, yb, p1):
    t = ya.shape[0]
    tb = _pick(t, (256, 128))

    def body(ya_ref, yb_ref, ga_ref, gb_ref, mix_ref):
        mix_ref[...] = (_sigmoid(ga_ref[...]) * ya_ref[...] + _sigmoid(gb_ref[...]) * yb_ref[...]).astype(bf16)

    return pl.pallas_call(
        body, name="mix_fwd", grid=(t // tb,), in_specs=[_row(tb, D), _row(tb, D), _row(tb, D, 7), _row(tb, D, 8)],
        out_specs=_row(tb, D), out_shape=jax.ShapeDtypeStruct((t, D), bf16), compiler_params=_params(1),
    )(ya, yb, p1, p1)


def _mix_bwd(dmix, ya, yb, p1):
    t = ya.shape[0]
    tb = _pick(t, (256, 128))

    def body(dm_ref, ya_ref, yb_ref, ga_ref, gb_ref, dya_ref, dyb_ref, dg_ref):
        dm = dm_ref[...]
        sa = _sigmoid(ga_ref[...])
        sb = _sigmoid(gb_ref[...])
        dya_ref[...] = (dm * sa).astype(bf16)
        dyb_ref[...] = (dm * sb).astype(bf16)
        dg_ref[:, :D] = (dm * ya_ref[...] * sa * (1.0 - sa)).astype(bf16)
        dg_ref[:, D:] = (dm * yb_ref[...] * sb * (1.0 - sb)).astype(bf16)

    return pl.pallas_call(
        body, name="mix_bwd", grid=(t // tb,),
        in_specs=[_row(tb, D), _row(tb, D), _row(tb, D), _row(tb, D, 7), _row(tb, D, 8)],
        out_specs=[_row(tb, D), _row(tb, D), _row(tb, 2 * D)],
        out_shape=[jax.ShapeDtypeStruct((t, D), bf16), jax.ShapeDtypeStruct((t, D), bf16),
                   jax.ShapeDtypeStruct((t, 2 * D), bf16)],
        compiler_params=_params(1),
    )(dmix, ya, yb, p1, p1)


def _ffn_fwd(up, wf):
    t = up.shape[0]
    tb = 128

    def body(up_ref, uph_ref, wf_ref, act_ref):
        c = _conv_down(up_ref[...], _first_zero(uph_ref), wf_ref, 3)
        gate = c[:, :DFF]
        act_ref[...] = (gate * _sigmoid(gate) * c[:, DFF:]).astype(bf16)

    return pl.pallas_call(
        body, name="ffn_fwd", grid=(t // tb,), in_specs=[_row(tb, 2 * DFF), _prev(tb, 2 * DFF), _fixed((8, 2 * DFF))],
        out_specs=_row(tb, DFF), out_shape=jax.ShapeDtypeStruct((t, DFF), bf16), compiler_params=_params(1),
    )(up, up, wf)


def _ffn_bwd1(dact, up, wf):
    t = up.shape[0]
    tb = 128

    def body(da_ref, up_ref, uph_ref, wf_ref, dc_ref, dw_ref):
        @pl.when(pl.program_id(0) == 0)
        def _():
            dw_ref[...] = jnp.zeros_like(dw_ref)

        upv = up_ref[...]
        uph = _first_zero(uph_ref)
        c = _conv_down(upv, uph, wf_ref, 3)
        gate = c[:, :DFF]
        val = c[:, DFF:]
        sg = _sigmoid(gate)
        da = da_ref[...]
        dgate = da * val * (sg * (1.0 + gate * (1.0 - sg)))
        dval = da * (gate * sg)
        dc_ref[:, :DFF] = dgate
        dc_ref[:, DFF:] = dval
        dc = jnp.concatenate([dgate, dval], axis=1)
        for j in range(3):
            dw_ref[j:j + 1, :] += jnp.sum(dc * _shift_down(upv, uph, 2 - j), axis=0, keepdims=True)

    return pl.pallas_call(
        body, name="ffn_bwd1", grid=(t // tb,),
        in_specs=[_row(tb, DFF), _row(tb, 2 * DFF), _prev(tb, 2 * DFF), _fixed((8, 2 * DFF))],
        out_specs=[_row(tb, 2 * DFF), _fixed((8, 2 * DFF))],
        out_shape=[jax.ShapeDtypeStruct((t, 2 * DFF), f32), jax.ShapeDtypeStruct((8, 2 * DFF), f32)],
        compiler_params=_params(1),
    )(dact, up, up, wf)


def _ffn_bwd2(dc, wf):
    t = dc.shape[0]
    tb = 128
    nb = t // tb

    def body(dc_ref, dch_ref, wf_ref, dup_ref):
        dup_ref[...] = _conv_up(dc_ref[...], _last_zero(dch_ref, nb), wf_ref, 3).astype(bf16)

    return pl.pallas_call(
        body, name="ffn_bwd2", grid=(nb,), in_specs=[_row(tb, 2 * DFF), _next(tb, 2 * DFF, t), _fixed((8, 2 * DFF))],
        out_specs=_row(tb, 2 * DFF), out_shape=jax.ShapeDtypeStruct((t, 2 * DFF), bf16), compiler_params=_params(1),
    )(dc, dc, wf)


def _final(x3, tgt, g):
    t = x3.shape[0]
    tb = _pick(t, (256, 128))

    def body(x_ref, t_ref, g_ref, loss_ref, dx_ref, dxb_ref, dg_ref):
        @pl.when(pl.program_id(0) == 0)
        def _():
            loss_ref[...] = jnp.zeros_like(loss_ref)
            dg_ref[...] = jnp.zeros_like(dg_ref)

        xv = x_ref[...]
        r = lax.rsqrt(jnp.mean(xv * xv, axis=-1, keepdims=True) + EPS)
        xh = xv * r
        gv = g_ref[...]
        e = xh * gv - t_ref[...]
        lrow = 0.5 * jnp.mean(e * e, axis=-1, keepdims=True)
        loss_ref[...] += jnp.sum(jnp.broadcast_to(lrow, (tb, 128)).reshape(tb // 8, 8, 128), axis=0)
        dy = e * (1.0 / D)
        dyg = dy * gv
        dx = r * (dyg - xh * jnp.mean(dyg * xh, axis=-1, keepdims=True))
        dx_ref[...] = dx
        dxb_ref[...] = dx.astype(bf16)
        dg_ref[...] += jnp.sum((dy * xh).reshape(tb // 8, 8, D), axis=0)

    return pl.pallas_call(
        body, name="final", grid=(t // tb,), in_specs=[_row(tb, D), _row(tb, D), _fixed((1, D))],
        out_specs=[_fixed((8, 128)), _row(tb, D), _row(tb, D), _fixed((8, D))],
        out_shape=[jax.ShapeDtypeStruct((8, 128), f32), jax.ShapeDtypeStruct((t, D), f32),
                   jax.ShapeDtypeStruct((t, D), bf16), jax.ShapeDtypeStruct((8, D), f32)],
        compiler_params=_params(1),
    )(x3, tgt, g)


def _pre_bwd1(p1, p2, dya_in, dqn, dkn, dvc, dgb, gbeta, wa, wg, alog, dtb):
    t = p1.shape[0]
    tb = 128

    def body(p0_ref, p0h_ref, pq_ref, pqh_ref, p2_ref, dya_ref, dqn_ref, dkn_ref, dvc_ref, dgb_ref, gb_ref,
             wa_ref, wg_ref, alog_ref, dtb_ref,
             dbg_ref, dca_ref, dc4_ref, dp2_ref, dwa_ref, dwg_ref, dal_ref, ddt_ref):
        @pl.when(pl.program_id(0) == 0)
        def _():
            dwa_ref[...] = jnp.zeros_like(dwa_ref)
            dwg_ref[...] = jnp.zeros_like(dwg_ref)
            dal_ref[...] = jnp.zeros_like(dal_ref)
            ddt_ref[...] = jnp.zeros_like(ddt_ref)

        p0 = p0_ref[...]
        h0 = _first_zero(p0h_ref)
        u = p0[:, D:2 * D] * p0[:, 2 * D:]
        uh = h0[:, D:2 * D] * h0[:, 2 * D:]
        dya = dya_ref[...]
        dbg_ref[...] = (dya * _conv_down(u, uh, wa_ref, 3)).astype(bf16)
        dca = dya * p0[:, :D]
        dca_ref[...] = dca
        for j in range(3):
            dwa_ref[j:j + 1, :] += jnp.sum(dca * _shift_down(u, uh, 2 - j), axis=0, keepdims=True)

        pq = pq_ref[...]
        pqh = _first_zero(pqh_ref)
        c4 = _conv_down(pq, pqh, wg_ref, 4)
        sg = _sigmoid(c4)
        s = c4 * sg
        dsilu = sg * (1.0 + c4 * (1.0 - sg))
        for h in range(H):
            for base, d_ref, scale in ((0, dqn_ref, DH ** -0.5), (D, dkn_ref, 1.0)):
                sl = slice(base + h * DH, base + (h + 1) * DH)
                a = s[:, sl]
                r = lax.rsqrt(jnp.sum(a * a, axis=-1, keepdims=True) + EPS)
                an = a * r
                dn = d_ref[:, h * DH:(h + 1) * DH] * scale
                dc4_ref[:, sl] = r * (dn - an * jnp.sum(dn * an, axis=-1, keepdims=True)) * dsilu[:, sl]
        dc4_ref[:, 2 * D:] = dvc_ref[...] * dsilu[:, 2 * D:]
        dc4 = dc4_ref[...]
        for j in range(4):
            dwg_ref[j:j + 1, :] += jnp.sum(dc4 * _shift_down(pq, pqh, 3 - j), axis=0, keepdims=True)

        ab = p2_ref[...]
        lane = lax.broadcasted_iota(jnp.int32, ab.shape, 1)
        dgbv = dgb_ref[...]
        gbv = gb_ref[...]
        da = dgbv * (-jnp.exp(alog_ref[...])) * _sigmoid(ab + dtb_ref[...])
        db = dgbv * gbv * (1.0 - gbv)
        dp2_ref[...] = jnp.where(lane < H, da, jnp.where(lane < 2 * H, db, 0.0)).astype(bf16)
        dal = jnp.where(lane < H, dgbv * gbv, 0.0)
        ddt = jnp.where(lane < H, da, 0.0)
        dal_ref[...] += jnp.sum(dal.reshape(tb // 8, 8, 128), axis=0)
        ddt_ref[...] += jnp.sum(ddt.reshape(tb // 8, 8, 128), axis=0)

    return pl.pallas_call(
        body, name="pre_bwd1", grid=(t // tb,),
        in_specs=[_row(tb, 3 * D, 0), _prev(tb, 3 * D, 0), _row(tb, 3 * D, 1), _prev(tb, 3 * D, 1), _row(tb, 128),
                  _row(tb, D), _row(tb, D), _row(tb, D), _row(tb, D), _row(tb, 128), _row(tb, 128),
                  _fixed((8, D)), _fixed((8, 3 * D)), _fixed((1, 128)), _fixed((1, 128))],
        out_specs=[_row(tb, D), _row(tb, D), _row(tb, 3 * D), _row(tb, 128),
                   _fixed((8, D)), _fixed((8, 3 * D)), _fixed((8, 128)), _fixed((8, 128))],
        out_shape=[jax.ShapeDtypeStruct((t, D), bf16), jax.ShapeDtypeStruct((t, D), f32),
                   jax.ShapeDtypeStruct((t, 3 * D), f32), jax.ShapeDtypeStruct((t, 128), bf16),
                   jax.ShapeDtypeStruct((8, D), f32), jax.ShapeDtypeStruct((8, 3 * D), f32),
                   jax.ShapeDtypeStruct((8, 128), f32), jax.ShapeDtypeStruct((8, 128), f32)],
        compiler_params=_params(1),
    )(p1, p1, p1, p1, p2, dya_in, dqn, dkn, dvc, dgb, gbeta, wa, wg, alog, dtb)


def _pre_bwd2(dca, dc4, p1, dbg, dz, dgates, wa, wg):
    t = p1.shape[0]
    tb = 128
    nb = t // tb

    def body(dca_ref, dcah_ref, dc4_ref, dc4h_ref, p0_ref, dbg_ref, dz_ref, dgt_ref, wa_ref, wg_ref, dp_ref):
        du = _conv_up(dca_ref[...], _last_zero(dcah_ref, nb), wa_ref, 3)
        dp_ref[:, :D] = dbg_ref[...]
        dp_ref[:, D:2 * D] = (du * p0_ref[:, 2 * D:]).astype(bf16)
        dp_ref[:, 2 * D:3 * D] = (du * p0_ref[:, D:2 * D]).astype(bf16)
        dp_ref[:, 3 * D:6 * D] = _conv_up(dc4_ref[...], _last_zero(dc4h_ref, nb), wg_ref, 4).astype(bf16)
        dp_ref[:, 6 * D:7 * D] = dz_ref[...]
        dp_ref[:, 7 * D:] = dgt_ref[...]

    return pl.pallas_call(
        body, name="pre_bwd2", grid=(nb,),
        in_specs=[_row(tb, D), _next(tb, D, t), _row(tb, 3 * D), _next(tb, 3 * D, t), _row(tb, 3 * D, 0),
                  _row(tb, D), _row(tb, D), _row(tb, 2 * D), _fixed((8, D)), _fixed((8, 3 * D))],
        out_specs=_row(tb, NW1), out_shape=jax.ShapeDtypeStruct((t, NW1), bf16), compiler_params=_params(1),
    )(dca, dca, dc4, dc4, p1, dbg, dz, dgates, wa, wg)


def _chunk_consts():
    r = lax.broadcasted_iota(jnp.int32, (CH, CH), 0)
    c = lax.broadcasted_iota(jnp.int32, (CH, CH), 1)
    return r, c, (r == c).astype(f32)


def _tri_inverse(lows, eye):
    xps = [-low for low in lows]
    invs = [eye + xp for xp in xps]
    for _ in range(5):
        xps = [_idot(xp, xp) for xp in xps]
        invs = [inv + _idot(inv, xp) for inv, xp in zip(invs, xps)]
    return invs


def _chunk_common(q, k, v, gcol, bcol, r, c, eye):
    grow = jnp.sum(eye * gcol, axis=0, keepdims=True)
    dec = jnp.exp(jnp.where(r >= c, gcol - grow, -jnp.inf))
    rcol = lax.broadcasted_iota(jnp.int32, (CH, 1), 0)
    glast = jnp.sum(jnp.where(rcol == CH - 1, gcol, 0.0), axis=0, keepdims=True)
    eg = jnp.exp(gcol)
    el = jnp.exp(glast - gcol)
    kb = k * bcol
    vb = v * bcol
    kk = _bdot_nt(kb, k)
    low = jnp.where(r > c, kk * dec, 0.0)
    qk = _bdot_nt(q, k)
    att = qk * dec
    return grow, dec, glast, eg, el, kb, vb, kk, low, qk, att, rcol


def _gdn_fwd(qn, kn, vc, gbeta):
    t = qn.shape[0]
    n_chunks = t // CH

    def body(q_ref, k_ref, v_ref, gb_ref, o_ref, s_ref, t_ref, state):
        @pl.when(pl.program_id(0) == 0)
        def _():
            state[...] = jnp.zeros_like(state)

        r, c, eye = _chunk_consts()
        gb = gb_ref[...]
        gall = _hdot((r >= c).astype(f32), gb)
        heads = range(H)
        qs = [q_ref[:, h * DH:(h + 1) * DH] for h in heads]
        ks = [k_ref[:, h * DH:(h + 1) * DH] for h in heads]
        vs = [v_ref[:, h * DH:(h + 1) * DH] for h in heads]
        sts = [state[h] for h in heads]
        cm = [_chunk_common(qs[h], ks[h], vs[h], gall[:, h:h + 1], gb[:, H + h:H + h + 1], r, c, eye) for h in heads]
        invs = _tri_inverse([m[8] for m in cm], eye)
        uws = [_bdot(invs[h], jnp.concatenate([cm[h][6], cm[h][5] * cm[h][3]], axis=1)) for h in heads]
        vns = [uws[h][:, :DH] - _bdot(uws[h][:, DH:], sts[h]) for h in heads]
        outs = [_bdot(qs[h] * cm[h][3], sts[h]) + _bdot(cm[h][10], vns[h]) for h in heads]
        news = [sts[h] * jnp.exp(cm[h][2]) + _bdot_tn(ks[h] * cm[h][4], vns[h]) for h in heads]
        for h in heads:
            s_ref[0, h] = sts[h].astype(bf16)
            t_ref[0, h] = invs[h]
            o_ref[:, h * DH:(h + 1) * DH] = outs[h]
            state[h] = news[h]

    return pl.pallas_call(
        body, name="gdn_fwd", grid=(n_chunks,),
        in_specs=[_row(CH, D), _row(CH, D), _row(CH, D), _row(CH, 128)],
        out_specs=[_row(CH, D), pl.BlockSpec((1, H, DH, DH), lambda i: (i, 0, 0, 0)),
                   pl.BlockSpec((1, H, CH, CH), lambda i: (i, 0, 0, 0))],
        out_shape=[jax.ShapeDtypeStruct((t, D), f32), jax.ShapeDtypeStruct((n_chunks, H, DH, DH), bf16),
                   jax.ShapeDtypeStruct((n_chunks, H, CH, CH), f32)],
        scratch_shapes=[pltpu.VMEM((H, DH, DH), f32)],
        compiler_params=_params(1),
    )(qn, kn, vc, gbeta)


def _gdn_bwd(qn, kn, vc, gbeta, do, s_all, t_all):
    t = qn.shape[0]
    n_chunks = t // CH

    def body(q_ref, k_ref, v_ref, gb_ref, do_ref, s_ref, t_ref, dq_ref, dk_ref, dv_ref, dgb_ref, dstate):
        @pl.when(pl.program_id(0) == 0)
        def _():
            dstate[...] = jnp.zeros_like(dstate)

        r, c, eye = _chunk_consts()
        tril = r >= c
        gb = gb_ref[...]
        gall = _hdot(tril.astype(f32), gb)
        lane = lax.broadcasted_iota(jnp.int32, (1, 128), 1)
        hs = range(H)

        def each(fn, *lists):
            return [fn(*args) for args in zip(*lists)]

        def rsum(a):
            return jnp.sum(a, axis=1, keepdims=True)

        q = [q_ref[:, h * DH:(h + 1) * DH] for h in hs]
        k = [k_ref[:, h * DH:(h + 1) * DH] for h in hs]
        v = [v_ref[:, h * DH:(h + 1) * DH] for h in hs]
        dout = [do_ref[:, h * DH:(h + 1) * DH] for h in hs]
        inv = [t_ref[0, h] for h in hs]
        st = [s_ref[0, h] for h in hs]
        ds = [dstate[h] for h in hs]
        bcol = [gb[:, H + h:H + h + 1] for h in hs]
        cm = [_chunk_common(q[h], k[h], v[h], gall[:, h:h + 1], bcol[h], r, c, eye) for h in hs]
        dec, glast, eg, el, kb, vb, low, att = ([m[i] for m in cm] for i in (1, 2, 3, 4, 5, 6, 8, 10))
        rcol = cm[0][11]
        elast = each(jnp.exp, glast)
        kbg = each(jnp.multiply, kb, eg)
        uw = each(lambda i, a, b: _bdot(i, jnp.concatenate([a, b], axis=1)), inv, vb, kbg)
        u = [a[:, :DH] for a in uw]
        w = [a[:, DH:] for a in uw]
        vn = each(lambda a, b, s: a - _bdot(b, s), u, w, st)
        qd = each(jnp.multiply, q, eg)
        kd = each(jnp.multiply, k, el)
        dvn = each(lambda a, d, kk, s: _bdot_tn(a, d) + _bdot(kk, s), att, dout, kd, ds)
        dqd = each(_bdot_nt, dout, st)
        datt = each(lambda d, x: jnp.where(tril, _bdot_nt(d, x), 0.0), dout, vn)
        dkd = each(_bdot_nt, vn, ds)
        dw = each(lambda a, s: -_bdot_nt(a, s), dvn, st)
        new_ds = each(lambda s, e, a, d, ww, dv_: s * e + _bdot_tn(a, d) - _bdot_tn(ww, dv_), ds, elast, qd, dout, w, dvn)
        dglast = each(lambda e, s, d: e * jnp.sum(rsum(s.astype(f32) * d), axis=0, keepdims=True), elast, st, ds)
        dr = each(lambda i, a, b: _bdot_tn(i, jnp.concatenate([a, b], axis=1)), inv, dvn, dw)
        dvb = [a[:, :DH] for a in dr]
        dkbg = [a[:, DH:] for a in dr]
        dlow = each(lambda a, b, x, y: -jnp.where(r > c, _bdot_nt(a, b) + _bdot_nt(x, y), 0.0), dvb, u, dkbg, w)
        dkk = each(jnp.multiply, dlow, dec)
        dqk = each(jnp.multiply, datt, dec)
        mm = each(lambda a, b, x, y: a * b + x * y, dlow, low, datt, att)
        dkb = each(lambda a, kk, b, e: _bdot(a, kk) + b * e, dkk, k, dkbg, eg)
        dk = each(lambda a, b, x, y, d, e, f, g: _bdot_tn(a, b) + _bdot_tn(x, y) + d * e + f * g,
                  dkk, kb, dqk, q, dkd, el, dkb, bcol)
        dq = each(lambda a, kk, d, e: _bdot(a, kk) + d * e, dqk, k, dqd, eg)
        dv = each(jnp.multiply, dvb, bcol)
        dbeta = each(lambda a, b, x, y: rsum(a * b) + rsum(x * y), dkb, k, dvb, v)
        deg = each(lambda a, b, x, y: rsum(a * b) + rsum(x * y), dkbg, kb, dqd, q)
        delc = each(lambda a, b, e: rsum(a * b) * e, dkd, k, el)
        dgc = each(lambda m, a, e, d: rsum(m) - rsum(eye * jnp.sum(m, axis=0, keepdims=True)) + a * e - d, mm, deg, eg, delc)
        dgc = each(lambda g, d, l: g + jnp.where(rcol == CH - 1, jnp.sum(d, axis=0, keepdims=True) + l, 0.0),
                   dgc, delc, dglast)
        dg_acc = jnp.zeros((CH, 128), f32)
        db_acc = jnp.zeros((CH, 128), f32)
        for h in hs:
            dq_ref[:, h * DH:(h + 1) * DH] = dq[h]
            dk_ref[:, h * DH:(h + 1) * DH] = dk[h]
            dv_ref[:, h * DH:(h + 1) * DH] = dv[h]
            dstate[h] = new_ds[h]
            dg_acc = dg_acc + dgc[h] * (lane == h).astype(f32)
            db_acc = db_acc + dbeta[h] * (lane == H + h).astype(f32)
        dgb_ref[...] = _hdot((r <= c).astype(f32), dg_acc) + db_acc

    rev = lambda i: (n_chunks - 1 - i, 0)
    rev4 = lambda i: (n_chunks - 1 - i, 0, 0, 0)
    return pl.pallas_call(
        body, name="gdn_bwd", grid=(n_chunks,),
        in_specs=[pl.BlockSpec((CH, D), rev), pl.BlockSpec((CH, D), rev), pl.BlockSpec((CH, D), rev),
                  pl.BlockSpec((CH, 128), rev), pl.BlockSpec((CH, D), rev),
                  pl.BlockSpec((1, H, DH, DH), rev4), pl.BlockSpec((1, H, CH, CH), rev4)],
        out_specs=[pl.BlockSpec((CH, D), rev), pl.BlockSpec((CH, D), rev), pl.BlockSpec((CH, D), rev),
                   pl.BlockSpec((CH, 128), rev)],
        out_shape=[jax.ShapeDtypeStruct((t, D), f32)] * 3 + [jax.ShapeDtypeStruct((t, 128), f32)],
        scratch_shapes=[pltpu.VMEM((H, DH, DH), f32)],
        compiler_params=_params(1),
    )(qn, kn, vc, gbeta, do, s_all, t_all)


def _pad_rows(w, rows=8):
    return jnp.pad(w, ((0, rows - w.shape[0]), (0, 0)))


def _local_step(x, tgt, w):
    w1, w2 = w["w1"], w["w2"]
    wa = _pad_rows(w["conv_a_w"])
    wg = _pad_rows(w["gdn_conv_w"])
    wf = _pad_rows(w["ffn_conv_w"])
    alog = jnp.pad(w["gdn_A_log"].reshape(1, H), ((0, 0), (0, 128 - H)))
    dtb = jnp.pad(w["gdn_dt_bias"].reshape(1, H), ((0, 0), (0, 128 - H)))
    g1 = w["norm_mix_g"].reshape(1, D)
    g2 = w["norm_ffn_g"].reshape(1, D)
    g3 = w["norm_final_g"].reshape(1, D)
    gn = w["gdn_norm_g"].reshape(1, DH)

    h1 = _rms_fwd(x, g1, name="rms1_fwd")
    p1 = _matmul(h1, w1, name="mm_in")
    p2 = _matmul(h1, w2, name="mm_in_ab")
    ya_in, qn, kn, vc, gbeta = _pre_fwd(p1, p2, wa, wg, alog, dtb)
    o, s_all, t_all = _gdn_fwd(qn, kn, vc, gbeta)
    yb_in = _post_fwd(o, p1, gn)
    ya = _matmul(ya_in, w["w_a_out"], name="mm_a")
    yb = _matmul(yb_in, w["w_b_out"], name="mm_b")
    mix = _mix_fwd(ya, yb, p1)
    x2 = _matmul(mix, w["w_o"], name="mm_o", add=x)
    h2 = _rms_fwd(x2, g2, name="rms2_fwd")
    up = _matmul(h2, w["w_up"], name="mm_up")
    act = _ffn_fwd(up, wf)
    x3 = _matmul(act, w["w_down"], name="mm_down", add=x2)
    loss_p, dx3, dx3b, dg3 = _final(x3, tgt, g3)

    grads = {"norm_final_g": dg3}
    dact = _matmul(dx3b, w["w_down"].T, name="mm_down_dx")
    grads["w_down"] = _matmul_tn(act, dx3b, name="mm_down_dw")
    dc, dwf = _ffn_bwd1(dact, up, wf)
    grads["ffn_conv_w"] = dwf
    dup = _ffn_bwd2(dc, wf)
    dh2 = _matmul(dup, w["w_up"].T, name="mm_up_dx")
    grads["w_up"] = _matmul_tn(h2, dup, name="mm_up_dw")
    dx2, dx2b, dg2 = _rms_bwd(dh2, x2, g2, dx3, name="rms2_bwd")
    grads["norm_ffn_g"] = dg2
    dmix = _matmul(dx2b, w["w_o"].T, name="mm_o_dx")
    grads["w_o"] = _matmul_tn(mix, dx2b, name="mm_o_dw")
    dya, dyb, dgates = _mix_bwd(dmix, ya, yb, p1)
    dya_in = _matmul(dya, w["w_a_out"].T, name="mm_a_dx")
    grads["w_a_out"] = _matmul_tn(ya_in, dya, name="mm_a_dw")
    dyb_in = _matmul(dyb, w["w_b_out"].T, name="mm_b_dx")
    grads["w_b_out"] = _matmul_tn(yb_in, dyb, name="mm_b_dw")
    do, dz, dgn = _post_bwd(dyb_in, o, p1, gn)
    grads["gdn_norm_g"] = dgn
    dqn, dkn, dvc, dgb = _gdn_bwd(qn, kn, vc, gbeta, do, s_all, t_all)
    dbg, dca, dc4, dp2, dwa, dwg, dal, ddt = _pre_bwd1(p1, p2, dya_in, dqn, dkn, dvc, dgb, gbeta, wa, wg, alog, dtb)
    grads["conv_a_w"] = dwa
    grads["gdn_conv_w"] = dwg
    grads["gdn_A_log"] = dal
    grads["gdn_dt_bias"] = ddt
    dp1 = _pre_bwd2(dca, dc4, p1, dbg, dz, dgates, wa, wg)
    dh1 = _matmul(dp1, w1.T, name="mm_in_dx")
    dh1 = _matmul(dp2, w2.T, name="mm_in_ab_dx", add=dh1)
    grads["w1"] = _matmul_tn(h1, dp1, name="mm_in_dw")
    grads["w2"] = _matmul_tn(h1, dp2, name="mm_in_ab_dw")
    dx, _, dg1 = _rms_bwd(dh1, x, g1, dx2, name="rms1_bwd")
    grads["norm_mix_g"] = dg1
    return loss_p, dx, grads


_ANY = pl.BlockSpec(memory_space=pl.ANY)


def _all_gather_many(shards, *, name):
    n = len(shards)

    def body(*refs):
        x_refs, out_refs = refs[:n], refs[n:2 * n]
        send_sems, recv_sems, local_sems = refs[2 * n:]
        x, y, c = lax.axis_index("x"), lax.axis_index("y"), lax.axis_index("c")
        me, sibling = (x, y, c), (x, y, 1 - c)
        chips = [(1 - x, y), (x, 1 - y), (1 - x, 1 - y)]

        def copy(a, k, blk, to, from_input=False):
            dst = out_refs[a].at[4 * blk[0] + 2 * blk[1] + blk[2]]
            return pltpu.make_async_remote_copy(
                src_ref=x_refs[a] if from_input else dst, dst_ref=dst,
                send_sem=send_sems.at[a, k], recv_sem=recv_sems.at[a, k], device_id=to, device_id_type=MESH)

        mine = [pltpu.make_async_copy(x_refs[a], out_refs[a].at[4 * x + 2 * y + c], local_sems.at[a]) for a in range(n)]
        for cp in mine:
            cp.start()
        first = []
        for a in range(n):
            first.append(copy(a, 0, me, sibling, from_input=True))
            first += [copy(a, 1 + j, me, (*chip, c), from_input=True) for j, chip in enumerate(chips)]
        for cp in first:
            cp.start()
        passed = []
        for j, chip in enumerate(chips):
            for a in range(n):
                copy(a, 1 + j, (*chip, c), me).wait_recv()
                passed.append(copy(a, 4 + j, (*chip, c), sibling))
                passed[-1].start()
        for a in range(n):
            copy(a, 0, sibling, me).wait_recv()
            for j, chip in enumerate(chips):
                copy(a, 4 + j, (*chip, 1 - c), me).wait_recv()
        for cp in first + passed:
            cp.wait_send()
        for cp in mine:
            cp.wait()

    return pl.pallas_call(
        body, name=name, out_shape=[jax.ShapeDtypeStruct((N_DEV, *s.shape), s.dtype) for s in shards],
        in_specs=[_ANY] * n, out_specs=[_ANY] * n,
        scratch_shapes=[pltpu.SemaphoreType.DMA((n, 7)), pltpu.SemaphoreType.DMA((n, 7)), pltpu.SemaphoreType.DMA((n,))],
    )(*shards)


def _exchange_sibling_many(halves):
    n = len(halves)

    def body(*refs):
        p_refs, out_refs = refs[:n], refs[n:2 * n]
        send_sems, recv_sems = refs[2 * n:]
        x, y, c = lax.axis_index("x"), lax.axis_index("y"), lax.axis_index("c")
        cps = [pltpu.make_async_remote_copy(src_ref=p_refs[a], dst_ref=out_refs[a], send_sem=send_sems.at[a],
                                            recv_sem=recv_sems.at[a], device_id=(x, y, 1 - c), device_id_type=MESH)
               for a in range(n)]
        for cp in cps:
            cp.start()
        for cp in cps:
            cp.wait()

    return pl.pallas_call(
        body, name="rs_sibling", out_shape=[jax.ShapeDtypeStruct(h.shape, h.dtype) for h in halves],
        in_specs=[_ANY] * n, out_specs=[_ANY] * n,
        scratch_shapes=[pltpu.SemaphoreType.DMA((n,)), pltpu.SemaphoreType.DMA((n,))],
    )(*halves)


def _exchange_chips_many(hsums):
    n = len(hsums)

    def body(*refs):
        h_refs, out_refs = refs[:n], refs[n:2 * n]
        send_sems, recv_sems = refs[2 * n:]
        x, y, c = lax.axis_index("x"), lax.axis_index("y"), lax.axis_index("c")
        chips = [(1 - x, y), (x, 1 - y), (1 - x, 1 - y)]
        cps = [pltpu.make_async_remote_copy(src_ref=h_refs[a].at[2 * px + py], dst_ref=out_refs[a].at[k],
                                            send_sem=send_sems.at[a, k], recv_sem=recv_sems.at[a, k],
                                            device_id=(px, py, c), device_id_type=MESH)
               for a in range(n) for k, (px, py) in enumerate(chips)]
        for cp in cps:
            cp.start()
        for cp in cps:
            cp.wait()

    return pl.pallas_call(
        body, name="rs_chips", out_shape=[jax.ShapeDtypeStruct((3, *h.shape[1:]), h.dtype) for h in hsums],
        in_specs=[_ANY] * n, out_specs=[_ANY] * n,
        scratch_shapes=[pltpu.SemaphoreType.DMA((n, 3)), pltpu.SemaphoreType.DMA((n, 3))],
    )(*hsums)


_IN_RANGES = ((0, 7 * D, 0, 0), (7 * D, 7 * D + 16, 1, 0), (7 * D + 16, 9 * D + 16, 0, 7 * D))
_UP_RANGES = ((0, 2 * DFF, 0, 0),)


def _col_pieces(width, ranges):
    pieces = []
    for d in range(N_DEV):
        lo, hi = d * width, (d + 1) * width
        for glo, ghi, mat, mlo in ranges:
            a, b = max(lo, glo), min(hi, ghi)
            if a < b:
                pieces.append((d, a - lo, b - lo, mat, mlo + a - glo))
    return pieces


def _cols_to_matrices(g, ranges, out_widths, *, name):
    _, rows, width = g.shape
    tb = 128
    pieces = _col_pieces(width, ranges)
    covered = [sum(p[2] - p[1] for p in pieces if p[3] == m) for m in range(len(out_widths))]

    def body(g_ref, *o_refs):
        for m, o_ref in enumerate(o_refs):
            if covered[m] < out_widths[m]:
                o_ref[...] = jnp.zeros_like(o_ref)
        for d, b0, b1, m, m0 in pieces:
            o_refs[m][:, m0:m0 + b1 - b0] = g_ref[d, :, b0:b1]

    return pl.pallas_call(
        body, name=name, grid=(rows // tb,), in_specs=[pl.BlockSpec((N_DEV, tb, width), lambda i: (0, i, 0))],
        out_specs=[pl.BlockSpec((tb, wo), lambda i: (i, 0)) for wo in out_widths],
        out_shape=[jax.ShapeDtypeStruct((rows, wo), g.dtype) for wo in out_widths], compiler_params=_params(1),
    )(g)


def _matrices_to_cols(mats, ranges, width, *, name):
    rows = mats[0].shape[0]
    tb = 128
    pieces = _col_pieces(width, ranges)

    def body(*refs):
        m_refs, g_ref = refs[:-1], refs[-1]
        for d, b0, b1, m, m0 in pieces:
            g_ref[d, :, b0:b1] = m_refs[m][:, m0:m0 + b1 - b0]

    return pl.pallas_call(
        body, name=name, grid=(rows // tb,),
        in_specs=[pl.BlockSpec((tb, mt.shape[1]), lambda i: (i, 0)) for mt in mats],
        out_specs=pl.BlockSpec((N_DEV, tb, width), lambda i: (0, i, 0)),
        out_shape=jax.ShapeDtypeStruct((N_DEV, rows, width), mats[0].dtype), compiler_params=_params(1),
    )(*mats)


def _row_block(rows):
    return 128 if rows % 128 == 0 else rows


def _half_bf16(g4, c_other, *, name):
    _, _, rows, width = g4.shape
    tb = _row_block(rows)

    def body(c_ref, p_ref, o_ref):
        o_ref[0] = p_ref[0, 0].astype(bf16)

    grid_spec = pltpu.PrefetchScalarGridSpec(
        num_scalar_prefetch=1, grid=(4, rows // tb),
        in_specs=[pl.BlockSpec((1, 1, tb, width), lambda j, i, c_ref: (j, c_ref[0], i, 0))],
        out_specs=pl.BlockSpec((1, tb, width), lambda j, i, c_ref: (j, i, 0)))
    return pl.pallas_call(
        body, name=name, grid_spec=grid_spec, out_shape=jax.ShapeDtypeStruct((4, rows, width), bf16),
        compiler_params=_params(2),
    )(c_other, g4)


def _pair_sum(g4, recv, c_me, *, name):
    _, _, rows, width = g4.shape
    tb = _row_block(rows)

    def body(c_ref, p_ref, r_ref, o_ref, ob_ref):
        s = p_ref[0, 0] + r_ref[0].astype(f32)
        o_ref[0] = s
        ob_ref[0] = s.astype(bf16)

    blk = pl.BlockSpec((1, tb, width), lambda j, i, c_ref: (j, i, 0))
    grid_spec = pltpu.PrefetchScalarGridSpec(
        num_scalar_prefetch=1, grid=(4, rows // tb),
        in_specs=[pl.BlockSpec((1, 1, tb, width), lambda j, i, c_ref: (j, c_ref[0], i, 0)), blk],
        out_specs=[blk, blk])
    return pl.pallas_call(
        body, name=name, grid_spec=grid_spec,
        out_shape=[jax.ShapeDtypeStruct((4, rows, width), f32), jax.ShapeDtypeStruct((4, rows, width), bf16)],
        compiler_params=_params(2),
    )(c_me, g4, recv)


def _adam_shard(hsum, recv, chip, w, m, v, *, name):
    _, rows, width = w.shape
    tb = _row_block(rows)

    def body(j_ref, h_ref, r_ref, w_ref, m_ref, v_ref, g_out, d_out, m_out, v_out):
        g = ((h_ref[0] + r_ref[0].astype(f32)) + r_ref[1].astype(f32)) + r_ref[2].astype(f32)
        delta, mn, vn = _adam_math(w_ref[0], g, m_ref[0], v_ref[0])
        g_out[0] = g
        d_out[0] = delta
        m_out[0] = mn
        v_out[0] = vn

    blk = pl.BlockSpec((1, tb, width), lambda i, j_ref: (0, i, 0))
    grid_spec = pltpu.PrefetchScalarGridSpec(
        num_scalar_prefetch=1, grid=(rows // tb,),
        in_specs=[pl.BlockSpec((1, tb, width), lambda i, j_ref: (j_ref[0], i, 0)),
                  pl.BlockSpec((3, tb, width), lambda i, j_ref: (0, i, 0)), blk, blk, blk],
        out_specs=[blk, blk, blk, blk])
    return pl.pallas_call(
        body, name=name, grid_spec=grid_spec, out_shape=[jax.ShapeDtypeStruct(w.shape, f32)] * 4,
        compiler_params=_params(1),
    )(chip, hsum, recv, w, m, v)


R_SMALL = 16 + 16 * N_DEV
_SMALL_LANES = {"gdn_norm_g": (0, DH), "gdn_A_log": (DH, DH + H), "gdn_dt_bias": (2 * DH, 2 * DH + H)}
_LOSS_LANE = 3 * DH


def _pack_small(dg1, dg2, dg3, dgn, dal, ddt, loss_p, dwa, dwg, dwf):
    def body(dg1_ref, dg2_ref, dg3_ref, dgn_ref, dal_ref, ddt_ref, loss_ref, dwa_ref, dwg_ref, dwf_ref, o_ref):
        def total(ref):
            return jnp.sum(ref[...], axis=0, keepdims=True)

        o_ref[...] = jnp.zeros_like(o_ref)
        o_ref[0:1, :] = total(dg1_ref)
        o_ref[1:2, :] = total(dg2_ref)
        o_ref[2:3, :] = total(dg3_ref)
        o_ref[3:4, 0:DH] = total(dgn_ref)
        o_ref[3:4, DH:2 * DH] = total(dal_ref)
        o_ref[3:4, 2 * DH:3 * DH] = total(ddt_ref)
        o_ref[3:4, 3 * DH:4 * DH] = total(loss_ref)
        for d in range(N_DEV):
            base = 16 + 16 * d
            o_ref[base:base + 3, 0:128] = dwa_ref[0:3, 128 * d:128 * (d + 1)]
            o_ref[base:base + 4, 128:512] = dwg_ref[0:4, 384 * d:384 * (d + 1)]
            o_ref[base + 8:base + 11, 0:704] = dwf_ref[0:3, 704 * d:704 * (d + 1)]

    return pl.pallas_call(body, name="pack_small", out_shape=jax.ShapeDtypeStruct((R_SMALL, D), f32))(
        dg1, dg2, dg3, dgn, dal, ddt, loss_p, dwa, dwg, dwf)


_SMALL = ("norm_mix_g", "norm_ffn_g", "norm_final_g", "gdn_norm_g", "gdn_A_log", "gdn_dt_bias",
          "conv_a_w", "gdn_conv_w", "ffn_conv_w")


def _adam_small(gath, me, w, m, v):
    arrays = [t[n] for n in _SMALL for t in (w, m, v)]

    def body(me_ref, ga_ref, gb_ref, *refs):
        ins, outs = refs[:len(arrays)], refs[len(arrays):]
        ga, gb = ga_ref[0], gb_ref[0]
        for s in range(1, N_DEV):
            ga = ga + ga_ref[s]
            gb = gb + gb_ref[s]
        grads = {"norm_mix_g": ga[0:1, :], "norm_ffn_g": ga[1:2, :], "norm_final_g": ga[2:3, :],
                 "conv_a_w": gb[0:3, 0:128], "gdn_conv_w": gb[0:4, 128:512], "ffn_conv_w": gb[8:11, 0:704]}
        for n, (lo, hi) in _SMALL_LANES.items():
            grads[n] = ga[3:4, lo:hi]
        for i, n in enumerate(_SMALL):
            three_d = len(w[n].shape) == 3
            wv, mv, vv = (r[0] if three_d else r[...] for r in ins[3 * i:3 * i + 3])
            delta, mn, vn = _adam_math(wv, grads[n], mv, vv)
            for o_ref, val in zip(outs[4 * i:4 * i + 4], (grads[n], delta, mn, vn)):
                if three_d:
                    o_ref[0] = val
                else:
                    o_ref[...] = val
        outs[-1][...] = ga[3:4, _LOSS_LANE:_LOSS_LANE + 1]

    def whole(shape):
        return pl.BlockSpec(shape, lambda i, me_ref: (0,) * len(shape))

    grid_spec = pltpu.PrefetchScalarGridSpec(
        num_scalar_prefetch=1, grid=(1,),
        in_specs=[pl.BlockSpec((N_DEV, 16, D), lambda i, me_ref: (0, 0, 0)),
                  pl.BlockSpec((N_DEV, 16, D), lambda i, me_ref: (0, 1 + me_ref[0], 0))] + [whole(a.shape) for a in arrays],
        out_specs=[whole(w[n].shape) for n in _SMALL for _ in range(4)] + [whole((1, 1))])
    res = pl.pallas_call(
        body, name="adam_small", grid_spec=grid_spec,
        out_shape=[jax.ShapeDtypeStruct(w[n].shape, f32) for n in _SMALL for _ in range(4)]
        + [jax.ShapeDtypeStruct((1, 1), f32)],
        compiler_params=_params(1),
    )(me, gath, gath, *arrays)
    return {n: tuple(res[4 * i:4 * i + 4]) for i, n in enumerate(_SMALL)}, res[-1]


def _adam_math(w, g, m, v):
    m = ADAM_B1 * m + (1.0 - ADAM_B1) * g
    v = ADAM_B2 * v + (1.0 - ADAM_B2) * jnp.square(g)
    m_hat = m / (1.0 - ADAM_B1 ** ADAM_STEP)
    v_hat = v / (1.0 - ADAM_B2 ** ADAM_STEP)
    delta = -ADAM_LR * (m_hat / (jnp.sqrt(v_hat) + ADAM_EPS) + ADAM_WD * w)
    return delta, m, v


_WEIGHTS = ("norm_mix_g", "w_in", "conv_a_w", "gdn_conv_w", "gdn_A_log", "gdn_dt_bias", "gdn_norm_g", "w_a_out",
            "w_b_out", "w_o", "norm_ffn_g", "w_up", "ffn_conv_w", "w_down", "norm_final_g")
_BIG = ("w_in", "w_up", "w_a_out", "w_b_out", "w_o", "w_down")
_CONVS = ("conv_a_w", "gdn_conv_w", "ffn_conv_w")


def kernel(x, norm_mix_g, w_in, conv_a_w, gdn_conv_w, gdn_A_log, gdn_dt_bias, gdn_norm_g, w_a_out, w_b_out, w_o, norm_ffn_g, w_up, ffn_conv_w, w_down, norm_final_g, loss_target, m_norm_mix_g, m_w_in, m_conv_a_w, m_gdn_conv_w, m_gdn_A_log, m_gdn_dt_bias, m_gdn_norm_g, m_w_a_out, m_w_b_out, m_w_o, m_norm_ffn_g, m_w_up, m_ffn_conv_w, m_w_down, m_norm_final_g, v_norm_mix_g, v_w_in, v_conv_a_w, v_gdn_conv_w, v_gdn_A_log, v_gdn_dt_bias, v_gdn_norm_g, v_w_a_out, v_w_b_out, v_w_o, v_norm_ffn_g, v_w_up, v_ffn_conv_w, v_w_down, v_norm_final_g):
    wts = dict(zip(_WEIGHTS, (norm_mix_g, w_in, conv_a_w, gdn_conv_w, gdn_A_log, gdn_dt_bias, gdn_norm_g, w_a_out,
                              w_b_out, w_o, norm_ffn_g, w_up, ffn_conv_w, w_down, norm_final_g)))
    mom = dict(zip(_WEIGHTS, (m_norm_mix_g, m_w_in, m_conv_a_w, m_gdn_conv_w, m_gdn_A_log, m_gdn_dt_bias,
                              m_gdn_norm_g, m_w_a_out, m_w_b_out, m_w_o, m_norm_ffn_g, m_w_up, m_ffn_conv_w,
                              m_w_down, m_norm_final_g)))
    var = dict(zip(_WEIGHTS, (v_norm_mix_g, v_w_in, v_conv_a_w, v_gdn_conv_w, v_gdn_A_log, v_gdn_dt_bias,
                              v_gdn_norm_g, v_w_a_out, v_w_b_out, v_w_o, v_norm_ffn_g, v_w_up, v_ffn_conv_w,
                              v_w_down, v_norm_final_g)))
    cx, cy, cc = lax.axis_index("x"), lax.axis_index("y"), lax.axis_index("c")
    c_me = jnp.reshape(cc, (1,)).astype(jnp.int32)
    chip = jnp.reshape(2 * cx + cy, (1,)).astype(jnp.int32)
    me = jnp.reshape(4 * cx + 2 * cy + cc, (1,)).astype(jnp.int32)

    gath = _all_gather_many([wts[n][0].astype(bf16) for n in _BIG] + [wts[n][0] for n in _CONVS], name="ag_weights")
    g_in, g_up, g_a, g_b, g_o, g_down, gc_a, gc_g, gc_f = gath
    w1, w2 = _cols_to_matrices(g_in, _IN_RANGES, (NW1, 128), name="relay_w_in")
    (w_up_full,) = _cols_to_matrices(g_up, _UP_RANGES, (2 * DFF,), name="relay_w_up")
    full = {"w1": w1, "w2": w2, "w_up": w_up_full, "w_a_out": g_a.reshape(D, D), "w_b_out": g_b.reshape(D, D),
            "w_o": g_o.reshape(D, D), "w_down": g_down.reshape(DFF, D),
            "conv_a_w": gc_a.transpose(1, 0, 2).reshape(3, D), "gdn_conv_w": gc_g.transpose(1, 0, 2).reshape(4, 3 * D),
            "ffn_conv_w": gc_f.transpose(1, 0, 2).reshape(3, 2 * DFF)}
    for n in ("norm_mix_g", "norm_ffn_g", "norm_final_g", "gdn_norm_g", "gdn_A_log", "gdn_dt_bias"):
        full[n] = wts[n]

    loss_p, dx, grads = _local_step(x[0], loss_target[0], full)

    g4 = {"w_in": _matrices_to_cols([grads["w1"], grads["w2"]], _IN_RANGES, R_IN, name="relay_dw_in"),
          "w_up": _matrices_to_cols([grads["w_up"]], _UP_RANGES, R_UP, name="relay_dw_up")}
    for n in ("w_a_out", "w_b_out", "w_o", "w_down"):
        g4[n] = grads[n]
    g4 = {n: g4[n].reshape(4, 2, wts[n].shape[1], wts[n].shape[2]) for n in _BIG}
    recv1 = _exchange_sibling_many([_half_bf16(g4[n], 1 - c_me, name="rs_half_" + n) for n in _BIG])
    sums = [_pair_sum(g4[n], r, c_me, name="rs_sum_" + n) for n, r in zip(_BIG, recv1)]
    recv2 = _exchange_chips_many([s[1] for s in sums])
    res = {n: _adam_shard(s[0], r, chip, wts[n], mom[n], var[n], name="adam_" + n)
           for n, s, r in zip(_BIG, sums, recv2)}

    small = _pack_small(grads["norm_mix_g"], grads["norm_ffn_g"], grads["norm_final_g"], grads["gdn_norm_g"],
                        grads["gdn_A_log"], grads["gdn_dt_bias"], loss_p, grads["conv_a_w"], grads["gdn_conv_w"],
                        grads["ffn_conv_w"])
    (small_all,) = _all_gather_many([small], name="ag_small")

    def raw(t):
        return {n: t[n].reshape(1, D) if n == "norm_final_g" else t[n] for n in _SMALL}

    res_small, loss = _adam_small(small_all, me, raw(wts), raw(mom), raw(var))
    for n in _SMALL:
        res[n] = tuple(a.reshape(wts[n].shape) for a in res_small[n])
    outs = [[res[n][i] for n in _WEIGHTS] for i in range(4)]
    return (loss.reshape(()), dx[None], *outs[0], *outs[1], *outs[2], *outs[3])
```

```python
import jax
import jax.numpy as jnp
from jax import lax
from jax.experimental import pallas as pl
from jax.experimental.pallas import tpu as pltpu

f32 = jnp.float32
bf16 = jnp.bfloat16

D = 1024
H = 8
DH = 128
CH = 64
DFF = 2816
NW1 = 9216
EPS = 1e-6
N_DEV = 8

ADAM_LR = 0.001
ADAM_B1 = 0.9
ADAM_B2 = 0.999
ADAM_EPS = 1e-08
ADAM_WD = 0.01
ADAM_STEP = 10

VMEM_LIMIT_BYTES = 48 * 1024 * 1024

R_IN, R_UP = 1154, 704

_HI = lax.Precision.HIGHEST
MESH = pl.DeviceIdType.MESH


def _params(n_grid):
    return pltpu.CompilerParams(dimension_semantics=("arbitrary",) * n_grid, vmem_limit_bytes=VMEM_LIMIT_BYTES)


def _bdot(a, b):
    return jnp.dot(a.astype(bf16), b.astype(bf16), preferred_element_type=f32)


def _bdot_nt(a, b):
    return lax.dot_general(a.astype(bf16), b.astype(bf16), (((1,), (1,)), ((), ())), preferred_element_type=f32)


def _bdot_tn(a, b):
    return lax.dot_general(a.astype(bf16), b.astype(bf16), (((0,), (0,)), ((), ())), preferred_element_type=f32)


def _hdot(a, b):
    return jnp.dot(a, b, preferred_element_type=f32, precision=_HI)


def _idot(a, b):
    return jnp.dot(a, b, preferred_element_type=f32, precision=lax.Precision.HIGH)


def _sigmoid(x):
    return 1.0 / (1.0 + jnp.exp(-x))


def _softplus(x):
    return jnp.maximum(x, 0.0) + jnp.log(1.0 + jnp.exp(-jnp.abs(x)))


def _shift_down(x, halo, j):
    if j == 0:
        return x
    xr = pltpu.roll(x, j, 0)
    hr = pltpu.roll(halo, j, 0)
    r8 = lax.broadcasted_iota(jnp.int32, hr.shape, 0)
    top = jnp.where(r8 < j, hr, xr[:8])
    return jnp.concatenate([top, xr[8:]], axis=0)


def _shift_up(x, halo, j):
    if j == 0:
        return x
    n = x.shape[0]
    xr = pltpu.roll(x, n - j, 0)
    hr = pltpu.roll(halo, 8 - j, 0)
    r8 = lax.broadcasted_iota(jnp.int32, hr.shape, 0)
    bot = jnp.where(r8 >= 8 - j, hr, xr[n - 8:])
    return jnp.concatenate([xr[:n - 8], bot], axis=0)


def _conv_down(x, halo, w_ref, k):
    out = w_ref[k - 1:k, :] * x
    for j in range(k - 1):
        out = out + w_ref[j:j + 1, :] * _shift_down(x, halo, k - 1 - j)
    return out


def _conv_up(dy, halo, w_ref, k):
    out = w_ref[k - 1:k, :] * dy
    for j in range(k - 1):
        out = out + w_ref[j:j + 1, :] * _shift_up(dy, halo, k - 1 - j)
    return out


def _row(tb, w, col=0):
    return pl.BlockSpec((tb, w), lambda i: (i, col))


def _prev(tb, w, col=0):
    return pl.BlockSpec((8, w), lambda i: (jnp.maximum(i * (tb // 8) - 1, 0), col))


def _next(tb, w, n_rows, col=0):
    last = n_rows // 8 - 1
    return pl.BlockSpec((8, w), lambda i: (jnp.minimum((i + 1) * (tb // 8), last), col))


def _fixed(shape):
    return pl.BlockSpec(shape, lambda i: (0,) * len(shape))


def _first_zero(halo_ref):
    return jnp.where(pl.program_id(0) == 0, 0.0, halo_ref[...])


def _last_zero(halo_ref, n_blocks):
    return jnp.where(pl.program_id(0) == n_blocks - 1, 0.0, halo_ref[...])


def _pick(n, prefs):
    for p in prefs:
        if n % p == 0:
            return p
    return n


def _matmul(a, b, *, name, nt=False, add=None, tm=1024, tn=1024, tk=None, exchange=None):
    m, kd = a.shape
    n = b.shape[0] if nt else b.shape[1]
    tm = _pick(m, (tm, 512, 256))
    tn = _pick(n, (tn, 1024, 512, 128))
    tk = kd if tk is None else tk
    nk = kd // tk
    dims = (((1,), (1,)), ((), ())) if nt else (((1,), (0,)), ((), ()))

    def body(a_ref, b_ref, *rest):
        o_ref = rest[-1]
        part = lax.dot_general(a_ref[...], b_ref[...], dims, preferred_element_type=f32)
        if nk == 1:
            o_ref[...] = part if add is None else part + rest[0][...]
            return
        k = pl.program_id(2)

        @pl.when(k == 0)
        def _():
            o_ref[...] = part if add is None else part + rest[0][...]

        @pl.when(k > 0)
        def _():
            o_ref[...] += part

    b_spec = pl.BlockSpec((tn, tk), lambda i, j, k: (j, k)) if nt else pl.BlockSpec((tk, tn), lambda i, j, k: (k, j))
    in_specs = [pl.BlockSpec((tm, tk), lambda i, j, k: (i, k)), b_spec]
    args = [a, b]
    if add is not None:
        in_specs.append(pl.BlockSpec((tm, tn), lambda i, j, k: (i, j)))
        args.append(add)
    return _call_with_exchange(
        body, exchange, name=name, grid=(m // tm, n // tn, nk), in_specs=in_specs,
        out_specs=pl.BlockSpec((tm, tn), lambda i, j, k: (i, j)),
        out_shape=jax.ShapeDtypeStruct((m, n), f32), args=args)


def _call_with_exchange(body, exchange, *, name, grid, in_specs, out_specs, out_shape, args):
    if exchange is None:
        return pl.pallas_call(body, name=name, grid=grid, in_specs=in_specs, out_specs=out_specs, out_shape=out_shape,
                              compiler_params=_params(len(grid)))(*args)
    x_arrays, x_shapes, x_sems, start, wait = exchange
    n_in, n_xin, n_xout = len(args), len(x_arrays), len(x_shapes)

    def full_body(*refs):
        c_in, x_in = refs[:n_in], refs[n_in:n_in + n_xin]
        c_out = refs[n_in + n_xin]
        x_out = refs[n_in + n_xin + 1:n_in + n_xin + 1 + n_xout]
        sems = refs[n_in + n_xin + 1 + n_xout:]
        ids = [pl.program_id(d) for d in range(len(grid))]
        first, last = ids[0] == 0, ids[0] == grid[0] - 1
        for d in range(1, len(grid)):
            first = first & (ids[d] == 0)
            last = last & (ids[d] == grid[d] - 1)

        @pl.when(first)
        def _():
            start(x_in, x_out, sems)

        body(*c_in, c_out)

        @pl.when(last)
        def _():
            wait(x_in, x_out, sems)

    res = pl.pallas_call(
        full_body, name=name, grid=grid, in_specs=list(in_specs) + [_ANY] * n_xin,
        out_specs=[out_specs] + [_ANY] * n_xout, out_shape=[out_shape] + list(x_shapes),
        scratch_shapes=list(x_sems), compiler_params=_params(len(grid)),
    )(*args, *x_arrays)
    return res[0], list(res[1:])


def _matmul_tn(a, b, *, name, tm=1024, tn=1024, exchange=None):
    t, m = a.shape
    _, n = b.shape
    tm = _pick(m, (tm, 1024, 512, 128))
    tn = _pick(n, (tn, 1024, 512, 128))
    tt = _pick(t, (2048, 1024, 512, 256))
    nt = t // tt

    def body(a_ref, b_ref, o_ref):
        k = pl.program_id(2)
        part = lax.dot_general(a_ref[...], b_ref[...], (((0,), (0,)), ((), ())), preferred_element_type=f32)

        @pl.when(k == 0)
        def _():
            o_ref[...] = part

        @pl.when(k > 0)
        def _():
            o_ref[...] += part

    return _call_with_exchange(
        body, exchange, name=name, grid=(m // tm, n // tn, nt),
        in_specs=[pl.BlockSpec((tt, tm), lambda i, j, k: (k, i)), pl.BlockSpec((tt, tn), lambda i, j, k: (k, j))],
        out_specs=pl.BlockSpec((tm, tn), lambda i, j, k: (i, j)),
        out_shape=jax.ShapeDtypeStruct((m, n), f32), args=[a, b])


def _rms_fwd(x, g, *, name):
    t = x.shape[0]
    tb = _pick(t, (256, 128))

    def body(x_ref, g_ref, h_ref):
        xv = x_ref[...]
        r = lax.rsqrt(jnp.mean(xv * xv, axis=-1, keepdims=True) + EPS)
        h_ref[...] = (xv * r * g_ref[...]).astype(bf16)

    return pl.pallas_call(
        body, name=name, grid=(t // tb,), in_specs=[_row(tb, D), _fixed((1, D))], out_specs=_row(tb, D),
        out_shape=jax.ShapeDtypeStruct((t, D), bf16), compiler_params=_params(1),
    )(x, g)


def _rms_bwd(dh, x, g, dres, *, name):
    t = x.shape[0]
    tb = _pick(t, (256, 128))

    def body(dh_ref, x_ref, g_ref, dres_ref, dx_ref, dxb_ref, dg_ref):
        xv = x_ref[...]
        r = lax.rsqrt(jnp.mean(xv * xv, axis=-1, keepdims=True) + EPS)
        xh = xv * r
        dy = dh_ref[...]
        dyg = dy * g_ref[...]
        dx = dres_ref[...] + r * (dyg - xh * jnp.mean(dyg * xh, axis=-1, keepdims=True))
        dx_ref[...] = dx
        dxb_ref[...] = dx.astype(bf16)

        @pl.when(pl.program_id(0) == 0)
        def _():
            dg_ref[...] = jnp.zeros_like(dg_ref)

        dg_ref[...] += jnp.sum((dy * xh).reshape(tb // 8, 8, D), axis=0)

    return pl.pallas_call(
        body, name=name, grid=(t // tb,),
        in_specs=[_row(tb, D), _row(tb, D), _fixed((1, D)), _row(tb, D)],
        out_specs=[_row(tb, D), _row(tb, D), _fixed((8, D))],
        out_shape=[jax.ShapeDtypeStruct((t, D), f32), jax.ShapeDtypeStruct((t, D), bf16),
                   jax.ShapeDtypeStruct((8, D), f32)],
        compiler_params=_params(1),
    )(dh, x, g, dres)


def _gdn_gates(ab, alog, dtb):
    lane = lax.broadcasted_iota(jnp.int32, ab.shape, 1)
    g = -jnp.exp(alog) * _softplus(ab + dtb)
    beta = _sigmoid(ab)
    return jnp.where(lane < H, g, jnp.where(lane < 2 * H, beta, 0.0))


def _pre_fwd(p1, p2, wa, wg, alog, dtb):
    t = p1.shape[0]
    tb = 128

    def body(p0_ref, p0h_ref, pq_ref, pqh_ref, p2_ref, wa_ref, wg_ref, alog_ref, dtb_ref,
             ya_ref, qn_ref, kn_ref, vc_ref, gb_ref):
        p0 = p0_ref[...]
        h0 = _first_zero(p0h_ref)
        u = p0[:, D:2 * D] * p0[:, 2 * D:]
        uh = h0[:, D:2 * D] * h0[:, 2 * D:]
        ya_ref[...] = (p0[:, :D] * _conv_down(u, uh, wa_ref, 3)).astype(bf16)
        s = _conv_down(pq_ref[...], _first_zero(pqh_ref), wg_ref, 4)
        s = s * _sigmoid(s)
        for h in range(H):
            q = s[:, h * DH:(h + 1) * DH]
            k = s[:, D + h * DH:D + (h + 1) * DH]
            qn_ref[:, h * DH:(h + 1) * DH] = q * (lax.rsqrt(jnp.sum(q * q, axis=-1, keepdims=True) + EPS) * DH ** -0.5)
            kn_ref[:, h * DH:(h + 1) * DH] = k * lax.rsqrt(jnp.sum(k * k, axis=-1, keepdims=True) + EPS)
        vc_ref[...] = s[:, 2 * D:]
        gb_ref[...] = _gdn_gates(p2_ref[...], alog_ref[...], dtb_ref[...])

    return pl.pallas_call(
        body, name="pre_fwd", grid=(t // tb,),
        in_specs=[_row(tb, 3 * D, 0), _prev(tb, 3 * D, 0), _row(tb, 3 * D, 1), _prev(tb, 3 * D, 1), _row(tb, 128),
                  _fixed((8, D)), _fixed((8, 3 * D)), _fixed((1, 128)), _fixed((1, 128))],
        out_specs=[_row(tb, D), _row(tb, D), _row(tb, D), _row(tb, D), _row(tb, 128)],
        out_shape=[jax.ShapeDtypeStruct((t, D), bf16), jax.ShapeDtypeStruct((t, D), f32),
                   jax.ShapeDtypeStruct((t, D), f32), jax.ShapeDtypeStruct((t, D), f32),
                   jax.ShapeDtypeStruct((t, 128), f32)],
        compiler_params=_params(1),
    )(p1, p1, p1, p1, p2, wa, wg, alog, dtb)


def _post_fwd(o, p1, gn):
    t = o.shape[0]
    tb = _pick(t, (256, 128))

    def body(o_ref, z_ref, gn_ref, yb_ref):
        for h in range(H):
            sl = slice(h * DH, (h + 1) * DH)
            oh = o_ref[:, sl]
            z = z_ref[:, sl]
            r = lax.rsqrt(jnp.mean(oh * oh, axis=-1, keepdims=True) + EPS)
            yb_ref[:, sl] = (oh * r * gn_ref[...] * (z * _sigmoid(z))).astype(bf16)

    return pl.pallas_call(
        body, name="post_fwd", grid=(t // tb,), in_specs=[_row(tb, D), _row(tb, D, 6), _fixed((1, DH))],
        out_specs=_row(tb, D), out_shape=jax.ShapeDtypeStruct((t, D), bf16), compiler_params=_params(1),
    )(o, p1, gn)


def _post_bwd(dyb, o, p1, gn):
    t = o.shape[0]
    tb = _pick(t, (256, 128))

    def body(dyb_ref, o_ref, z_ref, gn_ref, do_ref, dz_ref, dgn_ref):
        @pl.when(pl.program_id(0) == 0)
        def _():
            dgn_ref[...] = jnp.zeros_like(dgn_ref)

        gn_v = gn_ref[...]
        acc = jnp.zeros((8, DH), f32)
        for h in range(H):
            sl = slice(h * DH, (h + 1) * DH)
            oh = o_ref[:, sl]
            z = z_ref[:, sl]
            dy = dyb_ref[:, sl]
            r = lax.rsqrt(jnp.mean(oh * oh, axis=-1, keepdims=True) + EPS)
            on = oh * r
            sg = _sigmoid(z)
            sz = z * sg
            don = dy * sz
            dz_ref[:, sl] = (dy * on * gn_v * (sg * (1.0 + z * (1.0 - sg)))).astype(bf16)
            acc = acc + jnp.sum((don * on).reshape(tb // 8, 8, DH), axis=0)
            doh = don * gn_v
            do_ref[:, sl] = r * (doh - on * jnp.mean(doh * on, axis=-1, keepdims=True))
        dgn_ref[...] += acc

    return pl.pallas_call(
        body, name="post_bwd", grid=(t // tb,),
        in_specs=[_row(tb, D), _row(tb, D), _row(tb, D, 6), _fixed((1, DH))],
        out_specs=[_row(tb, D), _row(tb, D), _fixed((8, DH))],
        out_shape=[jax.ShapeDtypeStruct((t, D), f32), jax.ShapeDtypeStruct((t, D), bf16),
                   jax.ShapeDtypeStruct((8, DH), f32)],
        compiler_params=_params(1),
    )(dyb, o, p1, gn)


def _mix_fwd(ya, yb, p1):
    t = ya.shape[0]
    tb = _pick(t, (256, 128))

    def body(ya_ref, yb_ref, ga_ref, gb_ref, mix_ref):
        mix_ref[...] = (_sigmoid(ga_ref[...]) * ya_ref[...] + _sigmoid(gb_ref[...]) * yb_ref[...]).astype(bf16)

    return pl.pallas_call(
        body, name="mix_fwd", grid=(t // tb,), in_specs=[_row(tb, D), _row(tb, D), _row(tb, D, 7), _row(tb, D, 8)],
        out_specs=_row(tb, D), out_shape=jax.ShapeDtypeStruct((t, D), bf16), compiler_params=_params(1),
    )(ya, yb, p1, p1)


def _mix_bwd(dmix, ya, yb, p1):
    t = ya.shape[0]
    tb = _pick(t, (256, 128))

    def body(dm_ref, ya_ref, yb_ref, ga_ref, gb_ref, dya_ref, dyb_ref, dg_ref):
        dm = dm_ref[...]
        sa = _sigmoid(ga_ref[...])
        sb = _sigmoid(gb_ref[...])
        dya_ref[...] = (dm * sa).astype(bf16)
        dyb_ref[...] = (dm * sb).astype(bf16)
        dg_ref[:, :D] = (dm * ya_ref[...] * sa * (1.0 - sa)).astype(bf16)
        dg_ref[:, D:] = (dm * yb_ref[...] * sb * (1.0 - sb)).astype(bf16)

    return pl.pallas_call(
        body, name="mix_bwd", grid=(t // tb,),
        in_specs=[_row(tb, D), _row(tb, D), _row(tb, D), _row(tb, D, 7), _row(tb, D, 8)],
        out_specs=[_row(tb, D), _row(tb, D), _row(tb, 2 * D)],
        out_shape=[jax.ShapeDtypeStruct((t, D), bf16), jax.ShapeDtypeStruct((t, D), bf16),
                   jax.ShapeDtypeStruct((t, 2 * D), bf16)],
        compiler_params=_params(1),
    )(dmix, ya, yb, p1, p1)


def _ffn_fwd(up, wf):
    t = up.shape[0]
    tb = 128

    def body(up_ref, uph_ref, wf_ref, act_ref):
        c = _conv_down(up_ref[...], _first_zero(uph_ref), wf_ref, 3)
        gate = c[:, :DFF]
        act_ref[...] = (gate * _sigmoid(gate) * c[:, DFF:]).astype(bf16)

    return pl.pallas_call(
        body, name="ffn_fwd", grid=(t // tb,), in_specs=[_row(tb, 2 * DFF), _prev(tb, 2 * DFF), _fixed((8, 2 * DFF))],
        out_specs=_row(tb, DFF), out_shape=jax.ShapeDtypeStruct((t, DFF), bf16), compiler_params=_params(1),
    )(up, up, wf)


def _ffn_bwd1(dact, up, wf):
    t = up.shape[0]
    tb = 128

    def body(da_ref, up_ref, uph_ref, wf_ref, dc_ref, dw_ref):
        @pl.when(pl.program_id(0) == 0)
        def _():
            dw_ref[...] = jnp.zeros_like(dw_ref)

        upv = up_ref[...]
        uph = _first_zero(uph_ref)
        c = _conv_down(upv, uph, wf_ref, 3)
        gate = c[:, :DFF]
        val = c[:, DFF:]
        sg = _sigmoid(gate)
        da = da_ref[...]
        dgate = da * val * (sg * (1.0 + gate * (1.0 - sg)))
        dval = da * (gate * sg)
        dc_ref[:, :DFF] = dgate
        dc_ref[:, DFF:] = dval
        dc = jnp.concatenate([dgate, dval], axis=1)
        for j in range(3):
            dw_ref[j:j + 1, :] += jnp.sum(dc * _shift_down(upv, uph, 2 - j), axis=0, keepdims=True)

    return pl.pallas_call(
        body, name="ffn_bwd1", grid=(t // tb,),
        in_specs=[_row(tb, DFF), _row(tb, 2 * DFF), _prev(tb, 2 * DFF), _fixed((8, 2 * DFF))],
        out_specs=[_row(tb, 2 * DFF), _fixed((8, 2 * DFF))],
        out_shape=[jax.ShapeDtypeStruct((t, 2 * DFF), f32), jax.ShapeDtypeStruct((8, 2 * DFF), f32)],
        compiler_params=_params(1),
    )(dact, up, up, wf)


def _ffn_bwd2(dc, wf):
    t = dc.shape[0]
    tb = 128
    nb = t // tb

    def body(dc_ref, dch_ref, wf_ref, dup_ref):
        dup_ref[...] = _conv_up(dc_ref[...], _last_zero(dch_ref, nb), wf_ref, 3).astype(bf16)

    return pl.pallas_call(
        body, name="ffn_bwd2", grid=(nb,), in_specs=[_row(tb, 2 * DFF), _next(tb, 2 * DFF, t), _fixed((8, 2 * DFF))],
        out_specs=_row(tb, 2 * DFF), out_shape=jax.ShapeDtypeStruct((t, 2 * DFF), bf16), compiler_params=_params(1),
    )(dc, dc, wf)


def _final(x3, tgt, g):
    t = x3.shape[0]
    tb = _pick(t, (256, 128))

    def body(x_ref, t_ref, g_ref, loss_ref, dx_ref, dxb_ref, dg_ref):
        @pl.when(pl.program_id(0) == 0)
        def _():
            loss_ref[...] = jnp.zeros_like(loss_ref)
            dg_ref[...] = jnp.zeros_like(dg_ref)

        xv = x_ref[...]
        r = lax.rsqrt(jnp.mean(xv * xv, axis=-1, keepdims=True) + EPS)
        xh = xv * r
        gv = g_ref[...]
        e = xh * gv - t_ref[...]
        lrow = 0.5 * jnp.mean(e * e, axis=-1, keepdims=True)
        loss_ref[...] += jnp.sum(jnp.broadcast_to(lrow, (tb, 128)).reshape(tb // 8, 8, 128), axis=0)
        dy = e * (1.0 / D)
        dyg = dy * gv
        dx = r * (dyg - xh * jnp.mean(dyg * xh, axis=-1, keepdims=True))
        dx_ref[...] = dx
        dxb_ref[...] = dx.astype(bf16)
        dg_ref[...] += jnp.sum((dy * xh).reshape(tb // 8, 8, D), axis=0)

    return pl.pallas_call(
        body, name="final", grid=(t // tb,), in_specs=[_row(tb, D), _row(tb, D), _fixed((1, D))],
        out_specs=[_fixed((8, 128)), _row(tb, D), _row(tb, D), _fixed((8, D))],
        out_shape=[jax.ShapeDtypeStruct((8, 128), f32), jax.ShapeDtypeStruct((t, D), f32),
                   jax.ShapeDtypeStruct((t, D), bf16), jax.ShapeDtypeStruct((8, D), f32)],
        compiler_params=_params(1),
    )(x3, tgt, g)


def _pre_bwd1(p1, p2, dya_in, dqn, dkn, dvc, dgb, gbeta, wa, wg, alog, dtb):
    t = p1.shape[0]
    tb = 128

    def body(p0_ref, p0h_ref, pq_ref, pqh_ref, p2_ref, dya_ref, dqn_ref, dkn_ref, dvc_ref, dgb_ref, gb_ref,
             wa_ref, wg_ref, alog_ref, dtb_ref,
             dbg_ref, dca_ref, dc4_ref, dp2_ref, dwa_ref, dwg_ref, dal_ref, ddt_ref):
        @pl.when(pl.program_id(0) == 0)
        def _():
            dwa_ref[...] = jnp.zeros_like(dwa_ref)
            dwg_ref[...] = jnp.zeros_like(dwg_ref)
            dal_ref[...] = jnp.zeros_like(dal_ref)
            ddt_ref[...] = jnp.zeros_like(ddt_ref)

        p0 = p0_ref[...]
        h0 = _first_zero(p0h_ref)
        u = p0[:, D:2 * D] * p0[:, 2 * D:]
        uh = h0[:, D:2 * D] * h0[:, 2 * D:]
        dya = dya_ref[...]
        dbg_ref[...] = (dya * _conv_down(u, uh, wa_ref, 3)).astype(bf16)
        dca = dya * p0[:, :D]
        dca_ref[...] = dca
        for j in range(3):
            dwa_ref[j:j + 1, :] += jnp.sum(dca * _shift_down(u, uh, 2 - j), axis=0, keepdims=True)

        pq = pq_ref[...]
        pqh = _first_zero(pqh_ref)
        c4 = _conv_down(pq, pqh, wg_ref, 4)
        sg = _sigmoid(c4)
        s = c4 * sg
        dsilu = sg * (1.0 + c4 * (1.0 - sg))
        for h in range(H):
            for base, d_ref, scale in ((0, dqn_ref, DH ** -0.5), (D, dkn_ref, 1.0)):
                sl = slice(base + h * DH, base + (h + 1) * DH)
                a = s[:, sl]
                r = lax.rsqrt(jnp.sum(a * a, axis=-1, keepdims=True) + EPS)
                an = a * r
                dn = d_ref[:, h * DH:(h + 1) * DH] * scale
                dc4_ref[:, sl] = r * (dn - an * jnp.sum(dn * an, axis=-1, keepdims=True)) * dsilu[:, sl]
        dc4_ref[:, 2 * D:] = dvc_ref[...] * dsilu[:, 2 * D:]
        dc4 = dc4_ref[...]
        for j in range(4):
            dwg_ref[j:j + 1, :] += jnp.sum(dc4 * _shift_down(pq, pqh, 3 - j), axis=0, keepdims=True)

        ab = p2_ref[...]
        lane = lax.broadcasted_iota(jnp.int32, ab.shape, 1)
        dgbv = dgb_ref[...]
        gbv = gb_ref[...]
        da = dgbv * (-jnp.exp(alog_ref[...])) * _sigmoid(ab + dtb_ref[...])
        db = dgbv * gbv * (1.0 - gbv)
        dp2_ref[...] = jnp.where(lane < H, da, jnp.where(lane < 2 * H, db, 0.0)).astype(bf16)
        dal = jnp.where(lane < H, dgbv * gbv, 0.0)
        ddt = jnp.where(lane < H, da, 0.0)
        dal_ref[...] += jnp.sum(dal.reshape(tb // 8, 8, 128), axis=0)
        ddt_ref[...] += jnp.sum(ddt.reshape(tb // 8, 8, 128), axis=0)

    return pl.pallas_call(
        body, name="pre_bwd1", grid=(t // tb,),
        in_specs=[_row(tb, 3 * D, 0), _prev(tb, 3 * D, 0), _row(tb, 3 * D, 1), _prev(tb, 3 * D, 1), _row(tb, 128),
                  _row(tb, D), _row(tb, D), _row(tb, D), _row(tb, D), _row(tb, 128), _row(tb, 128),
                  _fixed((8, D)), _fixed((8, 3 * D)), _fixed((1, 128)), _fixed((1, 128))],
        out_specs=[_row(tb, D), _row(tb, D), _row(tb, 3 * D), _row(tb, 128),
                   _fixed((8, D)), _fixed((8, 3 * D)), _fixed((8, 128)), _fixed((8, 128))],
        out_shape=[jax.ShapeDtypeStruct((t, D), bf16), jax.ShapeDtypeStruct((t, D), f32),
                   jax.ShapeDtypeStruct((t, 3 * D), f32), jax.ShapeDtypeStruct((t, 128), bf16),
                   jax.ShapeDtypeStruct((8, D), f32), jax.ShapeDtypeStruct((8, 3 * D), f32),
                   jax.ShapeDtypeStruct((8, 128), f32), jax.ShapeDtypeStruct((8, 128), f32)],
        compiler_params=_params(1),
    )(p1, p1, p1, p1, p2, dya_in, dqn, dkn, dvc, dgb, gbeta, wa, wg, alog, dtb)


def _pre_bwd2(dca, dc4, p1, dbg, dz, dgates, wa, wg):
    t = p1.shape[0]
    tb = 128
    nb = t // tb

    def body(dca_ref, dcah_ref, dc4_ref, dc4h_ref, p0_ref, dbg_ref, dz_ref, dgt_ref, wa_ref, wg_ref, dp_ref):
        du = _conv_up(dca_ref[...], _last_zero(dcah_ref, nb), wa_ref, 3)
        dp_ref[:, :D] = dbg_ref[...]
        dp_ref[:, D:2 * D] = (du * p0_ref[:, 2 * D:]).astype(bf16)
        dp_ref[:, 2 * D:3 * D] = (du * p0_ref[:, D:2 * D]).astype(bf16)
        dp_ref[:, 3 * D:6 * D] = _conv_up(dc4_ref[...], _last_zero(dc4h_ref, nb), wg_ref, 4).astype(bf16)
        dp_ref[:, 6 * D:7 * D] = dz_ref[...]
        dp_ref[:, 7 * D:] = dgt_ref[...]

    return pl.pallas_call(
        body, name="pre_bwd2", grid=(nb,),
        in_specs=[_row(tb, D), _next(tb, D, t), _row(tb, 3 * D), _next(tb, 3 * D, t), _row(tb, 3 * D, 0),
                  _row(tb, D), _row(tb, D), _row(tb, 2 * D), _fixed((8, D)), _fixed((8, 3 * D))],
        out_specs=_row(tb, NW1), out_shape=jax.ShapeDtypeStruct((t, NW1), bf16), compiler_params=_params(1),
    )(dca, dca, dc4, dc4, p1, dbg, dz, dgates, wa, wg)


def _chunk_consts():
    r = lax.broadcasted_iota(jnp.int32, (CH, CH), 0)
    c = lax.broadcasted_iota(jnp.int32, (CH, CH), 1)
    return r, c, (r == c).astype(f32)


def _tri_inverse(lows, eye):
    xps = [-low for low in lows]
    invs = [eye + xp for xp in xps]
    for _ in range(5):
        xps = [_idot(xp, xp) for xp in xps]
        invs = [inv + _idot(inv, xp) for inv, xp in zip(invs, xps)]
    return invs


def _chunk_common(q, k, v, gcol, bcol, r, c, eye):
    grow = jnp.sum(eye * gcol, axis=0, keepdims=True)
    dec = jnp.exp(jnp.where(r >= c, gcol - grow, -jnp.inf))
    rcol = lax.broadcasted_iota(jnp.int32, (CH, 1), 0)
    glast = jnp.sum(jnp.where(rcol == CH - 1, gcol, 0.0), axis=0, keepdims=True)
    eg = jnp.exp(gcol)
    el = jnp.exp(glast - gcol)
    kb = k * bcol
    vb = v * bcol
    kk = _bdot_nt(kb, k)
    low = jnp.where(r > c, kk * dec, 0.0)
    qk = _bdot_nt(q, k)
    att = qk * dec
    return grow, dec, glast, eg, el, kb, vb, kk, low, qk, att, rcol


def _gdn_fwd(qn, kn, vc, gbeta):
    t = qn.shape[0]
    n_chunks = t // CH

    def body(q_ref, k_ref, v_ref, gb_ref, o_ref, s_ref, t_ref, state):
        @pl.when(pl.program_id(0) == 0)
        def _():
            state[...] = jnp.zeros_like(state)

        r, c, eye = _chunk_consts()
        gb = gb_ref[...]
        gall = _hdot((r >= c).astype(f32), gb)
        heads = range(H)
        qs = [q_ref[:, h * DH:(h + 1) * DH] for h in heads]
        ks = [k_ref[:, h * DH:(h + 1) * DH] for h in heads]
        vs = [v_ref[:, h * DH:(h + 1) * DH] for h in heads]
        sts = [state[h] for h in heads]
        cm = [_chunk_common(qs[h], ks[h], vs[h], gall[:, h:h + 1], gb[:, H + h:H + h + 1], r, c, eye) for h in heads]
        invs = _tri_inverse([m[8] for m in cm], eye)
        uws = [_bdot(invs[h], jnp.concatenate([cm[h][6], cm[h][5] * cm[h][3]], axis=1)) for h in heads]
        vns = [uws[h][:, :DH] - _bdot(uws[h][:, DH:], sts[h]) for h in heads]
        outs = [_bdot(qs[h] * cm[h][3], sts[h]) + _bdot(cm[h][10], vns[h]) for h in heads]
        news = [sts[h] * jnp.exp(cm[h][2]) + _bdot_tn(ks[h] * cm[h][4], vns[h]) for h in heads]
        for h in heads:
            s_ref[0, h] = sts[h].astype(bf16)
            t_ref[0, h] = invs[h]
            o_ref[:, h * DH:(h + 1) * DH] = outs[h]
            state[h] = news[h]

    return pl.pallas_call(
        body, name="gdn_fwd", grid=(n_chunks,),
        in_specs=[_row(CH, D), _row(CH, D), _row(CH, D), _row(CH, 128)],
        out_specs=[_row(CH, D), pl.BlockSpec((1, H, DH, DH), lambda i: (i, 0, 0, 0)),
                   pl.BlockSpec((1, H, CH, CH), lambda i: (i, 0, 0, 0))],
        out_shape=[jax.ShapeDtypeStruct((t, D), f32), jax.ShapeDtypeStruct((n_chunks, H, DH, DH), bf16),
                   jax.ShapeDtypeStruct((n_chunks, H, CH, CH), f32)],
        scratch_shapes=[pltpu.VMEM((H, DH, DH), f32)],
        compiler_params=_params(1),
    )(qn, kn, vc, gbeta)


def _gdn_bwd(qn, kn, vc, gbeta, do, s_all, t_all):
    t = qn.shape[0]
    n_chunks = t // CH

    def body(q_ref, k_ref, v_ref, gb_ref, do_ref, s_ref, t_ref, dq_ref, dk_ref, dv_ref, dgb_ref, dstate):
        @pl.when(pl.program_id(0) == 0)
        def _():
            dstate[...] = jnp.zeros_like(dstate)

        r, c, eye = _chunk_consts()
        tril = r >= c
        gb = gb_ref[...]
        gall = _hdot(tril.astype(f32), gb)
        lane = lax.broadcasted_iota(jnp.int32, (1, 128), 1)
        hs = range(H)

        def each(fn, *lists):
            return [fn(*args) for args in zip(*lists)]

        def rsum(a):
            return jnp.sum(a, axis=1, keepdims=True)

        q = [q_ref[:, h * DH:(h + 1) * DH] for h in hs]
        k = [k_ref[:, h * DH:(h + 1) * DH] for h in hs]
        v = [v_ref[:, h * DH:(h + 1) * DH] for h in hs]
        dout = [do_ref[:, h * DH:(h + 1) * DH] for h in hs]
        inv = [t_ref[0, h] for h in hs]
        st = [s_ref[0, h] for h in hs]
        ds = [dstate[h] for h in hs]
        bcol = [gb[:, H + h:H + h + 1] for h in hs]
        cm = [_chunk_common(q[h], k[h], v[h], gall[:, h:h + 1], bcol[h], r, c, eye) for h in hs]
        dec, glast, eg, el, kb, vb, low, att = ([m[i] for m in cm] for i in (1, 2, 3, 4, 5, 6, 8, 10))
        rcol = cm[0][11]
        elast = each(jnp.exp, glast)
        kbg = each(jnp.multiply, kb, eg)
        uw = each(lambda i, a, b: _bdot(i, jnp.concatenate([a, b], axis=1)), inv, vb, kbg)
        u = [a[:, :DH] for a in uw]
        w = [a[:, DH:] for a in uw]
        vn = each(lambda a, b, s: a - _bdot(b, s), u, w, st)
        qd = each(jnp.multiply, q, eg)
        kd = each(jnp.multiply, k, el)
        dvn = each(lambda a, d, kk, s: _bdot_tn(a, d) + _bdot(kk, s), att, dout, kd, ds)
        dqd = each(_bdot_nt, dout, st)
        datt = each(lambda d, x: jnp.where(tril, _bdot_nt(d, x), 0.0), dout, vn)
        dkd = each(_bdot_nt, vn, ds)
        dw = each(lambda a, s: -_bdot_nt(a, s), dvn, st)
        new_ds = each(lambda s, e, a, d, ww, dv_: s * e + _bdot_tn(a, d) - _bdot_tn(ww, dv_), ds, elast, qd, dout, w, dvn)
        dglast = each(lambda e, s, d: e * jnp.sum(rsum(s.astype(f32) * d), axis=0, keepdims=True), elast, st, ds)
        dr = each(lambda i, a, b: _bdot_tn(i, jnp.concatenate([a, b], axis=1)), inv, dvn, dw)
        dvb = [a[:, :DH] for a in dr]
        dkbg = [a[:, DH:] for a in dr]
        dlow = each(lambda a, b, x, y: -jnp.where(r > c, _bdot_nt(a, b) + _bdot_nt(x, y), 0.0), dvb, u, dkbg, w)
        dkk = each(jnp.multiply, dlow, dec)
        dqk = each(jnp.multiply, datt, dec)
        mm = each(lambda a, b, x, y: a * b + x * y, dlow, low, datt, att)
        dkb = each(lambda a, kk, b, e: _bdot(a, kk) + b * e, dkk, k, dkbg, eg)
        dk = each(lambda a, b, x, y, d, e, f, g: _bdot_tn(a, b) + _bdot_tn(x, y) + d * e + f * g,
                  dkk, kb, dqk, q, dkd, el, dkb, bcol)
        dq = each(lambda a, kk, d, e: _bdot(a, kk) + d * e, dqk, k, dqd, eg)
        dv = each(jnp.multiply, dvb, bcol)
        dbeta = each(lambda a, b, x, y: rsum(a * b) + rsum(x * y), dkb, k, dvb, v)
        deg = each(lambda a, b, x, y: rsum(a * b) + rsum(x * y), dkbg, kb, dqd, q)
        delc = each(lambda a, b, e: rsum(a * b) * e, dkd, k, el)
        dgc = each(lambda m, a, e, d: rsum(m) - rsum(eye * jnp.sum(m, axis=0, keepdims=True)) + a * e - d, mm, deg, eg, delc)
        dgc = each(lambda g, d, l: g + jnp.where(rcol == CH - 1, jnp.sum(d, axis=0, keepdims=True) + l, 0.0),
                   dgc, delc, dglast)
        dg_acc = jnp.zeros((CH, 128), f32)
        db_acc = jnp.zeros((CH, 128), f32)
        for h in hs:
            dq_ref[:, h * DH:(h + 1) * DH] = dq[h]
            dk_ref[:, h * DH:(h + 1) * DH] = dk[h]
            dv_ref[:, h * DH:(h + 1) * DH] = dv[h]
            dstate[h] = new_ds[h]
            dg_acc = dg_acc + dgc[h] * (lane == h).astype(f32)
            db_acc = db_acc + dbeta[h] * (lane == H + h).astype(f32)
        dgb_ref[...] = _hdot((r <= c).astype(f32), dg_acc) + db_acc

    rev = lambda i: (n_chunks - 1 - i, 0)
    rev4 = lambda i: (n_chunks - 1 - i, 0, 0, 0)
    return pl.pallas_call(
        body, name="gdn_bwd", grid=(n_chunks,),
        in_specs=[pl.BlockSpec((CH, D), rev), pl.BlockSpec((CH, D), rev), pl.BlockSpec((CH, D), rev),
                  pl.BlockSpec((CH, 128), rev), pl.BlockSpec((CH, D), rev),
                  pl.BlockSpec((1, H, DH, DH), rev4), pl.BlockSpec((1, H, CH, CH), rev4)],
        out_specs=[pl.BlockSpec((CH, D), rev), pl.BlockSpec((CH, D), rev), pl.BlockSpec((CH, D), rev),
                   pl.BlockSpec((CH, 128), rev)],
        out_shape=[jax.ShapeDtypeStruct((t, D), f32)] * 3 + [jax.ShapeDtypeStruct((t, 128), f32)],
        scratch_shapes=[pltpu.VMEM((H, DH, DH), f32)],
        compiler_params=_params(1),
    )(qn, kn, vc, gbeta, do, s_all, t_all)


def _pad_rows(w, rows=8):
    return jnp.pad(w, ((0, rows - w.shape[0]), (0, 0)))


_REST = ("w_up", "w_a_out", "w_b_out", "w_o", "w_down")


def _local_step(x, tgt, w, comm=None):
    w1, w2 = w["w1"], w["w2"]
    wa = _pad_rows(w["conv_a_w"])
    wg = _pad_rows(w["gdn_conv_w"])
    wf = _pad_rows(w["ffn_conv_w"])
    alog = jnp.pad(w["gdn_A_log"].reshape(1, H), ((0, 0), (0, 128 - H)))
    dtb = jnp.pad(w["gdn_dt_bias"].reshape(1, H), ((0, 0), (0, 128 - H)))
    g1 = w["norm_mix_g"].reshape(1, D)
    g2 = w["norm_ffn_g"].reshape(1, D)
    g3 = w["norm_final_g"].reshape(1, D)
    gn = w["gdn_norm_g"].reshape(1, DH)

    h1 = _rms_fwd(x, g1, name="rms1_fwd")
    if comm is None:
        p1 = _matmul(h1, w1, name="mm_in")
    else:
        p1, gathered = _matmul(h1, w1, name="mm_in", exchange=comm.gather_rest())
        w = {**w, **comm.finish_gather(gathered)}
    p2 = _matmul(h1, w2, name="mm_in_ab")
    ya_in, qn, kn, vc, gbeta = _pre_fwd(p1, p2, wa, wg, alog, dtb)
    o, s_all, t_all = _gdn_fwd(qn, kn, vc, gbeta)
    yb_in = _post_fwd(o, p1, gn)
    ya = _matmul(ya_in, w["w_a_out"], name="mm_a")
    yb = _matmul(yb_in, w["w_b_out"], name="mm_b")
    mix = _mix_fwd(ya, yb, p1)
    x2 = _matmul(mix, w["w_o"], name="mm_o", add=x)
    h2 = _rms_fwd(x2, g2, name="rms2_fwd")
    up = _matmul(h2, w["w_up"], name="mm_up", tm=512, tn=DFF)
    act = _ffn_fwd(up, wf)
    x3 = _matmul(act, w["w_down"], name="mm_down", add=x2, tm=512)
    loss_p, dx3, dx3b, dg3 = _final(x3, tgt, g3)

    grads = {"norm_final_g": dg3}
    dact = _matmul(dx3b, w["w_down"], nt=True, name="mm_down_dx", tm=512, tn=DFF)
    grads["w_down"] = _matmul_tn(act, dx3b, name="mm_down_dw", tm=DFF // 2)
    dc, dwf = _ffn_bwd1(dact, up, wf)
    grads["ffn_conv_w"] = dwf
    dup = _ffn_bwd2(dc, wf)
    dh2 = _matmul(dup, w["w_up"], nt=True, name="mm_up_dx", tk=DFF)
    grads["w_up"] = _matmul_tn(h2, dup, name="mm_up_dw", tn=512)
    dx2, dx2b, dg2 = _rms_bwd(dh2, x2, g2, dx3, name="rms2_bwd")
    grads["norm_ffn_g"] = dg2
    dmix = _matmul(dx2b, w["w_o"], nt=True, name="mm_o_dx")
    grads["w_o"] = _matmul_tn(mix, dx2b, name="mm_o_dw")
    dya, dyb, dgates = _mix_bwd(dmix, ya, yb, p1)
    dya_in = _matmul(dya, w["w_a_out"], nt=True, name="mm_a_dx")
    grads["w_a_out"] = _matmul_tn(ya_in, dya, name="mm_a_dw")
    dyb_in = _matmul(dyb, w["w_b_out"], nt=True, name="mm_b_dx")
    grads["w_b_out"] = _matmul_tn(yb_in, dyb, name="mm_b_dw")
    do, dz, dgn = _post_bwd(dyb_in, o, p1, gn)
    grads["gdn_norm_g"] = dgn
    dqn, dkn, dvc, dgb = _gdn_bwd(qn, kn, vc, gbeta, do, s_all, t_all)
    dbg, dca, dc4, dp2, dwa, dwg, dal, ddt = _pre_bwd1(p1, p2, dya_in, dqn, dkn, dvc, dgb, gbeta, wa, wg, alog, dtb)
    grads["conv_a_w"] = dwa
    grads["gdn_conv_w"] = dwg
    grads["gdn_A_log"] = dal
    grads["gdn_dt_bias"] = ddt
    dp1 = _pre_bwd2(dca, dc4, p1, dbg, dz, dgates, wa, wg)
    grads["w2"] = _matmul_tn(h1, dp2, name="mm_in_ab_dw")
    if comm is None:
        grads["w1"] = _matmul_tn(h1, dp1, name="mm_in_dw")
        dh1 = _matmul(dp1, w1, nt=True, name="mm_in_dx", tm=512, tk=NW1 // 2)
    else:
        exchange, sums = comm.reduce(_REST, grads)
        grads["w1"], recv = _matmul_tn(h1, dp1, name="mm_in_dw", exchange=exchange)
        comm.finish_reduce(_REST, sums, recv)
        exchange, sums = comm.reduce(("w_in",), grads)
        dh1, recv = _matmul(dp1, w1, nt=True, name="mm_in_dx", tm=512, tk=NW1 // 2, exchange=exchange)
        comm.finish_reduce(("w_in",), sums, recv)
    dh1 = _matmul(dp2, w2, nt=True, name="mm_in_ab_dx", add=dh1)
    dx, _, dg1 = _rms_bwd(dh1, x, g1, dx2, name="rms1_bwd")
    grads["norm_mix_g"] = dg1
    return loss_p, dx, grads


_ANY = pl.BlockSpec(memory_space=pl.ANY)


def _all_gather_many(shards, *, name):
    n = len(shards)

    def body(*refs):
        x_refs, out_refs = refs[:n], refs[n:2 * n]
        send_sems, recv_sems, local_sems = refs[2 * n:]
        x, y, c = lax.axis_index("x"), lax.axis_index("y"), lax.axis_index("c")
        me, sibling = (x, y, c), (x, y, 1 - c)
        chips = [(1 - x, y), (x, 1 - y), (1 - x, 1 - y)]

        def copy(a, k, blk, to, from_input=False):
            dst = out_refs[a].at[4 * blk[0] + 2 * blk[1] + blk[2]]
            return pltpu.make_async_remote_copy(
                src_ref=x_refs[a] if from_input else dst, dst_ref=dst,
                send_sem=send_sems.at[a, k], recv_sem=recv_sems.at[a, k], device_id=to, device_id_type=MESH)

        mine = [pltpu.make_async_copy(x_refs[a], out_refs[a].at[4 * x + 2 * y + c], local_sems.at[a]) for a in range(n)]
        for cp in mine:
            cp.start()
        first = []
        for a in range(n):
            first.append(copy(a, 0, me, sibling, from_input=True))
            first += [copy(a, 1 + j, me, (*chip, c), from_input=True) for j, chip in enumerate(chips)]
        for cp in first:
            cp.start()
        passed = []
        for j, chip in enumerate(chips):
            for a in range(n):
                copy(a, 1 + j, (*chip, c), me).wait_recv()
                passed.append(copy(a, 4 + j, (*chip, c), sibling))
                passed[-1].start()
        for a in range(n):
            copy(a, 0, sibling, me).wait_recv()
            for j, chip in enumerate(chips):
                copy(a, 4 + j, (*chip, 1 - c), me).wait_recv()
        for cp in first + passed:
            cp.wait_send()
        for cp in mine:
            cp.wait()

    return pl.pallas_call(
        body, name=name, out_shape=[jax.ShapeDtypeStruct((N_DEV, *s.shape), s.dtype) for s in shards],
        in_specs=[_ANY] * n, out_specs=[_ANY] * n,
        scratch_shapes=[pltpu.SemaphoreType.DMA((n, 7)), pltpu.SemaphoreType.DMA((n, 7)), pltpu.SemaphoreType.DMA((n,))],
    )(*shards)


def _remote(src, dst, send_sem, recv_sem, to):
    return pltpu.make_async_remote_copy(src_ref=src, dst_ref=dst, send_sem=send_sem, recv_sem=recv_sem,
                                        device_id=to, device_id_type=MESH)


def _gather_direct_exchange(shards):
    n = len(shards)

    def copies(x_refs, out_refs, sems):
        send_sems, recv_sems, local_sems = sems
        x, y, c = lax.axis_index("x"), lax.axis_index("y"), lax.axis_index("c")
        targets = [(x, y, 1 - c), (1 - x, y, c), (x, 1 - y, c), (1 - x, 1 - y, c)]
        local, sends, recvs = [], [], []
        for a in range(n):
            mine = out_refs[a].at[4 * x + 2 * y + c]
            local.append(pltpu.make_async_copy(x_refs[a], mine, local_sems.at[a]))
            for k, to in enumerate(targets):
                theirs = out_refs[a].at[4 * to[0] + 2 * to[1] + to[2]]
                sends.append(_remote(x_refs[a], mine, send_sems.at[a, k], recv_sems.at[a, k], to))
                recvs.append(_remote(theirs, theirs, send_sems.at[a, k], recv_sems.at[a, k], to))
        return local, sends, recvs

    def start(x_refs, out_refs, sems):
        local, sends, _ = copies(x_refs, out_refs, sems)
        for cp in local + sends:
            cp.start()

    def wait(x_refs, out_refs, sems):
        local, sends, recvs = copies(x_refs, out_refs, sems)
        for cp in recvs:
            cp.wait_recv()
        for cp in sends:
            cp.wait_send()
        for cp in local:
            cp.wait()

    shapes = [jax.ShapeDtypeStruct((N_DEV, *s.shape), s.dtype) for s in shards]
    sems = [pltpu.SemaphoreType.DMA((n, 4)), pltpu.SemaphoreType.DMA((n, 4)), pltpu.SemaphoreType.DMA((n,))]
    return shards, shapes, sems, start, wait


def _gather_forward(gathered):
    n = len(gathered)

    def body(*refs):
        out_refs = refs[n:2 * n]
        send_sems, recv_sems = refs[2 * n:]
        x, y, c = lax.axis_index("x"), lax.axis_index("y"), lax.axis_index("c")
        sibling = (x, y, 1 - c)
        sends, recvs = [], []
        for a in range(n):
            for j, (px, py) in enumerate([(1 - x, y), (x, 1 - y), (1 - x, 1 - y)]):
                mine = out_refs[a].at[4 * px + 2 * py + c]
                theirs = out_refs[a].at[4 * px + 2 * py + 1 - c]
                sends.append(_remote(mine, mine, send_sems.at[a, j], recv_sems.at[a, j], sibling))
                recvs.append(_remote(theirs, theirs, send_sems.at[a, j], recv_sems.at[a, j], sibling))
        for cp in sends:
            cp.start()
        for cp in recvs:
            cp.wait_recv()
        for cp in sends:
            cp.wait_send()

    return pl.pallas_call(
        body, name="ag_forward", out_shape=[jax.ShapeDtypeStruct(g.shape, g.dtype) for g in gathered],
        in_specs=[_ANY] * n, out_specs=[_ANY] * n, input_output_aliases={a: a for a in range(n)},
        scratch_shapes=[pltpu.SemaphoreType.DMA((n, 3)), pltpu.SemaphoreType.DMA((n, 3))],
    )(*gathered)


def _chips_exchange(hsums):
    n = len(hsums)

    def copies(h_refs, out_refs, sems):
        send_sems, recv_sems = sems
        x, y, c = lax.axis_index("x"), lax.axis_index("y"), lax.axis_index("c")
        chips = [(1 - x, y), (x, 1 - y), (1 - x, 1 - y)]
        return [_remote(h_refs[a].at[2 * px + py], out_refs[a].at[k], send_sems.at[a, k], recv_sems.at[a, k], (px, py, c))
                for a in range(n) for k, (px, py) in enumerate(chips)]

    def start(h_refs, out_refs, sems):
        for cp in copies(h_refs, out_refs, sems):
            cp.start()

    def wait(h_refs, out_refs, sems):
        for cp in copies(h_refs, out_refs, sems):
            cp.wait()

    shapes = [jax.ShapeDtypeStruct((3, *h.shape[1:]), h.dtype) for h in hsums]
    sems = [pltpu.SemaphoreType.DMA((n, 3)), pltpu.SemaphoreType.DMA((n, 3))]
    return hsums, shapes, sems, start, wait


def _exchange_sibling_many(halves, *, name):
    n = len(halves)

    def body(*refs):
        p_refs, out_refs = refs[:n], refs[n:2 * n]
        send_sems, recv_sems = refs[2 * n:]
        x, y, c = lax.axis_index("x"), lax.axis_index("y"), lax.axis_index("c")
        cps = [pltpu.make_async_remote_copy(src_ref=p_refs[a], dst_ref=out_refs[a], send_sem=send_sems.at[a],
                                            recv_sem=recv_sems.at[a], device_id=(x, y, 1 - c), device_id_type=MESH)
               for a in range(n)]
        for cp in cps:
            cp.start()
        for cp in cps:
            cp.wait()

    return pl.pallas_call(
        body, name=name, out_shape=[jax.ShapeDtypeStruct(h.shape, h.dtype) for h in halves],
        in_specs=[_ANY] * n, out_specs=[_ANY] * n,
        scratch_shapes=[pltpu.SemaphoreType.DMA((n,)), pltpu.SemaphoreType.DMA((n,))],
    )(*halves)


_IN_RANGES = ((0, 7 * D, 0, 0), (7 * D, 7 * D + 16, 1, 0), (7 * D + 16, 9 * D + 16, 0, 7 * D))
_UP_RANGES = ((0, 2 * DFF, 0, 0),)


def _col_pieces(width, ranges):
    pieces = []
    for d in range(N_DEV):
        lo, hi = d * width, (d + 1) * width
        for glo, ghi, mat, mlo in ranges:
            a, b = max(lo, glo), min(hi, ghi)
            if a < b:
                pieces.append((d, a - lo, b - lo, mat, mlo + a - glo))
    return pieces


def _cols_to_matrices(g, ranges, out_widths, *, name):
    _, rows, width = g.shape
    tb = 128
    pieces = _col_pieces(width, ranges)
    covered = [sum(p[2] - p[1] for p in pieces if p[3] == m) for m in range(len(out_widths))]

    def body(g_ref, *o_refs):
        for m, o_ref in enumerate(o_refs):
            if covered[m] < out_widths[m]:
                o_ref[...] = jnp.zeros_like(o_ref)
        for d, b0, b1, m, m0 in pieces:
            o_refs[m][:, m0:m0 + b1 - b0] = g_ref[d, :, b0:b1]

    return pl.pallas_call(
        body, name=name, grid=(rows // tb,), in_specs=[pl.BlockSpec((N_DEV, tb, width), lambda i: (0, i, 0))],
        out_specs=[pl.BlockSpec((tb, wo), lambda i: (i, 0)) for wo in out_widths],
        out_shape=[jax.ShapeDtypeStruct((rows, wo), g.dtype) for wo in out_widths], compiler_params=_params(1),
    )(g)


def _matrices_to_cols(mats, ranges, width, *, name):
    rows = mats[0].shape[0]
    tb = 128
    pieces = _col_pieces(width, ranges)

    def body(*refs):
        m_refs, g_ref = refs[:-1], refs[-1]
        for d, b0, b1, m, m0 in pieces:
            g_ref[d, :, b0:b1] = m_refs[m][:, m0:m0 + b1 - b0]

    return pl.pallas_call(
        body, name=name, grid=(rows // tb,),
        in_specs=[pl.BlockSpec((tb, mt.shape[1]), lambda i: (i, 0)) for mt in mats],
        out_specs=pl.BlockSpec((N_DEV, tb, width), lambda i: (0, i, 0)),
        out_shape=jax.ShapeDtypeStruct((N_DEV, rows, width), mats[0].dtype), compiler_params=_params(1),
    )(*mats)


def _row_block(rows):
    return 128 if rows % 128 == 0 else rows


def _half_bf16(g4, c_other, *, name):
    _, _, rows, width = g4.shape
    tb = _row_block(rows)

    def body(c_ref, p_ref, o_ref):
        o_ref[0] = p_ref[0, 0].astype(bf16)

    grid_spec = pltpu.PrefetchScalarGridSpec(
        num_scalar_prefetch=1, grid=(4, rows // tb),
        in_specs=[pl.BlockSpec((1, 1, tb, width), lambda j, i, c_ref: (j, c_ref[0], i, 0))],
        out_specs=pl.BlockSpec((1, tb, width), lambda j, i, c_ref: (j, i, 0)))
    return pl.pallas_call(
        body, name=name, grid_spec=grid_spec, out_shape=jax.ShapeDtypeStruct((4, rows, width), bf16),
        compiler_params=_params(2),
    )(c_other, g4)


def _pair_sum(g4, recv, c_me, *, name):
    _, _, rows, width = g4.shape
    tb = _row_block(rows)

    def body(c_ref, p_ref, r_ref, o_ref, ob_ref):
        s = p_ref[0, 0] + r_ref[0].astype(f32)
        o_ref[0] = s
        ob_ref[0] = s.astype(bf16)

    blk = pl.BlockSpec((1, tb, width), lambda j, i, c_ref: (j, i, 0))
    grid_spec = pltpu.PrefetchScalarGridSpec(
        num_scalar_prefetch=1, grid=(4, rows // tb),
        in_specs=[pl.BlockSpec((1, 1, tb, width), lambda j, i, c_ref: (j, c_ref[0], i, 0)), blk],
        out_specs=[blk, blk])
    return pl.pallas_call(
        body, name=name, grid_spec=grid_spec,
        out_shape=[jax.ShapeDtypeStruct((4, rows, width), f32), jax.ShapeDtypeStruct((4, rows, width), bf16)],
        compiler_params=_params(2),
    )(c_me, g4, recv)


def _adam_shard(hsum, recv, chip, w, m, v, *, name):
    _, rows, width = w.shape
    tb = _row_block(rows)

    def body(j_ref, h_ref, r_ref, w_ref, m_ref, v_ref, g_out, d_out, m_out, v_out):
        g = ((h_ref[0] + r_ref[0].astype(f32)) + r_ref[1].astype(f32)) + r_ref[2].astype(f32)
        delta, mn, vn = _adam_math(w_ref[0], g, m_ref[0], v_ref[0])
        g_out[0] = g
        d_out[0] = delta
        m_out[0] = mn
        v_out[0] = vn

    blk = pl.BlockSpec((1, tb, width), lambda i, j_ref: (0, i, 0))
    grid_spec = pltpu.PrefetchScalarGridSpec(
        num_scalar_prefetch=1, grid=(rows // tb,),
        in_specs=[pl.BlockSpec((1, tb, width), lambda i, j_ref: (j_ref[0], i, 0)),
                  pl.BlockSpec((3, tb, width), lambda i, j_ref: (0, i, 0)), blk, blk, blk],
        out_specs=[blk, blk, blk, blk])
    return pl.pallas_call(
        body, name=name, grid_spec=grid_spec, out_shape=[jax.ShapeDtypeStruct(w.shape, f32)] * 4,
        compiler_params=_params(1),
    )(chip, hsum, recv, w, m, v)


R_SMALL = 16 + 16 * N_DEV
_SMALL_LANES = {"gdn_norm_g": (0, DH), "gdn_A_log": (DH, DH + H), "gdn_dt_bias": (2 * DH, 2 * DH + H)}
_LOSS_LANE = 3 * DH


def _pack_small(dg1, dg2, dg3, dgn, dal, ddt, loss_p, dwa, dwg, dwf):
    def body(dg1_ref, dg2_ref, dg3_ref, dgn_ref, dal_ref, ddt_ref, loss_ref, dwa_ref, dwg_ref, dwf_ref, o_ref):
        def total(ref):
            return jnp.sum(ref[...], axis=0, keepdims=True)

        o_ref[...] = jnp.zeros_like(o_ref)
        o_ref[0:1, :] = total(dg1_ref)
        o_ref[1:2, :] = total(dg2_ref)
        o_ref[2:3, :] = total(dg3_ref)
        o_ref[3:4, 0:DH] = total(dgn_ref)
        o_ref[3:4, DH:2 * DH] = total(dal_ref)
        o_ref[3:4, 2 * DH:3 * DH] = total(ddt_ref)
        o_ref[3:4, 3 * DH:4 * DH] = total(loss_ref)
        for d in range(N_DEV):
            base = 16 + 16 * d
            o_ref[base:base + 3, 0:128] = dwa_ref[0:3, 128 * d:128 * (d + 1)]
            o_ref[base:base + 4, 128:512] = dwg_ref[0:4, 384 * d:384 * (d + 1)]
            o_ref[base + 8:base + 11, 0:704] = dwf_ref[0:3, 704 * d:704 * (d + 1)]

    return pl.pallas_call(body, name="pack_small", out_shape=jax.ShapeDtypeStruct((R_SMALL, D), f32))(
        dg1, dg2, dg3, dgn, dal, ddt, loss_p, dwa, dwg, dwf)


_SMALL = ("norm_mix_g", "norm_ffn_g", "norm_final_g", "gdn_norm_g", "gdn_A_log", "gdn_dt_bias",
          "conv_a_w", "gdn_conv_w", "ffn_conv_w")


def _adam_small(gath, me, w, m, v):
    arrays = [t[n] for n in _SMALL for t in (w, m, v)]

    def body(me_ref, ga_ref, gb_ref, *refs):
        ins, outs = refs[:len(arrays)], refs[len(arrays):]
        ga, gb = ga_ref[0], gb_ref[0]
        for s in range(1, N_DEV):
            ga = ga + ga_ref[s]
            gb = gb + gb_ref[s]
        grads = {"norm_mix_g": ga[0:1, :], "norm_ffn_g": ga[1:2, :], "norm_final_g": ga[2:3, :],
                 "conv_a_w": gb[0:3, 0:128], "gdn_conv_w": gb[0:4, 128:512], "ffn_conv_w": gb[8:11, 0:704]}
        for n, (lo, hi) in _SMALL_LANES.items():
            grads[n] = ga[3:4, lo:hi]
        for i, n in enumerate(_SMALL):
            three_d = len(w[n].shape) == 3
            wv, mv, vv = (r[0] if three_d else r[...] for r in ins[3 * i:3 * i + 3])
            delta, mn, vn = _adam_math(wv, grads[n], mv, vv)
            for o_ref, val in zip(outs[4 * i:4 * i + 4], (grads[n], delta, mn, vn)):
                if three_d:
                    o_ref[0] = val
                else:
                    o_ref[...] = val
        outs[-1][...] = ga[3:4, _LOSS_LANE:_LOSS_LANE + 1]

    def whole(shape):
        return pl.BlockSpec(shape, lambda i, me_ref: (0,) * len(shape))

    grid_spec = pltpu.PrefetchScalarGridSpec(
        num_scalar_prefetch=1, grid=(1,),
        in_specs=[pl.BlockSpec((N_DEV, 16, D), lambda i, me_ref: (0, 0, 0)),
                  pl.BlockSpec((N_DEV, 16, D), lambda i, me_ref: (0, 1 + me_ref[0], 0))] + [whole(a.shape) for a in arrays],
        out_specs=[whole(w[n].shape) for n in _SMALL for _ in range(4)] + [whole((1, 1))])
    res = pl.pallas_call(
        body, name="adam_small", grid_spec=grid_spec,
        out_shape=[jax.ShapeDtypeStruct(w[n].shape, f32) for n in _SMALL for _ in range(4)]
        + [jax.ShapeDtypeStruct((1, 1), f32)],
        compiler_params=_params(1),
    )(me, gath, gath, *arrays)
    return {n: tuple(res[4 * i:4 * i + 4]) for i, n in enumerate(_SMALL)}, res[-1]


def _adam_math(w, g, m, v):
    m = ADAM_B1 * m + (1.0 - ADAM_B1) * g
    v = ADAM_B2 * v + (1.0 - ADAM_B2) * jnp.square(g)
    m_hat = m / (1.0 - ADAM_B1 ** ADAM_STEP)
    v_hat = v / (1.0 - ADAM_B2 ** ADAM_STEP)
    delta = -ADAM_LR * (m_hat / (jnp.sqrt(v_hat) + ADAM_EPS) + ADAM_WD * w)
    return delta, m, v


_WEIGHTS = ("norm_mix_g", "w_in", "conv_a_w", "gdn_conv_w", "gdn_A_log", "gdn_dt_bias", "gdn_norm_g", "w_a_out",
            "w_b_out", "w_o", "norm_ffn_g", "w_up", "ffn_conv_w", "w_down", "norm_final_g")
_BIG = ("w_in",) + _REST
_CONVS = ("conv_a_w", "gdn_conv_w", "ffn_conv_w")


class _StepExchanges:
    def __init__(self, wts, mom, var, c_me, chip):
        self.wts, self.mom, self.var, self.c_me, self.chip = wts, mom, var, c_me, chip
        self.results = {}

    def gather_rest(self):
        return _gather_direct_exchange([self.wts[n][0].astype(bf16) for n in _REST])

    def finish_gather(self, gathered):
        g_up, g_a, g_b, g_o, g_down = _gather_forward(gathered)
        (w_up,) = _cols_to_matrices(g_up, _UP_RANGES, (2 * DFF,), name="relay_w_up")
        return {"w_up": w_up, "w_a_out": g_a.reshape(D, D), "w_b_out": g_b.reshape(D, D), "w_o": g_o.reshape(D, D),
                "w_down": g_down.reshape(DFF, D)}

    def reduce(self, names, grads):
        g4 = []
        for n in names:
            if n == "w_in":
                g = _matrices_to_cols([grads["w1"], grads["w2"]], _IN_RANGES, R_IN, name="relay_dw_in")
            elif n == "w_up":
                g = _matrices_to_cols([grads[n]], _UP_RANGES, R_UP, name="relay_dw_up")
            else:
                g = grads[n]
            g4.append(g.reshape(4, 2, *self.wts[n].shape[1:]))
        halves = [_half_bf16(g, 1 - self.c_me, name="rs_half_" + n) for n, g in zip(names, g4)]
        recv = _exchange_sibling_many(halves, name="rs_sibling_" + names[0])
        sums = [_pair_sum(g, r, self.c_me, name="rs_sum_" + n) for n, g, r in zip(names, g4, recv)]
        return _chips_exchange([s[1] for s in sums]), [s[0] for s in sums]

    def finish_reduce(self, names, sums, recv):
        for n, s, r in zip(names, sums, recv):
            self.results[n] = _adam_shard(s, r, self.chip, self.wts[n], self.mom[n], self.var[n], name="adam_" + n)


def kernel(x, norm_mix_g, w_in, conv_a_w, gdn_conv_w, gdn_A_log, gdn_dt_bias, gdn_norm_g, w_a_out, w_b_out, w_o, norm_ffn_g, w_up, ffn_conv_w, w_down, norm_final_g, loss_target, m_norm_mix_g, m_w_in, m_conv_a_w, m_gdn_conv_w, m_gdn_A_log, m_gdn_dt_bias, m_gdn_norm_g, m_w_a_out, m_w_b_out, m_w_o, m_norm_ffn_g, m_w_up, m_ffn_conv_w, m_w_down, m_norm_final_g, v_norm_mix_g, v_w_in, v_conv_a_w, v_gdn_conv_w, v_gdn_A_log, v_gdn_dt_bias, v_gdn_norm_g, v_w_a_out, v_w_b_out, v_w_o, v_norm_ffn_g, v_w_up, v_ffn_conv_w, v_w_down, v_norm_final_g):
    wts = dict(zip(_WEIGHTS, (norm_mix_g, w_in, conv_a_w, gdn_conv_w, gdn_A_log, gdn_dt_bias, gdn_norm_g, w_a_out,
                              w_b_out, w_o, norm_ffn_g, w_up, ffn_conv_w, w_down, norm_final_g)))
    mom = dict(zip(_WEIGHTS, (m_norm_mix_g, m_w_in, m_conv_a_w, m_gdn_conv_w, m_gdn_A_log, m_gdn_dt_bias,
                              m_gdn_norm_g, m_w_a_out, m_w_b_out, m_w_o, m_norm_ffn_g, m_w_up, m_ffn_conv_w,
                              m_w_down, m_norm_final_g)))
    var = dict(zip(_WEIGHTS, (v_norm_mix_g, v_w_in, v_conv_a_w, v_gdn_conv_w, v_gdn_A_log, v_gdn_dt_bias,
                              v_gdn_norm_g, v_w_a_out, v_w_b_out, v_w_o, v_norm_ffn_g, v_w_up, v_ffn_conv_w,
                              v_w_down, v_norm_final_g)))
    cx, cy, cc = lax.axis_index("x"), lax.axis_index("y"), lax.axis_index("c")
    c_me = jnp.reshape(cc, (1,)).astype(jnp.int32)
    chip = jnp.reshape(2 * cx + cy, (1,)).astype(jnp.int32)
    me = jnp.reshape(4 * cx + 2 * cy + cc, (1,)).astype(jnp.int32)

    g_in, gc_a, gc_g, gc_f = _all_gather_many([wts["w_in"][0].astype(bf16)] + [wts[n][0] for n in _CONVS], name="ag_w_in")
    w1, w2 = _cols_to_matrices(g_in, _IN_RANGES, (NW1, 128), name="relay_w_in")
    first = {"w1": w1, "w2": w2, "conv_a_w": gc_a.transpose(1, 0, 2).reshape(3, D),
             "gdn_conv_w": gc_g.transpose(1, 0, 2).reshape(4, 3 * D), "ffn_conv_w": gc_f.transpose(1, 0, 2).reshape(3, 2 * DFF)}
    for n in ("norm_mix_g", "norm_ffn_g", "norm_final_g", "gdn_norm_g", "gdn_A_log", "gdn_dt_bias"):
        first[n] = wts[n]
    comm = _StepExchanges(wts, mom, var, c_me, chip)
    loss_p, dx, grads = _local_step(x[0], loss_target[0], first, comm)
    res = comm.results

    small = _pack_small(grads["norm_mix_g"], grads["norm_ffn_g"], grads["norm_final_g"], grads["gdn_norm_g"],
                        grads["gdn_A_log"], grads["gdn_dt_bias"], loss_p, grads["conv_a_w"], grads["gdn_conv_w"],
                        grads["ffn_conv_w"])
    (small_all,) = _all_gather_many([small], name="ag_small")

    def raw(t):
        return {n: t[n].reshape(1, D) if n == "norm_final_g" else t[n] for n in _SMALL}

    res_small, loss = _adam_small(small_all, me, raw(wts), raw(mom), raw(var))
    for n in _SMALL:
        res[n] = tuple(a.reshape(wts[n].shape) for a in res_small[n])
    outs = [[res[n][i] for n in _WEIGHTS] for i in range(4)]
    return (loss.reshape(()), dx[None], *outs[0], *outs[1], *outs[2], *outs[3])
```

```python
import jax
import jax.numpy as jnp
from jax import lax
from jax.experimental import pallas as pl
from jax.experimental.pallas import tpu as pltpu

f32 = jnp.float32
bf16 = jnp.bfloat16

D = 1024
H = 8
DH = 128
CH = 64
GDN_STEP = 2
DFF = 2816
NW1 = 9216
EPS = 1e-6
N_DEV = 8

ADAM_LR = 0.001
ADAM_B1 = 0.9
ADAM_B2 = 0.999
ADAM_EPS = 1e-08
ADAM_WD = 0.01
ADAM_STEP = 10

VMEM_LIMIT_BYTES = 48 * 1024 * 1024

R_IN, R_UP = 1154, 704

_HI = lax.Precision.HIGHEST
MESH = pl.DeviceIdType.MESH


def _params(n_grid):
    return pltpu.CompilerParams(dimension_semantics=("arbitrary",) * n_grid, vmem_limit_bytes=VMEM_LIMIT_BYTES)


def _bdot(a, b):
    return jnp.dot(a.astype(bf16), b.astype(bf16), preferred_element_type=f32)


def _bdot_nt(a, b):
    return lax.dot_general(a.astype(bf16), b.astype(bf16), (((1,), (1,)), ((), ())), preferred_element_type=f32)


def _bdot_tn(a, b):
    return lax.dot_general(a.astype(bf16), b.astype(bf16), (((0,), (0,)), ((), ())), preferred_element_type=f32)


def _hdot(a, b):
    return jnp.dot(a, b, preferred_element_type=f32, precision=_HI)


def _idot(a, b):
    return jnp.dot(a, b, preferred_element_type=f32, precision=lax.Precision.HIGH)


def _sigmoid(x):
    return 1.0 / (1.0 + jnp.exp(-x))


def _softplus(x):
    return jnp.maximum(x, 0.0) + jnp.log(1.0 + jnp.exp(-jnp.abs(x)))


def _shift_down(x, halo, j):
    if j == 0:
        return x
    xr = pltpu.roll(x, j, 0)
    hr = pltpu.roll(halo, j, 0)
    r8 = lax.broadcasted_iota(jnp.int32, hr.shape, 0)
    top = jnp.where(r8 < j, hr, xr[:8])
    return jnp.concatenate([top, xr[8:]], axis=0)


def _shift_up(x, halo, j):
    if j == 0:
        return x
    n = x.shape[0]
    xr = pltpu.roll(x, n - j, 0)
    hr = pltpu.roll(halo, 8 - j, 0)
    r8 = lax.broadcasted_iota(jnp.int32, hr.shape, 0)
    bot = jnp.where(r8 >= 8 - j, hr, xr[n - 8:])
    return jnp.concatenate([xr[:n - 8], bot], axis=0)


def _taps_down(x, halo, k):
    return [_shift_down(x, halo, k - 1 - j) for j in range(k)]


def _conv_taps(taps, w_ref):
    out = w_ref[0:1, :] * taps[0]
    for j in range(1, len(taps)):
        out = out + w_ref[j:j + 1, :] * taps[j]
    return out


def _conv_down(x, halo, w_ref, k):
    return _conv_taps(_taps_down(x, halo, k), w_ref)


def _conv_weight_grad(dw_ref, dy, taps):
    for j, tap in enumerate(taps):
        dw_ref[j:j + 1, :] += jnp.sum(dy * tap, axis=0, keepdims=True)


def _conv_up(dy, halo, w_ref, k):
    out = w_ref[k - 1:k, :] * dy
    for j in range(k - 1):
        out = out + w_ref[j:j + 1, :] * _shift_up(dy, halo, k - 1 - j)
    return out


def _row(tb, w, col=0):
    return pl.BlockSpec((tb, w), lambda i: (i, col))


def _prev(tb, w, col=0):
    return pl.BlockSpec((8, w), lambda i: (jnp.maximum(i * (tb // 8) - 1, 0), col))


def _next(tb, w, n_rows, col=0):
    last = n_rows // 8 - 1
    return pl.BlockSpec((8, w), lambda i: (jnp.minimum((i + 1) * (tb // 8), last), col))


def _fixed(shape):
    return pl.BlockSpec(shape, lambda i: (0,) * len(shape))


def _first_zero(halo_ref):
    return jnp.where(pl.program_id(0) == 0, 0.0, halo_ref[...])


def _last_zero(halo_ref, n_blocks):
    return jnp.where(pl.program_id(0) == n_blocks - 1, 0.0, halo_ref[...])


def _pick(n, prefs):
    for p in prefs:
        if n % p == 0:
            return p
    return n


def _matmul(a, b, *, name, nt=False, add=None, tm=1024, tn=1024, tk=None, exchange=None):
    m, kd = a.shape
    n = b.shape[0] if nt else b.shape[1]
    tm = _pick(m, (tm, 512, 256))
    tn = _pick(n, (tn, 1024, 512, 128))
    tk = kd if tk is None else tk
    nk = kd // tk
    dims = (((1,), (1,)), ((), ())) if nt else (((1,), (0,)), ((), ()))

    def body(a_ref, b_ref, *rest):
        o_ref = rest[-1]
        part = lax.dot_general(a_ref[...], b_ref[...], dims, preferred_element_type=f32)
        if nk == 1:
            o_ref[...] = part if add is None else part + rest[0][...]
            return
        k = pl.program_id(2)

        @pl.when(k == 0)
        def _():
            o_ref[...] = part if add is None else part + rest[0][...]

        @pl.when(k > 0)
        def _():
            o_ref[...] += part

    b_spec = pl.BlockSpec((tn, tk), lambda i, j, k: (j, k)) if nt else pl.BlockSpec((tk, tn), lambda i, j, k: (k, j))
    in_specs = [pl.BlockSpec((tm, tk), lambda i, j, k: (i, k)), b_spec]
    args = [a, b]
    if add is not None:
        in_specs.append(pl.BlockSpec((tm, tn), lambda i, j, k: (i, j)))
        args.append(add)
    return _call_with_exchange(
        body, exchange, name=name, grid=(m // tm, n // tn, nk), in_specs=in_specs,
        out_specs=pl.BlockSpec((tm, tn), lambda i, j, k: (i, j)),
        out_shape=jax.ShapeDtypeStruct((m, n), f32), args=args)


def _call_with_exchange(body, exchange, *, name, grid, in_specs, out_specs, out_shape, args):
    if exchange is None:
        return pl.pallas_call(body, name=name, grid=grid, in_specs=in_specs, out_specs=out_specs, out_shape=out_shape,
                              compiler_params=_params(len(grid)))(*args)
    x_arrays, x_shapes, x_sems, start, wait = exchange
    n_in, n_xin, n_xout = len(args), len(x_arrays), len(x_shapes)

    def full_body(*refs):
        c_in, x_in = refs[:n_in], refs[n_in:n_in + n_xin]
        c_out = refs[n_in + n_xin]
        x_out = refs[n_in + n_xin + 1:n_in + n_xin + 1 + n_xout]
        sems = refs[n_in + n_xin + 1 + n_xout:]
        ids = [pl.program_id(d) for d in range(len(grid))]
        first, last = ids[0] == 0, ids[0] == grid[0] - 1
        for d in range(1, len(grid)):
            first = first & (ids[d] == 0)
            last = last & (ids[d] == grid[d] - 1)

        @pl.when(first)
        def _():
            start(x_in, x_out, sems)

        body(*c_in, c_out)

        @pl.when(last)
        def _():
            wait(x_in, x_out, sems)

    res = pl.pallas_call(
        full_body, name=name, grid=grid, in_specs=list(in_specs) + [_ANY] * n_xin,
        out_specs=[out_specs] + [_ANY] * n_xout, out_shape=[out_shape] + list(x_shapes),
        scratch_shapes=list(x_sems), compiler_params=_params(len(grid)),
    )(*args, *x_arrays)
    return res[0], list(res[1:])


def _matmul_tn(a, b, *, name, tm=1024, tn=1024, exchange=None):
    t, m = a.shape
    _, n = b.shape
    tm = _pick(m, (tm, 1024, 512, 128))
    tn = _pick(n, (tn, 1024, 512, 128))
    tt = _pick(t, (2048, 1024, 512, 256))
    nt = t // tt

    def body(a_ref, b_ref, o_ref):
        k = pl.program_id(2)
        part = lax.dot_general(a_ref[...], b_ref[...], (((0,), (0,)), ((), ())), preferred_element_type=f32)

        @pl.when(k == 0)
        def _():
            o_ref[...] = part

        @pl.when(k > 0)
        def _():
            o_ref[...] += part

    return _call_with_exchange(
        body, exchange, name=name, grid=(m // tm, n // tn, nt),
        in_specs=[pl.BlockSpec((tt, tm), lambda i, j, k: (k, i)), pl.BlockSpec((tt, tn), lambda i, j, k: (k, j))],
        out_specs=pl.BlockSpec((tm, tn), lambda i, j, k: (i, j)),
        out_shape=jax.ShapeDtypeStruct((m, n), f32), args=[a, b])


def _rms_fwd(x, g, *, name, exchange=None):
    t = x.shape[0]
    tb = _pick(t, (256, 128))

    def body(x_ref, g_ref, h_ref):
        xv = x_ref[...]
        r = lax.rsqrt(jnp.mean(xv * xv, axis=-1, keepdims=True) + EPS)
        h_ref[...] = (xv * r * g_ref[...]).astype(bf16)

    return _call_with_exchange(
        body, exchange, name=name, grid=(t // tb,), in_specs=[_row(tb, D), _fixed((1, D))], out_specs=_row(tb, D),
        out_shape=jax.ShapeDtypeStruct((t, D), bf16), args=[x, g])


def _rms_bwd(dh, x, g, dres, *, name):
    t = x.shape[0]
    tb = _pick(t, (256, 128))

    def body(dh_ref, x_ref, g_ref, dres_ref, dx_ref, dxb_ref, dg_ref):
        xv = x_ref[...]
        r = lax.rsqrt(jnp.mean(xv * xv, axis=-1, keepdims=True) + EPS)
        xh = xv * r
        dy = dh_ref[...]
        dyg = dy * g_ref[...]
        dx = dres_ref[...] + r * (dyg - xh * jnp.mean(dyg * xh, axis=-1, keepdims=True))
        dx_ref[...] = dx
        dxb_ref[...] = dx.astype(bf16)

        @pl.when(pl.program_id(0) == 0)
        def _():
            dg_ref[...] = jnp.zeros_like(dg_ref)

        dg_ref[...] += jnp.sum((dy * xh).reshape(tb // 8, 8, D), axis=0)

    return pl.pallas_call(
        body, name=name, grid=(t // tb,),
        in_specs=[_row(tb, D), _row(tb, D), _fixed((1, D)), _row(tb, D)],
        out_specs=[_row(tb, D), _row(tb, D), _fixed((8, D))],
        out_shape=[jax.ShapeDtypeStruct((t, D), f32), jax.ShapeDtypeStruct((t, D), bf16),
                   jax.ShapeDtypeStruct((8, D), f32)],
        compiler_params=_params(1),
    )(dh, x, g, dres)


def _gdn_gates(ab, alog, dtb):
    lane = lax.broadcasted_iota(jnp.int32, ab.shape, 1)
    g = -jnp.exp(alog) * _softplus(ab + dtb)
    beta = _sigmoid(ab)
    return jnp.where(lane < H, g, jnp.where(lane < 2 * H, beta, 0.0))


def _pre_fwd(p1, p2, wa, wg, alog, dtb):
    t = p1.shape[0]
    tb = 128

    def body(p0_ref, p0h_ref, pq_ref, pqh_ref, p2_ref, wa_ref, wg_ref, alog_ref, dtb_ref,
             ya_ref, qn_ref, kn_ref, vc_ref, gb_ref):
        p0 = p0_ref[...]
        h0 = _first_zero(p0h_ref)
        u = p0[:, D:2 * D] * p0[:, 2 * D:]
        uh = h0[:, D:2 * D] * h0[:, 2 * D:]
        ya_ref[...] = (p0[:, :D] * _conv_down(u, uh, wa_ref, 3)).astype(bf16)
        s = _conv_down(pq_ref[...], _first_zero(pqh_ref), wg_ref, 4)
        s = s * _sigmoid(s)
        for h in range(H):
            q = s[:, h * DH:(h + 1) * DH]
            k = s[:, D + h * DH:D + (h + 1) * DH]
            qn_ref[:, h * DH:(h + 1) * DH] = q * (lax.rsqrt(jnp.sum(q * q, axis=-1, keepdims=True) + EPS) * DH ** -0.5)
            kn_ref[:, h * DH:(h + 1) * DH] = k * lax.rsqrt(jnp.sum(k * k, axis=-1, keepdims=True) + EPS)
        vc_ref[...] = s[:, 2 * D:]
        gb_ref[...] = _gdn_gates(p2_ref[...], alog_ref[...], dtb_ref[...])

    return pl.pallas_call(
        body, name="pre_fwd", grid=(t // tb,),
        in_specs=[_row(tb, 3 * D, 0), _prev(tb, 3 * D, 0), _row(tb, 3 * D, 1), _prev(tb, 3 * D, 1), _row(tb, 128),
                  _fixed((8, D)), _fixed((8, 3 * D)), _fixed((1, 128)), _fixed((1, 128))],
        out_specs=[_row(tb, D), _row(tb, D), _row(tb, D), _row(tb, D), _row(tb, 128)],
        out_shape=[jax.ShapeDtypeStruct((t, D), bf16), jax.ShapeDtypeStruct((t, D), f32),
                   jax.ShapeDtypeStruct((t, D), f32), jax.ShapeDtypeStruct((t, D), f32),
                   jax.ShapeDtypeStruct((t, 128), f32)],
        compiler_params=_params(1),
    )(p1, p1, p1, p1, p2, wa, wg, alog, dtb)


def _post_fwd(o, p1, gn):
    t = o.shape[0]
    tb = _pick(t, (256, 128))

    def body(o_ref, z_ref, gn_ref, yb_ref):
        for h in range(H):
            sl = slice(h * DH, (h + 1) * DH)
            oh = o_ref[:, sl]
            z = z_ref[:, sl]
            r = lax.rsqrt(jnp.mean(oh * oh, axis=-1, keepdims=True) + EPS)
            yb_ref[:, sl] = (oh * r * gn_ref[...] * (z * _sigmoid(z))).astype(bf16)

    return pl.pallas_call(
        body, name="post_fwd", grid=(t // tb,), in_specs=[_row(tb, D), _row(tb, D, 6), _fixed((1, DH))],
        out_specs=_row(tb, D), out_shape=jax.ShapeDtypeStruct((t, D), bf16), compiler_params=_params(1),
    )(o, p1, gn)


def _post_bwd(dyb, o, p1, gn):
    t = o.shape[0]
    tb = _pick(t, (256, 128))

    def body(dyb_ref, o_ref, z_ref, gn_ref, do_ref, dz_ref, dgn_ref):
        @pl.when(pl.program_id(0) == 0)
        def _():
            dgn_ref[...] = jnp.zeros_like(dgn_ref)

        gn_v = gn_ref[...]
        acc = jnp.zeros((8, DH), f32)
        for h in range(H):
            sl = slice(h * DH, (h + 1) * DH)
            oh = o_ref[:, sl]
            z = z_ref[:, sl]
            dy = dyb_ref[:, sl]
            r = lax.rsqrt(jnp.mean(oh * oh, axis=-1, keepdims=True) + EPS)
            on = oh * r
            sg = _sigmoid(z)
            sz = z * sg
            don = dy * sz
            dz_ref[:, sl] = (dy * on * gn_v * (sg * (1.0 + z * (1.0 - sg)))).astype(bf16)
            acc = acc + jnp.sum((don * on).reshape(tb // 8, 8, DH), axis=0)
            doh = don * gn_v
            do_ref[:, sl] = r * (doh - on * jnp.mean(doh * on, axis=-1, keepdims=True))
        dgn_ref[...] += acc

    return pl.pallas_call(
        body, name="post_bwd", grid=(t // tb,),
        in_specs=[_row(tb, D), _row(tb, D), _row(tb, D, 6), _fixed((1, DH))],
        out_specs=[_row(tb, D), _row(tb, D), _fixed((8, DH))],
        out_shape=[jax.ShapeDtypeStruct((t, D), f32), jax.ShapeDtypeStruct((t, D), bf16),
                   jax.ShapeDtypeStruct((8, DH), f32)],
        compiler_params=_params(1),
    )(dyb, o, p1, gn)


def _mix_fwd(ya, yb, p1):
    t = ya.shape[0]
    tb = _pick(t, (256, 128))

    def body(ya_ref, yb_ref, ga_ref, gb_ref, mix_ref):
        mix_ref[...] = (_sigmoid(ga_ref[...]) * ya_ref[...] + _sigmoid(gb_ref[...]) * yb_ref[...]).astype(bf16)

    return pl.pallas_call(
        body, name="mix_fwd", grid=(t // tb,), in_specs=[_row(tb, D), _row(tb, D), _row(tb, D, 7), _row(tb, D, 8)],
        out_specs=_row(tb, D), out_shape=jax.ShapeDtypeStruct((t, D), bf16), compiler_params=_params(1),
    )(ya, yb, p1, p1)


def _mix_bwd(dmix, ya, yb, p1):
    t = ya.shape[0]
    tb = _pick(t, (256, 128))

    def body(dm_ref, ya_ref, yb_ref, ga_ref, gb_ref, dya_ref, dyb_ref, dg_ref):
        dm = dm_ref[...]
        sa = _sigmoid(ga_ref[...])
        sb = _sigmoid(gb_ref[...])
        dya_ref[...] = (dm * sa).astype(bf16)
        dyb_ref[...] = (dm * sb).astype(bf16)
        dg_ref[:, :D] = (dm * ya_ref[...] * sa * (1.0 - sa)).astype(bf16)
        dg_ref[:, D:] = (dm * yb_ref[...] * sb * (1.0 - sb)).astype(bf16)

    return pl.pallas_call(
        body, name="mix_bwd", grid=(t // tb,),
        in_specs=[_row(tb, D), _row(tb, D), _row(tb, D), _row(tb, D, 7), _row(tb, D, 8)],
        out_specs=[_row(tb, D), _row(tb, D), _row(tb, 2 * D)],
        out_shape=[jax.ShapeDtypeStruct((t, D), bf16), jax.ShapeDtypeStruct((t, D), bf16),
                   jax.ShapeDtypeStruct((t, 2 * D), bf16)],
        compiler_params=_params(1),
    )(dmix, ya, yb, p1, p1)


def _ffn_fwd(up, wf):
    t = up.shape[0]
    tb = 128

    def body(up_ref, uph_ref, wf_ref, act_ref):
        c = _conv_down(up_ref[...], _first_zero(uph_ref), wf_ref, 3)
        gate = c[:, :DFF]
        act_ref[...] = (gate * _sigmoid(gate) * c[:, DFF:]).astype(bf16)

    return pl.pallas_call(
        body, name="ffn_fwd", grid=(t // tb,), in_specs=[_row(tb, 2 * DFF), _prev(tb, 2 * DFF), _fixed((8, 2 * DFF))],
        out_specs=_row(tb, DFF), out_shape=jax.ShapeDtypeStruct((t, DFF), bf16), compiler_params=_params(1),
    )(up, up, wf)


def _ffn_bwd1(dact, up, wf):
    t = up.shape[0]
    tb = 128

    def body(da_ref, up_ref, uph_ref, wf_ref, dc_ref, dw_ref):
        @pl.when(pl.program_id(0) == 0)
        def _():
            dw_ref[...] = jnp.zeros_like(dw_ref)

        taps = _taps_down(up_ref[...], _first_zero(uph_ref), 3)
        c = _conv_taps(taps, wf_ref)
        gate = c[:, :DFF]
        val = c[:, DFF:]
        sg = _sigmoid(gate)
        da = da_ref[...]
        dgate = da * val * (sg * (1.0 + gate * (1.0 - sg)))
        dval = da * (gate * sg)
        dc_ref[:, :DFF] = dgate
        dc_ref[:, DFF:] = dval
        _conv_weight_grad(dw_ref, jnp.concatenate([dgate, dval], axis=1), taps)

    return pl.pallas_call(
        body, name="ffn_bwd1", grid=(t // tb,),
        in_specs=[_row(tb, DFF), _row(tb, 2 * DFF), _prev(tb, 2 * DFF), _fixed((8, 2 * DFF))],
        out_specs=[_row(tb, 2 * DFF), _fixed((8, 2 * DFF))],
        out_shape=[jax.ShapeDtypeStruct((t, 2 * DFF), f32), jax.ShapeDtypeStruct((8, 2 * DFF), f32)],
        compiler_params=_params(1),
    )(dact, up, up, wf)


def _ffn_bwd2(dc, wf):
    t = dc.shape[0]
    tb = 128
    nb = t // tb

    def body(dc_ref, dch_ref, wf_ref, dup_ref):
        dup_ref[...] = _conv_up(dc_ref[...], _last_zero(dch_ref, nb), wf_ref, 3).astype(bf16)

    return pl.pallas_call(
        body, name="ffn_bwd2", grid=(nb,), in_specs=[_row(tb, 2 * DFF), _next(tb, 2 * DFF, t), _fixed((8, 2 * DFF))],
        out_specs=_row(tb, 2 * DFF), out_shape=jax.ShapeDtypeStruct((t, 2 * DFF), bf16), compiler_params=_params(1),
    )(dc, dc, wf)


def _final(x3, tgt, g):
    t = x3.shape[0]
    tb = _pick(t, (256, 128))

    def body(x_ref, t_ref, g_ref, loss_ref, dx_ref, dxb_ref, dg_ref):
        @pl.when(pl.program_id(0) == 0)
        def _():
            loss_ref[...] = jnp.zeros_like(loss_ref)
            dg_ref[...] = jnp.zeros_like(dg_ref)

        xv = x_ref[...]
        r = lax.rsqrt(jnp.mean(xv * xv, axis=-1, keepdims=True) + EPS)
        xh = xv * r
        gv = g_ref[...]
        e = xh * gv - t_ref[...]
        lrow = 0.5 * jnp.mean(e * e, axis=-1, keepdims=True)
        loss_ref[...] += jnp.sum(jnp.broadcast_to(lrow, (tb, 128)).reshape(tb // 8, 8, 128), axis=0)
        dy = e * (1.0 / D)
        dyg = dy * gv
        dx = r * (dyg - xh * jnp.mean(dyg * xh, axis=-1, keepdims=True))
        dx_ref[...] = dx
        dxb_ref[...] = dx.astype(bf16)
        dg_ref[...] += jnp.sum((dy * xh).reshape(tb // 8, 8, D), axis=0)

    return pl.pallas_call(
        body, name="final", grid=(t // tb,), in_specs=[_row(tb, D), _row(tb, D), _fixed((1, D))],
        out_specs=[_fixed((8, 128)), _row(tb, D), _row(tb, D), _fixed((8, D))],
        out_shape=[jax.ShapeDtypeStruct((8, 128), f32), jax.ShapeDtypeStruct((t, D), f32),
                   jax.ShapeDtypeStruct((t, D), bf16), jax.ShapeDtypeStruct((8, D), f32)],
        compiler_params=_params(1),
    )(x3, tgt, g)


def _pre_bwd1(p1, p2, dya_in, dqn, dkn, dvc, dgb, gbeta, wa, wg, alog, dtb):
    t = p1.shape[0]
    tb = 128

    def body(p0_ref, p0h_ref, pq_ref, pqh_ref, p2_ref, dya_ref, dqn_ref, dkn_ref, dvc_ref, dgb_ref, gb_ref,
             wa_ref, wg_ref, alog_ref, dtb_ref,
             dbg_ref, dca_ref, dc4_ref, dp2_ref, dwa_ref, dwg_ref, dal_ref, ddt_ref):
        @pl.when(pl.program_id(0) == 0)
        def _():
            dwa_ref[...] = jnp.zeros_like(dwa_ref)
            dwg_ref[...] = jnp.zeros_like(dwg_ref)
            dal_ref[...] = jnp.zeros_like(dal_ref)
            ddt_ref[...] = jnp.zeros_like(ddt_ref)

        p0 = p0_ref[...]
        h0 = _first_zero(p0h_ref)
        u = p0[:, D:2 * D] * p0[:, 2 * D:]
        uh = h0[:, D:2 * D] * h0[:, 2 * D:]
        dya = dya_ref[...]
        u_taps = _taps_down(u, uh, 3)
        dbg_ref[...] = (dya * _conv_taps(u_taps, wa_ref)).astype(bf16)
        dca = dya * p0[:, :D]
        dca_ref[...] = dca
        _conv_weight_grad(dwa_ref, dca, u_taps)

        q_taps = _taps_down(pq_ref[...], _first_zero(pqh_ref), 4)
        c4 = _conv_taps(q_taps, wg_ref)
        sg = _sigmoid(c4)
        s = c4 * sg
        dsilu = sg * (1.0 + c4 * (1.0 - sg))
        for h in range(H):
            for base, d_ref, scale in ((0, dqn_ref, DH ** -0.5), (D, dkn_ref, 1.0)):
                sl = slice(base + h * DH, base + (h + 1) * DH)
                a = s[:, sl]
                r = lax.rsqrt(jnp.sum(a * a, axis=-1, keepdims=True) + EPS)
                an = a * r
                dn = d_ref[:, h * DH:(h + 1) * DH] * scale
                dc4_ref[:, sl] = r * (dn - an * jnp.sum(dn * an, axis=-1, keepdims=True)) * dsilu[:, sl]
        dc4_ref[:, 2 * D:] = dvc_ref[...] * dsilu[:, 2 * D:]
        _conv_weight_grad(dwg_ref, dc4_ref[...], q_taps)

        ab = p2_ref[...]
        lane = lax.broadcasted_iota(jnp.int32, ab.shape, 1)
        dgbv = dgb_ref[...]
        gbv = gb_ref[...]
        da = dgbv * (-jnp.exp(alog_ref[...])) * _sigmoid(ab + dtb_ref[...])
        db = dgbv * gbv * (1.0 - gbv)
        dp2_ref[...] = jnp.where(lane < H, da, jnp.where(lane < 2 * H, db, 0.0)).astype(bf16)
        dal = jnp.where(lane < H, dgbv * gbv, 0.0)
        ddt = jnp.where(lane < H, da, 0.0)
        dal_ref[...] += jnp.sum(dal.reshape(tb // 8, 8, 128), axis=0)
        ddt_ref[...] += jnp.sum(ddt.reshape(tb // 8, 8, 128), axis=0)

    return pl.pallas_call(
        body, name="pre_bwd1", grid=(t // tb,),
        in_specs=[_row(tb, 3 * D, 0), _prev(tb, 3 * D, 0), _row(tb, 3 * D, 1), _prev(tb, 3 * D, 1), _row(tb, 128),
                  _row(tb, D), _row(tb, D), _row(tb, D), _row(tb, D), _row(tb, 128), _row(tb, 128),
                  _fixed((8, D)), _fixed((8, 3 * D)), _fixed((1, 128)), _fixed((1, 128))],
        out_specs=[_row(tb, D), _row(tb, D), _row(tb, 3 * D), _row(tb, 128),
                   _fixed((8, D)), _fixed((8, 3 * D)), _fixed((8, 128)), _fixed((8, 128))],
        out_shape=[jax.ShapeDtypeStruct((t, D), bf16), jax.ShapeDtypeStruct((t, D), f32),
                   jax.ShapeDtypeStruct((t, 3 * D), f32), jax.ShapeDtypeStruct((t, 128), bf16),
                   jax.ShapeDtypeStruct((8, D), f32), jax.ShapeDtypeStruct((8, 3 * D), f32),
                   jax.ShapeDtypeStruct((8, 128), f32), jax.ShapeDtypeStruct((8, 128), f32)],
        compiler_params=_params(1),
    )(p1, p1, p1, p1, p2, dya_in, dqn, dkn, dvc, dgb, gbeta, wa, wg, alog, dtb)


def _pre_bwd2(dca, dc4, p1, dbg, dz, dgates, wa, wg, exchange=None):
    t = p1.shape[0]
    tb = 128
    nb = t // tb

    def body(dca_ref, dcah_ref, dc4_ref, dc4h_ref, p0_ref, dbg_ref, dz_ref, dgt_ref, wa_ref, wg_ref, dp_ref):
        du = _conv_up(dca_ref[...], _last_zero(dcah_ref, nb), wa_ref, 3)
        dp_ref[:, :D] = dbg_ref[...]
        dp_ref[:, D:2 * D] = (du * p0_ref[:, 2 * D:]).astype(bf16)
        dp_ref[:, 2 * D:3 * D] = (du * p0_ref[:, D:2 * D]).astype(bf16)
        dp_ref[:, 3 * D:6 * D] = _conv_up(dc4_ref[...], _last_zero(dc4h_ref, nb), wg_ref, 4).astype(bf16)
        dp_ref[:, 6 * D:7 * D] = dz_ref[...]
        dp_ref[:, 7 * D:] = dgt_ref[...]

    return _call_with_exchange(
        body, exchange, name="pre_bwd2", grid=(nb,),
        in_specs=[_row(tb, D), _next(tb, D, t), _row(tb, 3 * D), _next(tb, 3 * D, t), _row(tb, 3 * D, 0),
                  _row(tb, D), _row(tb, D), _row(tb, 2 * D), _fixed((8, D)), _fixed((8, 3 * D))],
        out_specs=_row(tb, NW1), out_shape=jax.ShapeDtypeStruct((t, NW1), bf16),
        args=[dca, dca, dc4, dc4, p1, dbg, dz, dgates, wa, wg])


def _chunk_consts():
    r = lax.broadcasted_iota(jnp.int32, (CH, CH), 0)
    c = lax.broadcasted_iota(jnp.int32, (CH, CH), 1)
    return r, c, (r == c).astype(f32)


def _tri_inverse(lows, eye, r, c):
    def same_block(b):
        return jnp.bitwise_xor(r, c) < b

    xs = [jnp.where(same_block(8), -low, 0.0) for low in lows]
    ts = [eye + x for x in xs]
    for _ in range(2):
        xs = [_idot(x, x) for x in xs]
        ts = [t + _idot(t, x) for t, x in zip(ts, xs)]
    for b in (8, 16, 32):
        below = same_block(2 * b) & jnp.logical_not(same_block(b))
        ts = [t - _idot(_idot(t, jnp.where(below, low, 0.0)), t) for t, low in zip(ts, lows)]
    return ts


def _chunk_common(q, k, v, gcol, bcol, r, c, eye):
    grow = jnp.sum(eye * gcol, axis=0, keepdims=True)
    dec = jnp.exp(jnp.where(r >= c, gcol - grow, -jnp.inf))
    rcol = lax.broadcasted_iota(jnp.int32, (CH, 1), 0)
    glast = jnp.sum(jnp.where(rcol == CH - 1, gcol, 0.0), axis=0, keepdims=True)
    eg = jnp.exp(gcol)
    el = jnp.exp(glast - gcol)
    kb = k * bcol
    vb = v * bcol
    kk = _bdot_nt(kb, k)
    low = jnp.where(r > c, kk * dec, 0.0)
    qk = _bdot_nt(q, k)
    att = qk * dec
    return grow, dec, glast, eg, el, kb, vb, kk, low, qk, att, rcol


def _gdn_fwd(qn, kn, vc, gbeta):
    t = qn.shape[0]
    n_chunks = t // CH

    def body(q_ref, k_ref, v_ref, gb_ref, o_ref, s_ref, t_ref, state):
        @pl.when(pl.program_id(0) == 0)
        def _():
            state[...] = jnp.zeros_like(state)

        r, c, eye = _chunk_consts()
        tri = (r >= c).astype(f32)
        heads = range(H)
        keys = [(s, h) for s in range(GDN_STEP) for h in heads]
        rows = [slice(s * CH, (s + 1) * CH) for s in range(GDN_STEP)]
        gbs = [gb_ref[rows[s], :] for s in range(GDN_STEP)]
        galls = [_hdot(tri, gb) for gb in gbs]
        qs = {(s, h): q_ref[rows[s], h * DH:(h + 1) * DH] for s, h in keys}
        ks = {(s, h): k_ref[rows[s], h * DH:(h + 1) * DH] for s, h in keys}
        cm = {(s, h): _chunk_common(qs[s, h], ks[s, h], v_ref[rows[s], h * DH:(h + 1) * DH], galls[s][:, h:h + 1],
                                    gbs[s][:, H + h:H + h + 1], r, c, eye) for s, h in keys}
        invs = dict(zip(keys, _tri_inverse([cm[key][8] for key in keys], eye, r, c)))
        uws = {key: _bdot(invs[key], jnp.concatenate([cm[key][6], cm[key][5] * cm[key][3]], axis=1)) for key in keys}
        sts = [state[h] for h in heads]
        for s in range(GDN_STEP):
            vns = [uws[s, h][:, :DH] - _bdot(uws[s, h][:, DH:], sts[h]) for h in heads]
            outs = [_bdot(qs[s, h] * cm[s, h][3], sts[h]) + _bdot(cm[s, h][10], vns[h]) for h in heads]
            news = [sts[h] * jnp.exp(cm[s, h][2]) + _bdot_tn(ks[s, h] * cm[s, h][4], vns[h]) for h in heads]
            for h in heads:
                s_ref[s, h] = sts[h].astype(bf16)
                t_ref[s, h] = invs[s, h]
                o_ref[rows[s], h * DH:(h + 1) * DH] = outs[h]
            sts = news
        for h in heads:
            state[h] = sts[h]

    tb = GDN_STEP * CH
    return pl.pallas_call(
        body, name="gdn_fwd", grid=(t // tb,),
        in_specs=[_row(tb, D), _row(tb, D), _row(tb, D), _row(tb, 128)],
        out_specs=[_row(tb, D), pl.BlockSpec((GDN_STEP, H, DH, DH), lambda i: (i, 0, 0, 0)),
                   pl.BlockSpec((GDN_STEP, H, CH, CH), lambda i: (i, 0, 0, 0))],
        out_shape=[jax.ShapeDtypeStruct((t, D), f32), jax.ShapeDtypeStruct((n_chunks, H, DH, DH), bf16),
                   jax.ShapeDtypeStruct((n_chunks, H, CH, CH), f32)],
        scratch_shapes=[pltpu.VMEM((H, DH, DH), f32)],
        compiler_params=_params(1),
    )(qn, kn, vc, gbeta)


def _gdn_bwd(qn, kn, vc, gbeta, do, s_all, t_all):
    t = qn.shape[0]

    def body(q_ref, k_ref, v_ref, gb_ref, do_ref, s_ref, t_ref, dq_ref, dk_ref, dv_ref, dgb_ref, dstate):
        @pl.when(pl.program_id(0) == 0)
        def _():
            dstate[...] = jnp.zeros_like(dstate)

        r, c, eye = _chunk_consts()
        tril = r >= c
        lane = lax.broadcasted_iota(jnp.int32, (1, 128), 1)
        hs = range(H)

        def each(fn, *lists):
            return [fn(*args) for args in zip(*lists)]

        def rsum(a):
            return jnp.sum(a, axis=1, keepdims=True)

        def before_state(s):
            rows = slice(s * CH, (s + 1) * CH)
            gb = gb_ref[rows, :]
            gall = _hdot(tril.astype(f32), gb)
            p = {"rows": rows}
            p["q"] = q = [q_ref[rows, h * DH:(h + 1) * DH] for h in hs]
            p["k"] = k = [k_ref[rows, h * DH:(h + 1) * DH] for h in hs]
            p["v"] = v = [v_ref[rows, h * DH:(h + 1) * DH] for h in hs]
            p["dout"] = dout = [do_ref[rows, h * DH:(h + 1) * DH] for h in hs]
            p["inv"] = inv = [t_ref[s, h] for h in hs]
            p["st"] = st = [s_ref[s, h] for h in hs]
            p["bcol"] = bcol = [gb[:, H + h:H + h + 1] for h in hs]
            cm = [_chunk_common(q[h], k[h], v[h], gall[:, h:h + 1], bcol[h], r, c, eye) for h in hs]
            for name, i in (("dec", 1), ("glast", 2), ("eg", 3), ("el", 4), ("kb", 5), ("vb", 6), ("low", 8), ("att", 10)):
                p[name] = [m[i] for m in cm]
            p["rcol"] = cm[0][11]
            p["elast"] = each(jnp.exp, p["glast"])
            p["kbg"] = each(jnp.multiply, p["kb"], p["eg"])
            uw = each(lambda i, a, b: _bdot(i, jnp.concatenate([a, b], axis=1)), inv, p["vb"], p["kbg"])
            p["u"] = [a[:, :DH] for a in uw]
            p["w"] = [a[:, DH:] for a in uw]
            p["vn"] = each(lambda a, b, x: a - _bdot(b, x), p["u"], p["w"], st)
            p["qd"] = each(jnp.multiply, q, p["eg"])
            p["kd"] = each(jnp.multiply, k, p["el"])
            p["dqd"] = each(_bdot_nt, dout, st)
            p["datt"] = each(lambda d, x: jnp.where(tril, _bdot_nt(d, x), 0.0), dout, p["vn"])
            p["dqk"] = each(jnp.multiply, p["datt"], p["dec"])
            p["qd_do"] = each(_bdot_tn, p["qd"], dout)
            p["att_do"] = each(_bdot_tn, p["att"], dout)
            return p

        def after_state(p, ds):
            q, k, v, st, inv, bcol = p["q"], p["k"], p["v"], p["st"], p["inv"], p["bcol"]
            eg, el, kb, u, w = p["eg"], p["el"], p["kb"], p["u"], p["w"]
            dvn = each(lambda a, kk, x: a + _bdot(kk, x), p["att_do"], p["kd"], ds)
            dkd = each(_bdot_nt, p["vn"], ds)
            dw = each(lambda a, x: -_bdot_nt(a, x), dvn, st)
            new_ds = each(lambda x, e, a, ww, dv_: x * e + a - _bdot_tn(ww, dv_), ds, p["elast"], p["qd_do"], w, dvn)
            dglast = each(lambda e, x, d: e * jnp.sum(rsum(x.astype(f32) * d), axis=0, keepdims=True), p["elast"], st, ds)
            dr = each(lambda i, a, b: _bdot_tn(i, jnp.concatenate([a, b], axis=1)), inv, dvn, dw)
            dvb = [a[:, :DH] for a in dr]
            dkbg = [a[:, DH:] for a in dr]
            dlow = each(lambda a, b, x, y: -jnp.where(r > c, _bdot_nt(a, b) + _bdot_nt(x, y), 0.0), dvb, u, dkbg, w)
            dkk = each(jnp.multiply, dlow, p["dec"])
            mm = each(lambda a, b, x, y: a * b + x * y, dlow, p["low"], p["datt"], p["att"])
            dkb = each(lambda a, kk, b, e: _bdot(a, kk) + b * e, dkk, k, dkbg, eg)
            dk = each(lambda a, b, x, y, d, e, f, g: _bdot_tn(a, b) + _bdot_tn(x, y) + d * e + f * g,
                      dkk, kb, p["dqk"], q, dkd, el, dkb, bcol)
            dq = each(lambda a, kk, d, e: _bdot(a, kk) + d * e, p["dqk"], k, p["dqd"], eg)
            dv = each(jnp.multiply, dvb, bcol)
            dbeta = each(lambda a, b, x, y: rsum(a * b) + rsum(x * y), dkb, k, dvb, v)
            deg = each(lambda a, b, x, y: rsum(a * b) + rsum(x * y), dkbg, kb, p["dqd"], q)
            delc = each(lambda a, b, e: rsum(a * b) * e, dkd, k, el)
            dgc = each(lambda m, a, e, d: rsum(m) - rsum(eye * jnp.sum(m, axis=0, keepdims=True)) + a * e - d,
                       mm, deg, eg, delc)
            dgc = each(lambda g, d, l: g + jnp.where(p["rcol"] == CH - 1, jnp.sum(d, axis=0, keepdims=True) + l, 0.0),
                       dgc, delc, dglast)
            dg_acc = jnp.zeros((CH, 128), f32)
            db_acc = jnp.zeros((CH, 128), f32)
            rows = p["rows"]
            for h in hs:
                dq_ref[rows, h * DH:(h + 1) * DH] = dq[h]
                dk_ref[rows, h * DH:(h + 1) * DH] = dk[h]
                dv_ref[rows, h * DH:(h + 1) * DH] = dv[h]
                dg_acc = dg_acc + dgc[h] * (lane == h).astype(f32)
                db_acc = db_acc + dbeta[h] * (lane == H + h).astype(f32)
            dgb_ref[rows, :] = _hdot((r <= c).astype(f32), dg_acc) + db_acc
            return new_ds

        order = list(reversed(range(GDN_STEP)))
        pre = [before_state(s) for s in order]
        ds = [dstate[h] for h in hs]
        for p in pre:
            ds = after_state(p, ds)
        for h in hs:
            dstate[h] = ds[h]

    tb = GDN_STEP * CH
    n_steps = t // tb
    rev = lambda i: (n_steps - 1 - i, 0)
    rev4 = lambda i: (n_steps - 1 - i, 0, 0, 0)
    return pl.pallas_call(
        body, name="gdn_bwd", grid=(n_steps,),
        in_specs=[pl.BlockSpec((tb, D), rev), pl.BlockSpec((tb, D), rev), pl.BlockSpec((tb, D), rev),
                  pl.BlockSpec((tb, 128), rev), pl.BlockSpec((tb, D), rev),
                  pl.BlockSpec((GDN_STEP, H, DH, DH), rev4), pl.BlockSpec((GDN_STEP, H, CH, CH), rev4)],
        out_specs=[pl.BlockSpec((tb, D), rev), pl.BlockSpec((tb, D), rev), pl.BlockSpec((tb, D), rev),
                   pl.BlockSpec((tb, 128), rev)],
        out_shape=[jax.ShapeDtypeStruct((t, D), f32)] * 3 + [jax.ShapeDtypeStruct((t, 128), f32)],
        scratch_shapes=[pltpu.VMEM((H, DH, DH), f32)],
        compiler_params=_params(1),
    )(qn, kn, vc, gbeta, do, s_all, t_all)


def _pad_rows(w, rows=8):
    return jnp.pad(w, ((0, rows - w.shape[0]), (0, 0)))


_REST = ("w_up", "w_a_out", "w_b_out", "w_o", "w_down")


def _local_step(x, tgt, w, comm=None):
    g1 = w["norm_mix_g"].reshape(1, D)
    if comm is None:
        h1 = _rms_fwd(x, g1, name="rms1_fwd")
    else:
        h1, gathered = _rms_fwd(x, g1, name="rms1_fwd", exchange=comm.gather_first())
        w = {**w, **comm.finish_first(gathered)}
    w1, w2 = w["w1"], w["w2"]
    wa = _pad_rows(w["conv_a_w"])
    wg = _pad_rows(w["gdn_conv_w"])
    wf = _pad_rows(w["ffn_conv_w"])
    alog = jnp.pad(w["gdn_A_log"].reshape(1, H), ((0, 0), (0, 128 - H)))
    dtb = jnp.pad(w["gdn_dt_bias"].reshape(1, H), ((0, 0), (0, 128 - H)))
    g2 = w["norm_ffn_g"].reshape(1, D)
    g3 = w["norm_final_g"].reshape(1, D)
    gn = w["gdn_norm_g"].reshape(1, DH)

    if comm is None:
        p1 = _matmul(h1, w1, name="mm_in")
    else:
        p1, gathered = _matmul(h1, w1, name="mm_in", exchange=comm.gather_rest())
        w = {**w, **comm.finish_gather(gathered)}
    p2 = _matmul(h1, w2, name="mm_in_ab")
    ya_in, qn, kn, vc, gbeta = _pre_fwd(p1, p2, wa, wg, alog, dtb)
    o, s_all, t_all = _gdn_fwd(qn, kn, vc, gbeta)
    yb_in = _post_fwd(o, p1, gn)
    ya = _matmul(ya_in, w["w_a_out"], name="mm_a")
    yb = _matmul(yb_in, w["w_b_out"], name="mm_b")
    mix = _mix_fwd(ya, yb, p1)
    x2 = _matmul(mix, w["w_o"], name="mm_o", add=x)
    h2 = _rms_fwd(x2, g2, name="rms2_fwd")
    up = _matmul(h2, w["w_up"], name="mm_up", tn=DFF // 2)
    act = _ffn_fwd(up, wf)
    x3 = _matmul(act, w["w_down"], name="mm_down", add=x2, tm=512)
    loss_p, dx3, dx3b, dg3 = _final(x3, tgt, g3)

    grads = {"norm_final_g": dg3}
    dact = _matmul(dx3b, w["w_down"], nt=True, name="mm_down_dx", tm=512, tn=DFF)
    grads["w_down"] = _matmul_tn(act, dx3b, name="mm_down_dw", tm=DFF // 2)
    dc, dwf = _ffn_bwd1(dact, up, wf)
    grads["ffn_conv_w"] = dwf
    dup = _ffn_bwd2(dc, wf)
    dh2 = _matmul(dup, w["w_up"], nt=True, name="mm_up_dx", tk=DFF)
    grads["w_up"] = _matmul_tn(h2, dup, name="mm_up_dw", tn=512)
    dx2, dx2b, dg2 = _rms_bwd(dh2, x2, g2, dx3, name="rms2_bwd")
    grads["norm_ffn_g"] = dg2
    dmix = _matmul(dx2b, w["w_o"], nt=True, name="mm_o_dx")
    grads["w_o"] = _matmul_tn(mix, dx2b, name="mm_o_dw")
    dya, dyb, dgates = _mix_bwd(dmix, ya, yb, p1)
    dya_in = _matmul(dya, w["w_a_out"], nt=True, name="mm_a_dx")
    grads["w_a_out"] = _matmul_tn(ya_in, dya, name="mm_a_dw")
    dyb_in = _matmul(dyb, w["w_b_out"], nt=True, name="mm_b_dx")
    grads["w_b_out"] = _matmul_tn(yb_in, dyb, name="mm_b_dw")
    do, dz, dgn = _post_bwd(dyb_in, o, p1, gn)
    grads["gdn_norm_g"] = dgn
    dqn, dkn, dvc, dgb = _gdn_bwd(qn, kn, vc, gbeta, do, s_all, t_all)
    dbg, dca, dc4, dp2, dwa, dwg, dal, ddt = _pre_bwd1(p1, p2, dya_in, dqn, dkn, dvc, dgb, gbeta, wa, wg, alog, dtb)
    grads["conv_a_w"] = dwa
    grads["gdn_conv_w"] = dwg
    grads["gdn_A_log"] = dal
    grads["gdn_dt_bias"] = ddt
    grads["w2"] = _matmul_tn(h1, dp2, name="mm_in_ab_dw")
    if comm is None:
        dp1 = _pre_bwd2(dca, dc4, p1, dbg, dz, dgates, wa, wg)
        grads["w1"] = _matmul_tn(h1, dp1, name="mm_in_dw")
        dh1 = _matmul(dp1, w1, nt=True, name="mm_in_dx", tm=512, tk=NW1 // 2)
    else:
        exchange, blocks = comm.reduce_halves(_REST, grads)
        dp1, recv = _pre_bwd2(dca, dc4, p1, dbg, dz, dgates, wa, wg, exchange=exchange)
        exchange, sums = comm.reduce_sums(_REST, blocks, recv)
        grads["w1"], recv = _matmul_tn(h1, dp1, name="mm_in_dw", exchange=exchange)
        comm.finish_reduce(_REST, sums, recv)
        exchange, blocks = comm.reduce_halves(("w_in",), grads)
        exchange, sums = comm.reduce_sums(("w_in",), blocks, _run_exchange(exchange, name="rs_sibling_w_in"))
        dh1, recv = _matmul(dp1, w1, nt=True, name="mm_in_dx", tm=512, tk=NW1 // 2, exchange=exchange)
        comm.finish_reduce(("w_in",), sums, recv)
    dh1 = _matmul(dp2, w2, nt=True, name="mm_in_ab_dx", add=dh1)
    dx, _, dg1 = _rms_bwd(dh1, x, g1, dx2, name="rms1_bwd")
    grads["norm_mix_g"] = dg1
    return loss_p, dx, grads


_ANY = pl.BlockSpec(memory_space=pl.ANY)


def _remote(src, dst, send_sem, recv_sem, to):
    return pltpu.make_async_remote_copy(src_ref=src, dst_ref=dst, send_sem=send_sem, recv_sem=recv_sem,
                                        device_id=to, device_id_type=MESH)


def _run_exchange(exchange, *, name):
    arrays, shapes, sems, start, wait = exchange
    n_in, n_out = len(arrays), len(shapes)

    def body(*refs):
        start(refs[:n_in], refs[n_in:n_in + n_out], refs[n_in + n_out:])
        wait(refs[:n_in], refs[n_in:n_in + n_out], refs[n_in + n_out:])

    return pl.pallas_call(body, name=name, out_shape=list(shapes), in_specs=[_ANY] * n_in, out_specs=[_ANY] * n_out,
                          scratch_shapes=list(sems))(*arrays)


def _gather_exchange(shards):
    n = len(shards)

    def copies(x_refs, out_refs, sems):
        send_sems, recv_sems, local_sems = sems
        x, y, c = lax.axis_index("x"), lax.axis_index("y"), lax.axis_index("c")
        me, sibling = (x, y, c), (x, y, 1 - c)
        chips = [(1 - x, y), (x, 1 - y), (1 - x, 1 - y)]

        def copy(a, k, blk, to, from_input=False):
            dst = out_refs[a].at[4 * blk[0] + 2 * blk[1] + blk[2]]
            return _remote(x_refs[a] if from_input else dst, dst, send_sems.at[a, k], recv_sems.at[a, k], to)

        mine = [pltpu.make_async_copy(x_refs[a], out_refs[a].at[4 * x + 2 * y + c], local_sems.at[a]) for a in range(n)]
        first = []
        for a in range(n):
            first.append(copy(a, 0, me, sibling, from_input=True))
            first += [copy(a, 1 + j, me, (*chip, c), from_input=True) for j, chip in enumerate(chips)]
        return copy, mine, first, me, sibling, chips, c

    def start(x_refs, out_refs, sems):
        _, mine, first, *_ = copies(x_refs, out_refs, sems)
        for cp in mine + first:
            cp.start()

    def wait(x_refs, out_refs, sems):
        copy, mine, first, me, sibling, chips, c = copies(x_refs, out_refs, sems)
        passed = []
        for j, chip in enumerate(chips):
            for a in range(n):
                copy(a, 1 + j, (*chip, c), me).wait_recv()
                passed.append(copy(a, 4 + j, (*chip, c), sibling))
                passed[-1].start()
        for a in range(n):
            copy(a, 0, sibling, me).wait_recv()
            for j, chip in enumerate(chips):
                copy(a, 4 + j, (*chip, 1 - c), me).wait_recv()
        for cp in first + passed:
            cp.wait_send()
        for cp in mine:
            cp.wait()

    shapes = [jax.ShapeDtypeStruct((N_DEV, *s.shape), s.dtype) for s in shards]
    sems = [pltpu.SemaphoreType.DMA((n, 7)), pltpu.SemaphoreType.DMA((n, 7)), pltpu.SemaphoreType.DMA((n,))]
    return shards, shapes, sems, start, wait


def _gather_direct_exchange(shards):
    n = len(shards)

    def copies(x_refs, out_refs, sems):
        send_sems, recv_sems, local_sems = sems
        x, y, c = lax.axis_index("x"), lax.axis_index("y"), lax.axis_index("c")
        targets = [(x, y, 1 - c), (1 - x, y, c), (x, 1 - y, c), (1 - x, 1 - y, c)]
        local, sends, recvs = [], [], []
        for a in range(n):
            mine = out_refs[a].at[4 * x + 2 * y + c]
            local.append(pltpu.make_async_copy(x_refs[a], mine, local_sems.at[a]))
            for k, to in enumerate(targets):
                theirs = out_refs[a].at[4 * to[0] + 2 * to[1] + to[2]]
                sends.append(_remote(x_refs[a], mine, send_sems.at[a, k], recv_sems.at[a, k], to))
                recvs.append(_remote(theirs, theirs, send_sems.at[a, k], recv_sems.at[a, k], to))
        return local, sends, recvs

    def start(x_refs, out_refs, sems):
        local, sends, _ = copies(x_refs, out_refs, sems)
        for cp in local + sends:
            cp.start()

    def wait(x_refs, out_refs, sems):
        local, sends, recvs = copies(x_refs, out_refs, sems)
        for cp in recvs:
            cp.wait_recv()
        for cp in sends:
            cp.wait_send()
        for cp in local:
            cp.wait()

    shapes = [jax.ShapeDtypeStruct((N_DEV, *s.shape), s.dtype) for s in shards]
    sems = [pltpu.SemaphoreType.DMA((n, 4)), pltpu.SemaphoreType.DMA((n, 4)), pltpu.SemaphoreType.DMA((n,))]
    return shards, shapes, sems, start, wait


def _gather_forward(gathered):
    n = len(gathered)

    def body(*refs):
        out_refs = refs[n:2 * n]
        send_sems, recv_sems = refs[2 * n:]
        x, y, c = lax.axis_index("x"), lax.axis_index("y"), lax.axis_index("c")
        sibling = (x, y, 1 - c)
        sends, recvs = [], []
        for a in range(n):
            for j, (px, py) in enumerate([(1 - x, y), (x, 1 - y), (1 - x, 1 - y)]):
                mine = out_refs[a].at[4 * px + 2 * py + c]
                theirs = out_refs[a].at[4 * px + 2 * py + 1 - c]
                sends.append(_remote(mine, mine, send_sems.at[a, j], recv_sems.at[a, j], sibling))
                recvs.append(_remote(theirs, theirs, send_sems.at[a, j], recv_sems.at[a, j], sibling))
        for cp in sends:
            cp.start()
        for cp in recvs:
            cp.wait_recv()
        for cp in sends:
            cp.wait_send()

    return pl.pallas_call(
        body, name="ag_forward", out_shape=[jax.ShapeDtypeStruct(g.shape, g.dtype) for g in gathered],
        in_specs=[_ANY] * n, out_specs=[_ANY] * n, input_output_aliases={a: a for a in range(n)},
        scratch_shapes=[pltpu.SemaphoreType.DMA((n, 3)), pltpu.SemaphoreType.DMA((n, 3))],
    )(*gathered)


def _chips_exchange(hsums):
    n = len(hsums)

    def copies(h_refs, out_refs, sems):
        send_sems, recv_sems = sems
        x, y, c = lax.axis_index("x"), lax.axis_index("y"), lax.axis_index("c")
        chips = [(1 - x, y), (x, 1 - y), (1 - x, 1 - y)]
        return [_remote(h_refs[a].at[2 * px + py], out_refs[a].at[k], send_sems.at[a, k], recv_sems.at[a, k], (px, py, c))
                for a in range(n) for k, (px, py) in enumerate(chips)]

    def start(h_refs, out_refs, sems):
        for cp in copies(h_refs, out_refs, sems):
            cp.start()

    def wait(h_refs, out_refs, sems):
        for cp in copies(h_refs, out_refs, sems):
            cp.wait()

    shapes = [jax.ShapeDtypeStruct((3, *h.shape[1:]), h.dtype) for h in hsums]
    sems = [pltpu.SemaphoreType.DMA((n, 3)), pltpu.SemaphoreType.DMA((n, 3))]
    return hsums, shapes, sems, start, wait


def _sibling_exchange(halves):
    n = len(halves)

    def copies(p_refs, out_refs, sems):
        send_sems, recv_sems = sems
        x, y, c = lax.axis_index("x"), lax.axis_index("y"), lax.axis_index("c")
        return [_remote(p_refs[a], out_refs[a], send_sems.at[a], recv_sems.at[a], (x, y, 1 - c)) for a in range(n)]

    def start(p_refs, out_refs, sems):
        for cp in copies(p_refs, out_refs, sems):
            cp.start()

    def wait(p_refs, out_refs, sems):
        for cp in copies(p_refs, out_refs, sems):
            cp.wait()

    shapes = [jax.ShapeDtypeStruct(h.shape, h.dtype) for h in halves]
    return halves, shapes, [pltpu.SemaphoreType.DMA((n,)), pltpu.SemaphoreType.DMA((n,))], start, wait


_IN_RANGES = ((0, 7 * D, 0, 0), (7 * D, 7 * D + 16, 1, 0), (7 * D + 16, 9 * D + 16, 0, 7 * D))
_UP_RANGES = ((0, 2 * DFF, 0, 0),)


def _col_pieces(width, ranges):
    pieces = []
    for d in range(N_DEV):
        lo, hi = d * width, (d + 1) * width
        for glo, ghi, mat, mlo in ranges:
            a, b = max(lo, glo), min(hi, ghi)
            if a < b:
                pieces.append((d, a - lo, b - lo, mat, mlo + a - glo))
    return pieces


def _cols_to_matrices(g, ranges, out_widths, *, name):
    _, rows, width = g.shape
    tb = 128
    pieces = _col_pieces(width, ranges)
    covered = [sum(p[2] - p[1] for p in pieces if p[3] == m) for m in range(len(out_widths))]

    def body(g_ref, *o_refs):
        for m, o_ref in enumerate(o_refs):
            if covered[m] < out_widths[m]:
                o_ref[...] = jnp.zeros_like(o_ref)
        for d, b0, b1, m, m0 in pieces:
            o_refs[m][:, m0:m0 + b1 - b0] = g_ref[d, :, b0:b1]

    return pl.pallas_call(
        body, name=name, grid=(rows // tb,), in_specs=[pl.BlockSpec((N_DEV, tb, width), lambda i: (0, i, 0))],
        out_specs=[pl.BlockSpec((tb, wo), lambda i: (i, 0)) for wo in out_widths],
        out_shape=[jax.ShapeDtypeStruct((rows, wo), g.dtype) for wo in out_widths], compiler_params=_params(1),
    )(g)


def _matrices_to_cols(mats, ranges, width, *, name):
    rows = mats[0].shape[0]
    tb = 128
    pieces = _col_pieces(width, ranges)

    def body(*refs):
        m_refs, g_ref = refs[:-1], refs[-1]
        for d, b0, b1, m, m0 in pieces:
            g_ref[d, :, b0:b1] = m_refs[m][:, m0:m0 + b1 - b0]

    return pl.pallas_call(
        body, name=name, grid=(rows // tb,),
        in_specs=[pl.BlockSpec((tb, mt.shape[1]), lambda i: (i, 0)) for mt in mats],
        out_specs=pl.BlockSpec((N_DEV, tb, width), lambda i: (0, i, 0)),
        out_shape=jax.ShapeDtypeStruct((N_DEV, rows, width), mats[0].dtype), compiler_params=_params(1),
    )(*mats)


def _row_block(rows):
    return 128 if rows % 128 == 0 else rows


def _half_bf16(g4, c_other, *, name):
    _, _, rows, width = g4.shape
    tb = _row_block(rows)

    def body(c_ref, p_ref, o_ref):
        o_ref[0] = p_ref[0, 0].astype(bf16)

    grid_spec = pltpu.PrefetchScalarGridSpec(
        num_scalar_prefetch=1, grid=(4, rows // tb),
        in_specs=[pl.BlockSpec((1, 1, tb, width), lambda j, i, c_ref: (j, c_ref[0], i, 0))],
        out_specs=pl.BlockSpec((1, tb, width), lambda j, i, c_ref: (j, i, 0)))
    return pl.pallas_call(
        body, name=name, grid_spec=grid_spec, out_shape=jax.ShapeDtypeStruct((4, rows, width), bf16),
        compiler_params=_params(2),
    )(c_other, g4)


def _pair_sum(g4, recv, c_me, *, name):
    _, _, rows, width = g4.shape
    tb = _row_block(rows)

    def body(c_ref, p_ref, r_ref, o_ref, ob_ref):
        s = p_ref[0, 0] + r_ref[0].astype(f32)
        o_ref[0] = s
        ob_ref[0] = s.astype(bf16)

    blk = pl.BlockSpec((1, tb, width), lambda j, i, c_ref: (j, i, 0))
    grid_spec = pltpu.PrefetchScalarGridSpec(
        num_scalar_prefetch=1, grid=(4, rows // tb),
        in_specs=[pl.BlockSpec((1, 1, tb, width), lambda j, i, c_ref: (j, c_ref[0], i, 0)), blk],
        out_specs=[blk, blk])
    return pl.pallas_call(
        body, name=name, grid_spec=grid_spec,
        out_shape=[jax.ShapeDtypeStruct((4, rows, width), f32), jax.ShapeDtypeStruct((4, rows, width), bf16)],
        compiler_params=_params(2),
    )(c_me, g4, recv)


def _adam_shard(hsum, recv, chip, w, m, v, *, name):
    _, rows, width = w.shape
    tb = _row_block(rows)

    def body(j_ref, h_ref, r_ref, w_ref, m_ref, v_ref, g_out, d_out, m_out, v_out):
        g = ((h_ref[0] + r_ref[0].astype(f32)) + r_ref[1].astype(f32)) + r_ref[2].astype(f32)
        delta, mn, vn = _adam_math(w_ref[0], g, m_ref[0], v_ref[0])
        g_out[0] = g
        d_out[0] = delta
        m_out[0] = mn
        v_out[0] = vn

    blk = pl.BlockSpec((1, tb, width), lambda i, j_ref: (0, i, 0))
    grid_spec = pltpu.PrefetchScalarGridSpec(
        num_scalar_prefetch=1, grid=(rows // tb,),
        in_specs=[pl.BlockSpec((1, tb, width), lambda i, j_ref: (j_ref[0], i, 0)),
                  pl.BlockSpec((3, tb, width), lambda i, j_ref: (0, i, 0)), blk, blk, blk],
        out_specs=[blk, blk, blk, blk])
    return pl.pallas_call(
        body, name=name, grid_spec=grid_spec, out_shape=[jax.ShapeDtypeStruct(w.shape, f32)] * 4,
        compiler_params=_params(1),
    )(chip, hsum, recv, w, m, v)


R_SMALL = 16 + 16 * N_DEV
_SMALL_LANES = {"gdn_norm_g": (0, DH), "gdn_A_log": (DH, DH + H), "gdn_dt_bias": (2 * DH, 2 * DH + H)}
_LOSS_LANE = 3 * DH


def _pack_small(dg1, dg2, dg3, dgn, dal, ddt, loss_p, dwa, dwg, dwf):
    def body(dg1_ref, dg2_ref, dg3_ref, dgn_ref, dal_ref, ddt_ref, loss_ref, dwa_ref, dwg_ref, dwf_ref, o_ref):
        def total(ref):
            return jnp.sum(ref[...], axis=0, keepdims=True)

        o_ref[...] = jnp.zeros_like(o_ref)
        o_ref[0:1, :] = total(dg1_ref)
        o_ref[1:2, :] = total(dg2_ref)
        o_ref[2:3, :] = total(dg3_ref)
        o_ref[3:4, 0:DH] = total(dgn_ref)
        o_ref[3:4, DH:2 * DH] = total(dal_ref)
        o_ref[3:4, 2 * DH:3 * DH] = total(ddt_ref)
        o_ref[3:4, 3 * DH:4 * DH] = total(loss_ref)
        for d in range(N_DEV):
            base = 16 + 16 * d
            o_ref[base:base + 3, 0:128] = dwa_ref[0:3, 128 * d:128 * (d + 1)]
            o_ref[base:base + 4, 128:512] = dwg_ref[0:4, 384 * d:384 * (d + 1)]
            o_ref[base + 8:base + 11, 0:704] = dwf_ref[0:3, 704 * d:704 * (d + 1)]

    return pl.pallas_call(body, name="pack_small", out_shape=jax.ShapeDtypeStruct((R_SMALL, D), f32))(
        dg1, dg2, dg3, dgn, dal, ddt, loss_p, dwa, dwg, dwf)


_SMALL = ("norm_mix_g", "norm_ffn_g", "norm_final_g", "gdn_norm_g", "gdn_A_log", "gdn_dt_bias",
          "conv_a_w", "gdn_conv_w", "ffn_conv_w")


def _adam_small(gath, me, w, m, v):
    arrays = [t[n] for n in _SMALL for t in (w, m, v)]

    def body(me_ref, ga_ref, gb_ref, *refs):
        ins, outs = refs[:len(arrays)], refs[len(arrays):]
        ga, gb = ga_ref[0], gb_ref[0]
        for s in range(1, N_DEV):
            ga = ga + ga_ref[s]
            gb = gb + gb_ref[s]
        grads = {"norm_mix_g": ga[0:1, :], "norm_ffn_g": ga[1:2, :], "norm_final_g": ga[2:3, :],
                 "conv_a_w": gb[0:3, 0:128], "gdn_conv_w": gb[0:4, 128:512], "ffn_conv_w": gb[8:11, 0:704]}
        for n, (lo, hi) in _SMALL_LANES.items():
            grads[n] = ga[3:4, lo:hi]
        for i, n in enumerate(_SMALL):
            three_d = len(w[n].shape) == 3
            wv, mv, vv = (r[0] if three_d else r[...] for r in ins[3 * i:3 * i + 3])
            delta, mn, vn = _adam_math(wv, grads[n], mv, vv)
            for o_ref, val in zip(outs[4 * i:4 * i + 4], (grads[n], delta, mn, vn)):
                if three_d:
                    o_ref[0] = val
                else:
                    o_ref[...] = val
        outs[-1][...] = ga[3:4, _LOSS_LANE:_LOSS_LANE + 1]

    def whole(shape):
        return pl.BlockSpec(shape, lambda i, me_ref: (0,) * len(shape))

    grid_spec = pltpu.PrefetchScalarGridSpec(
        num_scalar_prefetch=1, grid=(1,),
        in_specs=[pl.BlockSpec((N_DEV, 16, D), lambda i, me_ref: (0, 0, 0)),
                  pl.BlockSpec((N_DEV, 16, D), lambda i, me_ref: (0, 1 + me_ref[0], 0))] + [whole(a.shape) for a in arrays],
        out_specs=[whole(w[n].shape) for n in _SMALL for _ in range(4)] + [whole((1, 1))])
    res = pl.pallas_call(
        body, name="adam_small", grid_spec=grid_spec,
        out_shape=[jax.ShapeDtypeStruct(w[n].shape, f32) for n in _SMALL for _ in range(4)]
        + [jax.ShapeDtypeStruct((1, 1), f32)],
        compiler_params=_params(1),
    )(me, gath, gath, *arrays)
    return {n: tuple(res[4 * i:4 * i + 4]) for i, n in enumerate(_SMALL)}, res[-1]


def _adam_math(w, g, m, v):
    m = ADAM_B1 * m + (1.0 - ADAM_B1) * g
    v = ADAM_B2 * v + (1.0 - ADAM_B2) * jnp.square(g)
    m_hat = m / (1.0 - ADAM_B1 ** ADAM_STEP)
    v_hat = v / (1.0 - ADAM_B2 ** ADAM_STEP)
    delta = -ADAM_LR * (m_hat / (jnp.sqrt(v_hat) + ADAM_EPS) + ADAM_WD * w)
    return delta, m, v


_WEIGHTS = ("norm_mix_g", "w_in", "conv_a_w", "gdn_conv_w", "gdn_A_log", "gdn_dt_bias", "gdn_norm_g", "w_a_out",
            "w_b_out", "w_o", "norm_ffn_g", "w_up", "ffn_conv_w", "w_down", "norm_final_g")
_BIG = ("w_in",) + _REST
_CONVS = ("conv_a_w", "gdn_conv_w", "ffn_conv_w")


class _StepExchanges:
    def __init__(self, wts, mom, var, c_me, chip):
        self.wts, self.mom, self.var, self.c_me, self.chip = wts, mom, var, c_me, chip
        self.results = {}

    def gather_first(self):
        return _gather_exchange([self.wts["w_in"][0].astype(bf16)] + [self.wts[n][0] for n in _CONVS])

    def finish_first(self, gathered):
        g_in, gc_a, gc_g, gc_f = gathered
        w1, w2 = _cols_to_matrices(g_in, _IN_RANGES, (NW1, 128), name="relay_w_in")
        return {"w1": w1, "w2": w2, "conv_a_w": gc_a.transpose(1, 0, 2).reshape(3, D),
                "gdn_conv_w": gc_g.transpose(1, 0, 2).reshape(4, 3 * D),
                "ffn_conv_w": gc_f.transpose(1, 0, 2).reshape(3, 2 * DFF)}

    def gather_rest(self):
        return _gather_direct_exchange([self.wts[n][0].astype(bf16) for n in _REST])

    def finish_gather(self, gathered):
        g_up, g_a, g_b, g_o, g_down = _gather_forward(gathered)
        (w_up,) = _cols_to_matrices(g_up, _UP_RANGES, (2 * DFF,), name="relay_w_up")
        return {"w_up": w_up, "w_a_out": g_a.reshape(D, D), "w_b_out": g_b.reshape(D, D), "w_o": g_o.reshape(D, D),
                "w_down": g_down.reshape(DFF, D)}

    def reduce_halves(self, names, grads):
        blocks = []
        for n in names:
            if n == "w_in":
                g = _matrices_to_cols([grads["w1"], grads["w2"]], _IN_RANGES, R_IN, name="relay_dw_in")
            elif n == "w_up":
                g = _matrices_to_cols([grads[n]], _UP_RANGES, R_UP, name="relay_dw_up")
            else:
                g = grads[n]
            blocks.append(g.reshape(4, 2, *self.wts[n].shape[1:]))
        return _sibling_exchange([_half_bf16(g, 1 - self.c_me, name="rs_half_" + n) for n, g in zip(names, blocks)]), blocks

    def reduce_sums(self, names, blocks, recv):
        sums = [_pair_sum(g, r, self.c_me, name="rs_sum_" + n) for n, g, r in zip(names, blocks, recv)]
        return _chips_exchange([s[1] for s in sums]), [s[0] for s in sums]

    def finish_reduce(self, names, sums, recv):
        for n, s, r in zip(names, sums, recv):
            self.results[n] = _adam_shard(s, r, self.chip, self.wts[n], self.mom[n], self.var[n], name="adam_" + n)


def kernel(x, norm_mix_g, w_in, conv_a_w, gdn_conv_w, gdn_A_log, gdn_dt_bias, gdn_norm_g, w_a_out, w_b_out, w_o, norm_ffn_g, w_up, ffn_conv_w, w_down, norm_final_g, loss_target, m_norm_mix_g, m_w_in, m_conv_a_w, m_gdn_conv_w, m_gdn_A_log, m_gdn_dt_bias, m_gdn_norm_g, m_w_a_out, m_w_b_out, m_w_o, m_norm_ffn_g, m_w_up, m_ffn_conv_w, m_w_down, m_norm_final_g, v_norm_mix_g, v_w_in, v_conv_a_w, v_gdn_conv_w, v_gdn_A_log, v_gdn_dt_bias, v_gdn_norm_g, v_w_a_out, v_w_b_out, v_w_o, v_norm_ffn_g, v_w_up, v_ffn_conv_w, v_w_down, v_norm_final_g):
    wts = dict(zip(_WEIGHTS, (norm_mix_g, w_in, conv_a_w, gdn_conv_w, gdn_A_log, gdn_dt_bias, gdn_norm_g, w_a_out,
                              w_b_out, w_o, norm_ffn_g, w_up, ffn_conv_w, w_down, norm_final_g)))
    mom = dict(zip(_WEIGHTS, (m_norm_mix_g, m_w_in, m_conv_a_w, m_gdn_conv_w, m_gdn_A_log, m_gdn_dt_bias,
                              m_gdn_norm_g, m_w_a_out, m_w_b_out, m_w_o, m_norm_ffn_g, m_w_up, m_ffn_conv_w,
                              m_w_down, m_norm_final_g)))
    var = dict(zip(_WEIGHTS, (v_norm_mix_g, v_w_in, v_conv_a_w, v_gdn_conv_w, v_gdn_A_log, v_gdn_dt_bias,
                              v_gdn_norm_g, v_w_a_out, v_w_b_out, v_w_o, v_norm_ffn_g, v_w_up, v_ffn_conv_w,
                              v_w_down, v_norm_final_g)))
    cx, cy, cc = lax.axis_index("x"), lax.axis_index("y"), lax.axis_index("c")
    c_me = jnp.reshape(cc, (1,)).astype(jnp.int32)
    chip = jnp.reshape(2 * cx + cy, (1,)).astype(jnp.int32)
    me = jnp.reshape(4 * cx + 2 * cy + cc, (1,)).astype(jnp.int32)

    comm = _StepExchanges(wts, mom, var, c_me, chip)
    replicated = {n: wts[n] for n in ("norm_mix_g", "norm_ffn_g", "norm_final_g", "gdn_norm_g", "gdn_A_log", "gdn_dt_bias")}
    loss_p, dx, grads = _local_step(x[0], loss_target[0], replicated, comm)
    res = comm.results

    small = _pack_small(grads["norm_mix_g"], grads["norm_ffn_g"], grads["norm_final_g"], grads["gdn_norm_g"],
                        grads["gdn_A_log"], grads["gdn_dt_bias"], loss_p, grads["conv_a_w"], grads["gdn_conv_w"],
                        grads["ffn_conv_w"])
    (small_all,) = _run_exchange(_gather_exchange([small]), name="ag_small")

    def raw(t):
        return {n: t[n].reshape(1, D) if n == "norm_final_g" else t[n] for n in _SMALL}

    res_small, loss = _adam_small(small_all, me, raw(wts), raw(mom), raw(var))
    for n in _SMALL:
        res[n] = tuple(a.reshape(wts[n].shape) for a in res_small[n])
    outs = [[res[n][i] for n in _WEIGHTS] for i in range(4)]
    return (loss.reshape(()), dx[None], *outs[0], *outs[1], *outs[2], *outs[3])
```

```python
import jax
import jax.numpy as jnp
from jax import lax
from jax.experimental import pallas as pl
from jax.experimental.pallas import tpu as pltpu

f32 = jnp.float32
bf16 = jnp.bfloat16

D = 1024
H = 8
DH = 128
CH = 64
GDN_STEP = 2
DFF = 2816
NW1 = 9216
EPS = 1e-6
N_DEV = 8

ADAM_LR = 0.001
ADAM_B1 = 0.9
ADAM_B2 = 0.999
ADAM_EPS = 1e-08
ADAM_WD = 0.01
ADAM_STEP = 10

VMEM_LIMIT_BYTES = 48 * 1024 * 1024

R_IN, R_UP = 1154, 704

_HI = lax.Precision.HIGHEST
MESH = pl.DeviceIdType.MESH


def _params(n_grid):
    return pltpu.CompilerParams(dimension_semantics=("arbitrary",) * n_grid, vmem_limit_bytes=VMEM_LIMIT_BYTES)


def _bdot(a, b):
    return jnp.dot(a.astype(bf16), b.astype(bf16), preferred_element_type=f32)


def _bdot_nt(a, b):
    return lax.dot_general(a.astype(bf16), b.astype(bf16), (((1,), (1,)), ((), ())), preferred_element_type=f32)


def _bdot_tn(a, b):
    return lax.dot_general(a.astype(bf16), b.astype(bf16), (((0,), (0,)), ((), ())), preferred_element_type=f32)


def _hdot(a, b):
    return jnp.dot(a, b, preferred_element_type=f32, precision=_HI)


def _idot(a, b):
    return jnp.dot(a, b, preferred_element_type=f32, precision=lax.Precision.HIGH)


def _sigmoid(x):
    return 1.0 / (1.0 + jnp.exp(-x))


def _softplus(x):
    return jnp.maximum(x, 0.0) + jnp.log(1.0 + jnp.exp(-jnp.abs(x)))


def _shift_down(x, halo, j):
    if j == 0:
        return x
    xr = pltpu.roll(x, j, 0)
    hr = pltpu.roll(halo, j, 0)
    r8 = lax.broadcasted_iota(jnp.int32, hr.shape, 0)
    top = jnp.where(r8 < j, hr, xr[:8])
    return jnp.concatenate([top, xr[8:]], axis=0)


def _shift_up(x, halo, j):
    if j == 0:
        return x
    n = x.shape[0]
    xr = pltpu.roll(x, n - j, 0)
    hr = pltpu.roll(halo, 8 - j, 0)
    r8 = lax.broadcasted_iota(jnp.int32, hr.shape, 0)
    bot = jnp.where(r8 >= 8 - j, hr, xr[n - 8:])
    return jnp.concatenate([xr[:n - 8], bot], axis=0)


def _taps_down(x, halo, k):
    return [_shift_down(x, halo, k - 1 - j) for j in range(k)]


def _strip(i, base=0):
    return slice(base + i * 128, base + (i + 1) * 128)


def _strip_taps(x, halo, first, k):
    return _taps_down(x, jnp.where(first, 0.0, halo), k)


def _strip_conv(w_ref, sl, taps):
    out = w_ref[0:1, sl] * taps[0]
    for j in range(1, len(taps)):
        out = out + w_ref[j:j + 1, sl] * taps[j]
    return out


def _strip_weight_grad(dw_ref, sl, dy, taps):
    for j, tap in enumerate(taps):
        dw_ref[j:j + 1, sl] += jnp.sum(dy * tap, axis=0, keepdims=True)


def _strip_conv_up(dy, halo, last, w_ref, sl, k):
    halo = jnp.where(last, 0.0, halo)
    out = w_ref[k - 1:k, sl] * dy
    for j in range(k - 1):
        out = out + w_ref[j:j + 1, sl] * _shift_up(dy, halo, k - 1 - j)
    return out


def _row(tb, w, col=0):
    return pl.BlockSpec((tb, w), lambda i: (i, col))


def _prev(tb, w, col=0):
    return pl.BlockSpec((8, w), lambda i: (jnp.maximum(i * (tb // 8) - 1, 0), col))


def _next(tb, w, n_rows, col=0):
    last = n_rows // 8 - 1
    return pl.BlockSpec((8, w), lambda i: (jnp.minimum((i + 1) * (tb // 8), last), col))


def _fixed(shape):
    return pl.BlockSpec(shape, lambda i: (0,) * len(shape))


def _pick(n, prefs):
    for p in prefs:
        if n % p == 0:
            return p
    return n


def _matmul(a, b, *, name, nt=False, add=None, tm=1024, tn=1024, tk=None, exchange=None):
    m, kd = a.shape
    n = b.shape[0] if nt else b.shape[1]
    tm = _pick(m, (tm, 512, 256))
    tn = _pick(n, (tn, 1024, 512, 128))
    tk = kd if tk is None else tk
    nk = kd // tk
    dims = (((1,), (1,)), ((), ())) if nt else (((1,), (0,)), ((), ()))

    def body(a_ref, b_ref, *rest):
        o_ref = rest[-1]
        part = lax.dot_general(a_ref[...], b_ref[...], dims, preferred_element_type=f32)
        if nk == 1:
            o_ref[...] = part if add is None else part + rest[0][...]
            return
        k = pl.program_id(2)

        @pl.when(k == 0)
        def _():
            o_ref[...] = part if add is None else part + rest[0][...]

        @pl.when(k > 0)
        def _():
            o_ref[...] += part

    b_spec = pl.BlockSpec((tn, tk), lambda i, j, k: (j, k)) if nt else pl.BlockSpec((tk, tn), lambda i, j, k: (k, j))
    in_specs = [pl.BlockSpec((tm, tk), lambda i, j, k: (i, k)), b_spec]
    args = [a, b]
    if add is not None:
        in_specs.append(pl.BlockSpec((tm, tn), lambda i, j, k: (i, j)))
        args.append(add)
    return _call_with_exchange(
        body, exchange, name=name, grid=(m // tm, n // tn, nk), in_specs=in_specs,
        out_specs=pl.BlockSpec((tm, tn), lambda i, j, k: (i, j)),
        out_shape=jax.ShapeDtypeStruct((m, n), f32), args=args)


def _call_with_exchange(body, exchange, *, name, grid, in_specs, out_specs, out_shape, args):
    if exchange is None:
        return pl.pallas_call(body, name=name, grid=grid, in_specs=in_specs, out_specs=out_specs, out_shape=out_shape,
                              compiler_params=_params(len(grid)))(*args)
    x_arrays, x_shapes, x_sems, start, wait = exchange
    n_in, n_xin, n_xout = len(args), len(x_arrays), len(x_shapes)

    def full_body(*refs):
        c_in, x_in = refs[:n_in], refs[n_in:n_in + n_xin]
        c_out = refs[n_in + n_xin]
        x_out = refs[n_in + n_xin + 1:n_in + n_xin + 1 + n_xout]
        sems = refs[n_in + n_xin + 1 + n_xout:]
        ids = [pl.program_id(d) for d in range(len(grid))]
        first, last = ids[0] == 0, ids[0] == grid[0] - 1
        for d in range(1, len(grid)):
            first = first & (ids[d] == 0)
            last = last & (ids[d] == grid[d] - 1)

        @pl.when(first)
        def _():
            start(x_in, x_out, sems)

        body(*c_in, c_out)

        @pl.when(last)
        def _():
            wait(x_in, x_out, sems)

    res = pl.pallas_call(
        full_body, name=name, grid=grid, in_specs=list(in_specs) + [_ANY] * n_xin,
        out_specs=[out_specs] + [_ANY] * n_xout, out_shape=[out_shape] + list(x_shapes),
        scratch_shapes=list(x_sems), compiler_params=_params(len(grid)),
    )(*args, *x_arrays)
    return res[0], list(res[1:])


def _matmul_tn(a, b, *, name, tm=1024, tn=1024, exchange=None):
    t, m = a.shape
    _, n = b.shape
    tm = _pick(m, (tm, 1024, 512, 128))
    tn = _pick(n, (tn, 1024, 512, 128))
    tt = _pick(t, (2048, 1024, 512, 256))
    nt = t // tt

    def body(a_ref, b_ref, o_ref):
        k = pl.program_id(2)
        part = lax.dot_general(a_ref[...], b_ref[...], (((0,), (0,)), ((), ())), preferred_element_type=f32)

        @pl.when(k == 0)
        def _():
            o_ref[...] = part

        @pl.when(k > 0)
        def _():
            o_ref[...] += part

    return _call_with_exchange(
        body, exchange, name=name, grid=(m // tm, n // tn, nt),
        in_specs=[pl.BlockSpec((tt, tm), lambda i, j, k: (k, i)), pl.BlockSpec((tt, tn), lambda i, j, k: (k, j))],
        out_specs=pl.BlockSpec((tm, tn), lambda i, j, k: (i, j)),
        out_shape=jax.ShapeDtypeStruct((m, n), f32), args=[a, b])


def _rms_fwd(x, g, *, name, exchange=None):
    t = x.shape[0]
    tb = _pick(t, (256, 128))

    def body(x_ref, g_ref, h_ref):
        xv = x_ref[...]
        r = lax.rsqrt(jnp.mean(xv * xv, axis=-1, keepdims=True) + EPS)
        h_ref[...] = (xv * r * g_ref[...]).astype(bf16)

    return _call_with_exchange(
        body, exchange, name=name, grid=(t // tb,), in_specs=[_row(tb, D), _fixed((1, D))], out_specs=_row(tb, D),
        out_shape=jax.ShapeDtypeStruct((t, D), bf16), args=[x, g])


def _rms_bwd(dh, x, g, dres, *, name):
    t = x.shape[0]
    tb = _pick(t, (256, 128))

    def body(dh_ref, x_ref, g_ref, dres_ref, dx_ref, dxb_ref, dg_ref):
        xv = x_ref[...]
        r = lax.rsqrt(jnp.mean(xv * xv, axis=-1, keepdims=True) + EPS)
        xh = xv * r
        dy = dh_ref[...]
        dyg = dy * g_ref[...]
        dx = dres_ref[...] + r * (dyg - xh * jnp.mean(dyg * xh, axis=-1, keepdims=True))
        dx_ref[...] = dx
        dxb_ref[...] = dx.astype(bf16)

        @pl.when(pl.program_id(0) == 0)
        def _():
            dg_ref[...] = jnp.zeros_like(dg_ref)

        dg_ref[...] += jnp.sum((dy * xh).reshape(tb // 8, 8, D), axis=0)

    return pl.pallas_call(
        body, name=name, grid=(t // tb,),
        in_specs=[_row(tb, D), _row(tb, D), _fixed((1, D)), _row(tb, D)],
        out_specs=[_row(tb, D), _row(tb, D), _fixed((8, D))],
        out_shape=[jax.ShapeDtypeStruct((t, D), f32), jax.ShapeDtypeStruct((t, D), bf16),
                   jax.ShapeDtypeStruct((8, D), f32)],
        compiler_params=_params(1),
    )(dh, x, g, dres)


def _gdn_gates(ab, alog, dtb):
    lane = lax.broadcasted_iota(jnp.int32, ab.shape, 1)
    g = -jnp.exp(alog) * _softplus(ab + dtb)
    beta = _sigmoid(ab)
    return jnp.where(lane < H, g, jnp.where(lane < 2 * H, beta, 0.0))


def _pre_fwd(p1, p2, wa, wg, alog, dtb):
    t = p1.shape[0]
    tb = 128

    def body(p0_ref, p0h_ref, pq_ref, pqh_ref, p2_ref, wa_ref, wg_ref, alog_ref, dtb_ref,
             ya_ref, qn_ref, kn_ref, vc_ref, gb_ref):
        first = pl.program_id(0) == 0
        for i in range(D // 128):
            sl, cg, xv = _strip(i), _strip(i, D), _strip(i, 2 * D)
            taps = _strip_taps(p0_ref[:, cg] * p0_ref[:, xv], p0h_ref[:, cg] * p0h_ref[:, xv], first, 3)
            ya_ref[:, sl] = (p0_ref[:, sl] * _strip_conv(wa_ref, sl, taps)).astype(bf16)
        for part, out_ref, scale in ((0, qn_ref, DH ** -0.5), (1, kn_ref, 1.0), (2, vc_ref, None)):
            for h in range(H):
                sl = _strip(h, part * D)
                s = _strip_conv(wg_ref, sl, _strip_taps(pq_ref[:, sl], pqh_ref[:, sl], first, 4))
                s = s * _sigmoid(s)
                if scale is not None:
                    s = s * (lax.rsqrt(jnp.sum(s * s, axis=-1, keepdims=True) + EPS) * scale)
                out_ref[:, _strip(h)] = s
        gb_ref[...] = _gdn_gates(p2_ref[...], alog_ref[...], dtb_ref[...])

    return pl.pallas_call(
        body, name="pre_fwd", grid=(t // tb,),
        in_specs=[_row(tb, 3 * D, 0), _prev(tb, 3 * D, 0), _row(tb, 3 * D, 1), _prev(tb, 3 * D, 1), _row(tb, 128),
                  _fixed((8, D)), _fixed((8, 3 * D)), _fixed((1, 128)), _fixed((1, 128))],
        out_specs=[_row(tb, D), _row(tb, D), _row(tb, D), _row(tb, D), _row(tb, 128)],
        out_shape=[jax.ShapeDtypeStruct((t, D), bf16), jax.ShapeDtypeStruct((t, D), f32),
                   jax.ShapeDtypeStruct((t, D), f32), jax.ShapeDtypeStruct((t, D), f32),
                   jax.ShapeDtypeStruct((t, 128), f32)],
        compiler_params=_params(1),
    )(p1, p1, p1, p1, p2, wa, wg, alog, dtb)


def _post_fwd(o, p1, gn):
    t = o.shape[0]
    tb = _pick(t, (256, 128))

    def body(o_ref, z_ref, gn_ref, yb_ref):
        for h in range(H):
            sl = slice(h * DH, (h + 1) * DH)
            oh = o_ref[:, sl]
            z = z_ref[:, sl]
            r = lax.rsqrt(jnp.mean(oh * oh, axis=-1, keepdims=True) + EPS)
            yb_ref[:, sl] = (oh * r * gn_ref[...] * (z * _sigmoid(z))).astype(bf16)

    return pl.pallas_call(
        body, name="post_fwd", grid=(t // tb,), in_specs=[_row(tb, D), _row(tb, D, 6), _fixed((1, DH))],
        out_specs=_row(tb, D), out_shape=jax.ShapeDtypeStruct((t, D), bf16), compiler_params=_params(1),
    )(o, p1, gn)


def _post_bwd(dyb, o, p1, gn):
    t = o.shape[0]
    tb = _pick(t, (256, 128))

    def body(dyb_ref, o_ref, z_ref, gn_ref, do_ref, dz_ref, dgn_ref):
        @pl.when(pl.program_id(0) == 0)
        def _():
            dgn_ref[...] = jnp.zeros_like(dgn_ref)

        gn_v = gn_ref[...]
        acc = jnp.zeros((8, DH), f32)
        for h in range(H):
            sl = slice(h * DH, (h + 1) * DH)
            oh = o_ref[:, sl]
            z = z_ref[:, sl]
            dy = dyb_ref[:, sl]
            r = lax.rsqrt(jnp.mean(oh * oh, axis=-1, keepdims=True) + EPS)
            on = oh * r
            sg = _sigmoid(z)
            sz = z * sg
            don = dy * sz
            dz_ref[:, sl] = (dy * on * gn_v * (sg * (1.0 + z * (1.0 - sg)))).astype(bf16)
            acc = acc + jnp.sum((don * on).reshape(tb // 8, 8, DH), axis=0)
            doh = don * gn_v
            do_ref[:, sl] = r * (doh - on * jnp.mean(doh * on, axis=-1, keepdims=True))
        dgn_ref[...] += acc

    return pl.pallas_call(
        body, name="post_bwd", grid=(t // tb,),
        in_specs=[_row(tb, D), _row(tb, D), _row(tb, D, 6), _fixed((1, DH))],
        out_specs=[_row(tb, D), _row(tb, D), _fixed((8, DH))],
        out_shape=[jax.ShapeDtypeStruct((t, D), f32), jax.ShapeDtypeStruct((t, D), bf16),
                   jax.ShapeDtypeStruct((8, DH), f32)],
        compiler_params=_params(1),
    )(dyb, o, p1, gn)


def _mix_fwd(ya, yb, p1):
    t = ya.shape[0]
    tb = _pick(t, (256, 128))

    def body(ya_ref, yb_ref, ga_ref, gb_ref, mix_ref):
        mix_ref[...] = (_sigmoid(ga_ref[...]) * ya_ref[...] + _sigmoid(gb_ref[...]) * yb_ref[...]).astype(bf16)

    return pl.pallas_call(
        body, name="mix_fwd", grid=(t // tb,), in_specs=[_row(tb, D), _row(tb, D), _row(tb, D, 7), _row(tb, D, 8)],
        out_specs=_row(tb, D), out_shape=jax.ShapeDtypeStruct((t, D), bf16), compiler_params=_params(1),
    )(ya, yb, p1, p1)


def _mix_bwd(dmix, ya, yb, p1):
    t = ya.shape[0]
    tb = _pick(t, (256, 128))

    def body(dm_ref, ya_ref, yb_ref, ga_ref, gb_ref, dya_ref, dyb_ref, dg_ref):
        dm = dm_ref[...]
        sa = _sigmoid(ga_ref[...])
        sb = _sigmoid(gb_ref[...])
        dya_ref[...] = (dm * sa).astype(bf16)
        dyb_ref[...] = (dm * sb).astype(bf16)
        dg_ref[:, :D] = (dm * ya_ref[...] * sa * (1.0 - sa)).astype(bf16)
        dg_ref[:, D:] = (dm * yb_ref[...] * sb * (1.0 - sb)).astype(bf16)

    return pl.pallas_call(
        body, name="mix_bwd", grid=(t // tb,),
        in_specs=[_row(tb, D), _row(tb, D), _row(tb, D), _row(tb, D, 7), _row(tb, D, 8)],
        out_specs=[_row(tb, D), _row(tb, D), _row(tb, 2 * D)],
        out_shape=[jax.ShapeDtypeStruct((t, D), bf16), jax.ShapeDtypeStruct((t, D), bf16),
                   jax.ShapeDtypeStruct((t, 2 * D), bf16)],
        compiler_params=_params(1),
    )(dmix, ya, yb, p1, p1)


def _ffn_fwd(up, wf):
    t = up.shape[0]
    tb = 128

    def body(up_ref, uph_ref, wf_ref, act_ref):
        first = pl.program_id(0) == 0
        for i in range(DFF // 128):
            g, v = _strip(i), _strip(i, DFF)
            gate = _strip_conv(wf_ref, g, _strip_taps(up_ref[:, g], uph_ref[:, g], first, 3))
            val = _strip_conv(wf_ref, v, _strip_taps(up_ref[:, v], uph_ref[:, v], first, 3))
            act_ref[:, g] = (gate * _sigmoid(gate) * val).astype(bf16)

    return pl.pallas_call(
        body, name="ffn_fwd", grid=(t // tb,), in_specs=[_row(tb, 2 * DFF), _prev(tb, 2 * DFF), _fixed((8, 2 * DFF))],
        out_specs=_row(tb, DFF), out_shape=jax.ShapeDtypeStruct((t, DFF), bf16), compiler_params=_params(1),
    )(up, up, wf)


def _ffn_bwd1(dact, up, wf):
    t = up.shape[0]
    tb = 128

    def body(da_ref, up_ref, uph_ref, wf_ref, dc_ref, dw_ref):
        @pl.when(pl.program_id(0) == 0)
        def _():
            dw_ref[...] = jnp.zeros_like(dw_ref)

        first = pl.program_id(0) == 0
        for i in range(DFF // 128):
            g, v = _strip(i), _strip(i, DFF)
            g_taps = _strip_taps(up_ref[:, g], uph_ref[:, g], first, 3)
            v_taps = _strip_taps(up_ref[:, v], uph_ref[:, v], first, 3)
            gate = _strip_conv(wf_ref, g, g_taps)
            val = _strip_conv(wf_ref, v, v_taps)
            sg = _sigmoid(gate)
            da = da_ref[:, g]
            dgate = da * val * (sg * (1.0 + gate * (1.0 - sg)))
            dval = da * (gate * sg)
            dc_ref[:, g] = dgate
            dc_ref[:, v] = dval
            _strip_weight_grad(dw_ref, g, dgate, g_taps)
            _strip_weight_grad(dw_ref, v, dval, v_taps)

    return pl.pallas_call(
        body, name="ffn_bwd1", grid=(t // tb,),
        in_specs=[_row(tb, DFF), _row(tb, 2 * DFF), _prev(tb, 2 * DFF), _fixed((8, 2 * DFF))],
        out_specs=[_row(tb, 2 * DFF), _fixed((8, 2 * DFF))],
        out_shape=[jax.ShapeDtypeStruct((t, 2 * DFF), f32), jax.ShapeDtypeStruct((8, 2 * DFF), f32)],
        compiler_params=_params(1),
    )(dact, up, up, wf)


def _ffn_bwd2(dc, wf):
    t = dc.shape[0]
    tb = 128
    nb = t // tb

    def body(dc_ref, dch_ref, wf_ref, dup_ref):
        last = pl.program_id(0) == nb - 1
        for i in range(2 * DFF // 128):
            sl = _strip(i)
            dup_ref[:, sl] = _strip_conv_up(dc_ref[:, sl], dch_ref[:, sl], last, wf_ref, sl, 3).astype(bf16)

    return pl.pallas_call(
        body, name="ffn_bwd2", grid=(nb,), in_specs=[_row(tb, 2 * DFF), _next(tb, 2 * DFF, t), _fixed((8, 2 * DFF))],
        out_specs=_row(tb, 2 * DFF), out_shape=jax.ShapeDtypeStruct((t, 2 * DFF), bf16), compiler_params=_params(1),
    )(dc, dc, wf)


def _final(x3, tgt, g):
    t = x3.shape[0]
    tb = _pick(t, (256, 128))

    def body(x_ref, t_ref, g_ref, loss_ref, dx_ref, dxb_ref, dg_ref):
        @pl.when(pl.program_id(0) == 0)
        def _():
            loss_ref[...] = jnp.zeros_like(loss_ref)
            dg_ref[...] = jnp.zeros_like(dg_ref)

        xv = x_ref[...]
        r = lax.rsqrt(jnp.mean(xv * xv, axis=-1, keepdims=True) + EPS)
        xh = xv * r
        gv = g_ref[...]
        e = xh * gv - t_ref[...]
        lrow = 0.5 * jnp.mean(e * e, axis=-1, keepdims=True)
        loss_ref[...] += jnp.sum(jnp.broadcast_to(lrow, (tb, 128)).reshape(tb // 8, 8, 128), axis=0)
        dy = e * (1.0 / D)
        dyg = dy * gv
        dx = r * (dyg - xh * jnp.mean(dyg * xh, axis=-1, keepdims=True))
        dx_ref[...] = dx
        dxb_ref[...] = dx.astype(bf16)
        dg_ref[...] += jnp.sum((dy * xh).reshape(tb // 8, 8, D), axis=0)

    return pl.pallas_call(
        body, name="final", grid=(t // tb,), in_specs=[_row(tb, D), _row(tb, D), _fixed((1, D))],
        out_specs=[_fixed((8, 128)), _row(tb, D), _row(tb, D), _fixed((8, D))],
        out_shape=[jax.ShapeDtypeStruct((8, 128), f32), jax.ShapeDtypeStruct((t, D), f32),
                   jax.ShapeDtypeStruct((t, D), bf16), jax.ShapeDtypeStruct((8, D), f32)],
        compiler_params=_params(1),
    )(x3, tgt, g)


def _pre_bwd1(p1, p2, dya_in, dqn, dkn, dvc, dgb, gbeta, wa, wg, alog, dtb):
    t = p1.shape[0]
    tb = 128

    def body(p0_ref, p0h_ref, pq_ref, pqh_ref, p2_ref, dya_ref, dqn_ref, dkn_ref, dvc_ref, dgb_ref, gb_ref,
             wa_ref, wg_ref, alog_ref, dtb_ref,
             dbg_ref, dca_ref, dc4_ref, dp2_ref, dwa_ref, dwg_ref, dal_ref, ddt_ref):
        @pl.when(pl.program_id(0) == 0)
        def _():
            dwa_ref[...] = jnp.zeros_like(dwa_ref)
            dwg_ref[...] = jnp.zeros_like(dwg_ref)
            dal_ref[...] = jnp.zeros_like(dal_ref)
            ddt_ref[...] = jnp.zeros_like(ddt_ref)

        first = pl.program_id(0) == 0

        for i in range(D // 128):
            sl, cg, xv = _strip(i), _strip(i, D), _strip(i, 2 * D)
            taps = _strip_taps(p0_ref[:, cg] * p0_ref[:, xv], p0h_ref[:, cg] * p0h_ref[:, xv], first, 3)
            dya = dya_ref[:, sl]
            dbg_ref[:, sl] = (dya * _strip_conv(wa_ref, sl, taps)).astype(bf16)
            dca = dya * p0_ref[:, sl]
            dca_ref[:, sl] = dca
            _strip_weight_grad(dwa_ref, sl, dca, taps)

        for part, d_ref, scale in ((0, dqn_ref, DH ** -0.5), (1, dkn_ref, 1.0), (2, dvc_ref, None)):
            for h in range(H):
                sl = _strip(h, part * D)
                taps = _strip_taps(pq_ref[:, sl], pqh_ref[:, sl], first, 4)
                c4 = _strip_conv(wg_ref, sl, taps)
                sg = _sigmoid(c4)
                dn = d_ref[:, _strip(h)]
                if scale is not None:
                    a = c4 * sg
                    r = lax.rsqrt(jnp.sum(a * a, axis=-1, keepdims=True) + EPS)
                    an = a * r
                    dn = dn * scale
                    dn = r * (dn - an * jnp.sum(dn * an, axis=-1, keepdims=True))
                dc4 = dn * (sg * (1.0 + c4 * (1.0 - sg)))
                dc4_ref[:, sl] = dc4
                _strip_weight_grad(dwg_ref, sl, dc4, taps)

        ab = p2_ref[...]
        lane = lax.broadcasted_iota(jnp.int32, ab.shape, 1)
        dgbv = dgb_ref[...]
        gbv = gb_ref[...]
        da = dgbv * (-jnp.exp(alog_ref[...])) * _sigmoid(ab + dtb_ref[...])
        db = dgbv * gbv * (1.0 - gbv)
        dp2_ref[...] = jnp.where(lane < H, da, jnp.where(lane < 2 * H, db, 0.0)).astype(bf16)
        dal = jnp.where(lane < H, dgbv * gbv, 0.0)
        ddt = jnp.where(lane < H, da, 0.0)
        dal_ref[...] += jnp.sum(dal.reshape(tb // 8, 8, 128), axis=0)
        ddt_ref[...] += jnp.sum(ddt.reshape(tb // 8, 8, 128), axis=0)

    return pl.pallas_call(
        body, name="pre_bwd1", grid=(t // tb,),
        in_specs=[_row(tb, 3 * D, 0), _prev(tb, 3 * D, 0), _row(tb, 3 * D, 1), _prev(tb, 3 * D, 1), _row(tb, 128),
                  _row(tb, D), _row(tb, D), _row(tb, D), _row(tb, D), _row(tb, 128), _row(tb, 128),
                  _fixed((8, D)), _fixed((8, 3 * D)), _fixed((1, 128)), _fixed((1, 128))],
        out_specs=[_row(tb, D), _row(tb, D), _row(tb, 3 * D), _row(tb, 128),
                   _fixed((8, D)), _fixed((8, 3 * D)), _fixed((8, 128)), _fixed((8, 128))],
        out_shape=[jax.ShapeDtypeStruct((t, D), bf16), jax.ShapeDtypeStruct((t, D), f32),
                   jax.ShapeDtypeStruct((t, 3 * D), f32), jax.ShapeDtypeStruct((t, 128), bf16),
                   jax.ShapeDtypeStruct((8, D), f32), jax.ShapeDtypeStruct((8, 3 * D), f32),
                   jax.ShapeDtypeStruct((8, 128), f32), jax.ShapeDtypeStruct((8, 128), f32)],
        compiler_params=_params(1),
    )(p1, p1, p1, p1, p2, dya_in, dqn, dkn, dvc, dgb, gbeta, wa, wg, alog, dtb)


def _pre_bwd2(dca, dc4, p1, dbg, dz, dgates, wa, wg, exchange=None):
    t = p1.shape[0]
    tb = 128
    nb = t // tb

    def body(dca_ref, dcah_ref, dc4_ref, dc4h_ref, p0_ref, dbg_ref, dz_ref, dgt_ref, wa_ref, wg_ref, dp_ref):
        last = pl.program_id(0) == nb - 1
        dp_ref[:, :D] = dbg_ref[...]
        for i in range(D // 128):
            sl, cg, xv = _strip(i), _strip(i, D), _strip(i, 2 * D)
            du = _strip_conv_up(dca_ref[:, sl], dcah_ref[:, sl], last, wa_ref, sl, 3)
            dp_ref[:, cg] = (du * p0_ref[:, xv]).astype(bf16)
            dp_ref[:, xv] = (du * p0_ref[:, cg]).astype(bf16)
        for i in range(3 * D // 128):
            sl = _strip(i)
            dp_ref[:, _strip(i, 3 * D)] = _strip_conv_up(dc4_ref[:, sl], dc4h_ref[:, sl], last, wg_ref, sl, 4).astype(bf16)
        dp_ref[:, 6 * D:7 * D] = dz_ref[...]
        dp_ref[:, 7 * D:] = dgt_ref[...]

    return _call_with_exchange(
        body, exchange, name="pre_bwd2", grid=(nb,),
        in_specs=[_row(tb, D), _next(tb, D, t), _row(tb, 3 * D), _next(tb, 3 * D, t), _row(tb, 3 * D, 0),
                  _row(tb, D), _row(tb, D), _row(tb, 2 * D), _fixed((8, D)), _fixed((8, 3 * D))],
        out_specs=_row(tb, NW1), out_shape=jax.ShapeDtypeStruct((t, NW1), bf16),
        args=[dca, dca, dc4, dc4, p1, dbg, dz, dgates, wa, wg])


def _chunk_consts():
    r = lax.broadcasted_iota(jnp.int32, (CH, CH), 0)
    c = lax.broadcasted_iota(jnp.int32, (CH, CH), 1)
    return r, c, (r == c).astype(f32)


def _tri_inverse(lows, eye, r, c):
    def same_block(b):
        return jnp.bitwise_xor(r, c) < b

    xs = [jnp.where(same_block(8), -low, 0.0) for low in lows]
    ts = [eye + x for x in xs]
    for _ in range(2):
        xs = [_idot(x, x) for x in xs]
        ts = [t + _idot(t, x) for t, x in zip(ts, xs)]
    for b in (8, 16, 32):
        below = same_block(2 * b) & jnp.logical_not(same_block(b))
        ts = [t - _idot(_idot(t, jnp.where(below, low, 0.0)), t) for t, low in zip(ts, lows)]
    return ts


def _chunk_common(q, k, v, gcol, bcol, r, c, eye):
    grow = jnp.sum(eye * gcol, axis=0, keepdims=True)
    dec = jnp.exp(jnp.where(r >= c, gcol - grow, -jnp.inf))
    rcol = lax.broadcasted_iota(jnp.int32, (CH, 1), 0)
    glast = jnp.sum(jnp.where(rcol == CH - 1, gcol, 0.0), axis=0, keepdims=True)
    eg = jnp.exp(gcol)
    el = jnp.exp(glast - gcol)
    kb = k * bcol
    vb = v * bcol
    kk = _bdot_nt(kb, k)
    low = jnp.where(r > c, kk * dec, 0.0)
    qk = _bdot_nt(q, k)
    att = qk * dec
    return grow, dec, glast, eg, el, kb, vb, kk, low, qk, att, rcol


def _gdn_fwd(qn, kn, vc, gbeta):
    t = qn.shape[0]
    n_chunks = t // CH

    def body(q_ref, k_ref, v_ref, gb_ref, o_ref, s_ref, t_ref, state):
        @pl.when(pl.program_id(0) == 0)
        def _():
            state[...] = jnp.zeros_like(state)

        r, c, eye = _chunk_consts()
        tri = (r >= c).astype(f32)
        heads = range(H)
        keys = [(s, h) for s in range(GDN_STEP) for h in heads]
        rows = [slice(s * CH, (s + 1) * CH) for s in range(GDN_STEP)]
        gbs = [gb_ref[rows[s], :] for s in range(GDN_STEP)]
        galls = [_hdot(tri, gb) for gb in gbs]
        qs = {(s, h): q_ref[rows[s], h * DH:(h + 1) * DH] for s, h in keys}
        ks = {(s, h): k_ref[rows[s], h * DH:(h + 1) * DH] for s, h in keys}
        cm = {(s, h): _chunk_common(qs[s, h], ks[s, h], v_ref[rows[s], h * DH:(h + 1) * DH], galls[s][:, h:h + 1],
                                    gbs[s][:, H + h:H + h + 1], r, c, eye) for s, h in keys}
        invs = dict(zip(keys, _tri_inverse([cm[key][8] for key in keys], eye, r, c)))
        uws = {key: _bdot(invs[key], jnp.concatenate([cm[key][6], cm[key][5] * cm[key][3]], axis=1)) for key in keys}
        sts = [state[h] for h in heads]
        for s in range(GDN_STEP):
            vns = [uws[s, h][:, :DH] - _bdot(uws[s, h][:, DH:], sts[h]) for h in heads]
            outs = [_bdot(qs[s, h] * cm[s, h][3], sts[h]) + _bdot(cm[s, h][10], vns[h]) for h in heads]
            news = [sts[h] * jnp.exp(cm[s, h][2]) + _bdot_tn(ks[s, h] * cm[s, h][4], vns[h]) for h in heads]
            for h in heads:
                s_ref[s, h] = sts[h].astype(bf16)
                t_ref[s, h] = invs[s, h]
                o_ref[rows[s], h * DH:(h + 1) * DH] = outs[h]
            sts = news
        for h in heads:
            state[h] = sts[h]

    tb = GDN_STEP * CH
    return pl.pallas_call(
        body, name="gdn_fwd", grid=(t // tb,),
        in_specs=[_row(tb, D), _row(tb, D), _row(tb, D), _row(tb, 128)],
        out_specs=[_row(tb, D), pl.BlockSpec((GDN_STEP, H, DH, DH), lambda i: (i, 0, 0, 0)),
                   pl.BlockSpec((GDN_STEP, H, CH, CH), lambda i: (i, 0, 0, 0))],
        out_shape=[jax.ShapeDtypeStruct((t, D), f32), jax.ShapeDtypeStruct((n_chunks, H, DH, DH), bf16),
                   jax.ShapeDtypeStruct((n_chunks, H, CH, CH), f32)],
        scratch_shapes=[pltpu.VMEM((H, DH, DH), f32)],
        compiler_params=_params(1),
    )(qn, kn, vc, gbeta)


def _gdn_bwd(qn, kn, vc, gbeta, do, s_all, t_all):
    t = qn.shape[0]

    def body(q_ref, k_ref, v_ref, gb_ref, do_ref, s_ref, t_ref, dq_ref, dk_ref, dv_ref, dgb_ref, dstate):
        @pl.when(pl.program_id(0) == 0)
        def _():
            dstate[...] = jnp.zeros_like(dstate)

        r, c, eye = _chunk_consts()
        tril = r >= c
        lane = lax.broadcasted_iota(jnp.int32, (1, 128), 1)
        hs = range(H)

        def each(fn, *lists):
            return [fn(*args) for args in zip(*lists)]

        def rsum(a):
            return jnp.sum(a, axis=1, keepdims=True)

        def before_state(s):
            rows = slice(s * CH, (s + 1) * CH)
            gb = gb_ref[rows, :]
            gall = _hdot(tril.astype(f32), gb)
            p = {"rows": rows}
            p["q"] = q = [q_ref[rows, h * DH:(h + 1) * DH] for h in hs]
            p["k"] = k = [k_ref[rows, h * DH:(h + 1) * DH] for h in hs]
            p["v"] = v = [v_ref[rows, h * DH:(h + 1) * DH] for h in hs]
            p["dout"] = dout = [do_ref[rows, h * DH:(h + 1) * DH] for h in hs]
            p["inv"] = inv = [t_ref[s, h] for h in hs]
            p["st"] = st = [s_ref[s, h] for h in hs]
            p["bcol"] = bcol = [gb[:, H + h:H + h + 1] for h in hs]
            cm = [_chunk_common(q[h], k[h], v[h], gall[:, h:h + 1], bcol[h], r, c, eye) for h in hs]
            for name, i in (("dec", 1), ("glast", 2), ("eg", 3), ("el", 4), ("kb", 5), ("vb", 6), ("low", 8), ("att", 10)):
                p[name] = [m[i] for m in cm]
            p["rcol"] = cm[0][11]
            p["elast"] = each(jnp.exp, p["glast"])
            p["kbg"] = each(jnp.multiply, p["kb"], p["eg"])
            uw = each(lambda i, a, b: _bdot(i, jnp.concatenate([a, b], axis=1)), inv, p["vb"], p["kbg"])
            p["u"] = [a[:, :DH] for a in uw]
            p["w"] = [a[:, DH:] for a in uw]
            p["vn"] = each(lambda a, b, x: a - _bdot(b, x), p["u"], p["w"], st)
            p["qd"] = each(jnp.multiply, q, p["eg"])
            p["kd"] = each(jnp.multiply, k, p["el"])
            p["dqd"] = each(_bdot_nt, dout, st)
            p["datt"] = each(lambda d, x: jnp.where(tril, _bdot_nt(d, x), 0.0), dout, p["vn"])
            p["dqk"] = each(jnp.multiply, p["datt"], p["dec"])
            p["qd_do"] = each(_bdot_tn, p["qd"], dout)
            p["att_do"] = each(_bdot_tn, p["att"], dout)
            return p

        def after_state(p, ds):
            q, k, v, st, inv, bcol = p["q"], p["k"], p["v"], p["st"], p["inv"], p["bcol"]
            eg, el, kb, u, w = p["eg"], p["el"], p["kb"], p["u"], p["w"]
            dvn = each(lambda a, kk, x: a + _bdot(kk, x), p["att_do"], p["kd"], ds)
            dkd = each(_bdot_nt, p["vn"], ds)
            dw = each(lambda a, x: -_bdot_nt(a, x), dvn, st)
            new_ds = each(lambda x, e, a, ww, dv_: x * e + a - _bdot_tn(ww, dv_), ds, p["elast"], p["qd_do"], w, dvn)
            dglast = each(lambda e, x, d: e * jnp.sum(rsum(x.astype(f32) * d), axis=0, keepdims=True), p["elast"], st, ds)
            dr = each(lambda i, a, b: _bdot_tn(i, jnp.concatenate([a, b], axis=1)), inv, dvn, dw)
            dvb = [a[:, :DH] for a in dr]
            dkbg = [a[:, DH:] for a in dr]
            dlow = each(lambda a, b, x, y: -jnp.where(r > c, _bdot_nt(a, b) + _bdot_nt(x, y), 0.0), dvb, u, dkbg, w)
            dkk = each(jnp.multiply, dlow, p["dec"])
            mm = each(lambda a, b, x, y: a * b + x * y, dlow, p["low"], p["datt"], p["att"])
            dkb = each(lambda a, kk, b, e: _bdot(a, kk) + b * e, dkk, k, dkbg, eg)
            dk = each(lambda a, b, x, y, d, e, f, g: _bdot_tn(a, b) + _bdot_tn(x, y) + d * e + f * g,
                      dkk, kb, p["dqk"], q, dkd, el, dkb, bcol)
            dq = each(lambda a, kk, d, e: _bdot(a, kk) + d * e, p["dqk"], k, p["dqd"], eg)
            dv = each(jnp.multiply, dvb, bcol)
            dbeta = each(lambda a, b, x, y: rsum(a * b) + rsum(x * y), dkb, k, dvb, v)
            deg = each(lambda a, b, x, y: rsum(a * b) + rsum(x * y), dkbg, kb, p["dqd"], q)
            delc = each(lambda a, b, e: rsum(a * b) * e, dkd, k, el)
            dgc = each(lambda m, a, e, d: rsum(m) - rsum(eye * jnp.sum(m, axis=0, keepdims=True)) + a * e - d,
                       mm, deg, eg, delc)
            dgc = each(lambda g, d, l: g + jnp.where(p["rcol"] == CH - 1, jnp.sum(d, axis=0, keepdims=True) + l, 0.0),
                       dgc, delc, dglast)
            dg_acc = jnp.zeros((CH, 128), f32)
            db_acc = jnp.zeros((CH, 128), f32)
            rows = p["rows"]
            for h in hs:
                dq_ref[rows, h * DH:(h + 1) * DH] = dq[h]
                dk_ref[rows, h * DH:(h + 1) * DH] = dk[h]
                dv_ref[rows, h * DH:(h + 1) * DH] = dv[h]
                dg_acc = dg_acc + dgc[h] * (lane == h).astype(f32)
                db_acc = db_acc + dbeta[h] * (lane == H + h).astype(f32)
            dgb_ref[rows, :] = _hdot((r <= c).astype(f32), dg_acc) + db_acc
            return new_ds

        order = list(reversed(range(GDN_STEP)))
        pre = [before_state(s) for s in order]
        ds = [dstate[h] for h in hs]
        for p in pre:
            ds = after_state(p, ds)
        for h in hs:
            dstate[h] = ds[h]

    tb = GDN_STEP * CH
    n_steps = t // tb
    rev = lambda i: (n_steps - 1 - i, 0)
    rev4 = lambda i: (n_steps - 1 - i, 0, 0, 0)
    return pl.pallas_call(
        body, name="gdn_bwd", grid=(n_steps,),
        in_specs=[pl.BlockSpec((tb, D), rev), pl.BlockSpec((tb, D), rev), pl.BlockSpec((tb, D), rev),
                  pl.BlockSpec((tb, 128), rev), pl.BlockSpec((tb, D), rev),
                  pl.BlockSpec((GDN_STEP, H, DH, DH), rev4), pl.BlockSpec((GDN_STEP, H, CH, CH), rev4)],
        out_specs=[pl.BlockSpec((tb, D), rev), pl.BlockSpec((tb, D), rev), pl.BlockSpec((tb, D), rev),
                   pl.BlockSpec((tb, 128), rev)],
        out_shape=[jax.ShapeDtypeStruct((t, D), f32)] * 3 + [jax.ShapeDtypeStruct((t, 128), f32)],
        scratch_shapes=[pltpu.VMEM((H, DH, DH), f32)],
        compiler_params=_params(1),
    )(qn, kn, vc, gbeta, do, s_all, t_all)


def _pad_rows(w, rows=8):
    return jnp.pad(w, ((0, rows - w.shape[0]), (0, 0)))


_REST = ("w_up", "w_a_out", "w_b_out", "w_o", "w_down")


def _local_step(x, tgt, w, comm=None):
    g1 = w["norm_mix_g"].reshape(1, D)
    if comm is None:
        h1 = _rms_fwd(x, g1, name="rms1_fwd")
    else:
        h1, gathered = _rms_fwd(x, g1, name="rms1_fwd", exchange=comm.gather_first())
        w = {**w, **comm.finish_first(gathered)}
    w1, w2 = w["w1"], w["w2"]
    wa = _pad_rows(w["conv_a_w"])
    wg = _pad_rows(w["gdn_conv_w"])
    wf = _pad_rows(w["ffn_conv_w"])
    alog = jnp.pad(w["gdn_A_log"].reshape(1, H), ((0, 0), (0, 128 - H)))
    dtb = jnp.pad(w["gdn_dt_bias"].reshape(1, H), ((0, 0), (0, 128 - H)))
    g2 = w["norm_ffn_g"].reshape(1, D)
    g3 = w["norm_final_g"].reshape(1, D)
    gn = w["gdn_norm_g"].reshape(1, DH)

    if comm is None:
        p1 = _matmul(h1, w1, name="mm_in")
    else:
        p1, gathered = _matmul(h1, w1, name="mm_in", exchange=comm.gather_rest())
        w = {**w, **comm.finish_gather(gathered)}
    p2 = _matmul(h1, w2, name="mm_in_ab")
    ya_in, qn, kn, vc, gbeta = _pre_fwd(p1, p2, wa, wg, alog, dtb)
    o, s_all, t_all = _gdn_fwd(qn, kn, vc, gbeta)
    yb_in = _post_fwd(o, p1, gn)
    ya = _matmul(ya_in, w["w_a_out"], name="mm_a")
    yb = _matmul(yb_in, w["w_b_out"], name="mm_b")
    mix = _mix_fwd(ya, yb, p1)
    x2 = _matmul(mix, w["w_o"], name="mm_o", add=x)
    h2 = _rms_fwd(x2, g2, name="rms2_fwd")
    up = _matmul(h2, w["w_up"], name="mm_up", tn=DFF // 2)
    act = _ffn_fwd(up, wf)
    x3 = _matmul(act, w["w_down"], name="mm_down", add=x2, tm=512)
    loss_p, dx3, dx3b, dg3 = _final(x3, tgt, g3)

    grads = {"norm_final_g": dg3}
    dact = _matmul(dx3b, w["w_down"], nt=True, name="mm_down_dx", tm=512, tn=DFF)
    grads["w_down"] = _matmul_tn(act, dx3b, name="mm_down_dw", tm=DFF // 2)
    dc, dwf = _ffn_bwd1(dact, up, wf)
    grads["ffn_conv_w"] = dwf
    dup = _ffn_bwd2(dc, wf)
    dh2 = _matmul(dup, w["w_up"], nt=True, name="mm_up_dx", tk=DFF)
    grads["w_up"] = _matmul_tn(h2, dup, name="mm_up_dw", tn=512)
    dx2, dx2b, dg2 = _rms_bwd(dh2, x2, g2, dx3, name="rms2_bwd")
    grads["norm_ffn_g"] = dg2
    dmix = _matmul(dx2b, w["w_o"], nt=True, name="mm_o_dx")
    grads["w_o"] = _matmul_tn(mix, dx2b, name="mm_o_dw")
    dya, dyb, dgates = _mix_bwd(dmix, ya, yb, p1)
    dya_in = _matmul(dya, w["w_a_out"], nt=True, name="mm_a_dx")
    grads["w_a_out"] = _matmul_tn(ya_in, dya, name="mm_a_dw")
    dyb_in = _matmul(dyb, w["w_b_out"], nt=True, name="mm_b_dx")
    grads["w_b_out"] = _matmul_tn(yb_in, dyb, name="mm_b_dw")
    do, dz, dgn = _post_bwd(dyb_in, o, p1, gn)
    grads["gdn_norm_g"] = dgn
    dqn, dkn, dvc, dgb = _gdn_bwd(qn, kn, vc, gbeta, do, s_all, t_all)
    dbg, dca, dc4, dp2, dwa, dwg, dal, ddt = _pre_bwd1(p1, p2, dya_in, dqn, dkn, dvc, dgb, gbeta, wa, wg, alog, dtb)
    grads["conv_a_w"] = dwa
    grads["gdn_conv_w"] = dwg
    grads["gdn_A_log"] = dal
    grads["gdn_dt_bias"] = ddt
    grads["w2"] = _matmul_tn(h1, dp2, name="mm_in_ab_dw")
    if comm is None:
        dp1 = _pre_bwd2(dca, dc4, p1, dbg, dz, dgates, wa, wg)
        grads["w1"] = _matmul_tn(h1, dp1, name="mm_in_dw")
        dh1 = _matmul(dp1, w1, nt=True, name="mm_in_dx", tm=512, tk=NW1 // 2)
    else:
        exchange, blocks = comm.reduce_halves(_REST, grads)
        dp1, recv = _pre_bwd2(dca, dc4, p1, dbg, dz, dgates, wa, wg, exchange=exchange)
        exchange, sums = comm.reduce_sums(_REST, blocks, recv)
        grads["w1"], recv = _matmul_tn(h1, dp1, name="mm_in_dw", exchange=exchange)
        comm.finish_reduce(_REST, sums, recv)
        exchange, blocks = comm.reduce_halves(("w_in",), grads)
        exchange, sums = comm.reduce_sums(("w_in",), blocks, _run_exchange(exchange, name="rs_sibling_w_in"))
        dh1, recv = _matmul(dp1, w1, nt=True, name="mm_in_dx", tm=512, tk=NW1 // 2, exchange=exchange)
        comm.finish_reduce(("w_in",), sums, recv)
    dh1 = _matmul(dp2, w2, nt=True, name="mm_in_ab_dx", add=dh1)
    dx, _, dg1 = _rms_bwd(dh1, x, g1, dx2, name="rms1_bwd")
    grads["norm_mix_g"] = dg1
    return loss_p, dx, grads


_ANY = pl.BlockSpec(memory_space=pl.ANY)


def _remote(src, dst, send_sem, recv_sem, to):
    return pltpu.make_async_remote_copy(src_ref=src, dst_ref=dst, send_sem=send_sem, recv_sem=recv_sem,
                                        device_id=to, device_id_type=MESH)


def _run_exchange(exchange, *, name):
    arrays, shapes, sems, start, wait = exchange
    n_in, n_out = len(arrays), len(shapes)

    def body(*refs):
        start(refs[:n_in], refs[n_in:n_in + n_out], refs[n_in + n_out:])
        wait(refs[:n_in], refs[n_in:n_in + n_out], refs[n_in + n_out:])

    return pl.pallas_call(body, name=name, out_shape=list(shapes), in_specs=[_ANY] * n_in, out_specs=[_ANY] * n_out,
                          scratch_shapes=list(sems))(*arrays)


def _gather_exchange(shards):
    n = len(shards)

    def copies(x_refs, out_refs, sems):
        send_sems, recv_sems, local_sems = sems
        x, y, c = lax.axis_index("x"), lax.axis_index("y"), lax.axis_index("c")
        me, sibling = (x, y, c), (x, y, 1 - c)
        chips = [(1 - x, y), (x, 1 - y), (1 - x, 1 - y)]

        def copy(a, k, blk, to, from_input=False):
            dst = out_refs[a].at[4 * blk[0] + 2 * blk[1] + blk[2]]
            return _remote(x_refs[a] if from_input else dst, dst, send_sems.at[a, k], recv_sems.at[a, k], to)

        mine = [pltpu.make_async_copy(x_refs[a], out_refs[a].at[4 * x + 2 * y + c], local_sems.at[a]) for a in range(n)]
        first = []
        for a in range(n):
            first.append(copy(a, 0, me, sibling, from_input=True))
            first += [copy(a, 1 + j, me, (*chip, c), from_input=True) for j, chip in enumerate(chips)]
        return copy, mine, first, me, sibling, chips, c

    def start(x_refs, out_refs, sems):
        _, mine, first, *_ = copies(x_refs, out_refs, sems)
        for cp in mine + first:
            cp.start()

    def wait(x_refs, out_refs, sems):
        copy, mine, first, me, sibling, chips, c = copies(x_refs, out_refs, sems)
        passed = []
        for j, chip in enumerate(chips):
            for a in range(n):
                copy(a, 1 + j, (*chip, c), me).wait_recv()
                passed.append(copy(a, 4 + j, (*chip, c), sibling))
                passed[-1].start()
        for a in range(n):
            copy(a, 0, sibling, me).wait_recv()
            for j, chip in enumerate(chips):
                copy(a, 4 + j, (*chip, 1 - c), me).wait_recv()
        for cp in first + passed:
            cp.wait_send()
        for cp in mine:
            cp.wait()

    shapes = [jax.ShapeDtypeStruct((N_DEV, *s.shape), s.dtype) for s in shards]
    sems = [pltpu.SemaphoreType.DMA((n, 7)), pltpu.SemaphoreType.DMA((n, 7)), pltpu.SemaphoreType.DMA((n,))]
    return shards, shapes, sems, start, wait


def _gather_direct_exchange(shards):
    n = len(shards)

    def copies(x_refs, out_refs, sems):
        send_sems, recv_sems, local_sems = sems
        x, y, c = lax.axis_index("x"), lax.axis_index("y"), lax.axis_index("c")
        targets = [(x, y, 1 - c), (1 - x, y, c), (x, 1 - y, c), (1 - x, 1 - y, c)]
        local, sends, recvs = [], [], []
        for a in range(n):
            mine = out_refs[a].at[4 * x + 2 * y + c]
            local.append(pltpu.make_async_copy(x_refs[a], mine, local_sems.at[a]))
            for k, to in enumerate(targets):
                theirs = out_refs[a].at[4 * to[0] + 2 * to[1] + to[2]]
                sends.append(_remote(x_refs[a], mine, send_sems.at[a, k], recv_sems.at[a, k], to))
                recvs.append(_remote(theirs, theirs, send_sems.at[a, k], recv_sems.at[a, k], to))
        return local, sends, recvs

    def start(x_refs, out_refs, sems):
        local, sends, _ = copies(x_refs, out_refs, sems)
        for cp in local + sends:
            cp.start()

    def wait(x_refs, out_refs, sems):
        local, sends, recvs = copies(x_refs, out_refs, sems)
        for cp in recvs:
            cp.wait_recv()
        for cp in sends:
            cp.wait_send()
        for cp in local:
            cp.wait()

    shapes = [jax.ShapeDtypeStruct((N_DEV, *s.shape), s.dtype) for s in shards]
    sems = [pltpu.SemaphoreType.DMA((n, 4)), pltpu.SemaphoreType.DMA((n, 4)), pltpu.SemaphoreType.DMA((n,))]
    return shards, shapes, sems, start, wait


def _gather_forward(gathered):
    n = len(gathered)

    def body(*refs):
        out_refs = refs[n:2 * n]
        send_sems, recv_sems = refs[2 * n:]
        x, y, c = lax.axis_index("x"), lax.axis_index("y"), lax.axis_index("c")
        sibling = (x, y, 1 - c)
        sends, recvs = [], []
        for a in range(n):
            for j, (px, py) in enumerate([(1 - x, y), (x, 1 - y), (1 - x, 1 - y)]):
                mine = out_refs[a].at[4 * px + 2 * py + c]
                theirs = out_refs[a].at[4 * px + 2 * py + 1 - c]
                sends.append(_remote(mine, mine, send_sems.at[a, j], recv_sems.at[a, j], sibling))
                recvs.append(_remote(theirs, theirs, send_sems.at[a, j], recv_sems.at[a, j], sibling))
        for cp in sends:
            cp.start()
        for cp in recvs:
            cp.wait_recv()
        for cp in sends:
            cp.wait_send()

    return pl.pallas_call(
        body, name="ag_forward", out_shape=[jax.ShapeDtypeStruct(g.shape, g.dtype) for g in gathered],
        in_specs=[_ANY] * n, out_specs=[_ANY] * n, input_output_aliases={a: a for a in range(n)},
        scratch_shapes=[pltpu.SemaphoreType.DMA((n, 3)), pltpu.SemaphoreType.DMA((n, 3))],
    )(*gathered)


def _chips_exchange(hsums):
    n = len(hsums)

    def copies(h_refs, out_refs, sems):
        send_sems, recv_sems = sems
        x, y, c = lax.axis_index("x"), lax.axis_index("y"), lax.axis_index("c")
        chips = [(1 - x, y), (x, 1 - y), (1 - x, 1 - y)]
        return [_remote(h_refs[a].at[2 * px + py], out_refs[a].at[k], send_sems.at[a, k], recv_sems.at[a, k], (px, py, c))
                for a in range(n) for k, (px, py) in enumerate(chips)]

    def start(h_refs, out_refs, sems):
        for cp in copies(h_refs, out_refs, sems):
            cp.start()

    def wait(h_refs, out_refs, sems):
        for cp in copies(h_refs, out_refs, sems):
            cp.wait()

    shapes = [jax.ShapeDtypeStruct((3, *h.shape[1:]), h.dtype) for h in hsums]
    sems = [pltpu.SemaphoreType.DMA((n, 3)), pltpu.SemaphoreType.DMA((n, 3))]
    return hsums, shapes, sems, start, wait


def _sibling_exchange(halves):
    n = len(halves)

    def copies(p_refs, out_refs, sems):
        send_sems, recv_sems = sems
        x, y, c = lax.axis_index("x"), lax.axis_index("y"), lax.axis_index("c")
        return [_remote(p_refs[a], out_refs[a], send_sems.at[a], recv_sems.at[a], (x, y, 1 - c)) for a in range(n)]

    def start(p_refs, out_refs, sems):
        for cp in copies(p_refs, out_refs, sems):
            cp.start()

    def wait(p_refs, out_refs, sems):
        for cp in copies(p_refs, out_refs, sems):
            cp.wait()

    shapes = [jax.ShapeDtypeStruct(h.shape, h.dtype) for h in halves]
    return halves, shapes, [pltpu.SemaphoreType.DMA((n,)), pltpu.SemaphoreType.DMA((n,))], start, wait


_IN_RANGES = ((0, 7 * D, 0, 0), (7 * D, 7 * D + 16, 1, 0), (7 * D + 16, 9 * D + 16, 0, 7 * D))
_UP_RANGES = ((0, 2 * DFF, 0, 0),)


def _col_pieces(width, ranges):
    pieces = []
    for d in range(N_DEV):
        lo, hi = d * width, (d + 1) * width
        for glo, ghi, mat, mlo in ranges:
            a, b = max(lo, glo), min(hi, ghi)
            if a < b:
                pieces.append((d, a - lo, b - lo, mat, mlo + a - glo))
    return pieces


def _cols_to_matrices(g, ranges, out_widths, *, name):
    _, rows, width = g.shape
    tb = 128
    pieces = _col_pieces(width, ranges)
    covered = [sum(p[2] - p[1] for p in pieces if p[3] == m) for m in range(len(out_widths))]

    def body(g_ref, *o_refs):
        for m, o_ref in enumerate(o_refs):
            if covered[m] < out_widths[m]:
                o_ref[...] = jnp.zeros_like(o_ref)
        for d, b0, b1, m, m0 in pieces:
            o_refs[m][:, m0:m0 + b1 - b0] = g_ref[d, :, b0:b1]

    return pl.pallas_call(
        body, name=name, grid=(rows // tb,), in_specs=[pl.BlockSpec((N_DEV, tb, width), lambda i: (0, i, 0))],
        out_specs=[pl.BlockSpec((tb, wo), lambda i: (i, 0)) for wo in out_widths],
        out_shape=[jax.ShapeDtypeStruct((rows, wo), g.dtype) for wo in out_widths], compiler_params=_params(1),
    )(g)


def _matrices_to_cols(mats, ranges, width, *, name):
    rows = mats[0].shape[0]
    tb = 128
    pieces = _col_pieces(width, ranges)

    def body(*refs):
        m_refs, g_ref = refs[:-1], refs[-1]
        for d, b0, b1, m, m0 in pieces:
            g_ref[d, :, b0:b1] = m_refs[m][:, m0:m0 + b1 - b0]

    return pl.pallas_call(
        body, name=name, grid=(rows // tb,),
        in_specs=[pl.BlockSpec((tb, mt.shape[1]), lambda i: (i, 0)) for mt in mats],
        out_specs=pl.BlockSpec((N_DEV, tb, width), lambda i: (0, i, 0)),
        out_shape=jax.ShapeDtypeStruct((N_DEV, rows, width), mats[0].dtype), compiler_params=_params(1),
    )(*mats)


def _row_block(rows):
    return 128 if rows % 128 == 0 else rows


def _half_bf16(g4, c_other, *, name):
    _, _, rows, width = g4.shape
    tb = _row_block(rows)

    def body(c_ref, p_ref, o_ref):
        o_ref[0] = p_ref[0, 0].astype(bf16)

    grid_spec = pltpu.PrefetchScalarGridSpec(
        num_scalar_prefetch=1, grid=(4, rows // tb),
        in_specs=[pl.BlockSpec((1, 1, tb, width), lambda j, i, c_ref: (j, c_ref[0], i, 0))],
        out_specs=pl.BlockSpec((1, tb, width), lambda j, i, c_ref: (j, i, 0)))
    return pl.pallas_call(
        body, name=name, grid_spec=grid_spec, out_shape=jax.ShapeDtypeStruct((4, rows, width), bf16),
        compiler_params=_params(2),
    )(c_other, g4)


def _pair_sum(g4, recv, c_me, *, name):
    _, _, rows, width = g4.shape
    tb = _row_block(rows)

    def body(c_ref, p_ref, r_ref, o_ref, ob_ref):
        s = p_ref[0, 0] + r_ref[0].astype(f32)
        o_ref[0] = s
        ob_ref[0] = s.astype(bf16)

    blk = pl.BlockSpec((1, tb, width), lambda j, i, c_ref: (j, i, 0))
    grid_spec = pltpu.PrefetchScalarGridSpec(
        num_scalar_prefetch=1, grid=(4, rows // tb),
        in_specs=[pl.BlockSpec((1, 1, tb, width), lambda j, i, c_ref: (j, c_ref[0], i, 0)), blk],
        out_specs=[blk, blk])
    return pl.pallas_call(
        body, name=name, grid_spec=grid_spec,
        out_shape=[jax.ShapeDtypeStruct((4, rows, width), f32), jax.ShapeDtypeStruct((4, rows, width), bf16)],
        compiler_params=_params(2),
    )(c_me, g4, recv)


def _adam_shard(hsum, recv, chip, w, m, v, *, name):
    _, rows, width = w.shape
    tb = _row_block(rows)

    def body(j_ref, h_ref, r_ref, w_ref, m_ref, v_ref, g_out, d_out, m_out, v_out):
        g = ((h_ref[0] + r_ref[0].astype(f32)) + r_ref[1].astype(f32)) + r_ref[2].astype(f32)
        delta, mn, vn = _adam_math(w_ref[0], g, m_ref[0], v_ref[0])
        g_out[0] = g
        d_out[0] = delta
        m_out[0] = mn
        v_out[0] = vn

    blk = pl.BlockSpec((1, tb, width), lambda i, j_ref: (0, i, 0))
    grid_spec = pltpu.PrefetchScalarGridSpec(
        num_scalar_prefetch=1, grid=(rows // tb,),
        in_specs=[pl.BlockSpec((1, tb, width), lambda i, j_ref: (j_ref[0], i, 0)),
                  pl.BlockSpec((3, tb, width), lambda i, j_ref: (0, i, 0)), blk, blk, blk],
        out_specs=[blk, blk, blk, blk])
    return pl.pallas_call(
        body, name=name, grid_spec=grid_spec, out_shape=[jax.ShapeDtypeStruct(w.shape, f32)] * 4,
        compiler_params=_params(1),
    )(chip, hsum, recv, w, m, v)


R_SMALL = 16 + 16 * N_DEV
_SMALL_LANES = {"gdn_norm_g": (0, DH), "gdn_A_log": (DH, DH + H), "gdn_dt_bias": (2 * DH, 2 * DH + H)}
_LOSS_LANE = 3 * DH


def _pack_small(dg1, dg2, dg3, dgn, dal, ddt, loss_p, dwa, dwg, dwf):
    def body(dg1_ref, dg2_ref, dg3_ref, dgn_ref, dal_ref, ddt_ref, loss_ref, dwa_ref, dwg_ref, dwf_ref, o_ref):
        def total(ref):
            return jnp.sum(ref[...], axis=0, keepdims=True)

        o_ref[...] = jnp.zeros_like(o_ref)
        o_ref[0:1, :] = total(dg1_ref)
        o_ref[1:2, :] = total(dg2_ref)
        o_ref[2:3, :] = total(dg3_ref)
        o_ref[3:4, 0:DH] = total(dgn_ref)
        o_ref[3:4, DH:2 * DH] = total(dal_ref)
        o_ref[3:4, 2 * DH:3 * DH] = total(ddt_ref)
        o_ref[3:4, 3 * DH:4 * DH] = total(loss_ref)
        for d in range(N_DEV):
            base = 16 + 16 * d
            o_ref[base:base + 3, 0:128] = dwa_ref[0:3, 128 * d:128 * (d + 1)]
            o_ref[base:base + 4, 128:512] = dwg_ref[0:4, 384 * d:384 * (d + 1)]
            o_ref[base + 8:base + 11, 0:704] = dwf_ref[0:3, 704 * d:704 * (d + 1)]

    return pl.pallas_call(body, name="pack_small", out_shape=jax.ShapeDtypeStruct((R_SMALL, D), f32))(
        dg1, dg2, dg3, dgn, dal, ddt, loss_p, dwa, dwg, dwf)


_SMALL = ("norm_mix_g", "norm_ffn_g", "norm_final_g", "gdn_norm_g", "gdn_A_log", "gdn_dt_bias",
          "conv_a_w", "gdn_conv_w", "ffn_conv_w")


def _adam_small(gath, me, w, m, v):
    arrays = [t[n] for n in _SMALL for t in (w, m, v)]

    def body(me_ref, ga_ref, gb_ref, *refs):
        ins, outs = refs[:len(arrays)], refs[len(arrays):]
        ga, gb = ga_ref[0], gb_ref[0]
        for s in range(1, N_DEV):
            ga = ga + ga_ref[s]
            gb = gb + gb_ref[s]
        grads = {"norm_mix_g": ga[0:1, :], "norm_ffn_g": ga[1:2, :], "norm_final_g": ga[2:3, :],
                 "conv_a_w": gb[0:3, 0:128], "gdn_conv_w": gb[0:4, 128:512], "ffn_conv_w": gb[8:11, 0:704]}
        for n, (lo, hi) in _SMALL_LANES.items():
            grads[n] = ga[3:4, lo:hi]
        for i, n in enumerate(_SMALL):
            three_d = len(w[n].shape) == 3
            wv, mv, vv = (r[0] if three_d else r[...] for r in ins[3 * i:3 * i + 3])
            delta, mn, vn = _adam_math(wv, grads[n], mv, vv)
            for o_ref, val in zip(outs[4 * i:4 * i + 4], (grads[n], delta, mn, vn)):
                if three_d:
                    o_ref[0] = val
                else:
                    o_ref[...] = val
        outs[-1][...] = ga[3:4, _LOSS_LANE:_LOSS_LANE + 1]

    def whole(shape):
        return pl.BlockSpec(shape, lambda i, me_ref: (0,) * len(shape))

    grid_spec = pltpu.PrefetchScalarGridSpec(
        num_scalar_prefetch=1, grid=(1,),
        in_specs=[pl.BlockSpec((N_DEV, 16, D), lambda i, me_ref: (0, 0, 0)),
                  pl.BlockSpec((N_DEV, 16, D), lambda i, me_ref: (0, 1 + me_ref[0], 0))] + [whole(a.shape) for a in arrays],
        out_specs=[whole(w[n].shape) for n in _SMALL for _ in range(4)] + [whole((1, 1))])
    res = pl.pallas_call(
        body, name="adam_small", grid_spec=grid_spec,
        out_shape=[jax.ShapeDtypeStruct(w[n].shape, f32) for n in _SMALL for _ in range(4)]
        + [jax.ShapeDtypeStruct((1, 1), f32)],
        compiler_params=_params(1),
    )(me, gath, gath, *arrays)
    return {n: tuple(res[4 * i:4 * i + 4]) for i, n in enumerate(_SMALL)}, res[-1]


def _adam_math(w, g, m, v):
    m = ADAM_B1 * m + (1.0 - ADAM_B1) * g
    v = ADAM_B2 * v + (1.0 - ADAM_B2) * jnp.square(g)
    m_hat = m / (1.0 - ADAM_B1 ** ADAM_STEP)
    v_hat = v / (1.0 - ADAM_B2 ** ADAM_STEP)
    delta = -ADAM_LR * (m_hat / (jnp.sqrt(v_hat) + ADAM_EPS) + ADAM_WD * w)
    return delta, m, v


_WEIGHTS = ("norm_mix_g", "w_in", "conv_a_w", "gdn_conv_w", "gdn_A_log", "gdn_dt_bias", "gdn_norm_g", "w_a_out",
            "w_b_out", "w_o", "norm_ffn_g", "w_up", "ffn_conv_w", "w_down", "norm_final_g")
_BIG = ("w_in",) + _REST
_CONVS = ("conv_a_w", "gdn_conv_w", "ffn_conv_w")


class _StepExchanges:
    def __init__(self, wts, mom, var, c_me, chip):
        self.wts, self.mom, self.var, self.c_me, self.chip = wts, mom, var, c_me, chip
        self.results = {}

    def gather_first(self):
        return _gather_exchange([self.wts["w_in"][0].astype(bf16)] + [self.wts[n][0] for n in _CONVS])

    def finish_first(self, gathered):
        g_in, gc_a, gc_g, gc_f = gathered
        w1, w2 = _cols_to_matrices(g_in, _IN_RANGES, (NW1, 128), name="relay_w_in")
        return {"w1": w1, "w2": w2, "conv_a_w": gc_a.transpose(1, 0, 2).reshape(3, D),
                "gdn_conv_w": gc_g.transpose(1, 0, 2).reshape(4, 3 * D),
                "ffn_conv_w": gc_f.transpose(1, 0, 2).reshape(3, 2 * DFF)}

    def gather_rest(self):
        return _gather_direct_exchange([self.wts[n][0].astype(bf16) for n in _REST])

    def finish_gather(self, gathered):
        g_up, g_a, g_b, g_o, g_down = _gather_forward(gathered)
        (w_up,) = _cols_to_matrices(g_up, _UP_RANGES, (2 * DFF,), name="relay_w_up")
        return {"w_up": w_up, "w_a_out": g_a.reshape(D, D), "w_b_out": g_b.reshape(D, D), "w_o": g_o.reshape(D, D),
                "w_down": g_down.reshape(DFF, D)}

    def reduce_halves(self, names, grads):
        blocks = []
        for n in names:
            if n == "w_in":
                g = _matrices_to_cols([grads["w1"], grads["w2"]], _IN_RANGES, R_IN, name="relay_dw_in")
            elif n == "w_up":
                g = _matrices_to_cols([grads[n]], _UP_RANGES, R_UP, name="relay_dw_up")
            else:
                g = grads[n]
            blocks.append(g.reshape(4, 2, *self.wts[n].shape[1:]))
        return _sibling_exchange([_half_bf16(g, 1 - self.c_me, name="rs_half_" + n) for n, g in zip(names, blocks)]), blocks

    def reduce_sums(self, names, blocks, recv):
        sums = [_pair_sum(g, r, self.c_me, name="rs_sum_" + n) for n, g, r in zip(names, blocks, recv)]
        return _chips_exchange([s[1] for s in sums]), [s[0] for s in sums]

    def finish_reduce(self, names, sums, recv):
        for n, s, r in zip(names, sums, recv):
            self.results[n] = _adam_shard(s, r, self.chip, self.wts[n], self.mom[n], self.var[n], name="adam_" + n)


def kernel(x, norm_mix_g, w_in, conv_a_w, gdn_conv_w, gdn_A_log, gdn_dt_bias, gdn_norm_g, w_a_out, w_b_out, w_o, norm_ffn_g, w_up, ffn_conv_w, w_down, norm_final_g, loss_target, m_norm_mix_g, m_w_in, m_conv_a_w, m_gdn_conv_w, m_gdn_A_log, m_gdn_dt_bias, m_gdn_norm_g, m_w_a_out, m_w_b_out, m_w_o, m_norm_ffn_g, m_w_up, m_ffn_conv_w, m_w_down, m_norm_final_g, v_norm_mix_g, v_w_in, v_conv_a_w, v_gdn_conv_w, v_gdn_A_log, v_gdn_dt_bias, v_gdn_norm_g, v_w_a_out, v_w_b_out, v_w_o, v_norm_ffn_g, v_w_up, v_ffn_conv_w, v_w_down, v_norm_final_g):
    wts = dict(zip(_WEIGHTS, (norm_mix_g, w_in, conv_a_w, gdn_conv_w, gdn_A_log, gdn_dt_bias, gdn_norm_g, w_a_out,
                              w_b_out, w_o, norm_ffn_g, w_up, ffn_conv_w, w_down, norm_final_g)))
    mom = dict(zip(_WEIGHTS, (m_norm_mix_g, m_w_in, m_conv_a_w, m_gdn_conv_w, m_gdn_A_log, m_gdn_dt_bias,
                              m_gdn_norm_g, m_w_a_out, m_w_b_out, m_w_o, m_norm_ffn_g, m_w_up, m_ffn_conv_w,
                              m_w_down, m_norm_final_g)))
    var = dict(zip(_WEIGHTS, (v_norm_mix_g, v_w_in, v_conv_a_w, v_gdn_conv_w, v_gdn_A_log, v_gdn_dt_bias,
                              v_gdn_norm_g, v_w_a_out, v_w_b_out, v_w_o, v_norm_ffn_g, v_w_up, v_ffn_conv_w,
                              v_w_down, v_norm_final_g)))
    cx, cy, cc = lax.axis_index("x"), lax.axis_index("y"), lax.axis_index("c")
    c_me = jnp.reshape(cc, (1,)).astype(jnp.int32)
    chip = jnp.reshape(2 * cx + cy, (1,)).astype(jnp.int32)
    me = jnp.reshape(4 * cx + 2 * cy + cc, (1,)).astype(jnp.int32)

    comm = _StepExchanges(wts, mom, var, c_me, chip)
    replicated = {n: wts[n] for n in ("norm_mix_g", "norm_ffn_g", "norm_final_g", "gdn_norm_g", "gdn_A_log", "gdn_dt_bias")}
    loss_p, dx, grads = _local_step(x[0], loss_target[0], replicated, comm)
    res = comm.results

    small = _pack_small(grads["norm_mix_g"], grads["norm_ffn_g"], grads["norm_final_g"], grads["gdn_norm_g"],
                        grads["gdn_A_log"], grads["gdn_dt_bias"], loss_p, grads["conv_a_w"], grads["gdn_conv_w"],
                        grads["ffn_conv_w"])
    (small_all,) = _run_exchange(_gather_exchange([small]), name="ag_small")

    def raw(t):
        return {n: t[n].reshape(1, D) if n == "norm_final_g" else t[n] for n in _SMALL}

    res_small, loss = _adam_small(small_all, me, raw(wts), raw(mom), raw(var))
    for n in _SMALL:
        res[n] = tuple(a.reshape(wts[n].shape) for a in res_small[n])
    outs = [[res[n][i] for n in _WEIGHTS] for i in range(4)]
    return (loss.reshape(()), dx[None], *outs[0], *outs[1], *outs[2], *outs[3])
```

```python
import jax
import jax.numpy as jnp
from jax import lax
from jax.experimental import pallas as pl
from jax.experimental.pallas import tpu as pltpu

f32 = jnp.float32
bf16 = jnp.bfloat16

D = 1024
H = 8
DH = 128
CH = 64
GDN_STEP = 2
DFF = 2816
NW1 = 9216
EPS = 1e-6
N_DEV = 8

ADAM_LR = 0.001
ADAM_B1 = 0.9
ADAM_B2 = 0.999
ADAM_EPS = 1e-08
ADAM_WD = 0.01
ADAM_STEP = 10

VMEM_LIMIT_BYTES = 48 * 1024 * 1024

R_IN, R_UP = 1154, 704

_HI = lax.Precision.HIGHEST
MESH = pl.DeviceIdType.MESH


def _params(n_grid):
    return pltpu.CompilerParams(dimension_semantics=("arbitrary",) * n_grid, vmem_limit_bytes=VMEM_LIMIT_BYTES)


def _bdot(a, b):
    return jnp.dot(a.astype(bf16), b.astype(bf16), preferred_element_type=f32)


def _bdot_nt(a, b):
    return lax.dot_general(a.astype(bf16), b.astype(bf16), (((1,), (1,)), ((), ())), preferred_element_type=f32)


def _bdot_tn(a, b):
    return lax.dot_general(a.astype(bf16), b.astype(bf16), (((0,), (0,)), ((), ())), preferred_element_type=f32)


def _hdot(a, b):
    return jnp.dot(a, b, preferred_element_type=f32, precision=_HI)


def _idot(a, b):
    return jnp.dot(a, b, preferred_element_type=f32, precision=lax.Precision.HIGH)


def _sigmoid(x):
    return 1.0 / (1.0 + jnp.exp(-x))


def _softplus(x):
    return jnp.maximum(x, 0.0) + jnp.log(1.0 + jnp.exp(-jnp.abs(x)))


def _shift_down(x, halo, j):
    if j == 0:
        return x
    xr = pltpu.roll(x, j, 0)
    hr = pltpu.roll(halo, j, 0)
    r8 = lax.broadcasted_iota(jnp.int32, hr.shape, 0)
    top = jnp.where(r8 < j, hr, xr[:8])
    return jnp.concatenate([top, xr[8:]], axis=0)


def _shift_up(x, halo, j):
    if j == 0:
        return x
    n = x.shape[0]
    xr = pltpu.roll(x, n - j, 0)
    hr = pltpu.roll(halo, 8 - j, 0)
    r8 = lax.broadcasted_iota(jnp.int32, hr.shape, 0)
    bot = jnp.where(r8 >= 8 - j, hr, xr[n - 8:])
    return jnp.concatenate([xr[:n - 8], bot], axis=0)


def _taps_down(x, halo, k):
    return [_shift_down(x, halo, k - 1 - j) for j in range(k)]


def _strip(i, base=0):
    return slice(base + i * 128, base + (i + 1) * 128)


def _strip_taps(x, halo, first, k):
    return _taps_down(x, jnp.where(first, 0.0, halo), k)


def _strip_conv(w_ref, sl, taps):
    out = w_ref[0:1, sl] * taps[0]
    for j in range(1, len(taps)):
        out = out + w_ref[j:j + 1, sl] * taps[j]
    return out


def _strip_weight_grad(dw_ref, sl, dy, taps):
    for j, tap in enumerate(taps):
        dw_ref[j:j + 1, sl] += jnp.sum(dy * tap, axis=0, keepdims=True)


def _strip_conv_up(dy, halo, last, w_ref, sl, k):
    halo = jnp.where(last, 0.0, halo)
    out = w_ref[k - 1:k, sl] * dy
    for j in range(k - 1):
        out = out + w_ref[j:j + 1, sl] * _shift_up(dy, halo, k - 1 - j)
    return out


def _row(tb, w, col=0):
    return pl.BlockSpec((tb, w), lambda i: (i, col))


def _prev(tb, w, col=0, rows=8):
    return pl.BlockSpec((rows, w), lambda i: (jnp.maximum(i * (tb // rows) - 1, 0), col))


def _next(tb, w, n_rows, col=0, rows=8):
    last = n_rows // rows - 1
    return pl.BlockSpec((rows, w), lambda i: (jnp.minimum((i + 1) * (tb // rows), last), col))


def _f32(ref, sl):
    return ref[:, sl].astype(f32)


def _halo_before(ref, sl):
    h = _f32(ref, sl)
    return h[h.shape[0] - 8:]


def _halo_after(ref, sl):
    return _f32(ref, sl)[:8]


def _fixed(shape):
    return pl.BlockSpec(shape, lambda i: (0,) * len(shape))


def _pick(n, prefs):
    for p in prefs:
        if n % p == 0:
            return p
    return n


def _matmul(a, b, *, name, nt=False, add=None, tm=1024, tn=1024, tk=None, out_dtype=f32, exchange=None):
    m, kd = a.shape
    n = b.shape[0] if nt else b.shape[1]
    tm = _pick(m, (tm, 512, 256))
    tn = _pick(n, (tn, 1024, 512, 128))
    tk = kd if tk is None else tk
    nk = kd // tk
    assert nk == 1 or out_dtype == f32
    dims = (((1,), (1,)), ((), ())) if nt else (((1,), (0,)), ((), ()))

    def body(a_ref, b_ref, *rest):
        o_ref = rest[-1]
        part = lax.dot_general(a_ref[...], b_ref[...], dims, preferred_element_type=f32)
        if nk == 1:
            o_ref[...] = (part if add is None else part + rest[0][...]).astype(out_dtype)
            return
        k = pl.program_id(2)

        @pl.when(k == 0)
        def _():
            o_ref[...] = part if add is None else part + rest[0][...]

        @pl.when(k > 0)
        def _():
            o_ref[...] += part

    b_spec = pl.BlockSpec((tn, tk), lambda i, j, k: (j, k)) if nt else pl.BlockSpec((tk, tn), lambda i, j, k: (k, j))
    in_specs = [pl.BlockSpec((tm, tk), lambda i, j, k: (i, k)), b_spec]
    args = [a, b]
    if add is not None:
        in_specs.append(pl.BlockSpec((tm, tn), lambda i, j, k: (i, j)))
        args.append(add)
    return _call_with_exchange(
        body, exchange, name=name, grid=(m // tm, n // tn, nk), in_specs=in_specs,
        out_specs=pl.BlockSpec((tm, tn), lambda i, j, k: (i, j)),
        out_shape=jax.ShapeDtypeStruct((m, n), out_dtype), args=args)


def _call_with_exchange(body, exchange, *, name, grid, in_specs, out_specs, out_shape, args):
    if exchange is None:
        return pl.pallas_call(body, name=name, grid=grid, in_specs=in_specs, out_specs=out_specs, out_shape=out_shape,
                              compiler_params=_params(len(grid)))(*args)
    x_arrays, x_shapes, x_sems, start, wait = exchange
    n_in, n_xin, n_xout = len(args), len(x_arrays), len(x_shapes)

    def full_body(*refs):
        c_in, x_in = refs[:n_in], refs[n_in:n_in + n_xin]
        c_out = refs[n_in + n_xin]
        x_out = refs[n_in + n_xin + 1:n_in + n_xin + 1 + n_xout]
        sems = refs[n_in + n_xin + 1 + n_xout:]
        ids = [pl.program_id(d) for d in range(len(grid))]
        first, last = ids[0] == 0, ids[0] == grid[0] - 1
        for d in range(1, len(grid)):
            first = first & (ids[d] == 0)
            last = last & (ids[d] == grid[d] - 1)

        @pl.when(first)
        def _():
            start(x_in, x_out, sems)

        body(*c_in, c_out)

        @pl.when(last)
        def _():
            wait(x_in, x_out, sems)

    res = pl.pallas_call(
        full_body, name=name, grid=grid, in_specs=list(in_specs) + [_ANY] * n_xin,
        out_specs=[out_specs] + [_ANY] * n_xout, out_shape=[out_shape] + list(x_shapes),
        scratch_shapes=list(x_sems), compiler_params=_params(len(grid)),
    )(*args, *x_arrays)
    return res[0], list(res[1:])


def _matmul_tn(a, b, *, name, tm=1024, tn=1024, exchange=None):
    t, m = a.shape
    _, n = b.shape
    tm = _pick(m, (tm, 1024, 512, 128))
    tn = _pick(n, (tn, 1024, 512, 128))
    tt = _pick(t, (2048, 1024, 512, 256))
    nt = t // tt

    def body(a_ref, b_ref, o_ref):
        k = pl.program_id(2)
        part = lax.dot_general(a_ref[...], b_ref[...], (((0,), (0,)), ((), ())), preferred_element_type=f32)

        @pl.when(k == 0)
        def _():
            o_ref[...] = part

        @pl.when(k > 0)
        def _():
            o_ref[...] += part

    return _call_with_exchange(
        body, exchange, name=name, grid=(m // tm, n // tn, nt),
        in_specs=[pl.BlockSpec((tt, tm), lambda i, j, k: (k, i)), pl.BlockSpec((tt, tn), lambda i, j, k: (k, j))],
        out_specs=pl.BlockSpec((tm, tn), lambda i, j, k: (i, j)),
        out_shape=jax.ShapeDtypeStruct((m, n), f32), args=[a, b])


def _rms_fwd(x, g, *, name, exchange=None):
    t = x.shape[0]
    tb = _pick(t, (256, 128))

    def body(x_ref, g_ref, h_ref):
        xv = x_ref[...]
        r = lax.rsqrt(jnp.mean(xv * xv, axis=-1, keepdims=True) + EPS)
        h_ref[...] = (xv * r * g_ref[...]).astype(bf16)

    return _call_with_exchange(
        body, exchange, name=name, grid=(t // tb,), in_specs=[_row(tb, D), _fixed((1, D))], out_specs=_row(tb, D),
        out_shape=jax.ShapeDtypeStruct((t, D), bf16), args=[x, g])


def _rms_bwd(dh, x, g, dres, *, name):
    t = x.shape[0]
    tb = _pick(t, (256, 128))

    def body(dh_ref, x_ref, g_ref, dres_ref, dx_ref, dxb_ref, dg_ref):
        xv = x_ref[...]
        r = lax.rsqrt(jnp.mean(xv * xv, axis=-1, keepdims=True) + EPS)
        xh = xv * r
        dy = dh_ref[...]
        dyg = dy * g_ref[...]
        dx = dres_ref[...] + r * (dyg - xh * jnp.mean(dyg * xh, axis=-1, keepdims=True))
        dx_ref[...] = dx
        dxb_ref[...] = dx.astype(bf16)

        @pl.when(pl.program_id(0) == 0)
        def _():
            dg_ref[...] = jnp.zeros_like(dg_ref)

        dg_ref[...] += jnp.sum((dy * xh).reshape(tb // 8, 8, D), axis=0)

    return pl.pallas_call(
        body, name=name, grid=(t // tb,),
        in_specs=[_row(tb, D), _row(tb, D), _fixed((1, D)), _row(tb, D)],
        out_specs=[_row(tb, D), _row(tb, D), _fixed((8, D))],
        out_shape=[jax.ShapeDtypeStruct((t, D), f32), jax.ShapeDtypeStruct((t, D), bf16),
                   jax.ShapeDtypeStruct((8, D), f32)],
        compiler_params=_params(1),
    )(dh, x, g, dres)


def _gdn_gates(ab, alog, dtb):
    lane = lax.broadcasted_iota(jnp.int32, ab.shape, 1)
    g = -jnp.exp(alog) * _softplus(ab + dtb)
    beta = _sigmoid(ab)
    return jnp.where(lane < H, g, jnp.where(lane < 2 * H, beta, 0.0))


def _pre_fwd(p1, p2, wa, wg, alog, dtb):
    t = p1.shape[0]
    tb = 128

    def body(p0_ref, p0h_ref, pq_ref, pqh_ref, p2_ref, wa_ref, wg_ref, alog_ref, dtb_ref,
             ya_ref, qn_ref, kn_ref, vc_ref, gb_ref):
        first = pl.program_id(0) == 0
        for i in range(D // 128):
            sl, cg, xv = _strip(i), _strip(i, D), _strip(i, 2 * D)
            taps = _strip_taps(p0_ref[:, cg] * p0_ref[:, xv], p0h_ref[:, cg] * p0h_ref[:, xv], first, 3)
            ya_ref[:, sl] = (p0_ref[:, sl] * _strip_conv(wa_ref, sl, taps)).astype(bf16)
        for part, out_ref, scale in ((0, qn_ref, DH ** -0.5), (1, kn_ref, 1.0), (2, vc_ref, None)):
            for h in range(H):
                sl = _strip(h, part * D)
                s = _strip_conv(wg_ref, sl, _strip_taps(pq_ref[:, sl], pqh_ref[:, sl], first, 4))
                s = s * _sigmoid(s)
                if scale is not None:
                    s = s * (lax.rsqrt(jnp.sum(s * s, axis=-1, keepdims=True) + EPS) * scale)
                out_ref[:, _strip(h)] = s
        gb_ref[...] = _gdn_gates(p2_ref[...], alog_ref[...], dtb_ref[...])

    return pl.pallas_call(
        body, name="pre_fwd", grid=(t // tb,),
        in_specs=[_row(tb, 3 * D, 0), _prev(tb, 3 * D, 0), _row(tb, 3 * D, 1), _prev(tb, 3 * D, 1), _row(tb, 128),
                  _fixed((8, D)), _fixed((8, 3 * D)), _fixed((1, 128)), _fixed((1, 128))],
        out_specs=[_row(tb, D), _row(tb, D), _row(tb, D), _row(tb, D), _row(tb, 128)],
        out_shape=[jax.ShapeDtypeStruct((t, D), bf16), jax.ShapeDtypeStruct((t, D), f32),
                   jax.ShapeDtypeStruct((t, D), f32), jax.ShapeDtypeStruct((t, D), f32),
                   jax.ShapeDtypeStruct((t, 128), f32)],
        compiler_params=_params(1),
    )(p1, p1, p1, p1, p2, wa, wg, alog, dtb)


def _post_fwd(o, p1, gn):
    t = o.shape[0]
    tb = _pick(t, (256, 128))

    def body(o_ref, z_ref, gn_ref, yb_ref):
        for h in range(H):
            sl = slice(h * DH, (h + 1) * DH)
            oh = o_ref[:, sl]
            z = z_ref[:, sl]
            r = lax.rsqrt(jnp.mean(oh * oh, axis=-1, keepdims=True) + EPS)
            yb_ref[:, sl] = (oh * r * gn_ref[...] * (z * _sigmoid(z))).astype(bf16)

    return pl.pallas_call(
        body, name="post_fwd", grid=(t // tb,), in_specs=[_row(tb, D), _row(tb, D, 6), _fixed((1, DH))],
        out_specs=_row(tb, D), out_shape=jax.ShapeDtypeStruct((t, D), bf16), compiler_params=_params(1),
    )(o, p1, gn)


def _post_bwd(dyb, o, p1, gn):
    t = o.shape[0]
    tb = _pick(t, (256, 128))

    def body(dyb_ref, o_ref, z_ref, gn_ref, do_ref, dz_ref, dgn_ref):
        @pl.when(pl.program_id(0) == 0)
        def _():
            dgn_ref[...] = jnp.zeros_like(dgn_ref)

        gn_v = gn_ref[...]
        acc = jnp.zeros((8, DH), f32)
        for h in range(H):
            sl = slice(h * DH, (h + 1) * DH)
            oh = o_ref[:, sl]
            z = z_ref[:, sl]
            dy = dyb_ref[:, sl]
            r = lax.rsqrt(jnp.mean(oh * oh, axis=-1, keepdims=True) + EPS)
            on = oh * r
            sg = _sigmoid(z)
            sz = z * sg
            don = dy * sz
            dz_ref[:, sl] = (dy * on * gn_v * (sg * (1.0 + z * (1.0 - sg)))).astype(bf16)
            acc = acc + jnp.sum((don * on).reshape(tb // 8, 8, DH), axis=0)
            doh = don * gn_v
            do_ref[:, sl] = r * (doh - on * jnp.mean(doh * on, axis=-1, keepdims=True))
        dgn_ref[...] += acc

    return pl.pallas_call(
        body, name="post_bwd", grid=(t // tb,),
        in_specs=[_row(tb, D), _row(tb, D), _row(tb, D, 6), _fixed((1, DH))],
        out_specs=[_row(tb, D), _row(tb, D), _fixed((8, DH))],
        out_shape=[jax.ShapeDtypeStruct((t, D), f32), jax.ShapeDtypeStruct((t, D), bf16),
                   jax.ShapeDtypeStruct((8, DH), f32)],
        compiler_params=_params(1),
    )(dyb, o, p1, gn)


def _mix_fwd(ya, yb, p1):
    t = ya.shape[0]
    tb = _pick(t, (256, 128))

    def body(ya_ref, yb_ref, ga_ref, gb_ref, mix_ref):
        mix_ref[...] = (_sigmoid(ga_ref[...]) * ya_ref[...] + _sigmoid(gb_ref[...]) * yb_ref[...]).astype(bf16)

    return pl.pallas_call(
        body, name="mix_fwd", grid=(t // tb,), in_specs=[_row(tb, D), _row(tb, D), _row(tb, D, 7), _row(tb, D, 8)],
        out_specs=_row(tb, D), out_shape=jax.ShapeDtypeStruct((t, D), bf16), compiler_params=_params(1),
    )(ya, yb, p1, p1)


def _mix_bwd(dmix, ya, yb, p1):
    t = ya.shape[0]
    tb = _pick(t, (256, 128))

    def body(dm_ref, ya_ref, yb_ref, ga_ref, gb_ref, dya_ref, dyb_ref, dg_ref):
        dm = dm_ref[...]
        sa = _sigmoid(ga_ref[...])
        sb = _sigmoid(gb_ref[...])
        dya_ref[...] = (dm * sa).astype(bf16)
        dyb_ref[...] = (dm * sb).astype(bf16)
        dg_ref[:, :D] = (dm * ya_ref[...] * sa * (1.0 - sa)).astype(bf16)
        dg_ref[:, D:] = (dm * yb_ref[...] * sb * (1.0 - sb)).astype(bf16)

    return pl.pallas_call(
        body, name="mix_bwd", grid=(t // tb,),
        in_specs=[_row(tb, D), _row(tb, D), _row(tb, D), _row(tb, D, 7), _row(tb, D, 8)],
        out_specs=[_row(tb, D), _row(tb, D), _row(tb, 2 * D)],
        out_shape=[jax.ShapeDtypeStruct((t, D), bf16), jax.ShapeDtypeStruct((t, D), bf16),
                   jax.ShapeDtypeStruct((t, 2 * D), bf16)],
        compiler_params=_params(1),
    )(dmix, ya, yb, p1, p1)


def _ffn_fwd(up, wf):
    t = up.shape[0]
    tb = 128

    def body(up_ref, uph_ref, wf_ref, act_ref):
        first = pl.program_id(0) == 0
        for i in range(DFF // 128):
            g, v = _strip(i), _strip(i, DFF)
            gate = _strip_conv(wf_ref, g, _strip_taps(_f32(up_ref, g), _halo_before(uph_ref, g), first, 3))
            val = _strip_conv(wf_ref, v, _strip_taps(_f32(up_ref, v), _halo_before(uph_ref, v), first, 3))
            act_ref[:, g] = (gate * _sigmoid(gate) * val).astype(bf16)

    return pl.pallas_call(
        body, name="ffn_fwd", grid=(t // tb,),
        in_specs=[_row(tb, 2 * DFF), _prev(tb, 2 * DFF, rows=16), _fixed((8, 2 * DFF))],
        out_specs=_row(tb, DFF), out_shape=jax.ShapeDtypeStruct((t, DFF), bf16), compiler_params=_params(1),
    )(up, up, wf)


def _ffn_bwd1(dact, up, wf):
    t = up.shape[0]
    tb = 128

    def body(da_ref, up_ref, uph_ref, wf_ref, dc_ref, dw_ref):
        @pl.when(pl.program_id(0) == 0)
        def _():
            dw_ref[...] = jnp.zeros_like(dw_ref)

        first = pl.program_id(0) == 0
        for i in range(DFF // 128):
            g, v = _strip(i), _strip(i, DFF)
            g_taps = _strip_taps(_f32(up_ref, g), _halo_before(uph_ref, g), first, 3)
            v_taps = _strip_taps(_f32(up_ref, v), _halo_before(uph_ref, v), first, 3)
            gate = _strip_conv(wf_ref, g, g_taps)
            val = _strip_conv(wf_ref, v, v_taps)
            sg = _sigmoid(gate)
            da = _f32(da_ref, g)
            dgate = da * val * (sg * (1.0 + gate * (1.0 - sg)))
            dval = da * (gate * sg)
            dc_ref[:, g] = dgate.astype(bf16)
            dc_ref[:, v] = dval.astype(bf16)
            _strip_weight_grad(dw_ref, g, dgate, g_taps)
            _strip_weight_grad(dw_ref, v, dval, v_taps)

    return pl.pallas_call(
        body, name="ffn_bwd1", grid=(t // tb,),
        in_specs=[_row(tb, DFF), _row(tb, 2 * DFF), _prev(tb, 2 * DFF, rows=16), _fixed((8, 2 * DFF))],
        out_specs=[_row(tb, 2 * DFF), _fixed((8, 2 * DFF))],
        out_shape=[jax.ShapeDtypeStruct((t, 2 * DFF), bf16), jax.ShapeDtypeStruct((8, 2 * DFF), f32)],
        compiler_params=_params(1),
    )(dact, up, up, wf)


def _ffn_bwd2(dc, wf):
    t = dc.shape[0]
    tb = 128
    nb = t // tb

    def body(dc_ref, dch_ref, wf_ref, dup_ref):
        last = pl.program_id(0) == nb - 1
        for i in range(2 * DFF // 128):
            sl = _strip(i)
            dup_ref[:, sl] = _strip_conv_up(_f32(dc_ref, sl), _halo_after(dch_ref, sl), last, wf_ref, sl, 3).astype(bf16)

    return pl.pallas_call(
        body, name="ffn_bwd2", grid=(nb,),
        in_specs=[_row(tb, 2 * DFF), _next(tb, 2 * DFF, t, rows=16), _fixed((8, 2 * DFF))],
        out_specs=_row(tb, 2 * DFF), out_shape=jax.ShapeDtypeStruct((t, 2 * DFF), bf16), compiler_params=_params(1),
    )(dc, dc, wf)


def _final(x3, tgt, g):
    t = x3.shape[0]
    tb = _pick(t, (256, 128))

    def body(x_ref, t_ref, g_ref, loss_ref, dx_ref, dxb_ref, dg_ref):
        @pl.when(pl.program_id(0) == 0)
        def _():
            loss_ref[...] = jnp.zeros_like(loss_ref)
            dg_ref[...] = jnp.zeros_like(dg_ref)

        xv = x_ref[...]
        r = lax.rsqrt(jnp.mean(xv * xv, axis=-1, keepdims=True) + EPS)
        xh = xv * r
        gv = g_ref[...]
        e = xh * gv - t_ref[...]
        lrow = 0.5 * jnp.mean(e * e, axis=-1, keepdims=True)
        loss_ref[...] += jnp.sum(jnp.broadcast_to(lrow, (tb, 128)).reshape(tb // 8, 8, 128), axis=0)
        dy = e * (1.0 / D)
        dyg = dy * gv
        dx = r * (dyg - xh * jnp.mean(dyg * xh, axis=-1, keepdims=True))
        dx_ref[...] = dx
        dxb_ref[...] = dx.astype(bf16)
        dg_ref[...] += jnp.sum((dy * xh).reshape(tb // 8, 8, D), axis=0)

    return pl.pallas_call(
        body, name="final", grid=(t // tb,), in_specs=[_row(tb, D), _row(tb, D), _fixed((1, D))],
        out_specs=[_fixed((8, 128)), _row(tb, D), _row(tb, D), _fixed((8, D))],
        out_shape=[jax.ShapeDtypeStruct((8, 128), f32), jax.ShapeDtypeStruct((t, D), f32),
                   jax.ShapeDtypeStruct((t, D), bf16), jax.ShapeDtypeStruct((8, D), f32)],
        compiler_params=_params(1),
    )(x3, tgt, g)


def _pre_bwd1(p1, p2, dya_in, dqn, dkn, dvc, dgb, gbeta, wa, wg, alog, dtb):
    t = p1.shape[0]
    tb = 128

    def body(p0_ref, p0h_ref, pq_ref, pqh_ref, p2_ref, dya_ref, dqn_ref, dkn_ref, dvc_ref, dgb_ref, gb_ref,
             wa_ref, wg_ref, alog_ref, dtb_ref,
             dbg_ref, dca_ref, dc4_ref, dp2_ref, dwa_ref, dwg_ref, dal_ref, ddt_ref):
        @pl.when(pl.program_id(0) == 0)
        def _():
            dwa_ref[...] = jnp.zeros_like(dwa_ref)
            dwg_ref[...] = jnp.zeros_like(dwg_ref)
            dal_ref[...] = jnp.zeros_like(dal_ref)
            ddt_ref[...] = jnp.zeros_like(ddt_ref)

        first = pl.program_id(0) == 0

        for i in range(D // 128):
            sl, cg, xv = _strip(i), _strip(i, D), _strip(i, 2 * D)
            taps = _strip_taps(p0_ref[:, cg] * p0_ref[:, xv], p0h_ref[:, cg] * p0h_ref[:, xv], first, 3)
            dya = dya_ref[:, sl]
            dbg_ref[:, sl] = (dya * _strip_conv(wa_ref, sl, taps)).astype(bf16)
            dca = dya * p0_ref[:, sl]
            dca_ref[:, sl] = dca.astype(bf16)
            _strip_weight_grad(dwa_ref, sl, dca, taps)

        for part, d_ref, scale in ((0, dqn_ref, DH ** -0.5), (1, dkn_ref, 1.0), (2, dvc_ref, None)):
            for h in range(H):
                sl = _strip(h, part * D)
                taps = _strip_taps(pq_ref[:, sl], pqh_ref[:, sl], first, 4)
                c4 = _strip_conv(wg_ref, sl, taps)
                sg = _sigmoid(c4)
                dn = d_ref[:, _strip(h)]
                if scale is not None:
                    a = c4 * sg
                    r = lax.rsqrt(jnp.sum(a * a, axis=-1, keepdims=True) + EPS)
                    an = a * r
                    dn = dn * scale
                    dn = r * (dn - an * jnp.sum(dn * an, axis=-1, keepdims=True))
                dc4 = dn * (sg * (1.0 + c4 * (1.0 - sg)))
                dc4_ref[:, sl] = dc4.astype(bf16)
                _strip_weight_grad(dwg_ref, sl, dc4, taps)

        ab = p2_ref[...]
        lane = lax.broadcasted_iota(jnp.int32, ab.shape, 1)
        dgbv = dgb_ref[...]
        gbv = gb_ref[...]
        da = dgbv * (-jnp.exp(alog_ref[...])) * _sigmoid(ab + dtb_ref[...])
        db = dgbv * gbv * (1.0 - gbv)
        dp2_ref[...] = jnp.where(lane < H, da, jnp.where(lane < 2 * H, db, 0.0)).astype(bf16)
        dal = jnp.where(lane < H, dgbv * gbv, 0.0)
        ddt = jnp.where(lane < H, da, 0.0)
        dal_ref[...] += jnp.sum(dal.reshape(tb // 8, 8, 128), axis=0)
        ddt_ref[...] += jnp.sum(ddt.reshape(tb // 8, 8, 128), axis=0)

    return pl.pallas_call(
        body, name="pre_bwd1", grid=(t // tb,),
        in_specs=[_row(tb, 3 * D, 0), _prev(tb, 3 * D, 0), _row(tb, 3 * D, 1), _prev(tb, 3 * D, 1), _row(tb, 128),
                  _row(tb, D), _row(tb, D), _row(tb, D), _row(tb, D), _row(tb, 128), _row(tb, 128),
                  _fixed((8, D)), _fixed((8, 3 * D)), _fixed((1, 128)), _fixed((1, 128))],
        out_specs=[_row(tb, D), _row(tb, D), _row(tb, 3 * D), _row(tb, 128),
                   _fixed((8, D)), _fixed((8, 3 * D)), _fixed((8, 128)), _fixed((8, 128))],
        out_shape=[jax.ShapeDtypeStruct((t, D), bf16), jax.ShapeDtypeStruct((t, D), bf16),
                   jax.ShapeDtypeStruct((t, 3 * D), bf16), jax.ShapeDtypeStruct((t, 128), bf16),
                   jax.ShapeDtypeStruct((8, D), f32), jax.ShapeDtypeStruct((8, 3 * D), f32),
                   jax.ShapeDtypeStruct((8, 128), f32), jax.ShapeDtypeStruct((8, 128), f32)],
        compiler_params=_params(1),
    )(p1, p1, p1, p1, p2, dya_in, dqn, dkn, dvc, dgb, gbeta, wa, wg, alog, dtb)


def _pre_bwd2(dca, dc4, p1, dbg, dz, dgates, wa, wg, exchange=None):
    t = p1.shape[0]
    tb = 128
    nb = t // tb

    def body(dca_ref, dcah_ref, dc4_ref, dc4h_ref, p0_ref, dbg_ref, dz_ref, dgt_ref, wa_ref, wg_ref, dp_ref):
        last = pl.program_id(0) == nb - 1
        dp_ref[:, :D] = dbg_ref[...]
        for i in range(D // 128):
            sl, cg, xv = _strip(i), _strip(i, D), _strip(i, 2 * D)
            du = _strip_conv_up(_f32(dca_ref, sl), _halo_after(dcah_ref, sl), last, wa_ref, sl, 3)
            dp_ref[:, cg] = (du * p0_ref[:, xv]).astype(bf16)
            dp_ref[:, xv] = (du * p0_ref[:, cg]).astype(bf16)
        for i in range(3 * D // 128):
            sl = _strip(i)
            dq = _strip_conv_up(_f32(dc4_ref, sl), _halo_after(dc4h_ref, sl), last, wg_ref, sl, 4)
            dp_ref[:, _strip(i, 3 * D)] = dq.astype(bf16)
        dp_ref[:, 6 * D:7 * D] = dz_ref[...]
        dp_ref[:, 7 * D:] = dgt_ref[...]

    return _call_with_exchange(
        body, exchange, name="pre_bwd2", grid=(nb,),
        in_specs=[_row(tb, D), _next(tb, D, t, rows=16), _row(tb, 3 * D), _next(tb, 3 * D, t, rows=16), _row(tb, 3 * D, 0),
                  _row(tb, D), _row(tb, D), _row(tb, 2 * D), _fixed((8, D)), _fixed((8, 3 * D))],
        out_specs=_row(tb, NW1), out_shape=jax.ShapeDtypeStruct((t, NW1), bf16),
        args=[dca, dca, dc4, dc4, p1, dbg, dz, dgates, wa, wg])


def _chunk_consts():
    r = lax.broadcasted_iota(jnp.int32, (CH, CH), 0)
    c = lax.broadcasted_iota(jnp.int32, (CH, CH), 1)
    return r, c, (r == c).astype(f32)


def _tri_inverse(lows, eye, r, c):
    def same_block(b):
        return jnp.bitwise_xor(r, c) < b

    xs = [jnp.where(same_block(8), -low, 0.0) for low in lows]
    ts = [eye + x for x in xs]
    for _ in range(2):
        xs = [_idot(x, x) for x in xs]
        ts = [t + _idot(t, x) for t, x in zip(ts, xs)]
    for b in (8, 16, 32):
        below = same_block(2 * b) & jnp.logical_not(same_block(b))
        ts = [t - _idot(_idot(t, jnp.where(below, low, 0.0)), t) for t, low in zip(ts, lows)]
    return ts


def _chunk_common(q, k, v, gcol, bcol, r, c, eye):
    grow = jnp.sum(eye * gcol, axis=0, keepdims=True)
    dec = jnp.exp(jnp.where(r >= c, gcol - grow, -jnp.inf))
    rcol = lax.broadcasted_iota(jnp.int32, (CH, 1), 0)
    glast = jnp.sum(jnp.where(rcol == CH - 1, gcol, 0.0), axis=0, keepdims=True)
    eg = jnp.exp(gcol)
    el = jnp.exp(glast - gcol)
    kb = k * bcol
    vb = v * bcol
    kk = _bdot_nt(kb, k)
    low = jnp.where(r > c, kk * dec, 0.0)
    qk = _bdot_nt(q, k)
    att = qk * dec
    return grow, dec, glast, eg, el, kb, vb, kk, low, qk, att, rcol


def _gdn_fwd(qn, kn, vc, gbeta):
    t = qn.shape[0]
    n_chunks = t // CH

    def body(q_ref, k_ref, v_ref, gb_ref, o_ref, s_ref, t_ref, state):
        @pl.when(pl.program_id(0) == 0)
        def _():
            state[...] = jnp.zeros_like(state)

        r, c, eye = _chunk_consts()
        tri = (r >= c).astype(f32)
        heads = range(H)
        keys = [(s, h) for s in range(GDN_STEP) for h in heads]
        rows = [slice(s * CH, (s + 1) * CH) for s in range(GDN_STEP)]
        gbs = [gb_ref[rows[s], :] for s in range(GDN_STEP)]
        galls = [_hdot(tri, gb) for gb in gbs]
        qs = {(s, h): q_ref[rows[s], h * DH:(h + 1) * DH] for s, h in keys}
        ks = {(s, h): k_ref[rows[s], h * DH:(h + 1) * DH] for s, h in keys}
        cm = {(s, h): _chunk_common(qs[s, h], ks[s, h], v_ref[rows[s], h * DH:(h + 1) * DH], galls[s][:, h:h + 1],
                                    gbs[s][:, H + h:H + h + 1], r, c, eye) for s, h in keys}
        invs = dict(zip(keys, _tri_inverse([cm[key][8] for key in keys], eye, r, c)))
        uws = {key: _bdot(invs[key], jnp.concatenate([cm[key][6], cm[key][5] * cm[key][3]], axis=1)) for key in keys}
        sts = [state[h] for h in heads]
        for s in range(GDN_STEP):
            vns = [uws[s, h][:, :DH] - _bdot(uws[s, h][:, DH:], sts[h]) for h in heads]
            outs = [_bdot(qs[s, h] * cm[s, h][3], sts[h]) + _bdot(cm[s, h][10], vns[h]) for h in heads]
            news = [sts[h] * jnp.exp(cm[s, h][2]) + _bdot_tn(ks[s, h] * cm[s, h][4], vns[h]) for h in heads]
            for h in heads:
                s_ref[s, h] = sts[h].astype(bf16)
                t_ref[s, h] = invs[s, h]
                o_ref[rows[s], h * DH:(h + 1) * DH] = outs[h]
            sts = news
        for h in heads:
            state[h] = sts[h]

    tb = GDN_STEP * CH
    return pl.pallas_call(
        body, name="gdn_fwd", grid=(t // tb,),
        in_specs=[_row(tb, D), _row(tb, D), _row(tb, D), _row(tb, 128)],
        out_specs=[_row(tb, D), pl.BlockSpec((GDN_STEP, H, DH, DH), lambda i: (i, 0, 0, 0)),
                   pl.BlockSpec((GDN_STEP, H, CH, CH), lambda i: (i, 0, 0, 0))],
        out_shape=[jax.ShapeDtypeStruct((t, D), f32), jax.ShapeDtypeStruct((n_chunks, H, DH, DH), bf16),
                   jax.ShapeDtypeStruct((n_chunks, H, CH, CH), f32)],
        scratch_shapes=[pltpu.VMEM((H, DH, DH), f32)],
        compiler_params=_params(1),
    )(qn, kn, vc, gbeta)


def _gdn_bwd(qn, kn, vc, gbeta, do, s_all, t_all):
    t = qn.shape[0]

    def body(q_ref, k_ref, v_ref, gb_ref, do_ref, s_ref, t_ref, dq_ref, dk_ref, dv_ref, dgb_ref, dstate):
        @pl.when(pl.program_id(0) == 0)
        def _():
            dstate[...] = jnp.zeros_like(dstate)

        r, c, eye = _chunk_consts()
        tril = r >= c
        lane = lax.broadcasted_iota(jnp.int32, (1, 128), 1)
        hs = range(H)

        def each(fn, *lists):
            return [fn(*args) for args in zip(*lists)]

        def rsum(a):
            return jnp.sum(a, axis=1, keepdims=True)

        def before_state(s):
            rows = slice(s * CH, (s + 1) * CH)
            gb = gb_ref[rows, :]
            gall = _hdot(tril.astype(f32), gb)
            p = {"rows": rows}
            p["q"] = q = [q_ref[rows, h * DH:(h + 1) * DH] for h in hs]
            p["k"] = k = [k_ref[rows, h * DH:(h + 1) * DH] for h in hs]
            p["v"] = v = [v_ref[rows, h * DH:(h + 1) * DH] for h in hs]
            p["dout"] = dout = [do_ref[rows, h * DH:(h + 1) * DH] for h in hs]
            p["inv"] = inv = [t_ref[s, h] for h in hs]
            p["st"] = st = [s_ref[s, h] for h in hs]
            p["bcol"] = bcol = [gb[:, H + h:H + h + 1] for h in hs]
            cm = [_chunk_common(q[h], k[h], v[h], gall[:, h:h + 1], bcol[h], r, c, eye) for h in hs]
            for name, i in (("dec", 1), ("glast", 2), ("eg", 3), ("el", 4), ("kb", 5), ("vb", 6), ("low", 8), ("att", 10)):
                p[name] = [m[i] for m in cm]
            p["rcol"] = cm[0][11]
            p["elast"] = each(jnp.exp, p["glast"])
            p["kbg"] = each(jnp.multiply, p["kb"], p["eg"])
            uw = each(lambda i, a, b: _bdot(i, jnp.concatenate([a, b], axis=1)), inv, p["vb"], p["kbg"])
            p["u"] = [a[:, :DH] for a in uw]
            p["w"] = [a[:, DH:] for a in uw]
            p["vn"] = each(lambda a, b, x: a - _bdot(b, x), p["u"], p["w"], st)
            p["qd"] = each(jnp.multiply, q, p["eg"])
            p["kd"] = each(jnp.multiply, k, p["el"])
            p["dqd"] = each(_bdot_nt, dout, st)
            p["datt"] = each(lambda d, x: jnp.where(tril, _bdot_nt(d, x), 0.0), dout, p["vn"])
            p["dqk"] = each(jnp.multiply, p["datt"], p["dec"])
            p["qd_do"] = each(_bdot_tn, p["qd"], dout)
            p["att_do"] = each(_bdot_tn, p["att"], dout)
            return p

        def after_state(p, ds):
            q, k, v, st, inv, bcol = p["q"], p["k"], p["v"], p["st"], p["inv"], p["bcol"]
            eg, el, kb, u, w = p["eg"], p["el"], p["kb"], p["u"], p["w"]
            dvn = each(lambda a, kk, x: a + _bdot(kk, x), p["att_do"], p["kd"], ds)
            dkd = each(_bdot_nt, p["vn"], ds)
            dw = each(lambda a, x: -_bdot_nt(a, x), dvn, st)
            new_ds = each(lambda x, e, a, ww, dv_: x * e + a - _bdot_tn(ww, dv_), ds, p["elast"], p["qd_do"], w, dvn)
            dglast = each(lambda e, x, d: e * jnp.sum(rsum(x.astype(f32) * d), axis=0, keepdims=True), p["elast"], st, ds)
            dr = each(lambda i, a, b: _bdot_tn(i, jnp.concatenate([a, b], axis=1)), inv, dvn, dw)
            dvb = [a[:, :DH] for a in dr]
            dkbg = [a[:, DH:] for a in dr]
            dlow = each(lambda a, b, x, y: -jnp.where(r > c, _bdot_nt(a, b) + _bdot_nt(x, y), 0.0), dvb, u, dkbg, w)
            dkk = each(jnp.multiply, dlow, p["dec"])
            mm = each(lambda a, b, x, y: a * b + x * y, dlow, p["low"], p["datt"], p["att"])
            dkb = each(lambda a, kk, b, e: _bdot(a, kk) + b * e, dkk, k, dkbg, eg)
            dk = each(lambda a, b, x, y, d, e, f, g: _bdot_tn(a, b) + _bdot_tn(x, y) + d * e + f * g,
                      dkk, kb, p["dqk"], q, dkd, el, dkb, bcol)
            dq = each(lambda a, kk, d, e: _bdot(a, kk) + d * e, p["dqk"], k, p["dqd"], eg)
            dv = each(jnp.multiply, dvb, bcol)
            dbeta = each(lambda a, b, x, y: rsum(a * b) + rsum(x * y), dkb, k, dvb, v)
            deg = each(lambda a, b, x, y: rsum(a * b) + rsum(x * y), dkbg, kb, p["dqd"], q)
            delc = each(lambda a, b, e: rsum(a * b) * e, dkd, k, el)
            dgc = each(lambda m, a, e, d: rsum(m) - rsum(eye * jnp.sum(m, axis=0, keepdims=True)) + a * e - d,
                       mm, deg, eg, delc)
            dgc = each(lambda g, d, l: g + jnp.where(p["rcol"] == CH - 1, jnp.sum(d, axis=0, keepdims=True) + l, 0.0),
                       dgc, delc, dglast)
            dg_acc = jnp.zeros((CH, 128), f32)
            db_acc = jnp.zeros((CH, 128), f32)
            rows = p["rows"]
            for h in hs:
                dq_ref[rows, h * DH:(h + 1) * DH] = dq[h]
                dk_ref[rows, h * DH:(h + 1) * DH] = dk[h]
                dv_ref[rows, h * DH:(h + 1) * DH] = dv[h]
                dg_acc = dg_acc + dgc[h] * (lane == h).astype(f32)
                db_acc = db_acc + dbeta[h] * (lane == H + h).astype(f32)
            dgb_ref[rows, :] = _hdot((r <= c).astype(f32), dg_acc) + db_acc
            return new_ds

        order = list(reversed(range(GDN_STEP)))
        pre = [before_state(s) for s in order]
        ds = [dstate[h] for h in hs]
        for p in pre:
            ds = after_state(p, ds)
        for h in hs:
            dstate[h] = ds[h]

    tb = GDN_STEP * CH
    n_steps = t // tb
    rev = lambda i: (n_steps - 1 - i, 0)
    rev4 = lambda i: (n_steps - 1 - i, 0, 0, 0)
    return pl.pallas_call(
        body, name="gdn_bwd", grid=(n_steps,),
        in_specs=[pl.BlockSpec((tb, D), rev), pl.BlockSpec((tb, D), rev), pl.BlockSpec((tb, D), rev),
                  pl.BlockSpec((tb, 128), rev), pl.BlockSpec((tb, D), rev),
                  pl.BlockSpec((GDN_STEP, H, DH, DH), rev4), pl.BlockSpec((GDN_STEP, H, CH, CH), rev4)],
        out_specs=[pl.BlockSpec((tb, D), rev), pl.BlockSpec((tb, D), rev), pl.BlockSpec((tb, D), rev),
                   pl.BlockSpec((tb, 128), rev)],
        out_shape=[jax.ShapeDtypeStruct((t, D), f32)] * 3 + [jax.ShapeDtypeStruct((t, 128), f32)],
        scratch_shapes=[pltpu.VMEM((H, DH, DH), f32)],
        compiler_params=_params(1),
    )(qn, kn, vc, gbeta, do, s_all, t_all)


def _pad_rows(w, rows=8):
    return jnp.pad(w, ((0, rows - w.shape[0]), (0, 0)))


_REST = ("w_up", "w_a_out", "w_b_out", "w_o", "w_down")


def _local_step(x, tgt, w, comm=None):
    g1 = w["norm_mix_g"].reshape(1, D)
    if comm is None:
        h1 = _rms_fwd(x, g1, name="rms1_fwd")
    else:
        h1, gathered = _rms_fwd(x, g1, name="rms1_fwd", exchange=comm.gather_first())
        w = {**w, **comm.finish_first(gathered)}
    w1, w2 = w["w1"], w["w2"]
    wa = _pad_rows(w["conv_a_w"])
    wg = _pad_rows(w["gdn_conv_w"])
    wf = _pad_rows(w["ffn_conv_w"])
    alog = jnp.pad(w["gdn_A_log"].reshape(1, H), ((0, 0), (0, 128 - H)))
    dtb = jnp.pad(w["gdn_dt_bias"].reshape(1, H), ((0, 0), (0, 128 - H)))
    g2 = w["norm_ffn_g"].reshape(1, D)
    g3 = w["norm_final_g"].reshape(1, D)
    gn = w["gdn_norm_g"].reshape(1, DH)

    if comm is None:
        p1 = _matmul(h1, w1, name="mm_in")
    else:
        p1, gathered = _matmul(h1, w1, name="mm_in", exchange=comm.gather_rest())
        w = {**w, **comm.finish_gather(gathered)}
    p2 = _matmul(h1, w2, name="mm_in_ab")
    ya_in, qn, kn, vc, gbeta = _pre_fwd(p1, p2, wa, wg, alog, dtb)
    o, s_all, t_all = _gdn_fwd(qn, kn, vc, gbeta)
    yb_in = _post_fwd(o, p1, gn)
    ya = _matmul(ya_in, w["w_a_out"], name="mm_a", out_dtype=bf16)
    yb = _matmul(yb_in, w["w_b_out"], name="mm_b", out_dtype=bf16)
    mix = _mix_fwd(ya, yb, p1)
    x2 = _matmul(mix, w["w_o"], name="mm_o", add=x)
    h2 = _rms_fwd(x2, g2, name="rms2_fwd")
    up = _matmul(h2, w["w_up"], name="mm_up", tn=DFF // 2, out_dtype=bf16)
    act = _ffn_fwd(up, wf)
    x3 = _matmul(act, w["w_down"], name="mm_down", add=x2, tm=512)
    loss_p, dx3, dx3b, dg3 = _final(x3, tgt, g3)

    grads = {"norm_final_g": dg3}
    dact = _matmul(dx3b, w["w_down"], nt=True, name="mm_down_dx", tm=512, tn=DFF, out_dtype=bf16)
    grads["w_down"] = _matmul_tn(act, dx3b, name="mm_down_dw", tm=DFF // 2)
    dc, dwf = _ffn_bwd1(dact, up, wf)
    grads["ffn_conv_w"] = dwf
    dup = _ffn_bwd2(dc, wf)
    dh2 = _matmul(dup, w["w_up"], nt=True, name="mm_up_dx", tk=DFF)
    grads["w_up"] = _matmul_tn(h2, dup, name="mm_up_dw", tn=512)
    dx2, dx2b, dg2 = _rms_bwd(dh2, x2, g2, dx3, name="rms2_bwd")
    grads["norm_ffn_g"] = dg2
    dmix = _matmul(dx2b, w["w_o"], nt=True, name="mm_o_dx", out_dtype=bf16)
    grads["w_o"] = _matmul_tn(mix, dx2b, name="mm_o_dw")
    dya, dyb, dgates = _mix_bwd(dmix, ya, yb, p1)
    dya_in = _matmul(dya, w["w_a_out"], nt=True, name="mm_a_dx", out_dtype=bf16)
    grads["w_a_out"] = _matmul_tn(ya_in, dya, name="mm_a_dw")
    dyb_in = _matmul(dyb, w["w_b_out"], nt=True, name="mm_b_dx")
    grads["w_b_out"] = _matmul_tn(yb_in, dyb, name="mm_b_dw")
    do, dz, dgn = _post_bwd(dyb_in, o, p1, gn)
    grads["gdn_norm_g"] = dgn
    dqn, dkn, dvc, dgb = _gdn_bwd(qn, kn, vc, gbeta, do, s_all, t_all)
    dbg, dca, dc4, dp2, dwa, dwg, dal, ddt = _pre_bwd1(p1, p2, dya_in, dqn, dkn, dvc, dgb, gbeta, wa, wg, alog, dtb)
    grads["conv_a_w"] = dwa
    grads["gdn_conv_w"] = dwg
    grads["gdn_A_log"] = dal
    grads["gdn_dt_bias"] = ddt
    grads["w2"] = _matmul_tn(h1, dp2, name="mm_in_ab_dw")
    if comm is None:
        dp1 = _pre_bwd2(dca, dc4, p1, dbg, dz, dgates, wa, wg)
        grads["w1"] = _matmul_tn(h1, dp1, name="mm_in_dw")
        dh1 = _matmul(dp1, w1, nt=True, name="mm_in_dx", tm=512, tk=NW1 // 2)
    else:
        exchange, blocks = comm.reduce_halves(_REST, grads)
        dp1, recv = _pre_bwd2(dca, dc4, p1, dbg, dz, dgates, wa, wg, exchange=exchange)
        exchange, sums = comm.reduce_sums(_REST, blocks, recv)
        grads["w1"], recv = _matmul_tn(h1, dp1, name="mm_in_dw", exchange=exchange)
        comm.finish_reduce(_REST, sums, recv)
        exchange, blocks = comm.reduce_halves(("w_in",), grads)
        exchange, sums = comm.reduce_sums(("w_in",), blocks, _run_exchange(exchange, name="rs_sibling_w_in"))
        dh1, recv = _matmul(dp1, w1, nt=True, name="mm_in_dx", tm=512, tk=NW1 // 2, exchange=exchange)
        comm.finish_reduce(("w_in",), sums, recv)
    dh1 = _matmul(dp2, w2, nt=True, name="mm_in_ab_dx", add=dh1)
    dx, _, dg1 = _rms_bwd(dh1, x, g1, dx2, name="rms1_bwd")
    grads["norm_mix_g"] = dg1
    return loss_p, dx, grads


_ANY = pl.BlockSpec(memory_space=pl.ANY)


def _remote(src, dst, send_sem, recv_sem, to):
    return pltpu.make_async_remote_copy(src_ref=src, dst_ref=dst, send_sem=send_sem, recv_sem=recv_sem,
                                        device_id=to, device_id_type=MESH)


def _run_exchange(exchange, *, name):
    arrays, shapes, sems, start, wait = exchange
    n_in, n_out = len(arrays), len(shapes)

    def body(*refs):
        start(refs[:n_in], refs[n_in:n_in + n_out], refs[n_in + n_out:])
        wait(refs[:n_in], refs[n_in:n_in + n_out], refs[n_in + n_out:])

    return pl.pallas_call(body, name=name, out_shape=list(shapes), in_specs=[_ANY] * n_in, out_specs=[_ANY] * n_out,
                          scratch_shapes=list(sems))(*arrays)


def _gather_exchange(shards):
    n = len(shards)

    def copies(x_refs, out_refs, sems):
        send_sems, recv_sems, local_sems = sems
        x, y, c = lax.axis_index("x"), lax.axis_index("y"), lax.axis_index("c")
        me, sibling = (x, y, c), (x, y, 1 - c)
        chips = [(1 - x, y), (x, 1 - y), (1 - x, 1 - y)]

        def copy(a, k, blk, to, from_input=False):
            dst = out_refs[a].at[4 * blk[0] + 2 * blk[1] + blk[2]]
            return _remote(x_refs[a] if from_input else dst, dst, send_sems.at[a, k], recv_sems.at[a, k], to)

        mine = [pltpu.make_async_copy(x_refs[a], out_refs[a].at[4 * x + 2 * y + c], local_sems.at[a]) for a in range(n)]
        first = []
        for a in range(n):
            first.append(copy(a, 0, me, sibling, from_input=True))
            first += [copy(a, 1 + j, me, (*chip, c), from_input=True) for j, chip in enumerate(chips)]
        return copy, mine, first, me, sibling, chips, c

    def start(x_refs, out_refs, sems):
        _, mine, first, *_ = copies(x_refs, out_refs, sems)
        for cp in mine + first:
            cp.start()

    def wait(x_refs, out_refs, sems):
        copy, mine, first, me, sibling, chips, c = copies(x_refs, out_refs, sems)
        passed = []
        for j, chip in enumerate(chips):
            for a in range(n):
                copy(a, 1 + j, (*chip, c), me).wait_recv()
                passed.append(copy(a, 4 + j, (*chip, c), sibling))
                passed[-1].start()
        for a in range(n):
            copy(a, 0, sibling, me).wait_recv()
            for j, chip in enumerate(chips):
                copy(a, 4 + j, (*chip, 1 - c), me).wait_recv()
        for cp in first + passed:
            cp.wait_send()
        for cp in mine:
            cp.wait()

    shapes = [jax.ShapeDtypeStruct((N_DEV, *s.shape), s.dtype) for s in shards]
    sems = [pltpu.SemaphoreType.DMA((n, 7)), pltpu.SemaphoreType.DMA((n, 7)), pltpu.SemaphoreType.DMA((n,))]
    return shards, shapes, sems, start, wait


def _gather_direct_exchange(shards):
    n = len(shards)

    def copies(x_refs, out_refs, sems):
        send_sems, recv_sems, local_sems = sems
        x, y, c = lax.axis_index("x"), lax.axis_index("y"), lax.axis_index("c")
        targets = [(x, y, 1 - c), (1 - x, y, c), (x, 1 - y, c), (1 - x, 1 - y, c)]
        local, sends, recvs = [], [], []
        for a in range(n):
            mine = out_refs[a].at[4 * x + 2 * y + c]
            local.append(pltpu.make_async_copy(x_refs[a], mine, local_sems.at[a]))
            for k, to in enumerate(targets):
                theirs = out_refs[a].at[4 * to[0] + 2 * to[1] + to[2]]
                sends.append(_remote(x_refs[a], mine, send_sems.at[a, k], recv_sems.at[a, k], to))
                recvs.append(_remote(theirs, theirs, send_sems.at[a, k], recv_sems.at[a, k], to))
        return local, sends, recvs

    def start(x_refs, out_refs, sems):
        local, sends, _ = copies(x_refs, out_refs, sems)
        for cp in local + sends:
            cp.start()

    def wait(x_refs, out_refs, sems):
        local, sends, recvs = copies(x_refs, out_refs, sems)
        for cp in recvs:
            cp.wait_recv()
        for cp in sends:
            cp.wait_send()
        for cp in local:
            cp.wait()

    shapes = [jax.ShapeDtypeStruct((N_DEV, *s.shape), s.dtype) for s in shards]
    sems = [pltpu.SemaphoreType.DMA((n, 4)), pltpu.SemaphoreType.DMA((n, 4)), pltpu.SemaphoreType.DMA((n,))]
    return shards, shapes, sems, start, wait


def _gather_forward(gathered):
    n = len(gathered)

    def body(*refs):
        out_refs = refs[n:2 * n]
        send_sems, recv_sems = refs[2 * n:]
        x, y, c = lax.axis_index("x"), lax.axis_index("y"), lax.axis_index("c")
        sibling = (x, y, 1 - c)
        sends, recvs = [], []
        for a in range(n):
            for j, (px, py) in enumerate([(1 - x, y), (x, 1 - y), (1 - x, 1 - y)]):
                mine = out_refs[a].at[4 * px + 2 * py + c]
                theirs = out_refs[a].at[4 * px + 2 * py + 1 - c]
                sends.append(_remote(mine, mine, send_sems.at[a, j], recv_sems.at[a, j], sibling))
                recvs.append(_remote(theirs, theirs, send_sems.at[a, j], recv_sems.at[a, j], sibling))
        for cp in sends:
            cp.start()
        for cp in recvs:
            cp.wait_recv()
        for cp in sends:
            cp.wait_send()

    return pl.pallas_call(
        body, name="ag_forward", out_shape=[jax.ShapeDtypeStruct(g.shape, g.dtype) for g in gathered],
        in_specs=[_ANY] * n, out_specs=[_ANY] * n, input_output_aliases={a: a for a in range(n)},
        scratch_shapes=[pltpu.SemaphoreType.DMA((n, 3)), pltpu.SemaphoreType.DMA((n, 3))],
    )(*gathered)


def _chips_exchange(hsums):
    n = len(hsums)

    def copies(h_refs, out_refs, sems):
        send_sems, recv_sems = sems
        x, y, c = lax.axis_index("x"), lax.axis_index("y"), lax.axis_index("c")
        chips = [(1 - x, y), (x, 1 - y), (1 - x, 1 - y)]
        return [_remote(h_refs[a].at[2 * px + py], out_refs[a].at[k], send_sems.at[a, k], recv_sems.at[a, k], (px, py, c))
                for a in range(n) for k, (px, py) in enumerate(chips)]

    def start(h_refs, out_refs, sems):
        for cp in copies(h_refs, out_refs, sems):
            cp.start()

    def wait(h_refs, out_refs, sems):
        for cp in copies(h_refs, out_refs, sems):
            cp.wait()

    shapes = [jax.ShapeDtypeStruct((3, *h.shape[1:]), h.dtype) for h in hsums]
    sems = [pltpu.SemaphoreType.DMA((n, 3)), pltpu.SemaphoreType.DMA((n, 3))]
    return hsums, shapes, sems, start, wait


def _sibling_exchange(halves):
    n = len(halves)

    def copies(p_refs, out_refs, sems):
        send_sems, recv_sems = sems
        x, y, c = lax.axis_index("x"), lax.axis_index("y"), lax.axis_index("c")
        return [_remote(p_refs[a], out_refs[a], send_sems.at[a], recv_sems.at[a], (x, y, 1 - c)) for a in range(n)]

    def start(p_refs, out_refs, sems):
        for cp in copies(p_refs, out_refs, sems):
            cp.start()

    def wait(p_refs, out_refs, sems):
        for cp in copies(p_refs, out_refs, sems):
            cp.wait()

    shapes = [jax.ShapeDtypeStruct(h.shape, h.dtype) for h in halves]
    return halves, shapes, [pltpu.SemaphoreType.DMA((n,)), pltpu.SemaphoreType.DMA((n,))], start, wait


_IN_RANGES = ((0, 7 * D, 0, 0), (7 * D, 7 * D + 16, 1, 0), (7 * D + 16, 9 * D + 16, 0, 7 * D))
_UP_RANGES = ((0, 2 * DFF, 0, 0),)


def _col_pieces(width, ranges):
    pieces = []
    for d in range(N_DEV):
        lo, hi = d * width, (d + 1) * width
        for glo, ghi, mat, mlo in ranges:
            a, b = max(lo, glo), min(hi, ghi)
            if a < b:
                pieces.append((d, a - lo, b - lo, mat, mlo + a - glo))
    return pieces


def _cols_to_matrices(g, ranges, out_widths, *, name):
    _, rows, width = g.shape
    tb = 128
    pieces = _col_pieces(width, ranges)
    covered = [sum(p[2] - p[1] for p in pieces if p[3] == m) for m in range(len(out_widths))]

    def body(g_ref, *o_refs):
        for m, o_ref in enumerate(o_refs):
            if covered[m] < out_widths[m]:
                o_ref[...] = jnp.zeros_like(o_ref)
        for d, b0, b1, m, m0 in pieces:
            o_refs[m][:, m0:m0 + b1 - b0] = g_ref[d, :, b0:b1]

    return pl.pallas_call(
        body, name=name, grid=(rows // tb,), in_specs=[pl.BlockSpec((N_DEV, tb, width), lambda i: (0, i, 0))],
        out_specs=[pl.BlockSpec((tb, wo), lambda i: (i, 0)) for wo in out_widths],
        out_shape=[jax.ShapeDtypeStruct((rows, wo), g.dtype) for wo in out_widths], compiler_params=_params(1),
    )(g)


def _matrices_to_cols(mats, ranges, width, *, name):
    rows = mats[0].shape[0]
    tb = 128
    pieces = _col_pieces(width, ranges)

    def body(*refs):
        m_refs, g_ref = refs[:-1], refs[-1]
        for d, b0, b1, m, m0 in pieces:
            g_ref[d, :, b0:b1] = m_refs[m][:, m0:m0 + b1 - b0]

    return pl.pallas_call(
        body, name=name, grid=(rows // tb,),
        in_specs=[pl.BlockSpec((tb, mt.shape[1]), lambda i: (i, 0)) for mt in mats],
        out_specs=pl.BlockSpec((N_DEV, tb, width), lambda i: (0, i, 0)),
        out_shape=jax.ShapeDtypeStruct((N_DEV, rows, width), mats[0].dtype), compiler_params=_params(1),
    )(*mats)


def _row_block(rows):
    return 128 if rows % 128 == 0 else rows


def _half_bf16(g4, c_other, *, name):
    _, _, rows, width = g4.shape
    tb = _row_block(rows)

    def body(c_ref, p_ref, o_ref):
        o_ref[0] = p_ref[0, 0].astype(bf16)

    grid_spec = pltpu.PrefetchScalarGridSpec(
        num_scalar_prefetch=1, grid=(4, rows // tb),
        in_specs=[pl.BlockSpec((1, 1, tb, width), lambda j, i, c_ref: (j, c_ref[0], i, 0))],
        out_specs=pl.BlockSpec((1, tb, width), lambda j, i, c_ref: (j, i, 0)))
    return pl.pallas_call(
        body, name=name, grid_spec=grid_spec, out_shape=jax.ShapeDtypeStruct((4, rows, width), bf16),
        compiler_params=_params(2),
    )(c_other, g4)


def _pair_sum(g4, recv, c_me, *, name):
    _, _, rows, width = g4.shape
    tb = _row_block(rows)

    def body(c_ref, p_ref, r_ref, o_ref, ob_ref):
        s = p_ref[0, 0] + r_ref[0].astype(f32)
        o_ref[0] = s
        ob_ref[0] = s.astype(bf16)

    blk = pl.BlockSpec((1, tb, width), lambda j, i, c_ref: (j, i, 0))
    grid_spec = pltpu.PrefetchScalarGridSpec(
        num_scalar_prefetch=1, grid=(4, rows // tb),
        in_specs=[pl.BlockSpec((1, 1, tb, width), lambda j, i, c_ref: (j, c_ref[0], i, 0)), blk],
        out_specs=[blk, blk])
    return pl.pallas_call(
        body, name=name, grid_spec=grid_spec,
        out_shape=[jax.ShapeDtypeStruct((4, rows, width), f32), jax.ShapeDtypeStruct((4, rows, width), bf16)],
        compiler_params=_params(2),
    )(c_me, g4, recv)


def _adam_shard(hsum, recv, chip, w, m, v, *, name):
    _, rows, width = w.shape
    tb = _row_block(rows)

    def body(j_ref, h_ref, r_ref, w_ref, m_ref, v_ref, g_out, d_out, m_out, v_out):
        g = ((h_ref[0] + r_ref[0].astype(f32)) + r_ref[1].astype(f32)) + r_ref[2].astype(f32)
        delta, mn, vn = _adam_math(w_ref[0], g, m_ref[0], v_ref[0])
        g_out[0] = g
        d_out[0] = delta
        m_out[0] = mn
        v_out[0] = vn

    blk = pl.BlockSpec((1, tb, width), lambda i, j_ref: (0, i, 0))
    grid_spec = pltpu.PrefetchScalarGridSpec(
        num_scalar_prefetch=1, grid=(rows // tb,),
        in_specs=[pl.BlockSpec((1, tb, width), lambda i, j_ref: (j_ref[0], i, 0)),
                  pl.BlockSpec((3, tb, width), lambda i, j_ref: (0, i, 0)), blk, blk, blk],
        out_specs=[blk, blk, blk, blk])
    return pl.pallas_call(
        body, name=name, grid_spec=grid_spec, out_shape=[jax.ShapeDtypeStruct(w.shape, f32)] * 4,
        compiler_params=_params(1),
    )(chip, hsum, recv, w, m, v)


R_SMALL = 16 + 16 * N_DEV
_SMALL_LANES = {"gdn_norm_g": (0, DH), "gdn_A_log": (DH, DH + H), "gdn_dt_bias": (2 * DH, 2 * DH + H)}
_LOSS_LANE = 3 * DH


def _pack_small(dg1, dg2, dg3, dgn, dal, ddt, loss_p, dwa, dwg, dwf):
    def body(dg1_ref, dg2_ref, dg3_ref, dgn_ref, dal_ref, ddt_ref, loss_ref, dwa_ref, dwg_ref, dwf_ref, o_ref):
        def total(ref):
            return jnp.sum(ref[...], axis=0, keepdims=True)

        o_ref[...] = jnp.zeros_like(o_ref)
        o_ref[0:1, :] = total(dg1_ref)
        o_ref[1:2, :] = total(dg2_ref)
        o_ref[2:3, :] = total(dg3_ref)
        o_ref[3:4, 0:DH] = total(dgn_ref)
        o_ref[3:4, DH:2 * DH] = total(dal_ref)
        o_ref[3:4, 2 * DH:3 * DH] = total(ddt_ref)
        o_ref[3:4, 3 * DH:4 * DH] = total(loss_ref)
        for d in range(N_DEV):
            base = 16 + 16 * d
            o_ref[base:base + 3, 0:128] = dwa_ref[0:3, 128 * d:128 * (d + 1)]
            o_ref[base:base + 4, 128:512] = dwg_ref[0:4, 384 * d:384 * (d + 1)]
            o_ref[base + 8:base + 11, 0:704] = dwf_ref[0:3, 704 * d:704 * (d + 1)]

    return pl.pallas_call(body, name="pack_small", out_shape=jax.ShapeDtypeStruct((R_SMALL, D), f32))(
        dg1, dg2, dg3, dgn, dal, ddt, loss_p, dwa, dwg, dwf)


_SMALL = ("norm_mix_g", "norm_ffn_g", "norm_final_g", "gdn_norm_g", "gdn_A_log", "gdn_dt_bias",
          "conv_a_w", "gdn_conv_w", "ffn_conv_w")


def _adam_small(gath, me, w, m, v):
    arrays = [t[n] for n in _SMALL for t in (w, m, v)]

    def body(me_ref, ga_ref, gb_ref, *refs):
        ins, outs = refs[:len(arrays)], refs[len(arrays):]
        ga, gb = ga_ref[0], gb_ref[0]
        for s in range(1, N_DEV):
            ga = ga + ga_ref[s]
            gb = gb + gb_ref[s]
        grads = {"norm_mix_g": ga[0:1, :], "norm_ffn_g": ga[1:2, :], "norm_final_g": ga[2:3, :],
                 "conv_a_w": gb[0:3, 0:128], "gdn_conv_w": gb[0:4, 128:512], "ffn_conv_w": gb[8:11, 0:704]}
        for n, (lo, hi) in _SMALL_LANES.items():
            grads[n] = ga[3:4, lo:hi]
        for i, n in enumerate(_SMALL):
            three_d = len(w[n].shape) == 3
            wv, mv, vv = (r[0] if three_d else r[...] for r in ins[3 * i:3 * i + 3])
            delta, mn, vn = _adam_math(wv, grads[n], mv, vv)
            for o_ref, val in zip(outs[4 * i:4 * i + 4], (grads[n], delta, mn, vn)):
                if three_d:
                    o_ref[0] = val
                else:
                    o_ref[...] = val
        outs[-1][...] = ga[3:4, _LOSS_LANE:_LOSS_LANE + 1]

    def whole(shape):
        return pl.BlockSpec(shape, lambda i, me_ref: (0,) * len(shape))

    grid_spec = pltpu.PrefetchScalarGridSpec(
        num_scalar_prefetch=1, grid=(1,),
        in_specs=[pl.BlockSpec((N_DEV, 16, D), lambda i, me_ref: (0, 0, 0)),
                  pl.BlockSpec((N_DEV, 16, D), lambda i, me_ref: (0, 1 + me_ref[0], 0))] + [whole(a.shape) for a in arrays],
        out_specs=[whole(w[n].shape) for n in _SMALL for _ in range(4)] + [whole((1, 1))])
    res = pl.pallas_call(
        body, name="adam_small", grid_spec=grid_spec,
        out_shape=[jax.ShapeDtypeStruct(w[n].shape, f32) for n in _SMALL for _ in range(4)]
        + [jax.ShapeDtypeStruct((1, 1), f32)],
        compiler_params=_params(1),
    )(me, gath, gath, *arrays)
    return {n: tuple(res[4 * i:4 * i + 4]) for i, n in enumerate(_SMALL)}, res[-1]


def _adam_math(w, g, m, v):
    m = ADAM_B1 * m + (1.0 - ADAM_B1) * g
    v = ADAM_B2 * v + (1.0 - ADAM_B2) * jnp.square(g)
    m_hat = m / (1.0 - ADAM_B1 ** ADAM_STEP)
    v_hat = v / (1.0 - ADAM_B2 ** ADAM_STEP)
    delta = -ADAM_LR * (m_hat / (jnp.sqrt(v_hat) + ADAM_EPS) + ADAM_WD * w)
    return delta, m, v


_WEIGHTS = ("norm_mix_g", "w_in", "conv_a_w", "gdn_conv_w", "gdn_A_log", "gdn_dt_bias", "gdn_norm_g", "w_a_out",
            "w_b_out", "w_o", "norm_ffn_g", "w_up", "ffn_conv_w", "w_down", "norm_final_g")
_BIG = ("w_in",) + _REST
_CONVS = ("conv_a_w", "gdn_conv_w", "ffn_conv_w")


class _StepExchanges:
    def __init__(self, wts, mom, var, c_me, chip):
        self.wts, self.mom, self.var, self.c_me, self.chip = wts, mom, var, c_me, chip
        self.results = {}

    def gather_first(self):
        return _gather_exchange([self.wts["w_in"][0].astype(bf16)] + [self.wts[n][0] for n in _CONVS])

    def finish_first(self, gathered):
        g_in, gc_a, gc_g, gc_f = gathered
        w1, w2 = _cols_to_matrices(g_in, _IN_RANGES, (NW1, 128), name="relay_w_in")
        return {"w1": w1, "w2": w2, "conv_a_w": gc_a.transpose(1, 0, 2).reshape(3, D),
                "gdn_conv_w": gc_g.transpose(1, 0, 2).reshape(4, 3 * D),
                "ffn_conv_w": gc_f.transpose(1, 0, 2).reshape(3, 2 * DFF)}

    def gather_rest(self):
        return _gather_direct_exchange([self.wts[n][0].astype(bf16) for n in _REST])

    def finish_gather(self, gathered):
        g_up, g_a, g_b, g_o, g_down = _gather_forward(gathered)
        (w_up,) = _cols_to_matrices(g_up, _UP_RANGES, (2 * DFF,), name="relay_w_up")
        return {"w_up": w_up, "w_a_out": g_a.reshape(D, D), "w_b_out": g_b.reshape(D, D), "w_o": g_o.reshape(D, D),
                "w_down": g_down.reshape(DFF, D)}

    def reduce_halves(self, names, grads):
        blocks = []
        for n in names:
            if n == "w_in":
                g = _matrices_to_cols([grads["w1"], grads["w2"]], _IN_RANGES, R_IN, name="relay_dw_in")
            elif n == "w_up":
                g = _matrices_to_cols([grads[n]], _UP_RANGES, R_UP, name="relay_dw_up")
            else:
                g = grads[n]
            blocks.append(g.reshape(4, 2, *self.wts[n].shape[1:]))
        return _sibling_exchange([_half_bf16(g, 1 - self.c_me, name="rs_half_" + n) for n, g in zip(names, blocks)]), blocks

    def reduce_sums(self, names, blocks, recv):
        sums = [_pair_sum(g, r, self.c_me, name="rs_sum_" + n) for n, g, r in zip(names, blocks, recv)]
        return _chips_exchange([s[1] for s in sums]), [s[0] for s in sums]

    def finish_reduce(self, names, sums, recv):
        for n, s, r in zip(names, sums, recv):
            self.results[n] = _adam_shard(s, r, self.chip, self.wts[n], self.mom[n], self.var[n], name="adam_" + n)


def kernel(x, norm_mix_g, w_in, conv_a_w, gdn_conv_w, gdn_A_log, gdn_dt_bias, gdn_norm_g, w_a_out, w_b_out, w_o, norm_ffn_g, w_up, ffn_conv_w, w_down, norm_final_g, loss_target, m_norm_mix_g, m_w_in, m_conv_a_w, m_gdn_conv_w, m_gdn_A_log, m_gdn_dt_bias, m_gdn_norm_g, m_w_a_out, m_w_b_out, m_w_o, m_norm_ffn_g, m_w_up, m_ffn_conv_w, m_w_down, m_norm_final_g, v_norm_mix_g, v_w_in, v_conv_a_w, v_gdn_conv_w, v_gdn_A_log, v_gdn_dt_bias, v_gdn_norm_g, v_w_a_out, v_w_b_out, v_w_o, v_norm_ffn_g, v_w_up, v_ffn_conv_w, v_w_down, v_norm_final_g):
    wts = dict(zip(_WEIGHTS, (norm_mix_g, w_in, conv_a_w, gdn_conv_w, gdn_A_log, gdn_dt_bias, gdn_norm_g, w_a_out,
                              w_b_out, w_o, norm_ffn_g, w_up, ffn_conv_w, w_down, norm_final_g)))
    mom = dict(zip(_WEIGHTS, (m_norm_mix_g, m_w_in, m_conv_a_w, m_gdn_conv_w, m_gdn_A_log, m_gdn_dt_bias,
                              m_gdn_norm_g, m_w_a_out, m_w_b_out, m_w_o, m_norm_ffn_g, m_w_up, m_ffn_conv_w,
                              m_w_down, m_norm_final_g)))
    var = dict(zip(_WEIGHTS, (v_norm_mix_g, v_w_in, v_conv_a_w, v_gdn_conv_w, v_gdn_A_log, v_gdn_dt_bias,
                              v_gdn_norm_g, v_w_a_out, v_w_b_out, v_w_o, v_norm_ffn_g, v_w_up, v_ffn_conv_w,
                              v_w_down, v_norm_final_g)))
    cx, cy, cc = lax.axis_index("x"), lax.axis_index("y"), lax.axis_index("c")
    c_me = jnp.reshape(cc, (1,)).astype(jnp.int32)
    chip = jnp.reshape(2 * cx + cy, (1,)).astype(jnp.int32)
    me = jnp.reshape(4 * cx + 2 * cy + cc, (1,)).astype(jnp.int32)

    comm = _StepExchanges(wts, mom, var, c_me, chip)
    replicated = {n: wts[n] for n in ("norm_mix_g", "norm_ffn_g", "norm_final_g", "gdn_norm_g", "gdn_A_log", "gdn_dt_bias")}
    loss_p, dx, grads = _local_step(x[0], loss_target[0], replicated, comm)
    res = comm.results

    small = _pack_small(grads["norm_mix_g"], grads["norm_ffn_g"], grads["norm_final_g"], grads["gdn_norm_g"],
                        grads["gdn_A_log"], grads["gdn_dt_bias"], loss_p, grads["conv_a_w"], grads["gdn_conv_w"],
                        grads["ffn_conv_w"])
    (small_all,) = _run_exchange(_gather_exchange([small]), name="ag_small")

    def raw(t):
        return {n: t[n].reshape(1, D) if n == "norm_final_g" else t[n] for n in _SMALL}

    res_small, loss = _adam_small(small_all, me, raw(wts), raw(mom), raw(var))
    for n in _SMALL:
        res[n] = tuple(a.reshape(wts[n].shape) for a in res_small[n])
    outs = [[res[n][i] for n in _WEIGHTS] for i in range(4)]
    return (loss.reshape(()), dx[None], *outs[0], *outs[1], *outs[2], *outs[3])
```

```python
import jax
import jax.numpy as jnp
from jax import lax
from jax.experimental import pallas as pl
from jax.experimental.pallas import tpu as pltpu

f32 = jnp.float32
bf16 = jnp.bfloat16

D = 1024
H = 8
DH = 128
CH = 64
GDN_STEP = 2
DFF = 2816
NW1 = 9216
EPS = 1e-6
N_DEV = 8

ADAM_LR = 0.001
ADAM_B1 = 0.9
ADAM_B2 = 0.999
ADAM_EPS = 1e-08
ADAM_WD = 0.01
ADAM_STEP = 10

VMEM_LIMIT_BYTES = 48 * 1024 * 1024

R_IN, R_UP = 1154, 704

_HI = lax.Precision.HIGHEST
MESH = pl.DeviceIdType.MESH


def _params(n_grid):
    return pltpu.CompilerParams(dimension_semantics=("arbitrary",) * n_grid, vmem_limit_bytes=VMEM_LIMIT_BYTES)


def _bdot(a, b):
    return jnp.dot(a.astype(bf16), b.astype(bf16), preferred_element_type=f32)


def _bdot_nt(a, b):
    return lax.dot_general(a.astype(bf16), b.astype(bf16), (((1,), (1,)), ((), ())), preferred_element_type=f32)


def _bdot_tn(a, b):
    return lax.dot_general(a.astype(bf16), b.astype(bf16), (((0,), (0,)), ((), ())), preferred_element_type=f32)


def _hdot(a, b):
    return jnp.dot(a, b, preferred_element_type=f32, precision=_HI)


def _idot(a, b):
    return jnp.dot(a, b, preferred_element_type=f32, precision=lax.Precision.HIGH)


def _sigmoid(x):
    return 1.0 / (1.0 + jnp.exp(-x))


def _softplus(x):
    return jnp.maximum(x, 0.0) + jnp.log(1.0 + jnp.exp(-jnp.abs(x)))


def _shift_down(x, halo, j):
    if j == 0:
        return x
    xr = pltpu.roll(x, j, 0)
    hr = pltpu.roll(halo, j, 0)
    r8 = lax.broadcasted_iota(jnp.int32, hr.shape, 0)
    top = jnp.where(r8 < j, hr, xr[:8])
    return jnp.concatenate([top, xr[8:]], axis=0)


def _shift_up(x, halo, j):
    if j == 0:
        return x
    n = x.shape[0]
    xr = pltpu.roll(x, n - j, 0)
    hr = pltpu.roll(halo, 8 - j, 0)
    r8 = lax.broadcasted_iota(jnp.int32, hr.shape, 0)
    bot = jnp.where(r8 >= 8 - j, hr, xr[n - 8:])
    return jnp.concatenate([xr[:n - 8], bot], axis=0)


def _taps_down(x, halo, k):
    return [_shift_down(x, halo, k - 1 - j) for j in range(k)]


def _strip(i, base=0):
    return slice(base + i * 128, base + (i + 1) * 128)


def _strip_taps(x, halo, first, k):
    return _taps_down(x, jnp.where(first, 0.0, halo), k)


def _strip_conv(w_ref, sl, taps):
    out = w_ref[0:1, sl] * taps[0]
    for j in range(1, len(taps)):
        out = out + w_ref[j:j + 1, sl] * taps[j]
    return out


def _strip_weight_grad(dw_ref, sl, dy, taps):
    for j, tap in enumerate(taps):
        dw_ref[j:j + 1, sl] += jnp.sum(dy * tap, axis=0, keepdims=True)


def _strip_conv_up(dy, halo, last, w_ref, sl, k):
    halo = jnp.where(last, 0.0, halo)
    out = w_ref[k - 1:k, sl] * dy
    for j in range(k - 1):
        out = out + w_ref[j:j + 1, sl] * _shift_up(dy, halo, k - 1 - j)
    return out


def _row(tb, w, col=0):
    return pl.BlockSpec((tb, w), lambda i: (i, col))


def _prev(tb, w, col=0, rows=8):
    return pl.BlockSpec((rows, w), lambda i: (jnp.maximum(i * (tb // rows) - 1, 0), col))


def _next(tb, w, n_rows, col=0, rows=8):
    last = n_rows // rows - 1
    return pl.BlockSpec((rows, w), lambda i: (jnp.minimum((i + 1) * (tb // rows), last), col))


def _f32(ref, sl):
    return ref[:, sl].astype(f32)


def _halo_before(ref, sl):
    h = _f32(ref, sl)
    return h[h.shape[0] - 8:]


def _halo_after(ref, sl):
    return _f32(ref, sl)[:8]


def _fixed(shape):
    return pl.BlockSpec(shape, lambda i: (0,) * len(shape))


def _pick(n, prefs):
    for p in prefs:
        if n % p == 0:
            return p
    return n


def _matmul(a, b, *, name, nt=False, add=None, tm=1024, tn=1024, tk=None, out_dtype=f32, cols=None, exchange=None):
    m, kd = a.shape
    col0, n = cols if cols is not None else (0, b.shape[0] if nt else b.shape[1])
    tm = _pick(m, (tm, 512, 256))
    tn = _pick(n, (tn, 1024, 512, 128))
    tk = kd if tk is None else tk
    nk = kd // tk
    assert nk == 1 or out_dtype == f32
    assert col0 % tn == 0 and not (nt and cols)
    j0 = col0 // tn
    dims = (((1,), (1,)), ((), ())) if nt else (((1,), (0,)), ((), ()))

    def body(a_ref, b_ref, *rest):
        o_ref = rest[-1]
        part = lax.dot_general(a_ref[...], b_ref[...], dims, preferred_element_type=f32)
        if nk == 1:
            o_ref[...] = (part if add is None else part + rest[0][...]).astype(out_dtype)
            return
        k = pl.program_id(2)

        @pl.when(k == 0)
        def _():
            o_ref[...] = part if add is None else part + rest[0][...]

        @pl.when(k > 0)
        def _():
            o_ref[...] += part

    b_spec = pl.BlockSpec((tn, tk), lambda i, j, k: (j, k)) if nt else pl.BlockSpec((tk, tn), lambda i, j, k: (k, j + j0))
    in_specs = [pl.BlockSpec((tm, tk), lambda i, j, k: (i, k)), b_spec]
    args = [a, b]
    if add is not None:
        in_specs.append(pl.BlockSpec((tm, tn), lambda i, j, k: (i, j)))
        args.append(add)
    return _call_with_exchange(
        body, exchange, name=name, grid=(m // tm, n // tn, nk), in_specs=in_specs,
        out_specs=pl.BlockSpec((tm, tn), lambda i, j, k: (i, j)),
        out_shape=jax.ShapeDtypeStruct((m, n), out_dtype), args=args)


def _call_with_exchange(body, exchange, *, name, grid, in_specs, out_specs, out_shape, args):
    if exchange is None:
        return pl.pallas_call(body, name=name, grid=grid, in_specs=in_specs, out_specs=out_specs, out_shape=out_shape,
                              compiler_params=_params(len(grid)))(*args)
    x_arrays, x_shapes, x_sems, start, wait = exchange
    n_in, n_xin, n_xout = len(args), len(x_arrays), len(x_shapes)

    def full_body(*refs):
        c_in, x_in = refs[:n_in], refs[n_in:n_in + n_xin]
        c_out = refs[n_in + n_xin]
        x_out = refs[n_in + n_xin + 1:n_in + n_xin + 1 + n_xout]
        sems = refs[n_in + n_xin + 1 + n_xout:]
        ids = [pl.program_id(d) for d in range(len(grid))]
        first, last = ids[0] == 0, ids[0] == grid[0] - 1
        for d in range(1, len(grid)):
            first = first & (ids[d] == 0)
            last = last & (ids[d] == grid[d] - 1)

        @pl.when(first)
        def _():
            start(x_in, x_out, sems)

        body(*c_in, c_out)

        @pl.when(last)
        def _():
            wait(x_in, x_out, sems)

    res = pl.pallas_call(
        full_body, name=name, grid=grid, in_specs=list(in_specs) + [_ANY] * n_xin,
        out_specs=[out_specs] + [_ANY] * n_xout, out_shape=[out_shape] + list(x_shapes),
        scratch_shapes=list(x_sems), compiler_params=_params(len(grid)),
    )(*args, *x_arrays)
    return res[0], list(res[1:])


def _matmul_tn(a, b, *, name, tm=1024, tn=1024, exchange=None):
    t, m = a.shape
    _, n = b.shape
    tm = _pick(m, (tm, 1024, 512, 128))
    tn = _pick(n, (tn, 1024, 512, 128))
    tt = _pick(t, (2048, 1024, 512, 256))
    nt = t // tt

    def body(a_ref, b_ref, o_ref):
        k = pl.program_id(2)
        part = lax.dot_general(a_ref[...], b_ref[...], (((0,), (0,)), ((), ())), preferred_element_type=f32)

        @pl.when(k == 0)
        def _():
            o_ref[...] = part

        @pl.when(k > 0)
        def _():
            o_ref[...] += part

    return _call_with_exchange(
        body, exchange, name=name, grid=(m // tm, n // tn, nt),
        in_specs=[pl.BlockSpec((tt, tm), lambda i, j, k: (k, i)), pl.BlockSpec((tt, tn), lambda i, j, k: (k, j))],
        out_specs=pl.BlockSpec((tm, tn), lambda i, j, k: (i, j)),
        out_shape=jax.ShapeDtypeStruct((m, n), f32), args=[a, b])


def _rms_fwd(x, g, *, name, exchange=None):
    t = x.shape[0]
    tb = _pick(t, (256, 128))

    def body(x_ref, g_ref, h_ref):
        xv = x_ref[...]
        r = lax.rsqrt(jnp.mean(xv * xv, axis=-1, keepdims=True) + EPS)
        h_ref[...] = (xv * r * g_ref[...]).astype(bf16)

    return _call_with_exchange(
        body, exchange, name=name, grid=(t // tb,), in_specs=[_row(tb, D), _fixed((1, D))], out_specs=_row(tb, D),
        out_shape=jax.ShapeDtypeStruct((t, D), bf16), args=[x, g])


def _rms_bwd(dh, x, g, dres, *, name):
    t = x.shape[0]
    tb = _pick(t, (256, 128))

    def body(dh_ref, x_ref, g_ref, dres_ref, dx_ref, dxb_ref, dg_ref):
        xv = x_ref[...]
        r = lax.rsqrt(jnp.mean(xv * xv, axis=-1, keepdims=True) + EPS)
        xh = xv * r
        dy = dh_ref[...]
        dyg = dy * g_ref[...]
        dx = dres_ref[...] + r * (dyg - xh * jnp.mean(dyg * xh, axis=-1, keepdims=True))
        dx_ref[...] = dx
        dxb_ref[...] = dx.astype(bf16)

        @pl.when(pl.program_id(0) == 0)
        def _():
            dg_ref[...] = jnp.zeros_like(dg_ref)

        dg_ref[...] += jnp.sum((dy * xh).reshape(tb // 8, 8, D), axis=0)

    return pl.pallas_call(
        body, name=name, grid=(t // tb,),
        in_specs=[_row(tb, D), _row(tb, D), _fixed((1, D)), _row(tb, D)],
        out_specs=[_row(tb, D), _row(tb, D), _fixed((8, D))],
        out_shape=[jax.ShapeDtypeStruct((t, D), f32), jax.ShapeDtypeStruct((t, D), bf16),
                   jax.ShapeDtypeStruct((8, D), f32)],
        compiler_params=_params(1),
    )(dh, x, g, dres)


def _gdn_gates(ab, alog, dtb):
    lane = lax.broadcasted_iota(jnp.int32, ab.shape, 1)
    g = -jnp.exp(alog) * _softplus(ab + dtb)
    beta = _sigmoid(ab)
    return jnp.where(lane < H, g, jnp.where(lane < 2 * H, beta, 0.0))


def _pre_fwd(pg, pq, p2, wa, wg, alog, dtb):
    t = pg.shape[0]
    tb = 128

    def body(p0_ref, p0h_ref, pq_ref, pqh_ref, p2_ref, wa_ref, wg_ref, alog_ref, dtb_ref,
             ya_ref, qn_ref, kn_ref, vc_ref, gb_ref):
        first = pl.program_id(0) == 0
        for i in range(D // 128):
            sl, cg, xv = _strip(i), _strip(i, D), _strip(i, 2 * D)
            taps = _strip_taps(_f32(p0_ref, cg) * _f32(p0_ref, xv), _halo_before(p0h_ref, cg) * _halo_before(p0h_ref, xv),
                               first, 3)
            ya_ref[:, sl] = (_f32(p0_ref, sl) * _strip_conv(wa_ref, sl, taps)).astype(bf16)
        for part, out_ref, scale in ((0, qn_ref, DH ** -0.5), (1, kn_ref, 1.0), (2, vc_ref, None)):
            for h in range(H):
                sl = _strip(h, part * D)
                s = _strip_conv(wg_ref, sl, _strip_taps(pq_ref[:, sl], pqh_ref[:, sl], first, 4))
                s = s * _sigmoid(s)
                if scale is not None:
                    s = s * (lax.rsqrt(jnp.sum(s * s, axis=-1, keepdims=True) + EPS) * scale)
                out_ref[:, _strip(h)] = s
        gb_ref[...] = _gdn_gates(p2_ref[...], alog_ref[...], dtb_ref[...])

    return pl.pallas_call(
        body, name="pre_fwd", grid=(t // tb,),
        in_specs=[_row(tb, 3 * D, 0), _prev(tb, 3 * D, 0, rows=16), _row(tb, 3 * D), _prev(tb, 3 * D), _row(tb, 128),
                  _fixed((8, D)), _fixed((8, 3 * D)), _fixed((1, 128)), _fixed((1, 128))],
        out_specs=[_row(tb, D), _row(tb, D), _row(tb, D), _row(tb, D), _row(tb, 128)],
        out_shape=[jax.ShapeDtypeStruct((t, D), bf16), jax.ShapeDtypeStruct((t, D), f32),
                   jax.ShapeDtypeStruct((t, D), f32), jax.ShapeDtypeStruct((t, D), f32),
                   jax.ShapeDtypeStruct((t, 128), f32)],
        compiler_params=_params(1),
    )(pg, pg, pq, pq, p2, wa, wg, alog, dtb)


_Z_COL, _GA_COL, _GB_COL = 3, 4, 5


def _post_fwd(o, pg, gn):
    t = o.shape[0]
    tb = _pick(t, (256, 128))

    def body(o_ref, z_ref, gn_ref, yb_ref):
        for h in range(H):
            sl = slice(h * DH, (h + 1) * DH)
            oh = o_ref[:, sl]
            z = _f32(z_ref, sl)
            r = lax.rsqrt(jnp.mean(oh * oh, axis=-1, keepdims=True) + EPS)
            yb_ref[:, sl] = (oh * r * gn_ref[...] * (z * _sigmoid(z))).astype(bf16)

    return pl.pallas_call(
        body, name="post_fwd", grid=(t // tb,), in_specs=[_row(tb, D), _row(tb, D, _Z_COL), _fixed((1, DH))],
        out_specs=_row(tb, D), out_shape=jax.ShapeDtypeStruct((t, D), bf16), compiler_params=_params(1),
    )(o, pg, gn)


def _post_bwd(dyb, o, pg, gn):
    t = o.shape[0]
    tb = _pick(t, (256, 128))

    def body(dyb_ref, o_ref, z_ref, gn_ref, do_ref, dz_ref, dgn_ref):
        @pl.when(pl.program_id(0) == 0)
        def _():
            dgn_ref[...] = jnp.zeros_like(dgn_ref)

        gn_v = gn_ref[...]
        acc = jnp.zeros((8, DH), f32)
        for h in range(H):
            sl = slice(h * DH, (h + 1) * DH)
            oh = o_ref[:, sl]
            z = _f32(z_ref, sl)
            dy = dyb_ref[:, sl]
            r = lax.rsqrt(jnp.mean(oh * oh, axis=-1, keepdims=True) + EPS)
            on = oh * r
            sg = _sigmoid(z)
            sz = z * sg
            don = dy * sz
            dz_ref[:, sl] = (dy * on * gn_v * (sg * (1.0 + z * (1.0 - sg)))).astype(bf16)
            acc = acc + jnp.sum((don * on).reshape(tb // 8, 8, DH), axis=0)
            doh = don * gn_v
            do_ref[:, sl] = r * (doh - on * jnp.mean(doh * on, axis=-1, keepdims=True))
        dgn_ref[...] += acc

    return pl.pallas_call(
        body, name="post_bwd", grid=(t // tb,),
        in_specs=[_row(tb, D), _row(tb, D), _row(tb, D, _Z_COL), _fixed((1, DH))],
        out_specs=[_row(tb, D), _row(tb, D), _fixed((8, DH))],
        out_shape=[jax.ShapeDtypeStruct((t, D), f32), jax.ShapeDtypeStruct((t, D), bf16),
                   jax.ShapeDtypeStruct((8, DH), f32)],
        compiler_params=_params(1),
    )(dyb, o, pg, gn)


def _mix_fwd(ya, yb, pg):
    t = ya.shape[0]
    tb = _pick(t, (256, 128))

    def body(ya_ref, yb_ref, ga_ref, gb_ref, mix_ref):
        ya_v, yb_v = ya_ref[...].astype(f32), yb_ref[...].astype(f32)
        mix = _sigmoid(ga_ref[...].astype(f32)) * ya_v + _sigmoid(gb_ref[...].astype(f32)) * yb_v
        mix_ref[...] = mix.astype(bf16)

    return pl.pallas_call(
        body, name="mix_fwd", grid=(t // tb,),
        in_specs=[_row(tb, D), _row(tb, D), _row(tb, D, _GA_COL), _row(tb, D, _GB_COL)],
        out_specs=_row(tb, D), out_shape=jax.ShapeDtypeStruct((t, D), bf16), compiler_params=_params(1),
    )(ya, yb, pg, pg)


def _mix_bwd(dmix, ya, yb, pg):
    t = ya.shape[0]
    tb = _pick(t, (256, 128))

    def body(dm_ref, ya_ref, yb_ref, ga_ref, gb_ref, dya_ref, dyb_ref, dg_ref):
        dm = dm_ref[...].astype(f32)
        sa = _sigmoid(ga_ref[...].astype(f32))
        sb = _sigmoid(gb_ref[...].astype(f32))
        dya_ref[...] = (dm * sa).astype(bf16)
        dyb_ref[...] = (dm * sb).astype(bf16)
        dg_ref[:, :D] = (dm * ya_ref[...].astype(f32) * sa * (1.0 - sa)).astype(bf16)
        dg_ref[:, D:] = (dm * yb_ref[...].astype(f32) * sb * (1.0 - sb)).astype(bf16)

    return pl.pallas_call(
        body, name="mix_bwd", grid=(t // tb,),
        in_specs=[_row(tb, D), _row(tb, D), _row(tb, D), _row(tb, D, _GA_COL), _row(tb, D, _GB_COL)],
        out_specs=[_row(tb, D), _row(tb, D), _row(tb, 2 * D)],
        out_shape=[jax.ShapeDtypeStruct((t, D), bf16), jax.ShapeDtypeStruct((t, D), bf16),
                   jax.ShapeDtypeStruct((t, 2 * D), bf16)],
        compiler_params=_params(1),
    )(dmix, ya, yb, pg, pg)


def _ffn_fwd(up, wf):
    t = up.shape[0]
    tb = 128

    def body(up_ref, uph_ref, wf_ref, act_ref):
        first = pl.program_id(0) == 0
        for i in range(DFF // 128):
            g, v = _strip(i), _strip(i, DFF)
            gate = _strip_conv(wf_ref, g, _strip_taps(_f32(up_ref, g), _halo_before(uph_ref, g), first, 3))
            val = _strip_conv(wf_ref, v, _strip_taps(_f32(up_ref, v), _halo_before(uph_ref, v), first, 3))
            act_ref[:, g] = (gate * _sigmoid(gate) * val).astype(bf16)

    return pl.pallas_call(
        body, name="ffn_fwd", grid=(t // tb,),
        in_specs=[_row(tb, 2 * DFF), _prev(tb, 2 * DFF, rows=16), _fixed((8, 2 * DFF))],
        out_specs=_row(tb, DFF), out_shape=jax.ShapeDtypeStruct((t, DFF), bf16), compiler_params=_params(1),
    )(up, up, wf)


def _ffn_bwd1(dact, up, wf):
    t = up.shape[0]
    tb = 128

    def body(da_ref, up_ref, uph_ref, wf_ref, dc_ref, dw_ref):
        @pl.when(pl.program_id(0) == 0)
        def _():
            dw_ref[...] = jnp.zeros_like(dw_ref)

        first = pl.program_id(0) == 0
        for i in range(DFF // 128):
            g, v = _strip(i), _strip(i, DFF)
            g_taps = _strip_taps(_f32(up_ref, g), _halo_before(uph_ref, g), first, 3)
            v_taps = _strip_taps(_f32(up_ref, v), _halo_before(uph_ref, v), first, 3)
            gate = _strip_conv(wf_ref, g, g_taps)
            val = _strip_conv(wf_ref, v, v_taps)
            sg = _sigmoid(gate)
            da = _f32(da_ref, g)
            dgate = da * val * (sg * (1.0 + gate * (1.0 - sg)))
            dval = da * (gate * sg)
            dc_ref[:, g] = dgate.astype(bf16)
            dc_ref[:, v] = dval.astype(bf16)
            _strip_weight_grad(dw_ref, g, dgate, g_taps)
            _strip_weight_grad(dw_ref, v, dval, v_taps)

    return pl.pallas_call(
        body, name="ffn_bwd1", grid=(t // tb,),
        in_specs=[_row(tb, DFF), _row(tb, 2 * DFF), _prev(tb, 2 * DFF, rows=16), _fixed((8, 2 * DFF))],
        out_specs=[_row(tb, 2 * DFF), _fixed((8, 2 * DFF))],
        out_shape=[jax.ShapeDtypeStruct((t, 2 * DFF), bf16), jax.ShapeDtypeStruct((8, 2 * DFF), f32)],
        compiler_params=_params(1),
    )(dact, up, up, wf)


def _ffn_bwd2(dc, wf):
    t = dc.shape[0]
    tb = 128
    nb = t // tb

    def body(dc_ref, dch_ref, wf_ref, dup_ref):
        last = pl.program_id(0) == nb - 1
        for i in range(2 * DFF // 128):
            sl = _strip(i)
            dup_ref[:, sl] = _strip_conv_up(_f32(dc_ref, sl), _halo_after(dch_ref, sl), last, wf_ref, sl, 3).astype(bf16)

    return pl.pallas_call(
        body, name="ffn_bwd2", grid=(nb,),
        in_specs=[_row(tb, 2 * DFF), _next(tb, 2 * DFF, t, rows=16), _fixed((8, 2 * DFF))],
        out_specs=_row(tb, 2 * DFF), out_shape=jax.ShapeDtypeStruct((t, 2 * DFF), bf16), compiler_params=_params(1),
    )(dc, dc, wf)


def _final(x3, tgt, g):
    t = x3.shape[0]
    tb = _pick(t, (256, 128))

    def body(x_ref, t_ref, g_ref, loss_ref, dx_ref, dxb_ref, dg_ref):
        @pl.when(pl.program_id(0) == 0)
        def _():
            loss_ref[...] = jnp.zeros_like(loss_ref)
            dg_ref[...] = jnp.zeros_like(dg_ref)

        xv = x_ref[...]
        r = lax.rsqrt(jnp.mean(xv * xv, axis=-1, keepdims=True) + EPS)
        xh = xv * r
        gv = g_ref[...]
        e = xh * gv - t_ref[...]
        lrow = 0.5 * jnp.mean(e * e, axis=-1, keepdims=True)
        loss_ref[...] += jnp.sum(jnp.broadcast_to(lrow, (tb, 128)).reshape(tb // 8, 8, 128), axis=0)
        dy = e * (1.0 / D)
        dyg = dy * gv
        dx = r * (dyg - xh * jnp.mean(dyg * xh, axis=-1, keepdims=True))
        dx_ref[...] = dx
        dxb_ref[...] = dx.astype(bf16)
        dg_ref[...] += jnp.sum((dy * xh).reshape(tb // 8, 8, D), axis=0)

    return pl.pallas_call(
        body, name="final", grid=(t // tb,), in_specs=[_row(tb, D), _row(tb, D), _fixed((1, D))],
        out_specs=[_fixed((8, 128)), _row(tb, D), _row(tb, D), _fixed((8, D))],
        out_shape=[jax.ShapeDtypeStruct((8, 128), f32), jax.ShapeDtypeStruct((t, D), f32),
                   jax.ShapeDtypeStruct((t, D), bf16), jax.ShapeDtypeStruct((8, D), f32)],
        compiler_params=_params(1),
    )(x3, tgt, g)


def _pre_bwd1(pg, pq, p2, dya_in, dqn, dkn, dvc, dgb, gbeta, wa, wg, alog, dtb):
    t = pg.shape[0]
    tb = 128

    def body(p0_ref, p0h_ref, pq_ref, pqh_ref, p2_ref, dya_ref, dqn_ref, dkn_ref, dvc_ref, dgb_ref, gb_ref,
             wa_ref, wg_ref, alog_ref, dtb_ref,
             dbg_ref, dca_ref, dc4_ref, dp2_ref, dwa_ref, dwg_ref, dal_ref, ddt_ref):
        @pl.when(pl.program_id(0) == 0)
        def _():
            dwa_ref[...] = jnp.zeros_like(dwa_ref)
            dwg_ref[...] = jnp.zeros_like(dwg_ref)
            dal_ref[...] = jnp.zeros_like(dal_ref)
            ddt_ref[...] = jnp.zeros_like(ddt_ref)

        first = pl.program_id(0) == 0

        for i in range(D // 128):
            sl, cg, xv = _strip(i), _strip(i, D), _strip(i, 2 * D)
            taps = _strip_taps(_f32(p0_ref, cg) * _f32(p0_ref, xv), _halo_before(p0h_ref, cg) * _halo_before(p0h_ref, xv),
                               first, 3)
            dya = _f32(dya_ref, sl)
            dbg_ref[:, sl] = (dya * _strip_conv(wa_ref, sl, taps)).astype(bf16)
            dca = dya * _f32(p0_ref, sl)
            dca_ref[:, sl] = dca.astype(bf16)
            _strip_weight_grad(dwa_ref, sl, dca, taps)

        for part, d_ref, scale in ((0, dqn_ref, DH ** -0.5), (1, dkn_ref, 1.0), (2, dvc_ref, None)):
            for h in range(H):
                sl = _strip(h, part * D)
                taps = _strip_taps(pq_ref[:, sl], pqh_ref[:, sl], first, 4)
                c4 = _strip_conv(wg_ref, sl, taps)
                sg = _sigmoid(c4)
                dn = d_ref[:, _strip(h)]
                if scale is not None:
                    a = c4 * sg
                    r = lax.rsqrt(jnp.sum(a * a, axis=-1, keepdims=True) + EPS)
                    an = a * r
                    dn = dn * scale
                    dn = r * (dn - an * jnp.sum(dn * an, axis=-1, keepdims=True))
                dc4 = dn * (sg * (1.0 + c4 * (1.0 - sg)))
                dc4_ref[:, sl] = dc4.astype(bf16)
                _strip_weight_grad(dwg_ref, sl, dc4, taps)

        ab = p2_ref[...]
        lane = lax.broadcasted_iota(jnp.int32, ab.shape, 1)
        dgbv = dgb_ref[...]
        gbv = gb_ref[...]
        da = dgbv * (-jnp.exp(alog_ref[...])) * _sigmoid(ab + dtb_ref[...])
        db = dgbv * gbv * (1.0 - gbv)
        dp2_ref[...] = jnp.where(lane < H, da, jnp.where(lane < 2 * H, db, 0.0)).astype(bf16)
        dal = jnp.where(lane < H, dgbv * gbv, 0.0)
        ddt = jnp.where(lane < H, da, 0.0)
        dal_ref[...] += jnp.sum(dal.reshape(tb // 8, 8, 128), axis=0)
        ddt_ref[...] += jnp.sum(ddt.reshape(tb // 8, 8, 128), axis=0)

    return pl.pallas_call(
        body, name="pre_bwd1", grid=(t // tb,),
        in_specs=[_row(tb, 3 * D, 0), _prev(tb, 3 * D, 0, rows=16), _row(tb, 3 * D), _prev(tb, 3 * D), _row(tb, 128),
                  _row(tb, D), _row(tb, D), _row(tb, D), _row(tb, D), _row(tb, 128), _row(tb, 128),
                  _fixed((8, D)), _fixed((8, 3 * D)), _fixed((1, 128)), _fixed((1, 128))],
        out_specs=[_row(tb, D), _row(tb, D), _row(tb, 3 * D), _row(tb, 128),
                   _fixed((8, D)), _fixed((8, 3 * D)), _fixed((8, 128)), _fixed((8, 128))],
        out_shape=[jax.ShapeDtypeStruct((t, D), bf16), jax.ShapeDtypeStruct((t, D), bf16),
                   jax.ShapeDtypeStruct((t, 3 * D), bf16), jax.ShapeDtypeStruct((t, 128), bf16),
                   jax.ShapeDtypeStruct((8, D), f32), jax.ShapeDtypeStruct((8, 3 * D), f32),
                   jax.ShapeDtypeStruct((8, 128), f32), jax.ShapeDtypeStruct((8, 128), f32)],
        compiler_params=_params(1),
    )(pg, pg, pq, pq, p2, dya_in, dqn, dkn, dvc, dgb, gbeta, wa, wg, alog, dtb)


def _pre_bwd2(dca, dc4, pg, dbg, dz, dgates, wa, wg, exchange=None):
    t = pg.shape[0]
    tb = 128
    nb = t // tb

    def body(dca_ref, dcah_ref, dc4_ref, dc4h_ref, p0_ref, dbg_ref, dz_ref, dgt_ref, wa_ref, wg_ref, dp_ref):
        last = pl.program_id(0) == nb - 1
        dp_ref[:, :D] = dbg_ref[...]
        for i in range(D // 128):
            sl, cg, xv = _strip(i), _strip(i, D), _strip(i, 2 * D)
            du = _strip_conv_up(_f32(dca_ref, sl), _halo_after(dcah_ref, sl), last, wa_ref, sl, 3)
            dp_ref[:, cg] = (du * _f32(p0_ref, xv)).astype(bf16)
            dp_ref[:, xv] = (du * _f32(p0_ref, cg)).astype(bf16)
        dp_ref[:, 3 * D:4 * D] = dz_ref[...]
        dp_ref[:, 4 * D:6 * D] = dgt_ref[...]
        for i in range(3 * D // 128):
            sl = _strip(i)
            dq = _strip_conv_up(_f32(dc4_ref, sl), _halo_after(dc4h_ref, sl), last, wg_ref, sl, 4)
            dp_ref[:, _strip(i, 6 * D)] = dq.astype(bf16)

    return _call_with_exchange(
        body, exchange, name="pre_bwd2", grid=(nb,),
        in_specs=[_row(tb, D), _next(tb, D, t, rows=16), _row(tb, 3 * D), _next(tb, 3 * D, t, rows=16), _row(tb, 3 * D, 0),
                  _row(tb, D), _row(tb, D), _row(tb, 2 * D), _fixed((8, D)), _fixed((8, 3 * D))],
        out_specs=_row(tb, NW1), out_shape=jax.ShapeDtypeStruct((t, NW1), bf16),
        args=[dca, dca, dc4, dc4, pg, dbg, dz, dgates, wa, wg])


def _chunk_consts():
    r = lax.broadcasted_iota(jnp.int32, (CH, CH), 0)
    c = lax.broadcasted_iota(jnp.int32, (CH, CH), 1)
    return r, c, (r == c).astype(f32)


def _tri_inverse(lows, eye, r, c):
    def same_block(b):
        return jnp.bitwise_xor(r, c) < b

    xs = [jnp.where(same_block(8), -low, 0.0) for low in lows]
    ts = [eye + x for x in xs]
    for _ in range(2):
        xs = [_idot(x, x) for x in xs]
        ts = [t + _idot(t, x) for t, x in zip(ts, xs)]
    for b in (8, 16, 32):
        below = same_block(2 * b) & jnp.logical_not(same_block(b))
        ts = [t - _idot(_idot(t, jnp.where(below, low, 0.0)), t) for t, low in zip(ts, lows)]
    return ts


def _chunk_common(q, k, v, gcol, bcol, r, c, eye):
    grow = jnp.sum(eye * gcol, axis=0, keepdims=True)
    dec = jnp.exp(jnp.where(r >= c, gcol - grow, -jnp.inf))
    rcol = lax.broadcasted_iota(jnp.int32, (CH, 1), 0)
    glast = jnp.sum(jnp.where(rcol == CH - 1, gcol, 0.0), axis=0, keepdims=True)
    eg = jnp.exp(gcol)
    el = jnp.exp(glast - gcol)
    kb = k * bcol
    vb = v * bcol
    kk = _bdot_nt(kb, k)
    low = jnp.where(r > c, kk * dec, 0.0)
    qk = _bdot_nt(q, k)
    att = qk * dec
    return grow, dec, glast, eg, el, kb, vb, kk, low, qk, att, rcol


def _gdn_fwd(qn, kn, vc, gbeta):
    t = qn.shape[0]
    n_chunks = t // CH

    def body(q_ref, k_ref, v_ref, gb_ref, o_ref, s_ref, t_ref, state):
        @pl.when(pl.program_id(0) == 0)
        def _():
            state[...] = jnp.zeros_like(state)

        r, c, eye = _chunk_consts()
        tri = (r >= c).astype(f32)
        heads = range(H)
        keys = [(s, h) for s in range(GDN_STEP) for h in heads]
        rows = [slice(s * CH, (s + 1) * CH) for s in range(GDN_STEP)]
        gbs = [gb_ref[rows[s], :] for s in range(GDN_STEP)]
        galls = [_hdot(tri, gb) for gb in gbs]
        qs = {(s, h): q_ref[rows[s], h * DH:(h + 1) * DH] for s, h in keys}
        ks = {(s, h): k_ref[rows[s], h * DH:(h + 1) * DH] for s, h in keys}
        cm = {(s, h): _chunk_common(qs[s, h], ks[s, h], v_ref[rows[s], h * DH:(h + 1) * DH], galls[s][:, h:h + 1],
                                    gbs[s][:, H + h:H + h + 1], r, c, eye) for s, h in keys}
        invs = dict(zip(keys, _tri_inverse([cm[key][8] for key in keys], eye, r, c)))
        uws = {key: _bdot(invs[key], jnp.concatenate([cm[key][6], cm[key][5] * cm[key][3]], axis=1)) for key in keys}
        sts = [state[h] for h in heads]
        for s in range(GDN_STEP):
            vns = [uws[s, h][:, :DH] - _bdot(uws[s, h][:, DH:], sts[h]) for h in heads]
            outs = [_bdot(qs[s, h] * cm[s, h][3], sts[h]) + _bdot(cm[s, h][10], vns[h]) for h in heads]
            news = [sts[h] * jnp.exp(cm[s, h][2]) + _bdot_tn(ks[s, h] * cm[s, h][4], vns[h]) for h in heads]
            for h in heads:
                s_ref[s, h] = sts[h].astype(bf16)
                t_ref[s, h] = invs[s, h]
                o_ref[rows[s], h * DH:(h + 1) * DH] = outs[h]
            sts = news
        for h in heads:
            state[h] = sts[h]

    tb = GDN_STEP * CH
    return pl.pallas_call(
        body, name="gdn_fwd", grid=(t // tb,),
        in_specs=[_row(tb, D), _row(tb, D), _row(tb, D), _row(tb, 128)],
        out_specs=[_row(tb, D), pl.BlockSpec((GDN_STEP, H, DH, DH), lambda i: (i, 0, 0, 0)),
                   pl.BlockSpec((GDN_STEP, H, CH, CH), lambda i: (i, 0, 0, 0))],
        out_shape=[jax.ShapeDtypeStruct((t, D), f32), jax.ShapeDtypeStruct((n_chunks, H, DH, DH), bf16),
                   jax.ShapeDtypeStruct((n_chunks, H, CH, CH), f32)],
        scratch_shapes=[pltpu.VMEM((H, DH, DH), f32)],
        compiler_params=_params(1),
    )(qn, kn, vc, gbeta)


def _gdn_bwd(qn, kn, vc, gbeta, do, s_all, t_all):
    t = qn.shape[0]

    def body(q_ref, k_ref, v_ref, gb_ref, do_ref, s_ref, t_ref, dq_ref, dk_ref, dv_ref, dgb_ref, dstate):
        @pl.when(pl.program_id(0) == 0)
        def _():
            dstate[...] = jnp.zeros_like(dstate)

        r, c, eye = _chunk_consts()
        tril = r >= c
        lane = lax.broadcasted_iota(jnp.int32, (1, 128), 1)
        hs = range(H)

        def each(fn, *lists):
            return [fn(*args) for args in zip(*lists)]

        def rsum(a):
            return jnp.sum(a, axis=1, keepdims=True)

        def before_state(s):
            rows = slice(s * CH, (s + 1) * CH)
            gb = gb_ref[rows, :]
            gall = _hdot(tril.astype(f32), gb)
            p = {"rows": rows}
            p["q"] = q = [q_ref[rows, h * DH:(h + 1) * DH] for h in hs]
            p["k"] = k = [k_ref[rows, h * DH:(h + 1) * DH] for h in hs]
            p["v"] = v = [v_ref[rows, h * DH:(h + 1) * DH] for h in hs]
            p["dout"] = dout = [do_ref[rows, h * DH:(h + 1) * DH] for h in hs]
            p["inv"] = inv = [t_ref[s, h] for h in hs]
            p["st"] = st = [s_ref[s, h] for h in hs]
            p["bcol"] = bcol = [gb[:, H + h:H + h + 1] for h in hs]
            cm = [_chunk_common(q[h], k[h], v[h], gall[:, h:h + 1], bcol[h], r, c, eye) for h in hs]
            for name, i in (("dec", 1), ("glast", 2), ("eg", 3), ("el", 4), ("kb", 5), ("vb", 6), ("low", 8), ("att", 10)):
                p[name] = [m[i] for m in cm]
            p["rcol"] = cm[0][11]
            p["elast"] = each(jnp.exp, p["glast"])
            p["kbg"] = each(jnp.multiply, p["kb"], p["eg"])
            uw = each(lambda i, a, b: _bdot(i, jnp.concatenate([a, b], axis=1)), inv, p["vb"], p["kbg"])
            p["u"] = [a[:, :DH] for a in uw]
            p["w"] = [a[:, DH:] for a in uw]
            p["vn"] = each(lambda a, b, x: a - _bdot(b, x), p["u"], p["w"], st)
            p["qd"] = each(jnp.multiply, q, p["eg"])
            p["kd"] = each(jnp.multiply, k, p["el"])
            p["dqd"] = each(_bdot_nt, dout, st)
            p["datt"] = each(lambda d, x: jnp.where(tril, _bdot_nt(d, x), 0.0), dout, p["vn"])
            p["dqk"] = each(jnp.multiply, p["datt"], p["dec"])
            p["qd_do"] = each(_bdot_tn, p["qd"], dout)
            p["att_do"] = each(_bdot_tn, p["att"], dout)
            return p

        def after_state(p, ds):
            q, k, v, st, inv, bcol = p["q"], p["k"], p["v"], p["st"], p["inv"], p["bcol"]
            eg, el, kb, u, w = p["eg"], p["el"], p["kb"], p["u"], p["w"]
            dvn = each(lambda a, kk, x: a + _bdot(kk, x), p["att_do"], p["kd"], ds)
            dkd = each(_bdot_nt, p["vn"], ds)
            dw = each(lambda a, x: -_bdot_nt(a, x), dvn, st)
            new_ds = each(lambda x, e, a, ww, dv_: x * e + a - _bdot_tn(ww, dv_), ds, p["elast"], p["qd_do"], w, dvn)
            dglast = each(lambda e, x, d: e * jnp.sum(rsum(x.astype(f32) * d), axis=0, keepdims=True), p["elast"], st, ds)
            dr = each(lambda i, a, b: _bdot_tn(i, jnp.concatenate([a, b], axis=1)), inv, dvn, dw)
            dvb = [a[:, :DH] for a in dr]
            dkbg = [a[:, DH:] for a in dr]
            dlow = each(lambda a, b, x, y: -jnp.where(r > c, _bdot_nt(a, b) + _bdot_nt(x, y), 0.0), dvb, u, dkbg, w)
            dkk = each(jnp.multiply, dlow, p["dec"])
            mm = each(lambda a, b, x, y: a * b + x * y, dlow, p["low"], p["datt"], p["att"])
            dkb = each(lambda a, kk, b, e: _bdot(a, kk) + b * e, dkk, k, dkbg, eg)
            dk = each(lambda a, b, x, y, d, e, f, g: _bdot_tn(a, b) + _bdot_tn(x, y) + d * e + f * g,
                      dkk, kb, p["dqk"], q, dkd, el, dkb, bcol)
            dq = each(lambda a, kk, d, e: _bdot(a, kk) + d * e, p["dqk"], k, p["dqd"], eg)
            dv = each(jnp.multiply, dvb, bcol)
            dbeta = each(lambda a, b, x, y: rsum(a * b) + rsum(x * y), dkb, k, dvb, v)
            deg = each(lambda a, b, x, y: rsum(a * b) + rsum(x * y), dkbg, kb, p["dqd"], q)
            delc = each(lambda a, b, e: rsum(a * b) * e, dkd, k, el)
            dgc = each(lambda m, a, e, d: rsum(m) - rsum(eye * jnp.sum(m, axis=0, keepdims=True)) + a * e - d,
                       mm, deg, eg, delc)
            dgc = each(lambda g, d, l: g + jnp.where(p["rcol"] == CH - 1, jnp.sum(d, axis=0, keepdims=True) + l, 0.0),
                       dgc, delc, dglast)
            dg_acc = jnp.zeros((CH, 128), f32)
            db_acc = jnp.zeros((CH, 128), f32)
            rows = p["rows"]
            for h in hs:
                dq_ref[rows, h * DH:(h + 1) * DH] = dq[h]
                dk_ref[rows, h * DH:(h + 1) * DH] = dk[h]
                dv_ref[rows, h * DH:(h + 1) * DH] = dv[h]
                dg_acc = dg_acc + dgc[h] * (lane == h).astype(f32)
                db_acc = db_acc + dbeta[h] * (lane == H + h).astype(f32)
            dgb_ref[rows, :] = _hdot((r <= c).astype(f32), dg_acc) + db_acc
            return new_ds

        order = list(reversed(range(GDN_STEP)))
        pre = [before_state(s) for s in order]
        ds = [dstate[h] for h in hs]
        for p in pre:
            ds = after_state(p, ds)
        for h in hs:
            dstate[h] = ds[h]

    tb = GDN_STEP * CH
    n_steps = t // tb
    rev = lambda i: (n_steps - 1 - i, 0)
    rev4 = lambda i: (n_steps - 1 - i, 0, 0, 0)
    return pl.pallas_call(
        body, name="gdn_bwd", grid=(n_steps,),
        in_specs=[pl.BlockSpec((tb, D), rev), pl.BlockSpec((tb, D), rev), pl.BlockSpec((tb, D), rev),
                  pl.BlockSpec((tb, 128), rev), pl.BlockSpec((tb, D), rev),
                  pl.BlockSpec((GDN_STEP, H, DH, DH), rev4), pl.BlockSpec((GDN_STEP, H, CH, CH), rev4)],
        out_specs=[pl.BlockSpec((tb, D), rev), pl.BlockSpec((tb, D), rev), pl.BlockSpec((tb, D), rev),
                   pl.BlockSpec((tb, 128), rev)],
        out_shape=[jax.ShapeDtypeStruct((t, D), f32)] * 3 + [jax.ShapeDtypeStruct((t, 128), f32)],
        scratch_shapes=[pltpu.VMEM((H, DH, DH), f32)],
        compiler_params=_params(1),
    )(qn, kn, vc, gbeta, do, s_all, t_all)


def _pad_rows(w, rows=8):
    return jnp.pad(w, ((0, rows - w.shape[0]), (0, 0)))


_REST = ("w_up", "w_a_out", "w_b_out", "w_o", "w_down")


def _local_step(x, tgt, w, comm=None):
    g1 = w["norm_mix_g"].reshape(1, D)
    if comm is None:
        h1 = _rms_fwd(x, g1, name="rms1_fwd")
    else:
        h1, gathered = _rms_fwd(x, g1, name="rms1_fwd", exchange=comm.gather_first())
        w = {**w, **comm.finish_first(gathered)}
    w1, w2 = w["w1"], w["w2"]
    wa = _pad_rows(w["conv_a_w"])
    wg = _pad_rows(w["gdn_conv_w"])
    wf = _pad_rows(w["ffn_conv_w"])
    alog = jnp.pad(w["gdn_A_log"].reshape(1, H), ((0, 0), (0, 128 - H)))
    dtb = jnp.pad(w["gdn_dt_bias"].reshape(1, H), ((0, 0), (0, 128 - H)))
    g2 = w["norm_ffn_g"].reshape(1, D)
    g3 = w["norm_final_g"].reshape(1, D)
    gn = w["gdn_norm_g"].reshape(1, DH)

    if comm is None:
        pg = _matmul(h1, w1, name="mm_in", cols=(0, 6 * D), out_dtype=bf16)
    else:
        pg, gathered = _matmul(h1, w1, name="mm_in", cols=(0, 6 * D), out_dtype=bf16, exchange=comm.gather_rest())
        w = {**w, **comm.finish_gather(gathered)}
    pq = _matmul(h1, w1, name="mm_in_qkv", cols=(6 * D, 3 * D))
    p2 = _matmul(h1, w2, name="mm_in_ab")
    ya_in, qn, kn, vc, gbeta = _pre_fwd(pg, pq, p2, wa, wg, alog, dtb)
    o, s_all, t_all = _gdn_fwd(qn, kn, vc, gbeta)
    yb_in = _post_fwd(o, pg, gn)
    ya = _matmul(ya_in, w["w_a_out"], name="mm_a", out_dtype=bf16)
    yb = _matmul(yb_in, w["w_b_out"], name="mm_b", out_dtype=bf16)
    mix = _mix_fwd(ya, yb, pg)
    x2 = _matmul(mix, w["w_o"], name="mm_o", add=x)
    h2 = _rms_fwd(x2, g2, name="rms2_fwd")
    up = _matmul(h2, w["w_up"], name="mm_up", tn=DFF // 2, out_dtype=bf16)
    act = _ffn_fwd(up, wf)
    x3 = _matmul(act, w["w_down"], name="mm_down", add=x2, tm=512)
    loss_p, dx3, dx3b, dg3 = _final(x3, tgt, g3)

    grads = {"norm_final_g": dg3}
    dact = _matmul(dx3b, w["w_down"], nt=True, name="mm_down_dx", tm=512, tn=DFF, out_dtype=bf16)
    grads["w_down"] = _matmul_tn(act, dx3b, name="mm_down_dw", tm=DFF // 2)
    dc, dwf = _ffn_bwd1(dact, up, wf)
    grads["ffn_conv_w"] = dwf
    dup = _ffn_bwd2(dc, wf)
    dh2 = _matmul(dup, w["w_up"], nt=True, name="mm_up_dx", tk=DFF)
    grads["w_up"] = _matmul_tn(h2, dup, name="mm_up_dw", tn=512)
    dx2, dx2b, dg2 = _rms_bwd(dh2, x2, g2, dx3, name="rms2_bwd")
    grads["norm_ffn_g"] = dg2
    dmix = _matmul(dx2b, w["w_o"], nt=True, name="mm_o_dx", out_dtype=bf16)
    grads["w_o"] = _matmul_tn(mix, dx2b, name="mm_o_dw")
    dya, dyb, dgates = _mix_bwd(dmix, ya, yb, pg)
    dya_in = _matmul(dya, w["w_a_out"], nt=True, name="mm_a_dx", out_dtype=bf16)
    grads["w_a_out"] = _matmul_tn(ya_in, dya, name="mm_a_dw")
    dyb_in = _matmul(dyb, w["w_b_out"], nt=True, name="mm_b_dx")
    grads["w_b_out"] = _matmul_tn(yb_in, dyb, name="mm_b_dw")
    do, dz, dgn = _post_bwd(dyb_in, o, pg, gn)
    grads["gdn_norm_g"] = dgn
    dqn, dkn, dvc, dgb = _gdn_bwd(qn, kn, vc, gbeta, do, s_all, t_all)
    dbg, dca, dc4, dp2, dwa, dwg, dal, ddt = _pre_bwd1(pg, pq, p2, dya_in, dqn, dkn, dvc, dgb, gbeta, wa, wg, alog, dtb)
    grads["conv_a_w"] = dwa
    grads["gdn_conv_w"] = dwg
    grads["gdn_A_log"] = dal
    grads["gdn_dt_bias"] = ddt
    grads["w2"] = _matmul_tn(h1, dp2, name="mm_in_ab_dw")
    if comm is None:
        dp1 = _pre_bwd2(dca, dc4, pg, dbg, dz, dgates, wa, wg)
        grads["w1"] = _matmul_tn(h1, dp1, name="mm_in_dw")
        dh1 = _matmul(dp1, w1, nt=True, name="mm_in_dx", tm=512, tk=NW1 // 2)
    else:
        exchange, blocks = comm.reduce_halves(_REST, grads)
        dp1, recv = _pre_bwd2(dca, dc4, pg, dbg, dz, dgates, wa, wg, exchange=exchange)
        exchange, sums = comm.reduce_sums(_REST, blocks, recv)
        grads["w1"], recv = _matmul_tn(h1, dp1, name="mm_in_dw", exchange=exchange)
        comm.finish_reduce(_REST, sums, recv)
        exchange, blocks = comm.reduce_halves(("w_in",), grads)
        exchange, sums = comm.reduce_sums(("w_in",), blocks, _run_exchange(exchange, name="rs_sibling_w_in"))
        dh1, recv = _matmul(dp1, w1, nt=True, name="mm_in_dx", tm=512, tk=NW1 // 2, exchange=exchange)
        comm.finish_reduce(("w_in",), sums, recv)
    dh1 = _matmul(dp2, w2, nt=True, name="mm_in_ab_dx", add=dh1)
    dx, _, dg1 = _rms_bwd(dh1, x, g1, dx2, name="rms1_bwd")
    grads["norm_mix_g"] = dg1
    return loss_p, dx, grads


_ANY = pl.BlockSpec(memory_space=pl.ANY)


def _remote(src, dst, send_sem, recv_sem, to):
    return pltpu.make_async_remote_copy(src_ref=src, dst_ref=dst, send_sem=send_sem, recv_sem=recv_sem,
                                        device_id=to, device_id_type=MESH)


def _run_exchange(exchange, *, name):
    arrays, shapes, sems, start, wait = exchange
    n_in, n_out = len(arrays), len(shapes)

    def body(*refs):
        start(refs[:n_in], refs[n_in:n_in + n_out], refs[n_in + n_out:])
        wait(refs[:n_in], refs[n_in:n_in + n_out], refs[n_in + n_out:])

    return pl.pallas_call(body, name=name, out_shape=list(shapes), in_specs=[_ANY] * n_in, out_specs=[_ANY] * n_out,
                          scratch_shapes=list(sems))(*arrays)


def _gather_exchange(shards):
    n = len(shards)

    def copies(x_refs, out_refs, sems):
        send_sems, recv_sems, local_sems = sems
        x, y, c = lax.axis_index("x"), lax.axis_index("y"), lax.axis_index("c")
        me, sibling = (x, y, c), (x, y, 1 - c)
        chips = [(1 - x, y), (x, 1 - y), (1 - x, 1 - y)]

        def copy(a, k, blk, to, from_input=False):
            dst = out_refs[a].at[4 * blk[0] + 2 * blk[1] + blk[2]]
            return _remote(x_refs[a] if from_input else dst, dst, send_sems.at[a, k], recv_sems.at[a, k], to)

        mine = [pltpu.make_async_copy(x_refs[a], out_refs[a].at[4 * x + 2 * y + c], local_sems.at[a]) for a in range(n)]
        first = []
        for a in range(n):
            first.append(copy(a, 0, me, sibling, from_input=True))
            first += [copy(a, 1 + j, me, (*chip, c), from_input=True) for j, chip in enumerate(chips)]
        return copy, mine, first, me, sibling, chips, c

    def start(x_refs, out_refs, sems):
        _, mine, first, *_ = copies(x_refs, out_refs, sems)
        for cp in mine + first:
            cp.start()

    def wait(x_refs, out_refs, sems):
        copy, mine, first, me, sibling, chips, c = copies(x_refs, out_refs, sems)
        passed = []
        for j, chip in enumerate(chips):
            for a in range(n):
                copy(a, 1 + j, (*chip, c), me).wait_recv()
                passed.append(copy(a, 4 + j, (*chip, c), sibling))
                passed[-1].start()
        for a in range(n):
            copy(a, 0, sibling, me).wait_recv()
            for j, chip in enumerate(chips):
                copy(a, 4 + j, (*chip, 1 - c), me).wait_recv()
        for cp in first + passed:
            cp.wait_send()
        for cp in mine:
            cp.wait()

    shapes = [jax.ShapeDtypeStruct((N_DEV, *s.shape), s.dtype) for s in shards]
    sems = [pltpu.SemaphoreType.DMA((n, 7)), pltpu.SemaphoreType.DMA((n, 7)), pltpu.SemaphoreType.DMA((n,))]
    return shards, shapes, sems, start, wait


def _gather_direct_exchange(shards):
    n = len(shards)

    def copies(x_refs, out_refs, sems):
        send_sems, recv_sems, local_sems = sems
        x, y, c = lax.axis_index("x"), lax.axis_index("y"), lax.axis_index("c")
        targets = [(x, y, 1 - c), (1 - x, y, c), (x, 1 - y, c), (1 - x, 1 - y, c)]
        local, sends, recvs = [], [], []
        for a in range(n):
            mine = out_refs[a].at[4 * x + 2 * y + c]
            local.append(pltpu.make_async_copy(x_refs[a], mine, local_sems.at[a]))
            for k, to in enumerate(targets):
                theirs = out_refs[a].at[4 * to[0] + 2 * to[1] + to[2]]
                sends.append(_remote(x_refs[a], mine, send_sems.at[a, k], recv_sems.at[a, k], to))
                recvs.append(_remote(theirs, theirs, send_sems.at[a, k], recv_sems.at[a, k], to))
        return local, sends, recvs

    def start(x_refs, out_refs, sems):
        local, sends, _ = copies(x_refs, out_refs, sems)
        for cp in local + sends:
            cp.start()

    def wait(x_refs, out_refs, sems):
        local, sends, recvs = copies(x_refs, out_refs, sems)
        for cp in recvs:
            cp.wait_recv()
        for cp in sends:
            cp.wait_send()
        for cp in local:
            cp.wait()

    shapes = [jax.ShapeDtypeStruct((N_DEV, *s.shape), s.dtype) for s in shards]
    sems = [pltpu.SemaphoreType.DMA((n, 4)), pltpu.SemaphoreType.DMA((n, 4)), pltpu.SemaphoreType.DMA((n,))]
    return shards, shapes, sems, start, wait


def _gather_forward(gathered):
    n = len(gathered)

    def body(*refs):
        out_refs = refs[n:2 * n]
        send_sems, recv_sems = refs[2 * n:]
        x, y, c = lax.axis_index("x"), lax.axis_index("y"), lax.axis_index("c")
        sibling = (x, y, 1 - c)
        sends, recvs = [], []
        for a in range(n):
            for j, (px, py) in enumerate([(1 - x, y), (x, 1 - y), (1 - x, 1 - y)]):
                mine = out_refs[a].at[4 * px + 2 * py + c]
                theirs = out_refs[a].at[4 * px + 2 * py + 1 - c]
                sends.append(_remote(mine, mine, send_sems.at[a, j], recv_sems.at[a, j], sibling))
                recvs.append(_remote(theirs, theirs, send_sems.at[a, j], recv_sems.at[a, j], sibling))
        for cp in sends:
            cp.start()
        for cp in recvs:
            cp.wait_recv()
        for cp in sends:
            cp.wait_send()

    return pl.pallas_call(
        body, name="ag_forward", out_shape=[jax.ShapeDtypeStruct(g.shape, g.dtype) for g in gathered],
        in_specs=[_ANY] * n, out_specs=[_ANY] * n, input_output_aliases={a: a for a in range(n)},
        scratch_shapes=[pltpu.SemaphoreType.DMA((n, 3)), pltpu.SemaphoreType.DMA((n, 3))],
    )(*gathered)


def _chips_exchange(hsums):
    n = len(hsums)

    def copies(h_refs, out_refs, sems):
        send_sems, recv_sems = sems
        x, y, c = lax.axis_index("x"), lax.axis_index("y"), lax.axis_index("c")
        chips = [(1 - x, y), (x, 1 - y), (1 - x, 1 - y)]
        return [_remote(h_refs[a].at[2 * px + py], out_refs[a].at[k], send_sems.at[a, k], recv_sems.at[a, k], (px, py, c))
                for a in range(n) for k, (px, py) in enumerate(chips)]

    def start(h_refs, out_refs, sems):
        for cp in copies(h_refs, out_refs, sems):
            cp.start()

    def wait(h_refs, out_refs, sems):
        for cp in copies(h_refs, out_refs, sems):
            cp.wait()

    shapes = [jax.ShapeDtypeStruct((3, *h.shape[1:]), h.dtype) for h in hsums]
    sems = [pltpu.SemaphoreType.DMA((n, 3)), pltpu.SemaphoreType.DMA((n, 3))]
    return hsums, shapes, sems, start, wait


def _sibling_exchange(halves):
    n = len(halves)

    def copies(p_refs, out_refs, sems):
        send_sems, recv_sems = sems
        x, y, c = lax.axis_index("x"), lax.axis_index("y"), lax.axis_index("c")
        return [_remote(p_refs[a], out_refs[a], send_sems.at[a], recv_sems.at[a], (x, y, 1 - c)) for a in range(n)]

    def start(p_refs, out_refs, sems):
        for cp in copies(p_refs, out_refs, sems):
            cp.start()

    def wait(p_refs, out_refs, sems):
        for cp in copies(p_refs, out_refs, sems):
            cp.wait()

    shapes = [jax.ShapeDtypeStruct(h.shape, h.dtype) for h in halves]
    return halves, shapes, [pltpu.SemaphoreType.DMA((n,)), pltpu.SemaphoreType.DMA((n,))], start, wait


_IN_RANGES = ((0, 3 * D, 0, 0), (3 * D, 6 * D, 0, 6 * D), (6 * D, 7 * D, 0, 3 * D), (7 * D, 7 * D + 16, 1, 0),
              (7 * D + 16, 9 * D + 16, 0, 4 * D))
_UP_RANGES = ((0, 2 * DFF, 0, 0),)


def _col_pieces(width, ranges):
    pieces = []
    for d in range(N_DEV):
        lo, hi = d * width, (d + 1) * width
        for glo, ghi, mat, mlo in ranges:
            a, b = max(lo, glo), min(hi, ghi)
            if a < b:
                pieces.append((d, a - lo, b - lo, mat, mlo + a - glo))
    return pieces


def _cols_to_matrices(g, ranges, out_widths, *, name):
    _, rows, width = g.shape
    tb = 128
    pieces = _col_pieces(width, ranges)
    covered = [sum(p[2] - p[1] for p in pieces if p[3] == m) for m in range(len(out_widths))]

    def body(g_ref, *o_refs):
        for m, o_ref in enumerate(o_refs):
            if covered[m] < out_widths[m]:
                o_ref[...] = jnp.zeros_like(o_ref)
        for d, b0, b1, m, m0 in pieces:
            o_refs[m][:, m0:m0 + b1 - b0] = g_ref[d, :, b0:b1]

    return pl.pallas_call(
        body, name=name, grid=(rows // tb,), in_specs=[pl.BlockSpec((N_DEV, tb, width), lambda i: (0, i, 0))],
        out_specs=[pl.BlockSpec((tb, wo), lambda i: (i, 0)) for wo in out_widths],
        out_shape=[jax.ShapeDtypeStruct((rows, wo), g.dtype) for wo in out_widths], compiler_params=_params(1),
    )(g)


def _matrices_to_cols(mats, ranges, width, *, name):
    rows = mats[0].shape[0]
    tb = 128
    pieces = _col_pieces(width, ranges)

    def body(*refs):
        m_refs, g_ref = refs[:-1], refs[-1]
        for d, b0, b1, m, m0 in pieces:
            g_ref[d, :, b0:b1] = m_refs[m][:, m0:m0 + b1 - b0]

    return pl.pallas_call(
        body, name=name, grid=(rows // tb,),
        in_specs=[pl.BlockSpec((tb, mt.shape[1]), lambda i: (i, 0)) for mt in mats],
        out_specs=pl.BlockSpec((N_DEV, tb, width), lambda i: (0, i, 0)),
        out_shape=jax.ShapeDtypeStruct((N_DEV, rows, width), mats[0].dtype), compiler_params=_params(1),
    )(*mats)


def _row_block(rows):
    return 128 if rows % 128 == 0 else rows


def _half_bf16(g4, c_other, *, name):
    _, _, rows, width = g4.shape
    tb = _row_block(rows)

    def body(c_ref, p_ref, o_ref):
        o_ref[0] = p_ref[0, 0].astype(bf16)

    grid_spec = pltpu.PrefetchScalarGridSpec(
        num_scalar_prefetch=1, grid=(4, rows // tb),
        in_specs=[pl.BlockSpec((1, 1, tb, width), lambda j, i, c_ref: (j, c_ref[0], i, 0))],
        out_specs=pl.BlockSpec((1, tb, width), lambda j, i, c_ref: (j, i, 0)))
    return pl.pallas_call(
        body, name=name, grid_spec=grid_spec, out_shape=jax.ShapeDtypeStruct((4, rows, width), bf16),
        compiler_params=_params(2),
    )(c_other, g4)


def _pair_sum(g4, recv, c_me, *, name):
    _, _, rows, width = g4.shape
    tb = _row_block(rows)

    def body(c_ref, p_ref, r_ref, o_ref, ob_ref):
        s = p_ref[0, 0] + r_ref[0].astype(f32)
        o_ref[0] = s
        ob_ref[0] = s.astype(bf16)

    blk = pl.BlockSpec((1, tb, width), lambda j, i, c_ref: (j, i, 0))
    grid_spec = pltpu.PrefetchScalarGridSpec(
        num_scalar_prefetch=1, grid=(4, rows // tb),
        in_specs=[pl.BlockSpec((1, 1, tb, width), lambda j, i, c_ref: (j, c_ref[0], i, 0)), blk],
        out_specs=[blk, blk])
    return pl.pallas_call(
        body, name=name, grid_spec=grid_spec,
        out_shape=[jax.ShapeDtypeStruct((4, rows, width), f32), jax.ShapeDtypeStruct((4, rows, width), bf16)],
        compiler_params=_params(2),
    )(c_me, g4, recv)


def _adam_shard(hsum, recv, chip, w, m, v, *, name):
    _, rows, width = w.shape
    tb = _row_block(rows)

    def body(j_ref, h_ref, r_ref, w_ref, m_ref, v_ref, g_out, d_out, m_out, v_out):
        g = ((h_ref[0] + r_ref[0].astype(f32)) + r_ref[1].astype(f32)) + r_ref[2].astype(f32)
        delta, mn, vn = _adam_math(w_ref[0], g, m_ref[0], v_ref[0])
        g_out[0] = g
        d_out[0] = delta
        m_out[0] = mn
        v_out[0] = vn

    blk = pl.BlockSpec((1, tb, width), lambda i, j_ref: (0, i, 0))
    grid_spec = pltpu.PrefetchScalarGridSpec(
        num_scalar_prefetch=1, grid=(rows // tb,),
        in_specs=[pl.BlockSpec((1, tb, width), lambda i, j_ref: (j_ref[0], i, 0)),
                  pl.BlockSpec((3, tb, width), lambda i, j_ref: (0, i, 0)), blk, blk, blk],
        out_specs=[blk, blk, blk, blk])
    return pl.pallas_call(
        body, name=name, grid_spec=grid_spec, out_shape=[jax.ShapeDtypeStruct(w.shape, f32)] * 4,
        compiler_params=_params(1),
    )(chip, hsum, recv, w, m, v)


R_SMALL = 16 + 16 * N_DEV
_SMALL_LANES = {"gdn_norm_g": (0, DH), "gdn_A_log": (DH, DH + H), "gdn_dt_bias": (2 * DH, 2 * DH + H)}
_LOSS_LANE = 3 * DH


def _pack_small(dg1, dg2, dg3, dgn, dal, ddt, loss_p, dwa, dwg, dwf):
    def body(dg1_ref, dg2_ref, dg3_ref, dgn_ref, dal_ref, ddt_ref, loss_ref, dwa_ref, dwg_ref, dwf_ref, o_ref):
        def total(ref):
            return jnp.sum(ref[...], axis=0, keepdims=True)

        o_ref[...] = jnp.zeros_like(o_ref)
        o_ref[0:1, :] = total(dg1_ref)
        o_ref[1:2, :] = total(dg2_ref)
        o_ref[2:3, :] = total(dg3_ref)
        o_ref[3:4, 0:DH] = total(dgn_ref)
        o_ref[3:4, DH:2 * DH] = total(dal_ref)
        o_ref[3:4, 2 * DH:3 * DH] = total(ddt_ref)
        o_ref[3:4, 3 * DH:4 * DH] = total(loss_ref)
        for d in range(N_DEV):
            base = 16 + 16 * d
            o_ref[base:base + 3, 0:128] = dwa_ref[0:3, 128 * d:128 * (d + 1)]
            o_ref[base:base + 4, 128:512] = dwg_ref[0:4, 384 * d:384 * (d + 1)]
            o_ref[base + 8:base + 11, 0:704] = dwf_ref[0:3, 704 * d:704 * (d + 1)]

    return pl.pallas_call(body, name="pack_small", out_shape=jax.ShapeDtypeStruct((R_SMALL, D), f32))(
        dg1, dg2, dg3, dgn, dal, ddt, loss_p, dwa, dwg, dwf)


_SMALL = ("norm_mix_g", "norm_ffn_g", "norm_final_g", "gdn_norm_g", "gdn_A_log", "gdn_dt_bias",
          "conv_a_w", "gdn_conv_w", "ffn_conv_w")


def _adam_small(gath, me, w, m, v):
    arrays = [t[n] for n in _SMALL for t in (w, m, v)]

    def body(me_ref, ga_ref, gb_ref, *refs):
        ins, outs = refs[:len(arrays)], refs[len(arrays):]
        ga, gb = ga_ref[0], gb_ref[0]
        for s in range(1, N_DEV):
            ga = ga + ga_ref[s]
            gb = gb + gb_ref[s]
        grads = {"norm_mix_g": ga[0:1, :], "norm_ffn_g": ga[1:2, :], "norm_final_g": ga[2:3, :],
                 "conv_a_w": gb[0:3, 0:128], "gdn_conv_w": gb[0:4, 128:512], "ffn_conv_w": gb[8:11, 0:704]}
        for n, (lo, hi) in _SMALL_LANES.items():
            grads[n] = ga[3:4, lo:hi]
        for i, n in enumerate(_SMALL):
            three_d = len(w[n].shape) == 3
            wv, mv, vv = (r[0] if three_d else r[...] for r in ins[3 * i:3 * i + 3])
            delta, mn, vn = _adam_math(wv, grads[n], mv, vv)
            for o_ref, val in zip(outs[4 * i:4 * i + 4], (grads[n], delta, mn, vn)):
                if three_d:
                    o_ref[0] = val
                else:
                    o_ref[...] = val
        outs[-1][...] = ga[3:4, _LOSS_LANE:_LOSS_LANE + 1]

    def whole(shape):
        return pl.BlockSpec(shape, lambda i, me_ref: (0,) * len(shape))

    grid_spec = pltpu.PrefetchScalarGridSpec(
        num_scalar_prefetch=1, grid=(1,),
        in_specs=[pl.BlockSpec((N_DEV, 16, D), lambda i, me_ref: (0, 0, 0)),
                  pl.BlockSpec((N_DEV, 16, D), lambda i, me_ref: (0, 1 + me_ref[0], 0))] + [whole(a.shape) for a in arrays],
        out_specs=[whole(w[n].shape) for n in _SMALL for _ in range(4)] + [whole((1, 1))])
    res = pl.pallas_call(
        body, name="adam_small", grid_spec=grid_spec,
        out_shape=[jax.ShapeDtypeStruct(w[n].shape, f32) for n in _SMALL for _ in range(4)]
        + [jax.ShapeDtypeStruct((1, 1), f32)],
        compiler_params=_params(1),
    )(me, gath, gath, *arrays)
    return {n: tuple(res[4 * i:4 * i + 4]) for i, n in enumerate(_SMALL)}, res[-1]


def _adam_math(w, g, m, v):
    m = ADAM_B1 * m + (1.0 - ADAM_B1) * g
    v = ADAM_B2 * v + (1.0 - ADAM_B2) * jnp.square(g)
    m_hat = m / (1.0 - ADAM_B1 ** ADAM_STEP)
    v_hat = v / (1.0 - ADAM_B2 ** ADAM_STEP)
    delta = -ADAM_LR * (m_hat / (jnp.sqrt(v_hat) + ADAM_EPS) + ADAM_WD * w)
    return delta, m, v


_WEIGHTS = ("norm_mix_g", "w_in", "conv_a_w", "gdn_conv_w", "gdn_A_log", "gdn_dt_bias", "gdn_norm_g", "w_a_out",
            "w_b_out", "w_o", "norm_ffn_g", "w_up", "ffn_conv_w", "w_down", "norm_final_g")
_BIG = ("w_in",) + _REST
_CONVS = ("conv_a_w", "gdn_conv_w", "ffn_conv_w")


class _StepExchanges:
    def __init__(self, wts, mom, var, c_me, chip):
        self.wts, self.mom, self.var, self.c_me, self.chip = wts, mom, var, c_me, chip
        self.results = {}

    def gather_first(self):
        return _gather_exchange([self.wts["w_in"][0].astype(bf16)] + [self.wts[n][0] for n in _CONVS])

    def finish_first(self, gathered):
        g_in, gc_a, gc_g, gc_f = gathered
        w1, w2 = _cols_to_matrices(g_in, _IN_RANGES, (NW1, 128), name="relay_w_in")
        return {"w1": w1, "w2": w2, "conv_a_w": gc_a.transpose(1, 0, 2).reshape(3, D),
                "gdn_conv_w": gc_g.transpose(1, 0, 2).reshape(4, 3 * D),
                "ffn_conv_w": gc_f.transpose(1, 0, 2).reshape(3, 2 * DFF)}

    def gather_rest(self):
        return _gather_direct_exchange([self.wts[n][0].astype(bf16) for n in _REST])

    def finish_gather(self, gathered):
        g_up, g_a, g_b, g_o, g_down = _gather_forward(gathered)
        (w_up,) = _cols_to_matrices(g_up, _UP_RANGES, (2 * DFF,), name="relay_w_up")
        return {"w_up": w_up, "w_a_out": g_a.reshape(D, D), "w_b_out": g_b.reshape(D, D), "w_o": g_o.reshape(D, D),
                "w_down": g_down.reshape(DFF, D)}

    def reduce_halves(self, names, grads):
        blocks = []
        for n in names:
            if n == "w_in":
                g = _matrices_to_cols([grads["w1"], grads["w2"]], _IN_RANGES, R_IN, name="relay_dw_in")
            elif n == "w_up":
                g = _matrices_to_cols([grads[n]], _UP_RANGES, R_UP, name="relay_dw_up")
            else:
                g = grads[n]
            blocks.append(g.reshape(4, 2, *self.wts[n].shape[1:]))
        return _sibling_exchange([_half_bf16(g, 1 - self.c_me, name="rs_half_" + n) for n, g in zip(names, blocks)]), blocks

    def reduce_sums(self, names, blocks, recv):
        sums = [_pair_sum(g, r, self.c_me, name="rs_sum_" + n) for n, g, r in zip(names, blocks, recv)]
        return _chips_exchange([s[1] for s in sums]), [s[0] for s in sums]

    def finish_reduce(self, names, sums, recv):
        for n, s, r in zip(names, sums, recv):
            self.results[n] = _adam_shard(s, r, self.chip, self.wts[n], self.mom[n], self.var[n], name="adam_" + n)


def kernel(x, norm_mix_g, w_in, conv_a_w, gdn_conv_w, gdn_A_log, gdn_dt_bias, gdn_norm_g, w_a_out, w_b_out, w_o, norm_ffn_g, w_up, ffn_conv_w, w_down, norm_final_g, loss_target, m_norm_mix_g, m_w_in, m_conv_a_w, m_gdn_conv_w, m_gdn_A_log, m_gdn_dt_bias, m_gdn_norm_g, m_w_a_out, m_w_b_out, m_w_o, m_norm_ffn_g, m_w_up, m_ffn_conv_w, m_w_down, m_norm_final_g, v_norm_mix_g, v_w_in, v_conv_a_w, v_gdn_conv_w, v_gdn_A_log, v_gdn_dt_bias, v_gdn_norm_g, v_w_a_out, v_w_b_out, v_w_o, v_norm_ffn_g, v_w_up, v_ffn_conv_w, v_w_down, v_norm_final_g):
    wts = dict(zip(_WEIGHTS, (norm_mix_g, w_in, conv_a_w, gdn_conv_w, gdn_A_log, gdn_dt_bias, gdn_norm_g, w_a_out,
                              w_b_out, w_o, norm_ffn_g, w_up, ffn_conv_w, w_down, norm_final_g)))
    mom = dict(zip(_WEIGHTS, (m_norm_mix_g, m_w_in, m_conv_a_w, m_gdn_conv_w, m_gdn_A_log, m_gdn_dt_bias,
                              m_gdn_norm_g, m_w_a_out, m_w_b_out, m_w_o, m_norm_ffn_g, m_w_up, m_ffn_conv_w,
                              m_w_down, m_norm_final_g)))
    var = dict(zip(_WEIGHTS, (v_norm_mix_g, v_w_in, v_conv_a_w, v_gdn_conv_w, v_gdn_A_log, v_gdn_dt_bias,
                              v_gdn_norm_g, v_w_a_out, v_w_b_out, v_w_o, v_norm_ffn_g, v_w_up, v_ffn_conv_w,
                              v_w_down, v_norm_final_g)))
    cx, cy, cc = lax.axis_index("x"), lax.axis_index("y"), lax.axis_index("c")
    c_me = jnp.reshape(cc, (1,)).astype(jnp.int32)
    chip = jnp.reshape(2 * cx + cy, (1,)).astype(jnp.int32)
    me = jnp.reshape(4 * cx + 2 * cy + cc, (1,)).astype(jnp.int32)

    comm = _StepExchanges(wts, mom, var, c_me, chip)
    replicated = {n: wts[n] for n in ("norm_mix_g", "norm_ffn_g", "norm_final_g", "gdn_norm_g", "gdn_A_log", "gdn_dt_bias")}
    loss_p, dx, grads = _local_step(x[0], loss_target[0], replicated, comm)
    res = comm.results

    small = _pack_small(grads["norm_mix_g"], grads["norm_ffn_g"], grads["norm_final_g"], grads["gdn_norm_g"],
                        grads["gdn_A_log"], grads["gdn_dt_bias"], loss_p, grads["conv_a_w"], grads["gdn_conv_w"],
                        grads["ffn_conv_w"])
    (small_all,) = _run_exchange(_gather_exchange([small]), name="ag_small")

    def raw(t):
        return {n: t[n].reshape(1, D) if n == "norm_final_g" else t[n] for n in _SMALL}

    res_small, loss = _adam_small(small_all, me, raw(wts), raw(mom), raw(var))
    for n in _SMALL:
        res[n] = tuple(a.reshape(wts[n].shape) for a in res_small[n])
    outs = [[res[n][i] for n in _WEIGHTS] for i in range(4)]
    return (loss.reshape(()), dx[None], *outs[0], *outs[1], *outs[2], *outs[3])
```

```python
import jax
import jax.numpy as jnp
from jax import lax
from jax.experimental import pallas as pl
from jax.experimental.pallas import tpu as pltpu

f32 = jnp.float32
bf16 = jnp.bfloat16

D = 1024
H = 8
DH = 128
CH = 64
GDN_STEP = 2
DFF = 2816
NW1 = 9216
EPS = 1e-6
N_DEV = 8

ADAM_LR = 0.001
ADAM_B1 = 0.9
ADAM_B2 = 0.999
ADAM_EPS = 1e-08
ADAM_WD = 0.01
ADAM_STEP = 10

VMEM_LIMIT_BYTES = 48 * 1024 * 1024

R_IN, R_UP = 1154, 704

_HI = lax.Precision.HIGHEST
MESH = pl.DeviceIdType.MESH


def _params(n_grid):
    return pltpu.CompilerParams(dimension_semantics=("arbitrary",) * n_grid, vmem_limit_bytes=VMEM_LIMIT_BYTES)


def _bdot(a, b):
    return jnp.dot(a.astype(bf16), b.astype(bf16), preferred_element_type=f32)


def _bdot_nt(a, b):
    return lax.dot_general(a.astype(bf16), b.astype(bf16), (((1,), (1,)), ((), ())), preferred_element_type=f32)


def _bdot_tn(a, b):
    return lax.dot_general(a.astype(bf16), b.astype(bf16), (((0,), (0,)), ((), ())), preferred_element_type=f32)


def _hdot(a, b):
    return jnp.dot(a, b, preferred_element_type=f32, precision=_HI)


def _idot(a, b):
    return jnp.dot(a, b, preferred_element_type=f32, precision=lax.Precision.HIGH)


def _sigmoid(x):
    return 1.0 / (1.0 + jnp.exp(-x))


def _softplus(x):
    return jnp.maximum(x, 0.0) + jnp.log(1.0 + jnp.exp(-jnp.abs(x)))


def _shift_down(x, halo, j):
    if j == 0:
        return x
    xr = pltpu.roll(x, j, 0)
    hr = pltpu.roll(halo, j, 0)
    r8 = lax.broadcasted_iota(jnp.int32, hr.shape, 0)
    top = jnp.where(r8 < j, hr, xr[:8])
    return jnp.concatenate([top, xr[8:]], axis=0)


def _shift_up(x, halo, j):
    if j == 0:
        return x
    n = x.shape[0]
    xr = pltpu.roll(x, n - j, 0)
    hr = pltpu.roll(halo, 8 - j, 0)
    r8 = lax.broadcasted_iota(jnp.int32, hr.shape, 0)
    bot = jnp.where(r8 >= 8 - j, hr, xr[n - 8:])
    return jnp.concatenate([xr[:n - 8], bot], axis=0)


def _taps_down(x, halo, k):
    return [_shift_down(x, halo, k - 1 - j) for j in range(k)]


def _strip(i, base=0):
    return slice(base + i * 128, base + (i + 1) * 128)


def _strip_taps(x, halo, first, k):
    return _taps_down(x, jnp.where(first, 0.0, halo), k)


def _strip_conv(w_ref, sl, taps):
    out = w_ref[0:1, sl] * taps[0]
    for j in range(1, len(taps)):
        out = out + w_ref[j:j + 1, sl] * taps[j]
    return out


def _strip_weight_grad(dw_ref, sl, dy, taps):
    for j, tap in enumerate(taps):
        dw_ref[j:j + 1, sl] += jnp.sum(dy * tap, axis=0, keepdims=True)


def _strip_conv_up(dy, halo, last, w_ref, sl, k):
    halo = jnp.where(last, 0.0, halo)
    out = w_ref[k - 1:k, sl] * dy
    for j in range(k - 1):
        out = out + w_ref[j:j + 1, sl] * _shift_up(dy, halo, k - 1 - j)
    return out


def _row(tb, w, col=0):
    return pl.BlockSpec((tb, w), lambda i: (i, col))


def _prev(tb, w, col=0, rows=8):
    return pl.BlockSpec((rows, w), lambda i: (jnp.maximum(i * (tb // rows) - 1, 0), col))


def _next(tb, w, n_rows, col=0, rows=8):
    last = n_rows // rows - 1
    return pl.BlockSpec((rows, w), lambda i: (jnp.minimum((i + 1) * (tb // rows), last), col))


def _f32(ref, sl):
    return ref[:, sl].astype(f32)


def _halo_before(ref, sl):
    h = _f32(ref, sl)
    return h[h.shape[0] - 8:]


def _halo_after(ref, sl):
    return _f32(ref, sl)[:8]


def _fixed(shape):
    return pl.BlockSpec(shape, lambda i: (0,) * len(shape))


def _pick(n, prefs):
    for p in prefs:
        if n % p == 0:
            return p
    return n


def _matmul(a, b, *, name, nt=False, add=None, tm=1024, tn=1024, tk=None, out_dtype=f32, cols=None, exchange=None):
    m, kd = a.shape
    col0, n = cols if cols is not None else (0, b.shape[0] if nt else b.shape[1])
    tm = _pick(m, (tm, 512, 256))
    tn = _pick(n, (tn, 1024, 512, 128))
    tk = kd if tk is None else tk
    nk = kd // tk
    assert nk == 1 or out_dtype == f32
    assert col0 % tn == 0 and not (nt and cols)
    j0 = col0 // tn
    dims = (((1,), (1,)), ((), ())) if nt else (((1,), (0,)), ((), ()))

    def body(a_ref, b_ref, *rest):
        o_ref = rest[-1]
        part = lax.dot_general(a_ref[...], b_ref[...], dims, preferred_element_type=f32)
        if nk == 1:
            o_ref[...] = (part if add is None else part + rest[0][...]).astype(out_dtype)
            return
        k = pl.program_id(2)

        @pl.when(k == 0)
        def _():
            o_ref[...] = part if add is None else part + rest[0][...]

        @pl.when(k > 0)
        def _():
            o_ref[...] += part

    b_spec = pl.BlockSpec((tn, tk), lambda i, j, k: (j, k)) if nt else pl.BlockSpec((tk, tn), lambda i, j, k: (k, j + j0))
    in_specs = [pl.BlockSpec((tm, tk), lambda i, j, k: (i, k)), b_spec]
    args = [a, b]
    if add is not None:
        in_specs.append(pl.BlockSpec((tm, tn), lambda i, j, k: (i, j)))
        args.append(add)
    return _call_with_exchange(
        body, exchange, name=name, grid=(m // tm, n // tn, nk), in_specs=in_specs,
        out_specs=pl.BlockSpec((tm, tn), lambda i, j, k: (i, j)),
        out_shape=jax.ShapeDtypeStruct((m, n), out_dtype), args=args)


def _call_with_exchange(body, exchange, *, name, grid, in_specs, out_specs, out_shape, args):
    if exchange is None:
        return pl.pallas_call(body, name=name, grid=grid, in_specs=in_specs, out_specs=out_specs, out_shape=out_shape,
                              compiler_params=_params(len(grid)))(*args)
    x_arrays, x_shapes, x_sems, start, wait = exchange
    n_in, n_xin, n_xout = len(args), len(x_arrays), len(x_shapes)

    def full_body(*refs):
        c_in, x_in = refs[:n_in], refs[n_in:n_in + n_xin]
        c_out = refs[n_in + n_xin]
        x_out = refs[n_in + n_xin + 1:n_in + n_xin + 1 + n_xout]
        sems = refs[n_in + n_xin + 1 + n_xout:]
        ids = [pl.program_id(d) for d in range(len(grid))]
        first, last = ids[0] == 0, ids[0] == grid[0] - 1
        for d in range(1, len(grid)):
            first = first & (ids[d] == 0)
            last = last & (ids[d] == grid[d] - 1)

        @pl.when(first)
        def _():
            start(x_in, x_out, sems)

        body(*c_in, c_out)

        @pl.when(last)
        def _():
            wait(x_in, x_out, sems)

    res = pl.pallas_call(
        full_body, name=name, grid=grid, in_specs=list(in_specs) + [_ANY] * n_xin,
        out_specs=[out_specs] + [_ANY] * n_xout, out_shape=[out_shape] + list(x_shapes),
        scratch_shapes=list(x_sems), compiler_params=_params(len(grid)),
    )(*args, *x_arrays)
    return res[0], list(res[1:])


def _matmul_tn(a, b, *, name, tm=1024, tn=1024, exchange=None):
    t, m = a.shape
    _, n = b.shape
    tm = _pick(m, (tm, 1024, 512, 128))
    tn = _pick(n, (tn, 1024, 512, 128))
    tt = _pick(t, (2048, 1024, 512, 256))
    nt = t // tt

    def body(a_ref, b_ref, o_ref):
        k = pl.program_id(2)
        part = lax.dot_general(a_ref[...], b_ref[...], (((0,), (0,)), ((), ())), preferred_element_type=f32)

        @pl.when(k == 0)
        def _():
            o_ref[...] = part

        @pl.when(k > 0)
        def _():
            o_ref[...] += part

    return _call_with_exchange(
        body, exchange, name=name, grid=(m // tm, n // tn, nt),
        in_specs=[pl.BlockSpec((tt, tm), lambda i, j, k: (k, i)), pl.BlockSpec((tt, tn), lambda i, j, k: (k, j))],
        out_specs=pl.BlockSpec((tm, tn), lambda i, j, k: (i, j)),
        out_shape=jax.ShapeDtypeStruct((m, n), f32), args=[a, b])


def _rms_fwd(x, g, *, name, exchange=None):
    t = x.shape[0]
    tb = _pick(t, (256, 128))

    def body(x_ref, g_ref, h_ref):
        xv = x_ref[...]
        r = lax.rsqrt(jnp.mean(xv * xv, axis=-1, keepdims=True) + EPS)
        h_ref[...] = (xv * r * g_ref[...]).astype(bf16)

    return _call_with_exchange(
        body, exchange, name=name, grid=(t // tb,), in_specs=[_row(tb, D), _fixed((1, D))], out_specs=_row(tb, D),
        out_shape=jax.ShapeDtypeStruct((t, D), bf16), args=[x, g])


def _rms_bwd(dh, x, g, dres, *, name):
    t = x.shape[0]
    tb = _pick(t, (256, 128))

    def body(dh_ref, x_ref, g_ref, dres_ref, dx_ref, dxb_ref, dg_ref):
        xv = x_ref[...]
        r = lax.rsqrt(jnp.mean(xv * xv, axis=-1, keepdims=True) + EPS)
        xh = xv * r
        dy = dh_ref[...]
        dyg = dy * g_ref[...]
        dx = dres_ref[...] + r * (dyg - xh * jnp.mean(dyg * xh, axis=-1, keepdims=True))
        dx_ref[...] = dx
        dxb_ref[...] = dx.astype(bf16)

        @pl.when(pl.program_id(0) == 0)
        def _():
            dg_ref[...] = jnp.zeros_like(dg_ref)

        dg_ref[...] += jnp.sum((dy * xh).reshape(tb // 8, 8, D), axis=0)

    return pl.pallas_call(
        body, name=name, grid=(t // tb,),
        in_specs=[_row(tb, D), _row(tb, D), _fixed((1, D)), _row(tb, D)],
        out_specs=[_row(tb, D), _row(tb, D), _fixed((8, D))],
        out_shape=[jax.ShapeDtypeStruct((t, D), f32), jax.ShapeDtypeStruct((t, D), bf16),
                   jax.ShapeDtypeStruct((8, D), f32)],
        compiler_params=_params(1),
    )(dh, x, g, dres)


def _gdn_gates(ab, alog, dtb):
    lane = lax.broadcasted_iota(jnp.int32, ab.shape, 1)
    g = -jnp.exp(alog) * _softplus(ab + dtb)
    beta = _sigmoid(ab)
    return jnp.where(lane < H, g, jnp.where(lane < 2 * H, beta, 0.0))


def _pre_fwd(pg, pq, p2, wa, wg, alog, dtb):
    t = pg.shape[0]
    tb = 128

    def body(p0_ref, p0h_ref, pq_ref, pqh_ref, p2_ref, wa_ref, wg_ref, alog_ref, dtb_ref,
             ya_ref, qn_ref, kn_ref, vc_ref, gb_ref):
        first = pl.program_id(0) == 0
        for i in range(D // 128):
            sl, cg, xv = _strip(i), _strip(i, D), _strip(i, 2 * D)
            taps = _strip_taps(_f32(p0_ref, cg) * _f32(p0_ref, xv), _halo_before(p0h_ref, cg) * _halo_before(p0h_ref, xv),
                               first, 3)
            ya_ref[:, sl] = (_f32(p0_ref, sl) * _strip_conv(wa_ref, sl, taps)).astype(bf16)
        for part, out_ref, scale in ((0, qn_ref, DH ** -0.5), (1, kn_ref, 1.0), (2, vc_ref, None)):
            for h in range(H):
                sl = _strip(h, part * D)
                s = _strip_conv(wg_ref, sl, _strip_taps(pq_ref[:, sl], pqh_ref[:, sl], first, 4))
                s = s * _sigmoid(s)
                if scale is not None:
                    s = s * (lax.rsqrt(jnp.sum(s * s, axis=-1, keepdims=True) + EPS) * scale)
                out_ref[:, _strip(h)] = s
        gb_ref[...] = _gdn_gates(p2_ref[...], alog_ref[...], dtb_ref[...])

    return pl.pallas_call(
        body, name="pre_fwd", grid=(t // tb,),
        in_specs=[_row(tb, 3 * D, 0), _prev(tb, 3 * D, 0, rows=16), _row(tb, 3 * D), _prev(tb, 3 * D), _row(tb, 128),
                  _fixed((8, D)), _fixed((8, 3 * D)), _fixed((1, 128)), _fixed((1, 128))],
        out_specs=[_row(tb, D), _row(tb, D), _row(tb, D), _row(tb, D), _row(tb, 128)],
        out_shape=[jax.ShapeDtypeStruct((t, D), bf16), jax.ShapeDtypeStruct((t, D), f32),
                   jax.ShapeDtypeStruct((t, D), f32), jax.ShapeDtypeStruct((t, D), f32),
                   jax.ShapeDtypeStruct((t, 128), f32)],
        compiler_params=_params(1),
    )(pg, pg, pq, pq, p2, wa, wg, alog, dtb)


_Z_COL, _GA_COL, _GB_COL = 3, 4, 5


def _post_fwd(o, pg, gn):
    t = o.shape[0]
    tb = _pick(t, (256, 128))

    def body(o_ref, z_ref, gn_ref, yb_ref):
        for h in range(H):
            sl = slice(h * DH, (h + 1) * DH)
            oh = o_ref[:, sl]
            z = _f32(z_ref, sl)
            r = lax.rsqrt(jnp.mean(oh * oh, axis=-1, keepdims=True) + EPS)
            yb_ref[:, sl] = (oh * r * gn_ref[...] * (z * _sigmoid(z))).astype(bf16)

    return pl.pallas_call(
        body, name="post_fwd", grid=(t // tb,), in_specs=[_row(tb, D), _row(tb, D, _Z_COL), _fixed((1, DH))],
        out_specs=_row(tb, D), out_shape=jax.ShapeDtypeStruct((t, D), bf16), compiler_params=_params(1),
    )(o, pg, gn)


def _post_bwd(dyb, o, pg, gn):
    t = o.shape[0]
    tb = _pick(t, (256, 128))

    def body(dyb_ref, o_ref, z_ref, gn_ref, do_ref, dz_ref, dgn_ref):
        @pl.when(pl.program_id(0) == 0)
        def _():
            dgn_ref[...] = jnp.zeros_like(dgn_ref)

        gn_v = gn_ref[...]
        acc = jnp.zeros((8, DH), f32)
        for h in range(H):
            sl = slice(h * DH, (h + 1) * DH)
            oh = o_ref[:, sl]
            z = _f32(z_ref, sl)
            dy = dyb_ref[:, sl]
            r = lax.rsqrt(jnp.mean(oh * oh, axis=-1, keepdims=True) + EPS)
            on = oh * r
            sg = _sigmoid(z)
            sz = z * sg
            don = dy * sz
            dz_ref[:, sl] = (dy * on * gn_v * (sg * (1.0 + z * (1.0 - sg)))).astype(bf16)
            acc = acc + jnp.sum((don * on).reshape(tb // 8, 8, DH), axis=0)
            doh = don * gn_v
            do_ref[:, sl] = r * (doh - on * jnp.mean(doh * on, axis=-1, keepdims=True))
        dgn_ref[...] += acc

    return pl.pallas_call(
        body, name="post_bwd", grid=(t // tb,),
        in_specs=[_row(tb, D), _row(tb, D), _row(tb, D, _Z_COL), _fixed((1, DH))],
        out_specs=[_row(tb, D), _row(tb, D), _fixed((8, DH))],
        out_shape=[jax.ShapeDtypeStruct((t, D), f32), jax.ShapeDtypeStruct((t, D), bf16),
                   jax.ShapeDtypeStruct((8, DH), f32)],
        compiler_params=_params(1),
    )(dyb, o, pg, gn)


def _mix_fwd(ya, yb, pg):
    t = ya.shape[0]
    tb = _pick(t, (256, 128))

    def body(ya_ref, yb_ref, ga_ref, gb_ref, mix_ref):
        ya_v, yb_v = ya_ref[...].astype(f32), yb_ref[...].astype(f32)
        mix = _sigmoid(ga_ref[...].astype(f32)) * ya_v + _sigmoid(gb_ref[...].astype(f32)) * yb_v
        mix_ref[...] = mix.astype(bf16)

    return pl.pallas_call(
        body, name="mix_fwd", grid=(t // tb,),
        in_specs=[_row(tb, D), _row(tb, D), _row(tb, D, _GA_COL), _row(tb, D, _GB_COL)],
        out_specs=_row(tb, D), out_shape=jax.ShapeDtypeStruct((t, D), bf16), compiler_params=_params(1),
    )(ya, yb, pg, pg)


def _mix_bwd(dmix, ya, yb, pg):
    t = ya.shape[0]
    tb = _pick(t, (256, 128))

    def body(dm_ref, ya_ref, yb_ref, ga_ref, gb_ref, dya_ref, dyb_ref, dg_ref):
        dm = dm_ref[...].astype(f32)
        sa = _sigmoid(ga_ref[...].astype(f32))
        sb = _sigmoid(gb_ref[...].astype(f32))
        dya_ref[...] = (dm * sa).astype(bf16)
        dyb_ref[...] = (dm * sb).astype(bf16)
        dg_ref[:, :D] = (dm * ya_ref[...].astype(f32) * sa * (1.0 - sa)).astype(bf16)
        dg_ref[:, D:] = (dm * yb_ref[...].astype(f32) * sb * (1.0 - sb)).astype(bf16)

    return pl.pallas_call(
        body, name="mix_bwd", grid=(t // tb,),
        in_specs=[_row(tb, D), _row(tb, D), _row(tb, D), _row(tb, D, _GA_COL), _row(tb, D, _GB_COL)],
        out_specs=[_row(tb, D), _row(tb, D), _row(tb, 2 * D)],
        out_shape=[jax.ShapeDtypeStruct((t, D), bf16), jax.ShapeDtypeStruct((t, D), bf16),
                   jax.ShapeDtypeStruct((t, 2 * D), bf16)],
        compiler_params=_params(1),
    )(dmix, ya, yb, pg, pg)


def _ffn_fwd(up, wf):
    t = up.shape[0]
    tb = 128

    def body(up_ref, uph_ref, wf_ref, act_ref):
        first = pl.program_id(0) == 0
        for i in range(DFF // 128):
            g, v = _strip(i), _strip(i, DFF)
            gate = _strip_conv(wf_ref, g, _strip_taps(_f32(up_ref, g), _halo_before(uph_ref, g), first, 3))
            val = _strip_conv(wf_ref, v, _strip_taps(_f32(up_ref, v), _halo_before(uph_ref, v), first, 3))
            act_ref[:, g] = (gate * _sigmoid(gate) * val).astype(bf16)

    return pl.pallas_call(
        body, name="ffn_fwd", grid=(t // tb,),
        in_specs=[_row(tb, 2 * DFF), _prev(tb, 2 * DFF, rows=16), _fixed((8, 2 * DFF))],
        out_specs=_row(tb, DFF), out_shape=jax.ShapeDtypeStruct((t, DFF), bf16), compiler_params=_params(1),
    )(up, up, wf)


def _ffn_bwd1(dact, up, wf):
    t = up.shape[0]
    tb = 128

    def body(da_ref, up_ref, uph_ref, wf_ref, dc_ref, dw_ref):
        @pl.when(pl.program_id(0) == 0)
        def _():
            dw_ref[...] = jnp.zeros_like(dw_ref)

        first = pl.program_id(0) == 0
        for i in range(DFF // 128):
            g, v = _strip(i), _strip(i, DFF)
            g_taps = _strip_taps(_f32(up_ref, g), _halo_before(uph_ref, g), first, 3)
            v_taps = _strip_taps(_f32(up_ref, v), _halo_before(uph_ref, v), first, 3)
            gate = _strip_conv(wf_ref, g, g_taps)
            val = _strip_conv(wf_ref, v, v_taps)
            sg = _sigmoid(gate)
            da = _f32(da_ref, g)
            dgate = da * val * (sg * (1.0 + gate * (1.0 - sg)))
            dval = da * (gate * sg)
            dc_ref[:, g] = dgate.astype(bf16)
            dc_ref[:, v] = dval.astype(bf16)
            _strip_weight_grad(dw_ref, g, dgate, g_taps)
            _strip_weight_grad(dw_ref, v, dval, v_taps)

    return pl.pallas_call(
        body, name="ffn_bwd1", grid=(t // tb,),
        in_specs=[_row(tb, DFF), _row(tb, 2 * DFF), _prev(tb, 2 * DFF, rows=16), _fixed((8, 2 * DFF))],
        out_specs=[_row(tb, 2 * DFF), _fixed((8, 2 * DFF))],
        out_shape=[jax.ShapeDtypeStruct((t, 2 * DFF), bf16), jax.ShapeDtypeStruct((8, 2 * DFF), f32)],
        compiler_params=_params(1),
    )(dact, up, up, wf)


def _ffn_bwd2(dc, wf):
    t = dc.shape[0]
    tb = 128
    nb = t // tb

    def body(dc_ref, dch_ref, wf_ref, dup_ref):
        last = pl.program_id(0) == nb - 1
        for i in range(2 * DFF // 128):
            sl = _strip(i)
            dup_ref[:, sl] = _strip_conv_up(_f32(dc_ref, sl), _halo_after(dch_ref, sl), last, wf_ref, sl, 3).astype(bf16)

    return pl.pallas_call(
        body, name="ffn_bwd2", grid=(nb,),
        in_specs=[_row(tb, 2 * DFF), _next(tb, 2 * DFF, t, rows=16), _fixed((8, 2 * DFF))],
        out_specs=_row(tb, 2 * DFF), out_shape=jax.ShapeDtypeStruct((t, 2 * DFF), bf16), compiler_params=_params(1),
    )(dc, dc, wf)


def _final(x3, tgt, g):
    t = x3.shape[0]
    tb = _pick(t, (256, 128))

    def body(x_ref, t_ref, g_ref, loss_ref, dx_ref, dxb_ref, dg_ref):
        @pl.when(pl.program_id(0) == 0)
        def _():
            loss_ref[...] = jnp.zeros_like(loss_ref)
            dg_ref[...] = jnp.zeros_like(dg_ref)

        xv = x_ref[...]
        r = lax.rsqrt(jnp.mean(xv * xv, axis=-1, keepdims=True) + EPS)
        xh = xv * r
        gv = g_ref[...]
        e = xh * gv - t_ref[...]
        lrow = 0.5 * jnp.mean(e * e, axis=-1, keepdims=True)
        loss_ref[...] += jnp.sum(jnp.broadcast_to(lrow, (tb, 128)).reshape(tb // 8, 8, 128), axis=0)
        dy = e * (1.0 / D)
        dyg = dy * gv
        dx = r * (dyg - xh * jnp.mean(dyg * xh, axis=-1, keepdims=True))
        dx_ref[...] = dx
        dxb_ref[...] = dx.astype(bf16)
        dg_ref[...] += jnp.sum((dy * xh).reshape(tb // 8, 8, D), axis=0)

    return pl.pallas_call(
        body, name="final", grid=(t // tb,), in_specs=[_row(tb, D), _row(tb, D), _fixed((1, D))],
        out_specs=[_fixed((8, 128)), _row(tb, D), _row(tb, D), _fixed((8, D))],
        out_shape=[jax.ShapeDtypeStruct((8, 128), f32), jax.ShapeDtypeStruct((t, D), f32),
                   jax.ShapeDtypeStruct((t, D), bf16), jax.ShapeDtypeStruct((8, D), f32)],
        compiler_params=_params(1),
    )(x3, tgt, g)


def _pre_bwd1(pg, pq, p2, dya_in, dqn, dkn, dvc, dgb, gbeta, wa, wg, alog, dtb):
    t = pg.shape[0]
    tb = 128

    def body(p0_ref, p0h_ref, pq_ref, pqh_ref, p2_ref, dya_ref, dqn_ref, dkn_ref, dvc_ref, dgb_ref, gb_ref,
             wa_ref, wg_ref, alog_ref, dtb_ref,
             dbg_ref, dca_ref, dc4_ref, dp2_ref, dwa_ref, dwg_ref, dal_ref, ddt_ref):
        @pl.when(pl.program_id(0) == 0)
        def _():
            dwa_ref[...] = jnp.zeros_like(dwa_ref)
            dwg_ref[...] = jnp.zeros_like(dwg_ref)
            dal_ref[...] = jnp.zeros_like(dal_ref)
            ddt_ref[...] = jnp.zeros_like(ddt_ref)

        first = pl.program_id(0) == 0

        for i in range(D // 128):
            sl, cg, xv = _strip(i), _strip(i, D), _strip(i, 2 * D)
            taps = _strip_taps(_f32(p0_ref, cg) * _f32(p0_ref, xv), _halo_before(p0h_ref, cg) * _halo_before(p0h_ref, xv),
                               first, 3)
            dya = _f32(dya_ref, sl)
            dbg_ref[:, sl] = (dya * _strip_conv(wa_ref, sl, taps)).astype(bf16)
            dca = dya * _f32(p0_ref, sl)
            dca_ref[:, sl] = dca.astype(bf16)
            _strip_weight_grad(dwa_ref, sl, dca, taps)

        for part, d_ref, scale in ((0, dqn_ref, DH ** -0.5), (1, dkn_ref, 1.0), (2, dvc_ref, None)):
            for h in range(H):
                sl = _strip(h, part * D)
                taps = _strip_taps(pq_ref[:, sl], pqh_ref[:, sl], first, 4)
                c4 = _strip_conv(wg_ref, sl, taps)
                sg = _sigmoid(c4)
                dn = d_ref[:, _strip(h)]
                if scale is not None:
                    a = c4 * sg
                    r = lax.rsqrt(jnp.sum(a * a, axis=-1, keepdims=True) + EPS)
                    an = a * r
                    dn = dn * scale
                    dn = r * (dn - an * jnp.sum(dn * an, axis=-1, keepdims=True))
                dc4 = dn * (sg * (1.0 + c4 * (1.0 - sg)))
                dc4_ref[:, sl] = dc4.astype(bf16)
                _strip_weight_grad(dwg_ref, sl, dc4, taps)

        ab = p2_ref[...]
        lane = lax.broadcasted_iota(jnp.int32, ab.shape, 1)
        dgbv = dgb_ref[...]
        gbv = gb_ref[...]
        da = dgbv * (-jnp.exp(alog_ref[...])) * _sigmoid(ab + dtb_ref[...])
        db = dgbv * gbv * (1.0 - gbv)
        dp2_ref[...] = jnp.where(lane < H, da, jnp.where(lane < 2 * H, db, 0.0)).astype(bf16)
        dal = jnp.where(lane < H, dgbv * gbv, 0.0)
        ddt = jnp.where(lane < H, da, 0.0)
        dal_ref[...] += jnp.sum(dal.reshape(tb // 8, 8, 128), axis=0)
        ddt_ref[...] += jnp.sum(ddt.reshape(tb // 8, 8, 128), axis=0)

    return pl.pallas_call(
        body, name="pre_bwd1", grid=(t // tb,),
        in_specs=[_row(tb, 3 * D, 0), _prev(tb, 3 * D, 0, rows=16), _row(tb, 3 * D), _prev(tb, 3 * D), _row(tb, 128),
                  _row(tb, D), _row(tb, D), _row(tb, D), _row(tb, D), _row(tb, 128), _row(tb, 128),
                  _fixed((8, D)), _fixed((8, 3 * D)), _fixed((1, 128)), _fixed((1, 128))],
        out_specs=[_row(tb, D), _row(tb, D), _row(tb, 3 * D), _row(tb, 128),
                   _fixed((8, D)), _fixed((8, 3 * D)), _fixed((8, 128)), _fixed((8, 128))],
        out_shape=[jax.ShapeDtypeStruct((t, D), bf16), jax.ShapeDtypeStruct((t, D), bf16),
                   jax.ShapeDtypeStruct((t, 3 * D), bf16), jax.ShapeDtypeStruct((t, 128), bf16),
                   jax.ShapeDtypeStruct((8, D), f32), jax.ShapeDtypeStruct((8, 3 * D), f32),
                   jax.ShapeDtypeStruct((8, 128), f32), jax.ShapeDtypeStruct((8, 128), f32)],
        compiler_params=_params(1),
    )(pg, pg, pq, pq, p2, dya_in, dqn, dkn, dvc, dgb, gbeta, wa, wg, alog, dtb)


def _pre_bwd2(dca, dc4, pg, dbg, dz, dgates, wa, wg, exchange=None):
    t = pg.shape[0]
    tb = 128
    nb = t // tb

    def body(dca_ref, dcah_ref, dc4_ref, dc4h_ref, p0_ref, dbg_ref, dz_ref, dgt_ref, wa_ref, wg_ref, dp_ref):
        last = pl.program_id(0) == nb - 1
        dp_ref[:, :D] = dbg_ref[...]
        for i in range(D // 128):
            sl, cg, xv = _strip(i), _strip(i, D), _strip(i, 2 * D)
            du = _strip_conv_up(_f32(dca_ref, sl), _halo_after(dcah_ref, sl), last, wa_ref, sl, 3)
            dp_ref[:, cg] = (du * _f32(p0_ref, xv)).astype(bf16)
            dp_ref[:, xv] = (du * _f32(p0_ref, cg)).astype(bf16)
        dp_ref[:, 3 * D:4 * D] = dz_ref[...]
        dp_ref[:, 4 * D:6 * D] = dgt_ref[...]
        for i in range(3 * D // 128):
            sl = _strip(i)
            dq = _strip_conv_up(_f32(dc4_ref, sl), _halo_after(dc4h_ref, sl), last, wg_ref, sl, 4)
            dp_ref[:, _strip(i, 6 * D)] = dq.astype(bf16)

    return _call_with_exchange(
        body, exchange, name="pre_bwd2", grid=(nb,),
        in_specs=[_row(tb, D), _next(tb, D, t, rows=16), _row(tb, 3 * D), _next(tb, 3 * D, t, rows=16), _row(tb, 3 * D, 0),
                  _row(tb, D), _row(tb, D), _row(tb, 2 * D), _fixed((8, D)), _fixed((8, 3 * D))],
        out_specs=_row(tb, NW1), out_shape=jax.ShapeDtypeStruct((t, NW1), bf16),
        args=[dca, dca, dc4, dc4, pg, dbg, dz, dgates, wa, wg])


def _chunk_consts():
    r = lax.broadcasted_iota(jnp.int32, (CH, CH), 0)
    c = lax.broadcasted_iota(jnp.int32, (CH, CH), 1)
    return r, c, (r == c).astype(f32)


def _tri_inverse(lows, eye, r, c):
    def same_block(b):
        return jnp.bitwise_xor(r, c) < b

    xs = [jnp.where(same_block(8), -low, 0.0) for low in lows]
    ts = [eye + x for x in xs]
    for _ in range(2):
        xs = [_idot(x, x) for x in xs]
        ts = [t + _idot(t, x) for t, x in zip(ts, xs)]
    for b in (8, 16, 32):
        below = same_block(2 * b) & jnp.logical_not(same_block(b))
        ts = [t - _idot(_idot(t, jnp.where(below, low, 0.0)), t) for t, low in zip(ts, lows)]
    return ts


def _chunk_common(q, k, v, gcol, bcol, r, c, eye):
    grow = jnp.sum(eye * gcol, axis=0, keepdims=True)
    dec = jnp.exp(jnp.where(r >= c, gcol - grow, -jnp.inf))
    rcol = lax.broadcasted_iota(jnp.int32, (CH, 1), 0)
    glast = jnp.sum(jnp.where(rcol == CH - 1, gcol, 0.0), axis=0, keepdims=True)
    eg = jnp.exp(gcol)
    el = jnp.exp(glast - gcol)
    kb = k * bcol
    vb = v * bcol
    kk = _bdot_nt(kb, k)
    low = jnp.where(r > c, kk * dec, 0.0)
    qk = _bdot_nt(q, k)
    att = qk * dec
    return grow, dec, glast, eg, el, kb, vb, kk, low, qk, att, rcol


def _gdn_fwd(qn, kn, vc, gbeta):
    t = qn.shape[0]
    n_chunks = t // CH

    def body(q_ref, k_ref, v_ref, gb_ref, o_ref, s_ref, t_ref, state):
        @pl.when(pl.program_id(0) == 0)
        def _():
            state[...] = jnp.zeros_like(state)

        r, c, eye = _chunk_consts()
        tri = (r >= c).astype(f32)
        heads = range(H)
        keys = [(s, h) for s in range(GDN_STEP) for h in heads]
        rows = [slice(s * CH, (s + 1) * CH) for s in range(GDN_STEP)]
        gbs = [gb_ref[rows[s], :] for s in range(GDN_STEP)]
        galls = [_hdot(tri, gb) for gb in gbs]
        qs = {(s, h): q_ref[rows[s], h * DH:(h + 1) * DH] for s, h in keys}
        ks = {(s, h): k_ref[rows[s], h * DH:(h + 1) * DH] for s, h in keys}
        cm = {(s, h): _chunk_common(qs[s, h], ks[s, h], v_ref[rows[s], h * DH:(h + 1) * DH], galls[s][:, h:h + 1],
                                    gbs[s][:, H + h:H + h + 1], r, c, eye) for s, h in keys}
        invs = dict(zip(keys, _tri_inverse([cm[key][8] for key in keys], eye, r, c)))
        uws = {key: _bdot(invs[key], jnp.concatenate([cm[key][6], cm[key][5] * cm[key][3]], axis=1)) for key in keys}
        sts = [state[h] for h in heads]
        for s in range(GDN_STEP):
            vns = [uws[s, h][:, :DH] - _bdot(uws[s, h][:, DH:], sts[h]) for h in heads]
            outs = [_bdot(qs[s, h] * cm[s, h][3], sts[h]) + _bdot(cm[s, h][10], vns[h]) for h in heads]
            news = [sts[h] * jnp.exp(cm[s, h][2]) + _bdot_tn(ks[s, h] * cm[s, h][4], vns[h]) for h in heads]
            for h in heads:
                s_ref[s, h] = sts[h].astype(bf16)
                t_ref[s, h] = invs[s, h]
                o_ref[rows[s], h * DH:(h + 1) * DH] = outs[h]
            sts = news
        for h in heads:
            state[h] = sts[h]

    tb = GDN_STEP * CH
    return pl.pallas_call(
        body, name="gdn_fwd", grid=(t // tb,),
        in_specs=[_row(tb, D), _row(tb, D), _row(tb, D), _row(tb, 128)],
        out_specs=[_row(tb, D), pl.BlockSpec((GDN_STEP, H, DH, DH), lambda i: (i, 0, 0, 0)),
                   pl.BlockSpec((GDN_STEP, H, CH, CH), lambda i: (i, 0, 0, 0))],
        out_shape=[jax.ShapeDtypeStruct((t, D), f32), jax.ShapeDtypeStruct((n_chunks, H, DH, DH), bf16),
                   jax.ShapeDtypeStruct((n_chunks, H, CH, CH), f32)],
        scratch_shapes=[pltpu.VMEM((H, DH, DH), f32)],
        compiler_params=_params(1),
    )(qn, kn, vc, gbeta)


def _gdn_bwd(qn, kn, vc, gbeta, do, s_all, t_all):
    t = qn.shape[0]

    def body(q_ref, k_ref, v_ref, gb_ref, do_ref, s_ref, t_ref, dq_ref, dk_ref, dv_ref, dgb_ref, dstate):
        @pl.when(pl.program_id(0) == 0)
        def _():
            dstate[...] = jnp.zeros_like(dstate)

        r, c, eye = _chunk_consts()
        tril = r >= c
        lane = lax.broadcasted_iota(jnp.int32, (1, 128), 1)
        hs = range(H)

        def each(fn, *lists):
            return [fn(*args) for args in zip(*lists)]

        def rsum(a):
            return jnp.sum(a, axis=1, keepdims=True)

        def before_state(s):
            rows = slice(s * CH, (s + 1) * CH)
            gb = gb_ref[rows, :]
            gall = _hdot(tril.astype(f32), gb)
            p = {"rows": rows}
            p["q"] = q = [q_ref[rows, h * DH:(h + 1) * DH] for h in hs]
            p["k"] = k = [k_ref[rows, h * DH:(h + 1) * DH] for h in hs]
            p["v"] = v = [v_ref[rows, h * DH:(h + 1) * DH] for h in hs]
            p["dout"] = dout = [do_ref[rows, h * DH:(h + 1) * DH] for h in hs]
            p["inv"] = inv = [t_ref[s, h] for h in hs]
            p["st"] = st = [s_ref[s, h] for h in hs]
            p["bcol"] = bcol = [gb[:, H + h:H + h + 1] for h in hs]
            cm = [_chunk_common(q[h], k[h], v[h], gall[:, h:h + 1], bcol[h], r, c, eye) for h in hs]
            for name, i in (("dec", 1), ("glast", 2), ("eg", 3), ("el", 4), ("kb", 5), ("vb", 6), ("low", 8), ("att", 10)):
                p[name] = [m[i] for m in cm]
            p["rcol"] = cm[0][11]
            p["elast"] = each(jnp.exp, p["glast"])
            p["kbg"] = each(jnp.multiply, p["kb"], p["eg"])
            uw = each(lambda i, a, b: _bdot(i, jnp.concatenate([a, b], axis=1)), inv, p["vb"], p["kbg"])
            p["u"] = [a[:, :DH] for a in uw]
            p["w"] = [a[:, DH:] for a in uw]
            p["vn"] = each(lambda a, b, x: a - _bdot(b, x), p["u"], p["w"], st)
            p["qd"] = each(jnp.multiply, q, p["eg"])
            p["kd"] = each(jnp.multiply, k, p["el"])
            p["dqd"] = each(_bdot_nt, dout, st)
            p["datt"] = each(lambda d, x: jnp.where(tril, _bdot_nt(d, x), 0.0), dout, p["vn"])
            p["dqk"] = each(jnp.multiply, p["datt"], p["dec"])
            p["qd_do"] = each(_bdot_tn, p["qd"], dout)
            p["att_do"] = each(_bdot_tn, p["att"], dout)
            return p

        def after_state(p, ds):
            q, k, v, st, inv, bcol = p["q"], p["k"], p["v"], p["st"], p["inv"], p["bcol"]
            eg, el, kb, u, w = p["eg"], p["el"], p["kb"], p["u"], p["w"]
            dvn = each(lambda a, kk, x: a + _bdot(kk, x), p["att_do"], p["kd"], ds)
            dkd = each(_bdot_nt, p["vn"], ds)
            dw = each(lambda a, x: -_bdot_nt(a, x), dvn, st)
            new_ds = each(lambda x, e, a, ww, dv_: x * e + a - _bdot_tn(ww, dv_), ds, p["elast"], p["qd_do"], w, dvn)
            dglast = each(lambda e, x, d: e * jnp.sum(rsum(x.astype(f32) * d), axis=0, keepdims=True), p["elast"], st, ds)
            dr = each(lambda i, a, b: _bdot_tn(i, jnp.concatenate([a, b], axis=1)), inv, dvn, dw)
            dvb = [a[:, :DH] for a in dr]
            dkbg = [a[:, DH:] for a in dr]
            dlow = each(lambda a, b, x, y: -jnp.where(r > c, _bdot_nt(a, b) + _bdot_nt(x, y), 0.0), dvb, u, dkbg, w)
            dkk = each(jnp.multiply, dlow, p["dec"])
            mm = each(lambda a, b, x, y: a * b + x * y, dlow, p["low"], p["datt"], p["att"])
            dkb = each(lambda a, kk, b, e: _bdot(a, kk) + b * e, dkk, k, dkbg, eg)
            dk = each(lambda a, b, x, y, d, e, f, g: _bdot_tn(a, b) + _bdot_tn(x, y) + d * e + f * g,
                      dkk, kb, p["dqk"], q, dkd, el, dkb, bcol)
            dq = each(lambda a, kk, d, e: _bdot(a, kk) + d * e, p["dqk"], k, p["dqd"], eg)
            dv = each(jnp.multiply, dvb, bcol)
            dbeta = each(lambda a, b, x, y: rsum(a * b) + rsum(x * y), dkb, k, dvb, v)
            deg = each(lambda a, b, x, y: rsum(a * b) + rsum(x * y), dkbg, kb, p["dqd"], q)
            delc = each(lambda a, b, e: rsum(a * b) * e, dkd, k, el)
            dgc = each(lambda m, a, e, d: rsum(m) - rsum(eye * jnp.sum(m, axis=0, keepdims=True)) + a * e - d,
                       mm, deg, eg, delc)
            dgc = each(lambda g, d, l: g + jnp.where(p["rcol"] == CH - 1, jnp.sum(d, axis=0, keepdims=True) + l, 0.0),
                       dgc, delc, dglast)
            dg_acc = jnp.zeros((CH, 128), f32)
            db_acc = jnp.zeros((CH, 128), f32)
            rows = p["rows"]
            for h in hs:
                dq_ref[rows, h * DH:(h + 1) * DH] = dq[h]
                dk_ref[rows, h * DH:(h + 1) * DH] = dk[h]
                dv_ref[rows, h * DH:(h + 1) * DH] = dv[h]
                dg_acc = dg_acc + dgc[h] * (lane == h).astype(f32)
                db_acc = db_acc + dbeta[h] * (lane == H + h).astype(f32)
            dgb_ref[rows, :] = _hdot((r <= c).astype(f32), dg_acc) + db_acc
            return new_ds

        order = list(reversed(range(GDN_STEP)))
        pre = [before_state(s) for s in order]
        ds = [dstate[h] for h in hs]
        for p in pre:
            ds = after_state(p, ds)
        for h in hs:
            dstate[h] = ds[h]

    tb = GDN_STEP * CH
    n_steps = t // tb
    rev = lambda i: (n_steps - 1 - i, 0)
    rev4 = lambda i: (n_steps - 1 - i, 0, 0, 0)
    return pl.pallas_call(
        body, name="gdn_bwd", grid=(n_steps,),
        in_specs=[pl.BlockSpec((tb, D), rev), pl.BlockSpec((tb, D), rev), pl.BlockSpec((tb, D), rev),
                  pl.BlockSpec((tb, 128), rev), pl.BlockSpec((tb, D), rev),
                  pl.BlockSpec((GDN_STEP, H, DH, DH), rev4), pl.BlockSpec((GDN_STEP, H, CH, CH), rev4)],
        out_specs=[pl.BlockSpec((tb, D), rev), pl.BlockSpec((tb, D), rev), pl.BlockSpec((tb, D), rev),
                   pl.BlockSpec((tb, 128), rev)],
        out_shape=[jax.ShapeDtypeStruct((t, D), f32)] * 3 + [jax.ShapeDtypeStruct((t, 128), f32)],
        scratch_shapes=[pltpu.VMEM((H, DH, DH), f32)],
        compiler_params=_params(1),
    )(qn, kn, vc, gbeta, do, s_all, t_all)


def _pad_rows(w, rows=8):
    return jnp.pad(w, ((0, rows - w.shape[0]), (0, 0)))


_REST = ("w_up", "w_a_out", "w_b_out", "w_o", "w_down")


def _local_step(x, tgt, w, comm=None):
    g1 = w["norm_mix_g"].reshape(1, D)
    if comm is None:
        h1 = _rms_fwd(x, g1, name="rms1_fwd")
    else:
        h1, gathered = _rms_fwd(x, g1, name="rms1_fwd", exchange=comm.gather_first())
        w = {**w, **comm.finish_first(gathered)}
    w1, w2 = w["w1"], w["w2"]
    wa = _pad_rows(w["conv_a_w"])
    wg = _pad_rows(w["gdn_conv_w"])
    wf = _pad_rows(w["ffn_conv_w"])
    alog = jnp.pad(w["gdn_A_log"].reshape(1, H), ((0, 0), (0, 128 - H)))
    dtb = jnp.pad(w["gdn_dt_bias"].reshape(1, H), ((0, 0), (0, 128 - H)))
    g2 = w["norm_ffn_g"].reshape(1, D)
    g3 = w["norm_final_g"].reshape(1, D)
    gn = w["gdn_norm_g"].reshape(1, DH)

    if comm is None:
        pg = _matmul(h1, w1, name="mm_in", cols=(0, 6 * D), out_dtype=bf16)
    else:
        pg, gathered = _matmul(h1, w1, name="mm_in", cols=(0, 6 * D), out_dtype=bf16, exchange=comm.gather_rest())
        w = {**w, **comm.finish_gather(gathered)}
    pq = _matmul(h1, w1, name="mm_in_qkv", cols=(6 * D, 3 * D))
    p2 = _matmul(h1, w2, name="mm_in_ab")
    ya_in, qn, kn, vc, gbeta = _pre_fwd(pg, pq, p2, wa, wg, alog, dtb)
    o, s_all, t_all = _gdn_fwd(qn, kn, vc, gbeta)
    yb_in = _post_fwd(o, pg, gn)
    ya = _matmul(ya_in, w["w_a_out"], name="mm_a", out_dtype=bf16)
    yb = _matmul(yb_in, w["w_b_out"], name="mm_b", out_dtype=bf16)
    mix = _mix_fwd(ya, yb, pg)
    x2 = _matmul(mix, w["w_o"], name="mm_o", add=x)
    h2 = _rms_fwd(x2, g2, name="rms2_fwd")
    up = _matmul(h2, w["w_up"], nt=True, name="mm_up", tn=DFF // 2, out_dtype=bf16)
    act = _ffn_fwd(up, wf)
    x3 = _matmul(act, w["w_down"], name="mm_down", add=x2, tm=512)
    loss_p, dx3, dx3b, dg3 = _final(x3, tgt, g3)

    grads = {"norm_final_g": dg3}
    dact = _matmul(dx3b, w["w_down"], nt=True, name="mm_down_dx", tm=512, tn=DFF, out_dtype=bf16)
    grads["w_down"] = _matmul_tn(act, dx3b, name="mm_down_dw", tm=DFF // 2)
    dc, dwf = _ffn_bwd1(dact, up, wf)
    grads["ffn_conv_w"] = dwf
    dup = _ffn_bwd2(dc, wf)
    dh2 = _matmul(dup, w["w_up"], name="mm_up_dx", tk=DFF)
    grads["w_up"] = _matmul_tn(dup, h2, name="mm_up_dw", tm=DFF // 2)
    dx2, dx2b, dg2 = _rms_bwd(dh2, x2, g2, dx3, name="rms2_bwd")
    grads["norm_ffn_g"] = dg2
    dmix = _matmul(dx2b, w["w_o"], nt=True, name="mm_o_dx", out_dtype=bf16)
    grads["w_o"] = _matmul_tn(mix, dx2b, name="mm_o_dw")
    dya, dyb, dgates = _mix_bwd(dmix, ya, yb, pg)
    dya_in = _matmul(dya, w["w_a_out"], nt=True, name="mm_a_dx", out_dtype=bf16)
    grads["w_a_out"] = _matmul_tn(ya_in, dya, name="mm_a_dw")
    dyb_in = _matmul(dyb, w["w_b_out"], nt=True, name="mm_b_dx")
    grads["w_b_out"] = _matmul_tn(yb_in, dyb, name="mm_b_dw")
    do, dz, dgn = _post_bwd(dyb_in, o, pg, gn)
    grads["gdn_norm_g"] = dgn
    dqn, dkn, dvc, dgb = _gdn_bwd(qn, kn, vc, gbeta, do, s_all, t_all)
    dbg, dca, dc4, dp2, dwa, dwg, dal, ddt = _pre_bwd1(pg, pq, p2, dya_in, dqn, dkn, dvc, dgb, gbeta, wa, wg, alog, dtb)
    grads["conv_a_w"] = dwa
    grads["gdn_conv_w"] = dwg
    grads["gdn_A_log"] = dal
    grads["gdn_dt_bias"] = ddt
    grads["w2"] = _matmul_tn(h1, dp2, name="mm_in_ab_dw")
    if comm is None:
        dp1 = _pre_bwd2(dca, dc4, pg, dbg, dz, dgates, wa, wg)
        grads["w1"] = _matmul_tn(h1, dp1, name="mm_in_dw")
        dh1 = _matmul(dp1, w1, nt=True, name="mm_in_dx", tm=512, tk=NW1 // 2)
    else:
        exchange, blocks = comm.reduce_halves(_REST, grads)
        dp1, recv = _pre_bwd2(dca, dc4, pg, dbg, dz, dgates, wa, wg, exchange=exchange)
        exchange, sums = comm.reduce_sums(_REST, blocks, recv)
        grads["w1"], recv = _matmul_tn(h1, dp1, name="mm_in_dw", exchange=exchange)
        comm.finish_reduce(_REST, sums, recv)
        exchange, blocks = comm.reduce_halves(("w_in",), grads)
        exchange, sums = comm.reduce_sums(("w_in",), blocks, _run_exchange(exchange, name="rs_sibling_w_in"))
        dh1, recv = _matmul(dp1, w1, nt=True, name="mm_in_dx", tm=512, tk=NW1 // 2, exchange=exchange)
        comm.finish_reduce(("w_in",), sums, recv)
    dh1 = _matmul(dp2, w2, nt=True, name="mm_in_ab_dx", add=dh1)
    dx, _, dg1 = _rms_bwd(dh1, x, g1, dx2, name="rms1_bwd")
    grads["norm_mix_g"] = dg1
    return loss_p, dx, grads


_ANY = pl.BlockSpec(memory_space=pl.ANY)


def _remote(src, dst, send_sem, recv_sem, to):
    return pltpu.make_async_remote_copy(src_ref=src, dst_ref=dst, send_sem=send_sem, recv_sem=recv_sem,
                                        device_id=to, device_id_type=MESH)


def _run_exchange(exchange, *, name):
    arrays, shapes, sems, start, wait = exchange
    n_in, n_out = len(arrays), len(shapes)

    def body(*refs):
        start(refs[:n_in], refs[n_in:n_in + n_out], refs[n_in + n_out:])
        wait(refs[:n_in], refs[n_in:n_in + n_out], refs[n_in + n_out:])

    return pl.pallas_call(body, name=name, out_shape=list(shapes), in_specs=[_ANY] * n_in, out_specs=[_ANY] * n_out,
                          scratch_shapes=list(sems))(*arrays)


def _gather_exchange(shards):
    n = len(shards)

    def copies(x_refs, out_refs, sems):
        send_sems, recv_sems, local_sems = sems
        x, y, c = lax.axis_index("x"), lax.axis_index("y"), lax.axis_index("c")
        me, sibling = (x, y, c), (x, y, 1 - c)
        chips = [(1 - x, y), (x, 1 - y), (1 - x, 1 - y)]

        def copy(a, k, blk, to, from_input=False):
            dst = out_refs[a].at[4 * blk[0] + 2 * blk[1] + blk[2]]
            return _remote(x_refs[a] if from_input else dst, dst, send_sems.at[a, k], recv_sems.at[a, k], to)

        mine = [pltpu.make_async_copy(x_refs[a], out_refs[a].at[4 * x + 2 * y + c], local_sems.at[a]) for a in range(n)]
        first = []
        for a in range(n):
            first.append(copy(a, 0, me, sibling, from_input=True))
            first += [copy(a, 1 + j, me, (*chip, c), from_input=True) for j, chip in enumerate(chips)]
        return copy, mine, first, me, sibling, chips, c

    def start(x_refs, out_refs, sems):
        _, mine, first, *_ = copies(x_refs, out_refs, sems)
        for cp in mine + first:
            cp.start()

    def wait(x_refs, out_refs, sems):
        copy, mine, first, me, sibling, chips, c = copies(x_refs, out_refs, sems)
        passed = []
        for j, chip in enumerate(chips):
            for a in range(n):
                copy(a, 1 + j, (*chip, c), me).wait_recv()
                passed.append(copy(a, 4 + j, (*chip, c), sibling))
                passed[-1].start()
        for a in range(n):
            copy(a, 0, sibling, me).wait_recv()
            for j, chip in enumerate(chips):
                copy(a, 4 + j, (*chip, 1 - c), me).wait_recv()
        for cp in first + passed:
            cp.wait_send()
        for cp in mine:
            cp.wait()

    shapes = [jax.ShapeDtypeStruct((N_DEV, *s.shape), s.dtype) for s in shards]
    sems = [pltpu.SemaphoreType.DMA((n, 7)), pltpu.SemaphoreType.DMA((n, 7)), pltpu.SemaphoreType.DMA((n,))]
    return shards, shapes, sems, start, wait


def _gather_direct_exchange(shards):
    n = len(shards)

    def copies(x_refs, out_refs, sems):
        send_sems, recv_sems, local_sems = sems
        x, y, c = lax.axis_index("x"), lax.axis_index("y"), lax.axis_index("c")
        targets = [(x, y, 1 - c), (1 - x, y, c), (x, 1 - y, c), (1 - x, 1 - y, c)]
        local, sends, recvs = [], [], []
        for a in range(n):
            mine = out_refs[a].at[4 * x + 2 * y + c]
            local.append(pltpu.make_async_copy(x_refs[a], mine, local_sems.at[a]))
            for k, to in enumerate(targets):
                theirs = out_refs[a].at[4 * to[0] + 2 * to[1] + to[2]]
                sends.append(_remote(x_refs[a], mine, send_sems.at[a, k], recv_sems.at[a, k], to))
                recvs.append(_remote(theirs, theirs, send_sems.at[a, k], recv_sems.at[a, k], to))
        return local, sends, recvs

    def start(x_refs, out_refs, sems):
        local, sends, _ = copies(x_refs, out_refs, sems)
        for cp in local + sends:
            cp.start()

    def wait(x_refs, out_refs, sems):
        local, sends, recvs = copies(x_refs, out_refs, sems)
        for cp in recvs:
            cp.wait_recv()
        for cp in sends:
            cp.wait_send()
        for cp in local:
            cp.wait()

    shapes = [jax.ShapeDtypeStruct((N_DEV, *s.shape), s.dtype) for s in shards]
    sems = [pltpu.SemaphoreType.DMA((n, 4)), pltpu.SemaphoreType.DMA((n, 4)), pltpu.SemaphoreType.DMA((n,))]
    return shards, shapes, sems, start, wait


def _gather_forward(gathered):
    n = len(gathered)

    def body(*refs):
        out_refs = refs[n:2 * n]
        send_sems, recv_sems = refs[2 * n:]
        x, y, c = lax.axis_index("x"), lax.axis_index("y"), lax.axis_index("c")
        sibling = (x, y, 1 - c)
        sends, recvs = [], []
        for a in range(n):
            for j, (px, py) in enumerate([(1 - x, y), (x, 1 - y), (1 - x, 1 - y)]):
                mine = out_refs[a].at[4 * px + 2 * py + c]
                theirs = out_refs[a].at[4 * px + 2 * py + 1 - c]
                sends.append(_remote(mine, mine, send_sems.at[a, j], recv_sems.at[a, j], sibling))
                recvs.append(_remote(theirs, theirs, send_sems.at[a, j], recv_sems.at[a, j], sibling))
        for cp in sends:
            cp.start()
        for cp in recvs:
            cp.wait_recv()
        for cp in sends:
            cp.wait_send()

    return pl.pallas_call(
        body, name="ag_forward", out_shape=[jax.ShapeDtypeStruct(g.shape, g.dtype) for g in gathered],
        in_specs=[_ANY] * n, out_specs=[_ANY] * n, input_output_aliases={a: a for a in range(n)},
        scratch_shapes=[pltpu.SemaphoreType.DMA((n, 3)), pltpu.SemaphoreType.DMA((n, 3))],
    )(*gathered)


def _chips_exchange(hsums):
    n = len(hsums)

    def copies(h_refs, out_refs, sems):
        send_sems, recv_sems = sems
        x, y, c = lax.axis_index("x"), lax.axis_index("y"), lax.axis_index("c")
        chips = [(1 - x, y), (x, 1 - y), (1 - x, 1 - y)]
        return [_remote(h_refs[a].at[2 * px + py], out_refs[a].at[k], send_sems.at[a, k], recv_sems.at[a, k], (px, py, c))
                for a in range(n) for k, (px, py) in enumerate(chips)]

    def start(h_refs, out_refs, sems):
        for cp in copies(h_refs, out_refs, sems):
            cp.start()

    def wait(h_refs, out_refs, sems):
        for cp in copies(h_refs, out_refs, sems):
            cp.wait()

    shapes = [jax.ShapeDtypeStruct((3, *h.shape[1:]), h.dtype) for h in hsums]
    sems = [pltpu.SemaphoreType.DMA((n, 3)), pltpu.SemaphoreType.DMA((n, 3))]
    return hsums, shapes, sems, start, wait


def _sibling_exchange(halves):
    n = len(halves)

    def copies(p_refs, out_refs, sems):
        send_sems, recv_sems = sems
        x, y, c = lax.axis_index("x"), lax.axis_index("y"), lax.axis_index("c")
        return [_remote(p_refs[a], out_refs[a], send_sems.at[a], recv_sems.at[a], (x, y, 1 - c)) for a in range(n)]

    def start(p_refs, out_refs, sems):
        for cp in copies(p_refs, out_refs, sems):
            cp.start()

    def wait(p_refs, out_refs, sems):
        for cp in copies(p_refs, out_refs, sems):
            cp.wait()

    shapes = [jax.ShapeDtypeStruct(h.shape, h.dtype) for h in halves]
    return halves, shapes, [pltpu.SemaphoreType.DMA((n,)), pltpu.SemaphoreType.DMA((n,))], start, wait


_IN_RANGES = ((0, 3 * D, 0, 0), (3 * D, 6 * D, 0, 6 * D), (6 * D, 7 * D, 0, 3 * D), (7 * D, 7 * D + 16, 1, 0),
              (7 * D + 16, 9 * D + 16, 0, 4 * D))


def _col_pieces(width, ranges):
    pieces = []
    for d in range(N_DEV):
        lo, hi = d * width, (d + 1) * width
        for glo, ghi, mat, mlo in ranges:
            a, b = max(lo, glo), min(hi, ghi)
            if a < b:
                pieces.append((d, a - lo, b - lo, mat, mlo + a - glo))
    return pieces


def _cols_to_matrices(g, ranges, out_widths, *, name):
    _, rows, width = g.shape
    tb = 128
    pieces = _col_pieces(width, ranges)
    covered = [sum(p[2] - p[1] for p in pieces if p[3] == m) for m in range(len(out_widths))]

    def body(g_ref, *o_refs):
        for m, o_ref in enumerate(o_refs):
            if covered[m] < out_widths[m]:
                o_ref[...] = jnp.zeros_like(o_ref)
        for d, b0, b1, m, m0 in pieces:
            o_refs[m][:, m0:m0 + b1 - b0] = g_ref[d, :, b0:b1]

    return pl.pallas_call(
        body, name=name, grid=(rows // tb,), in_specs=[pl.BlockSpec((N_DEV, tb, width), lambda i: (0, i, 0))],
        out_specs=[pl.BlockSpec((tb, wo), lambda i: (i, 0)) for wo in out_widths],
        out_shape=[jax.ShapeDtypeStruct((rows, wo), g.dtype) for wo in out_widths], compiler_params=_params(1),
    )(g)


def _matrices_to_cols(mats, ranges, width, *, name):
    rows = mats[0].shape[0]
    tb = 128
    pieces = _col_pieces(width, ranges)

    def body(*refs):
        m_refs, g_ref = refs[:-1], refs[-1]
        for d, b0, b1, m, m0 in pieces:
            g_ref[d, :, b0:b1] = m_refs[m][:, m0:m0 + b1 - b0]

    return pl.pallas_call(
        body, name=name, grid=(rows // tb,),
        in_specs=[pl.BlockSpec((tb, mt.shape[1]), lambda i: (i, 0)) for mt in mats],
        out_specs=pl.BlockSpec((N_DEV, tb, width), lambda i: (0, i, 0)),
        out_shape=jax.ShapeDtypeStruct((N_DEV, rows, width), mats[0].dtype), compiler_params=_params(1),
    )(*mats)


def _row_block(rows):
    return 128 if rows % 128 == 0 else rows


def _half_bf16(g4, c_other, *, name):
    _, _, rows, width = g4.shape
    tb = _row_block(rows)

    def body(c_ref, p_ref, o_ref):
        o_ref[0] = p_ref[0, 0].astype(bf16)

    grid_spec = pltpu.PrefetchScalarGridSpec(
        num_scalar_prefetch=1, grid=(4, rows // tb),
        in_specs=[pl.BlockSpec((1, 1, tb, width), lambda j, i, c_ref: (j, c_ref[0], i, 0))],
        out_specs=pl.BlockSpec((1, tb, width), lambda j, i, c_ref: (j, i, 0)))
    return pl.pallas_call(
        body, name=name, grid_spec=grid_spec, out_shape=jax.ShapeDtypeStruct((4, rows, width), bf16),
        compiler_params=_params(2),
    )(c_other, g4)


def _pair_sum(g4, recv, c_me, *, name):
    _, _, rows, width = g4.shape
    tb = _row_block(rows)

    def body(c_ref, p_ref, r_ref, o_ref, ob_ref):
        s = p_ref[0, 0] + r_ref[0].astype(f32)
        o_ref[0] = s
        ob_ref[0] = s.astype(bf16)

    blk = pl.BlockSpec((1, tb, width), lambda j, i, c_ref: (j, i, 0))
    grid_spec = pltpu.PrefetchScalarGridSpec(
        num_scalar_prefetch=1, grid=(4, rows // tb),
        in_specs=[pl.BlockSpec((1, 1, tb, width), lambda j, i, c_ref: (j, c_ref[0], i, 0)), blk],
        out_specs=[blk, blk])
    return pl.pallas_call(
        body, name=name, grid_spec=grid_spec,
        out_shape=[jax.ShapeDtypeStruct((4, rows, width), f32), jax.ShapeDtypeStruct((4, rows, width), bf16)],
        compiler_params=_params(2),
    )(c_me, g4, recv)


def _adam_shard(hsum, recv, chip, w, m, v, *, name):
    _, rows, width = w.shape
    tb = _row_block(rows)

    def body(j_ref, h_ref, r_ref, w_ref, m_ref, v_ref, g_out, d_out, m_out, v_out):
        g = ((h_ref[0] + r_ref[0].astype(f32)) + r_ref[1].astype(f32)) + r_ref[2].astype(f32)
        delta, mn, vn = _adam_math(w_ref[0], g, m_ref[0], v_ref[0])
        g_out[0] = g
        d_out[0] = delta
        m_out[0] = mn
        v_out[0] = vn

    blk = pl.BlockSpec((1, tb, width), lambda i, j_ref: (0, i, 0))
    grid_spec = pltpu.PrefetchScalarGridSpec(
        num_scalar_prefetch=1, grid=(rows // tb,),
        in_specs=[pl.BlockSpec((1, tb, width), lambda i, j_ref: (j_ref[0], i, 0)),
                  pl.BlockSpec((3, tb, width), lambda i, j_ref: (0, i, 0)), blk, blk, blk],
        out_specs=[blk, blk, blk, blk])
    return pl.pallas_call(
        body, name=name, grid_spec=grid_spec, out_shape=[jax.ShapeDtypeStruct(w.shape, f32)] * 4,
        compiler_params=_params(1),
    )(chip, hsum, recv, w, m, v)


R_SMALL = 16 + 16 * N_DEV
_SMALL_LANES = {"gdn_norm_g": (0, DH), "gdn_A_log": (DH, DH + H), "gdn_dt_bias": (2 * DH, 2 * DH + H)}
_LOSS_LANE = 3 * DH


def _pack_small(dg1, dg2, dg3, dgn, dal, ddt, loss_p, dwa, dwg, dwf):
    def body(dg1_ref, dg2_ref, dg3_ref, dgn_ref, dal_ref, ddt_ref, loss_ref, dwa_ref, dwg_ref, dwf_ref, o_ref):
        def total(ref):
            return jnp.sum(ref[...], axis=0, keepdims=True)

        o_ref[...] = jnp.zeros_like(o_ref)
        o_ref[0:1, :] = total(dg1_ref)
        o_ref[1:2, :] = total(dg2_ref)
        o_ref[2:3, :] = total(dg3_ref)
        o_ref[3:4, 0:DH] = total(dgn_ref)
        o_ref[3:4, DH:2 * DH] = total(dal_ref)
        o_ref[3:4, 2 * DH:3 * DH] = total(ddt_ref)
        o_ref[3:4, 3 * DH:4 * DH] = total(loss_ref)
        for d in range(N_DEV):
            base = 16 + 16 * d
            o_ref[base:base + 3, 0:128] = dwa_ref[0:3, 128 * d:128 * (d + 1)]
            o_ref[base:base + 4, 128:512] = dwg_ref[0:4, 384 * d:384 * (d + 1)]
            o_ref[base + 8:base + 11, 0:704] = dwf_ref[0:3, 704 * d:704 * (d + 1)]

    return pl.pallas_call(body, name="pack_small", out_shape=jax.ShapeDtypeStruct((R_SMALL, D), f32))(
        dg1, dg2, dg3, dgn, dal, ddt, loss_p, dwa, dwg, dwf)


_SMALL = ("norm_mix_g", "norm_ffn_g", "norm_final_g", "gdn_norm_g", "gdn_A_log", "gdn_dt_bias",
          "conv_a_w", "gdn_conv_w", "ffn_conv_w")


def _adam_small(gath, me, w, m, v):
    arrays = [t[n] for n in _SMALL for t in (w, m, v)]

    def body(me_ref, ga_ref, gb_ref, *refs):
        ins, outs = refs[:len(arrays)], refs[len(arrays):]
        ga, gb = ga_ref[0], gb_ref[0]
        for s in range(1, N_DEV):
            ga = ga + ga_ref[s]
            gb = gb + gb_ref[s]
        grads = {"norm_mix_g": ga[0:1, :], "norm_ffn_g": ga[1:2, :], "norm_final_g": ga[2:3, :],
                 "conv_a_w": gb[0:3, 0:128], "gdn_conv_w": gb[0:4, 128:512], "ffn_conv_w": gb[8:11, 0:704]}
        for n, (lo, hi) in _SMALL_LANES.items():
            grads[n] = ga[3:4, lo:hi]
        for i, n in enumerate(_SMALL):
            three_d = len(w[n].shape) == 3
            wv, mv, vv = (r[0] if three_d else r[...] for r in ins[3 * i:3 * i + 3])
            delta, mn, vn = _adam_math(wv, grads[n], mv, vv)
            for o_ref, val in zip(outs[4 * i:4 * i + 4], (grads[n], delta, mn, vn)):
                if three_d:
                    o_ref[0] = val
                else:
                    o_ref[...] = val
        outs[-1][...] = ga[3:4, _LOSS_LANE:_LOSS_LANE + 1]

    def whole(shape):
        return pl.BlockSpec(shape, lambda i, me_ref: (0,) * len(shape))

    grid_spec = pltpu.PrefetchScalarGridSpec(
        num_scalar_prefetch=1, grid=(1,),
        in_specs=[pl.BlockSpec((N_DEV, 16, D), lambda i, me_ref: (0, 0, 0)),
                  pl.BlockSpec((N_DEV, 16, D), lambda i, me_ref: (0, 1 + me_ref[0], 0))] + [whole(a.shape) for a in arrays],
        out_specs=[whole(w[n].shape) for n in _SMALL for _ in range(4)] + [whole((1, 1))])
    res = pl.pallas_call(
        body, name="adam_small", grid_spec=grid_spec,
        out_shape=[jax.ShapeDtypeStruct(w[n].shape, f32) for n in _SMALL for _ in range(4)]
        + [jax.ShapeDtypeStruct((1, 1), f32)],
        compiler_params=_params(1),
    )(me, gath, gath, *arrays)
    return {n: tuple(res[4 * i:4 * i + 4]) for i, n in enumerate(_SMALL)}, res[-1]


def _adam_math(w, g, m, v):
    m = ADAM_B1 * m + (1.0 - ADAM_B1) * g
    v = ADAM_B2 * v + (1.0 - ADAM_B2) * jnp.square(g)
    m_hat = m / (1.0 - ADAM_B1 ** ADAM_STEP)
    v_hat = v / (1.0 - ADAM_B2 ** ADAM_STEP)
    delta = -ADAM_LR * (m_hat / (jnp.sqrt(v_hat) + ADAM_EPS) + ADAM_WD * w)
    return delta, m, v


_WEIGHTS = ("norm_mix_g", "w_in", "conv_a_w", "gdn_conv_w", "gdn_A_log", "gdn_dt_bias", "gdn_norm_g", "w_a_out",
            "w_b_out", "w_o", "norm_ffn_g", "w_up", "ffn_conv_w", "w_down", "norm_final_g")
_BIG = ("w_in",) + _REST
_CONVS = ("conv_a_w", "gdn_conv_w", "ffn_conv_w")


class _StepExchanges:
    def __init__(self, wts, mom, var, c_me, chip):
        self.wts, self.mom, self.var, self.c_me, self.chip = wts, mom, var, c_me, chip
        self.results = {}

    def gather_first(self):
        return _gather_exchange([self.wts["w_in"][0].astype(bf16)] + [self.wts[n][0] for n in _CONVS])

    def finish_first(self, gathered):
        g_in, gc_a, gc_g, gc_f = gathered
        w1, w2 = _cols_to_matrices(g_in, _IN_RANGES, (NW1, 128), name="relay_w_in")
        return {"w1": w1, "w2": w2, "conv_a_w": gc_a.transpose(1, 0, 2).reshape(3, D),
                "gdn_conv_w": gc_g.transpose(1, 0, 2).reshape(4, 3 * D),
                "ffn_conv_w": gc_f.transpose(1, 0, 2).reshape(3, 2 * DFF)}

    def gather_rest(self):
        return _gather_direct_exchange([self.wts[n][0].astype(bf16) for n in _REST])

    def finish_gather(self, gathered):
        g_up, g_a, g_b, g_o, g_down = _gather_forward(gathered)
        return {"w_up": g_up.reshape(2 * DFF, D), "w_a_out": g_a.reshape(D, D), "w_b_out": g_b.reshape(D, D),
                "w_o": g_o.reshape(D, D), "w_down": g_down.reshape(DFF, D)}

    def reduce_halves(self, names, grads):
        blocks = []
        for n in names:
            if n == "w_in":
                g = _matrices_to_cols([grads["w1"], grads["w2"]], _IN_RANGES, R_IN, name="relay_dw_in")
            else:
                g = grads[n]
            blocks.append(g.reshape(4, 2, *self.wts[n].shape[1:]))
        return _sibling_exchange([_half_bf16(g, 1 - self.c_me, name="rs_half_" + n) for n, g in zip(names, blocks)]), blocks

    def reduce_sums(self, names, blocks, recv):
        sums = [_pair_sum(g, r, self.c_me, name="rs_sum_" + n) for n, g, r in zip(names, blocks, recv)]
        return _chips_exchange([s[1] for s in sums]), [s[0] for s in sums]

    def finish_reduce(self, names, sums, recv):
        for n, s, r in zip(names, sums, recv):
            self.results[n] = _adam_shard(s, r, self.chip, self.wts[n], self.mom[n], self.var[n], name="adam_" + n)


def kernel(x, norm_mix_g, w_in, conv_a_w, gdn_conv_w, gdn_A_log, gdn_dt_bias, gdn_norm_g, w_a_out, w_b_out, w_o, norm_ffn_g, w_up, ffn_conv_w, w_down, norm_final_g, loss_target, m_norm_mix_g, m_w_in, m_conv_a_w, m_gdn_conv_w, m_gdn_A_log, m_gdn_dt_bias, m_gdn_norm_g, m_w_a_out, m_w_b_out, m_w_o, m_norm_ffn_g, m_w_up, m_ffn_conv_w, m_w_down, m_norm_final_g, v_norm_mix_g, v_w_in, v_conv_a_w, v_gdn_conv_w, v_gdn_A_log, v_gdn_dt_bias, v_gdn_norm_g, v_w_a_out, v_w_b_out, v_w_o, v_norm_ffn_g, v_w_up, v_ffn_conv_w, v_w_down, v_norm_final_g):
    wts = dict(zip(_WEIGHTS, (norm_mix_g, w_in, conv_a_w, gdn_conv_w, gdn_A_log, gdn_dt_bias, gdn_norm_g, w_a_out,
                              w_b_out, w_o, norm_ffn_g, w_up, ffn_conv_w, w_down, norm_final_g)))
    mom = dict(zip(_WEIGHTS, (m_norm_mix_g, m_w_in, m_conv_a_w, m_gdn_conv_w, m_gdn_A_log, m_gdn_dt_bias,
                              m_gdn_norm_g, m_w_a_out, m_w_b_out, m_w_o, m_norm_ffn_g, m_w_up, m_ffn_conv_w,
                              m_w_down, m_norm_final_g)))
    var = dict(zip(_WEIGHTS, (v_norm_mix_g, v_w_in, v_conv_a_w, v_gdn_conv_w, v_gdn_A_log, v_gdn_dt_bias,
                              v_gdn_norm_g, v_w_a_out, v_w_b_out, v_w_o, v_norm_ffn_g, v_w_up, v_ffn_conv_w,
                              v_w_down, v_norm_final_g)))
    cx, cy, cc = lax.axis_index("x"), lax.axis_index("y"), lax.axis_index("c")
    c_me = jnp.reshape(cc, (1,)).astype(jnp.int32)
    chip = jnp.reshape(2 * cx + cy, (1,)).astype(jnp.int32)
    me = jnp.reshape(4 * cx + 2 * cy + cc, (1,)).astype(jnp.int32)

    def with_up_transposed(t):
        return {**t, "w_up": jnp.swapaxes(t["w_up"], 1, 2)}

    comm = _StepExchanges(with_up_transposed(wts), with_up_transposed(mom), with_up_transposed(var), c_me, chip)
    replicated = {n: wts[n] for n in ("norm_mix_g", "norm_ffn_g", "norm_final_g", "gdn_norm_g", "gdn_A_log", "gdn_dt_bias")}
    loss_p, dx, grads = _local_step(x[0], loss_target[0], replicated, comm)
    res = comm.results
    res["w_up"] = tuple(jnp.swapaxes(a, 1, 2) for a in res["w_up"])

    small = _pack_small(grads["norm_mix_g"], grads["norm_ffn_g"], grads["norm_final_g"], grads["gdn_norm_g"],
                        grads["gdn_A_log"], grads["gdn_dt_bias"], loss_p, grads["conv_a_w"], grads["gdn_conv_w"],
                        grads["ffn_conv_w"])
    (small_all,) = _run_exchange(_gather_exchange([small]), name="ag_small")

    def raw(t):
        return {n: t[n].reshape(1, D) if n == "norm_final_g" else t[n] for n in _SMALL}

    res_small, loss = _adam_small(small_all, me, raw(wts), raw(mom), raw(var))
    for n in _SMALL:
        res[n] = tuple(a.reshape(wts[n].shape) for a in res_small[n])
    outs = [[res[n][i] for n in _WEIGHTS] for i in range(4)]
    return (loss.reshape(()), dx[None], *outs[0], *outs[1], *outs[2], *outs[3])
```

```python
import jax
import jax.numpy as jnp
from jax import lax
from jax.experimental import pallas as pl
from jax.experimental.pallas import tpu as pltpu

f32 = jnp.float32
bf16 = jnp.bfloat16

D = 1024
H = 8
DH = 128
CH = 64
GDN_STEP = 2
DFF = 2816
NW1 = 9216
EPS = 1e-6
N_DEV = 8

ADAM_LR = 0.001
ADAM_B1 = 0.9
ADAM_B2 = 0.999
ADAM_EPS = 1e-08
ADAM_WD = 0.01
ADAM_STEP = 10

VMEM_LIMIT_BYTES = 48 * 1024 * 1024

R_IN, R_UP = 1154, 704

_HI = lax.Precision.HIGHEST
MESH = pl.DeviceIdType.MESH


def _params(n_grid):
    return pltpu.CompilerParams(dimension_semantics=("arbitrary",) * n_grid, vmem_limit_bytes=VMEM_LIMIT_BYTES)


def _bdot(a, b):
    return jnp.dot(a.astype(bf16), b.astype(bf16), preferred_element_type=f32)


def _bdot_nt(a, b):
    return lax.dot_general(a.astype(bf16), b.astype(bf16), (((1,), (1,)), ((), ())), preferred_element_type=f32)


def _bdot_tn(a, b):
    return lax.dot_general(a.astype(bf16), b.astype(bf16), (((0,), (0,)), ((), ())), preferred_element_type=f32)


def _hdot(a, b):
    return jnp.dot(a, b, preferred_element_type=f32, precision=_HI)


def _idot(a, b):
    return jnp.dot(a, b, preferred_element_type=f32, precision=lax.Precision.HIGH)


def _sigmoid(x):
    return 1.0 / (1.0 + jnp.exp(-x))


def _softplus(x):
    return jnp.maximum(x, 0.0) + jnp.log(1.0 + jnp.exp(-jnp.abs(x)))


def _shift_down(x, halo, j):
    if j == 0:
        return x
    xr = pltpu.roll(x, j, 0)
    hr = pltpu.roll(halo, j, 0)
    r8 = lax.broadcasted_iota(jnp.int32, hr.shape, 0)
    top = jnp.where(r8 < j, hr, xr[:8])
    return jnp.concatenate([top, xr[8:]], axis=0)


def _shift_up(x, halo, j):
    if j == 0:
        return x
    n = x.shape[0]
    xr = pltpu.roll(x, n - j, 0)
    hr = pltpu.roll(halo, 8 - j, 0)
    r8 = lax.broadcasted_iota(jnp.int32, hr.shape, 0)
    bot = jnp.where(r8 >= 8 - j, hr, xr[n - 8:])
    return jnp.concatenate([xr[:n - 8], bot], axis=0)


def _taps_down(x, halo, k):
    return [_shift_down(x, halo, k - 1 - j) for j in range(k)]


def _strip(i, base=0):
    return slice(base + i * 128, base + (i + 1) * 128)


def _strip_taps(x, halo, first, k):
    return _taps_down(x, jnp.where(first, 0.0, halo), k)


def _strip_conv(w_ref, sl, taps):
    out = w_ref[0:1, sl] * taps[0]
    for j in range(1, len(taps)):
        out = out + w_ref[j:j + 1, sl] * taps[j]
    return out


def _strip_weight_grad(dw_ref, sl, dy, taps):
    for j, tap in enumerate(taps):
        dw_ref[j:j + 1, sl] += jnp.sum(dy * tap, axis=0, keepdims=True)


def _strip_conv_up(dy, halo, last, w_ref, sl, k):
    halo = jnp.where(last, 0.0, halo)
    out = w_ref[k - 1:k, sl] * dy
    for j in range(k - 1):
        out = out + w_ref[j:j + 1, sl] * _shift_up(dy, halo, k - 1 - j)
    return out


def _row(tb, w, col=0):
    return pl.BlockSpec((tb, w), lambda i: (i, col))


def _prev(tb, w, col=0, rows=8):
    return pl.BlockSpec((rows, w), lambda i: (jnp.maximum(i * (tb // rows) - 1, 0), col))


def _next(tb, w, n_rows, col=0, rows=8):
    last = n_rows // rows - 1
    return pl.BlockSpec((rows, w), lambda i: (jnp.minimum((i + 1) * (tb // rows), last), col))


def _f32(ref, sl):
    return ref[:, sl].astype(f32)


def _halo_before(ref, sl):
    h = _f32(ref, sl)
    return h[h.shape[0] - 8:]


def _halo_after(ref, sl):
    return _f32(ref, sl)[:8]


def _fixed(shape):
    return pl.BlockSpec(shape, lambda i: (0,) * len(shape))


def _pick(n, prefs):
    for p in prefs:
        if n % p == 0:
            return p
    return n


def _matmul(a, b, *, name, nt=False, add=None, tm=1024, tn=1024, tk=None, out_dtype=f32, cols=None, exchange=None):
    m, kd = a.shape
    col0, n = cols if cols is not None else (0, b.shape[0] if nt else b.shape[1])
    tm = _pick(m, (tm, 512, 256))
    tn = _pick(n, (tn, 1024, 512, 128))
    tk = kd if tk is None else tk
    nk = kd // tk
    assert nk == 1 or out_dtype == f32
    assert col0 % tn == 0 and not (nt and cols)
    j0 = col0 // tn
    dims = (((1,), (1,)), ((), ())) if nt else (((1,), (0,)), ((), ()))

    def body(a_ref, b_ref, *rest):
        o_ref = rest[-1]
        part = lax.dot_general(a_ref[...], b_ref[...], dims, preferred_element_type=f32)
        if nk == 1:
            o_ref[...] = (part if add is None else part + rest[0][...]).astype(out_dtype)
            return
        k = pl.program_id(2)

        @pl.when(k == 0)
        def _():
            o_ref[...] = part if add is None else part + rest[0][...]

        @pl.when(k > 0)
        def _():
            o_ref[...] += part

    b_spec = pl.BlockSpec((tn, tk), lambda i, j, k: (j, k)) if nt else pl.BlockSpec((tk, tn), lambda i, j, k: (k, j + j0))
    in_specs = [pl.BlockSpec((tm, tk), lambda i, j, k: (i, k)), b_spec]
    args = [a, b]
    if add is not None:
        in_specs.append(pl.BlockSpec((tm, tn), lambda i, j, k: (i, j)))
        args.append(add)
    return _call_with_exchange(
        body, exchange, name=name, grid=(m // tm, n // tn, nk), in_specs=in_specs,
        out_specs=pl.BlockSpec((tm, tn), lambda i, j, k: (i, j)),
        out_shape=jax.ShapeDtypeStruct((m, n), out_dtype), args=args)


def _call_with_exchange(body, exchange, *, name, grid, in_specs, out_specs, out_shape, args):
    if exchange is None:
        return pl.pallas_call(body, name=name, grid=grid, in_specs=in_specs, out_specs=out_specs, out_shape=out_shape,
                              compiler_params=_params(len(grid)))(*args)
    x_arrays, x_shapes, x_sems, start, wait = exchange
    n_in, n_xin, n_xout = len(args), len(x_arrays), len(x_shapes)

    def full_body(*refs):
        c_in, x_in = refs[:n_in], refs[n_in:n_in + n_xin]
        c_out = refs[n_in + n_xin]
        x_out = refs[n_in + n_xin + 1:n_in + n_xin + 1 + n_xout]
        sems = refs[n_in + n_xin + 1 + n_xout:]
        ids = [pl.program_id(d) for d in range(len(grid))]
        first, last = ids[0] == 0, ids[0] == grid[0] - 1
        for d in range(1, len(grid)):
            first = first & (ids[d] == 0)
            last = last & (ids[d] == grid[d] - 1)

        @pl.when(first)
        def _():
            start(x_in, x_out, sems)

        body(*c_in, c_out)

        @pl.when(last)
        def _():
            wait(x_in, x_out, sems)

    res = pl.pallas_call(
        full_body, name=name, grid=grid, in_specs=list(in_specs) + [_ANY] * n_xin,
        out_specs=[out_specs] + [_ANY] * n_xout, out_shape=[out_shape] + list(x_shapes),
        scratch_shapes=list(x_sems), compiler_params=_params(len(grid)),
    )(*args, *x_arrays)
    return res[0], list(res[1:])


def _matmul_tn(a, b, *, name, tm=1024, tn=1024, exchange=None):
    t, m = a.shape
    _, n = b.shape
    tm = _pick(m, (tm, 1024, 512, 128))
    tn = _pick(n, (tn, 1024, 512, 128))
    tt = _pick(t, (2048, 1024, 512, 256))
    nt = t // tt

    def body(a_ref, b_ref, o_ref):
        k = pl.program_id(2)
        part = lax.dot_general(a_ref[...], b_ref[...], (((0,), (0,)), ((), ())), preferred_element_type=f32)

        @pl.when(k == 0)
        def _():
            o_ref[...] = part

        @pl.when(k > 0)
        def _():
            o_ref[...] += part

    return _call_with_exchange(
        body, exchange, name=name, grid=(m // tm, n // tn, nt),
        in_specs=[pl.BlockSpec((tt, tm), lambda i, j, k: (k, i)), pl.BlockSpec((tt, tn), lambda i, j, k: (k, j))],
        out_specs=pl.BlockSpec((tm, tn), lambda i, j, k: (i, j)),
        out_shape=jax.ShapeDtypeStruct((m, n), f32), args=[a, b])


def _rms_fwd(x, g, *, name, exchange=None):
    t = x.shape[0]
    tb = _pick(t, (256, 128))

    def body(x_ref, g_ref, h_ref):
        xv = x_ref[...]
        r = lax.rsqrt(jnp.mean(xv * xv, axis=-1, keepdims=True) + EPS)
        h_ref[...] = (xv * r * g_ref[...]).astype(bf16)

    return _call_with_exchange(
        body, exchange, name=name, grid=(t // tb,), in_specs=[_row(tb, D), _fixed((1, D))], out_specs=_row(tb, D),
        out_shape=jax.ShapeDtypeStruct((t, D), bf16), args=[x, g])


def _rms_bwd(dh, x, g, dres, *, name):
    t = x.shape[0]
    tb = _pick(t, (256, 128))

    def body(dh_ref, x_ref, g_ref, dres_ref, dx_ref, dxb_ref, dg_ref):
        xv = x_ref[...]
        r = lax.rsqrt(jnp.mean(xv * xv, axis=-1, keepdims=True) + EPS)
        xh = xv * r
        dy = dh_ref[...]
        dyg = dy * g_ref[...]
        dx = dres_ref[...] + r * (dyg - xh * jnp.mean(dyg * xh, axis=-1, keepdims=True))
        dx_ref[...] = dx
        dxb_ref[...] = dx.astype(bf16)

        @pl.when(pl.program_id(0) == 0)
        def _():
            dg_ref[...] = jnp.zeros_like(dg_ref)

        dg_ref[...] += jnp.sum((dy * xh).reshape(tb // 8, 8, D), axis=0)

    return pl.pallas_call(
        body, name=name, grid=(t // tb,),
        in_specs=[_row(tb, D), _row(tb, D), _fixed((1, D)), _row(tb, D)],
        out_specs=[_row(tb, D), _row(tb, D), _fixed((8, D))],
        out_shape=[jax.ShapeDtypeStruct((t, D), f32), jax.ShapeDtypeStruct((t, D), bf16),
                   jax.ShapeDtypeStruct((8, D), f32)],
        compiler_params=_params(1),
    )(dh, x, g, dres)


def _gdn_gates(ab, alog, dtb):
    lane = lax.broadcasted_iota(jnp.int32, ab.shape, 1)
    g = -jnp.exp(alog) * _softplus(ab + dtb)
    beta = _sigmoid(ab)
    return jnp.where(lane < H, g, jnp.where(lane < 2 * H, beta, 0.0))


def _pre_fwd(pg, pq, p2, wa, wg, alog, dtb):
    t = pg.shape[0]
    tb = 128

    def body(p0_ref, p0h_ref, pq_ref, pqh_ref, p2_ref, wa_ref, wg_ref, alog_ref, dtb_ref,
             ya_ref, qn_ref, kn_ref, vc_ref, gb_ref):
        first = pl.program_id(0) == 0
        for i in range(D // 128):
            sl, cg, xv = _strip(i), _strip(i, D), _strip(i, 2 * D)
            taps = _strip_taps(_f32(p0_ref, cg) * _f32(p0_ref, xv), _halo_before(p0h_ref, cg) * _halo_before(p0h_ref, xv),
                               first, 3)
            ya_ref[:, sl] = (_f32(p0_ref, sl) * _strip_conv(wa_ref, sl, taps)).astype(bf16)
        for part, out_ref, scale in ((0, qn_ref, DH ** -0.5), (1, kn_ref, 1.0), (2, vc_ref, None)):
            for h in range(H):
                sl = _strip(h, part * D)
                s = _strip_conv(wg_ref, sl, _strip_taps(pq_ref[:, sl], pqh_ref[:, sl], first, 4))
                s = s * _sigmoid(s)
                if scale is not None:
                    s = s * (lax.rsqrt(jnp.sum(s * s, axis=-1, keepdims=True) + EPS) * scale)
                out_ref[:, _strip(h)] = s
        gb_ref[...] = _gdn_gates(p2_ref[...], alog_ref[...], dtb_ref[...])

    return pl.pallas_call(
        body, name="pre_fwd", grid=(t // tb,),
        in_specs=[_row(tb, 3 * D, 0), _prev(tb, 3 * D, 0, rows=16), _row(tb, 3 * D), _prev(tb, 3 * D), _row(tb, 128),
                  _fixed((8, D)), _fixed((8, 3 * D)), _fixed((1, 128)), _fixed((1, 128))],
        out_specs=[_row(tb, D), _row(tb, D), _row(tb, D), _row(tb, D), _row(tb, 128)],
        out_shape=[jax.ShapeDtypeStruct((t, D), bf16), jax.ShapeDtypeStruct((t, D), f32),
                   jax.ShapeDtypeStruct((t, D), f32), jax.ShapeDtypeStruct((t, D), f32),
                   jax.ShapeDtypeStruct((t, 128), f32)],
        compiler_params=_params(1),
    )(pg, pg, pq, pq, p2, wa, wg, alog, dtb)


_Z_COL, _GA_COL, _GB_COL = 3, 4, 5


def _post_fwd(o, pg, gn):
    t = o.shape[0]
    tb = _pick(t, (256, 128))

    def body(o_ref, z_ref, gn_ref, yb_ref):
        for h in range(H):
            sl = slice(h * DH, (h + 1) * DH)
            oh = o_ref[:, sl]
            z = _f32(z_ref, sl)
            r = lax.rsqrt(jnp.mean(oh * oh, axis=-1, keepdims=True) + EPS)
            yb_ref[:, sl] = (oh * r * gn_ref[...] * (z * _sigmoid(z))).astype(bf16)

    return pl.pallas_call(
        body, name="post_fwd", grid=(t // tb,), in_specs=[_row(tb, D), _row(tb, D, _Z_COL), _fixed((1, DH))],
        out_specs=_row(tb, D), out_shape=jax.ShapeDtypeStruct((t, D), bf16), compiler_params=_params(1),
    )(o, pg, gn)


def _post_bwd(dyb, o, pg, gn):
    t = o.shape[0]
    tb = _pick(t, (256, 128))

    def body(dyb_ref, o_ref, z_ref, gn_ref, do_ref, dz_ref, dgn_ref):
        @pl.when(pl.program_id(0) == 0)
        def _():
            dgn_ref[...] = jnp.zeros_like(dgn_ref)

        gn_v = gn_ref[...]
        acc = jnp.zeros((8, DH), f32)
        for h in range(H):
            sl = slice(h * DH, (h + 1) * DH)
            oh = o_ref[:, sl]
            z = _f32(z_ref, sl)
            dy = dyb_ref[:, sl]
            r = lax.rsqrt(jnp.mean(oh * oh, axis=-1, keepdims=True) + EPS)
            on = oh * r
            sg = _sigmoid(z)
            sz = z * sg
            don = dy * sz
            dz_ref[:, sl] = (dy * on * gn_v * (sg * (1.0 + z * (1.0 - sg)))).astype(bf16)
            acc = acc + jnp.sum((don * on).reshape(tb // 8, 8, DH), axis=0)
            doh = don * gn_v
            do_ref[:, sl] = r * (doh - on * jnp.mean(doh * on, axis=-1, keepdims=True))
        dgn_ref[...] += acc

    return pl.pallas_call(
        body, name="post_bwd", grid=(t // tb,),
        in_specs=[_row(tb, D), _row(tb, D), _row(tb, D, _Z_COL), _fixed((1, DH))],
        out_specs=[_row(tb, D), _row(tb, D), _fixed((8, DH))],
        out_shape=[jax.ShapeDtypeStruct((t, D), f32), jax.ShapeDtypeStruct((t, D), bf16),
                   jax.ShapeDtypeStruct((8, DH), f32)],
        compiler_params=_params(1),
    )(dyb, o, pg, gn)


def _mix_fwd(ya, yb, pg):
    t = ya.shape[0]
    tb = _pick(t, (256, 128))

    def body(ya_ref, yb_ref, ga_ref, gb_ref, mix_ref):
        ya_v, yb_v = ya_ref[...].astype(f32), yb_ref[...].astype(f32)
        mix = _sigmoid(ga_ref[...].astype(f32)) * ya_v + _sigmoid(gb_ref[...].astype(f32)) * yb_v
        mix_ref[...] = mix.astype(bf16)

    return pl.pallas_call(
        body, name="mix_fwd", grid=(t // tb,),
        in_specs=[_row(tb, D), _row(tb, D), _row(tb, D, _GA_COL), _row(tb, D, _GB_COL)],
        out_specs=_row(tb, D), out_shape=jax.ShapeDtypeStruct((t, D), bf16), compiler_params=_params(1),
    )(ya, yb, pg, pg)


def _mix_bwd(dmix, ya, yb, pg):
    t = ya.shape[0]
    tb = _pick(t, (256, 128))

    def body(dm_ref, ya_ref, yb_ref, ga_ref, gb_ref, dya_ref, dyb_ref, dg_ref):
        dm = dm_ref[...].astype(f32)
        sa = _sigmoid(ga_ref[...].astype(f32))
        sb = _sigmoid(gb_ref[...].astype(f32))
        dya_ref[...] = (dm * sa).astype(bf16)
        dyb_ref[...] = (dm * sb).astype(bf16)
        dg_ref[:, :D] = (dm * ya_ref[...].astype(f32) * sa * (1.0 - sa)).astype(bf16)
        dg_ref[:, D:] = (dm * yb_ref[...].astype(f32) * sb * (1.0 - sb)).astype(bf16)

    return pl.pallas_call(
        body, name="mix_bwd", grid=(t // tb,),
        in_specs=[_row(tb, D), _row(tb, D), _row(tb, D), _row(tb, D, _GA_COL), _row(tb, D, _GB_COL)],
        out_specs=[_row(tb, D), _row(tb, D), _row(tb, 2 * D)],
        out_shape=[jax.ShapeDtypeStruct((t, D), bf16), jax.ShapeDtypeStruct((t, D), bf16),
                   jax.ShapeDtypeStruct((t, 2 * D), bf16)],
        compiler_params=_params(1),
    )(dmix, ya, yb, pg, pg)


def _ffn_fwd(up, wf):
    t = up.shape[0]
    tb = 128

    def body(up_ref, uph_ref, wf_ref, act_ref):
        first = pl.program_id(0) == 0
        for i in range(DFF // 128):
            g, v = _strip(i), _strip(i, DFF)
            gate = _strip_conv(wf_ref, g, _strip_taps(_f32(up_ref, g), _halo_before(uph_ref, g), first, 3))
            val = _strip_conv(wf_ref, v, _strip_taps(_f32(up_ref, v), _halo_before(uph_ref, v), first, 3))
            act_ref[:, g] = (gate * _sigmoid(gate) * val).astype(bf16)

    return pl.pallas_call(
        body, name="ffn_fwd", grid=(t // tb,),
        in_specs=[_row(tb, 2 * DFF), _prev(tb, 2 * DFF, rows=16), _fixed((8, 2 * DFF))],
        out_specs=_row(tb, DFF), out_shape=jax.ShapeDtypeStruct((t, DFF), bf16), compiler_params=_params(1),
    )(up, up, wf)


def _ffn_bwd1(dact, up, wf):
    t = up.shape[0]
    tb = 128

    def body(da_ref, up_ref, uph_ref, wf_ref, dc_ref, dw_ref):
        @pl.when(pl.program_id(0) == 0)
        def _():
            dw_ref[...] = jnp.zeros_like(dw_ref)

        first = pl.program_id(0) == 0
        for i in range(DFF // 128):
            g, v = _strip(i), _strip(i, DFF)
            g_taps = _strip_taps(_f32(up_ref, g), _halo_before(uph_ref, g), first, 3)
            v_taps = _strip_taps(_f32(up_ref, v), _halo_before(uph_ref, v), first, 3)
            gate = _strip_conv(wf_ref, g, g_taps)
            val = _strip_conv(wf_ref, v, v_taps)
            sg = _sigmoid(gate)
            da = _f32(da_ref, g)
            dgate = da * val * (sg * (1.0 + gate * (1.0 - sg)))
            dval = da * (gate * sg)
            dc_ref[:, g] = dgate.astype(bf16)
            dc_ref[:, v] = dval.astype(bf16)
            _strip_weight_grad(dw_ref, g, dgate, g_taps)
            _strip_weight_grad(dw_ref, v, dval, v_taps)

    return pl.pallas_call(
        body, name="ffn_bwd1", grid=(t // tb,),
        in_specs=[_row(tb, DFF), _row(tb, 2 * DFF), _prev(tb, 2 * DFF, rows=16), _fixed((8, 2 * DFF))],
        out_specs=[_row(tb, 2 * DFF), _fixed((8, 2 * DFF))],
        out_shape=[jax.ShapeDtypeStruct((t, 2 * DFF), bf16), jax.ShapeDtypeStruct((8, 2 * DFF), f32)],
        compiler_params=_params(1),
    )(dact, up, up, wf)


def _ffn_bwd2(dc, wf):
    t = dc.shape[0]
    tb = 128
    nb = t // tb

    def body(dc_ref, dch_ref, wf_ref, dup_ref):
        last = pl.program_id(0) == nb - 1
        for i in range(2 * DFF // 128):
            sl = _strip(i)
            dup_ref[:, sl] = _strip_conv_up(_f32(dc_ref, sl), _halo_after(dch_ref, sl), last, wf_ref, sl, 3).astype(bf16)

    return pl.pallas_call(
        body, name="ffn_bwd2", grid=(nb,),
        in_specs=[_row(tb, 2 * DFF), _next(tb, 2 * DFF, t, rows=16), _fixed((8, 2 * DFF))],
        out_specs=_row(tb, 2 * DFF), out_shape=jax.ShapeDtypeStruct((t, 2 * DFF), bf16), compiler_params=_params(1),
    )(dc, dc, wf)


def _final(x3, tgt, g):
    t = x3.shape[0]
    tb = _pick(t, (256, 128))

    def body(x_ref, t_ref, g_ref, loss_ref, dx_ref, dxb_ref, dg_ref):
        @pl.when(pl.program_id(0) == 0)
        def _():
            loss_ref[...] = jnp.zeros_like(loss_ref)
            dg_ref[...] = jnp.zeros_like(dg_ref)

        xv = x_ref[...]
        r = lax.rsqrt(jnp.mean(xv * xv, axis=-1, keepdims=True) + EPS)
        xh = xv * r
        gv = g_ref[...]
        e = xh * gv - t_ref[...]
        lrow = 0.5 * jnp.mean(e * e, axis=-1, keepdims=True)
        loss_ref[...] += jnp.sum(jnp.broadcast_to(lrow, (tb, 128)).reshape(tb // 8, 8, 128), axis=0)
        dy = e * (1.0 / D)
        dyg = dy * gv
        dx = r * (dyg - xh * jnp.mean(dyg * xh, axis=-1, keepdims=True))
        dx_ref[...] = dx
        dxb_ref[...] = dx.astype(bf16)
        dg_ref[...] += jnp.sum((dy * xh).reshape(tb // 8, 8, D), axis=0)

    return pl.pallas_call(
        body, name="final", grid=(t // tb,), in_specs=[_row(tb, D), _row(tb, D), _fixed((1, D))],
        out_specs=[_fixed((8, 128)), _row(tb, D), _row(tb, D), _fixed((8, D))],
        out_shape=[jax.ShapeDtypeStruct((8, 128), f32), jax.ShapeDtypeStruct((t, D), f32),
                   jax.ShapeDtypeStruct((t, D), bf16), jax.ShapeDtypeStruct((8, D), f32)],
        compiler_params=_params(1),
    )(x3, tgt, g)


def _pre_bwd1(pg, pq, p2, dya_in, dqn, dkn, dvc, dgb, gbeta, wa, wg, alog, dtb):
    t = pg.shape[0]
    tb = 128

    def body(p0_ref, p0h_ref, pq_ref, pqh_ref, p2_ref, dya_ref, dqn_ref, dkn_ref, dvc_ref, dgb_ref, gb_ref,
             wa_ref, wg_ref, alog_ref, dtb_ref,
             dbg_ref, dca_ref, dc4_ref, dp2_ref, dwa_ref, dwg_ref, dal_ref, ddt_ref):
        @pl.when(pl.program_id(0) == 0)
        def _():
            dwa_ref[...] = jnp.zeros_like(dwa_ref)
            dwg_ref[...] = jnp.zeros_like(dwg_ref)
            dal_ref[...] = jnp.zeros_like(dal_ref)
            ddt_ref[...] = jnp.zeros_like(ddt_ref)

        first = pl.program_id(0) == 0

        for i in range(D // 128):
            sl, cg, xv = _strip(i), _strip(i, D), _strip(i, 2 * D)
            taps = _strip_taps(_f32(p0_ref, cg) * _f32(p0_ref, xv), _halo_before(p0h_ref, cg) * _halo_before(p0h_ref, xv),
                               first, 3)
            dya = _f32(dya_ref, sl)
            dbg_ref[:, sl] = (dya * _strip_conv(wa_ref, sl, taps)).astype(bf16)
            dca = dya * _f32(p0_ref, sl)
            dca_ref[:, sl] = dca.astype(bf16)
            _strip_weight_grad(dwa_ref, sl, dca, taps)

        for part, d_ref, scale in ((0, dqn_ref, DH ** -0.5), (1, dkn_ref, 1.0), (2, dvc_ref, None)):
            for h in range(H):
                sl = _strip(h, part * D)
                taps = _strip_taps(pq_ref[:, sl], pqh_ref[:, sl], first, 4)
                c4 = _strip_conv(wg_ref, sl, taps)
                sg = _sigmoid(c4)
                dn = d_ref[:, _strip(h)]
                if scale is not None:
                    a = c4 * sg
                    r = lax.rsqrt(jnp.sum(a * a, axis=-1, keepdims=True) + EPS)
                    an = a * r
                    dn = dn * scale
                    dn = r * (dn - an * jnp.sum(dn * an, axis=-1, keepdims=True))
                dc4 = dn * (sg * (1.0 + c4 * (1.0 - sg)))
                dc4_ref[:, sl] = dc4.astype(bf16)
                _strip_weight_grad(dwg_ref, sl, dc4, taps)

        ab = p2_ref[...]
        lane = lax.broadcasted_iota(jnp.int32, ab.shape, 1)
        dgbv = dgb_ref[...]
        gbv = gb_ref[...]
        da = dgbv * (-jnp.exp(alog_ref[...])) * _sigmoid(ab + dtb_ref[...])
        db = dgbv * gbv * (1.0 - gbv)
        dp2_ref[...] = jnp.where(lane < H, da, jnp.where(lane < 2 * H, db, 0.0)).astype(bf16)
        dal = jnp.where(lane < H, dgbv * gbv, 0.0)
        ddt = jnp.where(lane < H, da, 0.0)
        dal_ref[...] += jnp.sum(dal.reshape(tb // 8, 8, 128), axis=0)
        ddt_ref[...] += jnp.sum(ddt.reshape(tb // 8, 8, 128), axis=0)

    return pl.pallas_call(
        body, name="pre_bwd1", grid=(t // tb,),
        in_specs=[_row(tb, 3 * D, 0), _prev(tb, 3 * D, 0, rows=16), _row(tb, 3 * D), _prev(tb, 3 * D), _row(tb, 128),
                  _row(tb, D), _row(tb, D), _row(tb, D), _row(tb, D), _row(tb, 128), _row(tb, 128),
                  _fixed((8, D)), _fixed((8, 3 * D)), _fixed((1, 128)), _fixed((1, 128))],
        out_specs=[_row(tb, D), _row(tb, D), _row(tb, 3 * D), _row(tb, 128),
                   _fixed((8, D)), _fixed((8, 3 * D)), _fixed((8, 128)), _fixed((8, 128))],
        out_shape=[jax.ShapeDtypeStruct((t, D), bf16), jax.ShapeDtypeStruct((t, D), bf16),
                   jax.ShapeDtypeStruct((t, 3 * D), bf16), jax.ShapeDtypeStruct((t, 128), bf16),
                   jax.ShapeDtypeStruct((8, D), f32), jax.ShapeDtypeStruct((8, 3 * D), f32),
                   jax.ShapeDtypeStruct((8, 128), f32), jax.ShapeDtypeStruct((8, 128), f32)],
        compiler_params=_params(1),
    )(pg, pg, pq, pq, p2, dya_in, dqn, dkn, dvc, dgb, gbeta, wa, wg, alog, dtb)


def _pre_bwd2(dca, dc4, pg, dbg, dz, dgates, wa, wg, exchange=None):
    t = pg.shape[0]
    tb = 128
    nb = t // tb

    def body(dca_ref, dcah_ref, dc4_ref, dc4h_ref, p0_ref, dbg_ref, dz_ref, dgt_ref, wa_ref, wg_ref, dp_ref):
        last = pl.program_id(0) == nb - 1
        dp_ref[:, :D] = dbg_ref[...]
        for i in range(D // 128):
            sl, cg, xv = _strip(i), _strip(i, D), _strip(i, 2 * D)
            du = _strip_conv_up(_f32(dca_ref, sl), _halo_after(dcah_ref, sl), last, wa_ref, sl, 3)
            dp_ref[:, cg] = (du * _f32(p0_ref, xv)).astype(bf16)
            dp_ref[:, xv] = (du * _f32(p0_ref, cg)).astype(bf16)
        dp_ref[:, 3 * D:4 * D] = dz_ref[...]
        dp_ref[:, 4 * D:6 * D] = dgt_ref[...]
        for i in range(3 * D // 128):
            sl = _strip(i)
            dq = _strip_conv_up(_f32(dc4_ref, sl), _halo_after(dc4h_ref, sl), last, wg_ref, sl, 4)
            dp_ref[:, _strip(i, 6 * D)] = dq.astype(bf16)

    return _call_with_exchange(
        body, exchange, name="pre_bwd2", grid=(nb,),
        in_specs=[_row(tb, D), _next(tb, D, t, rows=16), _row(tb, 3 * D), _next(tb, 3 * D, t, rows=16), _row(tb, 3 * D, 0),
                  _row(tb, D), _row(tb, D), _row(tb, 2 * D), _fixed((8, D)), _fixed((8, 3 * D))],
        out_specs=_row(tb, NW1), out_shape=jax.ShapeDtypeStruct((t, NW1), bf16),
        args=[dca, dca, dc4, dc4, pg, dbg, dz, dgates, wa, wg])


def _chunk_consts():
    r = lax.broadcasted_iota(jnp.int32, (CH, CH), 0)
    c = lax.broadcasted_iota(jnp.int32, (CH, CH), 1)
    return r, c, (r == c).astype(f32)


def _tri_inverse(lows, eye, r, c):
    def same_block(b):
        return jnp.bitwise_xor(r, c) < b

    xs = [jnp.where(same_block(8), -low, 0.0) for low in lows]
    ts = [eye + x for x in xs]
    for _ in range(2):
        xs = [_idot(x, x) for x in xs]
        ts = [t + _idot(t, x) for t, x in zip(ts, xs)]
    for b in (8, 16, 32):
        below = same_block(2 * b) & jnp.logical_not(same_block(b))
        ts = [t - _idot(_idot(t, jnp.where(below, low, 0.0)), t) for t, low in zip(ts, lows)]
    return ts


def _chunk_common(q, k, v, gcol, bcol, r, c, eye):
    grow = jnp.sum(eye * gcol, axis=0, keepdims=True)
    dec = jnp.exp(jnp.where(r >= c, gcol - grow, -jnp.inf))
    rcol = lax.broadcasted_iota(jnp.int32, (CH, 1), 0)
    glast = jnp.sum(jnp.where(rcol == CH - 1, gcol, 0.0), axis=0, keepdims=True)
    eg = jnp.exp(gcol)
    el = jnp.exp(glast - gcol)
    kb = k * bcol
    vb = v * bcol
    kk = _bdot_nt(kb, k)
    low = jnp.where(r > c, kk * dec, 0.0)
    qk = _bdot_nt(q, k)
    att = qk * dec
    return grow, dec, glast, eg, el, kb, vb, kk, low, qk, att, rcol


def _gdn_fwd(qn, kn, vc, gbeta):
    t = qn.shape[0]
    n_chunks = t // CH

    def body(q_ref, k_ref, v_ref, gb_ref, o_ref, s_ref, t_ref, state):
        @pl.when(pl.program_id(0) == 0)
        def _():
            state[...] = jnp.zeros_like(state)

        r, c, eye = _chunk_consts()
        tri = (r >= c).astype(f32)
        heads = range(H)
        keys = [(s, h) for s in range(GDN_STEP) for h in heads]
        rows = [slice(s * CH, (s + 1) * CH) for s in range(GDN_STEP)]
        gbs = [gb_ref[rows[s], :] for s in range(GDN_STEP)]
        galls = [_hdot(tri, gb) for gb in gbs]
        qs = {(s, h): q_ref[rows[s], h * DH:(h + 1) * DH] for s, h in keys}
        ks = {(s, h): k_ref[rows[s], h * DH:(h + 1) * DH] for s, h in keys}
        cm = {(s, h): _chunk_common(qs[s, h], ks[s, h], v_ref[rows[s], h * DH:(h + 1) * DH], galls[s][:, h:h + 1],
                                    gbs[s][:, H + h:H + h + 1], r, c, eye) for s, h in keys}
        invs = dict(zip(keys, _tri_inverse([cm[key][8] for key in keys], eye, r, c)))
        uws = {key: _bdot(invs[key], jnp.concatenate([cm[key][6], cm[key][5] * cm[key][3]], axis=1)) for key in keys}
        sts = [state[h] for h in heads]
        for s in range(GDN_STEP):
            vns = [uws[s, h][:, :DH] - _bdot(uws[s, h][:, DH:], sts[h]) for h in heads]
            outs = [_bdot(qs[s, h] * cm[s, h][3], sts[h]) + _bdot(cm[s, h][10], vns[h]) for h in heads]
            news = [sts[h] * jnp.exp(cm[s, h][2]) + _bdot_tn(ks[s, h] * cm[s, h][4], vns[h]) for h in heads]
            for h in heads:
                s_ref[s, h] = sts[h].astype(bf16)
                t_ref[s, h] = invs[s, h]
                o_ref[rows[s], h * DH:(h + 1) * DH] = outs[h]
            sts = news
        for h in heads:
            state[h] = sts[h]

    tb = GDN_STEP * CH
    return pl.pallas_call(
        body, name="gdn_fwd", grid=(t // tb,),
        in_specs=[_row(tb, D), _row(tb, D), _row(tb, D), _row(tb, 128)],
        out_specs=[_row(tb, D), pl.BlockSpec((GDN_STEP, H, DH, DH), lambda i: (i, 0, 0, 0)),
                   pl.BlockSpec((GDN_STEP, H, CH, CH), lambda i: (i, 0, 0, 0))],
        out_shape=[jax.ShapeDtypeStruct((t, D), f32), jax.ShapeDtypeStruct((n_chunks, H, DH, DH), bf16),
                   jax.ShapeDtypeStruct((n_chunks, H, CH, CH), f32)],
        scratch_shapes=[pltpu.VMEM((H, DH, DH), f32)],
        compiler_params=_params(1),
    )(qn, kn, vc, gbeta)


def _gdn_bwd(qn, kn, vc, gbeta, do, s_all, t_all):
    t = qn.shape[0]

    def body(q_ref, k_ref, v_ref, gb_ref, do_ref, s_ref, t_ref, dq_ref, dk_ref, dv_ref, dgb_ref, dstate):
        @pl.when(pl.program_id(0) == 0)
        def _():
            dstate[...] = jnp.zeros_like(dstate)

        r, c, eye = _chunk_consts()
        tril = r >= c
        lane = lax.broadcasted_iota(jnp.int32, (1, 128), 1)
        hs = range(H)

        def each(fn, *lists):
            return [fn(*args) for args in zip(*lists)]

        def rsum(a):
            return jnp.sum(a, axis=1, keepdims=True)

        def before_state(s):
            rows = slice(s * CH, (s + 1) * CH)
            gb = gb_ref[rows, :]
            gall = _hdot(tril.astype(f32), gb)
            p = {"rows": rows}
            p["q"] = q = [q_ref[rows, h * DH:(h + 1) * DH] for h in hs]
            p["k"] = k = [k_ref[rows, h * DH:(h + 1) * DH] for h in hs]
            p["v"] = v = [v_ref[rows, h * DH:(h + 1) * DH] for h in hs]
            p["dout"] = dout = [do_ref[rows, h * DH:(h + 1) * DH] for h in hs]
            p["inv"] = inv = [t_ref[s, h] for h in hs]
            p["st"] = st = [s_ref[s, h] for h in hs]
            p["bcol"] = bcol = [gb[:, H + h:H + h + 1] for h in hs]
            cm = [_chunk_common(q[h], k[h], v[h], gall[:, h:h + 1], bcol[h], r, c, eye) for h in hs]
            for name, i in (("dec", 1), ("glast", 2), ("eg", 3), ("el", 4), ("kb", 5), ("vb", 6), ("low", 8), ("att", 10)):
                p[name] = [m[i] for m in cm]
            p["rcol"] = cm[0][11]
            p["elast"] = each(jnp.exp, p["glast"])
            p["kbg"] = each(jnp.multiply, p["kb"], p["eg"])
            uw = each(lambda i, a, b: _bdot(i, jnp.concatenate([a, b], axis=1)), inv, p["vb"], p["kbg"])
            p["u"] = [a[:, :DH] for a in uw]
            p["w"] = [a[:, DH:] for a in uw]
            p["vn"] = each(lambda a, b, x: a - _bdot(b, x), p["u"], p["w"], st)
            p["qd"] = each(jnp.multiply, q, p["eg"])
            p["kd"] = each(jnp.multiply, k, p["el"])
            p["dqd"] = each(_bdot_nt, dout, st)
            p["datt"] = each(lambda d, x: jnp.where(tril, _bdot_nt(d, x), 0.0), dout, p["vn"])
            p["dqk"] = each(jnp.multiply, p["datt"], p["dec"])
            p["qd_do"] = each(_bdot_tn, p["qd"], dout)
            p["att_do"] = each(_bdot_tn, p["att"], dout)
            return p

        def after_state(p, ds):
            q, k, v, st, inv, bcol = p["q"], p["k"], p["v"], p["st"], p["inv"], p["bcol"]
            eg, el, kb, u, w = p["eg"], p["el"], p["kb"], p["u"], p["w"]
            dvn = each(lambda a, kk, x: a + _bdot(kk, x), p["att_do"], p["kd"], ds)
            dkd = each(_bdot_nt, p["vn"], ds)
            dw = each(lambda a, x: -_bdot_nt(a, x), dvn, st)
            new_ds = each(lambda x, e, a, ww, dv_: x * e + a - _bdot_tn(ww, dv_), ds, p["elast"], p["qd_do"], w, dvn)
            dglast = each(lambda e, x, d: e * jnp.sum(rsum(x.astype(f32) * d), axis=0, keepdims=True), p["elast"], st, ds)
            dr = each(lambda i, a, b: _bdot_tn(i, jnp.concatenate([a, b], axis=1)), inv, dvn, dw)
            dvb = [a[:, :DH] for a in dr]
            dkbg = [a[:, DH:] for a in dr]
            dlow = each(lambda a, b, x, y: -jnp.where(r > c, _bdot_nt(a, b) + _bdot_nt(x, y), 0.0), dvb, u, dkbg, w)
            dkk = each(jnp.multiply, dlow, p["dec"])
            mm = each(lambda a, b, x, y: a * b + x * y, dlow, p["low"], p["datt"], p["att"])
            dkb = each(lambda a, kk, b, e: _bdot(a, kk) + b * e, dkk, k, dkbg, eg)
            dk = each(lambda a, b, x, y, d, e, f, g: _bdot_tn(a, b) + _bdot_tn(x, y) + d * e + f * g,
                      dkk, kb, p["dqk"], q, dkd, el, dkb, bcol)
            dq = each(lambda a, kk, d, e: _bdot(a, kk) + d * e, p["dqk"], k, p["dqd"], eg)
            dv = each(jnp.multiply, dvb, bcol)
            dbeta = each(lambda a, b, x, y: rsum(a * b) + rsum(x * y), dkb, k, dvb, v)
            deg = each(lambda a, b, x, y: rsum(a * b) + rsum(x * y), dkbg, kb, p["dqd"], q)
            delc = each(lambda a, b, e: rsum(a * b) * e, dkd, k, el)
            dgc = each(lambda m, a, e, d: rsum(m) - rsum(eye * jnp.sum(m, axis=0, keepdims=True)) + a * e - d,
                       mm, deg, eg, delc)
            dgc = each(lambda g, d, l: g + jnp.where(p["rcol"] == CH - 1, jnp.sum(d, axis=0, keepdims=True) + l, 0.0),
                       dgc, delc, dglast)
            dg_acc = jnp.zeros((CH, 128), f32)
            db_acc = jnp.zeros((CH, 128), f32)
            rows = p["rows"]
            for h in hs:
                dq_ref[rows, h * DH:(h + 1) * DH] = dq[h]
                dk_ref[rows, h * DH:(h + 1) * DH] = dk[h]
                dv_ref[rows, h * DH:(h + 1) * DH] = dv[h]
                dg_acc = dg_acc + dgc[h] * (lane == h).astype(f32)
                db_acc = db_acc + dbeta[h] * (lane == H + h).astype(f32)
            dgb_ref[rows, :] = _hdot((r <= c).astype(f32), dg_acc) + db_acc
            return new_ds

        order = list(reversed(range(GDN_STEP)))
        pre = [before_state(s) for s in order]
        ds = [dstate[h] for h in hs]
        for p in pre:
            ds = after_state(p, ds)
        for h in hs:
            dstate[h] = ds[h]

    tb = GDN_STEP * CH
    n_steps = t // tb
    rev = lambda i: (n_steps - 1 - i, 0)
    rev4 = lambda i: (n_steps - 1 - i, 0, 0, 0)
    return pl.pallas_call(
        body, name="gdn_bwd", grid=(n_steps,),
        in_specs=[pl.BlockSpec((tb, D), rev), pl.BlockSpec((tb, D), rev), pl.BlockSpec((tb, D), rev),
                  pl.BlockSpec((tb, 128), rev), pl.BlockSpec((tb, D), rev),
                  pl.BlockSpec((GDN_STEP, H, DH, DH), rev4), pl.BlockSpec((GDN_STEP, H, CH, CH), rev4)],
        out_specs=[pl.BlockSpec((tb, D), rev), pl.BlockSpec((tb, D), rev), pl.BlockSpec((tb, D), rev),
                   pl.BlockSpec((tb, 128), rev)],
        out_shape=[jax.ShapeDtypeStruct((t, D), f32)] * 3 + [jax.ShapeDtypeStruct((t, 128), f32)],
        scratch_shapes=[pltpu.VMEM((H, DH, DH), f32)],
        compiler_params=_params(1),
    )(qn, kn, vc, gbeta, do, s_all, t_all)


def _pad_rows(w, rows=8):
    return jnp.pad(w, ((0, rows - w.shape[0]), (0, 0)))


_REST = ("w_up", "w_a_out", "w_b_out", "w_o", "w_down")


def _local_step(x, tgt, w, comm=None):
    g1 = w["norm_mix_g"].reshape(1, D)
    if comm is None:
        h1 = _rms_fwd(x, g1, name="rms1_fwd")
    else:
        h1, gathered = _rms_fwd(x, g1, name="rms1_fwd", exchange=comm.gather_first())
        w = {**w, **comm.finish_first(gathered)}
    w1, w2 = w["w1"], w["w2"]
    wa = _pad_rows(w["conv_a_w"])
    wg = _pad_rows(w["gdn_conv_w"])
    wf = _pad_rows(w["ffn_conv_w"])
    alog = jnp.pad(w["gdn_A_log"].reshape(1, H), ((0, 0), (0, 128 - H)))
    dtb = jnp.pad(w["gdn_dt_bias"].reshape(1, H), ((0, 0), (0, 128 - H)))
    g2 = w["norm_ffn_g"].reshape(1, D)
    g3 = w["norm_final_g"].reshape(1, D)
    gn = w["gdn_norm_g"].reshape(1, DH)

    if comm is None:
        pg = _matmul(h1, w1, name="mm_in", cols=(0, 6 * D), out_dtype=bf16)
    else:
        pg, gathered = _matmul(h1, w1, name="mm_in", cols=(0, 6 * D), out_dtype=bf16, exchange=comm.gather_rest())
        w = {**w, **comm.finish_gather(gathered)}
    pq = _matmul(h1, w1, name="mm_in_qkv", cols=(6 * D, 3 * D))
    p2 = _matmul(h1, w2, name="mm_in_ab")
    ya_in, qn, kn, vc, gbeta = _pre_fwd(pg, pq, p2, wa, wg, alog, dtb)
    o, s_all, t_all = _gdn_fwd(qn, kn, vc, gbeta)
    yb_in = _post_fwd(o, pg, gn)
    ya = _matmul(ya_in, w["w_a_out"], name="mm_a", out_dtype=bf16)
    yb = _matmul(yb_in, w["w_b_out"], name="mm_b", out_dtype=bf16)
    mix = _mix_fwd(ya, yb, pg)
    x2 = _matmul(mix, w["w_o"], name="mm_o", add=x)
    h2 = _rms_fwd(x2, g2, name="rms2_fwd")
    up = _matmul(h2, w["w_up"], nt=True, name="mm_up", tn=DFF // 2, out_dtype=bf16)
    act = _ffn_fwd(up, wf)
    x3 = _matmul(act, w["w_down"], name="mm_down", add=x2, tm=512)
    loss_p, dx3, dx3b, dg3 = _final(x3, tgt, g3)

    grads = {"norm_final_g": dg3}
    dact = _matmul(dx3b, w["w_down"], nt=True, name="mm_down_dx", tm=512, tn=DFF, out_dtype=bf16)
    grads["w_down"] = _matmul_tn(act, dx3b, name="mm_down_dw", tm=DFF // 2)
    dc, dwf = _ffn_bwd1(dact, up, wf)
    grads["ffn_conv_w"] = dwf
    dup = _ffn_bwd2(dc, wf)
    dh2 = _matmul(dup, w["w_up"], name="mm_up_dx", tk=DFF)
    grads["w_up"] = _matmul_tn(dup, h2, name="mm_up_dw", tm=DFF // 2)
    dx2, dx2b, dg2 = _rms_bwd(dh2, x2, g2, dx3, name="rms2_bwd")
    grads["norm_ffn_g"] = dg2
    dmix = _matmul(dx2b, w["w_o"], nt=True, name="mm_o_dx", out_dtype=bf16)
    grads["w_o"] = _matmul_tn(mix, dx2b, name="mm_o_dw")
    dya, dyb, dgates = _mix_bwd(dmix, ya, yb, pg)
    dya_in = _matmul(dya, w["w_a_out"], nt=True, name="mm_a_dx", out_dtype=bf16)
    grads["w_a_out"] = _matmul_tn(ya_in, dya, name="mm_a_dw")
    dyb_in = _matmul(dyb, w["w_b_out"], nt=True, name="mm_b_dx")
    grads["w_b_out"] = _matmul_tn(yb_in, dyb, name="mm_b_dw")
    do, dz, dgn = _post_bwd(dyb_in, o, pg, gn)
    grads["gdn_norm_g"] = dgn
    dqn, dkn, dvc, dgb = _gdn_bwd(qn, kn, vc, gbeta, do, s_all, t_all)
    dbg, dca, dc4, dp2, dwa, dwg, dal, ddt = _pre_bwd1(pg, pq, p2, dya_in, dqn, dkn, dvc, dgb, gbeta, wa, wg, alog, dtb)
    grads["conv_a_w"] = dwa
    grads["gdn_conv_w"] = dwg
    grads["gdn_A_log"] = dal
    grads["gdn_dt_bias"] = ddt
    grads["w2"] = _matmul_tn(h1, dp2, name="mm_in_ab_dw")
    if comm is None:
        dp1 = _pre_bwd2(dca, dc4, pg, dbg, dz, dgates, wa, wg)
        grads["w1"] = _matmul_tn(h1, dp1, name="mm_in_dw")
        dh1 = _matmul(dp1, w1, nt=True, name="mm_in_dx", tm=512, tk=NW1 // 2)
    else:
        exchange, blocks = comm.reduce_halves(_REST, grads)
        dp1, recv = _pre_bwd2(dca, dc4, pg, dbg, dz, dgates, wa, wg, exchange=exchange)
        exchange, sums = comm.reduce_sums(_REST, blocks, recv)
        grads["w1"], recv = _matmul_tn(h1, dp1, name="mm_in_dw", exchange=exchange)
        comm.finish_reduce(_REST, sums, recv)
        exchange, blocks = comm.reduce_halves(("w_in",), grads)
        exchange, sums = comm.reduce_sums(("w_in",), blocks, _run_exchange(exchange, name="rs_sibling_w_in"))
        dh1, recv = _matmul(dp1, w1, nt=True, name="mm_in_dx", tm=512, tk=NW1 // 2, exchange=exchange)
        comm.finish_reduce(("w_in",), sums, recv)
    dh1 = _matmul(dp2, w2, nt=True, name="mm_in_ab_dx", add=dh1)
    dx, _, dg1 = _rms_bwd(dh1, x, g1, dx2, name="rms1_bwd")
    grads["norm_mix_g"] = dg1
    return loss_p, dx, grads


_ANY = pl.BlockSpec(memory_space=pl.ANY)


def _remote(src, dst, send_sem, recv_sem, to):
    return pltpu.make_async_remote_copy(src_ref=src, dst_ref=dst, send_sem=send_sem, recv_sem=recv_sem,
                                        device_id=to, device_id_type=MESH)


def _run_exchange(exchange, *, name):
    arrays, shapes, sems, start, wait = exchange
    n_in, n_out = len(arrays), len(shapes)

    def body(*refs):
        start(refs[:n_in], refs[n_in:n_in + n_out], refs[n_in + n_out:])
        wait(refs[:n_in], refs[n_in:n_in + n_out], refs[n_in + n_out:])

    return pl.pallas_call(body, name=name, out_shape=list(shapes), in_specs=[_ANY] * n_in, out_specs=[_ANY] * n_out,
                          scratch_shapes=list(sems))(*arrays)


def _gather_exchange(shards):
    n = len(shards)

    def copies(x_refs, out_refs, sems):
        send_sems, recv_sems, local_sems = sems
        x, y, c = lax.axis_index("x"), lax.axis_index("y"), lax.axis_index("c")
        me, sibling = (x, y, c), (x, y, 1 - c)
        chips = [(1 - x, y), (x, 1 - y), (1 - x, 1 - y)]

        def copy(a, k, blk, to, from_input=False):
            dst = out_refs[a].at[4 * blk[0] + 2 * blk[1] + blk[2]]
            return _remote(x_refs[a] if from_input else dst, dst, send_sems.at[a, k], recv_sems.at[a, k], to)

        mine = [pltpu.make_async_copy(x_refs[a], out_refs[a].at[4 * x + 2 * y + c], local_sems.at[a]) for a in range(n)]
        first = []
        for a in range(n):
            first.append(copy(a, 0, me, sibling, from_input=True))
            first += [copy(a, 1 + j, me, (*chip, c), from_input=True) for j, chip in enumerate(chips)]
        return copy, mine, first, me, sibling, chips, c

    def start(x_refs, out_refs, sems):
        _, mine, first, *_ = copies(x_refs, out_refs, sems)
        for cp in mine + first:
            cp.start()

    def wait(x_refs, out_refs, sems):
        copy, mine, first, me, sibling, chips, c = copies(x_refs, out_refs, sems)
        passed = []
        for j, chip in enumerate(chips):
            for a in range(n):
                copy(a, 1 + j, (*chip, c), me).wait_recv()
                passed.append(copy(a, 4 + j, (*chip, c), sibling))
                passed[-1].start()
        for a in range(n):
            copy(a, 0, sibling, me).wait_recv()
            for j, chip in enumerate(chips):
                copy(a, 4 + j, (*chip, 1 - c), me).wait_recv()
        for cp in first + passed:
            cp.wait_send()
        for cp in mine:
            cp.wait()

    shapes = [jax.ShapeDtypeStruct((N_DEV, *s.shape), s.dtype) for s in shards]
    sems = [pltpu.SemaphoreType.DMA((n, 7)), pltpu.SemaphoreType.DMA((n, 7)), pltpu.SemaphoreType.DMA((n,))]
    return shards, shapes, sems, start, wait


def _gather_direct_exchange(shards):
    n = len(shards)

    def copies(x_refs, out_refs, sems):
        send_sems, recv_sems, local_sems = sems
        x, y, c = lax.axis_index("x"), lax.axis_index("y"), lax.axis_index("c")
        targets = [(x, y, 1 - c), (1 - x, y, c), (x, 1 - y, c), (1 - x, 1 - y, c)]
        local, sends, recvs = [], [], []
        for a in range(n):
            mine = out_refs[a].at[4 * x + 2 * y + c]
            local.append(pltpu.make_async_copy(x_refs[a], mine, local_sems.at[a]))
            for k, to in enumerate(targets):
                theirs = out_refs[a].at[4 * to[0] + 2 * to[1] + to[2]]
                sends.append(_remote(x_refs[a], mine, send_sems.at[a, k], recv_sems.at[a, k], to))
                recvs.append(_remote(theirs, theirs, send_sems.at[a, k], recv_sems.at[a, k], to))
        return local, sends, recvs

    def start(x_refs, out_refs, sems):
        local, sends, _ = copies(x_refs, out_refs, sems)
        for cp in local + sends:
            cp.start()

    def wait(x_refs, out_refs, sems):
        local, sends, recvs = copies(x_refs, out_refs, sems)
        for cp in recvs:
            cp.wait_recv()
        for cp in sends:
            cp.wait_send()
        for cp in local:
            cp.wait()

    shapes = [jax.ShapeDtypeStruct((N_DEV, *s.shape), s.dtype) for s in shards]
    sems = [pltpu.SemaphoreType.DMA((n, 4)), pltpu.SemaphoreType.DMA((n, 4)), pltpu.SemaphoreType.DMA((n,))]
    return shards, shapes, sems, start, wait


def _gather_forward(gathered):
    n = len(gathered)

    def body(*refs):
        out_refs = refs[n:2 * n]
        send_sems, recv_sems = refs[2 * n:]
        x, y, c = lax.axis_index("x"), lax.axis_index("y"), lax.axis_index("c")
        sibling = (x, y, 1 - c)
        sends, recvs = [], []
        for a in range(n):
            for j, (px, py) in enumerate([(1 - x, y), (x, 1 - y), (1 - x, 1 - y)]):
                mine = out_refs[a].at[4 * px + 2 * py + c]
                theirs = out_refs[a].at[4 * px + 2 * py + 1 - c]
                sends.append(_remote(mine, mine, send_sems.at[a, j], recv_sems.at[a, j], sibling))
                recvs.append(_remote(theirs, theirs, send_sems.at[a, j], recv_sems.at[a, j], sibling))
        for cp in sends:
            cp.start()
        for cp in recvs:
            cp.wait_recv()
        for cp in sends:
            cp.wait_send()

    return pl.pallas_call(
        body, name="ag_forward", out_shape=[jax.ShapeDtypeStruct(g.shape, g.dtype) for g in gathered],
        in_specs=[_ANY] * n, out_specs=[_ANY] * n, input_output_aliases={a: a for a in range(n)},
        scratch_shapes=[pltpu.SemaphoreType.DMA((n, 3)), pltpu.SemaphoreType.DMA((n, 3))],
    )(*gathered)


def _chips_exchange(hsums):
    n = len(hsums)

    def copies(h_refs, out_refs, sems):
        send_sems, recv_sems = sems
        x, y, c = lax.axis_index("x"), lax.axis_index("y"), lax.axis_index("c")
        chips = [(1 - x, y), (x, 1 - y), (1 - x, 1 - y)]
        return [_remote(h_refs[a].at[2 * px + py], out_refs[a].at[k], send_sems.at[a, k], recv_sems.at[a, k], (px, py, c))
                for a in range(n) for k, (px, py) in enumerate(chips)]

    def start(h_refs, out_refs, sems):
        for cp in copies(h_refs, out_refs, sems):
            cp.start()

    def wait(h_refs, out_refs, sems):
        for cp in copies(h_refs, out_refs, sems):
            cp.wait()

    shapes = [jax.ShapeDtypeStruct((3, *h.shape[1:]), h.dtype) for h in hsums]
    sems = [pltpu.SemaphoreType.DMA((n, 3)), pltpu.SemaphoreType.DMA((n, 3))]
    return hsums, shapes, sems, start, wait


def _sibling_exchange(halves):
    n = len(halves)

    def copies(p_refs, out_refs, sems):
        send_sems, recv_sems = sems
        x, y, c = lax.axis_index("x"), lax.axis_index("y"), lax.axis_index("c")
        return [_remote(p_refs[a], out_refs[a], send_sems.at[a], recv_sems.at[a], (x, y, 1 - c)) for a in range(n)]

    def start(p_refs, out_refs, sems):
        for cp in copies(p_refs, out_refs, sems):
            cp.start()

    def wait(p_refs, out_refs, sems):
        for cp in copies(p_refs, out_refs, sems):
            cp.wait()

    shapes = [jax.ShapeDtypeStruct(h.shape, h.dtype) for h in halves]
    return halves, shapes, [pltpu.SemaphoreType.DMA((n,)), pltpu.SemaphoreType.DMA((n,))], start, wait


_IN_RANGES = ((0, 3 * D, 0, 0), (3 * D, 6 * D, 0, 6 * D), (6 * D, 7 * D, 0, 3 * D), (7 * D, 7 * D + 16, 1, 0),
              (7 * D + 16, 9 * D + 16, 0, 4 * D))


def _col_pieces(width, ranges):
    pieces = []
    for d in range(N_DEV):
        lo, hi = d * width, (d + 1) * width
        for glo, ghi, mat, mlo in ranges:
            a, b = max(lo, glo), min(hi, ghi)
            if a < b:
                pieces.append((d, a - lo, b - lo, mat, mlo + a - glo))
    return pieces


def _cols_to_matrices(g, ranges, out_widths, *, name):
    _, rows, width = g.shape
    tb = 128
    pieces = _col_pieces(width, ranges)
    covered = [sum(p[2] - p[1] for p in pieces if p[3] == m) for m in range(len(out_widths))]

    def body(g_ref, *o_refs):
        for m, o_ref in enumerate(o_refs):
            if covered[m] < out_widths[m]:
                o_ref[...] = jnp.zeros_like(o_ref)
        for d, b0, b1, m, m0 in pieces:
            o_refs[m][:, m0:m0 + b1 - b0] = g_ref[d, :, b0:b1]

    return pl.pallas_call(
        body, name=name, grid=(rows // tb,), in_specs=[pl.BlockSpec((N_DEV, tb, width), lambda i: (0, i, 0))],
        out_specs=[pl.BlockSpec((tb, wo), lambda i: (i, 0)) for wo in out_widths],
        out_shape=[jax.ShapeDtypeStruct((rows, wo), g.dtype) for wo in out_widths], compiler_params=_params(1),
    )(g)


def _matrices_to_cols(mats, ranges, width, *, name):
    rows = mats[0].shape[0]
    tb = 128
    pieces = _col_pieces(width, ranges)

    def body(*refs):
        m_refs, g_ref = refs[:-1], refs[-1]
        for d, b0, b1, m, m0 in pieces:
            g_ref[d, :, b0:b1] = m_refs[m][:, m0:m0 + b1 - b0]

    return pl.pallas_call(
        body, name=name, grid=(rows // tb,),
        in_specs=[pl.BlockSpec((tb, mt.shape[1]), lambda i: (i, 0)) for mt in mats],
        out_specs=pl.BlockSpec((N_DEV, tb, width), lambda i: (0, i, 0)),
        out_shape=jax.ShapeDtypeStruct((N_DEV, rows, width), mats[0].dtype), compiler_params=_params(1),
    )(*mats)


def _row_block(rows):
    return 128 if rows % 128 == 0 else rows


def _half_bf16(g4, c_other, *, name):
    _, _, rows, width = g4.shape
    tb = _row_block(rows)

    def body(c_ref, p_ref, o_ref):
        o_ref[0] = p_ref[0, 0].astype(bf16)

    grid_spec = pltpu.PrefetchScalarGridSpec(
        num_scalar_prefetch=1, grid=(4, rows // tb),
        in_specs=[pl.BlockSpec((1, 1, tb, width), lambda j, i, c_ref: (j, c_ref[0], i, 0))],
        out_specs=pl.BlockSpec((1, tb, width), lambda j, i, c_ref: (j, i, 0)))
    return pl.pallas_call(
        body, name=name, grid_spec=grid_spec, out_shape=jax.ShapeDtypeStruct((4, rows, width), bf16),
        compiler_params=_params(2),
    )(c_other, g4)


def _pair_sum(g4, recv, c_me, *, name):
    _, _, rows, width = g4.shape
    tb = _row_block(rows)

    def body(c_ref, p_ref, r_ref, o_ref, ob_ref):
        s = p_ref[0, 0] + r_ref[0].astype(f32)
        o_ref[0] = s
        ob_ref[0] = s.astype(bf16)

    blk = pl.BlockSpec((1, tb, width), lambda j, i, c_ref: (j, i, 0))
    grid_spec = pltpu.PrefetchScalarGridSpec(
        num_scalar_prefetch=1, grid=(4, rows // tb),
        in_specs=[pl.BlockSpec((1, 1, tb, width), lambda j, i, c_ref: (j, c_ref[0], i, 0)), blk],
        out_specs=[blk, blk])
    return pl.pallas_call(
        body, name=name, grid_spec=grid_spec,
        out_shape=[jax.ShapeDtypeStruct((4, rows, width), f32), jax.ShapeDtypeStruct((4, rows, width), bf16)],
        compiler_params=_params(2),
    )(c_me, g4, recv)


def _adam_shard(hsum, recv, chip, w, m, v, *, name):
    _, rows, width = w.shape
    tb = _row_block(rows)

    def body(j_ref, h_ref, r_ref, w_ref, m_ref, v_ref, g_out, d_out, m_out, v_out):
        g = ((h_ref[0] + r_ref[0].astype(f32)) + r_ref[1].astype(f32)) + r_ref[2].astype(f32)
        delta, mn, vn = _adam_math(w_ref[0], g, m_ref[0], v_ref[0])
        g_out[0] = g
        d_out[0] = delta
        m_out[0] = mn
        v_out[0] = vn

    blk = pl.BlockSpec((1, tb, width), lambda i, j_ref: (0, i, 0))
    grid_spec = pltpu.PrefetchScalarGridSpec(
        num_scalar_prefetch=1, grid=(rows // tb,),
        in_specs=[pl.BlockSpec((1, tb, width), lambda i, j_ref: (j_ref[0], i, 0)),
                  pl.BlockSpec((3, tb, width), lambda i, j_ref: (0, i, 0)), blk, blk, blk],
        out_specs=[blk, blk, blk, blk])
    return pl.pallas_call(
        body, name=name, grid_spec=grid_spec, out_shape=[jax.ShapeDtypeStruct(w.shape, f32)] * 4,
        compiler_params=_params(1),
    )(chip, hsum, recv, w, m, v)


def _sum_shard(hsum, recv, chip, *, name):
    _, rows, width = hsum.shape
    tb = _row_block(rows)

    def body(j_ref, h_ref, r_ref, g_out):
        g_out[...] = ((h_ref[0] + r_ref[0].astype(f32)) + r_ref[1].astype(f32)) + r_ref[2].astype(f32)

    grid_spec = pltpu.PrefetchScalarGridSpec(
        num_scalar_prefetch=1, grid=(rows // tb,),
        in_specs=[pl.BlockSpec((1, tb, width), lambda i, j_ref: (j_ref[0], i, 0)),
                  pl.BlockSpec((3, tb, width), lambda i, j_ref: (0, i, 0))],
        out_specs=pl.BlockSpec((tb, width), lambda i, j_ref: (i, 0)))
    return pl.pallas_call(body, name=name, grid_spec=grid_spec, out_shape=jax.ShapeDtypeStruct((rows, width), f32),
                          compiler_params=_params(1))(chip, hsum, recv)


def _adam_columns(g, w, m, v, *, name):
    cols, _, rows = w.shape
    tb = cols // 2

    def body(g_ref, w_ref, m_ref, v_ref, d_out, m_out, v_out):
        delta, mn, vn = _adam_math(w_ref[...], g_ref[...], m_ref[...], v_ref[...])
        d_out[...] = delta
        m_out[...] = mn
        v_out[...] = vn

    blk = pl.BlockSpec((tb, 1, rows), lambda i: (i, 0, 0))
    return pl.pallas_call(
        body, name=name, grid=(cols // tb,), in_specs=[blk] * 4, out_specs=[blk] * 3,
        out_shape=[jax.ShapeDtypeStruct(w.shape, f32)] * 3, compiler_params=_params(1),
    )(g, w, m, v)


R_SMALL = 16 + 16 * N_DEV
_SMALL_LANES = {"gdn_norm_g": (0, DH), "gdn_A_log": (DH, DH + H), "gdn_dt_bias": (2 * DH, 2 * DH + H)}
_LOSS_LANE = 3 * DH


def _pack_small(dg1, dg2, dg3, dgn, dal, ddt, loss_p, dwa, dwg, dwf):
    def body(dg1_ref, dg2_ref, dg3_ref, dgn_ref, dal_ref, ddt_ref, loss_ref, dwa_ref, dwg_ref, dwf_ref, o_ref):
        def total(ref):
            return jnp.sum(ref[...], axis=0, keepdims=True)

        o_ref[...] = jnp.zeros_like(o_ref)
        o_ref[0:1, :] = total(dg1_ref)
        o_ref[1:2, :] = total(dg2_ref)
        o_ref[2:3, :] = total(dg3_ref)
        o_ref[3:4, 0:DH] = total(dgn_ref)
        o_ref[3:4, DH:2 * DH] = total(dal_ref)
        o_ref[3:4, 2 * DH:3 * DH] = total(ddt_ref)
        o_ref[3:4, 3 * DH:4 * DH] = total(loss_ref)
        for d in range(N_DEV):
            base = 16 + 16 * d
            o_ref[base:base + 3, 0:128] = dwa_ref[0:3, 128 * d:128 * (d + 1)]
            o_ref[base:base + 4, 128:512] = dwg_ref[0:4, 384 * d:384 * (d + 1)]
            o_ref[base + 8:base + 11, 0:704] = dwf_ref[0:3, 704 * d:704 * (d + 1)]

    return pl.pallas_call(body, name="pack_small", out_shape=jax.ShapeDtypeStruct((R_SMALL, D), f32))(
        dg1, dg2, dg3, dgn, dal, ddt, loss_p, dwa, dwg, dwf)


_SMALL = ("norm_mix_g", "norm_ffn_g", "norm_final_g", "gdn_norm_g", "gdn_A_log", "gdn_dt_bias",
          "conv_a_w", "gdn_conv_w", "ffn_conv_w")


def _adam_small(gath, me, w, m, v):
    arrays = [t[n] for n in _SMALL for t in (w, m, v)]

    def body(me_ref, ga_ref, gb_ref, *refs):
        ins, outs = refs[:len(arrays)], refs[len(arrays):]
        ga, gb = ga_ref[0], gb_ref[0]
        for s in range(1, N_DEV):
            ga = ga + ga_ref[s]
            gb = gb + gb_ref[s]
        grads = {"norm_mix_g": ga[0:1, :], "norm_ffn_g": ga[1:2, :], "norm_final_g": ga[2:3, :],
                 "conv_a_w": gb[0:3, 0:128], "gdn_conv_w": gb[0:4, 128:512], "ffn_conv_w": gb[8:11, 0:704]}
        for n, (lo, hi) in _SMALL_LANES.items():
            grads[n] = ga[3:4, lo:hi]
        for i, n in enumerate(_SMALL):
            three_d = len(w[n].shape) == 3
            wv, mv, vv = (r[0] if three_d else r[...] for r in ins[3 * i:3 * i + 3])
            delta, mn, vn = _adam_math(wv, grads[n], mv, vv)
            for o_ref, val in zip(outs[4 * i:4 * i + 4], (grads[n], delta, mn, vn)):
                if three_d:
                    o_ref[0] = val
                else:
                    o_ref[...] = val
        outs[-1][...] = ga[3:4, _LOSS_LANE:_LOSS_LANE + 1]

    def whole(shape):
        return pl.BlockSpec(shape, lambda i, me_ref: (0,) * len(shape))

    grid_spec = pltpu.PrefetchScalarGridSpec(
        num_scalar_prefetch=1, grid=(1,),
        in_specs=[pl.BlockSpec((N_DEV, 16, D), lambda i, me_ref: (0, 0, 0)),
                  pl.BlockSpec((N_DEV, 16, D), lambda i, me_ref: (0, 1 + me_ref[0], 0))] + [whole(a.shape) for a in arrays],
        out_specs=[whole(w[n].shape) for n in _SMALL for _ in range(4)] + [whole((1, 1))])
    res = pl.pallas_call(
        body, name="adam_small", grid_spec=grid_spec,
        out_shape=[jax.ShapeDtypeStruct(w[n].shape, f32) for n in _SMALL for _ in range(4)]
        + [jax.ShapeDtypeStruct((1, 1), f32)],
        compiler_params=_params(1),
    )(me, gath, gath, *arrays)
    return {n: tuple(res[4 * i:4 * i + 4]) for i, n in enumerate(_SMALL)}, res[-1]


def _adam_math(w, g, m, v):
    m = ADAM_B1 * m + (1.0 - ADAM_B1) * g
    v = ADAM_B2 * v + (1.0 - ADAM_B2) * jnp.square(g)
    m_hat = m / (1.0 - ADAM_B1 ** ADAM_STEP)
    v_hat = v / (1.0 - ADAM_B2 ** ADAM_STEP)
    delta = -ADAM_LR * (m_hat / (jnp.sqrt(v_hat) + ADAM_EPS) + ADAM_WD * w)
    return delta, m, v


_WEIGHTS = ("norm_mix_g", "w_in", "conv_a_w", "gdn_conv_w", "gdn_A_log", "gdn_dt_bias", "gdn_norm_g", "w_a_out",
            "w_b_out", "w_o", "norm_ffn_g", "w_up", "ffn_conv_w", "w_down", "norm_final_g")
_BIG = ("w_in",) + _REST
_CONVS = ("conv_a_w", "gdn_conv_w", "ffn_conv_w")


class _StepExchanges:
    def __init__(self, wts, mom, var, c_me, chip):
        self.wts, self.mom, self.var, self.c_me, self.chip = wts, mom, var, c_me, chip
        self.results = {}

    def gather_first(self):
        return _gather_exchange([self.wts["w_in"][0].astype(bf16)] + [self.wts[n][0] for n in _CONVS])

    def finish_first(self, gathered):
        g_in, gc_a, gc_g, gc_f = gathered
        w1, w2 = _cols_to_matrices(g_in, _IN_RANGES, (NW1, 128), name="relay_w_in")
        return {"w1": w1, "w2": w2, "conv_a_w": gc_a.transpose(1, 0, 2).reshape(3, D),
                "gdn_conv_w": gc_g.transpose(1, 0, 2).reshape(4, 3 * D),
                "ffn_conv_w": gc_f.transpose(1, 0, 2).reshape(3, 2 * DFF)}

    def gather_rest(self):
        return _gather_direct_exchange([self.wts[n][0].astype(bf16) for n in _REST])

    def finish_gather(self, gathered):
        g_up, g_a, g_b, g_o, g_down = _gather_forward(gathered)
        return {"w_up": g_up.reshape(2 * DFF, D), "w_a_out": g_a.reshape(D, D), "w_b_out": g_b.reshape(D, D),
                "w_o": g_o.reshape(D, D), "w_down": g_down.reshape(DFF, D)}

    def reduce_halves(self, names, grads):
        blocks = []
        for n in names:
            if n == "w_in":
                g = _matrices_to_cols([grads["w1"], grads["w2"]], _IN_RANGES, R_IN, name="relay_dw_in")
            else:
                g = grads[n]
            blocks.append(g.reshape(4, 2, *self.wts[n].shape[1:]))
        return _sibling_exchange([_half_bf16(g, 1 - self.c_me, name="rs_half_" + n) for n, g in zip(names, blocks)]), blocks

    def reduce_sums(self, names, blocks, recv):
        sums = [_pair_sum(g, r, self.c_me, name="rs_sum_" + n) for n, g, r in zip(names, blocks, recv)]
        return _chips_exchange([s[1] for s in sums]), [s[0] for s in sums]

    def finish_reduce(self, names, sums, recv):
        for n, s, r in zip(names, sums, recv):
            if n == "w_in":
                g = jnp.transpose(_sum_shard(s, r, self.chip, name="rs_total_w_in"))[:, None, :]
                w, m, v = (jnp.transpose(t[n], (2, 0, 1)) for t in (self.wts, self.mom, self.var))
                res = (g, *_adam_columns(g, w, m, v, name="adam_w_in"))
                self.results[n] = tuple(jnp.transpose(a, (1, 2, 0)) for a in res)
            else:
                self.results[n] = _adam_shard(s, r, self.chip, self.wts[n], self.mom[n], self.var[n], name="adam_" + n)


def kernel(x, norm_mix_g, w_in, conv_a_w, gdn_conv_w, gdn_A_log, gdn_dt_bias, gdn_norm_g, w_a_out, w_b_out, w_o, norm_ffn_g, w_up, ffn_conv_w, w_down, norm_final_g, loss_target, m_norm_mix_g, m_w_in, m_conv_a_w, m_gdn_conv_w, m_gdn_A_log, m_gdn_dt_bias, m_gdn_norm_g, m_w_a_out, m_w_b_out, m_w_o, m_norm_ffn_g, m_w_up, m_ffn_conv_w, m_w_down, m_norm_final_g, v_norm_mix_g, v_w_in, v_conv_a_w, v_gdn_conv_w, v_gdn_A_log, v_gdn_dt_bias, v_gdn_norm_g, v_w_a_out, v_w_b_out, v_w_o, v_norm_ffn_g, v_w_up, v_ffn_conv_w, v_w_down, v_norm_final_g):
    wts = dict(zip(_WEIGHTS, (norm_mix_g, w_in, conv_a_w, gdn_conv_w, gdn_A_log, gdn_dt_bias, gdn_norm_g, w_a_out,
                              w_b_out, w_o, norm_ffn_g, w_up, ffn_conv_w, w_down, norm_final_g)))
    mom = dict(zip(_WEIGHTS, (m_norm_mix_g, m_w_in, m_conv_a_w, m_gdn_conv_w, m_gdn_A_log, m_gdn_dt_bias,
                              m_gdn_norm_g, m_w_a_out, m_w_b_out, m_w_o, m_norm_ffn_g, m_w_up, m_ffn_conv_w,
                              m_w_down, m_norm_final_g)))
    var = dict(zip(_WEIGHTS, (v_norm_mix_g, v_w_in, v_conv_a_w, v_gdn_conv_w, v_gdn_A_log, v_gdn_dt_bias,
                              v_gdn_norm_g, v_w_a_out, v_w_b_out, v_w_o, v_norm_ffn_g, v_w_up, v_ffn_conv_w,
                              v_w_down, v_norm_final_g)))
    cx, cy, cc = lax.axis_index("x"), lax.axis_index("y"), lax.axis_index("c")
    c_me = jnp.reshape(cc, (1,)).astype(jnp.int32)
    chip = jnp.reshape(2 * cx + cy, (1,)).astype(jnp.int32)
    me = jnp.reshape(4 * cx + 2 * cy + cc, (1,)).astype(jnp.int32)

    def with_up_transposed(t):
        return {**t, "w_up": jnp.swapaxes(t["w_up"], 1, 2)}

    comm = _StepExchanges(with_up_transposed(wts), with_up_transposed(mom), with_up_transposed(var), c_me, chip)
    replicated = {n: wts[n] for n in ("norm_mix_g", "norm_ffn_g", "norm_final_g", "gdn_norm_g", "gdn_A_log", "gdn_dt_bias")}
    loss_p, dx, grads = _local_step(x[0], loss_target[0], replicated, comm)
    res = comm.results
    res["w_up"] = tuple(jnp.swapaxes(a, 1, 2) for a in res["w_up"])

    small = _pack_small(grads["norm_mix_g"], grads["norm_ffn_g"], grads["norm_final_g"], grads["gdn_norm_g"],
                        grads["gdn_A_log"], grads["gdn_dt_bias"], loss_p, grads["conv_a_w"], grads["gdn_conv_w"],
                        grads["ffn_conv_w"])
    (small_all,) = _run_exchange(_gather_exchange([small]), name="ag_small")

    def raw(t):
        return {n: t[n].reshape(1, D) if n == "norm_final_g" else t[n] for n in _SMALL}

    res_small, loss = _adam_small(small_all, me, raw(wts), raw(mom), raw(var))
    for n in _SMALL:
        res[n] = tuple(a.reshape(wts[n].shape) for a in res_small[n])
    outs = [[res[n][i] for n in _WEIGHTS] for i in range(4)]
    return (loss.reshape(()), dx[None], *outs[0], *outs[1], *outs[2], *outs[3])
```

```python
import jax
import jax.numpy as jnp
from jax import lax
from jax.experimental import pallas as pl
from jax.experimental.pallas import tpu as pltpu

f32 = jnp.float32
bf16 = jnp.bfloat16

D = 1024
H = 8
DH = 128
CH = 64
GDN_STEP = 2
DFF = 2816
NW1 = 9216
EPS = 1e-6
N_DEV = 8

ADAM_LR = 0.001
ADAM_B1 = 0.9
ADAM_B2 = 0.999
ADAM_EPS = 1e-08
ADAM_WD = 0.01
ADAM_STEP = 10

VMEM_LIMIT_BYTES = 48 * 1024 * 1024

R_IN, R_UP = 1154, 704

_HI = lax.Precision.HIGHEST
MESH = pl.DeviceIdType.MESH


def _params(n_grid):
    return pltpu.CompilerParams(dimension_semantics=("arbitrary",) * n_grid, vmem_limit_bytes=VMEM_LIMIT_BYTES)


def _bdot(a, b):
    return jnp.dot(a.astype(bf16), b.astype(bf16), preferred_element_type=f32)


def _bdot_nt(a, b):
    return lax.dot_general(a.astype(bf16), b.astype(bf16), (((1,), (1,)), ((), ())), preferred_element_type=f32)


def _bdot_tn(a, b):
    return lax.dot_general(a.astype(bf16), b.astype(bf16), (((0,), (0,)), ((), ())), preferred_element_type=f32)


def _hdot(a, b):
    return jnp.dot(a, b, preferred_element_type=f32, precision=_HI)


def _idot(a, b):
    return jnp.dot(a, b, preferred_element_type=f32, precision=lax.Precision.HIGH)


def _sigmoid(x):
    return 1.0 / (1.0 + jnp.exp(-x))


def _softplus(x):
    return jnp.maximum(x, 0.0) + jnp.log(1.0 + jnp.exp(-jnp.abs(x)))


def _shift_down(x, halo, j):
    if j == 0:
        return x
    xr = pltpu.roll(x, j, 0)
    hr = pltpu.roll(halo, j, 0)
    r8 = lax.broadcasted_iota(jnp.int32, hr.shape, 0)
    top = jnp.where(r8 < j, hr, xr[:8])
    return jnp.concatenate([top, xr[8:]], axis=0)


def _shift_up(x, halo, j):
    if j == 0:
        return x
    n = x.shape[0]
    xr = pltpu.roll(x, n - j, 0)
    hr = pltpu.roll(halo, 8 - j, 0)
    r8 = lax.broadcasted_iota(jnp.int32, hr.shape, 0)
    bot = jnp.where(r8 >= 8 - j, hr, xr[n - 8:])
    return jnp.concatenate([xr[:n - 8], bot], axis=0)


def _taps_down(x, halo, k):
    return [_shift_down(x, halo, k - 1 - j) for j in range(k)]


def _strip(i, base=0):
    return slice(base + i * 128, base + (i + 1) * 128)


def _strip_taps(x, halo, first, k):
    return _taps_down(x, jnp.where(first, 0.0, halo), k)


def _strip_conv(w_ref, sl, taps):
    out = w_ref[0:1, sl] * taps[0]
    for j in range(1, len(taps)):
        out = out + w_ref[j:j + 1, sl] * taps[j]
    return out


def _strip_weight_grad(dw_ref, sl, dy, taps):
    for j, tap in enumerate(taps):
        dw_ref[j:j + 1, sl] += jnp.sum(dy * tap, axis=0, keepdims=True)


def _strip_conv_up(dy, halo, last, w_ref, sl, k):
    halo = jnp.where(last, 0.0, halo)
    out = w_ref[k - 1:k, sl] * dy
    for j in range(k - 1):
        out = out + w_ref[j:j + 1, sl] * _shift_up(dy, halo, k - 1 - j)
    return out


def _row(tb, w, col=0):
    return pl.BlockSpec((tb, w), lambda i: (i, col))


def _prev(tb, w, col=0, rows=8):
    return pl.BlockSpec((rows, w), lambda i: (jnp.maximum(i * (tb // rows) - 1, 0), col))


def _next(tb, w, n_rows, col=0, rows=8):
    last = n_rows // rows - 1
    return pl.BlockSpec((rows, w), lambda i: (jnp.minimum((i + 1) * (tb // rows), last), col))


def _f32(ref, sl):
    return ref[:, sl].astype(f32)


def _halo_before(ref, sl):
    h = _f32(ref, sl)
    return h[h.shape[0] - 8:]


def _halo_after(ref, sl):
    return _f32(ref, sl)[:8]


def _fixed(shape):
    return pl.BlockSpec(shape, lambda i: (0,) * len(shape))


def _pick(n, prefs):
    for p in prefs:
        if n % p == 0:
            return p
    return n


def _matmul(a, b, *, name, nt=False, add=None, tm=1024, tn=1024, tk=None, out_dtype=f32, cols=None, exchange=None):
    m, kd = a.shape
    col0, n = cols if cols is not None else (0, b.shape[0] if nt else b.shape[1])
    tm = _pick(m, (tm, 512, 256))
    tn = _pick(n, (tn, 1024, 512, 128))
    tk = kd if tk is None else tk
    nk = kd // tk
    assert nk == 1 or out_dtype == f32
    assert col0 % tn == 0 and not (nt and cols)
    j0 = col0 // tn
    dims = (((1,), (1,)), ((), ())) if nt else (((1,), (0,)), ((), ()))

    def body(a_ref, b_ref, *rest):
        o_ref = rest[-1]
        part = lax.dot_general(a_ref[...], b_ref[...], dims, preferred_element_type=f32)
        if nk == 1:
            o_ref[...] = (part if add is None else part + rest[0][...]).astype(out_dtype)
            return
        k = pl.program_id(2)

        @pl.when(k == 0)
        def _():
            o_ref[...] = part if add is None else part + rest[0][...]

        @pl.when(k > 0)
        def _():
            o_ref[...] += part

    b_spec = pl.BlockSpec((tn, tk), lambda i, j, k: (j, k)) if nt else pl.BlockSpec((tk, tn), lambda i, j, k: (k, j + j0))
    in_specs = [pl.BlockSpec((tm, tk), lambda i, j, k: (i, k)), b_spec]
    args = [a, b]
    if add is not None:
        in_specs.append(pl.BlockSpec((tm, tn), lambda i, j, k: (i, j)))
        args.append(add)
    return _call_with_exchange(
        body, exchange, name=name, grid=(m // tm, n // tn, nk), in_specs=in_specs,
        out_specs=pl.BlockSpec((tm, tn), lambda i, j, k: (i, j)),
        out_shape=jax.ShapeDtypeStruct((m, n), out_dtype), args=args)


def _call_with_exchange(body, exchange, *, name, grid, in_specs, out_specs, out_shape, args):
    if exchange is None:
        return pl.pallas_call(body, name=name, grid=grid, in_specs=in_specs, out_specs=out_specs, out_shape=out_shape,
                              compiler_params=_params(len(grid)))(*args)
    x_arrays, x_shapes, x_sems, start, wait = exchange
    n_in, n_xin, n_xout = len(args), len(x_arrays), len(x_shapes)

    def full_body(*refs):
        c_in, x_in = refs[:n_in], refs[n_in:n_in + n_xin]
        c_out = refs[n_in + n_xin]
        x_out = refs[n_in + n_xin + 1:n_in + n_xin + 1 + n_xout]
        sems = refs[n_in + n_xin + 1 + n_xout:]
        ids = [pl.program_id(d) for d in range(len(grid))]
        first, last = ids[0] == 0, ids[0] == grid[0] - 1
        for d in range(1, len(grid)):
            first = first & (ids[d] == 0)
            last = last & (ids[d] == grid[d] - 1)

        @pl.when(first)
        def _():
            start(x_in, x_out, sems)

        body(*c_in, c_out)

        @pl.when(last)
        def _():
            wait(x_in, x_out, sems)

    res = pl.pallas_call(
        full_body, name=name, grid=grid, in_specs=list(in_specs) + [_ANY] * n_xin,
        out_specs=[out_specs] + [_ANY] * n_xout, out_shape=[out_shape] + list(x_shapes),
        scratch_shapes=list(x_sems), compiler_params=_params(len(grid)),
    )(*args, *x_arrays)
    return res[0], list(res[1:])


def _matmul_tn(a, b, *, name, tm=1024, tn=1024, exchange=None):
    t, m = a.shape
    _, n = b.shape
    tm = _pick(m, (tm, 1024, 512, 128))
    tn = _pick(n, (tn, 1024, 512, 128))
    tt = _pick(t, (2048, 1024, 512, 256))
    nt = t // tt

    def body(a_ref, b_ref, o_ref):
        k = pl.program_id(2)
        part = lax.dot_general(a_ref[...], b_ref[...], (((0,), (0,)), ((), ())), preferred_element_type=f32)

        @pl.when(k == 0)
        def _():
            o_ref[...] = part

        @pl.when(k > 0)
        def _():
            o_ref[...] += part

    return _call_with_exchange(
        body, exchange, name=name, grid=(m // tm, n // tn, nt),
        in_specs=[pl.BlockSpec((tt, tm), lambda i, j, k: (k, i)), pl.BlockSpec((tt, tn), lambda i, j, k: (k, j))],
        out_specs=pl.BlockSpec((tm, tn), lambda i, j, k: (i, j)),
        out_shape=jax.ShapeDtypeStruct((m, n), f32), args=[a, b])


def _rms_fwd(x, g, *, name, exchange=None):
    t = x.shape[0]
    tb = _pick(t, (256, 128))

    def body(x_ref, g_ref, h_ref):
        xv = x_ref[...]
        r = lax.rsqrt(jnp.mean(xv * xv, axis=-1, keepdims=True) + EPS)
        h_ref[...] = (xv * r * g_ref[...]).astype(bf16)

    return _call_with_exchange(
        body, exchange, name=name, grid=(t // tb,), in_specs=[_row(tb, D), _fixed((1, D))], out_specs=_row(tb, D),
        out_shape=jax.ShapeDtypeStruct((t, D), bf16), args=[x, g])


def _rms_bwd(dh, x, g, dres, *, name, more=None):
    t = x.shape[0]
    tb = _pick(t, (256, 128))

    def body(dh_ref, x_ref, g_ref, dres_ref, *rest):
        dx_ref, dxb_ref, dg_ref = rest[-3:]
        xv = x_ref[...]
        r = lax.rsqrt(jnp.mean(xv * xv, axis=-1, keepdims=True) + EPS)
        xh = xv * r
        dy = dh_ref[...]
        if more is not None:
            dy = dy + lax.dot_general(rest[0][...], rest[1][...], (((1,), (1,)), ((), ())), preferred_element_type=f32)
        dyg = dy * g_ref[...]
        dx = dres_ref[...] + r * (dyg - xh * jnp.mean(dyg * xh, axis=-1, keepdims=True))
        dx_ref[...] = dx
        dxb_ref[...] = dx.astype(bf16)

        @pl.when(pl.program_id(0) == 0)
        def _():
            dg_ref[...] = jnp.zeros_like(dg_ref)

        dg_ref[...] += jnp.sum((dy * xh).reshape(tb // 8, 8, D), axis=0)

    in_specs, args = [_row(tb, D), _row(tb, D), _fixed((1, D)), _row(tb, D)], [dh, x, g, dres]
    if more is not None:
        in_specs += [_row(tb, 128), _fixed(more[1].shape)]
        args += list(more)
    return pl.pallas_call(
        body, name=name, grid=(t // tb,), in_specs=in_specs,
        out_specs=[_row(tb, D), _row(tb, D), _fixed((8, D))],
        out_shape=[jax.ShapeDtypeStruct((t, D), f32), jax.ShapeDtypeStruct((t, D), bf16),
                   jax.ShapeDtypeStruct((8, D), f32)],
        compiler_params=_params(1),
    )(*args)


def _gdn_gates(ab, alog, dtb):
    lane = lax.broadcasted_iota(jnp.int32, ab.shape, 1)
    g = -jnp.exp(alog) * _softplus(ab + dtb)
    beta = _sigmoid(ab)
    return jnp.where(lane < H, g, jnp.where(lane < 2 * H, beta, 0.0))


def _pre_fwd(pg, pq, h1, w2, wa, wg, alog, dtb):
    t = pg.shape[0]
    tb = 128

    def body(p0_ref, p0h_ref, pq_ref, pqh_ref, h1_ref, w2_ref, wa_ref, wg_ref, alog_ref, dtb_ref,
             ya_ref, qn_ref, kn_ref, vc_ref, gb_ref, p2_ref):
        first = pl.program_id(0) == 0
        p2_ref[...] = jnp.dot(h1_ref[...], w2_ref[...], preferred_element_type=f32)
        for i in range(D // 128):
            sl, cg, xv = _strip(i), _strip(i, D), _strip(i, 2 * D)
            taps = _strip_taps(_f32(p0_ref, cg) * _f32(p0_ref, xv), _halo_before(p0h_ref, cg) * _halo_before(p0h_ref, xv),
                               first, 3)
            ya_ref[:, sl] = (_f32(p0_ref, sl) * _strip_conv(wa_ref, sl, taps)).astype(bf16)
        for part, out_ref, scale in ((0, qn_ref, DH ** -0.5), (1, kn_ref, 1.0), (2, vc_ref, None)):
            for h in range(H):
                sl = _strip(h, part * D)
                s = _strip_conv(wg_ref, sl, _strip_taps(pq_ref[:, sl], pqh_ref[:, sl], first, 4))
                s = s * _sigmoid(s)
                if scale is not None:
                    s = s * (lax.rsqrt(jnp.sum(s * s, axis=-1, keepdims=True) + EPS) * scale)
                out_ref[:, _strip(h)] = s
        gb_ref[...] = _gdn_gates(p2_ref[...], alog_ref[...], dtb_ref[...])

    return pl.pallas_call(
        body, name="pre_fwd", grid=(t // tb,),
        in_specs=[_row(tb, 3 * D, 0), _prev(tb, 3 * D, 0, rows=16), _row(tb, 3 * D), _prev(tb, 3 * D), _row(tb, D),
                  _fixed((D, 128)), _fixed((8, D)), _fixed((8, 3 * D)), _fixed((1, 128)), _fixed((1, 128))],
        out_specs=[_row(tb, D), _row(tb, D), _row(tb, D), _row(tb, D), _row(tb, 128), _row(tb, 128)],
        out_shape=[jax.ShapeDtypeStruct((t, D), bf16), jax.ShapeDtypeStruct((t, D), f32),
                   jax.ShapeDtypeStruct((t, D), f32), jax.ShapeDtypeStruct((t, D), f32),
                   jax.ShapeDtypeStruct((t, 128), f32), jax.ShapeDtypeStruct((t, 128), f32)],
        compiler_params=_params(1),
    )(pg, pg, pq, pq, h1, w2, wa, wg, alog, dtb)


_Z_COL, _GA_COL, _GB_COL = 3, 4, 5


def _post_fwd(o, pg, gn):
    t = o.shape[0]
    tb = _pick(t, (256, 128))

    def body(o_ref, z_ref, gn_ref, yb_ref):
        for h in range(H):
            sl = slice(h * DH, (h + 1) * DH)
            oh = o_ref[:, sl]
            z = _f32(z_ref, sl)
            r = lax.rsqrt(jnp.mean(oh * oh, axis=-1, keepdims=True) + EPS)
            yb_ref[:, sl] = (oh * r * gn_ref[...] * (z * _sigmoid(z))).astype(bf16)

    return pl.pallas_call(
        body, name="post_fwd", grid=(t // tb,), in_specs=[_row(tb, D), _row(tb, D, _Z_COL), _fixed((1, DH))],
        out_specs=_row(tb, D), out_shape=jax.ShapeDtypeStruct((t, D), bf16), compiler_params=_params(1),
    )(o, pg, gn)


def _post_bwd(dyb, o, pg, gn):
    t = o.shape[0]
    tb = _pick(t, (256, 128))

    def body(dyb_ref, o_ref, z_ref, gn_ref, do_ref, dz_ref, dgn_ref):
        @pl.when(pl.program_id(0) == 0)
        def _():
            dgn_ref[...] = jnp.zeros_like(dgn_ref)

        gn_v = gn_ref[...]
        acc = jnp.zeros((8, DH), f32)
        for h in range(H):
            sl = slice(h * DH, (h + 1) * DH)
            oh = o_ref[:, sl]
            z = _f32(z_ref, sl)
            dy = dyb_ref[:, sl]
            r = lax.rsqrt(jnp.mean(oh * oh, axis=-1, keepdims=True) + EPS)
            on = oh * r
            sg = _sigmoid(z)
            sz = z * sg
            don = dy * sz
            dz_ref[:, sl] = (dy * on * gn_v * (sg * (1.0 + z * (1.0 - sg)))).astype(bf16)
            acc = acc + jnp.sum((don * on).reshape(tb // 8, 8, DH), axis=0)
            doh = don * gn_v
            do_ref[:, sl] = r * (doh - on * jnp.mean(doh * on, axis=-1, keepdims=True))
        dgn_ref[...] += acc

    return pl.pallas_call(
        body, name="post_bwd", grid=(t // tb,),
        in_specs=[_row(tb, D), _row(tb, D), _row(tb, D, _Z_COL), _fixed((1, DH))],
        out_specs=[_row(tb, D), _row(tb, D), _fixed((8, DH))],
        out_shape=[jax.ShapeDtypeStruct((t, D), f32), jax.ShapeDtypeStruct((t, D), bf16),
                   jax.ShapeDtypeStruct((8, DH), f32)],
        compiler_params=_params(1),
    )(dyb, o, pg, gn)


def _mix_fwd(ya, yb, pg):
    t = ya.shape[0]
    tb = _pick(t, (256, 128))

    def body(ya_ref, yb_ref, ga_ref, gb_ref, mix_ref):
        ya_v, yb_v = ya_ref[...].astype(f32), yb_ref[...].astype(f32)
        mix = _sigmoid(ga_ref[...].astype(f32)) * ya_v + _sigmoid(gb_ref[...].astype(f32)) * yb_v
        mix_ref[...] = mix.astype(bf16)

    return pl.pallas_call(
        body, name="mix_fwd", grid=(t // tb,),
        in_specs=[_row(tb, D), _row(tb, D), _row(tb, D, _GA_COL), _row(tb, D, _GB_COL)],
        out_specs=_row(tb, D), out_shape=jax.ShapeDtypeStruct((t, D), bf16), compiler_params=_params(1),
    )(ya, yb, pg, pg)


def _mix_bwd(dmix, ya, yb, pg):
    t = ya.shape[0]
    tb = _pick(t, (256, 128))

    def body(dm_ref, ya_ref, yb_ref, ga_ref, gb_ref, dya_ref, dyb_ref, dg_ref):
        dm = dm_ref[...].astype(f32)
        sa = _sigmoid(ga_ref[...].astype(f32))
        sb = _sigmoid(gb_ref[...].astype(f32))
        dya_ref[...] = (dm * sa).astype(bf16)
        dyb_ref[...] = (dm * sb).astype(bf16)
        dg_ref[:, :D] = (dm * ya_ref[...].astype(f32) * sa * (1.0 - sa)).astype(bf16)
        dg_ref[:, D:] = (dm * yb_ref[...].astype(f32) * sb * (1.0 - sb)).astype(bf16)

    return pl.pallas_call(
        body, name="mix_bwd", grid=(t // tb,),
        in_specs=[_row(tb, D), _row(tb, D), _row(tb, D), _row(tb, D, _GA_COL), _row(tb, D, _GB_COL)],
        out_specs=[_row(tb, D), _row(tb, D), _row(tb, 2 * D)],
        out_shape=[jax.ShapeDtypeStruct((t, D), bf16), jax.ShapeDtypeStruct((t, D), bf16),
                   jax.ShapeDtypeStruct((t, 2 * D), bf16)],
        compiler_params=_params(1),
    )(dmix, ya, yb, pg, pg)


def _ffn_fwd(up, wf):
    t = up.shape[0]
    tb = 128

    def body(up_ref, uph_ref, wf_ref, act_ref):
        first = pl.program_id(0) == 0
        for i in range(DFF // 128):
            g, v = _strip(i), _strip(i, DFF)
            gate = _strip_conv(wf_ref, g, _strip_taps(_f32(up_ref, g), _halo_before(uph_ref, g), first, 3))
            val = _strip_conv(wf_ref, v, _strip_taps(_f32(up_ref, v), _halo_before(uph_ref, v), first, 3))
            act_ref[:, g] = (gate * _sigmoid(gate) * val).astype(bf16)

    return pl.pallas_call(
        body, name="ffn_fwd", grid=(t // tb,),
        in_specs=[_row(tb, 2 * DFF), _prev(tb, 2 * DFF, rows=16), _fixed((8, 2 * DFF))],
        out_specs=_row(tb, DFF), out_shape=jax.ShapeDtypeStruct((t, DFF), bf16), compiler_params=_params(1),
    )(up, up, wf)


def _ffn_bwd1(dact, up, wf):
    t = up.shape[0]
    tb = 128

    def body(da_ref, up_ref, uph_ref, wf_ref, dc_ref, dw_ref):
        @pl.when(pl.program_id(0) == 0)
        def _():
            dw_ref[...] = jnp.zeros_like(dw_ref)

        first = pl.program_id(0) == 0
        for i in range(DFF // 128):
            g, v = _strip(i), _strip(i, DFF)
            g_taps = _strip_taps(_f32(up_ref, g), _halo_before(uph_ref, g), first, 3)
            v_taps = _strip_taps(_f32(up_ref, v), _halo_before(uph_ref, v), first, 3)
            gate = _strip_conv(wf_ref, g, g_taps)
            val = _strip_conv(wf_ref, v, v_taps)
            sg = _sigmoid(gate)
            da = _f32(da_ref, g)
            dgate = da * val * (sg * (1.0 + gate * (1.0 - sg)))
            dval = da * (gate * sg)
            dc_ref[:, g] = dgate.astype(bf16)
            dc_ref[:, v] = dval.astype(bf16)
            _strip_weight_grad(dw_ref, g, dgate, g_taps)
            _strip_weight_grad(dw_ref, v, dval, v_taps)

    return pl.pallas_call(
        body, name="ffn_bwd1", grid=(t // tb,),
        in_specs=[_row(tb, DFF), _row(tb, 2 * DFF), _prev(tb, 2 * DFF, rows=16), _fixed((8, 2 * DFF))],
        out_specs=[_row(tb, 2 * DFF), _fixed((8, 2 * DFF))],
        out_shape=[jax.ShapeDtypeStruct((t, 2 * DFF), bf16), jax.ShapeDtypeStruct((8, 2 * DFF), f32)],
        compiler_params=_params(1),
    )(dact, up, up, wf)


def _ffn_bwd2(dc, wf):
    t = dc.shape[0]
    tb = 128
    nb = t // tb

    def body(dc_ref, dch_ref, wf_ref, dup_ref):
        last = pl.program_id(0) == nb - 1
        for i in range(2 * DFF // 128):
            sl = _strip(i)
            dup_ref[:, sl] = _strip_conv_up(_f32(dc_ref, sl), _halo_after(dch_ref, sl), last, wf_ref, sl, 3).astype(bf16)

    return pl.pallas_call(
        body, name="ffn_bwd2", grid=(nb,),
        in_specs=[_row(tb, 2 * DFF), _next(tb, 2 * DFF, t, rows=16), _fixed((8, 2 * DFF))],
        out_specs=_row(tb, 2 * DFF), out_shape=jax.ShapeDtypeStruct((t, 2 * DFF), bf16), compiler_params=_params(1),
    )(dc, dc, wf)


def _final(x3, tgt, g):
    t = x3.shape[0]
    tb = _pick(t, (256, 128))

    def body(x_ref, t_ref, g_ref, loss_ref, dx_ref, dxb_ref, dg_ref):
        @pl.when(pl.program_id(0) == 0)
        def _():
            loss_ref[...] = jnp.zeros_like(loss_ref)
            dg_ref[...] = jnp.zeros_like(dg_ref)

        xv = x_ref[...]
        r = lax.rsqrt(jnp.mean(xv * xv, axis=-1, keepdims=True) + EPS)
        xh = xv * r
        gv = g_ref[...]
        e = xh * gv - t_ref[...]
        lrow = 0.5 * jnp.mean(e * e, axis=-1, keepdims=True)
        loss_ref[...] += jnp.sum(jnp.broadcast_to(lrow, (tb, 128)).reshape(tb // 8, 8, 128), axis=0)
        dy = e * (1.0 / D)
        dyg = dy * gv
        dx = r * (dyg - xh * jnp.mean(dyg * xh, axis=-1, keepdims=True))
        dx_ref[...] = dx
        dxb_ref[...] = dx.astype(bf16)
        dg_ref[...] += jnp.sum((dy * xh).reshape(tb // 8, 8, D), axis=0)

    return pl.pallas_call(
        body, name="final", grid=(t // tb,), in_specs=[_row(tb, D), _row(tb, D), _fixed((1, D))],
        out_specs=[_fixed((8, 128)), _row(tb, D), _row(tb, D), _fixed((8, D))],
        out_shape=[jax.ShapeDtypeStruct((8, 128), f32), jax.ShapeDtypeStruct((t, D), f32),
                   jax.ShapeDtypeStruct((t, D), bf16), jax.ShapeDtypeStruct((8, D), f32)],
        compiler_params=_params(1),
    )(x3, tgt, g)


def _pre_bwd1(pg, pq, p2, dya_in, dqn, dkn, dvc, dgb, gbeta, h1, wa, wg, alog, dtb):
    t = pg.shape[0]
    tb = 128

    def body(p0_ref, p0h_ref, pq_ref, pqh_ref, p2_ref, dya_ref, dqn_ref, dkn_ref, dvc_ref, dgb_ref, gb_ref, h1_ref,
             wa_ref, wg_ref, alog_ref, dtb_ref,
             dbg_ref, dca_ref, dc4_ref, dp2_ref, dwa_ref, dwg_ref, dal_ref, ddt_ref, dw2_ref):
        @pl.when(pl.program_id(0) == 0)
        def _():
            dwa_ref[...] = jnp.zeros_like(dwa_ref)
            dwg_ref[...] = jnp.zeros_like(dwg_ref)
            dal_ref[...] = jnp.zeros_like(dal_ref)
            ddt_ref[...] = jnp.zeros_like(ddt_ref)
            dw2_ref[...] = jnp.zeros_like(dw2_ref)

        first = pl.program_id(0) == 0

        for i in range(D // 128):
            sl, cg, xv = _strip(i), _strip(i, D), _strip(i, 2 * D)
            taps = _strip_taps(_f32(p0_ref, cg) * _f32(p0_ref, xv), _halo_before(p0h_ref, cg) * _halo_before(p0h_ref, xv),
                               first, 3)
            dya = _f32(dya_ref, sl)
            dbg_ref[:, sl] = (dya * _strip_conv(wa_ref, sl, taps)).astype(bf16)
            dca = dya * _f32(p0_ref, sl)
            dca_ref[:, sl] = dca.astype(bf16)
            _strip_weight_grad(dwa_ref, sl, dca, taps)

        for part, d_ref, scale in ((0, dqn_ref, DH ** -0.5), (1, dkn_ref, 1.0), (2, dvc_ref, None)):
            for h in range(H):
                sl = _strip(h, part * D)
                taps = _strip_taps(pq_ref[:, sl], pqh_ref[:, sl], first, 4)
                c4 = _strip_conv(wg_ref, sl, taps)
                sg = _sigmoid(c4)
                dn = d_ref[:, _strip(h)]
                if scale is not None:
                    a = c4 * sg
                    r = lax.rsqrt(jnp.sum(a * a, axis=-1, keepdims=True) + EPS)
                    an = a * r
                    dn = dn * scale
                    dn = r * (dn - an * jnp.sum(dn * an, axis=-1, keepdims=True))
                dc4 = dn * (sg * (1.0 + c4 * (1.0 - sg)))
                dc4_ref[:, sl] = dc4.astype(bf16)
                _strip_weight_grad(dwg_ref, sl, dc4, taps)

        ab = p2_ref[...]
        lane = lax.broadcasted_iota(jnp.int32, ab.shape, 1)
        dgbv = dgb_ref[...]
        gbv = gb_ref[...]
        da = dgbv * (-jnp.exp(alog_ref[...])) * _sigmoid(ab + dtb_ref[...])
        db = dgbv * gbv * (1.0 - gbv)
        dp2 = jnp.where(lane < H, da, jnp.where(lane < 2 * H, db, 0.0)).astype(bf16)
        dp2_ref[...] = dp2
        dw2_ref[...] += lax.dot_general(h1_ref[...], dp2, (((0,), (0,)), ((), ())), preferred_element_type=f32)
        dal = jnp.where(lane < H, dgbv * gbv, 0.0)
        ddt = jnp.where(lane < H, da, 0.0)
        dal_ref[...] += jnp.sum(dal.reshape(tb // 8, 8, 128), axis=0)
        ddt_ref[...] += jnp.sum(ddt.reshape(tb // 8, 8, 128), axis=0)

    return pl.pallas_call(
        body, name="pre_bwd1", grid=(t // tb,),
        in_specs=[_row(tb, 3 * D, 0), _prev(tb, 3 * D, 0, rows=16), _row(tb, 3 * D), _prev(tb, 3 * D), _row(tb, 128),
                  _row(tb, D), _row(tb, D), _row(tb, D), _row(tb, D), _row(tb, 128), _row(tb, 128), _row(tb, D),
                  _fixed((8, D)), _fixed((8, 3 * D)), _fixed((1, 128)), _fixed((1, 128))],
        out_specs=[_row(tb, D), _row(tb, D), _row(tb, 3 * D), _row(tb, 128),
                   _fixed((8, D)), _fixed((8, 3 * D)), _fixed((8, 128)), _fixed((8, 128)), _fixed((D, 128))],
        out_shape=[jax.ShapeDtypeStruct((t, D), bf16), jax.ShapeDtypeStruct((t, D), bf16),
                   jax.ShapeDtypeStruct((t, 3 * D), bf16), jax.ShapeDtypeStruct((t, 128), bf16),
                   jax.ShapeDtypeStruct((8, D), f32), jax.ShapeDtypeStruct((8, 3 * D), f32),
                   jax.ShapeDtypeStruct((8, 128), f32), jax.ShapeDtypeStruct((8, 128), f32),
                   jax.ShapeDtypeStruct((D, 128), f32)],
        compiler_params=_params(1),
    )(pg, pg, pq, pq, p2, dya_in, dqn, dkn, dvc, dgb, gbeta, h1, wa, wg, alog, dtb)


def _pre_bwd2(dca, dc4, pg, dbg, dz, dgates, wa, wg, exchange=None):
    t = pg.shape[0]
    tb = 128
    nb = t // tb

    def body(dca_ref, dcah_ref, dc4_ref, dc4h_ref, p0_ref, dbg_ref, dz_ref, dgt_ref, wa_ref, wg_ref, dp_ref):
        last = pl.program_id(0) == nb - 1
        dp_ref[:, :D] = dbg_ref[...]
        for i in range(D // 128):
            sl, cg, xv = _strip(i), _strip(i, D), _strip(i, 2 * D)
            du = _strip_conv_up(_f32(dca_ref, sl), _halo_after(dcah_ref, sl), last, wa_ref, sl, 3)
            dp_ref[:, cg] = (du * _f32(p0_ref, xv)).astype(bf16)
            dp_ref[:, xv] = (du * _f32(p0_ref, cg)).astype(bf16)
        dp_ref[:, 3 * D:4 * D] = dz_ref[...]
        dp_ref[:, 4 * D:6 * D] = dgt_ref[...]
        for i in range(3 * D // 128):
            sl = _strip(i)
            dq = _strip_conv_up(_f32(dc4_ref, sl), _halo_after(dc4h_ref, sl), last, wg_ref, sl, 4)
            dp_ref[:, _strip(i, 6 * D)] = dq.astype(bf16)

    return _call_with_exchange(
        body, exchange, name="pre_bwd2", grid=(nb,),
        in_specs=[_row(tb, D), _next(tb, D, t, rows=16), _row(tb, 3 * D), _next(tb, 3 * D, t, rows=16), _row(tb, 3 * D, 0),
                  _row(tb, D), _row(tb, D), _row(tb, 2 * D), _fixed((8, D)), _fixed((8, 3 * D))],
        out_specs=_row(tb, NW1), out_shape=jax.ShapeDtypeStruct((t, NW1), bf16),
        args=[dca, dca, dc4, dc4, pg, dbg, dz, dgates, wa, wg])


def _chunk_consts():
    r = lax.broadcasted_iota(jnp.int32, (CH, CH), 0)
    c = lax.broadcasted_iota(jnp.int32, (CH, CH), 1)
    return r, c, (r == c).astype(f32)


def _tri_inverse(lows, eye, r, c):
    def same_block(b):
        return jnp.bitwise_xor(r, c) < b

    xs = [jnp.where(same_block(8), -low, 0.0) for low in lows]
    ts = [eye + x for x in xs]
    for _ in range(2):
        xs = [_idot(x, x) for x in xs]
        ts = [t + _idot(t, x) for t, x in zip(ts, xs)]
    for b in (8, 16, 32):
        below = same_block(2 * b) & jnp.logical_not(same_block(b))
        ts = [t - _idot(_idot(t, jnp.where(below, low, 0.0)), t) for t, low in zip(ts, lows)]
    return ts


def _chunk_common(q, k, v, gcol, bcol, r, c, eye):
    grow = jnp.sum(eye * gcol, axis=0, keepdims=True)
    dec = jnp.exp(jnp.where(r >= c, gcol - grow, -jnp.inf))
    rcol = lax.broadcasted_iota(jnp.int32, (CH, 1), 0)
    glast = jnp.sum(jnp.where(rcol == CH - 1, gcol, 0.0), axis=0, keepdims=True)
    eg = jnp.exp(gcol)
    el = jnp.exp(glast - gcol)
    kb = k * bcol
    vb = v * bcol
    kk = _bdot_nt(kb, k)
    low = jnp.where(r > c, kk * dec, 0.0)
    qk = _bdot_nt(q, k)
    att = qk * dec
    return grow, dec, glast, eg, el, kb, vb, kk, low, qk, att, rcol


def _gdn_fwd(qn, kn, vc, gbeta):
    t = qn.shape[0]
    n_chunks = t // CH

    def body(q_ref, k_ref, v_ref, gb_ref, o_ref, s_ref, t_ref, state):
        @pl.when(pl.program_id(0) == 0)
        def _():
            state[...] = jnp.zeros_like(state)

        r, c, eye = _chunk_consts()
        tri = (r >= c).astype(f32)
        heads = range(H)
        keys = [(s, h) for s in range(GDN_STEP) for h in heads]
        rows = [slice(s * CH, (s + 1) * CH) for s in range(GDN_STEP)]
        gbs = [gb_ref[rows[s], :] for s in range(GDN_STEP)]
        galls = [_hdot(tri, gb) for gb in gbs]
        qs = {(s, h): q_ref[rows[s], h * DH:(h + 1) * DH] for s, h in keys}
        ks = {(s, h): k_ref[rows[s], h * DH:(h + 1) * DH] for s, h in keys}
        cm = {(s, h): _chunk_common(qs[s, h], ks[s, h], v_ref[rows[s], h * DH:(h + 1) * DH], galls[s][:, h:h + 1],
                                    gbs[s][:, H + h:H + h + 1], r, c, eye) for s, h in keys}
        invs = dict(zip(keys, _tri_inverse([cm[key][8] for key in keys], eye, r, c)))
        uws = {key: _bdot(invs[key], jnp.concatenate([cm[key][6], cm[key][5] * cm[key][3]], axis=1)) for key in keys}
        sts = [state[h] for h in heads]
        for s in range(GDN_STEP):
            vns = [uws[s, h][:, :DH] - _bdot(uws[s, h][:, DH:], sts[h]) for h in heads]
            outs = [_bdot(qs[s, h] * cm[s, h][3], sts[h]) + _bdot(cm[s, h][10], vns[h]) for h in heads]
            news = [sts[h] * jnp.exp(cm[s, h][2]) + _bdot_tn(ks[s, h] * cm[s, h][4], vns[h]) for h in heads]
            for h in heads:
                s_ref[s, h] = sts[h].astype(bf16)
                t_ref[s, h] = invs[s, h]
                o_ref[rows[s], h * DH:(h + 1) * DH] = outs[h]
            sts = news
        for h in heads:
            state[h] = sts[h]

    tb = GDN_STEP * CH
    return pl.pallas_call(
        body, name="gdn_fwd", grid=(t // tb,),
        in_specs=[_row(tb, D), _row(tb, D), _row(tb, D), _row(tb, 128)],
        out_specs=[_row(tb, D), pl.BlockSpec((GDN_STEP, H, DH, DH), lambda i: (i, 0, 0, 0)),
                   pl.BlockSpec((GDN_STEP, H, CH, CH), lambda i: (i, 0, 0, 0))],
        out_shape=[jax.ShapeDtypeStruct((t, D), f32), jax.ShapeDtypeStruct((n_chunks, H, DH, DH), bf16),
                   jax.ShapeDtypeStruct((n_chunks, H, CH, CH), f32)],
        scratch_shapes=[pltpu.VMEM((H, DH, DH), f32)],
        compiler_params=_params(1),
    )(qn, kn, vc, gbeta)


def _gdn_bwd(qn, kn, vc, gbeta, do, s_all, t_all):
    t = qn.shape[0]

    def body(q_ref, k_ref, v_ref, gb_ref, do_ref, s_ref, t_ref, dq_ref, dk_ref, dv_ref, dgb_ref, dstate):
        @pl.when(pl.program_id(0) == 0)
        def _():
            dstate[...] = jnp.zeros_like(dstate)

        r, c, eye = _chunk_consts()
        tril = r >= c
        lane = lax.broadcasted_iota(jnp.int32, (1, 128), 1)
        hs = range(H)

        def each(fn, *lists):
            return [fn(*args) for args in zip(*lists)]

        def rsum(a):
            return jnp.sum(a, axis=1, keepdims=True)

        def before_state(s):
            rows = slice(s * CH, (s + 1) * CH)
            gb = gb_ref[rows, :]
            gall = _hdot(tril.astype(f32), gb)
            p = {"rows": rows}
            p["q"] = q = [q_ref[rows, h * DH:(h + 1) * DH] for h in hs]
            p["k"] = k = [k_ref[rows, h * DH:(h + 1) * DH] for h in hs]
            p["v"] = v = [v_ref[rows, h * DH:(h + 1) * DH] for h in hs]
            p["dout"] = dout = [do_ref[rows, h * DH:(h + 1) * DH] for h in hs]
            p["inv"] = inv = [t_ref[s, h] for h in hs]
            p["st"] = st = [s_ref[s, h] for h in hs]
            p["bcol"] = bcol = [gb[:, H + h:H + h + 1] for h in hs]
            cm = [_chunk_common(q[h], k[h], v[h], gall[:, h:h + 1], bcol[h], r, c, eye) for h in hs]
            for name, i in (("dec", 1), ("glast", 2), ("eg", 3), ("el", 4), ("kb", 5), ("vb", 6), ("low", 8), ("att", 10)):
                p[name] = [m[i] for m in cm]
            p["rcol"] = cm[0][11]
            p["elast"] = each(jnp.exp, p["glast"])
            p["kbg"] = each(jnp.multiply, p["kb"], p["eg"])
            uw = each(lambda i, a, b: _bdot(i, jnp.concatenate([a, b], axis=1)), inv, p["vb"], p["kbg"])
            p["u"] = [a[:, :DH] for a in uw]
            p["w"] = [a[:, DH:] for a in uw]
            p["vn"] = each(lambda a, b, x: a - _bdot(b, x), p["u"], p["w"], st)
            p["qd"] = each(jnp.multiply, q, p["eg"])
            p["kd"] = each(jnp.multiply, k, p["el"])
            p["dqd"] = each(_bdot_nt, dout, st)
            p["datt"] = each(lambda d, x: jnp.where(tril, _bdot_nt(d, x), 0.0), dout, p["vn"])
            p["dqk"] = each(jnp.multiply, p["datt"], p["dec"])
            p["qd_do"] = each(_bdot_tn, p["qd"], dout)
            p["att_do"] = each(_bdot_tn, p["att"], dout)
            return p

        def after_state(p, ds):
            q, k, v, st, inv, bcol = p["q"], p["k"], p["v"], p["st"], p["inv"], p["bcol"]
            eg, el, kb, u, w = p["eg"], p["el"], p["kb"], p["u"], p["w"]
            dvn = each(lambda a, kk, x: a + _bdot(kk, x), p["att_do"], p["kd"], ds)
            dkd = each(_bdot_nt, p["vn"], ds)
            dw = each(lambda a, x: -_bdot_nt(a, x), dvn, st)
            new_ds = each(lambda x, e, a, ww, dv_: x * e + a - _bdot_tn(ww, dv_), ds, p["elast"], p["qd_do"], w, dvn)
            dglast = each(lambda e, x, d: e * jnp.sum(rsum(x.astype(f32) * d), axis=0, keepdims=True), p["elast"], st, ds)
            dr = each(lambda i, a, b: _bdot_tn(i, jnp.concatenate([a, b], axis=1)), inv, dvn, dw)
            dvb = [a[:, :DH] for a in dr]
            dkbg = [a[:, DH:] for a in dr]
            dlow = each(lambda a, b, x, y: -jnp.where(r > c, _bdot_nt(a, b) + _bdot_nt(x, y), 0.0), dvb, u, dkbg, w)
            dkk = each(jnp.multiply, dlow, p["dec"])
            mm = each(lambda a, b, x, y: a * b + x * y, dlow, p["low"], p["datt"], p["att"])
            dkb = each(lambda a, kk, b, e: _bdot(a, kk) + b * e, dkk, k, dkbg, eg)
            dk = each(lambda a, b, x, y, d, e, f, g: _bdot_tn(a, b) + _bdot_tn(x, y) + d * e + f * g,
                      dkk, kb, p["dqk"], q, dkd, el, dkb, bcol)
            dq = each(lambda a, kk, d, e: _bdot(a, kk) + d * e, p["dqk"], k, p["dqd"], eg)
            dv = each(jnp.multiply, dvb, bcol)
            dbeta = each(lambda a, b, x, y: rsum(a * b) + rsum(x * y), dkb, k, dvb, v)
            deg = each(lambda a, b, x, y: rsum(a * b) + rsum(x * y), dkbg, kb, p["dqd"], q)
            delc = each(lambda a, b, e: rsum(a * b) * e, dkd, k, el)
            dgc = each(lambda m, a, e, d: rsum(m) - rsum(eye * jnp.sum(m, axis=0, keepdims=True)) + a * e - d,
                       mm, deg, eg, delc)
            dgc = each(lambda g, d, l: g + jnp.where(p["rcol"] == CH - 1, jnp.sum(d, axis=0, keepdims=True) + l, 0.0),
                       dgc, delc, dglast)
            dg_acc = jnp.zeros((CH, 128), f32)
            db_acc = jnp.zeros((CH, 128), f32)
            rows = p["rows"]
            for h in hs:
                dq_ref[rows, h * DH:(h + 1) * DH] = dq[h]
                dk_ref[rows, h * DH:(h + 1) * DH] = dk[h]
                dv_ref[rows, h * DH:(h + 1) * DH] = dv[h]
                dg_acc = dg_acc + dgc[h] * (lane == h).astype(f32)
                db_acc = db_acc + dbeta[h] * (lane == H + h).astype(f32)
            dgb_ref[rows, :] = _hdot((r <= c).astype(f32), dg_acc) + db_acc
            return new_ds

        order = list(reversed(range(GDN_STEP)))
        pre = [before_state(s) for s in order]
        ds = [dstate[h] for h in hs]
        for p in pre:
            ds = after_state(p, ds)
        for h in hs:
            dstate[h] = ds[h]

    tb = GDN_STEP * CH
    n_steps = t // tb
    rev = lambda i: (n_steps - 1 - i, 0)
    rev4 = lambda i: (n_steps - 1 - i, 0, 0, 0)
    return pl.pallas_call(
        body, name="gdn_bwd", grid=(n_steps,),
        in_specs=[pl.BlockSpec((tb, D), rev), pl.BlockSpec((tb, D), rev), pl.BlockSpec((tb, D), rev),
                  pl.BlockSpec((tb, 128), rev), pl.BlockSpec((tb, D), rev),
                  pl.BlockSpec((GDN_STEP, H, DH, DH), rev4), pl.BlockSpec((GDN_STEP, H, CH, CH), rev4)],
        out_specs=[pl.BlockSpec((tb, D), rev), pl.BlockSpec((tb, D), rev), pl.BlockSpec((tb, D), rev),
                   pl.BlockSpec((tb, 128), rev)],
        out_shape=[jax.ShapeDtypeStruct((t, D), f32)] * 3 + [jax.ShapeDtypeStruct((t, 128), f32)],
        scratch_shapes=[pltpu.VMEM((H, DH, DH), f32)],
        compiler_params=_params(1),
    )(qn, kn, vc, gbeta, do, s_all, t_all)


def _pad_rows(w, rows=8):
    return jnp.pad(w, ((0, rows - w.shape[0]), (0, 0)))


_REST = ("w_up", "w_a_out", "w_b_out", "w_o", "w_down")


def _local_step(x, tgt, w, comm=None):
    g1 = w["norm_mix_g"].reshape(1, D)
    if comm is None:
        h1 = _rms_fwd(x, g1, name="rms1_fwd")
    else:
        h1, gathered = _rms_fwd(x, g1, name="rms1_fwd", exchange=comm.gather_first())
        w = {**w, **comm.finish_first(gathered)}
    w1, w2 = w["w1"], w["w2"]
    wa = _pad_rows(w["conv_a_w"])
    wg = _pad_rows(w["gdn_conv_w"])
    wf = _pad_rows(w["ffn_conv_w"])
    alog = jnp.pad(w["gdn_A_log"].reshape(1, H), ((0, 0), (0, 128 - H)))
    dtb = jnp.pad(w["gdn_dt_bias"].reshape(1, H), ((0, 0), (0, 128 - H)))
    g2 = w["norm_ffn_g"].reshape(1, D)
    g3 = w["norm_final_g"].reshape(1, D)
    gn = w["gdn_norm_g"].reshape(1, DH)

    if comm is None:
        pg = _matmul(h1, w1, name="mm_in", cols=(0, 6 * D), out_dtype=bf16)
    else:
        pg, gathered = _matmul(h1, w1, name="mm_in", cols=(0, 6 * D), out_dtype=bf16, exchange=comm.gather_rest())
        w = {**w, **comm.finish_gather(gathered)}
    pq = _matmul(h1, w1, name="mm_in_qkv", cols=(6 * D, 3 * D))
    ya_in, qn, kn, vc, gbeta, p2 = _pre_fwd(pg, pq, h1, w2, wa, wg, alog, dtb)
    o, s_all, t_all = _gdn_fwd(qn, kn, vc, gbeta)
    yb_in = _post_fwd(o, pg, gn)
    ya = _matmul(ya_in, w["w_a_out"], name="mm_a", out_dtype=bf16)
    yb = _matmul(yb_in, w["w_b_out"], name="mm_b", out_dtype=bf16)
    mix = _mix_fwd(ya, yb, pg)
    x2 = _matmul(mix, w["w_o"], name="mm_o", add=x)
    h2 = _rms_fwd(x2, g2, name="rms2_fwd")
    up = _matmul(h2, w["w_up"], nt=True, name="mm_up", tn=DFF // 2, out_dtype=bf16)
    act = _ffn_fwd(up, wf)
    x3 = _matmul(act, w["w_down"], name="mm_down", add=x2, tm=512)
    loss_p, dx3, dx3b, dg3 = _final(x3, tgt, g3)

    grads = {"norm_final_g": dg3}
    dact = _matmul(dx3b, w["w_down"], nt=True, name="mm_down_dx", tm=512, tn=DFF, out_dtype=bf16)
    grads["w_down"] = _matmul_tn(act, dx3b, name="mm_down_dw", tm=DFF // 2)
    dc, dwf = _ffn_bwd1(dact, up, wf)
    grads["ffn_conv_w"] = dwf
    dup = _ffn_bwd2(dc, wf)
    dh2 = _matmul(dup, w["w_up"], name="mm_up_dx", tk=DFF)
    grads["w_up"] = _matmul_tn(dup, h2, name="mm_up_dw", tm=DFF // 2)
    dx2, dx2b, dg2 = _rms_bwd(dh2, x2, g2, dx3, name="rms2_bwd")
    grads["norm_ffn_g"] = dg2
    dmix = _matmul(dx2b, w["w_o"], nt=True, name="mm_o_dx", out_dtype=bf16)
    grads["w_o"] = _matmul_tn(mix, dx2b, name="mm_o_dw")
    dya, dyb, dgates = _mix_bwd(dmix, ya, yb, pg)
    dya_in = _matmul(dya, w["w_a_out"], nt=True, name="mm_a_dx", out_dtype=bf16)
    grads["w_a_out"] = _matmul_tn(ya_in, dya, name="mm_a_dw")
    dyb_in = _matmul(dyb, w["w_b_out"], nt=True, name="mm_b_dx")
    grads["w_b_out"] = _matmul_tn(yb_in, dyb, name="mm_b_dw")
    do, dz, dgn = _post_bwd(dyb_in, o, pg, gn)
    grads["gdn_norm_g"] = dgn
    dqn, dkn, dvc, dgb = _gdn_bwd(qn, kn, vc, gbeta, do, s_all, t_all)
    dbg, dca, dc4, dp2, dwa, dwg, dal, ddt, grads["w2"] = _pre_bwd1(pg, pq, p2, dya_in, dqn, dkn, dvc, dgb, gbeta, h1,
                                                                    wa, wg, alog, dtb)
    grads["conv_a_w"] = dwa
    grads["gdn_conv_w"] = dwg
    grads["gdn_A_log"] = dal
    grads["gdn_dt_bias"] = ddt
    if comm is None:
        dp1 = _pre_bwd2(dca, dc4, pg, dbg, dz, dgates, wa, wg)
        grads["w1"] = _matmul_tn(h1, dp1, name="mm_in_dw")
        dh1 = _matmul(dp1, w1, nt=True, name="mm_in_dx", tm=512, tk=NW1 // 2)
    else:
        exchange, blocks = comm.reduce_halves(_REST, grads)
        dp1, recv = _pre_bwd2(dca, dc4, pg, dbg, dz, dgates, wa, wg, exchange=exchange)
        exchange, sums = comm.reduce_sums(_REST, blocks, recv)
        grads["w1"], recv = _matmul_tn(h1, dp1, name="mm_in_dw", exchange=exchange)
        comm.finish_reduce(_REST, sums, recv)
        exchange, blocks = comm.reduce_halves(("w_in",), grads)
        exchange, sums = comm.reduce_sums(("w_in",), blocks, _run_exchange(exchange, name="rs_sibling_w_in"))
        dh1, recv = _matmul(dp1, w1, nt=True, name="mm_in_dx", tm=512, tk=NW1 // 2, exchange=exchange)
        comm.finish_reduce(("w_in",), sums, recv)
    dx, _, dg1 = _rms_bwd(dh1, x, g1, dx2, name="rms1_bwd", more=(dp2, w2))
    grads["norm_mix_g"] = dg1
    return loss_p, dx, grads


_ANY = pl.BlockSpec(memory_space=pl.ANY)


def _remote(src, dst, send_sem, recv_sem, to):
    return pltpu.make_async_remote_copy(src_ref=src, dst_ref=dst, send_sem=send_sem, recv_sem=recv_sem,
                                        device_id=to, device_id_type=MESH)


def _run_exchange(exchange, *, name):
    arrays, shapes, sems, start, wait = exchange
    n_in, n_out = len(arrays), len(shapes)

    def body(*refs):
        start(refs[:n_in], refs[n_in:n_in + n_out], refs[n_in + n_out:])
        wait(refs[:n_in], refs[n_in:n_in + n_out], refs[n_in + n_out:])

    return pl.pallas_call(body, name=name, out_shape=list(shapes), in_specs=[_ANY] * n_in, out_specs=[_ANY] * n_out,
                          scratch_shapes=list(sems))(*arrays)


def _gather_exchange(shards):
    n = len(shards)

    def copies(x_refs, out_refs, sems):
        send_sems, recv_sems, local_sems = sems
        x, y, c = lax.axis_index("x"), lax.axis_index("y"), lax.axis_index("c")

        def flip(v, b):
            return v + b - 2 * v * b

        me, sibling = (x, y, c), (x, y, 1 - c)
        chip1, chip2, diag = (flip(x, 1 - c), flip(y, c)), (flip(x, c), flip(y, 1 - c)), (1 - x, 1 - y)

        def copy(a, k, blk, to, from_input=False):
            dst = out_refs[a].at[4 * blk[0] + 2 * blk[1] + blk[2]]
            return _remote(x_refs[a] if from_input else dst, dst, send_sems.at[a, k], recv_sems.at[a, k], to)

        mine = [pltpu.make_async_copy(x_refs[a], out_refs[a].at[4 * x + 2 * y + c], local_sems.at[a]) for a in range(n)]
        first = []
        for a in range(n):
            first += [copy(a, 0, me, sibling, from_input=True), copy(a, 1, me, (*chip1, c), from_input=True),
                      copy(a, 2, me, (*chip2, c), from_input=True)]
        return copy, mine, first, me, sibling, chip1, chip2, diag, c

    def start(x_refs, out_refs, sems):
        _, mine, first, *_ = copies(x_refs, out_refs, sems)
        for cp in mine + first:
            cp.start()

    def wait(x_refs, out_refs, sems):
        copy, mine, first, me, sibling, chip1, chip2, diag, c = copies(x_refs, out_refs, sems)
        passed = []

        def pass_on(cp):
            passed.append(cp)
            cp.start()

        for a in range(n):
            copy(a, 1, (*chip1, c), me).wait_recv()
            pass_on(copy(a, 3, (*chip1, c), (*chip2, c)))
            pass_on(copy(a, 4, (*chip1, c), sibling))
        for a in range(n):
            copy(a, 2, (*chip2, c), me).wait_recv()
            pass_on(copy(a, 5, (*chip2, c), sibling))
        for a in range(n):
            copy(a, 3, (*diag, c), me).wait_recv()
            pass_on(copy(a, 6, (*diag, c), sibling))
        for a in range(n):
            copy(a, 0, sibling, me).wait_recv()
            copy(a, 4, (*chip2, 1 - c), me).wait_recv()
            copy(a, 5, (*chip1, 1 - c), me).wait_recv()
            copy(a, 6, (*diag, 1 - c), me).wait_recv()
        for cp in first + passed:
            cp.wait_send()
        for cp in mine:
            cp.wait()

    shapes = [jax.ShapeDtypeStruct((N_DEV, *s.shape), s.dtype) for s in shards]
    sems = [pltpu.SemaphoreType.DMA((n, 7)), pltpu.SemaphoreType.DMA((n, 7)), pltpu.SemaphoreType.DMA((n,))]
    return shards, shapes, sems, start, wait


def _gather_direct_exchange(shards):
    n = len(shards)

    def copies(x_refs, out_refs, sems):
        send_sems, recv_sems, local_sems = sems
        x, y, c = lax.axis_index("x"), lax.axis_index("y"), lax.axis_index("c")
        targets = [(x, y, 1 - c), (1 - x, y, c), (x, 1 - y, c), (1 - x, 1 - y, c)]
        local, sends, recvs = [], [], []
        for a in range(n):
            mine = out_refs[a].at[4 * x + 2 * y + c]
            local.append(pltpu.make_async_copy(x_refs[a], mine, local_sems.at[a]))
            for k, to in enumerate(targets):
                theirs = out_refs[a].at[4 * to[0] + 2 * to[1] + to[2]]
                sends.append(_remote(x_refs[a], mine, send_sems.at[a, k], recv_sems.at[a, k], to))
                recvs.append(_remote(theirs, theirs, send_sems.at[a, k], recv_sems.at[a, k], to))
        return local, sends, recvs

    def start(x_refs, out_refs, sems):
        local, sends, _ = copies(x_refs, out_refs, sems)
        for cp in local + sends:
            cp.start()

    def wait(x_refs, out_refs, sems):
        local, sends, recvs = copies(x_refs, out_refs, sems)
        for cp in recvs:
            cp.wait_recv()
        for cp in sends:
            cp.wait_send()
        for cp in local:
            cp.wait()

    shapes = [jax.ShapeDtypeStruct((N_DEV, *s.shape), s.dtype) for s in shards]
    sems = [pltpu.SemaphoreType.DMA((n, 4)), pltpu.SemaphoreType.DMA((n, 4)), pltpu.SemaphoreType.DMA((n,))]
    return shards, shapes, sems, start, wait


def _gather_forward(gathered):
    n = len(gathered)

    def body(*refs):
        out_refs = refs[n:2 * n]
        send_sems, recv_sems = refs[2 * n:]
        x, y, c = lax.axis_index("x"), lax.axis_index("y"), lax.axis_index("c")
        sibling = (x, y, 1 - c)
        sends, recvs = [], []
        for a in range(n):
            for j, (px, py) in enumerate([(1 - x, y), (x, 1 - y), (1 - x, 1 - y)]):
                mine = out_refs[a].at[4 * px + 2 * py + c]
                theirs = out_refs[a].at[4 * px + 2 * py + 1 - c]
                sends.append(_remote(mine, mine, send_sems.at[a, j], recv_sems.at[a, j], sibling))
                recvs.append(_remote(theirs, theirs, send_sems.at[a, j], recv_sems.at[a, j], sibling))
        for cp in sends:
            cp.start()
        for cp in recvs:
            cp.wait_recv()
        for cp in sends:
            cp.wait_send()

    return pl.pallas_call(
        body, name="ag_forward", out_shape=[jax.ShapeDtypeStruct(g.shape, g.dtype) for g in gathered],
        in_specs=[_ANY] * n, out_specs=[_ANY] * n, input_output_aliases={a: a for a in range(n)},
        scratch_shapes=[pltpu.SemaphoreType.DMA((n, 3)), pltpu.SemaphoreType.DMA((n, 3))],
    )(*gathered)


def _chips_exchange(hsums):
    n = len(hsums)

    def copies(h_refs, out_refs, sems):
        send_sems, recv_sems = sems
        x, y, c = lax.axis_index("x"), lax.axis_index("y"), lax.axis_index("c")
        chips = [(1 - x, y), (x, 1 - y), (1 - x, 1 - y)]
        return [_remote(h_refs[a].at[2 * px + py], out_refs[a].at[k], send_sems.at[a, k], recv_sems.at[a, k], (px, py, c))
                for a in range(n) for k, (px, py) in enumerate(chips)]

    def start(h_refs, out_refs, sems):
        for cp in copies(h_refs, out_refs, sems):
            cp.start()

    def wait(h_refs, out_refs, sems):
        for cp in copies(h_refs, out_refs, sems):
            cp.wait()

    shapes = [jax.ShapeDtypeStruct((3, *h.shape[1:]), h.dtype) for h in hsums]
    sems = [pltpu.SemaphoreType.DMA((n, 3)), pltpu.SemaphoreType.DMA((n, 3))]
    return hsums, shapes, sems, start, wait


def _sibling_exchange(halves):
    n = len(halves)

    def copies(p_refs, out_refs, sems):
        send_sems, recv_sems = sems
        x, y, c = lax.axis_index("x"), lax.axis_index("y"), lax.axis_index("c")
        return [_remote(p_refs[a], out_refs[a], send_sems.at[a], recv_sems.at[a], (x, y, 1 - c)) for a in range(n)]

    def start(p_refs, out_refs, sems):
        for cp in copies(p_refs, out_refs, sems):
            cp.start()

    def wait(p_refs, out_refs, sems):
        for cp in copies(p_refs, out_refs, sems):
            cp.wait()

    shapes = [jax.ShapeDtypeStruct(h.shape, h.dtype) for h in halves]
    return halves, shapes, [pltpu.SemaphoreType.DMA((n,)), pltpu.SemaphoreType.DMA((n,))], start, wait


_IN_RANGES = ((0, 3 * D, 0, 0), (3 * D, 6 * D, 0, 6 * D), (6 * D, 7 * D, 0, 3 * D), (7 * D, 7 * D + 16, 1, 0),
              (7 * D + 16, 9 * D + 16, 0, 4 * D))


def _col_pieces(width, ranges):
    pieces = []
    for d in range(N_DEV):
        lo, hi = d * width, (d + 1) * width
        for glo, ghi, mat, mlo in ranges:
            a, b = max(lo, glo), min(hi, ghi)
            if a < b:
                pieces.append((d, a - lo, b - lo, mat, mlo + a - glo))
    return pieces


def _cols_to_matrices(g, ranges, out_widths, *, name):
    _, rows, width = g.shape
    tb = 128
    pieces = _col_pieces(width, ranges)
    covered = [sum(p[2] - p[1] for p in pieces if p[3] == m) for m in range(len(out_widths))]

    def body(g_ref, *o_refs):
        for m, o_ref in enumerate(o_refs):
            if covered[m] < out_widths[m]:
                o_ref[...] = jnp.zeros_like(o_ref)
        for d, b0, b1, m, m0 in pieces:
            o_refs[m][:, m0:m0 + b1 - b0] = g_ref[d, :, b0:b1]

    return pl.pallas_call(
        body, name=name, grid=(rows // tb,), in_specs=[pl.BlockSpec((N_DEV, tb, width), lambda i: (0, i, 0))],
        out_specs=[pl.BlockSpec((tb, wo), lambda i: (i, 0)) for wo in out_widths],
        out_shape=[jax.ShapeDtypeStruct((rows, wo), g.dtype) for wo in out_widths], compiler_params=_params(1),
    )(g)


def _matrices_to_cols(mats, ranges, width, *, name):
    rows = mats[0].shape[0]
    tb = 128
    pieces = _col_pieces(width, ranges)

    def body(*refs):
        m_refs, g_ref = refs[:-1], refs[-1]
        for d, b0, b1, m, m0 in pieces:
            g_ref[d, :, b0:b1] = m_refs[m][:, m0:m0 + b1 - b0]

    return pl.pallas_call(
        body, name=name, grid=(rows // tb,),
        in_specs=[pl.BlockSpec((tb, mt.shape[1]), lambda i: (i, 0)) for mt in mats],
        out_specs=pl.BlockSpec((N_DEV, tb, width), lambda i: (0, i, 0)),
        out_shape=jax.ShapeDtypeStruct((N_DEV, rows, width), mats[0].dtype), compiler_params=_params(1),
    )(*mats)


def _row_block(rows):
    return 128 if rows % 128 == 0 else rows


def _half_bf16(g4, c_other, *, name):
    _, _, rows, width = g4.shape
    tb = _row_block(rows)

    def body(c_ref, p_ref, o_ref):
        o_ref[0] = p_ref[0, 0].astype(bf16)

    grid_spec = pltpu.PrefetchScalarGridSpec(
        num_scalar_prefetch=1, grid=(4, rows // tb),
        in_specs=[pl.BlockSpec((1, 1, tb, width), lambda j, i, c_ref: (j, c_ref[0], i, 0))],
        out_specs=pl.BlockSpec((1, tb, width), lambda j, i, c_ref: (j, i, 0)))
    return pl.pallas_call(
        body, name=name, grid_spec=grid_spec, out_shape=jax.ShapeDtypeStruct((4, rows, width), bf16),
        compiler_params=_params(2),
    )(c_other, g4)


def _pair_sum(g4, recv, c_me, *, name):
    _, _, rows, width = g4.shape
    tb = _row_block(rows)

    def body(c_ref, p_ref, r_ref, o_ref, ob_ref):
        s = p_ref[0, 0] + r_ref[0].astype(f32)
        o_ref[0] = s
        ob_ref[0] = s.astype(bf16)

    blk = pl.BlockSpec((1, tb, width), lambda j, i, c_ref: (j, i, 0))
    grid_spec = pltpu.PrefetchScalarGridSpec(
        num_scalar_prefetch=1, grid=(4, rows // tb),
        in_specs=[pl.BlockSpec((1, 1, tb, width), lambda j, i, c_ref: (j, c_ref[0], i, 0)), blk],
        out_specs=[blk, blk])
    return pl.pallas_call(
        body, name=name, grid_spec=grid_spec,
        out_shape=[jax.ShapeDtypeStruct((4, rows, width), f32), jax.ShapeDtypeStruct((4, rows, width), bf16)],
        compiler_params=_params(2),
    )(c_me, g4, recv)


def _adam_shard(hsum, recv, chip, w, m, v, *, name):
    _, rows, width = w.shape
    tb = _row_block(rows)

    def body(j_ref, h_ref, r_ref, w_ref, m_ref, v_ref, g_out, d_out, m_out, v_out):
        g = ((h_ref[0] + r_ref[0].astype(f32)) + r_ref[1].astype(f32)) + r_ref[2].astype(f32)
        delta, mn, vn = _adam_math(w_ref[0], g, m_ref[0], v_ref[0])
        g_out[0] = g
        d_out[0] = delta
        m_out[0] = mn
        v_out[0] = vn

    blk = pl.BlockSpec((1, tb, width), lambda i, j_ref: (0, i, 0))
    grid_spec = pltpu.PrefetchScalarGridSpec(
        num_scalar_prefetch=1, grid=(rows // tb,),
        in_specs=[pl.BlockSpec((1, tb, width), lambda i, j_ref: (j_ref[0], i, 0)),
                  pl.BlockSpec((3, tb, width), lambda i, j_ref: (0, i, 0)), blk, blk, blk],
        out_specs=[blk, blk, blk, blk])
    return pl.pallas_call(
        body, name=name, grid_spec=grid_spec, out_shape=[jax.ShapeDtypeStruct(w.shape, f32)] * 4,
        compiler_params=_params(1),
    )(chip, hsum, recv, w, m, v)


def _sum_shard(hsum, recv, chip, *, name):
    _, rows, width = hsum.shape
    tb = _row_block(rows)

    def body(j_ref, h_ref, r_ref, g_out):
        g_out[...] = ((h_ref[0] + r_ref[0].astype(f32)) + r_ref[1].astype(f32)) + r_ref[2].astype(f32)

    grid_spec = pltpu.PrefetchScalarGridSpec(
        num_scalar_prefetch=1, grid=(rows // tb,),
        in_specs=[pl.BlockSpec((1, tb, width), lambda i, j_ref: (j_ref[0], i, 0)),
                  pl.BlockSpec((3, tb, width), lambda i, j_ref: (0, i, 0))],
        out_specs=pl.BlockSpec((tb, width), lambda i, j_ref: (i, 0)))
    return pl.pallas_call(body, name=name, grid_spec=grid_spec, out_shape=jax.ShapeDtypeStruct((rows, width), f32),
                          compiler_params=_params(1))(chip, hsum, recv)


def _adam_columns(g, w, m, v, *, name):
    cols, _, rows = w.shape
    tb = cols // 2

    def body(g_ref, w_ref, m_ref, v_ref, d_out, m_out, v_out):
        delta, mn, vn = _adam_math(w_ref[...], g_ref[...], m_ref[...], v_ref[...])
        d_out[...] = delta
        m_out[...] = mn
        v_out[...] = vn

    blk = pl.BlockSpec((tb, 1, rows), lambda i: (i, 0, 0))
    return pl.pallas_call(
        body, name=name, grid=(cols // tb,), in_specs=[blk] * 4, out_specs=[blk] * 3,
        out_shape=[jax.ShapeDtypeStruct(w.shape, f32)] * 3, compiler_params=_params(1),
    )(g, w, m, v)


R_SMALL = 16 + 16 * N_DEV
_SMALL_LANES = {"gdn_norm_g": (0, DH), "gdn_A_log": (DH, DH + H), "gdn_dt_bias": (2 * DH, 2 * DH + H)}
_LOSS_LANE = 3 * DH


def _pack_small(dg1, dg2, dg3, dgn, dal, ddt, loss_p, dwa, dwg, dwf):
    def body(dg1_ref, dg2_ref, dg3_ref, dgn_ref, dal_ref, ddt_ref, loss_ref, dwa_ref, dwg_ref, dwf_ref, o_ref):
        def total(ref):
            return jnp.sum(ref[...], axis=0, keepdims=True)

        o_ref[...] = jnp.zeros_like(o_ref)
        o_ref[0:1, :] = total(dg1_ref)
        o_ref[1:2, :] = total(dg2_ref)
        o_ref[2:3, :] = total(dg3_ref)
        o_ref[3:4, 0:DH] = total(dgn_ref)
        o_ref[3:4, DH:2 * DH] = total(dal_ref)
        o_ref[3:4, 2 * DH:3 * DH] = total(ddt_ref)
        o_ref[3:4, 3 * DH:4 * DH] = total(loss_ref)
        for d in range(N_DEV):
            base = 16 + 16 * d
            o_ref[base:base + 3, 0:128] = dwa_ref[0:3, 128 * d:128 * (d + 1)]
            o_ref[base:base + 4, 128:512] = dwg_ref[0:4, 384 * d:384 * (d + 1)]
            o_ref[base + 8:base + 11, 0:704] = dwf_ref[0:3, 704 * d:704 * (d + 1)]

    return pl.pallas_call(body, name="pack_small", out_shape=jax.ShapeDtypeStruct((R_SMALL, D), f32))(
        dg1, dg2, dg3, dgn, dal, ddt, loss_p, dwa, dwg, dwf)


_SMALL = ("norm_mix_g", "norm_ffn_g", "norm_final_g", "gdn_norm_g", "gdn_A_log", "gdn_dt_bias",
          "conv_a_w", "gdn_conv_w", "ffn_conv_w")


def _adam_small(gath, me, w, m, v):
    arrays = [t[n] for n in _SMALL for t in (w, m, v)]

    def body(me_ref, ga_ref, gb_ref, *refs):
        ins, outs = refs[:len(arrays)], refs[len(arrays):]
        ga, gb = ga_ref[0], gb_ref[0]
        for s in range(1, N_DEV):
            ga = ga + ga_ref[s]
            gb = gb + gb_ref[s]
        grads = {"norm_mix_g": ga[0:1, :], "norm_ffn_g": ga[1:2, :], "norm_final_g": ga[2:3, :],
                 "conv_a_w": gb[0:3, 0:128], "gdn_conv_w": gb[0:4, 128:512], "ffn_conv_w": gb[8:11, 0:704]}
        for n, (lo, hi) in _SMALL_LANES.items():
            grads[n] = ga[3:4, lo:hi]
        for i, n in enumerate(_SMALL):
            three_d = len(w[n].shape) == 3
            wv, mv, vv = (r[0] if three_d else r[...] for r in ins[3 * i:3 * i + 3])
            delta, mn, vn = _adam_math(wv, grads[n], mv, vv)
            for o_ref, val in zip(outs[4 * i:4 * i + 4], (grads[n], delta, mn, vn)):
                if three_d:
                    o_ref[0] = val
                else:
                    o_ref[...] = val
        outs[-1][...] = ga[3:4, _LOSS_LANE:_LOSS_LANE + 1]

    def whole(shape):
        return pl.BlockSpec(shape, lambda i, me_ref: (0,) * len(shape))

    grid_spec = pltpu.PrefetchScalarGridSpec(
        num_scalar_prefetch=1, grid=(1,),
        in_specs=[pl.BlockSpec((N_DEV, 16, D), lambda i, me_ref: (0, 0, 0)),
                  pl.BlockSpec((N_DEV, 16, D), lambda i, me_ref: (0, 1 + me_ref[0], 0))] + [whole(a.shape) for a in arrays],
        out_specs=[whole(w[n].shape) for n in _SMALL for _ in range(4)] + [whole((1, 1))])
    res = pl.pallas_call(
        body, name="adam_small", grid_spec=grid_spec,
        out_shape=[jax.ShapeDtypeStruct(w[n].shape, f32) for n in _SMALL for _ in range(4)]
        + [jax.ShapeDtypeStruct((1, 1), f32)],
        compiler_params=_params(1),
    )(me, gath, gath, *arrays)
    return {n: tuple(res[4 * i:4 * i + 4]) for i, n in enumerate(_SMALL)}, res[-1]


def _adam_math(w, g, m, v):
    m = ADAM_B1 * m + (1.0 - ADAM_B1) * g
    v = ADAM_B2 * v + (1.0 - ADAM_B2) * jnp.square(g)
    m_hat = m / (1.0 - ADAM_B1 ** ADAM_STEP)
    v_hat = v / (1.0 - ADAM_B2 ** ADAM_STEP)
    delta = -ADAM_LR * (m_hat / (jnp.sqrt(v_hat) + ADAM_EPS) + ADAM_WD * w)
    return delta, m, v


_WEIGHTS = ("norm_mix_g", "w_in", "conv_a_w", "gdn_conv_w", "gdn_A_log", "gdn_dt_bias", "gdn_norm_g", "w_a_out",
            "w_b_out", "w_o", "norm_ffn_g", "w_up", "ffn_conv_w", "w_down", "norm_final_g")
_BIG = ("w_in",) + _REST
_CONVS = ("conv_a_w", "gdn_conv_w", "ffn_conv_w")


class _StepExchanges:
    def __init__(self, wts, mom, var, c_me, chip):
        self.wts, self.mom, self.var, self.c_me, self.chip = wts, mom, var, c_me, chip
        self.results = {}

    def gather_first(self):
        return _gather_exchange([self.wts["w_in"][0].astype(bf16)] + [self.wts[n][0] for n in _CONVS])

    def finish_first(self, gathered):
        g_in, gc_a, gc_g, gc_f = gathered
        w1, w2 = _cols_to_matrices(g_in, _IN_RANGES, (NW1, 128), name="relay_w_in")
        return {"w1": w1, "w2": w2, "conv_a_w": gc_a.transpose(1, 0, 2).reshape(3, D),
                "gdn_conv_w": gc_g.transpose(1, 0, 2).reshape(4, 3 * D),
                "ffn_conv_w": gc_f.transpose(1, 0, 2).reshape(3, 2 * DFF)}

    def gather_rest(self):
        return _gather_direct_exchange([self.wts[n][0].astype(bf16) for n in _REST])

    def finish_gather(self, gathered):
        g_up, g_a, g_b, g_o, g_down = _gather_forward(gathered)
        return {"w_up": g_up.reshape(2 * DFF, D), "w_a_out": g_a.reshape(D, D), "w_b_out": g_b.reshape(D, D),
                "w_o": g_o.reshape(D, D), "w_down": g_down.reshape(DFF, D)}

    def reduce_halves(self, names, grads):
        blocks = []
        for n in names:
            if n == "w_in":
                g = _matrices_to_cols([grads["w1"], grads["w2"]], _IN_RANGES, R_IN, name="relay_dw_in")
            else:
                g = grads[n]
            blocks.append(g.reshape(4, 2, *self.wts[n].shape[1:]))
        return _sibling_exchange([_half_bf16(g, 1 - self.c_me, name="rs_half_" + n) for n, g in zip(names, blocks)]), blocks

    def reduce_sums(self, names, blocks, recv):
        sums = [_pair_sum(g, r, self.c_me, name="rs_sum_" + n) for n, g, r in zip(names, blocks, recv)]
        return _chips_exchange([s[1] for s in sums]), [s[0] for s in sums]

    def finish_reduce(self, names, sums, recv):
        for n, s, r in zip(names, sums, recv):
            if n == "w_in":
                g = jnp.transpose(_sum_shard(s, r, self.chip, name="rs_total_w_in"))[:, None, :]
                w, m, v = (jnp.transpose(t[n], (2, 0, 1)) for t in (self.wts, self.mom, self.var))
                res = (g, *_adam_columns(g, w, m, v, name="adam_w_in"))
                self.results[n] = tuple(jnp.transpose(a, (1, 2, 0)) for a in res)
            else:
                self.results[n] = _adam_shard(s, r, self.chip, self.wts[n], self.mom[n], self.var[n], name="adam_" + n)


def kernel(x, norm_mix_g, w_in, conv_a_w, gdn_conv_w, gdn_A_log, gdn_dt_bias, gdn_norm_g, w_a_out, w_b_out, w_o, norm_ffn_g, w_up, ffn_conv_w, w_down, norm_final_g, loss_target, m_norm_mix_g, m_w_in, m_conv_a_w, m_gdn_conv_w, m_gdn_A_log, m_gdn_dt_bias, m_gdn_norm_g, m_w_a_out, m_w_b_out, m_w_o, m_norm_ffn_g, m_w_up, m_ffn_conv_w, m_w_down, m_norm_final_g, v_norm_mix_g, v_w_in, v_conv_a_w, v_gdn_conv_w, v_gdn_A_log, v_gdn_dt_bias, v_gdn_norm_g, v_w_a_out, v_w_b_out, v_w_o, v_norm_ffn_g, v_w_up, v_ffn_conv_w, v_w_down, v_norm_final_g):
    wts = dict(zip(_WEIGHTS, (norm_mix_g, w_in, conv_a_w, gdn_conv_w, gdn_A_log, gdn_dt_bias, gdn_norm_g, w_a_out,
                              w_b_out, w_o, norm_ffn_g, w_up, ffn_conv_w, w_down, norm_final_g)))
    mom = dict(zip(_WEIGHTS, (m_norm_mix_g, m_w_in, m_conv_a_w, m_gdn_conv_w, m_gdn_A_log, m_gdn_dt_bias,
                              m_gdn_norm_g, m_w_a_out, m_w_b_out, m_w_o, m_norm_ffn_g, m_w_up, m_ffn_conv_w,
                              m_w_down, m_norm_final_g)))
    var = dict(zip(_WEIGHTS, (v_norm_mix_g, v_w_in, v_conv_a_w, v_gdn_conv_w, v_gdn_A_log, v_gdn_dt_bias,
                              v_gdn_norm_g, v_w_a_out, v_w_b_out, v_w_o, v_norm_ffn_g, v_w_up, v_ffn_conv_w,
                              v_w_down, v_norm_final_g)))
    cx, cy, cc = lax.axis_index("x"), lax.axis_index("y"), lax.axis_index("c")
    c_me = jnp.reshape(cc, (1,)).astype(jnp.int32)
    chip = jnp.reshape(2 * cx + cy, (1,)).astype(jnp.int32)
    me = jnp.reshape(4 * cx + 2 * cy + cc, (1,)).astype(jnp.int32)

    def with_up_transposed(t):
        return {**t, "w_up": jnp.swapaxes(t["w_up"], 1, 2)}

    comm = _StepExchanges(with_up_transposed(wts), with_up_transposed(mom), with_up_transposed(var), c_me, chip)
    replicated = {n: wts[n] for n in ("norm_mix_g", "norm_ffn_g", "norm_final_g", "gdn_norm_g", "gdn_A_log", "gdn_dt_bias")}
    loss_p, dx, grads = _local_step(x[0], loss_target[0], replicated, comm)
    res = comm.results
    res["w_up"] = tuple(jnp.swapaxes(a, 1, 2) for a in res["w_up"])

    small = _pack_small(grads["norm_mix_g"], grads["norm_ffn_g"], grads["norm_final_g"], grads["gdn_norm_g"],
                        grads["gdn_A_log"], grads["gdn_dt_bias"], loss_p, grads["conv_a_w"], grads["gdn_conv_w"],
                        grads["ffn_conv_w"])
    (small_all,) = _run_exchange(_gather_exchange([small]), name="ag_small")

    def raw(t):
        return {n: t[n].reshape(1, D) if n == "norm_final_g" else t[n] for n in _SMALL}

    res_small, loss = _adam_small(small_all, me, raw(wts), raw(mom), raw(var))
    for n in _SMALL:
        res[n] = tuple(a.reshape(wts[n].shape) for a in res_small[n])
    outs = [[res[n][i] for n in _WEIGHTS] for i in range(4)]
    return (loss.reshape(()), dx[None], *outs[0], *outs[1], *outs[2], *outs[3])
```

```python
import jax
import jax.numpy as jnp
from jax import lax
from jax.experimental import pallas as pl
from jax.experimental.pallas import tpu as pltpu

f32 = jnp.float32
bf16 = jnp.bfloat16

D = 1024
H = 8
DH = 128
CH = 64
GDN_STEP = 2
DFF = 2816
NW1 = 9216
EPS = 1e-6
N_DEV = 8

ADAM_LR = 0.001
ADAM_B1 = 0.9
ADAM_B2 = 0.999
ADAM_EPS = 1e-08
ADAM_WD = 0.01
ADAM_STEP = 10

VMEM_LIMIT_BYTES = 48 * 1024 * 1024

R_IN, R_UP = 1154, 704

_HI = lax.Precision.HIGHEST
MESH = pl.DeviceIdType.MESH


def _params(n_grid):
    return pltpu.CompilerParams(dimension_semantics=("arbitrary",) * n_grid, vmem_limit_bytes=VMEM_LIMIT_BYTES)


def _bdot(a, b):
    return jnp.dot(a.astype(bf16), b.astype(bf16), preferred_element_type=f32)


def _bdot_nt(a, b):
    return lax.dot_general(a.astype(bf16), b.astype(bf16), (((1,), (1,)), ((), ())), preferred_element_type=f32)


def _bdot_tn(a, b):
    return lax.dot_general(a.astype(bf16), b.astype(bf16), (((0,), (0,)), ((), ())), preferred_element_type=f32)


def _hdot(a, b):
    return jnp.dot(a, b, preferred_element_type=f32, precision=_HI)


def _idot(a, b):
    return jnp.dot(a, b, preferred_element_type=f32, precision=lax.Precision.HIGH)


def _sigmoid(x):
    return 1.0 / (1.0 + jnp.exp(-x))


def _softplus(x):
    return jnp.maximum(x, 0.0) + jnp.log(1.0 + jnp.exp(-jnp.abs(x)))


def _shift_down(x, halo, j):
    if j == 0:
        return x
    xr = pltpu.roll(x, j, 0)
    hr = pltpu.roll(halo, j, 0)
    r8 = lax.broadcasted_iota(jnp.int32, hr.shape, 0)
    top = jnp.where(r8 < j, hr, xr[:8])
    return jnp.concatenate([top, xr[8:]], axis=0)


def _shift_up(x, halo, j):
    if j == 0:
        return x
    n = x.shape[0]
    xr = pltpu.roll(x, n - j, 0)
    hr = pltpu.roll(halo, 8 - j, 0)
    r8 = lax.broadcasted_iota(jnp.int32, hr.shape, 0)
    bot = jnp.where(r8 >= 8 - j, hr, xr[n - 8:])
    return jnp.concatenate([xr[:n - 8], bot], axis=0)


def _taps_down(x, halo, k):
    return [_shift_down(x, halo, k - 1 - j) for j in range(k)]


def _strip(i, base=0):
    return slice(base + i * 128, base + (i + 1) * 128)


def _strip_taps(x, halo, first, k):
    return _taps_down(x, jnp.where(first, 0.0, halo), k)


def _strip_conv(w_ref, sl, taps):
    out = w_ref[0:1, sl] * taps[0]
    for j in range(1, len(taps)):
        out = out + w_ref[j:j + 1, sl] * taps[j]
    return out


def _strip_weight_grad(dw_ref, sl, dy, taps):
    for j, tap in enumerate(taps):
        dw_ref[j:j + 1, sl] += jnp.sum(dy * tap, axis=0, keepdims=True)


def _strip_conv_up(dy, halo, last, w_ref, sl, k):
    halo = jnp.where(last, 0.0, halo)
    out = w_ref[k - 1:k, sl] * dy
    for j in range(k - 1):
        out = out + w_ref[j:j + 1, sl] * _shift_up(dy, halo, k - 1 - j)
    return out


def _row(tb, w, col=0):
    return pl.BlockSpec((tb, w), lambda i: (i, col))


def _prev(tb, w, col=0, rows=8):
    return pl.BlockSpec((rows, w), lambda i: (jnp.maximum(i * (tb // rows) - 1, 0), col))


def _next(tb, w, n_rows, col=0, rows=8):
    last = n_rows // rows - 1
    return pl.BlockSpec((rows, w), lambda i: (jnp.minimum((i + 1) * (tb // rows), last), col))


def _f32(ref, sl):
    return ref[:, sl].astype(f32)


def _halo_before(ref, sl):
    h = _f32(ref, sl)
    return h[h.shape[0] - 8:]


def _halo_after(ref, sl):
    return _f32(ref, sl)[:8]


def _fixed(shape):
    return pl.BlockSpec(shape, lambda i: (0,) * len(shape))


def _pick(n, prefs):
    for p in prefs:
        if n % p == 0:
            return p
    return n


def _matmul(a, b, *, name, nt=False, add=None, tm=1024, tn=1024, tk=None, out_dtype=f32, cols=None, exchange=None):
    m, kd = a.shape
    col0, n = cols if cols is not None else (0, b.shape[0] if nt else b.shape[1])
    tm = _pick(m, (tm, 512, 256))
    tn = _pick(n, (tn, 1024, 512, 128))
    tk = kd if tk is None else tk
    nk = kd // tk
    assert nk == 1 or out_dtype == f32
    assert col0 % tn == 0 and not (nt and cols)
    j0 = col0 // tn
    dims = (((1,), (1,)), ((), ())) if nt else (((1,), (0,)), ((), ()))

    def body(a_ref, b_ref, *rest):
        o_ref = rest[-1]
        part = lax.dot_general(a_ref[...], b_ref[...], dims, preferred_element_type=f32)
        if nk == 1:
            o_ref[...] = (part if add is None else part + rest[0][...]).astype(out_dtype)
            return
        k = pl.program_id(2)

        @pl.when(k == 0)
        def _():
            o_ref[...] = part if add is None else part + rest[0][...]

        @pl.when(k > 0)
        def _():
            o_ref[...] += part

    b_spec = pl.BlockSpec((tn, tk), lambda i, j, k: (j, k)) if nt else pl.BlockSpec((tk, tn), lambda i, j, k: (k, j + j0))
    in_specs = [pl.BlockSpec((tm, tk), lambda i, j, k: (i, k)), b_spec]
    args = [a, b]
    if add is not None:
        in_specs.append(pl.BlockSpec((tm, tn), lambda i, j, k: (i, j)))
        args.append(add)
    return _call_with_exchange(
        body, exchange, name=name, grid=(m // tm, n // tn, nk), in_specs=in_specs,
        out_specs=pl.BlockSpec((tm, tn), lambda i, j, k: (i, j)),
        out_shape=jax.ShapeDtypeStruct((m, n), out_dtype), args=args)


def _call_with_exchange(body, exchange, *, name, grid, in_specs, out_specs, out_shape, args):
    if exchange is None:
        return pl.pallas_call(body, name=name, grid=grid, in_specs=in_specs, out_specs=out_specs, out_shape=out_shape,
                              compiler_params=_params(len(grid)))(*args)
    x_arrays, x_shapes, x_sems, start, wait = exchange
    n_in, n_xin, n_xout = len(args), len(x_arrays), len(x_shapes)

    def full_body(*refs):
        c_in, x_in = refs[:n_in], refs[n_in:n_in + n_xin]
        c_out = refs[n_in + n_xin]
        x_out = refs[n_in + n_xin + 1:n_in + n_xin + 1 + n_xout]
        sems = refs[n_in + n_xin + 1 + n_xout:]
        ids = [pl.program_id(d) for d in range(len(grid))]
        first, last = ids[0] == 0, ids[0] == grid[0] - 1
        for d in range(1, len(grid)):
            first = first & (ids[d] == 0)
            last = last & (ids[d] == grid[d] - 1)

        @pl.when(first)
        def _():
            start(x_in, x_out, sems)

        body(*c_in, c_out)

        @pl.when(last)
        def _():
            wait(x_in, x_out, sems)

    res = pl.pallas_call(
        full_body, name=name, grid=grid, in_specs=list(in_specs) + [_ANY] * n_xin,
        out_specs=[out_specs] + [_ANY] * n_xout, out_shape=[out_shape] + list(x_shapes),
        scratch_shapes=list(x_sems), compiler_params=_params(len(grid)),
    )(*args, *x_arrays)
    return res[0], list(res[1:])


def _matmul_tn(a, b, *, name, tm=1024, tn=1024, exchange=None):
    t, m = a.shape
    _, n = b.shape
    tm = _pick(m, (tm, 1024, 512, 128))
    tn = _pick(n, (tn, 1024, 512, 128))
    tt = _pick(t, (2048, 1024, 512, 256))
    nt = t // tt

    def body(a_ref, b_ref, o_ref):
        k = pl.program_id(2)
        part = lax.dot_general(a_ref[...], b_ref[...], (((0,), (0,)), ((), ())), preferred_element_type=f32)

        @pl.when(k == 0)
        def _():
            o_ref[...] = part

        @pl.when(k > 0)
        def _():
            o_ref[...] += part

    return _call_with_exchange(
        body, exchange, name=name, grid=(m // tm, n // tn, nt),
        in_specs=[pl.BlockSpec((tt, tm), lambda i, j, k: (k, i)), pl.BlockSpec((tt, tn), lambda i, j, k: (k, j))],
        out_specs=pl.BlockSpec((tm, tn), lambda i, j, k: (i, j)),
        out_shape=jax.ShapeDtypeStruct((m, n), f32), args=[a, b])


def _rms_fwd(x, g, *, name, exchange=None):
    t = x.shape[0]
    tb = _pick(t, (256, 128))

    def body(x_ref, g_ref, h_ref):
        xv = x_ref[...]
        r = lax.rsqrt(jnp.mean(xv * xv, axis=-1, keepdims=True) + EPS)
        h_ref[...] = (xv * r * g_ref[...]).astype(bf16)

    return _call_with_exchange(
        body, exchange, name=name, grid=(t // tb,), in_specs=[_row(tb, D), _fixed((1, D))], out_specs=_row(tb, D),
        out_shape=jax.ShapeDtypeStruct((t, D), bf16), args=[x, g])


def _rms_bwd(dh, x, g, dres, *, name, more=None):
    t = x.shape[0]
    tb = _pick(t, (256, 128))

    def body(dh_ref, x_ref, g_ref, dres_ref, *rest):
        dx_ref, dxb_ref, dg_ref = rest[-3:]
        xv = x_ref[...]
        r = lax.rsqrt(jnp.mean(xv * xv, axis=-1, keepdims=True) + EPS)
        xh = xv * r
        dy = dh_ref[...]
        if more is not None:
            dy = dy + lax.dot_general(rest[0][...], rest[1][...], (((1,), (1,)), ((), ())), preferred_element_type=f32)
        dyg = dy * g_ref[...]
        dx = dres_ref[...] + r * (dyg - xh * jnp.mean(dyg * xh, axis=-1, keepdims=True))
        dx_ref[...] = dx
        dxb_ref[...] = dx.astype(bf16)

        @pl.when(pl.program_id(0) == 0)
        def _():
            dg_ref[...] = jnp.zeros_like(dg_ref)

        dg_ref[...] += jnp.sum((dy * xh).reshape(tb // 8, 8, D), axis=0)

    in_specs, args = [_row(tb, D), _row(tb, D), _fixed((1, D)), _row(tb, D)], [dh, x, g, dres]
    if more is not None:
        in_specs += [_row(tb, 128), _fixed(more[1].shape)]
        args += list(more)
    return pl.pallas_call(
        body, name=name, grid=(t // tb,), in_specs=in_specs,
        out_specs=[_row(tb, D), _row(tb, D), _fixed((8, D))],
        out_shape=[jax.ShapeDtypeStruct((t, D), f32), jax.ShapeDtypeStruct((t, D), bf16),
                   jax.ShapeDtypeStruct((8, D), f32)],
        compiler_params=_params(1),
    )(*args)


def _gdn_gates(ab, alog, dtb):
    lane = lax.broadcasted_iota(jnp.int32, ab.shape, 1)
    g = -jnp.exp(alog) * _softplus(ab + dtb)
    beta = _sigmoid(ab)
    return jnp.where(lane < H, g, jnp.where(lane < 2 * H, beta, 0.0))


def _pre_fwd(pg, pq, h1, w2, wa, wg, alog, dtb):
    t = pg.shape[0]
    tb = 128

    def body(p0_ref, p0h_ref, pq_ref, pqh_ref, h1_ref, w2_ref, wa_ref, wg_ref, alog_ref, dtb_ref,
             ya_ref, qn_ref, kn_ref, vc_ref, gb_ref, p2_ref):
        first = pl.program_id(0) == 0
        p2_ref[...] = jnp.dot(h1_ref[...], w2_ref[...], preferred_element_type=f32)
        for i in range(D // 128):
            sl, cg, xv = _strip(i), _strip(i, D), _strip(i, 2 * D)
            taps = _strip_taps(_f32(p0_ref, cg) * _f32(p0_ref, xv), _halo_before(p0h_ref, cg) * _halo_before(p0h_ref, xv),
                               first, 3)
            ya_ref[:, sl] = (_f32(p0_ref, sl) * _strip_conv(wa_ref, sl, taps)).astype(bf16)
        for part, out_ref, scale in ((0, qn_ref, DH ** -0.5), (1, kn_ref, 1.0), (2, vc_ref, None)):
            for h in range(H):
                sl = _strip(h, part * D)
                s = _strip_conv(wg_ref, sl, _strip_taps(pq_ref[:, sl], pqh_ref[:, sl], first, 4))
                s = s * _sigmoid(s)
                if scale is not None:
                    s = s * (lax.rsqrt(jnp.sum(s * s, axis=-1, keepdims=True) + EPS) * scale)
                out_ref[:, _strip(h)] = s
        gb_ref[...] = _gdn_gates(p2_ref[...], alog_ref[...], dtb_ref[...])

    return pl.pallas_call(
        body, name="pre_fwd", grid=(t // tb,),
        in_specs=[_row(tb, 3 * D, 0), _prev(tb, 3 * D, 0, rows=16), _row(tb, 3 * D), _prev(tb, 3 * D), _row(tb, D),
                  _fixed((D, 128)), _fixed((8, D)), _fixed((8, 3 * D)), _fixed((1, 128)), _fixed((1, 128))],
        out_specs=[_row(tb, D), _row(tb, D), _row(tb, D), _row(tb, D), _row(tb, 128), _row(tb, 128)],
        out_shape=[jax.ShapeDtypeStruct((t, D), bf16), jax.ShapeDtypeStruct((t, D), f32),
                   jax.ShapeDtypeStruct((t, D), f32), jax.ShapeDtypeStruct((t, D), f32),
                   jax.ShapeDtypeStruct((t, 128), f32), jax.ShapeDtypeStruct((t, 128), f32)],
        compiler_params=_params(1),
    )(pg, pg, pq, pq, h1, w2, wa, wg, alog, dtb)


_Z_COL, _GA_COL, _GB_COL = 3, 4, 5


def _post_fwd(o, pg, gn):
    t = o.shape[0]
    tb = _pick(t, (256, 128))

    def body(o_ref, z_ref, gn_ref, yb_ref):
        for h in range(H):
            sl = slice(h * DH, (h + 1) * DH)
            oh = o_ref[:, sl]
            z = _f32(z_ref, sl)
            r = lax.rsqrt(jnp.mean(oh * oh, axis=-1, keepdims=True) + EPS)
            yb_ref[:, sl] = (oh * r * gn_ref[...] * (z * _sigmoid(z))).astype(bf16)

    return pl.pallas_call(
        body, name="post_fwd", grid=(t // tb,), in_specs=[_row(tb, D), _row(tb, D, _Z_COL), _fixed((1, DH))],
        out_specs=_row(tb, D), out_shape=jax.ShapeDtypeStruct((t, D), bf16), compiler_params=_params(1),
    )(o, pg, gn)


def _post_bwd(dyb, o, pg, gn):
    t = o.shape[0]
    tb = _pick(t, (256, 128))

    def body(dyb_ref, o_ref, z_ref, gn_ref, do_ref, dz_ref, dgn_ref):
        @pl.when(pl.program_id(0) == 0)
        def _():
            dgn_ref[...] = jnp.zeros_like(dgn_ref)

        gn_v = gn_ref[...]
        acc = jnp.zeros((8, DH), f32)
        for h in range(H):
            sl = slice(h * DH, (h + 1) * DH)
            oh = o_ref[:, sl]
            z = _f32(z_ref, sl)
            dy = dyb_ref[:, sl]
            r = lax.rsqrt(jnp.mean(oh * oh, axis=-1, keepdims=True) + EPS)
            on = oh * r
            sg = _sigmoid(z)
            sz = z * sg
            don = dy * sz
            dz_ref[:, sl] = (dy * on * gn_v * (sg * (1.0 + z * (1.0 - sg)))).astype(bf16)
            acc = acc + jnp.sum((don * on).reshape(tb // 8, 8, DH), axis=0)
            doh = don * gn_v
            do_ref[:, sl] = r * (doh - on * jnp.mean(doh * on, axis=-1, keepdims=True))
        dgn_ref[...] += acc

    return pl.pallas_call(
        body, name="post_bwd", grid=(t // tb,),
        in_specs=[_row(tb, D), _row(tb, D), _row(tb, D, _Z_COL), _fixed((1, DH))],
        out_specs=[_row(tb, D), _row(tb, D), _fixed((8, DH))],
        out_shape=[jax.ShapeDtypeStruct((t, D), f32), jax.ShapeDtypeStruct((t, D), bf16),
                   jax.ShapeDtypeStruct((8, DH), f32)],
        compiler_params=_params(1),
    )(dyb, o, pg, gn)


def _mix_fwd(ya, yb, pg):
    t = ya.shape[0]
    tb = _pick(t, (256, 128))

    def body(ya_ref, yb_ref, ga_ref, gb_ref, mix_ref):
        ya_v, yb_v = ya_ref[...].astype(f32), yb_ref[...].astype(f32)
        mix = _sigmoid(ga_ref[...].astype(f32)) * ya_v + _sigmoid(gb_ref[...].astype(f32)) * yb_v
        mix_ref[...] = mix.astype(bf16)

    return pl.pallas_call(
        body, name="mix_fwd", grid=(t // tb,),
        in_specs=[_row(tb, D), _row(tb, D), _row(tb, D, _GA_COL), _row(tb, D, _GB_COL)],
        out_specs=_row(tb, D), out_shape=jax.ShapeDtypeStruct((t, D), bf16), compiler_params=_params(1),
    )(ya, yb, pg, pg)


def _mix_bwd(dmix, ya, yb, pg):
    t = ya.shape[0]
    tb = _pick(t, (256, 128))

    def body(dm_ref, ya_ref, yb_ref, ga_ref, gb_ref, dya_ref, dyb_ref, dg_ref):
        dm = dm_ref[...].astype(f32)
        sa = _sigmoid(ga_ref[...].astype(f32))
        sb = _sigmoid(gb_ref[...].astype(f32))
        dya_ref[...] = (dm * sa).astype(bf16)
        dyb_ref[...] = (dm * sb).astype(bf16)
        dg_ref[:, :D] = (dm * ya_ref[...].astype(f32) * sa * (1.0 - sa)).astype(bf16)
        dg_ref[:, D:] = (dm * yb_ref[...].astype(f32) * sb * (1.0 - sb)).astype(bf16)

    return pl.pallas_call(
        body, name="mix_bwd", grid=(t // tb,),
        in_specs=[_row(tb, D), _row(tb, D), _row(tb, D), _row(tb, D, _GA_COL), _row(tb, D, _GB_COL)],
        out_specs=[_row(tb, D), _row(tb, D), _row(tb, 2 * D)],
        out_shape=[jax.ShapeDtypeStruct((t, D), bf16), jax.ShapeDtypeStruct((t, D), bf16),
                   jax.ShapeDtypeStruct((t, 2 * D), bf16)],
        compiler_params=_params(1),
    )(dmix, ya, yb, pg, pg)


def _ffn_fwd(up, wf):
    t = up.shape[0]
    tb = 128

    def body(up_ref, uph_ref, wf_ref, act_ref):
        first = pl.program_id(0) == 0
        for i in range(DFF // 128):
            g, v = _strip(i), _strip(i, DFF)
            gate = _strip_conv(wf_ref, g, _strip_taps(_f32(up_ref, g), _halo_before(uph_ref, g), first, 3))
            val = _strip_conv(wf_ref, v, _strip_taps(_f32(up_ref, v), _halo_before(uph_ref, v), first, 3))
            act_ref[:, g] = (gate * _sigmoid(gate) * val).astype(bf16)

    return pl.pallas_call(
        body, name="ffn_fwd", grid=(t // tb,),
        in_specs=[_row(tb, 2 * DFF), _prev(tb, 2 * DFF, rows=16), _fixed((8, 2 * DFF))],
        out_specs=_row(tb, DFF), out_shape=jax.ShapeDtypeStruct((t, DFF), bf16), compiler_params=_params(1),
    )(up, up, wf)


def _ffn_bwd1(dact, up, wf):
    t = up.shape[0]
    tb = 128

    def body(da_ref, up_ref, uph_ref, wf_ref, dc_ref, dw_ref):
        @pl.when(pl.program_id(0) == 0)
        def _():
            dw_ref[...] = jnp.zeros_like(dw_ref)

        first = pl.program_id(0) == 0
        for i in range(DFF // 128):
            g, v = _strip(i), _strip(i, DFF)
            g_taps = _strip_taps(_f32(up_ref, g), _halo_before(uph_ref, g), first, 3)
            v_taps = _strip_taps(_f32(up_ref, v), _halo_before(uph_ref, v), first, 3)
            gate = _strip_conv(wf_ref, g, g_taps)
            val = _strip_conv(wf_ref, v, v_taps)
            sg = _sigmoid(gate)
            da = _f32(da_ref, g)
            dgate = da * val * (sg * (1.0 + gate * (1.0 - sg)))
            dval = da * (gate * sg)
            dc_ref[:, g] = dgate.astype(bf16)
            dc_ref[:, v] = dval.astype(bf16)
            _strip_weight_grad(dw_ref, g, dgate, g_taps)
            _strip_weight_grad(dw_ref, v, dval, v_taps)

    return pl.pallas_call(
        body, name="ffn_bwd1", grid=(t // tb,),
        in_specs=[_row(tb, DFF), _row(tb, 2 * DFF), _prev(tb, 2 * DFF, rows=16), _fixed((8, 2 * DFF))],
        out_specs=[_row(tb, 2 * DFF), _fixed((8, 2 * DFF))],
        out_shape=[jax.ShapeDtypeStruct((t, 2 * DFF), bf16), jax.ShapeDtypeStruct((8, 2 * DFF), f32)],
        compiler_params=_params(1),
    )(dact, up, up, wf)


def _ffn_bwd2(dc, wf):
    t = dc.shape[0]
    tb = 128
    nb = t // tb

    def body(dc_ref, dch_ref, wf_ref, dup_ref):
        last = pl.program_id(0) == nb - 1
        for i in range(2 * DFF // 128):
            sl = _strip(i)
            dup_ref[:, sl] = _strip_conv_up(_f32(dc_ref, sl), _halo_after(dch_ref, sl), last, wf_ref, sl, 3).astype(bf16)

    return pl.pallas_call(
        body, name="ffn_bwd2", grid=(nb,),
        in_specs=[_row(tb, 2 * DFF), _next(tb, 2 * DFF, t, rows=16), _fixed((8, 2 * DFF))],
        out_specs=_row(tb, 2 * DFF), out_shape=jax.ShapeDtypeStruct((t, 2 * DFF), bf16), compiler_params=_params(1),
    )(dc, dc, wf)


def _final(x3, tgt, g):
    t = x3.shape[0]
    tb = _pick(t, (256, 128))

    def body(x_ref, t_ref, g_ref, loss_ref, dx_ref, dxb_ref, dg_ref):
        @pl.when(pl.program_id(0) == 0)
        def _():
            loss_ref[...] = jnp.zeros_like(loss_ref)
            dg_ref[...] = jnp.zeros_like(dg_ref)

        xv = x_ref[...]
        r = lax.rsqrt(jnp.mean(xv * xv, axis=-1, keepdims=True) + EPS)
        xh = xv * r
        gv = g_ref[...]
        e = xh * gv - t_ref[...]
        lrow = 0.5 * jnp.mean(e * e, axis=-1, keepdims=True)
        loss_ref[...] += jnp.sum(jnp.broadcast_to(lrow, (tb, 128)).reshape(tb // 8, 8, 128), axis=0)
        dy = e * (1.0 / D)
        dyg = dy * gv
        dx = r * (dyg - xh * jnp.mean(dyg * xh, axis=-1, keepdims=True))
        dx_ref[...] = dx
        dxb_ref[...] = dx.astype(bf16)
        dg_ref[...] += jnp.sum((dy * xh).reshape(tb // 8, 8, D), axis=0)

    return pl.pallas_call(
        body, name="final", grid=(t // tb,), in_specs=[_row(tb, D), _row(tb, D), _fixed((1, D))],
        out_specs=[_fixed((8, 128)), _row(tb, D), _row(tb, D), _fixed((8, D))],
        out_shape=[jax.ShapeDtypeStruct((8, 128), f32), jax.ShapeDtypeStruct((t, D), f32),
                   jax.ShapeDtypeStruct((t, D), bf16), jax.ShapeDtypeStruct((8, D), f32)],
        compiler_params=_params(1),
    )(x3, tgt, g)


def _pre_bwd1(pg, pq, p2, dya_in, dqn, dkn, dvc, dgb, gbeta, h1, wa, wg, alog, dtb):
    t = pg.shape[0]
    tb = 128

    def body(p0_ref, p0h_ref, pq_ref, pqh_ref, p2_ref, dya_ref, dqn_ref, dkn_ref, dvc_ref, dgb_ref, gb_ref, h1_ref,
             wa_ref, wg_ref, alog_ref, dtb_ref,
             dbg_ref, dca_ref, dc4_ref, dp2_ref, dwa_ref, dwg_ref, dal_ref, ddt_ref, dw2_ref):
        @pl.when(pl.program_id(0) == 0)
        def _():
            dwa_ref[...] = jnp.zeros_like(dwa_ref)
            dwg_ref[...] = jnp.zeros_like(dwg_ref)
            dal_ref[...] = jnp.zeros_like(dal_ref)
            ddt_ref[...] = jnp.zeros_like(ddt_ref)
            dw2_ref[...] = jnp.zeros_like(dw2_ref)

        first = pl.program_id(0) == 0

        for i in range(D // 128):
            sl, cg, xv = _strip(i), _strip(i, D), _strip(i, 2 * D)
            taps = _strip_taps(_f32(p0_ref, cg) * _f32(p0_ref, xv), _halo_before(p0h_ref, cg) * _halo_before(p0h_ref, xv),
                               first, 3)
            dya = _f32(dya_ref, sl)
            dbg_ref[:, sl] = (dya * _strip_conv(wa_ref, sl, taps)).astype(bf16)
            dca = dya * _f32(p0_ref, sl)
            dca_ref[:, sl] = dca.astype(bf16)
            _strip_weight_grad(dwa_ref, sl, dca, taps)

        for part, d_ref, scale in ((0, dqn_ref, DH ** -0.5), (1, dkn_ref, 1.0), (2, dvc_ref, None)):
            for h in range(H):
                sl = _strip(h, part * D)
                taps = _strip_taps(pq_ref[:, sl], pqh_ref[:, sl], first, 4)
                c4 = _strip_conv(wg_ref, sl, taps)
                sg = _sigmoid(c4)
                dn = d_ref[:, _strip(h)]
                if scale is not None:
                    a = c4 * sg
                    r = lax.rsqrt(jnp.sum(a * a, axis=-1, keepdims=True) + EPS)
                    an = a * r
                    dn = dn * scale
                    dn = r * (dn - an * jnp.sum(dn * an, axis=-1, keepdims=True))
                dc4 = dn * (sg * (1.0 + c4 * (1.0 - sg)))
                dc4_ref[:, sl] = dc4.astype(bf16)
                _strip_weight_grad(dwg_ref, sl, dc4, taps)

        ab = p2_ref[...]
        lane = lax.broadcasted_iota(jnp.int32, ab.shape, 1)
        dgbv = dgb_ref[...]
        gbv = gb_ref[...]
        da = dgbv * (-jnp.exp(alog_ref[...])) * _sigmoid(ab + dtb_ref[...])
        db = dgbv * gbv * (1.0 - gbv)
        dp2 = jnp.where(lane < H, da, jnp.where(lane < 2 * H, db, 0.0)).astype(bf16)
        dp2_ref[...] = dp2
        dw2_ref[...] += lax.dot_general(dp2, h1_ref[...], (((0,), (0,)), ((), ())), preferred_element_type=f32)
        dal = jnp.where(lane < H, dgbv * gbv, 0.0)
        ddt = jnp.where(lane < H, da, 0.0)
        dal_ref[...] += jnp.sum(dal.reshape(tb // 8, 8, 128), axis=0)
        ddt_ref[...] += jnp.sum(ddt.reshape(tb // 8, 8, 128), axis=0)

    return pl.pallas_call(
        body, name="pre_bwd1", grid=(t // tb,),
        in_specs=[_row(tb, 3 * D, 0), _prev(tb, 3 * D, 0, rows=16), _row(tb, 3 * D), _prev(tb, 3 * D), _row(tb, 128),
                  _row(tb, D), _row(tb, D), _row(tb, D), _row(tb, D), _row(tb, 128), _row(tb, 128), _row(tb, D),
                  _fixed((8, D)), _fixed((8, 3 * D)), _fixed((1, 128)), _fixed((1, 128))],
        out_specs=[_row(tb, D), _row(tb, D), _row(tb, 3 * D), _row(tb, 128),
                   _fixed((8, D)), _fixed((8, 3 * D)), _fixed((8, 128)), _fixed((8, 128)), _fixed((128, D))],
        out_shape=[jax.ShapeDtypeStruct((t, D), bf16), jax.ShapeDtypeStruct((t, D), bf16),
                   jax.ShapeDtypeStruct((t, 3 * D), bf16), jax.ShapeDtypeStruct((t, 128), bf16),
                   jax.ShapeDtypeStruct((8, D), f32), jax.ShapeDtypeStruct((8, 3 * D), f32),
                   jax.ShapeDtypeStruct((8, 128), f32), jax.ShapeDtypeStruct((8, 128), f32),
                   jax.ShapeDtypeStruct((128, D), f32)],
        compiler_params=_params(1),
    )(pg, pg, pq, pq, p2, dya_in, dqn, dkn, dvc, dgb, gbeta, h1, wa, wg, alog, dtb)


def _pre_bwd2(dca, dc4, pg, dbg, dz, dgates, wa, wg, exchange=None):
    t = pg.shape[0]
    tb = 128
    nb = t // tb

    def body(dca_ref, dcah_ref, dc4_ref, dc4h_ref, p0_ref, dbg_ref, dz_ref, dgt_ref, wa_ref, wg_ref, dp_ref):
        last = pl.program_id(0) == nb - 1
        dp_ref[:, :D] = dbg_ref[...]
        for i in range(D // 128):
            sl, cg, xv = _strip(i), _strip(i, D), _strip(i, 2 * D)
            du = _strip_conv_up(_f32(dca_ref, sl), _halo_after(dcah_ref, sl), last, wa_ref, sl, 3)
            dp_ref[:, cg] = (du * _f32(p0_ref, xv)).astype(bf16)
            dp_ref[:, xv] = (du * _f32(p0_ref, cg)).astype(bf16)
        dp_ref[:, 3 * D:4 * D] = dz_ref[...]
        dp_ref[:, 4 * D:6 * D] = dgt_ref[...]
        for i in range(3 * D // 128):
            sl = _strip(i)
            dq = _strip_conv_up(_f32(dc4_ref, sl), _halo_after(dc4h_ref, sl), last, wg_ref, sl, 4)
            dp_ref[:, _strip(i, 6 * D)] = dq.astype(bf16)

    return _call_with_exchange(
        body, exchange, name="pre_bwd2", grid=(nb,),
        in_specs=[_row(tb, D), _next(tb, D, t, rows=16), _row(tb, 3 * D), _next(tb, 3 * D, t, rows=16), _row(tb, 3 * D, 0),
                  _row(tb, D), _row(tb, D), _row(tb, 2 * D), _fixed((8, D)), _fixed((8, 3 * D))],
        out_specs=_row(tb, NW1), out_shape=jax.ShapeDtypeStruct((t, NW1), bf16),
        args=[dca, dca, dc4, dc4, pg, dbg, dz, dgates, wa, wg])


def _chunk_consts():
    r = lax.broadcasted_iota(jnp.int32, (CH, CH), 0)
    c = lax.broadcasted_iota(jnp.int32, (CH, CH), 1)
    return r, c, (r == c).astype(f32)


def _tri_inverse(lows, eye, r, c):
    def same_block(b):
        return jnp.bitwise_xor(r, c) < b

    xs = [jnp.where(same_block(8), -low, 0.0) for low in lows]
    ts = [eye + x for x in xs]
    for _ in range(2):
        xs = [_idot(x, x) for x in xs]
        ts = [t + _idot(t, x) for t, x in zip(ts, xs)]
    for b in (8, 16, 32):
        below = same_block(2 * b) & jnp.logical_not(same_block(b))
        ts = [t - _idot(_idot(t, jnp.where(below, low, 0.0)), t) for t, low in zip(ts, lows)]
    return ts


def _chunk_common(q, k, v, gcol, bcol, r, c, eye):
    grow = jnp.sum(eye * gcol, axis=0, keepdims=True)
    dec = jnp.exp(jnp.where(r >= c, gcol - grow, -jnp.inf))
    rcol = lax.broadcasted_iota(jnp.int32, (CH, 1), 0)
    glast = jnp.sum(jnp.where(rcol == CH - 1, gcol, 0.0), axis=0, keepdims=True)
    eg = jnp.exp(gcol)
    el = jnp.exp(glast - gcol)
    kb = k * bcol
    vb = v * bcol
    kk = _bdot_nt(kb, k)
    low = jnp.where(r > c, kk * dec, 0.0)
    qk = _bdot_nt(q, k)
    att = qk * dec
    return grow, dec, glast, eg, el, kb, vb, kk, low, qk, att, rcol


def _gdn_fwd(qn, kn, vc, gbeta):
    t = qn.shape[0]
    n_chunks = t // CH

    def body(q_ref, k_ref, v_ref, gb_ref, o_ref, s_ref, t_ref, state):
        @pl.when(pl.program_id(0) == 0)
        def _():
            state[...] = jnp.zeros_like(state)

        r, c, eye = _chunk_consts()
        tri = (r >= c).astype(f32)
        heads = range(H)
        keys = [(s, h) for s in range(GDN_STEP) for h in heads]
        rows = [slice(s * CH, (s + 1) * CH) for s in range(GDN_STEP)]
        gbs = [gb_ref[rows[s], :] for s in range(GDN_STEP)]
        galls = [_hdot(tri, gb) for gb in gbs]
        qs = {(s, h): q_ref[rows[s], h * DH:(h + 1) * DH] for s, h in keys}
        ks = {(s, h): k_ref[rows[s], h * DH:(h + 1) * DH] for s, h in keys}
        cm = {(s, h): _chunk_common(qs[s, h], ks[s, h], v_ref[rows[s], h * DH:(h + 1) * DH], galls[s][:, h:h + 1],
                                    gbs[s][:, H + h:H + h + 1], r, c, eye) for s, h in keys}
        invs = dict(zip(keys, _tri_inverse([cm[key][8] for key in keys], eye, r, c)))
        uws = {key: _bdot(invs[key], jnp.concatenate([cm[key][6], cm[key][5] * cm[key][3]], axis=1)) for key in keys}
        sts = [state[h] for h in heads]
        for s in range(GDN_STEP):
            vns = [uws[s, h][:, :DH] - _bdot(uws[s, h][:, DH:], sts[h]) for h in heads]
            outs = [_bdot(qs[s, h] * cm[s, h][3], sts[h]) + _bdot(cm[s, h][10], vns[h]) for h in heads]
            news = [sts[h] * jnp.exp(cm[s, h][2]) + _bdot_tn(ks[s, h] * cm[s, h][4], vns[h]) for h in heads]
            for h in heads:
                s_ref[s, h] = sts[h].astype(bf16)
                t_ref[s, h] = invs[s, h]
                o_ref[rows[s], h * DH:(h + 1) * DH] = outs[h]
            sts = news
        for h in heads:
            state[h] = sts[h]

    tb = GDN_STEP * CH
    return pl.pallas_call(
        body, name="gdn_fwd", grid=(t // tb,),
        in_specs=[_row(tb, D), _row(tb, D), _row(tb, D), _row(tb, 128)],
        out_specs=[_row(tb, D), pl.BlockSpec((GDN_STEP, H, DH, DH), lambda i: (i, 0, 0, 0)),
                   pl.BlockSpec((GDN_STEP, H, CH, CH), lambda i: (i, 0, 0, 0))],
        out_shape=[jax.ShapeDtypeStruct((t, D), f32), jax.ShapeDtypeStruct((n_chunks, H, DH, DH), bf16),
                   jax.ShapeDtypeStruct((n_chunks, H, CH, CH), f32)],
        scratch_shapes=[pltpu.VMEM((H, DH, DH), f32)],
        compiler_params=_params(1),
    )(qn, kn, vc, gbeta)


def _gdn_bwd(qn, kn, vc, gbeta, do, s_all, t_all):
    t = qn.shape[0]

    def body(q_ref, k_ref, v_ref, gb_ref, do_ref, s_ref, t_ref, dq_ref, dk_ref, dv_ref, dgb_ref, dstate):
        @pl.when(pl.program_id(0) == 0)
        def _():
            dstate[...] = jnp.zeros_like(dstate)

        r, c, eye = _chunk_consts()
        tril = r >= c
        lane = lax.broadcasted_iota(jnp.int32, (1, 128), 1)
        hs = range(H)

        def each(fn, *lists):
            return [fn(*args) for args in zip(*lists)]

        def rsum(a):
            return jnp.sum(a, axis=1, keepdims=True)

        def before_state(s):
            rows = slice(s * CH, (s + 1) * CH)
            gb = gb_ref[rows, :]
            gall = _hdot(tril.astype(f32), gb)
            p = {"rows": rows}
            p["q"] = q = [q_ref[rows, h * DH:(h + 1) * DH] for h in hs]
            p["k"] = k = [k_ref[rows, h * DH:(h + 1) * DH] for h in hs]
            p["v"] = v = [v_ref[rows, h * DH:(h + 1) * DH] for h in hs]
            p["dout"] = dout = [do_ref[rows, h * DH:(h + 1) * DH] for h in hs]
            p["inv"] = inv = [t_ref[s, h] for h in hs]
            p["st"] = st = [s_ref[s, h] for h in hs]
            p["bcol"] = bcol = [gb[:, H + h:H + h + 1] for h in hs]
            cm = [_chunk_common(q[h], k[h], v[h], gall[:, h:h + 1], bcol[h], r, c, eye) for h in hs]
            for name, i in (("dec", 1), ("glast", 2), ("eg", 3), ("el", 4), ("kb", 5), ("vb", 6), ("low", 8), ("att", 10)):
                p[name] = [m[i] for m in cm]
            p["rcol"] = cm[0][11]
            p["elast"] = each(jnp.exp, p["glast"])
            p["kbg"] = each(jnp.multiply, p["kb"], p["eg"])
            uw = each(lambda i, a, b: _bdot(i, jnp.concatenate([a, b], axis=1)), inv, p["vb"], p["kbg"])
            p["u"] = [a[:, :DH] for a in uw]
            p["w"] = [a[:, DH:] for a in uw]
            p["vn"] = each(lambda a, b, x: a - _bdot(b, x), p["u"], p["w"], st)
            p["qd"] = each(jnp.multiply, q, p["eg"])
            p["kd"] = each(jnp.multiply, k, p["el"])
            p["dqd"] = each(_bdot_nt, dout, st)
            p["datt"] = each(lambda d, x: jnp.where(tril, _bdot_nt(d, x), 0.0), dout, p["vn"])
            p["dqk"] = each(jnp.multiply, p["datt"], p["dec"])
            p["qd_do"] = each(_bdot_tn, p["qd"], dout)
            p["att_do"] = each(_bdot_tn, p["att"], dout)
            return p

        def after_state(p, ds):
            q, k, v, st, inv, bcol = p["q"], p["k"], p["v"], p["st"], p["inv"], p["bcol"]
            eg, el, kb, u, w = p["eg"], p["el"], p["kb"], p["u"], p["w"]
            dvn = each(lambda a, kk, x: a + _bdot(kk, x), p["att_do"], p["kd"], ds)
            dkd = each(_bdot_nt, p["vn"], ds)
            dw = each(lambda a, x: -_bdot_nt(a, x), dvn, st)
            new_ds = each(lambda x, e, a, ww, dv_: x * e + a - _bdot_tn(ww, dv_), ds, p["elast"], p["qd_do"], w, dvn)
            dglast = each(lambda e, x, d: e * jnp.sum(rsum(x.astype(f32) * d), axis=0, keepdims=True), p["elast"], st, ds)
            dr = each(lambda i, a, b: _bdot_tn(i, jnp.concatenate([a, b], axis=1)), inv, dvn, dw)
            dvb = [a[:, :DH] for a in dr]
            dkbg = [a[:, DH:] for a in dr]
            dlow = each(lambda a, b, x, y: -jnp.where(r > c, _bdot_nt(a, b) + _bdot_nt(x, y), 0.0), dvb, u, dkbg, w)
            dkk = each(jnp.multiply, dlow, p["dec"])
            mm = each(lambda a, b, x, y: a * b + x * y, dlow, p["low"], p["datt"], p["att"])
            dkb = each(lambda a, kk, b, e: _bdot(a, kk) + b * e, dkk, k, dkbg, eg)
            dk = each(lambda a, b, x, y, d, e, f, g: _bdot_tn(a, b) + _bdot_tn(x, y) + d * e + f * g,
                      dkk, kb, p["dqk"], q, dkd, el, dkb, bcol)
            dq = each(lambda a, kk, d, e: _bdot(a, kk) + d * e, p["dqk"], k, p["dqd"], eg)
            dv = each(jnp.multiply, dvb, bcol)
            dbeta = each(lambda a, b, x, y: rsum(a * b) + rsum(x * y), dkb, k, dvb, v)
            deg = each(lambda a, b, x, y: rsum(a * b) + rsum(x * y), dkbg, kb, p["dqd"], q)
            delc = each(lambda a, b, e: rsum(a * b) * e, dkd, k, el)
            dgc = each(lambda m, a, e, d: rsum(m) - rsum(eye * jnp.sum(m, axis=0, keepdims=True)) + a * e - d,
                       mm, deg, eg, delc)
            dgc = each(lambda g, d, l: g + jnp.where(p["rcol"] == CH - 1, jnp.sum(d, axis=0, keepdims=True) + l, 0.0),
                       dgc, delc, dglast)
            dg_acc = jnp.zeros((CH, 128), f32)
            db_acc = jnp.zeros((CH, 128), f32)
            rows = p["rows"]
            for h in hs:
                dq_ref[rows, h * DH:(h + 1) * DH] = dq[h]
                dk_ref[rows, h * DH:(h + 1) * DH] = dk[h]
                dv_ref[rows, h * DH:(h + 1) * DH] = dv[h]
                dg_acc = dg_acc + dgc[h] * (lane == h).astype(f32)
                db_acc = db_acc + dbeta[h] * (lane == H + h).astype(f32)
            dgb_ref[rows, :] = _hdot((r <= c).astype(f32), dg_acc) + db_acc
            return new_ds

        order = list(reversed(range(GDN_STEP)))
        pre = [before_state(s) for s in order]
        ds = [dstate[h] for h in hs]
        for p in pre:
            ds = after_state(p, ds)
        for h in hs:
            dstate[h] = ds[h]

    tb = GDN_STEP * CH
    n_steps = t // tb
    rev = lambda i: (n_steps - 1 - i, 0)
    rev4 = lambda i: (n_steps - 1 - i, 0, 0, 0)
    return pl.pallas_call(
        body, name="gdn_bwd", grid=(n_steps,),
        in_specs=[pl.BlockSpec((tb, D), rev), pl.BlockSpec((tb, D), rev), pl.BlockSpec((tb, D), rev),
                  pl.BlockSpec((tb, 128), rev), pl.BlockSpec((tb, D), rev),
                  pl.BlockSpec((GDN_STEP, H, DH, DH), rev4), pl.BlockSpec((GDN_STEP, H, CH, CH), rev4)],
        out_specs=[pl.BlockSpec((tb, D), rev), pl.BlockSpec((tb, D), rev), pl.BlockSpec((tb, D), rev),
                   pl.BlockSpec((tb, 128), rev)],
        out_shape=[jax.ShapeDtypeStruct((t, D), f32)] * 3 + [jax.ShapeDtypeStruct((t, 128), f32)],
        scratch_shapes=[pltpu.VMEM((H, DH, DH), f32)],
        compiler_params=_params(1),
    )(qn, kn, vc, gbeta, do, s_all, t_all)


def _pad_rows(w, rows=8):
    return jnp.pad(w, ((0, rows - w.shape[0]), (0, 0)))


_REST = ("w_up", "w_a_out", "w_b_out", "w_o", "w_down")


def _local_step(x, tgt, w, comm=None):
    g1 = w["norm_mix_g"].reshape(1, D)
    if comm is None:
        h1 = _rms_fwd(x, g1, name="rms1_fwd")
    else:
        h1, gathered = _rms_fwd(x, g1, name="rms1_fwd", exchange=comm.gather_first())
        w = {**w, **comm.finish_first(gathered)}
    w1, w2 = w["w1"], w["w2"]
    wa = _pad_rows(w["conv_a_w"])
    wg = _pad_rows(w["gdn_conv_w"])
    wf = _pad_rows(w["ffn_conv_w"])
    alog = jnp.pad(w["gdn_A_log"].reshape(1, H), ((0, 0), (0, 128 - H)))
    dtb = jnp.pad(w["gdn_dt_bias"].reshape(1, H), ((0, 0), (0, 128 - H)))
    g2 = w["norm_ffn_g"].reshape(1, D)
    g3 = w["norm_final_g"].reshape(1, D)
    gn = w["gdn_norm_g"].reshape(1, DH)

    if comm is None:
        pg = _matmul(h1, w1, name="mm_in", cols=(0, 6 * D), out_dtype=bf16)
    else:
        pg, gathered = _matmul(h1, w1, name="mm_in", cols=(0, 6 * D), out_dtype=bf16, exchange=comm.gather_rest())
        w = {**w, **comm.finish_gather(gathered)}
    pq = _matmul(h1, w1, name="mm_in_qkv", cols=(6 * D, 3 * D))
    ya_in, qn, kn, vc, gbeta, p2 = _pre_fwd(pg, pq, h1, w2, wa, wg, alog, dtb)
    o, s_all, t_all = _gdn_fwd(qn, kn, vc, gbeta)
    yb_in = _post_fwd(o, pg, gn)
    ya = _matmul(ya_in, w["w_a_out"], name="mm_a", out_dtype=bf16)
    yb = _matmul(yb_in, w["w_b_out"], name="mm_b", out_dtype=bf16)
    mix = _mix_fwd(ya, yb, pg)
    x2 = _matmul(mix, w["w_o"], name="mm_o", add=x)
    h2 = _rms_fwd(x2, g2, name="rms2_fwd")
    up = _matmul(h2, w["w_up"], nt=True, name="mm_up", tn=DFF // 2, out_dtype=bf16)
    act = _ffn_fwd(up, wf)
    x3 = _matmul(act, w["w_down"], name="mm_down", add=x2, tm=512)
    loss_p, dx3, dx3b, dg3 = _final(x3, tgt, g3)

    grads = {"norm_final_g": dg3}
    dact = _matmul(dx3b, w["w_down"], nt=True, name="mm_down_dx", tm=512, tn=DFF, out_dtype=bf16)
    grads["w_down"] = _matmul_tn(act, dx3b, name="mm_down_dw", tm=DFF // 2)
    dc, dwf = _ffn_bwd1(dact, up, wf)
    grads["ffn_conv_w"] = dwf
    dup = _ffn_bwd2(dc, wf)
    dh2 = _matmul(dup, w["w_up"], name="mm_up_dx", tk=DFF)
    grads["w_up"] = _matmul_tn(dup, h2, name="mm_up_dw", tm=DFF // 2)
    dx2, dx2b, dg2 = _rms_bwd(dh2, x2, g2, dx3, name="rms2_bwd")
    grads["norm_ffn_g"] = dg2
    dmix = _matmul(dx2b, w["w_o"], nt=True, name="mm_o_dx", out_dtype=bf16)
    grads["w_o"] = _matmul_tn(mix, dx2b, name="mm_o_dw")
    dya, dyb, dgates = _mix_bwd(dmix, ya, yb, pg)
    dya_in = _matmul(dya, w["w_a_out"], nt=True, name="mm_a_dx", out_dtype=bf16)
    grads["w_a_out"] = _matmul_tn(ya_in, dya, name="mm_a_dw")
    dyb_in = _matmul(dyb, w["w_b_out"], nt=True, name="mm_b_dx")
    grads["w_b_out"] = _matmul_tn(yb_in, dyb, name="mm_b_dw")
    do, dz, dgn = _post_bwd(dyb_in, o, pg, gn)
    grads["gdn_norm_g"] = dgn
    dqn, dkn, dvc, dgb = _gdn_bwd(qn, kn, vc, gbeta, do, s_all, t_all)
    dbg, dca, dc4, dp2, dwa, dwg, dal, ddt, grads["w2"] = _pre_bwd1(pg, pq, p2, dya_in, dqn, dkn, dvc, dgb, gbeta, h1,
                                                                    wa, wg, alog, dtb)
    grads["conv_a_w"] = dwa
    grads["gdn_conv_w"] = dwg
    grads["gdn_A_log"] = dal
    grads["gdn_dt_bias"] = ddt
    if comm is None:
        dp1 = _pre_bwd2(dca, dc4, pg, dbg, dz, dgates, wa, wg)
        grads["w1"] = _matmul_tn(dp1, h1, name="mm_in_dw")
        dh1 = _matmul(dp1, w1, nt=True, name="mm_in_dx", tm=512, tk=NW1 // 2)
    else:
        exchange, blocks = comm.reduce_halves(_REST, grads)
        dp1, recv = _pre_bwd2(dca, dc4, pg, dbg, dz, dgates, wa, wg, exchange=exchange)
        exchange, sums = comm.reduce_sums(_REST, blocks, recv)
        grads["w1"], recv = _matmul_tn(dp1, h1, name="mm_in_dw", exchange=exchange)
        comm.finish_reduce(_REST, sums, recv)
        exchange, blocks = comm.reduce_halves(("w_in",), grads)
        exchange, sums = comm.reduce_sums(("w_in",), blocks, _run_exchange(exchange, name="rs_sibling_w_in"))
        dh1, recv = _matmul(dp1, w1, nt=True, name="mm_in_dx", tm=512, tk=NW1 // 2, exchange=exchange)
        comm.finish_reduce(("w_in",), sums, recv)
    dx, _, dg1 = _rms_bwd(dh1, x, g1, dx2, name="rms1_bwd", more=(dp2, w2))
    grads["norm_mix_g"] = dg1
    return loss_p, dx, grads


_ANY = pl.BlockSpec(memory_space=pl.ANY)


def _remote(src, dst, send_sem, recv_sem, to):
    return pltpu.make_async_remote_copy(src_ref=src, dst_ref=dst, send_sem=send_sem, recv_sem=recv_sem,
                                        device_id=to, device_id_type=MESH)


def _run_exchange(exchange, *, name):
    arrays, shapes, sems, start, wait = exchange
    n_in, n_out = len(arrays), len(shapes)

    def body(*refs):
        start(refs[:n_in], refs[n_in:n_in + n_out], refs[n_in + n_out:])
        wait(refs[:n_in], refs[n_in:n_in + n_out], refs[n_in + n_out:])

    return pl.pallas_call(body, name=name, out_shape=list(shapes), in_specs=[_ANY] * n_in, out_specs=[_ANY] * n_out,
                          scratch_shapes=list(sems))(*arrays)


def _gather_exchange(shards):
    n = len(shards)

    def copies(x_refs, out_refs, sems):
        send_sems, recv_sems, local_sems = sems
        x, y, c = lax.axis_index("x"), lax.axis_index("y"), lax.axis_index("c")

        def flip(v, b):
            return v + b - 2 * v * b

        me, sibling = (x, y, c), (x, y, 1 - c)
        chip1, chip2, diag = (flip(x, 1 - c), flip(y, c)), (flip(x, c), flip(y, 1 - c)), (1 - x, 1 - y)

        def copy(a, k, blk, to, from_input=False):
            dst = out_refs[a].at[4 * blk[0] + 2 * blk[1] + blk[2]]
            return _remote(x_refs[a] if from_input else dst, dst, send_sems.at[a, k], recv_sems.at[a, k], to)

        mine = [pltpu.make_async_copy(x_refs[a], out_refs[a].at[4 * x + 2 * y + c], local_sems.at[a]) for a in range(n)]
        first = []
        for a in range(n):
            first += [copy(a, 0, me, sibling, from_input=True), copy(a, 1, me, (*chip1, c), from_input=True),
                      copy(a, 2, me, (*chip2, c), from_input=True)]
        return copy, mine, first, me, sibling, chip1, chip2, diag, c

    def start(x_refs, out_refs, sems):
        _, mine, first, *_ = copies(x_refs, out_refs, sems)
        for cp in mine + first:
            cp.start()

    def wait(x_refs, out_refs, sems):
        copy, mine, first, me, sibling, chip1, chip2, diag, c = copies(x_refs, out_refs, sems)
        passed = []

        def pass_on(cp):
            passed.append(cp)
            cp.start()

        for a in range(n):
            copy(a, 1, (*chip1, c), me).wait_recv()
            pass_on(copy(a, 3, (*chip1, c), (*chip2, c)))
            pass_on(copy(a, 4, (*chip1, c), sibling))
        for a in range(n):
            copy(a, 2, (*chip2, c), me).wait_recv()
            pass_on(copy(a, 5, (*chip2, c), sibling))
        for a in range(n):
            copy(a, 3, (*diag, c), me).wait_recv()
            pass_on(copy(a, 6, (*diag, c), sibling))
        for a in range(n):
            copy(a, 0, sibling, me).wait_recv()
            copy(a, 4, (*chip2, 1 - c), me).wait_recv()
            copy(a, 5, (*chip1, 1 - c), me).wait_recv()
            copy(a, 6, (*diag, 1 - c), me).wait_recv()
        for cp in first + passed:
            cp.wait_send()
        for cp in mine:
            cp.wait()

    shapes = [jax.ShapeDtypeStruct((N_DEV, *s.shape), s.dtype) for s in shards]
    sems = [pltpu.SemaphoreType.DMA((n, 7)), pltpu.SemaphoreType.DMA((n, 7)), pltpu.SemaphoreType.DMA((n,))]
    return shards, shapes, sems, start, wait


def _gather_direct_exchange(shards):
    n = len(shards)

    def copies(x_refs, out_refs, sems):
        send_sems, recv_sems, local_sems = sems
        x, y, c = lax.axis_index("x"), lax.axis_index("y"), lax.axis_index("c")
        targets = [(x, y, 1 - c), (1 - x, y, c), (x, 1 - y, c), (1 - x, 1 - y, c)]
        local, sends, recvs = [], [], []
        for a in range(n):
            mine = out_refs[a].at[4 * x + 2 * y + c]
            local.append(pltpu.make_async_copy(x_refs[a], mine, local_sems.at[a]))
            for k, to in enumerate(targets):
                theirs = out_refs[a].at[4 * to[0] + 2 * to[1] + to[2]]
                sends.append(_remote(x_refs[a], mine, send_sems.at[a, k], recv_sems.at[a, k], to))
                recvs.append(_remote(theirs, theirs, send_sems.at[a, k], recv_sems.at[a, k], to))
        return local, sends, recvs

    def start(x_refs, out_refs, sems):
        local, sends, _ = copies(x_refs, out_refs, sems)
        for cp in local + sends:
            cp.start()

    def wait(x_refs, out_refs, sems):
        local, sends, recvs = copies(x_refs, out_refs, sems)
        for cp in recvs:
            cp.wait_recv()
        for cp in sends:
            cp.wait_send()
        for cp in local:
            cp.wait()

    shapes = [jax.ShapeDtypeStruct((N_DEV, *s.shape), s.dtype) for s in shards]
    sems = [pltpu.SemaphoreType.DMA((n, 4)), pltpu.SemaphoreType.DMA((n, 4)), pltpu.SemaphoreType.DMA((n,))]
    return shards, shapes, sems, start, wait


def _gather_forward(gathered):
    n = len(gathered)

    def body(*refs):
        out_refs = refs[n:2 * n]
        send_sems, recv_sems = refs[2 * n:]
        x, y, c = lax.axis_index("x"), lax.axis_index("y"), lax.axis_index("c")
        sibling = (x, y, 1 - c)
        sends, recvs = [], []
        for a in range(n):
            for j, (px, py) in enumerate([(1 - x, y), (x, 1 - y), (1 - x, 1 - y)]):
                mine = out_refs[a].at[4 * px + 2 * py + c]
                theirs = out_refs[a].at[4 * px + 2 * py + 1 - c]
                sends.append(_remote(mine, mine, send_sems.at[a, j], recv_sems.at[a, j], sibling))
                recvs.append(_remote(theirs, theirs, send_sems.at[a, j], recv_sems.at[a, j], sibling))
        for cp in sends:
            cp.start()
        for cp in recvs:
            cp.wait_recv()
        for cp in sends:
            cp.wait_send()

    return pl.pallas_call(
        body, name="ag_forward", out_shape=[jax.ShapeDtypeStruct(g.shape, g.dtype) for g in gathered],
        in_specs=[_ANY] * n, out_specs=[_ANY] * n, input_output_aliases={a: a for a in range(n)},
        scratch_shapes=[pltpu.SemaphoreType.DMA((n, 3)), pltpu.SemaphoreType.DMA((n, 3))],
    )(*gathered)


def _chips_exchange(hsums):
    n = len(hsums)

    def copies(h_refs, out_refs, sems):
        send_sems, recv_sems = sems
        x, y, c = lax.axis_index("x"), lax.axis_index("y"), lax.axis_index("c")
        chips = [(1 - x, y), (x, 1 - y), (1 - x, 1 - y)]
        return [_remote(h_refs[a].at[2 * px + py], out_refs[a].at[k], send_sems.at[a, k], recv_sems.at[a, k], (px, py, c))
                for a in range(n) for k, (px, py) in enumerate(chips)]

    def start(h_refs, out_refs, sems):
        for cp in copies(h_refs, out_refs, sems):
            cp.start()

    def wait(h_refs, out_refs, sems):
        for cp in copies(h_refs, out_refs, sems):
            cp.wait()

    shapes = [jax.ShapeDtypeStruct((3, *h.shape[1:]), h.dtype) for h in hsums]
    sems = [pltpu.SemaphoreType.DMA((n, 3)), pltpu.SemaphoreType.DMA((n, 3))]
    return hsums, shapes, sems, start, wait


def _sibling_exchange(halves):
    n = len(halves)

    def copies(p_refs, out_refs, sems):
        send_sems, recv_sems = sems
        x, y, c = lax.axis_index("x"), lax.axis_index("y"), lax.axis_index("c")
        return [_remote(p_refs[a], out_refs[a], send_sems.at[a], recv_sems.at[a], (x, y, 1 - c)) for a in range(n)]

    def start(p_refs, out_refs, sems):
        for cp in copies(p_refs, out_refs, sems):
            cp.start()

    def wait(p_refs, out_refs, sems):
        for cp in copies(p_refs, out_refs, sems):
            cp.wait()

    shapes = [jax.ShapeDtypeStruct(h.shape, h.dtype) for h in halves]
    return halves, shapes, [pltpu.SemaphoreType.DMA((n,)), pltpu.SemaphoreType.DMA((n,))], start, wait


_IN_RANGES = ((0, 3 * D, 0, 0), (3 * D, 6 * D, 0, 6 * D), (6 * D, 7 * D, 0, 3 * D), (7 * D, 7 * D + 16, 1, 0),
              (7 * D + 16, 9 * D + 16, 0, 4 * D))


def _col_pieces(width, ranges):
    pieces = []
    for d in range(N_DEV):
        lo, hi = d * width, (d + 1) * width
        for glo, ghi, mat, mlo in ranges:
            a, b = max(lo, glo), min(hi, ghi)
            if a < b:
                pieces.append((d, a - lo, b - lo, mat, mlo + a - glo))
    return pieces


def _cols_to_matrices(g, ranges, out_widths, *, name):
    _, rows, width = g.shape
    tb = 128
    pieces = _col_pieces(width, ranges)
    covered = [sum(p[2] - p[1] for p in pieces if p[3] == m) for m in range(len(out_widths))]

    def body(g_ref, *o_refs):
        for m, o_ref in enumerate(o_refs):
            if covered[m] < out_widths[m]:
                o_ref[...] = jnp.zeros_like(o_ref)
        for d, b0, b1, m, m0 in pieces:
            o_refs[m][:, m0:m0 + b1 - b0] = g_ref[d, :, b0:b1]

    return pl.pallas_call(
        body, name=name, grid=(rows // tb,), in_specs=[pl.BlockSpec((N_DEV, tb, width), lambda i: (0, i, 0))],
        out_specs=[pl.BlockSpec((tb, wo), lambda i: (i, 0)) for wo in out_widths],
        out_shape=[jax.ShapeDtypeStruct((rows, wo), g.dtype) for wo in out_widths], compiler_params=_params(1),
    )(g)


def _transposed_matrices_to_blocks(mats, ranges, width, *, name):
    rows = mats[0].shape[1]
    pieces = _col_pieces(width, ranges)

    def body(*refs):
        m_refs, g_ref = refs[:-1], refs[-1]
        for d, b0, b1, m, m0 in pieces:
            g_ref[d, b0:b1, :] = m_refs[m][m0:m0 + b1 - b0, :]

    return pl.pallas_call(
        body, name=name, grid=(rows // 128,),
        in_specs=[pl.BlockSpec((mt.shape[0], 128), lambda i: (0, i)) for mt in mats],
        out_specs=pl.BlockSpec((N_DEV, width, 128), lambda i: (0, 0, i)),
        out_shape=jax.ShapeDtypeStruct((N_DEV, width, rows), mats[0].dtype), compiler_params=_params(1),
    )(*mats)


def _row_block(rows):
    return 128 if rows % 128 == 0 else rows


def _half_bf16(g4, c_other, *, name):
    _, _, rows, width = g4.shape
    tb = _row_block(rows)

    def body(c_ref, p_ref, o_ref):
        o_ref[0] = p_ref[0, 0].astype(bf16)

    grid_spec = pltpu.PrefetchScalarGridSpec(
        num_scalar_prefetch=1, grid=(4, rows // tb),
        in_specs=[pl.BlockSpec((1, 1, tb, width), lambda j, i, c_ref: (j, c_ref[0], i, 0))],
        out_specs=pl.BlockSpec((1, tb, width), lambda j, i, c_ref: (j, i, 0)))
    return pl.pallas_call(
        body, name=name, grid_spec=grid_spec, out_shape=jax.ShapeDtypeStruct((4, rows, width), bf16),
        compiler_params=_params(2),
    )(c_other, g4)


def _pair_sum(g4, recv, c_me, *, name):
    _, _, rows, width = g4.shape
    tb = _row_block(rows)

    def body(c_ref, p_ref, r_ref, o_ref, ob_ref):
        s = p_ref[0, 0] + r_ref[0].astype(f32)
        o_ref[0] = s
        ob_ref[0] = s.astype(bf16)

    blk = pl.BlockSpec((1, tb, width), lambda j, i, c_ref: (j, i, 0))
    grid_spec = pltpu.PrefetchScalarGridSpec(
        num_scalar_prefetch=1, grid=(4, rows // tb),
        in_specs=[pl.BlockSpec((1, 1, tb, width), lambda j, i, c_ref: (j, c_ref[0], i, 0)), blk],
        out_specs=[blk, blk])
    return pl.pallas_call(
        body, name=name, grid_spec=grid_spec,
        out_shape=[jax.ShapeDtypeStruct((4, rows, width), f32), jax.ShapeDtypeStruct((4, rows, width), bf16)],
        compiler_params=_params(2),
    )(c_me, g4, recv)


def _adam_shard(hsum, recv, chip, w, m, v, *, name):
    _, rows, width = w.shape
    tb = _row_block(rows)

    def body(j_ref, h_ref, r_ref, w_ref, m_ref, v_ref, g_out, d_out, m_out, v_out):
        g = ((h_ref[0] + r_ref[0].astype(f32)) + r_ref[1].astype(f32)) + r_ref[2].astype(f32)
        delta, mn, vn = _adam_math(w_ref[0], g, m_ref[0], v_ref[0])
        g_out[0] = g
        d_out[0] = delta
        m_out[0] = mn
        v_out[0] = vn

    blk = pl.BlockSpec((1, tb, width), lambda i, j_ref: (0, i, 0))
    grid_spec = pltpu.PrefetchScalarGridSpec(
        num_scalar_prefetch=1, grid=(rows // tb,),
        in_specs=[pl.BlockSpec((1, tb, width), lambda i, j_ref: (j_ref[0], i, 0)),
                  pl.BlockSpec((3, tb, width), lambda i, j_ref: (0, i, 0)), blk, blk, blk],
        out_specs=[blk, blk, blk, blk])
    return pl.pallas_call(
        body, name=name, grid_spec=grid_spec, out_shape=[jax.ShapeDtypeStruct(w.shape, f32)] * 4,
        compiler_params=_params(1),
    )(chip, hsum, recv, w, m, v)


def _sum_shard(hsum, recv, chip, *, name):
    _, rows, width = hsum.shape
    tb = _row_block(rows)

    def body(j_ref, h_ref, r_ref, g_out):
        g_out[...] = ((h_ref[0] + r_ref[0].astype(f32)) + r_ref[1].astype(f32)) + r_ref[2].astype(f32)

    grid_spec = pltpu.PrefetchScalarGridSpec(
        num_scalar_prefetch=1, grid=(rows // tb,),
        in_specs=[pl.BlockSpec((1, tb, width), lambda i, j_ref: (j_ref[0], i, 0)),
                  pl.BlockSpec((3, tb, width), lambda i, j_ref: (0, i, 0))],
        out_specs=pl.BlockSpec((tb, width), lambda i, j_ref: (i, 0)))
    return pl.pallas_call(body, name=name, grid_spec=grid_spec, out_shape=jax.ShapeDtypeStruct((rows, width), f32),
                          compiler_params=_params(1))(chip, hsum, recv)


def _adam_columns(g, w, m, v, *, name):
    cols, _, rows = w.shape
    tb = cols // 2

    def body(g_ref, w_ref, m_ref, v_ref, d_out, m_out, v_out):
        delta, mn, vn = _adam_math(w_ref[...], g_ref[...], m_ref[...], v_ref[...])
        d_out[...] = delta
        m_out[...] = mn
        v_out[...] = vn

    blk = pl.BlockSpec((tb, 1, rows), lambda i: (i, 0, 0))
    return pl.pallas_call(
        body, name=name, grid=(cols // tb,), in_specs=[blk] * 4, out_specs=[blk] * 3,
        out_shape=[jax.ShapeDtypeStruct(w.shape, f32)] * 3, compiler_params=_params(1),
    )(g, w, m, v)


R_SMALL = 16 + 16 * N_DEV
_SMALL_LANES = {"gdn_norm_g": (0, DH), "gdn_A_log": (DH, DH + H), "gdn_dt_bias": (2 * DH, 2 * DH + H)}
_LOSS_LANE = 3 * DH


def _pack_small(dg1, dg2, dg3, dgn, dal, ddt, loss_p, dwa, dwg, dwf):
    def body(dg1_ref, dg2_ref, dg3_ref, dgn_ref, dal_ref, ddt_ref, loss_ref, dwa_ref, dwg_ref, dwf_ref, o_ref):
        def total(ref):
            return jnp.sum(ref[...], axis=0, keepdims=True)

        o_ref[...] = jnp.zeros_like(o_ref)
        o_ref[0:1, :] = total(dg1_ref)
        o_ref[1:2, :] = total(dg2_ref)
        o_ref[2:3, :] = total(dg3_ref)
        o_ref[3:4, 0:DH] = total(dgn_ref)
        o_ref[3:4, DH:2 * DH] = total(dal_ref)
        o_ref[3:4, 2 * DH:3 * DH] = total(ddt_ref)
        o_ref[3:4, 3 * DH:4 * DH] = total(loss_ref)
        for d in range(N_DEV):
            base = 16 + 16 * d
            o_ref[base:base + 3, 0:128] = dwa_ref[0:3, 128 * d:128 * (d + 1)]
            o_ref[base:base + 4, 128:512] = dwg_ref[0:4, 384 * d:384 * (d + 1)]
            o_ref[base + 8:base + 11, 0:704] = dwf_ref[0:3, 704 * d:704 * (d + 1)]

    return pl.pallas_call(body, name="pack_small", out_shape=jax.ShapeDtypeStruct((R_SMALL, D), f32))(
        dg1, dg2, dg3, dgn, dal, ddt, loss_p, dwa, dwg, dwf)


_SMALL = ("norm_mix_g", "norm_ffn_g", "norm_final_g", "gdn_norm_g", "gdn_A_log", "gdn_dt_bias",
          "conv_a_w", "gdn_conv_w", "ffn_conv_w")


def _adam_small(gath, me, w, m, v):
    arrays = [t[n] for n in _SMALL for t in (w, m, v)]

    def body(me_ref, ga_ref, gb_ref, *refs):
        ins, outs = refs[:len(arrays)], refs[len(arrays):]
        ga, gb = ga_ref[0], gb_ref[0]
        for s in range(1, N_DEV):
            ga = ga + ga_ref[s]
            gb = gb + gb_ref[s]
        grads = {"norm_mix_g": ga[0:1, :], "norm_ffn_g": ga[1:2, :], "norm_final_g": ga[2:3, :],
                 "conv_a_w": gb[0:3, 0:128], "gdn_conv_w": gb[0:4, 128:512], "ffn_conv_w": gb[8:11, 0:704]}
        for n, (lo, hi) in _SMALL_LANES.items():
            grads[n] = ga[3:4, lo:hi]
        for i, n in enumerate(_SMALL):
            three_d = len(w[n].shape) == 3
            wv, mv, vv = (r[0] if three_d else r[...] for r in ins[3 * i:3 * i + 3])
            delta, mn, vn = _adam_math(wv, grads[n], mv, vv)
            for o_ref, val in zip(outs[4 * i:4 * i + 4], (grads[n], delta, mn, vn)):
                if three_d:
                    o_ref[0] = val
                else:
                    o_ref[...] = val
        outs[-1][...] = ga[3:4, _LOSS_LANE:_LOSS_LANE + 1]

    def whole(shape):
        return pl.BlockSpec(shape, lambda i, me_ref: (0,) * len(shape))

    grid_spec = pltpu.PrefetchScalarGridSpec(
        num_scalar_prefetch=1, grid=(1,),
        in_specs=[pl.BlockSpec((N_DEV, 16, D), lambda i, me_ref: (0, 0, 0)),
                  pl.BlockSpec((N_DEV, 16, D), lambda i, me_ref: (0, 1 + me_ref[0], 0))] + [whole(a.shape) for a in arrays],
        out_specs=[whole(w[n].shape) for n in _SMALL for _ in range(4)] + [whole((1, 1))])
    res = pl.pallas_call(
        body, name="adam_small", grid_spec=grid_spec,
        out_shape=[jax.ShapeDtypeStruct(w[n].shape, f32) for n in _SMALL for _ in range(4)]
        + [jax.ShapeDtypeStruct((1, 1), f32)],
        compiler_params=_params(1),
    )(me, gath, gath, *arrays)
    return {n: tuple(res[4 * i:4 * i + 4]) for i, n in enumerate(_SMALL)}, res[-1]


def _adam_math(w, g, m, v):
    m = ADAM_B1 * m + (1.0 - ADAM_B1) * g
    v = ADAM_B2 * v + (1.0 - ADAM_B2) * jnp.square(g)
    m_hat = m / (1.0 - ADAM_B1 ** ADAM_STEP)
    v_hat = v / (1.0 - ADAM_B2 ** ADAM_STEP)
    delta = -ADAM_LR * (m_hat / (jnp.sqrt(v_hat) + ADAM_EPS) + ADAM_WD * w)
    return delta, m, v


_WEIGHTS = ("norm_mix_g", "w_in", "conv_a_w", "gdn_conv_w", "gdn_A_log", "gdn_dt_bias", "gdn_norm_g", "w_a_out",
            "w_b_out", "w_o", "norm_ffn_g", "w_up", "ffn_conv_w", "w_down", "norm_final_g")
_BIG = ("w_in",) + _REST
_CONVS = ("conv_a_w", "gdn_conv_w", "ffn_conv_w")


class _StepExchanges:
    def __init__(self, wts, mom, var, c_me, chip):
        self.wts, self.mom, self.var, self.c_me, self.chip = wts, mom, var, c_me, chip
        self.results = {}

    def gather_first(self):
        return _gather_exchange([self.wts["w_in"][0].astype(bf16)] + [self.wts[n][0] for n in _CONVS])

    def finish_first(self, gathered):
        g_in, gc_a, gc_g, gc_f = gathered
        w1, w2 = _cols_to_matrices(g_in, _IN_RANGES, (NW1, 128), name="relay_w_in")
        return {"w1": w1, "w2": w2, "conv_a_w": gc_a.transpose(1, 0, 2).reshape(3, D),
                "gdn_conv_w": gc_g.transpose(1, 0, 2).reshape(4, 3 * D),
                "ffn_conv_w": gc_f.transpose(1, 0, 2).reshape(3, 2 * DFF)}

    def gather_rest(self):
        return _gather_direct_exchange([self.wts[n][0].astype(bf16) for n in _REST])

    def finish_gather(self, gathered):
        g_up, g_a, g_b, g_o, g_down = _gather_forward(gathered)
        return {"w_up": g_up.reshape(2 * DFF, D), "w_a_out": g_a.reshape(D, D), "w_b_out": g_b.reshape(D, D),
                "w_o": g_o.reshape(D, D), "w_down": g_down.reshape(DFF, D)}

    def reduce_halves(self, names, grads):
        blocks = []
        for n in names:
            if n == "w_in":
                g = _transposed_matrices_to_blocks([grads["w1"], grads["w2"]], _IN_RANGES, R_IN, name="relay_dw_in")
                blocks.append(g.reshape(4, 2, R_IN, D))
            else:
                blocks.append(grads[n].reshape(4, 2, *self.wts[n].shape[1:]))
        return _sibling_exchange([_half_bf16(g, 1 - self.c_me, name="rs_half_" + n) for n, g in zip(names, blocks)]), blocks

    def reduce_sums(self, names, blocks, recv):
        sums = [_pair_sum(g, r, self.c_me, name="rs_sum_" + n) for n, g, r in zip(names, blocks, recv)]
        return _chips_exchange([s[1] for s in sums]), [s[0] for s in sums]

    def finish_reduce(self, names, sums, recv):
        for n, s, r in zip(names, sums, recv):
            if n == "w_in":
                g = _sum_shard(s, r, self.chip, name="rs_total_w_in")[:, None, :]
                w, m, v = (jnp.transpose(t[n], (2, 0, 1)) for t in (self.wts, self.mom, self.var))
                res = (g, *_adam_columns(g, w, m, v, name="adam_w_in"))
                self.results[n] = tuple(jnp.transpose(a, (1, 2, 0)) for a in res)
            else:
                self.results[n] = _adam_shard(s, r, self.chip, self.wts[n], self.mom[n], self.var[n], name="adam_" + n)


def kernel(x, norm_mix_g, w_in, conv_a_w, gdn_conv_w, gdn_A_log, gdn_dt_bias, gdn_norm_g, w_a_out, w_b_out, w_o, norm_ffn_g, w_up, ffn_conv_w, w_down, norm_final_g, loss_target, m_norm_mix_g, m_w_in, m_conv_a_w, m_gdn_conv_w, m_gdn_A_log, m_gdn_dt_bias, m_gdn_norm_g, m_w_a_out, m_w_b_out, m_w_o, m_norm_ffn_g, m_w_up, m_ffn_conv_w, m_w_down, m_norm_final_g, v_norm_mix_g, v_w_in, v_conv_a_w, v_gdn_conv_w, v_gdn_A_log, v_gdn_dt_bias, v_gdn_norm_g, v_w_a_out, v_w_b_out, v_w_o, v_norm_ffn_g, v_w_up, v_ffn_conv_w, v_w_down, v_norm_final_g):
    wts = dict(zip(_WEIGHTS, (norm_mix_g, w_in, conv_a_w, gdn_conv_w, gdn_A_log, gdn_dt_bias, gdn_norm_g, w_a_out,
                              w_b_out, w_o, norm_ffn_g, w_up, ffn_conv_w, w_down, norm_final_g)))
    mom = dict(zip(_WEIGHTS, (m_norm_mix_g, m_w_in, m_conv_a_w, m_gdn_conv_w, m_gdn_A_log, m_gdn_dt_bias,
                              m_gdn_norm_g, m_w_a_out, m_w_b_out, m_w_o, m_norm_ffn_g, m_w_up, m_ffn_conv_w,
                              m_w_down, m_norm_final_g)))
    var = dict(zip(_WEIGHTS, (v_norm_mix_g, v_w_in, v_conv_a_w, v_gdn_conv_w, v_gdn_A_log, v_gdn_dt_bias,
                              v_gdn_norm_g, v_w_a_out, v_w_b_out, v_w_o, v_norm_ffn_g, v_w_up, v_ffn_conv_w,
                              v_w_down, v_norm_final_g)))
    cx, cy, cc = lax.axis_index("x"), lax.axis_index("y"), lax.axis_index("c")
    c_me = jnp.reshape(cc, (1,)).astype(jnp.int32)
    chip = jnp.reshape(2 * cx + cy, (1,)).astype(jnp.int32)
    me = jnp.reshape(4 * cx + 2 * cy + cc, (1,)).astype(jnp.int32)

    def with_up_transposed(t):
        return {**t, "w_up": jnp.swapaxes(t["w_up"], 1, 2)}

    comm = _StepExchanges(with_up_transposed(wts), with_up_transposed(mom), with_up_transposed(var), c_me, chip)
    replicated = {n: wts[n] for n in ("norm_mix_g", "norm_ffn_g", "norm_final_g", "gdn_norm_g", "gdn_A_log", "gdn_dt_bias")}
    loss_p, dx, grads = _local_step(x[0], loss_target[0], replicated, comm)
    res = comm.results
    res["w_up"] = tuple(jnp.swapaxes(a, 1, 2) for a in res["w_up"])

    small = _pack_small(grads["norm_mix_g"], grads["norm_ffn_g"], grads["norm_final_g"], grads["gdn_norm_g"],
                        grads["gdn_A_log"], grads["gdn_dt_bias"], loss_p, grads["conv_a_w"], grads["gdn_conv_w"],
                        grads["ffn_conv_w"])
    (small_all,) = _run_exchange(_gather_exchange([small]), name="ag_small")

    def raw(t):
        return {n: t[n].reshape(1, D) if n == "norm_final_g" else t[n] for n in _SMALL}

    res_small, loss = _adam_small(small_all, me, raw(wts), raw(mom), raw(var))
    for n in _SMALL:
        res[n] = tuple(a.reshape(wts[n].shape) for a in res_small[n])
    outs = [[res[n][i] for n in _WEIGHTS] for i in range(4)]
    return (loss.reshape(()), dx[None], *outs[0], *outs[1], *outs[2], *outs[3])
```

```python
import jax
import jax.numpy as jnp
from jax import lax
from jax.experimental import pallas as pl
from jax.experimental.pallas import tpu as pltpu

f32 = jnp.float32
bf16 = jnp.bfloat16

D = 1024
H = 8
DH = 128
CH = 64
GDN_STEP = 2
DFF = 2816
NW1 = 9216
EPS = 1e-6
N_DEV = 8

ADAM_LR = 0.001
ADAM_B1 = 0.9
ADAM_B2 = 0.999
ADAM_EPS = 1e-08
ADAM_WD = 0.01
ADAM_STEP = 10

VMEM_LIMIT_BYTES = 48 * 1024 * 1024

R_IN, R_UP = 1154, 704

_HI = lax.Precision.HIGHEST
MESH = pl.DeviceIdType.MESH


def _params(n_grid):
    return pltpu.CompilerParams(dimension_semantics=("arbitrary",) * n_grid, vmem_limit_bytes=VMEM_LIMIT_BYTES)


def _bdot(a, b):
    return jnp.dot(a.astype(bf16), b.astype(bf16), preferred_element_type=f32)


def _bdot_nt(a, b):
    return lax.dot_general(a.astype(bf16), b.astype(bf16), (((1,), (1,)), ((), ())), preferred_element_type=f32)


def _bdot_tn(a, b):
    return lax.dot_general(a.astype(bf16), b.astype(bf16), (((0,), (0,)), ((), ())), preferred_element_type=f32)


def _hdot(a, b):
    return jnp.dot(a, b, preferred_element_type=f32, precision=_HI)


def _idot(a, b):
    return jnp.dot(a, b, preferred_element_type=f32, precision=lax.Precision.HIGH)


def _sigmoid(x):
    return 1.0 / (1.0 + jnp.exp(-x))


def _softplus(x):
    return jnp.maximum(x, 0.0) + jnp.log(1.0 + jnp.exp(-jnp.abs(x)))


def _shift_down(x, halo, j):
    if j == 0:
        return x
    xr = pltpu.roll(x, j, 0)
    hr = pltpu.roll(halo, j, 0)
    r8 = lax.broadcasted_iota(jnp.int32, hr.shape, 0)
    top = jnp.where(r8 < j, hr, xr[:8])
    return jnp.concatenate([top, xr[8:]], axis=0)


def _shift_up(x, halo, j):
    if j == 0:
        return x
    n = x.shape[0]
    xr = pltpu.roll(x, n - j, 0)
    hr = pltpu.roll(halo, 8 - j, 0)
    r8 = lax.broadcasted_iota(jnp.int32, hr.shape, 0)
    bot = jnp.where(r8 >= 8 - j, hr, xr[n - 8:])
    return jnp.concatenate([xr[:n - 8], bot], axis=0)


def _taps_down(x, halo, k):
    return [_shift_down(x, halo, k - 1 - j) for j in range(k)]


def _strip(i, base=0):
    return slice(base + i * 128, base + (i + 1) * 128)


def _strip_taps(x, halo, first, k):
    return _taps_down(x, jnp.where(first, 0.0, halo), k)


def _strip_conv(w_ref, sl, taps):
    out = w_ref[0:1, sl] * taps[0]
    for j in range(1, len(taps)):
        out = out + w_ref[j:j + 1, sl] * taps[j]
    return out


def _strip_weight_grad(dw_ref, sl, dy, taps):
    for j, tap in enumerate(taps):
        dw_ref[j:j + 1, sl] += jnp.sum(dy * tap, axis=0, keepdims=True)


def _strip_conv_up(dy, halo, last, w_ref, sl, k):
    halo = jnp.where(last, 0.0, halo)
    out = w_ref[k - 1:k, sl] * dy
    for j in range(k - 1):
        out = out + w_ref[j:j + 1, sl] * _shift_up(dy, halo, k - 1 - j)
    return out


def _row(tb, w, col=0):
    return pl.BlockSpec((tb, w), lambda i: (i, col))


def _prev(tb, w, col=0, rows=8):
    return pl.BlockSpec((rows, w), lambda i: (jnp.maximum(i * (tb // rows) - 1, 0), col))


def _next(tb, w, n_rows, col=0, rows=8):
    last = n_rows // rows - 1
    return pl.BlockSpec((rows, w), lambda i: (jnp.minimum((i + 1) * (tb // rows), last), col))


def _f32(ref, sl):
    return ref[:, sl].astype(f32)


def _halo_before(ref, sl):
    h = _f32(ref, sl)
    return h[h.shape[0] - 8:]


def _halo_after(ref, sl):
    return _f32(ref, sl)[:8]


def _fixed(shape):
    return pl.BlockSpec(shape, lambda i: (0,) * len(shape))


def _pick(n, prefs):
    for p in prefs:
        if n % p == 0:
            return p
    return n


def _matmul(a, b, *, name, nt=False, add=None, tm=1024, tn=1024, tk=None, out_dtype=f32, cols=None, exchange=None):
    m, kd = a.shape
    col0, n = cols if cols is not None else (0, b.shape[0] if nt else b.shape[1])
    tm = _pick(m, (tm, 512, 256))
    tn = _pick(n, (tn, 1024, 512, 128))
    tk = kd if tk is None else tk
    nk = kd // tk
    assert nk == 1 or out_dtype == f32
    assert col0 % tn == 0 and not (nt and cols)
    j0 = col0 // tn
    dims = (((1,), (1,)), ((), ())) if nt else (((1,), (0,)), ((), ()))

    def body(a_ref, b_ref, *rest):
        o_ref = rest[-1]
        part = lax.dot_general(a_ref[...], b_ref[...], dims, preferred_element_type=f32)
        if nk == 1:
            o_ref[...] = (part if add is None else part + rest[0][...]).astype(out_dtype)
            return
        k = pl.program_id(2)

        @pl.when(k == 0)
        def _():
            o_ref[...] = part if add is None else part + rest[0][...]

        @pl.when(k > 0)
        def _():
            o_ref[...] += part

    b_spec = pl.BlockSpec((tn, tk), lambda i, j, k: (j, k)) if nt else pl.BlockSpec((tk, tn), lambda i, j, k: (k, j + j0))
    in_specs = [pl.BlockSpec((tm, tk), lambda i, j, k: (i, k)), b_spec]
    args = [a, b]
    if add is not None:
        in_specs.append(pl.BlockSpec((tm, tn), lambda i, j, k: (i, j)))
        args.append(add)
    return _call_with_exchange(
        body, exchange, name=name, grid=(m // tm, n // tn, nk), in_specs=in_specs,
        out_specs=pl.BlockSpec((tm, tn), lambda i, j, k: (i, j)),
        out_shape=jax.ShapeDtypeStruct((m, n), out_dtype), args=args)


def _call_with_exchange(body, exchange, *, name, grid, in_specs, out_specs, out_shape, args):
    if exchange is None:
        return pl.pallas_call(body, name=name, grid=grid, in_specs=in_specs, out_specs=out_specs, out_shape=out_shape,
                              compiler_params=_params(len(grid)))(*args)
    x_arrays, x_shapes, x_sems, start, wait = exchange[:5]
    n_in, n_xin, n_xout = len(args), len(x_arrays), len(x_shapes)
    aliases = {n_in + i: 1 + i for i in range(n_xin)} if len(exchange) > 5 and exchange[5] else {}

    def full_body(*refs):
        c_in, x_in = refs[:n_in], refs[n_in:n_in + n_xin]
        c_out = refs[n_in + n_xin]
        x_out = refs[n_in + n_xin + 1:n_in + n_xin + 1 + n_xout]
        sems = refs[n_in + n_xin + 1 + n_xout:]
        ids = [pl.program_id(d) for d in range(len(grid))]
        first, last = ids[0] == 0, ids[0] == grid[0] - 1
        for d in range(1, len(grid)):
            first = first & (ids[d] == 0)
            last = last & (ids[d] == grid[d] - 1)

        @pl.when(first)
        def _():
            start(x_in, x_out, sems)

        body(*c_in, c_out)

        @pl.when(last)
        def _():
            wait(x_in, x_out, sems)

    res = pl.pallas_call(
        full_body, name=name, grid=grid, in_specs=list(in_specs) + [_ANY] * n_xin,
        out_specs=[out_specs] + [_ANY] * n_xout, out_shape=[out_shape] + list(x_shapes),
        scratch_shapes=list(x_sems), input_output_aliases=aliases, compiler_params=_params(len(grid)),
    )(*args, *x_arrays)
    return res[0], list(res[1:])


def _matmul_tn(a, b, *, name, tm=1024, tn=1024, exchange=None):
    t, m = a.shape
    _, n = b.shape
    tm = _pick(m, (tm, 1024, 512, 128))
    tn = _pick(n, (tn, 1024, 512, 128))
    tt = _pick(t, (2048, 1024, 512, 256))
    nt = t // tt

    def body(a_ref, b_ref, o_ref):
        k = pl.program_id(2)
        part = lax.dot_general(a_ref[...], b_ref[...], (((0,), (0,)), ((), ())), preferred_element_type=f32)

        @pl.when(k == 0)
        def _():
            o_ref[...] = part

        @pl.when(k > 0)
        def _():
            o_ref[...] += part

    return _call_with_exchange(
        body, exchange, name=name, grid=(m // tm, n // tn, nt),
        in_specs=[pl.BlockSpec((tt, tm), lambda i, j, k: (k, i)), pl.BlockSpec((tt, tn), lambda i, j, k: (k, j))],
        out_specs=pl.BlockSpec((tm, tn), lambda i, j, k: (i, j)),
        out_shape=jax.ShapeDtypeStruct((m, n), f32), args=[a, b])


def _rms_fwd(x, g, *, name, exchange=None):
    t = x.shape[0]
    tb = _pick(t, (256, 128))

    def body(x_ref, g_ref, h_ref):
        xv = x_ref[...]
        r = lax.rsqrt(jnp.mean(xv * xv, axis=-1, keepdims=True) + EPS)
        h_ref[...] = (xv * r * g_ref[...]).astype(bf16)

    return _call_with_exchange(
        body, exchange, name=name, grid=(t // tb,), in_specs=[_row(tb, D), _fixed((1, D))], out_specs=_row(tb, D),
        out_shape=jax.ShapeDtypeStruct((t, D), bf16), args=[x, g])


def _rms_bwd(dh, x, g, dres, *, name, more=None):
    t = x.shape[0]
    tb = _pick(t, (256, 128))

    def body(dh_ref, x_ref, g_ref, dres_ref, *rest):
        dx_ref, dxb_ref, dg_ref = rest[-3:]
        xv = x_ref[...]
        r = lax.rsqrt(jnp.mean(xv * xv, axis=-1, keepdims=True) + EPS)
        xh = xv * r
        dy = dh_ref[...]
        if more is not None:
            dy = dy + lax.dot_general(rest[0][...], rest[1][...], (((1,), (1,)), ((), ())), preferred_element_type=f32)
        dyg = dy * g_ref[...]
        dx = dres_ref[...] + r * (dyg - xh * jnp.mean(dyg * xh, axis=-1, keepdims=True))
        dx_ref[...] = dx
        dxb_ref[...] = dx.astype(bf16)

        @pl.when(pl.program_id(0) == 0)
        def _():
            dg_ref[...] = jnp.zeros_like(dg_ref)

        dg_ref[...] += jnp.sum((dy * xh).reshape(tb // 8, 8, D), axis=0)

    in_specs, args = [_row(tb, D), _row(tb, D), _fixed((1, D)), _row(tb, D)], [dh, x, g, dres]
    if more is not None:
        in_specs += [_row(tb, 128), _fixed(more[1].shape)]
        args += list(more)
    return pl.pallas_call(
        body, name=name, grid=(t // tb,), in_specs=in_specs,
        out_specs=[_row(tb, D), _row(tb, D), _fixed((8, D))],
        out_shape=[jax.ShapeDtypeStruct((t, D), f32), jax.ShapeDtypeStruct((t, D), bf16),
                   jax.ShapeDtypeStruct((8, D), f32)],
        compiler_params=_params(1),
    )(*args)


def _gdn_gates(ab, alog, dtb):
    lane = lax.broadcasted_iota(jnp.int32, ab.shape, 1)
    g = -jnp.exp(alog) * _softplus(ab + dtb)
    beta = _sigmoid(ab)
    return jnp.where(lane < H, g, jnp.where(lane < 2 * H, beta, 0.0))


def _pre_fwd(pg, pq, h1, w2, wa, wg, alog, dtb):
    t = pg.shape[0]
    tb = 128

    def body(p0_ref, p0h_ref, pq_ref, pqh_ref, h1_ref, w2_ref, wa_ref, wg_ref, alog_ref, dtb_ref,
             ya_ref, qn_ref, kn_ref, vc_ref, gb_ref, p2_ref):
        first = pl.program_id(0) == 0
        p2_ref[...] = jnp.dot(h1_ref[...], w2_ref[...], preferred_element_type=f32)
        for i in range(D // 128):
            sl, cg, xv = _strip(i), _strip(i, D), _strip(i, 2 * D)
            taps = _strip_taps(_f32(p0_ref, cg) * _f32(p0_ref, xv), _halo_before(p0h_ref, cg) * _halo_before(p0h_ref, xv),
                               first, 3)
            ya_ref[:, sl] = (_f32(p0_ref, sl) * _strip_conv(wa_ref, sl, taps)).astype(bf16)
        for part, out_ref, scale in ((0, qn_ref, DH ** -0.5), (1, kn_ref, 1.0), (2, vc_ref, None)):
            for h in range(H):
                sl = _strip(h, part * D)
                s = _strip_conv(wg_ref, sl, _strip_taps(pq_ref[:, sl], pqh_ref[:, sl], first, 4))
                s = s * _sigmoid(s)
                if scale is not None:
                    s = s * (lax.rsqrt(jnp.sum(s * s, axis=-1, keepdims=True) + EPS) * scale)
                out_ref[:, _strip(h)] = s
        gb_ref[...] = _gdn_gates(p2_ref[...], alog_ref[...], dtb_ref[...])

    return pl.pallas_call(
        body, name="pre_fwd", grid=(t // tb,),
        in_specs=[_row(tb, 3 * D, 0), _prev(tb, 3 * D, 0, rows=16), _row(tb, 3 * D), _prev(tb, 3 * D), _row(tb, D),
                  _fixed((D, 128)), _fixed((8, D)), _fixed((8, 3 * D)), _fixed((1, 128)), _fixed((1, 128))],
        out_specs=[_row(tb, D), _row(tb, D), _row(tb, D), _row(tb, D), _row(tb, 128), _row(tb, 128)],
        out_shape=[jax.ShapeDtypeStruct((t, D), bf16), jax.ShapeDtypeStruct((t, D), f32),
                   jax.ShapeDtypeStruct((t, D), f32), jax.ShapeDtypeStruct((t, D), f32),
                   jax.ShapeDtypeStruct((t, 128), f32), jax.ShapeDtypeStruct((t, 128), f32)],
        compiler_params=_params(1),
    )(pg, pg, pq, pq, h1, w2, wa, wg, alog, dtb)


_Z_COL, _GA_COL, _GB_COL = 3, 4, 5


def _post_fwd(o, pg, gn):
    t = o.shape[0]
    tb = _pick(t, (256, 128))

    def body(o_ref, z_ref, gn_ref, yb_ref):
        for h in range(H):
            sl = slice(h * DH, (h + 1) * DH)
            oh = o_ref[:, sl]
            z = _f32(z_ref, sl)
            r = lax.rsqrt(jnp.mean(oh * oh, axis=-1, keepdims=True) + EPS)
            yb_ref[:, sl] = (oh * r * gn_ref[...] * (z * _sigmoid(z))).astype(bf16)

    return pl.pallas_call(
        body, name="post_fwd", grid=(t // tb,), in_specs=[_row(tb, D), _row(tb, D, _Z_COL), _fixed((1, DH))],
        out_specs=_row(tb, D), out_shape=jax.ShapeDtypeStruct((t, D), bf16), compiler_params=_params(1),
    )(o, pg, gn)


def _post_bwd(dyb, o, pg, gn):
    t = o.shape[0]
    tb = _pick(t, (256, 128))

    def body(dyb_ref, o_ref, z_ref, gn_ref, do_ref, dz_ref, dgn_ref):
        @pl.when(pl.program_id(0) == 0)
        def _():
            dgn_ref[...] = jnp.zeros_like(dgn_ref)

        gn_v = gn_ref[...]
        acc = jnp.zeros((8, DH), f32)
        for h in range(H):
            sl = slice(h * DH, (h + 1) * DH)
            oh = o_ref[:, sl]
            z = _f32(z_ref, sl)
            dy = dyb_ref[:, sl]
            r = lax.rsqrt(jnp.mean(oh * oh, axis=-1, keepdims=True) + EPS)
            on = oh * r
            sg = _sigmoid(z)
            sz = z * sg
            don = dy * sz
            dz_ref[:, sl] = (dy * on * gn_v * (sg * (1.0 + z * (1.0 - sg)))).astype(bf16)
            acc = acc + jnp.sum((don * on).reshape(tb // 8, 8, DH), axis=0)
            doh = don * gn_v
            do_ref[:, sl] = r * (doh - on * jnp.mean(doh * on, axis=-1, keepdims=True))
        dgn_ref[...] += acc

    return pl.pallas_call(
        body, name="post_bwd", grid=(t // tb,),
        in_specs=[_row(tb, D), _row(tb, D), _row(tb, D, _Z_COL), _fixed((1, DH))],
        out_specs=[_row(tb, D), _row(tb, D), _fixed((8, DH))],
        out_shape=[jax.ShapeDtypeStruct((t, D), f32), jax.ShapeDtypeStruct((t, D), bf16),
                   jax.ShapeDtypeStruct((8, DH), f32)],
        compiler_params=_params(1),
    )(dyb, o, pg, gn)


def _mix_fwd(ya, yb, pg):
    t = ya.shape[0]
    tb = _pick(t, (256, 128))

    def body(ya_ref, yb_ref, ga_ref, gb_ref, mix_ref):
        ya_v, yb_v = ya_ref[...].astype(f32), yb_ref[...].astype(f32)
        mix = _sigmoid(ga_ref[...].astype(f32)) * ya_v + _sigmoid(gb_ref[...].astype(f32)) * yb_v
        mix_ref[...] = mix.astype(bf16)

    return pl.pallas_call(
        body, name="mix_fwd", grid=(t // tb,),
        in_specs=[_row(tb, D), _row(tb, D), _row(tb, D, _GA_COL), _row(tb, D, _GB_COL)],
        out_specs=_row(tb, D), out_shape=jax.ShapeDtypeStruct((t, D), bf16), compiler_params=_params(1),
    )(ya, yb, pg, pg)


def _mix_bwd(dmix, ya, yb, pg):
    t = ya.shape[0]
    tb = _pick(t, (256, 128))

    def body(dm_ref, ya_ref, yb_ref, ga_ref, gb_ref, dya_ref, dyb_ref, dg_ref):
        dm = dm_ref[...].astype(f32)
        sa = _sigmoid(ga_ref[...].astype(f32))
        sb = _sigmoid(gb_ref[...].astype(f32))
        dya_ref[...] = (dm * sa).astype(bf16)
        dyb_ref[...] = (dm * sb).astype(bf16)
        dg_ref[:, :D] = (dm * ya_ref[...].astype(f32) * sa * (1.0 - sa)).astype(bf16)
        dg_ref[:, D:] = (dm * yb_ref[...].astype(f32) * sb * (1.0 - sb)).astype(bf16)

    return pl.pallas_call(
        body, name="mix_bwd", grid=(t // tb,),
        in_specs=[_row(tb, D), _row(tb, D), _row(tb, D), _row(tb, D, _GA_COL), _row(tb, D, _GB_COL)],
        out_specs=[_row(tb, D), _row(tb, D), _row(tb, 2 * D)],
        out_shape=[jax.ShapeDtypeStruct((t, D), bf16), jax.ShapeDtypeStruct((t, D), bf16),
                   jax.ShapeDtypeStruct((t, 2 * D), bf16)],
        compiler_params=_params(1),
    )(dmix, ya, yb, pg, pg)


def _ffn_fwd(up, wf):
    t = up.shape[0]
    tb = 128

    def body(up_ref, uph_ref, wf_ref, act_ref):
        first = pl.program_id(0) == 0
        for i in range(DFF // 128):
            g, v = _strip(i), _strip(i, DFF)
            gate = _strip_conv(wf_ref, g, _strip_taps(_f32(up_ref, g), _halo_before(uph_ref, g), first, 3))
            val = _strip_conv(wf_ref, v, _strip_taps(_f32(up_ref, v), _halo_before(uph_ref, v), first, 3))
            act_ref[:, g] = (gate * _sigmoid(gate) * val).astype(bf16)

    return pl.pallas_call(
        body, name="ffn_fwd", grid=(t // tb,),
        in_specs=[_row(tb, 2 * DFF), _prev(tb, 2 * DFF, rows=16), _fixed((8, 2 * DFF))],
        out_specs=_row(tb, DFF), out_shape=jax.ShapeDtypeStruct((t, DFF), bf16), compiler_params=_params(1),
    )(up, up, wf)


def _ffn_bwd1(dact, up, wf):
    t = up.shape[0]
    tb = 128

    def body(da_ref, up_ref, uph_ref, wf_ref, dc_ref, dw_ref):
        @pl.when(pl.program_id(0) == 0)
        def _():
            dw_ref[...] = jnp.zeros_like(dw_ref)

        first = pl.program_id(0) == 0
        for i in range(DFF // 128):
            g, v = _strip(i), _strip(i, DFF)
            g_taps = _strip_taps(_f32(up_ref, g), _halo_before(uph_ref, g), first, 3)
            v_taps = _strip_taps(_f32(up_ref, v), _halo_before(uph_ref, v), first, 3)
            gate = _strip_conv(wf_ref, g, g_taps)
            val = _strip_conv(wf_ref, v, v_taps)
            sg = _sigmoid(gate)
            da = _f32(da_ref, g)
            dgate = da * val * (sg * (1.0 + gate * (1.0 - sg)))
            dval = da * (gate * sg)
            dc_ref[:, g] = dgate.astype(bf16)
            dc_ref[:, v] = dval.astype(bf16)
            _strip_weight_grad(dw_ref, g, dgate, g_taps)
            _strip_weight_grad(dw_ref, v, dval, v_taps)

    return pl.pallas_call(
        body, name="ffn_bwd1", grid=(t // tb,),
        in_specs=[_row(tb, DFF), _row(tb, 2 * DFF), _prev(tb, 2 * DFF, rows=16), _fixed((8, 2 * DFF))],
        out_specs=[_row(tb, 2 * DFF), _fixed((8, 2 * DFF))],
        out_shape=[jax.ShapeDtypeStruct((t, 2 * DFF), bf16), jax.ShapeDtypeStruct((8, 2 * DFF), f32)],
        compiler_params=_params(1),
    )(dact, up, up, wf)


def _ffn_bwd2(dc, wf):
    t = dc.shape[0]
    tb = 128
    nb = t // tb

    def body(dc_ref, dch_ref, wf_ref, dup_ref):
        last = pl.program_id(0) == nb - 1
        for i in range(2 * DFF // 128):
            sl = _strip(i)
            dup_ref[:, sl] = _strip_conv_up(_f32(dc_ref, sl), _halo_after(dch_ref, sl), last, wf_ref, sl, 3).astype(bf16)

    return pl.pallas_call(
        body, name="ffn_bwd2", grid=(nb,),
        in_specs=[_row(tb, 2 * DFF), _next(tb, 2 * DFF, t, rows=16), _fixed((8, 2 * DFF))],
        out_specs=_row(tb, 2 * DFF), out_shape=jax.ShapeDtypeStruct((t, 2 * DFF), bf16), compiler_params=_params(1),
    )(dc, dc, wf)


def _final(x3, tgt, g):
    t = x3.shape[0]
    tb = _pick(t, (256, 128))

    def body(x_ref, t_ref, g_ref, loss_ref, dx_ref, dxb_ref, dg_ref):
        @pl.when(pl.program_id(0) == 0)
        def _():
            loss_ref[...] = jnp.zeros_like(loss_ref)
            dg_ref[...] = jnp.zeros_like(dg_ref)

        xv = x_ref[...]
        r = lax.rsqrt(jnp.mean(xv * xv, axis=-1, keepdims=True) + EPS)
        xh = xv * r
        gv = g_ref[...]
        e = xh * gv - t_ref[...]
        lrow = 0.5 * jnp.mean(e * e, axis=-1, keepdims=True)
        loss_ref[...] += jnp.sum(jnp.broadcast_to(lrow, (tb, 128)).reshape(tb // 8, 8, 128), axis=0)
        dy = e * (1.0 / D)
        dyg = dy * gv
        dx = r * (dyg - xh * jnp.mean(dyg * xh, axis=-1, keepdims=True))
        dx_ref[...] = dx
        dxb_ref[...] = dx.astype(bf16)
        dg_ref[...] += jnp.sum((dy * xh).reshape(tb // 8, 8, D), axis=0)

    return pl.pallas_call(
        body, name="final", grid=(t // tb,), in_specs=[_row(tb, D), _row(tb, D), _fixed((1, D))],
        out_specs=[_fixed((8, 128)), _row(tb, D), _row(tb, D), _fixed((8, D))],
        out_shape=[jax.ShapeDtypeStruct((8, 128), f32), jax.ShapeDtypeStruct((t, D), f32),
                   jax.ShapeDtypeStruct((t, D), bf16), jax.ShapeDtypeStruct((8, D), f32)],
        compiler_params=_params(1),
    )(x3, tgt, g)


def _pre_bwd1(pg, pq, p2, dya_in, dqn, dkn, dvc, dgb, gbeta, h1, wa, wg, alog, dtb):
    t = pg.shape[0]
    tb = 128

    def body(p0_ref, p0h_ref, pq_ref, pqh_ref, p2_ref, dya_ref, dqn_ref, dkn_ref, dvc_ref, dgb_ref, gb_ref, h1_ref,
             wa_ref, wg_ref, alog_ref, dtb_ref,
             dbg_ref, dca_ref, dc4_ref, dp2_ref, dwa_ref, dwg_ref, dal_ref, ddt_ref, dw2_ref):
        @pl.when(pl.program_id(0) == 0)
        def _():
            dwa_ref[...] = jnp.zeros_like(dwa_ref)
            dwg_ref[...] = jnp.zeros_like(dwg_ref)
            dal_ref[...] = jnp.zeros_like(dal_ref)
            ddt_ref[...] = jnp.zeros_like(ddt_ref)
            dw2_ref[...] = jnp.zeros_like(dw2_ref)

        first = pl.program_id(0) == 0

        for i in range(D // 128):
            sl, cg, xv = _strip(i), _strip(i, D), _strip(i, 2 * D)
            taps = _strip_taps(_f32(p0_ref, cg) * _f32(p0_ref, xv), _halo_before(p0h_ref, cg) * _halo_before(p0h_ref, xv),
                               first, 3)
            dya = _f32(dya_ref, sl)
            dbg_ref[:, sl] = (dya * _strip_conv(wa_ref, sl, taps)).astype(bf16)
            dca = dya * _f32(p0_ref, sl)
            dca_ref[:, sl] = dca.astype(bf16)
            _strip_weight_grad(dwa_ref, sl, dca, taps)

        for part, d_ref, scale in ((0, dqn_ref, DH ** -0.5), (1, dkn_ref, 1.0), (2, dvc_ref, None)):
            for h in range(H):
                sl = _strip(h, part * D)
                taps = _strip_taps(pq_ref[:, sl], pqh_ref[:, sl], first, 4)
                c4 = _strip_conv(wg_ref, sl, taps)
                sg = _sigmoid(c4)
                dn = d_ref[:, _strip(h)]
                if scale is not None:
                    a = c4 * sg
                    r = lax.rsqrt(jnp.sum(a * a, axis=-1, keepdims=True) + EPS)
                    an = a * r
                    dn = dn * scale
                    dn = r * (dn - an * jnp.sum(dn * an, axis=-1, keepdims=True))
                dc4 = dn * (sg * (1.0 + c4 * (1.0 - sg)))
                dc4_ref[:, sl] = dc4.astype(bf16)
                _strip_weight_grad(dwg_ref, sl, dc4, taps)

        ab = p2_ref[...]
        lane = lax.broadcasted_iota(jnp.int32, ab.shape, 1)
        dgbv = dgb_ref[...]
        gbv = gb_ref[...]
        da = dgbv * (-jnp.exp(alog_ref[...])) * _sigmoid(ab + dtb_ref[...])
        db = dgbv * gbv * (1.0 - gbv)
        dp2 = jnp.where(lane < H, da, jnp.where(lane < 2 * H, db, 0.0)).astype(bf16)
        dp2_ref[...] = dp2
        dw2_ref[...] += lax.dot_general(dp2, h1_ref[...], (((0,), (0,)), ((), ())), preferred_element_type=f32)
        dal = jnp.where(lane < H, dgbv * gbv, 0.0)
        ddt = jnp.where(lane < H, da, 0.0)
        dal_ref[...] += jnp.sum(dal.reshape(tb // 8, 8, 128), axis=0)
        ddt_ref[...] += jnp.sum(ddt.reshape(tb // 8, 8, 128), axis=0)

    return pl.pallas_call(
        body, name="pre_bwd1", grid=(t // tb,),
        in_specs=[_row(tb, 3 * D, 0), _prev(tb, 3 * D, 0, rows=16), _row(tb, 3 * D), _prev(tb, 3 * D), _row(tb, 128),
                  _row(tb, D), _row(tb, D), _row(tb, D), _row(tb, D), _row(tb, 128), _row(tb, 128), _row(tb, D),
                  _fixed((8, D)), _fixed((8, 3 * D)), _fixed((1, 128)), _fixed((1, 128))],
        out_specs=[_row(tb, D), _row(tb, D), _row(tb, 3 * D), _row(tb, 128),
                   _fixed((8, D)), _fixed((8, 3 * D)), _fixed((8, 128)), _fixed((8, 128)), _fixed((128, D))],
        out_shape=[jax.ShapeDtypeStruct((t, D), bf16), jax.ShapeDtypeStruct((t, D), bf16),
                   jax.ShapeDtypeStruct((t, 3 * D), bf16), jax.ShapeDtypeStruct((t, 128), bf16),
                   jax.ShapeDtypeStruct((8, D), f32), jax.ShapeDtypeStruct((8, 3 * D), f32),
                   jax.ShapeDtypeStruct((8, 128), f32), jax.ShapeDtypeStruct((8, 128), f32),
                   jax.ShapeDtypeStruct((128, D), f32)],
        compiler_params=_params(1),
    )(pg, pg, pq, pq, p2, dya_in, dqn, dkn, dvc, dgb, gbeta, h1, wa, wg, alog, dtb)


def _pre_bwd2(dca, dc4, pg, dbg, dz, dgates, wa, wg, exchange=None):
    t = pg.shape[0]
    tb = 128
    nb = t // tb

    def body(dca_ref, dcah_ref, dc4_ref, dc4h_ref, p0_ref, dbg_ref, dz_ref, dgt_ref, wa_ref, wg_ref, dp_ref):
        last = pl.program_id(0) == nb - 1
        dp_ref[:, :D] = dbg_ref[...]
        for i in range(D // 128):
            sl, cg, xv = _strip(i), _strip(i, D), _strip(i, 2 * D)
            du = _strip_conv_up(_f32(dca_ref, sl), _halo_after(dcah_ref, sl), last, wa_ref, sl, 3)
            dp_ref[:, cg] = (du * _f32(p0_ref, xv)).astype(bf16)
            dp_ref[:, xv] = (du * _f32(p0_ref, cg)).astype(bf16)
        dp_ref[:, 3 * D:4 * D] = dz_ref[...]
        dp_ref[:, 4 * D:6 * D] = dgt_ref[...]
        for i in range(3 * D // 128):
            sl = _strip(i)
            dq = _strip_conv_up(_f32(dc4_ref, sl), _halo_after(dc4h_ref, sl), last, wg_ref, sl, 4)
            dp_ref[:, _strip(i, 6 * D)] = dq.astype(bf16)

    return _call_with_exchange(
        body, exchange, name="pre_bwd2", grid=(nb,),
        in_specs=[_row(tb, D), _next(tb, D, t, rows=16), _row(tb, 3 * D), _next(tb, 3 * D, t, rows=16), _row(tb, 3 * D, 0),
                  _row(tb, D), _row(tb, D), _row(tb, 2 * D), _fixed((8, D)), _fixed((8, 3 * D))],
        out_specs=_row(tb, NW1), out_shape=jax.ShapeDtypeStruct((t, NW1), bf16),
        args=[dca, dca, dc4, dc4, pg, dbg, dz, dgates, wa, wg])


def _chunk_consts():
    r = lax.broadcasted_iota(jnp.int32, (CH, CH), 0)
    c = lax.broadcasted_iota(jnp.int32, (CH, CH), 1)
    return r, c, (r == c).astype(f32)


def _tri_inverse(lows, eye, r, c):
    def same_block(b):
        return jnp.bitwise_xor(r, c) < b

    xs = [jnp.where(same_block(8), -low, 0.0) for low in lows]
    ts = [eye + x for x in xs]
    for _ in range(2):
        xs = [_idot(x, x) for x in xs]
        ts = [t + _idot(t, x) for t, x in zip(ts, xs)]
    for b in (8, 16, 32):
        below = same_block(2 * b) & jnp.logical_not(same_block(b))
        ts = [t - _idot(_idot(t, jnp.where(below, low, 0.0)), t) for t, low in zip(ts, lows)]
    return ts


def _chunk_common(q, k, v, gcol, bcol, r, c, eye):
    grow = jnp.sum(eye * gcol, axis=0, keepdims=True)
    dec = jnp.exp(jnp.where(r >= c, gcol - grow, -jnp.inf))
    rcol = lax.broadcasted_iota(jnp.int32, (CH, 1), 0)
    glast = jnp.sum(jnp.where(rcol == CH - 1, gcol, 0.0), axis=0, keepdims=True)
    eg = jnp.exp(gcol)
    el = jnp.exp(glast - gcol)
    kb = k * bcol
    vb = v * bcol
    kk = _bdot_nt(kb, k)
    low = jnp.where(r > c, kk * dec, 0.0)
    qk = _bdot_nt(q, k)
    att = qk * dec
    return grow, dec, glast, eg, el, kb, vb, kk, low, qk, att, rcol


def _gdn_fwd(qn, kn, vc, gbeta):
    t = qn.shape[0]
    n_chunks = t // CH

    def body(q_ref, k_ref, v_ref, gb_ref, o_ref, s_ref, t_ref, state):
        @pl.when(pl.program_id(0) == 0)
        def _():
            state[...] = jnp.zeros_like(state)

        r, c, eye = _chunk_consts()
        tri = (r >= c).astype(f32)
        heads = range(H)
        keys = [(s, h) for s in range(GDN_STEP) for h in heads]
        rows = [slice(s * CH, (s + 1) * CH) for s in range(GDN_STEP)]
        gbs = [gb_ref[rows[s], :] for s in range(GDN_STEP)]
        galls = [_hdot(tri, gb) for gb in gbs]
        qs = {(s, h): q_ref[rows[s], h * DH:(h + 1) * DH] for s, h in keys}
        ks = {(s, h): k_ref[rows[s], h * DH:(h + 1) * DH] for s, h in keys}
        cm = {(s, h): _chunk_common(qs[s, h], ks[s, h], v_ref[rows[s], h * DH:(h + 1) * DH], galls[s][:, h:h + 1],
                                    gbs[s][:, H + h:H + h + 1], r, c, eye) for s, h in keys}
        invs = dict(zip(keys, _tri_inverse([cm[key][8] for key in keys], eye, r, c)))
        uws = {key: _bdot(invs[key], jnp.concatenate([cm[key][6], cm[key][5] * cm[key][3]], axis=1)) for key in keys}
        sts = [state[h] for h in heads]
        for s in range(GDN_STEP):
            vns = [uws[s, h][:, :DH] - _bdot(uws[s, h][:, DH:], sts[h]) for h in heads]
            outs = [_bdot(qs[s, h] * cm[s, h][3], sts[h]) + _bdot(cm[s, h][10], vns[h]) for h in heads]
            news = [sts[h] * jnp.exp(cm[s, h][2]) + _bdot_tn(ks[s, h] * cm[s, h][4], vns[h]) for h in heads]
            for h in heads:
                s_ref[s, h] = sts[h].astype(bf16)
                t_ref[s, h] = invs[s, h]
                o_ref[rows[s], h * DH:(h + 1) * DH] = outs[h]
            sts = news
        for h in heads:
            state[h] = sts[h]

    tb = GDN_STEP * CH
    return pl.pallas_call(
        body, name="gdn_fwd", grid=(t // tb,),
        in_specs=[_row(tb, D), _row(tb, D), _row(tb, D), _row(tb, 128)],
        out_specs=[_row(tb, D), pl.BlockSpec((GDN_STEP, H, DH, DH), lambda i: (i, 0, 0, 0)),
                   pl.BlockSpec((GDN_STEP, H, CH, CH), lambda i: (i, 0, 0, 0))],
        out_shape=[jax.ShapeDtypeStruct((t, D), f32), jax.ShapeDtypeStruct((n_chunks, H, DH, DH), bf16),
                   jax.ShapeDtypeStruct((n_chunks, H, CH, CH), f32)],
        scratch_shapes=[pltpu.VMEM((H, DH, DH), f32)],
        compiler_params=_params(1),
    )(qn, kn, vc, gbeta)


def _gdn_bwd(qn, kn, vc, gbeta, do, s_all, t_all):
    t = qn.shape[0]

    def body(q_ref, k_ref, v_ref, gb_ref, do_ref, s_ref, t_ref, dq_ref, dk_ref, dv_ref, dgb_ref, dstate):
        @pl.when(pl.program_id(0) == 0)
        def _():
            dstate[...] = jnp.zeros_like(dstate)

        r, c, eye = _chunk_consts()
        tril = r >= c
        lane = lax.broadcasted_iota(jnp.int32, (1, 128), 1)
        hs = range(H)

        def each(fn, *lists):
            return [fn(*args) for args in zip(*lists)]

        def rsum(a):
            return jnp.sum(a, axis=1, keepdims=True)

        def before_state(s):
            rows = slice(s * CH, (s + 1) * CH)
            gb = gb_ref[rows, :]
            gall = _hdot(tril.astype(f32), gb)
            p = {"rows": rows}
            p["q"] = q = [q_ref[rows, h * DH:(h + 1) * DH] for h in hs]
            p["k"] = k = [k_ref[rows, h * DH:(h + 1) * DH] for h in hs]
            p["v"] = v = [v_ref[rows, h * DH:(h + 1) * DH] for h in hs]
            p["dout"] = dout = [do_ref[rows, h * DH:(h + 1) * DH] for h in hs]
            p["inv"] = inv = [t_ref[s, h] for h in hs]
            p["st"] = st = [s_ref[s, h] for h in hs]
            p["bcol"] = bcol = [gb[:, H + h:H + h + 1] for h in hs]
            cm = [_chunk_common(q[h], k[h], v[h], gall[:, h:h + 1], bcol[h], r, c, eye) for h in hs]
            for name, i in (("dec", 1), ("glast", 2), ("eg", 3), ("el", 4), ("kb", 5), ("vb", 6), ("low", 8), ("att", 10)):
                p[name] = [m[i] for m in cm]
            p["rcol"] = cm[0][11]
            p["elast"] = each(jnp.exp, p["glast"])
            p["kbg"] = each(jnp.multiply, p["kb"], p["eg"])
            uw = each(lambda i, a, b: _bdot(i, jnp.concatenate([a, b], axis=1)), inv, p["vb"], p["kbg"])
            p["u"] = [a[:, :DH] for a in uw]
            p["w"] = [a[:, DH:] for a in uw]
            p["vn"] = each(lambda a, b, x: a - _bdot(b, x), p["u"], p["w"], st)
            p["qd"] = each(jnp.multiply, q, p["eg"])
            p["kd"] = each(jnp.multiply, k, p["el"])
            p["dqd"] = each(_bdot_nt, dout, st)
            p["datt"] = each(lambda d, x: jnp.where(tril, _bdot_nt(d, x), 0.0), dout, p["vn"])
            p["dqk"] = each(jnp.multiply, p["datt"], p["dec"])
            p["qd_do"] = each(_bdot_tn, p["qd"], dout)
            p["att_do"] = each(_bdot_tn, p["att"], dout)
            return p

        def after_state(p, ds):
            q, k, v, st, inv, bcol = p["q"], p["k"], p["v"], p["st"], p["inv"], p["bcol"]
            eg, el, kb, u, w = p["eg"], p["el"], p["kb"], p["u"], p["w"]
            dvn = each(lambda a, kk, x: a + _bdot(kk, x), p["att_do"], p["kd"], ds)
            dkd = each(_bdot_nt, p["vn"], ds)
            dw = each(lambda a, x: -_bdot_nt(a, x), dvn, st)
            new_ds = each(lambda x, e, a, ww, dv_: x * e + a - _bdot_tn(ww, dv_), ds, p["elast"], p["qd_do"], w, dvn)
            dglast = each(lambda e, x, d: e * jnp.sum(rsum(x.astype(f32) * d), axis=0, keepdims=True), p["elast"], st, ds)
            dr = each(lambda i, a, b: _bdot_tn(i, jnp.concatenate([a, b], axis=1)), inv, dvn, dw)
            dvb = [a[:, :DH] for a in dr]
            dkbg = [a[:, DH:] for a in dr]
            dlow = each(lambda a, b, x, y: -jnp.where(r > c, _bdot_nt(a, b) + _bdot_nt(x, y), 0.0), dvb, u, dkbg, w)
            dkk = each(jnp.multiply, dlow, p["dec"])
            mm = each(lambda a, b, x, y: a * b + x * y, dlow, p["low"], p["datt"], p["att"])
            dkb = each(lambda a, kk, b, e: _bdot(a, kk) + b * e, dkk, k, dkbg, eg)
            dk = each(lambda a, b, x, y, d, e, f, g: _bdot_tn(a, b) + _bdot_tn(x, y) + d * e + f * g,
                      dkk, kb, p["dqk"], q, dkd, el, dkb, bcol)
            dq = each(lambda a, kk, d, e: _bdot(a, kk) + d * e, p["dqk"], k, p["dqd"], eg)
            dv = each(jnp.multiply, dvb, bcol)
            dbeta = each(lambda a, b, x, y: rsum(a * b) + rsum(x * y), dkb, k, dvb, v)
            deg = each(lambda a, b, x, y: rsum(a * b) + rsum(x * y), dkbg, kb, p["dqd"], q)
            delc = each(lambda a, b, e: rsum(a * b) * e, dkd, k, el)
            dgc = each(lambda m, a, e, d: rsum(m) - rsum(eye * jnp.sum(m, axis=0, keepdims=True)) + a * e - d,
                       mm, deg, eg, delc)
            dgc = each(lambda g, d, l: g + jnp.where(p["rcol"] == CH - 1, jnp.sum(d, axis=0, keepdims=True) + l, 0.0),
                       dgc, delc, dglast)
            dg_acc = jnp.zeros((CH, 128), f32)
            db_acc = jnp.zeros((CH, 128), f32)
            rows = p["rows"]
            for h in hs:
                dq_ref[rows, h * DH:(h + 1) * DH] = dq[h]
                dk_ref[rows, h * DH:(h + 1) * DH] = dk[h]
                dv_ref[rows, h * DH:(h + 1) * DH] = dv[h]
                dg_acc = dg_acc + dgc[h] * (lane == h).astype(f32)
                db_acc = db_acc + dbeta[h] * (lane == H + h).astype(f32)
            dgb_ref[rows, :] = _hdot((r <= c).astype(f32), dg_acc) + db_acc
            return new_ds

        order = list(reversed(range(GDN_STEP)))
        pre = [before_state(s) for s in order]
        ds = [dstate[h] for h in hs]
        for p in pre:
            ds = after_state(p, ds)
        for h in hs:
            dstate[h] = ds[h]

    tb = GDN_STEP * CH
    n_steps = t // tb
    rev = lambda i: (n_steps - 1 - i, 0)
    rev4 = lambda i: (n_steps - 1 - i, 0, 0, 0)
    return pl.pallas_call(
        body, name="gdn_bwd", grid=(n_steps,),
        in_specs=[pl.BlockSpec((tb, D), rev), pl.BlockSpec((tb, D), rev), pl.BlockSpec((tb, D), rev),
                  pl.BlockSpec((tb, 128), rev), pl.BlockSpec((tb, D), rev),
                  pl.BlockSpec((GDN_STEP, H, DH, DH), rev4), pl.BlockSpec((GDN_STEP, H, CH, CH), rev4)],
        out_specs=[pl.BlockSpec((tb, D), rev), pl.BlockSpec((tb, D), rev), pl.BlockSpec((tb, D), rev),
                   pl.BlockSpec((tb, 128), rev)],
        out_shape=[jax.ShapeDtypeStruct((t, D), f32)] * 3 + [jax.ShapeDtypeStruct((t, 128), f32)],
        scratch_shapes=[pltpu.VMEM((H, DH, DH), f32)],
        compiler_params=_params(1),
    )(qn, kn, vc, gbeta, do, s_all, t_all)


def _pad_rows(w, rows=8):
    return jnp.pad(w, ((0, rows - w.shape[0]), (0, 0)))


_REST = ("w_up", "w_a_out", "w_b_out", "w_o", "w_down")


def _local_step(x, tgt, w, comm=None):
    g1 = w["norm_mix_g"].reshape(1, D)
    if comm is None:
        h1 = _rms_fwd(x, g1, name="rms1_fwd")
    else:
        h1, gathered = _rms_fwd(x, g1, name="rms1_fwd", exchange=comm.gather_first())
        w = {**w, **comm.finish_first(gathered)}
    w1, w2 = w["w1"], w["w2"]
    wa = _pad_rows(w["conv_a_w"])
    wg = _pad_rows(w["gdn_conv_w"])
    wf = _pad_rows(w["ffn_conv_w"])
    alog = jnp.pad(w["gdn_A_log"].reshape(1, H), ((0, 0), (0, 128 - H)))
    dtb = jnp.pad(w["gdn_dt_bias"].reshape(1, H), ((0, 0), (0, 128 - H)))
    g2 = w["norm_ffn_g"].reshape(1, D)
    g3 = w["norm_final_g"].reshape(1, D)
    gn = w["gdn_norm_g"].reshape(1, DH)

    if comm is None:
        pg = _matmul(h1, w1, name="mm_in", cols=(0, 6 * D), out_dtype=bf16)
        pq = _matmul(h1, w1, name="mm_in_qkv", cols=(6 * D, 3 * D))
    else:
        pg, gathered = _matmul(h1, w1, name="mm_in", cols=(0, 6 * D), out_dtype=bf16, exchange=comm.gather_rest())
        pq, gathered = _matmul(h1, w1, name="mm_in_qkv", cols=(6 * D, 3 * D), exchange=_gather_forward_exchange(gathered))
        w = {**w, **comm.finish_gather(gathered)}
    ya_in, qn, kn, vc, gbeta, p2 = _pre_fwd(pg, pq, h1, w2, wa, wg, alog, dtb)
    o, s_all, t_all = _gdn_fwd(qn, kn, vc, gbeta)
    yb_in = _post_fwd(o, pg, gn)
    ya = _matmul(ya_in, w["w_a_out"], name="mm_a", out_dtype=bf16)
    yb = _matmul(yb_in, w["w_b_out"], name="mm_b", out_dtype=bf16)
    mix = _mix_fwd(ya, yb, pg)
    x2 = _matmul(mix, w["w_o"], name="mm_o", add=x)
    h2 = _rms_fwd(x2, g2, name="rms2_fwd")
    up = _matmul(h2, w["w_up"], nt=True, name="mm_up", tn=DFF // 2, out_dtype=bf16)
    act = _ffn_fwd(up, wf)
    x3 = _matmul(act, w["w_down"], name="mm_down", add=x2, tm=512)
    loss_p, dx3, dx3b, dg3 = _final(x3, tgt, g3)

    grads = {"norm_final_g": dg3}
    dact = _matmul(dx3b, w["w_down"], nt=True, name="mm_down_dx", tm=512, tn=DFF, out_dtype=bf16)
    grads["w_down"] = _matmul_tn(act, dx3b, name="mm_down_dw", tm=DFF // 2)
    dc, dwf = _ffn_bwd1(dact, up, wf)
    grads["ffn_conv_w"] = dwf
    dup = _ffn_bwd2(dc, wf)
    dh2 = _matmul(dup, w["w_up"], name="mm_up_dx", tk=DFF)
    grads["w_up"] = _matmul_tn(dup, h2, name="mm_up_dw", tm=DFF // 2)
    dx2, dx2b, dg2 = _rms_bwd(dh2, x2, g2, dx3, name="rms2_bwd")
    grads["norm_ffn_g"] = dg2
    dmix = _matmul(dx2b, w["w_o"], nt=True, name="mm_o_dx", out_dtype=bf16)
    grads["w_o"] = _matmul_tn(mix, dx2b, name="mm_o_dw")
    dya, dyb, dgates = _mix_bwd(dmix, ya, yb, pg)
    dya_in = _matmul(dya, w["w_a_out"], nt=True, name="mm_a_dx", out_dtype=bf16)
    grads["w_a_out"] = _matmul_tn(ya_in, dya, name="mm_a_dw")
    dyb_in = _matmul(dyb, w["w_b_out"], nt=True, name="mm_b_dx")
    grads["w_b_out"] = _matmul_tn(yb_in, dyb, name="mm_b_dw")
    do, dz, dgn = _post_bwd(dyb_in, o, pg, gn)
    grads["gdn_norm_g"] = dgn
    dqn, dkn, dvc, dgb = _gdn_bwd(qn, kn, vc, gbeta, do, s_all, t_all)
    dbg, dca, dc4, dp2, dwa, dwg, dal, ddt, grads["w2"] = _pre_bwd1(pg, pq, p2, dya_in, dqn, dkn, dvc, dgb, gbeta, h1,
                                                                    wa, wg, alog, dtb)
    grads["conv_a_w"] = dwa
    grads["gdn_conv_w"] = dwg
    grads["gdn_A_log"] = dal
    grads["gdn_dt_bias"] = ddt
    if comm is None:
        dp1 = _pre_bwd2(dca, dc4, pg, dbg, dz, dgates, wa, wg)
        grads["w1"] = _matmul_tn(dp1, h1, name="mm_in_dw")
        dh1 = _matmul(dp1, w1, nt=True, name="mm_in_dx", tm=512, tk=NW1 // 2)
    else:
        exchange, blocks = comm.reduce_halves(_REST, grads)
        dp1, recv = _pre_bwd2(dca, dc4, pg, dbg, dz, dgates, wa, wg, exchange=exchange)
        exchange, sums = comm.reduce_sums(_REST, blocks, recv)
        grads["w1"], recv = _matmul_tn(dp1, h1, name="mm_in_dw", exchange=exchange)
        comm.finish_reduce(_REST, sums, recv)
        exchange, blocks = comm.reduce_halves(("w_in",), grads)
        exchange, sums = comm.reduce_sums(("w_in",), blocks, _run_exchange(exchange, name="rs_sibling_w_in"))
        dh1, recv = _matmul(dp1, w1, nt=True, name="mm_in_dx", tm=512, tk=NW1 // 2, exchange=exchange)
        comm.finish_reduce(("w_in",), sums, recv)
    dx, _, dg1 = _rms_bwd(dh1, x, g1, dx2, name="rms1_bwd", more=(dp2, w2))
    grads["norm_mix_g"] = dg1
    return loss_p, dx, grads


_ANY = pl.BlockSpec(memory_space=pl.ANY)


def _remote(src, dst, send_sem, recv_sem, to):
    return pltpu.make_async_remote_copy(src_ref=src, dst_ref=dst, send_sem=send_sem, recv_sem=recv_sem,
                                        device_id=to, device_id_type=MESH)


def _run_exchange(exchange, *, name):
    arrays, shapes, sems, start, wait = exchange
    n_in, n_out = len(arrays), len(shapes)

    def body(*refs):
        start(refs[:n_in], refs[n_in:n_in + n_out], refs[n_in + n_out:])
        wait(refs[:n_in], refs[n_in:n_in + n_out], refs[n_in + n_out:])

    return pl.pallas_call(body, name=name, out_shape=list(shapes), in_specs=[_ANY] * n_in, out_specs=[_ANY] * n_out,
                          scratch_shapes=list(sems))(*arrays)


def _gather_exchange(shards):
    n = len(shards)

    def copies(x_refs, out_refs, sems):
        send_sems, recv_sems, local_sems = sems
        x, y, c = lax.axis_index("x"), lax.axis_index("y"), lax.axis_index("c")

        def flip(v, b):
            return v + b - 2 * v * b

        me, sibling = (x, y, c), (x, y, 1 - c)
        chip1, chip2, diag = (flip(x, 1 - c), flip(y, c)), (flip(x, c), flip(y, 1 - c)), (1 - x, 1 - y)

        def copy(a, k, blk, to, from_input=False):
            dst = out_refs[a].at[4 * blk[0] + 2 * blk[1] + blk[2]]
            return _remote(x_refs[a] if from_input else dst, dst, send_sems.at[a, k], recv_sems.at[a, k], to)

        mine = [pltpu.make_async_copy(x_refs[a], out_refs[a].at[4 * x + 2 * y + c], local_sems.at[a]) for a in range(n)]
        first = []
        for a in range(n):
            first += [copy(a, 0, me, sibling, from_input=True), copy(a, 1, me, (*chip1, c), from_input=True),
                      copy(a, 2, me, (*chip2, c), from_input=True)]
        return copy, mine, first, me, sibling, chip1, chip2, diag, c

    def start(x_refs, out_refs, sems):
        _, mine, first, *_ = copies(x_refs, out_refs, sems)
        for cp in mine + first:
            cp.start()

    def wait(x_refs, out_refs, sems):
        copy, mine, first, me, sibling, chip1, chip2, diag, c = copies(x_refs, out_refs, sems)
        passed = []

        def pass_on(cp):
            passed.append(cp)
            cp.start()

        for a in range(n):
            copy(a, 1, (*chip1, c), me).wait_recv()
            pass_on(copy(a, 3, (*chip1, c), (*chip2, c)))
            pass_on(copy(a, 4, (*chip1, c), sibling))
        for a in range(n):
            copy(a, 2, (*chip2, c), me).wait_recv()
            pass_on(copy(a, 5, (*chip2, c), sibling))
        for a in range(n):
            copy(a, 3, (*diag, c), me).wait_recv()
            pass_on(copy(a, 6, (*diag, c), sibling))
        for a in range(n):
            copy(a, 0, sibling, me).wait_recv()
            copy(a, 4, (*chip2, 1 - c), me).wait_recv()
            copy(a, 5, (*chip1, 1 - c), me).wait_recv()
            copy(a, 6, (*diag, 1 - c), me).wait_recv()
        for cp in first + passed:
            cp.wait_send()
        for cp in mine:
            cp.wait()

    shapes = [jax.ShapeDtypeStruct((N_DEV, *s.shape), s.dtype) for s in shards]
    sems = [pltpu.SemaphoreType.DMA((n, 7)), pltpu.SemaphoreType.DMA((n, 7)), pltpu.SemaphoreType.DMA((n,))]
    return shards, shapes, sems, start, wait


def _gather_direct_exchange(shards):
    n = len(shards)

    def copies(x_refs, out_refs, sems):
        send_sems, recv_sems, local_sems = sems
        x, y, c = lax.axis_index("x"), lax.axis_index("y"), lax.axis_index("c")
        targets = [(x, y, 1 - c), (1 - x, y, c), (x, 1 - y, c), (1 - x, 1 - y, c)]
        local, sends, recvs = [], [], []
        for a in range(n):
            mine = out_refs[a].at[4 * x + 2 * y + c]
            local.append(pltpu.make_async_copy(x_refs[a], mine, local_sems.at[a]))
            for k, to in enumerate(targets):
                theirs = out_refs[a].at[4 * to[0] + 2 * to[1] + to[2]]
                sends.append(_remote(x_refs[a], mine, send_sems.at[a, k], recv_sems.at[a, k], to))
                recvs.append(_remote(theirs, theirs, send_sems.at[a, k], recv_sems.at[a, k], to))
        return local, sends, recvs

    def start(x_refs, out_refs, sems):
        local, sends, _ = copies(x_refs, out_refs, sems)
        for cp in local + sends:
            cp.start()

    def wait(x_refs, out_refs, sems):
        local, sends, recvs = copies(x_refs, out_refs, sems)
        for cp in recvs:
            cp.wait_recv()
        for cp in sends:
            cp.wait_send()
        for cp in local:
            cp.wait()

    shapes = [jax.ShapeDtypeStruct((N_DEV, *s.shape), s.dtype) for s in shards]
    sems = [pltpu.SemaphoreType.DMA((n, 4)), pltpu.SemaphoreType.DMA((n, 4)), pltpu.SemaphoreType.DMA((n,))]
    return shards, shapes, sems, start, wait


def _gather_forward_exchange(gathered):
    n = len(gathered)

    def copies(_, out_refs, sems):
        send_sems, recv_sems = sems
        x, y, c = lax.axis_index("x"), lax.axis_index("y"), lax.axis_index("c")
        sibling = (x, y, 1 - c)
        sends, recvs = [], []
        for a in range(n):
            for j, (px, py) in enumerate([(1 - x, y), (x, 1 - y), (1 - x, 1 - y)]):
                mine = out_refs[a].at[4 * px + 2 * py + c]
                theirs = out_refs[a].at[4 * px + 2 * py + 1 - c]
                sends.append(_remote(mine, mine, send_sems.at[a, j], recv_sems.at[a, j], sibling))
                recvs.append(_remote(theirs, theirs, send_sems.at[a, j], recv_sems.at[a, j], sibling))
        return sends, recvs

    def start(in_refs, out_refs, sems):
        for cp in copies(in_refs, out_refs, sems)[0]:
            cp.start()

    def wait(in_refs, out_refs, sems):
        sends, recvs = copies(in_refs, out_refs, sems)
        for cp in recvs:
            cp.wait_recv()
        for cp in sends:
            cp.wait_send()

    shapes = [jax.ShapeDtypeStruct(g.shape, g.dtype) for g in gathered]
    sems = [pltpu.SemaphoreType.DMA((n, 3)), pltpu.SemaphoreType.DMA((n, 3))]
    return gathered, shapes, sems, start, wait, True


def _chips_exchange(hsums):
    n = len(hsums)

    def copies(h_refs, out_refs, sems):
        send_sems, recv_sems = sems
        x, y, c = lax.axis_index("x"), lax.axis_index("y"), lax.axis_index("c")
        chips = [(1 - x, y), (x, 1 - y), (1 - x, 1 - y)]
        return [_remote(h_refs[a].at[2 * px + py], out_refs[a].at[k], send_sems.at[a, k], recv_sems.at[a, k], (px, py, c))
                for a in range(n) for k, (px, py) in enumerate(chips)]

    def start(h_refs, out_refs, sems):
        for cp in copies(h_refs, out_refs, sems):
            cp.start()

    def wait(h_refs, out_refs, sems):
        for cp in copies(h_refs, out_refs, sems):
            cp.wait()

    shapes = [jax.ShapeDtypeStruct((3, *h.shape[1:]), h.dtype) for h in hsums]
    sems = [pltpu.SemaphoreType.DMA((n, 3)), pltpu.SemaphoreType.DMA((n, 3))]
    return hsums, shapes, sems, start, wait


def _sibling_exchange(halves):
    n = len(halves)

    def copies(p_refs, out_refs, sems):
        send_sems, recv_sems = sems
        x, y, c = lax.axis_index("x"), lax.axis_index("y"), lax.axis_index("c")
        return [_remote(p_refs[a], out_refs[a], send_sems.at[a], recv_sems.at[a], (x, y, 1 - c)) for a in range(n)]

    def start(p_refs, out_refs, sems):
        for cp in copies(p_refs, out_refs, sems):
            cp.start()

    def wait(p_refs, out_refs, sems):
        for cp in copies(p_refs, out_refs, sems):
            cp.wait()

    shapes = [jax.ShapeDtypeStruct(h.shape, h.dtype) for h in halves]
    return halves, shapes, [pltpu.SemaphoreType.DMA((n,)), pltpu.SemaphoreType.DMA((n,))], start, wait


_IN_RANGES = ((0, 3 * D, 0, 0), (3 * D, 6 * D, 0, 6 * D), (6 * D, 7 * D, 0, 3 * D), (7 * D, 7 * D + 16, 1, 0),
              (7 * D + 16, 9 * D + 16, 0, 4 * D))


def _col_pieces(width, ranges):
    pieces = []
    for d in range(N_DEV):
        lo, hi = d * width, (d + 1) * width
        for glo, ghi, mat, mlo in ranges:
            a, b = max(lo, glo), min(hi, ghi)
            if a < b:
                pieces.append((d, a - lo, b - lo, mat, mlo + a - glo))
    return pieces


def _cols_to_matrices(g, ranges, out_widths, *, name):
    _, rows, width = g.shape
    tb = 128
    pieces = _col_pieces(width, ranges)
    covered = [sum(p[2] - p[1] for p in pieces if p[3] == m) for m in range(len(out_widths))]

    def body(g_ref, *o_refs):
        for m, o_ref in enumerate(o_refs):
            if covered[m] < out_widths[m]:
                o_ref[...] = jnp.zeros_like(o_ref)
        for d, b0, b1, m, m0 in pieces:
            o_refs[m][:, m0:m0 + b1 - b0] = g_ref[d, :, b0:b1]

    return pl.pallas_call(
        body, name=name, grid=(rows // tb,), in_specs=[pl.BlockSpec((N_DEV, tb, width), lambda i: (0, i, 0))],
        out_specs=[pl.BlockSpec((tb, wo), lambda i: (i, 0)) for wo in out_widths],
        out_shape=[jax.ShapeDtypeStruct((rows, wo), g.dtype) for wo in out_widths], compiler_params=_params(1),
    )(g)


def _transposed_matrices_to_blocks(mats, ranges, width, *, name):
    rows = mats[0].shape[1]
    pieces = _col_pieces(width, ranges)

    def body(*refs):
        m_refs, g_ref = refs[:-1], refs[-1]
        for d, b0, b1, m, m0 in pieces:
            g_ref[d, b0:b1, :] = m_refs[m][m0:m0 + b1 - b0, :]

    return pl.pallas_call(
        body, name=name, grid=(rows // 128,),
        in_specs=[pl.BlockSpec((mt.shape[0], 128), lambda i: (0, i)) for mt in mats],
        out_specs=pl.BlockSpec((N_DEV, width, 128), lambda i: (0, 0, i)),
        out_shape=jax.ShapeDtypeStruct((N_DEV, width, rows), mats[0].dtype), compiler_params=_params(1),
    )(*mats)


def _row_block(rows):
    return 128 if rows % 128 == 0 else rows


def _half_bf16(g4, c_other, *, name):
    _, _, rows, width = g4.shape
    tb = _row_block(rows)

    def body(c_ref, p_ref, o_ref):
        o_ref[0] = p_ref[0, 0].astype(bf16)

    grid_spec = pltpu.PrefetchScalarGridSpec(
        num_scalar_prefetch=1, grid=(4, rows // tb),
        in_specs=[pl.BlockSpec((1, 1, tb, width), lambda j, i, c_ref: (j, c_ref[0], i, 0))],
        out_specs=pl.BlockSpec((1, tb, width), lambda j, i, c_ref: (j, i, 0)))
    return pl.pallas_call(
        body, name=name, grid_spec=grid_spec, out_shape=jax.ShapeDtypeStruct((4, rows, width), bf16),
        compiler_params=_params(2),
    )(c_other, g4)


def _pair_sum(g4, recv, c_me, *, name):
    _, _, rows, width = g4.shape
    tb = _row_block(rows)

    def body(c_ref, p_ref, r_ref, o_ref, ob_ref):
        s = p_ref[0, 0] + r_ref[0].astype(f32)
        o_ref[0] = s
        ob_ref[0] = s.astype(bf16)

    blk = pl.BlockSpec((1, tb, width), lambda j, i, c_ref: (j, i, 0))
    grid_spec = pltpu.PrefetchScalarGridSpec(
        num_scalar_prefetch=1, grid=(4, rows // tb),
        in_specs=[pl.BlockSpec((1, 1, tb, width), lambda j, i, c_ref: (j, c_ref[0], i, 0)), blk],
        out_specs=[blk, blk])
    return pl.pallas_call(
        body, name=name, grid_spec=grid_spec,
        out_shape=[jax.ShapeDtypeStruct((4, rows, width), f32), jax.ShapeDtypeStruct((4, rows, width), bf16)],
        compiler_params=_params(2),
    )(c_me, g4, recv)


def _adam_shard(hsum, recv, chip, w, m, v, *, name):
    _, rows, width = w.shape
    tb = _row_block(rows)

    def body(j_ref, h_ref, r_ref, w_ref, m_ref, v_ref, g_out, d_out, m_out, v_out):
        g = ((h_ref[0] + r_ref[0].astype(f32)) + r_ref[1].astype(f32)) + r_ref[2].astype(f32)
        delta, mn, vn = _adam_math(w_ref[0], g, m_ref[0], v_ref[0])
        g_out[0] = g
        d_out[0] = delta
        m_out[0] = mn
        v_out[0] = vn

    blk = pl.BlockSpec((1, tb, width), lambda i, j_ref: (0, i, 0))
    grid_spec = pltpu.PrefetchScalarGridSpec(
        num_scalar_prefetch=1, grid=(rows // tb,),
        in_specs=[pl.BlockSpec((1, tb, width), lambda i, j_ref: (j_ref[0], i, 0)),
                  pl.BlockSpec((3, tb, width), lambda i, j_ref: (0, i, 0)), blk, blk, blk],
        out_specs=[blk, blk, blk, blk])
    return pl.pallas_call(
        body, name=name, grid_spec=grid_spec, out_shape=[jax.ShapeDtypeStruct(w.shape, f32)] * 4,
        compiler_params=_params(1),
    )(chip, hsum, recv, w, m, v)


def _sum_shard(hsum, recv, chip, *, name):
    _, rows, width = hsum.shape
    tb = _row_block(rows)

    def body(j_ref, h_ref, r_ref, g_out):
        g_out[...] = ((h_ref[0] + r_ref[0].astype(f32)) + r_ref[1].astype(f32)) + r_ref[2].astype(f32)

    grid_spec = pltpu.PrefetchScalarGridSpec(
        num_scalar_prefetch=1, grid=(rows // tb,),
        in_specs=[pl.BlockSpec((1, tb, width), lambda i, j_ref: (j_ref[0], i, 0)),
                  pl.BlockSpec((3, tb, width), lambda i, j_ref: (0, i, 0))],
        out_specs=pl.BlockSpec((tb, width), lambda i, j_ref: (i, 0)))
    return pl.pallas_call(body, name=name, grid_spec=grid_spec, out_shape=jax.ShapeDtypeStruct((rows, width), f32),
                          compiler_params=_params(1))(chip, hsum, recv)


def _adam_columns(g, w, m, v, *, name):
    cols, _, rows = w.shape
    tb = cols // 2

    def body(g_ref, w_ref, m_ref, v_ref, d_out, m_out, v_out):
        delta, mn, vn = _adam_math(w_ref[...], g_ref[...], m_ref[...], v_ref[...])
        d_out[...] = delta
        m_out[...] = mn
        v_out[...] = vn

    blk = pl.BlockSpec((tb, 1, rows), lambda i: (i, 0, 0))
    return pl.pallas_call(
        body, name=name, grid=(cols // tb,), in_specs=[blk] * 4, out_specs=[blk] * 3,
        out_shape=[jax.ShapeDtypeStruct(w.shape, f32)] * 3, compiler_params=_params(1),
    )(g, w, m, v)


R_SMALL = 8 + 8 * N_DEV
_SMALL_LANES = {"gdn_norm_g": (0, DH), "gdn_A_log": (DH, DH + H), "gdn_dt_bias": (2 * DH, 2 * DH + H)}
_LOSS_LANE = 3 * DH


def _pack_small(dg1, dg2, dg3, dgn, dal, ddt, loss_p, dwa, dwg, dwf):
    def body(dg1_ref, dg2_ref, dg3_ref, dgn_ref, dal_ref, ddt_ref, loss_ref, dwa_ref, dwg_ref, dwf_ref, o_ref):
        def total(ref):
            return jnp.sum(ref[...], axis=0, keepdims=True)

        o_ref[...] = jnp.zeros_like(o_ref)
        o_ref[0:1, :] = total(dg1_ref)
        o_ref[1:2, :] = total(dg2_ref)
        o_ref[2:3, :] = total(dg3_ref)
        o_ref[3:4, 0:DH] = total(dgn_ref)
        o_ref[3:4, DH:2 * DH] = total(dal_ref)
        o_ref[3:4, 2 * DH:3 * DH] = total(ddt_ref)
        o_ref[3:4, 3 * DH:4 * DH] = total(loss_ref)
        for d in range(N_DEV):
            base = 8 + 8 * d
            o_ref[base:base + 3, 0:128] = dwa_ref[0:3, 128 * d:128 * (d + 1)]
            o_ref[base:base + 4, 128:512] = dwg_ref[0:4, 384 * d:384 * (d + 1)]
            o_ref[base + 4:base + 7, 0:704] = dwf_ref[0:3, 704 * d:704 * (d + 1)]

    return pl.pallas_call(body, name="pack_small", out_shape=jax.ShapeDtypeStruct((R_SMALL, D), f32))(
        dg1, dg2, dg3, dgn, dal, ddt, loss_p, dwa, dwg, dwf)


_SMALL = ("norm_mix_g", "norm_ffn_g", "norm_final_g", "gdn_norm_g", "gdn_A_log", "gdn_dt_bias",
          "conv_a_w", "gdn_conv_w", "ffn_conv_w")


def _adam_small(gath, me, w, m, v):
    arrays = [t[n] for n in _SMALL for t in (w, m, v)]

    def body(me_ref, ga_ref, gb_ref, *refs):
        ins, outs = refs[:len(arrays)], refs[len(arrays):]
        ga, gb = ga_ref[0], gb_ref[0]
        for s in range(1, N_DEV):
            ga = ga + ga_ref[s]
            gb = gb + gb_ref[s]
        grads = {"norm_mix_g": ga[0:1, :], "norm_ffn_g": ga[1:2, :], "norm_final_g": ga[2:3, :],
                 "conv_a_w": gb[0:3, 0:128], "gdn_conv_w": gb[0:4, 128:512], "ffn_conv_w": gb[4:7, 0:704]}
        for n, (lo, hi) in _SMALL_LANES.items():
            grads[n] = ga[3:4, lo:hi]
        for i, n in enumerate(_SMALL):
            three_d = len(w[n].shape) == 3
            wv, mv, vv = (r[0] if three_d else r[...] for r in ins[3 * i:3 * i + 3])
            delta, mn, vn = _adam_math(wv, grads[n], mv, vv)
            for o_ref, val in zip(outs[4 * i:4 * i + 4], (grads[n], delta, mn, vn)):
                if three_d:
                    o_ref[0] = val
                else:
                    o_ref[...] = val
        outs[-1][...] = ga[3:4, _LOSS_LANE:_LOSS_LANE + 1]

    def whole(shape):
        return pl.BlockSpec(shape, lambda i, me_ref: (0,) * len(shape))

    grid_spec = pltpu.PrefetchScalarGridSpec(
        num_scalar_prefetch=1, grid=(1,),
        in_specs=[pl.BlockSpec((N_DEV, 8, D), lambda i, me_ref: (0, 0, 0)),
                  pl.BlockSpec((N_DEV, 8, D), lambda i, me_ref: (0, 1 + me_ref[0], 0))] + [whole(a.shape) for a in arrays],
        out_specs=[whole(w[n].shape) for n in _SMALL for _ in range(4)] + [whole((1, 1))])
    res = pl.pallas_call(
        body, name="adam_small", grid_spec=grid_spec,
        out_shape=[jax.ShapeDtypeStruct(w[n].shape, f32) for n in _SMALL for _ in range(4)]
        + [jax.ShapeDtypeStruct((1, 1), f32)],
        compiler_params=_params(1),
    )(me, gath, gath, *arrays)
    return {n: tuple(res[4 * i:4 * i + 4]) for i, n in enumerate(_SMALL)}, res[-1]


def _adam_math(w, g, m, v):
    m = ADAM_B1 * m + (1.0 - ADAM_B1) * g
    v = ADAM_B2 * v + (1.0 - ADAM_B2) * jnp.square(g)
    m_hat = m / (1.0 - ADAM_B1 ** ADAM_STEP)
    v_hat = v / (1.0 - ADAM_B2 ** ADAM_STEP)
    delta = -ADAM_LR * (m_hat / (jnp.sqrt(v_hat) + ADAM_EPS) + ADAM_WD * w)
    return delta, m, v


_WEIGHTS = ("norm_mix_g", "w_in", "conv_a_w", "gdn_conv_w", "gdn_A_log", "gdn_dt_bias", "gdn_norm_g", "w_a_out",
            "w_b_out", "w_o", "norm_ffn_g", "w_up", "ffn_conv_w", "w_down", "norm_final_g")
_CONVS = ("conv_a_w", "gdn_conv_w", "ffn_conv_w")


class _StepExchanges:
    def __init__(self, wts, mom, var, c_me, chip):
        self.wts, self.mom, self.var, self.c_me, self.chip = wts, mom, var, c_me, chip
        self.results = {}

    def gather_first(self):
        return _gather_exchange([self.wts["w_in"][0].astype(bf16)] + [self.wts[n][0] for n in _CONVS])

    def finish_first(self, gathered):
        g_in, gc_a, gc_g, gc_f = gathered
        w1, w2 = _cols_to_matrices(g_in, _IN_RANGES, (NW1, 128), name="relay_w_in")
        return {"w1": w1, "w2": w2, "conv_a_w": gc_a.transpose(1, 0, 2).reshape(3, D),
                "gdn_conv_w": gc_g.transpose(1, 0, 2).reshape(4, 3 * D),
                "ffn_conv_w": gc_f.transpose(1, 0, 2).reshape(3, 2 * DFF)}

    def gather_rest(self):
        return _gather_direct_exchange([self.wts[n][0].astype(bf16) for n in _REST])

    def finish_gather(self, gathered):
        g_up, g_a, g_b, g_o, g_down = gathered
        return {"w_up": g_up.reshape(2 * DFF, D), "w_a_out": g_a.reshape(D, D), "w_b_out": g_b.reshape(D, D),
                "w_o": g_o.reshape(D, D), "w_down": g_down.reshape(DFF, D)}

    def reduce_halves(self, names, grads):
        blocks = []
        for n in names:
            if n == "w_in":
                g = _transposed_matrices_to_blocks([grads["w1"], grads["w2"]], _IN_RANGES, R_IN, name="relay_dw_in")
                blocks.append(g.reshape(4, 2, R_IN, D))
            else:
                blocks.append(grads[n].reshape(4, 2, *self.wts[n].shape[1:]))
        return _sibling_exchange([_half_bf16(g, 1 - self.c_me, name="rs_half_" + n) for n, g in zip(names, blocks)]), blocks

    def reduce_sums(self, names, blocks, recv):
        sums = [_pair_sum(g, r, self.c_me, name="rs_sum_" + n) for n, g, r in zip(names, blocks, recv)]
        return _chips_exchange([s[1] for s in sums]), [s[0] for s in sums]

    def finish_reduce(self, names, sums, recv):
        for n, s, r in zip(names, sums, recv):
            if n == "w_in":
                g = _sum_shard(s, r, self.chip, name="rs_total_w_in")[:, None, :]
                w, m, v = (jnp.transpose(t[n], (2, 0, 1)) for t in (self.wts, self.mom, self.var))
                res = (g, *_adam_columns(g, w, m, v, name="adam_w_in"))
                self.results[n] = tuple(jnp.transpose(a, (1, 2, 0)) for a in res)
            else:
                self.results[n] = _adam_shard(s, r, self.chip, self.wts[n], self.mom[n], self.var[n], name="adam_" + n)


def kernel(x, norm_mix_g, w_in, conv_a_w, gdn_conv_w, gdn_A_log, gdn_dt_bias, gdn_norm_g, w_a_out, w_b_out, w_o, norm_ffn_g, w_up, ffn_conv_w, w_down, norm_final_g, loss_target, m_norm_mix_g, m_w_in, m_conv_a_w, m_gdn_conv_w, m_gdn_A_log, m_gdn_dt_bias, m_gdn_norm_g, m_w_a_out, m_w_b_out, m_w_o, m_norm_ffn_g, m_w_up, m_ffn_conv_w, m_w_down, m_norm_final_g, v_norm_mix_g, v_w_in, v_conv_a_w, v_gdn_conv_w, v_gdn_A_log, v_gdn_dt_bias, v_gdn_norm_g, v_w_a_out, v_w_b_out, v_w_o, v_norm_ffn_g, v_w_up, v_ffn_conv_w, v_w_down, v_norm_final_g):
    wts = dict(zip(_WEIGHTS, (norm_mix_g, w_in, conv_a_w, gdn_conv_w, gdn_A_log, gdn_dt_bias, gdn_norm_g, w_a_out,
                              w_b_out, w_o, norm_ffn_g, w_up, ffn_conv_w, w_down, norm_final_g)))
    mom = dict(zip(_WEIGHTS, (m_norm_mix_g, m_w_in, m_conv_a_w, m_gdn_conv_w, m_gdn_A_log, m_gdn_dt_bias,
                              m_gdn_norm_g, m_w_a_out, m_w_b_out, m_w_o, m_norm_ffn_g, m_w_up, m_ffn_conv_w,
                              m_w_down, m_norm_final_g)))
    var = dict(zip(_WEIGHTS, (v_norm_mix_g, v_w_in, v_conv_a_w, v_gdn_conv_w, v_gdn_A_log, v_gdn_dt_bias,
                              v_gdn_norm_g, v_w_a_out, v_w_b_out, v_w_o, v_norm_ffn_g, v_w_up, v_ffn_conv_w,
                              v_w_down, v_norm_final_g)))
    cx, cy, cc = lax.axis_index("x"), lax.axis_index("y"), lax.axis_index("c")
    c_me = jnp.reshape(cc, (1,)).astype(jnp.int32)
    chip = jnp.reshape(2 * cx + cy, (1,)).astype(jnp.int32)
    me = jnp.reshape(4 * cx + 2 * cy + cc, (1,)).astype(jnp.int32)

    def with_up_transposed(t):
        return {**t, "w_up": jnp.swapaxes(t["w_up"], 1, 2)}

    comm = _StepExchanges(with_up_transposed(wts), with_up_transposed(mom), with_up_transposed(var), c_me, chip)
    replicated = {n: wts[n] for n in ("norm_mix_g", "norm_ffn_g", "norm_final_g", "gdn_norm_g", "gdn_A_log", "gdn_dt_bias")}
    loss_p, dx, grads = _local_step(x[0], loss_target[0], replicated, comm)
    res = comm.results
    res["w_up"] = tuple(jnp.swapaxes(a, 1, 2) for a in res["w_up"])

    small = _pack_small(grads["norm_mix_g"], grads["norm_ffn_g"], grads["norm_final_g"], grads["gdn_norm_g"],
                        grads["gdn_A_log"], grads["gdn_dt_bias"], loss_p, grads["conv_a_w"], grads["gdn_conv_w"],
                        grads["ffn_conv_w"])
    (small_all,) = _run_exchange(_gather_exchange([small]), name="ag_small")

    def raw(t):
        return {n: t[n].reshape(1, D) if n == "norm_final_g" else t[n] for n in _SMALL}

    res_small, loss = _adam_small(small_all, me, raw(wts), raw(mom), raw(var))
    for n in _SMALL:
        res[n] = tuple(a.reshape(wts[n].shape) for a in res_small[n])
    outs = [[res[n][i] for n in _WEIGHTS] for i in range(4)]
    return (loss.reshape(()), dx[None], *outs[0], *outs[1], *outs[2], *outs[3])
```

```python
import jax
import jax.numpy as jnp
from jax import lax
from jax.experimental import pallas as pl
from jax.experimental.pallas import tpu as pltpu

f32 = jnp.float32
bf16 = jnp.bfloat16

D = 1024
H = 8
DH = 128
CH = 64
GDN_STEP = 2
DFF = 2816
NW1 = 9216
EPS = 1e-6
N_DEV = 8

ADAM_LR = 0.001
ADAM_B1 = 0.9
ADAM_B2 = 0.999
ADAM_EPS = 1e-08
ADAM_WD = 0.01
ADAM_STEP = 10

VMEM_LIMIT_BYTES = 56 * 1024 * 1024

R_IN, R_UP = 1154, 704

_HI = lax.Precision.HIGHEST
MESH = pl.DeviceIdType.MESH


def _params(n_grid):
    return pltpu.CompilerParams(dimension_semantics=("arbitrary",) * n_grid, vmem_limit_bytes=VMEM_LIMIT_BYTES)


def _bdot(a, b):
    return jnp.dot(a.astype(bf16), b.astype(bf16), preferred_element_type=f32)


def _bdot_nt(a, b):
    return lax.dot_general(a.astype(bf16), b.astype(bf16), (((1,), (1,)), ((), ())), preferred_element_type=f32)


def _bdot_tn(a, b):
    return lax.dot_general(a.astype(bf16), b.astype(bf16), (((0,), (0,)), ((), ())), preferred_element_type=f32)


def _hdot(a, b):
    return jnp.dot(a, b, preferred_element_type=f32, precision=_HI)


def _idot(a, b):
    return jnp.dot(a, b, preferred_element_type=f32, precision=lax.Precision.HIGH)


def _sigmoid(x):
    return 1.0 / (1.0 + jnp.exp(-x))


def _softplus(x):
    return jnp.maximum(x, 0.0) + jnp.log(1.0 + jnp.exp(-jnp.abs(x)))


def _shift_down(x, halo, j):
    if j == 0:
        return x
    xr = pltpu.roll(x, j, 0)
    hr = pltpu.roll(halo, j, 0)
    r8 = lax.broadcasted_iota(jnp.int32, hr.shape, 0)
    top = jnp.where(r8 < j, hr, xr[:8])
    return jnp.concatenate([top, xr[8:]], axis=0)


def _shift_up(x, halo, j):
    if j == 0:
        return x
    n = x.shape[0]
    xr = pltpu.roll(x, n - j, 0)
    hr = pltpu.roll(halo, 8 - j, 0)
    r8 = lax.broadcasted_iota(jnp.int32, hr.shape, 0)
    bot = jnp.where(r8 >= 8 - j, hr, xr[n - 8:])
    return jnp.concatenate([xr[:n - 8], bot], axis=0)


def _taps_down(x, halo, k):
    return [_shift_down(x, halo, k - 1 - j) for j in range(k)]


def _strip(i, base=0):
    return slice(base + i * 128, base + (i + 1) * 128)


def _strip_taps(x, halo, first, k):
    return _taps_down(x, jnp.where(first, 0.0, halo), k)


def _strip_conv(w_ref, sl, taps):
    out = w_ref[0:1, sl] * taps[0]
    for j in range(1, len(taps)):
        out = out + w_ref[j:j + 1, sl] * taps[j]
    return out


def _strip_weight_grad(dw_ref, sl, dy, taps):
    for j, tap in enumerate(taps):
        dw_ref[j:j + 1, sl] += jnp.sum(dy * tap, axis=0, keepdims=True)


def _strip_conv_up(dy, halo, last, w_ref, sl, k):
    halo = jnp.where(last, 0.0, halo)
    out = w_ref[k - 1:k, sl] * dy
    for j in range(k - 1):
        out = out + w_ref[j:j + 1, sl] * _shift_up(dy, halo, k - 1 - j)
    return out


def _row(tb, w, col=0):
    return pl.BlockSpec((tb, w), lambda i: (i, col))


def _prev(tb, w, col=0, rows=8):
    return pl.BlockSpec((rows, w), lambda i: (jnp.maximum(i * (tb // rows) - 1, 0), col))


def _next(tb, w, n_rows, col=0, rows=8):
    last = n_rows // rows - 1
    return pl.BlockSpec((rows, w), lambda i: (jnp.minimum((i + 1) * (tb // rows), last), col))


def _f32(ref, sl):
    return ref[:, sl].astype(f32)


def _halo_before(ref, sl):
    h = _f32(ref, sl)
    return h[h.shape[0] - 8:]


def _halo_after(ref, sl):
    return _f32(ref, sl)[:8]


def _fixed(shape):
    return pl.BlockSpec(shape, lambda i: (0,) * len(shape))


def _pick(n, prefs):
    for p in prefs:
        if n % p == 0:
            return p
    return n


def _matmul(a, b, *, name, nt=False, add=None, tm=1024, tn=1024, tk=None, out_dtype=f32, cols=None, exchange=None):
    m, kd = a.shape
    col0, n = cols if cols is not None else (0, b.shape[0] if nt else b.shape[1])
    tm = _pick(m, (tm, 512, 256))
    tn = _pick(n, (tn, 1024, 512, 128))
    tk = kd if tk is None else tk
    nk = kd // tk
    assert nk == 1 or out_dtype == f32
    assert col0 % tn == 0 and not (nt and cols)
    j0 = col0 // tn
    dims = (((1,), (1,)), ((), ())) if nt else (((1,), (0,)), ((), ()))

    def body(a_ref, b_ref, *rest):
        o_ref = rest[-1]
        part = lax.dot_general(a_ref[...], b_ref[...], dims, preferred_element_type=f32)
        if nk == 1:
            o_ref[...] = (part if add is None else part + rest[0][...]).astype(out_dtype)
            return
        k = pl.program_id(2)

        @pl.when(k == 0)
        def _():
            o_ref[...] = part if add is None else part + rest[0][...]

        @pl.when(k > 0)
        def _():
            o_ref[...] += part

    b_spec = pl.BlockSpec((tn, tk), lambda i, j, k: (j, k)) if nt else pl.BlockSpec((tk, tn), lambda i, j, k: (k, j + j0))
    in_specs = [pl.BlockSpec((tm, tk), lambda i, j, k: (i, k)), b_spec]
    args = [a, b]
    if add is not None:
        in_specs.append(pl.BlockSpec((tm, tn), lambda i, j, k: (i, j)))
        args.append(add)
    return _call_with_exchange(
        body, exchange, name=name, grid=(m // tm, n // tn, nk), in_specs=in_specs,
        out_specs=pl.BlockSpec((tm, tn), lambda i, j, k: (i, j)),
        out_shape=jax.ShapeDtypeStruct((m, n), out_dtype), args=args)


def _call_with_exchange(body, exchange, *, name, grid, in_specs, out_specs, out_shape, args):
    if exchange is None:
        return pl.pallas_call(body, name=name, grid=grid, in_specs=in_specs, out_specs=out_specs, out_shape=out_shape,
                              compiler_params=_params(len(grid)))(*args)
    x_arrays, x_shapes, x_sems, start, wait = exchange[:5]
    n_in, n_xin, n_xout = len(args), len(x_arrays), len(x_shapes)
    aliases = {n_in + i: 1 + i for i in range(n_xin)} if len(exchange) > 5 and exchange[5] else {}

    def full_body(*refs):
        c_in, x_in = refs[:n_in], refs[n_in:n_in + n_xin]
        c_out = refs[n_in + n_xin]
        x_out = refs[n_in + n_xin + 1:n_in + n_xin + 1 + n_xout]
        sems = refs[n_in + n_xin + 1 + n_xout:]
        ids = [pl.program_id(d) for d in range(len(grid))]
        first, last = ids[0] == 0, ids[0] == grid[0] - 1
        for d in range(1, len(grid)):
            first = first & (ids[d] == 0)
            last = last & (ids[d] == grid[d] - 1)

        @pl.when(first)
        def _():
            start(x_in, x_out, sems)

        body(*c_in, c_out)

        @pl.when(last)
        def _():
            wait(x_in, x_out, sems)

    res = pl.pallas_call(
        full_body, name=name, grid=grid, in_specs=list(in_specs) + [_ANY] * n_xin,
        out_specs=[out_specs] + [_ANY] * n_xout, out_shape=[out_shape] + list(x_shapes),
        scratch_shapes=list(x_sems), input_output_aliases=aliases, compiler_params=_params(len(grid)),
    )(*args, *x_arrays)
    return res[0], list(res[1:])


def _matmul_tn(a, b, *, name, tm=1024, tn=1024, exchange=None):
    t, m = a.shape
    _, n = b.shape
    tm = _pick(m, (tm, 1024, 512, 128))
    tn = _pick(n, (tn, 1024, 512, 128))
    tt = _pick(t, (4096, 2048, 1024, 512, 256))
    nt = t // tt

    def body(a_ref, b_ref, o_ref):
        k = pl.program_id(2)
        part = lax.dot_general(a_ref[...], b_ref[...], (((0,), (0,)), ((), ())), preferred_element_type=f32)

        @pl.when(k == 0)
        def _():
            o_ref[...] = part

        @pl.when(k > 0)
        def _():
            o_ref[...] += part

    return _call_with_exchange(
        body, exchange, name=name, grid=(m // tm, n // tn, nt),
        in_specs=[pl.BlockSpec((tt, tm), lambda i, j, k: (k, i)), pl.BlockSpec((tt, tn), lambda i, j, k: (k, j))],
        out_specs=pl.BlockSpec((tm, tn), lambda i, j, k: (i, j)),
        out_shape=jax.ShapeDtypeStruct((m, n), f32), args=[a, b])


def _rms_fwd(x, g, *, name, exchange=None):
    t = x.shape[0]
    tb = _pick(t, (256, 128))

    def body(x_ref, g_ref, h_ref):
        xv = x_ref[...]
        r = lax.rsqrt(jnp.mean(xv * xv, axis=-1, keepdims=True) + EPS)
        h_ref[...] = (xv * r * g_ref[...]).astype(bf16)

    return _call_with_exchange(
        body, exchange, name=name, grid=(t // tb,), in_specs=[_row(tb, D), _fixed((1, D))], out_specs=_row(tb, D),
        out_shape=jax.ShapeDtypeStruct((t, D), bf16), args=[x, g])


def _rms_bwd(dh, x, g, dres, *, name, more=None):
    t = x.shape[0]
    tb = _pick(t, (256, 128))

    def body(dh_ref, x_ref, g_ref, dres_ref, *rest):
        dx_ref, dxb_ref, dg_ref = rest[-3:]
        xv = x_ref[...]
        r = lax.rsqrt(jnp.mean(xv * xv, axis=-1, keepdims=True) + EPS)
        xh = xv * r
        dy = dh_ref[...]
        if more is not None:
            dy = dy + lax.dot_general(rest[0][...], rest[1][...], (((1,), (1,)), ((), ())), preferred_element_type=f32)
        dyg = dy * g_ref[...]
        dx = dres_ref[...] + r * (dyg - xh * jnp.mean(dyg * xh, axis=-1, keepdims=True))
        dx_ref[...] = dx
        dxb_ref[...] = dx.astype(bf16)

        @pl.when(pl.program_id(0) == 0)
        def _():
            dg_ref[...] = jnp.zeros_like(dg_ref)

        dg_ref[...] += jnp.sum((dy * xh).reshape(tb // 8, 8, D), axis=0)

    in_specs, args = [_row(tb, D), _row(tb, D), _fixed((1, D)), _row(tb, D)], [dh, x, g, dres]
    if more is not None:
        in_specs += [_row(tb, 128), _fixed(more[1].shape)]
        args += list(more)
    return pl.pallas_call(
        body, name=name, grid=(t // tb,), in_specs=in_specs,
        out_specs=[_row(tb, D), _row(tb, D), _fixed((8, D))],
        out_shape=[jax.ShapeDtypeStruct((t, D), f32), jax.ShapeDtypeStruct((t, D), bf16),
                   jax.ShapeDtypeStruct((8, D), f32)],
        compiler_params=_params(1),
    )(*args)


def _gdn_gates(ab, alog, dtb):
    lane = lax.broadcasted_iota(jnp.int32, ab.shape, 1)
    g = -jnp.exp(alog) * _softplus(ab + dtb)
    beta = _sigmoid(ab)
    return jnp.where(lane < H, g, jnp.where(lane < 2 * H, beta, 0.0))


def _pre_fwd(pg, pq, h1, w2, wa, wg, alog, dtb):
    t = pg.shape[0]
    tb = 128

    def body(p0_ref, p0h_ref, pq_ref, pqh_ref, h1_ref, w2_ref, wa_ref, wg_ref, alog_ref, dtb_ref,
             ya_ref, qn_ref, kn_ref, vc_ref, gb_ref, p2_ref):
        first = pl.program_id(0) == 0
        p2_ref[...] = jnp.dot(h1_ref[...], w2_ref[...], preferred_element_type=f32)
        for i in range(D // 128):
            sl, cg, xv = _strip(i), _strip(i, D), _strip(i, 2 * D)
            taps = _strip_taps(_f32(p0_ref, cg) * _f32(p0_ref, xv), _halo_before(p0h_ref, cg) * _halo_before(p0h_ref, xv),
                               first, 3)
            ya_ref[:, sl] = (_f32(p0_ref, sl) * _strip_conv(wa_ref, sl, taps)).astype(bf16)
        for part, out_ref, scale in ((0, qn_ref, DH ** -0.5), (1, kn_ref, 1.0), (2, vc_ref, None)):
            for h in range(H):
                sl = _strip(h, part * D)
                s = _strip_conv(wg_ref, sl, _strip_taps(pq_ref[:, sl], pqh_ref[:, sl], first, 4))
                s = s * _sigmoid(s)
                if scale is not None:
                    s = s * (lax.rsqrt(jnp.sum(s * s, axis=-1, keepdims=True) + EPS) * scale)
                out_ref[:, _strip(h)] = s
        gb_ref[...] = _gdn_gates(p2_ref[...], alog_ref[...], dtb_ref[...])

    return pl.pallas_call(
        body, name="pre_fwd", grid=(t // tb,),
        in_specs=[_row(tb, 3 * D, 0), _prev(tb, 3 * D, 0, rows=16), _row(tb, 3 * D), _prev(tb, 3 * D), _row(tb, D),
                  _fixed((D, 128)), _fixed((8, D)), _fixed((8, 3 * D)), _fixed((1, 128)), _fixed((1, 128))],
        out_specs=[_row(tb, D), _row(tb, D), _row(tb, D), _row(tb, D), _row(tb, 128), _row(tb, 128)],
        out_shape=[jax.ShapeDtypeStruct((t, D), bf16), jax.ShapeDtypeStruct((t, D), f32),
                   jax.ShapeDtypeStruct((t, D), f32), jax.ShapeDtypeStruct((t, D), f32),
                   jax.ShapeDtypeStruct((t, 128), f32), jax.ShapeDtypeStruct((t, 128), f32)],
        compiler_params=_params(1),
    )(pg, pg, pq, pq, h1, w2, wa, wg, alog, dtb)


_Z_COL, _GA_COL, _GB_COL = 3, 4, 5


def _post_fwd(o, pg, gn):
    t = o.shape[0]
    tb = _pick(t, (256, 128))

    def body(o_ref, z_ref, gn_ref, yb_ref):
        for h in range(H):
            sl = slice(h * DH, (h + 1) * DH)
            oh = o_ref[:, sl]
            z = _f32(z_ref, sl)
            r = lax.rsqrt(jnp.mean(oh * oh, axis=-1, keepdims=True) + EPS)
            yb_ref[:, sl] = (oh * r * gn_ref[...] * (z * _sigmoid(z))).astype(bf16)

    return pl.pallas_call(
        body, name="post_fwd", grid=(t // tb,), in_specs=[_row(tb, D), _row(tb, D, _Z_COL), _fixed((1, DH))],
        out_specs=_row(tb, D), out_shape=jax.ShapeDtypeStruct((t, D), bf16), compiler_params=_params(1),
    )(o, pg, gn)


def _post_bwd(dyb, o, pg, gn):
    t = o.shape[0]
    tb = _pick(t, (256, 128))

    def body(dyb_ref, o_ref, z_ref, gn_ref, do_ref, dz_ref, dgn_ref):
        @pl.when(pl.program_id(0) == 0)
        def _():
            dgn_ref[...] = jnp.zeros_like(dgn_ref)

        gn_v = gn_ref[...]
        acc = jnp.zeros((8, DH), f32)
        for h in range(H):
            sl = slice(h * DH, (h + 1) * DH)
            oh = o_ref[:, sl]
            z = _f32(z_ref, sl)
            dy = dyb_ref[:, sl]
            r = lax.rsqrt(jnp.mean(oh * oh, axis=-1, keepdims=True) + EPS)
            on = oh * r
            sg = _sigmoid(z)
            sz = z * sg
            don = dy * sz
            dz_ref[:, sl] = (dy * on * gn_v * (sg * (1.0 + z * (1.0 - sg)))).astype(bf16)
            acc = acc + jnp.sum((don * on).reshape(tb // 8, 8, DH), axis=0)
            doh = don * gn_v
            do_ref[:, sl] = r * (doh - on * jnp.mean(doh * on, axis=-1, keepdims=True))
        dgn_ref[...] += acc

    return pl.pallas_call(
        body, name="post_bwd", grid=(t // tb,),
        in_specs=[_row(tb, D), _row(tb, D), _row(tb, D, _Z_COL), _fixed((1, DH))],
        out_specs=[_row(tb, D), _row(tb, D), _fixed((8, DH))],
        out_shape=[jax.ShapeDtypeStruct((t, D), f32), jax.ShapeDtypeStruct((t, D), bf16),
                   jax.ShapeDtypeStruct((8, DH), f32)],
        compiler_params=_params(1),
    )(dyb, o, pg, gn)


def _mix_fwd(ya, yb, pg):
    t = ya.shape[0]
    tb = _pick(t, (256, 128))

    def body(ya_ref, yb_ref, ga_ref, gb_ref, mix_ref):
        ya_v, yb_v = ya_ref[...].astype(f32), yb_ref[...].astype(f32)
        mix = _sigmoid(ga_ref[...].astype(f32)) * ya_v + _sigmoid(gb_ref[...].astype(f32)) * yb_v
        mix_ref[...] = mix.astype(bf16)

    return pl.pallas_call(
        body, name="mix_fwd", grid=(t // tb,),
        in_specs=[_row(tb, D), _row(tb, D), _row(tb, D, _GA_COL), _row(tb, D, _GB_COL)],
        out_specs=_row(tb, D), out_shape=jax.ShapeDtypeStruct((t, D), bf16), compiler_params=_params(1),
    )(ya, yb, pg, pg)


def _mix_bwd(dmix, ya, yb, pg):
    t = ya.shape[0]
    tb = _pick(t, (256, 128))

    def body(dm_ref, ya_ref, yb_ref, ga_ref, gb_ref, dya_ref, dyb_ref, dg_ref):
        dm = dm_ref[...].astype(f32)
        sa = _sigmoid(ga_ref[...].astype(f32))
        sb = _sigmoid(gb_ref[...].astype(f32))
        dya_ref[...] = (dm * sa).astype(bf16)
        dyb_ref[...] = (dm * sb).astype(bf16)
        dg_ref[:, :D] = (dm * ya_ref[...].astype(f32) * sa * (1.0 - sa)).astype(bf16)
        dg_ref[:, D:] = (dm * yb_ref[...].astype(f32) * sb * (1.0 - sb)).astype(bf16)

    return pl.pallas_call(
        body, name="mix_bwd", grid=(t // tb,),
        in_specs=[_row(tb, D), _row(tb, D), _row(tb, D), _row(tb, D, _GA_COL), _row(tb, D, _GB_COL)],
        out_specs=[_row(tb, D), _row(tb, D), _row(tb, 2 * D)],
        out_shape=[jax.ShapeDtypeStruct((t, D), bf16), jax.ShapeDtypeStruct((t, D), bf16),
                   jax.ShapeDtypeStruct((t, 2 * D), bf16)],
        compiler_params=_params(1),
    )(dmix, ya, yb, pg, pg)


def _ffn_fwd(up, wf):
    t = up.shape[0]
    tb = 128

    def body(up_ref, uph_ref, wf_ref, act_ref):
        first = pl.program_id(0) == 0
        for i in range(DFF // 128):
            g, v = _strip(i), _strip(i, DFF)
            gate = _strip_conv(wf_ref, g, _strip_taps(_f32(up_ref, g), _halo_before(uph_ref, g), first, 3))
            val = _strip_conv(wf_ref, v, _strip_taps(_f32(up_ref, v), _halo_before(uph_ref, v), first, 3))
            act_ref[:, g] = (gate * _sigmoid(gate) * val).astype(bf16)

    return pl.pallas_call(
        body, name="ffn_fwd", grid=(t // tb,),
        in_specs=[_row(tb, 2 * DFF), _prev(tb, 2 * DFF, rows=16), _fixed((8, 2 * DFF))],
        out_specs=_row(tb, DFF), out_shape=jax.ShapeDtypeStruct((t, DFF), bf16), compiler_params=_params(1),
    )(up, up, wf)


def _ffn_bwd1(dact, up, wf):
    t = up.shape[0]
    tb = 128

    def body(da_ref, up_ref, uph_ref, wf_ref, dc_ref, dw_ref):
        @pl.when(pl.program_id(0) == 0)
        def _():
            dw_ref[...] = jnp.zeros_like(dw_ref)

        first = pl.program_id(0) == 0
        for i in range(DFF // 128):
            g, v = _strip(i), _strip(i, DFF)
            g_taps = _strip_taps(_f32(up_ref, g), _halo_before(uph_ref, g), first, 3)
            v_taps = _strip_taps(_f32(up_ref, v), _halo_before(uph_ref, v), first, 3)
            gate = _strip_conv(wf_ref, g, g_taps)
            val = _strip_conv(wf_ref, v, v_taps)
            sg = _sigmoid(gate)
            da = _f32(da_ref, g)
            dgate = da * val * (sg * (1.0 + gate * (1.0 - sg)))
            dval = da * (gate * sg)
            dc_ref[:, g] = dgate.astype(bf16)
            dc_ref[:, v] = dval.astype(bf16)
            _strip_weight_grad(dw_ref, g, dgate, g_taps)
            _strip_weight_grad(dw_ref, v, dval, v_taps)

    return pl.pallas_call(
        body, name="ffn_bwd1", grid=(t // tb,),
        in_specs=[_row(tb, DFF), _row(tb, 2 * DFF), _prev(tb, 2 * DFF, rows=16), _fixed((8, 2 * DFF))],
        out_specs=[_row(tb, 2 * DFF), _fixed((8, 2 * DFF))],
        out_shape=[jax.ShapeDtypeStruct((t, 2 * DFF), bf16), jax.ShapeDtypeStruct((8, 2 * DFF), f32)],
        compiler_params=_params(1),
    )(dact, up, up, wf)


def _ffn_bwd2(dc, wf):
    t = dc.shape[0]
    tb = 128
    nb = t // tb

    def body(dc_ref, dch_ref, wf_ref, dup_ref):
        last = pl.program_id(0) == nb - 1
        for i in range(2 * DFF // 128):
            sl = _strip(i)
            dup_ref[:, sl] = _strip_conv_up(_f32(dc_ref, sl), _halo_after(dch_ref, sl), last, wf_ref, sl, 3).astype(bf16)

    return pl.pallas_call(
        body, name="ffn_bwd2", grid=(nb,),
        in_specs=[_row(tb, 2 * DFF), _next(tb, 2 * DFF, t, rows=16), _fixed((8, 2 * DFF))],
        out_specs=_row(tb, 2 * DFF), out_shape=jax.ShapeDtypeStruct((t, 2 * DFF), bf16), compiler_params=_params(1),
    )(dc, dc, wf)


def _final(x3, tgt, g):
    t = x3.shape[0]
    tb = _pick(t, (256, 128))

    def body(x_ref, t_ref, g_ref, loss_ref, dx_ref, dxb_ref, dg_ref):
        @pl.when(pl.program_id(0) == 0)
        def _():
            loss_ref[...] = jnp.zeros_like(loss_ref)
            dg_ref[...] = jnp.zeros_like(dg_ref)

        xv = x_ref[...]
        r = lax.rsqrt(jnp.mean(xv * xv, axis=-1, keepdims=True) + EPS)
        xh = xv * r
        gv = g_ref[...]
        e = xh * gv - t_ref[...]
        lrow = 0.5 * jnp.mean(e * e, axis=-1, keepdims=True)
        loss_ref[...] += jnp.sum(jnp.broadcast_to(lrow, (tb, 128)).reshape(tb // 8, 8, 128), axis=0)
        dy = e * (1.0 / D)
        dyg = dy * gv
        dx = r * (dyg - xh * jnp.mean(dyg * xh, axis=-1, keepdims=True))
        dx_ref[...] = dx
        dxb_ref[...] = dx.astype(bf16)
        dg_ref[...] += jnp.sum((dy * xh).reshape(tb // 8, 8, D), axis=0)

    return pl.pallas_call(
        body, name="final", grid=(t // tb,), in_specs=[_row(tb, D), _row(tb, D), _fixed((1, D))],
        out_specs=[_fixed((8, 128)), _row(tb, D), _row(tb, D), _fixed((8, D))],
        out_shape=[jax.ShapeDtypeStruct((8, 128), f32), jax.ShapeDtypeStruct((t, D), f32),
                   jax.ShapeDtypeStruct((t, D), bf16), jax.ShapeDtypeStruct((8, D), f32)],
        compiler_params=_params(1),
    )(x3, tgt, g)


def _pre_bwd1(pg, pq, p2, dya_in, dqn, dkn, dvc, dgb, gbeta, h1, wa, wg, alog, dtb):
    t = pg.shape[0]
    tb = 128

    def body(p0_ref, p0h_ref, pq_ref, pqh_ref, p2_ref, dya_ref, dqn_ref, dkn_ref, dvc_ref, dgb_ref, gb_ref, h1_ref,
             wa_ref, wg_ref, alog_ref, dtb_ref,
             dbg_ref, dca_ref, dc4_ref, dp2_ref, dwa_ref, dwg_ref, dal_ref, ddt_ref, dw2_ref):
        @pl.when(pl.program_id(0) == 0)
        def _():
            dwa_ref[...] = jnp.zeros_like(dwa_ref)
            dwg_ref[...] = jnp.zeros_like(dwg_ref)
            dal_ref[...] = jnp.zeros_like(dal_ref)
            ddt_ref[...] = jnp.zeros_like(ddt_ref)
            dw2_ref[...] = jnp.zeros_like(dw2_ref)

        first = pl.program_id(0) == 0

        for i in range(D // 128):
            sl, cg, xv = _strip(i), _strip(i, D), _strip(i, 2 * D)
            taps = _strip_taps(_f32(p0_ref, cg) * _f32(p0_ref, xv), _halo_before(p0h_ref, cg) * _halo_before(p0h_ref, xv),
                               first, 3)
            dya = _f32(dya_ref, sl)
            dbg_ref[:, sl] = (dya * _strip_conv(wa_ref, sl, taps)).astype(bf16)
            dca = dya * _f32(p0_ref, sl)
            dca_ref[:, sl] = dca.astype(bf16)
            _strip_weight_grad(dwa_ref, sl, dca, taps)

        for part, d_ref, scale in ((0, dqn_ref, DH ** -0.5), (1, dkn_ref, 1.0), (2, dvc_ref, None)):
            for h in range(H):
                sl = _strip(h, part * D)
                taps = _strip_taps(pq_ref[:, sl], pqh_ref[:, sl], first, 4)
                c4 = _strip_conv(wg_ref, sl, taps)
                sg = _sigmoid(c4)
                dn = d_ref[:, _strip(h)]
                if scale is not None:
                    a = c4 * sg
                    r = lax.rsqrt(jnp.sum(a * a, axis=-1, keepdims=True) + EPS)
                    an = a * r
                    dn = dn * scale
                    dn = r * (dn - an * jnp.sum(dn * an, axis=-1, keepdims=True))
                dc4 = dn * (sg * (1.0 + c4 * (1.0 - sg)))
                dc4_ref[:, sl] = dc4.astype(bf16)
                _strip_weight_grad(dwg_ref, sl, dc4, taps)

        ab = p2_ref[...]
        lane = lax.broadcasted_iota(jnp.int32, ab.shape, 1)
        dgbv = dgb_ref[...]
        gbv = gb_ref[...]
        da = dgbv * (-jnp.exp(alog_ref[...])) * _sigmoid(ab + dtb_ref[...])
        db = dgbv * gbv * (1.0 - gbv)
        dp2 = jnp.where(lane < H, da, jnp.where(lane < 2 * H, db, 0.0)).astype(bf16)
        dp2_ref[...] = dp2
        dw2_ref[...] += lax.dot_general(dp2, h1_ref[...], (((0,), (0,)), ((), ())), preferred_element_type=f32)
        dal = jnp.where(lane < H, dgbv * gbv, 0.0)
        ddt = jnp.where(lane < H, da, 0.0)
        dal_ref[...] += jnp.sum(dal.reshape(tb // 8, 8, 128), axis=0)
        ddt_ref[...] += jnp.sum(ddt.reshape(tb // 8, 8, 128), axis=0)

    return pl.pallas_call(
        body, name="pre_bwd1", grid=(t // tb,),
        in_specs=[_row(tb, 3 * D, 0), _prev(tb, 3 * D, 0, rows=16), _row(tb, 3 * D), _prev(tb, 3 * D), _row(tb, 128),
                  _row(tb, D), _row(tb, D), _row(tb, D), _row(tb, D), _row(tb, 128), _row(tb, 128), _row(tb, D),
                  _fixed((8, D)), _fixed((8, 3 * D)), _fixed((1, 128)), _fixed((1, 128))],
        out_specs=[_row(tb, D), _row(tb, D), _row(tb, 3 * D), _row(tb, 128),
                   _fixed((8, D)), _fixed((8, 3 * D)), _fixed((8, 128)), _fixed((8, 128)), _fixed((128, D))],
        out_shape=[jax.ShapeDtypeStruct((t, D), bf16), jax.ShapeDtypeStruct((t, D), bf16),
                   jax.ShapeDtypeStruct((t, 3 * D), bf16), jax.ShapeDtypeStruct((t, 128), bf16),
                   jax.ShapeDtypeStruct((8, D), f32), jax.ShapeDtypeStruct((8, 3 * D), f32),
                   jax.ShapeDtypeStruct((8, 128), f32), jax.ShapeDtypeStruct((8, 128), f32),
                   jax.ShapeDtypeStruct((128, D), f32)],
        compiler_params=_params(1),
    )(pg, pg, pq, pq, p2, dya_in, dqn, dkn, dvc, dgb, gbeta, h1, wa, wg, alog, dtb)


def _pre_bwd2(dca, dc4, pg, dbg, dz, dgates, wa, wg, exchange=None):
    t = pg.shape[0]
    tb = 128
    nb = t // tb

    def body(dca_ref, dcah_ref, dc4_ref, dc4h_ref, p0_ref, dbg_ref, dz_ref, dgt_ref, wa_ref, wg_ref, dp_ref):
        last = pl.program_id(0) == nb - 1
        dp_ref[:, :D] = dbg_ref[...]
        for i in range(D // 128):
            sl, cg, xv = _strip(i), _strip(i, D), _strip(i, 2 * D)
            du = _strip_conv_up(_f32(dca_ref, sl), _halo_after(dcah_ref, sl), last, wa_ref, sl, 3)
            dp_ref[:, cg] = (du * _f32(p0_ref, xv)).astype(bf16)
            dp_ref[:, xv] = (du * _f32(p0_ref, cg)).astype(bf16)
        dp_ref[:, 3 * D:4 * D] = dz_ref[...]
        dp_ref[:, 4 * D:6 * D] = dgt_ref[...]
        for i in range(3 * D // 128):
            sl = _strip(i)
            dq = _strip_conv_up(_f32(dc4_ref, sl), _halo_after(dc4h_ref, sl), last, wg_ref, sl, 4)
            dp_ref[:, _strip(i, 6 * D)] = dq.astype(bf16)

    return _call_with_exchange(
        body, exchange, name="pre_bwd2", grid=(nb,),
        in_specs=[_row(tb, D), _next(tb, D, t, rows=16), _row(tb, 3 * D), _next(tb, 3 * D, t, rows=16), _row(tb, 3 * D, 0),
                  _row(tb, D), _row(tb, D), _row(tb, 2 * D), _fixed((8, D)), _fixed((8, 3 * D))],
        out_specs=_row(tb, NW1), out_shape=jax.ShapeDtypeStruct((t, NW1), bf16),
        args=[dca, dca, dc4, dc4, pg, dbg, dz, dgates, wa, wg])


def _chunk_consts():
    r = lax.broadcasted_iota(jnp.int32, (CH, CH), 0)
    c = lax.broadcasted_iota(jnp.int32, (CH, CH), 1)
    return r, c, (r == c).astype(f32)


def _tri_inverse(lows, eye, r, c):
    def same_block(b):
        return jnp.bitwise_xor(r, c) < b

    xs = [jnp.where(same_block(8), -low, 0.0) for low in lows]
    ts = [eye + x for x in xs]
    for _ in range(2):
        xs = [_idot(x, x) for x in xs]
        ts = [t + _idot(t, x) for t, x in zip(ts, xs)]
    for b in (8, 16, 32):
        below = same_block(2 * b) & jnp.logical_not(same_block(b))
        ts = [t - _idot(_idot(t, jnp.where(below, low, 0.0)), t) for t, low in zip(ts, lows)]
    return ts


def _chunk_common(q, k, v, gcol, bcol, r, c, eye):
    grow = jnp.sum(eye * gcol, axis=0, keepdims=True)
    dec = jnp.exp(jnp.where(r >= c, gcol - grow, -jnp.inf))
    rcol = lax.broadcasted_iota(jnp.int32, (CH, 1), 0)
    glast = jnp.sum(jnp.where(rcol == CH - 1, gcol, 0.0), axis=0, keepdims=True)
    eg = jnp.exp(gcol)
    el = jnp.exp(glast - gcol)
    kb = k * bcol
    vb = v * bcol
    kk = _bdot_nt(kb, k)
    low = jnp.where(r > c, kk * dec, 0.0)
    qk = _bdot_nt(q, k)
    att = qk * dec
    return grow, dec, glast, eg, el, kb, vb, kk, low, qk, att, rcol


def _gdn_fwd(qn, kn, vc, gbeta):
    t = qn.shape[0]
    n_chunks = t // CH

    def body(q_ref, k_ref, v_ref, gb_ref, o_ref, s_ref, t_ref, state):
        @pl.when(pl.program_id(0) == 0)
        def _():
            state[...] = jnp.zeros_like(state)

        r, c, eye = _chunk_consts()
        tri = (r >= c).astype(f32)
        heads = range(H)
        keys = [(s, h) for s in range(GDN_STEP) for h in heads]
        rows = [slice(s * CH, (s + 1) * CH) for s in range(GDN_STEP)]
        gbs = [gb_ref[rows[s], :] for s in range(GDN_STEP)]
        galls = [_hdot(tri, gb) for gb in gbs]
        qs = {(s, h): q_ref[rows[s], h * DH:(h + 1) * DH] for s, h in keys}
        ks = {(s, h): k_ref[rows[s], h * DH:(h + 1) * DH] for s, h in keys}
        cm = {(s, h): _chunk_common(qs[s, h], ks[s, h], v_ref[rows[s], h * DH:(h + 1) * DH], galls[s][:, h:h + 1],
                                    gbs[s][:, H + h:H + h + 1], r, c, eye) for s, h in keys}
        invs = dict(zip(keys, _tri_inverse([cm[key][8] for key in keys], eye, r, c)))
        uws = {key: _bdot(invs[key], jnp.concatenate([cm[key][6], cm[key][5] * cm[key][3]], axis=1)) for key in keys}
        sts = [state[h] for h in heads]
        for s in range(GDN_STEP):
            vns = [uws[s, h][:, :DH] - _bdot(uws[s, h][:, DH:], sts[h]) for h in heads]
            outs = [_bdot(qs[s, h] * cm[s, h][3], sts[h]) + _bdot(cm[s, h][10], vns[h]) for h in heads]
            news = [sts[h] * jnp.exp(cm[s, h][2]) + _bdot_tn(ks[s, h] * cm[s, h][4], vns[h]) for h in heads]
            for h in heads:
                s_ref[s, h] = sts[h].astype(bf16)
                t_ref[s, h] = invs[s, h]
                o_ref[rows[s], h * DH:(h + 1) * DH] = outs[h]
            sts = news
        for h in heads:
            state[h] = sts[h]

    tb = GDN_STEP * CH
    return pl.pallas_call(
        body, name="gdn_fwd", grid=(t // tb,),
        in_specs=[_row(tb, D), _row(tb, D), _row(tb, D), _row(tb, 128)],
        out_specs=[_row(tb, D), pl.BlockSpec((GDN_STEP, H, DH, DH), lambda i: (i, 0, 0, 0)),
                   pl.BlockSpec((GDN_STEP, H, CH, CH), lambda i: (i, 0, 0, 0))],
        out_shape=[jax.ShapeDtypeStruct((t, D), f32), jax.ShapeDtypeStruct((n_chunks, H, DH, DH), bf16),
                   jax.ShapeDtypeStruct((n_chunks, H, CH, CH), f32)],
        scratch_shapes=[pltpu.VMEM((H, DH, DH), f32)],
        compiler_params=_params(1),
    )(qn, kn, vc, gbeta)


def _gdn_bwd(qn, kn, vc, gbeta, do, s_all, t_all):
    t = qn.shape[0]

    def body(q_ref, k_ref, v_ref, gb_ref, do_ref, s_ref, t_ref, dq_ref, dk_ref, dv_ref, dgb_ref, dstate):
        @pl.when(pl.program_id(0) == 0)
        def _():
            dstate[...] = jnp.zeros_like(dstate)

        r, c, eye = _chunk_consts()
        tril = r >= c
        lane = lax.broadcasted_iota(jnp.int32, (1, 128), 1)
        hs = range(H)

        def each(fn, *lists):
            return [fn(*args) for args in zip(*lists)]

        def rsum(a):
            return jnp.sum(a, axis=1, keepdims=True)

        def before_state(s):
            rows = slice(s * CH, (s + 1) * CH)
            gb = gb_ref[rows, :]
            gall = _hdot(tril.astype(f32), gb)
            p = {"rows": rows}
            p["q"] = q = [q_ref[rows, h * DH:(h + 1) * DH] for h in hs]
            p["k"] = k = [k_ref[rows, h * DH:(h + 1) * DH] for h in hs]
            p["v"] = v = [v_ref[rows, h * DH:(h + 1) * DH] for h in hs]
            p["dout"] = dout = [do_ref[rows, h * DH:(h + 1) * DH] for h in hs]
            p["inv"] = inv = [t_ref[s, h] for h in hs]
            p["st"] = st = [s_ref[s, h] for h in hs]
            p["bcol"] = bcol = [gb[:, H + h:H + h + 1] for h in hs]
            cm = [_chunk_common(q[h], k[h], v[h], gall[:, h:h + 1], bcol[h], r, c, eye) for h in hs]
            for name, i in (("dec", 1), ("glast", 2), ("eg", 3), ("el", 4), ("kb", 5), ("vb", 6), ("low", 8), ("att", 10)):
                p[name] = [m[i] for m in cm]
            p["rcol"] = cm[0][11]
            p["elast"] = each(jnp.exp, p["glast"])
            p["kbg"] = each(jnp.multiply, p["kb"], p["eg"])
            uw = each(lambda i, a, b: _bdot(i, jnp.concatenate([a, b], axis=1)), inv, p["vb"], p["kbg"])
            p["u"] = [a[:, :DH] for a in uw]
            p["w"] = [a[:, DH:] for a in uw]
            p["vn"] = each(lambda a, b, x: a - _bdot(b, x), p["u"], p["w"], st)
            p["qd"] = each(jnp.multiply, q, p["eg"])
            p["kd"] = each(jnp.multiply, k, p["el"])
            p["dqd"] = each(_bdot_nt, dout, st)
            p["datt"] = each(lambda d, x: jnp.where(tril, _bdot_nt(d, x), 0.0), dout, p["vn"])
            p["dqk"] = each(jnp.multiply, p["datt"], p["dec"])
            p["qd_do"] = each(_bdot_tn, p["qd"], dout)
            p["att_do"] = each(_bdot_tn, p["att"], dout)
            return p

        def after_state(p, ds):
            q, k, v, st, inv, bcol = p["q"], p["k"], p["v"], p["st"], p["inv"], p["bcol"]
            eg, el, kb, u, w = p["eg"], p["el"], p["kb"], p["u"], p["w"]
            dvn = each(lambda a, kk, x: a + _bdot(kk, x), p["att_do"], p["kd"], ds)
            dkd = each(_bdot_nt, p["vn"], ds)
            dw = each(lambda a, x: -_bdot_nt(a, x), dvn, st)
            new_ds = each(lambda x, e, a, ww, dv_: x * e + a - _bdot_tn(ww, dv_), ds, p["elast"], p["qd_do"], w, dvn)
            dglast = each(lambda e, x, d: e * jnp.sum(rsum(x.astype(f32) * d), axis=0, keepdims=True), p["elast"], st, ds)
            dr = each(lambda i, a, b: _bdot_tn(i, jnp.concatenate([a, b], axis=1)), inv, dvn, dw)
            dvb = [a[:, :DH] for a in dr]
            dkbg = [a[:, DH:] for a in dr]
            dlow = each(lambda a, b, x, y: -jnp.where(r > c, _bdot_nt(a, b) + _bdot_nt(x, y), 0.0), dvb, u, dkbg, w)
            dkk = each(jnp.multiply, dlow, p["dec"])
            mm = each(lambda a, b, x, y: a * b + x * y, dlow, p["low"], p["datt"], p["att"])
            dkb = each(lambda a, kk, b, e: _bdot(a, kk) + b * e, dkk, k, dkbg, eg)
            dk = each(lambda a, b, x, y, d, e, f, g: _bdot_tn(a, b) + _bdot_tn(x, y) + d * e + f * g,
                      dkk, kb, p["dqk"], q, dkd, el, dkb, bcol)
            dq = each(lambda a, kk, d, e: _bdot(a, kk) + d * e, p["dqk"], k, p["dqd"], eg)
            dv = each(jnp.multiply, dvb, bcol)
            dbeta = each(lambda a, b, x, y: rsum(a * b) + rsum(x * y), dkb, k, dvb, v)
            deg = each(lambda a, b, x, y: rsum(a * b) + rsum(x * y), dkbg, kb, p["dqd"], q)
            delc = each(lambda a, b, e: rsum(a * b) * e, dkd, k, el)
            dgc = each(lambda m, a, e, d: rsum(m) - rsum(eye * jnp.sum(m, axis=0, keepdims=True)) + a * e - d,
                       mm, deg, eg, delc)
            dgc = each(lambda g, d, l: g + jnp.where(p["rcol"] == CH - 1, jnp.sum(d, axis=0, keepdims=True) + l, 0.0),
                       dgc, delc, dglast)
            dg_acc = jnp.zeros((CH, 128), f32)
            db_acc = jnp.zeros((CH, 128), f32)
            rows = p["rows"]
            for h in hs:
                dq_ref[rows, h * DH:(h + 1) * DH] = dq[h]
                dk_ref[rows, h * DH:(h + 1) * DH] = dk[h]
                dv_ref[rows, h * DH:(h + 1) * DH] = dv[h]
                dg_acc = dg_acc + dgc[h] * (lane == h).astype(f32)
                db_acc = db_acc + dbeta[h] * (lane == H + h).astype(f32)
            dgb_ref[rows, :] = _hdot((r <= c).astype(f32), dg_acc) + db_acc
            return new_ds

        order = list(reversed(range(GDN_STEP)))
        pre = [before_state(s) for s in order]
        ds = [dstate[h] for h in hs]
        for p in pre:
            ds = after_state(p, ds)
        for h in hs:
            dstate[h] = ds[h]

    tb = GDN_STEP * CH
    n_steps = t // tb
    rev = lambda i: (n_steps - 1 - i, 0)
    rev4 = lambda i: (n_steps - 1 - i, 0, 0, 0)
    return pl.pallas_call(
        body, name="gdn_bwd", grid=(n_steps,),
        in_specs=[pl.BlockSpec((tb, D), rev), pl.BlockSpec((tb, D), rev), pl.BlockSpec((tb, D), rev),
                  pl.BlockSpec((tb, 128), rev), pl.BlockSpec((tb, D), rev),
                  pl.BlockSpec((GDN_STEP, H, DH, DH), rev4), pl.BlockSpec((GDN_STEP, H, CH, CH), rev4)],
        out_specs=[pl.BlockSpec((tb, D), rev), pl.BlockSpec((tb, D), rev), pl.BlockSpec((tb, D), rev),
                   pl.BlockSpec((tb, 128), rev)],
        out_shape=[jax.ShapeDtypeStruct((t, D), f32)] * 3 + [jax.ShapeDtypeStruct((t, 128), f32)],
        scratch_shapes=[pltpu.VMEM((H, DH, DH), f32)],
        compiler_params=_params(1),
    )(qn, kn, vc, gbeta, do, s_all, t_all)


def _pad_rows(w, rows=8):
    return jnp.pad(w, ((0, rows - w.shape[0]), (0, 0)))


_REST = ("w_up", "w_a_out", "w_b_out", "w_o", "w_down")


def _local_step(x, tgt, w, comm=None):
    g1 = w["norm_mix_g"].reshape(1, D)
    if comm is None:
        h1 = _rms_fwd(x, g1, name="rms1_fwd")
    else:
        h1, gathered = _rms_fwd(x, g1, name="rms1_fwd", exchange=comm.gather_first())
        w = {**w, **comm.finish_first(gathered)}
    w1, w2 = w["w1"], w["w2"]
    wa = _pad_rows(w["conv_a_w"])
    wg = _pad_rows(w["gdn_conv_w"])
    wf = _pad_rows(w["ffn_conv_w"])
    alog = jnp.pad(w["gdn_A_log"].reshape(1, H), ((0, 0), (0, 128 - H)))
    dtb = jnp.pad(w["gdn_dt_bias"].reshape(1, H), ((0, 0), (0, 128 - H)))
    g2 = w["norm_ffn_g"].reshape(1, D)
    g3 = w["norm_final_g"].reshape(1, D)
    gn = w["gdn_norm_g"].reshape(1, DH)

    if comm is None:
        pg = _matmul(h1, w1, name="mm_in", cols=(0, 6 * D), out_dtype=bf16)
        pq = _matmul(h1, w1, name="mm_in_qkv", cols=(6 * D, 3 * D))
    else:
        pg, gathered = _matmul(h1, w1, name="mm_in", cols=(0, 6 * D), out_dtype=bf16, exchange=comm.gather_rest())
        pq, gathered = _matmul(h1, w1, name="mm_in_qkv", cols=(6 * D, 3 * D), exchange=_gather_forward_exchange(gathered))
        w = {**w, **comm.finish_gather(gathered)}
    ya_in, qn, kn, vc, gbeta, p2 = _pre_fwd(pg, pq, h1, w2, wa, wg, alog, dtb)
    o, s_all, t_all = _gdn_fwd(qn, kn, vc, gbeta)
    yb_in = _post_fwd(o, pg, gn)
    ya = _matmul(ya_in, w["w_a_out"], name="mm_a", out_dtype=bf16)
    yb = _matmul(yb_in, w["w_b_out"], name="mm_b", out_dtype=bf16)
    mix = _mix_fwd(ya, yb, pg)
    x2 = _matmul(mix, w["w_o"], name="mm_o", add=x)
    h2 = _rms_fwd(x2, g2, name="rms2_fwd")
    up = _matmul(h2, w["w_up"], nt=True, name="mm_up", tn=DFF // 2, out_dtype=bf16)
    act = _ffn_fwd(up, wf)
    x3 = _matmul(act, w["w_down"], name="mm_down", add=x2, tm=512)
    loss_p, dx3, dx3b, dg3 = _final(x3, tgt, g3)

    grads = {"norm_final_g": dg3}
    dact = _matmul(dx3b, w["w_down"], nt=True, name="mm_down_dx", tm=512, tn=DFF, out_dtype=bf16)
    grads["w_down"] = _matmul_tn(act, dx3b, name="mm_down_dw", tm=DFF // 2, tn=512)
    dc, dwf = _ffn_bwd1(dact, up, wf)
    grads["ffn_conv_w"] = dwf
    dup = _ffn_bwd2(dc, wf)
    dh2 = _matmul(dup, w["w_up"], name="mm_up_dx", tk=DFF)
    grads["w_up"] = _matmul_tn(dup, h2, name="mm_up_dw", tm=DFF // 2, tn=512)
    dx2, dx2b, dg2 = _rms_bwd(dh2, x2, g2, dx3, name="rms2_bwd")
    grads["norm_ffn_g"] = dg2
    dmix = _matmul(dx2b, w["w_o"], nt=True, name="mm_o_dx", out_dtype=bf16)
    grads["w_o"] = _matmul_tn(mix, dx2b, name="mm_o_dw")
    dya, dyb, dgates = _mix_bwd(dmix, ya, yb, pg)
    dya_in = _matmul(dya, w["w_a_out"], nt=True, name="mm_a_dx", out_dtype=bf16)
    grads["w_a_out"] = _matmul_tn(ya_in, dya, name="mm_a_dw")
    dyb_in = _matmul(dyb, w["w_b_out"], nt=True, name="mm_b_dx")
    grads["w_b_out"] = _matmul_tn(yb_in, dyb, name="mm_b_dw")
    do, dz, dgn = _post_bwd(dyb_in, o, pg, gn)
    grads["gdn_norm_g"] = dgn
    dqn, dkn, dvc, dgb = _gdn_bwd(qn, kn, vc, gbeta, do, s_all, t_all)
    dbg, dca, dc4, dp2, dwa, dwg, dal, ddt, grads["w2"] = _pre_bwd1(pg, pq, p2, dya_in, dqn, dkn, dvc, dgb, gbeta, h1,
                                                                    wa, wg, alog, dtb)
    grads["conv_a_w"] = dwa
    grads["gdn_conv_w"] = dwg
    grads["gdn_A_log"] = dal
    grads["gdn_dt_bias"] = ddt
    if comm is None:
        dp1 = _pre_bwd2(dca, dc4, pg, dbg, dz, dgates, wa, wg)
        grads["w1"] = _matmul_tn(dp1, h1, name="mm_in_dw")
        dh1 = _matmul(dp1, w1, nt=True, name="mm_in_dx", tm=512, tk=NW1 // 2)
    else:
        exchange, blocks = comm.reduce_halves(_REST, grads)
        dp1, recv = _pre_bwd2(dca, dc4, pg, dbg, dz, dgates, wa, wg, exchange=exchange)
        exchange, sums = comm.reduce_sums(_REST, blocks, recv)
        grads["w1"], recv = _matmul_tn(dp1, h1, name="mm_in_dw", exchange=exchange)
        comm.finish_reduce(_REST, sums, recv)
        exchange, blocks = comm.reduce_halves(("w_in",), grads)
        exchange, sums = comm.reduce_sums(("w_in",), blocks, _run_exchange(exchange, name="rs_sibling_w_in"))
        dh1, recv = _matmul(dp1, w1, nt=True, name="mm_in_dx", tm=512, tk=NW1 // 2, exchange=exchange)
        comm.finish_reduce(("w_in",), sums, recv)
    dx, _, dg1 = _rms_bwd(dh1, x, g1, dx2, name="rms1_bwd", more=(dp2, w2))
    grads["norm_mix_g"] = dg1
    return loss_p, dx, grads


_ANY = pl.BlockSpec(memory_space=pl.ANY)


def _remote(src, dst, send_sem, recv_sem, to):
    return pltpu.make_async_remote_copy(src_ref=src, dst_ref=dst, send_sem=send_sem, recv_sem=recv_sem,
                                        device_id=to, device_id_type=MESH)


def _run_exchange(exchange, *, name):
    arrays, shapes, sems, start, wait = exchange
    n_in, n_out = len(arrays), len(shapes)

    def body(*refs):
        start(refs[:n_in], refs[n_in:n_in + n_out], refs[n_in + n_out:])
        wait(refs[:n_in], refs[n_in:n_in + n_out], refs[n_in + n_out:])

    return pl.pallas_call(body, name=name, out_shape=list(shapes), in_specs=[_ANY] * n_in, out_specs=[_ANY] * n_out,
                          scratch_shapes=list(sems))(*arrays)


def _gather_exchange(shards):
    n = len(shards)

    def copies(x_refs, out_refs, sems):
        send_sems, recv_sems, local_sems = sems
        x, y, c = lax.axis_index("x"), lax.axis_index("y"), lax.axis_index("c")

        def flip(v, b):
            return v + b - 2 * v * b

        me, sibling = (x, y, c), (x, y, 1 - c)
        chip1, chip2, diag = (flip(x, 1 - c), flip(y, c)), (flip(x, c), flip(y, 1 - c)), (1 - x, 1 - y)

        def copy(a, k, blk, to, from_input=False):
            dst = out_refs[a].at[4 * blk[0] + 2 * blk[1] + blk[2]]
            return _remote(x_refs[a] if from_input else dst, dst, send_sems.at[a, k], recv_sems.at[a, k], to)

        mine = [pltpu.make_async_copy(x_refs[a], out_refs[a].at[4 * x + 2 * y + c], local_sems.at[a]) for a in range(n)]
        first = []
        for a in range(n):
            first += [copy(a, 0, me, sibling, from_input=True), copy(a, 1, me, (*chip1, c), from_input=True),
                      copy(a, 2, me, (*chip2, c), from_input=True)]
        return copy, mine, first, me, sibling, chip1, chip2, diag, c

    def start(x_refs, out_refs, sems):
        _, mine, first, *_ = copies(x_refs, out_refs, sems)
        for cp in mine + first:
            cp.start()

    def wait(x_refs, out_refs, sems):
        copy, mine, first, me, sibling, chip1, chip2, diag, c = copies(x_refs, out_refs, sems)
        passed = []

        def pass_on(cp):
            passed.append(cp)
            cp.start()

        for a in range(n):
            copy(a, 1, (*chip1, c), me).wait_recv()
            pass_on(copy(a, 3, (*chip1, c), (*chip2, c)))
            pass_on(copy(a, 4, (*chip1, c), sibling))
        for a in range(n):
            copy(a, 2, (*chip2, c), me).wait_recv()
            pass_on(copy(a, 5, (*chip2, c), sibling))
        for a in range(n):
            copy(a, 3, (*diag, c), me).wait_recv()
            pass_on(copy(a, 6, (*diag, c), sibling))
        for a in range(n):
            copy(a, 0, sibling, me).wait_recv()
            copy(a, 4, (*chip2, 1 - c), me).wait_recv()
            copy(a, 5, (*chip1, 1 - c), me).wait_recv()
            copy(a, 6, (*diag, 1 - c), me).wait_recv()
        for cp in first + passed:
            cp.wait_send()
        for cp in mine:
            cp.wait()

    shapes = [jax.ShapeDtypeStruct((N_DEV, *s.shape), s.dtype) for s in shards]
    sems = [pltpu.SemaphoreType.DMA((n, 7)), pltpu.SemaphoreType.DMA((n, 7)), pltpu.SemaphoreType.DMA((n,))]
    return shards, shapes, sems, start, wait


def _gather_direct_exchange(shards):
    n = len(shards)

    def copies(x_refs, out_refs, sems):
        send_sems, recv_sems, local_sems = sems
        x, y, c = lax.axis_index("x"), lax.axis_index("y"), lax.axis_index("c")
        targets = [(x, y, 1 - c), (1 - x, y, c), (x, 1 - y, c), (1 - x, 1 - y, c)]
        local, sends, recvs = [], [], []
        for a in range(n):
            mine = out_refs[a].at[4 * x + 2 * y + c]
            local.append(pltpu.make_async_copy(x_refs[a], mine, local_sems.at[a]))
            for k, to in enumerate(targets):
                theirs = out_refs[a].at[4 * to[0] + 2 * to[1] + to[2]]
                sends.append(_remote(x_refs[a], mine, send_sems.at[a, k], recv_sems.at[a, k], to))
                recvs.append(_remote(theirs, theirs, send_sems.at[a, k], recv_sems.at[a, k], to))
        return local, sends, recvs

    def start(x_refs, out_refs, sems):
        local, sends, _ = copies(x_refs, out_refs, sems)
        for cp in local + sends:
            cp.start()

    def wait(x_refs, out_refs, sems):
        local, sends, recvs = copies(x_refs, out_refs, sems)
        for cp in recvs:
            cp.wait_recv()
        for cp in sends:
            cp.wait_send()
        for cp in local:
            cp.wait()

    shapes = [jax.ShapeDtypeStruct((N_DEV, *s.shape), s.dtype) for s in shards]
    sems = [pltpu.SemaphoreType.DMA((n, 4)), pltpu.SemaphoreType.DMA((n, 4)), pltpu.SemaphoreType.DMA((n,))]
    return shards, shapes, sems, start, wait


def _gather_forward_exchange(gathered):
    n = len(gathered)

    def copies(_, out_refs, sems):
        send_sems, recv_sems = sems
        x, y, c = lax.axis_index("x"), lax.axis_index("y"), lax.axis_index("c")
        sibling = (x, y, 1 - c)
        sends, recvs = [], []
        for a in range(n):
            for j, (px, py) in enumerate([(1 - x, y), (x, 1 - y), (1 - x, 1 - y)]):
                mine = out_refs[a].at[4 * px + 2 * py + c]
                theirs = out_refs[a].at[4 * px + 2 * py + 1 - c]
                sends.append(_remote(mine, mine, send_sems.at[a, j], recv_sems.at[a, j], sibling))
                recvs.append(_remote(theirs, theirs, send_sems.at[a, j], recv_sems.at[a, j], sibling))
        return sends, recvs

    def start(in_refs, out_refs, sems):
        for cp in copies(in_refs, out_refs, sems)[0]:
            cp.start()

    def wait(in_refs, out_refs, sems):
        sends, recvs = copies(in_refs, out_refs, sems)
        for cp in recvs:
            cp.wait_recv()
        for cp in sends:
            cp.wait_send()

    shapes = [jax.ShapeDtypeStruct(g.shape, g.dtype) for g in gathered]
    sems = [pltpu.SemaphoreType.DMA((n, 3)), pltpu.SemaphoreType.DMA((n, 3))]
    return gathered, shapes, sems, start, wait, True


def _chips_exchange(hsums):
    n = len(hsums)

    def copies(h_refs, out_refs, sems):
        send_sems, recv_sems = sems
        x, y, c = lax.axis_index("x"), lax.axis_index("y"), lax.axis_index("c")
        chips = [(1 - x, y), (x, 1 - y), (1 - x, 1 - y)]
        return [_remote(h_refs[a].at[2 * px + py], out_refs[a].at[k], send_sems.at[a, k], recv_sems.at[a, k], (px, py, c))
                for a in range(n) for k, (px, py) in enumerate(chips)]

    def start(h_refs, out_refs, sems):
        for cp in copies(h_refs, out_refs, sems):
            cp.start()

    def wait(h_refs, out_refs, sems):
        for cp in copies(h_refs, out_refs, sems):
            cp.wait()

    shapes = [jax.ShapeDtypeStruct((3, *h.shape[1:]), h.dtype) for h in hsums]
    sems = [pltpu.SemaphoreType.DMA((n, 3)), pltpu.SemaphoreType.DMA((n, 3))]
    return hsums, shapes, sems, start, wait


def _sibling_exchange(halves):
    n = len(halves)

    def copies(p_refs, out_refs, sems):
        send_sems, recv_sems = sems
        x, y, c = lax.axis_index("x"), lax.axis_index("y"), lax.axis_index("c")
        return [_remote(p_refs[a], out_refs[a], send_sems.at[a], recv_sems.at[a], (x, y, 1 - c)) for a in range(n)]

    def start(p_refs, out_refs, sems):
        for cp in copies(p_refs, out_refs, sems):
            cp.start()

    def wait(p_refs, out_refs, sems):
        for cp in copies(p_refs, out_refs, sems):
            cp.wait()

    shapes = [jax.ShapeDtypeStruct(h.shape, h.dtype) for h in halves]
    return halves, shapes, [pltpu.SemaphoreType.DMA((n,)), pltpu.SemaphoreType.DMA((n,))], start, wait


_IN_RANGES = ((0, 3 * D, 0, 0), (3 * D, 6 * D, 0, 6 * D), (6 * D, 7 * D, 0, 3 * D), (7 * D, 7 * D + 16, 1, 0),
              (7 * D + 16, 9 * D + 16, 0, 4 * D))


def _col_pieces(width, ranges):
    pieces = []
    for d in range(N_DEV):
        lo, hi = d * width, (d + 1) * width
        for glo, ghi, mat, mlo in ranges:
            a, b = max(lo, glo), min(hi, ghi)
            if a < b:
                pieces.append((d, a - lo, b - lo, mat, mlo + a - glo))
    return pieces


def _cols_to_matrices(g, ranges, out_widths, *, name):
    _, rows, width = g.shape
    tb = 128
    pieces = _col_pieces(width, ranges)
    covered = [sum(p[2] - p[1] for p in pieces if p[3] == m) for m in range(len(out_widths))]

    def body(g_ref, *o_refs):
        for m, o_ref in enumerate(o_refs):
            if covered[m] < out_widths[m]:
                o_ref[...] = jnp.zeros_like(o_ref)
        for d, b0, b1, m, m0 in pieces:
            o_refs[m][:, m0:m0 + b1 - b0] = g_ref[d, :, b0:b1]

    return pl.pallas_call(
        body, name=name, grid=(rows // tb,), in_specs=[pl.BlockSpec((N_DEV, tb, width), lambda i: (0, i, 0))],
        out_specs=[pl.BlockSpec((tb, wo), lambda i: (i, 0)) for wo in out_widths],
        out_shape=[jax.ShapeDtypeStruct((rows, wo), g.dtype) for wo in out_widths], compiler_params=_params(1),
    )(g)


def _transposed_matrices_to_blocks(mats, ranges, width, *, name):
    rows = mats[0].shape[1]
    pieces = _col_pieces(width, ranges)

    def body(*refs):
        m_refs, g_ref = refs[:-1], refs[-1]
        for d, b0, b1, m, m0 in pieces:
            g_ref[d, b0:b1, :] = m_refs[m][m0:m0 + b1 - b0, :]

    return pl.pallas_call(
        body, name=name, grid=(rows // 128,),
        in_specs=[pl.BlockSpec((mt.shape[0], 128), lambda i: (0, i)) for mt in mats],
        out_specs=pl.BlockSpec((N_DEV, width, 128), lambda i: (0, 0, i)),
        out_shape=jax.ShapeDtypeStruct((N_DEV, width, rows), mats[0].dtype), compiler_params=_params(1),
    )(*mats)


def _row_block(rows):
    return 128 if rows % 128 == 0 else rows


def _half_bf16(g4, c_other, *, name):
    _, _, rows, width = g4.shape
    tb = _row_block(rows)

    def body(c_ref, p_ref, o_ref):
        o_ref[0] = p_ref[0, 0].astype(bf16)

    grid_spec = pltpu.PrefetchScalarGridSpec(
        num_scalar_prefetch=1, grid=(4, rows // tb),
        in_specs=[pl.BlockSpec((1, 1, tb, width), lambda j, i, c_ref: (j, c_ref[0], i, 0))],
        out_specs=pl.BlockSpec((1, tb, width), lambda j, i, c_ref: (j, i, 0)))
    return pl.pallas_call(
        body, name=name, grid_spec=grid_spec, out_shape=jax.ShapeDtypeStruct((4, rows, width), bf16),
        compiler_params=_params(2),
    )(c_other, g4)


def _pair_sum(g4, recv, c_me, *, name):
    _, _, rows, width = g4.shape
    tb = _row_block(rows)

    def body(c_ref, p_ref, r_ref, o_ref, ob_ref):
        s = p_ref[0, 0] + r_ref[0].astype(f32)
        o_ref[0] = s
        ob_ref[0] = s.astype(bf16)

    blk = pl.BlockSpec((1, tb, width), lambda j, i, c_ref: (j, i, 0))
    grid_spec = pltpu.PrefetchScalarGridSpec(
        num_scalar_prefetch=1, grid=(4, rows // tb),
        in_specs=[pl.BlockSpec((1, 1, tb, width), lambda j, i, c_ref: (j, c_ref[0], i, 0)), blk],
        out_specs=[blk, blk])
    return pl.pallas_call(
        body, name=name, grid_spec=grid_spec,
        out_shape=[jax.ShapeDtypeStruct((4, rows, width), f32), jax.ShapeDtypeStruct((4, rows, width), bf16)],
        compiler_params=_params(2),
    )(c_me, g4, recv)


def _adam_shard(hsum, recv, chip, w, m, v, *, name):
    _, rows, width = w.shape
    tb = _row_block(rows)

    def body(j_ref, h_ref, r_ref, w_ref, m_ref, v_ref, g_out, d_out, m_out, v_out):
        g = ((h_ref[0] + r_ref[0].astype(f32)) + r_ref[1].astype(f32)) + r_ref[2].astype(f32)
        delta, mn, vn = _adam_math(w_ref[0], g, m_ref[0], v_ref[0])
        g_out[0] = g
        d_out[0] = delta
        m_out[0] = mn
        v_out[0] = vn

    blk = pl.BlockSpec((1, tb, width), lambda i, j_ref: (0, i, 0))
    grid_spec = pltpu.PrefetchScalarGridSpec(
        num_scalar_prefetch=1, grid=(rows // tb,),
        in_specs=[pl.BlockSpec((1, tb, width), lambda i, j_ref: (j_ref[0], i, 0)),
                  pl.BlockSpec((3, tb, width), lambda i, j_ref: (0, i, 0)), blk, blk, blk],
        out_specs=[blk, blk, blk, blk])
    return pl.pallas_call(
        body, name=name, grid_spec=grid_spec, out_shape=[jax.ShapeDtypeStruct(w.shape, f32)] * 4,
        compiler_params=_params(1),
    )(chip, hsum, recv, w, m, v)


def _sum_shard(hsum, recv, chip, *, name):
    _, rows, width = hsum.shape
    tb = _row_block(rows)

    def body(j_ref, h_ref, r_ref, g_out):
        g_out[...] = ((h_ref[0] + r_ref[0].astype(f32)) + r_ref[1].astype(f32)) + r_ref[2].astype(f32)

    grid_spec = pltpu.PrefetchScalarGridSpec(
        num_scalar_prefetch=1, grid=(rows // tb,),
        in_specs=[pl.BlockSpec((1, tb, width), lambda i, j_ref: (j_ref[0], i, 0)),
                  pl.BlockSpec((3, tb, width), lambda i, j_ref: (0, i, 0))],
        out_specs=pl.BlockSpec((tb, width), lambda i, j_ref: (i, 0)))
    return pl.pallas_call(body, name=name, grid_spec=grid_spec, out_shape=jax.ShapeDtypeStruct((rows, width), f32),
                          compiler_params=_params(1))(chip, hsum, recv)


def _adam_columns(g, w, m, v, *, name):
    cols, _, rows = w.shape
    tb = cols // 2

    def body(g_ref, w_ref, m_ref, v_ref, d_out, m_out, v_out):
        delta, mn, vn = _adam_math(w_ref[...], g_ref[...], m_ref[...], v_ref[...])
        d_out[...] = delta
        m_out[...] = mn
        v_out[...] = vn

    blk = pl.BlockSpec((tb, 1, rows), lambda i: (i, 0, 0))
    return pl.pallas_call(
        body, name=name, grid=(cols // tb,), in_specs=[blk] * 4, out_specs=[blk] * 3,
        out_shape=[jax.ShapeDtypeStruct(w.shape, f32)] * 3, compiler_params=_params(1),
    )(g, w, m, v)


R_SMALL = 8 + 8 * N_DEV
_SMALL_LANES = {"gdn_norm_g": (0, DH), "gdn_A_log": (DH, DH + H), "gdn_dt_bias": (2 * DH, 2 * DH + H)}
_LOSS_LANE = 3 * DH


def _pack_small(dg1, dg2, dg3, dgn, dal, ddt, loss_p, dwa, dwg, dwf):
    def body(dg1_ref, dg2_ref, dg3_ref, dgn_ref, dal_ref, ddt_ref, loss_ref, dwa_ref, dwg_ref, dwf_ref, o_ref):
        def total(ref):
            return jnp.sum(ref[...], axis=0, keepdims=True)

        o_ref[...] = jnp.zeros_like(o_ref)
        o_ref[0:1, :] = total(dg1_ref)
        o_ref[1:2, :] = total(dg2_ref)
        o_ref[2:3, :] = total(dg3_ref)
        o_ref[3:4, 0:DH] = total(dgn_ref)
        o_ref[3:4, DH:2 * DH] = total(dal_ref)
        o_ref[3:4, 2 * DH:3 * DH] = total(ddt_ref)
        o_ref[3:4, 3 * DH:4 * DH] = total(loss_ref)
        for d in range(N_DEV):
            base = 8 + 8 * d
            o_ref[base:base + 3, 0:128] = dwa_ref[0:3, 128 * d:128 * (d + 1)]
            o_ref[base:base + 4, 128:512] = dwg_ref[0:4, 384 * d:384 * (d + 1)]
            o_ref[base + 4:base + 7, 0:704] = dwf_ref[0:3, 704 * d:704 * (d + 1)]

    return pl.pallas_call(body, name="pack_small", out_shape=jax.ShapeDtypeStruct((R_SMALL, D), f32))(
        dg1, dg2, dg3, dgn, dal, ddt, loss_p, dwa, dwg, dwf)


_SMALL = ("norm_mix_g", "norm_ffn_g", "norm_final_g", "gdn_norm_g", "gdn_A_log", "gdn_dt_bias",
          "conv_a_w", "gdn_conv_w", "ffn_conv_w")


def _adam_small(gath, me, w, m, v):
    arrays = [t[n] for n in _SMALL for t in (w, m, v)]

    def body(me_ref, ga_ref, gb_ref, *refs):
        ins, outs = refs[:len(arrays)], refs[len(arrays):]
        ga, gb = ga_ref[0], gb_ref[0]
        for s in range(1, N_DEV):
            ga = ga + ga_ref[s]
            gb = gb + gb_ref[s]
        grads = {"norm_mix_g": ga[0:1, :], "norm_ffn_g": ga[1:2, :], "norm_final_g": ga[2:3, :],
                 "conv_a_w": gb[0:3, 0:128], "gdn_conv_w": gb[0:4, 128:512], "ffn_conv_w": gb[4:7, 0:704]}
        for n, (lo, hi) in _SMALL_LANES.items():
            grads[n] = ga[3:4, lo:hi]
        for i, n in enumerate(_SMALL):
            three_d = len(w[n].shape) == 3
            wv, mv, vv = (r[0] if three_d else r[...] for r in ins[3 * i:3 * i + 3])
            delta, mn, vn = _adam_math(wv, grads[n], mv, vv)
            for o_ref, val in zip(outs[4 * i:4 * i + 4], (grads[n], delta, mn, vn)):
                if three_d:
                    o_ref[0] = val
                else:
                    o_ref[...] = val
        outs[-1][...] = ga[3:4, _LOSS_LANE:_LOSS_LANE + 1]

    def whole(shape):
        return pl.BlockSpec(shape, lambda i, me_ref: (0,) * len(shape))

    grid_spec = pltpu.PrefetchScalarGridSpec(
        num_scalar_prefetch=1, grid=(1,),
        in_specs=[pl.BlockSpec((N_DEV, 8, D), lambda i, me_ref: (0, 0, 0)),
                  pl.BlockSpec((N_DEV, 8, D), lambda i, me_ref: (0, 1 + me_ref[0], 0))] + [whole(a.shape) for a in arrays],
        out_specs=[whole(w[n].shape) for n in _SMALL for _ in range(4)] + [whole((1, 1))])
    res = pl.pallas_call(
        body, name="adam_small", grid_spec=grid_spec,
        out_shape=[jax.ShapeDtypeStruct(w[n].shape, f32) for n in _SMALL for _ in range(4)]
        + [jax.ShapeDtypeStruct((1, 1), f32)],
        compiler_params=_params(1),
    )(me, gath, gath, *arrays)
    return {n: tuple(res[4 * i:4 * i + 4]) for i, n in enumerate(_SMALL)}, res[-1]


def _adam_math(w, g, m, v):
    m = ADAM_B1 * m + (1.0 - ADAM_B1) * g
    v = ADAM_B2 * v + (1.0 - ADAM_B2) * jnp.square(g)
    m_hat = m / (1.0 - ADAM_B1 ** ADAM_STEP)
    v_hat = v / (1.0 - ADAM_B2 ** ADAM_STEP)
    delta = -ADAM_LR * (m_hat / (jnp.sqrt(v_hat) + ADAM_EPS) + ADAM_WD * w)
    return delta, m, v


_WEIGHTS = ("norm_mix_g", "w_in", "conv_a_w", "gdn_conv_w", "gdn_A_log", "gdn_dt_bias", "gdn_norm_g", "w_a_out",
            "w_b_out", "w_o", "norm_ffn_g", "w_up", "ffn_conv_w", "w_down", "norm_final_g")
_CONVS = ("conv_a_w", "gdn_conv_w", "ffn_conv_w")


class _StepExchanges:
    def __init__(self, wts, mom, var, c_me, chip):
        self.wts, self.mom, self.var, self.c_me, self.chip = wts, mom, var, c_me, chip
        self.results = {}

    def gather_first(self):
        return _gather_exchange([self.wts["w_in"][0].astype(bf16)] + [self.wts[n][0] for n in _CONVS])

    def finish_first(self, gathered):
        g_in, gc_a, gc_g, gc_f = gathered
        w1, w2 = _cols_to_matrices(g_in, _IN_RANGES, (NW1, 128), name="relay_w_in")
        return {"w1": w1, "w2": w2, "conv_a_w": gc_a.transpose(1, 0, 2).reshape(3, D),
                "gdn_conv_w": gc_g.transpose(1, 0, 2).reshape(4, 3 * D),
                "ffn_conv_w": gc_f.transpose(1, 0, 2).reshape(3, 2 * DFF)}

    def gather_rest(self):
        return _gather_direct_exchange([self.wts[n][0].astype(bf16) for n in _REST])

    def finish_gather(self, gathered):
        g_up, g_a, g_b, g_o, g_down = gathered
        return {"w_up": g_up.reshape(2 * DFF, D), "w_a_out": g_a.reshape(D, D), "w_b_out": g_b.reshape(D, D),
                "w_o": g_o.reshape(D, D), "w_down": g_down.reshape(DFF, D)}

    def reduce_halves(self, names, grads):
        blocks = []
        for n in names:
            if n == "w_in":
                g = _transposed_matrices_to_blocks([grads["w1"], grads["w2"]], _IN_RANGES, R_IN, name="relay_dw_in")
                blocks.append(g.reshape(4, 2, R_IN, D))
            else:
                blocks.append(grads[n].reshape(4, 2, *self.wts[n].shape[1:]))
        return _sibling_exchange([_half_bf16(g, 1 - self.c_me, name="rs_half_" + n) for n, g in zip(names, blocks)]), blocks

    def reduce_sums(self, names, blocks, recv):
        sums = [_pair_sum(g, r, self.c_me, name="rs_sum_" + n) for n, g, r in zip(names, blocks, recv)]
        return _chips_exchange([s[1] for s in sums]), [s[0] for s in sums]

    def finish_reduce(self, names, sums, recv):
        for n, s, r in zip(names, sums, recv):
            if n == "w_in":
                g = _sum_shard(s, r, self.chip, name="rs_total_w_in")[:, None, :]
                w, m, v = (jnp.transpose(t[n], (2, 0, 1)) for t in (self.wts, self.mom, self.var))
                res = (g, *_adam_columns(g, w, m, v, name="adam_w_in"))
                self.results[n] = tuple(jnp.transpose(a, (1, 2, 0)) for a in res)
            else:
                self.results[n] = _adam_shard(s, r, self.chip, self.wts[n], self.mom[n], self.var[n], name="adam_" + n)


def kernel(x, norm_mix_g, w_in, conv_a_w, gdn_conv_w, gdn_A_log, gdn_dt_bias, gdn_norm_g, w_a_out, w_b_out, w_o, norm_ffn_g, w_up, ffn_conv_w, w_down, norm_final_g, loss_target, m_norm_mix_g, m_w_in, m_conv_a_w, m_gdn_conv_w, m_gdn_A_log, m_gdn_dt_bias, m_gdn_norm_g, m_w_a_out, m_w_b_out, m_w_o, m_norm_ffn_g, m_w_up, m_ffn_conv_w, m_w_down, m_norm_final_g, v_norm_mix_g, v_w_in, v_conv_a_w, v_gdn_conv_w, v_gdn_A_log, v_gdn_dt_bias, v_gdn_norm_g, v_w_a_out, v_w_b_out, v_w_o, v_norm_ffn_g, v_w_up, v_ffn_conv_w, v_w_down, v_norm_final_g):
    wts = dict(zip(_WEIGHTS, (norm_mix_g, w_in, conv_a_w, gdn_conv_w, gdn_A_log, gdn_dt_bias, gdn_norm_g, w_a_out,
                              w_b_out, w_o, norm_ffn_g, w_up, ffn_conv_w, w_down, norm_final_g)))
    mom = dict(zip(_WEIGHTS, (m_norm_mix_g, m_w_in, m_conv_a_w, m_gdn_conv_w, m_gdn_A_log, m_gdn_dt_bias,
                              m_gdn_norm_g, m_w_a_out, m_w_b_out, m_w_o, m_norm_ffn_g, m_w_up, m_ffn_conv_w,
                              m_w_down, m_norm_final_g)))
    var = dict(zip(_WEIGHTS, (v_norm_mix_g, v_w_in, v_conv_a_w, v_gdn_conv_w, v_gdn_A_log, v_gdn_dt_bias,
                              v_gdn_norm_g, v_w_a_out, v_w_b_out, v_w_o, v_norm_ffn_g, v_w_up, v_ffn_conv_w,
                              v_w_down, v_norm_final_g)))
    cx, cy, cc = lax.axis_index("x"), lax.axis_index("y"), lax.axis_index("c")
    c_me = jnp.reshape(cc, (1,)).astype(jnp.int32)
    chip = jnp.reshape(2 * cx + cy, (1,)).astype(jnp.int32)
    me = jnp.reshape(4 * cx + 2 * cy + cc, (1,)).astype(jnp.int32)

    def with_up_transposed(t):
        return {**t, "w_up": jnp.swapaxes(t["w_up"], 1, 2)}

    comm = _StepExchanges(with_up_transposed(wts), with_up_transposed(mom), with_up_transposed(var), c_me, chip)
    replicated = {n: wts[n] for n in ("norm_mix_g", "norm_ffn_g", "norm_final_g", "gdn_norm_g", "gdn_A_log", "gdn_dt_bias")}
    loss_p, dx, grads = _local_step(x[0], loss_target[0], replicated, comm)
    res = comm.results
    res["w_up"] = tuple(jnp.swapaxes(a, 1, 2) for a in res["w_up"])

    small = _pack_small(grads["norm_mix_g"], grads["norm_ffn_g"], grads["norm_final_g"], grads["gdn_norm_g"],
                        grads["gdn_A_log"], grads["gdn_dt_bias"], loss_p, grads["conv_a_w"], grads["gdn_conv_w"],
                        grads["ffn_conv_w"])
    (small_all,) = _run_exchange(_gather_exchange([small]), name="ag_small")

    def raw(t):
        return {n: t[n].reshape(1, D) if n == "norm_final_g" else t[n] for n in _SMALL}

    res_small, loss = _adam_small(small_all, me, raw(wts), raw(mom), raw(var))
    for n in _SMALL:
        res[n] = tuple(a.reshape(wts[n].shape) for a in res_small[n])
    outs = [[res[n][i] for n in _WEIGHTS] for i in range(4)]
    return (loss.reshape(()), dx[None], *outs[0], *outs[1], *outs[2], *outs[3])
```

```python
import jax
import jax.numpy as jnp
from jax import lax
from jax.experimental import pallas as pl
from jax.experimental.pallas import tpu as pltpu

f32 = jnp.float32
bf16 = jnp.bfloat16

D = 1024
H = 8
DH = 128
CH = 64
GDN_STEP = 2
DFF = 2816
NW1 = 9216
EPS = 1e-6
N_DEV = 8

ADAM_LR = 0.001
ADAM_B1 = 0.9
ADAM_B2 = 0.999
ADAM_EPS = 1e-08
ADAM_WD = 0.01
ADAM_STEP = 10

VMEM_LIMIT_BYTES = 32 * 1024 * 1024
VMEM_MAX_BYTES = 56 * 1024 * 1024

R_IN, R_UP = 1154, 704

_HI = lax.Precision.HIGHEST
MESH = pl.DeviceIdType.MESH


def _params(n_grid, vmem_bytes=None):
    return pltpu.CompilerParams(dimension_semantics=("arbitrary",) * n_grid,
                                vmem_limit_bytes=VMEM_LIMIT_BYTES if vmem_bytes is None else vmem_bytes)


def _vmem_for(*block_bytes, extra=0):
    need = 2 * sum(block_bytes) + extra + 4 * 1024 * 1024
    return min(max(need, 16 * 1024 * 1024), VMEM_MAX_BYTES)


def _bdot(a, b):
    return jnp.dot(a.astype(bf16), b.astype(bf16), preferred_element_type=f32)


def _bdot_nt(a, b):
    return lax.dot_general(a.astype(bf16), b.astype(bf16), (((1,), (1,)), ((), ())), preferred_element_type=f32)


def _bdot_tn(a, b):
    return lax.dot_general(a.astype(bf16), b.astype(bf16), (((0,), (0,)), ((), ())), preferred_element_type=f32)


def _hdot(a, b):
    return jnp.dot(a, b, preferred_element_type=f32, precision=_HI)


def _idot(a, b):
    return jnp.dot(a, b, preferred_element_type=f32, precision=lax.Precision.HIGH)


def _sigmoid(x):
    return 1.0 / (1.0 + jnp.exp(-x))


def _softplus(x):
    return jnp.maximum(x, 0.0) + jnp.log(1.0 + jnp.exp(-jnp.abs(x)))


def _shift_down(x, halo, j):
    if j == 0:
        return x
    xr = pltpu.roll(x, j, 0)
    hr = pltpu.roll(halo, j, 0)
    r8 = lax.broadcasted_iota(jnp.int32, hr.shape, 0)
    top = jnp.where(r8 < j, hr, xr[:8])
    return jnp.concatenate([top, xr[8:]], axis=0)


def _shift_up(x, halo, j):
    if j == 0:
        return x
    n = x.shape[0]
    xr = pltpu.roll(x, n - j, 0)
    hr = pltpu.roll(halo, 8 - j, 0)
    r8 = lax.broadcasted_iota(jnp.int32, hr.shape, 0)
    bot = jnp.where(r8 >= 8 - j, hr, xr[n - 8:])
    return jnp.concatenate([xr[:n - 8], bot], axis=0)


def _taps_down(x, halo, k):
    return [_shift_down(x, halo, k - 1 - j) for j in range(k)]


def _strip(i, base=0):
    return slice(base + i * 128, base + (i + 1) * 128)


def _strip_taps(x, halo, first, k):
    return _taps_down(x, jnp.where(first, 0.0, halo), k)


def _strip_conv(w_ref, sl, taps):
    out = w_ref[0:1, sl] * taps[0]
    for j in range(1, len(taps)):
        out = out + w_ref[j:j + 1, sl] * taps[j]
    return out


def _strip_weight_grad(dw_ref, sl, dy, taps):
    for j, tap in enumerate(taps):
        dw_ref[j:j + 1, sl] += jnp.sum(dy * tap, axis=0, keepdims=True)


def _strip_conv_up(dy, halo, last, w_ref, sl, k):
    halo = jnp.where(last, 0.0, halo)
    out = w_ref[k - 1:k, sl] * dy
    for j in range(k - 1):
        out = out + w_ref[j:j + 1, sl] * _shift_up(dy, halo, k - 1 - j)
    return out


def _row(tb, w, col=0):
    return pl.BlockSpec((tb, w), lambda i: (i, col))


def _prev(tb, w, col=0, rows=8):
    return pl.BlockSpec((rows, w), lambda i: (jnp.maximum(i * (tb // rows) - 1, 0), col))


def _next(tb, w, n_rows, col=0, rows=8):
    last = n_rows // rows - 1
    return pl.BlockSpec((rows, w), lambda i: (jnp.minimum((i + 1) * (tb // rows), last), col))


def _f32(ref, sl):
    return ref[:, sl].astype(f32)


def _halo_before(ref, sl):
    h = _f32(ref, sl)
    return h[h.shape[0] - 8:]


def _halo_after(ref, sl):
    return _f32(ref, sl)[:8]


def _fixed(shape):
    return pl.BlockSpec(shape, lambda i: (0,) * len(shape))


def _pick(n, prefs):
    for p in prefs:
        if n % p == 0:
            return p
    return n


def _matmul(a, b, *, name, nt=False, add=None, tm=1024, tn=1024, tk=None, out_dtype=f32, cols=None, exchange=None):
    m, kd = a.shape
    col0, n = cols if cols is not None else (0, b.shape[0] if nt else b.shape[1])
    tm = _pick(m, (tm, 512, 256))
    tn = _pick(n, (tn, 1024, 512, 128))
    tk = kd if tk is None else tk
    nk = kd // tk
    assert nk == 1 or out_dtype == f32
    assert col0 % tn == 0 and not (nt and cols)
    j0 = col0 // tn
    dims = (((1,), (1,)), ((), ())) if nt else (((1,), (0,)), ((), ()))

    def body(a_ref, b_ref, *rest):
        o_ref = rest[-1]
        part = lax.dot_general(a_ref[...], b_ref[...], dims, preferred_element_type=f32)
        if nk == 1:
            o_ref[...] = (part if add is None else part + rest[0][...]).astype(out_dtype)
            return
        k = pl.program_id(2)

        @pl.when(k == 0)
        def _():
            o_ref[...] = part if add is None else part + rest[0][...]

        @pl.when(k > 0)
        def _():
            o_ref[...] += part

    b_spec = pl.BlockSpec((tn, tk), lambda i, j, k: (j, k)) if nt else pl.BlockSpec((tk, tn), lambda i, j, k: (k, j + j0))
    in_specs = [pl.BlockSpec((tm, tk), lambda i, j, k: (i, k)), b_spec]
    args = [a, b]
    if add is not None:
        in_specs.append(pl.BlockSpec((tm, tn), lambda i, j, k: (i, j)))
        args.append(add)
    vmem = _vmem_for(2 * tm * tk, 2 * tk * tn, tm * tn * jnp.dtype(out_dtype).itemsize,
                     4 * tm * tn if add is not None else 0, extra=4 * tm * tn)
    return _call_with_exchange(
        body, exchange, name=name, grid=(m // tm, n // tn, nk), in_specs=in_specs,
        out_specs=pl.BlockSpec((tm, tn), lambda i, j, k: (i, j)),
        out_shape=jax.ShapeDtypeStruct((m, n), out_dtype), args=args, vmem_bytes=vmem)


def _call_with_exchange(body, exchange, *, name, grid, in_specs, out_specs, out_shape, args, vmem_bytes=None):
    if exchange is None:
        return pl.pallas_call(body, name=name, grid=grid, in_specs=in_specs, out_specs=out_specs, out_shape=out_shape,
                              compiler_params=_params(len(grid), vmem_bytes))(*args)
    x_arrays, x_shapes, x_sems, start, wait = exchange[:5]
    n_in, n_xin, n_xout = len(args), len(x_arrays), len(x_shapes)
    aliases = {n_in + i: 1 + i for i in range(n_xin)} if len(exchange) > 5 and exchange[5] else {}

    def full_body(*refs):
        c_in, x_in = refs[:n_in], refs[n_in:n_in + n_xin]
        c_out = refs[n_in + n_xin]
        x_out = refs[n_in + n_xin + 1:n_in + n_xin + 1 + n_xout]
        sems = refs[n_in + n_xin + 1 + n_xout:]
        ids = [pl.program_id(d) for d in range(len(grid))]
        first, last = ids[0] == 0, ids[0] == grid[0] - 1
        for d in range(1, len(grid)):
            first = first & (ids[d] == 0)
            last = last & (ids[d] == grid[d] - 1)

        @pl.when(first)
        def _():
            start(x_in, x_out, sems)

        body(*c_in, c_out)

        @pl.when(last)
        def _():
            wait(x_in, x_out, sems)

    res = pl.pallas_call(
        full_body, name=name, grid=grid, in_specs=list(in_specs) + [_ANY] * n_xin,
        out_specs=[out_specs] + [_ANY] * n_xout, out_shape=[out_shape] + list(x_shapes),
        scratch_shapes=list(x_sems), input_output_aliases=aliases, compiler_params=_params(len(grid), vmem_bytes),
    )(*args, *x_arrays)
    return res[0], list(res[1:])


def _matmul_tn(a, b, *, name, tm=1024, tn=1024, tt=2048, exchange=None):
    t, m = a.shape
    _, n = b.shape
    tm = _pick(m, (tm, 1024, 512, 128))
    tn = _pick(n, (tn, 1024, 512, 128))
    tt = _pick(t, (tt, 2048, 1024, 512, 256))
    nt = t // tt

    def body(a_ref, b_ref, o_ref):
        k = pl.program_id(2)
        part = lax.dot_general(a_ref[...], b_ref[...], (((0,), (0,)), ((), ())), preferred_element_type=f32)

        @pl.when(k == 0)
        def _():
            o_ref[...] = part

        @pl.when(k > 0)
        def _():
            o_ref[...] += part

    return _call_with_exchange(
        body, exchange, name=name, grid=(m // tm, n // tn, nt),
        in_specs=[pl.BlockSpec((tt, tm), lambda i, j, k: (k, i)), pl.BlockSpec((tt, tn), lambda i, j, k: (k, j))],
        out_specs=pl.BlockSpec((tm, tn), lambda i, j, k: (i, j)),
        out_shape=jax.ShapeDtypeStruct((m, n), f32), args=[a, b],
        vmem_bytes=_vmem_for(2 * tt * tm, 2 * tt * tn, 4 * tm * tn, extra=4 * tm * tn + 2 * tt * tm))


def _rms_fwd(x, g, *, name, exchange=None):
    t = x.shape[0]
    tb = _pick(t, (256, 128))

    def body(x_ref, g_ref, h_ref):
        xv = x_ref[...]
        r = lax.rsqrt(jnp.mean(xv * xv, axis=-1, keepdims=True) + EPS)
        h_ref[...] = (xv * r * g_ref[...]).astype(bf16)

    return _call_with_exchange(
        body, exchange, name=name, grid=(t // tb,), in_specs=[_row(tb, D), _fixed((1, D))], out_specs=_row(tb, D),
        out_shape=jax.ShapeDtypeStruct((t, D), bf16), args=[x, g])


def _rms_bwd(dh, x, g, dres, *, name, more=None):
    t = x.shape[0]
    tb = _pick(t, (256, 128))

    def body(dh_ref, x_ref, g_ref, dres_ref, *rest):
        dx_ref, dxb_ref, dg_ref = rest[-3:]
        xv = x_ref[...]
        r = lax.rsqrt(jnp.mean(xv * xv, axis=-1, keepdims=True) + EPS)
        xh = xv * r
        dy = dh_ref[...]
        if more is not None:
            dy = dy + lax.dot_general(rest[0][...], rest[1][...], (((1,), (1,)), ((), ())), preferred_element_type=f32)
        dyg = dy * g_ref[...]
        dx = dres_ref[...] + r * (dyg - xh * jnp.mean(dyg * xh, axis=-1, keepdims=True))
        dx_ref[...] = dx
        dxb_ref[...] = dx.astype(bf16)

        @pl.when(pl.program_id(0) == 0)
        def _():
            dg_ref[...] = jnp.zeros_like(dg_ref)

        dg_ref[...] += jnp.sum((dy * xh).reshape(tb // 8, 8, D), axis=0)

    in_specs, args = [_row(tb, D), _row(tb, D), _fixed((1, D)), _row(tb, D)], [dh, x, g, dres]
    if more is not None:
        in_specs += [_row(tb, 128), _fixed(more[1].shape)]
        args += list(more)
    return pl.pallas_call(
        body, name=name, grid=(t // tb,), in_specs=in_specs,
        out_specs=[_row(tb, D), _row(tb, D), _fixed((8, D))],
        out_shape=[jax.ShapeDtypeStruct((t, D), f32), jax.ShapeDtypeStruct((t, D), bf16),
                   jax.ShapeDtypeStruct((8, D), f32)],
        compiler_params=_params(1),
    )(*args)


def _gdn_gates(ab, alog, dtb):
    lane = lax.broadcasted_iota(jnp.int32, ab.shape, 1)
    g = -jnp.exp(alog) * _softplus(ab + dtb)
    beta = _sigmoid(ab)
    return jnp.where(lane < H, g, jnp.where(lane < 2 * H, beta, 0.0))


def _pre_fwd(pg, pq, h1, w2, wa, wg, alog, dtb):
    t = pg.shape[0]
    tb = 128

    def body(p0_ref, p0h_ref, pq_ref, pqh_ref, h1_ref, w2_ref, wa_ref, wg_ref, alog_ref, dtb_ref,
             ya_ref, qn_ref, kn_ref, vc_ref, gb_ref, p2_ref):
        first = pl.program_id(0) == 0
        p2_ref[...] = jnp.dot(h1_ref[...], w2_ref[...], preferred_element_type=f32)
        for i in range(D // 128):
            sl, cg, xv = _strip(i), _strip(i, D), _strip(i, 2 * D)
            taps = _strip_taps(_f32(p0_ref, cg) * _f32(p0_ref, xv), _halo_before(p0h_ref, cg) * _halo_before(p0h_ref, xv),
                               first, 3)
            ya_ref[:, sl] = (_f32(p0_ref, sl) * _strip_conv(wa_ref, sl, taps)).astype(bf16)
        for part, out_ref, scale in ((0, qn_ref, DH ** -0.5), (1, kn_ref, 1.0), (2, vc_ref, None)):
            for h in range(H):
                sl = _strip(h, part * D)
                s = _strip_conv(wg_ref, sl, _strip_taps(pq_ref[:, sl], pqh_ref[:, sl], first, 4))
                s = s * _sigmoid(s)
                if scale is not None:
                    s = s * (lax.rsqrt(jnp.sum(s * s, axis=-1, keepdims=True) + EPS) * scale)
                out_ref[:, _strip(h)] = s
        gb_ref[...] = _gdn_gates(p2_ref[...], alog_ref[...], dtb_ref[...])

    return pl.pallas_call(
        body, name="pre_fwd", grid=(t // tb,),
        in_specs=[_row(tb, 3 * D, 0), _prev(tb, 3 * D, 0, rows=16), _row(tb, 3 * D), _prev(tb, 3 * D), _row(tb, D),
                  _fixed((D, 128)), _fixed((8, D)), _fixed((8, 3 * D)), _fixed((1, 128)), _fixed((1, 128))],
        out_specs=[_row(tb, D), _row(tb, D), _row(tb, D), _row(tb, D), _row(tb, 128), _row(tb, 128)],
        out_shape=[jax.ShapeDtypeStruct((t, D), bf16), jax.ShapeDtypeStruct((t, D), f32),
                   jax.ShapeDtypeStruct((t, D), f32), jax.ShapeDtypeStruct((t, D), f32),
                   jax.ShapeDtypeStruct((t, 128), f32), jax.ShapeDtypeStruct((t, 128), f32)],
        compiler_params=_params(1),
    )(pg, pg, pq, pq, h1, w2, wa, wg, alog, dtb)


_Z_COL, _GA_COL, _GB_COL = 3, 4, 5


def _post_fwd(o, pg, gn):
    t = o.shape[0]
    tb = _pick(t, (256, 128))

    def body(o_ref, z_ref, gn_ref, yb_ref):
        for h in range(H):
            sl = slice(h * DH, (h + 1) * DH)
            oh = o_ref[:, sl]
            z = _f32(z_ref, sl)
            r = lax.rsqrt(jnp.mean(oh * oh, axis=-1, keepdims=True) + EPS)
            yb_ref[:, sl] = (oh * r * gn_ref[...] * (z * _sigmoid(z))).astype(bf16)

    return pl.pallas_call(
        body, name="post_fwd", grid=(t // tb,), in_specs=[_row(tb, D), _row(tb, D, _Z_COL), _fixed((1, DH))],
        out_specs=_row(tb, D), out_shape=jax.ShapeDtypeStruct((t, D), bf16), compiler_params=_params(1),
    )(o, pg, gn)


def _post_bwd(dyb, o, pg, gn):
    t = o.shape[0]
    tb = _pick(t, (256, 128))

    def body(dyb_ref, o_ref, z_ref, gn_ref, do_ref, dz_ref, dgn_ref):
        @pl.when(pl.program_id(0) == 0)
        def _():
            dgn_ref[...] = jnp.zeros_like(dgn_ref)

        gn_v = gn_ref[...]
        acc = jnp.zeros((8, DH), f32)
        for h in range(H):
            sl = slice(h * DH, (h + 1) * DH)
            oh = o_ref[:, sl]
            z = _f32(z_ref, sl)
            dy = dyb_ref[:, sl]
            r = lax.rsqrt(jnp.mean(oh * oh, axis=-1, keepdims=True) + EPS)
            on = oh * r
            sg = _sigmoid(z)
            sz = z * sg
            don = dy * sz
            dz_ref[:, sl] = (dy * on * gn_v * (sg * (1.0 + z * (1.0 - sg)))).astype(bf16)
            acc = acc + jnp.sum((don * on).reshape(tb // 8, 8, DH), axis=0)
            doh = don * gn_v
            do_ref[:, sl] = r * (doh - on * jnp.mean(doh * on, axis=-1, keepdims=True))
        dgn_ref[...] += acc

    return pl.pallas_call(
        body, name="post_bwd", grid=(t // tb,),
        in_specs=[_row(tb, D), _row(tb, D), _row(tb, D, _Z_COL), _fixed((1, DH))],
        out_specs=[_row(tb, D), _row(tb, D), _fixed((8, DH))],
        out_shape=[jax.ShapeDtypeStruct((t, D), f32), jax.ShapeDtypeStruct((t, D), bf16),
                   jax.ShapeDtypeStruct((8, DH), f32)],
        compiler_params=_params(1),
    )(dyb, o, pg, gn)


def _mix_fwd(ya, yb, pg):
    t = ya.shape[0]
    tb = _pick(t, (256, 128))

    def body(ya_ref, yb_ref, ga_ref, gb_ref, mix_ref):
        ya_v, yb_v = ya_ref[...].astype(f32), yb_ref[...].astype(f32)
        mix = _sigmoid(ga_ref[...].astype(f32)) * ya_v + _sigmoid(gb_ref[...].astype(f32)) * yb_v
        mix_ref[...] = mix.astype(bf16)

    return pl.pallas_call(
        body, name="mix_fwd", grid=(t // tb,),
        in_specs=[_row(tb, D), _row(tb, D), _row(tb, D, _GA_COL), _row(tb, D, _GB_COL)],
        out_specs=_row(tb, D), out_shape=jax.ShapeDtypeStruct((t, D), bf16), compiler_params=_params(1),
    )(ya, yb, pg, pg)


def _mix_bwd(dmix, ya, yb, pg):
    t = ya.shape[0]
    tb = _pick(t, (256, 128))

    def body(dm_ref, ya_ref, yb_ref, ga_ref, gb_ref, dya_ref, dyb_ref, dg_ref):
        dm = dm_ref[...].astype(f32)
        sa = _sigmoid(ga_ref[...].astype(f32))
        sb = _sigmoid(gb_ref[...].astype(f32))
        dya_ref[...] = (dm * sa).astype(bf16)
        dyb_ref[...] = (dm * sb).astype(bf16)
        dg_ref[:, :D] = (dm * ya_ref[...].astype(f32) * sa * (1.0 - sa)).astype(bf16)
        dg_ref[:, D:] = (dm * yb_ref[...].astype(f32) * sb * (1.0 - sb)).astype(bf16)

    return pl.pallas_call(
        body, name="mix_bwd", grid=(t // tb,),
        in_specs=[_row(tb, D), _row(tb, D), _row(tb, D), _row(tb, D, _GA_COL), _row(tb, D, _GB_COL)],
        out_specs=[_row(tb, D), _row(tb, D), _row(tb, 2 * D)],
        out_shape=[jax.ShapeDtypeStruct((t, D), bf16), jax.ShapeDtypeStruct((t, D), bf16),
                   jax.ShapeDtypeStruct((t, 2 * D), bf16)],
        compiler_params=_params(1),
    )(dmix, ya, yb, pg, pg)


def _ffn_fwd(up, wf):
    t = up.shape[0]
    tb = 128

    def body(up_ref, uph_ref, wf_ref, act_ref):
        first = pl.program_id(0) == 0
        for i in range(DFF // 128):
            g, v = _strip(i), _strip(i, DFF)
            gate = _strip_conv(wf_ref, g, _strip_taps(_f32(up_ref, g), _halo_before(uph_ref, g), first, 3))
            val = _strip_conv(wf_ref, v, _strip_taps(_f32(up_ref, v), _halo_before(uph_ref, v), first, 3))
            act_ref[:, g] = (gate * _sigmoid(gate) * val).astype(bf16)

    return pl.pallas_call(
        body, name="ffn_fwd", grid=(t // tb,),
        in_specs=[_row(tb, 2 * DFF), _prev(tb, 2 * DFF, rows=16), _fixed((8, 2 * DFF))],
        out_specs=_row(tb, DFF), out_shape=jax.ShapeDtypeStruct((t, DFF), bf16), compiler_params=_params(1),
    )(up, up, wf)


def _ffn_bwd1(dact, up, wf):
    t = up.shape[0]
    tb = 128

    def body(da_ref, up_ref, uph_ref, wf_ref, dc_ref, dw_ref):
        @pl.when(pl.program_id(0) == 0)
        def _():
            dw_ref[...] = jnp.zeros_like(dw_ref)

        first = pl.program_id(0) == 0
        for i in range(DFF // 128):
            g, v = _strip(i), _strip(i, DFF)
            g_taps = _strip_taps(_f32(up_ref, g), _halo_before(uph_ref, g), first, 3)
            v_taps = _strip_taps(_f32(up_ref, v), _halo_before(uph_ref, v), first, 3)
            gate = _strip_conv(wf_ref, g, g_taps)
            val = _strip_conv(wf_ref, v, v_taps)
            sg = _sigmoid(gate)
            da = _f32(da_ref, g)
            dgate = da * val * (sg * (1.0 + gate * (1.0 - sg)))
            dval = da * (gate * sg)
            dc_ref[:, g] = dgate.astype(bf16)
            dc_ref[:, v] = dval.astype(bf16)
            _strip_weight_grad(dw_ref, g, dgate, g_taps)
            _strip_weight_grad(dw_ref, v, dval, v_taps)

    return pl.pallas_call(
        body, name="ffn_bwd1", grid=(t // tb,),
        in_specs=[_row(tb, DFF), _row(tb, 2 * DFF), _prev(tb, 2 * DFF, rows=16), _fixed((8, 2 * DFF))],
        out_specs=[_row(tb, 2 * DFF), _fixed((8, 2 * DFF))],
        out_shape=[jax.ShapeDtypeStruct((t, 2 * DFF), bf16), jax.ShapeDtypeStruct((8, 2 * DFF), f32)],
        compiler_params=_params(1),
    )(dact, up, up, wf)


def _ffn_bwd2(dc, wf):
    t = dc.shape[0]
    tb = 128
    nb = t // tb

    def body(dc_ref, dch_ref, wf_ref, dup_ref):
        last = pl.program_id(0) == nb - 1
        for i in range(2 * DFF // 128):
            sl = _strip(i)
            dup_ref[:, sl] = _strip_conv_up(_f32(dc_ref, sl), _halo_after(dch_ref, sl), last, wf_ref, sl, 3).astype(bf16)

    return pl.pallas_call(
        body, name="ffn_bwd2", grid=(nb,),
        in_specs=[_row(tb, 2 * DFF), _next(tb, 2 * DFF, t, rows=16), _fixed((8, 2 * DFF))],
        out_specs=_row(tb, 2 * DFF), out_shape=jax.ShapeDtypeStruct((t, 2 * DFF), bf16), compiler_params=_params(1),
    )(dc, dc, wf)


def _final(x3, tgt, g):
    t = x3.shape[0]
    tb = _pick(t, (256, 128))

    def body(x_ref, t_ref, g_ref, loss_ref, dx_ref, dxb_ref, dg_ref):
        @pl.when(pl.program_id(0) == 0)
        def _():
            loss_ref[...] = jnp.zeros_like(loss_ref)
            dg_ref[...] = jnp.zeros_like(dg_ref)

        xv = x_ref[...]
        r = lax.rsqrt(jnp.mean(xv * xv, axis=-1, keepdims=True) + EPS)
        xh = xv * r
        gv = g_ref[...]
        e = xh * gv - t_ref[...]
        lrow = 0.5 * jnp.mean(e * e, axis=-1, keepdims=True)
        loss_ref[...] += jnp.sum(jnp.broadcast_to(lrow, (tb, 128)).reshape(tb // 8, 8, 128), axis=0)
        dy = e * (1.0 / D)
        dyg = dy * gv
        dx = r * (dyg - xh * jnp.mean(dyg * xh, axis=-1, keepdims=True))
        dx_ref[...] = dx
        dxb_ref[...] = dx.astype(bf16)
        dg_ref[...] += jnp.sum((dy * xh).reshape(tb // 8, 8, D), axis=0)

    return pl.pallas_call(
        body, name="final", grid=(t // tb,), in_specs=[_row(tb, D), _row(tb, D), _fixed((1, D))],
        out_specs=[_fixed((8, 128)), _row(tb, D), _row(tb, D), _fixed((8, D))],
        out_shape=[jax.ShapeDtypeStruct((8, 128), f32), jax.ShapeDtypeStruct((t, D), f32),
                   jax.ShapeDtypeStruct((t, D), bf16), jax.ShapeDtypeStruct((8, D), f32)],
        compiler_params=_params(1),
    )(x3, tgt, g)


def _pre_bwd1(pg, pq, p2, dya_in, dqn, dkn, dvc, dgb, gbeta, h1, wa, wg, alog, dtb):
    t = pg.shape[0]
    tb = 128

    def body(p0_ref, p0h_ref, pq_ref, pqh_ref, p2_ref, dya_ref, dqn_ref, dkn_ref, dvc_ref, dgb_ref, gb_ref, h1_ref,
             wa_ref, wg_ref, alog_ref, dtb_ref,
             dbg_ref, dca_ref, dc4_ref, dp2_ref, dwa_ref, dwg_ref, dal_ref, ddt_ref, dw2_ref):
        @pl.when(pl.program_id(0) == 0)
        def _():
            dwa_ref[...] = jnp.zeros_like(dwa_ref)
            dwg_ref[...] = jnp.zeros_like(dwg_ref)
            dal_ref[...] = jnp.zeros_like(dal_ref)
            ddt_ref[...] = jnp.zeros_like(ddt_ref)
            dw2_ref[...] = jnp.zeros_like(dw2_ref)

        first = pl.program_id(0) == 0

        for i in range(D // 128):
            sl, cg, xv = _strip(i), _strip(i, D), _strip(i, 2 * D)
            taps = _strip_taps(_f32(p0_ref, cg) * _f32(p0_ref, xv), _halo_before(p0h_ref, cg) * _halo_before(p0h_ref, xv),
                               first, 3)
            dya = _f32(dya_ref, sl)
            dbg_ref[:, sl] = (dya * _strip_conv(wa_ref, sl, taps)).astype(bf16)
            dca = dya * _f32(p0_ref, sl)
            dca_ref[:, sl] = dca.astype(bf16)
            _strip_weight_grad(dwa_ref, sl, dca, taps)

        for part, d_ref, scale in ((0, dqn_ref, DH ** -0.5), (1, dkn_ref, 1.0), (2, dvc_ref, None)):
            for h in range(H):
                sl = _strip(h, part * D)
                taps = _strip_taps(pq_ref[:, sl], pqh_ref[:, sl], first, 4)
                c4 = _strip_conv(wg_ref, sl, taps)
                sg = _sigmoid(c4)
                dn = d_ref[:, _strip(h)]
                if scale is not None:
                    a = c4 * sg
                    r = lax.rsqrt(jnp.sum(a * a, axis=-1, keepdims=True) + EPS)
                    an = a * r
                    dn = dn * scale
                    dn = r * (dn - an * jnp.sum(dn * an, axis=-1, keepdims=True))
                dc4 = dn * (sg * (1.0 + c4 * (1.0 - sg)))
                dc4_ref[:, sl] = dc4.astype(bf16)
                _strip_weight_grad(dwg_ref, sl, dc4, taps)

        ab = p2_ref[...]
        lane = lax.broadcasted_iota(jnp.int32, ab.shape, 1)
        dgbv = dgb_ref[...]
        gbv = gb_ref[...]
        da = dgbv * (-jnp.exp(alog_ref[...])) * _sigmoid(ab + dtb_ref[...])
        db = dgbv * gbv * (1.0 - gbv)
        dp2 = jnp.where(lane < H, da, jnp.where(lane < 2 * H, db, 0.0)).astype(bf16)
        dp2_ref[...] = dp2
        dw2_ref[...] += lax.dot_general(dp2, h1_ref[...], (((0,), (0,)), ((), ())), preferred_element_type=f32)
        dal = jnp.where(lane < H, dgbv * gbv, 0.0)
        ddt = jnp.where(lane < H, da, 0.0)
        dal_ref[...] += jnp.sum(dal.reshape(tb // 8, 8, 128), axis=0)
        ddt_ref[...] += jnp.sum(ddt.reshape(tb // 8, 8, 128), axis=0)

    return pl.pallas_call(
        body, name="pre_bwd1", grid=(t // tb,),
        in_specs=[_row(tb, 3 * D, 0), _prev(tb, 3 * D, 0, rows=16), _row(tb, 3 * D), _prev(tb, 3 * D), _row(tb, 128),
                  _row(tb, D), _row(tb, D), _row(tb, D), _row(tb, D), _row(tb, 128), _row(tb, 128), _row(tb, D),
                  _fixed((8, D)), _fixed((8, 3 * D)), _fixed((1, 128)), _fixed((1, 128))],
        out_specs=[_row(tb, D), _row(tb, D), _row(tb, 3 * D), _row(tb, 128),
                   _fixed((8, D)), _fixed((8, 3 * D)), _fixed((8, 128)), _fixed((8, 128)), _fixed((128, D))],
        out_shape=[jax.ShapeDtypeStruct((t, D), bf16), jax.ShapeDtypeStruct((t, D), bf16),
                   jax.ShapeDtypeStruct((t, 3 * D), bf16), jax.ShapeDtypeStruct((t, 128), bf16),
                   jax.ShapeDtypeStruct((8, D), f32), jax.ShapeDtypeStruct((8, 3 * D), f32),
                   jax.ShapeDtypeStruct((8, 128), f32), jax.ShapeDtypeStruct((8, 128), f32),
                   jax.ShapeDtypeStruct((128, D), f32)],
        compiler_params=_params(1),
    )(pg, pg, pq, pq, p2, dya_in, dqn, dkn, dvc, dgb, gbeta, h1, wa, wg, alog, dtb)


def _pre_bwd2(dca, dc4, pg, dbg, dz, dgates, wa, wg, exchange=None):
    t = pg.shape[0]
    tb = 128
    nb = t // tb

    def body(dca_ref, dcah_ref, dc4_ref, dc4h_ref, p0_ref, dbg_ref, dz_ref, dgt_ref, wa_ref, wg_ref, dp_ref):
        last = pl.program_id(0) == nb - 1
        dp_ref[:, :D] = dbg_ref[...]
        for i in range(D // 128):
            sl, cg, xv = _strip(i), _strip(i, D), _strip(i, 2 * D)
            du = _strip_conv_up(_f32(dca_ref, sl), _halo_after(dcah_ref, sl), last, wa_ref, sl, 3)
            dp_ref[:, cg] = (du * _f32(p0_ref, xv)).astype(bf16)
            dp_ref[:, xv] = (du * _f32(p0_ref, cg)).astype(bf16)
        dp_ref[:, 3 * D:4 * D] = dz_ref[...]
        dp_ref[:, 4 * D:6 * D] = dgt_ref[...]
        for i in range(3 * D // 128):
            sl = _strip(i)
            dq = _strip_conv_up(_f32(dc4_ref, sl), _halo_after(dc4h_ref, sl), last, wg_ref, sl, 4)
            dp_ref[:, _strip(i, 6 * D)] = dq.astype(bf16)

    return _call_with_exchange(
        body, exchange, name="pre_bwd2", grid=(nb,),
        in_specs=[_row(tb, D), _next(tb, D, t, rows=16), _row(tb, 3 * D), _next(tb, 3 * D, t, rows=16), _row(tb, 3 * D, 0),
                  _row(tb, D), _row(tb, D), _row(tb, 2 * D), _fixed((8, D)), _fixed((8, 3 * D))],
        out_specs=_row(tb, NW1), out_shape=jax.ShapeDtypeStruct((t, NW1), bf16),
        args=[dca, dca, dc4, dc4, pg, dbg, dz, dgates, wa, wg])


def _chunk_consts():
    r = lax.broadcasted_iota(jnp.int32, (CH, CH), 0)
    c = lax.broadcasted_iota(jnp.int32, (CH, CH), 1)
    return r, c, (r == c).astype(f32)


def _tri_inverse(lows, eye, r, c):
    def same_block(b):
        return jnp.bitwise_xor(r, c) < b

    xs = [jnp.where(same_block(8), -low, 0.0) for low in lows]
    ts = [eye + x for x in xs]
    for _ in range(2):
        xs = [_idot(x, x) for x in xs]
        ts = [t + _idot(t, x) for t, x in zip(ts, xs)]
    for b in (8, 16, 32):
        below = same_block(2 * b) & jnp.logical_not(same_block(b))
        ts = [t - _idot(_idot(t, jnp.where(below, low, 0.0)), t) for t, low in zip(ts, lows)]
    return ts


def _chunk_common(q, k, v, gcol, bcol, r, c, eye):
    grow = jnp.sum(eye * gcol, axis=0, keepdims=True)
    dec = jnp.exp(jnp.where(r >= c, gcol - grow, -jnp.inf))
    rcol = lax.broadcasted_iota(jnp.int32, (CH, 1), 0)
    glast = jnp.sum(jnp.where(rcol == CH - 1, gcol, 0.0), axis=0, keepdims=True)
    eg = jnp.exp(gcol)
    el = jnp.exp(glast - gcol)
    kb = k * bcol
    vb = v * bcol
    kk = _bdot_nt(kb, k)
    low = jnp.where(r > c, kk * dec, 0.0)
    qk = _bdot_nt(q, k)
    att = qk * dec
    return grow, dec, glast, eg, el, kb, vb, kk, low, qk, att, rcol


def _gdn_fwd(qn, kn, vc, gbeta):
    t = qn.shape[0]
    n_chunks = t // CH

    def body(q_ref, k_ref, v_ref, gb_ref, o_ref, s_ref, t_ref, state):
        @pl.when(pl.program_id(0) == 0)
        def _():
            state[...] = jnp.zeros_like(state)

        r, c, eye = _chunk_consts()
        tri = (r >= c).astype(f32)
        heads = range(H)
        keys = [(s, h) for s in range(GDN_STEP) for h in heads]
        rows = [slice(s * CH, (s + 1) * CH) for s in range(GDN_STEP)]
        gbs = [gb_ref[rows[s], :] for s in range(GDN_STEP)]
        galls = [_hdot(tri, gb) for gb in gbs]
        qs = {(s, h): q_ref[rows[s], h * DH:(h + 1) * DH] for s, h in keys}
        ks = {(s, h): k_ref[rows[s], h * DH:(h + 1) * DH] for s, h in keys}
        cm = {(s, h): _chunk_common(qs[s, h], ks[s, h], v_ref[rows[s], h * DH:(h + 1) * DH], galls[s][:, h:h + 1],
                                    gbs[s][:, H + h:H + h + 1], r, c, eye) for s, h in keys}
        invs = dict(zip(keys, _tri_inverse([cm[key][8] for key in keys], eye, r, c)))
        uws = {key: _bdot(invs[key], jnp.concatenate([cm[key][6], cm[key][5] * cm[key][3]], axis=1)) for key in keys}
        sts = [state[h] for h in heads]
        for s in range(GDN_STEP):
            vns = [uws[s, h][:, :DH] - _bdot(uws[s, h][:, DH:], sts[h]) for h in heads]
            outs = [_bdot(qs[s, h] * cm[s, h][3], sts[h]) + _bdot(cm[s, h][10], vns[h]) for h in heads]
            news = [sts[h] * jnp.exp(cm[s, h][2]) + _bdot_tn(ks[s, h] * cm[s, h][4], vns[h]) for h in heads]
            for h in heads:
                s_ref[s, h] = sts[h].astype(bf16)
                t_ref[s, h] = invs[s, h]
                o_ref[rows[s], h * DH:(h + 1) * DH] = outs[h]
            sts = news
        for h in heads:
            state[h] = sts[h]

    tb = GDN_STEP * CH
    return pl.pallas_call(
        body, name="gdn_fwd", grid=(t // tb,),
        in_specs=[_row(tb, D), _row(tb, D), _row(tb, D), _row(tb, 128)],
        out_specs=[_row(tb, D), pl.BlockSpec((GDN_STEP, H, DH, DH), lambda i: (i, 0, 0, 0)),
                   pl.BlockSpec((GDN_STEP, H, CH, CH), lambda i: (i, 0, 0, 0))],
        out_shape=[jax.ShapeDtypeStruct((t, D), f32), jax.ShapeDtypeStruct((n_chunks, H, DH, DH), bf16),
                   jax.ShapeDtypeStruct((n_chunks, H, CH, CH), f32)],
        scratch_shapes=[pltpu.VMEM((H, DH, DH), f32)],
        compiler_params=_params(1),
    )(qn, kn, vc, gbeta)


def _gdn_bwd(qn, kn, vc, gbeta, do, s_all, t_all):
    t = qn.shape[0]

    def body(q_ref, k_ref, v_ref, gb_ref, do_ref, s_ref, t_ref, dq_ref, dk_ref, dv_ref, dgb_ref, dstate):
        @pl.when(pl.program_id(0) == 0)
        def _():
            dstate[...] = jnp.zeros_like(dstate)

        r, c, eye = _chunk_consts()
        tril = r >= c
        lane = lax.broadcasted_iota(jnp.int32, (1, 128), 1)
        hs = range(H)

        def each(fn, *lists):
            return [fn(*args) for args in zip(*lists)]

        def rsum(a):
            return jnp.sum(a, axis=1, keepdims=True)

        def before_state(s):
            rows = slice(s * CH, (s + 1) * CH)
            gb = gb_ref[rows, :]
            gall = _hdot(tril.astype(f32), gb)
            p = {"rows": rows}
            p["q"] = q = [q_ref[rows, h * DH:(h + 1) * DH] for h in hs]
            p["k"] = k = [k_ref[rows, h * DH:(h + 1) * DH] for h in hs]
            p["v"] = v = [v_ref[rows, h * DH:(h + 1) * DH] for h in hs]
            p["dout"] = dout = [do_ref[rows, h * DH:(h + 1) * DH] for h in hs]
            p["inv"] = inv = [t_ref[s, h] for h in hs]
            p["st"] = st = [s_ref[s, h] for h in hs]
            p["bcol"] = bcol = [gb[:, H + h:H + h + 1] for h in hs]
            cm = [_chunk_common(q[h], k[h], v[h], gall[:, h:h + 1], bcol[h], r, c, eye) for h in hs]
            for name, i in (("dec", 1), ("glast", 2), ("eg", 3), ("el", 4), ("kb", 5), ("vb", 6), ("low", 8), ("att", 10)):
                p[name] = [m[i] for m in cm]
            p["rcol"] = cm[0][11]
            p["elast"] = each(jnp.exp, p["glast"])
            p["kbg"] = each(jnp.multiply, p["kb"], p["eg"])
            uw = each(lambda i, a, b: _bdot(i, jnp.concatenate([a, b], axis=1)), inv, p["vb"], p["kbg"])
            p["u"] = [a[:, :DH] for a in uw]
            p["w"] = [a[:, DH:] for a in uw]
            p["vn"] = each(lambda a, b, x: a - _bdot(b, x), p["u"], p["w"], st)
            p["qd"] = each(jnp.multiply, q, p["eg"])
            p["kd"] = each(jnp.multiply, k, p["el"])
            p["dqd"] = each(_bdot_nt, dout, st)
            p["datt"] = each(lambda d, x: jnp.where(tril, _bdot_nt(d, x), 0.0), dout, p["vn"])
            p["dqk"] = each(jnp.multiply, p["datt"], p["dec"])
            p["qd_do"] = each(_bdot_tn, p["qd"], dout)
            p["att_do"] = each(_bdot_tn, p["att"], dout)
            return p

        def after_state(p, ds):
            q, k, v, st, inv, bcol = p["q"], p["k"], p["v"], p["st"], p["inv"], p["bcol"]
            eg, el, kb, u, w = p["eg"], p["el"], p["kb"], p["u"], p["w"]
            dvn = each(lambda a, kk, x: a + _bdot(kk, x), p["att_do"], p["kd"], ds)
            dkd = each(_bdot_nt, p["vn"], ds)
            dw = each(lambda a, x: -_bdot_nt(a, x), dvn, st)
            new_ds = each(lambda x, e, a, ww, dv_: x * e + a - _bdot_tn(ww, dv_), ds, p["elast"], p["qd_do"], w, dvn)
            dglast = each(lambda e, x, d: e * jnp.sum(rsum(x.astype(f32) * d), axis=0, keepdims=True), p["elast"], st, ds)
            dr = each(lambda i, a, b: _bdot_tn(i, jnp.concatenate([a, b], axis=1)), inv, dvn, dw)
            dvb = [a[:, :DH] for a in dr]
            dkbg = [a[:, DH:] for a in dr]
            dlow = each(lambda a, b, x, y: -jnp.where(r > c, _bdot_nt(a, b) + _bdot_nt(x, y), 0.0), dvb, u, dkbg, w)
            dkk = each(jnp.multiply, dlow, p["dec"])
            mm = each(lambda a, b, x, y: a * b + x * y, dlow, p["low"], p["datt"], p["att"])
            dkb = each(lambda a, kk, b, e: _bdot(a, kk) + b * e, dkk, k, dkbg, eg)
            dk = each(lambda a, b, x, y, d, e, f, g: _bdot_tn(a, b) + _bdot_tn(x, y) + d * e + f * g,
                      dkk, kb, p["dqk"], q, dkd, el, dkb, bcol)
            dq = each(lambda a, kk, d, e: _bdot(a, kk) + d * e, p["dqk"], k, p["dqd"], eg)
            dv = each(jnp.multiply, dvb, bcol)
            dbeta = each(lambda a, b, x, y: rsum(a * b) + rsum(x * y), dkb, k, dvb, v)
            deg = each(lambda a, b, x, y: rsum(a * b) + rsum(x * y), dkbg, kb, p["dqd"], q)
            delc = each(lambda a, b, e: rsum(a * b) * e, dkd, k, el)
            dgc = each(lambda m, a, e, d: rsum(m) - rsum(eye * jnp.sum(m, axis=0, keepdims=True)) + a * e - d,
                       mm, deg, eg, delc)
            dgc = each(lambda g, d, l: g + jnp.where(p["rcol"] == CH - 1, jnp.sum(d, axis=0, keepdims=True) + l, 0.0),
                       dgc, delc, dglast)
            dg_acc = jnp.zeros((CH, 128), f32)
            db_acc = jnp.zeros((CH, 128), f32)
            rows = p["rows"]
            for h in hs:
                dq_ref[rows, h * DH:(h + 1) * DH] = dq[h]
                dk_ref[rows, h * DH:(h + 1) * DH] = dk[h]
                dv_ref[rows, h * DH:(h + 1) * DH] = dv[h]
                dg_acc = dg_acc + dgc[h] * (lane == h).astype(f32)
                db_acc = db_acc + dbeta[h] * (lane == H + h).astype(f32)
            dgb_ref[rows, :] = _hdot((r <= c).astype(f32), dg_acc) + db_acc
            return new_ds

        order = list(reversed(range(GDN_STEP)))
        pre = [before_state(s) for s in order]
        ds = [dstate[h] for h in hs]
        for p in pre:
            ds = after_state(p, ds)
        for h in hs:
            dstate[h] = ds[h]

    tb = GDN_STEP * CH
    n_steps = t // tb
    rev = lambda i: (n_steps - 1 - i, 0)
    rev4 = lambda i: (n_steps - 1 - i, 0, 0, 0)
    return pl.pallas_call(
        body, name="gdn_bwd", grid=(n_steps,),
        in_specs=[pl.BlockSpec((tb, D), rev), pl.BlockSpec((tb, D), rev), pl.BlockSpec((tb, D), rev),
                  pl.BlockSpec((tb, 128), rev), pl.BlockSpec((tb, D), rev),
                  pl.BlockSpec((GDN_STEP, H, DH, DH), rev4), pl.BlockSpec((GDN_STEP, H, CH, CH), rev4)],
        out_specs=[pl.BlockSpec((tb, D), rev), pl.BlockSpec((tb, D), rev), pl.BlockSpec((tb, D), rev),
                   pl.BlockSpec((tb, 128), rev)],
        out_shape=[jax.ShapeDtypeStruct((t, D), f32)] * 3 + [jax.ShapeDtypeStruct((t, 128), f32)],
        scratch_shapes=[pltpu.VMEM((H, DH, DH), f32)],
        compiler_params=_params(1),
    )(qn, kn, vc, gbeta, do, s_all, t_all)


def _pad_rows(w, rows=8):
    return jnp.pad(w, ((0, rows - w.shape[0]), (0, 0)))


_REST = ("w_up", "w_a_out", "w_b_out", "w_o", "w_down")


def _local_step(x, tgt, w, comm=None):
    g1 = w["norm_mix_g"].reshape(1, D)
    if comm is None:
        h1 = _rms_fwd(x, g1, name="rms1_fwd")
    else:
        h1, gathered = _rms_fwd(x, g1, name="rms1_fwd", exchange=comm.gather_first())
        w = {**w, **comm.finish_first(gathered)}
    w1, w2 = w["w1"], w["w2"]
    wa = _pad_rows(w["conv_a_w"])
    wg = _pad_rows(w["gdn_conv_w"])
    wf = _pad_rows(w["ffn_conv_w"])
    alog = jnp.pad(w["gdn_A_log"].reshape(1, H), ((0, 0), (0, 128 - H)))
    dtb = jnp.pad(w["gdn_dt_bias"].reshape(1, H), ((0, 0), (0, 128 - H)))
    g2 = w["norm_ffn_g"].reshape(1, D)
    g3 = w["norm_final_g"].reshape(1, D)
    gn = w["gdn_norm_g"].reshape(1, DH)

    if comm is None:
        pg = _matmul(h1, w1, name="mm_in", cols=(0, 6 * D), out_dtype=bf16)
        pq = _matmul(h1, w1, name="mm_in_qkv", cols=(6 * D, 3 * D))
    else:
        pg, gathered = _matmul(h1, w1, name="mm_in", cols=(0, 6 * D), out_dtype=bf16, exchange=comm.gather_rest())
        pq, gathered = _matmul(h1, w1, name="mm_in_qkv", cols=(6 * D, 3 * D), exchange=_gather_forward_exchange(gathered))
        w = {**w, **comm.finish_gather(gathered)}
    ya_in, qn, kn, vc, gbeta, p2 = _pre_fwd(pg, pq, h1, w2, wa, wg, alog, dtb)
    o, s_all, t_all = _gdn_fwd(qn, kn, vc, gbeta)
    yb_in = _post_fwd(o, pg, gn)
    ya = _matmul(ya_in, w["w_a_out"], name="mm_a", out_dtype=bf16)
    yb = _matmul(yb_in, w["w_b_out"], name="mm_b", out_dtype=bf16)
    mix = _mix_fwd(ya, yb, pg)
    x2 = _matmul(mix, w["w_o"], name="mm_o", add=x)
    h2 = _rms_fwd(x2, g2, name="rms2_fwd")
    up = _matmul(h2, w["w_up"], nt=True, name="mm_up", tn=DFF // 2, out_dtype=bf16)
    act = _ffn_fwd(up, wf)
    x3 = _matmul(act, w["w_down"], name="mm_down", add=x2, tm=512)
    loss_p, dx3, dx3b, dg3 = _final(x3, tgt, g3)

    grads = {"norm_final_g": dg3}
    dact = _matmul(dx3b, w["w_down"], nt=True, name="mm_down_dx", tm=512, tn=DFF, out_dtype=bf16)
    grads["w_down"] = _matmul_tn(act, dx3b, name="mm_down_dw", tm=DFF // 2)
    dc, dwf = _ffn_bwd1(dact, up, wf)
    grads["ffn_conv_w"] = dwf
    dup = _ffn_bwd2(dc, wf)
    dh2 = _matmul(dup, w["w_up"], name="mm_up_dx", tk=DFF)
    grads["w_up"] = _matmul_tn(dup, h2, name="mm_up_dw", tm=DFF // 2)
    dx2, dx2b, dg2 = _rms_bwd(dh2, x2, g2, dx3, name="rms2_bwd")
    grads["norm_ffn_g"] = dg2
    dmix = _matmul(dx2b, w["w_o"], nt=True, name="mm_o_dx", out_dtype=bf16)
    grads["w_o"] = _matmul_tn(mix, dx2b, name="mm_o_dw")
    dya, dyb, dgates = _mix_bwd(dmix, ya, yb, pg)
    dya_in = _matmul(dya, w["w_a_out"], nt=True, name="mm_a_dx", out_dtype=bf16)
    grads["w_a_out"] = _matmul_tn(ya_in, dya, name="mm_a_dw")
    dyb_in = _matmul(dyb, w["w_b_out"], nt=True, name="mm_b_dx")
    grads["w_b_out"] = _matmul_tn(yb_in, dyb, name="mm_b_dw")
    do, dz, dgn = _post_bwd(dyb_in, o, pg, gn)
    grads["gdn_norm_g"] = dgn
    dqn, dkn, dvc, dgb = _gdn_bwd(qn, kn, vc, gbeta, do, s_all, t_all)
    dbg, dca, dc4, dp2, dwa, dwg, dal, ddt, grads["w2"] = _pre_bwd1(pg, pq, p2, dya_in, dqn, dkn, dvc, dgb, gbeta, h1,
                                                                    wa, wg, alog, dtb)
    grads["conv_a_w"] = dwa
    grads["gdn_conv_w"] = dwg
    grads["gdn_A_log"] = dal
    grads["gdn_dt_bias"] = ddt
    if comm is None:
        dp1 = _pre_bwd2(dca, dc4, pg, dbg, dz, dgates, wa, wg)
        grads["w1"] = _matmul_tn(dp1, h1, name="mm_in_dw", tt=4096)
        dh1 = _matmul(dp1, w1, nt=True, name="mm_in_dx", tm=512, tk=NW1 // 2)
    else:
        exchange, blocks = comm.reduce_halves(_REST, grads)
        dp1, recv = _pre_bwd2(dca, dc4, pg, dbg, dz, dgates, wa, wg, exchange=exchange)
        exchange, sums = comm.reduce_sums(_REST, blocks, recv)
        grads["w1"], recv = _matmul_tn(dp1, h1, name="mm_in_dw", tt=4096, exchange=exchange)
        comm.finish_reduce(_REST, sums, recv)
        exchange, blocks = comm.reduce_halves(("w_in",), grads)
        exchange, sums = comm.reduce_sums(("w_in",), blocks, _run_exchange(exchange, name="rs_sibling_w_in"))
        dh1, recv = _matmul(dp1, w1, nt=True, name="mm_in_dx", tm=512, tk=NW1 // 2, exchange=exchange)
        comm.finish_reduce(("w_in",), sums, recv)
    dx, _, dg1 = _rms_bwd(dh1, x, g1, dx2, name="rms1_bwd", more=(dp2, w2))
    grads["norm_mix_g"] = dg1
    return loss_p, dx, grads


_ANY = pl.BlockSpec(memory_space=pl.ANY)


def _remote(src, dst, send_sem, recv_sem, to):
    return pltpu.make_async_remote_copy(src_ref=src, dst_ref=dst, send_sem=send_sem, recv_sem=recv_sem,
                                        device_id=to, device_id_type=MESH)


def _run_exchange(exchange, *, name):
    arrays, shapes, sems, start, wait = exchange
    n_in, n_out = len(arrays), len(shapes)

    def body(*refs):
        start(refs[:n_in], refs[n_in:n_in + n_out], refs[n_in + n_out:])
        wait(refs[:n_in], refs[n_in:n_in + n_out], refs[n_in + n_out:])

    return pl.pallas_call(body, name=name, out_shape=list(shapes), in_specs=[_ANY] * n_in, out_specs=[_ANY] * n_out,
                          scratch_shapes=list(sems))(*arrays)


def _gather_exchange(shards):
    n = len(shards)

    def copies(x_refs, out_refs, sems):
        send_sems, recv_sems, local_sems = sems
        x, y, c = lax.axis_index("x"), lax.axis_index("y"), lax.axis_index("c")

        def flip(v, b):
            return v + b - 2 * v * b

        me, sibling = (x, y, c), (x, y, 1 - c)
        chip1, chip2, diag = (flip(x, 1 - c), flip(y, c)), (flip(x, c), flip(y, 1 - c)), (1 - x, 1 - y)

        def copy(a, k, blk, to, from_input=False):
            dst = out_refs[a].at[4 * blk[0] + 2 * blk[1] + blk[2]]
            return _remote(x_refs[a] if from_input else dst, dst, send_sems.at[a, k], recv_sems.at[a, k], to)

        mine = [pltpu.make_async_copy(x_refs[a], out_refs[a].at[4 * x + 2 * y + c], local_sems.at[a]) for a in range(n)]
        first = []
        for a in range(n):
            first += [copy(a, 0, me, sibling, from_input=True), copy(a, 1, me, (*chip1, c), from_input=True),
                      copy(a, 2, me, (*chip2, c), from_input=True)]
        return copy, mine, first, me, sibling, chip1, chip2, diag, c

    def start(x_refs, out_refs, sems):
        _, mine, first, *_ = copies(x_refs, out_refs, sems)
        for cp in mine + first:
            cp.start()

    def wait(x_refs, out_refs, sems):
        copy, mine, first, me, sibling, chip1, chip2, diag, c = copies(x_refs, out_refs, sems)
        passed = []

        def pass_on(cp):
            passed.append(cp)
            cp.start()

        for a in range(n):
            copy(a, 1, (*chip1, c), me).wait_recv()
            pass_on(copy(a, 3, (*chip1, c), (*chip2, c)))
            pass_on(copy(a, 4, (*chip1, c), sibling))
        for a in range(n):
            copy(a, 2, (*chip2, c), me).wait_recv()
            pass_on(copy(a, 5, (*chip2, c), sibling))
        for a in range(n):
            copy(a, 3, (*diag, c), me).wait_recv()
            pass_on(copy(a, 6, (*diag, c), sibling))
        for a in range(n):
            copy(a, 0, sibling, me).wait_recv()
            copy(a, 4, (*chip2, 1 - c), me).wait_recv()
            copy(a, 5, (*chip1, 1 - c), me).wait_recv()
            copy(a, 6, (*diag, 1 - c), me).wait_recv()
        for cp in first + passed:
            cp.wait_send()
        for cp in mine:
            cp.wait()

    shapes = [jax.ShapeDtypeStruct((N_DEV, *s.shape), s.dtype) for s in shards]
    sems = [pltpu.SemaphoreType.DMA((n, 7)), pltpu.SemaphoreType.DMA((n, 7)), pltpu.SemaphoreType.DMA((n,))]
    return shards, shapes, sems, start, wait


def _gather_direct_exchange(shards):
    n = len(shards)

    def copies(x_refs, out_refs, sems):
        send_sems, recv_sems, local_sems = sems
        x, y, c = lax.axis_index("x"), lax.axis_index("y"), lax.axis_index("c")
        targets = [(x, y, 1 - c), (1 - x, y, c), (x, 1 - y, c), (1 - x, 1 - y, c)]
        local, sends, recvs = [], [], []
        for a in range(n):
            mine = out_refs[a].at[4 * x + 2 * y + c]
            local.append(pltpu.make_async_copy(x_refs[a], mine, local_sems.at[a]))
            for k, to in enumerate(targets):
                theirs = out_refs[a].at[4 * to[0] + 2 * to[1] + to[2]]
                sends.append(_remote(x_refs[a], mine, send_sems.at[a, k], recv_sems.at[a, k], to))
                recvs.append(_remote(theirs, theirs, send_sems.at[a, k], recv_sems.at[a, k], to))
        return local, sends, recvs

    def start(x_refs, out_refs, sems):
        local, sends, _ = copies(x_refs, out_refs, sems)
        for cp in local + sends:
            cp.start()

    def wait(x_refs, out_refs, sems):
        local, sends, recvs = copies(x_refs, out_refs, sems)
        for cp in recvs:
            cp.wait_recv()
        for cp in sends:
            cp.wait_send()
        for cp in local:
            cp.wait()

    shapes = [jax.ShapeDtypeStruct((N_DEV, *s.shape), s.dtype) for s in shards]
    sems = [pltpu.SemaphoreType.DMA((n, 4)), pltpu.SemaphoreType.DMA((n, 4)), pltpu.SemaphoreType.DMA((n,))]
    return shards, shapes, sems, start, wait


def _gather_forward_exchange(gathered):
    n = len(gathered)

    def copies(_, out_refs, sems):
        send_sems, recv_sems = sems
        x, y, c = lax.axis_index("x"), lax.axis_index("y"), lax.axis_index("c")
        sibling = (x, y, 1 - c)
        sends, recvs = [], []
        for a in range(n):
            for j, (px, py) in enumerate([(1 - x, y), (x, 1 - y), (1 - x, 1 - y)]):
                mine = out_refs[a].at[4 * px + 2 * py + c]
                theirs = out_refs[a].at[4 * px + 2 * py + 1 - c]
                sends.append(_remote(mine, mine, send_sems.at[a, j], recv_sems.at[a, j], sibling))
                recvs.append(_remote(theirs, theirs, send_sems.at[a, j], recv_sems.at[a, j], sibling))
        return sends, recvs

    def start(in_refs, out_refs, sems):
        for cp in copies(in_refs, out_refs, sems)[0]:
            cp.start()

    def wait(in_refs, out_refs, sems):
        sends, recvs = copies(in_refs, out_refs, sems)
        for cp in recvs:
            cp.wait_recv()
        for cp in sends:
            cp.wait_send()

    shapes = [jax.ShapeDtypeStruct(g.shape, g.dtype) for g in gathered]
    sems = [pltpu.SemaphoreType.DMA((n, 3)), pltpu.SemaphoreType.DMA((n, 3))]
    return gathered, shapes, sems, start, wait, True


def _chips_exchange(hsums):
    n = len(hsums)

    def copies(h_refs, out_refs, sems):
        send_sems, recv_sems = sems
        x, y, c = lax.axis_index("x"), lax.axis_index("y"), lax.axis_index("c")
        chips = [(1 - x, y), (x, 1 - y), (1 - x, 1 - y)]
        return [_remote(h_refs[a].at[2 * px + py], out_refs[a].at[k], send_sems.at[a, k], recv_sems.at[a, k], (px, py, c))
                for a in range(n) for k, (px, py) in enumerate(chips)]

    def start(h_refs, out_refs, sems):
        for cp in copies(h_refs, out_refs, sems):
            cp.start()

    def wait(h_refs, out_refs, sems):
        for cp in copies(h_refs, out_refs, sems):
            cp.wait()

    shapes = [jax.ShapeDtypeStruct((3, *h.shape[1:]), h.dtype) for h in hsums]
    sems = [pltpu.SemaphoreType.DMA((n, 3)), pltpu.SemaphoreType.DMA((n, 3))]
    return hsums, shapes, sems, start, wait


def _sibling_exchange(halves):
    n = len(halves)

    def copies(p_refs, out_refs, sems):
        send_sems, recv_sems = sems
        x, y, c = lax.axis_index("x"), lax.axis_index("y"), lax.axis_index("c")
        return [_remote(p_refs[a], out_refs[a], send_sems.at[a], recv_sems.at[a], (x, y, 1 - c)) for a in range(n)]

    def start(p_refs, out_refs, sems):
        for cp in copies(p_refs, out_refs, sems):
            cp.start()

    def wait(p_refs, out_refs, sems):
        for cp in copies(p_refs, out_refs, sems):
            cp.wait()

    shapes = [jax.ShapeDtypeStruct(h.shape, h.dtype) for h in halves]
    return halves, shapes, [pltpu.SemaphoreType.DMA((n,)), pltpu.SemaphoreType.DMA((n,))], start, wait


_IN_RANGES = ((0, 3 * D, 0, 0), (3 * D, 6 * D, 0, 6 * D), (6 * D, 7 * D, 0, 3 * D), (7 * D, 7 * D + 16, 1, 0),
              (7 * D + 16, 9 * D + 16, 0, 4 * D))


def _col_pieces(width, ranges):
    pieces = []
    for d in range(N_DEV):
        lo, hi = d * width, (d + 1) * width
        for glo, ghi, mat, mlo in ranges:
            a, b = max(lo, glo), min(hi, ghi)
            if a < b:
                pieces.append((d, a - lo, b - lo, mat, mlo + a - glo))
    return pieces


def _cols_to_matrices(g, ranges, out_widths, *, name):
    _, rows, width = g.shape
    tb = 128
    pieces = _col_pieces(width, ranges)
    covered = [sum(p[2] - p[1] for p in pieces if p[3] == m) for m in range(len(out_widths))]

    def body(g_ref, *o_refs):
        for m, o_ref in enumerate(o_refs):
            if covered[m] < out_widths[m]:
                o_ref[...] = jnp.zeros_like(o_ref)
        for d, b0, b1, m, m0 in pieces:
            o_refs[m][:, m0:m0 + b1 - b0] = g_ref[d, :, b0:b1]

    return pl.pallas_call(
        body, name=name, grid=(rows // tb,), in_specs=[pl.BlockSpec((N_DEV, tb, width), lambda i: (0, i, 0))],
        out_specs=[pl.BlockSpec((tb, wo), lambda i: (i, 0)) for wo in out_widths],
        out_shape=[jax.ShapeDtypeStruct((rows, wo), g.dtype) for wo in out_widths], compiler_params=_params(1),
    )(g)


def _transposed_matrices_to_blocks(mats, ranges, width, *, name):
    rows = mats[0].shape[1]
    pieces = _col_pieces(width, ranges)

    def body(*refs):
        m_refs, g_ref = refs[:-1], refs[-1]
        for d, b0, b1, m, m0 in pieces:
            g_ref[d, b0:b1, :] = m_refs[m][m0:m0 + b1 - b0, :]

    return pl.pallas_call(
        body, name=name, grid=(rows // 128,),
        in_specs=[pl.BlockSpec((mt.shape[0], 128), lambda i: (0, i)) for mt in mats],
        out_specs=pl.BlockSpec((N_DEV, width, 128), lambda i: (0, 0, i)),
        out_shape=jax.ShapeDtypeStruct((N_DEV, width, rows), mats[0].dtype), compiler_params=_params(1),
    )(*mats)


def _row_block(rows):
    return 128 if rows % 128 == 0 else rows


def _half_bf16(g4, c_other, *, name):
    _, _, rows, width = g4.shape
    tb = _row_block(rows)

    def body(c_ref, p_ref, o_ref):
        o_ref[0] = p_ref[0, 0].astype(bf16)

    grid_spec = pltpu.PrefetchScalarGridSpec(
        num_scalar_prefetch=1, grid=(4, rows // tb),
        in_specs=[pl.BlockSpec((1, 1, tb, width), lambda j, i, c_ref: (j, c_ref[0], i, 0))],
        out_specs=pl.BlockSpec((1, tb, width), lambda j, i, c_ref: (j, i, 0)))
    return pl.pallas_call(
        body, name=name, grid_spec=grid_spec, out_shape=jax.ShapeDtypeStruct((4, rows, width), bf16),
        compiler_params=_params(2, _vmem_for(4 * tb * width, 2 * tb * width)),
    )(c_other, g4)


def _pair_sum(g4, recv, c_me, *, name):
    _, _, rows, width = g4.shape
    tb = _row_block(rows)

    def body(c_ref, p_ref, r_ref, o_ref, ob_ref):
        s = p_ref[0, 0] + r_ref[0].astype(f32)
        o_ref[0] = s
        ob_ref[0] = s.astype(bf16)

    blk = pl.BlockSpec((1, tb, width), lambda j, i, c_ref: (j, i, 0))
    grid_spec = pltpu.PrefetchScalarGridSpec(
        num_scalar_prefetch=1, grid=(4, rows // tb),
        in_specs=[pl.BlockSpec((1, 1, tb, width), lambda j, i, c_ref: (j, c_ref[0], i, 0)), blk],
        out_specs=[blk, blk])
    return pl.pallas_call(
        body, name=name, grid_spec=grid_spec,
        out_shape=[jax.ShapeDtypeStruct((4, rows, width), f32), jax.ShapeDtypeStruct((4, rows, width), bf16)],
        compiler_params=_params(2, _vmem_for(4 * tb * width, 2 * tb * width, 4 * tb * width, 2 * tb * width)),
    )(c_me, g4, recv)


def _adam_shard(hsum, recv, chip, w, m, v, *, name):
    _, rows, width = w.shape
    tb = _row_block(rows)

    def body(j_ref, h_ref, r_ref, w_ref, m_ref, v_ref, g_out, d_out, m_out, v_out):
        g = ((h_ref[0] + r_ref[0].astype(f32)) + r_ref[1].astype(f32)) + r_ref[2].astype(f32)
        delta, mn, vn = _adam_math(w_ref[0], g, m_ref[0], v_ref[0])
        g_out[0] = g
        d_out[0] = delta
        m_out[0] = mn
        v_out[0] = vn

    blk = pl.BlockSpec((1, tb, width), lambda i, j_ref: (0, i, 0))
    grid_spec = pltpu.PrefetchScalarGridSpec(
        num_scalar_prefetch=1, grid=(rows // tb,),
        in_specs=[pl.BlockSpec((1, tb, width), lambda i, j_ref: (j_ref[0], i, 0)),
                  pl.BlockSpec((3, tb, width), lambda i, j_ref: (0, i, 0)), blk, blk, blk],
        out_specs=[blk, blk, blk, blk])
    return pl.pallas_call(
        body, name=name, grid_spec=grid_spec, out_shape=[jax.ShapeDtypeStruct(w.shape, f32)] * 4,
        compiler_params=_params(1, _vmem_for(*[4 * tb * width] * 8, 6 * tb * width)),
    )(chip, hsum, recv, w, m, v)


def _sum_shard(hsum, recv, chip, *, name):
    _, rows, width = hsum.shape
    tb = _row_block(rows)

    def body(j_ref, h_ref, r_ref, g_out):
        g_out[...] = ((h_ref[0] + r_ref[0].astype(f32)) + r_ref[1].astype(f32)) + r_ref[2].astype(f32)

    grid_spec = pltpu.PrefetchScalarGridSpec(
        num_scalar_prefetch=1, grid=(rows // tb,),
        in_specs=[pl.BlockSpec((1, tb, width), lambda i, j_ref: (j_ref[0], i, 0)),
                  pl.BlockSpec((3, tb, width), lambda i, j_ref: (0, i, 0))],
        out_specs=pl.BlockSpec((tb, width), lambda i, j_ref: (i, 0)))
    return pl.pallas_call(body, name=name, grid_spec=grid_spec, out_shape=jax.ShapeDtypeStruct((rows, width), f32),
                          compiler_params=_params(1, _vmem_for(*[4 * tb * width] * 2, 6 * tb * width)))(chip, hsum, recv)


def _adam_columns(g, w, m, v, *, name):
    cols, _, rows = w.shape
    tb = cols // 2

    def body(g_ref, w_ref, m_ref, v_ref, d_out, m_out, v_out):
        delta, mn, vn = _adam_math(w_ref[...], g_ref[...], m_ref[...], v_ref[...])
        d_out[...] = delta
        m_out[...] = mn
        v_out[...] = vn

    blk = pl.BlockSpec((tb, 1, rows), lambda i: (i, 0, 0))
    return pl.pallas_call(
        body, name=name, grid=(cols // tb,), in_specs=[blk] * 4, out_specs=[blk] * 3,
        out_shape=[jax.ShapeDtypeStruct(w.shape, f32)] * 3,
        compiler_params=_params(1, _vmem_for(*[4 * tb * rows] * 7)),
    )(g, w, m, v)


R_SMALL = 8 + 8 * N_DEV
_SMALL_LANES = {"gdn_norm_g": (0, DH), "gdn_A_log": (DH, DH + H), "gdn_dt_bias": (2 * DH, 2 * DH + H)}
_LOSS_LANE = 3 * DH


def _pack_small(dg1, dg2, dg3, dgn, dal, ddt, loss_p, dwa, dwg, dwf):
    def body(dg1_ref, dg2_ref, dg3_ref, dgn_ref, dal_ref, ddt_ref, loss_ref, dwa_ref, dwg_ref, dwf_ref, o_ref):
        def total(ref):
            return jnp.sum(ref[...], axis=0, keepdims=True)

        o_ref[...] = jnp.zeros_like(o_ref)
        o_ref[0:1, :] = total(dg1_ref)
        o_ref[1:2, :] = total(dg2_ref)
        o_ref[2:3, :] = total(dg3_ref)
        o_ref[3:4, 0:DH] = total(dgn_ref)
        o_ref[3:4, DH:2 * DH] = total(dal_ref)
        o_ref[3:4, 2 * DH:3 * DH] = total(ddt_ref)
        o_ref[3:4, 3 * DH:4 * DH] = total(loss_ref)
        for d in range(N_DEV):
            base = 8 + 8 * d
            o_ref[base:base + 3, 0:128] = dwa_ref[0:3, 128 * d:128 * (d + 1)]
            o_ref[base:base + 4, 128:512] = dwg_ref[0:4, 384 * d:384 * (d + 1)]
            o_ref[base + 4:base + 7, 0:704] = dwf_ref[0:3, 704 * d:704 * (d + 1)]

    return pl.pallas_call(body, name="pack_small", out_shape=jax.ShapeDtypeStruct((R_SMALL, D), f32))(
        dg1, dg2, dg3, dgn, dal, ddt, loss_p, dwa, dwg, dwf)


_SMALL = ("norm_mix_g", "norm_ffn_g", "norm_final_g", "gdn_norm_g", "gdn_A_log", "gdn_dt_bias",
          "conv_a_w", "gdn_conv_w", "ffn_conv_w")


def _adam_small(gath, me, w, m, v):
    arrays = [t[n] for n in _SMALL for t in (w, m, v)]

    def body(me_ref, ga_ref, gb_ref, *refs):
        ins, outs = refs[:len(arrays)], refs[len(arrays):]
        ga, gb = ga_ref[0], gb_ref[0]
        for s in range(1, N_DEV):
            ga = ga + ga_ref[s]
            gb = gb + gb_ref[s]
        grads = {"norm_mix_g": ga[0:1, :], "norm_ffn_g": ga[1:2, :], "norm_final_g": ga[2:3, :],
                 "conv_a_w": gb[0:3, 0:128], "gdn_conv_w": gb[0:4, 128:512], "ffn_conv_w": gb[4:7, 0:704]}
        for n, (lo, hi) in _SMALL_LANES.items():
            grads[n] = ga[3:4, lo:hi]
        for i, n in enumerate(_SMALL):
            three_d = len(w[n].shape) == 3
            wv, mv, vv = (r[0] if three_d else r[...] for r in ins[3 * i:3 * i + 3])
            delta, mn, vn = _adam_math(wv, grads[n], mv, vv)
            for o_ref, val in zip(outs[4 * i:4 * i + 4], (grads[n], delta, mn, vn)):
                if three_d:
                    o_ref[0] = val
                else:
                    o_ref[...] = val
        outs[-1][...] = ga[3:4, _LOSS_LANE:_LOSS_LANE + 1]

    def whole(shape):
        return pl.BlockSpec(shape, lambda i, me_ref: (0,) * len(shape))

    grid_spec = pltpu.PrefetchScalarGridSpec(
        num_scalar_prefetch=1, grid=(1,),
        in_specs=[pl.BlockSpec((N_DEV, 8, D), lambda i, me_ref: (0, 0, 0)),
                  pl.BlockSpec((N_DEV, 8, D), lambda i, me_ref: (0, 1 + me_ref[0], 0))] + [whole(a.shape) for a in arrays],
        out_specs=[whole(w[n].shape) for n in _SMALL for _ in range(4)] + [whole((1, 1))])
    res = pl.pallas_call(
        body, name="adam_small", grid_spec=grid_spec,
        out_shape=[jax.ShapeDtypeStruct(w[n].shape, f32) for n in _SMALL for _ in range(4)]
        + [jax.ShapeDtypeStruct((1, 1), f32)],
        compiler_params=_params(1),
    )(me, gath, gath, *arrays)
    return {n: tuple(res[4 * i:4 * i + 4]) for i, n in enumerate(_SMALL)}, res[-1]


def _adam_math(w, g, m, v):
    m = ADAM_B1 * m + (1.0 - ADAM_B1) * g
    v = ADAM_B2 * v + (1.0 - ADAM_B2) * jnp.square(g)
    m_hat = m / (1.0 - ADAM_B1 ** ADAM_STEP)
    v_hat = v / (1.0 - ADAM_B2 ** ADAM_STEP)
    delta = -ADAM_LR * (m_hat / (jnp.sqrt(v_hat) + ADAM_EPS) + ADAM_WD * w)
    return delta, m, v


_WEIGHTS = ("norm_mix_g", "w_in", "conv_a_w", "gdn_conv_w", "gdn_A_log", "gdn_dt_bias", "gdn_norm_g", "w_a_out",
            "w_b_out", "w_o", "norm_ffn_g", "w_up", "ffn_conv_w", "w_down", "norm_final_g")
_CONVS = ("conv_a_w", "gdn_conv_w", "ffn_conv_w")


class _StepExchanges:
    def __init__(self, wts, mom, var, c_me, chip):
        self.wts, self.mom, self.var, self.c_me, self.chip = wts, mom, var, c_me, chip
        self.results = {}

    def gather_first(self):
        return _gather_exchange([self.wts["w_in"][0].astype(bf16)] + [self.wts[n][0] for n in _CONVS])

    def finish_first(self, gathered):
        g_in, gc_a, gc_g, gc_f = gathered
        w1, w2 = _cols_to_matrices(g_in, _IN_RANGES, (NW1, 128), name="relay_w_in")
        return {"w1": w1, "w2": w2, "conv_a_w": gc_a.transpose(1, 0, 2).reshape(3, D),
                "gdn_conv_w": gc_g.transpose(1, 0, 2).reshape(4, 3 * D),
                "ffn_conv_w": gc_f.transpose(1, 0, 2).reshape(3, 2 * DFF)}

    def gather_rest(self):
        return _gather_direct_exchange([self.wts[n][0].astype(bf16) for n in _REST])

    def finish_gather(self, gathered):
        g_up, g_a, g_b, g_o, g_down = gathered
        return {"w_up": g_up.reshape(2 * DFF, D), "w_a_out": g_a.reshape(D, D), "w_b_out": g_b.reshape(D, D),
                "w_o": g_o.reshape(D, D), "w_down": g_down.reshape(DFF, D)}

    def reduce_halves(self, names, grads):
        blocks = []
        for n in names:
            if n == "w_in":
                g = _transposed_matrices_to_blocks([grads["w1"], grads["w2"]], _IN_RANGES, R_IN, name="relay_dw_in")
                blocks.append(g.reshape(4, 2, R_IN, D))
            else:
                blocks.append(grads[n].reshape(4, 2, *self.wts[n].shape[1:]))
        return _sibling_exchange([_half_bf16(g, 1 - self.c_me, name="rs_half_" + n) for n, g in zip(names, blocks)]), blocks

    def reduce_sums(self, names, blocks, recv):
        sums = [_pair_sum(g, r, self.c_me, name="rs_sum_" + n) for n, g, r in zip(names, blocks, recv)]
        return _chips_exchange([s[1] for s in sums]), [s[0] for s in sums]

    def finish_reduce(self, names, sums, recv):
        for n, s, r in zip(names, sums, recv):
            if n == "w_in":
                g = _sum_shard(s, r, self.chip, name="rs_total_w_in")[:, None, :]
                w, m, v = (jnp.transpose(t[n], (2, 0, 1)) for t in (self.wts, self.mom, self.var))
                res = (g, *_adam_columns(g, w, m, v, name="adam_w_in"))
                self.results[n] = tuple(jnp.transpose(a, (1, 2, 0)) for a in res)
            else:
                self.results[n] = _adam_shard(s, r, self.chip, self.wts[n], self.mom[n], self.var[n], name="adam_" + n)


def kernel(x, norm_mix_g, w_in, conv_a_w, gdn_conv_w, gdn_A_log, gdn_dt_bias, gdn_norm_g, w_a_out, w_b_out, w_o, norm_ffn_g, w_up, ffn_conv_w, w_down, norm_final_g, loss_target, m_norm_mix_g, m_w_in, m_conv_a_w, m_gdn_conv_w, m_gdn_A_log, m_gdn_dt_bias, m_gdn_norm_g, m_w_a_out, m_w_b_out, m_w_o, m_norm_ffn_g, m_w_up, m_ffn_conv_w, m_w_down, m_norm_final_g, v_norm_mix_g, v_w_in, v_conv_a_w, v_gdn_conv_w, v_gdn_A_log, v_gdn_dt_bias, v_gdn_norm_g, v_w_a_out, v_w_b_out, v_w_o, v_norm_ffn_g, v_w_up, v_ffn_conv_w, v_w_down, v_norm_final_g):
    wts = dict(zip(_WEIGHTS, (norm_mix_g, w_in, conv_a_w, gdn_conv_w, gdn_A_log, gdn_dt_bias, gdn_norm_g, w_a_out,
                              w_b_out, w_o, norm_ffn_g, w_up, ffn_conv_w, w_down, norm_final_g)))
    mom = dict(zip(_WEIGHTS, (m_norm_mix_g, m_w_in, m_conv_a_w, m_gdn_conv_w, m_gdn_A_log, m_gdn_dt_bias,
                              m_gdn_norm_g, m_w_a_out, m_w_b_out, m_w_o, m_norm_ffn_g, m_w_up, m_ffn_conv_w,
                              m_w_down, m_norm_final_g)))
    var = dict(zip(_WEIGHTS, (v_norm_mix_g, v_w_in, v_conv_a_w, v_gdn_conv_w, v_gdn_A_log, v_gdn_dt_bias,
                              v_gdn_norm_g, v_w_a_out, v_w_b_out, v_w_o, v_norm_ffn_g, v_w_up, v_ffn_conv_w,
                              v_w_down, v_norm_final_g)))
    cx, cy, cc = lax.axis_index("x"), lax.axis_index("y"), lax.axis_index("c")
    c_me = jnp.reshape(cc, (1,)).astype(jnp.int32)
    chip = jnp.reshape(2 * cx + cy, (1,)).astype(jnp.int32)
    me = jnp.reshape(4 * cx + 2 * cy + cc, (1,)).astype(jnp.int32)

    def with_up_transposed(t):
        return {**t, "w_up": jnp.swapaxes(t["w_up"], 1, 2)}

    comm = _StepExchanges(with_up_transposed(wts), with_up_transposed(mom), with_up_transposed(var), c_me, chip)
    replicated = {n: wts[n] for n in ("norm_mix_g", "norm_ffn_g", "norm_final_g", "gdn_norm_g", "gdn_A_log", "gdn_dt_bias")}
    loss_p, dx, grads = _local_step(x[0], loss_target[0], replicated, comm)
    res = comm.results
    res["w_up"] = tuple(jnp.swapaxes(a, 1, 2) for a in res["w_up"])

    small = _pack_small(grads["norm_mix_g"], grads["norm_ffn_g"], grads["norm_final_g"], grads["gdn_norm_g"],
                        grads["gdn_A_log"], grads["gdn_dt_bias"], loss_p, grads["conv_a_w"], grads["gdn_conv_w"],
                        grads["ffn_conv_w"])
    (small_all,) = _run_exchange(_gather_exchange([small]), name="ag_small")

    def raw(t):
        return {n: t[n].reshape(1, D) if n == "norm_final_g" else t[n] for n in _SMALL}

    res_small, loss = _adam_small(small_all, me, raw(wts), raw(mom), raw(var))
    for n in _SMALL:
        res[n] = tuple(a.reshape(wts[n].shape) for a in res_small[n])
    outs = [[res[n][i] for n in _WEIGHTS] for i in range(4)]
    return (loss.reshape(()), dx[None], *outs[0], *outs[1], *outs[2], *outs[3])
```

```python
import jax
import jax.numpy as jnp
from jax import lax
from jax.experimental import pallas as pl
from jax.experimental.pallas import tpu as pltpu

f32 = jnp.float32
bf16 = jnp.bfloat16

D = 1024
H = 8
DH = 128
CH = 64
GDN_STEP = 2
DFF = 2816
NW1 = 9216
EPS = 1e-6
N_DEV = 8

ADAM_LR = 0.001
ADAM_B1 = 0.9
ADAM_B2 = 0.999
ADAM_EPS = 1e-08
ADAM_WD = 0.01
ADAM_STEP = 10

VMEM_LIMIT_BYTES = 48 * 1024 * 1024
VMEM_MAX_BYTES = 56 * 1024 * 1024

R_IN, R_UP = 1154, 704

_HI = lax.Precision.HIGHEST
MESH = pl.DeviceIdType.MESH


def _params(n_grid, vmem_bytes=None):
    return pltpu.CompilerParams(dimension_semantics=("arbitrary",) * n_grid,
                                vmem_limit_bytes=VMEM_LIMIT_BYTES if vmem_bytes is None else vmem_bytes)


def _vmem_for(*block_bytes, extra=0):
    need = 2 * sum(block_bytes) + extra + 4 * 1024 * 1024
    return min(max(need, VMEM_LIMIT_BYTES), VMEM_MAX_BYTES)


def _bdot(a, b):
    return jnp.dot(a.astype(bf16), b.astype(bf16), preferred_element_type=f32)


def _bdot_nt(a, b):
    return lax.dot_general(a.astype(bf16), b.astype(bf16), (((1,), (1,)), ((), ())), preferred_element_type=f32)


def _bdot_tn(a, b):
    return lax.dot_general(a.astype(bf16), b.astype(bf16), (((0,), (0,)), ((), ())), preferred_element_type=f32)


def _hdot(a, b):
    return jnp.dot(a, b, preferred_element_type=f32, precision=_HI)


def _idot(a, b):
    return jnp.dot(a, b, preferred_element_type=f32, precision=lax.Precision.HIGH)


def _sigmoid(x):
    return 1.0 / (1.0 + jnp.exp(-x))


def _softplus(x):
    return jnp.maximum(x, 0.0) + jnp.log(1.0 + jnp.exp(-jnp.abs(x)))


def _shift_down(x, halo, j):
    if j == 0:
        return x
    xr = pltpu.roll(x, j, 0)
    hr = pltpu.roll(halo, j, 0)
    r8 = lax.broadcasted_iota(jnp.int32, hr.shape, 0)
    top = jnp.where(r8 < j, hr, xr[:8])
    return jnp.concatenate([top, xr[8:]], axis=0)


def _shift_up(x, halo, j):
    if j == 0:
        return x
    n = x.shape[0]
    xr = pltpu.roll(x, n - j, 0)
    hr = pltpu.roll(halo, 8 - j, 0)
    r8 = lax.broadcasted_iota(jnp.int32, hr.shape, 0)
    bot = jnp.where(r8 >= 8 - j, hr, xr[n - 8:])
    return jnp.concatenate([xr[:n - 8], bot], axis=0)


def _taps_down(x, halo, k):
    return [_shift_down(x, halo, k - 1 - j) for j in range(k)]


def _strip(i, base=0):
    return slice(base + i * 128, base + (i + 1) * 128)


def _strip_taps(x, halo, first, k):
    return _taps_down(x, jnp.where(first, 0.0, halo), k)


def _strip_conv(w_ref, sl, taps):
    out = w_ref[0:1, sl] * taps[0]
    for j in range(1, len(taps)):
        out = out + w_ref[j:j + 1, sl] * taps[j]
    return out


def _strip_weight_grad(dw_ref, sl, dy, taps):
    for j, tap in enumerate(taps):
        dw_ref[j:j + 1, sl] += jnp.sum(dy * tap, axis=0, keepdims=True)


def _strip_conv_up(dy, halo, last, w_ref, sl, k):
    halo = jnp.where(last, 0.0, halo)
    out = w_ref[k - 1:k, sl] * dy
    for j in range(k - 1):
        out = out + w_ref[j:j + 1, sl] * _shift_up(dy, halo, k - 1 - j)
    return out


def _row(tb, w, col=0):
    return pl.BlockSpec((tb, w), lambda i: (i, col))


def _prev(tb, w, col=0, rows=8):
    return pl.BlockSpec((rows, w), lambda i: (jnp.maximum(i * (tb // rows) - 1, 0), col))


def _next(tb, w, n_rows, col=0, rows=8):
    last = n_rows // rows - 1
    return pl.BlockSpec((rows, w), lambda i: (jnp.minimum((i + 1) * (tb // rows), last), col))


def _f32(ref, sl):
    return ref[:, sl].astype(f32)


def _halo_before(ref, sl):
    h = _f32(ref, sl)
    return h[h.shape[0] - 8:]


def _halo_after(ref, sl):
    return _f32(ref, sl)[:8]


def _fixed(shape):
    return pl.BlockSpec(shape, lambda i: (0,) * len(shape))


def _pick(n, prefs):
    for p in prefs:
        if n % p == 0:
            return p
    return n


def _matmul(a, b, *, name, nt=False, add=None, tm=1024, tn=1024, tk=None, out_dtype=f32, cols=None, exchange=None):
    m, kd = a.shape
    col0, n = cols if cols is not None else (0, b.shape[0] if nt else b.shape[1])
    tm = _pick(m, (tm, 512, 256))
    tn = _pick(n, (tn, 1024, 512, 128))
    tk = kd if tk is None else tk
    nk = kd // tk
    assert nk == 1 or out_dtype == f32
    assert col0 % tn == 0 and not (nt and cols)
    j0 = col0 // tn
    dims = (((1,), (1,)), ((), ())) if nt else (((1,), (0,)), ((), ()))

    def body(a_ref, b_ref, *rest):
        o_ref = rest[-1]
        part = lax.dot_general(a_ref[...], b_ref[...], dims, preferred_element_type=f32)
        if nk == 1:
            o_ref[...] = (part if add is None else part + rest[0][...]).astype(out_dtype)
            return
        k = pl.program_id(2)

        @pl.when(k == 0)
        def _():
            o_ref[...] = part if add is None else part + rest[0][...]

        @pl.when(k > 0)
        def _():
            o_ref[...] += part

    b_spec = pl.BlockSpec((tn, tk), lambda i, j, k: (j, k)) if nt else pl.BlockSpec((tk, tn), lambda i, j, k: (k, j + j0))
    in_specs = [pl.BlockSpec((tm, tk), lambda i, j, k: (i, k)), b_spec]
    args = [a, b]
    if add is not None:
        in_specs.append(pl.BlockSpec((tm, tn), lambda i, j, k: (i, j)))
        args.append(add)
    vmem = _vmem_for(2 * tm * tk, 2 * tk * tn, tm * tn * jnp.dtype(out_dtype).itemsize,
                     4 * tm * tn if add is not None else 0, extra=4 * tm * tn)
    return _call_with_exchange(
        body, exchange, name=name, grid=(m // tm, n // tn, nk), in_specs=in_specs,
        out_specs=pl.BlockSpec((tm, tn), lambda i, j, k: (i, j)),
        out_shape=jax.ShapeDtypeStruct((m, n), out_dtype), args=args, vmem_bytes=vmem)


def _call_with_exchange(body, exchange, *, name, grid, in_specs, out_specs, out_shape, args, vmem_bytes=None):
    if exchange is None:
        return pl.pallas_call(body, name=name, grid=grid, in_specs=in_specs, out_specs=out_specs, out_shape=out_shape,
                              compiler_params=_params(len(grid), vmem_bytes))(*args)
    x_arrays, x_shapes, x_sems, start, wait = exchange[:5]
    n_in, n_xin, n_xout = len(args), len(x_arrays), len(x_shapes)
    aliases = {n_in + i: 1 + i for i in range(n_xin)} if len(exchange) > 5 and exchange[5] else {}

    def full_body(*refs):
        c_in, x_in = refs[:n_in], refs[n_in:n_in + n_xin]
        c_out = refs[n_in + n_xin]
        x_out = refs[n_in + n_xin + 1:n_in + n_xin + 1 + n_xout]
        sems = refs[n_in + n_xin + 1 + n_xout:]
        ids = [pl.program_id(d) for d in range(len(grid))]
        first, last = ids[0] == 0, ids[0] == grid[0] - 1
        for d in range(1, len(grid)):
            first = first & (ids[d] == 0)
            last = last & (ids[d] == grid[d] - 1)

        @pl.when(first)
        def _():
            start(x_in, x_out, sems)

        body(*c_in, c_out)

        @pl.when(last)
        def _():
            wait(x_in, x_out, sems)

    res = pl.pallas_call(
        full_body, name=name, grid=grid, in_specs=list(in_specs) + [_ANY] * n_xin,
        out_specs=[out_specs] + [_ANY] * n_xout, out_shape=[out_shape] + list(x_shapes),
        scratch_shapes=list(x_sems), input_output_aliases=aliases, compiler_params=_params(len(grid), vmem_bytes),
    )(*args, *x_arrays)
    return res[0], list(res[1:])


def _matmul_tn(a, b, *, name, tm=1024, tn=1024, tt=2048, exchange=None):
    t, m = a.shape
    _, n = b.shape
    tm = _pick(m, (tm, 1024, 512, 128))
    tn = _pick(n, (tn, 1024, 512, 128))
    tt = _pick(t, (tt, 2048, 1024, 512, 256))
    nt = t // tt

    def body(a_ref, b_ref, o_ref):
        k = pl.program_id(2)
        part = lax.dot_general(a_ref[...], b_ref[...], (((0,), (0,)), ((), ())), preferred_element_type=f32)

        @pl.when(k == 0)
        def _():
            o_ref[...] = part

        @pl.when(k > 0)
        def _():
            o_ref[...] += part

    return _call_with_exchange(
        body, exchange, name=name, grid=(m // tm, n // tn, nt),
        in_specs=[pl.BlockSpec((tt, tm), lambda i, j, k: (k, i)), pl.BlockSpec((tt, tn), lambda i, j, k: (k, j))],
        out_specs=pl.BlockSpec((tm, tn), lambda i, j, k: (i, j)),
        out_shape=jax.ShapeDtypeStruct((m, n), f32), args=[a, b],
        vmem_bytes=_vmem_for(2 * tt * tm, 2 * tt * tn, 4 * tm * tn, extra=4 * tm * tn + 2 * tt * tm))


def _rms_fwd(x, g, *, name, exchange=None):
    t = x.shape[0]
    tb = _pick(t, (256, 128))

    def body(x_ref, g_ref, h_ref):
        xv = x_ref[...]
        r = lax.rsqrt(jnp.mean(xv * xv, axis=-1, keepdims=True) + EPS)
        h_ref[...] = (xv * r * g_ref[...]).astype(bf16)

    return _call_with_exchange(
        body, exchange, name=name, grid=(t // tb,), in_specs=[_row(tb, D), _fixed((1, D))], out_specs=_row(tb, D),
        out_shape=jax.ShapeDtypeStruct((t, D), bf16), args=[x, g])


def _rms_bwd(dh, x, g, dres, *, name, more=None):
    t = x.shape[0]
    tb = _pick(t, (256, 128))

    def body(dh_ref, x_ref, g_ref, dres_ref, *rest):
        dx_ref, dxb_ref, dg_ref = rest[-3:]
        xv = x_ref[...]
        r = lax.rsqrt(jnp.mean(xv * xv, axis=-1, keepdims=True) + EPS)
        xh = xv * r
        dy = dh_ref[...]
        if more is not None:
            dy = dy + lax.dot_general(rest[0][...], rest[1][...], (((1,), (1,)), ((), ())), preferred_element_type=f32)
        dyg = dy * g_ref[...]
        dx = dres_ref[...] + r * (dyg - xh * jnp.mean(dyg * xh, axis=-1, keepdims=True))
        dx_ref[...] = dx
        dxb_ref[...] = dx.astype(bf16)

        @pl.when(pl.program_id(0) == 0)
        def _():
            dg_ref[...] = jnp.zeros_like(dg_ref)

        dg_ref[...] += jnp.sum((dy * xh).reshape(tb // 8, 8, D), axis=0)

    in_specs, args = [_row(tb, D), _row(tb, D), _fixed((1, D)), _row(tb, D)], [dh, x, g, dres]
    if more is not None:
        in_specs += [_row(tb, 128), _fixed(more[1].shape)]
        args += list(more)
    return pl.pallas_call(
        body, name=name, grid=(t // tb,), in_specs=in_specs,
        out_specs=[_row(tb, D), _row(tb, D), _fixed((8, D))],
        out_shape=[jax.ShapeDtypeStruct((t, D), f32), jax.ShapeDtypeStruct((t, D), bf16),
                   jax.ShapeDtypeStruct((8, D), f32)],
        compiler_params=_params(1),
    )(*args)


def _gdn_gates(ab, alog, dtb):
    lane = lax.broadcasted_iota(jnp.int32, ab.shape, 1)
    g = -jnp.exp(alog) * _softplus(ab + dtb)
    beta = _sigmoid(ab)
    return jnp.where(lane < H, g, jnp.where(lane < 2 * H, beta, 0.0))


def _pre_fwd(pg, pq, h1, w2, wa, wg, alog, dtb):
    t = pg.shape[0]
    tb = 128

    def body(p0_ref, p0h_ref, pq_ref, pqh_ref, h1_ref, w2_ref, wa_ref, wg_ref, alog_ref, dtb_ref,
             ya_ref, qn_ref, kn_ref, vc_ref, gb_ref, p2_ref):
        first = pl.program_id(0) == 0
        p2_ref[...] = jnp.dot(h1_ref[...], w2_ref[...], preferred_element_type=f32)
        for i in range(D // 128):
            sl, cg, xv = _strip(i), _strip(i, D), _strip(i, 2 * D)
            taps = _strip_taps(_f32(p0_ref, cg) * _f32(p0_ref, xv), _halo_before(p0h_ref, cg) * _halo_before(p0h_ref, xv),
                               first, 3)
            ya_ref[:, sl] = (_f32(p0_ref, sl) * _strip_conv(wa_ref, sl, taps)).astype(bf16)
        for part, out_ref, scale in ((0, qn_ref, DH ** -0.5), (1, kn_ref, 1.0), (2, vc_ref, None)):
            for h in range(H):
                sl = _strip(h, part * D)
                s = _strip_conv(wg_ref, sl, _strip_taps(pq_ref[:, sl], pqh_ref[:, sl], first, 4))
                s = s * _sigmoid(s)
                if scale is not None:
                    s = s * (lax.rsqrt(jnp.sum(s * s, axis=-1, keepdims=True) + EPS) * scale)
                out_ref[:, _strip(h)] = s
        gb_ref[...] = _gdn_gates(p2_ref[...], alog_ref[...], dtb_ref[...])

    return pl.pallas_call(
        body, name="pre_fwd", grid=(t // tb,),
        in_specs=[_row(tb, 3 * D, 0), _prev(tb, 3 * D, 0, rows=16), _row(tb, 3 * D), _prev(tb, 3 * D), _row(tb, D),
                  _fixed((D, 128)), _fixed((8, D)), _fixed((8, 3 * D)), _fixed((1, 128)), _fixed((1, 128))],
        out_specs=[_row(tb, D), _row(tb, D), _row(tb, D), _row(tb, D), _row(tb, 128), _row(tb, 128)],
        out_shape=[jax.ShapeDtypeStruct((t, D), bf16), jax.ShapeDtypeStruct((t, D), f32),
                   jax.ShapeDtypeStruct((t, D), f32), jax.ShapeDtypeStruct((t, D), f32),
                   jax.ShapeDtypeStruct((t, 128), f32), jax.ShapeDtypeStruct((t, 128), f32)],
        compiler_params=_params(1),
    )(pg, pg, pq, pq, h1, w2, wa, wg, alog, dtb)


_Z_COL, _GA_COL, _GB_COL = 3, 4, 5


def _post_fwd(o, pg, gn):
    t = o.shape[0]
    tb = _pick(t, (256, 128))

    def body(o_ref, z_ref, gn_ref, yb_ref):
        for h in range(H):
            sl = slice(h * DH, (h + 1) * DH)
            oh = o_ref[:, sl]
            z = _f32(z_ref, sl)
            r = lax.rsqrt(jnp.mean(oh * oh, axis=-1, keepdims=True) + EPS)
            yb_ref[:, sl] = (oh * r * gn_ref[...] * (z * _sigmoid(z))).astype(bf16)

    return pl.pallas_call(
        body, name="post_fwd", grid=(t // tb,), in_specs=[_row(tb, D), _row(tb, D, _Z_COL), _fixed((1, DH))],
        out_specs=_row(tb, D), out_shape=jax.ShapeDtypeStruct((t, D), bf16), compiler_params=_params(1),
    )(o, pg, gn)


def _post_bwd(dyb, o, pg, gn):
    t = o.shape[0]
    tb = _pick(t, (256, 128))

    def body(dyb_ref, o_ref, z_ref, gn_ref, do_ref, dz_ref, dgn_ref):
        @pl.when(pl.program_id(0) == 0)
        def _():
            dgn_ref[...] = jnp.zeros_like(dgn_ref)

        gn_v = gn_ref[...]
        acc = jnp.zeros((8, DH), f32)
        for h in range(H):
            sl = slice(h * DH, (h + 1) * DH)
            oh = o_ref[:, sl]
            z = _f32(z_ref, sl)
            dy = dyb_ref[:, sl]
            r = lax.rsqrt(jnp.mean(oh * oh, axis=-1, keepdims=True) + EPS)
            on = oh * r
            sg = _sigmoid(z)
            sz = z * sg
            don = dy * sz
            dz_ref[:, sl] = (dy * on * gn_v * (sg * (1.0 + z * (1.0 - sg)))).astype(bf16)
            acc = acc + jnp.sum((don * on).reshape(tb // 8, 8, DH), axis=0)
            doh = don * gn_v
            do_ref[:, sl] = r * (doh - on * jnp.mean(doh * on, axis=-1, keepdims=True))
        dgn_ref[...] += acc

    return pl.pallas_call(
        body, name="post_bwd", grid=(t // tb,),
        in_specs=[_row(tb, D), _row(tb, D), _row(tb, D, _Z_COL), _fixed((1, DH))],
        out_specs=[_row(tb, D), _row(tb, D), _fixed((8, DH))],
        out_shape=[jax.ShapeDtypeStruct((t, D), f32), jax.ShapeDtypeStruct((t, D), bf16),
                   jax.ShapeDtypeStruct((8, DH), f32)],
        compiler_params=_params(1),
    )(dyb, o, pg, gn)


def _mix_fwd(ya, yb, pg):
    t = ya.shape[0]
    tb = _pick(t, (256, 128))

    def body(ya_ref, yb_ref, ga_ref, gb_ref, mix_ref):
        ya_v, yb_v = ya_ref[...].astype(f32), yb_ref[...].astype(f32)
        mix = _sigmoid(ga_ref[...].astype(f32)) * ya_v + _sigmoid(gb_ref[...].astype(f32)) * yb_v
        mix_ref[...] = mix.astype(bf16)

    return pl.pallas_call(
        body, name="mix_fwd", grid=(t // tb,),
        in_specs=[_row(tb, D), _row(tb, D), _row(tb, D, _GA_COL), _row(tb, D, _GB_COL)],
        out_specs=_row(tb, D), out_shape=jax.ShapeDtypeStruct((t, D), bf16), compiler_params=_params(1),
    )(ya, yb, pg, pg)


def _mix_bwd(dmix, ya, yb, pg):
    t = ya.shape[0]
    tb = _pick(t, (256, 128))

    def body(dm_ref, ya_ref, yb_ref, ga_ref, gb_ref, dya_ref, dyb_ref, dg_ref):
        dm = dm_ref[...].astype(f32)
        sa = _sigmoid(ga_ref[...].astype(f32))
        sb = _sigmoid(gb_ref[...].astype(f32))
        dya_ref[...] = (dm * sa).astype(bf16)
        dyb_ref[...] = (dm * sb).astype(bf16)
        dg_ref[:, :D] = (dm * ya_ref[...].astype(f32) * sa * (1.0 - sa)).astype(bf16)
        dg_ref[:, D:] = (dm * yb_ref[...].astype(f32) * sb * (1.0 - sb)).astype(bf16)

    return pl.pallas_call(
        body, name="mix_bwd", grid=(t // tb,),
        in_specs=[_row(tb, D), _row(tb, D), _row(tb, D), _row(tb, D, _GA_COL), _row(tb, D, _GB_COL)],
        out_specs=[_row(tb, D), _row(tb, D), _row(tb, 2 * D)],
        out_shape=[jax.ShapeDtypeStruct((t, D), bf16), jax.ShapeDtypeStruct((t, D), bf16),
                   jax.ShapeDtypeStruct((t, 2 * D), bf16)],
        compiler_params=_params(1),
    )(dmix, ya, yb, pg, pg)


def _ffn_fwd(up, wf):
    t = up.shape[0]
    tb = 128

    def body(up_ref, uph_ref, wf_ref, act_ref):
        first = pl.program_id(0) == 0
        for i in range(DFF // 128):
            g, v = _strip(i), _strip(i, DFF)
            gate = _strip_conv(wf_ref, g, _strip_taps(_f32(up_ref, g), _halo_before(uph_ref, g), first, 3))
            val = _strip_conv(wf_ref, v, _strip_taps(_f32(up_ref, v), _halo_before(uph_ref, v), first, 3))
            act_ref[:, g] = (gate * _sigmoid(gate) * val).astype(bf16)

    return pl.pallas_call(
        body, name="ffn_fwd", grid=(t // tb,),
        in_specs=[_row(tb, 2 * DFF), _prev(tb, 2 * DFF, rows=16), _fixed((8, 2 * DFF))],
        out_specs=_row(tb, DFF), out_shape=jax.ShapeDtypeStruct((t, DFF), bf16), compiler_params=_params(1),
    )(up, up, wf)


def _ffn_bwd1(dact, up, wf):
    t = up.shape[0]
    tb = 128

    def body(da_ref, up_ref, uph_ref, wf_ref, dc_ref, dw_ref):
        @pl.when(pl.program_id(0) == 0)
        def _():
            dw_ref[...] = jnp.zeros_like(dw_ref)

        first = pl.program_id(0) == 0
        for i in range(DFF // 128):
            g, v = _strip(i), _strip(i, DFF)
            g_taps = _strip_taps(_f32(up_ref, g), _halo_before(uph_ref, g), first, 3)
            v_taps = _strip_taps(_f32(up_ref, v), _halo_before(uph_ref, v), first, 3)
            gate = _strip_conv(wf_ref, g, g_taps)
            val = _strip_conv(wf_ref, v, v_taps)
            sg = _sigmoid(gate)
            da = _f32(da_ref, g)
            dgate = da * val * (sg * (1.0 + gate * (1.0 - sg)))
            dval = da * (gate * sg)
            dc_ref[:, g] = dgate.astype(bf16)
            dc_ref[:, v] = dval.astype(bf16)
            _strip_weight_grad(dw_ref, g, dgate, g_taps)
            _strip_weight_grad(dw_ref, v, dval, v_taps)

    return pl.pallas_call(
        body, name="ffn_bwd1", grid=(t // tb,),
        in_specs=[_row(tb, DFF), _row(tb, 2 * DFF), _prev(tb, 2 * DFF, rows=16), _fixed((8, 2 * DFF))],
        out_specs=[_row(tb, 2 * DFF), _fixed((8, 2 * DFF))],
        out_shape=[jax.ShapeDtypeStruct((t, 2 * DFF), bf16), jax.ShapeDtypeStruct((8, 2 * DFF), f32)],
        compiler_params=_params(1),
    )(dact, up, up, wf)


def _ffn_bwd2(dc, wf):
    t = dc.shape[0]
    tb = 128
    nb = t // tb

    def body(dc_ref, dch_ref, wf_ref, dup_ref):
        last = pl.program_id(0) == nb - 1
        for i in range(2 * DFF // 128):
            sl = _strip(i)
            dup_ref[:, sl] = _strip_conv_up(_f32(dc_ref, sl), _halo_after(dch_ref, sl), last, wf_ref, sl, 3).astype(bf16)

    return pl.pallas_call(
        body, name="ffn_bwd2", grid=(nb,),
        in_specs=[_row(tb, 2 * DFF), _next(tb, 2 * DFF, t, rows=16), _fixed((8, 2 * DFF))],
        out_specs=_row(tb, 2 * DFF), out_shape=jax.ShapeDtypeStruct((t, 2 * DFF), bf16), compiler_params=_params(1),
    )(dc, dc, wf)


def _final(x3, tgt, g):
    t = x3.shape[0]
    tb = _pick(t, (256, 128))

    def body(x_ref, t_ref, g_ref, loss_ref, dx_ref, dxb_ref, dg_ref):
        @pl.when(pl.program_id(0) == 0)
        def _():
            loss_ref[...] = jnp.zeros_like(loss_ref)
            dg_ref[...] = jnp.zeros_like(dg_ref)

        xv = x_ref[...]
        r = lax.rsqrt(jnp.mean(xv * xv, axis=-1, keepdims=True) + EPS)
        xh = xv * r
        gv = g_ref[...]
        e = xh * gv - t_ref[...]
        lrow = 0.5 * jnp.mean(e * e, axis=-1, keepdims=True)
        loss_ref[...] += jnp.sum(jnp.broadcast_to(lrow, (tb, 128)).reshape(tb // 8, 8, 128), axis=0)
        dy = e * (1.0 / D)
        dyg = dy * gv
        dx = r * (dyg - xh * jnp.mean(dyg * xh, axis=-1, keepdims=True))
        dx_ref[...] = dx
        dxb_ref[...] = dx.astype(bf16)
        dg_ref[...] += jnp.sum((dy * xh).reshape(tb // 8, 8, D), axis=0)

    return pl.pallas_call(
        body, name="final", grid=(t // tb,), in_specs=[_row(tb, D), _row(tb, D), _fixed((1, D))],
        out_specs=[_fixed((8, 128)), _row(tb, D), _row(tb, D), _fixed((8, D))],
        out_shape=[jax.ShapeDtypeStruct((8, 128), f32), jax.ShapeDtypeStruct((t, D), f32),
                   jax.ShapeDtypeStruct((t, D), bf16), jax.ShapeDtypeStruct((8, D), f32)],
        compiler_params=_params(1),
    )(x3, tgt, g)


def _pre_bwd1(pg, pq, p2, dya_in, dqn, dkn, dvc, dgb, gbeta, h1, wa, wg, alog, dtb):
    t = pg.shape[0]
    tb = 128

    def body(p0_ref, p0h_ref, pq_ref, pqh_ref, p2_ref, dya_ref, dqn_ref, dkn_ref, dvc_ref, dgb_ref, gb_ref, h1_ref,
             wa_ref, wg_ref, alog_ref, dtb_ref,
             dbg_ref, dca_ref, dc4_ref, dp2_ref, dwa_ref, dwg_ref, dal_ref, ddt_ref, dw2_ref):
        @pl.when(pl.program_id(0) == 0)
        def _():
            dwa_ref[...] = jnp.zeros_like(dwa_ref)
            dwg_ref[...] = jnp.zeros_like(dwg_ref)
            dal_ref[...] = jnp.zeros_like(dal_ref)
            ddt_ref[...] = jnp.zeros_like(ddt_ref)
            dw2_ref[...] = jnp.zeros_like(dw2_ref)

        first = pl.program_id(0) == 0

        for i in range(D // 128):
            sl, cg, xv = _strip(i), _strip(i, D), _strip(i, 2 * D)
            taps = _strip_taps(_f32(p0_ref, cg) * _f32(p0_ref, xv), _halo_before(p0h_ref, cg) * _halo_before(p0h_ref, xv),
                               first, 3)
            dya = _f32(dya_ref, sl)
            dbg_ref[:, sl] = (dya * _strip_conv(wa_ref, sl, taps)).astype(bf16)
            dca = dya * _f32(p0_ref, sl)
            dca_ref[:, sl] = dca.astype(bf16)
            _strip_weight_grad(dwa_ref, sl, dca, taps)

        for part, d_ref, scale in ((0, dqn_ref, DH ** -0.5), (1, dkn_ref, 1.0), (2, dvc_ref, None)):
            for h in range(H):
                sl = _strip(h, part * D)
                taps = _strip_taps(pq_ref[:, sl], pqh_ref[:, sl], first, 4)
                c4 = _strip_conv(wg_ref, sl, taps)
                sg = _sigmoid(c4)
                dn = d_ref[:, _strip(h)]
                if scale is not None:
                    a = c4 * sg
                    r = lax.rsqrt(jnp.sum(a * a, axis=-1, keepdims=True) + EPS)
                    an = a * r
                    dn = dn * scale
                    dn = r * (dn - an * jnp.sum(dn * an, axis=-1, keepdims=True))
                dc4 = dn * (sg * (1.0 + c4 * (1.0 - sg)))
                dc4_ref[:, sl] = dc4.astype(bf16)
                _strip_weight_grad(dwg_ref, sl, dc4, taps)

        ab = p2_ref[...]
        lane = lax.broadcasted_iota(jnp.int32, ab.shape, 1)
        dgbv = dgb_ref[...]
        gbv = gb_ref[...]
        da = dgbv * (-jnp.exp(alog_ref[...])) * _sigmoid(ab + dtb_ref[...])
        db = dgbv * gbv * (1.0 - gbv)
        dp2 = jnp.where(lane < H, da, jnp.where(lane < 2 * H, db, 0.0)).astype(bf16)
        dp2_ref[...] = dp2
        dw2_ref[...] += lax.dot_general(dp2, h1_ref[...], (((0,), (0,)), ((), ())), preferred_element_type=f32)
        dal = jnp.where(lane < H, dgbv * gbv, 0.0)
        ddt = jnp.where(lane < H, da, 0.0)
        dal_ref[...] += jnp.sum(dal.reshape(tb // 8, 8, 128), axis=0)
        ddt_ref[...] += jnp.sum(ddt.reshape(tb // 8, 8, 128), axis=0)

    return pl.pallas_call(
        body, name="pre_bwd1", grid=(t // tb,),
        in_specs=[_row(tb, 3 * D, 0), _prev(tb, 3 * D, 0, rows=16), _row(tb, 3 * D), _prev(tb, 3 * D), _row(tb, 128),
                  _row(tb, D), _row(tb, D), _row(tb, D), _row(tb, D), _row(tb, 128), _row(tb, 128), _row(tb, D),
                  _fixed((8, D)), _fixed((8, 3 * D)), _fixed((1, 128)), _fixed((1, 128))],
        out_specs=[_row(tb, D), _row(tb, D), _row(tb, 3 * D), _row(tb, 128),
                   _fixed((8, D)), _fixed((8, 3 * D)), _fixed((8, 128)), _fixed((8, 128)), _fixed((128, D))],
        out_shape=[jax.ShapeDtypeStruct((t, D), bf16), jax.ShapeDtypeStruct((t, D), bf16),
                   jax.ShapeDtypeStruct((t, 3 * D), bf16), jax.ShapeDtypeStruct((t, 128), bf16),
                   jax.ShapeDtypeStruct((8, D), f32), jax.ShapeDtypeStruct((8, 3 * D), f32),
                   jax.ShapeDtypeStruct((8, 128), f32), jax.ShapeDtypeStruct((8, 128), f32),
                   jax.ShapeDtypeStruct((128, D), f32)],
        compiler_params=_params(1),
    )(pg, pg, pq, pq, p2, dya_in, dqn, dkn, dvc, dgb, gbeta, h1, wa, wg, alog, dtb)


def _pre_bwd2(dca, dc4, pg, dbg, dz, dgates, wa, wg, exchange=None):
    t = pg.shape[0]
    tb = 128
    nb = t // tb

    def body(dca_ref, dcah_ref, dc4_ref, dc4h_ref, p0_ref, dbg_ref, dz_ref, dgt_ref, wa_ref, wg_ref, dp_ref):
        last = pl.program_id(0) == nb - 1
        dp_ref[:, :D] = dbg_ref[...]
        for i in range(D // 128):
            sl, cg, xv = _strip(i), _strip(i, D), _strip(i, 2 * D)
            du = _strip_conv_up(_f32(dca_ref, sl), _halo_after(dcah_ref, sl), last, wa_ref, sl, 3)
            dp_ref[:, cg] = (du * _f32(p0_ref, xv)).astype(bf16)
            dp_ref[:, xv] = (du * _f32(p0_ref, cg)).astype(bf16)
        dp_ref[:, 3 * D:4 * D] = dz_ref[...]
        dp_ref[:, 4 * D:6 * D] = dgt_ref[...]
        for i in range(3 * D // 128):
            sl = _strip(i)
            dq = _strip_conv_up(_f32(dc4_ref, sl), _halo_after(dc4h_ref, sl), last, wg_ref, sl, 4)
            dp_ref[:, _strip(i, 6 * D)] = dq.astype(bf16)

    return _call_with_exchange(
        body, exchange, name="pre_bwd2", grid=(nb,),
        in_specs=[_row(tb, D), _next(tb, D, t, rows=16), _row(tb, 3 * D), _next(tb, 3 * D, t, rows=16), _row(tb, 3 * D, 0),
                  _row(tb, D), _row(tb, D), _row(tb, 2 * D), _fixed((8, D)), _fixed((8, 3 * D))],
        out_specs=_row(tb, NW1), out_shape=jax.ShapeDtypeStruct((t, NW1), bf16),
        args=[dca, dca, dc4, dc4, pg, dbg, dz, dgates, wa, wg])


def _chunk_consts():
    r = lax.broadcasted_iota(jnp.int32, (CH, CH), 0)
    c = lax.broadcasted_iota(jnp.int32, (CH, CH), 1)
    return r, c, (r == c).astype(f32)


def _tri_inverse(lows, eye, r, c):
    def same_block(b):
        return jnp.bitwise_xor(r, c) < b

    xs = [jnp.where(same_block(8), -low, 0.0) for low in lows]
    ts = [eye + x for x in xs]
    for _ in range(2):
        xs = [_idot(x, x) for x in xs]
        ts = [t + _idot(t, x) for t, x in zip(ts, xs)]
    for b in (8, 16, 32):
        below = same_block(2 * b) & jnp.logical_not(same_block(b))
        ts = [t - _idot(_idot(t, jnp.where(below, low, 0.0)), t) for t, low in zip(ts, lows)]
    return ts


def _chunk_common(q, k, v, gcol, bcol, r, c, eye):
    grow = jnp.sum(eye * gcol, axis=0, keepdims=True)
    dec = jnp.exp(jnp.where(r >= c, gcol - grow, -jnp.inf))
    rcol = lax.broadcasted_iota(jnp.int32, (CH, 1), 0)
    glast = jnp.sum(jnp.where(rcol == CH - 1, gcol, 0.0), axis=0, keepdims=True)
    eg = jnp.exp(gcol)
    el = jnp.exp(glast - gcol)
    kb = k * bcol
    vb = v * bcol
    kk = _bdot_nt(kb, k)
    low = jnp.where(r > c, kk * dec, 0.0)
    qk = _bdot_nt(q, k)
    att = qk * dec
    return grow, dec, glast, eg, el, kb, vb, kk, low, qk, att, rcol


def _gdn_fwd(qn, kn, vc, gbeta):
    t = qn.shape[0]
    n_chunks = t // CH

    def body(q_ref, k_ref, v_ref, gb_ref, o_ref, s_ref, t_ref, state):
        @pl.when(pl.program_id(0) == 0)
        def _():
            state[...] = jnp.zeros_like(state)

        r, c, eye = _chunk_consts()
        tri = (r >= c).astype(f32)
        heads = range(H)
        keys = [(s, h) for s in range(GDN_STEP) for h in heads]
        rows = [slice(s * CH, (s + 1) * CH) for s in range(GDN_STEP)]
        gbs = [gb_ref[rows[s], :] for s in range(GDN_STEP)]
        galls = [_hdot(tri, gb) for gb in gbs]
        qs = {(s, h): q_ref[rows[s], h * DH:(h + 1) * DH] for s, h in keys}
        ks = {(s, h): k_ref[rows[s], h * DH:(h + 1) * DH] for s, h in keys}
        cm = {(s, h): _chunk_common(qs[s, h], ks[s, h], v_ref[rows[s], h * DH:(h + 1) * DH], galls[s][:, h:h + 1],
                                    gbs[s][:, H + h:H + h + 1], r, c, eye) for s, h in keys}
        invs = dict(zip(keys, _tri_inverse([cm[key][8] for key in keys], eye, r, c)))
        uws = {key: _bdot(invs[key], jnp.concatenate([cm[key][6], cm[key][5] * cm[key][3]], axis=1)) for key in keys}
        sts = [state[h] for h in heads]
        for s in range(GDN_STEP):
            vns = [uws[s, h][:, :DH] - _bdot(uws[s, h][:, DH:], sts[h]) for h in heads]
            outs = [_bdot(qs[s, h] * cm[s, h][3], sts[h]) + _bdot(cm[s, h][10], vns[h]) for h in heads]
            news = [sts[h] * jnp.exp(cm[s, h][2]) + _bdot_tn(ks[s, h] * cm[s, h][4], vns[h]) for h in heads]
            for h in heads:
                s_ref[s, h] = sts[h].astype(bf16)
                t_ref[s, h] = invs[s, h]
                o_ref[rows[s], h * DH:(h + 1) * DH] = outs[h]
            sts = news
        for h in heads:
            state[h] = sts[h]

    tb = GDN_STEP * CH
    return pl.pallas_call(
        body, name="gdn_fwd", grid=(t // tb,),
        in_specs=[_row(tb, D), _row(tb, D), _row(tb, D), _row(tb, 128)],
        out_specs=[_row(tb, D), pl.BlockSpec((GDN_STEP, H, DH, DH), lambda i: (i, 0, 0, 0)),
                   pl.BlockSpec((GDN_STEP, H, CH, CH), lambda i: (i, 0, 0, 0))],
        out_shape=[jax.ShapeDtypeStruct((t, D), f32), jax.ShapeDtypeStruct((n_chunks, H, DH, DH), bf16),
                   jax.ShapeDtypeStruct((n_chunks, H, CH, CH), f32)],
        scratch_shapes=[pltpu.VMEM((H, DH, DH), f32)],
        compiler_params=_params(1),
    )(qn, kn, vc, gbeta)


def _gdn_bwd(qn, kn, vc, gbeta, do, s_all, t_all):
    t = qn.shape[0]

    def body(q_ref, k_ref, v_ref, gb_ref, do_ref, s_ref, t_ref, dq_ref, dk_ref, dv_ref, dgb_ref, dstate):
        @pl.when(pl.program_id(0) == 0)
        def _():
            dstate[...] = jnp.zeros_like(dstate)

        r, c, eye = _chunk_consts()
        tril = r >= c
        lane = lax.broadcasted_iota(jnp.int32, (1, 128), 1)
        hs = range(H)

        def each(fn, *lists):
            return [fn(*args) for args in zip(*lists)]

        def rsum(a):
            return jnp.sum(a, axis=1, keepdims=True)

        def before_state(s):
            rows = slice(s * CH, (s + 1) * CH)
            gb = gb_ref[rows, :]
            gall = _hdot(tril.astype(f32), gb)
            p = {"rows": rows}
            p["q"] = q = [q_ref[rows, h * DH:(h + 1) * DH] for h in hs]
            p["k"] = k = [k_ref[rows, h * DH:(h + 1) * DH] for h in hs]
            p["v"] = v = [v_ref[rows, h * DH:(h + 1) * DH] for h in hs]
            p["dout"] = dout = [do_ref[rows, h * DH:(h + 1) * DH] for h in hs]
            p["inv"] = inv = [t_ref[s, h] for h in hs]
            p["st"] = st = [s_ref[s, h] for h in hs]
            p["bcol"] = bcol = [gb[:, H + h:H + h + 1] for h in hs]
            cm = [_chunk_common(q[h], k[h], v[h], gall[:, h:h + 1], bcol[h], r, c, eye) for h in hs]
            for name, i in (("dec", 1), ("glast", 2), ("eg", 3), ("el", 4), ("kb", 5), ("vb", 6), ("low", 8), ("att", 10)):
                p[name] = [m[i] for m in cm]
            p["rcol"] = cm[0][11]
            p["elast"] = each(jnp.exp, p["glast"])
            p["kbg"] = each(jnp.multiply, p["kb"], p["eg"])
            uw = each(lambda i, a, b: _bdot(i, jnp.concatenate([a, b], axis=1)), inv, p["vb"], p["kbg"])
            p["u"] = [a[:, :DH] for a in uw]
            p["w"] = [a[:, DH:] for a in uw]
            p["vn"] = each(lambda a, b, x: a - _bdot(b, x), p["u"], p["w"], st)
            p["qd"] = each(jnp.multiply, q, p["eg"])
            p["kd"] = each(jnp.multiply, k, p["el"])
            p["dqd"] = each(_bdot_nt, dout, st)
            p["datt"] = each(lambda d, x: jnp.where(tril, _bdot_nt(d, x), 0.0), dout, p["vn"])
            p["dqk"] = each(jnp.multiply, p["datt"], p["dec"])
            p["qd_do"] = each(_bdot_tn, p["qd"], dout)
            p["att_do"] = each(_bdot_tn, p["att"], dout)
            return p

        def after_state(p, ds):
            q, k, v, st, inv, bcol = p["q"], p["k"], p["v"], p["st"], p["inv"], p["bcol"]
            eg, el, kb, u, w = p["eg"], p["el"], p["kb"], p["u"], p["w"]
            dvn = each(lambda a, kk, x: a + _bdot(kk, x), p["att_do"], p["kd"], ds)
            dkd = each(_bdot_nt, p["vn"], ds)
            dw = each(lambda a, x: -_bdot_nt(a, x), dvn, st)
            new_ds = each(lambda x, e, a, ww, dv_: x * e + a - _bdot_tn(ww, dv_), ds, p["elast"], p["qd_do"], w, dvn)
            dglast = each(lambda e, x, d: e * jnp.sum(rsum(x.astype(f32) * d), axis=0, keepdims=True), p["elast"], st, ds)
            dr = each(lambda i, a, b: _bdot_tn(i, jnp.concatenate([a, b], axis=1)), inv, dvn, dw)
            dvb = [a[:, :DH] for a in dr]
            dkbg = [a[:, DH:] for a in dr]
            dlow = each(lambda a, b, x, y: -jnp.where(r > c, _bdot_nt(a, b) + _bdot_nt(x, y), 0.0), dvb, u, dkbg, w)
            dkk = each(jnp.multiply, dlow, p["dec"])
            mm = each(lambda a, b, x, y: a * b + x * y, dlow, p["low"], p["datt"], p["att"])
            dkb = each(lambda a, kk, b, e: _bdot(a, kk) + b * e, dkk, k, dkbg, eg)
            dk = each(lambda a, b, x, y, d, e, f, g: _bdot_tn(a, b) + _bdot_tn(x, y) + d * e + f * g,
                      dkk, kb, p["dqk"], q, dkd, el, dkb, bcol)
            dq = each(lambda a, kk, d, e: _bdot(a, kk) + d * e, p["dqk"], k, p["dqd"], eg)
            dv = each(jnp.multiply, dvb, bcol)
            dbeta = each(lambda a, b, x, y: rsum(a * b) + rsum(x * y), dkb, k, dvb, v)
            deg = each(lambda a, b, x, y: rsum(a * b) + rsum(x * y), dkbg, kb, p["dqd"], q)
            delc = each(lambda a, b, e: rsum(a * b) * e, dkd, k, el)
            dgc = each(lambda m, a, e, d: rsum(m) - rsum(eye * jnp.sum(m, axis=0, keepdims=True)) + a * e - d,
                       mm, deg, eg, delc)
            dgc = each(lambda g, d, l: g + jnp.where(p["rcol"] == CH - 1, jnp.sum(d, axis=0, keepdims=True) + l, 0.0),
                       dgc, delc, dglast)
            dg_acc = jnp.zeros((CH, 128), f32)
            db_acc = jnp.zeros((CH, 128), f32)
            rows = p["rows"]
            for h in hs:
                dq_ref[rows, h * DH:(h + 1) * DH] = dq[h]
                dk_ref[rows, h * DH:(h + 1) * DH] = dk[h]
                dv_ref[rows, h * DH:(h + 1) * DH] = dv[h]
                dg_acc = dg_acc + dgc[h] * (lane == h).astype(f32)
                db_acc = db_acc + dbeta[h] * (lane == H + h).astype(f32)
            dgb_ref[rows, :] = _hdot((r <= c).astype(f32), dg_acc) + db_acc
            return new_ds

        order = list(reversed(range(GDN_STEP)))
        pre = [before_state(s) for s in order]
        ds = [dstate[h] for h in hs]
        for p in pre:
            ds = after_state(p, ds)
        for h in hs:
            dstate[h] = ds[h]

    tb = GDN_STEP * CH
    n_steps = t // tb
    rev = lambda i: (n_steps - 1 - i, 0)
    rev4 = lambda i: (n_steps - 1 - i, 0, 0, 0)
    return pl.pallas_call(
        body, name="gdn_bwd", grid=(n_steps,),
        in_specs=[pl.BlockSpec((tb, D), rev), pl.BlockSpec((tb, D), rev), pl.BlockSpec((tb, D), rev),
                  pl.BlockSpec((tb, 128), rev), pl.BlockSpec((tb, D), rev),
                  pl.BlockSpec((GDN_STEP, H, DH, DH), rev4), pl.BlockSpec((GDN_STEP, H, CH, CH), rev4)],
        out_specs=[pl.BlockSpec((tb, D), rev), pl.BlockSpec((tb, D), rev), pl.BlockSpec((tb, D), rev),
                   pl.BlockSpec((tb, 128), rev)],
        out_shape=[jax.ShapeDtypeStruct((t, D), f32)] * 3 + [jax.ShapeDtypeStruct((t, 128), f32)],
        scratch_shapes=[pltpu.VMEM((H, DH, DH), f32)],
        compiler_params=_params(1),
    )(qn, kn, vc, gbeta, do, s_all, t_all)


def _pad_rows(w, rows=8):
    return jnp.pad(w, ((0, rows - w.shape[0]), (0, 0)))


_REST = ("w_up", "w_a_out", "w_b_out", "w_o", "w_down")


def _local_step(x, tgt, w, comm=None):
    g1 = w["norm_mix_g"].reshape(1, D)
    if comm is None:
        h1 = _rms_fwd(x, g1, name="rms1_fwd")
    else:
        h1, gathered = _rms_fwd(x, g1, name="rms1_fwd", exchange=comm.gather_first())
        w = {**w, **comm.finish_first(gathered)}
    w1, w2 = w["w1"], w["w2"]
    wa = _pad_rows(w["conv_a_w"])
    wg = _pad_rows(w["gdn_conv_w"])
    wf = _pad_rows(w["ffn_conv_w"])
    alog = jnp.pad(w["gdn_A_log"].reshape(1, H), ((0, 0), (0, 128 - H)))
    dtb = jnp.pad(w["gdn_dt_bias"].reshape(1, H), ((0, 0), (0, 128 - H)))
    g2 = w["norm_ffn_g"].reshape(1, D)
    g3 = w["norm_final_g"].reshape(1, D)
    gn = w["gdn_norm_g"].reshape(1, DH)

    if comm is None:
        pg = _matmul(h1, w1, name="mm_in", cols=(0, 6 * D), out_dtype=bf16)
        pq = _matmul(h1, w1, name="mm_in_qkv", cols=(6 * D, 3 * D))
    else:
        pg, gathered = _matmul(h1, w1, name="mm_in", cols=(0, 6 * D), out_dtype=bf16, exchange=comm.gather_rest())
        pq, gathered = _matmul(h1, w1, name="mm_in_qkv", cols=(6 * D, 3 * D), exchange=_gather_forward_exchange(gathered))
        w = {**w, **comm.finish_gather(gathered)}
    ya_in, qn, kn, vc, gbeta, p2 = _pre_fwd(pg, pq, h1, w2, wa, wg, alog, dtb)
    o, s_all, t_all = _gdn_fwd(qn, kn, vc, gbeta)
    yb_in = _post_fwd(o, pg, gn)
    ya = _matmul(ya_in, w["w_a_out"], name="mm_a", out_dtype=bf16)
    yb = _matmul(yb_in, w["w_b_out"], name="mm_b", out_dtype=bf16)
    mix = _mix_fwd(ya, yb, pg)
    x2 = _matmul(mix, w["w_o"], name="mm_o", add=x)
    h2 = _rms_fwd(x2, g2, name="rms2_fwd")
    up = _matmul(h2, w["w_up"], nt=True, name="mm_up", tn=DFF // 2, out_dtype=bf16)
    act = _ffn_fwd(up, wf)
    x3 = _matmul(act, w["w_down"], name="mm_down", add=x2, tm=512)
    loss_p, dx3, dx3b, dg3 = _final(x3, tgt, g3)

    grads = {"norm_final_g": dg3}
    dact = _matmul(dx3b, w["w_down"], nt=True, name="mm_down_dx", tm=512, tn=DFF, out_dtype=bf16)
    grads["w_down"] = _matmul_tn(act, dx3b, name="mm_down_dw", tm=DFF // 2)
    dc, dwf = _ffn_bwd1(dact, up, wf)
    grads["ffn_conv_w"] = dwf
    dup = _ffn_bwd2(dc, wf)
    dh2 = _matmul(dup, w["w_up"], name="mm_up_dx", tk=DFF)
    grads["w_up"] = _matmul_tn(dup, h2, name="mm_up_dw", tm=DFF // 2)
    dx2, dx2b, dg2 = _rms_bwd(dh2, x2, g2, dx3, name="rms2_bwd")
    grads["norm_ffn_g"] = dg2
    dmix = _matmul(dx2b, w["w_o"], nt=True, name="mm_o_dx", out_dtype=bf16)
    grads["w_o"] = _matmul_tn(mix, dx2b, name="mm_o_dw")
    dya, dyb, dgates = _mix_bwd(dmix, ya, yb, pg)
    dya_in = _matmul(dya, w["w_a_out"], nt=True, name="mm_a_dx", out_dtype=bf16)
    grads["w_a_out"] = _matmul_tn(ya_in, dya, name="mm_a_dw")
    dyb_in = _matmul(dyb, w["w_b_out"], nt=True, name="mm_b_dx")
    grads["w_b_out"] = _matmul_tn(yb_in, dyb, name="mm_b_dw")
    do, dz, dgn = _post_bwd(dyb_in, o, pg, gn)
    grads["gdn_norm_g"] = dgn
    dqn, dkn, dvc, dgb = _gdn_bwd(qn, kn, vc, gbeta, do, s_all, t_all)
    dbg, dca, dc4, dp2, dwa, dwg, dal, ddt, grads["w2"] = _pre_bwd1(pg, pq, p2, dya_in, dqn, dkn, dvc, dgb, gbeta, h1,
                                                                    wa, wg, alog, dtb)
    grads["conv_a_w"] = dwa
    grads["gdn_conv_w"] = dwg
    grads["gdn_A_log"] = dal
    grads["gdn_dt_bias"] = ddt
    if comm is None:
        dp1 = _pre_bwd2(dca, dc4, pg, dbg, dz, dgates, wa, wg)
        grads["w1"] = _matmul_tn(dp1, h1, name="mm_in_dw", tt=4096)
        dh1 = _matmul(dp1, w1, nt=True, name="mm_in_dx", tm=512, tk=NW1 // 2)
    else:
        exchange, blocks = comm.reduce_halves(_REST, grads)
        dp1, recv = _pre_bwd2(dca, dc4, pg, dbg, dz, dgates, wa, wg, exchange=exchange)
        exchange, sums = comm.reduce_sums(_REST, blocks, recv)
        grads["w1"], recv = _matmul_tn(dp1, h1, name="mm_in_dw", tt=4096, exchange=exchange)
        comm.finish_reduce(_REST, sums, recv)
        exchange, blocks = comm.reduce_halves(("w_in",), grads)
        exchange, sums = comm.reduce_sums(("w_in",), blocks, _run_exchange(exchange, name="rs_sibling_w_in"))
        dh1, recv = _matmul(dp1, w1, nt=True, name="mm_in_dx", tm=512, tk=NW1 // 2, exchange=exchange)
        comm.finish_reduce(("w_in",), sums, recv)
    dx, _, dg1 = _rms_bwd(dh1, x, g1, dx2, name="rms1_bwd", more=(dp2, w2))
    grads["norm_mix_g"] = dg1
    return loss_p, dx, grads


_ANY = pl.BlockSpec(memory_space=pl.ANY)


def _remote(src, dst, send_sem, recv_sem, to):
    return pltpu.make_async_remote_copy(src_ref=src, dst_ref=dst, send_sem=send_sem, recv_sem=recv_sem,
                                        device_id=to, device_id_type=MESH)


def _run_exchange(exchange, *, name):
    arrays, shapes, sems, start, wait = exchange
    n_in, n_out = len(arrays), len(shapes)

    def body(*refs):
        start(refs[:n_in], refs[n_in:n_in + n_out], refs[n_in + n_out:])
        wait(refs[:n_in], refs[n_in:n_in + n_out], refs[n_in + n_out:])

    return pl.pallas_call(body, name=name, out_shape=list(shapes), in_specs=[_ANY] * n_in, out_specs=[_ANY] * n_out,
                          scratch_shapes=list(sems))(*arrays)


def _gather_exchange(shards):
    n = len(shards)

    def copies(x_refs, out_refs, sems):
        send_sems, recv_sems, local_sems = sems
        x, y, c = lax.axis_index("x"), lax.axis_index("y"), lax.axis_index("c")

        def flip(v, b):
            return v + b - 2 * v * b

        me, sibling = (x, y, c), (x, y, 1 - c)
        chip1, chip2, diag = (flip(x, 1 - c), flip(y, c)), (flip(x, c), flip(y, 1 - c)), (1 - x, 1 - y)

        def copy(a, k, blk, to, from_input=False):
            dst = out_refs[a].at[4 * blk[0] + 2 * blk[1] + blk[2]]
            return _remote(x_refs[a] if from_input else dst, dst, send_sems.at[a, k], recv_sems.at[a, k], to)

        mine = [pltpu.make_async_copy(x_refs[a], out_refs[a].at[4 * x + 2 * y + c], local_sems.at[a]) for a in range(n)]
        first = []
        for a in range(n):
            first += [copy(a, 0, me, sibling, from_input=True), copy(a, 1, me, (*chip1, c), from_input=True),
                      copy(a, 2, me, (*chip2, c), from_input=True)]
        return copy, mine, first, me, sibling, chip1, chip2, diag, c

    def start(x_refs, out_refs, sems):
        _, mine, first, *_ = copies(x_refs, out_refs, sems)
        for cp in mine + first:
            cp.start()

    def wait(x_refs, out_refs, sems):
        copy, mine, first, me, sibling, chip1, chip2, diag, c = copies(x_refs, out_refs, sems)
        passed = []

        def pass_on(cp):
            passed.append(cp)
            cp.start()

        for a in range(n):
            copy(a, 1, (*chip1, c), me).wait_recv()
            pass_on(copy(a, 3, (*chip1, c), (*chip2, c)))
            pass_on(copy(a, 4, (*chip1, c), sibling))
        for a in range(n):
            copy(a, 2, (*chip2, c), me).wait_recv()
            pass_on(copy(a, 5, (*chip2, c), sibling))
        for a in range(n):
            copy(a, 3, (*diag, c), me).wait_recv()
            pass_on(copy(a, 6, (*diag, c), sibling))
        for a in range(n):
            copy(a, 0, sibling, me).wait_recv()
            copy(a, 4, (*chip2, 1 - c), me).wait_recv()
            copy(a, 5, (*chip1, 1 - c), me).wait_recv()
            copy(a, 6, (*diag, 1 - c), me).wait_recv()
        for cp in first + passed:
            cp.wait_send()
        for cp in mine:
            cp.wait()

    shapes = [jax.ShapeDtypeStruct((N_DEV, *s.shape), s.dtype) for s in shards]
    sems = [pltpu.SemaphoreType.DMA((n, 7)), pltpu.SemaphoreType.DMA((n, 7)), pltpu.SemaphoreType.DMA((n,))]
    return shards, shapes, sems, start, wait


def _gather_direct_exchange(shards):
    n = len(shards)

    def copies(x_refs, out_refs, sems):
        send_sems, recv_sems, local_sems = sems
        x, y, c = lax.axis_index("x"), lax.axis_index("y"), lax.axis_index("c")
        targets = [(x, y, 1 - c), (1 - x, y, c), (x, 1 - y, c), (1 - x, 1 - y, c)]
        local, sends, recvs = [], [], []
        for a in range(n):
            mine = out_refs[a].at[4 * x + 2 * y + c]
            local.append(pltpu.make_async_copy(x_refs[a], mine, local_sems.at[a]))
            for k, to in enumerate(targets):
                theirs = out_refs[a].at[4 * to[0] + 2 * to[1] + to[2]]
                sends.append(_remote(x_refs[a], mine, send_sems.at[a, k], recv_sems.at[a, k], to))
                recvs.append(_remote(theirs, theirs, send_sems.at[a, k], recv_sems.at[a, k], to))
        return local, sends, recvs

    def start(x_refs, out_refs, sems):
        local, sends, _ = copies(x_refs, out_refs, sems)
        for cp in local + sends:
            cp.start()

    def wait(x_refs, out_refs, sems):
        local, sends, recvs = copies(x_refs, out_refs, sems)
        for cp in recvs:
            cp.wait_recv()
        for cp in sends:
            cp.wait_send()
        for cp in local:
            cp.wait()

    shapes = [jax.ShapeDtypeStruct((N_DEV, *s.shape), s.dtype) for s in shards]
    sems = [pltpu.SemaphoreType.DMA((n, 4)), pltpu.SemaphoreType.DMA((n, 4)), pltpu.SemaphoreType.DMA((n,))]
    return shards, shapes, sems, start, wait


def _gather_forward_exchange(gathered):
    n = len(gathered)

    def copies(_, out_refs, sems):
        send_sems, recv_sems = sems
        x, y, c = lax.axis_index("x"), lax.axis_index("y"), lax.axis_index("c")
        sibling = (x, y, 1 - c)
        sends, recvs = [], []
        for a in range(n):
            for j, (px, py) in enumerate([(1 - x, y), (x, 1 - y), (1 - x, 1 - y)]):
                mine = out_refs[a].at[4 * px + 2 * py + c]
                theirs = out_refs[a].at[4 * px + 2 * py + 1 - c]
                sends.append(_remote(mine, mine, send_sems.at[a, j], recv_sems.at[a, j], sibling))
                recvs.append(_remote(theirs, theirs, send_sems.at[a, j], recv_sems.at[a, j], sibling))
        return sends, recvs

    def start(in_refs, out_refs, sems):
        for cp in copies(in_refs, out_refs, sems)[0]:
            cp.start()

    def wait(in_refs, out_refs, sems):
        sends, recvs = copies(in_refs, out_refs, sems)
        for cp in recvs:
            cp.wait_recv()
        for cp in sends:
            cp.wait_send()

    shapes = [jax.ShapeDtypeStruct(g.shape, g.dtype) for g in gathered]
    sems = [pltpu.SemaphoreType.DMA((n, 3)), pltpu.SemaphoreType.DMA((n, 3))]
    return gathered, shapes, sems, start, wait, True


def _chips_exchange(hsums):
    n = len(hsums)

    def copies(h_refs, out_refs, sems):
        send_sems, recv_sems = sems
        x, y, c = lax.axis_index("x"), lax.axis_index("y"), lax.axis_index("c")
        chips = [(1 - x, y), (x, 1 - y), (1 - x, 1 - y)]
        return [_remote(h_refs[a].at[2 * px + py], out_refs[a].at[k], send_sems.at[a, k], recv_sems.at[a, k], (px, py, c))
                for a in range(n) for k, (px, py) in enumerate(chips)]

    def start(h_refs, out_refs, sems):
        for cp in copies(h_refs, out_refs, sems):
            cp.start()

    def wait(h_refs, out_refs, sems):
        for cp in copies(h_refs, out_refs, sems):
            cp.wait()

    shapes = [jax.ShapeDtypeStruct((3, *h.shape[1:]), h.dtype) for h in hsums]
    sems = [pltpu.SemaphoreType.DMA((n, 3)), pltpu.SemaphoreType.DMA((n, 3))]
    return hsums, shapes, sems, start, wait


def _sibling_exchange(halves):
    n = len(halves)

    def copies(p_refs, out_refs, sems):
        send_sems, recv_sems = sems
        x, y, c = lax.axis_index("x"), lax.axis_index("y"), lax.axis_index("c")
        return [_remote(p_refs[a], out_refs[a], send_sems.at[a], recv_sems.at[a], (x, y, 1 - c)) for a in range(n)]

    def start(p_refs, out_refs, sems):
        for cp in copies(p_refs, out_refs, sems):
            cp.start()

    def wait(p_refs, out_refs, sems):
        for cp in copies(p_refs, out_refs, sems):
            cp.wait()

    shapes = [jax.ShapeDtypeStruct(h.shape, h.dtype) for h in halves]
    return halves, shapes, [pltpu.SemaphoreType.DMA((n,)), pltpu.SemaphoreType.DMA((n,))], start, wait


_IN_RANGES = ((0, 3 * D, 0, 0), (3 * D, 6 * D, 0, 6 * D), (6 * D, 7 * D, 0, 3 * D), (7 * D, 7 * D + 16, 1, 0),
              (7 * D + 16, 9 * D + 16, 0, 4 * D))


def _col_pieces(width, ranges):
    pieces = []
    for d in range(N_DEV):
        lo, hi = d * width, (d + 1) * width
        for glo, ghi, mat, mlo in ranges:
            a, b = max(lo, glo), min(hi, ghi)
            if a < b:
                pieces.append((d, a - lo, b - lo, mat, mlo + a - glo))
    return pieces


def _cols_to_matrices(g, ranges, out_widths, *, name):
    _, rows, width = g.shape
    tb = 128
    pieces = _col_pieces(width, ranges)
    covered = [sum(p[2] - p[1] for p in pieces if p[3] == m) for m in range(len(out_widths))]

    def body(g_ref, *o_refs):
        for m, o_ref in enumerate(o_refs):
            if covered[m] < out_widths[m]:
                o_ref[...] = jnp.zeros_like(o_ref)
        for d, b0, b1, m, m0 in pieces:
            o_refs[m][:, m0:m0 + b1 - b0] = g_ref[d, :, b0:b1]

    return pl.pallas_call(
        body, name=name, grid=(rows // tb,), in_specs=[pl.BlockSpec((N_DEV, tb, width), lambda i: (0, i, 0))],
        out_specs=[pl.BlockSpec((tb, wo), lambda i: (i, 0)) for wo in out_widths],
        out_shape=[jax.ShapeDtypeStruct((rows, wo), g.dtype) for wo in out_widths], compiler_params=_params(1),
    )(g)


def _transposed_matrices_to_blocks(mats, ranges, width, *, name):
    rows = mats[0].shape[1]
    pieces = _col_pieces(width, ranges)

    def body(*refs):
        m_refs, g_ref = refs[:-1], refs[-1]
        for d, b0, b1, m, m0 in pieces:
            g_ref[d, b0:b1, :] = m_refs[m][m0:m0 + b1 - b0, :]

    return pl.pallas_call(
        body, name=name, grid=(rows // 128,),
        in_specs=[pl.BlockSpec((mt.shape[0], 128), lambda i: (0, i)) for mt in mats],
        out_specs=pl.BlockSpec((N_DEV, width, 128), lambda i: (0, 0, i)),
        out_shape=jax.ShapeDtypeStruct((N_DEV, width, rows), mats[0].dtype), compiler_params=_params(1),
    )(*mats)


def _row_block(rows):
    return 128 if rows % 128 == 0 else rows


def _half_bf16(g4, c_other, *, name):
    _, _, rows, width = g4.shape
    tb = _row_block(rows)

    def body(c_ref, p_ref, o_ref):
        o_ref[0] = p_ref[0, 0].astype(bf16)

    grid_spec = pltpu.PrefetchScalarGridSpec(
        num_scalar_prefetch=1, grid=(4, rows // tb),
        in_specs=[pl.BlockSpec((1, 1, tb, width), lambda j, i, c_ref: (j, c_ref[0], i, 0))],
        out_specs=pl.BlockSpec((1, tb, width), lambda j, i, c_ref: (j, i, 0)))
    return pl.pallas_call(
        body, name=name, grid_spec=grid_spec, out_shape=jax.ShapeDtypeStruct((4, rows, width), bf16),
        compiler_params=_params(2, _vmem_for(4 * tb * width, 2 * tb * width)),
    )(c_other, g4)


def _pair_sum(g4, recv, c_me, *, name):
    _, _, rows, width = g4.shape
    tb = _row_block(rows)

    def body(c_ref, p_ref, r_ref, o_ref, ob_ref):
        s = p_ref[0, 0] + r_ref[0].astype(f32)
        o_ref[0] = s
        ob_ref[0] = s.astype(bf16)

    blk = pl.BlockSpec((1, tb, width), lambda j, i, c_ref: (j, i, 0))
    grid_spec = pltpu.PrefetchScalarGridSpec(
        num_scalar_prefetch=1, grid=(4, rows // tb),
        in_specs=[pl.BlockSpec((1, 1, tb, width), lambda j, i, c_ref: (j, c_ref[0], i, 0)), blk],
        out_specs=[blk, blk])
    return pl.pallas_call(
        body, name=name, grid_spec=grid_spec,
        out_shape=[jax.ShapeDtypeStruct((4, rows, width), f32), jax.ShapeDtypeStruct((4, rows, width), bf16)],
        compiler_params=_params(2, _vmem_for(4 * tb * width, 2 * tb * width, 4 * tb * width, 2 * tb * width)),
    )(c_me, g4, recv)


def _adam_shard(hsum, recv, chip, w, m, v, *, name):
    _, rows, width = w.shape
    tb = _row_block(rows)

    def body(j_ref, h_ref, r_ref, w_ref, m_ref, v_ref, g_out, d_out, m_out, v_out):
        g = ((h_ref[0] + r_ref[0].astype(f32)) + r_ref[1].astype(f32)) + r_ref[2].astype(f32)
        delta, mn, vn = _adam_math(w_ref[0], g, m_ref[0], v_ref[0])
        g_out[0] = g
        d_out[0] = delta
        m_out[0] = mn
        v_out[0] = vn

    blk = pl.BlockSpec((1, tb, width), lambda i, j_ref: (0, i, 0))
    grid_spec = pltpu.PrefetchScalarGridSpec(
        num_scalar_prefetch=1, grid=(rows // tb,),
        in_specs=[pl.BlockSpec((1, tb, width), lambda i, j_ref: (j_ref[0], i, 0)),
                  pl.BlockSpec((3, tb, width), lambda i, j_ref: (0, i, 0)), blk, blk, blk],
        out_specs=[blk, blk, blk, blk])
    return pl.pallas_call(
        body, name=name, grid_spec=grid_spec, out_shape=[jax.ShapeDtypeStruct(w.shape, f32)] * 4,
        compiler_params=_params(1, _vmem_for(*[4 * tb * width] * 8, 6 * tb * width)),
    )(chip, hsum, recv, w, m, v)


def _sum_shard(hsum, recv, chip, *, name):
    _, rows, width = hsum.shape
    tb = _row_block(rows)

    def body(j_ref, h_ref, r_ref, g_out):
        g_out[...] = ((h_ref[0] + r_ref[0].astype(f32)) + r_ref[1].astype(f32)) + r_ref[2].astype(f32)

    grid_spec = pltpu.PrefetchScalarGridSpec(
        num_scalar_prefetch=1, grid=(rows // tb,),
        in_specs=[pl.BlockSpec((1, tb, width), lambda i, j_ref: (j_ref[0], i, 0)),
                  pl.BlockSpec((3, tb, width), lambda i, j_ref: (0, i, 0))],
        out_specs=pl.BlockSpec((tb, width), lambda i, j_ref: (i, 0)))
    return pl.pallas_call(body, name=name, grid_spec=grid_spec, out_shape=jax.ShapeDtypeStruct((rows, width), f32),
                          compiler_params=_params(1, _vmem_for(*[4 * tb * width] * 2, 6 * tb * width)))(chip, hsum, recv)


def _adam_columns(g, w, m, v, *, name):
    cols, _, rows = w.shape
    tb = cols // 2

    def body(g_ref, w_ref, m_ref, v_ref, d_out, m_out, v_out):
        delta, mn, vn = _adam_math(w_ref[...], g_ref[...], m_ref[...], v_ref[...])
        d_out[...] = delta
        m_out[...] = mn
        v_out[...] = vn

    blk = pl.BlockSpec((tb, 1, rows), lambda i: (i, 0, 0))
    return pl.pallas_call(
        body, name=name, grid=(cols // tb,), in_specs=[blk] * 4, out_specs=[blk] * 3,
        out_shape=[jax.ShapeDtypeStruct(w.shape, f32)] * 3,
        compiler_params=_params(1, _vmem_for(*[4 * tb * rows] * 7)),
    )(g, w, m, v)


R_SMALL = 8 + 8 * N_DEV
_SMALL_LANES = {"gdn_norm_g": (0, DH), "gdn_A_log": (DH, DH + H), "gdn_dt_bias": (2 * DH, 2 * DH + H)}
_LOSS_LANE = 3 * DH


def _pack_small(dg1, dg2, dg3, dgn, dal, ddt, loss_p, dwa, dwg, dwf):
    def body(dg1_ref, dg2_ref, dg3_ref, dgn_ref, dal_ref, ddt_ref, loss_ref, dwa_ref, dwg_ref, dwf_ref, o_ref):
        def total(ref):
            return jnp.sum(ref[...], axis=0, keepdims=True)

        o_ref[...] = jnp.zeros_like(o_ref)
        o_ref[0:1, :] = total(dg1_ref)
        o_ref[1:2, :] = total(dg2_ref)
        o_ref[2:3, :] = total(dg3_ref)
        o_ref[3:4, 0:DH] = total(dgn_ref)
        o_ref[3:4, DH:2 * DH] = total(dal_ref)
        o_ref[3:4, 2 * DH:3 * DH] = total(ddt_ref)
        o_ref[3:4, 3 * DH:4 * DH] = total(loss_ref)
        for d in range(N_DEV):
            base = 8 + 8 * d
            o_ref[base:base + 3, 0:128] = dwa_ref[0:3, 128 * d:128 * (d + 1)]
            o_ref[base:base + 4, 128:512] = dwg_ref[0:4, 384 * d:384 * (d + 1)]
            o_ref[base + 4:base + 7, 0:704] = dwf_ref[0:3, 704 * d:704 * (d + 1)]

    return pl.pallas_call(body, name="pack_small", out_shape=jax.ShapeDtypeStruct((R_SMALL, D), f32))(
        dg1, dg2, dg3, dgn, dal, ddt, loss_p, dwa, dwg, dwf)


_SMALL = ("norm_mix_g", "norm_ffn_g", "norm_final_g", "gdn_norm_g", "gdn_A_log", "gdn_dt_bias",
          "conv_a_w", "gdn_conv_w", "ffn_conv_w")


def _adam_small(gath, me, w, m, v):
    arrays = [t[n] for n in _SMALL for t in (w, m, v)]

    def body(me_ref, ga_ref, gb_ref, *refs):
        ins, outs = refs[:len(arrays)], refs[len(arrays):]
        ga, gb = ga_ref[0], gb_ref[0]
        for s in range(1, N_DEV):
            ga = ga + ga_ref[s]
            gb = gb + gb_ref[s]
        grads = {"norm_mix_g": ga[0:1, :], "norm_ffn_g": ga[1:2, :], "norm_final_g": ga[2:3, :],
                 "conv_a_w": gb[0:3, 0:128], "gdn_conv_w": gb[0:4, 128:512], "ffn_conv_w": gb[4:7, 0:704]}
        for n, (lo, hi) in _SMALL_LANES.items():
            grads[n] = ga[3:4, lo:hi]
        for i, n in enumerate(_SMALL):
            three_d = len(w[n].shape) == 3
            wv, mv, vv = (r[0] if three_d else r[...] for r in ins[3 * i:3 * i + 3])
            delta, mn, vn = _adam_math(wv, grads[n], mv, vv)
            for o_ref, val in zip(outs[4 * i:4 * i + 4], (grads[n], delta, mn, vn)):
                if three_d:
                    o_ref[0] = val
                else:
                    o_ref[...] = val
        outs[-1][...] = ga[3:4, _LOSS_LANE:_LOSS_LANE + 1]

    def whole(shape):
        return pl.BlockSpec(shape, lambda i, me_ref: (0,) * len(shape))

    grid_spec = pltpu.PrefetchScalarGridSpec(
        num_scalar_prefetch=1, grid=(1,),
        in_specs=[pl.BlockSpec((N_DEV, 8, D), lambda i, me_ref: (0, 0, 0)),
                  pl.BlockSpec((N_DEV, 8, D), lambda i, me_ref: (0, 1 + me_ref[0], 0))] + [whole(a.shape) for a in arrays],
        out_specs=[whole(w[n].shape) for n in _SMALL for _ in range(4)] + [whole((1, 1))])
    res = pl.pallas_call(
        body, name="adam_small", grid_spec=grid_spec,
        out_shape=[jax.ShapeDtypeStruct(w[n].shape, f32) for n in _SMALL for _ in range(4)]
        + [jax.ShapeDtypeStruct((1, 1), f32)],
        compiler_params=_params(1),
    )(me, gath, gath, *arrays)
    return {n: tuple(res[4 * i:4 * i + 4]) for i, n in enumerate(_SMALL)}, res[-1]


def _adam_math(w, g, m, v):
    m = ADAM_B1 * m + (1.0 - ADAM_B1) * g
    v = ADAM_B2 * v + (1.0 - ADAM_B2) * jnp.square(g)
    m_hat = m / (1.0 - ADAM_B1 ** ADAM_STEP)
    v_hat = v / (1.0 - ADAM_B2 ** ADAM_STEP)
    delta = -ADAM_LR * (m_hat / (jnp.sqrt(v_hat) + ADAM_EPS) + ADAM_WD * w)
    return delta, m, v


_WEIGHTS = ("norm_mix_g", "w_in", "conv_a_w", "gdn_conv_w", "gdn_A_log", "gdn_dt_bias", "gdn_norm_g", "w_a_out",
            "w_b_out", "w_o", "norm_ffn_g", "w_up", "ffn_conv_w", "w_down", "norm_final_g")
_CONVS = ("conv_a_w", "gdn_conv_w", "ffn_conv_w")


class _StepExchanges:
    def __init__(self, wts, mom, var, c_me, chip):
        self.wts, self.mom, self.var, self.c_me, self.chip = wts, mom, var, c_me, chip
        self.results = {}

    def gather_first(self):
        return _gather_exchange([self.wts["w_in"][0].astype(bf16)] + [self.wts[n][0] for n in _CONVS])

    def finish_first(self, gathered):
        g_in, gc_a, gc_g, gc_f = gathered
        w1, w2 = _cols_to_matrices(g_in, _IN_RANGES, (NW1, 128), name="relay_w_in")
        return {"w1": w1, "w2": w2, "conv_a_w": gc_a.transpose(1, 0, 2).reshape(3, D),
                "gdn_conv_w": gc_g.transpose(1, 0, 2).reshape(4, 3 * D),
                "ffn_conv_w": gc_f.transpose(1, 0, 2).reshape(3, 2 * DFF)}

    def gather_rest(self):
        return _gather_direct_exchange([self.wts[n][0].astype(bf16) for n in _REST])

    def finish_gather(self, gathered):
        g_up, g_a, g_b, g_o, g_down = gathered
        return {"w_up": g_up.reshape(2 * DFF, D), "w_a_out": g_a.reshape(D, D), "w_b_out": g_b.reshape(D, D),
                "w_o": g_o.reshape(D, D), "w_down": g_down.reshape(DFF, D)}

    def reduce_halves(self, names, grads):
        blocks = []
        for n in names:
            if n == "w_in":
                g = _transposed_matrices_to_blocks([grads["w1"], grads["w2"]], _IN_RANGES, R_IN, name="relay_dw_in")
                blocks.append(g.reshape(4, 2, R_IN, D))
            else:
                blocks.append(grads[n].reshape(4, 2, *self.wts[n].shape[1:]))
        return _sibling_exchange([_half_bf16(g, 1 - self.c_me, name="rs_half_" + n) for n, g in zip(names, blocks)]), blocks

    def reduce_sums(self, names, blocks, recv):
        sums = [_pair_sum(g, r, self.c_me, name="rs_sum_" + n) for n, g, r in zip(names, blocks, recv)]
        return _chips_exchange([s[1] for s in sums]), [s[0] for s in sums]

    def finish_reduce(self, names, sums, recv):
        for n, s, r in zip(names, sums, recv):
            if n == "w_in":
                g = _sum_shard(s, r, self.chip, name="rs_total_w_in")[:, None, :]
                w, m, v = (jnp.transpose(t[n], (2, 0, 1)) for t in (self.wts, self.mom, self.var))
                res = (g, *_adam_columns(g, w, m, v, name="adam_w_in"))
                self.results[n] = tuple(jnp.transpose(a, (1, 2, 0)) for a in res)
            else:
                self.results[n] = _adam_shard(s, r, self.chip, self.wts[n], self.mom[n], self.var[n], name="adam_" + n)


def kernel(x, norm_mix_g, w_in, conv_a_w, gdn_conv_w, gdn_A_log, gdn_dt_bias, gdn_norm_g, w_a_out, w_b_out, w_o, norm_ffn_g, w_up, ffn_conv_w, w_down, norm_final_g, loss_target, m_norm_mix_g, m_w_in, m_conv_a_w, m_gdn_conv_w, m_gdn_A_log, m_gdn_dt_bias, m_gdn_norm_g, m_w_a_out, m_w_b_out, m_w_o, m_norm_ffn_g, m_w_up, m_ffn_conv_w, m_w_down, m_norm_final_g, v_norm_mix_g, v_w_in, v_conv_a_w, v_gdn_conv_w, v_gdn_A_log, v_gdn_dt_bias, v_gdn_norm_g, v_w_a_out, v_w_b_out, v_w_o, v_norm_ffn_g, v_w_up, v_ffn_conv_w, v_w_down, v_norm_final_g):
    wts = dict(zip(_WEIGHTS, (norm_mix_g, w_in, conv_a_w, gdn_conv_w, gdn_A_log, gdn_dt_bias, gdn_norm_g, w_a_out,
                              w_b_out, w_o, norm_ffn_g, w_up, ffn_conv_w, w_down, norm_final_g)))
    mom = dict(zip(_WEIGHTS, (m_norm_mix_g, m_w_in, m_conv_a_w, m_gdn_conv_w, m_gdn_A_log, m_gdn_dt_bias,
                              m_gdn_norm_g, m_w_a_out, m_w_b_out, m_w_o, m_norm_ffn_g, m_w_up, m_ffn_conv_w,
                              m_w_down, m_norm_final_g)))
    var = dict(zip(_WEIGHTS, (v_norm_mix_g, v_w_in, v_conv_a_w, v_gdn_conv_w, v_gdn_A_log, v_gdn_dt_bias,
                              v_gdn_norm_g, v_w_a_out, v_w_b_out, v_w_o, v_norm_ffn_g, v_w_up, v_ffn_conv_w,
                              v_w_down, v_norm_final_g)))
    cx, cy, cc = lax.axis_index("x"), lax.axis_index("y"), lax.axis_index("c")
    c_me = jnp.reshape(cc, (1,)).astype(jnp.int32)
    chip = jnp.reshape(2 * cx + cy, (1,)).astype(jnp.int32)
    me = jnp.reshape(4 * cx + 2 * cy + cc, (1,)).astype(jnp.int32)

    def with_up_transposed(t):
        return {**t, "w_up": jnp.swapaxes(t["w_up"], 1, 2)}

    comm = _StepExchanges(with_up_transposed(wts), with_up_transposed(mom), with_up_transposed(var), c_me, chip)
    replicated = {n: wts[n] for n in ("norm_mix_g", "norm_ffn_g", "norm_final_g", "gdn_norm_g", "gdn_A_log", "gdn_dt_bias")}
    loss_p, dx, grads = _local_step(x[0], loss_target[0], replicated, comm)
    res = comm.results
    res["w_up"] = tuple(jnp.swapaxes(a, 1, 2) for a in res["w_up"])

    small = _pack_small(grads["norm_mix_g"], grads["norm_ffn_g"], grads["norm_final_g"], grads["gdn_norm_g"],
                        grads["gdn_A_log"], grads["gdn_dt_bias"], loss_p, grads["conv_a_w"], grads["gdn_conv_w"],
                        grads["ffn_conv_w"])
    (small_all,) = _run_exchange(_gather_exchange([small]), name="ag_small")

    def raw(t):
        return {n: t[n].reshape(1, D) if n == "norm_final_g" else t[n] for n in _SMALL}

    res_small, loss = _adam_small(small_all, me, raw(wts), raw(mom), raw(var))
    for n in _SMALL:
        res[n] = tuple(a.reshape(wts[n].shape) for a in res_small[n])
    outs = [[res[n][i] for n in _WEIGHTS] for i in range(4)]
    return (loss.reshape(()), dx[None], *outs[0], *outs[1], *outs[2], *outs[3])
```

```python
import jax
import jax.numpy as jnp
from jax import lax
from jax.experimental import pallas as pl
from jax.experimental.pallas import tpu as pltpu

f32 = jnp.float32
bf16 = jnp.bfloat16

D = 1024
H = 8
DH = 128
CH = 64
GDN_STEP = 2
DFF = 2816
NW1 = 9216
EPS = 1e-6
N_DEV = 8

ADAM_LR = 0.001
ADAM_B1 = 0.9
ADAM_B2 = 0.999
ADAM_EPS = 1e-08
ADAM_WD = 0.01
ADAM_STEP = 10

VMEM_LIMIT_BYTES = 48 * 1024 * 1024
VMEM_MAX_BYTES = 56 * 1024 * 1024

R_IN, R_UP = 1154, 704

_HI = lax.Precision.HIGHEST
MESH = pl.DeviceIdType.MESH


def _params(n_grid, vmem_bytes=None):
    return pltpu.CompilerParams(dimension_semantics=("arbitrary",) * n_grid,
                                vmem_limit_bytes=VMEM_LIMIT_BYTES if vmem_bytes is None else vmem_bytes)


def _vmem_for(*block_bytes, extra=0):
    need = 2 * sum(block_bytes) + extra + 4 * 1024 * 1024
    return min(max(need, VMEM_LIMIT_BYTES), VMEM_MAX_BYTES)


def _bdot(a, b):
    return jnp.dot(a.astype(bf16), b.astype(bf16), preferred_element_type=f32)


def _bdot_nt(a, b):
    return lax.dot_general(a.astype(bf16), b.astype(bf16), (((1,), (1,)), ((), ())), preferred_element_type=f32)


def _bdot_tn(a, b):
    return lax.dot_general(a.astype(bf16), b.astype(bf16), (((0,), (0,)), ((), ())), preferred_element_type=f32)


def _hdot(a, b):
    return jnp.dot(a, b, preferred_element_type=f32, precision=_HI)


def _idot(a, b):
    return jnp.dot(a, b, preferred_element_type=f32, precision=lax.Precision.HIGH)


def _sigmoid(x):
    return 1.0 / (1.0 + jnp.exp(-x))


def _softplus(x):
    return jnp.maximum(x, 0.0) + jnp.log(1.0 + jnp.exp(-jnp.abs(x)))


def _shift_down(x, halo, j):
    if j == 0:
        return x
    xr = pltpu.roll(x, j, 0)
    hr = pltpu.roll(halo, j, 0)
    r8 = lax.broadcasted_iota(jnp.int32, hr.shape, 0)
    top = jnp.where(r8 < j, hr, xr[:8])
    return jnp.concatenate([top, xr[8:]], axis=0)


def _shift_up(x, halo, j):
    if j == 0:
        return x
    n = x.shape[0]
    xr = pltpu.roll(x, n - j, 0)
    hr = pltpu.roll(halo, 8 - j, 0)
    r8 = lax.broadcasted_iota(jnp.int32, hr.shape, 0)
    bot = jnp.where(r8 >= 8 - j, hr, xr[n - 8:])
    return jnp.concatenate([xr[:n - 8], bot], axis=0)


def _taps_down(x, halo, k):
    return [_shift_down(x, halo, k - 1 - j) for j in range(k)]


def _strip(i, base=0):
    return slice(base + i * 128, base + (i + 1) * 128)


def _strip_taps(x, halo, first, k):
    return _taps_down(x, jnp.where(first, 0.0, halo), k)


def _strip_conv(w_ref, sl, taps):
    out = w_ref[0:1, sl] * taps[0]
    for j in range(1, len(taps)):
        out = out + w_ref[j:j + 1, sl] * taps[j]
    return out


def _strip_weight_grad(dw_ref, sl, dy, taps):
    for j, tap in enumerate(taps):
        dw_ref[j:j + 1, sl] += jnp.sum(dy * tap, axis=0, keepdims=True)


def _strip_conv_up(dy, halo, last, w_ref, sl, k):
    halo = jnp.where(last, 0.0, halo)
    out = w_ref[k - 1:k, sl] * dy
    for j in range(k - 1):
        out = out + w_ref[j:j + 1, sl] * _shift_up(dy, halo, k - 1 - j)
    return out


def _row(tb, w, col=0):
    return pl.BlockSpec((tb, w), lambda i: (i, col))


def _prev(tb, w, col=0, rows=8):
    return pl.BlockSpec((rows, w), lambda i: (jnp.maximum(i * (tb // rows) - 1, 0), col))


def _next(tb, w, n_rows, col=0, rows=8):
    last = n_rows // rows - 1
    return pl.BlockSpec((rows, w), lambda i: (jnp.minimum((i + 1) * (tb // rows), last), col))


def _f32(ref, sl):
    return ref[:, sl].astype(f32)


def _halo_before(ref, sl):
    h = _f32(ref, sl)
    return h[h.shape[0] - 8:]


def _halo_after(ref, sl):
    return _f32(ref, sl)[:8]


def _fixed(shape):
    return pl.BlockSpec(shape, lambda i: (0,) * len(shape))


def _pick(n, prefs):
    for p in prefs:
        if n % p == 0:
            return p
    return n


def _matmul(a, b, *, name, nt=False, add=None, tm=1024, tn=1024, tk=None, out_dtype=f32, cols=None, exchange=None):
    m, kd = a.shape
    col0, n = cols if cols is not None else (0, b.shape[0] if nt else b.shape[1])
    tm = _pick(m, (tm, 512, 256))
    tn = _pick(n, (tn, 1024, 512, 128))
    tk = kd if tk is None else tk
    nk = kd // tk
    assert nk == 1 or out_dtype == f32
    assert col0 % tn == 0 and not (nt and cols)
    j0 = col0 // tn
    dims = (((1,), (1,)), ((), ())) if nt else (((1,), (0,)), ((), ()))

    def body(a_ref, b_ref, *rest):
        o_ref = rest[-1]
        part = lax.dot_general(a_ref[...], b_ref[...], dims, preferred_element_type=f32)
        if nk == 1:
            o_ref[...] = (part if add is None else part + rest[0][...]).astype(out_dtype)
            return
        k = pl.program_id(2)

        @pl.when(k == 0)
        def _():
            o_ref[...] = part if add is None else part + rest[0][...]

        @pl.when(k > 0)
        def _():
            o_ref[...] += part

    b_spec = pl.BlockSpec((tn, tk), lambda i, j, k: (j, k)) if nt else pl.BlockSpec((tk, tn), lambda i, j, k: (k, j + j0))
    in_specs = [pl.BlockSpec((tm, tk), lambda i, j, k: (i, k)), b_spec]
    args = [a, b]
    if add is not None:
        in_specs.append(pl.BlockSpec((tm, tn), lambda i, j, k: (i, j)))
        args.append(add)
    vmem = _vmem_for(2 * tm * tk, 2 * tk * tn, tm * tn * jnp.dtype(out_dtype).itemsize,
                     4 * tm * tn if add is not None else 0, extra=4 * tm * tn)
    return _call_with_exchange(
        body, exchange, name=name, grid=(m // tm, n // tn, nk), in_specs=in_specs,
        out_specs=pl.BlockSpec((tm, tn), lambda i, j, k: (i, j)),
        out_shape=jax.ShapeDtypeStruct((m, n), out_dtype), args=args, vmem_bytes=vmem)


def _call_with_exchange(body, exchange, *, name, grid, in_specs, out_specs, out_shape, args, vmem_bytes=None):
    if exchange is None:
        return pl.pallas_call(body, name=name, grid=grid, in_specs=in_specs, out_specs=out_specs, out_shape=out_shape,
                              compiler_params=_params(len(grid), vmem_bytes))(*args)
    x_arrays, x_shapes, x_sems, start, wait = exchange[:5]
    n_in, n_xin, n_xout = len(args), len(x_arrays), len(x_shapes)
    aliases = {n_in + i: 1 + i for i in range(n_xin)} if len(exchange) > 5 and exchange[5] else {}

    def full_body(*refs):
        c_in, x_in = refs[:n_in], refs[n_in:n_in + n_xin]
        c_out = refs[n_in + n_xin]
        x_out = refs[n_in + n_xin + 1:n_in + n_xin + 1 + n_xout]
        sems = refs[n_in + n_xin + 1 + n_xout:]
        ids = [pl.program_id(d) for d in range(len(grid))]
        first, last = ids[0] == 0, ids[0] == grid[0] - 1
        for d in range(1, len(grid)):
            first = first & (ids[d] == 0)
            last = last & (ids[d] == grid[d] - 1)

        @pl.when(first)
        def _():
            start(x_in, x_out, sems)

        body(*c_in, c_out)

        @pl.when(last)
        def _():
            wait(x_in, x_out, sems)

    res = pl.pallas_call(
        full_body, name=name, grid=grid, in_specs=list(in_specs) + [_ANY] * n_xin,
        out_specs=[out_specs] + [_ANY] * n_xout, out_shape=[out_shape] + list(x_shapes),
        scratch_shapes=list(x_sems), input_output_aliases=aliases, compiler_params=_params(len(grid), vmem_bytes),
    )(*args, *x_arrays)
    return res[0], list(res[1:])


def _matmul_tn(a, b, *, name, tm=1024, tn=1024, tt=2048, exchange=None):
    t, m = a.shape
    _, n = b.shape
    tm = _pick(m, (tm, 1024, 512, 128))
    tn = _pick(n, (tn, 1024, 512, 128))
    tt = _pick(t, (tt, 2048, 1024, 512, 256))
    nt = t // tt

    def body(a_ref, b_ref, o_ref):
        k = pl.program_id(2)
        part = lax.dot_general(a_ref[...], b_ref[...], (((0,), (0,)), ((), ())), preferred_element_type=f32)

        @pl.when(k == 0)
        def _():
            o_ref[...] = part

        @pl.when(k > 0)
        def _():
            o_ref[...] += part

    return _call_with_exchange(
        body, exchange, name=name, grid=(m // tm, n // tn, nt),
        in_specs=[pl.BlockSpec((tt, tm), lambda i, j, k: (k, i)), pl.BlockSpec((tt, tn), lambda i, j, k: (k, j))],
        out_specs=pl.BlockSpec((tm, tn), lambda i, j, k: (i, j)),
        out_shape=jax.ShapeDtypeStruct((m, n), f32), args=[a, b],
        vmem_bytes=_vmem_for(2 * tt * tm, 2 * tt * tn, 4 * tm * tn, extra=4 * tm * tn + 2 * tt * tm))


def _rms_fwd(x, g, *, name, exchange=None):
    t = x.shape[0]
    tb = _pick(t, (256, 128))

    def body(x_ref, g_ref, h_ref):
        xv = x_ref[...]
        r = lax.rsqrt(jnp.mean(xv * xv, axis=-1, keepdims=True) + EPS)
        h_ref[...] = (xv * r * g_ref[...]).astype(bf16)

    return _call_with_exchange(
        body, exchange, name=name, grid=(t // tb,), in_specs=[_row(tb, D), _fixed((1, D))], out_specs=_row(tb, D),
        out_shape=jax.ShapeDtypeStruct((t, D), bf16), args=[x, g])


def _rms_bwd(dh, x, g, dres, *, name, more=None, bf16_copy=True):
    t = x.shape[0]
    tb = _pick(t, (256, 128))

    def body(dh_ref, x_ref, g_ref, dres_ref, *rest):
        dx_ref, dg_ref = rest[-3 if bf16_copy else -2], rest[-1]
        xv = x_ref[...]
        r = lax.rsqrt(jnp.mean(xv * xv, axis=-1, keepdims=True) + EPS)
        xh = xv * r
        dy = dh_ref[...]
        if more is not None:
            dy = dy + lax.dot_general(rest[0][...], rest[1][...], (((1,), (1,)), ((), ())), preferred_element_type=f32)
        dyg = dy * g_ref[...]
        dx = dres_ref[...] + r * (dyg - xh * jnp.mean(dyg * xh, axis=-1, keepdims=True))
        dx_ref[...] = dx
        if bf16_copy:
            rest[-2][...] = dx.astype(bf16)

        @pl.when(pl.program_id(0) == 0)
        def _():
            dg_ref[...] = jnp.zeros_like(dg_ref)

        dg_ref[...] += jnp.sum((dy * xh).reshape(tb // 8, 8, D), axis=0)

    in_specs, args = [_row(tb, D), _row(tb, D), _fixed((1, D)), _row(tb, D)], [dh, x, g, dres]
    if more is not None:
        in_specs += [_row(tb, 128), _fixed(more[1].shape)]
        args += list(more)
    dx_dtypes = (f32, bf16) if bf16_copy else (f32,)
    return pl.pallas_call(
        body, name=name, grid=(t // tb,), in_specs=in_specs,
        out_specs=[_row(tb, D) for _ in dx_dtypes] + [_fixed((8, D))],
        out_shape=[jax.ShapeDtypeStruct((t, D), dt) for dt in dx_dtypes] + [jax.ShapeDtypeStruct((8, D), f32)],
        compiler_params=_params(1),
    )(*args)


def _gdn_gates(ab, alog, dtb):
    lane = lax.broadcasted_iota(jnp.int32, ab.shape, 1)
    g = -jnp.exp(alog) * _softplus(ab + dtb)
    beta = _sigmoid(ab)
    return jnp.where(lane < H, g, jnp.where(lane < 2 * H, beta, 0.0))


def _pre_fwd(pg, pq, h1, w2, wa, wg, alog, dtb):
    t = pg.shape[0]
    tb = 128

    def body(p0_ref, p0h_ref, pq_ref, pqh_ref, h1_ref, w2_ref, wa_ref, wg_ref, alog_ref, dtb_ref,
             ya_ref, qn_ref, kn_ref, vc_ref, gb_ref, p2_ref):
        first = pl.program_id(0) == 0
        p2_ref[...] = jnp.dot(h1_ref[...], w2_ref[...], preferred_element_type=f32)
        for i in range(D // 128):
            sl, cg, xv = _strip(i), _strip(i, D), _strip(i, 2 * D)
            taps = _strip_taps(_f32(p0_ref, cg) * _f32(p0_ref, xv), _halo_before(p0h_ref, cg) * _halo_before(p0h_ref, xv),
                               first, 3)
            ya_ref[:, sl] = (_f32(p0_ref, sl) * _strip_conv(wa_ref, sl, taps)).astype(bf16)
        for part, out_ref, scale in ((0, qn_ref, DH ** -0.5), (1, kn_ref, 1.0), (2, vc_ref, None)):
            for h in range(H):
                sl = _strip(h, part * D)
                s = _strip_conv(wg_ref, sl, _strip_taps(pq_ref[:, sl], pqh_ref[:, sl], first, 4))
                s = s * _sigmoid(s)
                if scale is not None:
                    s = s * (lax.rsqrt(jnp.sum(s * s, axis=-1, keepdims=True) + EPS) * scale)
                out_ref[:, _strip(h)] = s
        gb_ref[...] = _gdn_gates(p2_ref[...], alog_ref[...], dtb_ref[...])

    return pl.pallas_call(
        body, name="pre_fwd", grid=(t // tb,),
        in_specs=[_row(tb, 3 * D, 0), _prev(tb, 3 * D, 0, rows=16), _row(tb, 3 * D), _prev(tb, 3 * D), _row(tb, D),
                  _fixed((D, 128)), _fixed((8, D)), _fixed((8, 3 * D)), _fixed((1, 128)), _fixed((1, 128))],
        out_specs=[_row(tb, D), _row(tb, D), _row(tb, D), _row(tb, D), _row(tb, 128), _row(tb, 128)],
        out_shape=[jax.ShapeDtypeStruct((t, D), bf16), jax.ShapeDtypeStruct((t, D), f32),
                   jax.ShapeDtypeStruct((t, D), f32), jax.ShapeDtypeStruct((t, D), f32),
                   jax.ShapeDtypeStruct((t, 128), f32), jax.ShapeDtypeStruct((t, 128), f32)],
        compiler_params=_params(1),
    )(pg, pg, pq, pq, h1, w2, wa, wg, alog, dtb)


_Z_COL, _GA_COL, _GB_COL = 3, 4, 5


def _post_fwd(o, pg, gn):
    t = o.shape[0]
    tb = _pick(t, (256, 128))

    def body(o_ref, z_ref, gn_ref, yb_ref):
        for h in range(H):
            sl = slice(h * DH, (h + 1) * DH)
            oh = o_ref[:, sl]
            z = _f32(z_ref, sl)
            r = lax.rsqrt(jnp.mean(oh * oh, axis=-1, keepdims=True) + EPS)
            yb_ref[:, sl] = (oh * r * gn_ref[...] * (z * _sigmoid(z))).astype(bf16)

    return pl.pallas_call(
        body, name="post_fwd", grid=(t // tb,), in_specs=[_row(tb, D), _row(tb, D, _Z_COL), _fixed((1, DH))],
        out_specs=_row(tb, D), out_shape=jax.ShapeDtypeStruct((t, D), bf16), compiler_params=_params(1),
    )(o, pg, gn)


def _post_bwd(dyb, o, pg, gn):
    t = o.shape[0]
    tb = _pick(t, (256, 128))

    def body(dyb_ref, o_ref, z_ref, gn_ref, do_ref, dz_ref, dgn_ref):
        @pl.when(pl.program_id(0) == 0)
        def _():
            dgn_ref[...] = jnp.zeros_like(dgn_ref)

        gn_v = gn_ref[...]
        acc = jnp.zeros((8, DH), f32)
        for h in range(H):
            sl = slice(h * DH, (h + 1) * DH)
            oh = o_ref[:, sl]
            z = _f32(z_ref, sl)
            dy = dyb_ref[:, sl]
            r = lax.rsqrt(jnp.mean(oh * oh, axis=-1, keepdims=True) + EPS)
            on = oh * r
            sg = _sigmoid(z)
            sz = z * sg
            don = dy * sz
            dz_ref[:, sl] = (dy * on * gn_v * (sg * (1.0 + z * (1.0 - sg)))).astype(bf16)
            acc = acc + jnp.sum((don * on).reshape(tb // 8, 8, DH), axis=0)
            doh = don * gn_v
            do_ref[:, sl] = r * (doh - on * jnp.mean(doh * on, axis=-1, keepdims=True))
        dgn_ref[...] += acc

    return pl.pallas_call(
        body, name="post_bwd", grid=(t // tb,),
        in_specs=[_row(tb, D), _row(tb, D), _row(tb, D, _Z_COL), _fixed((1, DH))],
        out_specs=[_row(tb, D), _row(tb, D), _fixed((8, DH))],
        out_shape=[jax.ShapeDtypeStruct((t, D), f32), jax.ShapeDtypeStruct((t, D), bf16),
                   jax.ShapeDtypeStruct((8, DH), f32)],
        compiler_params=_params(1),
    )(dyb, o, pg, gn)


def _mix_fwd(ya, yb, pg):
    t = ya.shape[0]
    tb = _pick(t, (256, 128))

    def body(ya_ref, yb_ref, ga_ref, gb_ref, mix_ref):
        ya_v, yb_v = ya_ref[...].astype(f32), yb_ref[...].astype(f32)
        mix = _sigmoid(ga_ref[...].astype(f32)) * ya_v + _sigmoid(gb_ref[...].astype(f32)) * yb_v
        mix_ref[...] = mix.astype(bf16)

    return pl.pallas_call(
        body, name="mix_fwd", grid=(t // tb,),
        in_specs=[_row(tb, D), _row(tb, D), _row(tb, D, _GA_COL), _row(tb, D, _GB_COL)],
        out_specs=_row(tb, D), out_shape=jax.ShapeDtypeStruct((t, D), bf16), compiler_params=_params(1),
    )(ya, yb, pg, pg)


def _mix_bwd(dmix, ya, yb, pg):
    t = ya.shape[0]
    tb = _pick(t, (256, 128))

    def body(dm_ref, ya_ref, yb_ref, ga_ref, gb_ref, dya_ref, dyb_ref, dg_ref):
        dm = dm_ref[...].astype(f32)
        sa = _sigmoid(ga_ref[...].astype(f32))
        sb = _sigmoid(gb_ref[...].astype(f32))
        dya_ref[...] = (dm * sa).astype(bf16)
        dyb_ref[...] = (dm * sb).astype(bf16)
        dg_ref[:, :D] = (dm * ya_ref[...].astype(f32) * sa * (1.0 - sa)).astype(bf16)
        dg_ref[:, D:] = (dm * yb_ref[...].astype(f32) * sb * (1.0 - sb)).astype(bf16)

    return pl.pallas_call(
        body, name="mix_bwd", grid=(t // tb,),
        in_specs=[_row(tb, D), _row(tb, D), _row(tb, D), _row(tb, D, _GA_COL), _row(tb, D, _GB_COL)],
        out_specs=[_row(tb, D), _row(tb, D), _row(tb, 2 * D)],
        out_shape=[jax.ShapeDtypeStruct((t, D), bf16), jax.ShapeDtypeStruct((t, D), bf16),
                   jax.ShapeDtypeStruct((t, 2 * D), bf16)],
        compiler_params=_params(1),
    )(dmix, ya, yb, pg, pg)


def _ffn_fwd(up, wf):
    t = up.shape[0]
    tb = 128

    def body(up_ref, uph_ref, wf_ref, act_ref):
        first = pl.program_id(0) == 0
        for i in range(DFF // 128):
            g, v = _strip(i), _strip(i, DFF)
            gate = _strip_conv(wf_ref, g, _strip_taps(_f32(up_ref, g), _halo_before(uph_ref, g), first, 3))
            val = _strip_conv(wf_ref, v, _strip_taps(_f32(up_ref, v), _halo_before(uph_ref, v), first, 3))
            act_ref[:, g] = (gate * _sigmoid(gate) * val).astype(bf16)

    return pl.pallas_call(
        body, name="ffn_fwd", grid=(t // tb,),
        in_specs=[_row(tb, 2 * DFF), _prev(tb, 2 * DFF, rows=16), _fixed((8, 2 * DFF))],
        out_specs=_row(tb, DFF), out_shape=jax.ShapeDtypeStruct((t, DFF), bf16), compiler_params=_params(1),
    )(up, up, wf)


def _ffn_bwd1(dact, up, wf):
    t = up.shape[0]
    tb = 128

    def body(da_ref, up_ref, uph_ref, wf_ref, dc_ref, dw_ref):
        @pl.when(pl.program_id(0) == 0)
        def _():
            dw_ref[...] = jnp.zeros_like(dw_ref)

        first = pl.program_id(0) == 0
        for i in range(DFF // 128):
            g, v = _strip(i), _strip(i, DFF)
            g_taps = _strip_taps(_f32(up_ref, g), _halo_before(uph_ref, g), first, 3)
            v_taps = _strip_taps(_f32(up_ref, v), _halo_before(uph_ref, v), first, 3)
            gate = _strip_conv(wf_ref, g, g_taps)
            val = _strip_conv(wf_ref, v, v_taps)
            sg = _sigmoid(gate)
            da = _f32(da_ref, g)
            dgate = da * val * (sg * (1.0 + gate * (1.0 - sg)))
            dval = da * (gate * sg)
            dc_ref[:, g] = dgate.astype(bf16)
            dc_ref[:, v] = dval.astype(bf16)
            _strip_weight_grad(dw_ref, g, dgate, g_taps)
            _strip_weight_grad(dw_ref, v, dval, v_taps)

    return pl.pallas_call(
        body, name="ffn_bwd1", grid=(t // tb,),
        in_specs=[_row(tb, DFF), _row(tb, 2 * DFF), _prev(tb, 2 * DFF, rows=16), _fixed((8, 2 * DFF))],
        out_specs=[_row(tb, 2 * DFF), _fixed((8, 2 * DFF))],
        out_shape=[jax.ShapeDtypeStruct((t, 2 * DFF), bf16), jax.ShapeDtypeStruct((8, 2 * DFF), f32)],
        compiler_params=_params(1),
    )(dact, up, up, wf)


def _ffn_bwd2(dc, wf):
    t = dc.shape[0]
    tb = 128
    nb = t // tb

    def body(dc_ref, dch_ref, wf_ref, dup_ref):
        last = pl.program_id(0) == nb - 1
        for i in range(2 * DFF // 128):
            sl = _strip(i)
            dup_ref[:, sl] = _strip_conv_up(_f32(dc_ref, sl), _halo_after(dch_ref, sl), last, wf_ref, sl, 3).astype(bf16)

    return pl.pallas_call(
        body, name="ffn_bwd2", grid=(nb,),
        in_specs=[_row(tb, 2 * DFF), _next(tb, 2 * DFF, t, rows=16), _fixed((8, 2 * DFF))],
        out_specs=_row(tb, 2 * DFF), out_shape=jax.ShapeDtypeStruct((t, 2 * DFF), bf16), compiler_params=_params(1),
    )(dc, dc, wf)


def _final(x3, tgt, g):
    t = x3.shape[0]
    tb = _pick(t, (256, 128))

    def body(x_ref, t_ref, g_ref, loss_ref, dx_ref, dxb_ref, dg_ref):
        @pl.when(pl.program_id(0) == 0)
        def _():
            loss_ref[...] = jnp.zeros_like(loss_ref)
            dg_ref[...] = jnp.zeros_like(dg_ref)

        xv = x_ref[...]
        r = lax.rsqrt(jnp.mean(xv * xv, axis=-1, keepdims=True) + EPS)
        xh = xv * r
        gv = g_ref[...]
        e = xh * gv - t_ref[...]
        lrow = 0.5 * jnp.mean(e * e, axis=-1, keepdims=True)
        loss_ref[...] += jnp.sum(jnp.broadcast_to(lrow, (tb, 128)).reshape(tb // 8, 8, 128), axis=0)
        dy = e * (1.0 / D)
        dyg = dy * gv
        dx = r * (dyg - xh * jnp.mean(dyg * xh, axis=-1, keepdims=True))
        dx_ref[...] = dx
        dxb_ref[...] = dx.astype(bf16)
        dg_ref[...] += jnp.sum((dy * xh).reshape(tb // 8, 8, D), axis=0)

    return pl.pallas_call(
        body, name="final", grid=(t // tb,), in_specs=[_row(tb, D), _row(tb, D), _fixed((1, D))],
        out_specs=[_fixed((8, 128)), _row(tb, D), _row(tb, D), _fixed((8, D))],
        out_shape=[jax.ShapeDtypeStruct((8, 128), f32), jax.ShapeDtypeStruct((t, D), f32),
                   jax.ShapeDtypeStruct((t, D), bf16), jax.ShapeDtypeStruct((8, D), f32)],
        compiler_params=_params(1),
    )(x3, tgt, g)


def _pre_bwd1(pg, pq, p2, dya_in, dqn, dkn, dvc, dgb, gbeta, h1, wa, wg, alog, dtb):
    t = pg.shape[0]
    tb = 128

    def body(p0_ref, p0h_ref, pq_ref, pqh_ref, p2_ref, dya_ref, dqn_ref, dkn_ref, dvc_ref, dgb_ref, gb_ref, h1_ref,
             wa_ref, wg_ref, alog_ref, dtb_ref,
             dbg_ref, dca_ref, dc4_ref, dp2_ref, dwa_ref, dwg_ref, dal_ref, ddt_ref, dw2_ref):
        @pl.when(pl.program_id(0) == 0)
        def _():
            dwa_ref[...] = jnp.zeros_like(dwa_ref)
            dwg_ref[...] = jnp.zeros_like(dwg_ref)
            dal_ref[...] = jnp.zeros_like(dal_ref)
            ddt_ref[...] = jnp.zeros_like(ddt_ref)
            dw2_ref[...] = jnp.zeros_like(dw2_ref)

        first = pl.program_id(0) == 0

        for i in range(D // 128):
            sl, cg, xv = _strip(i), _strip(i, D), _strip(i, 2 * D)
            taps = _strip_taps(_f32(p0_ref, cg) * _f32(p0_ref, xv), _halo_before(p0h_ref, cg) * _halo_before(p0h_ref, xv),
                               first, 3)
            dya = _f32(dya_ref, sl)
            dbg_ref[:, sl] = (dya * _strip_conv(wa_ref, sl, taps)).astype(bf16)
            dca = dya * _f32(p0_ref, sl)
            dca_ref[:, sl] = dca.astype(bf16)
            _strip_weight_grad(dwa_ref, sl, dca, taps)

        for part, d_ref, scale in ((0, dqn_ref, DH ** -0.5), (1, dkn_ref, 1.0), (2, dvc_ref, None)):
            for h in range(H):
                sl = _strip(h, part * D)
                taps = _strip_taps(pq_ref[:, sl], pqh_ref[:, sl], first, 4)
                c4 = _strip_conv(wg_ref, sl, taps)
                sg = _sigmoid(c4)
                dn = d_ref[:, _strip(h)]
                if scale is not None:
                    a = c4 * sg
                    r = lax.rsqrt(jnp.sum(a * a, axis=-1, keepdims=True) + EPS)
                    an = a * r
                    dn = dn * scale
                    dn = r * (dn - an * jnp.sum(dn * an, axis=-1, keepdims=True))
                dc4 = dn * (sg * (1.0 + c4 * (1.0 - sg)))
                dc4_ref[:, sl] = dc4.astype(bf16)
                _strip_weight_grad(dwg_ref, sl, dc4, taps)

        ab = p2_ref[...]
        lane = lax.broadcasted_iota(jnp.int32, ab.shape, 1)
        dgbv = dgb_ref[...]
        gbv = gb_ref[...]
        da = dgbv * (-jnp.exp(alog_ref[...])) * _sigmoid(ab + dtb_ref[...])
        db = dgbv * gbv * (1.0 - gbv)
        dp2 = jnp.where(lane < H, da, jnp.where(lane < 2 * H, db, 0.0)).astype(bf16)
        dp2_ref[...] = dp2
        dw2_ref[...] += lax.dot_general(dp2, h1_ref[...], (((0,), (0,)), ((), ())), preferred_element_type=f32)
        dal = jnp.where(lane < H, dgbv * gbv, 0.0)
        ddt = jnp.where(lane < H, da, 0.0)
        dal_ref[...] += jnp.sum(dal.reshape(tb // 8, 8, 128), axis=0)
        ddt_ref[...] += jnp.sum(ddt.reshape(tb // 8, 8, 128), axis=0)

    return pl.pallas_call(
        body, name="pre_bwd1", grid=(t // tb,),
        in_specs=[_row(tb, 3 * D, 0), _prev(tb, 3 * D, 0, rows=16), _row(tb, 3 * D), _prev(tb, 3 * D), _row(tb, 128),
                  _row(tb, D), _row(tb, D), _row(tb, D), _row(tb, D), _row(tb, 128), _row(tb, 128), _row(tb, D),
                  _fixed((8, D)), _fixed((8, 3 * D)), _fixed((1, 128)), _fixed((1, 128))],
        out_specs=[_row(tb, D), _row(tb, D), _row(tb, 3 * D), _row(tb, 128),
                   _fixed((8, D)), _fixed((8, 3 * D)), _fixed((8, 128)), _fixed((8, 128)), _fixed((128, D))],
        out_shape=[jax.ShapeDtypeStruct((t, D), bf16), jax.ShapeDtypeStruct((t, D), bf16),
                   jax.ShapeDtypeStruct((t, 3 * D), bf16), jax.ShapeDtypeStruct((t, 128), bf16),
                   jax.ShapeDtypeStruct((8, D), f32), jax.ShapeDtypeStruct((8, 3 * D), f32),
                   jax.ShapeDtypeStruct((8, 128), f32), jax.ShapeDtypeStruct((8, 128), f32),
                   jax.ShapeDtypeStruct((128, D), f32)],
        compiler_params=_params(1),
    )(pg, pg, pq, pq, p2, dya_in, dqn, dkn, dvc, dgb, gbeta, h1, wa, wg, alog, dtb)


def _pre_bwd2(dca, dc4, pg, dbg, dz, dgates, wa, wg, exchange=None):
    t = pg.shape[0]
    tb = 128
    nb = t // tb

    def body(dca_ref, dcah_ref, dc4_ref, dc4h_ref, p0_ref, dbg_ref, dz_ref, dgt_ref, wa_ref, wg_ref, dp_ref):
        last = pl.program_id(0) == nb - 1
        dp_ref[:, :D] = dbg_ref[...]
        for i in range(D // 128):
            sl, cg, xv = _strip(i), _strip(i, D), _strip(i, 2 * D)
            du = _strip_conv_up(_f32(dca_ref, sl), _halo_after(dcah_ref, sl), last, wa_ref, sl, 3)
            dp_ref[:, cg] = (du * _f32(p0_ref, xv)).astype(bf16)
            dp_ref[:, xv] = (du * _f32(p0_ref, cg)).astype(bf16)
        dp_ref[:, 3 * D:4 * D] = dz_ref[...]
        dp_ref[:, 4 * D:6 * D] = dgt_ref[...]
        for i in range(3 * D // 128):
            sl = _strip(i)
            dq = _strip_conv_up(_f32(dc4_ref, sl), _halo_after(dc4h_ref, sl), last, wg_ref, sl, 4)
            dp_ref[:, _strip(i, 6 * D)] = dq.astype(bf16)

    return _call_with_exchange(
        body, exchange, name="pre_bwd2", grid=(nb,),
        in_specs=[_row(tb, D), _next(tb, D, t, rows=16), _row(tb, 3 * D), _next(tb, 3 * D, t, rows=16), _row(tb, 3 * D, 0),
                  _row(tb, D), _row(tb, D), _row(tb, 2 * D), _fixed((8, D)), _fixed((8, 3 * D))],
        out_specs=_row(tb, NW1), out_shape=jax.ShapeDtypeStruct((t, NW1), bf16),
        args=[dca, dca, dc4, dc4, pg, dbg, dz, dgates, wa, wg])


def _chunk_consts():
    r = lax.broadcasted_iota(jnp.int32, (CH, CH), 0)
    c = lax.broadcasted_iota(jnp.int32, (CH, CH), 1)
    return r, c, (r == c).astype(f32)


def _tri_inverse(lows, eye, r, c):
    def same_block(b):
        return jnp.bitwise_xor(r, c) < b

    xs = [jnp.where(same_block(8), -low, 0.0) for low in lows]
    ts = [eye + x for x in xs]
    for _ in range(2):
        xs = [_idot(x, x) for x in xs]
        ts = [t + _idot(t, x) for t, x in zip(ts, xs)]
    for b in (8, 16, 32):
        below = same_block(2 * b) & jnp.logical_not(same_block(b))
        ts = [t - _idot(_idot(t, jnp.where(below, low, 0.0)), t) for t, low in zip(ts, lows)]
    return ts


def _chunk_common(q, k, v, gcol, bcol, r, c, eye):
    grow = jnp.sum(eye * gcol, axis=0, keepdims=True)
    dec = jnp.exp(jnp.where(r >= c, gcol - grow, -jnp.inf))
    rcol = lax.broadcasted_iota(jnp.int32, (CH, 1), 0)
    glast = jnp.sum(jnp.where(rcol == CH - 1, gcol, 0.0), axis=0, keepdims=True)
    eg = jnp.exp(gcol)
    el = jnp.exp(glast - gcol)
    kb = k * bcol
    vb = v * bcol
    kk = _bdot_nt(kb, k)
    low = jnp.where(r > c, kk * dec, 0.0)
    qk = _bdot_nt(q, k)
    att = qk * dec
    return grow, dec, glast, eg, el, kb, vb, kk, low, qk, att, rcol


def _gdn_fwd(qn, kn, vc, gbeta):
    t = qn.shape[0]
    n_chunks = t // CH

    def body(q_ref, k_ref, v_ref, gb_ref, o_ref, s_ref, t_ref, state):
        @pl.when(pl.program_id(0) == 0)
        def _():
            state[...] = jnp.zeros_like(state)

        r, c, eye = _chunk_consts()
        tri = (r >= c).astype(f32)
        heads = range(H)
        keys = [(s, h) for s in range(GDN_STEP) for h in heads]
        rows = [slice(s * CH, (s + 1) * CH) for s in range(GDN_STEP)]
        gbs = [gb_ref[rows[s], :] for s in range(GDN_STEP)]
        galls = [_hdot(tri, gb) for gb in gbs]
        qs = {(s, h): q_ref[rows[s], h * DH:(h + 1) * DH] for s, h in keys}
        ks = {(s, h): k_ref[rows[s], h * DH:(h + 1) * DH] for s, h in keys}
        cm = {(s, h): _chunk_common(qs[s, h], ks[s, h], v_ref[rows[s], h * DH:(h + 1) * DH], galls[s][:, h:h + 1],
                                    gbs[s][:, H + h:H + h + 1], r, c, eye) for s, h in keys}
        invs = dict(zip(keys, _tri_inverse([cm[key][8] for key in keys], eye, r, c)))
        uws = {key: _bdot(invs[key], jnp.concatenate([cm[key][6], cm[key][5] * cm[key][3]], axis=1)) for key in keys}
        sts = [state[h] for h in heads]
        for s in range(GDN_STEP):
            vns = [uws[s, h][:, :DH] - _bdot(uws[s, h][:, DH:], sts[h]) for h in heads]
            outs = [_bdot(qs[s, h] * cm[s, h][3], sts[h]) + _bdot(cm[s, h][10], vns[h]) for h in heads]
            news = [sts[h] * jnp.exp(cm[s, h][2]) + _bdot_tn(ks[s, h] * cm[s, h][4], vns[h]) for h in heads]
            for h in heads:
                s_ref[s, h] = sts[h].astype(bf16)
                t_ref[s, h] = invs[s, h]
                o_ref[rows[s], h * DH:(h + 1) * DH] = outs[h]
            sts = news
        for h in heads:
            state[h] = sts[h]

    tb = GDN_STEP * CH
    return pl.pallas_call(
        body, name="gdn_fwd", grid=(t // tb,),
        in_specs=[_row(tb, D), _row(tb, D), _row(tb, D), _row(tb, 128)],
        out_specs=[_row(tb, D), pl.BlockSpec((GDN_STEP, H, DH, DH), lambda i: (i, 0, 0, 0)),
                   pl.BlockSpec((GDN_STEP, H, CH, CH), lambda i: (i, 0, 0, 0))],
        out_shape=[jax.ShapeDtypeStruct((t, D), f32), jax.ShapeDtypeStruct((n_chunks, H, DH, DH), bf16),
                   jax.ShapeDtypeStruct((n_chunks, H, CH, CH), f32)],
        scratch_shapes=[pltpu.VMEM((H, DH, DH), f32)],
        compiler_params=_params(1),
    )(qn, kn, vc, gbeta)


def _gdn_bwd(qn, kn, vc, gbeta, do, s_all, t_all):
    t = qn.shape[0]

    def body(q_ref, k_ref, v_ref, gb_ref, do_ref, s_ref, t_ref, dq_ref, dk_ref, dv_ref, dgb_ref, dstate):
        @pl.when(pl.program_id(0) == 0)
        def _():
            dstate[...] = jnp.zeros_like(dstate)

        r, c, eye = _chunk_consts()
        tril = r >= c
        lane = lax.broadcasted_iota(jnp.int32, (1, 128), 1)
        hs = range(H)

        def each(fn, *lists):
            return [fn(*args) for args in zip(*lists)]

        def rsum(a):
            return jnp.sum(a, axis=1, keepdims=True)

        def before_state(s):
            rows = slice(s * CH, (s + 1) * CH)
            gb = gb_ref[rows, :]
            gall = _hdot(tril.astype(f32), gb)
            p = {"rows": rows}
            p["q"] = q = [q_ref[rows, h * DH:(h + 1) * DH] for h in hs]
            p["k"] = k = [k_ref[rows, h * DH:(h + 1) * DH] for h in hs]
            p["v"] = v = [v_ref[rows, h * DH:(h + 1) * DH] for h in hs]
            p["dout"] = dout = [do_ref[rows, h * DH:(h + 1) * DH] for h in hs]
            p["inv"] = inv = [t_ref[s, h] for h in hs]
            p["st"] = st = [s_ref[s, h] for h in hs]
            p["bcol"] = bcol = [gb[:, H + h:H + h + 1] for h in hs]
            cm = [_chunk_common(q[h], k[h], v[h], gall[:, h:h + 1], bcol[h], r, c, eye) for h in hs]
            for name, i in (("dec", 1), ("glast", 2), ("eg", 3), ("el", 4), ("kb", 5), ("vb", 6), ("low", 8), ("att", 10)):
                p[name] = [m[i] for m in cm]
            p["rcol"] = cm[0][11]
            p["elast"] = each(jnp.exp, p["glast"])
            p["kbg"] = each(jnp.multiply, p["kb"], p["eg"])
            uw = each(lambda i, a, b: _bdot(i, jnp.concatenate([a, b], axis=1)), inv, p["vb"], p["kbg"])
            p["u"] = [a[:, :DH] for a in uw]
            p["w"] = [a[:, DH:] for a in uw]
            p["vn"] = each(lambda a, b, x: a - _bdot(b, x), p["u"], p["w"], st)
            p["qd"] = each(jnp.multiply, q, p["eg"])
            p["kd"] = each(jnp.multiply, k, p["el"])
            p["dqd"] = each(_bdot_nt, dout, st)
            p["datt"] = each(lambda d, x: jnp.where(tril, _bdot_nt(d, x), 0.0), dout, p["vn"])
            p["dqk"] = each(jnp.multiply, p["datt"], p["dec"])
            p["qd_do"] = each(_bdot_tn, p["qd"], dout)
            p["att_do"] = each(_bdot_tn, p["att"], dout)
            return p

        def after_state(p, ds):
            q, k, v, st, inv, bcol = p["q"], p["k"], p["v"], p["st"], p["inv"], p["bcol"]
            eg, el, kb, u, w = p["eg"], p["el"], p["kb"], p["u"], p["w"]
            dvn = each(lambda a, kk, x: a + _bdot(kk, x), p["att_do"], p["kd"], ds)
            dkd = each(_bdot_nt, p["vn"], ds)
            dw = each(lambda a, x: -_bdot_nt(a, x), dvn, st)
            new_ds = each(lambda x, e, a, ww, dv_: x * e + a - _bdot_tn(ww, dv_), ds, p["elast"], p["qd_do"], w, dvn)
            dglast = each(lambda e, x, d: e * jnp.sum(rsum(x.astype(f32) * d), axis=0, keepdims=True), p["elast"], st, ds)
            dr = each(lambda i, a, b: _bdot_tn(i, jnp.concatenate([a, b], axis=1)), inv, dvn, dw)
            dvb = [a[:, :DH] for a in dr]
            dkbg = [a[:, DH:] for a in dr]
            dlow = each(lambda a, b, x, y: -jnp.where(r > c, _bdot_nt(a, b) + _bdot_nt(x, y), 0.0), dvb, u, dkbg, w)
            dkk = each(jnp.multiply, dlow, p["dec"])
            mm = each(lambda a, b, x, y: a * b + x * y, dlow, p["low"], p["datt"], p["att"])
            dkb = each(lambda a, kk, b, e: _bdot(a, kk) + b * e, dkk, k, dkbg, eg)
            dk = each(lambda a, b, x, y, d, e, f, g: _bdot_tn(a, b) + _bdot_tn(x, y) + d * e + f * g,
                      dkk, kb, p["dqk"], q, dkd, el, dkb, bcol)
            dq = each(lambda a, kk, d, e: _bdot(a, kk) + d * e, p["dqk"], k, p["dqd"], eg)
            dv = each(jnp.multiply, dvb, bcol)
            dbeta = each(lambda a, b, x, y: rsum(a * b) + rsum(x * y), dkb, k, dvb, v)
            deg = each(lambda a, b, x, y: rsum(a * b) + rsum(x * y), dkbg, kb, p["dqd"], q)
            delc = each(lambda a, b, e: rsum(a * b) * e, dkd, k, el)
            dgc = each(lambda m, a, e, d: rsum(m) - rsum(eye * jnp.sum(m, axis=0, keepdims=True)) + a * e - d,
                       mm, deg, eg, delc)
            dgc = each(lambda g, d, l: g + jnp.where(p["rcol"] == CH - 1, jnp.sum(d, axis=0, keepdims=True) + l, 0.0),
                       dgc, delc, dglast)
            dg_acc = jnp.zeros((CH, 128), f32)
            db_acc = jnp.zeros((CH, 128), f32)
            rows = p["rows"]
            for h in hs:
                dq_ref[rows, h * DH:(h + 1) * DH] = dq[h]
                dk_ref[rows, h * DH:(h + 1) * DH] = dk[h]
                dv_ref[rows, h * DH:(h + 1) * DH] = dv[h]
                dg_acc = dg_acc + dgc[h] * (lane == h).astype(f32)
                db_acc = db_acc + dbeta[h] * (lane == H + h).astype(f32)
            dgb_ref[rows, :] = _hdot((r <= c).astype(f32), dg_acc) + db_acc
            return new_ds

        order = list(reversed(range(GDN_STEP)))
        pre = [before_state(s) for s in order]
        ds = [dstate[h] for h in hs]
        for p in pre:
            ds = after_state(p, ds)
        for h in hs:
            dstate[h] = ds[h]

    tb = GDN_STEP * CH
    n_steps = t // tb
    rev = lambda i: (n_steps - 1 - i, 0)
    rev4 = lambda i: (n_steps - 1 - i, 0, 0, 0)
    return pl.pallas_call(
        body, name="gdn_bwd", grid=(n_steps,),
        in_specs=[pl.BlockSpec((tb, D), rev), pl.BlockSpec((tb, D), rev), pl.BlockSpec((tb, D), rev),
                  pl.BlockSpec((tb, 128), rev), pl.BlockSpec((tb, D), rev),
                  pl.BlockSpec((GDN_STEP, H, DH, DH), rev4), pl.BlockSpec((GDN_STEP, H, CH, CH), rev4)],
        out_specs=[pl.BlockSpec((tb, D), rev), pl.BlockSpec((tb, D), rev), pl.BlockSpec((tb, D), rev),
                   pl.BlockSpec((tb, 128), rev)],
        out_shape=[jax.ShapeDtypeStruct((t, D), f32)] * 3 + [jax.ShapeDtypeStruct((t, 128), f32)],
        scratch_shapes=[pltpu.VMEM((H, DH, DH), f32)],
        compiler_params=_params(1),
    )(qn, kn, vc, gbeta, do, s_all, t_all)


def _pad_rows(w, rows=8):
    return jnp.pad(w, ((0, rows - w.shape[0]), (0, 0)))


_REST = ("w_up", "w_a_out", "w_b_out", "w_o", "w_down")


def _local_step(x, tgt, w, comm=None):
    g1 = w["norm_mix_g"].reshape(1, D)
    if comm is None:
        h1 = _rms_fwd(x, g1, name="rms1_fwd")
    else:
        h1, gathered = _rms_fwd(x, g1, name="rms1_fwd", exchange=comm.gather_first())
        w = {**w, **comm.finish_first(gathered)}
    w1, w2 = w["w1"], w["w2"]
    wa = _pad_rows(w["conv_a_w"])
    wg = _pad_rows(w["gdn_conv_w"])
    wf = _pad_rows(w["ffn_conv_w"])
    alog = jnp.pad(w["gdn_A_log"].reshape(1, H), ((0, 0), (0, 128 - H)))
    dtb = jnp.pad(w["gdn_dt_bias"].reshape(1, H), ((0, 0), (0, 128 - H)))
    g2 = w["norm_ffn_g"].reshape(1, D)
    g3 = w["norm_final_g"].reshape(1, D)
    gn = w["gdn_norm_g"].reshape(1, DH)

    if comm is None:
        pg = _matmul(h1, w1, name="mm_in", cols=(0, 6 * D), out_dtype=bf16)
        pq = _matmul(h1, w1, name="mm_in_qkv", cols=(6 * D, 3 * D))
    else:
        pg, gathered = _matmul(h1, w1, name="mm_in", cols=(0, 6 * D), out_dtype=bf16, exchange=comm.gather_rest())
        pq, gathered = _matmul(h1, w1, name="mm_in_qkv", cols=(6 * D, 3 * D), exchange=_gather_forward_exchange(gathered))
        w = {**w, **comm.finish_gather(gathered)}
    ya_in, qn, kn, vc, gbeta, p2 = _pre_fwd(pg, pq, h1, w2, wa, wg, alog, dtb)
    o, s_all, t_all = _gdn_fwd(qn, kn, vc, gbeta)
    yb_in = _post_fwd(o, pg, gn)
    ya = _matmul(ya_in, w["w_a_out"], name="mm_a", out_dtype=bf16)
    yb = _matmul(yb_in, w["w_b_out"], name="mm_b", out_dtype=bf16)
    mix = _mix_fwd(ya, yb, pg)
    x2 = _matmul(mix, w["w_o"], name="mm_o", add=x)
    h2 = _rms_fwd(x2, g2, name="rms2_fwd")
    up = _matmul(h2, w["w_up"], nt=True, name="mm_up", tn=DFF // 2, out_dtype=bf16)
    act = _ffn_fwd(up, wf)
    x3 = _matmul(act, w["w_down"], name="mm_down", add=x2, tm=512)
    loss_p, dx3, dx3b, dg3 = _final(x3, tgt, g3)

    grads = {"norm_final_g": dg3}
    dact = _matmul(dx3b, w["w_down"], nt=True, name="mm_down_dx", tm=512, tn=DFF, out_dtype=bf16)
    grads["w_down"] = _matmul_tn(act, dx3b, name="mm_down_dw", tm=DFF // 2)
    dc, dwf = _ffn_bwd1(dact, up, wf)
    grads["ffn_conv_w"] = dwf
    dup = _ffn_bwd2(dc, wf)
    dh2 = _matmul(dup, w["w_up"], name="mm_up_dx", tk=DFF)
    grads["w_up"] = _matmul_tn(dup, h2, name="mm_up_dw", tm=DFF // 2)
    dx2, dx2b, dg2 = _rms_bwd(dh2, x2, g2, dx3, name="rms2_bwd")
    grads["norm_ffn_g"] = dg2
    dmix = _matmul(dx2b, w["w_o"], nt=True, name="mm_o_dx", out_dtype=bf16)
    grads["w_o"] = _matmul_tn(mix, dx2b, name="mm_o_dw")
    dya, dyb, dgates = _mix_bwd(dmix, ya, yb, pg)
    dya_in = _matmul(dya, w["w_a_out"], nt=True, name="mm_a_dx", out_dtype=bf16)
    grads["w_a_out"] = _matmul_tn(ya_in, dya, name="mm_a_dw")
    dyb_in = _matmul(dyb, w["w_b_out"], nt=True, name="mm_b_dx")
    grads["w_b_out"] = _matmul_tn(yb_in, dyb, name="mm_b_dw")
    do, dz, dgn = _post_bwd(dyb_in, o, pg, gn)
    grads["gdn_norm_g"] = dgn
    dqn, dkn, dvc, dgb = _gdn_bwd(qn, kn, vc, gbeta, do, s_all, t_all)
    dbg, dca, dc4, dp2, dwa, dwg, dal, ddt, grads["w2"] = _pre_bwd1(pg, pq, p2, dya_in, dqn, dkn, dvc, dgb, gbeta, h1,
                                                                    wa, wg, alog, dtb)
    grads["conv_a_w"] = dwa
    grads["gdn_conv_w"] = dwg
    grads["gdn_A_log"] = dal
    grads["gdn_dt_bias"] = ddt
    if comm is None:
        dp1 = _pre_bwd2(dca, dc4, pg, dbg, dz, dgates, wa, wg)
        grads["w1"] = _matmul_tn(dp1, h1, name="mm_in_dw", tt=4096)
        dh1 = _matmul(dp1, w1, nt=True, name="mm_in_dx", tm=512, tk=NW1 // 2)
    else:
        exchange, blocks = comm.reduce_halves(_REST, grads)
        dp1, recv = _pre_bwd2(dca, dc4, pg, dbg, dz, dgates, wa, wg, exchange=exchange)
        exchange, sums = comm.reduce_sums(_REST, blocks, recv)
        grads["w1"], recv = _matmul_tn(dp1, h1, name="mm_in_dw", tt=4096, exchange=exchange)
        comm.finish_reduce(_REST, sums, recv)
        exchange, blocks = comm.reduce_halves(("w_in",), grads)
        exchange, sums = comm.reduce_sums(("w_in",), blocks, _run_exchange(exchange, name="rs_sibling_w_in"))
        dh1, recv = _matmul(dp1, w1, nt=True, name="mm_in_dx", tm=512, tk=NW1 // 2, exchange=exchange)
        comm.finish_reduce(("w_in",), sums, recv)
    dx, dg1 = _rms_bwd(dh1, x, g1, dx2, name="rms1_bwd", more=(dp2, w2), bf16_copy=False)
    grads["norm_mix_g"] = dg1
    return loss_p, dx, grads


_ANY = pl.BlockSpec(memory_space=pl.ANY)


def _remote(src, dst, send_sem, recv_sem, to):
    return pltpu.make_async_remote_copy(src_ref=src, dst_ref=dst, send_sem=send_sem, recv_sem=recv_sem,
                                        device_id=to, device_id_type=MESH)


def _run_exchange(exchange, *, name):
    arrays, shapes, sems, start, wait = exchange
    n_in, n_out = len(arrays), len(shapes)

    def body(*refs):
        start(refs[:n_in], refs[n_in:n_in + n_out], refs[n_in + n_out:])
        wait(refs[:n_in], refs[n_in:n_in + n_out], refs[n_in + n_out:])

    return pl.pallas_call(body, name=name, out_shape=list(shapes), in_specs=[_ANY] * n_in, out_specs=[_ANY] * n_out,
                          scratch_shapes=list(sems))(*arrays)


def _gather_exchange(shards):
    n = len(shards)

    def copies(x_refs, out_refs, sems):
        send_sems, recv_sems, local_sems = sems
        x, y, c = lax.axis_index("x"), lax.axis_index("y"), lax.axis_index("c")

        def flip(v, b):
            return v + b - 2 * v * b

        me, sibling = (x, y, c), (x, y, 1 - c)
        chip1, chip2, diag = (flip(x, 1 - c), flip(y, c)), (flip(x, c), flip(y, 1 - c)), (1 - x, 1 - y)

        def copy(a, k, blk, to, from_input=False):
            dst = out_refs[a].at[4 * blk[0] + 2 * blk[1] + blk[2]]
            return _remote(x_refs[a] if from_input else dst, dst, send_sems.at[a, k], recv_sems.at[a, k], to)

        mine = [pltpu.make_async_copy(x_refs[a], out_refs[a].at[4 * x + 2 * y + c], local_sems.at[a]) for a in range(n)]
        first = []
        for a in range(n):
            first += [copy(a, 0, me, sibling, from_input=True), copy(a, 1, me, (*chip1, c), from_input=True),
                      copy(a, 2, me, (*chip2, c), from_input=True)]
        return copy, mine, first, me, sibling, chip1, chip2, diag, c

    def start(x_refs, out_refs, sems):
        _, mine, first, *_ = copies(x_refs, out_refs, sems)
        for cp in mine + first:
            cp.start()

    def wait(x_refs, out_refs, sems):
        copy, mine, first, me, sibling, chip1, chip2, diag, c = copies(x_refs, out_refs, sems)
        passed = []

        def pass_on(cp):
            passed.append(cp)
            cp.start()

        for a in range(n):
            copy(a, 1, (*chip1, c), me).wait_recv()
            pass_on(copy(a, 3, (*chip1, c), (*chip2, c)))
            pass_on(copy(a, 4, (*chip1, c), sibling))
        for a in range(n):
            copy(a, 2, (*chip2, c), me).wait_recv()
            pass_on(copy(a, 5, (*chip2, c), sibling))
        for a in range(n):
            copy(a, 3, (*diag, c), me).wait_recv()
            pass_on(copy(a, 6, (*diag, c), sibling))
        for a in range(n):
            copy(a, 0, sibling, me).wait_recv()
            copy(a, 4, (*chip2, 1 - c), me).wait_recv()
            copy(a, 5, (*chip1, 1 - c), me).wait_recv()
            copy(a, 6, (*diag, 1 - c), me).wait_recv()
        for cp in first + passed:
            cp.wait_send()
        for cp in mine:
            cp.wait()

    shapes = [jax.ShapeDtypeStruct((N_DEV, *s.shape), s.dtype) for s in shards]
    sems = [pltpu.SemaphoreType.DMA((n, 7)), pltpu.SemaphoreType.DMA((n, 7)), pltpu.SemaphoreType.DMA((n,))]
    return shards, shapes, sems, start, wait


def _gather_direct_exchange(shards):
    n = len(shards)

    def copies(x_refs, out_refs, sems):
        send_sems, recv_sems, local_sems = sems
        x, y, c = lax.axis_index("x"), lax.axis_index("y"), lax.axis_index("c")
        targets = [(x, y, 1 - c), (1 - x, y, c), (x, 1 - y, c), (1 - x, 1 - y, c)]
        local, sends, recvs = [], [], []
        for a in range(n):
            mine = out_refs[a].at[4 * x + 2 * y + c]
            local.append(pltpu.make_async_copy(x_refs[a], mine, local_sems.at[a]))
            for k, to in enumerate(targets):
                theirs = out_refs[a].at[4 * to[0] + 2 * to[1] + to[2]]
                sends.append(_remote(x_refs[a], mine, send_sems.at[a, k], recv_sems.at[a, k], to))
                recvs.append(_remote(theirs, theirs, send_sems.at[a, k], recv_sems.at[a, k], to))
        return local, sends, recvs

    def start(x_refs, out_refs, sems):
        local, sends, _ = copies(x_refs, out_refs, sems)
        for cp in local + sends:
            cp.start()

    def wait(x_refs, out_refs, sems):
        local, sends, recvs = copies(x_refs, out_refs, sems)
        for cp in recvs:
            cp.wait_recv()
        for cp in sends:
            cp.wait_send()
        for cp in local:
            cp.wait()

    shapes = [jax.ShapeDtypeStruct((N_DEV, *s.shape), s.dtype) for s in shards]
    sems = [pltpu.SemaphoreType.DMA((n, 4)), pltpu.SemaphoreType.DMA((n, 4)), pltpu.SemaphoreType.DMA((n,))]
    return shards, shapes, sems, start, wait


def _gather_forward_exchange(gathered):
    n = len(gathered)

    def copies(_, out_refs, sems):
        send_sems, recv_sems = sems
        x, y, c = lax.axis_index("x"), lax.axis_index("y"), lax.axis_index("c")
        sibling = (x, y, 1 - c)
        sends, recvs = [], []
        for a in range(n):
            for j, (px, py) in enumerate([(1 - x, y), (x, 1 - y), (1 - x, 1 - y)]):
                mine = out_refs[a].at[4 * px + 2 * py + c]
                theirs = out_refs[a].at[4 * px + 2 * py + 1 - c]
                sends.append(_remote(mine, mine, send_sems.at[a, j], recv_sems.at[a, j], sibling))
                recvs.append(_remote(theirs, theirs, send_sems.at[a, j], recv_sems.at[a, j], sibling))
        return sends, recvs

    def start(in_refs, out_refs, sems):
        for cp in copies(in_refs, out_refs, sems)[0]:
            cp.start()

    def wait(in_refs, out_refs, sems):
        sends, recvs = copies(in_refs, out_refs, sems)
        for cp in recvs:
            cp.wait_recv()
        for cp in sends:
            cp.wait_send()

    shapes = [jax.ShapeDtypeStruct(g.shape, g.dtype) for g in gathered]
    sems = [pltpu.SemaphoreType.DMA((n, 3)), pltpu.SemaphoreType.DMA((n, 3))]
    return gathered, shapes, sems, start, wait, True


def _chips_exchange(hsums):
    n = len(hsums)

    def copies(h_refs, out_refs, sems):
        send_sems, recv_sems = sems
        x, y, c = lax.axis_index("x"), lax.axis_index("y"), lax.axis_index("c")
        chips = [(1 - x, y), (x, 1 - y), (1 - x, 1 - y)]
        return [_remote(h_refs[a].at[2 * px + py], out_refs[a].at[k], send_sems.at[a, k], recv_sems.at[a, k], (px, py, c))
                for a in range(n) for k, (px, py) in enumerate(chips)]

    def start(h_refs, out_refs, sems):
        for cp in copies(h_refs, out_refs, sems):
            cp.start()

    def wait(h_refs, out_refs, sems):
        for cp in copies(h_refs, out_refs, sems):
            cp.wait()

    shapes = [jax.ShapeDtypeStruct((3, *h.shape[1:]), h.dtype) for h in hsums]
    sems = [pltpu.SemaphoreType.DMA((n, 3)), pltpu.SemaphoreType.DMA((n, 3))]
    return hsums, shapes, sems, start, wait


def _sibling_exchange(halves):
    n = len(halves)

    def copies(p_refs, out_refs, sems):
        send_sems, recv_sems = sems
        x, y, c = lax.axis_index("x"), lax.axis_index("y"), lax.axis_index("c")
        return [_remote(p_refs[a], out_refs[a], send_sems.at[a], recv_sems.at[a], (x, y, 1 - c)) for a in range(n)]

    def start(p_refs, out_refs, sems):
        for cp in copies(p_refs, out_refs, sems):
            cp.start()

    def wait(p_refs, out_refs, sems):
        for cp in copies(p_refs, out_refs, sems):
            cp.wait()

    shapes = [jax.ShapeDtypeStruct(h.shape, h.dtype) for h in halves]
    return halves, shapes, [pltpu.SemaphoreType.DMA((n,)), pltpu.SemaphoreType.DMA((n,))], start, wait


_IN_RANGES = ((0, 3 * D, 0, 0), (3 * D, 6 * D, 0, 6 * D), (6 * D, 7 * D, 0, 3 * D), (7 * D, 7 * D + 16, 1, 0),
              (7 * D + 16, 9 * D + 16, 0, 4 * D))


def _col_pieces(width, ranges):
    pieces = []
    for d in range(N_DEV):
        lo, hi = d * width, (d + 1) * width
        for glo, ghi, mat, mlo in ranges:
            a, b = max(lo, glo), min(hi, ghi)
            if a < b:
                pieces.append((d, a - lo, b - lo, mat, mlo + a - glo))
    return pieces


def _cols_to_matrices(g, ranges, out_widths, *, name):
    _, rows, width = g.shape
    tb = 128
    pieces = _col_pieces(width, ranges)
    covered = [sum(p[2] - p[1] for p in pieces if p[3] == m) for m in range(len(out_widths))]

    def body(g_ref, *o_refs):
        for m, o_ref in enumerate(o_refs):
            if covered[m] < out_widths[m]:
                o_ref[...] = jnp.zeros_like(o_ref)
        for d, b0, b1, m, m0 in pieces:
            o_refs[m][:, m0:m0 + b1 - b0] = g_ref[d, :, b0:b1]

    return pl.pallas_call(
        body, name=name, grid=(rows // tb,), in_specs=[pl.BlockSpec((N_DEV, tb, width), lambda i: (0, i, 0))],
        out_specs=[pl.BlockSpec((tb, wo), lambda i: (i, 0)) for wo in out_widths],
        out_shape=[jax.ShapeDtypeStruct((rows, wo), g.dtype) for wo in out_widths], compiler_params=_params(1),
    )(g)


def _transposed_matrices_to_blocks(mats, ranges, width, *, name):
    rows = mats[0].shape[1]
    pieces = _col_pieces(width, ranges)

    def body(*refs):
        m_refs, g_ref = refs[:-1], refs[-1]
        for d, b0, b1, m, m0 in pieces:
            g_ref[d, b0:b1, :] = m_refs[m][m0:m0 + b1 - b0, :]

    return pl.pallas_call(
        body, name=name, grid=(rows // 128,),
        in_specs=[pl.BlockSpec((mt.shape[0], 128), lambda i: (0, i)) for mt in mats],
        out_specs=pl.BlockSpec((N_DEV, width, 128), lambda i: (0, 0, i)),
        out_shape=jax.ShapeDtypeStruct((N_DEV, width, rows), mats[0].dtype), compiler_params=_params(1),
    )(*mats)


def _row_block(rows):
    return 128 if rows % 128 == 0 else rows


def _half_bf16(g4, c_other, *, name):
    _, _, rows, width = g4.shape
    tb = _row_block(rows)

    def body(c_ref, p_ref, o_ref):
        o_ref[0] = p_ref[0, 0].astype(bf16)

    grid_spec = pltpu.PrefetchScalarGridSpec(
        num_scalar_prefetch=1, grid=(4, rows // tb),
        in_specs=[pl.BlockSpec((1, 1, tb, width), lambda j, i, c_ref: (j, c_ref[0], i, 0))],
        out_specs=pl.BlockSpec((1, tb, width), lambda j, i, c_ref: (j, i, 0)))
    return pl.pallas_call(
        body, name=name, grid_spec=grid_spec, out_shape=jax.ShapeDtypeStruct((4, rows, width), bf16),
        compiler_params=_params(2, _vmem_for(4 * tb * width, 2 * tb * width)),
    )(c_other, g4)


def _pair_sum(g4, recv, c_me, *, name):
    _, _, rows, width = g4.shape
    tb = _row_block(rows)

    def body(c_ref, p_ref, r_ref, o_ref, ob_ref):
        s = p_ref[0, 0] + r_ref[0].astype(f32)
        o_ref[0] = s
        ob_ref[0] = s.astype(bf16)

    blk = pl.BlockSpec((1, tb, width), lambda j, i, c_ref: (j, i, 0))
    grid_spec = pltpu.PrefetchScalarGridSpec(
        num_scalar_prefetch=1, grid=(4, rows // tb),
        in_specs=[pl.BlockSpec((1, 1, tb, width), lambda j, i, c_ref: (j, c_ref[0], i, 0)), blk],
        out_specs=[blk, blk])
    return pl.pallas_call(
        body, name=name, grid_spec=grid_spec,
        out_shape=[jax.ShapeDtypeStruct((4, rows, width), f32), jax.ShapeDtypeStruct((4, rows, width), bf16)],
        compiler_params=_params(2, _vmem_for(4 * tb * width, 2 * tb * width, 4 * tb * width, 2 * tb * width)),
    )(c_me, g4, recv)


def _adam_shard(hsum, recv, chip, w, m, v, *, name):
    _, rows, width = w.shape
    tb = _row_block(rows)

    def body(j_ref, h_ref, r_ref, w_ref, m_ref, v_ref, g_out, d_out, m_out, v_out):
        g = ((h_ref[0] + r_ref[0].astype(f32)) + r_ref[1].astype(f32)) + r_ref[2].astype(f32)
        delta, mn, vn = _adam_math(w_ref[0], g, m_ref[0], v_ref[0])
        g_out[0] = g
        d_out[0] = delta
        m_out[0] = mn
        v_out[0] = vn

    blk = pl.BlockSpec((1, tb, width), lambda i, j_ref: (0, i, 0))
    grid_spec = pltpu.PrefetchScalarGridSpec(
        num_scalar_prefetch=1, grid=(rows // tb,),
        in_specs=[pl.BlockSpec((1, tb, width), lambda i, j_ref: (j_ref[0], i, 0)),
                  pl.BlockSpec((3, tb, width), lambda i, j_ref: (0, i, 0)), blk, blk, blk],
        out_specs=[blk, blk, blk, blk])
    return pl.pallas_call(
        body, name=name, grid_spec=grid_spec, out_shape=[jax.ShapeDtypeStruct(w.shape, f32)] * 4,
        compiler_params=_params(1, _vmem_for(*[4 * tb * width] * 8, 6 * tb * width)),
    )(chip, hsum, recv, w, m, v)


def _sum_shard(hsum, recv, chip, *, name):
    _, rows, width = hsum.shape
    tb = _row_block(rows)

    def body(j_ref, h_ref, r_ref, g_out):
        g_out[...] = ((h_ref[0] + r_ref[0].astype(f32)) + r_ref[1].astype(f32)) + r_ref[2].astype(f32)

    grid_spec = pltpu.PrefetchScalarGridSpec(
        num_scalar_prefetch=1, grid=(rows // tb,),
        in_specs=[pl.BlockSpec((1, tb, width), lambda i, j_ref: (j_ref[0], i, 0)),
                  pl.BlockSpec((3, tb, width), lambda i, j_ref: (0, i, 0))],
        out_specs=pl.BlockSpec((tb, width), lambda i, j_ref: (i, 0)))
    return pl.pallas_call(body, name=name, grid_spec=grid_spec, out_shape=jax.ShapeDtypeStruct((rows, width), f32),
                          compiler_params=_params(1, _vmem_for(*[4 * tb * width] * 2, 6 * tb * width)))(chip, hsum, recv)


def _adam_columns(g, w, m, v, *, name):
    cols, _, rows = w.shape
    tb = cols // 2

    def body(g_ref, w_ref, m_ref, v_ref, d_out, m_out, v_out):
        delta, mn, vn = _adam_math(w_ref[...], g_ref[...], m_ref[...], v_ref[...])
        d_out[...] = delta
        m_out[...] = mn
        v_out[...] = vn

    blk = pl.BlockSpec((tb, 1, rows), lambda i: (i, 0, 0))
    return pl.pallas_call(
        body, name=name, grid=(cols // tb,), in_specs=[blk] * 4, out_specs=[blk] * 3,
        out_shape=[jax.ShapeDtypeStruct(w.shape, f32)] * 3,
        compiler_params=_params(1, _vmem_for(*[4 * tb * rows] * 7)),
    )(g, w, m, v)


R_SMALL = 8 + 8 * N_DEV
_SMALL_LANES = {"gdn_norm_g": (0, DH), "gdn_A_log": (DH, DH + H), "gdn_dt_bias": (2 * DH, 2 * DH + H)}
_LOSS_LANE = 3 * DH


def _pack_small(dg1, dg2, dg3, dgn, dal, ddt, loss_p, dwa, dwg, dwf):
    def body(dg1_ref, dg2_ref, dg3_ref, dgn_ref, dal_ref, ddt_ref, loss_ref, dwa_ref, dwg_ref, dwf_ref, o_ref):
        def total(ref):
            return jnp.sum(ref[...], axis=0, keepdims=True)

        o_ref[...] = jnp.zeros_like(o_ref)
        o_ref[0:1, :] = total(dg1_ref)
        o_ref[1:2, :] = total(dg2_ref)
        o_ref[2:3, :] = total(dg3_ref)
        o_ref[3:4, 0:DH] = total(dgn_ref)
        o_ref[3:4, DH:2 * DH] = total(dal_ref)
        o_ref[3:4, 2 * DH:3 * DH] = total(ddt_ref)
        o_ref[3:4, 3 * DH:4 * DH] = total(loss_ref)
        for d in range(N_DEV):
            base = 8 + 8 * d
            o_ref[base:base + 3, 0:128] = dwa_ref[0:3, 128 * d:128 * (d + 1)]
            o_ref[base:base + 4, 128:512] = dwg_ref[0:4, 384 * d:384 * (d + 1)]
            o_ref[base + 4:base + 7, 0:704] = dwf_ref[0:3, 704 * d:704 * (d + 1)]

    return pl.pallas_call(body, name="pack_small", out_shape=jax.ShapeDtypeStruct((R_SMALL, D), f32))(
        dg1, dg2, dg3, dgn, dal, ddt, loss_p, dwa, dwg, dwf)


_SMALL = ("norm_mix_g", "norm_ffn_g", "norm_final_g", "gdn_norm_g", "gdn_A_log", "gdn_dt_bias",
          "conv_a_w", "gdn_conv_w", "ffn_conv_w")


def _adam_small(gath, me, w, m, v):
    arrays = [t[n] for n in _SMALL for t in (w, m, v)]

    def body(me_ref, ga_ref, gb_ref, *refs):
        ins, outs = refs[:len(arrays)], refs[len(arrays):]
        ga, gb = ga_ref[0], gb_ref[0]
        for s in range(1, N_DEV):
            ga = ga + ga_ref[s]
            gb = gb + gb_ref[s]
        grads = {"norm_mix_g": ga[0:1, :], "norm_ffn_g": ga[1:2, :], "norm_final_g": ga[2:3, :],
                 "conv_a_w": gb[0:3, 0:128], "gdn_conv_w": gb[0:4, 128:512], "ffn_conv_w": gb[4:7, 0:704]}
        for n, (lo, hi) in _SMALL_LANES.items():
            grads[n] = ga[3:4, lo:hi]
        for i, n in enumerate(_SMALL):
            three_d = len(w[n].shape) == 3
            wv, mv, vv = (r[0] if three_d else r[...] for r in ins[3 * i:3 * i + 3])
            delta, mn, vn = _adam_math(wv, grads[n], mv, vv)
            for o_ref, val in zip(outs[4 * i:4 * i + 4], (grads[n], delta, mn, vn)):
                if three_d:
                    o_ref[0] = val
                else:
                    o_ref[...] = val
        outs[-1][...] = ga[3:4, _LOSS_LANE:_LOSS_LANE + 1]

    def whole(shape):
        return pl.BlockSpec(shape, lambda i, me_ref: (0,) * len(shape))

    grid_spec = pltpu.PrefetchScalarGridSpec(
        num_scalar_prefetch=1, grid=(1,),
        in_specs=[pl.BlockSpec((N_DEV, 8, D), lambda i, me_ref: (0, 0, 0)),
                  pl.BlockSpec((N_DEV, 8, D), lambda i, me_ref: (0, 1 + me_ref[0], 0))] + [whole(a.shape) for a in arrays],
        out_specs=[whole(w[n].shape) for n in _SMALL for _ in range(4)] + [whole((1, 1))])
    res = pl.pallas_call(
        body, name="adam_small", grid_spec=grid_spec,
        out_shape=[jax.ShapeDtypeStruct(w[n].shape, f32) for n in _SMALL for _ in range(4)]
        + [jax.ShapeDtypeStruct((1, 1), f32)],
        compiler_params=_params(1),
    )(me, gath, gath, *arrays)
    return {n: tuple(res[4 * i:4 * i + 4]) for i, n in enumerate(_SMALL)}, res[-1]


def _adam_math(w, g, m, v):
    m = ADAM_B1 * m + (1.0 - ADAM_B1) * g
    v = ADAM_B2 * v + (1.0 - ADAM_B2) * jnp.square(g)
    m_hat = m / (1.0 - ADAM_B1 ** ADAM_STEP)
    v_hat = v / (1.0 - ADAM_B2 ** ADAM_STEP)
    delta = -ADAM_LR * (m_hat / (jnp.sqrt(v_hat) + ADAM_EPS) + ADAM_WD * w)
    return delta, m, v


_WEIGHTS = ("norm_mix_g", "w_in", "conv_a_w", "gdn_conv_w", "gdn_A_log", "gdn_dt_bias", "gdn_norm_g", "w_a_out",
            "w_b_out", "w_o", "norm_ffn_g", "w_up", "ffn_conv_w", "w_down", "norm_final_g")
_CONVS = ("conv_a_w", "gdn_conv_w", "ffn_conv_w")


class _StepExchanges:
    def __init__(self, wts, mom, var, c_me, chip):
        self.wts, self.mom, self.var, self.c_me, self.chip = wts, mom, var, c_me, chip
        self.results = {}

    def gather_first(self):
        return _gather_exchange([self.wts["w_in"][0].astype(bf16)] + [self.wts[n][0] for n in _CONVS])

    def finish_first(self, gathered):
        g_in, gc_a, gc_g, gc_f = gathered
        w1, w2 = _cols_to_matrices(g_in, _IN_RANGES, (NW1, 128), name="relay_w_in")
        return {"w1": w1, "w2": w2, "conv_a_w": gc_a.transpose(1, 0, 2).reshape(3, D),
                "gdn_conv_w": gc_g.transpose(1, 0, 2).reshape(4, 3 * D),
                "ffn_conv_w": gc_f.transpose(1, 0, 2).reshape(3, 2 * DFF)}

    def gather_rest(self):
        return _gather_direct_exchange([self.wts[n][0].astype(bf16) for n in _REST])

    def finish_gather(self, gathered):
        g_up, g_a, g_b, g_o, g_down = gathered
        return {"w_up": g_up.reshape(2 * DFF, D), "w_a_out": g_a.reshape(D, D), "w_b_out": g_b.reshape(D, D),
                "w_o": g_o.reshape(D, D), "w_down": g_down.reshape(DFF, D)}

    def reduce_halves(self, names, grads):
        blocks = []
        for n in names:
            if n == "w_in":
                g = _transposed_matrices_to_blocks([grads["w1"], grads["w2"]], _IN_RANGES, R_IN, name="relay_dw_in")
                blocks.append(g.reshape(4, 2, R_IN, D))
            else:
                blocks.append(grads[n].reshape(4, 2, *self.wts[n].shape[1:]))
        return _sibling_exchange([_half_bf16(g, 1 - self.c_me, name="rs_half_" + n) for n, g in zip(names, blocks)]), blocks

    def reduce_sums(self, names, blocks, recv):
        sums = [_pair_sum(g, r, self.c_me, name="rs_sum_" + n) for n, g, r in zip(names, blocks, recv)]
        return _chips_exchange([s[1] for s in sums]), [s[0] for s in sums]

    def finish_reduce(self, names, sums, recv):
        for n, s, r in zip(names, sums, recv):
            if n == "w_in":
                g = _sum_shard(s, r, self.chip, name="rs_total_w_in")[:, None, :]
                w, m, v = (jnp.transpose(t[n], (2, 0, 1)) for t in (self.wts, self.mom, self.var))
                res = (g, *_adam_columns(g, w, m, v, name="adam_w_in"))
                self.results[n] = tuple(jnp.transpose(a, (1, 2, 0)) for a in res)
            else:
                self.results[n] = _adam_shard(s, r, self.chip, self.wts[n], self.mom[n], self.var[n], name="adam_" + n)


def kernel(x, norm_mix_g, w_in, conv_a_w, gdn_conv_w, gdn_A_log, gdn_dt_bias, gdn_norm_g, w_a_out, w_b_out, w_o, norm_ffn_g, w_up, ffn_conv_w, w_down, norm_final_g, loss_target, m_norm_mix_g, m_w_in, m_conv_a_w, m_gdn_conv_w, m_gdn_A_log, m_gdn_dt_bias, m_gdn_norm_g, m_w_a_out, m_w_b_out, m_w_o, m_norm_ffn_g, m_w_up, m_ffn_conv_w, m_w_down, m_norm_final_g, v_norm_mix_g, v_w_in, v_conv_a_w, v_gdn_conv_w, v_gdn_A_log, v_gdn_dt_bias, v_gdn_norm_g, v_w_a_out, v_w_b_out, v_w_o, v_norm_ffn_g, v_w_up, v_ffn_conv_w, v_w_down, v_norm_final_g):
    wts = dict(zip(_WEIGHTS, (norm_mix_g, w_in, conv_a_w, gdn_conv_w, gdn_A_log, gdn_dt_bias, gdn_norm_g, w_a_out,
                              w_b_out, w_o, norm_ffn_g, w_up, ffn_conv_w, w_down, norm_final_g)))
    mom = dict(zip(_WEIGHTS, (m_norm_mix_g, m_w_in, m_conv_a_w, m_gdn_conv_w, m_gdn_A_log, m_gdn_dt_bias,
                              m_gdn_norm_g, m_w_a_out, m_w_b_out, m_w_o, m_norm_ffn_g, m_w_up, m_ffn_conv_w,
                              m_w_down, m_norm_final_g)))
    var = dict(zip(_WEIGHTS, (v_norm_mix_g, v_w_in, v_conv_a_w, v_gdn_conv_w, v_gdn_A_log, v_gdn_dt_bias,
                              v_gdn_norm_g, v_w_a_out, v_w_b_out, v_w_o, v_norm_ffn_g, v_w_up, v_ffn_conv_w,
                              v_w_down, v_norm_final_g)))
    cx, cy, cc = lax.axis_index("x"), lax.axis_index("y"), lax.axis_index("c")
    c_me = jnp.reshape(cc, (1,)).astype(jnp.int32)
    chip = jnp.reshape(2 * cx + cy, (1,)).astype(jnp.int32)
    me = jnp.reshape(4 * cx + 2 * cy + cc, (1,)).astype(jnp.int32)

    def with_up_transposed(t):
        return {**t, "w_up": jnp.swapaxes(t["w_up"], 1, 2)}

    comm = _StepExchanges(with_up_transposed(wts), with_up_transposed(mom), with_up_transposed(var), c_me, chip)
    replicated = {n: wts[n] for n in ("norm_mix_g", "norm_ffn_g", "norm_final_g", "gdn_norm_g", "gdn_A_log", "gdn_dt_bias")}
    loss_p, dx, grads = _local_step(x[0], loss_target[0], replicated, comm)
    res = comm.results
    res["w_up"] = tuple(jnp.swapaxes(a, 1, 2) for a in res["w_up"])

    small = _pack_small(grads["norm_mix_g"], grads["norm_ffn_g"], grads["norm_final_g"], grads["gdn_norm_g"],
                        grads["gdn_A_log"], grads["gdn_dt_bias"], loss_p, grads["conv_a_w"], grads["gdn_conv_w"],
                        grads["ffn_conv_w"])
    (small_all,) = _run_exchange(_gather_exchange([small]), name="ag_small")

    def raw(t):
        return {n: t[n].reshape(1, D) if n == "norm_final_g" else t[n] for n in _SMALL}

    res_small, loss = _adam_small(small_all, me, raw(wts), raw(mom), raw(var))
    for n in _SMALL:
        res[n] = tuple(a.reshape(wts[n].shape) for a in res_small[n])
    outs = [[res[n][i] for n in _WEIGHTS] for i in range(4)]
    return (loss.reshape(()), dx[None], *outs[0], *outs[1], *outs[2], *outs[3])
```

```python
import jax
import jax.numpy as jnp
from jax import lax
from jax.experimental import pallas as pl
from jax.experimental.pallas import tpu as pltpu

f32 = jnp.float32
bf16 = jnp.bfloat16

D = 1024
H = 8
DH = 128
CH = 64
GDN_STEP = 2
ROW_BLOCK = 256
DFF = 2816
NW1 = 9216
EPS = 1e-6
N_DEV = 8

ADAM_LR = 0.001
ADAM_B1 = 0.9
ADAM_B2 = 0.999
ADAM_EPS = 1e-08
ADAM_WD = 0.01
ADAM_STEP = 10

VMEM_LIMIT_BYTES = 48 * 1024 * 1024
VMEM_MAX_BYTES = 56 * 1024 * 1024

R_IN, R_UP = 1154, 704

_HI = lax.Precision.HIGHEST
MESH = pl.DeviceIdType.MESH


def _params(n_grid, vmem_bytes=None):
    return pltpu.CompilerParams(dimension_semantics=("arbitrary",) * n_grid,
                                vmem_limit_bytes=VMEM_LIMIT_BYTES if vmem_bytes is None else vmem_bytes)


def _vmem_for(*block_bytes, extra=0):
    need = 2 * sum(block_bytes) + extra + 4 * 1024 * 1024
    return min(max(need, VMEM_LIMIT_BYTES), VMEM_MAX_BYTES)


def _bdot(a, b):
    return jnp.dot(a.astype(bf16), b.astype(bf16), preferred_element_type=f32)


def _bdot_nt(a, b):
    return lax.dot_general(a.astype(bf16), b.astype(bf16), (((1,), (1,)), ((), ())), preferred_element_type=f32)


def _bdot_tn(a, b):
    return lax.dot_general(a.astype(bf16), b.astype(bf16), (((0,), (0,)), ((), ())), preferred_element_type=f32)


def _hdot(a, b):
    return jnp.dot(a, b, preferred_element_type=f32, precision=_HI)


def _idot(a, b):
    return jnp.dot(a, b, preferred_element_type=f32, precision=lax.Precision.HIGH)


def _sigmoid(x):
    return 1.0 / (1.0 + jnp.exp(-x))


def _softplus(x):
    return jnp.maximum(x, 0.0) + jnp.log(1.0 + jnp.exp(-jnp.abs(x)))


def _shift_down(x, halo, j):
    if j == 0:
        return x
    xr = pltpu.roll(x, j, 0)
    hr = pltpu.roll(halo, j, 0)
    r8 = lax.broadcasted_iota(jnp.int32, hr.shape, 0)
    top = jnp.where(r8 < j, hr, xr[:8])
    return jnp.concatenate([top, xr[8:]], axis=0)


def _shift_up(x, halo, j):
    if j == 0:
        return x
    n = x.shape[0]
    xr = pltpu.roll(x, n - j, 0)
    hr = pltpu.roll(halo, 8 - j, 0)
    r8 = lax.broadcasted_iota(jnp.int32, hr.shape, 0)
    bot = jnp.where(r8 >= 8 - j, hr, xr[n - 8:])
    return jnp.concatenate([xr[:n - 8], bot], axis=0)


def _taps_down(x, halo, k):
    return [_shift_down(x, halo, k - 1 - j) for j in range(k)]


def _strip(i, base=0):
    return slice(base + i * 128, base + (i + 1) * 128)


def _strip_taps(x, halo, first, k):
    return _taps_down(x, jnp.where(first, 0.0, halo), k)


def _strip_conv(w_ref, sl, taps):
    out = w_ref[0:1, sl] * taps[0]
    for j in range(1, len(taps)):
        out = out + w_ref[j:j + 1, sl] * taps[j]
    return out


def _strip_weight_grad(dw_ref, sl, dy, taps):
    for j, tap in enumerate(taps):
        dw_ref[j:j + 1, sl] += jnp.sum(dy * tap, axis=0, keepdims=True)


def _strip_conv_up(dy, halo, last, w_ref, sl, k):
    halo = jnp.where(last, 0.0, halo)
    out = w_ref[k - 1:k, sl] * dy
    for j in range(k - 1):
        out = out + w_ref[j:j + 1, sl] * _shift_up(dy, halo, k - 1 - j)
    return out


def _row(tb, w, col=0):
    return pl.BlockSpec((tb, w), lambda i: (i, col))


def _prev(tb, w, col=0, rows=8):
    return pl.BlockSpec((rows, w), lambda i: (jnp.maximum(i * (tb // rows) - 1, 0), col))


def _next(tb, w, n_rows, col=0, rows=8):
    last = n_rows // rows - 1
    return pl.BlockSpec((rows, w), lambda i: (jnp.minimum((i + 1) * (tb // rows), last), col))


def _f32(ref, sl):
    return ref[:, sl].astype(f32)


def _halo_before(ref, sl):
    h = _f32(ref, sl)
    return h[h.shape[0] - 8:]


def _halo_after(ref, sl):
    return _f32(ref, sl)[:8]


def _fixed(shape):
    return pl.BlockSpec(shape, lambda i: (0,) * len(shape))


def _pick(n, prefs):
    for p in prefs:
        if n % p == 0:
            return p
    return n


def _matmul(a, b, *, name, nt=False, add=None, tm=1024, tn=1024, tk=None, out_dtype=f32, cols=None, exchange=None):
    m, kd = a.shape
    col0, n = cols if cols is not None else (0, b.shape[0] if nt else b.shape[1])
    tm = _pick(m, (tm, 512, 256))
    tn = _pick(n, (tn, 1024, 512, 128))
    tk = kd if tk is None else tk
    nk = kd // tk
    assert nk == 1 or out_dtype == f32
    assert col0 % tn == 0 and not (nt and cols)
    j0 = col0 // tn
    dims = (((1,), (1,)), ((), ())) if nt else (((1,), (0,)), ((), ()))

    def body(a_ref, b_ref, *rest):
        o_ref = rest[-1]
        part = lax.dot_general(a_ref[...], b_ref[...], dims, preferred_element_type=f32)
        if nk == 1:
            o_ref[...] = (part if add is None else part + rest[0][...]).astype(out_dtype)
            return
        k = pl.program_id(2)

        @pl.when(k == 0)
        def _():
            o_ref[...] = part if add is None else part + rest[0][...]

        @pl.when(k > 0)
        def _():
            o_ref[...] += part

    b_spec = pl.BlockSpec((tn, tk), lambda i, j, k: (j, k)) if nt else pl.BlockSpec((tk, tn), lambda i, j, k: (k, j + j0))
    in_specs = [pl.BlockSpec((tm, tk), lambda i, j, k: (i, k)), b_spec]
    args = [a, b]
    if add is not None:
        in_specs.append(pl.BlockSpec((tm, tn), lambda i, j, k: (i, j)))
        args.append(add)
    vmem = _vmem_for(2 * tm * tk, 2 * tk * tn, tm * tn * jnp.dtype(out_dtype).itemsize,
                     4 * tm * tn if add is not None else 0, extra=4 * tm * tn)
    return _call_with_exchange(
        body, exchange, name=name, grid=(m // tm, n // tn, nk), in_specs=in_specs,
        out_specs=pl.BlockSpec((tm, tn), lambda i, j, k: (i, j)),
        out_shape=jax.ShapeDtypeStruct((m, n), out_dtype), args=args, vmem_bytes=vmem)


def _call_with_exchange(body, exchange, *, name, grid, in_specs, out_specs, out_shape, args, vmem_bytes=None):
    if exchange is None:
        return pl.pallas_call(body, name=name, grid=grid, in_specs=in_specs, out_specs=out_specs, out_shape=out_shape,
                              compiler_params=_params(len(grid), vmem_bytes))(*args)
    x_arrays, x_shapes, x_sems, start, wait = exchange[:5]
    n_in, n_xin, n_xout = len(args), len(x_arrays), len(x_shapes)
    aliases = {n_in + i: 1 + i for i in range(n_xin)} if len(exchange) > 5 and exchange[5] else {}

    def full_body(*refs):
        c_in, x_in = refs[:n_in], refs[n_in:n_in + n_xin]
        c_out = refs[n_in + n_xin]
        x_out = refs[n_in + n_xin + 1:n_in + n_xin + 1 + n_xout]
        sems = refs[n_in + n_xin + 1 + n_xout:]
        ids = [pl.program_id(d) for d in range(len(grid))]
        first, last = ids[0] == 0, ids[0] == grid[0] - 1
        for d in range(1, len(grid)):
            first = first & (ids[d] == 0)
            last = last & (ids[d] == grid[d] - 1)

        @pl.when(first)
        def _():
            start(x_in, x_out, sems)

        body(*c_in, c_out)

        @pl.when(last)
        def _():
            wait(x_in, x_out, sems)

    res = pl.pallas_call(
        full_body, name=name, grid=grid, in_specs=list(in_specs) + [_ANY] * n_xin,
        out_specs=[out_specs] + [_ANY] * n_xout, out_shape=[out_shape] + list(x_shapes),
        scratch_shapes=list(x_sems), input_output_aliases=aliases, compiler_params=_params(len(grid), vmem_bytes),
    )(*args, *x_arrays)
    return res[0], list(res[1:])


def _matmul_tn(a, b, *, name, tm=1024, tn=1024, tt=2048, exchange=None):
    t, m = a.shape
    _, n = b.shape
    tm = _pick(m, (tm, 1024, 512, 128))
    tn = _pick(n, (tn, 1024, 512, 128))
    tt = _pick(t, (tt, 2048, 1024, 512, 256))
    nt = t // tt

    def body(a_ref, b_ref, o_ref):
        k = pl.program_id(2)
        part = lax.dot_general(a_ref[...], b_ref[...], (((0,), (0,)), ((), ())), preferred_element_type=f32)

        @pl.when(k == 0)
        def _():
            o_ref[...] = part

        @pl.when(k > 0)
        def _():
            o_ref[...] += part

    return _call_with_exchange(
        body, exchange, name=name, grid=(m // tm, n // tn, nt),
        in_specs=[pl.BlockSpec((tt, tm), lambda i, j, k: (k, i)), pl.BlockSpec((tt, tn), lambda i, j, k: (k, j))],
        out_specs=pl.BlockSpec((tm, tn), lambda i, j, k: (i, j)),
        out_shape=jax.ShapeDtypeStruct((m, n), f32), args=[a, b],
        vmem_bytes=_vmem_for(2 * tt * tm, 2 * tt * tn, 4 * tm * tn, extra=4 * tm * tn + 2 * tt * tm))


def _rms_fwd(x, g, *, name, exchange=None):
    t = x.shape[0]
    tb = _pick(t, (256, 128))

    def body(x_ref, g_ref, h_ref):
        xv = x_ref[...]
        r = lax.rsqrt(jnp.mean(xv * xv, axis=-1, keepdims=True) + EPS)
        h_ref[...] = (xv * r * g_ref[...]).astype(bf16)

    return _call_with_exchange(
        body, exchange, name=name, grid=(t // tb,), in_specs=[_row(tb, D), _fixed((1, D))], out_specs=_row(tb, D),
        out_shape=jax.ShapeDtypeStruct((t, D), bf16), args=[x, g])


def _rms_bwd(dh, x, g, dres, *, name, more=None, bf16_copy=True):
    t = x.shape[0]
    tb = _pick(t, (256, 128))

    def body(dh_ref, x_ref, g_ref, dres_ref, *rest):
        dx_ref, dg_ref = rest[-3 if bf16_copy else -2], rest[-1]
        xv = x_ref[...]
        r = lax.rsqrt(jnp.mean(xv * xv, axis=-1, keepdims=True) + EPS)
        xh = xv * r
        dy = dh_ref[...]
        if more is not None:
            dy = dy + lax.dot_general(rest[0][...], rest[1][...], (((1,), (1,)), ((), ())), preferred_element_type=f32)
        dyg = dy * g_ref[...]
        dx = dres_ref[...] + r * (dyg - xh * jnp.mean(dyg * xh, axis=-1, keepdims=True))
        dx_ref[...] = dx
        if bf16_copy:
            rest[-2][...] = dx.astype(bf16)

        @pl.when(pl.program_id(0) == 0)
        def _():
            dg_ref[...] = jnp.zeros_like(dg_ref)

        dg_ref[...] += jnp.sum((dy * xh).reshape(tb // 8, 8, D), axis=0)

    in_specs, args = [_row(tb, D), _row(tb, D), _fixed((1, D)), _row(tb, D)], [dh, x, g, dres]
    if more is not None:
        in_specs += [_row(tb, 128), _fixed(more[1].shape)]
        args += list(more)
    dx_dtypes = (f32, bf16) if bf16_copy else (f32,)
    return pl.pallas_call(
        body, name=name, grid=(t // tb,), in_specs=in_specs,
        out_specs=[_row(tb, D) for _ in dx_dtypes] + [_fixed((8, D))],
        out_shape=[jax.ShapeDtypeStruct((t, D), dt) for dt in dx_dtypes] + [jax.ShapeDtypeStruct((8, D), f32)],
        compiler_params=_params(1),
    )(*args)


def _gdn_gates(ab, alog, dtb):
    lane = lax.broadcasted_iota(jnp.int32, ab.shape, 1)
    g = -jnp.exp(alog) * _softplus(ab + dtb)
    beta = _sigmoid(ab)
    return jnp.where(lane < H, g, jnp.where(lane < 2 * H, beta, 0.0))


def _pre_fwd(pg, pq, h1, w2, wa, wg, alog, dtb):
    t = pg.shape[0]
    tb = _pick(t, (ROW_BLOCK, 128))

    def body(p0_ref, p0h_ref, pq_ref, pqh_ref, h1_ref, w2_ref, wa_ref, wg_ref, alog_ref, dtb_ref,
             ya_ref, qn_ref, kn_ref, vc_ref, gb_ref, p2_ref):
        first = pl.program_id(0) == 0
        p2_ref[...] = jnp.dot(h1_ref[...], w2_ref[...], preferred_element_type=f32)
        for i in range(D // 128):
            sl, cg, xv = _strip(i), _strip(i, D), _strip(i, 2 * D)
            taps = _strip_taps(_f32(p0_ref, cg) * _f32(p0_ref, xv), _halo_before(p0h_ref, cg) * _halo_before(p0h_ref, xv),
                               first, 3)
            ya_ref[:, sl] = (_f32(p0_ref, sl) * _strip_conv(wa_ref, sl, taps)).astype(bf16)
        for part, out_ref, scale in ((0, qn_ref, DH ** -0.5), (1, kn_ref, 1.0), (2, vc_ref, None)):
            for h in range(H):
                sl = _strip(h, part * D)
                s = _strip_conv(wg_ref, sl, _strip_taps(pq_ref[:, sl], pqh_ref[:, sl], first, 4))
                s = s * _sigmoid(s)
                if scale is not None:
                    s = s * (lax.rsqrt(jnp.sum(s * s, axis=-1, keepdims=True) + EPS) * scale)
                out_ref[:, _strip(h)] = s
        gb_ref[...] = _gdn_gates(p2_ref[...], alog_ref[...], dtb_ref[...])

    return pl.pallas_call(
        body, name="pre_fwd", grid=(t // tb,),
        in_specs=[_row(tb, 3 * D, 0), _prev(tb, 3 * D, 0, rows=16), _row(tb, 3 * D), _prev(tb, 3 * D), _row(tb, D),
                  _fixed((D, 128)), _fixed((8, D)), _fixed((8, 3 * D)), _fixed((1, 128)), _fixed((1, 128))],
        out_specs=[_row(tb, D), _row(tb, D), _row(tb, D), _row(tb, D), _row(tb, 128), _row(tb, 128)],
        out_shape=[jax.ShapeDtypeStruct((t, D), bf16), jax.ShapeDtypeStruct((t, D), f32),
                   jax.ShapeDtypeStruct((t, D), f32), jax.ShapeDtypeStruct((t, D), f32),
                   jax.ShapeDtypeStruct((t, 128), f32), jax.ShapeDtypeStruct((t, 128), f32)],
        compiler_params=_params(1),
    )(pg, pg, pq, pq, h1, w2, wa, wg, alog, dtb)


_Z_COL, _GA_COL, _GB_COL = 3, 4, 5


def _post_fwd(o, pg, gn):
    t = o.shape[0]
    tb = _pick(t, (256, 128))

    def body(o_ref, z_ref, gn_ref, yb_ref):
        for h in range(H):
            sl = slice(h * DH, (h + 1) * DH)
            oh = o_ref[:, sl]
            z = _f32(z_ref, sl)
            r = lax.rsqrt(jnp.mean(oh * oh, axis=-1, keepdims=True) + EPS)
            yb_ref[:, sl] = (oh * r * gn_ref[...] * (z * _sigmoid(z))).astype(bf16)

    return pl.pallas_call(
        body, name="post_fwd", grid=(t // tb,), in_specs=[_row(tb, D), _row(tb, D, _Z_COL), _fixed((1, DH))],
        out_specs=_row(tb, D), out_shape=jax.ShapeDtypeStruct((t, D), bf16), compiler_params=_params(1),
    )(o, pg, gn)


def _post_bwd(dyb, o, pg, gn):
    t = o.shape[0]
    tb = _pick(t, (256, 128))

    def body(dyb_ref, o_ref, z_ref, gn_ref, do_ref, dz_ref, dgn_ref):
        @pl.when(pl.program_id(0) == 0)
        def _():
            dgn_ref[...] = jnp.zeros_like(dgn_ref)

        gn_v = gn_ref[...]
        acc = jnp.zeros((8, DH), f32)
        for h in range(H):
            sl = slice(h * DH, (h + 1) * DH)
            oh = o_ref[:, sl]
            z = _f32(z_ref, sl)
            dy = dyb_ref[:, sl]
            r = lax.rsqrt(jnp.mean(oh * oh, axis=-1, keepdims=True) + EPS)
            on = oh * r
            sg = _sigmoid(z)
            sz = z * sg
            don = dy * sz
            dz_ref[:, sl] = (dy * on * gn_v * (sg * (1.0 + z * (1.0 - sg)))).astype(bf16)
            acc = acc + jnp.sum((don * on).reshape(tb // 8, 8, DH), axis=0)
            doh = don * gn_v
            do_ref[:, sl] = r * (doh - on * jnp.mean(doh * on, axis=-1, keepdims=True))
        dgn_ref[...] += acc

    return pl.pallas_call(
        body, name="post_bwd", grid=(t // tb,),
        in_specs=[_row(tb, D), _row(tb, D), _row(tb, D, _Z_COL), _fixed((1, DH))],
        out_specs=[_row(tb, D), _row(tb, D), _fixed((8, DH))],
        out_shape=[jax.ShapeDtypeStruct((t, D), f32), jax.ShapeDtypeStruct((t, D), bf16),
                   jax.ShapeDtypeStruct((8, DH), f32)],
        compiler_params=_params(1),
    )(dyb, o, pg, gn)


def _mix_fwd(ya, yb, pg):
    t = ya.shape[0]
    tb = _pick(t, (256, 128))

    def body(ya_ref, yb_ref, ga_ref, gb_ref, mix_ref):
        ya_v, yb_v = ya_ref[...].astype(f32), yb_ref[...].astype(f32)
        mix = _sigmoid(ga_ref[...].astype(f32)) * ya_v + _sigmoid(gb_ref[...].astype(f32)) * yb_v
        mix_ref[...] = mix.astype(bf16)

    return pl.pallas_call(
        body, name="mix_fwd", grid=(t // tb,),
        in_specs=[_row(tb, D), _row(tb, D), _row(tb, D, _GA_COL), _row(tb, D, _GB_COL)],
        out_specs=_row(tb, D), out_shape=jax.ShapeDtypeStruct((t, D), bf16), compiler_params=_params(1),
    )(ya, yb, pg, pg)


def _mix_bwd(dmix, ya, yb, pg):
    t = ya.shape[0]
    tb = _pick(t, (256, 128))

    def body(dm_ref, ya_ref, yb_ref, ga_ref, gb_ref, dya_ref, dyb_ref, dg_ref):
        dm = dm_ref[...].astype(f32)
        sa = _sigmoid(ga_ref[...].astype(f32))
        sb = _sigmoid(gb_ref[...].astype(f32))
        dya_ref[...] = (dm * sa).astype(bf16)
        dyb_ref[...] = (dm * sb).astype(bf16)
        dg_ref[:, :D] = (dm * ya_ref[...].astype(f32) * sa * (1.0 - sa)).astype(bf16)
        dg_ref[:, D:] = (dm * yb_ref[...].astype(f32) * sb * (1.0 - sb)).astype(bf16)

    return pl.pallas_call(
        body, name="mix_bwd", grid=(t // tb,),
        in_specs=[_row(tb, D), _row(tb, D), _row(tb, D), _row(tb, D, _GA_COL), _row(tb, D, _GB_COL)],
        out_specs=[_row(tb, D), _row(tb, D), _row(tb, 2 * D)],
        out_shape=[jax.ShapeDtypeStruct((t, D), bf16), jax.ShapeDtypeStruct((t, D), bf16),
                   jax.ShapeDtypeStruct((t, 2 * D), bf16)],
        compiler_params=_params(1),
    )(dmix, ya, yb, pg, pg)


def _ffn_fwd(up, wf):
    t = up.shape[0]
    tb = _pick(t, (ROW_BLOCK, 128))

    def body(up_ref, uph_ref, wf_ref, act_ref):
        first = pl.program_id(0) == 0
        for i in range(DFF // 128):
            g, v = _strip(i), _strip(i, DFF)
            gate = _strip_conv(wf_ref, g, _strip_taps(_f32(up_ref, g), _halo_before(uph_ref, g), first, 3))
            val = _strip_conv(wf_ref, v, _strip_taps(_f32(up_ref, v), _halo_before(uph_ref, v), first, 3))
            act_ref[:, g] = (gate * _sigmoid(gate) * val).astype(bf16)

    return pl.pallas_call(
        body, name="ffn_fwd", grid=(t // tb,),
        in_specs=[_row(tb, 2 * DFF), _prev(tb, 2 * DFF, rows=16), _fixed((8, 2 * DFF))],
        out_specs=_row(tb, DFF), out_shape=jax.ShapeDtypeStruct((t, DFF), bf16), compiler_params=_params(1),
    )(up, up, wf)


def _ffn_bwd1(dact, up, wf):
    t = up.shape[0]
    tb = _pick(t, (ROW_BLOCK, 128))

    def body(da_ref, up_ref, uph_ref, wf_ref, dc_ref, dw_ref):
        @pl.when(pl.program_id(0) == 0)
        def _():
            dw_ref[...] = jnp.zeros_like(dw_ref)

        first = pl.program_id(0) == 0
        for i in range(DFF // 128):
            g, v = _strip(i), _strip(i, DFF)
            g_taps = _strip_taps(_f32(up_ref, g), _halo_before(uph_ref, g), first, 3)
            v_taps = _strip_taps(_f32(up_ref, v), _halo_before(uph_ref, v), first, 3)
            gate = _strip_conv(wf_ref, g, g_taps)
            val = _strip_conv(wf_ref, v, v_taps)
            sg = _sigmoid(gate)
            da = _f32(da_ref, g)
            dgate = da * val * (sg * (1.0 + gate * (1.0 - sg)))
            dval = da * (gate * sg)
            dc_ref[:, g] = dgate.astype(bf16)
            dc_ref[:, v] = dval.astype(bf16)
            _strip_weight_grad(dw_ref, g, dgate, g_taps)
            _strip_weight_grad(dw_ref, v, dval, v_taps)

    return pl.pallas_call(
        body, name="ffn_bwd1", grid=(t // tb,),
        in_specs=[_row(tb, DFF), _row(tb, 2 * DFF), _prev(tb, 2 * DFF, rows=16), _fixed((8, 2 * DFF))],
        out_specs=[_row(tb, 2 * DFF), _fixed((8, 2 * DFF))],
        out_shape=[jax.ShapeDtypeStruct((t, 2 * DFF), bf16), jax.ShapeDtypeStruct((8, 2 * DFF), f32)],
        compiler_params=_params(1),
    )(dact, up, up, wf)


def _ffn_bwd2(dc, wf):
    t = dc.shape[0]
    tb = _pick(t, (ROW_BLOCK, 128))
    nb = t // tb

    def body(dc_ref, dch_ref, wf_ref, dup_ref):
        last = pl.program_id(0) == nb - 1
        for i in range(2 * DFF // 128):
            sl = _strip(i)
            dup_ref[:, sl] = _strip_conv_up(_f32(dc_ref, sl), _halo_after(dch_ref, sl), last, wf_ref, sl, 3).astype(bf16)

    return pl.pallas_call(
        body, name="ffn_bwd2", grid=(nb,),
        in_specs=[_row(tb, 2 * DFF), _next(tb, 2 * DFF, t, rows=16), _fixed((8, 2 * DFF))],
        out_specs=_row(tb, 2 * DFF), out_shape=jax.ShapeDtypeStruct((t, 2 * DFF), bf16), compiler_params=_params(1),
    )(dc, dc, wf)


def _final(x3, tgt, g):
    t = x3.shape[0]
    tb = _pick(t, (256, 128))

    def body(x_ref, t_ref, g_ref, loss_ref, dx_ref, dxb_ref, dg_ref):
        @pl.when(pl.program_id(0) == 0)
        def _():
            loss_ref[...] = jnp.zeros_like(loss_ref)
            dg_ref[...] = jnp.zeros_like(dg_ref)

        xv = x_ref[...]
        r = lax.rsqrt(jnp.mean(xv * xv, axis=-1, keepdims=True) + EPS)
        xh = xv * r
        gv = g_ref[...]
        e = xh * gv - t_ref[...]
        lrow = 0.5 * jnp.mean(e * e, axis=-1, keepdims=True)
        loss_ref[...] += jnp.sum(jnp.broadcast_to(lrow, (tb, 128)).reshape(tb // 8, 8, 128), axis=0)
        dy = e * (1.0 / D)
        dyg = dy * gv
        dx = r * (dyg - xh * jnp.mean(dyg * xh, axis=-1, keepdims=True))
        dx_ref[...] = dx
        dxb_ref[...] = dx.astype(bf16)
        dg_ref[...] += jnp.sum((dy * xh).reshape(tb // 8, 8, D), axis=0)

    return pl.pallas_call(
        body, name="final", grid=(t // tb,), in_specs=[_row(tb, D), _row(tb, D), _fixed((1, D))],
        out_specs=[_fixed((8, 128)), _row(tb, D), _row(tb, D), _fixed((8, D))],
        out_shape=[jax.ShapeDtypeStruct((8, 128), f32), jax.ShapeDtypeStruct((t, D), f32),
                   jax.ShapeDtypeStruct((t, D), bf16), jax.ShapeDtypeStruct((8, D), f32)],
        compiler_params=_params(1),
    )(x3, tgt, g)


def _pre_bwd1(pg, pq, p2, dya_in, dqn, dkn, dvc, dgb, gbeta, h1, wa, wg, alog, dtb):
    t = pg.shape[0]
    tb = _pick(t, (ROW_BLOCK, 128))

    def body(p0_ref, p0h_ref, pq_ref, pqh_ref, p2_ref, dya_ref, dqn_ref, dkn_ref, dvc_ref, dgb_ref, gb_ref, h1_ref,
             wa_ref, wg_ref, alog_ref, dtb_ref,
             dbg_ref, dca_ref, dc4_ref, dp2_ref, dwa_ref, dwg_ref, dal_ref, ddt_ref, dw2_ref):
        @pl.when(pl.program_id(0) == 0)
        def _():
            dwa_ref[...] = jnp.zeros_like(dwa_ref)
            dwg_ref[...] = jnp.zeros_like(dwg_ref)
            dal_ref[...] = jnp.zeros_like(dal_ref)
            ddt_ref[...] = jnp.zeros_like(ddt_ref)
            dw2_ref[...] = jnp.zeros_like(dw2_ref)

        first = pl.program_id(0) == 0

        for i in range(D // 128):
            sl, cg, xv = _strip(i), _strip(i, D), _strip(i, 2 * D)
            taps = _strip_taps(_f32(p0_ref, cg) * _f32(p0_ref, xv), _halo_before(p0h_ref, cg) * _halo_before(p0h_ref, xv),
                               first, 3)
            dya = _f32(dya_ref, sl)
            dbg_ref[:, sl] = (dya * _strip_conv(wa_ref, sl, taps)).astype(bf16)
            dca = dya * _f32(p0_ref, sl)
            dca_ref[:, sl] = dca.astype(bf16)
            _strip_weight_grad(dwa_ref, sl, dca, taps)

        for part, d_ref, scale in ((0, dqn_ref, DH ** -0.5), (1, dkn_ref, 1.0), (2, dvc_ref, None)):
            for h in range(H):
                sl = _strip(h, part * D)
                taps = _strip_taps(pq_ref[:, sl], pqh_ref[:, sl], first, 4)
                c4 = _strip_conv(wg_ref, sl, taps)
                sg = _sigmoid(c4)
                dn = d_ref[:, _strip(h)]
                if scale is not None:
                    a = c4 * sg
                    r = lax.rsqrt(jnp.sum(a * a, axis=-1, keepdims=True) + EPS)
                    an = a * r
                    dn = dn * scale
                    dn = r * (dn - an * jnp.sum(dn * an, axis=-1, keepdims=True))
                dc4 = dn * (sg * (1.0 + c4 * (1.0 - sg)))
                dc4_ref[:, sl] = dc4.astype(bf16)
                _strip_weight_grad(dwg_ref, sl, dc4, taps)

        ab = p2_ref[...]
        lane = lax.broadcasted_iota(jnp.int32, ab.shape, 1)
        dgbv = dgb_ref[...]
        gbv = gb_ref[...]
        da = dgbv * (-jnp.exp(alog_ref[...])) * _sigmoid(ab + dtb_ref[...])
        db = dgbv * gbv * (1.0 - gbv)
        dp2 = jnp.where(lane < H, da, jnp.where(lane < 2 * H, db, 0.0)).astype(bf16)
        dp2_ref[...] = dp2
        dw2_ref[...] += lax.dot_general(dp2, h1_ref[...], (((0,), (0,)), ((), ())), preferred_element_type=f32)
        dal = jnp.where(lane < H, dgbv * gbv, 0.0)
        ddt = jnp.where(lane < H, da, 0.0)
        dal_ref[...] += jnp.sum(dal.reshape(tb // 8, 8, 128), axis=0)
        ddt_ref[...] += jnp.sum(ddt.reshape(tb // 8, 8, 128), axis=0)

    return pl.pallas_call(
        body, name="pre_bwd1", grid=(t // tb,),
        in_specs=[_row(tb, 3 * D, 0), _prev(tb, 3 * D, 0, rows=16), _row(tb, 3 * D), _prev(tb, 3 * D), _row(tb, 128),
                  _row(tb, D), _row(tb, D), _row(tb, D), _row(tb, D), _row(tb, 128), _row(tb, 128), _row(tb, D),
                  _fixed((8, D)), _fixed((8, 3 * D)), _fixed((1, 128)), _fixed((1, 128))],
        out_specs=[_row(tb, D), _row(tb, D), _row(tb, 3 * D), _row(tb, 128),
                   _fixed((8, D)), _fixed((8, 3 * D)), _fixed((8, 128)), _fixed((8, 128)), _fixed((128, D))],
        out_shape=[jax.ShapeDtypeStruct((t, D), bf16), jax.ShapeDtypeStruct((t, D), bf16),
                   jax.ShapeDtypeStruct((t, 3 * D), bf16), jax.ShapeDtypeStruct((t, 128), bf16),
                   jax.ShapeDtypeStruct((8, D), f32), jax.ShapeDtypeStruct((8, 3 * D), f32),
                   jax.ShapeDtypeStruct((8, 128), f32), jax.ShapeDtypeStruct((8, 128), f32),
                   jax.ShapeDtypeStruct((128, D), f32)],
        compiler_params=_params(1),
    )(pg, pg, pq, pq, p2, dya_in, dqn, dkn, dvc, dgb, gbeta, h1, wa, wg, alog, dtb)


def _pre_bwd2(dca, dc4, pg, dbg, dz, dgates, wa, wg, exchange=None):
    t = pg.shape[0]
    tb = _pick(t, (ROW_BLOCK, 128))
    nb = t // tb

    def body(dca_ref, dcah_ref, dc4_ref, dc4h_ref, p0_ref, dbg_ref, dz_ref, dgt_ref, wa_ref, wg_ref, dp_ref):
        last = pl.program_id(0) == nb - 1
        dp_ref[:, :D] = dbg_ref[...]
        for i in range(D // 128):
            sl, cg, xv = _strip(i), _strip(i, D), _strip(i, 2 * D)
            du = _strip_conv_up(_f32(dca_ref, sl), _halo_after(dcah_ref, sl), last, wa_ref, sl, 3)
            dp_ref[:, cg] = (du * _f32(p0_ref, xv)).astype(bf16)
            dp_ref[:, xv] = (du * _f32(p0_ref, cg)).astype(bf16)
        dp_ref[:, 3 * D:4 * D] = dz_ref[...]
        dp_ref[:, 4 * D:6 * D] = dgt_ref[...]
        for i in range(3 * D // 128):
            sl = _strip(i)
            dq = _strip_conv_up(_f32(dc4_ref, sl), _halo_after(dc4h_ref, sl), last, wg_ref, sl, 4)
            dp_ref[:, _strip(i, 6 * D)] = dq.astype(bf16)

    return _call_with_exchange(
        body, exchange, name="pre_bwd2", grid=(nb,),
        in_specs=[_row(tb, D), _next(tb, D, t, rows=16), _row(tb, 3 * D), _next(tb, 3 * D, t, rows=16), _row(tb, 3 * D, 0),
                  _row(tb, D), _row(tb, D), _row(tb, 2 * D), _fixed((8, D)), _fixed((8, 3 * D))],
        out_specs=_row(tb, NW1), out_shape=jax.ShapeDtypeStruct((t, NW1), bf16),
        args=[dca, dca, dc4, dc4, pg, dbg, dz, dgates, wa, wg])


def _chunk_consts():
    r = lax.broadcasted_iota(jnp.int32, (CH, CH), 0)
    c = lax.broadcasted_iota(jnp.int32, (CH, CH), 1)
    return r, c, (r == c).astype(f32)


def _tri_inverse(lows, eye, r, c):
    def same_block(b):
        return jnp.bitwise_xor(r, c) < b

    xs = [jnp.where(same_block(8), -low, 0.0) for low in lows]
    ts = [eye + x for x in xs]
    for _ in range(2):
        xs = [_idot(x, x) for x in xs]
        ts = [t + _idot(t, x) for t, x in zip(ts, xs)]
    for b in (8, 16, 32):
        below = same_block(2 * b) & jnp.logical_not(same_block(b))
        ts = [t - _idot(_idot(t, jnp.where(below, low, 0.0)), t) for t, low in zip(ts, lows)]
    return ts


def _chunk_common(q, k, v, gcol, bcol, r, c, eye):
    grow = jnp.sum(eye * gcol, axis=0, keepdims=True)
    dec = jnp.exp(jnp.where(r >= c, gcol - grow, -jnp.inf))
    rcol = lax.broadcasted_iota(jnp.int32, (CH, 1), 0)
    glast = jnp.sum(jnp.where(rcol == CH - 1, gcol, 0.0), axis=0, keepdims=True)
    eg = jnp.exp(gcol)
    el = jnp.exp(glast - gcol)
    kb = k * bcol
    vb = v * bcol
    kk = _bdot_nt(kb, k)
    low = jnp.where(r > c, kk * dec, 0.0)
    qk = _bdot_nt(q, k)
    att = qk * dec
    return grow, dec, glast, eg, el, kb, vb, kk, low, qk, att, rcol


def _gdn_fwd(qn, kn, vc, gbeta):
    t = qn.shape[0]
    n_chunks = t // CH

    def body(q_ref, k_ref, v_ref, gb_ref, o_ref, s_ref, t_ref, state):
        @pl.when(pl.program_id(0) == 0)
        def _():
            state[...] = jnp.zeros_like(state)

        r, c, eye = _chunk_consts()
        tri = (r >= c).astype(f32)
        heads = range(H)
        keys = [(s, h) for s in range(GDN_STEP) for h in heads]
        rows = [slice(s * CH, (s + 1) * CH) for s in range(GDN_STEP)]
        gbs = [gb_ref[rows[s], :] for s in range(GDN_STEP)]
        galls = [_hdot(tri, gb) for gb in gbs]
        qs = {(s, h): q_ref[rows[s], h * DH:(h + 1) * DH] for s, h in keys}
        ks = {(s, h): k_ref[rows[s], h * DH:(h + 1) * DH] for s, h in keys}
        cm = {(s, h): _chunk_common(qs[s, h], ks[s, h], v_ref[rows[s], h * DH:(h + 1) * DH], galls[s][:, h:h + 1],
                                    gbs[s][:, H + h:H + h + 1], r, c, eye) for s, h in keys}
        invs = dict(zip(keys, _tri_inverse([cm[key][8] for key in keys], eye, r, c)))
        uws = {key: _bdot(invs[key], jnp.concatenate([cm[key][6], cm[key][5] * cm[key][3]], axis=1)) for key in keys}
        sts = [state[h] for h in heads]
        for s in range(GDN_STEP):
            vns = [uws[s, h][:, :DH] - _bdot(uws[s, h][:, DH:], sts[h]) for h in heads]
            outs = [_bdot(qs[s, h] * cm[s, h][3], sts[h]) + _bdot(cm[s, h][10], vns[h]) for h in heads]
            news = [sts[h] * jnp.exp(cm[s, h][2]) + _bdot_tn(ks[s, h] * cm[s, h][4], vns[h]) for h in heads]
            for h in heads:
                s_ref[s, h] = sts[h].astype(bf16)
                t_ref[s, h] = invs[s, h]
                o_ref[rows[s], h * DH:(h + 1) * DH] = outs[h]
            sts = news
        for h in heads:
            state[h] = sts[h]

    tb = GDN_STEP * CH
    return pl.pallas_call(
        body, name="gdn_fwd", grid=(t // tb,),
        in_specs=[_row(tb, D), _row(tb, D), _row(tb, D), _row(tb, 128)],
        out_specs=[_row(tb, D), pl.BlockSpec((GDN_STEP, H, DH, DH), lambda i: (i, 0, 0, 0)),
                   pl.BlockSpec((GDN_STEP, H, CH, CH), lambda i: (i, 0, 0, 0))],
        out_shape=[jax.ShapeDtypeStruct((t, D), f32), jax.ShapeDtypeStruct((n_chunks, H, DH, DH), bf16),
                   jax.ShapeDtypeStruct((n_chunks, H, CH, CH), f32)],
        scratch_shapes=[pltpu.VMEM((H, DH, DH), f32)],
        compiler_params=_params(1),
    )(qn, kn, vc, gbeta)


def _gdn_bwd(qn, kn, vc, gbeta, do, s_all, t_all):
    t = qn.shape[0]

    def body(q_ref, k_ref, v_ref, gb_ref, do_ref, s_ref, t_ref, dq_ref, dk_ref, dv_ref, dgb_ref, dstate):
        @pl.when(pl.program_id(0) == 0)
        def _():
            dstate[...] = jnp.zeros_like(dstate)

        r, c, eye = _chunk_consts()
        tril = r >= c
        lane = lax.broadcasted_iota(jnp.int32, (1, 128), 1)
        hs = range(H)

        def each(fn, *lists):
            return [fn(*args) for args in zip(*lists)]

        def rsum(a):
            return jnp.sum(a, axis=1, keepdims=True)

        def before_state(s):
            rows = slice(s * CH, (s + 1) * CH)
            gb = gb_ref[rows, :]
            gall = _hdot(tril.astype(f32), gb)
            p = {"rows": rows}
            p["q"] = q = [q_ref[rows, h * DH:(h + 1) * DH] for h in hs]
            p["k"] = k = [k_ref[rows, h * DH:(h + 1) * DH] for h in hs]
            p["v"] = v = [v_ref[rows, h * DH:(h + 1) * DH] for h in hs]
            p["dout"] = dout = [do_ref[rows, h * DH:(h + 1) * DH] for h in hs]
            p["inv"] = inv = [t_ref[s, h] for h in hs]
            p["st"] = st = [s_ref[s, h] for h in hs]
            p["bcol"] = bcol = [gb[:, H + h:H + h + 1] for h in hs]
            cm = [_chunk_common(q[h], k[h], v[h], gall[:, h:h + 1], bcol[h], r, c, eye) for h in hs]
            for name, i in (("dec", 1), ("glast", 2), ("eg", 3), ("el", 4), ("kb", 5), ("vb", 6), ("low", 8), ("att", 10)):
                p[name] = [m[i] for m in cm]
            p["rcol"] = cm[0][11]
            p["elast"] = each(jnp.exp, p["glast"])
            p["kbg"] = each(jnp.multiply, p["kb"], p["eg"])
            uw = each(lambda i, a, b: _bdot(i, jnp.concatenate([a, b], axis=1)), inv, p["vb"], p["kbg"])
            p["u"] = [a[:, :DH] for a in uw]
            p["w"] = [a[:, DH:] for a in uw]
            p["vn"] = each(lambda a, b, x: a - _bdot(b, x), p["u"], p["w"], st)
            p["qd"] = each(jnp.multiply, q, p["eg"])
            p["kd"] = each(jnp.multiply, k, p["el"])
            p["dqd"] = each(_bdot_nt, dout, st)
            p["datt"] = each(lambda d, x: jnp.where(tril, _bdot_nt(d, x), 0.0), dout, p["vn"])
            p["dqk"] = each(jnp.multiply, p["datt"], p["dec"])
            p["qd_do"] = each(_bdot_tn, p["qd"], dout)
            p["att_do"] = each(_bdot_tn, p["att"], dout)
            return p

        def after_state(p, ds):
            q, k, v, st, inv, bcol = p["q"], p["k"], p["v"], p["st"], p["inv"], p["bcol"]
            eg, el, kb, u, w = p["eg"], p["el"], p["kb"], p["u"], p["w"]
            dvn = each(lambda a, kk, x: a + _bdot(kk, x), p["att_do"], p["kd"], ds)
            dkd = each(_bdot_nt, p["vn"], ds)
            dw = each(lambda a, x: -_bdot_nt(a, x), dvn, st)
            new_ds = each(lambda x, e, a, ww, dv_: x * e + a - _bdot_tn(ww, dv_), ds, p["elast"], p["qd_do"], w, dvn)
            dglast = each(lambda e, x, d: e * jnp.sum(rsum(x.astype(f32) * d), axis=0, keepdims=True), p["elast"], st, ds)
            dr = each(lambda i, a, b: _bdot_tn(i, jnp.concatenate([a, b], axis=1)), inv, dvn, dw)
            dvb = [a[:, :DH] for a in dr]
            dkbg = [a[:, DH:] for a in dr]
            dlow = each(lambda a, b, x, y: -jnp.where(r > c, _bdot_nt(a, b) + _bdot_nt(x, y), 0.0), dvb, u, dkbg, w)
            dkk = each(jnp.multiply, dlow, p["dec"])
            mm = each(lambda a, b, x, y: a * b + x * y, dlow, p["low"], p["datt"], p["att"])
            dkb = each(lambda a, kk, b, e: _bdot(a, kk) + b * e, dkk, k, dkbg, eg)
            dk = each(lambda a, b, x, y, d, e, f, g: _bdot_tn(a, b) + _bdot_tn(x, y) + d * e + f * g,
                      dkk, kb, p["dqk"], q, dkd, el, dkb, bcol)
            dq = each(lambda a, kk, d, e: _bdot(a, kk) + d * e, p["dqk"], k, p["dqd"], eg)
            dv = each(jnp.multiply, dvb, bcol)
            dbeta = each(lambda a, b, x, y: rsum(a * b) + rsum(x * y), dkb, k, dvb, v)
            deg = each(lambda a, b, x, y: rsum(a * b) + rsum(x * y), dkbg, kb, p["dqd"], q)
            delc = each(lambda a, b, e: rsum(a * b) * e, dkd, k, el)
            dgc = each(lambda m, a, e, d: rsum(m) - rsum(eye * jnp.sum(m, axis=0, keepdims=True)) + a * e - d,
                       mm, deg, eg, delc)
            dgc = each(lambda g, d, l: g + jnp.where(p["rcol"] == CH - 1, jnp.sum(d, axis=0, keepdims=True) + l, 0.0),
                       dgc, delc, dglast)
            dg_acc = jnp.zeros((CH, 128), f32)
            db_acc = jnp.zeros((CH, 128), f32)
            rows = p["rows"]
            for h in hs:
                dq_ref[rows, h * DH:(h + 1) * DH] = dq[h]
                dk_ref[rows, h * DH:(h + 1) * DH] = dk[h]
                dv_ref[rows, h * DH:(h + 1) * DH] = dv[h]
                dg_acc = dg_acc + dgc[h] * (lane == h).astype(f32)
                db_acc = db_acc + dbeta[h] * (lane == H + h).astype(f32)
            dgb_ref[rows, :] = _hdot((r <= c).astype(f32), dg_acc) + db_acc
            return new_ds

        order = list(reversed(range(GDN_STEP)))
        pre = [before_state(s) for s in order]
        ds = [dstate[h] for h in hs]
        for p in pre:
            ds = after_state(p, ds)
        for h in hs:
            dstate[h] = ds[h]

    tb = GDN_STEP * CH
    n_steps = t // tb
    rev = lambda i: (n_steps - 1 - i, 0)
    rev4 = lambda i: (n_steps - 1 - i, 0, 0, 0)
    return pl.pallas_call(
        body, name="gdn_bwd", grid=(n_steps,),
        in_specs=[pl.BlockSpec((tb, D), rev), pl.BlockSpec((tb, D), rev), pl.BlockSpec((tb, D), rev),
                  pl.BlockSpec((tb, 128), rev), pl.BlockSpec((tb, D), rev),
                  pl.BlockSpec((GDN_STEP, H, DH, DH), rev4), pl.BlockSpec((GDN_STEP, H, CH, CH), rev4)],
        out_specs=[pl.BlockSpec((tb, D), rev), pl.BlockSpec((tb, D), rev), pl.BlockSpec((tb, D), rev),
                   pl.BlockSpec((tb, 128), rev)],
        out_shape=[jax.ShapeDtypeStruct((t, D), f32)] * 3 + [jax.ShapeDtypeStruct((t, 128), f32)],
        scratch_shapes=[pltpu.VMEM((H, DH, DH), f32)],
        compiler_params=_params(1),
    )(qn, kn, vc, gbeta, do, s_all, t_all)


def _pad_rows(w, rows=8):
    return jnp.pad(w, ((0, rows - w.shape[0]), (0, 0)))


_REST = ("w_up", "w_a_out", "w_b_out", "w_o", "w_down")


def _local_step(x, tgt, w, comm=None):
    g1 = w["norm_mix_g"].reshape(1, D)
    if comm is None:
        h1 = _rms_fwd(x, g1, name="rms1_fwd")
    else:
        h1, gathered = _rms_fwd(x, g1, name="rms1_fwd", exchange=comm.gather_first())
        w = {**w, **comm.finish_first(gathered)}
    w1, w2 = w["w1"], w["w2"]
    wa = _pad_rows(w["conv_a_w"])
    wg = _pad_rows(w["gdn_conv_w"])
    wf = _pad_rows(w["ffn_conv_w"])
    alog = jnp.pad(w["gdn_A_log"].reshape(1, H), ((0, 0), (0, 128 - H)))
    dtb = jnp.pad(w["gdn_dt_bias"].reshape(1, H), ((0, 0), (0, 128 - H)))
    g2 = w["norm_ffn_g"].reshape(1, D)
    g3 = w["norm_final_g"].reshape(1, D)
    gn = w["gdn_norm_g"].reshape(1, DH)

    if comm is None:
        pg = _matmul(h1, w1, name="mm_in", cols=(0, 6 * D), out_dtype=bf16)
        pq = _matmul(h1, w1, name="mm_in_qkv", cols=(6 * D, 3 * D))
    else:
        pg, gathered = _matmul(h1, w1, name="mm_in", cols=(0, 6 * D), out_dtype=bf16, exchange=comm.gather_rest())
        pq, gathered = _matmul(h1, w1, name="mm_in_qkv", cols=(6 * D, 3 * D), exchange=_gather_forward_exchange(gathered))
        w = {**w, **comm.finish_gather(gathered)}
    ya_in, qn, kn, vc, gbeta, p2 = _pre_fwd(pg, pq, h1, w2, wa, wg, alog, dtb)
    o, s_all, t_all = _gdn_fwd(qn, kn, vc, gbeta)
    yb_in = _post_fwd(o, pg, gn)
    ya = _matmul(ya_in, w["w_a_out"], name="mm_a", out_dtype=bf16)
    yb = _matmul(yb_in, w["w_b_out"], name="mm_b", out_dtype=bf16)
    mix = _mix_fwd(ya, yb, pg)
    x2 = _matmul(mix, w["w_o"], name="mm_o", add=x)
    h2 = _rms_fwd(x2, g2, name="rms2_fwd")
    up = _matmul(h2, w["w_up"], nt=True, name="mm_up", tn=DFF // 2, out_dtype=bf16)
    act = _ffn_fwd(up, wf)
    x3 = _matmul(act, w["w_down"], name="mm_down", add=x2, tm=512)
    loss_p, dx3, dx3b, dg3 = _final(x3, tgt, g3)

    grads = {"norm_final_g": dg3}
    dact = _matmul(dx3b, w["w_down"], nt=True, name="mm_down_dx", tm=512, tn=DFF, out_dtype=bf16)
    grads["w_down"] = _matmul_tn(act, dx3b, name="mm_down_dw", tm=DFF // 2)
    dc, dwf = _ffn_bwd1(dact, up, wf)
    grads["ffn_conv_w"] = dwf
    dup = _ffn_bwd2(dc, wf)
    dh2 = _matmul(dup, w["w_up"], name="mm_up_dx", tk=DFF)
    grads["w_up"] = _matmul_tn(dup, h2, name="mm_up_dw", tm=DFF // 2)
    dx2, dx2b, dg2 = _rms_bwd(dh2, x2, g2, dx3, name="rms2_bwd")
    grads["norm_ffn_g"] = dg2
    dmix = _matmul(dx2b, w["w_o"], nt=True, name="mm_o_dx", out_dtype=bf16)
    grads["w_o"] = _matmul_tn(mix, dx2b, name="mm_o_dw")
    dya, dyb, dgates = _mix_bwd(dmix, ya, yb, pg)
    dya_in = _matmul(dya, w["w_a_out"], nt=True, name="mm_a_dx", out_dtype=bf16)
    grads["w_a_out"] = _matmul_tn(ya_in, dya, name="mm_a_dw")
    dyb_in = _matmul(dyb, w["w_b_out"], nt=True, name="mm_b_dx")
    grads["w_b_out"] = _matmul_tn(yb_in, dyb, name="mm_b_dw")
    do, dz, dgn = _post_bwd(dyb_in, o, pg, gn)
    grads["gdn_norm_g"] = dgn
    dqn, dkn, dvc, dgb = _gdn_bwd(qn, kn, vc, gbeta, do, s_all, t_all)
    dbg, dca, dc4, dp2, dwa, dwg, dal, ddt, grads["w2"] = _pre_bwd1(pg, pq, p2, dya_in, dqn, dkn, dvc, dgb, gbeta, h1,
                                                                    wa, wg, alog, dtb)
    grads["conv_a_w"] = dwa
    grads["gdn_conv_w"] = dwg
    grads["gdn_A_log"] = dal
    grads["gdn_dt_bias"] = ddt
    if comm is None:
        dp1 = _pre_bwd2(dca, dc4, pg, dbg, dz, dgates, wa, wg)
        grads["w1"] = _matmul_tn(dp1, h1, name="mm_in_dw", tt=4096)
        dh1 = _matmul(dp1, w1, nt=True, name="mm_in_dx", tm=512, tk=NW1 // 2)
    else:
        exchange, blocks = comm.reduce_halves(_REST, grads)
        dp1, recv = _pre_bwd2(dca, dc4, pg, dbg, dz, dgates, wa, wg, exchange=exchange)
        exchange, sums = comm.reduce_sums(_REST, blocks, recv)
        grads["w1"], recv = _matmul_tn(dp1, h1, name="mm_in_dw", tt=4096, exchange=exchange)
        comm.finish_reduce(_REST, sums, recv)
        exchange, blocks = comm.reduce_halves(("w_in",), grads)
        exchange, sums = comm.reduce_sums(("w_in",), blocks, _run_exchange(exchange, name="rs_sibling_w_in"))
        dh1, recv = _matmul(dp1, w1, nt=True, name="mm_in_dx", tm=512, tk=NW1 // 2, exchange=exchange)
        comm.finish_reduce(("w_in",), sums, recv)
    dx, dg1 = _rms_bwd(dh1, x, g1, dx2, name="rms1_bwd", more=(dp2, w2), bf16_copy=False)
    grads["norm_mix_g"] = dg1
    return loss_p, dx, grads


_ANY = pl.BlockSpec(memory_space=pl.ANY)


def _remote(src, dst, send_sem, recv_sem, to):
    return pltpu.make_async_remote_copy(src_ref=src, dst_ref=dst, send_sem=send_sem, recv_sem=recv_sem,
                                        device_id=to, device_id_type=MESH)


def _run_exchange(exchange, *, name):
    arrays, shapes, sems, start, wait = exchange
    n_in, n_out = len(arrays), len(shapes)

    def body(*refs):
        start(refs[:n_in], refs[n_in:n_in + n_out], refs[n_in + n_out:])
        wait(refs[:n_in], refs[n_in:n_in + n_out], refs[n_in + n_out:])

    return pl.pallas_call(body, name=name, out_shape=list(shapes), in_specs=[_ANY] * n_in, out_specs=[_ANY] * n_out,
                          scratch_shapes=list(sems))(*arrays)


def _gather_exchange(shards):
    n = len(shards)

    def copies(x_refs, out_refs, sems):
        send_sems, recv_sems, local_sems = sems
        x, y, c = lax.axis_index("x"), lax.axis_index("y"), lax.axis_index("c")

        def flip(v, b):
            return v + b - 2 * v * b

        me, sibling = (x, y, c), (x, y, 1 - c)
        chip1, chip2, diag = (flip(x, 1 - c), flip(y, c)), (flip(x, c), flip(y, 1 - c)), (1 - x, 1 - y)

        def copy(a, k, blk, to, from_input=False):
            dst = out_refs[a].at[4 * blk[0] + 2 * blk[1] + blk[2]]
            return _remote(x_refs[a] if from_input else dst, dst, send_sems.at[a, k], recv_sems.at[a, k], to)

        mine = [pltpu.make_async_copy(x_refs[a], out_refs[a].at[4 * x + 2 * y + c], local_sems.at[a]) for a in range(n)]
        first = []
        for a in range(n):
            first += [copy(a, 0, me, sibling, from_input=True), copy(a, 1, me, (*chip1, c), from_input=True),
                      copy(a, 2, me, (*chip2, c), from_input=True)]
        return copy, mine, first, me, sibling, chip1, chip2, diag, c

    def start(x_refs, out_refs, sems):
        _, mine, first, *_ = copies(x_refs, out_refs, sems)
        for cp in mine + first:
            cp.start()

    def wait(x_refs, out_refs, sems):
        copy, mine, first, me, sibling, chip1, chip2, diag, c = copies(x_refs, out_refs, sems)
        passed = []

        def pass_on(cp):
            passed.append(cp)
            cp.start()

        for a in range(n):
            copy(a, 1, (*chip1, c), me).wait_recv()
            pass_on(copy(a, 3, (*chip1, c), (*chip2, c)))
            pass_on(copy(a, 4, (*chip1, c), sibling))
        for a in range(n):
            copy(a, 2, (*chip2, c), me).wait_recv()
            pass_on(copy(a, 5, (*chip2, c), sibling))
        for a in range(n):
            copy(a, 3, (*diag, c), me).wait_recv()
            pass_on(copy(a, 6, (*diag, c), sibling))
        for a in range(n):
            copy(a, 0, sibling, me).wait_recv()
            copy(a, 4, (*chip2, 1 - c), me).wait_recv()
            copy(a, 5, (*chip1, 1 - c), me).wait_recv()
            copy(a, 6, (*diag, 1 - c), me).wait_recv()
        for cp in first + passed:
            cp.wait_send()
        for cp in mine:
            cp.wait()

    shapes = [jax.ShapeDtypeStruct((N_DEV, *s.shape), s.dtype) for s in shards]
    sems = [pltpu.SemaphoreType.DMA((n, 7)), pltpu.SemaphoreType.DMA((n, 7)), pltpu.SemaphoreType.DMA((n,))]
    return shards, shapes, sems, start, wait


def _gather_direct_exchange(shards):
    n = len(shards)

    def copies(x_refs, out_refs, sems):
        send_sems, recv_sems, local_sems = sems
        x, y, c = lax.axis_index("x"), lax.axis_index("y"), lax.axis_index("c")
        targets = [(x, y, 1 - c), (1 - x, y, c), (x, 1 - y, c), (1 - x, 1 - y, c)]
        local, sends, recvs = [], [], []
        for a in range(n):
            mine = out_refs[a].at[4 * x + 2 * y + c]
            local.append(pltpu.make_async_copy(x_refs[a], mine, local_sems.at[a]))
            for k, to in enumerate(targets):
                theirs = out_refs[a].at[4 * to[0] + 2 * to[1] + to[2]]
                sends.append(_remote(x_refs[a], mine, send_sems.at[a, k], recv_sems.at[a, k], to))
                recvs.append(_remote(theirs, theirs, send_sems.at[a, k], recv_sems.at[a, k], to))
        return local, sends, recvs

    def start(x_refs, out_refs, sems):
        local, sends, _ = copies(x_refs, out_refs, sems)
        for cp in local + sends:
            cp.start()

    def wait(x_refs, out_refs, sems):
        local, sends, recvs = copies(x_refs, out_refs, sems)
        for cp in recvs:
            cp.wait_recv()
        for cp in sends:
            cp.wait_send()
        for cp in local:
            cp.wait()

    shapes = [jax.ShapeDtypeStruct((N_DEV, *s.shape), s.dtype) for s in shards]
    sems = [pltpu.SemaphoreType.DMA((n, 4)), pltpu.SemaphoreType.DMA((n, 4)), pltpu.SemaphoreType.DMA((n,))]
    return shards, shapes, sems, start, wait


def _gather_forward_exchange(gathered):
    n = len(gathered)

    def copies(_, out_refs, sems):
        send_sems, recv_sems = sems
        x, y, c = lax.axis_index("x"), lax.axis_index("y"), lax.axis_index("c")
        sibling = (x, y, 1 - c)
        sends, recvs = [], []
        for a in range(n):
            for j, (px, py) in enumerate([(1 - x, y), (x, 1 - y), (1 - x, 1 - y)]):
                mine = out_refs[a].at[4 * px + 2 * py + c]
                theirs = out_refs[a].at[4 * px + 2 * py + 1 - c]
                sends.append(_remote(mine, mine, send_sems.at[a, j], recv_sems.at[a, j], sibling))
                recvs.append(_remote(theirs, theirs, send_sems.at[a, j], recv_sems.at[a, j], sibling))
        return sends, recvs

    def start(in_refs, out_refs, sems):
        for cp in copies(in_refs, out_refs, sems)[0]:
            cp.start()

    def wait(in_refs, out_refs, sems):
        sends, recvs = copies(in_refs, out_refs, sems)
        for cp in recvs:
            cp.wait_recv()
        for cp in sends:
            cp.wait_send()

    shapes = [jax.ShapeDtypeStruct(g.shape, g.dtype) for g in gathered]
    sems = [pltpu.SemaphoreType.DMA((n, 3)), pltpu.SemaphoreType.DMA((n, 3))]
    return gathered, shapes, sems, start, wait, True


def _chips_exchange(hsums):
    n = len(hsums)

    def copies(h_refs, out_refs, sems):
        send_sems, recv_sems = sems
        x, y, c = lax.axis_index("x"), lax.axis_index("y"), lax.axis_index("c")
        chips = [(1 - x, y), (x, 1 - y), (1 - x, 1 - y)]
        return [_remote(h_refs[a].at[2 * px + py], out_refs[a].at[k], send_sems.at[a, k], recv_sems.at[a, k], (px, py, c))
                for a in range(n) for k, (px, py) in enumerate(chips)]

    def start(h_refs, out_refs, sems):
        for cp in copies(h_refs, out_refs, sems):
            cp.start()

    def wait(h_refs, out_refs, sems):
        for cp in copies(h_refs, out_refs, sems):
            cp.wait()

    shapes = [jax.ShapeDtypeStruct((3, *h.shape[1:]), h.dtype) for h in hsums]
    sems = [pltpu.SemaphoreType.DMA((n, 3)), pltpu.SemaphoreType.DMA((n, 3))]
    return hsums, shapes, sems, start, wait


def _sibling_exchange(halves):
    n = len(halves)

    def copies(p_refs, out_refs, sems):
        send_sems, recv_sems = sems
        x, y, c = lax.axis_index("x"), lax.axis_index("y"), lax.axis_index("c")
        return [_remote(p_refs[a], out_refs[a], send_sems.at[a], recv_sems.at[a], (x, y, 1 - c)) for a in range(n)]

    def start(p_refs, out_refs, sems):
        for cp in copies(p_refs, out_refs, sems):
            cp.start()

    def wait(p_refs, out_refs, sems):
        for cp in copies(p_refs, out_refs, sems):
            cp.wait()

    shapes = [jax.ShapeDtypeStruct(h.shape, h.dtype) for h in halves]
    return halves, shapes, [pltpu.SemaphoreType.DMA((n,)), pltpu.SemaphoreType.DMA((n,))], start, wait


_IN_RANGES = ((0, 3 * D, 0, 0), (3 * D, 6 * D, 0, 6 * D), (6 * D, 7 * D, 0, 3 * D), (7 * D, 7 * D + 16, 1, 0),
              (7 * D + 16, 9 * D + 16, 0, 4 * D))


def _col_pieces(width, ranges):
    pieces = []
    for d in range(N_DEV):
        lo, hi = d * width, (d + 1) * width
        for glo, ghi, mat, mlo in ranges:
            a, b = max(lo, glo), min(hi, ghi)
            if a < b:
                pieces.append((d, a - lo, b - lo, mat, mlo + a - glo))
    return pieces


def _cols_to_matrices(g, ranges, out_widths, *, name):
    _, rows, width = g.shape
    tb = 128
    pieces = _col_pieces(width, ranges)
    covered = [sum(p[2] - p[1] for p in pieces if p[3] == m) for m in range(len(out_widths))]

    def body(g_ref, *o_refs):
        for m, o_ref in enumerate(o_refs):
            if covered[m] < out_widths[m]:
                o_ref[...] = jnp.zeros_like(o_ref)
        for d, b0, b1, m, m0 in pieces:
            o_refs[m][:, m0:m0 + b1 - b0] = g_ref[d, :, b0:b1]

    return pl.pallas_call(
        body, name=name, grid=(rows // tb,), in_specs=[pl.BlockSpec((N_DEV, tb, width), lambda i: (0, i, 0))],
        out_specs=[pl.BlockSpec((tb, wo), lambda i: (i, 0)) for wo in out_widths],
        out_shape=[jax.ShapeDtypeStruct((rows, wo), g.dtype) for wo in out_widths], compiler_params=_params(1),
    )(g)


def _transposed_matrices_to_blocks(mats, ranges, width, *, name):
    rows = mats[0].shape[1]
    pieces = _col_pieces(width, ranges)

    def body(*refs):
        m_refs, g_ref = refs[:-1], refs[-1]
        for d, b0, b1, m, m0 in pieces:
            g_ref[d, b0:b1, :] = m_refs[m][m0:m0 + b1 - b0, :]

    return pl.pallas_call(
        body, name=name, grid=(rows // 128,),
        in_specs=[pl.BlockSpec((mt.shape[0], 128), lambda i: (0, i)) for mt in mats],
        out_specs=pl.BlockSpec((N_DEV, width, 128), lambda i: (0, 0, i)),
        out_shape=jax.ShapeDtypeStruct((N_DEV, width, rows), mats[0].dtype), compiler_params=_params(1),
    )(*mats)


def _row_block(rows):
    return 128 if rows % 128 == 0 else rows


def _half_bf16(g4, c_other, *, name):
    _, _, rows, width = g4.shape
    tb = _row_block(rows)

    def body(c_ref, p_ref, o_ref):
        o_ref[0] = p_ref[0, 0].astype(bf16)

    grid_spec = pltpu.PrefetchScalarGridSpec(
        num_scalar_prefetch=1, grid=(4, rows // tb),
        in_specs=[pl.BlockSpec((1, 1, tb, width), lambda j, i, c_ref: (j, c_ref[0], i, 0))],
        out_specs=pl.BlockSpec((1, tb, width), lambda j, i, c_ref: (j, i, 0)))
    return pl.pallas_call(
        body, name=name, grid_spec=grid_spec, out_shape=jax.ShapeDtypeStruct((4, rows, width), bf16),
        compiler_params=_params(2, _vmem_for(4 * tb * width, 2 * tb * width)),
    )(c_other, g4)


def _pair_sum(g4, recv, c_me, *, name):
    _, _, rows, width = g4.shape
    tb = _row_block(rows)

    def body(c_ref, p_ref, r_ref, o_ref, ob_ref):
        s = p_ref[0, 0] + r_ref[0].astype(f32)
        o_ref[0] = s
        ob_ref[0] = s.astype(bf16)

    blk = pl.BlockSpec((1, tb, width), lambda j, i, c_ref: (j, i, 0))
    grid_spec = pltpu.PrefetchScalarGridSpec(
        num_scalar_prefetch=1, grid=(4, rows // tb),
        in_specs=[pl.BlockSpec((1, 1, tb, width), lambda j, i, c_ref: (j, c_ref[0], i, 0)), blk],
        out_specs=[blk, blk])
    return pl.pallas_call(
        body, name=name, grid_spec=grid_spec,
        out_shape=[jax.ShapeDtypeStruct((4, rows, width), f32), jax.ShapeDtypeStruct((4, rows, width), bf16)],
        compiler_params=_params(2, _vmem_for(4 * tb * width, 2 * tb * width, 4 * tb * width, 2 * tb * width)),
    )(c_me, g4, recv)


def _adam_shard(hsum, recv, chip, w, m, v, *, name):
    _, rows, width = w.shape
    tb = _row_block(rows)

    def body(j_ref, h_ref, r_ref, w_ref, m_ref, v_ref, g_out, d_out, m_out, v_out):
        g = ((h_ref[0] + r_ref[0].astype(f32)) + r_ref[1].astype(f32)) + r_ref[2].astype(f32)
        delta, mn, vn = _adam_math(w_ref[0], g, m_ref[0], v_ref[0])
        g_out[0] = g
        d_out[0] = delta
        m_out[0] = mn
        v_out[0] = vn

    blk = pl.BlockSpec((1, tb, width), lambda i, j_ref: (0, i, 0))
    grid_spec = pltpu.PrefetchScalarGridSpec(
        num_scalar_prefetch=1, grid=(rows // tb,),
        in_specs=[pl.BlockSpec((1, tb, width), lambda i, j_ref: (j_ref[0], i, 0)),
                  pl.BlockSpec((3, tb, width), lambda i, j_ref: (0, i, 0)), blk, blk, blk],
        out_specs=[blk, blk, blk, blk])
    return pl.pallas_call(
        body, name=name, grid_spec=grid_spec, out_shape=[jax.ShapeDtypeStruct(w.shape, f32)] * 4,
        compiler_params=_params(1, _vmem_for(*[4 * tb * width] * 8, 6 * tb * width)),
    )(chip, hsum, recv, w, m, v)


def _sum_shard(hsum, recv, chip, *, name):
    _, rows, width = hsum.shape
    tb = _row_block(rows)

    def body(j_ref, h_ref, r_ref, g_out):
        g_out[...] = ((h_ref[0] + r_ref[0].astype(f32)) + r_ref[1].astype(f32)) + r_ref[2].astype(f32)

    grid_spec = pltpu.PrefetchScalarGridSpec(
        num_scalar_prefetch=1, grid=(rows // tb,),
        in_specs=[pl.BlockSpec((1, tb, width), lambda i, j_ref: (j_ref[0], i, 0)),
                  pl.BlockSpec((3, tb, width), lambda i, j_ref: (0, i, 0))],
        out_specs=pl.BlockSpec((tb, width), lambda i, j_ref: (i, 0)))
    return pl.pallas_call(body, name=name, grid_spec=grid_spec, out_shape=jax.ShapeDtypeStruct((rows, width), f32),
                          compiler_params=_params(1, _vmem_for(*[4 * tb * width] * 2, 6 * tb * width)))(chip, hsum, recv)


def _adam_columns(g, w, m, v, *, name):
    cols, _, rows = w.shape
    tb = cols // 2

    def body(g_ref, w_ref, m_ref, v_ref, d_out, m_out, v_out):
        delta, mn, vn = _adam_math(w_ref[...], g_ref[...], m_ref[...], v_ref[...])
        d_out[...] = delta
        m_out[...] = mn
        v_out[...] = vn

    blk = pl.BlockSpec((tb, 1, rows), lambda i: (i, 0, 0))
    return pl.pallas_call(
        body, name=name, grid=(cols // tb,), in_specs=[blk] * 4, out_specs=[blk] * 3,
        out_shape=[jax.ShapeDtypeStruct(w.shape, f32)] * 3,
        compiler_params=_params(1, _vmem_for(*[4 * tb * rows] * 7)),
    )(g, w, m, v)


R_SMALL = 8 + 8 * N_DEV
_SMALL_LANES = {"gdn_norm_g": (0, DH), "gdn_A_log": (DH, DH + H), "gdn_dt_bias": (2 * DH, 2 * DH + H)}
_LOSS_LANE = 3 * DH


def _pack_small(dg1, dg2, dg3, dgn, dal, ddt, loss_p, dwa, dwg, dwf):
    def body(dg1_ref, dg2_ref, dg3_ref, dgn_ref, dal_ref, ddt_ref, loss_ref, dwa_ref, dwg_ref, dwf_ref, o_ref):
        def total(ref):
            return jnp.sum(ref[...], axis=0, keepdims=True)

        o_ref[...] = jnp.zeros_like(o_ref)
        o_ref[0:1, :] = total(dg1_ref)
        o_ref[1:2, :] = total(dg2_ref)
        o_ref[2:3, :] = total(dg3_ref)
        o_ref[3:4, 0:DH] = total(dgn_ref)
        o_ref[3:4, DH:2 * DH] = total(dal_ref)
        o_ref[3:4, 2 * DH:3 * DH] = total(ddt_ref)
        o_ref[3:4, 3 * DH:4 * DH] = total(loss_ref)
        for d in range(N_DEV):
            base = 8 + 8 * d
            o_ref[base:base + 3, 0:128] = dwa_ref[0:3, 128 * d:128 * (d + 1)]
            o_ref[base:base + 4, 128:512] = dwg_ref[0:4, 384 * d:384 * (d + 1)]
            o_ref[base + 4:base + 7, 0:704] = dwf_ref[0:3, 704 * d:704 * (d + 1)]

    return pl.pallas_call(body, name="pack_small", out_shape=jax.ShapeDtypeStruct((R_SMALL, D), f32))(
        dg1, dg2, dg3, dgn, dal, ddt, loss_p, dwa, dwg, dwf)


_SMALL = ("norm_mix_g", "norm_ffn_g", "norm_final_g", "gdn_norm_g", "gdn_A_log", "gdn_dt_bias",
          "conv_a_w", "gdn_conv_w", "ffn_conv_w")


def _adam_small(gath, me, w, m, v):
    arrays = [t[n] for n in _SMALL for t in (w, m, v)]

    def body(me_ref, ga_ref, gb_ref, *refs):
        ins, outs = refs[:len(arrays)], refs[len(arrays):]
        ga, gb = ga_ref[0], gb_ref[0]
        for s in range(1, N_DEV):
            ga = ga + ga_ref[s]
            gb = gb + gb_ref[s]
        grads = {"norm_mix_g": ga[0:1, :], "norm_ffn_g": ga[1:2, :], "norm_final_g": ga[2:3, :],
                 "conv_a_w": gb[0:3, 0:128], "gdn_conv_w": gb[0:4, 128:512], "ffn_conv_w": gb[4:7, 0:704]}
        for n, (lo, hi) in _SMALL_LANES.items():
            grads[n] = ga[3:4, lo:hi]
        for i, n in enumerate(_SMALL):
            three_d = len(w[n].shape) == 3
            wv, mv, vv = (r[0] if three_d else r[...] for r in ins[3 * i:3 * i + 3])
            delta, mn, vn = _adam_math(wv, grads[n], mv, vv)
            for o_ref, val in zip(outs[4 * i:4 * i + 4], (grads[n], delta, mn, vn)):
                if three_d:
                    o_ref[0] = val
                else:
                    o_ref[...] = val
        outs[-1][...] = ga[3:4, _LOSS_LANE:_LOSS_LANE + 1]

    def whole(shape):
        return pl.BlockSpec(shape, lambda i, me_ref: (0,) * len(shape))

    grid_spec = pltpu.PrefetchScalarGridSpec(
        num_scalar_prefetch=1, grid=(1,),
        in_specs=[pl.BlockSpec((N_DEV, 8, D), lambda i, me_ref: (0, 0, 0)),
                  pl.BlockSpec((N_DEV, 8, D), lambda i, me_ref: (0, 1 + me_ref[0], 0))] + [whole(a.shape) for a in arrays],
        out_specs=[whole(w[n].shape) for n in _SMALL for _ in range(4)] + [whole((1, 1))])
    res = pl.pallas_call(
        body, name="adam_small", grid_spec=grid_spec,
        out_shape=[jax.ShapeDtypeStruct(w[n].shape, f32) for n in _SMALL for _ in range(4)]
        + [jax.ShapeDtypeStruct((1, 1), f32)],
        compiler_params=_params(1),
    )(me, gath, gath, *arrays)
    return {n: tuple(res[4 * i:4 * i + 4]) for i, n in enumerate(_SMALL)}, res[-1]


def _adam_math(w, g, m, v):
    m = ADAM_B1 * m + (1.0 - ADAM_B1) * g
    v = ADAM_B2 * v + (1.0 - ADAM_B2) * jnp.square(g)
    m_hat = m / (1.0 - ADAM_B1 ** ADAM_STEP)
    v_hat = v / (1.0 - ADAM_B2 ** ADAM_STEP)
    delta = -ADAM_LR * (m_hat / (jnp.sqrt(v_hat) + ADAM_EPS) + ADAM_WD * w)
    return delta, m, v


_WEIGHTS = ("norm_mix_g", "w_in", "conv_a_w", "gdn_conv_w", "gdn_A_log", "gdn_dt_bias", "gdn_norm_g", "w_a_out",
            "w_b_out", "w_o", "norm_ffn_g", "w_up", "ffn_conv_w", "w_down", "norm_final_g")
_CONVS = ("conv_a_w", "gdn_conv_w", "ffn_conv_w")


class _StepExchanges:
    def __init__(self, wts, mom, var, c_me, chip):
        self.wts, self.mom, self.var, self.c_me, self.chip = wts, mom, var, c_me, chip
        self.results = {}

    def gather_first(self):
        return _gather_exchange([self.wts["w_in"][0].astype(bf16)] + [self.wts[n][0] for n in _CONVS])

    def finish_first(self, gathered):
        g_in, gc_a, gc_g, gc_f = gathered
        w1, w2 = _cols_to_matrices(g_in, _IN_RANGES, (NW1, 128), name="relay_w_in")
        return {"w1": w1, "w2": w2, "conv_a_w": gc_a.transpose(1, 0, 2).reshape(3, D),
                "gdn_conv_w": gc_g.transpose(1, 0, 2).reshape(4, 3 * D),
                "ffn_conv_w": gc_f.transpose(1, 0, 2).reshape(3, 2 * DFF)}

    def gather_rest(self):
        return _gather_direct_exchange([self.wts[n][0].astype(bf16) for n in _REST])

    def finish_gather(self, gathered):
        g_up, g_a, g_b, g_o, g_down = gathered
        return {"w_up": g_up.reshape(2 * DFF, D), "w_a_out": g_a.reshape(D, D), "w_b_out": g_b.reshape(D, D),
                "w_o": g_o.reshape(D, D), "w_down": g_down.reshape(DFF, D)}

    def reduce_halves(self, names, grads):
        blocks = []
        for n in names:
            if n == "w_in":
                g = _transposed_matrices_to_blocks([grads["w1"], grads["w2"]], _IN_RANGES, R_IN, name="relay_dw_in")
                blocks.append(g.reshape(4, 2, R_IN, D))
            else:
                blocks.append(grads[n].reshape(4, 2, *self.wts[n].shape[1:]))
        return _sibling_exchange([_half_bf16(g, 1 - self.c_me, name="rs_half_" + n) for n, g in zip(names, blocks)]), blocks

    def reduce_sums(self, names, blocks, recv):
        sums = [_pair_sum(g, r, self.c_me, name="rs_sum_" + n) for n, g, r in zip(names, blocks, recv)]
        return _chips_exchange([s[1] for s in sums]), [s[0] for s in sums]

    def finish_reduce(self, names, sums, recv):
        for n, s, r in zip(names, sums, recv):
            if n == "w_in":
                g = _sum_shard(s, r, self.chip, name="rs_total_w_in")[:, None, :]
                w, m, v = (jnp.transpose(t[n], (2, 0, 1)) for t in (self.wts, self.mom, self.var))
                res = (g, *_adam_columns(g, w, m, v, name="adam_w_in"))
                self.results[n] = tuple(jnp.transpose(a, (1, 2, 0)) for a in res)
            else:
                self.results[n] = _adam_shard(s, r, self.chip, self.wts[n], self.mom[n], self.var[n], name="adam_" + n)


def kernel(x, norm_mix_g, w_in, conv_a_w, gdn_conv_w, gdn_A_log, gdn_dt_bias, gdn_norm_g, w_a_out, w_b_out, w_o, norm_ffn_g, w_up, ffn_conv_w, w_down, norm_final_g, loss_target, m_norm_mix_g, m_w_in, m_conv_a_w, m_gdn_conv_w, m_gdn_A_log, m_gdn_dt_bias, m_gdn_norm_g, m_w_a_out, m_w_b_out, m_w_o, m_norm_ffn_g, m_w_up, m_ffn_conv_w, m_w_down, m_norm_final_g, v_norm_mix_g, v_w_in, v_conv_a_w, v_gdn_conv_w, v_gdn_A_log, v_gdn_dt_bias, v_gdn_norm_g, v_w_a_out, v_w_b_out, v_w_o, v_norm_ffn_g, v_w_up, v_ffn_conv_w, v_w_down, v_norm_final_g):
    wts = dict(zip(_WEIGHTS, (norm_mix_g, w_in, conv_a_w, gdn_conv_w, gdn_A_log, gdn_dt_bias, gdn_norm_g, w_a_out,
                              w_b_out, w_o, norm_ffn_g, w_up, ffn_conv_w, w_down, norm_final_g)))
    mom = dict(zip(_WEIGHTS, (m_norm_mix_g, m_w_in, m_conv_a_w, m_gdn_conv_w, m_gdn_A_log, m_gdn_dt_bias,
                              m_gdn_norm_g, m_w_a_out, m_w_b_out, m_w_o, m_norm_ffn_g, m_w_up, m_ffn_conv_w,
                              m_w_down, m_norm_final_g)))
    var = dict(zip(_WEIGHTS, (v_norm_mix_g, v_w_in, v_conv_a_w, v_gdn_conv_w, v_gdn_A_log, v_gdn_dt_bias,
                              v_gdn_norm_g, v_w_a_out, v_w_b_out, v_w_o, v_norm_ffn_g, v_w_up, v_ffn_conv_w,
                              v_w_down, v_norm_final_g)))
    cx, cy, cc = lax.axis_index("x"), lax.axis_index("y"), lax.axis_index("c")
    c_me = jnp.reshape(cc, (1,)).astype(jnp.int32)
    chip = jnp.reshape(2 * cx + cy, (1,)).astype(jnp.int32)
    me = jnp.reshape(4 * cx + 2 * cy + cc, (1,)).astype(jnp.int32)

    def with_up_transposed(t):
        return {**t, "w_up": jnp.swapaxes(t["w_up"], 1, 2)}

    comm = _StepExchanges(with_up_transposed(wts), with_up_transposed(mom), with_up_transposed(var), c_me, chip)
    replicated = {n: wts[n] for n in ("norm_mix_g", "norm_ffn_g", "norm_final_g", "gdn_norm_g", "gdn_A_log", "gdn_dt_bias")}
    loss_p, dx, grads = _local_step(x[0], loss_target[0], replicated, comm)
    res = comm.results
    res["w_up"] = tuple(jnp.swapaxes(a, 1, 2) for a in res["w_up"])

    small = _pack_small(grads["norm_mix_g"], grads["norm_ffn_g"], grads["norm_final_g"], grads["gdn_norm_g"],
                        grads["gdn_A_log"], grads["gdn_dt_bias"], loss_p, grads["conv_a_w"], grads["gdn_conv_w"],
                        grads["ffn_conv_w"])
    (small_all,) = _run_exchange(_gather_exchange([small]), name="ag_small")

    def raw(t):
        return {n: t[n].reshape(1, D) if n == "norm_final_g" else t[n] for n in _SMALL}

    res_small, loss = _adam_small(small_all, me, raw(wts), raw(mom), raw(var))
    for n in _SMALL:
        res[n] = tuple(a.reshape(wts[n].shape) for a in res_small[n])
    outs = [[res[n][i] for n in _WEIGHTS] for i in range(4)]
    return (loss.reshape(()), dx[None], *outs[0], *outs[1], *outs[2], *outs[3])
```

```python
import jax
import jax.numpy as jnp
from jax import lax
from jax.experimental import pallas as pl
from jax.experimental.pallas import tpu as pltpu

f32 = jnp.float32
bf16 = jnp.bfloat16

D = 1024
H = 8
DH = 128
CH = 64
GDN_STEP = 2
ROW_BLOCK = 512
DFF = 2816
NW1 = 9216
EPS = 1e-6
N_DEV = 8

ADAM_LR = 0.001
ADAM_B1 = 0.9
ADAM_B2 = 0.999
ADAM_EPS = 1e-08
ADAM_WD = 0.01
ADAM_STEP = 10

VMEM_LIMIT_BYTES = 48 * 1024 * 1024
VMEM_MAX_BYTES = 56 * 1024 * 1024

R_IN, R_UP = 1154, 704

_HI = lax.Precision.HIGHEST
MESH = pl.DeviceIdType.MESH


def _params(n_grid, vmem_bytes=None):
    return pltpu.CompilerParams(dimension_semantics=("arbitrary",) * n_grid,
                                vmem_limit_bytes=VMEM_LIMIT_BYTES if vmem_bytes is None else vmem_bytes)


def _vmem_for(*block_bytes, extra=0):
    need = 2 * sum(block_bytes) + extra + 4 * 1024 * 1024
    return min(max(need, VMEM_LIMIT_BYTES), VMEM_MAX_BYTES)


def _bdot(a, b):
    return jnp.dot(a.astype(bf16), b.astype(bf16), preferred_element_type=f32)


def _bdot_nt(a, b):
    return lax.dot_general(a.astype(bf16), b.astype(bf16), (((1,), (1,)), ((), ())), preferred_element_type=f32)


def _bdot_tn(a, b):
    return lax.dot_general(a.astype(bf16), b.astype(bf16), (((0,), (0,)), ((), ())), preferred_element_type=f32)


def _hdot(a, b):
    return jnp.dot(a, b, preferred_element_type=f32, precision=_HI)


def _idot(a, b):
    return jnp.dot(a, b, preferred_element_type=f32, precision=lax.Precision.HIGH)


def _sigmoid(x):
    return 1.0 / (1.0 + jnp.exp(-x))


def _softplus(x):
    return jnp.maximum(x, 0.0) + jnp.log(1.0 + jnp.exp(-jnp.abs(x)))


def _shift_down(x, halo, j):
    if j == 0:
        return x
    xr = pltpu.roll(x, j, 0)
    hr = pltpu.roll(halo, j, 0)
    r8 = lax.broadcasted_iota(jnp.int32, hr.shape, 0)
    top = jnp.where(r8 < j, hr, xr[:8])
    return jnp.concatenate([top, xr[8:]], axis=0)


def _shift_up(x, halo, j):
    if j == 0:
        return x
    n = x.shape[0]
    xr = pltpu.roll(x, n - j, 0)
    hr = pltpu.roll(halo, 8 - j, 0)
    r8 = lax.broadcasted_iota(jnp.int32, hr.shape, 0)
    bot = jnp.where(r8 >= 8 - j, hr, xr[n - 8:])
    return jnp.concatenate([xr[:n - 8], bot], axis=0)


def _taps_down(x, halo, k):
    return [_shift_down(x, halo, k - 1 - j) for j in range(k)]


def _strip(i, base=0):
    return slice(base + i * 128, base + (i + 1) * 128)


def _strip_taps(x, halo, first, k):
    return _taps_down(x, jnp.where(first, 0.0, halo), k)


def _strip_conv(w_ref, sl, taps):
    out = w_ref[0:1, sl] * taps[0]
    for j in range(1, len(taps)):
        out = out + w_ref[j:j + 1, sl] * taps[j]
    return out


def _strip_weight_grad(dw_ref, sl, dy, taps):
    for j, tap in enumerate(taps):
        dw_ref[j:j + 1, sl] += jnp.sum(dy * tap, axis=0, keepdims=True)


def _strip_conv_up(dy, halo, last, w_ref, sl, k):
    halo = jnp.where(last, 0.0, halo)
    out = w_ref[k - 1:k, sl] * dy
    for j in range(k - 1):
        out = out + w_ref[j:j + 1, sl] * _shift_up(dy, halo, k - 1 - j)
    return out


def _row(tb, w, col=0):
    return pl.BlockSpec((tb, w), lambda i: (i, col))


def _prev(tb, w, col=0, rows=8):
    return pl.BlockSpec((rows, w), lambda i: (jnp.maximum(i * (tb // rows) - 1, 0), col))


def _next(tb, w, n_rows, col=0, rows=8):
    last = n_rows // rows - 1
    return pl.BlockSpec((rows, w), lambda i: (jnp.minimum((i + 1) * (tb // rows), last), col))


def _f32(ref, sl):
    return ref[:, sl].astype(f32)


def _halo_before(ref, sl):
    h = _f32(ref, sl)
    return h[h.shape[0] - 8:]


def _halo_after(ref, sl):
    return _f32(ref, sl)[:8]


def _fixed(shape):
    return pl.BlockSpec(shape, lambda i: (0,) * len(shape))


def _pick(n, prefs):
    for p in prefs:
        if n % p == 0:
            return p
    return n


def _matmul(a, b, *, name, nt=False, add=None, tm=1024, tn=1024, tk=None, out_dtype=f32, cols=None, exchange=None):
    m, kd = a.shape
    col0, n = cols if cols is not None else (0, b.shape[0] if nt else b.shape[1])
    tm = _pick(m, (tm, 512, 256))
    tn = _pick(n, (tn, 1024, 512, 128))
    tk = kd if tk is None else tk
    nk = kd // tk
    assert nk == 1 or out_dtype == f32
    assert col0 % tn == 0 and not (nt and cols)
    j0 = col0 // tn
    dims = (((1,), (1,)), ((), ())) if nt else (((1,), (0,)), ((), ()))

    def body(a_ref, b_ref, *rest):
        o_ref = rest[-1]
        part = lax.dot_general(a_ref[...], b_ref[...], dims, preferred_element_type=f32)
        if nk == 1:
            o_ref[...] = (part if add is None else part + rest[0][...]).astype(out_dtype)
            return
        k = pl.program_id(2)

        @pl.when(k == 0)
        def _():
            o_ref[...] = part if add is None else part + rest[0][...]

        @pl.when(k > 0)
        def _():
            o_ref[...] += part

    b_spec = pl.BlockSpec((tn, tk), lambda i, j, k: (j, k)) if nt else pl.BlockSpec((tk, tn), lambda i, j, k: (k, j + j0))
    in_specs = [pl.BlockSpec((tm, tk), lambda i, j, k: (i, k)), b_spec]
    args = [a, b]
    if add is not None:
        in_specs.append(pl.BlockSpec((tm, tn), lambda i, j, k: (i, j)))
        args.append(add)
    vmem = _vmem_for(2 * tm * tk, 2 * tk * tn, tm * tn * jnp.dtype(out_dtype).itemsize,
                     4 * tm * tn if add is not None else 0, extra=4 * tm * tn)
    return _call_with_exchange(
        body, exchange, name=name, grid=(m // tm, n // tn, nk), in_specs=in_specs,
        out_specs=pl.BlockSpec((tm, tn), lambda i, j, k: (i, j)),
        out_shape=jax.ShapeDtypeStruct((m, n), out_dtype), args=args, vmem_bytes=vmem)


def _call_with_exchange(body, exchange, *, name, grid, in_specs, out_specs, out_shape, args, vmem_bytes=None):
    if exchange is None:
        return pl.pallas_call(body, name=name, grid=grid, in_specs=in_specs, out_specs=out_specs, out_shape=out_shape,
                              compiler_params=_params(len(grid), vmem_bytes))(*args)
    x_arrays, x_shapes, x_sems, start, wait = exchange[:5]
    n_in, n_xin, n_xout = len(args), len(x_arrays), len(x_shapes)
    aliases = {n_in + i: 1 + i for i in range(n_xin)} if len(exchange) > 5 and exchange[5] else {}

    def full_body(*refs):
        c_in, x_in = refs[:n_in], refs[n_in:n_in + n_xin]
        c_out = refs[n_in + n_xin]
        x_out = refs[n_in + n_xin + 1:n_in + n_xin + 1 + n_xout]
        sems = refs[n_in + n_xin + 1 + n_xout:]
        ids = [pl.program_id(d) for d in range(len(grid))]
        first, last = ids[0] == 0, ids[0] == grid[0] - 1
        for d in range(1, len(grid)):
            first = first & (ids[d] == 0)
            last = last & (ids[d] == grid[d] - 1)

        @pl.when(first)
        def _():
            start(x_in, x_out, sems)

        body(*c_in, c_out)

        @pl.when(last)
        def _():
            wait(x_in, x_out, sems)

    res = pl.pallas_call(
        full_body, name=name, grid=grid, in_specs=list(in_specs) + [_ANY] * n_xin,
        out_specs=[out_specs] + [_ANY] * n_xout, out_shape=[out_shape] + list(x_shapes),
        scratch_shapes=list(x_sems), input_output_aliases=aliases, compiler_params=_params(len(grid), vmem_bytes),
    )(*args, *x_arrays)
    return res[0], list(res[1:])


def _matmul_tn(a, b, *, name, tm=1024, tn=1024, tt=2048, exchange=None):
    t, m = a.shape
    _, n = b.shape
    tm = _pick(m, (tm, 1024, 512, 128))
    tn = _pick(n, (tn, 1024, 512, 128))
    tt = _pick(t, (tt, 2048, 1024, 512, 256))
    nt = t // tt

    def body(a_ref, b_ref, o_ref):
        k = pl.program_id(2)
        part = lax.dot_general(a_ref[...], b_ref[...], (((0,), (0,)), ((), ())), preferred_element_type=f32)

        @pl.when(k == 0)
        def _():
            o_ref[...] = part

        @pl.when(k > 0)
        def _():
            o_ref[...] += part

    return _call_with_exchange(
        body, exchange, name=name, grid=(m // tm, n // tn, nt),
        in_specs=[pl.BlockSpec((tt, tm), lambda i, j, k: (k, i)), pl.BlockSpec((tt, tn), lambda i, j, k: (k, j))],
        out_specs=pl.BlockSpec((tm, tn), lambda i, j, k: (i, j)),
        out_shape=jax.ShapeDtypeStruct((m, n), f32), args=[a, b],
        vmem_bytes=_vmem_for(2 * tt * tm, 2 * tt * tn, 4 * tm * tn, extra=4 * tm * tn + 2 * tt * tm))


def _rms_fwd(x, g, *, name, exchange=None):
    t = x.shape[0]
    tb = _pick(t, (ROW_BLOCK, 256, 128))

    def body(x_ref, g_ref, h_ref):
        xv = x_ref[...]
        r = lax.rsqrt(jnp.mean(xv * xv, axis=-1, keepdims=True) + EPS)
        h_ref[...] = (xv * r * g_ref[...]).astype(bf16)

    return _call_with_exchange(
        body, exchange, name=name, grid=(t // tb,), in_specs=[_row(tb, D), _fixed((1, D))], out_specs=_row(tb, D),
        out_shape=jax.ShapeDtypeStruct((t, D), bf16), args=[x, g])


def _rms_bwd(dh, x, g, dres, *, name, more=None, bf16_copy=True):
    t = x.shape[0]
    tb = _pick(t, (ROW_BLOCK, 256, 128))

    def body(dh_ref, x_ref, g_ref, dres_ref, *rest):
        dx_ref, dg_ref = rest[-3 if bf16_copy else -2], rest[-1]
        xv = x_ref[...]
        r = lax.rsqrt(jnp.mean(xv * xv, axis=-1, keepdims=True) + EPS)
        xh = xv * r
        dy = dh_ref[...]
        if more is not None:
            dy = dy + lax.dot_general(rest[0][...], rest[1][...], (((1,), (1,)), ((), ())), preferred_element_type=f32)
        dyg = dy * g_ref[...]
        dx = dres_ref[...] + r * (dyg - xh * jnp.mean(dyg * xh, axis=-1, keepdims=True))
        dx_ref[...] = dx
        if bf16_copy:
            rest[-2][...] = dx.astype(bf16)

        @pl.when(pl.program_id(0) == 0)
        def _():
            dg_ref[...] = jnp.zeros_like(dg_ref)

        dg_ref[...] += jnp.sum((dy * xh).reshape(tb // 8, 8, D), axis=0)

    in_specs, args = [_row(tb, D), _row(tb, D), _fixed((1, D)), _row(tb, D)], [dh, x, g, dres]
    if more is not None:
        in_specs += [_row(tb, 128), _fixed(more[1].shape)]
        args += list(more)
    dx_dtypes = (f32, bf16) if bf16_copy else (f32,)
    return pl.pallas_call(
        body, name=name, grid=(t // tb,), in_specs=in_specs,
        out_specs=[_row(tb, D) for _ in dx_dtypes] + [_fixed((8, D))],
        out_shape=[jax.ShapeDtypeStruct((t, D), dt) for dt in dx_dtypes] + [jax.ShapeDtypeStruct((8, D), f32)],
        compiler_params=_params(1),
    )(*args)


def _gdn_gates(ab, alog, dtb):
    lane = lax.broadcasted_iota(jnp.int32, ab.shape, 1)
    g = -jnp.exp(alog) * _softplus(ab + dtb)
    beta = _sigmoid(ab)
    return jnp.where(lane < H, g, jnp.where(lane < 2 * H, beta, 0.0))


def _pre_fwd(pg, pq, h1, w2, wa, wg, alog, dtb):
    t = pg.shape[0]
    tb = _pick(t, (ROW_BLOCK, 128))

    def body(p0_ref, p0h_ref, pq_ref, pqh_ref, h1_ref, w2_ref, wa_ref, wg_ref, alog_ref, dtb_ref,
             ya_ref, qn_ref, kn_ref, vc_ref, gb_ref, p2_ref):
        first = pl.program_id(0) == 0
        p2_ref[...] = jnp.dot(h1_ref[...], w2_ref[...], preferred_element_type=f32)
        for i in range(D // 128):
            sl, cg, xv = _strip(i), _strip(i, D), _strip(i, 2 * D)
            taps = _strip_taps(_f32(p0_ref, cg) * _f32(p0_ref, xv), _halo_before(p0h_ref, cg) * _halo_before(p0h_ref, xv),
                               first, 3)
            ya_ref[:, sl] = (_f32(p0_ref, sl) * _strip_conv(wa_ref, sl, taps)).astype(bf16)
        for part, out_ref, scale in ((0, qn_ref, DH ** -0.5), (1, kn_ref, 1.0), (2, vc_ref, None)):
            for h in range(H):
                sl = _strip(h, part * D)
                s = _strip_conv(wg_ref, sl, _strip_taps(pq_ref[:, sl], pqh_ref[:, sl], first, 4))
                s = s * _sigmoid(s)
                if scale is not None:
                    s = s * (lax.rsqrt(jnp.sum(s * s, axis=-1, keepdims=True) + EPS) * scale)
                out_ref[:, _strip(h)] = s
        gb_ref[...] = _gdn_gates(p2_ref[...], alog_ref[...], dtb_ref[...])

    return pl.pallas_call(
        body, name="pre_fwd", grid=(t // tb,),
        in_specs=[_row(tb, 3 * D, 0), _prev(tb, 3 * D, 0, rows=16), _row(tb, 3 * D), _prev(tb, 3 * D), _row(tb, D),
                  _fixed((D, 128)), _fixed((8, D)), _fixed((8, 3 * D)), _fixed((1, 128)), _fixed((1, 128))],
        out_specs=[_row(tb, D), _row(tb, D), _row(tb, D), _row(tb, D), _row(tb, 128), _row(tb, 128)],
        out_shape=[jax.ShapeDtypeStruct((t, D), bf16), jax.ShapeDtypeStruct((t, D), f32),
                   jax.ShapeDtypeStruct((t, D), f32), jax.ShapeDtypeStruct((t, D), f32),
                   jax.ShapeDtypeStruct((t, 128), f32), jax.ShapeDtypeStruct((t, 128), f32)],
        compiler_params=_params(1),
    )(pg, pg, pq, pq, h1, w2, wa, wg, alog, dtb)


_Z_COL, _GA_COL, _GB_COL = 3, 4, 5


def _post_fwd(o, pg, gn):
    t = o.shape[0]
    tb = _pick(t, (ROW_BLOCK, 256, 128))

    def body(o_ref, z_ref, gn_ref, yb_ref):
        for h in range(H):
            sl = slice(h * DH, (h + 1) * DH)
            oh = o_ref[:, sl]
            z = _f32(z_ref, sl)
            r = lax.rsqrt(jnp.mean(oh * oh, axis=-1, keepdims=True) + EPS)
            yb_ref[:, sl] = (oh * r * gn_ref[...] * (z * _sigmoid(z))).astype(bf16)

    return pl.pallas_call(
        body, name="post_fwd", grid=(t // tb,), in_specs=[_row(tb, D), _row(tb, D, _Z_COL), _fixed((1, DH))],
        out_specs=_row(tb, D), out_shape=jax.ShapeDtypeStruct((t, D), bf16), compiler_params=_params(1),
    )(o, pg, gn)


def _post_bwd(dyb, o, pg, gn):
    t = o.shape[0]
    tb = _pick(t, (ROW_BLOCK, 256, 128))

    def body(dyb_ref, o_ref, z_ref, gn_ref, do_ref, dz_ref, dgn_ref):
        @pl.when(pl.program_id(0) == 0)
        def _():
            dgn_ref[...] = jnp.zeros_like(dgn_ref)

        gn_v = gn_ref[...]
        acc = jnp.zeros((8, DH), f32)
        for h in range(H):
            sl = slice(h * DH, (h + 1) * DH)
            oh = o_ref[:, sl]
            z = _f32(z_ref, sl)
            dy = dyb_ref[:, sl]
            r = lax.rsqrt(jnp.mean(oh * oh, axis=-1, keepdims=True) + EPS)
            on = oh * r
            sg = _sigmoid(z)
            sz = z * sg
            don = dy * sz
            dz_ref[:, sl] = (dy * on * gn_v * (sg * (1.0 + z * (1.0 - sg)))).astype(bf16)
            acc = acc + jnp.sum((don * on).reshape(tb // 8, 8, DH), axis=0)
            doh = don * gn_v
            do_ref[:, sl] = r * (doh - on * jnp.mean(doh * on, axis=-1, keepdims=True))
        dgn_ref[...] += acc

    return pl.pallas_call(
        body, name="post_bwd", grid=(t // tb,),
        in_specs=[_row(tb, D), _row(tb, D), _row(tb, D, _Z_COL), _fixed((1, DH))],
        out_specs=[_row(tb, D), _row(tb, D), _fixed((8, DH))],
        out_shape=[jax.ShapeDtypeStruct((t, D), f32), jax.ShapeDtypeStruct((t, D), bf16),
                   jax.ShapeDtypeStruct((8, DH), f32)],
        compiler_params=_params(1),
    )(dyb, o, pg, gn)


def _mix_fwd(ya, yb, pg):
    t = ya.shape[0]
    tb = _pick(t, (ROW_BLOCK, 256, 128))

    def body(ya_ref, yb_ref, ga_ref, gb_ref, mix_ref):
        ya_v, yb_v = ya_ref[...].astype(f32), yb_ref[...].astype(f32)
        mix = _sigmoid(ga_ref[...].astype(f32)) * ya_v + _sigmoid(gb_ref[...].astype(f32)) * yb_v
        mix_ref[...] = mix.astype(bf16)

    return pl.pallas_call(
        body, name="mix_fwd", grid=(t // tb,),
        in_specs=[_row(tb, D), _row(tb, D), _row(tb, D, _GA_COL), _row(tb, D, _GB_COL)],
        out_specs=_row(tb, D), out_shape=jax.ShapeDtypeStruct((t, D), bf16), compiler_params=_params(1),
    )(ya, yb, pg, pg)


def _mix_bwd(dmix, ya, yb, pg):
    t = ya.shape[0]
    tb = _pick(t, (ROW_BLOCK, 256, 128))

    def body(dm_ref, ya_ref, yb_ref, ga_ref, gb_ref, dya_ref, dyb_ref, dg_ref):
        dm = dm_ref[...].astype(f32)
        sa = _sigmoid(ga_ref[...].astype(f32))
        sb = _sigmoid(gb_ref[...].astype(f32))
        dya_ref[...] = (dm * sa).astype(bf16)
        dyb_ref[...] = (dm * sb).astype(bf16)
        dg_ref[:, :D] = (dm * ya_ref[...].astype(f32) * sa * (1.0 - sa)).astype(bf16)
        dg_ref[:, D:] = (dm * yb_ref[...].astype(f32) * sb * (1.0 - sb)).astype(bf16)

    return pl.pallas_call(
        body, name="mix_bwd", grid=(t // tb,),
        in_specs=[_row(tb, D), _row(tb, D), _row(tb, D), _row(tb, D, _GA_COL), _row(tb, D, _GB_COL)],
        out_specs=[_row(tb, D), _row(tb, D), _row(tb, 2 * D)],
        out_shape=[jax.ShapeDtypeStruct((t, D), bf16), jax.ShapeDtypeStruct((t, D), bf16),
                   jax.ShapeDtypeStruct((t, 2 * D), bf16)],
        compiler_params=_params(1),
    )(dmix, ya, yb, pg, pg)


def _ffn_fwd(up, wf):
    t = up.shape[0]
    tb = _pick(t, (ROW_BLOCK, 128))

    def body(up_ref, uph_ref, wf_ref, act_ref):
        first = pl.program_id(0) == 0
        for i in range(DFF // 128):
            g, v = _strip(i), _strip(i, DFF)
            gate = _strip_conv(wf_ref, g, _strip_taps(_f32(up_ref, g), _halo_before(uph_ref, g), first, 3))
            val = _strip_conv(wf_ref, v, _strip_taps(_f32(up_ref, v), _halo_before(uph_ref, v), first, 3))
            act_ref[:, g] = (gate * _sigmoid(gate) * val).astype(bf16)

    return pl.pallas_call(
        body, name="ffn_fwd", grid=(t // tb,),
        in_specs=[_row(tb, 2 * DFF), _prev(tb, 2 * DFF, rows=16), _fixed((8, 2 * DFF))],
        out_specs=_row(tb, DFF), out_shape=jax.ShapeDtypeStruct((t, DFF), bf16), compiler_params=_params(1),
    )(up, up, wf)


def _ffn_bwd1(dact, up, wf):
    t = up.shape[0]
    tb = _pick(t, (ROW_BLOCK, 128))

    def body(da_ref, up_ref, uph_ref, wf_ref, dc_ref, dw_ref):
        @pl.when(pl.program_id(0) == 0)
        def _():
            dw_ref[...] = jnp.zeros_like(dw_ref)

        first = pl.program_id(0) == 0
        for i in range(DFF // 128):
            g, v = _strip(i), _strip(i, DFF)
            g_taps = _strip_taps(_f32(up_ref, g), _halo_before(uph_ref, g), first, 3)
            v_taps = _strip_taps(_f32(up_ref, v), _halo_before(uph_ref, v), first, 3)
            gate = _strip_conv(wf_ref, g, g_taps)
            val = _strip_conv(wf_ref, v, v_taps)
            sg = _sigmoid(gate)
            da = _f32(da_ref, g)
            dgate = da * val * (sg * (1.0 + gate * (1.0 - sg)))
            dval = da * (gate * sg)
            dc_ref[:, g] = dgate.astype(bf16)
            dc_ref[:, v] = dval.astype(bf16)
            _strip_weight_grad(dw_ref, g, dgate, g_taps)
            _strip_weight_grad(dw_ref, v, dval, v_taps)

    return pl.pallas_call(
        body, name="ffn_bwd1", grid=(t // tb,),
        in_specs=[_row(tb, DFF), _row(tb, 2 * DFF), _prev(tb, 2 * DFF, rows=16), _fixed((8, 2 * DFF))],
        out_specs=[_row(tb, 2 * DFF), _fixed((8, 2 * DFF))],
        out_shape=[jax.ShapeDtypeStruct((t, 2 * DFF), bf16), jax.ShapeDtypeStruct((8, 2 * DFF), f32)],
        compiler_params=_params(1),
    )(dact, up, up, wf)


def _ffn_bwd2(dc, wf):
    t = dc.shape[0]
    tb = _pick(t, (ROW_BLOCK, 128))
    nb = t // tb

    def body(dc_ref, dch_ref, wf_ref, dup_ref):
        last = pl.program_id(0) == nb - 1
        for i in range(2 * DFF // 128):
            sl = _strip(i)
            dup_ref[:, sl] = _strip_conv_up(_f32(dc_ref, sl), _halo_after(dch_ref, sl), last, wf_ref, sl, 3).astype(bf16)

    return pl.pallas_call(
        body, name="ffn_bwd2", grid=(nb,),
        in_specs=[_row(tb, 2 * DFF), _next(tb, 2 * DFF, t, rows=16), _fixed((8, 2 * DFF))],
        out_specs=_row(tb, 2 * DFF), out_shape=jax.ShapeDtypeStruct((t, 2 * DFF), bf16), compiler_params=_params(1),
    )(dc, dc, wf)


def _final(x3, tgt, g):
    t = x3.shape[0]
    tb = _pick(t, (ROW_BLOCK, 256, 128))

    def body(x_ref, t_ref, g_ref, loss_ref, dx_ref, dxb_ref, dg_ref):
        @pl.when(pl.program_id(0) == 0)
        def _():
            loss_ref[...] = jnp.zeros_like(loss_ref)
            dg_ref[...] = jnp.zeros_like(dg_ref)

        xv = x_ref[...]
        r = lax.rsqrt(jnp.mean(xv * xv, axis=-1, keepdims=True) + EPS)
        xh = xv * r
        gv = g_ref[...]
        e = xh * gv - t_ref[...]
        lrow = 0.5 * jnp.mean(e * e, axis=-1, keepdims=True)
        loss_ref[...] += jnp.sum(jnp.broadcast_to(lrow, (tb, 128)).reshape(tb // 8, 8, 128), axis=0)
        dy = e * (1.0 / D)
        dyg = dy * gv
        dx = r * (dyg - xh * jnp.mean(dyg * xh, axis=-1, keepdims=True))
        dx_ref[...] = dx
        dxb_ref[...] = dx.astype(bf16)
        dg_ref[...] += jnp.sum((dy * xh).reshape(tb // 8, 8, D), axis=0)

    return pl.pallas_call(
        body, name="final", grid=(t // tb,), in_specs=[_row(tb, D), _row(tb, D), _fixed((1, D))],
        out_specs=[_fixed((8, 128)), _row(tb, D), _row(tb, D), _fixed((8, D))],
        out_shape=[jax.ShapeDtypeStruct((8, 128), f32), jax.ShapeDtypeStruct((t, D), f32),
                   jax.ShapeDtypeStruct((t, D), bf16), jax.ShapeDtypeStruct((8, D), f32)],
        compiler_params=_params(1),
    )(x3, tgt, g)


def _pre_bwd1(pg, pq, p2, dya_in, dqn, dkn, dvc, dgb, gbeta, h1, wa, wg, alog, dtb):
    t = pg.shape[0]
    tb = _pick(t, (ROW_BLOCK // 2, 128))

    def body(p0_ref, p0h_ref, pq_ref, pqh_ref, p2_ref, dya_ref, dqn_ref, dkn_ref, dvc_ref, dgb_ref, gb_ref, h1_ref,
             wa_ref, wg_ref, alog_ref, dtb_ref,
             dbg_ref, dca_ref, dc4_ref, dp2_ref, dwa_ref, dwg_ref, dal_ref, ddt_ref, dw2_ref):
        @pl.when(pl.program_id(0) == 0)
        def _():
            dwa_ref[...] = jnp.zeros_like(dwa_ref)
            dwg_ref[...] = jnp.zeros_like(dwg_ref)
            dal_ref[...] = jnp.zeros_like(dal_ref)
            ddt_ref[...] = jnp.zeros_like(ddt_ref)
            dw2_ref[...] = jnp.zeros_like(dw2_ref)

        first = pl.program_id(0) == 0

        for i in range(D // 128):
            sl, cg, xv = _strip(i), _strip(i, D), _strip(i, 2 * D)
            taps = _strip_taps(_f32(p0_ref, cg) * _f32(p0_ref, xv), _halo_before(p0h_ref, cg) * _halo_before(p0h_ref, xv),
                               first, 3)
            dya = _f32(dya_ref, sl)
            dbg_ref[:, sl] = (dya * _strip_conv(wa_ref, sl, taps)).astype(bf16)
            dca = dya * _f32(p0_ref, sl)
            dca_ref[:, sl] = dca.astype(bf16)
            _strip_weight_grad(dwa_ref, sl, dca, taps)

        for part, d_ref, scale in ((0, dqn_ref, DH ** -0.5), (1, dkn_ref, 1.0), (2, dvc_ref, None)):
            for h in range(H):
                sl = _strip(h, part * D)
                taps = _strip_taps(pq_ref[:, sl], pqh_ref[:, sl], first, 4)
                c4 = _strip_conv(wg_ref, sl, taps)
                sg = _sigmoid(c4)
                dn = d_ref[:, _strip(h)]
                if scale is not None:
                    a = c4 * sg
                    r = lax.rsqrt(jnp.sum(a * a, axis=-1, keepdims=True) + EPS)
                    an = a * r
                    dn = dn * scale
                    dn = r * (dn - an * jnp.sum(dn * an, axis=-1, keepdims=True))
                dc4 = dn * (sg * (1.0 + c4 * (1.0 - sg)))
                dc4_ref[:, sl] = dc4.astype(bf16)
                _strip_weight_grad(dwg_ref, sl, dc4, taps)

        ab = p2_ref[...]
        lane = lax.broadcasted_iota(jnp.int32, ab.shape, 1)
        dgbv = dgb_ref[...]
        gbv = gb_ref[...]
        da = dgbv * (-jnp.exp(alog_ref[...])) * _sigmoid(ab + dtb_ref[...])
        db = dgbv * gbv * (1.0 - gbv)
        dp2 = jnp.where(lane < H, da, jnp.where(lane < 2 * H, db, 0.0)).astype(bf16)
        dp2_ref[...] = dp2
        dw2_ref[...] += lax.dot_general(dp2, h1_ref[...], (((0,), (0,)), ((), ())), preferred_element_type=f32)
        dal = jnp.where(lane < H, dgbv * gbv, 0.0)
        ddt = jnp.where(lane < H, da, 0.0)
        dal_ref[...] += jnp.sum(dal.reshape(tb // 8, 8, 128), axis=0)
        ddt_ref[...] += jnp.sum(ddt.reshape(tb // 8, 8, 128), axis=0)

    return pl.pallas_call(
        body, name="pre_bwd1", grid=(t // tb,),
        in_specs=[_row(tb, 3 * D, 0), _prev(tb, 3 * D, 0, rows=16), _row(tb, 3 * D), _prev(tb, 3 * D), _row(tb, 128),
                  _row(tb, D), _row(tb, D), _row(tb, D), _row(tb, D), _row(tb, 128), _row(tb, 128), _row(tb, D),
                  _fixed((8, D)), _fixed((8, 3 * D)), _fixed((1, 128)), _fixed((1, 128))],
        out_specs=[_row(tb, D), _row(tb, D), _row(tb, 3 * D), _row(tb, 128),
                   _fixed((8, D)), _fixed((8, 3 * D)), _fixed((8, 128)), _fixed((8, 128)), _fixed((128, D))],
        out_shape=[jax.ShapeDtypeStruct((t, D), bf16), jax.ShapeDtypeStruct((t, D), bf16),
                   jax.ShapeDtypeStruct((t, 3 * D), bf16), jax.ShapeDtypeStruct((t, 128), bf16),
                   jax.ShapeDtypeStruct((8, D), f32), jax.ShapeDtypeStruct((8, 3 * D), f32),
                   jax.ShapeDtypeStruct((8, 128), f32), jax.ShapeDtypeStruct((8, 128), f32),
                   jax.ShapeDtypeStruct((128, D), f32)],
        compiler_params=_params(1),
    )(pg, pg, pq, pq, p2, dya_in, dqn, dkn, dvc, dgb, gbeta, h1, wa, wg, alog, dtb)


def _pre_bwd2(dca, dc4, pg, dbg, dz, dgates, wa, wg, exchange=None):
    t = pg.shape[0]
    tb = _pick(t, (ROW_BLOCK, 128))
    nb = t // tb

    def body(dca_ref, dcah_ref, dc4_ref, dc4h_ref, p0_ref, dbg_ref, dz_ref, dgt_ref, wa_ref, wg_ref, dp_ref):
        last = pl.program_id(0) == nb - 1
        dp_ref[:, :D] = dbg_ref[...]
        for i in range(D // 128):
            sl, cg, xv = _strip(i), _strip(i, D), _strip(i, 2 * D)
            du = _strip_conv_up(_f32(dca_ref, sl), _halo_after(dcah_ref, sl), last, wa_ref, sl, 3)
            dp_ref[:, cg] = (du * _f32(p0_ref, xv)).astype(bf16)
            dp_ref[:, xv] = (du * _f32(p0_ref, cg)).astype(bf16)
        dp_ref[:, 3 * D:4 * D] = dz_ref[...]
        dp_ref[:, 4 * D:6 * D] = dgt_ref[...]
        for i in range(3 * D // 128):
            sl = _strip(i)
            dq = _strip_conv_up(_f32(dc4_ref, sl), _halo_after(dc4h_ref, sl), last, wg_ref, sl, 4)
            dp_ref[:, _strip(i, 6 * D)] = dq.astype(bf16)

    return _call_with_exchange(
        body, exchange, name="pre_bwd2", grid=(nb,),
        in_specs=[_row(tb, D), _next(tb, D, t, rows=16), _row(tb, 3 * D), _next(tb, 3 * D, t, rows=16), _row(tb, 3 * D, 0),
                  _row(tb, D), _row(tb, D), _row(tb, 2 * D), _fixed((8, D)), _fixed((8, 3 * D))],
        out_specs=_row(tb, NW1), out_shape=jax.ShapeDtypeStruct((t, NW1), bf16),
        args=[dca, dca, dc4, dc4, pg, dbg, dz, dgates, wa, wg])


def _chunk_consts():
    r = lax.broadcasted_iota(jnp.int32, (CH, CH), 0)
    c = lax.broadcasted_iota(jnp.int32, (CH, CH), 1)
    return r, c, (r == c).astype(f32)


def _tri_inverse(lows, eye, r, c):
    def same_block(b):
        return jnp.bitwise_xor(r, c) < b

    xs = [jnp.where(same_block(8), -low, 0.0) for low in lows]
    ts = [eye + x for x in xs]
    for _ in range(2):
        xs = [_idot(x, x) for x in xs]
        ts = [t + _idot(t, x) for t, x in zip(ts, xs)]
    for b in (8, 16, 32):
        below = same_block(2 * b) & jnp.logical_not(same_block(b))
        ts = [t - _idot(_idot(t, jnp.where(below, low, 0.0)), t) for t, low in zip(ts, lows)]
    return ts


def _chunk_common(q, k, v, gcol, bcol, r, c, eye):
    grow = jnp.sum(eye * gcol, axis=0, keepdims=True)
    dec = jnp.exp(jnp.where(r >= c, gcol - grow, -jnp.inf))
    rcol = lax.broadcasted_iota(jnp.int32, (CH, 1), 0)
    glast = jnp.sum(jnp.where(rcol == CH - 1, gcol, 0.0), axis=0, keepdims=True)
    eg = jnp.exp(gcol)
    el = jnp.exp(glast - gcol)
    kb = k * bcol
    vb = v * bcol
    kk = _bdot_nt(kb, k)
    low = jnp.where(r > c, kk * dec, 0.0)
    qk = _bdot_nt(q, k)
    att = qk * dec
    return grow, dec, glast, eg, el, kb, vb, kk, low, qk, att, rcol


def _gdn_fwd(qn, kn, vc, gbeta):
    t = qn.shape[0]
    n_chunks = t // CH

    def body(q_ref, k_ref, v_ref, gb_ref, o_ref, s_ref, t_ref, state):
        @pl.when(pl.program_id(0) == 0)
        def _():
            state[...] = jnp.zeros_like(state)

        r, c, eye = _chunk_consts()
        tri = (r >= c).astype(f32)
        heads = range(H)
        keys = [(s, h) for s in range(GDN_STEP) for h in heads]
        rows = [slice(s * CH, (s + 1) * CH) for s in range(GDN_STEP)]
        gbs = [gb_ref[rows[s], :] for s in range(GDN_STEP)]
        galls = [_hdot(tri, gb) for gb in gbs]
        qs = {(s, h): q_ref[rows[s], h * DH:(h + 1) * DH] for s, h in keys}
        ks = {(s, h): k_ref[rows[s], h * DH:(h + 1) * DH] for s, h in keys}
        cm = {(s, h): _chunk_common(qs[s, h], ks[s, h], v_ref[rows[s], h * DH:(h + 1) * DH], galls[s][:, h:h + 1],
                                    gbs[s][:, H + h:H + h + 1], r, c, eye) for s, h in keys}
        invs = dict(zip(keys, _tri_inverse([cm[key][8] for key in keys], eye, r, c)))
        uws = {key: _bdot(invs[key], jnp.concatenate([cm[key][6], cm[key][5] * cm[key][3]], axis=1)) for key in keys}
        sts = [state[h] for h in heads]
        for s in range(GDN_STEP):
            vns = [uws[s, h][:, :DH] - _bdot(uws[s, h][:, DH:], sts[h]) for h in heads]
            outs = [_bdot(qs[s, h] * cm[s, h][3], sts[h]) + _bdot(cm[s, h][10], vns[h]) for h in heads]
            news = [sts[h] * jnp.exp(cm[s, h][2]) + _bdot_tn(ks[s, h] * cm[s, h][4], vns[h]) for h in heads]
            for h in heads:
                s_ref[s, h] = sts[h].astype(bf16)
                t_ref[s, h] = invs[s, h]
                o_ref[rows[s], h * DH:(h + 1) * DH] = outs[h]
            sts = news
        for h in heads:
            state[h] = sts[h]

    tb = GDN_STEP * CH
    return pl.pallas_call(
        body, name="gdn_fwd", grid=(t // tb,),
        in_specs=[_row(tb, D), _row(tb, D), _row(tb, D), _row(tb, 128)],
        out_specs=[_row(tb, D), pl.BlockSpec((GDN_STEP, H, DH, DH), lambda i: (i, 0, 0, 0)),
                   pl.BlockSpec((GDN_STEP, H, CH, CH), lambda i: (i, 0, 0, 0))],
        out_shape=[jax.ShapeDtypeStruct((t, D), f32), jax.ShapeDtypeStruct((n_chunks, H, DH, DH), bf16),
                   jax.ShapeDtypeStruct((n_chunks, H, CH, CH), f32)],
        scratch_shapes=[pltpu.VMEM((H, DH, DH), f32)],
        compiler_params=_params(1),
    )(qn, kn, vc, gbeta)


def _gdn_bwd(qn, kn, vc, gbeta, do, s_all, t_all):
    t = qn.shape[0]

    def body(q_ref, k_ref, v_ref, gb_ref, do_ref, s_ref, t_ref, dq_ref, dk_ref, dv_ref, dgb_ref, dstate):
        @pl.when(pl.program_id(0) == 0)
        def _():
            dstate[...] = jnp.zeros_like(dstate)

        r, c, eye = _chunk_consts()
        tril = r >= c
        lane = lax.broadcasted_iota(jnp.int32, (1, 128), 1)
        hs = range(H)

        def each(fn, *lists):
            return [fn(*args) for args in zip(*lists)]

        def rsum(a):
            return jnp.sum(a, axis=1, keepdims=True)

        def before_state(s):
            rows = slice(s * CH, (s + 1) * CH)
            gb = gb_ref[rows, :]
            gall = _hdot(tril.astype(f32), gb)
            p = {"rows": rows}
            p["q"] = q = [q_ref[rows, h * DH:(h + 1) * DH] for h in hs]
            p["k"] = k = [k_ref[rows, h * DH:(h + 1) * DH] for h in hs]
            p["v"] = v = [v_ref[rows, h * DH:(h + 1) * DH] for h in hs]
            p["dout"] = dout = [do_ref[rows, h * DH:(h + 1) * DH] for h in hs]
            p["inv"] = inv = [t_ref[s, h] for h in hs]
            p["st"] = st = [s_ref[s, h] for h in hs]
            p["bcol"] = bcol = [gb[:, H + h:H + h + 1] for h in hs]
            cm = [_chunk_common(q[h], k[h], v[h], gall[:, h:h + 1], bcol[h], r, c, eye) for h in hs]
            for name, i in (("dec", 1), ("glast", 2), ("eg", 3), ("el", 4), ("kb", 5), ("vb", 6), ("low", 8), ("att", 10)):
                p[name] = [m[i] for m in cm]
            p["rcol"] = cm[0][11]
            p["elast"] = each(jnp.exp, p["glast"])
            p["kbg"] = each(jnp.multiply, p["kb"], p["eg"])
            uw = each(lambda i, a, b: _bdot(i, jnp.concatenate([a, b], axis=1)), inv, p["vb"], p["kbg"])
            p["u"] = [a[:, :DH] for a in uw]
            p["w"] = [a[:, DH:] for a in uw]
            p["vn"] = each(lambda a, b, x: a - _bdot(b, x), p["u"], p["w"], st)
            p["qd"] = each(jnp.multiply, q, p["eg"])
            p["kd"] = each(jnp.multiply, k, p["el"])
            p["dqd"] = each(_bdot_nt, dout, st)
            p["datt"] = each(lambda d, x: jnp.where(tril, _bdot_nt(d, x), 0.0), dout, p["vn"])
            p["dqk"] = each(jnp.multiply, p["datt"], p["dec"])
            p["qd_do"] = each(_bdot_tn, p["qd"], dout)
            p["att_do"] = each(_bdot_tn, p["att"], dout)
            return p

        def after_state(p, ds):
            q, k, v, st, inv, bcol = p["q"], p["k"], p["v"], p["st"], p["inv"], p["bcol"]
            eg, el, kb, u, w = p["eg"], p["el"], p["kb"], p["u"], p["w"]
            dvn = each(lambda a, kk, x: a + _bdot(kk, x), p["att_do"], p["kd"], ds)
            dkd = each(_bdot_nt, p["vn"], ds)
            dw = each(lambda a, x: -_bdot_nt(a, x), dvn, st)
            new_ds = each(lambda x, e, a, ww, dv_: x * e + a - _bdot_tn(ww, dv_), ds, p["elast"], p["qd_do"], w, dvn)
            dglast = each(lambda e, x, d: e * jnp.sum(rsum(x.astype(f32) * d), axis=0, keepdims=True), p["elast"], st, ds)
            dr = each(lambda i, a, b: _bdot_tn(i, jnp.concatenate([a, b], axis=1)), inv, dvn, dw)
            dvb = [a[:, :DH] for a in dr]
            dkbg = [a[:, DH:] for a in dr]
            dlow = each(lambda a, b, x, y: -jnp.where(r > c, _bdot_nt(a, b) + _bdot_nt(x, y), 0.0), dvb, u, dkbg, w)
            dkk = each(jnp.multiply, dlow, p["dec"])
            mm = each(lambda a, b, x, y: a * b + x * y, dlow, p["low"], p["datt"], p["att"])
            dkb = each(lambda a, kk, b, e: _bdot(a, kk) + b * e, dkk, k, dkbg, eg)
            dk = each(lambda a, b, x, y, d, e, f, g: _bdot_tn(a, b) + _bdot_tn(x, y) + d * e + f * g,
                      dkk, kb, p["dqk"], q, dkd, el, dkb, bcol)
            dq = each(lambda a, kk, d, e: _bdot(a, kk) + d * e, p["dqk"], k, p["dqd"], eg)
            dv = each(jnp.multiply, dvb, bcol)
            dbeta = each(lambda a, b, x, y: rsum(a * b) + rsum(x * y), dkb, k, dvb, v)
            deg = each(lambda a, b, x, y: rsum(a * b) + rsum(x * y), dkbg, kb, p["dqd"], q)
            delc = each(lambda a, b, e: rsum(a * b) * e, dkd, k, el)
            dgc = each(lambda m, a, e, d: rsum(m) - rsum(eye * jnp.sum(m, axis=0, keepdims=True)) + a * e - d,
                       mm, deg, eg, delc)
            dgc = each(lambda g, d, l: g + jnp.where(p["rcol"] == CH - 1, jnp.sum(d, axis=0, keepdims=True) + l, 0.0),
                       dgc, delc, dglast)
            dg_acc = jnp.zeros((CH, 128), f32)
            db_acc = jnp.zeros((CH, 128), f32)
            rows = p["rows"]
            for h in hs:
                dq_ref[rows, h * DH:(h + 1) * DH] = dq[h]
                dk_ref[rows, h * DH:(h + 1) * DH] = dk[h]
                dv_ref[rows, h * DH:(h + 1) * DH] = dv[h]
                dg_acc = dg_acc + dgc[h] * (lane == h).astype(f32)
                db_acc = db_acc + dbeta[h] * (lane == H + h).astype(f32)
            dgb_ref[rows, :] = _hdot((r <= c).astype(f32), dg_acc) + db_acc
            return new_ds

        order = list(reversed(range(GDN_STEP)))
        pre = [before_state(s) for s in order]
        ds = [dstate[h] for h in hs]
        for p in pre:
            ds = after_state(p, ds)
        for h in hs:
            dstate[h] = ds[h]

    tb = GDN_STEP * CH
    n_steps = t // tb
    rev = lambda i: (n_steps - 1 - i, 0)
    rev4 = lambda i: (n_steps - 1 - i, 0, 0, 0)
    return pl.pallas_call(
        body, name="gdn_bwd", grid=(n_steps,),
        in_specs=[pl.BlockSpec((tb, D), rev), pl.BlockSpec((tb, D), rev), pl.BlockSpec((tb, D), rev),
                  pl.BlockSpec((tb, 128), rev), pl.BlockSpec((tb, D), rev),
                  pl.BlockSpec((GDN_STEP, H, DH, DH), rev4), pl.BlockSpec((GDN_STEP, H, CH, CH), rev4)],
        out_specs=[pl.BlockSpec((tb, D), rev), pl.BlockSpec((tb, D), rev), pl.BlockSpec((tb, D), rev),
                   pl.BlockSpec((tb, 128), rev)],
        out_shape=[jax.ShapeDtypeStruct((t, D), f32)] * 3 + [jax.ShapeDtypeStruct((t, 128), f32)],
        scratch_shapes=[pltpu.VMEM((H, DH, DH), f32)],
        compiler_params=_params(1),
    )(qn, kn, vc, gbeta, do, s_all, t_all)


def _pad_rows(w, rows=8):
    return jnp.pad(w, ((0, rows - w.shape[0]), (0, 0)))


_REST = ("w_up", "w_a_out", "w_b_out", "w_o", "w_down")


def _local_step(x, tgt, w, comm=None):
    g1 = w["norm_mix_g"].reshape(1, D)
    if comm is None:
        h1 = _rms_fwd(x, g1, name="rms1_fwd")
    else:
        h1, gathered = _rms_fwd(x, g1, name="rms1_fwd", exchange=comm.gather_first())
        w = {**w, **comm.finish_first(gathered)}
    w1, w2 = w["w1"], w["w2"]
    wa = _pad_rows(w["conv_a_w"])
    wg = _pad_rows(w["gdn_conv_w"])
    wf = _pad_rows(w["ffn_conv_w"])
    alog = jnp.pad(w["gdn_A_log"].reshape(1, H), ((0, 0), (0, 128 - H)))
    dtb = jnp.pad(w["gdn_dt_bias"].reshape(1, H), ((0, 0), (0, 128 - H)))
    g2 = w["norm_ffn_g"].reshape(1, D)
    g3 = w["norm_final_g"].reshape(1, D)
    gn = w["gdn_norm_g"].reshape(1, DH)

    if comm is None:
        pg = _matmul(h1, w1, name="mm_in", cols=(0, 6 * D), out_dtype=bf16)
        pq = _matmul(h1, w1, name="mm_in_qkv", cols=(6 * D, 3 * D))
    else:
        pg, gathered = _matmul(h1, w1, name="mm_in", cols=(0, 6 * D), out_dtype=bf16, exchange=comm.gather_rest())
        pq, gathered = _matmul(h1, w1, name="mm_in_qkv", cols=(6 * D, 3 * D), exchange=_gather_forward_exchange(gathered))
        w = {**w, **comm.finish_gather(gathered)}
    ya_in, qn, kn, vc, gbeta, p2 = _pre_fwd(pg, pq, h1, w2, wa, wg, alog, dtb)
    o, s_all, t_all = _gdn_fwd(qn, kn, vc, gbeta)
    yb_in = _post_fwd(o, pg, gn)
    ya = _matmul(ya_in, w["w_a_out"], name="mm_a", out_dtype=bf16)
    yb = _matmul(yb_in, w["w_b_out"], name="mm_b", out_dtype=bf16)
    mix = _mix_fwd(ya, yb, pg)
    x2 = _matmul(mix, w["w_o"], name="mm_o", add=x)
    h2 = _rms_fwd(x2, g2, name="rms2_fwd")
    up = _matmul(h2, w["w_up"], nt=True, name="mm_up", tn=DFF // 2, out_dtype=bf16)
    act = _ffn_fwd(up, wf)
    x3 = _matmul(act, w["w_down"], name="mm_down", add=x2, tm=512)
    loss_p, dx3, dx3b, dg3 = _final(x3, tgt, g3)

    grads = {"norm_final_g": dg3}
    dact = _matmul(dx3b, w["w_down"], nt=True, name="mm_down_dx", tm=512, tn=DFF, out_dtype=bf16)
    grads["w_down"] = _matmul_tn(act, dx3b, name="mm_down_dw", tm=DFF // 2)
    dc, dwf = _ffn_bwd1(dact, up, wf)
    grads["ffn_conv_w"] = dwf
    dup = _ffn_bwd2(dc, wf)
    dh2 = _matmul(dup, w["w_up"], name="mm_up_dx", tk=DFF)
    grads["w_up"] = _matmul_tn(dup, h2, name="mm_up_dw", tm=DFF // 2)
    dx2, dx2b, dg2 = _rms_bwd(dh2, x2, g2, dx3, name="rms2_bwd")
    grads["norm_ffn_g"] = dg2
    dmix = _matmul(dx2b, w["w_o"], nt=True, name="mm_o_dx", out_dtype=bf16)
    grads["w_o"] = _matmul_tn(mix, dx2b, name="mm_o_dw")
    dya, dyb, dgates = _mix_bwd(dmix, ya, yb, pg)
    dya_in = _matmul(dya, w["w_a_out"], nt=True, name="mm_a_dx", out_dtype=bf16)
    grads["w_a_out"] = _matmul_tn(ya_in, dya, name="mm_a_dw")
    dyb_in = _matmul(dyb, w["w_b_out"], nt=True, name="mm_b_dx")
    grads["w_b_out"] = _matmul_tn(yb_in, dyb, name="mm_b_dw")
    do, dz, dgn = _post_bwd(dyb_in, o, pg, gn)
    grads["gdn_norm_g"] = dgn
    dqn, dkn, dvc, dgb = _gdn_bwd(qn, kn, vc, gbeta, do, s_all, t_all)
    dbg, dca, dc4, dp2, dwa, dwg, dal, ddt, grads["w2"] = _pre_bwd1(pg, pq, p2, dya_in, dqn, dkn, dvc, dgb, gbeta, h1,
                                                                    wa, wg, alog, dtb)
    grads["conv_a_w"] = dwa
    grads["gdn_conv_w"] = dwg
    grads["gdn_A_log"] = dal
    grads["gdn_dt_bias"] = ddt
    if comm is None:
        dp1 = _pre_bwd2(dca, dc4, pg, dbg, dz, dgates, wa, wg)
        grads["w1"] = _matmul_tn(dp1, h1, name="mm_in_dw", tt=4096)
        dh1 = _matmul(dp1, w1, nt=True, name="mm_in_dx", tm=512, tk=NW1 // 2)
    else:
        exchange, blocks = comm.reduce_halves(_REST, grads)
        dp1, recv = _pre_bwd2(dca, dc4, pg, dbg, dz, dgates, wa, wg, exchange=exchange)
        exchange, sums = comm.reduce_sums(_REST, blocks, recv)
        grads["w1"], recv = _matmul_tn(dp1, h1, name="mm_in_dw", tt=4096, exchange=exchange)
        comm.finish_reduce(_REST, sums, recv)
        exchange, blocks = comm.reduce_halves(("w_in",), grads)
        exchange, sums = comm.reduce_sums(("w_in",), blocks, _run_exchange(exchange, name="rs_sibling_w_in"))
        dh1, recv = _matmul(dp1, w1, nt=True, name="mm_in_dx", tm=512, tk=NW1 // 2, exchange=exchange)
        comm.finish_reduce(("w_in",), sums, recv)
    dx, dg1 = _rms_bwd(dh1, x, g1, dx2, name="rms1_bwd", more=(dp2, w2), bf16_copy=False)
    grads["norm_mix_g"] = dg1
    return loss_p, dx, grads


_ANY = pl.BlockSpec(memory_space=pl.ANY)


def _remote(src, dst, send_sem, recv_sem, to):
    return pltpu.make_async_remote_copy(src_ref=src, dst_ref=dst, send_sem=send_sem, recv_sem=recv_sem,
                                        device_id=to, device_id_type=MESH)


def _run_exchange(exchange, *, name):
    arrays, shapes, sems, start, wait = exchange
    n_in, n_out = len(arrays), len(shapes)

    def body(*refs):
        start(refs[:n_in], refs[n_in:n_in + n_out], refs[n_in + n_out:])
        wait(refs[:n_in], refs[n_in:n_in + n_out], refs[n_in + n_out:])

    return pl.pallas_call(body, name=name, out_shape=list(shapes), in_specs=[_ANY] * n_in, out_specs=[_ANY] * n_out,
                          scratch_shapes=list(sems))(*arrays)


def _gather_exchange(shards):
    n = len(shards)

    def copies(x_refs, out_refs, sems):
        send_sems, recv_sems, local_sems = sems
        x, y, c = lax.axis_index("x"), lax.axis_index("y"), lax.axis_index("c")

        def flip(v, b):
            return v + b - 2 * v * b

        me, sibling = (x, y, c), (x, y, 1 - c)
        chip1, chip2, diag = (flip(x, 1 - c), flip(y, c)), (flip(x, c), flip(y, 1 - c)), (1 - x, 1 - y)

        def copy(a, k, blk, to, from_input=False):
            dst = out_refs[a].at[4 * blk[0] + 2 * blk[1] + blk[2]]
            return _remote(x_refs[a] if from_input else dst, dst, send_sems.at[a, k], recv_sems.at[a, k], to)

        mine = [pltpu.make_async_copy(x_refs[a], out_refs[a].at[4 * x + 2 * y + c], local_sems.at[a]) for a in range(n)]
        first = []
        for a in range(n):
            first += [copy(a, 0, me, sibling, from_input=True), copy(a, 1, me, (*chip1, c), from_input=True),
                      copy(a, 2, me, (*chip2, c), from_input=True)]
        return copy, mine, first, me, sibling, chip1, chip2, diag, c

    def start(x_refs, out_refs, sems):
        _, mine, first, *_ = copies(x_refs, out_refs, sems)
        for cp in mine + first:
            cp.start()

    def wait(x_refs, out_refs, sems):
        copy, mine, first, me, sibling, chip1, chip2, diag, c = copies(x_refs, out_refs, sems)
        passed = []

        def pass_on(cp):
            passed.append(cp)
            cp.start()

        for a in range(n):
            copy(a, 1, (*chip1, c), me).wait_recv()
            pass_on(copy(a, 3, (*chip1, c), (*chip2, c)))
            pass_on(copy(a, 4, (*chip1, c), sibling))
        for a in range(n):
            copy(a, 2, (*chip2, c), me).wait_recv()
            pass_on(copy(a, 5, (*chip2, c), sibling))
        for a in range(n):
            copy(a, 3, (*diag, c), me).wait_recv()
            pass_on(copy(a, 6, (*diag, c), sibling))
        for a in range(n):
            copy(a, 0, sibling, me).wait_recv()
            copy(a, 4, (*chip2, 1 - c), me).wait_recv()
            copy(a, 5, (*chip1, 1 - c), me).wait_recv()
            copy(a, 6, (*diag, 1 - c), me).wait_recv()
        for cp in first + passed:
            cp.wait_send()
        for cp in mine:
            cp.wait()

    shapes = [jax.ShapeDtypeStruct((N_DEV, *s.shape), s.dtype) for s in shards]
    sems = [pltpu.SemaphoreType.DMA((n, 7)), pltpu.SemaphoreType.DMA((n, 7)), pltpu.SemaphoreType.DMA((n,))]
    return shards, shapes, sems, start, wait


def _gather_direct_exchange(shards):
    n = len(shards)

    def copies(x_refs, out_refs, sems):
        send_sems, recv_sems, local_sems = sems
        x, y, c = lax.axis_index("x"), lax.axis_index("y"), lax.axis_index("c")
        targets = [(x, y, 1 - c), (1 - x, y, c), (x, 1 - y, c), (1 - x, 1 - y, c)]
        local, sends, recvs = [], [], []
        for a in range(n):
            mine = out_refs[a].at[4 * x + 2 * y + c]
            local.append(pltpu.make_async_copy(x_refs[a], mine, local_sems.at[a]))
            for k, to in enumerate(targets):
                theirs = out_refs[a].at[4 * to[0] + 2 * to[1] + to[2]]
                sends.append(_remote(x_refs[a], mine, send_sems.at[a, k], recv_sems.at[a, k], to))
                recvs.append(_remote(theirs, theirs, send_sems.at[a, k], recv_sems.at[a, k], to))
        return local, sends, recvs

    def start(x_refs, out_refs, sems):
        local, sends, _ = copies(x_refs, out_refs, sems)
        for cp in local + sends:
            cp.start()

    def wait(x_refs, out_refs, sems):
        local, sends, recvs = copies(x_refs, out_refs, sems)
        for cp in recvs:
            cp.wait_recv()
        for cp in sends:
            cp.wait_send()
        for cp in local:
            cp.wait()

    shapes = [jax.ShapeDtypeStruct((N_DEV, *s.shape), s.dtype) for s in shards]
    sems = [pltpu.SemaphoreType.DMA((n, 4)), pltpu.SemaphoreType.DMA((n, 4)), pltpu.SemaphoreType.DMA((n,))]
    return shards, shapes, sems, start, wait


def _gather_forward_exchange(gathered):
    n = len(gathered)

    def copies(_, out_refs, sems):
        send_sems, recv_sems = sems
        x, y, c = lax.axis_index("x"), lax.axis_index("y"), lax.axis_index("c")
        sibling = (x, y, 1 - c)
        sends, recvs = [], []
        for a in range(n):
            for j, (px, py) in enumerate([(1 - x, y), (x, 1 - y), (1 - x, 1 - y)]):
                mine = out_refs[a].at[4 * px + 2 * py + c]
                theirs = out_refs[a].at[4 * px + 2 * py + 1 - c]
                sends.append(_remote(mine, mine, send_sems.at[a, j], recv_sems.at[a, j], sibling))
                recvs.append(_remote(theirs, theirs, send_sems.at[a, j], recv_sems.at[a, j], sibling))
        return sends, recvs

    def start(in_refs, out_refs, sems):
        for cp in copies(in_refs, out_refs, sems)[0]:
            cp.start()

    def wait(in_refs, out_refs, sems):
        sends, recvs = copies(in_refs, out_refs, sems)
        for cp in recvs:
            cp.wait_recv()
        for cp in sends:
            cp.wait_send()

    shapes = [jax.ShapeDtypeStruct(g.shape, g.dtype) for g in gathered]
    sems = [pltpu.SemaphoreType.DMA((n, 3)), pltpu.SemaphoreType.DMA((n, 3))]
    return gathered, shapes, sems, start, wait, True


def _chips_exchange(hsums):
    n = len(hsums)

    def copies(h_refs, out_refs, sems):
        send_sems, recv_sems = sems
        x, y, c = lax.axis_index("x"), lax.axis_index("y"), lax.axis_index("c")
        chips = [(1 - x, y), (x, 1 - y), (1 - x, 1 - y)]
        return [_remote(h_refs[a].at[2 * px + py], out_refs[a].at[k], send_sems.at[a, k], recv_sems.at[a, k], (px, py, c))
                for a in range(n) for k, (px, py) in enumerate(chips)]

    def start(h_refs, out_refs, sems):
        for cp in copies(h_refs, out_refs, sems):
            cp.start()

    def wait(h_refs, out_refs, sems):
        for cp in copies(h_refs, out_refs, sems):
            cp.wait()

    shapes = [jax.ShapeDtypeStruct((3, *h.shape[1:]), h.dtype) for h in hsums]
    sems = [pltpu.SemaphoreType.DMA((n, 3)), pltpu.SemaphoreType.DMA((n, 3))]
    return hsums, shapes, sems, start, wait


def _sibling_exchange(halves):
    n = len(halves)

    def copies(p_refs, out_refs, sems):
        send_sems, recv_sems = sems
        x, y, c = lax.axis_index("x"), lax.axis_index("y"), lax.axis_index("c")
        return [_remote(p_refs[a], out_refs[a], send_sems.at[a], recv_sems.at[a], (x, y, 1 - c)) for a in range(n)]

    def start(p_refs, out_refs, sems):
        for cp in copies(p_refs, out_refs, sems):
            cp.start()

    def wait(p_refs, out_refs, sems):
        for cp in copies(p_refs, out_refs, sems):
            cp.wait()

    shapes = [jax.ShapeDtypeStruct(h.shape, h.dtype) for h in halves]
    return halves, shapes, [pltpu.SemaphoreType.DMA((n,)), pltpu.SemaphoreType.DMA((n,))], start, wait


_IN_RANGES = ((0, 3 * D, 0, 0), (3 * D, 6 * D, 0, 6 * D), (6 * D, 7 * D, 0, 3 * D), (7 * D, 7 * D + 16, 1, 0),
              (7 * D + 16, 9 * D + 16, 0, 4 * D))


def _col_pieces(width, ranges):
    pieces = []
    for d in range(N_DEV):
        lo, hi = d * width, (d + 1) * width
        for glo, ghi, mat, mlo in ranges:
            a, b = max(lo, glo), min(hi, ghi)
            if a < b:
                pieces.append((d, a - lo, b - lo, mat, mlo + a - glo))
    return pieces


def _cols_to_matrices(g, ranges, out_widths, *, name):
    _, rows, width = g.shape
    tb = 128
    pieces = _col_pieces(width, ranges)
    covered = [sum(p[2] - p[1] for p in pieces if p[3] == m) for m in range(len(out_widths))]

    def body(g_ref, *o_refs):
        for m, o_ref in enumerate(o_refs):
            if covered[m] < out_widths[m]:
                o_ref[...] = jnp.zeros_like(o_ref)
        for d, b0, b1, m, m0 in pieces:
            o_refs[m][:, m0:m0 + b1 - b0] = g_ref[d, :, b0:b1]

    return pl.pallas_call(
        body, name=name, grid=(rows // tb,), in_specs=[pl.BlockSpec((N_DEV, tb, width), lambda i: (0, i, 0))],
        out_specs=[pl.BlockSpec((tb, wo), lambda i: (i, 0)) for wo in out_widths],
        out_shape=[jax.ShapeDtypeStruct((rows, wo), g.dtype) for wo in out_widths], compiler_params=_params(1),
    )(g)


def _transposed_matrices_to_blocks(mats, ranges, width, *, name):
    rows = mats[0].shape[1]
    pieces = _col_pieces(width, ranges)

    def body(*refs):
        m_refs, g_ref = refs[:-1], refs[-1]
        for d, b0, b1, m, m0 in pieces:
            g_ref[d, b0:b1, :] = m_refs[m][m0:m0 + b1 - b0, :]

    return pl.pallas_call(
        body, name=name, grid=(rows // 128,),
        in_specs=[pl.BlockSpec((mt.shape[0], 128), lambda i: (0, i)) for mt in mats],
        out_specs=pl.BlockSpec((N_DEV, width, 128), lambda i: (0, 0, i)),
        out_shape=jax.ShapeDtypeStruct((N_DEV, width, rows), mats[0].dtype), compiler_params=_params(1),
    )(*mats)


def _row_block(rows):
    return 128 if rows % 128 == 0 else rows


def _half_bf16(g4, c_other, *, name):
    _, _, rows, width = g4.shape
    tb = _row_block(rows)

    def body(c_ref, p_ref, o_ref):
        o_ref[0] = p_ref[0, 0].astype(bf16)

    grid_spec = pltpu.PrefetchScalarGridSpec(
        num_scalar_prefetch=1, grid=(4, rows // tb),
        in_specs=[pl.BlockSpec((1, 1, tb, width), lambda j, i, c_ref: (j, c_ref[0], i, 0))],
        out_specs=pl.BlockSpec((1, tb, width), lambda j, i, c_ref: (j, i, 0)))
    return pl.pallas_call(
        body, name=name, grid_spec=grid_spec, out_shape=jax.ShapeDtypeStruct((4, rows, width), bf16),
        compiler_params=_params(2, _vmem_for(4 * tb * width, 2 * tb * width)),
    )(c_other, g4)


def _pair_sum(g4, recv, c_me, *, name):
    _, _, rows, width = g4.shape
    tb = _row_block(rows)

    def body(c_ref, p_ref, r_ref, o_ref, ob_ref):
        s = p_ref[0, 0] + r_ref[0].astype(f32)
        o_ref[0] = s
        ob_ref[0] = s.astype(bf16)

    blk = pl.BlockSpec((1, tb, width), lambda j, i, c_ref: (j, i, 0))
    grid_spec = pltpu.PrefetchScalarGridSpec(
        num_scalar_prefetch=1, grid=(4, rows // tb),
        in_specs=[pl.BlockSpec((1, 1, tb, width), lambda j, i, c_ref: (j, c_ref[0], i, 0)), blk],
        out_specs=[blk, blk])
    return pl.pallas_call(
        body, name=name, grid_spec=grid_spec,
        out_shape=[jax.ShapeDtypeStruct((4, rows, width), f32), jax.ShapeDtypeStruct((4, rows, width), bf16)],
        compiler_params=_params(2, _vmem_for(4 * tb * width, 2 * tb * width, 4 * tb * width, 2 * tb * width)),
    )(c_me, g4, recv)


def _adam_shard(hsum, recv, chip, w, m, v, *, name):
    _, rows, width = w.shape
    tb = _row_block(rows)

    def body(j_ref, h_ref, r_ref, w_ref, m_ref, v_ref, g_out, d_out, m_out, v_out):
        g = ((h_ref[0] + r_ref[0].astype(f32)) + r_ref[1].astype(f32)) + r_ref[2].astype(f32)
        delta, mn, vn = _adam_math(w_ref[0], g, m_ref[0], v_ref[0])
        g_out[0] = g
        d_out[0] = delta
        m_out[0] = mn
        v_out[0] = vn

    blk = pl.BlockSpec((1, tb, width), lambda i, j_ref: (0, i, 0))
    grid_spec = pltpu.PrefetchScalarGridSpec(
        num_scalar_prefetch=1, grid=(rows // tb,),
        in_specs=[pl.BlockSpec((1, tb, width), lambda i, j_ref: (j_ref[0], i, 0)),
                  pl.BlockSpec((3, tb, width), lambda i, j_ref: (0, i, 0)), blk, blk, blk],
        out_specs=[blk, blk, blk, blk])
    return pl.pallas_call(
        body, name=name, grid_spec=grid_spec, out_shape=[jax.ShapeDtypeStruct(w.shape, f32)] * 4,
        compiler_params=_params(1, _vmem_for(*[4 * tb * width] * 8, 6 * tb * width)),
    )(chip, hsum, recv, w, m, v)


def _sum_shard(hsum, recv, chip, *, name):
    _, rows, width = hsum.shape
    tb = _row_block(rows)

    def body(j_ref, h_ref, r_ref, g_out):
        g_out[...] = ((h_ref[0] + r_ref[0].astype(f32)) + r_ref[1].astype(f32)) + r_ref[2].astype(f32)

    grid_spec = pltpu.PrefetchScalarGridSpec(
        num_scalar_prefetch=1, grid=(rows // tb,),
        in_specs=[pl.BlockSpec((1, tb, width), lambda i, j_ref: (j_ref[0], i, 0)),
                  pl.BlockSpec((3, tb, width), lambda i, j_ref: (0, i, 0))],
        out_specs=pl.BlockSpec((tb, width), lambda i, j_ref: (i, 0)))
    return pl.pallas_call(body, name=name, grid_spec=grid_spec, out_shape=jax.ShapeDtypeStruct((rows, width), f32),
                          compiler_params=_params(1, _vmem_for(*[4 * tb * width] * 2, 6 * tb * width)))(chip, hsum, recv)


def _adam_columns(g, w, m, v, *, name):
    cols, _, rows = w.shape
    tb = cols // 2

    def body(g_ref, w_ref, m_ref, v_ref, d_out, m_out, v_out):
        delta, mn, vn = _adam_math(w_ref[...], g_ref[...], m_ref[...], v_ref[...])
        d_out[...] = delta
        m_out[...] = mn
        v_out[...] = vn

    blk = pl.BlockSpec((tb, 1, rows), lambda i: (i, 0, 0))
    return pl.pallas_call(
        body, name=name, grid=(cols // tb,), in_specs=[blk] * 4, out_specs=[blk] * 3,
        out_shape=[jax.ShapeDtypeStruct(w.shape, f32)] * 3,
        compiler_params=_params(1, _vmem_for(*[4 * tb * rows] * 7)),
    )(g, w, m, v)


R_SMALL = 8 + 8 * N_DEV
_SMALL_LANES = {"gdn_norm_g": (0, DH), "gdn_A_log": (DH, DH + H), "gdn_dt_bias": (2 * DH, 2 * DH + H)}
_LOSS_LANE = 3 * DH


def _pack_small(dg1, dg2, dg3, dgn, dal, ddt, loss_p, dwa, dwg, dwf):
    def body(dg1_ref, dg2_ref, dg3_ref, dgn_ref, dal_ref, ddt_ref, loss_ref, dwa_ref, dwg_ref, dwf_ref, o_ref):
        def total(ref):
            return jnp.sum(ref[...], axis=0, keepdims=True)

        o_ref[...] = jnp.zeros_like(o_ref)
        o_ref[0:1, :] = total(dg1_ref)
        o_ref[1:2, :] = total(dg2_ref)
        o_ref[2:3, :] = total(dg3_ref)
        o_ref[3:4, 0:DH] = total(dgn_ref)
        o_ref[3:4, DH:2 * DH] = total(dal_ref)
        o_ref[3:4, 2 * DH:3 * DH] = total(ddt_ref)
        o_ref[3:4, 3 * DH:4 * DH] = total(loss_ref)
        for d in range(N_DEV):
            base = 8 + 8 * d
            o_ref[base:base + 3, 0:128] = dwa_ref[0:3, 128 * d:128 * (d + 1)]
            o_ref[base:base + 4, 128:512] = dwg_ref[0:4, 384 * d:384 * (d + 1)]
            o_ref[base + 4:base + 7, 0:704] = dwf_ref[0:3, 704 * d:704 * (d + 1)]

    return pl.pallas_call(body, name="pack_small", out_shape=jax.ShapeDtypeStruct((R_SMALL, D), f32))(
        dg1, dg2, dg3, dgn, dal, ddt, loss_p, dwa, dwg, dwf)


_SMALL = ("norm_mix_g", "norm_ffn_g", "norm_final_g", "gdn_norm_g", "gdn_A_log", "gdn_dt_bias",
          "conv_a_w", "gdn_conv_w", "ffn_conv_w")


def _adam_small(gath, me, w, m, v):
    arrays = [t[n] for n in _SMALL for t in (w, m, v)]

    def body(me_ref, ga_ref, gb_ref, *refs):
        ins, outs = refs[:len(arrays)], refs[len(arrays):]
        ga, gb = ga_ref[0], gb_ref[0]
        for s in range(1, N_DEV):
            ga = ga + ga_ref[s]
            gb = gb + gb_ref[s]
        grads = {"norm_mix_g": ga[0:1, :], "norm_ffn_g": ga[1:2, :], "norm_final_g": ga[2:3, :],
                 "conv_a_w": gb[0:3, 0:128], "gdn_conv_w": gb[0:4, 128:512], "ffn_conv_w": gb[4:7, 0:704]}
        for n, (lo, hi) in _SMALL_LANES.items():
            grads[n] = ga[3:4, lo:hi]
        for i, n in enumerate(_SMALL):
            three_d = len(w[n].shape) == 3
            wv, mv, vv = (r[0] if three_d else r[...] for r in ins[3 * i:3 * i + 3])
            delta, mn, vn = _adam_math(wv, grads[n], mv, vv)
            for o_ref, val in zip(outs[4 * i:4 * i + 4], (grads[n], delta, mn, vn)):
                if three_d:
                    o_ref[0] = val
                else:
                    o_ref[...] = val
        outs[-1][...] = ga[3:4, _LOSS_LANE:_LOSS_LANE + 1]

    def whole(shape):
        return pl.BlockSpec(shape, lambda i, me_ref: (0,) * len(shape))

    grid_spec = pltpu.PrefetchScalarGridSpec(
        num_scalar_prefetch=1, grid=(1,),
        in_specs=[pl.BlockSpec((N_DEV, 8, D), lambda i, me_ref: (0, 0, 0)),
                  pl.BlockSpec((N_DEV, 8, D), lambda i, me_ref: (0, 1 + me_ref[0], 0))] + [whole(a.shape) for a in arrays],
        out_specs=[whole(w[n].shape) for n in _SMALL for _ in range(4)] + [whole((1, 1))])
    res = pl.pallas_call(
        body, name="adam_small", grid_spec=grid_spec,
        out_shape=[jax.ShapeDtypeStruct(w[n].shape, f32) for n in _SMALL for _ in range(4)]
        + [jax.ShapeDtypeStruct((1, 1), f32)],
        compiler_params=_params(1),
    )(me, gath, gath, *arrays)
    return {n: tuple(res[4 * i:4 * i + 4]) for i, n in enumerate(_SMALL)}, res[-1]


def _adam_math(w, g, m, v):
    m = ADAM_B1 * m + (1.0 - ADAM_B1) * g
    v = ADAM_B2 * v + (1.0 - ADAM_B2) * jnp.square(g)
    m_hat = m / (1.0 - ADAM_B1 ** ADAM_STEP)
    v_hat = v / (1.0 - ADAM_B2 ** ADAM_STEP)
    delta = -ADAM_LR * (m_hat / (jnp.sqrt(v_hat) + ADAM_EPS) + ADAM_WD * w)
    return delta, m, v


_WEIGHTS = ("norm_mix_g", "w_in", "conv_a_w", "gdn_conv_w", "gdn_A_log", "gdn_dt_bias", "gdn_norm_g", "w_a_out",
            "w_b_out", "w_o", "norm_ffn_g", "w_up", "ffn_conv_w", "w_down", "norm_final_g")
_CONVS = ("conv_a_w", "gdn_conv_w", "ffn_conv_w")


class _StepExchanges:
    def __init__(self, wts, mom, var, c_me, chip):
        self.wts, self.mom, self.var, self.c_me, self.chip = wts, mom, var, c_me, chip
        self.results = {}

    def gather_first(self):
        return _gather_exchange([self.wts["w_in"][0].astype(bf16)] + [self.wts[n][0] for n in _CONVS])

    def finish_first(self, gathered):
        g_in, gc_a, gc_g, gc_f = gathered
        w1, w2 = _cols_to_matrices(g_in, _IN_RANGES, (NW1, 128), name="relay_w_in")
        return {"w1": w1, "w2": w2, "conv_a_w": gc_a.transpose(1, 0, 2).reshape(3, D),
                "gdn_conv_w": gc_g.transpose(1, 0, 2).reshape(4, 3 * D),
                "ffn_conv_w": gc_f.transpose(1, 0, 2).reshape(3, 2 * DFF)}

    def gather_rest(self):
        return _gather_direct_exchange([self.wts[n][0].astype(bf16) for n in _REST])

    def finish_gather(self, gathered):
        g_up, g_a, g_b, g_o, g_down = gathered
        return {"w_up": g_up.reshape(2 * DFF, D), "w_a_out": g_a.reshape(D, D), "w_b_out": g_b.reshape(D, D),
                "w_o": g_o.reshape(D, D), "w_down": g_down.reshape(DFF, D)}

    def reduce_halves(self, names, grads):
        blocks = []
        for n in names:
            if n == "w_in":
                g = _transposed_matrices_to_blocks([grads["w1"], grads["w2"]], _IN_RANGES, R_IN, name="relay_dw_in")
                blocks.append(g.reshape(4, 2, R_IN, D))
            else:
                blocks.append(grads[n].reshape(4, 2, *self.wts[n].shape[1:]))
        return _sibling_exchange([_half_bf16(g, 1 - self.c_me, name="rs_half_" + n) for n, g in zip(names, blocks)]), blocks

    def reduce_sums(self, names, blocks, recv):
        sums = [_pair_sum(g, r, self.c_me, name="rs_sum_" + n) for n, g, r in zip(names, blocks, recv)]
        return _chips_exchange([s[1] for s in sums]), [s[0] for s in sums]

    def finish_reduce(self, names, sums, recv):
        for n, s, r in zip(names, sums, recv):
            if n == "w_in":
                g = _sum_shard(s, r, self.chip, name="rs_total_w_in")[:, None, :]
                w, m, v = (jnp.transpose(t[n], (2, 0, 1)) for t in (self.wts, self.mom, self.var))
                res = (g, *_adam_columns(g, w, m, v, name="adam_w_in"))
                self.results[n] = tuple(jnp.transpose(a, (1, 2, 0)) for a in res)
            else:
                self.results[n] = _adam_shard(s, r, self.chip, self.wts[n], self.mom[n], self.var[n], name="adam_" + n)


def kernel(x, norm_mix_g, w_in, conv_a_w, gdn_conv_w, gdn_A_log, gdn_dt_bias, gdn_norm_g, w_a_out, w_b_out, w_o, norm_ffn_g, w_up, ffn_conv_w, w_down, norm_final_g, loss_target, m_norm_mix_g, m_w_in, m_conv_a_w, m_gdn_conv_w, m_gdn_A_log, m_gdn_dt_bias, m_gdn_norm_g, m_w_a_out, m_w_b_out, m_w_o, m_norm_ffn_g, m_w_up, m_ffn_conv_w, m_w_down, m_norm_final_g, v_norm_mix_g, v_w_in, v_conv_a_w, v_gdn_conv_w, v_gdn_A_log, v_gdn_dt_bias, v_gdn_norm_g, v_w_a_out, v_w_b_out, v_w_o, v_norm_ffn_g, v_w_up, v_ffn_conv_w, v_w_down, v_norm_final_g):
    wts = dict(zip(_WEIGHTS, (norm_mix_g, w_in, conv_a_w, gdn_conv_w, gdn_A_log, gdn_dt_bias, gdn_norm_g, w_a_out,
                              w_b_out, w_o, norm_ffn_g, w_up, ffn_conv_w, w_down, norm_final_g)))
    mom = dict(zip(_WEIGHTS, (m_norm_mix_g, m_w_in, m_conv_a_w, m_gdn_conv_w, m_gdn_A_log, m_gdn_dt_bias,
                              m_gdn_norm_g, m_w_a_out, m_w_b_out, m_w_o, m_norm_ffn_g, m_w_up, m_ffn_conv_w,
                              m_w_down, m_norm_final_g)))
    var = dict(zip(_WEIGHTS, (v_norm_mix_g, v_w_in, v_conv_a_w, v_gdn_conv_w, v_gdn_A_log, v_gdn_dt_bias,
                              v_gdn_norm_g, v_w_a_out, v_w_b_out, v_w_o, v_norm_ffn_g, v_w_up, v_ffn_conv_w,
                              v_w_down, v_norm_final_g)))
    cx, cy, cc = lax.axis_index("x"), lax.axis_index("y"), lax.axis_index("c")
    c_me = jnp.reshape(cc, (1,)).astype(jnp.int32)
    chip = jnp.reshape(2 * cx + cy, (1,)).astype(jnp.int32)
    me = jnp.reshape(4 * cx + 2 * cy + cc, (1,)).astype(jnp.int32)

    def with_up_transposed(t):
        return {**t, "w_up": jnp.swapaxes(t["w_up"], 1, 2)}

    comm = _StepExchanges(with_up_transposed(wts), with_up_transposed(mom), with_up_transposed(var), c_me, chip)
    replicated = {n: wts[n] for n in ("norm_mix_g", "norm_ffn_g", "norm_final_g", "gdn_norm_g", "gdn_A_log", "gdn_dt_bias")}
    loss_p, dx, grads = _local_step(x[0], loss_target[0], replicated, comm)
    res = comm.results
    res["w_up"] = tuple(jnp.swapaxes(a, 1, 2) for a in res["w_up"])

    small = _pack_small(grads["norm_mix_g"], grads["norm_ffn_g"], grads["norm_final_g"], grads["gdn_norm_g"],
                        grads["gdn_A_log"], grads["gdn_dt_bias"], loss_p, grads["conv_a_w"], grads["gdn_conv_w"],
                        grads["ffn_conv_w"])
    (small_all,) = _run_exchange(_gather_exchange([small]), name="ag_small")

    def raw(t):
        return {n: t[n].reshape(1, D) if n == "norm_final_g" else t[n] for n in _SMALL}

    res_small, loss = _adam_small(small_all, me, raw(wts), raw(mom), raw(var))
    for n in _SMALL:
        res[n] = tuple(a.reshape(wts[n].shape) for a in res_small[n])
    outs = [[res[n][i] for n in _WEIGHTS] for i in range(4)]
    return (loss.reshape(()), dx[None], *outs[0], *outs[1], *outs[2], *outs[3])
```

```python
import jax
import jax.numpy as jnp
from jax import lax
from jax.experimental import pallas as pl
from jax.experimental.pallas import tpu as pltpu

f32 = jnp.float32
bf16 = jnp.bfloat16

D = 1024
H = 8
DH = 128
CH = 64
GDN_STEP = 2
ROW_BLOCK = 512
ELEMENTWISE_BLOCK = 1024
DFF = 2816
NW1 = 9216
EPS = 1e-6
N_DEV = 8

ADAM_LR = 0.001
ADAM_B1 = 0.9
ADAM_B2 = 0.999
ADAM_EPS = 1e-08
ADAM_WD = 0.01
ADAM_STEP = 10

VMEM_LIMIT_BYTES = 48 * 1024 * 1024
VMEM_MAX_BYTES = 56 * 1024 * 1024

R_IN, R_UP = 1154, 704

_HI = lax.Precision.HIGHEST
MESH = pl.DeviceIdType.MESH


def _params(n_grid, vmem_bytes=None):
    return pltpu.CompilerParams(dimension_semantics=("arbitrary",) * n_grid,
                                vmem_limit_bytes=VMEM_LIMIT_BYTES if vmem_bytes is None else vmem_bytes)


def _vmem_for(*block_bytes, extra=0):
    need = 2 * sum(block_bytes) + extra + 4 * 1024 * 1024
    return min(max(need, VMEM_LIMIT_BYTES), VMEM_MAX_BYTES)


def _bdot(a, b):
    return jnp.dot(a.astype(bf16), b.astype(bf16), preferred_element_type=f32)


def _bdot_nt(a, b):
    return lax.dot_general(a.astype(bf16), b.astype(bf16), (((1,), (1,)), ((), ())), preferred_element_type=f32)


def _bdot_tn(a, b):
    return lax.dot_general(a.astype(bf16), b.astype(bf16), (((0,), (0,)), ((), ())), preferred_element_type=f32)


def _hdot(a, b):
    return jnp.dot(a, b, preferred_element_type=f32, precision=_HI)


def _idot(a, b):
    return jnp.dot(a, b, preferred_element_type=f32, precision=lax.Precision.HIGH)


def _sigmoid(x):
    return 1.0 / (1.0 + jnp.exp(-x))


def _softplus(x):
    return jnp.maximum(x, 0.0) + jnp.log(1.0 + jnp.exp(-jnp.abs(x)))


def _shift_down(x, halo, j):
    if j == 0:
        return x
    xr = pltpu.roll(x, j, 0)
    hr = pltpu.roll(halo, j, 0)
    r8 = lax.broadcasted_iota(jnp.int32, hr.shape, 0)
    top = jnp.where(r8 < j, hr, xr[:8])
    return jnp.concatenate([top, xr[8:]], axis=0)


def _shift_up(x, halo, j):
    if j == 0:
        return x
    n = x.shape[0]
    xr = pltpu.roll(x, n - j, 0)
    hr = pltpu.roll(halo, 8 - j, 0)
    r8 = lax.broadcasted_iota(jnp.int32, hr.shape, 0)
    bot = jnp.where(r8 >= 8 - j, hr, xr[n - 8:])
    return jnp.concatenate([xr[:n - 8], bot], axis=0)


def _taps_down(x, halo, k):
    return [_shift_down(x, halo, k - 1 - j) for j in range(k)]


def _strip(i, base=0):
    return slice(base + i * 128, base + (i + 1) * 128)


def _strip_taps(x, halo, first, k):
    return _taps_down(x, jnp.where(first, 0.0, halo), k)


def _strip_conv(w_ref, sl, taps):
    out = w_ref[0:1, sl] * taps[0]
    for j in range(1, len(taps)):
        out = out + w_ref[j:j + 1, sl] * taps[j]
    return out


def _strip_weight_grad(dw_ref, sl, dy, taps):
    for j, tap in enumerate(taps):
        dw_ref[j:j + 1, sl] += jnp.sum(dy * tap, axis=0, keepdims=True)


def _strip_conv_up(dy, halo, last, w_ref, sl, k):
    halo = jnp.where(last, 0.0, halo)
    out = w_ref[k - 1:k, sl] * dy
    for j in range(k - 1):
        out = out + w_ref[j:j + 1, sl] * _shift_up(dy, halo, k - 1 - j)
    return out


def _row(tb, w, col=0):
    return pl.BlockSpec((tb, w), lambda i: (i, col))


def _prev(tb, w, col=0, rows=8):
    return pl.BlockSpec((rows, w), lambda i: (jnp.maximum(i * (tb // rows) - 1, 0), col))


def _next(tb, w, n_rows, col=0, rows=8):
    last = n_rows // rows - 1
    return pl.BlockSpec((rows, w), lambda i: (jnp.minimum((i + 1) * (tb // rows), last), col))


def _f32(ref, sl):
    return ref[:, sl].astype(f32)


def _halo_before(ref, sl):
    h = _f32(ref, sl)
    return h[h.shape[0] - 8:]


def _halo_after(ref, sl):
    return _f32(ref, sl)[:8]


def _fixed(shape):
    return pl.BlockSpec(shape, lambda i: (0,) * len(shape))


def _pick(n, prefs):
    for p in prefs:
        if n % p == 0:
            return p
    return n


def _matmul(a, b, *, name, nt=False, add=None, tm=1024, tn=1024, tk=None, out_dtype=f32, cols=None, exchange=None):
    m, kd = a.shape
    col0, n = cols if cols is not None else (0, b.shape[0] if nt else b.shape[1])
    tm = _pick(m, (tm, 512, 256))
    tn = _pick(n, (tn, 1024, 512, 128))
    tk = kd if tk is None else tk
    nk = kd // tk
    assert nk == 1 or out_dtype == f32
    assert col0 % tn == 0 and not (nt and cols)
    j0 = col0 // tn
    dims = (((1,), (1,)), ((), ())) if nt else (((1,), (0,)), ((), ()))

    def body(a_ref, b_ref, *rest):
        o_ref = rest[-1]
        part = lax.dot_general(a_ref[...], b_ref[...], dims, preferred_element_type=f32)
        if nk == 1:
            o_ref[...] = (part if add is None else part + rest[0][...]).astype(out_dtype)
            return
        k = pl.program_id(2)

        @pl.when(k == 0)
        def _():
            o_ref[...] = part if add is None else part + rest[0][...]

        @pl.when(k > 0)
        def _():
            o_ref[...] += part

    b_spec = pl.BlockSpec((tn, tk), lambda i, j, k: (j, k)) if nt else pl.BlockSpec((tk, tn), lambda i, j, k: (k, j + j0))
    in_specs = [pl.BlockSpec((tm, tk), lambda i, j, k: (i, k)), b_spec]
    args = [a, b]
    if add is not None:
        in_specs.append(pl.BlockSpec((tm, tn), lambda i, j, k: (i, j)))
        args.append(add)
    vmem = _vmem_for(2 * tm * tk, 2 * tk * tn, tm * tn * jnp.dtype(out_dtype).itemsize,
                     4 * tm * tn if add is not None else 0, extra=4 * tm * tn)
    return _call_with_exchange(
        body, exchange, name=name, grid=(m // tm, n // tn, nk), in_specs=in_specs,
        out_specs=pl.BlockSpec((tm, tn), lambda i, j, k: (i, j)),
        out_shape=jax.ShapeDtypeStruct((m, n), out_dtype), args=args, vmem_bytes=vmem)


def _call_with_exchange(body, exchange, *, name, grid, in_specs, out_specs, out_shape, args, vmem_bytes=None):
    if exchange is None:
        return pl.pallas_call(body, name=name, grid=grid, in_specs=in_specs, out_specs=out_specs, out_shape=out_shape,
                              compiler_params=_params(len(grid), vmem_bytes))(*args)
    x_arrays, x_shapes, x_sems, start, wait = exchange[:5]
    n_in, n_xin, n_xout = len(args), len(x_arrays), len(x_shapes)
    aliases = {n_in + i: 1 + i for i in range(n_xin)} if len(exchange) > 5 and exchange[5] else {}

    def full_body(*refs):
        c_in, x_in = refs[:n_in], refs[n_in:n_in + n_xin]
        c_out = refs[n_in + n_xin]
        x_out = refs[n_in + n_xin + 1:n_in + n_xin + 1 + n_xout]
        sems = refs[n_in + n_xin + 1 + n_xout:]
        ids = [pl.program_id(d) for d in range(len(grid))]
        first, last = ids[0] == 0, ids[0] == grid[0] - 1
        for d in range(1, len(grid)):
            first = first & (ids[d] == 0)
            last = last & (ids[d] == grid[d] - 1)

        @pl.when(first)
        def _():
            start(x_in, x_out, sems)

        body(*c_in, c_out)

        @pl.when(last)
        def _():
            wait(x_in, x_out, sems)

    res = pl.pallas_call(
        full_body, name=name, grid=grid, in_specs=list(in_specs) + [_ANY] * n_xin,
        out_specs=[out_specs] + [_ANY] * n_xout, out_shape=[out_shape] + list(x_shapes),
        scratch_shapes=list(x_sems), input_output_aliases=aliases, compiler_params=_params(len(grid), vmem_bytes),
    )(*args, *x_arrays)
    return res[0], list(res[1:])


def _matmul_tn(a, b, *, name, tm=1024, tn=1024, tt=2048, exchange=None):
    t, m = a.shape
    _, n = b.shape
    tm = _pick(m, (tm, 1024, 512, 128))
    tn = _pick(n, (tn, 1024, 512, 128))
    tt = _pick(t, (tt, 2048, 1024, 512, 256))
    nt = t // tt

    def body(a_ref, b_ref, o_ref):
        k = pl.program_id(2)
        part = lax.dot_general(a_ref[...], b_ref[...], (((0,), (0,)), ((), ())), preferred_element_type=f32)

        @pl.when(k == 0)
        def _():
            o_ref[...] = part

        @pl.when(k > 0)
        def _():
            o_ref[...] += part

    return _call_with_exchange(
        body, exchange, name=name, grid=(m // tm, n // tn, nt),
        in_specs=[pl.BlockSpec((tt, tm), lambda i, j, k: (k, i)), pl.BlockSpec((tt, tn), lambda i, j, k: (k, j))],
        out_specs=pl.BlockSpec((tm, tn), lambda i, j, k: (i, j)),
        out_shape=jax.ShapeDtypeStruct((m, n), f32), args=[a, b],
        vmem_bytes=_vmem_for(2 * tt * tm, 2 * tt * tn, 4 * tm * tn, extra=4 * tm * tn + 2 * tt * tm))


def _rms_fwd(x, g, *, name, exchange=None):
    t = x.shape[0]
    tb = _pick(t, (ELEMENTWISE_BLOCK, 256, 128))

    def body(x_ref, g_ref, h_ref):
        xv = x_ref[...]
        r = lax.rsqrt(jnp.mean(xv * xv, axis=-1, keepdims=True) + EPS)
        h_ref[...] = (xv * r * g_ref[...]).astype(bf16)

    return _call_with_exchange(
        body, exchange, name=name, grid=(t // tb,), in_specs=[_row(tb, D), _fixed((1, D))], out_specs=_row(tb, D),
        out_shape=jax.ShapeDtypeStruct((t, D), bf16), args=[x, g])


def _rms_bwd(dh, x, g, dres, *, name, more=None, bf16_copy=True):
    t = x.shape[0]
    tb = _pick(t, (ELEMENTWISE_BLOCK, 256, 128))

    def body(dh_ref, x_ref, g_ref, dres_ref, *rest):
        dx_ref, dg_ref = rest[-3 if bf16_copy else -2], rest[-1]
        xv = x_ref[...]
        r = lax.rsqrt(jnp.mean(xv * xv, axis=-1, keepdims=True) + EPS)
        xh = xv * r
        dy = dh_ref[...]
        if more is not None:
            dy = dy + lax.dot_general(rest[0][...], rest[1][...], (((1,), (1,)), ((), ())), preferred_element_type=f32)
        dyg = dy * g_ref[...]
        dx = dres_ref[...] + r * (dyg - xh * jnp.mean(dyg * xh, axis=-1, keepdims=True))
        dx_ref[...] = dx
        if bf16_copy:
            rest[-2][...] = dx.astype(bf16)

        @pl.when(pl.program_id(0) == 0)
        def _():
            dg_ref[...] = jnp.zeros_like(dg_ref)

        dg_ref[...] += jnp.sum((dy * xh).reshape(tb // 8, 8, D), axis=0)

    in_specs, args = [_row(tb, D), _row(tb, D), _fixed((1, D)), _row(tb, D)], [dh, x, g, dres]
    if more is not None:
        in_specs += [_row(tb, 128), _fixed(more[1].shape)]
        args += list(more)
    dx_dtypes = (f32, bf16) if bf16_copy else (f32,)
    return pl.pallas_call(
        body, name=name, grid=(t // tb,), in_specs=in_specs,
        out_specs=[_row(tb, D) for _ in dx_dtypes] + [_fixed((8, D))],
        out_shape=[jax.ShapeDtypeStruct((t, D), dt) for dt in dx_dtypes] + [jax.ShapeDtypeStruct((8, D), f32)],
        compiler_params=_params(1),
    )(*args)


def _gdn_gates(ab, alog, dtb):
    lane = lax.broadcasted_iota(jnp.int32, ab.shape, 1)
    g = -jnp.exp(alog) * _softplus(ab + dtb)
    beta = _sigmoid(ab)
    return jnp.where(lane < H, g, jnp.where(lane < 2 * H, beta, 0.0))


def _pre_fwd(pg, pq, h1, w2, wa, wg, alog, dtb):
    t = pg.shape[0]
    tb = _pick(t, (ROW_BLOCK // 2, 128))

    def body(p0_ref, p0h_ref, pq_ref, pqh_ref, h1_ref, w2_ref, wa_ref, wg_ref, alog_ref, dtb_ref,
             ya_ref, qn_ref, kn_ref, vc_ref, gb_ref, p2_ref):
        first = pl.program_id(0) == 0
        p2_ref[...] = jnp.dot(h1_ref[...], w2_ref[...], preferred_element_type=f32)
        for i in range(D // 128):
            sl, cg, xv = _strip(i), _strip(i, D), _strip(i, 2 * D)
            taps = _strip_taps(_f32(p0_ref, cg) * _f32(p0_ref, xv), _halo_before(p0h_ref, cg) * _halo_before(p0h_ref, xv),
                               first, 3)
            ya_ref[:, sl] = (_f32(p0_ref, sl) * _strip_conv(wa_ref, sl, taps)).astype(bf16)
        for part, out_ref, scale in ((0, qn_ref, DH ** -0.5), (1, kn_ref, 1.0), (2, vc_ref, None)):
            for h in range(H):
                sl = _strip(h, part * D)
                s = _strip_conv(wg_ref, sl, _strip_taps(pq_ref[:, sl], pqh_ref[:, sl], first, 4))
                s = s * _sigmoid(s)
                if scale is not None:
                    s = s * (lax.rsqrt(jnp.sum(s * s, axis=-1, keepdims=True) + EPS) * scale)
                out_ref[:, _strip(h)] = s
        gb_ref[...] = _gdn_gates(p2_ref[...], alog_ref[...], dtb_ref[...])

    return pl.pallas_call(
        body, name="pre_fwd", grid=(t // tb,),
        in_specs=[_row(tb, 3 * D, 0), _prev(tb, 3 * D, 0, rows=16), _row(tb, 3 * D), _prev(tb, 3 * D), _row(tb, D),
                  _fixed((D, 128)), _fixed((8, D)), _fixed((8, 3 * D)), _fixed((1, 128)), _fixed((1, 128))],
        out_specs=[_row(tb, D), _row(tb, D), _row(tb, D), _row(tb, D), _row(tb, 128), _row(tb, 128)],
        out_shape=[jax.ShapeDtypeStruct((t, D), bf16), jax.ShapeDtypeStruct((t, D), f32),
                   jax.ShapeDtypeStruct((t, D), f32), jax.ShapeDtypeStruct((t, D), f32),
                   jax.ShapeDtypeStruct((t, 128), f32), jax.ShapeDtypeStruct((t, 128), f32)],
        compiler_params=_params(1),
    )(pg, pg, pq, pq, h1, w2, wa, wg, alog, dtb)


_Z_COL, _GA_COL, _GB_COL = 3, 4, 5


def _post_fwd(o, pg, gn):
    t = o.shape[0]
    tb = _pick(t, (ELEMENTWISE_BLOCK, 256, 128))

    def body(o_ref, z_ref, gn_ref, yb_ref):
        for h in range(H):
            sl = slice(h * DH, (h + 1) * DH)
            oh = o_ref[:, sl]
            z = _f32(z_ref, sl)
            r = lax.rsqrt(jnp.mean(oh * oh, axis=-1, keepdims=True) + EPS)
            yb_ref[:, sl] = (oh * r * gn_ref[...] * (z * _sigmoid(z))).astype(bf16)

    return pl.pallas_call(
        body, name="post_fwd", grid=(t // tb,), in_specs=[_row(tb, D), _row(tb, D, _Z_COL), _fixed((1, DH))],
        out_specs=_row(tb, D), out_shape=jax.ShapeDtypeStruct((t, D), bf16), compiler_params=_params(1),
    )(o, pg, gn)


def _post_bwd(dyb, o, pg, gn):
    t = o.shape[0]
    tb = _pick(t, (ELEMENTWISE_BLOCK, 256, 128))

    def body(dyb_ref, o_ref, z_ref, gn_ref, do_ref, dz_ref, dgn_ref):
        @pl.when(pl.program_id(0) == 0)
        def _():
            dgn_ref[...] = jnp.zeros_like(dgn_ref)

        gn_v = gn_ref[...]
        acc = jnp.zeros((8, DH), f32)
        for h in range(H):
            sl = slice(h * DH, (h + 1) * DH)
            oh = o_ref[:, sl]
            z = _f32(z_ref, sl)
            dy = dyb_ref[:, sl]
            r = lax.rsqrt(jnp.mean(oh * oh, axis=-1, keepdims=True) + EPS)
            on = oh * r
            sg = _sigmoid(z)
            sz = z * sg
            don = dy * sz
            dz_ref[:, sl] = (dy * on * gn_v * (sg * (1.0 + z * (1.0 - sg)))).astype(bf16)
            acc = acc + jnp.sum((don * on).reshape(tb // 8, 8, DH), axis=0)
            doh = don * gn_v
            do_ref[:, sl] = r * (doh - on * jnp.mean(doh * on, axis=-1, keepdims=True))
        dgn_ref[...] += acc

    return pl.pallas_call(
        body, name="post_bwd", grid=(t // tb,),
        in_specs=[_row(tb, D), _row(tb, D), _row(tb, D, _Z_COL), _fixed((1, DH))],
        out_specs=[_row(tb, D), _row(tb, D), _fixed((8, DH))],
        out_shape=[jax.ShapeDtypeStruct((t, D), f32), jax.ShapeDtypeStruct((t, D), bf16),
                   jax.ShapeDtypeStruct((8, DH), f32)],
        compiler_params=_params(1),
    )(dyb, o, pg, gn)


def _mix_fwd(ya, yb, pg):
    t = ya.shape[0]
    tb = _pick(t, (ELEMENTWISE_BLOCK, 256, 128))

    def body(ya_ref, yb_ref, ga_ref, gb_ref, mix_ref):
        ya_v, yb_v = ya_ref[...].astype(f32), yb_ref[...].astype(f32)
        mix = _sigmoid(ga_ref[...].astype(f32)) * ya_v + _sigmoid(gb_ref[...].astype(f32)) * yb_v
        mix_ref[...] = mix.astype(bf16)

    return pl.pallas_call(
        body, name="mix_fwd", grid=(t // tb,),
        in_specs=[_row(tb, D), _row(tb, D), _row(tb, D, _GA_COL), _row(tb, D, _GB_COL)],
        out_specs=_row(tb, D), out_shape=jax.ShapeDtypeStruct((t, D), bf16), compiler_params=_params(1),
    )(ya, yb, pg, pg)


def _mix_bwd(dmix, ya, yb, pg):
    t = ya.shape[0]
    tb = _pick(t, (ELEMENTWISE_BLOCK, 256, 128))

    def body(dm_ref, ya_ref, yb_ref, ga_ref, gb_ref, dya_ref, dyb_ref, dg_ref):
        dm = dm_ref[...].astype(f32)
        sa = _sigmoid(ga_ref[...].astype(f32))
        sb = _sigmoid(gb_ref[...].astype(f32))
        dya_ref[...] = (dm * sa).astype(bf16)
        dyb_ref[...] = (dm * sb).astype(bf16)
        dg_ref[:, :D] = (dm * ya_ref[...].astype(f32) * sa * (1.0 - sa)).astype(bf16)
        dg_ref[:, D:] = (dm * yb_ref[...].astype(f32) * sb * (1.0 - sb)).astype(bf16)

    return pl.pallas_call(
        body, name="mix_bwd", grid=(t // tb,),
        in_specs=[_row(tb, D), _row(tb, D), _row(tb, D), _row(tb, D, _GA_COL), _row(tb, D, _GB_COL)],
        out_specs=[_row(tb, D), _row(tb, D), _row(tb, 2 * D)],
        out_shape=[jax.ShapeDtypeStruct((t, D), bf16), jax.ShapeDtypeStruct((t, D), bf16),
                   jax.ShapeDtypeStruct((t, 2 * D), bf16)],
        compiler_params=_params(1),
    )(dmix, ya, yb, pg, pg)


def _ffn_fwd(up, wf):
    t = up.shape[0]
    tb = _pick(t, (ROW_BLOCK, 128))

    def body(up_ref, uph_ref, wf_ref, act_ref):
        first = pl.program_id(0) == 0
        for i in range(DFF // 128):
            g, v = _strip(i), _strip(i, DFF)
            gate = _strip_conv(wf_ref, g, _strip_taps(_f32(up_ref, g), _halo_before(uph_ref, g), first, 3))
            val = _strip_conv(wf_ref, v, _strip_taps(_f32(up_ref, v), _halo_before(uph_ref, v), first, 3))
            act_ref[:, g] = (gate * _sigmoid(gate) * val).astype(bf16)

    return pl.pallas_call(
        body, name="ffn_fwd", grid=(t // tb,),
        in_specs=[_row(tb, 2 * DFF), _prev(tb, 2 * DFF, rows=16), _fixed((8, 2 * DFF))],
        out_specs=_row(tb, DFF), out_shape=jax.ShapeDtypeStruct((t, DFF), bf16), compiler_params=_params(1),
    )(up, up, wf)


def _ffn_bwd1(dact, up, wf):
    t = up.shape[0]
    tb = _pick(t, (ROW_BLOCK, 128))

    def body(da_ref, up_ref, uph_ref, wf_ref, dc_ref, dw_ref):
        @pl.when(pl.program_id(0) == 0)
        def _():
            dw_ref[...] = jnp.zeros_like(dw_ref)

        first = pl.program_id(0) == 0
        for i in range(DFF // 128):
            g, v = _strip(i), _strip(i, DFF)
            g_taps = _strip_taps(_f32(up_ref, g), _halo_before(uph_ref, g), first, 3)
            v_taps = _strip_taps(_f32(up_ref, v), _halo_before(uph_ref, v), first, 3)
            gate = _strip_conv(wf_ref, g, g_taps)
            val = _strip_conv(wf_ref, v, v_taps)
            sg = _sigmoid(gate)
            da = _f32(da_ref, g)
            dgate = da * val * (sg * (1.0 + gate * (1.0 - sg)))
            dval = da * (gate * sg)
            dc_ref[:, g] = dgate.astype(bf16)
            dc_ref[:, v] = dval.astype(bf16)
            _strip_weight_grad(dw_ref, g, dgate, g_taps)
            _strip_weight_grad(dw_ref, v, dval, v_taps)

    return pl.pallas_call(
        body, name="ffn_bwd1", grid=(t // tb,),
        in_specs=[_row(tb, DFF), _row(tb, 2 * DFF), _prev(tb, 2 * DFF, rows=16), _fixed((8, 2 * DFF))],
        out_specs=[_row(tb, 2 * DFF), _fixed((8, 2 * DFF))],
        out_shape=[jax.ShapeDtypeStruct((t, 2 * DFF), bf16), jax.ShapeDtypeStruct((8, 2 * DFF), f32)],
        compiler_params=_params(1),
    )(dact, up, up, wf)


def _ffn_bwd2(dc, wf):
    t = dc.shape[0]
    tb = _pick(t, (ROW_BLOCK, 128))
    nb = t // tb

    def body(dc_ref, dch_ref, wf_ref, dup_ref):
        last = pl.program_id(0) == nb - 1
        for i in range(2 * DFF // 128):
            sl = _strip(i)
            dup_ref[:, sl] = _strip_conv_up(_f32(dc_ref, sl), _halo_after(dch_ref, sl), last, wf_ref, sl, 3).astype(bf16)

    return pl.pallas_call(
        body, name="ffn_bwd2", grid=(nb,),
        in_specs=[_row(tb, 2 * DFF), _next(tb, 2 * DFF, t, rows=16), _fixed((8, 2 * DFF))],
        out_specs=_row(tb, 2 * DFF), out_shape=jax.ShapeDtypeStruct((t, 2 * DFF), bf16), compiler_params=_params(1),
    )(dc, dc, wf)


def _final(x3, tgt, g):
    t = x3.shape[0]
    tb = _pick(t, (ELEMENTWISE_BLOCK, 256, 128))

    def body(x_ref, t_ref, g_ref, loss_ref, dx_ref, dxb_ref, dg_ref):
        @pl.when(pl.program_id(0) == 0)
        def _():
            loss_ref[...] = jnp.zeros_like(loss_ref)
            dg_ref[...] = jnp.zeros_like(dg_ref)

        xv = x_ref[...]
        r = lax.rsqrt(jnp.mean(xv * xv, axis=-1, keepdims=True) + EPS)
        xh = xv * r
        gv = g_ref[...]
        e = xh * gv - t_ref[...]
        lrow = 0.5 * jnp.mean(e * e, axis=-1, keepdims=True)
        loss_ref[...] += jnp.sum(jnp.broadcast_to(lrow, (tb, 128)).reshape(tb // 8, 8, 128), axis=0)
        dy = e * (1.0 / D)
        dyg = dy * gv
        dx = r * (dyg - xh * jnp.mean(dyg * xh, axis=-1, keepdims=True))
        dx_ref[...] = dx
        dxb_ref[...] = dx.astype(bf16)
        dg_ref[...] += jnp.sum((dy * xh).reshape(tb // 8, 8, D), axis=0)

    return pl.pallas_call(
        body, name="final", grid=(t // tb,), in_specs=[_row(tb, D), _row(tb, D), _fixed((1, D))],
        out_specs=[_fixed((8, 128)), _row(tb, D), _row(tb, D), _fixed((8, D))],
        out_shape=[jax.ShapeDtypeStruct((8, 128), f32), jax.ShapeDtypeStruct((t, D), f32),
                   jax.ShapeDtypeStruct((t, D), bf16), jax.ShapeDtypeStruct((8, D), f32)],
        compiler_params=_params(1),
    )(x3, tgt, g)


def _pre_bwd1(pg, pq, p2, dya_in, dqn, dkn, dvc, dgb, gbeta, h1, wa, wg, alog, dtb):
    t = pg.shape[0]
    tb = _pick(t, (ROW_BLOCK // 2, 128))

    def body(p0_ref, p0h_ref, pq_ref, pqh_ref, p2_ref, dya_ref, dqn_ref, dkn_ref, dvc_ref, dgb_ref, gb_ref, h1_ref,
             wa_ref, wg_ref, alog_ref, dtb_ref,
             dbg_ref, dca_ref, dc4_ref, dp2_ref, dwa_ref, dwg_ref, dal_ref, ddt_ref, dw2_ref):
        @pl.when(pl.program_id(0) == 0)
        def _():
            dwa_ref[...] = jnp.zeros_like(dwa_ref)
            dwg_ref[...] = jnp.zeros_like(dwg_ref)
            dal_ref[...] = jnp.zeros_like(dal_ref)
            ddt_ref[...] = jnp.zeros_like(ddt_ref)
            dw2_ref[...] = jnp.zeros_like(dw2_ref)

        first = pl.program_id(0) == 0

        for i in range(D // 128):
            sl, cg, xv = _strip(i), _strip(i, D), _strip(i, 2 * D)
            taps = _strip_taps(_f32(p0_ref, cg) * _f32(p0_ref, xv), _halo_before(p0h_ref, cg) * _halo_before(p0h_ref, xv),
                               first, 3)
            dya = _f32(dya_ref, sl)
            dbg_ref[:, sl] = (dya * _strip_conv(wa_ref, sl, taps)).astype(bf16)
            dca = dya * _f32(p0_ref, sl)
            dca_ref[:, sl] = dca.astype(bf16)
            _strip_weight_grad(dwa_ref, sl, dca, taps)

        for part, d_ref, scale in ((0, dqn_ref, DH ** -0.5), (1, dkn_ref, 1.0), (2, dvc_ref, None)):
            for h in range(H):
                sl = _strip(h, part * D)
                taps = _strip_taps(pq_ref[:, sl], pqh_ref[:, sl], first, 4)
                c4 = _strip_conv(wg_ref, sl, taps)
                sg = _sigmoid(c4)
                dn = d_ref[:, _strip(h)]
                if scale is not None:
                    a = c4 * sg
                    r = lax.rsqrt(jnp.sum(a * a, axis=-1, keepdims=True) + EPS)
                    an = a * r
                    dn = dn * scale
                    dn = r * (dn - an * jnp.sum(dn * an, axis=-1, keepdims=True))
                dc4 = dn * (sg * (1.0 + c4 * (1.0 - sg)))
                dc4_ref[:, sl] = dc4.astype(bf16)
                _strip_weight_grad(dwg_ref, sl, dc4, taps)

        ab = p2_ref[...]
        lane = lax.broadcasted_iota(jnp.int32, ab.shape, 1)
        dgbv = dgb_ref[...]
        gbv = gb_ref[...]
        da = dgbv * (-jnp.exp(alog_ref[...])) * _sigmoid(ab + dtb_ref[...])
        db = dgbv * gbv * (1.0 - gbv)
        dp2 = jnp.where(lane < H, da, jnp.where(lane < 2 * H, db, 0.0)).astype(bf16)
        dp2_ref[...] = dp2
        dw2_ref[...] += lax.dot_general(dp2, h1_ref[...], (((0,), (0,)), ((), ())), preferred_element_type=f32)
        dal = jnp.where(lane < H, dgbv * gbv, 0.0)
        ddt = jnp.where(lane < H, da, 0.0)
        dal_ref[...] += jnp.sum(dal.reshape(tb // 8, 8, 128), axis=0)
        ddt_ref[...] += jnp.sum(ddt.reshape(tb // 8, 8, 128), axis=0)

    return pl.pallas_call(
        body, name="pre_bwd1", grid=(t // tb,),
        in_specs=[_row(tb, 3 * D, 0), _prev(tb, 3 * D, 0, rows=16), _row(tb, 3 * D), _prev(tb, 3 * D), _row(tb, 128),
                  _row(tb, D), _row(tb, D), _row(tb, D), _row(tb, D), _row(tb, 128), _row(tb, 128), _row(tb, D),
                  _fixed((8, D)), _fixed((8, 3 * D)), _fixed((1, 128)), _fixed((1, 128))],
        out_specs=[_row(tb, D), _row(tb, D), _row(tb, 3 * D), _row(tb, 128),
                   _fixed((8, D)), _fixed((8, 3 * D)), _fixed((8, 128)), _fixed((8, 128)), _fixed((128, D))],
        out_shape=[jax.ShapeDtypeStruct((t, D), bf16), jax.ShapeDtypeStruct((t, D), bf16),
                   jax.ShapeDtypeStruct((t, 3 * D), bf16), jax.ShapeDtypeStruct((t, 128), bf16),
                   jax.ShapeDtypeStruct((8, D), f32), jax.ShapeDtypeStruct((8, 3 * D), f32),
                   jax.ShapeDtypeStruct((8, 128), f32), jax.ShapeDtypeStruct((8, 128), f32),
                   jax.ShapeDtypeStruct((128, D), f32)],
        compiler_params=_params(1),
    )(pg, pg, pq, pq, p2, dya_in, dqn, dkn, dvc, dgb, gbeta, h1, wa, wg, alog, dtb)


def _pre_bwd2(dca, dc4, pg, dbg, dz, dgates, wa, wg, exchange=None):
    t = pg.shape[0]
    tb = _pick(t, (ROW_BLOCK, 128))
    nb = t // tb

    def body(dca_ref, dcah_ref, dc4_ref, dc4h_ref, p0_ref, dbg_ref, dz_ref, dgt_ref, wa_ref, wg_ref, dp_ref):
        last = pl.program_id(0) == nb - 1
        dp_ref[:, :D] = dbg_ref[...]
        for i in range(D // 128):
            sl, cg, xv = _strip(i), _strip(i, D), _strip(i, 2 * D)
            du = _strip_conv_up(_f32(dca_ref, sl), _halo_after(dcah_ref, sl), last, wa_ref, sl, 3)
            dp_ref[:, cg] = (du * _f32(p0_ref, xv)).astype(bf16)
            dp_ref[:, xv] = (du * _f32(p0_ref, cg)).astype(bf16)
        dp_ref[:, 3 * D:4 * D] = dz_ref[...]
        dp_ref[:, 4 * D:6 * D] = dgt_ref[...]
        for i in range(3 * D // 128):
            sl = _strip(i)
            dq = _strip_conv_up(_f32(dc4_ref, sl), _halo_after(dc4h_ref, sl), last, wg_ref, sl, 4)
            dp_ref[:, _strip(i, 6 * D)] = dq.astype(bf16)

    return _call_with_exchange(
        body, exchange, name="pre_bwd2", grid=(nb,),
        in_specs=[_row(tb, D), _next(tb, D, t, rows=16), _row(tb, 3 * D), _next(tb, 3 * D, t, rows=16), _row(tb, 3 * D, 0),
                  _row(tb, D), _row(tb, D), _row(tb, 2 * D), _fixed((8, D)), _fixed((8, 3 * D))],
        out_specs=_row(tb, NW1), out_shape=jax.ShapeDtypeStruct((t, NW1), bf16),
        args=[dca, dca, dc4, dc4, pg, dbg, dz, dgates, wa, wg])


def _chunk_consts():
    r = lax.broadcasted_iota(jnp.int32, (CH, CH), 0)
    c = lax.broadcasted_iota(jnp.int32, (CH, CH), 1)
    return r, c, (r == c).astype(f32)


def _tri_inverse(lows, eye, r, c):
    def same_block(b):
        return jnp.bitwise_xor(r, c) < b

    xs = [jnp.where(same_block(8), -low, 0.0) for low in lows]
    ts = [eye + x for x in xs]
    for _ in range(2):
        xs = [_idot(x, x) for x in xs]
        ts = [t + _idot(t, x) for t, x in zip(ts, xs)]
    for b in (8, 16, 32):
        below = same_block(2 * b) & jnp.logical_not(same_block(b))
        ts = [t - _idot(_idot(t, jnp.where(below, low, 0.0)), t) for t, low in zip(ts, lows)]
    return ts


def _chunk_common(q, k, v, gcol, bcol, r, c, eye):
    grow = jnp.sum(eye * gcol, axis=0, keepdims=True)
    dec = jnp.exp(jnp.where(r >= c, gcol - grow, -jnp.inf))
    rcol = lax.broadcasted_iota(jnp.int32, (CH, 1), 0)
    glast = jnp.sum(jnp.where(rcol == CH - 1, gcol, 0.0), axis=0, keepdims=True)
    eg = jnp.exp(gcol)
    el = jnp.exp(glast - gcol)
    kb = k * bcol
    vb = v * bcol
    kk = _bdot_nt(kb, k)
    low = jnp.where(r > c, kk * dec, 0.0)
    qk = _bdot_nt(q, k)
    att = qk * dec
    return grow, dec, glast, eg, el, kb, vb, kk, low, qk, att, rcol


def _gdn_fwd(qn, kn, vc, gbeta):
    t = qn.shape[0]
    n_chunks = t // CH

    def body(q_ref, k_ref, v_ref, gb_ref, o_ref, s_ref, t_ref, state):
        @pl.when(pl.program_id(0) == 0)
        def _():
            state[...] = jnp.zeros_like(state)

        r, c, eye = _chunk_consts()
        tri = (r >= c).astype(f32)
        heads = range(H)
        keys = [(s, h) for s in range(GDN_STEP) for h in heads]
        rows = [slice(s * CH, (s + 1) * CH) for s in range(GDN_STEP)]
        gbs = [gb_ref[rows[s], :] for s in range(GDN_STEP)]
        galls = [_hdot(tri, gb) for gb in gbs]
        qs = {(s, h): q_ref[rows[s], h * DH:(h + 1) * DH] for s, h in keys}
        ks = {(s, h): k_ref[rows[s], h * DH:(h + 1) * DH] for s, h in keys}
        cm = {(s, h): _chunk_common(qs[s, h], ks[s, h], v_ref[rows[s], h * DH:(h + 1) * DH], galls[s][:, h:h + 1],
                                    gbs[s][:, H + h:H + h + 1], r, c, eye) for s, h in keys}
        invs = dict(zip(keys, _tri_inverse([cm[key][8] for key in keys], eye, r, c)))
        uws = {key: _bdot(invs[key], jnp.concatenate([cm[key][6], cm[key][5] * cm[key][3]], axis=1)) for key in keys}
        sts = [state[h] for h in heads]
        for s in range(GDN_STEP):
            vns = [uws[s, h][:, :DH] - _bdot(uws[s, h][:, DH:], sts[h]) for h in heads]
            outs = [_bdot(qs[s, h] * cm[s, h][3], sts[h]) + _bdot(cm[s, h][10], vns[h]) for h in heads]
            news = [sts[h] * jnp.exp(cm[s, h][2]) + _bdot_tn(ks[s, h] * cm[s, h][4], vns[h]) for h in heads]
            for h in heads:
                s_ref[s, h] = sts[h].astype(bf16)
                t_ref[s, h] = invs[s, h]
                o_ref[rows[s], h * DH:(h + 1) * DH] = outs[h]
            sts = news
        for h in heads:
            state[h] = sts[h]

    tb = GDN_STEP * CH
    return pl.pallas_call(
        body, name="gdn_fwd", grid=(t // tb,),
        in_specs=[_row(tb, D), _row(tb, D), _row(tb, D), _row(tb, 128)],
        out_specs=[_row(tb, D), pl.BlockSpec((GDN_STEP, H, DH, DH), lambda i: (i, 0, 0, 0)),
                   pl.BlockSpec((GDN_STEP, H, CH, CH), lambda i: (i, 0, 0, 0))],
        out_shape=[jax.ShapeDtypeStruct((t, D), f32), jax.ShapeDtypeStruct((n_chunks, H, DH, DH), bf16),
                   jax.ShapeDtypeStruct((n_chunks, H, CH, CH), f32)],
        scratch_shapes=[pltpu.VMEM((H, DH, DH), f32)],
        compiler_params=_params(1),
    )(qn, kn, vc, gbeta)


def _gdn_bwd(qn, kn, vc, gbeta, do, s_all, t_all):
    t = qn.shape[0]

    def body(q_ref, k_ref, v_ref, gb_ref, do_ref, s_ref, t_ref, dq_ref, dk_ref, dv_ref, dgb_ref, dstate):
        @pl.when(pl.program_id(0) == 0)
        def _():
            dstate[...] = jnp.zeros_like(dstate)

        r, c, eye = _chunk_consts()
        tril = r >= c
        lane = lax.broadcasted_iota(jnp.int32, (1, 128), 1)
        hs = range(H)

        def each(fn, *lists):
            return [fn(*args) for args in zip(*lists)]

        def rsum(a):
            return jnp.sum(a, axis=1, keepdims=True)

        def before_state(s):
            rows = slice(s * CH, (s + 1) * CH)
            gb = gb_ref[rows, :]
            gall = _hdot(tril.astype(f32), gb)
            p = {"rows": rows}
            p["q"] = q = [q_ref[rows, h * DH:(h + 1) * DH] for h in hs]
            p["k"] = k = [k_ref[rows, h * DH:(h + 1) * DH] for h in hs]
            p["v"] = v = [v_ref[rows, h * DH:(h + 1) * DH] for h in hs]
            p["dout"] = dout = [do_ref[rows, h * DH:(h + 1) * DH] for h in hs]
            p["inv"] = inv = [t_ref[s, h] for h in hs]
            p["st"] = st = [s_ref[s, h] for h in hs]
            p["bcol"] = bcol = [gb[:, H + h:H + h + 1] for h in hs]
            cm = [_chunk_common(q[h], k[h], v[h], gall[:, h:h + 1], bcol[h], r, c, eye) for h in hs]
            for name, i in (("dec", 1), ("glast", 2), ("eg", 3), ("el", 4), ("kb", 5), ("vb", 6), ("low", 8), ("att", 10)):
                p[name] = [m[i] for m in cm]
            p["rcol"] = cm[0][11]
            p["elast"] = each(jnp.exp, p["glast"])
            p["kbg"] = each(jnp.multiply, p["kb"], p["eg"])
            uw = each(lambda i, a, b: _bdot(i, jnp.concatenate([a, b], axis=1)), inv, p["vb"], p["kbg"])
            p["u"] = [a[:, :DH] for a in uw]
            p["w"] = [a[:, DH:] for a in uw]
            p["vn"] = each(lambda a, b, x: a - _bdot(b, x), p["u"], p["w"], st)
            p["qd"] = each(jnp.multiply, q, p["eg"])
            p["kd"] = each(jnp.multiply, k, p["el"])
            p["dqd"] = each(_bdot_nt, dout, st)
            p["datt"] = each(lambda d, x: jnp.where(tril, _bdot_nt(d, x), 0.0), dout, p["vn"])
            p["dqk"] = each(jnp.multiply, p["datt"], p["dec"])
            p["qd_do"] = each(_bdot_tn, p["qd"], dout)
            p["att_do"] = each(_bdot_tn, p["att"], dout)
            return p

        def after_state(p, ds):
            q, k, v, st, inv, bcol = p["q"], p["k"], p["v"], p["st"], p["inv"], p["bcol"]
            eg, el, kb, u, w = p["eg"], p["el"], p["kb"], p["u"], p["w"]
            dvn = each(lambda a, kk, x: a + _bdot(kk, x), p["att_do"], p["kd"], ds)
            dkd = each(_bdot_nt, p["vn"], ds)
            dw = each(lambda a, x: -_bdot_nt(a, x), dvn, st)
            new_ds = each(lambda x, e, a, ww, dv_: x * e + a - _bdot_tn(ww, dv_), ds, p["elast"], p["qd_do"], w, dvn)
            dglast = each(lambda e, x, d: e * jnp.sum(rsum(x.astype(f32) * d), axis=0, keepdims=True), p["elast"], st, ds)
            dr = each(lambda i, a, b: _bdot_tn(i, jnp.concatenate([a, b], axis=1)), inv, dvn, dw)
            dvb = [a[:, :DH] for a in dr]
            dkbg = [a[:, DH:] for a in dr]
            dlow = each(lambda a, b, x, y: -jnp.where(r > c, _bdot_nt(a, b) + _bdot_nt(x, y), 0.0), dvb, u, dkbg, w)
            dkk = each(jnp.multiply, dlow, p["dec"])
            mm = each(lambda a, b, x, y: a * b + x * y, dlow, p["low"], p["datt"], p["att"])
            dkb = each(lambda a, kk, b, e: _bdot(a, kk) + b * e, dkk, k, dkbg, eg)
            dk = each(lambda a, b, x, y, d, e, f, g: _bdot_tn(a, b) + _bdot_tn(x, y) + d * e + f * g,
                      dkk, kb, p["dqk"], q, dkd, el, dkb, bcol)
            dq = each(lambda a, kk, d, e: _bdot(a, kk) + d * e, p["dqk"], k, p["dqd"], eg)
            dv = each(jnp.multiply, dvb, bcol)
            dbeta = each(lambda a, b, x, y: rsum(a * b) + rsum(x * y), dkb, k, dvb, v)
            deg = each(lambda a, b, x, y: rsum(a * b) + rsum(x * y), dkbg, kb, p["dqd"], q)
            delc = each(lambda a, b, e: rsum(a * b) * e, dkd, k, el)
            dgc = each(lambda m, a, e, d: rsum(m) - rsum(eye * jnp.sum(m, axis=0, keepdims=True)) + a * e - d,
                       mm, deg, eg, delc)
            dgc = each(lambda g, d, l: g + jnp.where(p["rcol"] == CH - 1, jnp.sum(d, axis=0, keepdims=True) + l, 0.0),
                       dgc, delc, dglast)
            dg_acc = jnp.zeros((CH, 128), f32)
            db_acc = jnp.zeros((CH, 128), f32)
            rows = p["rows"]
            for h in hs:
                dq_ref[rows, h * DH:(h + 1) * DH] = dq[h]
                dk_ref[rows, h * DH:(h + 1) * DH] = dk[h]
                dv_ref[rows, h * DH:(h + 1) * DH] = dv[h]
                dg_acc = dg_acc + dgc[h] * (lane == h).astype(f32)
                db_acc = db_acc + dbeta[h] * (lane == H + h).astype(f32)
            dgb_ref[rows, :] = _hdot((r <= c).astype(f32), dg_acc) + db_acc
            return new_ds

        order = list(reversed(range(GDN_STEP)))
        pre = [before_state(s) for s in order]
        ds = [dstate[h] for h in hs]
        for p in pre:
            ds = after_state(p, ds)
        for h in hs:
            dstate[h] = ds[h]

    tb = GDN_STEP * CH
    n_steps = t // tb
    rev = lambda i: (n_steps - 1 - i, 0)
    rev4 = lambda i: (n_steps - 1 - i, 0, 0, 0)
    return pl.pallas_call(
        body, name="gdn_bwd", grid=(n_steps,),
        in_specs=[pl.BlockSpec((tb, D), rev), pl.BlockSpec((tb, D), rev), pl.BlockSpec((tb, D), rev),
                  pl.BlockSpec((tb, 128), rev), pl.BlockSpec((tb, D), rev),
                  pl.BlockSpec((GDN_STEP, H, DH, DH), rev4), pl.BlockSpec((GDN_STEP, H, CH, CH), rev4)],
        out_specs=[pl.BlockSpec((tb, D), rev), pl.BlockSpec((tb, D), rev), pl.BlockSpec((tb, D), rev),
                   pl.BlockSpec((tb, 128), rev)],
        out_shape=[jax.ShapeDtypeStruct((t, D), f32)] * 3 + [jax.ShapeDtypeStruct((t, 128), f32)],
        scratch_shapes=[pltpu.VMEM((H, DH, DH), f32)],
        compiler_params=_params(1),
    )(qn, kn, vc, gbeta, do, s_all, t_all)


def _pad_rows(w, rows=8):
    return jnp.pad(w, ((0, rows - w.shape[0]), (0, 0)))


_REST = ("w_up", "w_a_out", "w_b_out", "w_o", "w_down")


def _local_step(x, tgt, w, comm=None):
    g1 = w["norm_mix_g"].reshape(1, D)
    if comm is None:
        h1 = _rms_fwd(x, g1, name="rms1_fwd")
    else:
        h1, gathered = _rms_fwd(x, g1, name="rms1_fwd", exchange=comm.gather_first())
        w = {**w, **comm.finish_first(gathered)}
    w1, w2 = w["w1"], w["w2"]
    wa = _pad_rows(w["conv_a_w"])
    wg = _pad_rows(w["gdn_conv_w"])
    wf = _pad_rows(w["ffn_conv_w"])
    alog = jnp.pad(w["gdn_A_log"].reshape(1, H), ((0, 0), (0, 128 - H)))
    dtb = jnp.pad(w["gdn_dt_bias"].reshape(1, H), ((0, 0), (0, 128 - H)))
    g2 = w["norm_ffn_g"].reshape(1, D)
    g3 = w["norm_final_g"].reshape(1, D)
    gn = w["gdn_norm_g"].reshape(1, DH)

    if comm is None:
        pg = _matmul(h1, w1, name="mm_in", cols=(0, 6 * D), out_dtype=bf16)
        pq = _matmul(h1, w1, name="mm_in_qkv", cols=(6 * D, 3 * D))
    else:
        pg, gathered = _matmul(h1, w1, name="mm_in", cols=(0, 6 * D), out_dtype=bf16, exchange=comm.gather_rest())
        pq, gathered = _matmul(h1, w1, name="mm_in_qkv", cols=(6 * D, 3 * D), exchange=_gather_forward_exchange(gathered))
        w = {**w, **comm.finish_gather(gathered)}
    ya_in, qn, kn, vc, gbeta, p2 = _pre_fwd(pg, pq, h1, w2, wa, wg, alog, dtb)
    o, s_all, t_all = _gdn_fwd(qn, kn, vc, gbeta)
    yb_in = _post_fwd(o, pg, gn)
    ya = _matmul(ya_in, w["w_a_out"], name="mm_a", out_dtype=bf16)
    yb = _matmul(yb_in, w["w_b_out"], name="mm_b", out_dtype=bf16)
    mix = _mix_fwd(ya, yb, pg)
    x2 = _matmul(mix, w["w_o"], name="mm_o", add=x)
    h2 = _rms_fwd(x2, g2, name="rms2_fwd")
    up = _matmul(h2, w["w_up"], nt=True, name="mm_up", tn=DFF // 2, out_dtype=bf16)
    act = _ffn_fwd(up, wf)
    x3 = _matmul(act, w["w_down"], name="mm_down", add=x2, tm=512)
    loss_p, dx3, dx3b, dg3 = _final(x3, tgt, g3)

    grads = {"norm_final_g": dg3}
    dact = _matmul(dx3b, w["w_down"], nt=True, name="mm_down_dx", tm=512, tn=DFF, out_dtype=bf16)
    grads["w_down"] = _matmul_tn(act, dx3b, name="mm_down_dw", tm=DFF // 2)
    dc, dwf = _ffn_bwd1(dact, up, wf)
    grads["ffn_conv_w"] = dwf
    dup = _ffn_bwd2(dc, wf)
    dh2 = _matmul(dup, w["w_up"], name="mm_up_dx", tk=DFF)
    grads["w_up"] = _matmul_tn(dup, h2, name="mm_up_dw", tm=DFF // 2)
    dx2, dx2b, dg2 = _rms_bwd(dh2, x2, g2, dx3, name="rms2_bwd")
    grads["norm_ffn_g"] = dg2
    dmix = _matmul(dx2b, w["w_o"], nt=True, name="mm_o_dx", out_dtype=bf16)
    grads["w_o"] = _matmul_tn(mix, dx2b, name="mm_o_dw")
    dya, dyb, dgates = _mix_bwd(dmix, ya, yb, pg)
    dya_in = _matmul(dya, w["w_a_out"], nt=True, name="mm_a_dx", out_dtype=bf16)
    grads["w_a_out"] = _matmul_tn(ya_in, dya, name="mm_a_dw")
    dyb_in = _matmul(dyb, w["w_b_out"], nt=True, name="mm_b_dx")
    grads["w_b_out"] = _matmul_tn(yb_in, dyb, name="mm_b_dw")
    do, dz, dgn = _post_bwd(dyb_in, o, pg, gn)
    grads["gdn_norm_g"] = dgn
    dqn, dkn, dvc, dgb = _gdn_bwd(qn, kn, vc, gbeta, do, s_all, t_all)
    dbg, dca, dc4, dp2, dwa, dwg, dal, ddt, grads["w2"] = _pre_bwd1(pg, pq, p2, dya_in, dqn, dkn, dvc, dgb, gbeta, h1,
                                                                    wa, wg, alog, dtb)
    grads["conv_a_w"] = dwa
    grads["gdn_conv_w"] = dwg
    grads["gdn_A_log"] = dal
    grads["gdn_dt_bias"] = ddt
    if comm is None:
        dp1 = _pre_bwd2(dca, dc4, pg, dbg, dz, dgates, wa, wg)
        grads["w1"] = _matmul_tn(dp1, h1, name="mm_in_dw", tt=4096)
        dh1 = _matmul(dp1, w1, nt=True, name="mm_in_dx", tm=512, tk=NW1 // 2)
    else:
        exchange, blocks = comm.reduce_halves(_REST, grads)
        dp1, recv = _pre_bwd2(dca, dc4, pg, dbg, dz, dgates, wa, wg, exchange=exchange)
        exchange, sums = comm.reduce_sums(_REST, blocks, recv)
        grads["w1"], recv = _matmul_tn(dp1, h1, name="mm_in_dw", tt=4096, exchange=exchange)
        comm.finish_reduce(_REST, sums, recv)
        exchange, blocks = comm.reduce_halves(("w_in",), grads)
        exchange, sums = comm.reduce_sums(("w_in",), blocks, _run_exchange(exchange, name="rs_sibling_w_in"))
        dh1, recv = _matmul(dp1, w1, nt=True, name="mm_in_dx", tm=512, tk=NW1 // 2, exchange=exchange)
        comm.finish_reduce(("w_in",), sums, recv)
    dx, dg1 = _rms_bwd(dh1, x, g1, dx2, name="rms1_bwd", more=(dp2, w2), bf16_copy=False)
    grads["norm_mix_g"] = dg1
    return loss_p, dx, grads


_ANY = pl.BlockSpec(memory_space=pl.ANY)


def _remote(src, dst, send_sem, recv_sem, to):
    return pltpu.make_async_remote_copy(src_ref=src, dst_ref=dst, send_sem=send_sem, recv_sem=recv_sem,
                                        device_id=to, device_id_type=MESH)


def _run_exchange(exchange, *, name):
    arrays, shapes, sems, start, wait = exchange
    n_in, n_out = len(arrays), len(shapes)

    def body(*refs):
        start(refs[:n_in], refs[n_in:n_in + n_out], refs[n_in + n_out:])
        wait(refs[:n_in], refs[n_in:n_in + n_out], refs[n_in + n_out:])

    return pl.pallas_call(body, name=name, out_shape=list(shapes), in_specs=[_ANY] * n_in, out_specs=[_ANY] * n_out,
                          scratch_shapes=list(sems))(*arrays)


def _gather_exchange(shards):
    n = len(shards)

    def copies(x_refs, out_refs, sems):
        send_sems, recv_sems, local_sems = sems
        x, y, c = lax.axis_index("x"), lax.axis_index("y"), lax.axis_index("c")

        def flip(v, b):
            return v + b - 2 * v * b

        me, sibling = (x, y, c), (x, y, 1 - c)
        chip1, chip2, diag = (flip(x, 1 - c), flip(y, c)), (flip(x, c), flip(y, 1 - c)), (1 - x, 1 - y)

        def copy(a, k, blk, to, from_input=False):
            dst = out_refs[a].at[4 * blk[0] + 2 * blk[1] + blk[2]]
            return _remote(x_refs[a] if from_input else dst, dst, send_sems.at[a, k], recv_sems.at[a, k], to)

        mine = [pltpu.make_async_copy(x_refs[a], out_refs[a].at[4 * x + 2 * y + c], local_sems.at[a]) for a in range(n)]
        first = []
        for a in range(n):
            first += [copy(a, 0, me, sibling, from_input=True), copy(a, 1, me, (*chip1, c), from_input=True),
                      copy(a, 2, me, (*chip2, c), from_input=True)]
        return copy, mine, first, me, sibling, chip1, chip2, diag, c

    def start(x_refs, out_refs, sems):
        _, mine, first, *_ = copies(x_refs, out_refs, sems)
        for cp in mine + first:
            cp.start()

    def wait(x_refs, out_refs, sems):
        copy, mine, first, me, sibling, chip1, chip2, diag, c = copies(x_refs, out_refs, sems)
        passed = []

        def pass_on(cp):
            passed.append(cp)
            cp.start()

        for a in range(n):
            copy(a, 1, (*chip1, c), me).wait_recv()
            pass_on(copy(a, 3, (*chip1, c), (*chip2, c)))
            pass_on(copy(a, 4, (*chip1, c), sibling))
        for a in range(n):
            copy(a, 2, (*chip2, c), me).wait_recv()
            pass_on(copy(a, 5, (*chip2, c), sibling))
        for a in range(n):
            copy(a, 3, (*diag, c), me).wait_recv()
            pass_on(copy(a, 6, (*diag, c), sibling))
        for a in range(n):
            copy(a, 0, sibling, me).wait_recv()
            copy(a, 4, (*chip2, 1 - c), me).wait_recv()
            copy(a, 5, (*chip1, 1 - c), me).wait_recv()
            copy(a, 6, (*diag, 1 - c), me).wait_recv()
        for cp in first + passed:
            cp.wait_send()
        for cp in mine:
            cp.wait()

    shapes = [jax.ShapeDtypeStruct((N_DEV, *s.shape), s.dtype) for s in shards]
    sems = [pltpu.SemaphoreType.DMA((n, 7)), pltpu.SemaphoreType.DMA((n, 7)), pltpu.SemaphoreType.DMA((n,))]
    return shards, shapes, sems, start, wait


def _gather_direct_exchange(shards):
    n = len(shards)

    def copies(x_refs, out_refs, sems):
        send_sems, recv_sems, local_sems = sems
        x, y, c = lax.axis_index("x"), lax.axis_index("y"), lax.axis_index("c")
        targets = [(x, y, 1 - c), (1 - x, y, c), (x, 1 - y, c), (1 - x, 1 - y, c)]
        local, sends, recvs = [], [], []
        for a in range(n):
            mine = out_refs[a].at[4 * x + 2 * y + c]
            local.append(pltpu.make_async_copy(x_refs[a], mine, local_sems.at[a]))
            for k, to in enumerate(targets):
                theirs = out_refs[a].at[4 * to[0] + 2 * to[1] + to[2]]
                sends.append(_remote(x_refs[a], mine, send_sems.at[a, k], recv_sems.at[a, k], to))
                recvs.append(_remote(theirs, theirs, send_sems.at[a, k], recv_sems.at[a, k], to))
        return local, sends, recvs

    def start(x_refs, out_refs, sems):
        local, sends, _ = copies(x_refs, out_refs, sems)
        for cp in local + sends:
            cp.start()

    def wait(x_refs, out_refs, sems):
        local, sends, recvs = copies(x_refs, out_refs, sems)
        for cp in recvs:
            cp.wait_recv()
        for cp in sends:
            cp.wait_send()
        for cp in local:
            cp.wait()

    shapes = [jax.ShapeDtypeStruct((N_DEV, *s.shape), s.dtype) for s in shards]
    sems = [pltpu.SemaphoreType.DMA((n, 4)), pltpu.SemaphoreType.DMA((n, 4)), pltpu.SemaphoreType.DMA((n,))]
    return shards, shapes, sems, start, wait


def _gather_forward_exchange(gathered):
    n = len(gathered)

    def copies(_, out_refs, sems):
        send_sems, recv_sems = sems
        x, y, c = lax.axis_index("x"), lax.axis_index("y"), lax.axis_index("c")
        sibling = (x, y, 1 - c)
        sends, recvs = [], []
        for a in range(n):
            for j, (px, py) in enumerate([(1 - x, y), (x, 1 - y), (1 - x, 1 - y)]):
                mine = out_refs[a].at[4 * px + 2 * py + c]
                theirs = out_refs[a].at[4 * px + 2 * py + 1 - c]
                sends.append(_remote(mine, mine, send_sems.at[a, j], recv_sems.at[a, j], sibling))
                recvs.append(_remote(theirs, theirs, send_sems.at[a, j], recv_sems.at[a, j], sibling))
        return sends, recvs

    def start(in_refs, out_refs, sems):
        for cp in copies(in_refs, out_refs, sems)[0]:
            cp.start()

    def wait(in_refs, out_refs, sems):
        sends, recvs = copies(in_refs, out_refs, sems)
        for cp in recvs:
            cp.wait_recv()
        for cp in sends:
            cp.wait_send()

    shapes = [jax.ShapeDtypeStruct(g.shape, g.dtype) for g in gathered]
    sems = [pltpu.SemaphoreType.DMA((n, 3)), pltpu.SemaphoreType.DMA((n, 3))]
    return gathered, shapes, sems, start, wait, True


def _chips_exchange(hsums):
    n = len(hsums)

    def copies(h_refs, out_refs, sems):
        send_sems, recv_sems = sems
        x, y, c = lax.axis_index("x"), lax.axis_index("y"), lax.axis_index("c")
        chips = [(1 - x, y), (x, 1 - y), (1 - x, 1 - y)]
        return [_remote(h_refs[a].at[2 * px + py], out_refs[a].at[k], send_sems.at[a, k], recv_sems.at[a, k], (px, py, c))
                for a in range(n) for k, (px, py) in enumerate(chips)]

    def start(h_refs, out_refs, sems):
        for cp in copies(h_refs, out_refs, sems):
            cp.start()

    def wait(h_refs, out_refs, sems):
        for cp in copies(h_refs, out_refs, sems):
            cp.wait()

    shapes = [jax.ShapeDtypeStruct((3, *h.shape[1:]), h.dtype) for h in hsums]
    sems = [pltpu.SemaphoreType.DMA((n, 3)), pltpu.SemaphoreType.DMA((n, 3))]
    return hsums, shapes, sems, start, wait


def _sibling_exchange(halves):
    n = len(halves)

    def copies(p_refs, out_refs, sems):
        send_sems, recv_sems = sems
        x, y, c = lax.axis_index("x"), lax.axis_index("y"), lax.axis_index("c")
        return [_remote(p_refs[a], out_refs[a], send_sems.at[a], recv_sems.at[a], (x, y, 1 - c)) for a in range(n)]

    def start(p_refs, out_refs, sems):
        for cp in copies(p_refs, out_refs, sems):
            cp.start()

    def wait(p_refs, out_refs, sems):
        for cp in copies(p_refs, out_refs, sems):
            cp.wait()

    shapes = [jax.ShapeDtypeStruct(h.shape, h.dtype) for h in halves]
    return halves, shapes, [pltpu.SemaphoreType.DMA((n,)), pltpu.SemaphoreType.DMA((n,))], start, wait


_IN_RANGES = ((0, 3 * D, 0, 0), (3 * D, 6 * D, 0, 6 * D), (6 * D, 7 * D, 0, 3 * D), (7 * D, 7 * D + 16, 1, 0),
              (7 * D + 16, 9 * D + 16, 0, 4 * D))


def _col_pieces(width, ranges):
    pieces = []
    for d in range(N_DEV):
        lo, hi = d * width, (d + 1) * width
        for glo, ghi, mat, mlo in ranges:
            a, b = max(lo, glo), min(hi, ghi)
            if a < b:
                pieces.append((d, a - lo, b - lo, mat, mlo + a - glo))
    return pieces


def _cols_to_matrices(g, ranges, out_widths, *, name):
    _, rows, width = g.shape
    tb = 128
    pieces = _col_pieces(width, ranges)
    covered = [sum(p[2] - p[1] for p in pieces if p[3] == m) for m in range(len(out_widths))]

    def body(g_ref, *o_refs):
        for m, o_ref in enumerate(o_refs):
            if covered[m] < out_widths[m]:
                o_ref[...] = jnp.zeros_like(o_ref)
        for d, b0, b1, m, m0 in pieces:
            o_refs[m][:, m0:m0 + b1 - b0] = g_ref[d, :, b0:b1]

    return pl.pallas_call(
        body, name=name, grid=(rows // tb,), in_specs=[pl.BlockSpec((N_DEV, tb, width), lambda i: (0, i, 0))],
        out_specs=[pl.BlockSpec((tb, wo), lambda i: (i, 0)) for wo in out_widths],
        out_shape=[jax.ShapeDtypeStruct((rows, wo), g.dtype) for wo in out_widths], compiler_params=_params(1),
    )(g)


def _transposed_matrices_to_blocks(mats, ranges, width, *, name):
    rows = mats[0].shape[1]
    pieces = _col_pieces(width, ranges)

    def body(*refs):
        m_refs, g_ref = refs[:-1], refs[-1]
        for d, b0, b1, m, m0 in pieces:
            g_ref[d, b0:b1, :] = m_refs[m][m0:m0 + b1 - b0, :]

    return pl.pallas_call(
        body, name=name, grid=(rows // 128,),
        in_specs=[pl.BlockSpec((mt.shape[0], 128), lambda i: (0, i)) for mt in mats],
        out_specs=pl.BlockSpec((N_DEV, width, 128), lambda i: (0, 0, i)),
        out_shape=jax.ShapeDtypeStruct((N_DEV, width, rows), mats[0].dtype), compiler_params=_params(1),
    )(*mats)


def _row_block(rows):
    return 128 if rows % 128 == 0 else rows


def _half_bf16(g4, c_other, *, name):
    _, _, rows, width = g4.shape
    tb = _row_block(rows)

    def body(c_ref, p_ref, o_ref):
        o_ref[0] = p_ref[0, 0].astype(bf16)

    grid_spec = pltpu.PrefetchScalarGridSpec(
        num_scalar_prefetch=1, grid=(4, rows // tb),
        in_specs=[pl.BlockSpec((1, 1, tb, width), lambda j, i, c_ref: (j, c_ref[0], i, 0))],
        out_specs=pl.BlockSpec((1, tb, width), lambda j, i, c_ref: (j, i, 0)))
    return pl.pallas_call(
        body, name=name, grid_spec=grid_spec, out_shape=jax.ShapeDtypeStruct((4, rows, width), bf16),
        compiler_params=_params(2, _vmem_for(4 * tb * width, 2 * tb * width)),
    )(c_other, g4)


def _pair_sum(g4, recv, c_me, *, name):
    _, _, rows, width = g4.shape
    tb = _row_block(rows)

    def body(c_ref, p_ref, r_ref, o_ref, ob_ref):
        s = p_ref[0, 0] + r_ref[0].astype(f32)
        o_ref[0] = s
        ob_ref[0] = s.astype(bf16)

    blk = pl.BlockSpec((1, tb, width), lambda j, i, c_ref: (j, i, 0))
    grid_spec = pltpu.PrefetchScalarGridSpec(
        num_scalar_prefetch=1, grid=(4, rows // tb),
        in_specs=[pl.BlockSpec((1, 1, tb, width), lambda j, i, c_ref: (j, c_ref[0], i, 0)), blk],
        out_specs=[blk, blk])
    return pl.pallas_call(
        body, name=name, grid_spec=grid_spec,
        out_shape=[jax.ShapeDtypeStruct((4, rows, width), f32), jax.ShapeDtypeStruct((4, rows, width), bf16)],
        compiler_params=_params(2, _vmem_for(4 * tb * width, 2 * tb * width, 4 * tb * width, 2 * tb * width)),
    )(c_me, g4, recv)


def _adam_shard(hsum, recv, chip, w, m, v, *, name):
    _, rows, width = w.shape
    tb = _row_block(rows)

    def body(j_ref, h_ref, r_ref, w_ref, m_ref, v_ref, g_out, d_out, m_out, v_out):
        g = ((h_ref[0] + r_ref[0].astype(f32)) + r_ref[1].astype(f32)) + r_ref[2].astype(f32)
        delta, mn, vn = _adam_math(w_ref[0], g, m_ref[0], v_ref[0])
        g_out[0] = g
        d_out[0] = delta
        m_out[0] = mn
        v_out[0] = vn

    blk = pl.BlockSpec((1, tb, width), lambda i, j_ref: (0, i, 0))
    grid_spec = pltpu.PrefetchScalarGridSpec(
        num_scalar_prefetch=1, grid=(rows // tb,),
        in_specs=[pl.BlockSpec((1, tb, width), lambda i, j_ref: (j_ref[0], i, 0)),
                  pl.BlockSpec((3, tb, width), lambda i, j_ref: (0, i, 0)), blk, blk, blk],
        out_specs=[blk, blk, blk, blk])
    return pl.pallas_call(
        body, name=name, grid_spec=grid_spec, out_shape=[jax.ShapeDtypeStruct(w.shape, f32)] * 4,
        compiler_params=_params(1, _vmem_for(*[4 * tb * width] * 8, 6 * tb * width)),
    )(chip, hsum, recv, w, m, v)


def _sum_shard(hsum, recv, chip, *, name):
    _, rows, width = hsum.shape
    tb = _row_block(rows)

    def body(j_ref, h_ref, r_ref, g_out):
        g_out[...] = ((h_ref[0] + r_ref[0].astype(f32)) + r_ref[1].astype(f32)) + r_ref[2].astype(f32)

    grid_spec = pltpu.PrefetchScalarGridSpec(
        num_scalar_prefetch=1, grid=(rows // tb,),
        in_specs=[pl.BlockSpec((1, tb, width), lambda i, j_ref: (j_ref[0], i, 0)),
                  pl.BlockSpec((3, tb, width), lambda i, j_ref: (0, i, 0))],
        out_specs=pl.BlockSpec((tb, width), lambda i, j_ref: (i, 0)))
    return pl.pallas_call(body, name=name, grid_spec=grid_spec, out_shape=jax.ShapeDtypeStruct((rows, width), f32),
                          compiler_params=_params(1, _vmem_for(*[4 * tb * width] * 2, 6 * tb * width)))(chip, hsum, recv)


def _adam_columns(g, w, m, v, *, name):
    cols, _, rows = w.shape
    tb = cols // 2

    def body(g_ref, w_ref, m_ref, v_ref, d_out, m_out, v_out):
        delta, mn, vn = _adam_math(w_ref[...], g_ref[...], m_ref[...], v_ref[...])
        d_out[...] = delta
        m_out[...] = mn
        v_out[...] = vn

    blk = pl.BlockSpec((tb, 1, rows), lambda i: (i, 0, 0))
    return pl.pallas_call(
        body, name=name, grid=(cols // tb,), in_specs=[blk] * 4, out_specs=[blk] * 3,
        out_shape=[jax.ShapeDtypeStruct(w.shape, f32)] * 3,
        compiler_params=_params(1, _vmem_for(*[4 * tb * rows] * 7)),
    )(g, w, m, v)


R_SMALL = 8 + 8 * N_DEV
_SMALL_LANES = {"gdn_norm_g": (0, DH), "gdn_A_log": (DH, DH + H), "gdn_dt_bias": (2 * DH, 2 * DH + H)}
_LOSS_LANE = 3 * DH


def _pack_small(dg1, dg2, dg3, dgn, dal, ddt, loss_p, dwa, dwg, dwf):
    def body(dg1_ref, dg2_ref, dg3_ref, dgn_ref, dal_ref, ddt_ref, loss_ref, dwa_ref, dwg_ref, dwf_ref, o_ref):
        def total(ref):
            return jnp.sum(ref[...], axis=0, keepdims=True)

        o_ref[...] = jnp.zeros_like(o_ref)
        o_ref[0:1, :] = total(dg1_ref)
        o_ref[1:2, :] = total(dg2_ref)
        o_ref[2:3, :] = total(dg3_ref)
        o_ref[3:4, 0:DH] = total(dgn_ref)
        o_ref[3:4, DH:2 * DH] = total(dal_ref)
        o_ref[3:4, 2 * DH:3 * DH] = total(ddt_ref)
        o_ref[3:4, 3 * DH:4 * DH] = total(loss_ref)
        for d in range(N_DEV):
            base = 8 + 8 * d
            o_ref[base:base + 3, 0:128] = dwa_ref[0:3, 128 * d:128 * (d + 1)]
            o_ref[base:base + 4, 128:512] = dwg_ref[0:4, 384 * d:384 * (d + 1)]
            o_ref[base + 4:base + 7, 0:704] = dwf_ref[0:3, 704 * d:704 * (d + 1)]

    return pl.pallas_call(body, name="pack_small", out_shape=jax.ShapeDtypeStruct((R_SMALL, D), f32))(
        dg1, dg2, dg3, dgn, dal, ddt, loss_p, dwa, dwg, dwf)


_SMALL = ("norm_mix_g", "norm_ffn_g", "norm_final_g", "gdn_norm_g", "gdn_A_log", "gdn_dt_bias",
          "conv_a_w", "gdn_conv_w", "ffn_conv_w")


def _adam_small(gath, me, w, m, v):
    arrays = [t[n] for n in _SMALL for t in (w, m, v)]

    def body(me_ref, ga_ref, gb_ref, *refs):
        ins, outs = refs[:len(arrays)], refs[len(arrays):]
        ga, gb = ga_ref[0], gb_ref[0]
        for s in range(1, N_DEV):
            ga = ga + ga_ref[s]
            gb = gb + gb_ref[s]
        grads = {"norm_mix_g": ga[0:1, :], "norm_ffn_g": ga[1:2, :], "norm_final_g": ga[2:3, :],
                 "conv_a_w": gb[0:3, 0:128], "gdn_conv_w": gb[0:4, 128:512], "ffn_conv_w": gb[4:7, 0:704]}
        for n, (lo, hi) in _SMALL_LANES.items():
            grads[n] = ga[3:4, lo:hi]
        for i, n in enumerate(_SMALL):
            three_d = len(w[n].shape) == 3
            wv, mv, vv = (r[0] if three_d else r[...] for r in ins[3 * i:3 * i + 3])
            delta, mn, vn = _adam_math(wv, grads[n], mv, vv)
            for o_ref, val in zip(outs[4 * i:4 * i + 4], (grads[n], delta, mn, vn)):
                if three_d:
                    o_ref[0] = val
                else:
                    o_ref[...] = val
        outs[-1][...] = ga[3:4, _LOSS_LANE:_LOSS_LANE + 1]

    def whole(shape):
        return pl.BlockSpec(shape, lambda i, me_ref: (0,) * len(shape))

    grid_spec = pltpu.PrefetchScalarGridSpec(
        num_scalar_prefetch=1, grid=(1,),
        in_specs=[pl.BlockSpec((N_DEV, 8, D), lambda i, me_ref: (0, 0, 0)),
                  pl.BlockSpec((N_DEV, 8, D), lambda i, me_ref: (0, 1 + me_ref[0], 0))] + [whole(a.shape) for a in arrays],
        out_specs=[whole(w[n].shape) for n in _SMALL for _ in range(4)] + [whole((1, 1))])
    res = pl.pallas_call(
        body, name="adam_small", grid_spec=grid_spec,
        out_shape=[jax.ShapeDtypeStruct(w[n].shape, f32) for n in _SMALL for _ in range(4)]
        + [jax.ShapeDtypeStruct((1, 1), f32)],
        compiler_params=_params(1),
    )(me, gath, gath, *arrays)
    return {n: tuple(res[4 * i:4 * i + 4]) for i, n in enumerate(_SMALL)}, res[-1]


def _adam_math(w, g, m, v):
    m = ADAM_B1 * m + (1.0 - ADAM_B1) * g
    v = ADAM_B2 * v + (1.0 - ADAM_B2) * jnp.square(g)
    m_hat = m / (1.0 - ADAM_B1 ** ADAM_STEP)
    v_hat = v / (1.0 - ADAM_B2 ** ADAM_STEP)
    delta = -ADAM_LR * (m_hat / (jnp.sqrt(v_hat) + ADAM_EPS) + ADAM_WD * w)
    return delta, m, v


_WEIGHTS = ("norm_mix_g", "w_in", "conv_a_w", "gdn_conv_w", "gdn_A_log", "gdn_dt_bias", "gdn_norm_g", "w_a_out",
            "w_b_out", "w_o", "norm_ffn_g", "w_up", "ffn_conv_w", "w_down", "norm_final_g")
_CONVS = ("conv_a_w", "gdn_conv_w", "ffn_conv_w")


class _StepExchanges:
    def __init__(self, wts, mom, var, c_me, chip):
        self.wts, self.mom, self.var, self.c_me, self.chip = wts, mom, var, c_me, chip
        self.results = {}

    def gather_first(self):
        return _gather_exchange([self.wts["w_in"][0].astype(bf16)] + [self.wts[n][0] for n in _CONVS])

    def finish_first(self, gathered):
        g_in, gc_a, gc_g, gc_f = gathered
        w1, w2 = _cols_to_matrices(g_in, _IN_RANGES, (NW1, 128), name="relay_w_in")
        return {"w1": w1, "w2": w2, "conv_a_w": gc_a.transpose(1, 0, 2).reshape(3, D),
                "gdn_conv_w": gc_g.transpose(1, 0, 2).reshape(4, 3 * D),
                "ffn_conv_w": gc_f.transpose(1, 0, 2).reshape(3, 2 * DFF)}

    def gather_rest(self):
        return _gather_direct_exchange([self.wts[n][0].astype(bf16) for n in _REST])

    def finish_gather(self, gathered):
        g_up, g_a, g_b, g_o, g_down = gathered
        return {"w_up": g_up.reshape(2 * DFF, D), "w_a_out": g_a.reshape(D, D), "w_b_out": g_b.reshape(D, D),
                "w_o": g_o.reshape(D, D), "w_down": g_down.reshape(DFF, D)}

    def reduce_halves(self, names, grads):
        blocks = []
        for n in names:
            if n == "w_in":
                g = _transposed_matrices_to_blocks([grads["w1"], grads["w2"]], _IN_RANGES, R_IN, name="relay_dw_in")
                blocks.append(g.reshape(4, 2, R_IN, D))
            else:
                blocks.append(grads[n].reshape(4, 2, *self.wts[n].shape[1:]))
        return _sibling_exchange([_half_bf16(g, 1 - self.c_me, name="rs_half_" + n) for n, g in zip(names, blocks)]), blocks

    def reduce_sums(self, names, blocks, recv):
        sums = [_pair_sum(g, r, self.c_me, name="rs_sum_" + n) for n, g, r in zip(names, blocks, recv)]
        return _chips_exchange([s[1] for s in sums]), [s[0] for s in sums]

    def finish_reduce(self, names, sums, recv):
        for n, s, r in zip(names, sums, recv):
            if n == "w_in":
                g = _sum_shard(s, r, self.chip, name="rs_total_w_in")[:, None, :]
                w, m, v = (jnp.transpose(t[n], (2, 0, 1)) for t in (self.wts, self.mom, self.var))
                res = (g, *_adam_columns(g, w, m, v, name="adam_w_in"))
                self.results[n] = tuple(jnp.transpose(a, (1, 2, 0)) for a in res)
            else:
                self.results[n] = _adam_shard(s, r, self.chip, self.wts[n], self.mom[n], self.var[n], name="adam_" + n)


def kernel(x, norm_mix_g, w_in, conv_a_w, gdn_conv_w, gdn_A_log, gdn_dt_bias, gdn_norm_g, w_a_out, w_b_out, w_o, norm_ffn_g, w_up, ffn_conv_w, w_down, norm_final_g, loss_target, m_norm_mix_g, m_w_in, m_conv_a_w, m_gdn_conv_w, m_gdn_A_log, m_gdn_dt_bias, m_gdn_norm_g, m_w_a_out, m_w_b_out, m_w_o, m_norm_ffn_g, m_w_up, m_ffn_conv_w, m_w_down, m_norm_final_g, v_norm_mix_g, v_w_in, v_conv_a_w, v_gdn_conv_w, v_gdn_A_log, v_gdn_dt_bias, v_gdn_norm_g, v_w_a_out, v_w_b_out, v_w_o, v_norm_ffn_g, v_w_up, v_ffn_conv_w, v_w_down, v_norm_final_g):
    wts = dict(zip(_WEIGHTS, (norm_mix_g, w_in, conv_a_w, gdn_conv_w, gdn_A_log, gdn_dt_bias, gdn_norm_g, w_a_out,
                              w_b_out, w_o, norm_ffn_g, w_up, ffn_conv_w, w_down, norm_final_g)))
    mom = dict(zip(_WEIGHTS, (m_norm_mix_g, m_w_in, m_conv_a_w, m_gdn_conv_w, m_gdn_A_log, m_gdn_dt_bias,
                              m_gdn_norm_g, m_w_a_out, m_w_b_out, m_w_o, m_norm_ffn_g, m_w_up, m_ffn_conv_w,
                              m_w_down, m_norm_final_g)))
    var = dict(zip(_WEIGHTS, (v_norm_mix_g, v_w_in, v_conv_a_w, v_gdn_conv_w, v_gdn_A_log, v_gdn_dt_bias,
                              v_gdn_norm_g, v_w_a_out, v_w_b_out, v_w_o, v_norm_ffn_g, v_w_up, v_ffn_conv_w,
                              v_w_down, v_norm_final_g)))
    cx, cy, cc = lax.axis_index("x"), lax.axis_index("y"), lax.axis_index("c")
    c_me = jnp.reshape(cc, (1,)).astype(jnp.int32)
    chip = jnp.reshape(2 * cx + cy, (1,)).astype(jnp.int32)
    me = jnp.reshape(4 * cx + 2 * cy + cc, (1,)).astype(jnp.int32)

    def with_up_transposed(t):
        return {**t, "w_up": jnp.swapaxes(t["w_up"], 1, 2)}

    comm = _StepExchanges(with_up_transposed(wts), with_up_transposed(mom), with_up_transposed(var), c_me, chip)
    replicated = {n: wts[n] for n in ("norm_mix_g", "norm_ffn_g", "norm_final_g", "gdn_norm_g", "gdn_A_log", "gdn_dt_bias")}
    loss_p, dx, grads = _local_step(x[0], loss_target[0], replicated, comm)
    res = comm.results
    res["w_up"] = tuple(jnp.swapaxes(a, 1, 2) for a in res["w_up"])

    small = _pack_small(grads["norm_mix_g"], grads["norm_ffn_g"], grads["norm_final_g"], grads["gdn_norm_g"],
                        grads["gdn_A_log"], grads["gdn_dt_bias"], loss_p, grads["conv_a_w"], grads["gdn_conv_w"],
                        grads["ffn_conv_w"])
    (small_all,) = _run_exchange(_gather_exchange([small]), name="ag_small")

    def raw(t):
        return {n: t[n].reshape(1, D) if n == "norm_final_g" else t[n] for n in _SMALL}

    res_small, loss = _adam_small(small_all, me, raw(wts), raw(mom), raw(var))
    for n in _SMALL:
        res[n] = tuple(a.reshape(wts[n].shape) for a in res_small[n])
    outs = [[res[n][i] for n in _WEIGHTS] for i in range(4)]
    return (loss.reshape(()), dx[None], *outs[0], *outs[1], *outs[2], *outs[3])
```

```python
import jax
import jax.numpy as jnp
from jax import lax
from jax.experimental import pallas as pl
from jax.experimental.pallas import tpu as pltpu

f32 = jnp.float32
bf16 = jnp.bfloat16

D = 1024
H = 8
DH = 128
CH = 64
GDN_STEP = 2
ROW_BLOCK = 512
ELEMENTWISE_BLOCK = 1024
DFF = 2816
NW1 = 9216
EPS = 1e-6
N_DEV = 8

ADAM_LR = 0.001
ADAM_B1 = 0.9
ADAM_B2 = 0.999
ADAM_EPS = 1e-08
ADAM_WD = 0.01
ADAM_STEP = 10

VMEM_LIMIT_BYTES = 48 * 1024 * 1024
VMEM_MAX_BYTES = 56 * 1024 * 1024

R_IN, R_UP = 1154, 704

_HI = lax.Precision.HIGHEST
MESH = pl.DeviceIdType.MESH


def _params(n_grid, vmem_bytes=None):
    return pltpu.CompilerParams(dimension_semantics=("arbitrary",) * n_grid,
                                vmem_limit_bytes=VMEM_LIMIT_BYTES if vmem_bytes is None else vmem_bytes)


def _vmem_for(*block_bytes, extra=0):
    need = 2 * sum(block_bytes) + extra + 4 * 1024 * 1024
    return min(max(need, VMEM_LIMIT_BYTES), VMEM_MAX_BYTES)


def _bdot(a, b):
    return jnp.dot(a.astype(bf16), b.astype(bf16), preferred_element_type=f32)


def _bdot_nt(a, b):
    return lax.dot_general(a.astype(bf16), b.astype(bf16), (((1,), (1,)), ((), ())), preferred_element_type=f32)


def _bdot_tn(a, b):
    return lax.dot_general(a.astype(bf16), b.astype(bf16), (((0,), (0,)), ((), ())), preferred_element_type=f32)


def _hdot(a, b):
    return jnp.dot(a, b, preferred_element_type=f32, precision=_HI)


def _idot(a, b):
    return jnp.dot(a, b, preferred_element_type=f32, precision=lax.Precision.HIGH)


def _sigmoid(x):
    return 1.0 / (1.0 + jnp.exp(-x))


def _softplus(x):
    return jnp.maximum(x, 0.0) + jnp.log(1.0 + jnp.exp(-jnp.abs(x)))


def _shift_down(x, halo, j):
    if j == 0:
        return x
    xr = pltpu.roll(x, j, 0)
    hr = pltpu.roll(halo, j, 0)
    r8 = lax.broadcasted_iota(jnp.int32, hr.shape, 0)
    top = jnp.where(r8 < j, hr, xr[:8])
    return jnp.concatenate([top, xr[8:]], axis=0)


def _shift_up(x, halo, j):
    if j == 0:
        return x
    n = x.shape[0]
    xr = pltpu.roll(x, n - j, 0)
    hr = pltpu.roll(halo, 8 - j, 0)
    r8 = lax.broadcasted_iota(jnp.int32, hr.shape, 0)
    bot = jnp.where(r8 >= 8 - j, hr, xr[n - 8:])
    return jnp.concatenate([xr[:n - 8], bot], axis=0)


def _taps_down(x, halo, k):
    return [_shift_down(x, halo, k - 1 - j) for j in range(k)]


def _strip(i, base=0):
    return slice(base + i * 128, base + (i + 1) * 128)


def _strip_taps(x, halo, first, k):
    return _taps_down(x, jnp.where(first, 0.0, halo), k)


def _strip_conv(w_ref, sl, taps):
    out = w_ref[0:1, sl] * taps[0]
    for j in range(1, len(taps)):
        out = out + w_ref[j:j + 1, sl] * taps[j]
    return out


def _strip_weight_grad(dw_ref, sl, dy, taps):
    for j, tap in enumerate(taps):
        dw_ref[j:j + 1, sl] += jnp.sum(dy * tap, axis=0, keepdims=True)


def _strip_conv_up(dy, halo, last, w_ref, sl, k):
    halo = jnp.where(last, 0.0, halo)
    out = w_ref[k - 1:k, sl] * dy
    for j in range(k - 1):
        out = out + w_ref[j:j + 1, sl] * _shift_up(dy, halo, k - 1 - j)
    return out


def _row(tb, w, col=0):
    return pl.BlockSpec((tb, w), lambda i: (i, col))


def _prev(tb, w, col=0, rows=8):
    return pl.BlockSpec((rows, w), lambda i: (jnp.maximum(i * (tb // rows) - 1, 0), col))


def _next(tb, w, n_rows, col=0, rows=8):
    last = n_rows // rows - 1
    return pl.BlockSpec((rows, w), lambda i: (jnp.minimum((i + 1) * (tb // rows), last), col))


def _f32(ref, sl):
    return ref[:, sl].astype(f32)


def _halo_before(ref, sl):
    h = _f32(ref, sl)
    return h[h.shape[0] - 8:]


def _halo_after(ref, sl):
    return _f32(ref, sl)[:8]


def _fixed(shape):
    return pl.BlockSpec(shape, lambda i: (0,) * len(shape))


def _pick(n, prefs):
    for p in prefs:
        if n % p == 0:
            return p
    return n


def _matmul(a, b, *, name, nt=False, add=None, tm=2048, tn=1024, tk=None, out_dtype=f32, cols=None, exchange=None):
    m, kd = a.shape
    col0, n = cols if cols is not None else (0, b.shape[0] if nt else b.shape[1])
    tm = _pick(m, (tm, 1024, 512, 256))
    tn = _pick(n, (tn, 1024, 512, 128))
    tk = kd if tk is None else tk
    nk = kd // tk
    assert nk == 1 or out_dtype == f32
    assert col0 % tn == 0 and not (nt and cols)
    j0 = col0 // tn
    dims = (((1,), (1,)), ((), ())) if nt else (((1,), (0,)), ((), ()))

    def body(a_ref, b_ref, *rest):
        o_ref = rest[-1]
        part = lax.dot_general(a_ref[...], b_ref[...], dims, preferred_element_type=f32)
        if nk == 1:
            o_ref[...] = (part if add is None else part + rest[0][...]).astype(out_dtype)
            return
        k = pl.program_id(2)

        @pl.when(k == 0)
        def _():
            o_ref[...] = part if add is None else part + rest[0][...]

        @pl.when(k > 0)
        def _():
            o_ref[...] += part

    b_spec = pl.BlockSpec((tn, tk), lambda i, j, k: (j, k)) if nt else pl.BlockSpec((tk, tn), lambda i, j, k: (k, j + j0))
    in_specs = [pl.BlockSpec((tm, tk), lambda i, j, k: (i, k)), b_spec]
    args = [a, b]
    if add is not None:
        in_specs.append(pl.BlockSpec((tm, tn), lambda i, j, k: (i, j)))
        args.append(add)
    vmem = _vmem_for(2 * tm * tk, 2 * tk * tn, tm * tn * jnp.dtype(out_dtype).itemsize,
                     4 * tm * tn if add is not None else 0, extra=4 * tm * tn)
    return _call_with_exchange(
        body, exchange, name=name, grid=(m // tm, n // tn, nk), in_specs=in_specs,
        out_specs=pl.BlockSpec((tm, tn), lambda i, j, k: (i, j)),
        out_shape=jax.ShapeDtypeStruct((m, n), out_dtype), args=args, vmem_bytes=vmem)


def _call_with_exchange(body, exchange, *, name, grid, in_specs, out_specs, out_shape, args, vmem_bytes=None):
    if exchange is None:
        return pl.pallas_call(body, name=name, grid=grid, in_specs=in_specs, out_specs=out_specs, out_shape=out_shape,
                              compiler_params=_params(len(grid), vmem_bytes))(*args)
    x_arrays, x_shapes, x_sems, start, wait = exchange[:5]
    n_in, n_xin, n_xout = len(args), len(x_arrays), len(x_shapes)
    aliases = {n_in + i: 1 + i for i in range(n_xin)} if len(exchange) > 5 and exchange[5] else {}

    def full_body(*refs):
        c_in, x_in = refs[:n_in], refs[n_in:n_in + n_xin]
        c_out = refs[n_in + n_xin]
        x_out = refs[n_in + n_xin + 1:n_in + n_xin + 1 + n_xout]
        sems = refs[n_in + n_xin + 1 + n_xout:]
        ids = [pl.program_id(d) for d in range(len(grid))]
        first, last = ids[0] == 0, ids[0] == grid[0] - 1
        for d in range(1, len(grid)):
            first = first & (ids[d] == 0)
            last = last & (ids[d] == grid[d] - 1)

        @pl.when(first)
        def _():
            start(x_in, x_out, sems)

        body(*c_in, c_out)

        @pl.when(last)
        def _():
            wait(x_in, x_out, sems)

    res = pl.pallas_call(
        full_body, name=name, grid=grid, in_specs=list(in_specs) + [_ANY] * n_xin,
        out_specs=[out_specs] + [_ANY] * n_xout, out_shape=[out_shape] + list(x_shapes),
        scratch_shapes=list(x_sems), input_output_aliases=aliases, compiler_params=_params(len(grid), vmem_bytes),
    )(*args, *x_arrays)
    return res[0], list(res[1:])


def _matmul_tn(a, b, *, name, tm=1024, tn=1024, tt=2048, exchange=None):
    t, m = a.shape
    _, n = b.shape
    tm = _pick(m, (tm, 1024, 512, 128))
    tn = _pick(n, (tn, 1024, 512, 128))
    tt = _pick(t, (tt, 2048, 1024, 512, 256))
    nt = t // tt

    def body(a_ref, b_ref, o_ref):
        k = pl.program_id(2)
        part = lax.dot_general(a_ref[...], b_ref[...], (((0,), (0,)), ((), ())), preferred_element_type=f32)

        @pl.when(k == 0)
        def _():
            o_ref[...] = part

        @pl.when(k > 0)
        def _():
            o_ref[...] += part

    return _call_with_exchange(
        body, exchange, name=name, grid=(m // tm, n // tn, nt),
        in_specs=[pl.BlockSpec((tt, tm), lambda i, j, k: (k, i)), pl.BlockSpec((tt, tn), lambda i, j, k: (k, j))],
        out_specs=pl.BlockSpec((tm, tn), lambda i, j, k: (i, j)),
        out_shape=jax.ShapeDtypeStruct((m, n), f32), args=[a, b],
        vmem_bytes=_vmem_for(2 * tt * tm, 2 * tt * tn, 4 * tm * tn, extra=4 * tm * tn + 2 * tt * tm))


def _rms_fwd(x, g, *, name, exchange=None):
    t = x.shape[0]
    tb = _pick(t, (ELEMENTWISE_BLOCK, 256, 128))

    def body(x_ref, g_ref, h_ref):
        xv = x_ref[...]
        r = lax.rsqrt(jnp.mean(xv * xv, axis=-1, keepdims=True) + EPS)
        h_ref[...] = (xv * r * g_ref[...]).astype(bf16)

    return _call_with_exchange(
        body, exchange, name=name, grid=(t // tb,), in_specs=[_row(tb, D), _fixed((1, D))], out_specs=_row(tb, D),
        out_shape=jax.ShapeDtypeStruct((t, D), bf16), args=[x, g])


def _rms_bwd(dh, x, g, dres, *, name, more=None, bf16_copy=True):
    t = x.shape[0]
    tb = _pick(t, (ELEMENTWISE_BLOCK, 256, 128))

    def body(dh_ref, x_ref, g_ref, dres_ref, *rest):
        dx_ref, dg_ref = rest[-3 if bf16_copy else -2], rest[-1]
        xv = x_ref[...]
        r = lax.rsqrt(jnp.mean(xv * xv, axis=-1, keepdims=True) + EPS)
        xh = xv * r
        dy = dh_ref[...]
        if more is not None:
            dy = dy + lax.dot_general(rest[0][...], rest[1][...], (((1,), (1,)), ((), ())), preferred_element_type=f32)
        dyg = dy * g_ref[...]
        dx = dres_ref[...] + r * (dyg - xh * jnp.mean(dyg * xh, axis=-1, keepdims=True))
        dx_ref[...] = dx
        if bf16_copy:
            rest[-2][...] = dx.astype(bf16)

        @pl.when(pl.program_id(0) == 0)
        def _():
            dg_ref[...] = jnp.zeros_like(dg_ref)

        dg_ref[...] += jnp.sum((dy * xh).reshape(tb // 8, 8, D), axis=0)

    in_specs, args = [_row(tb, D), _row(tb, D), _fixed((1, D)), _row(tb, D)], [dh, x, g, dres]
    if more is not None:
        in_specs += [_row(tb, 128), _fixed(more[1].shape)]
        args += list(more)
    dx_dtypes = (f32, bf16) if bf16_copy else (f32,)
    return pl.pallas_call(
        body, name=name, grid=(t // tb,), in_specs=in_specs,
        out_specs=[_row(tb, D) for _ in dx_dtypes] + [_fixed((8, D))],
        out_shape=[jax.ShapeDtypeStruct((t, D), dt) for dt in dx_dtypes] + [jax.ShapeDtypeStruct((8, D), f32)],
        compiler_params=_params(1),
    )(*args)


def _gdn_gates(ab, alog, dtb):
    lane = lax.broadcasted_iota(jnp.int32, ab.shape, 1)
    g = -jnp.exp(alog) * _softplus(ab + dtb)
    beta = _sigmoid(ab)
    return jnp.where(lane < H, g, jnp.where(lane < 2 * H, beta, 0.0))


def _pre_fwd(pg, pq, h1, w2, wa, wg, alog, dtb):
    t = pg.shape[0]
    tb = _pick(t, (ROW_BLOCK // 2, 128))

    def body(p0_ref, p0h_ref, pq_ref, pqh_ref, h1_ref, w2_ref, wa_ref, wg_ref, alog_ref, dtb_ref,
             ya_ref, qn_ref, kn_ref, vc_ref, gb_ref, p2_ref):
        first = pl.program_id(0) == 0
        p2_ref[...] = jnp.dot(h1_ref[...], w2_ref[...], preferred_element_type=f32)
        for i in range(D // 128):
            sl, cg, xv = _strip(i), _strip(i, D), _strip(i, 2 * D)
            taps = _strip_taps(_f32(p0_ref, cg) * _f32(p0_ref, xv), _halo_before(p0h_ref, cg) * _halo_before(p0h_ref, xv),
                               first, 3)
            ya_ref[:, sl] = (_f32(p0_ref, sl) * _strip_conv(wa_ref, sl, taps)).astype(bf16)
        for part, out_ref, scale in ((0, qn_ref, DH ** -0.5), (1, kn_ref, 1.0), (2, vc_ref, None)):
            for h in range(H):
                sl = _strip(h, part * D)
                s = _strip_conv(wg_ref, sl, _strip_taps(pq_ref[:, sl], pqh_ref[:, sl], first, 4))
                s = s * _sigmoid(s)
                if scale is not None:
                    s = s * (lax.rsqrt(jnp.sum(s * s, axis=-1, keepdims=True) + EPS) * scale)
                out_ref[:, _strip(h)] = s
        gb_ref[...] = _gdn_gates(p2_ref[...], alog_ref[...], dtb_ref[...])

    return pl.pallas_call(
        body, name="pre_fwd", grid=(t // tb,),
        in_specs=[_row(tb, 3 * D, 0), _prev(tb, 3 * D, 0, rows=16), _row(tb, 3 * D), _prev(tb, 3 * D), _row(tb, D),
                  _fixed((D, 128)), _fixed((8, D)), _fixed((8, 3 * D)), _fixed((1, 128)), _fixed((1, 128))],
        out_specs=[_row(tb, D), _row(tb, D), _row(tb, D), _row(tb, D), _row(tb, 128), _row(tb, 128)],
        out_shape=[jax.ShapeDtypeStruct((t, D), bf16), jax.ShapeDtypeStruct((t, D), f32),
                   jax.ShapeDtypeStruct((t, D), f32), jax.ShapeDtypeStruct((t, D), f32),
                   jax.ShapeDtypeStruct((t, 128), f32), jax.ShapeDtypeStruct((t, 128), f32)],
        compiler_params=_params(1),
    )(pg, pg, pq, pq, h1, w2, wa, wg, alog, dtb)


_Z_COL, _GA_COL, _GB_COL = 3, 4, 5


def _post_fwd(o, pg, gn):
    t = o.shape[0]
    tb = _pick(t, (ELEMENTWISE_BLOCK, 256, 128))

    def body(o_ref, z_ref, gn_ref, yb_ref):
        for h in range(H):
            sl = slice(h * DH, (h + 1) * DH)
            oh = o_ref[:, sl]
            z = _f32(z_ref, sl)
            r = lax.rsqrt(jnp.mean(oh * oh, axis=-1, keepdims=True) + EPS)
            yb_ref[:, sl] = (oh * r * gn_ref[...] * (z * _sigmoid(z))).astype(bf16)

    return pl.pallas_call(
        body, name="post_fwd", grid=(t // tb,), in_specs=[_row(tb, D), _row(tb, D, _Z_COL), _fixed((1, DH))],
        out_specs=_row(tb, D), out_shape=jax.ShapeDtypeStruct((t, D), bf16), compiler_params=_params(1),
    )(o, pg, gn)


def _post_bwd(dyb, o, pg, gn):
    t = o.shape[0]
    tb = _pick(t, (ELEMENTWISE_BLOCK, 256, 128))

    def body(dyb_ref, o_ref, z_ref, gn_ref, do_ref, dz_ref, dgn_ref):
        @pl.when(pl.program_id(0) == 0)
        def _():
            dgn_ref[...] = jnp.zeros_like(dgn_ref)

        gn_v = gn_ref[...]
        acc = jnp.zeros((8, DH), f32)
        for h in range(H):
            sl = slice(h * DH, (h + 1) * DH)
            oh = o_ref[:, sl]
            z = _f32(z_ref, sl)
            dy = dyb_ref[:, sl]
            r = lax.rsqrt(jnp.mean(oh * oh, axis=-1, keepdims=True) + EPS)
            on = oh * r
            sg = _sigmoid(z)
            sz = z * sg
            don = dy * sz
            dz_ref[:, sl] = (dy * on * gn_v * (sg * (1.0 + z * (1.0 - sg)))).astype(bf16)
            acc = acc + jnp.sum((don * on).reshape(tb // 8, 8, DH), axis=0)
            doh = don * gn_v
            do_ref[:, sl] = r * (doh - on * jnp.mean(doh * on, axis=-1, keepdims=True))
        dgn_ref[...] += acc

    return pl.pallas_call(
        body, name="post_bwd", grid=(t // tb,),
        in_specs=[_row(tb, D), _row(tb, D), _row(tb, D, _Z_COL), _fixed((1, DH))],
        out_specs=[_row(tb, D), _row(tb, D), _fixed((8, DH))],
        out_shape=[jax.ShapeDtypeStruct((t, D), f32), jax.ShapeDtypeStruct((t, D), bf16),
                   jax.ShapeDtypeStruct((8, DH), f32)],
        compiler_params=_params(1),
    )(dyb, o, pg, gn)


def _mix_fwd(ya, yb, pg):
    t = ya.shape[0]
    tb = _pick(t, (ELEMENTWISE_BLOCK, 256, 128))

    def body(ya_ref, yb_ref, ga_ref, gb_ref, mix_ref):
        ya_v, yb_v = ya_ref[...].astype(f32), yb_ref[...].astype(f32)
        mix = _sigmoid(ga_ref[...].astype(f32)) * ya_v + _sigmoid(gb_ref[...].astype(f32)) * yb_v
        mix_ref[...] = mix.astype(bf16)

    return pl.pallas_call(
        body, name="mix_fwd", grid=(t // tb,),
        in_specs=[_row(tb, D), _row(tb, D), _row(tb, D, _GA_COL), _row(tb, D, _GB_COL)],
        out_specs=_row(tb, D), out_shape=jax.ShapeDtypeStruct((t, D), bf16), compiler_params=_params(1),
    )(ya, yb, pg, pg)


def _mix_bwd(dmix, ya, yb, pg):
    t = ya.shape[0]
    tb = _pick(t, (ELEMENTWISE_BLOCK, 256, 128))

    def body(dm_ref, ya_ref, yb_ref, ga_ref, gb_ref, dya_ref, dyb_ref, dg_ref):
        dm = dm_ref[...].astype(f32)
        sa = _sigmoid(ga_ref[...].astype(f32))
        sb = _sigmoid(gb_ref[...].astype(f32))
        dya_ref[...] = (dm * sa).astype(bf16)
        dyb_ref[...] = (dm * sb).astype(bf16)
        dg_ref[:, :D] = (dm * ya_ref[...].astype(f32) * sa * (1.0 - sa)).astype(bf16)
        dg_ref[:, D:] = (dm * yb_ref[...].astype(f32) * sb * (1.0 - sb)).astype(bf16)

    return pl.pallas_call(
        body, name="mix_bwd", grid=(t // tb,),
        in_specs=[_row(tb, D), _row(tb, D), _row(tb, D), _row(tb, D, _GA_COL), _row(tb, D, _GB_COL)],
        out_specs=[_row(tb, D), _row(tb, D), _row(tb, 2 * D)],
        out_shape=[jax.ShapeDtypeStruct((t, D), bf16), jax.ShapeDtypeStruct((t, D), bf16),
                   jax.ShapeDtypeStruct((t, 2 * D), bf16)],
        compiler_params=_params(1),
    )(dmix, ya, yb, pg, pg)


def _ffn_fwd(up, wf):
    t = up.shape[0]
    tb = _pick(t, (ROW_BLOCK, 128))

    def body(up_ref, uph_ref, wf_ref, act_ref):
        first = pl.program_id(0) == 0
        for i in range(DFF // 128):
            g, v = _strip(i), _strip(i, DFF)
            gate = _strip_conv(wf_ref, g, _strip_taps(_f32(up_ref, g), _halo_before(uph_ref, g), first, 3))
            val = _strip_conv(wf_ref, v, _strip_taps(_f32(up_ref, v), _halo_before(uph_ref, v), first, 3))
            act_ref[:, g] = (gate * _sigmoid(gate) * val).astype(bf16)

    return pl.pallas_call(
        body, name="ffn_fwd", grid=(t // tb,),
        in_specs=[_row(tb, 2 * DFF), _prev(tb, 2 * DFF, rows=16), _fixed((8, 2 * DFF))],
        out_specs=_row(tb, DFF), out_shape=jax.ShapeDtypeStruct((t, DFF), bf16), compiler_params=_params(1),
    )(up, up, wf)


def _ffn_bwd1(dact, up, wf):
    t = up.shape[0]
    tb = _pick(t, (ROW_BLOCK, 128))

    def body(da_ref, up_ref, uph_ref, wf_ref, dc_ref, dw_ref):
        @pl.when(pl.program_id(0) == 0)
        def _():
            dw_ref[...] = jnp.zeros_like(dw_ref)

        first = pl.program_id(0) == 0
        for i in range(DFF // 128):
            g, v = _strip(i), _strip(i, DFF)
            g_taps = _strip_taps(_f32(up_ref, g), _halo_before(uph_ref, g), first, 3)
            v_taps = _strip_taps(_f32(up_ref, v), _halo_before(uph_ref, v), first, 3)
            gate = _strip_conv(wf_ref, g, g_taps)
            val = _strip_conv(wf_ref, v, v_taps)
            sg = _sigmoid(gate)
            da = _f32(da_ref, g)
            dgate = da * val * (sg * (1.0 + gate * (1.0 - sg)))
            dval = da * (gate * sg)
            dc_ref[:, g] = dgate.astype(bf16)
            dc_ref[:, v] = dval.astype(bf16)
            _strip_weight_grad(dw_ref, g, dgate, g_taps)
            _strip_weight_grad(dw_ref, v, dval, v_taps)

    return pl.pallas_call(
        body, name="ffn_bwd1", grid=(t // tb,),
        in_specs=[_row(tb, DFF), _row(tb, 2 * DFF), _prev(tb, 2 * DFF, rows=16), _fixed((8, 2 * DFF))],
        out_specs=[_row(tb, 2 * DFF), _fixed((8, 2 * DFF))],
        out_shape=[jax.ShapeDtypeStruct((t, 2 * DFF), bf16), jax.ShapeDtypeStruct((8, 2 * DFF), f32)],
        compiler_params=_params(1),
    )(dact, up, up, wf)


def _ffn_bwd2(dc, wf):
    t = dc.shape[0]
    tb = _pick(t, (ROW_BLOCK, 128))
    nb = t // tb

    def body(dc_ref, dch_ref, wf_ref, dup_ref):
        last = pl.program_id(0) == nb - 1
        for i in range(2 * DFF // 128):
            sl = _strip(i)
            dup_ref[:, sl] = _strip_conv_up(_f32(dc_ref, sl), _halo_after(dch_ref, sl), last, wf_ref, sl, 3).astype(bf16)

    return pl.pallas_call(
        body, name="ffn_bwd2", grid=(nb,),
        in_specs=[_row(tb, 2 * DFF), _next(tb, 2 * DFF, t, rows=16), _fixed((8, 2 * DFF))],
        out_specs=_row(tb, 2 * DFF), out_shape=jax.ShapeDtypeStruct((t, 2 * DFF), bf16), compiler_params=_params(1),
    )(dc, dc, wf)


def _final(x3, tgt, g):
    t = x3.shape[0]
    tb = _pick(t, (ELEMENTWISE_BLOCK, 256, 128))

    def body(x_ref, t_ref, g_ref, loss_ref, dx_ref, dxb_ref, dg_ref):
        @pl.when(pl.program_id(0) == 0)
        def _():
            loss_ref[...] = jnp.zeros_like(loss_ref)
            dg_ref[...] = jnp.zeros_like(dg_ref)

        xv = x_ref[...]
        r = lax.rsqrt(jnp.mean(xv * xv, axis=-1, keepdims=True) + EPS)
        xh = xv * r
        gv = g_ref[...]
        e = xh * gv - t_ref[...]
        lrow = 0.5 * jnp.mean(e * e, axis=-1, keepdims=True)
        loss_ref[...] += jnp.sum(jnp.broadcast_to(lrow, (tb, 128)).reshape(tb // 8, 8, 128), axis=0)
        dy = e * (1.0 / D)
        dyg = dy * gv
        dx = r * (dyg - xh * jnp.mean(dyg * xh, axis=-1, keepdims=True))
        dx_ref[...] = dx
        dxb_ref[...] = dx.astype(bf16)
        dg_ref[...] += jnp.sum((dy * xh).reshape(tb // 8, 8, D), axis=0)

    return pl.pallas_call(
        body, name="final", grid=(t // tb,), in_specs=[_row(tb, D), _row(tb, D), _fixed((1, D))],
        out_specs=[_fixed((8, 128)), _row(tb, D), _row(tb, D), _fixed((8, D))],
        out_shape=[jax.ShapeDtypeStruct((8, 128), f32), jax.ShapeDtypeStruct((t, D), f32),
                   jax.ShapeDtypeStruct((t, D), bf16), jax.ShapeDtypeStruct((8, D), f32)],
        compiler_params=_params(1),
    )(x3, tgt, g)


def _pre_bwd1(pg, pq, p2, dya_in, dqn, dkn, dvc, dgb, gbeta, h1, wa, wg, alog, dtb):
    t = pg.shape[0]
    tb = _pick(t, (ROW_BLOCK // 2, 128))

    def body(p0_ref, p0h_ref, pq_ref, pqh_ref, p2_ref, dya_ref, dqn_ref, dkn_ref, dvc_ref, dgb_ref, gb_ref, h1_ref,
             wa_ref, wg_ref, alog_ref, dtb_ref,
             dbg_ref, dca_ref, dc4_ref, dp2_ref, dwa_ref, dwg_ref, dal_ref, ddt_ref, dw2_ref):
        @pl.when(pl.program_id(0) == 0)
        def _():
            dwa_ref[...] = jnp.zeros_like(dwa_ref)
            dwg_ref[...] = jnp.zeros_like(dwg_ref)
            dal_ref[...] = jnp.zeros_like(dal_ref)
            ddt_ref[...] = jnp.zeros_like(ddt_ref)
            dw2_ref[...] = jnp.zeros_like(dw2_ref)

        first = pl.program_id(0) == 0

        for i in range(D // 128):
            sl, cg, xv = _strip(i), _strip(i, D), _strip(i, 2 * D)
            taps = _strip_taps(_f32(p0_ref, cg) * _f32(p0_ref, xv), _halo_before(p0h_ref, cg) * _halo_before(p0h_ref, xv),
                               first, 3)
            dya = _f32(dya_ref, sl)
            dbg_ref[:, sl] = (dya * _strip_conv(wa_ref, sl, taps)).astype(bf16)
            dca = dya * _f32(p0_ref, sl)
            dca_ref[:, sl] = dca.astype(bf16)
            _strip_weight_grad(dwa_ref, sl, dca, taps)

        for part, d_ref, scale in ((0, dqn_ref, DH ** -0.5), (1, dkn_ref, 1.0), (2, dvc_ref, None)):
            for h in range(H):
                sl = _strip(h, part * D)
                taps = _strip_taps(pq_ref[:, sl], pqh_ref[:, sl], first, 4)
                c4 = _strip_conv(wg_ref, sl, taps)
                sg = _sigmoid(c4)
                dn = d_ref[:, _strip(h)]
                if scale is not None:
                    a = c4 * sg
                    r = lax.rsqrt(jnp.sum(a * a, axis=-1, keepdims=True) + EPS)
                    an = a * r
                    dn = dn * scale
                    dn = r * (dn - an * jnp.sum(dn * an, axis=-1, keepdims=True))
                dc4 = dn * (sg * (1.0 + c4 * (1.0 - sg)))
                dc4_ref[:, sl] = dc4.astype(bf16)
                _strip_weight_grad(dwg_ref, sl, dc4, taps)

        ab = p2_ref[...]
        lane = lax.broadcasted_iota(jnp.int32, ab.shape, 1)
        dgbv = dgb_ref[...]
        gbv = gb_ref[...]
        da = dgbv * (-jnp.exp(alog_ref[...])) * _sigmoid(ab + dtb_ref[...])
        db = dgbv * gbv * (1.0 - gbv)
        dp2 = jnp.where(lane < H, da, jnp.where(lane < 2 * H, db, 0.0)).astype(bf16)
        dp2_ref[...] = dp2
        dw2_ref[...] += lax.dot_general(dp2, h1_ref[...], (((0,), (0,)), ((), ())), preferred_element_type=f32)
        dal = jnp.where(lane < H, dgbv * gbv, 0.0)
        ddt = jnp.where(lane < H, da, 0.0)
        dal_ref[...] += jnp.sum(dal.reshape(tb // 8, 8, 128), axis=0)
        ddt_ref[...] += jnp.sum(ddt.reshape(tb // 8, 8, 128), axis=0)

    return pl.pallas_call(
        body, name="pre_bwd1", grid=(t // tb,),
        in_specs=[_row(tb, 3 * D, 0), _prev(tb, 3 * D, 0, rows=16), _row(tb, 3 * D), _prev(tb, 3 * D), _row(tb, 128),
                  _row(tb, D), _row(tb, D), _row(tb, D), _row(tb, D), _row(tb, 128), _row(tb, 128), _row(tb, D),
                  _fixed((8, D)), _fixed((8, 3 * D)), _fixed((1, 128)), _fixed((1, 128))],
        out_specs=[_row(tb, D), _row(tb, D), _row(tb, 3 * D), _row(tb, 128),
                   _fixed((8, D)), _fixed((8, 3 * D)), _fixed((8, 128)), _fixed((8, 128)), _fixed((128, D))],
        out_shape=[jax.ShapeDtypeStruct((t, D), bf16), jax.ShapeDtypeStruct((t, D), bf16),
                   jax.ShapeDtypeStruct((t, 3 * D), bf16), jax.ShapeDtypeStruct((t, 128), bf16),
                   jax.ShapeDtypeStruct((8, D), f32), jax.ShapeDtypeStruct((8, 3 * D), f32),
                   jax.ShapeDtypeStruct((8, 128), f32), jax.ShapeDtypeStruct((8, 128), f32),
                   jax.ShapeDtypeStruct((128, D), f32)],
        compiler_params=_params(1),
    )(pg, pg, pq, pq, p2, dya_in, dqn, dkn, dvc, dgb, gbeta, h1, wa, wg, alog, dtb)


def _pre_bwd2(dca, dc4, pg, dbg, dz, dgates, wa, wg, exchange=None):
    t = pg.shape[0]
    tb = _pick(t, (ROW_BLOCK, 128))
    nb = t // tb

    def body(dca_ref, dcah_ref, dc4_ref, dc4h_ref, p0_ref, dbg_ref, dz_ref, dgt_ref, wa_ref, wg_ref, dp_ref):
        last = pl.program_id(0) == nb - 1
        dp_ref[:, :D] = dbg_ref[...]
        for i in range(D // 128):
            sl, cg, xv = _strip(i), _strip(i, D), _strip(i, 2 * D)
            du = _strip_conv_up(_f32(dca_ref, sl), _halo_after(dcah_ref, sl), last, wa_ref, sl, 3)
            dp_ref[:, cg] = (du * _f32(p0_ref, xv)).astype(bf16)
            dp_ref[:, xv] = (du * _f32(p0_ref, cg)).astype(bf16)
        dp_ref[:, 3 * D:4 * D] = dz_ref[...]
        dp_ref[:, 4 * D:6 * D] = dgt_ref[...]
        for i in range(3 * D // 128):
            sl = _strip(i)
            dq = _strip_conv_up(_f32(dc4_ref, sl), _halo_after(dc4h_ref, sl), last, wg_ref, sl, 4)
            dp_ref[:, _strip(i, 6 * D)] = dq.astype(bf16)

    return _call_with_exchange(
        body, exchange, name="pre_bwd2", grid=(nb,),
        in_specs=[_row(tb, D), _next(tb, D, t, rows=16), _row(tb, 3 * D), _next(tb, 3 * D, t, rows=16), _row(tb, 3 * D, 0),
                  _row(tb, D), _row(tb, D), _row(tb, 2 * D), _fixed((8, D)), _fixed((8, 3 * D))],
        out_specs=_row(tb, NW1), out_shape=jax.ShapeDtypeStruct((t, NW1), bf16),
        args=[dca, dca, dc4, dc4, pg, dbg, dz, dgates, wa, wg])


def _chunk_consts():
    r = lax.broadcasted_iota(jnp.int32, (CH, CH), 0)
    c = lax.broadcasted_iota(jnp.int32, (CH, CH), 1)
    return r, c, (r == c).astype(f32)


def _tri_inverse(lows, eye, r, c):
    def same_block(b):
        return jnp.bitwise_xor(r, c) < b

    xs = [jnp.where(same_block(8), -low, 0.0) for low in lows]
    ts = [eye + x for x in xs]
    for _ in range(2):
        xs = [_idot(x, x) for x in xs]
        ts = [t + _idot(t, x) for t, x in zip(ts, xs)]
    for b in (8, 16, 32):
        below = same_block(2 * b) & jnp.logical_not(same_block(b))
        ts = [t - _idot(_idot(t, jnp.where(below, low, 0.0)), t) for t, low in zip(ts, lows)]
    return ts


def _chunk_common(q, k, v, gcol, bcol, r, c, eye):
    grow = jnp.sum(eye * gcol, axis=0, keepdims=True)
    dec = jnp.exp(jnp.where(r >= c, gcol - grow, -jnp.inf))
    rcol = lax.broadcasted_iota(jnp.int32, (CH, 1), 0)
    glast = jnp.sum(jnp.where(rcol == CH - 1, gcol, 0.0), axis=0, keepdims=True)
    eg = jnp.exp(gcol)
    el = jnp.exp(glast - gcol)
    kb = k * bcol
    vb = v * bcol
    kk = _bdot_nt(kb, k)
    low = jnp.where(r > c, kk * dec, 0.0)
    qk = _bdot_nt(q, k)
    att = qk * dec
    return grow, dec, glast, eg, el, kb, vb, kk, low, qk, att, rcol


def _gdn_fwd(qn, kn, vc, gbeta):
    t = qn.shape[0]
    n_chunks = t // CH

    def body(q_ref, k_ref, v_ref, gb_ref, o_ref, s_ref, t_ref, state):
        @pl.when(pl.program_id(0) == 0)
        def _():
            state[...] = jnp.zeros_like(state)

        r, c, eye = _chunk_consts()
        tri = (r >= c).astype(f32)
        heads = range(H)
        keys = [(s, h) for s in range(GDN_STEP) for h in heads]
        rows = [slice(s * CH, (s + 1) * CH) for s in range(GDN_STEP)]
        gbs = [gb_ref[rows[s], :] for s in range(GDN_STEP)]
        galls = [_hdot(tri, gb) for gb in gbs]
        qs = {(s, h): q_ref[rows[s], h * DH:(h + 1) * DH] for s, h in keys}
        ks = {(s, h): k_ref[rows[s], h * DH:(h + 1) * DH] for s, h in keys}
        cm = {(s, h): _chunk_common(qs[s, h], ks[s, h], v_ref[rows[s], h * DH:(h + 1) * DH], galls[s][:, h:h + 1],
                                    gbs[s][:, H + h:H + h + 1], r, c, eye) for s, h in keys}
        invs = dict(zip(keys, _tri_inverse([cm[key][8] for key in keys], eye, r, c)))
        uws = {key: _bdot(invs[key], jnp.concatenate([cm[key][6], cm[key][5] * cm[key][3]], axis=1)) for key in keys}
        sts = [state[h] for h in heads]
        for s in range(GDN_STEP):
            vns = [uws[s, h][:, :DH] - _bdot(uws[s, h][:, DH:], sts[h]) for h in heads]
            outs = [_bdot(qs[s, h] * cm[s, h][3], sts[h]) + _bdot(cm[s, h][10], vns[h]) for h in heads]
            news = [sts[h] * jnp.exp(cm[s, h][2]) + _bdot_tn(ks[s, h] * cm[s, h][4], vns[h]) for h in heads]
            for h in heads:
                s_ref[s, h] = sts[h].astype(bf16)
                t_ref[s, h] = invs[s, h]
                o_ref[rows[s], h * DH:(h + 1) * DH] = outs[h]
            sts = news
        for h in heads:
            state[h] = sts[h]

    tb = GDN_STEP * CH
    return pl.pallas_call(
        body, name="gdn_fwd", grid=(t // tb,),
        in_specs=[_row(tb, D), _row(tb, D), _row(tb, D), _row(tb, 128)],
        out_specs=[_row(tb, D), pl.BlockSpec((GDN_STEP, H, DH, DH), lambda i: (i, 0, 0, 0)),
                   pl.BlockSpec((GDN_STEP, H, CH, CH), lambda i: (i, 0, 0, 0))],
        out_shape=[jax.ShapeDtypeStruct((t, D), f32), jax.ShapeDtypeStruct((n_chunks, H, DH, DH), bf16),
                   jax.ShapeDtypeStruct((n_chunks, H, CH, CH), f32)],
        scratch_shapes=[pltpu.VMEM((H, DH, DH), f32)],
        compiler_params=_params(1),
    )(qn, kn, vc, gbeta)


def _gdn_bwd(qn, kn, vc, gbeta, do, s_all, t_all):
    t = qn.shape[0]

    def body(q_ref, k_ref, v_ref, gb_ref, do_ref, s_ref, t_ref, dq_ref, dk_ref, dv_ref, dgb_ref, dstate):
        @pl.when(pl.program_id(0) == 0)
        def _():
            dstate[...] = jnp.zeros_like(dstate)

        r, c, eye = _chunk_consts()
        tril = r >= c
        lane = lax.broadcasted_iota(jnp.int32, (1, 128), 1)
        hs = range(H)

        def each(fn, *lists):
            return [fn(*args) for args in zip(*lists)]

        def rsum(a):
            return jnp.sum(a, axis=1, keepdims=True)

        def before_state(s):
            rows = slice(s * CH, (s + 1) * CH)
            gb = gb_ref[rows, :]
            gall = _hdot(tril.astype(f32), gb)
            p = {"rows": rows}
            p["q"] = q = [q_ref[rows, h * DH:(h + 1) * DH] for h in hs]
            p["k"] = k = [k_ref[rows, h * DH:(h + 1) * DH] for h in hs]
            p["v"] = v = [v_ref[rows, h * DH:(h + 1) * DH] for h in hs]
            p["dout"] = dout = [do_ref[rows, h * DH:(h + 1) * DH] for h in hs]
            p["inv"] = inv = [t_ref[s, h] for h in hs]
            p["st"] = st = [s_ref[s, h] for h in hs]
            p["bcol"] = bcol = [gb[:, H + h:H + h + 1] for h in hs]
            cm = [_chunk_common(q[h], k[h], v[h], gall[:, h:h + 1], bcol[h], r, c, eye) for h in hs]
            for name, i in (("dec", 1), ("glast", 2), ("eg", 3), ("el", 4), ("kb", 5), ("vb", 6), ("low", 8), ("att", 10)):
                p[name] = [m[i] for m in cm]
            p["rcol"] = cm[0][11]
            p["elast"] = each(jnp.exp, p["glast"])
            p["kbg"] = each(jnp.multiply, p["kb"], p["eg"])
            uw = each(lambda i, a, b: _bdot(i, jnp.concatenate([a, b], axis=1)), inv, p["vb"], p["kbg"])
            p["u"] = [a[:, :DH] for a in uw]
            p["w"] = [a[:, DH:] for a in uw]
            p["vn"] = each(lambda a, b, x: a - _bdot(b, x), p["u"], p["w"], st)
            p["qd"] = each(jnp.multiply, q, p["eg"])
            p["kd"] = each(jnp.multiply, k, p["el"])
            p["dqd"] = each(_bdot_nt, dout, st)
            p["datt"] = each(lambda d, x: jnp.where(tril, _bdot_nt(d, x), 0.0), dout, p["vn"])
            p["dqk"] = each(jnp.multiply, p["datt"], p["dec"])
            p["qd_do"] = each(_bdot_tn, p["qd"], dout)
            p["att_do"] = each(_bdot_tn, p["att"], dout)
            return p

        def after_state(p, ds):
            q, k, v, st, inv, bcol = p["q"], p["k"], p["v"], p["st"], p["inv"], p["bcol"]
            eg, el, kb, u, w = p["eg"], p["el"], p["kb"], p["u"], p["w"]
            dvn = each(lambda a, kk, x: a + _bdot(kk, x), p["att_do"], p["kd"], ds)
            dkd = each(_bdot_nt, p["vn"], ds)
            dw = each(lambda a, x: -_bdot_nt(a, x), dvn, st)
            new_ds = each(lambda x, e, a, ww, dv_: x * e + a - _bdot_tn(ww, dv_), ds, p["elast"], p["qd_do"], w, dvn)
            dglast = each(lambda e, x, d: e * jnp.sum(rsum(x.astype(f32) * d), axis=0, keepdims=True), p["elast"], st, ds)
            dr = each(lambda i, a, b: _bdot_tn(i, jnp.concatenate([a, b], axis=1)), inv, dvn, dw)
            dvb = [a[:, :DH] for a in dr]
            dkbg = [a[:, DH:] for a in dr]
            dlow = each(lambda a, b, x, y: -jnp.where(r > c, _bdot_nt(a, b) + _bdot_nt(x, y), 0.0), dvb, u, dkbg, w)
            dkk = each(jnp.multiply, dlow, p["dec"])
            mm = each(lambda a, b, x, y: a * b + x * y, dlow, p["low"], p["datt"], p["att"])
            dkb = each(lambda a, kk, b, e: _bdot(a, kk) + b * e, dkk, k, dkbg, eg)
            dk = each(lambda a, b, x, y, d, e, f, g: _bdot_tn(a, b) + _bdot_tn(x, y) + d * e + f * g,
                      dkk, kb, p["dqk"], q, dkd, el, dkb, bcol)
            dq = each(lambda a, kk, d, e: _bdot(a, kk) + d * e, p["dqk"], k, p["dqd"], eg)
            dv = each(jnp.multiply, dvb, bcol)
            dbeta = each(lambda a, b, x, y: rsum(a * b) + rsum(x * y), dkb, k, dvb, v)
            deg = each(lambda a, b, x, y: rsum(a * b) + rsum(x * y), dkbg, kb, p["dqd"], q)
            delc = each(lambda a, b, e: rsum(a * b) * e, dkd, k, el)
            dgc = each(lambda m, a, e, d: rsum(m) - rsum(eye * jnp.sum(m, axis=0, keepdims=True)) + a * e - d,
                       mm, deg, eg, delc)
            dgc = each(lambda g, d, l: g + jnp.where(p["rcol"] == CH - 1, jnp.sum(d, axis=0, keepdims=True) + l, 0.0),
                       dgc, delc, dglast)
            dg_acc = jnp.zeros((CH, 128), f32)
            db_acc = jnp.zeros((CH, 128), f32)
            rows = p["rows"]
            for h in hs:
                dq_ref[rows, h * DH:(h + 1) * DH] = dq[h]
                dk_ref[rows, h * DH:(h + 1) * DH] = dk[h]
                dv_ref[rows, h * DH:(h + 1) * DH] = dv[h]
                dg_acc = dg_acc + dgc[h] * (lane == h).astype(f32)
                db_acc = db_acc + dbeta[h] * (lane == H + h).astype(f32)
            dgb_ref[rows, :] = _hdot((r <= c).astype(f32), dg_acc) + db_acc
            return new_ds

        order = list(reversed(range(GDN_STEP)))
        pre = [before_state(s) for s in order]
        ds = [dstate[h] for h in hs]
        for p in pre:
            ds = after_state(p, ds)
        for h in hs:
            dstate[h] = ds[h]

    tb = GDN_STEP * CH
    n_steps = t // tb
    rev = lambda i: (n_steps - 1 - i, 0)
    rev4 = lambda i: (n_steps - 1 - i, 0, 0, 0)
    return pl.pallas_call(
        body, name="gdn_bwd", grid=(n_steps,),
        in_specs=[pl.BlockSpec((tb, D), rev), pl.BlockSpec((tb, D), rev), pl.BlockSpec((tb, D), rev),
                  pl.BlockSpec((tb, 128), rev), pl.BlockSpec((tb, D), rev),
                  pl.BlockSpec((GDN_STEP, H, DH, DH), rev4), pl.BlockSpec((GDN_STEP, H, CH, CH), rev4)],
        out_specs=[pl.BlockSpec((tb, D), rev), pl.BlockSpec((tb, D), rev), pl.BlockSpec((tb, D), rev),
                   pl.BlockSpec((tb, 128), rev)],
        out_shape=[jax.ShapeDtypeStruct((t, D), f32)] * 3 + [jax.ShapeDtypeStruct((t, 128), f32)],
        scratch_shapes=[pltpu.VMEM((H, DH, DH), f32)],
        compiler_params=_params(1),
    )(qn, kn, vc, gbeta, do, s_all, t_all)


def _pad_rows(w, rows=8):
    return jnp.pad(w, ((0, rows - w.shape[0]), (0, 0)))


_REST = ("w_up", "w_a_out", "w_b_out", "w_o", "w_down")


def _local_step(x, tgt, w, comm=None):
    g1 = w["norm_mix_g"].reshape(1, D)
    if comm is None:
        h1 = _rms_fwd(x, g1, name="rms1_fwd")
    else:
        h1, gathered = _rms_fwd(x, g1, name="rms1_fwd", exchange=comm.gather_first())
        w = {**w, **comm.finish_first(gathered)}
    w1, w2 = w["w1"], w["w2"]
    wa = _pad_rows(w["conv_a_w"])
    wg = _pad_rows(w["gdn_conv_w"])
    wf = _pad_rows(w["ffn_conv_w"])
    alog = jnp.pad(w["gdn_A_log"].reshape(1, H), ((0, 0), (0, 128 - H)))
    dtb = jnp.pad(w["gdn_dt_bias"].reshape(1, H), ((0, 0), (0, 128 - H)))
    g2 = w["norm_ffn_g"].reshape(1, D)
    g3 = w["norm_final_g"].reshape(1, D)
    gn = w["gdn_norm_g"].reshape(1, DH)

    if comm is None:
        pg = _matmul(h1, w1, name="mm_in", cols=(0, 6 * D), out_dtype=bf16)
        pq = _matmul(h1, w1, name="mm_in_qkv", cols=(6 * D, 3 * D))
    else:
        pg, gathered = _matmul(h1, w1, name="mm_in", cols=(0, 6 * D), out_dtype=bf16, exchange=comm.gather_rest())
        pq, gathered = _matmul(h1, w1, name="mm_in_qkv", cols=(6 * D, 3 * D), exchange=_gather_forward_exchange(gathered))
        w = {**w, **comm.finish_gather(gathered)}
    ya_in, qn, kn, vc, gbeta, p2 = _pre_fwd(pg, pq, h1, w2, wa, wg, alog, dtb)
    o, s_all, t_all = _gdn_fwd(qn, kn, vc, gbeta)
    yb_in = _post_fwd(o, pg, gn)
    ya = _matmul(ya_in, w["w_a_out"], name="mm_a", out_dtype=bf16)
    yb = _matmul(yb_in, w["w_b_out"], name="mm_b", out_dtype=bf16)
    mix = _mix_fwd(ya, yb, pg)
    x2 = _matmul(mix, w["w_o"], name="mm_o", add=x, tm=1024)
    h2 = _rms_fwd(x2, g2, name="rms2_fwd")
    up = _matmul(h2, w["w_up"], nt=True, name="mm_up", tn=DFF // 2, out_dtype=bf16)
    act = _ffn_fwd(up, wf)
    x3 = _matmul(act, w["w_down"], name="mm_down", add=x2, tm=512)
    loss_p, dx3, dx3b, dg3 = _final(x3, tgt, g3)

    grads = {"norm_final_g": dg3}
    dact = _matmul(dx3b, w["w_down"], nt=True, name="mm_down_dx", tm=512, tn=DFF, out_dtype=bf16)
    grads["w_down"] = _matmul_tn(act, dx3b, name="mm_down_dw", tm=DFF // 2)
    dc, dwf = _ffn_bwd1(dact, up, wf)
    grads["ffn_conv_w"] = dwf
    dup = _ffn_bwd2(dc, wf)
    dh2 = _matmul(dup, w["w_up"], name="mm_up_dx", tm=1024, tk=DFF)
    grads["w_up"] = _matmul_tn(dup, h2, name="mm_up_dw", tm=DFF // 2)
    dx2, dx2b, dg2 = _rms_bwd(dh2, x2, g2, dx3, name="rms2_bwd")
    grads["norm_ffn_g"] = dg2
    dmix = _matmul(dx2b, w["w_o"], nt=True, name="mm_o_dx", out_dtype=bf16)
    grads["w_o"] = _matmul_tn(mix, dx2b, name="mm_o_dw")
    dya, dyb, dgates = _mix_bwd(dmix, ya, yb, pg)
    dya_in = _matmul(dya, w["w_a_out"], nt=True, name="mm_a_dx", out_dtype=bf16)
    grads["w_a_out"] = _matmul_tn(ya_in, dya, name="mm_a_dw")
    dyb_in = _matmul(dyb, w["w_b_out"], nt=True, name="mm_b_dx")
    grads["w_b_out"] = _matmul_tn(yb_in, dyb, name="mm_b_dw")
    do, dz, dgn = _post_bwd(dyb_in, o, pg, gn)
    grads["gdn_norm_g"] = dgn
    dqn, dkn, dvc, dgb = _gdn_bwd(qn, kn, vc, gbeta, do, s_all, t_all)
    dbg, dca, dc4, dp2, dwa, dwg, dal, ddt, grads["w2"] = _pre_bwd1(pg, pq, p2, dya_in, dqn, dkn, dvc, dgb, gbeta, h1,
                                                                    wa, wg, alog, dtb)
    grads["conv_a_w"] = dwa
    grads["gdn_conv_w"] = dwg
    grads["gdn_A_log"] = dal
    grads["gdn_dt_bias"] = ddt
    if comm is None:
        dp1 = _pre_bwd2(dca, dc4, pg, dbg, dz, dgates, wa, wg)
        grads["w1"] = _matmul_tn(dp1, h1, name="mm_in_dw", tt=4096)
        dh1 = _matmul(dp1, w1, nt=True, name="mm_in_dx", tm=512, tk=NW1 // 2)
    else:
        exchange, blocks = comm.reduce_halves(_REST, grads)
        dp1, recv = _pre_bwd2(dca, dc4, pg, dbg, dz, dgates, wa, wg, exchange=exchange)
        exchange, sums = comm.reduce_sums(_REST, blocks, recv)
        grads["w1"], recv = _matmul_tn(dp1, h1, name="mm_in_dw", tt=4096, exchange=exchange)
        comm.finish_reduce(_REST, sums, recv)
        exchange, blocks = comm.reduce_halves(("w_in",), grads)
        exchange, sums = comm.reduce_sums(("w_in",), blocks, _run_exchange(exchange, name="rs_sibling_w_in"))
        dh1, recv = _matmul(dp1, w1, nt=True, name="mm_in_dx", tm=512, tk=NW1 // 2, exchange=exchange)
        comm.finish_reduce(("w_in",), sums, recv)
    dx, dg1 = _rms_bwd(dh1, x, g1, dx2, name="rms1_bwd", more=(dp2, w2), bf16_copy=False)
    grads["norm_mix_g"] = dg1
    return loss_p, dx, grads


_ANY = pl.BlockSpec(memory_space=pl.ANY)


def _remote(src, dst, send_sem, recv_sem, to):
    return pltpu.make_async_remote_copy(src_ref=src, dst_ref=dst, send_sem=send_sem, recv_sem=recv_sem,
                                        device_id=to, device_id_type=MESH)


def _run_exchange(exchange, *, name):
    arrays, shapes, sems, start, wait = exchange
    n_in, n_out = len(arrays), len(shapes)

    def body(*refs):
        start(refs[:n_in], refs[n_in:n_in + n_out], refs[n_in + n_out:])
        wait(refs[:n_in], refs[n_in:n_in + n_out], refs[n_in + n_out:])

    return pl.pallas_call(body, name=name, out_shape=list(shapes), in_specs=[_ANY] * n_in, out_specs=[_ANY] * n_out,
                          scratch_shapes=list(sems))(*arrays)


def _gather_exchange(shards):
    n = len(shards)

    def copies(x_refs, out_refs, sems):
        send_sems, recv_sems, local_sems = sems
        x, y, c = lax.axis_index("x"), lax.axis_index("y"), lax.axis_index("c")

        def flip(v, b):
            return v + b - 2 * v * b

        me, sibling = (x, y, c), (x, y, 1 - c)
        chip1, chip2, diag = (flip(x, 1 - c), flip(y, c)), (flip(x, c), flip(y, 1 - c)), (1 - x, 1 - y)

        def copy(a, k, blk, to, from_input=False):
            dst = out_refs[a].at[4 * blk[0] + 2 * blk[1] + blk[2]]
            return _remote(x_refs[a] if from_input else dst, dst, send_sems.at[a, k], recv_sems.at[a, k], to)

        mine = [pltpu.make_async_copy(x_refs[a], out_refs[a].at[4 * x + 2 * y + c], local_sems.at[a]) for a in range(n)]
        first = []
        for a in range(n):
            first += [copy(a, 0, me, sibling, from_input=True), copy(a, 1, me, (*chip1, c), from_input=True),
                      copy(a, 2, me, (*chip2, c), from_input=True)]
        return copy, mine, first, me, sibling, chip1, chip2, diag, c

    def start(x_refs, out_refs, sems):
        _, mine, first, *_ = copies(x_refs, out_refs, sems)
        for cp in mine + first:
            cp.start()

    def wait(x_refs, out_refs, sems):
        copy, mine, first, me, sibling, chip1, chip2, diag, c = copies(x_refs, out_refs, sems)
        passed = []

        def pass_on(cp):
            passed.append(cp)
            cp.start()

        for a in range(n):
            copy(a, 1, (*chip1, c), me).wait_recv()
            pass_on(copy(a, 3, (*chip1, c), (*chip2, c)))
            pass_on(copy(a, 4, (*chip1, c), sibling))
        for a in range(n):
            copy(a, 2, (*chip2, c), me).wait_recv()
            pass_on(copy(a, 5, (*chip2, c), sibling))
        for a in range(n):
            copy(a, 3, (*diag, c), me).wait_recv()
            pass_on(copy(a, 6, (*diag, c), sibling))
        for a in range(n):
            copy(a, 0, sibling, me).wait_recv()
            copy(a, 4, (*chip2, 1 - c), me).wait_recv()
            copy(a, 5, (*chip1, 1 - c), me).wait_recv()
            copy(a, 6, (*diag, 1 - c), me).wait_recv()
        for cp in first + passed:
            cp.wait_send()
        for cp in mine:
            cp.wait()

    shapes = [jax.ShapeDtypeStruct((N_DEV, *s.shape), s.dtype) for s in shards]
    sems = [pltpu.SemaphoreType.DMA((n, 7)), pltpu.SemaphoreType.DMA((n, 7)), pltpu.SemaphoreType.DMA((n,))]
    return shards, shapes, sems, start, wait


def _gather_direct_exchange(shards):
    n = len(shards)

    def copies(x_refs, out_refs, sems):
        send_sems, recv_sems, local_sems = sems
        x, y, c = lax.axis_index("x"), lax.axis_index("y"), lax.axis_index("c")
        targets = [(x, y, 1 - c), (1 - x, y, c), (x, 1 - y, c), (1 - x, 1 - y, c)]
        local, sends, recvs = [], [], []
        for a in range(n):
            mine = out_refs[a].at[4 * x + 2 * y + c]
            local.append(pltpu.make_async_copy(x_refs[a], mine, local_sems.at[a]))
            for k, to in enumerate(targets):
                theirs = out_refs[a].at[4 * to[0] + 2 * to[1] + to[2]]
                sends.append(_remote(x_refs[a], mine, send_sems.at[a, k], recv_sems.at[a, k], to))
                recvs.append(_remote(theirs, theirs, send_sems.at[a, k], recv_sems.at[a, k], to))
        return local, sends, recvs

    def start(x_refs, out_refs, sems):
        local, sends, _ = copies(x_refs, out_refs, sems)
        for cp in local + sends:
            cp.start()

    def wait(x_refs, out_refs, sems):
        local, sends, recvs = copies(x_refs, out_refs, sems)
        for cp in recvs:
            cp.wait_recv()
        for cp in sends:
            cp.wait_send()
        for cp in local:
            cp.wait()

    shapes = [jax.ShapeDtypeStruct((N_DEV, *s.shape), s.dtype) for s in shards]
    sems = [pltpu.SemaphoreType.DMA((n, 4)), pltpu.SemaphoreType.DMA((n, 4)), pltpu.SemaphoreType.DMA((n,))]
    return shards, shapes, sems, start, wait


def _gather_forward_exchange(gathered):
    n = len(gathered)

    def copies(_, out_refs, sems):
        send_sems, recv_sems = sems
        x, y, c = lax.axis_index("x"), lax.axis_index("y"), lax.axis_index("c")
        sibling = (x, y, 1 - c)
        sends, recvs = [], []
        for a in range(n):
            for j, (px, py) in enumerate([(1 - x, y), (x, 1 - y), (1 - x, 1 - y)]):
                mine = out_refs[a].at[4 * px + 2 * py + c]
                theirs = out_refs[a].at[4 * px + 2 * py + 1 - c]
                sends.append(_remote(mine, mine, send_sems.at[a, j], recv_sems.at[a, j], sibling))
                recvs.append(_remote(theirs, theirs, send_sems.at[a, j], recv_sems.at[a, j], sibling))
        return sends, recvs

    def start(in_refs, out_refs, sems):
        for cp in copies(in_refs, out_refs, sems)[0]:
            cp.start()

    def wait(in_refs, out_refs, sems):
        sends, recvs = copies(in_refs, out_refs, sems)
        for cp in recvs:
            cp.wait_recv()
        for cp in sends:
            cp.wait_send()

    shapes = [jax.ShapeDtypeStruct(g.shape, g.dtype) for g in gathered]
    sems = [pltpu.SemaphoreType.DMA((n, 3)), pltpu.SemaphoreType.DMA((n, 3))]
    return gathered, shapes, sems, start, wait, True


def _chips_exchange(hsums):
    n = len(hsums)

    def copies(h_refs, out_refs, sems):
        send_sems, recv_sems = sems
        x, y, c = lax.axis_index("x"), lax.axis_index("y"), lax.axis_index("c")
        chips = [(1 - x, y), (x, 1 - y), (1 - x, 1 - y)]
        return [_remote(h_refs[a].at[2 * px + py], out_refs[a].at[k], send_sems.at[a, k], recv_sems.at[a, k], (px, py, c))
                for a in range(n) for k, (px, py) in enumerate(chips)]

    def start(h_refs, out_refs, sems):
        for cp in copies(h_refs, out_refs, sems):
            cp.start()

    def wait(h_refs, out_refs, sems):
        for cp in copies(h_refs, out_refs, sems):
            cp.wait()

    shapes = [jax.ShapeDtypeStruct((3, *h.shape[1:]), h.dtype) for h in hsums]
    sems = [pltpu.SemaphoreType.DMA((n, 3)), pltpu.SemaphoreType.DMA((n, 3))]
    return hsums, shapes, sems, start, wait


def _sibling_exchange(halves):
    n = len(halves)

    def copies(p_refs, out_refs, sems):
        send_sems, recv_sems = sems
        x, y, c = lax.axis_index("x"), lax.axis_index("y"), lax.axis_index("c")
        return [_remote(p_refs[a], out_refs[a], send_sems.at[a], recv_sems.at[a], (x, y, 1 - c)) for a in range(n)]

    def start(p_refs, out_refs, sems):
        for cp in copies(p_refs, out_refs, sems):
            cp.start()

    def wait(p_refs, out_refs, sems):
        for cp in copies(p_refs, out_refs, sems):
            cp.wait()

    shapes = [jax.ShapeDtypeStruct(h.shape, h.dtype) for h in halves]
    return halves, shapes, [pltpu.SemaphoreType.DMA((n,)), pltpu.SemaphoreType.DMA((n,))], start, wait


_IN_RANGES = ((0, 3 * D, 0, 0), (3 * D, 6 * D, 0, 6 * D), (6 * D, 7 * D, 0, 3 * D), (7 * D, 7 * D + 16, 1, 0),
              (7 * D + 16, 9 * D + 16, 0, 4 * D))


def _col_pieces(width, ranges):
    pieces = []
    for d in range(N_DEV):
        lo, hi = d * width, (d + 1) * width
        for glo, ghi, mat, mlo in ranges:
            a, b = max(lo, glo), min(hi, ghi)
            if a < b:
                pieces.append((d, a - lo, b - lo, mat, mlo + a - glo))
    return pieces


def _cols_to_matrices(g, ranges, out_widths, *, name):
    _, rows, width = g.shape
    tb = 128
    pieces = _col_pieces(width, ranges)
    covered = [sum(p[2] - p[1] for p in pieces if p[3] == m) for m in range(len(out_widths))]

    def body(g_ref, *o_refs):
        for m, o_ref in enumerate(o_refs):
            if covered[m] < out_widths[m]:
                o_ref[...] = jnp.zeros_like(o_ref)
        for d, b0, b1, m, m0 in pieces:
            o_refs[m][:, m0:m0 + b1 - b0] = g_ref[d, :, b0:b1]

    return pl.pallas_call(
        body, name=name, grid=(rows // tb,), in_specs=[pl.BlockSpec((N_DEV, tb, width), lambda i: (0, i, 0))],
        out_specs=[pl.BlockSpec((tb, wo), lambda i: (i, 0)) for wo in out_widths],
        out_shape=[jax.ShapeDtypeStruct((rows, wo), g.dtype) for wo in out_widths], compiler_params=_params(1),
    )(g)


def _transposed_matrices_to_blocks(mats, ranges, width, *, name):
    rows = mats[0].shape[1]
    pieces = _col_pieces(width, ranges)

    def body(*refs):
        m_refs, g_ref = refs[:-1], refs[-1]
        for d, b0, b1, m, m0 in pieces:
            g_ref[d, b0:b1, :] = m_refs[m][m0:m0 + b1 - b0, :]

    return pl.pallas_call(
        body, name=name, grid=(rows // 128,),
        in_specs=[pl.BlockSpec((mt.shape[0], 128), lambda i: (0, i)) for mt in mats],
        out_specs=pl.BlockSpec((N_DEV, width, 128), lambda i: (0, 0, i)),
        out_shape=jax.ShapeDtypeStruct((N_DEV, width, rows), mats[0].dtype), compiler_params=_params(1),
    )(*mats)


def _row_block(rows):
    return 128 if rows % 128 == 0 else rows


def _half_bf16(g4, c_other, *, name):
    _, _, rows, width = g4.shape
    tb = _row_block(rows)

    def body(c_ref, p_ref, o_ref):
        o_ref[0] = p_ref[0, 0].astype(bf16)

    grid_spec = pltpu.PrefetchScalarGridSpec(
        num_scalar_prefetch=1, grid=(4, rows // tb),
        in_specs=[pl.BlockSpec((1, 1, tb, width), lambda j, i, c_ref: (j, c_ref[0], i, 0))],
        out_specs=pl.BlockSpec((1, tb, width), lambda j, i, c_ref: (j, i, 0)))
    return pl.pallas_call(
        body, name=name, grid_spec=grid_spec, out_shape=jax.ShapeDtypeStruct((4, rows, width), bf16),
        compiler_params=_params(2, _vmem_for(4 * tb * width, 2 * tb * width)),
    )(c_other, g4)


def _pair_sum(g4, recv, c_me, *, name):
    _, _, rows, width = g4.shape
    tb = _row_block(rows)

    def body(c_ref, p_ref, r_ref, o_ref, ob_ref):
        s = p_ref[0, 0] + r_ref[0].astype(f32)
        o_ref[0] = s
        ob_ref[0] = s.astype(bf16)

    blk = pl.BlockSpec((1, tb, width), lambda j, i, c_ref: (j, i, 0))
    grid_spec = pltpu.PrefetchScalarGridSpec(
        num_scalar_prefetch=1, grid=(4, rows // tb),
        in_specs=[pl.BlockSpec((1, 1, tb, width), lambda j, i, c_ref: (j, c_ref[0], i, 0)), blk],
        out_specs=[blk, blk])
    return pl.pallas_call(
        body, name=name, grid_spec=grid_spec,
        out_shape=[jax.ShapeDtypeStruct((4, rows, width), f32), jax.ShapeDtypeStruct((4, rows, width), bf16)],
        compiler_params=_params(2, _vmem_for(4 * tb * width, 2 * tb * width, 4 * tb * width, 2 * tb * width)),
    )(c_me, g4, recv)


def _adam_shard(hsum, recv, chip, w, m, v, *, name):
    _, rows, width = w.shape
    tb = _row_block(rows)

    def body(j_ref, h_ref, r_ref, w_ref, m_ref, v_ref, g_out, d_out, m_out, v_out):
        g = ((h_ref[0] + r_ref[0].astype(f32)) + r_ref[1].astype(f32)) + r_ref[2].astype(f32)
        delta, mn, vn = _adam_math(w_ref[0], g, m_ref[0], v_ref[0])
        g_out[0] = g
        d_out[0] = delta
        m_out[0] = mn
        v_out[0] = vn

    blk = pl.BlockSpec((1, tb, width), lambda i, j_ref: (0, i, 0))
    grid_spec = pltpu.PrefetchScalarGridSpec(
        num_scalar_prefetch=1, grid=(rows // tb,),
        in_specs=[pl.BlockSpec((1, tb, width), lambda i, j_ref: (j_ref[0], i, 0)),
                  pl.BlockSpec((3, tb, width), lambda i, j_ref: (0, i, 0)), blk, blk, blk],
        out_specs=[blk, blk, blk, blk])
    return pl.pallas_call(
        body, name=name, grid_spec=grid_spec, out_shape=[jax.ShapeDtypeStruct(w.shape, f32)] * 4,
        compiler_params=_params(1, _vmem_for(*[4 * tb * width] * 8, 6 * tb * width)),
    )(chip, hsum, recv, w, m, v)


def _sum_shard(hsum, recv, chip, *, name):
    _, rows, width = hsum.shape
    tb = _row_block(rows)

    def body(j_ref, h_ref, r_ref, g_out):
        g_out[...] = ((h_ref[0] + r_ref[0].astype(f32)) + r_ref[1].astype(f32)) + r_ref[2].astype(f32)

    grid_spec = pltpu.PrefetchScalarGridSpec(
        num_scalar_prefetch=1, grid=(rows // tb,),
        in_specs=[pl.BlockSpec((1, tb, width), lambda i, j_ref: (j_ref[0], i, 0)),
                  pl.BlockSpec((3, tb, width), lambda i, j_ref: (0, i, 0))],
        out_specs=pl.BlockSpec((tb, width), lambda i, j_ref: (i, 0)))
    return pl.pallas_call(body, name=name, grid_spec=grid_spec, out_shape=jax.ShapeDtypeStruct((rows, width), f32),
                          compiler_params=_params(1, _vmem_for(*[4 * tb * width] * 2, 6 * tb * width)))(chip, hsum, recv)


def _adam_columns(g, w, m, v, *, name):
    cols, _, rows = w.shape
    tb = cols // 2

    def body(g_ref, w_ref, m_ref, v_ref, d_out, m_out, v_out):
        delta, mn, vn = _adam_math(w_ref[...], g_ref[...], m_ref[...], v_ref[...])
        d_out[...] = delta
        m_out[...] = mn
        v_out[...] = vn

    blk = pl.BlockSpec((tb, 1, rows), lambda i: (i, 0, 0))
    return pl.pallas_call(
        body, name=name, grid=(cols // tb,), in_specs=[blk] * 4, out_specs=[blk] * 3,
        out_shape=[jax.ShapeDtypeStruct(w.shape, f32)] * 3,
        compiler_params=_params(1, _vmem_for(*[4 * tb * rows] * 7)),
    )(g, w, m, v)


R_SMALL = 8 + 8 * N_DEV
_SMALL_LANES = {"gdn_norm_g": (0, DH), "gdn_A_log": (DH, DH + H), "gdn_dt_bias": (2 * DH, 2 * DH + H)}
_LOSS_LANE = 3 * DH


def _pack_small(dg1, dg2, dg3, dgn, dal, ddt, loss_p, dwa, dwg, dwf):
    def body(dg1_ref, dg2_ref, dg3_ref, dgn_ref, dal_ref, ddt_ref, loss_ref, dwa_ref, dwg_ref, dwf_ref, o_ref):
        def total(ref):
            return jnp.sum(ref[...], axis=0, keepdims=True)

        o_ref[...] = jnp.zeros_like(o_ref)
        o_ref[0:1, :] = total(dg1_ref)
        o_ref[1:2, :] = total(dg2_ref)
        o_ref[2:3, :] = total(dg3_ref)
        o_ref[3:4, 0:DH] = total(dgn_ref)
        o_ref[3:4, DH:2 * DH] = total(dal_ref)
        o_ref[3:4, 2 * DH:3 * DH] = total(ddt_ref)
        o_ref[3:4, 3 * DH:4 * DH] = total(loss_ref)
        for d in range(N_DEV):
            base = 8 + 8 * d
            o_ref[base:base + 3, 0:128] = dwa_ref[0:3, 128 * d:128 * (d + 1)]
            o_ref[base:base + 4, 128:512] = dwg_ref[0:4, 384 * d:384 * (d + 1)]
            o_ref[base + 4:base + 7, 0:704] = dwf_ref[0:3, 704 * d:704 * (d + 1)]

    return pl.pallas_call(body, name="pack_small", out_shape=jax.ShapeDtypeStruct((R_SMALL, D), f32))(
        dg1, dg2, dg3, dgn, dal, ddt, loss_p, dwa, dwg, dwf)


_SMALL = ("norm_mix_g", "norm_ffn_g", "norm_final_g", "gdn_norm_g", "gdn_A_log", "gdn_dt_bias",
          "conv_a_w", "gdn_conv_w", "ffn_conv_w")


def _adam_small(gath, me, w, m, v):
    arrays = [t[n] for n in _SMALL for t in (w, m, v)]

    def body(me_ref, ga_ref, gb_ref, *refs):
        ins, outs = refs[:len(arrays)], refs[len(arrays):]
        ga, gb = ga_ref[0], gb_ref[0]
        for s in range(1, N_DEV):
            ga = ga + ga_ref[s]
            gb = gb + gb_ref[s]
        grads = {"norm_mix_g": ga[0:1, :], "norm_ffn_g": ga[1:2, :], "norm_final_g": ga[2:3, :],
                 "conv_a_w": gb[0:3, 0:128], "gdn_conv_w": gb[0:4, 128:512], "ffn_conv_w": gb[4:7, 0:704]}
        for n, (lo, hi) in _SMALL_LANES.items():
            grads[n] = ga[3:4, lo:hi]
        for i, n in enumerate(_SMALL):
            three_d = len(w[n].shape) == 3
            wv, mv, vv = (r[0] if three_d else r[...] for r in ins[3 * i:3 * i + 3])
            delta, mn, vn = _adam_math(wv, grads[n], mv, vv)
            for o_ref, val in zip(outs[4 * i:4 * i + 4], (grads[n], delta, mn, vn)):
                if three_d:
                    o_ref[0] = val
                else:
                    o_ref[...] = val
        outs[-1][...] = ga[3:4, _LOSS_LANE:_LOSS_LANE + 1]

    def whole(shape):
        return pl.BlockSpec(shape, lambda i, me_ref: (0,) * len(shape))

    grid_spec = pltpu.PrefetchScalarGridSpec(
        num_scalar_prefetch=1, grid=(1,),
        in_specs=[pl.BlockSpec((N_DEV, 8, D), lambda i, me_ref: (0, 0, 0)),
                  pl.BlockSpec((N_DEV, 8, D), lambda i, me_ref: (0, 1 + me_ref[0], 0))] + [whole(a.shape) for a in arrays],
        out_specs=[whole(w[n].shape) for n in _SMALL for _ in range(4)] + [whole((1, 1))])
    res = pl.pallas_call(
        body, name="adam_small", grid_spec=grid_spec,
        out_shape=[jax.ShapeDtypeStruct(w[n].shape, f32) for n in _SMALL for _ in range(4)]
        + [jax.ShapeDtypeStruct((1, 1), f32)],
        compiler_params=_params(1),
    )(me, gath, gath, *arrays)
    return {n: tuple(res[4 * i:4 * i + 4]) for i, n in enumerate(_SMALL)}, res[-1]


def _adam_math(w, g, m, v):
    m = ADAM_B1 * m + (1.0 - ADAM_B1) * g
    v = ADAM_B2 * v + (1.0 - ADAM_B2) * jnp.square(g)
    m_hat = m / (1.0 - ADAM_B1 ** ADAM_STEP)
    v_hat = v / (1.0 - ADAM_B2 ** ADAM_STEP)
    delta = -ADAM_LR * (m_hat / (jnp.sqrt(v_hat) + ADAM_EPS) + ADAM_WD * w)
    return delta, m, v


_WEIGHTS = ("norm_mix_g", "w_in", "conv_a_w", "gdn_conv_w", "gdn_A_log", "gdn_dt_bias", "gdn_norm_g", "w_a_out",
            "w_b_out", "w_o", "norm_ffn_g", "w_up", "ffn_conv_w", "w_down", "norm_final_g")
_CONVS = ("conv_a_w", "gdn_conv_w", "ffn_conv_w")


class _StepExchanges:
    def __init__(self, wts, mom, var, c_me, chip):
        self.wts, self.mom, self.var, self.c_me, self.chip = wts, mom, var, c_me, chip
        self.results = {}

    def gather_first(self):
        return _gather_exchange([self.wts["w_in"][0].astype(bf16)] + [self.wts[n][0] for n in _CONVS])

    def finish_first(self, gathered):
        g_in, gc_a, gc_g, gc_f = gathered
        w1, w2 = _cols_to_matrices(g_in, _IN_RANGES, (NW1, 128), name="relay_w_in")
        return {"w1": w1, "w2": w2, "conv_a_w": gc_a.transpose(1, 0, 2).reshape(3, D),
                "gdn_conv_w": gc_g.transpose(1, 0, 2).reshape(4, 3 * D),
                "ffn_conv_w": gc_f.transpose(1, 0, 2).reshape(3, 2 * DFF)}

    def gather_rest(self):
        return _gather_direct_exchange([self.wts[n][0].astype(bf16) for n in _REST])

    def finish_gather(self, gathered):
        g_up, g_a, g_b, g_o, g_down = gathered
        return {"w_up": g_up.reshape(2 * DFF, D), "w_a_out": g_a.reshape(D, D), "w_b_out": g_b.reshape(D, D),
                "w_o": g_o.reshape(D, D), "w_down": g_down.reshape(DFF, D)}

    def reduce_halves(self, names, grads):
        blocks = []
        for n in names:
            if n == "w_in":
                g = _transposed_matrices_to_blocks([grads["w1"], grads["w2"]], _IN_RANGES, R_IN, name="relay_dw_in")
                blocks.append(g.reshape(4, 2, R_IN, D))
            else:
                blocks.append(grads[n].reshape(4, 2, *self.wts[n].shape[1:]))
        return _sibling_exchange([_half_bf16(g, 1 - self.c_me, name="rs_half_" + n) for n, g in zip(names, blocks)]), blocks

    def reduce_sums(self, names, blocks, recv):
        sums = [_pair_sum(g, r, self.c_me, name="rs_sum_" + n) for n, g, r in zip(names, blocks, recv)]
        return _chips_exchange([s[1] for s in sums]), [s[0] for s in sums]

    def finish_reduce(self, names, sums, recv):
        for n, s, r in zip(names, sums, recv):
            if n == "w_in":
                g = _sum_shard(s, r, self.chip, name="rs_total_w_in")[:, None, :]
                w, m, v = (jnp.transpose(t[n], (2, 0, 1)) for t in (self.wts, self.mom, self.var))
                res = (g, *_adam_columns(g, w, m, v, name="adam_w_in"))
                self.results[n] = tuple(jnp.transpose(a, (1, 2, 0)) for a in res)
            else:
                self.results[n] = _adam_shard(s, r, self.chip, self.wts[n], self.mom[n], self.var[n], name="adam_" + n)


def kernel(x, norm_mix_g, w_in, conv_a_w, gdn_conv_w, gdn_A_log, gdn_dt_bias, gdn_norm_g, w_a_out, w_b_out, w_o, norm_ffn_g, w_up, ffn_conv_w, w_down, norm_final_g, loss_target, m_norm_mix_g, m_w_in, m_conv_a_w, m_gdn_conv_w, m_gdn_A_log, m_gdn_dt_bias, m_gdn_norm_g, m_w_a_out, m_w_b_out, m_w_o, m_norm_ffn_g, m_w_up, m_ffn_conv_w, m_w_down, m_norm_final_g, v_norm_mix_g, v_w_in, v_conv_a_w, v_gdn_conv_w, v_gdn_A_log, v_gdn_dt_bias, v_gdn_norm_g, v_w_a_out, v_w_b_out, v_w_o, v_norm_ffn_g, v_w_up, v_ffn_conv_w, v_w_down, v_norm_final_g):
    wts = dict(zip(_WEIGHTS, (norm_mix_g, w_in, conv_a_w, gdn_conv_w, gdn_A_log, gdn_dt_bias, gdn_norm_g, w_a_out,
                              w_b_out, w_o, norm_ffn_g, w_up, ffn_conv_w, w_down, norm_final_g)))
    mom = dict(zip(_WEIGHTS, (m_norm_mix_g, m_w_in, m_conv_a_w, m_gdn_conv_w, m_gdn_A_log, m_gdn_dt_bias,
                              m_gdn_norm_g, m_w_a_out, m_w_b_out, m_w_o, m_norm_ffn_g, m_w_up, m_ffn_conv_w,
                              m_w_down, m_norm_final_g)))
    var = dict(zip(_WEIGHTS, (v_norm_mix_g, v_w_in, v_conv_a_w, v_gdn_conv_w, v_gdn_A_log, v_gdn_dt_bias,
                              v_gdn_norm_g, v_w_a_out, v_w_b_out, v_w_o, v_norm_ffn_g, v_w_up, v_ffn_conv_w,
                              v_w_down, v_norm_final_g)))
    cx, cy, cc = lax.axis_index("x"), lax.axis_index("y"), lax.axis_index("c")
    c_me = jnp.reshape(cc, (1,)).astype(jnp.int32)
    chip = jnp.reshape(2 * cx + cy, (1,)).astype(jnp.int32)
    me = jnp.reshape(4 * cx + 2 * cy + cc, (1,)).astype(jnp.int32)

    def with_up_transposed(t):
        return {**t, "w_up": jnp.swapaxes(t["w_up"], 1, 2)}

    comm = _StepExchanges(with_up_transposed(wts), with_up_transposed(mom), with_up_transposed(var), c_me, chip)
    replicated = {n: wts[n] for n in ("norm_mix_g", "norm_ffn_g", "norm_final_g", "gdn_norm_g", "gdn_A_log", "gdn_dt_bias")}
    loss_p, dx, grads = _local_step(x[0], loss_target[0], replicated, comm)
    res = comm.results
    res["w_up"] = tuple(jnp.swapaxes(a, 1, 2) for a in res["w_up"])

    small = _pack_small(grads["norm_mix_g"], grads["norm_ffn_g"], grads["norm_final_g"], grads["gdn_norm_g"],
                        grads["gdn_A_log"], grads["gdn_dt_bias"], loss_p, grads["conv_a_w"], grads["gdn_conv_w"],
                        grads["ffn_conv_w"])
    (small_all,) = _run_exchange(_gather_exchange([small]), name="ag_small")

    def raw(t):
        return {n: t[n].reshape(1, D) if n == "norm_final_g" else t[n] for n in _SMALL}

    res_small, loss = _adam_small(small_all, me, raw(wts), raw(mom), raw(var))
    for n in _SMALL:
        res[n] = tuple(a.reshape(wts[n].shape) for a in res_small[n])
    outs = [[res[n][i] for n in _WEIGHTS] for i in range(4)]
    return (loss.reshape(()), dx[None], *outs[0], *outs[1], *outs[2], *outs[3])
```

```python
import jax
import jax.numpy as jnp
from jax import lax
from jax.experimental import pallas as pl
from jax.experimental.pallas import tpu as pltpu

f32 = jnp.float32
bf16 = jnp.bfloat16

D = 1024
H = 8
DH = 128
CH = 64
GDN_STEP = 2
ROW_BLOCK = 512
ELEMENTWISE_BLOCK = 1024
RELAYOUT_BLOCK = 256
DFF = 2816
NW1 = 9216
EPS = 1e-6
N_DEV = 8

ADAM_LR = 0.001
ADAM_B1 = 0.9
ADAM_B2 = 0.999
ADAM_EPS = 1e-08
ADAM_WD = 0.01
ADAM_STEP = 10

VMEM_LIMIT_BYTES = 48 * 1024 * 1024
VMEM_MAX_BYTES = 56 * 1024 * 1024

R_IN, R_UP = 1154, 704

_HI = lax.Precision.HIGHEST
MESH = pl.DeviceIdType.MESH


def _params(n_grid, vmem_bytes=None):
    return pltpu.CompilerParams(dimension_semantics=("arbitrary",) * n_grid,
                                vmem_limit_bytes=VMEM_LIMIT_BYTES if vmem_bytes is None else vmem_bytes)


def _vmem_for(*block_bytes, extra=0):
    need = 2 * sum(block_bytes) + extra + 4 * 1024 * 1024
    return min(max(need, VMEM_LIMIT_BYTES), VMEM_MAX_BYTES)


def _bdot(a, b):
    return jnp.dot(a.astype(bf16), b.astype(bf16), preferred_element_type=f32)


def _bdot_nt(a, b):
    return lax.dot_general(a.astype(bf16), b.astype(bf16), (((1,), (1,)), ((), ())), preferred_element_type=f32)


def _bdot_tn(a, b):
    return lax.dot_general(a.astype(bf16), b.astype(bf16), (((0,), (0,)), ((), ())), preferred_element_type=f32)


def _hdot(a, b):
    return jnp.dot(a, b, preferred_element_type=f32, precision=_HI)


def _idot(a, b):
    return jnp.dot(a, b, preferred_element_type=f32, precision=lax.Precision.HIGH)


def _sigmoid(x):
    return 1.0 / (1.0 + jnp.exp(-x))


def _softplus(x):
    return jnp.maximum(x, 0.0) + jnp.log(1.0 + jnp.exp(-jnp.abs(x)))


def _shift_down(x, halo, j):
    if j == 0:
        return x
    xr = pltpu.roll(x, j, 0)
    hr = pltpu.roll(halo, j, 0)
    r8 = lax.broadcasted_iota(jnp.int32, hr.shape, 0)
    top = jnp.where(r8 < j, hr, xr[:8])
    return jnp.concatenate([top, xr[8:]], axis=0)


def _shift_up(x, halo, j):
    if j == 0:
        return x
    n = x.shape[0]
    xr = pltpu.roll(x, n - j, 0)
    hr = pltpu.roll(halo, 8 - j, 0)
    r8 = lax.broadcasted_iota(jnp.int32, hr.shape, 0)
    bot = jnp.where(r8 >= 8 - j, hr, xr[n - 8:])
    return jnp.concatenate([xr[:n - 8], bot], axis=0)


def _taps_down(x, halo, k):
    return [_shift_down(x, halo, k - 1 - j) for j in range(k)]


def _strip(i, base=0):
    return slice(base + i * 128, base + (i + 1) * 128)


def _strip_taps(x, halo, first, k):
    return _taps_down(x, jnp.where(first, 0.0, halo), k)


def _strip_conv(w_ref, sl, taps):
    out = w_ref[0:1, sl] * taps[0]
    for j in range(1, len(taps)):
        out = out + w_ref[j:j + 1, sl] * taps[j]
    return out


def _strip_weight_grad(dw_ref, sl, dy, taps):
    for j, tap in enumerate(taps):
        dw_ref[j:j + 1, sl] += jnp.sum(dy * tap, axis=0, keepdims=True)


def _strip_conv_up(dy, halo, last, w_ref, sl, k):
    halo = jnp.where(last, 0.0, halo)
    out = w_ref[k - 1:k, sl] * dy
    for j in range(k - 1):
        out = out + w_ref[j:j + 1, sl] * _shift_up(dy, halo, k - 1 - j)
    return out


def _row(tb, w, col=0):
    return pl.BlockSpec((tb, w), lambda i: (i, col))


def _prev(tb, w, col=0, rows=8):
    return pl.BlockSpec((rows, w), lambda i: (jnp.maximum(i * (tb // rows) - 1, 0), col))


def _next(tb, w, n_rows, col=0, rows=8):
    last = n_rows // rows - 1
    return pl.BlockSpec((rows, w), lambda i: (jnp.minimum((i + 1) * (tb // rows), last), col))


def _f32(ref, sl):
    return ref[:, sl].astype(f32)


def _halo_before(ref, sl):
    h = _f32(ref, sl)
    return h[h.shape[0] - 8:]


def _halo_after(ref, sl):
    return _f32(ref, sl)[:8]


def _fixed(shape):
    return pl.BlockSpec(shape, lambda i: (0,) * len(shape))


def _pick(n, prefs):
    for p in prefs:
        if n % p == 0:
            return p
    return n


def _matmul(a, b, *, name, nt=False, add=None, tm=2048, tn=1024, tk=None, out_dtype=f32, cols=None, exchange=None):
    m, kd = a.shape
    col0, n = cols if cols is not None else (0, b.shape[0] if nt else b.shape[1])
    tm = _pick(m, (tm, 1024, 512, 256))
    tn = _pick(n, (tn, 1024, 512, 128))
    tk = kd if tk is None else tk
    nk = kd // tk
    assert nk == 1 or out_dtype == f32
    assert col0 % tn == 0 and not (nt and cols)
    j0 = col0 // tn
    dims = (((1,), (1,)), ((), ())) if nt else (((1,), (0,)), ((), ()))

    def body(a_ref, b_ref, *rest):
        o_ref = rest[-1]
        part = lax.dot_general(a_ref[...], b_ref[...], dims, preferred_element_type=f32)
        if nk == 1:
            o_ref[...] = (part if add is None else part + rest[0][...]).astype(out_dtype)
            return
        k = pl.program_id(2)

        @pl.when(k == 0)
        def _():
            o_ref[...] = part if add is None else part + rest[0][...]

        @pl.when(k > 0)
        def _():
            o_ref[...] += part

    b_spec = pl.BlockSpec((tn, tk), lambda i, j, k: (j, k)) if nt else pl.BlockSpec((tk, tn), lambda i, j, k: (k, j + j0))
    in_specs = [pl.BlockSpec((tm, tk), lambda i, j, k: (i, k)), b_spec]
    args = [a, b]
    if add is not None:
        in_specs.append(pl.BlockSpec((tm, tn), lambda i, j, k: (i, j)))
        args.append(add)
    vmem = _vmem_for(2 * tm * tk, 2 * tk * tn, tm * tn * jnp.dtype(out_dtype).itemsize,
                     4 * tm * tn if add is not None else 0, extra=4 * tm * tn)
    return _call_with_exchange(
        body, exchange, name=name, grid=(m // tm, n // tn, nk), in_specs=in_specs,
        out_specs=pl.BlockSpec((tm, tn), lambda i, j, k: (i, j)),
        out_shape=jax.ShapeDtypeStruct((m, n), out_dtype), args=args, vmem_bytes=vmem)


def _call_with_exchange(body, exchange, *, name, grid, in_specs, out_specs, out_shape, args, vmem_bytes=None):
    if exchange is None:
        return pl.pallas_call(body, name=name, grid=grid, in_specs=in_specs, out_specs=out_specs, out_shape=out_shape,
                              compiler_params=_params(len(grid), vmem_bytes))(*args)
    x_arrays, x_shapes, x_sems, start, wait = exchange[:5]
    n_in, n_xin, n_xout = len(args), len(x_arrays), len(x_shapes)
    aliases = {n_in + i: 1 + i for i in range(n_xin)} if len(exchange) > 5 and exchange[5] else {}

    def full_body(*refs):
        c_in, x_in = refs[:n_in], refs[n_in:n_in + n_xin]
        c_out = refs[n_in + n_xin]
        x_out = refs[n_in + n_xin + 1:n_in + n_xin + 1 + n_xout]
        sems = refs[n_in + n_xin + 1 + n_xout:]
        ids = [pl.program_id(d) for d in range(len(grid))]
        first, last = ids[0] == 0, ids[0] == grid[0] - 1
        for d in range(1, len(grid)):
            first = first & (ids[d] == 0)
            last = last & (ids[d] == grid[d] - 1)

        @pl.when(first)
        def _():
            start(x_in, x_out, sems)

        body(*c_in, c_out)

        @pl.when(last)
        def _():
            wait(x_in, x_out, sems)

    res = pl.pallas_call(
        full_body, name=name, grid=grid, in_specs=list(in_specs) + [_ANY] * n_xin,
        out_specs=[out_specs] + [_ANY] * n_xout, out_shape=[out_shape] + list(x_shapes),
        scratch_shapes=list(x_sems), input_output_aliases=aliases, compiler_params=_params(len(grid), vmem_bytes),
    )(*args, *x_arrays)
    return res[0], list(res[1:])


def _matmul_tn(a, b, *, name, tm=1024, tn=1024, tt=2048, exchange=None):
    t, m = a.shape
    _, n = b.shape
    tm = _pick(m, (tm, 1024, 512, 128))
    tn = _pick(n, (tn, 1024, 512, 128))
    tt = _pick(t, (tt, 2048, 1024, 512, 256))
    nt = t // tt

    def body(a_ref, b_ref, o_ref):
        k = pl.program_id(2)
        part = lax.dot_general(a_ref[...], b_ref[...], (((0,), (0,)), ((), ())), preferred_element_type=f32)

        @pl.when(k == 0)
        def _():
            o_ref[...] = part

        @pl.when(k > 0)
        def _():
            o_ref[...] += part

    return _call_with_exchange(
        body, exchange, name=name, grid=(m // tm, n // tn, nt),
        in_specs=[pl.BlockSpec((tt, tm), lambda i, j, k: (k, i)), pl.BlockSpec((tt, tn), lambda i, j, k: (k, j))],
        out_specs=pl.BlockSpec((tm, tn), lambda i, j, k: (i, j)),
        out_shape=jax.ShapeDtypeStruct((m, n), f32), args=[a, b],
        vmem_bytes=_vmem_for(2 * tt * tm, 2 * tt * tn, 4 * tm * tn, extra=4 * tm * tn + 2 * tt * tm))


def _rms_fwd(x, g, *, name, exchange=None):
    t = x.shape[0]
    tb = _pick(t, (ELEMENTWISE_BLOCK, 256, 128))

    def body(x_ref, g_ref, h_ref):
        xv = x_ref[...]
        r = lax.rsqrt(jnp.mean(xv * xv, axis=-1, keepdims=True) + EPS)
        h_ref[...] = (xv * r * g_ref[...]).astype(bf16)

    return _call_with_exchange(
        body, exchange, name=name, grid=(t // tb,), in_specs=[_row(tb, D), _fixed((1, D))], out_specs=_row(tb, D),
        out_shape=jax.ShapeDtypeStruct((t, D), bf16), args=[x, g])


def _rms_bwd(dh, x, g, dres, *, name, more=None, bf16_copy=True):
    t = x.shape[0]
    tb = _pick(t, (ELEMENTWISE_BLOCK, 256, 128))

    def body(dh_ref, x_ref, g_ref, dres_ref, *rest):
        dx_ref, dg_ref = rest[-3 if bf16_copy else -2], rest[-1]
        xv = x_ref[...]
        r = lax.rsqrt(jnp.mean(xv * xv, axis=-1, keepdims=True) + EPS)
        xh = xv * r
        dy = dh_ref[...]
        if more is not None:
            dy = dy + lax.dot_general(rest[0][...], rest[1][...], (((1,), (1,)), ((), ())), preferred_element_type=f32)
        dyg = dy * g_ref[...]
        dx = dres_ref[...] + r * (dyg - xh * jnp.mean(dyg * xh, axis=-1, keepdims=True))
        dx_ref[...] = dx
        if bf16_copy:
            rest[-2][...] = dx.astype(bf16)

        @pl.when(pl.program_id(0) == 0)
        def _():
            dg_ref[...] = jnp.zeros_like(dg_ref)

        dg_ref[...] += jnp.sum((dy * xh).reshape(tb // 8, 8, D), axis=0)

    in_specs, args = [_row(tb, D), _row(tb, D), _fixed((1, D)), _row(tb, D)], [dh, x, g, dres]
    if more is not None:
        in_specs += [_row(tb, 128), _fixed(more[1].shape)]
        args += list(more)
    dx_dtypes = (f32, bf16) if bf16_copy else (f32,)
    return pl.pallas_call(
        body, name=name, grid=(t // tb,), in_specs=in_specs,
        out_specs=[_row(tb, D) for _ in dx_dtypes] + [_fixed((8, D))],
        out_shape=[jax.ShapeDtypeStruct((t, D), dt) for dt in dx_dtypes] + [jax.ShapeDtypeStruct((8, D), f32)],
        compiler_params=_params(1),
    )(*args)


def _gdn_gates(ab, alog, dtb):
    lane = lax.broadcasted_iota(jnp.int32, ab.shape, 1)
    g = -jnp.exp(alog) * _softplus(ab + dtb)
    beta = _sigmoid(ab)
    return jnp.where(lane < H, g, jnp.where(lane < 2 * H, beta, 0.0))


def _pre_fwd(pg, pq, h1, w2, wa, wg, alog, dtb):
    t = pg.shape[0]
    tb = _pick(t, (ROW_BLOCK // 2, 128))

    def body(p0_ref, p0h_ref, pq_ref, pqh_ref, h1_ref, w2_ref, wa_ref, wg_ref, alog_ref, dtb_ref,
             ya_ref, qn_ref, kn_ref, vc_ref, gb_ref, p2_ref):
        first = pl.program_id(0) == 0
        p2_ref[...] = jnp.dot(h1_ref[...], w2_ref[...], preferred_element_type=f32)
        for i in range(D // 128):
            sl, cg, xv = _strip(i), _strip(i, D), _strip(i, 2 * D)
            taps = _strip_taps(_f32(p0_ref, cg) * _f32(p0_ref, xv), _halo_before(p0h_ref, cg) * _halo_before(p0h_ref, xv),
                               first, 3)
            ya_ref[:, sl] = (_f32(p0_ref, sl) * _strip_conv(wa_ref, sl, taps)).astype(bf16)
        for part, out_ref, scale in ((0, qn_ref, DH ** -0.5), (1, kn_ref, 1.0), (2, vc_ref, None)):
            for h in range(H):
                sl = _strip(h, part * D)
                s = _strip_conv(wg_ref, sl, _strip_taps(pq_ref[:, sl], pqh_ref[:, sl], first, 4))
                s = s * _sigmoid(s)
                if scale is not None:
                    s = s * (lax.rsqrt(jnp.sum(s * s, axis=-1, keepdims=True) + EPS) * scale)
                out_ref[:, _strip(h)] = s
        gb_ref[...] = _gdn_gates(p2_ref[...], alog_ref[...], dtb_ref[...])

    return pl.pallas_call(
        body, name="pre_fwd", grid=(t // tb,),
        in_specs=[_row(tb, 3 * D, 0), _prev(tb, 3 * D, 0, rows=16), _row(tb, 3 * D), _prev(tb, 3 * D), _row(tb, D),
                  _fixed((D, 128)), _fixed((8, D)), _fixed((8, 3 * D)), _fixed((1, 128)), _fixed((1, 128))],
        out_specs=[_row(tb, D), _row(tb, D), _row(tb, D), _row(tb, D), _row(tb, 128), _row(tb, 128)],
        out_shape=[jax.ShapeDtypeStruct((t, D), bf16), jax.ShapeDtypeStruct((t, D), f32),
                   jax.ShapeDtypeStruct((t, D), f32), jax.ShapeDtypeStruct((t, D), f32),
                   jax.ShapeDtypeStruct((t, 128), f32), jax.ShapeDtypeStruct((t, 128), f32)],
        compiler_params=_params(1),
    )(pg, pg, pq, pq, h1, w2, wa, wg, alog, dtb)


_Z_COL, _GA_COL, _GB_COL = 3, 4, 5


def _post_fwd(o, pg, gn):
    t = o.shape[0]
    tb = _pick(t, (ELEMENTWISE_BLOCK, 256, 128))

    def body(o_ref, z_ref, gn_ref, yb_ref):
        for h in range(H):
            sl = slice(h * DH, (h + 1) * DH)
            oh = o_ref[:, sl]
            z = _f32(z_ref, sl)
            r = lax.rsqrt(jnp.mean(oh * oh, axis=-1, keepdims=True) + EPS)
            yb_ref[:, sl] = (oh * r * gn_ref[...] * (z * _sigmoid(z))).astype(bf16)

    return pl.pallas_call(
        body, name="post_fwd", grid=(t // tb,), in_specs=[_row(tb, D), _row(tb, D, _Z_COL), _fixed((1, DH))],
        out_specs=_row(tb, D), out_shape=jax.ShapeDtypeStruct((t, D), bf16), compiler_params=_params(1),
    )(o, pg, gn)


def _post_bwd(dyb, o, pg, gn):
    t = o.shape[0]
    tb = _pick(t, (ELEMENTWISE_BLOCK, 256, 128))

    def body(dyb_ref, o_ref, z_ref, gn_ref, do_ref, dz_ref, dgn_ref):
        @pl.when(pl.program_id(0) == 0)
        def _():
            dgn_ref[...] = jnp.zeros_like(dgn_ref)

        gn_v = gn_ref[...]
        acc = jnp.zeros((8, DH), f32)
        for h in range(H):
            sl = slice(h * DH, (h + 1) * DH)
            oh = o_ref[:, sl]
            z = _f32(z_ref, sl)
            dy = dyb_ref[:, sl]
            r = lax.rsqrt(jnp.mean(oh * oh, axis=-1, keepdims=True) + EPS)
            on = oh * r
            sg = _sigmoid(z)
            sz = z * sg
            don = dy * sz
            dz_ref[:, sl] = (dy * on * gn_v * (sg * (1.0 + z * (1.0 - sg)))).astype(bf16)
            acc = acc + jnp.sum((don * on).reshape(tb // 8, 8, DH), axis=0)
            doh = don * gn_v
            do_ref[:, sl] = r * (doh - on * jnp.mean(doh * on, axis=-1, keepdims=True))
        dgn_ref[...] += acc

    return pl.pallas_call(
        body, name="post_bwd", grid=(t // tb,),
        in_specs=[_row(tb, D), _row(tb, D), _row(tb, D, _Z_COL), _fixed((1, DH))],
        out_specs=[_row(tb, D), _row(tb, D), _fixed((8, DH))],
        out_shape=[jax.ShapeDtypeStruct((t, D), f32), jax.ShapeDtypeStruct((t, D), bf16),
                   jax.ShapeDtypeStruct((8, DH), f32)],
        compiler_params=_params(1),
    )(dyb, o, pg, gn)


def _mix_fwd(ya, yb, pg):
    t = ya.shape[0]
    tb = _pick(t, (ELEMENTWISE_BLOCK, 256, 128))

    def body(ya_ref, yb_ref, ga_ref, gb_ref, mix_ref):
        ya_v, yb_v = ya_ref[...].astype(f32), yb_ref[...].astype(f32)
        mix = _sigmoid(ga_ref[...].astype(f32)) * ya_v + _sigmoid(gb_ref[...].astype(f32)) * yb_v
        mix_ref[...] = mix.astype(bf16)

    return pl.pallas_call(
        body, name="mix_fwd", grid=(t // tb,),
        in_specs=[_row(tb, D), _row(tb, D), _row(tb, D, _GA_COL), _row(tb, D, _GB_COL)],
        out_specs=_row(tb, D), out_shape=jax.ShapeDtypeStruct((t, D), bf16), compiler_params=_params(1),
    )(ya, yb, pg, pg)


def _mix_bwd(dmix, ya, yb, pg):
    t = ya.shape[0]
    tb = _pick(t, (ELEMENTWISE_BLOCK, 256, 128))

    def body(dm_ref, ya_ref, yb_ref, ga_ref, gb_ref, dya_ref, dyb_ref, dg_ref):
        dm = dm_ref[...].astype(f32)
        sa = _sigmoid(ga_ref[...].astype(f32))
        sb = _sigmoid(gb_ref[...].astype(f32))
        dya_ref[...] = (dm * sa).astype(bf16)
        dyb_ref[...] = (dm * sb).astype(bf16)
        dg_ref[:, :D] = (dm * ya_ref[...].astype(f32) * sa * (1.0 - sa)).astype(bf16)
        dg_ref[:, D:] = (dm * yb_ref[...].astype(f32) * sb * (1.0 - sb)).astype(bf16)

    return pl.pallas_call(
        body, name="mix_bwd", grid=(t // tb,),
        in_specs=[_row(tb, D), _row(tb, D), _row(tb, D), _row(tb, D, _GA_COL), _row(tb, D, _GB_COL)],
        out_specs=[_row(tb, D), _row(tb, D), _row(tb, 2 * D)],
        out_shape=[jax.ShapeDtypeStruct((t, D), bf16), jax.ShapeDtypeStruct((t, D), bf16),
                   jax.ShapeDtypeStruct((t, 2 * D), bf16)],
        compiler_params=_params(1),
    )(dmix, ya, yb, pg, pg)


def _ffn_fwd(up, wf):
    t = up.shape[0]
    tb = _pick(t, (ROW_BLOCK, 128))

    def body(up_ref, uph_ref, wf_ref, act_ref):
        first = pl.program_id(0) == 0
        for i in range(DFF // 128):
            g, v = _strip(i), _strip(i, DFF)
            gate = _strip_conv(wf_ref, g, _strip_taps(_f32(up_ref, g), _halo_before(uph_ref, g), first, 3))
            val = _strip_conv(wf_ref, v, _strip_taps(_f32(up_ref, v), _halo_before(uph_ref, v), first, 3))
            act_ref[:, g] = (gate * _sigmoid(gate) * val).astype(bf16)

    return pl.pallas_call(
        body, name="ffn_fwd", grid=(t // tb,),
        in_specs=[_row(tb, 2 * DFF), _prev(tb, 2 * DFF, rows=16), _fixed((8, 2 * DFF))],
        out_specs=_row(tb, DFF), out_shape=jax.ShapeDtypeStruct((t, DFF), bf16), compiler_params=_params(1),
    )(up, up, wf)


def _ffn_bwd1(dact, up, wf):
    t = up.shape[0]
    tb = _pick(t, (ROW_BLOCK, 128))

    def body(da_ref, up_ref, uph_ref, wf_ref, dc_ref, dw_ref):
        @pl.when(pl.program_id(0) == 0)
        def _():
            dw_ref[...] = jnp.zeros_like(dw_ref)

        first = pl.program_id(0) == 0
        for i in range(DFF // 128):
            g, v = _strip(i), _strip(i, DFF)
            g_taps = _strip_taps(_f32(up_ref, g), _halo_before(uph_ref, g), first, 3)
            v_taps = _strip_taps(_f32(up_ref, v), _halo_before(uph_ref, v), first, 3)
            gate = _strip_conv(wf_ref, g, g_taps)
            val = _strip_conv(wf_ref, v, v_taps)
            sg = _sigmoid(gate)
            da = _f32(da_ref, g)
            dgate = da * val * (sg * (1.0 + gate * (1.0 - sg)))
            dval = da * (gate * sg)
            dc_ref[:, g] = dgate.astype(bf16)
            dc_ref[:, v] = dval.astype(bf16)
            _strip_weight_grad(dw_ref, g, dgate, g_taps)
            _strip_weight_grad(dw_ref, v, dval, v_taps)

    return pl.pallas_call(
        body, name="ffn_bwd1", grid=(t // tb,),
        in_specs=[_row(tb, DFF), _row(tb, 2 * DFF), _prev(tb, 2 * DFF, rows=16), _fixed((8, 2 * DFF))],
        out_specs=[_row(tb, 2 * DFF), _fixed((8, 2 * DFF))],
        out_shape=[jax.ShapeDtypeStruct((t, 2 * DFF), bf16), jax.ShapeDtypeStruct((8, 2 * DFF), f32)],
        compiler_params=_params(1),
    )(dact, up, up, wf)


def _ffn_bwd2(dc, wf):
    t = dc.shape[0]
    tb = _pick(t, (ROW_BLOCK, 128))
    nb = t // tb

    def body(dc_ref, dch_ref, wf_ref, dup_ref):
        last = pl.program_id(0) == nb - 1
        for i in range(2 * DFF // 128):
            sl = _strip(i)
            dup_ref[:, sl] = _strip_conv_up(_f32(dc_ref, sl), _halo_after(dch_ref, sl), last, wf_ref, sl, 3).astype(bf16)

    return pl.pallas_call(
        body, name="ffn_bwd2", grid=(nb,),
        in_specs=[_row(tb, 2 * DFF), _next(tb, 2 * DFF, t, rows=16), _fixed((8, 2 * DFF))],
        out_specs=_row(tb, 2 * DFF), out_shape=jax.ShapeDtypeStruct((t, 2 * DFF), bf16), compiler_params=_params(1),
    )(dc, dc, wf)


def _final(x3, tgt, g):
    t = x3.shape[0]
    tb = _pick(t, (ELEMENTWISE_BLOCK, 256, 128))

    def body(x_ref, t_ref, g_ref, loss_ref, dx_ref, dxb_ref, dg_ref):
        @pl.when(pl.program_id(0) == 0)
        def _():
            loss_ref[...] = jnp.zeros_like(loss_ref)
            dg_ref[...] = jnp.zeros_like(dg_ref)

        xv = x_ref[...]
        r = lax.rsqrt(jnp.mean(xv * xv, axis=-1, keepdims=True) + EPS)
        xh = xv * r
        gv = g_ref[...]
        e = xh * gv - t_ref[...]
        lrow = 0.5 * jnp.mean(e * e, axis=-1, keepdims=True)
        loss_ref[...] += jnp.sum(jnp.broadcast_to(lrow, (tb, 128)).reshape(tb // 8, 8, 128), axis=0)
        dy = e * (1.0 / D)
        dyg = dy * gv
        dx = r * (dyg - xh * jnp.mean(dyg * xh, axis=-1, keepdims=True))
        dx_ref[...] = dx
        dxb_ref[...] = dx.astype(bf16)
        dg_ref[...] += jnp.sum((dy * xh).reshape(tb // 8, 8, D), axis=0)

    return pl.pallas_call(
        body, name="final", grid=(t // tb,), in_specs=[_row(tb, D), _row(tb, D), _fixed((1, D))],
        out_specs=[_fixed((8, 128)), _row(tb, D), _row(tb, D), _fixed((8, D))],
        out_shape=[jax.ShapeDtypeStruct((8, 128), f32), jax.ShapeDtypeStruct((t, D), f32),
                   jax.ShapeDtypeStruct((t, D), bf16), jax.ShapeDtypeStruct((8, D), f32)],
        compiler_params=_params(1),
    )(x3, tgt, g)


def _pre_bwd1(pg, pq, p2, dya_in, dqn, dkn, dvc, dgb, gbeta, h1, wa, wg, alog, dtb):
    t = pg.shape[0]
    tb = _pick(t, (ROW_BLOCK // 2, 128))

    def body(p0_ref, p0h_ref, pq_ref, pqh_ref, p2_ref, dya_ref, dqn_ref, dkn_ref, dvc_ref, dgb_ref, gb_ref, h1_ref,
             wa_ref, wg_ref, alog_ref, dtb_ref,
             dbg_ref, dca_ref, dc4_ref, dp2_ref, dwa_ref, dwg_ref, dal_ref, ddt_ref, dw2_ref):
        @pl.when(pl.program_id(0) == 0)
        def _():
            dwa_ref[...] = jnp.zeros_like(dwa_ref)
            dwg_ref[...] = jnp.zeros_like(dwg_ref)
            dal_ref[...] = jnp.zeros_like(dal_ref)
            ddt_ref[...] = jnp.zeros_like(ddt_ref)
            dw2_ref[...] = jnp.zeros_like(dw2_ref)

        first = pl.program_id(0) == 0

        for i in range(D // 128):
            sl, cg, xv = _strip(i), _strip(i, D), _strip(i, 2 * D)
            taps = _strip_taps(_f32(p0_ref, cg) * _f32(p0_ref, xv), _halo_before(p0h_ref, cg) * _halo_before(p0h_ref, xv),
                               first, 3)
            dya = _f32(dya_ref, sl)
            dbg_ref[:, sl] = (dya * _strip_conv(wa_ref, sl, taps)).astype(bf16)
            dca = dya * _f32(p0_ref, sl)
            dca_ref[:, sl] = dca.astype(bf16)
            _strip_weight_grad(dwa_ref, sl, dca, taps)

        for part, d_ref, scale in ((0, dqn_ref, DH ** -0.5), (1, dkn_ref, 1.0), (2, dvc_ref, None)):
            for h in range(H):
                sl = _strip(h, part * D)
                taps = _strip_taps(pq_ref[:, sl], pqh_ref[:, sl], first, 4)
                c4 = _strip_conv(wg_ref, sl, taps)
                sg = _sigmoid(c4)
                dn = d_ref[:, _strip(h)]
                if scale is not None:
                    a = c4 * sg
                    r = lax.rsqrt(jnp.sum(a * a, axis=-1, keepdims=True) + EPS)
                    an = a * r
                    dn = dn * scale
                    dn = r * (dn - an * jnp.sum(dn * an, axis=-1, keepdims=True))
                dc4 = dn * (sg * (1.0 + c4 * (1.0 - sg)))
                dc4_ref[:, sl] = dc4.astype(bf16)
                _strip_weight_grad(dwg_ref, sl, dc4, taps)

        ab = p2_ref[...]
        lane = lax.broadcasted_iota(jnp.int32, ab.shape, 1)
        dgbv = dgb_ref[...]
        gbv = gb_ref[...]
        da = dgbv * (-jnp.exp(alog_ref[...])) * _sigmoid(ab + dtb_ref[...])
        db = dgbv * gbv * (1.0 - gbv)
        dp2 = jnp.where(lane < H, da, jnp.where(lane < 2 * H, db, 0.0)).astype(bf16)
        dp2_ref[...] = dp2
        dw2_ref[...] += lax.dot_general(dp2, h1_ref[...], (((0,), (0,)), ((), ())), preferred_element_type=f32)
        dal = jnp.where(lane < H, dgbv * gbv, 0.0)
        ddt = jnp.where(lane < H, da, 0.0)
        dal_ref[...] += jnp.sum(dal.reshape(tb // 8, 8, 128), axis=0)
        ddt_ref[...] += jnp.sum(ddt.reshape(tb // 8, 8, 128), axis=0)

    return pl.pallas_call(
        body, name="pre_bwd1", grid=(t // tb,),
        in_specs=[_row(tb, 3 * D, 0), _prev(tb, 3 * D, 0, rows=16), _row(tb, 3 * D), _prev(tb, 3 * D), _row(tb, 128),
                  _row(tb, D), _row(tb, D), _row(tb, D), _row(tb, D), _row(tb, 128), _row(tb, 128), _row(tb, D),
                  _fixed((8, D)), _fixed((8, 3 * D)), _fixed((1, 128)), _fixed((1, 128))],
        out_specs=[_row(tb, D), _row(tb, D), _row(tb, 3 * D), _row(tb, 128),
                   _fixed((8, D)), _fixed((8, 3 * D)), _fixed((8, 128)), _fixed((8, 128)), _fixed((128, D))],
        out_shape=[jax.ShapeDtypeStruct((t, D), bf16), jax.ShapeDtypeStruct((t, D), bf16),
                   jax.ShapeDtypeStruct((t, 3 * D), bf16), jax.ShapeDtypeStruct((t, 128), bf16),
                   jax.ShapeDtypeStruct((8, D), f32), jax.ShapeDtypeStruct((8, 3 * D), f32),
                   jax.ShapeDtypeStruct((8, 128), f32), jax.ShapeDtypeStruct((8, 128), f32),
                   jax.ShapeDtypeStruct((128, D), f32)],
        compiler_params=_params(1),
    )(pg, pg, pq, pq, p2, dya_in, dqn, dkn, dvc, dgb, gbeta, h1, wa, wg, alog, dtb)


def _pre_bwd2(dca, dc4, pg, dbg, dz, dgates, wa, wg, exchange=None):
    t = pg.shape[0]
    tb = _pick(t, (ROW_BLOCK, 128))
    nb = t // tb

    def body(dca_ref, dcah_ref, dc4_ref, dc4h_ref, p0_ref, dbg_ref, dz_ref, dgt_ref, wa_ref, wg_ref, dp_ref):
        last = pl.program_id(0) == nb - 1
        dp_ref[:, :D] = dbg_ref[...]
        for i in range(D // 128):
            sl, cg, xv = _strip(i), _strip(i, D), _strip(i, 2 * D)
            du = _strip_conv_up(_f32(dca_ref, sl), _halo_after(dcah_ref, sl), last, wa_ref, sl, 3)
            dp_ref[:, cg] = (du * _f32(p0_ref, xv)).astype(bf16)
            dp_ref[:, xv] = (du * _f32(p0_ref, cg)).astype(bf16)
        dp_ref[:, 3 * D:4 * D] = dz_ref[...]
        dp_ref[:, 4 * D:6 * D] = dgt_ref[...]
        for i in range(3 * D // 128):
            sl = _strip(i)
            dq = _strip_conv_up(_f32(dc4_ref, sl), _halo_after(dc4h_ref, sl), last, wg_ref, sl, 4)
            dp_ref[:, _strip(i, 6 * D)] = dq.astype(bf16)

    return _call_with_exchange(
        body, exchange, name="pre_bwd2", grid=(nb,),
        in_specs=[_row(tb, D), _next(tb, D, t, rows=16), _row(tb, 3 * D), _next(tb, 3 * D, t, rows=16), _row(tb, 3 * D, 0),
                  _row(tb, D), _row(tb, D), _row(tb, 2 * D), _fixed((8, D)), _fixed((8, 3 * D))],
        out_specs=_row(tb, NW1), out_shape=jax.ShapeDtypeStruct((t, NW1), bf16),
        args=[dca, dca, dc4, dc4, pg, dbg, dz, dgates, wa, wg])


def _chunk_consts():
    r = lax.broadcasted_iota(jnp.int32, (CH, CH), 0)
    c = lax.broadcasted_iota(jnp.int32, (CH, CH), 1)
    return r, c, (r == c).astype(f32)


def _tri_inverse(lows, eye, r, c):
    def same_block(b):
        return jnp.bitwise_xor(r, c) < b

    xs = [jnp.where(same_block(8), -low, 0.0) for low in lows]
    ts = [eye + x for x in xs]
    for _ in range(2):
        xs = [_idot(x, x) for x in xs]
        ts = [t + _idot(t, x) for t, x in zip(ts, xs)]
    for b in (8, 16, 32):
        below = same_block(2 * b) & jnp.logical_not(same_block(b))
        ts = [t - _idot(_idot(t, jnp.where(below, low, 0.0)), t) for t, low in zip(ts, lows)]
    return ts


def _chunk_common(q, k, v, gcol, bcol, r, c, eye):
    grow = jnp.sum(eye * gcol, axis=0, keepdims=True)
    dec = jnp.exp(jnp.where(r >= c, gcol - grow, -jnp.inf))
    rcol = lax.broadcasted_iota(jnp.int32, (CH, 1), 0)
    glast = jnp.sum(jnp.where(rcol == CH - 1, gcol, 0.0), axis=0, keepdims=True)
    eg = jnp.exp(gcol)
    el = jnp.exp(glast - gcol)
    kb = k * bcol
    vb = v * bcol
    kk = _bdot_nt(kb, k)
    low = jnp.where(r > c, kk * dec, 0.0)
    qk = _bdot_nt(q, k)
    att = qk * dec
    return grow, dec, glast, eg, el, kb, vb, kk, low, qk, att, rcol


def _gdn_fwd(qn, kn, vc, gbeta):
    t = qn.shape[0]
    n_chunks = t // CH

    def body(q_ref, k_ref, v_ref, gb_ref, o_ref, s_ref, t_ref, state):
        @pl.when(pl.program_id(0) == 0)
        def _():
            state[...] = jnp.zeros_like(state)

        r, c, eye = _chunk_consts()
        tri = (r >= c).astype(f32)
        heads = range(H)
        keys = [(s, h) for s in range(GDN_STEP) for h in heads]
        rows = [slice(s * CH, (s + 1) * CH) for s in range(GDN_STEP)]
        gbs = [gb_ref[rows[s], :] for s in range(GDN_STEP)]
        galls = [_hdot(tri, gb) for gb in gbs]
        qs = {(s, h): q_ref[rows[s], h * DH:(h + 1) * DH] for s, h in keys}
        ks = {(s, h): k_ref[rows[s], h * DH:(h + 1) * DH] for s, h in keys}
        cm = {(s, h): _chunk_common(qs[s, h], ks[s, h], v_ref[rows[s], h * DH:(h + 1) * DH], galls[s][:, h:h + 1],
                                    gbs[s][:, H + h:H + h + 1], r, c, eye) for s, h in keys}
        invs = dict(zip(keys, _tri_inverse([cm[key][8] for key in keys], eye, r, c)))
        uws = {key: _bdot(invs[key], jnp.concatenate([cm[key][6], cm[key][5] * cm[key][3]], axis=1)) for key in keys}
        sts = [state[h] for h in heads]
        for s in range(GDN_STEP):
            vns = [uws[s, h][:, :DH] - _bdot(uws[s, h][:, DH:], sts[h]) for h in heads]
            outs = [_bdot(qs[s, h] * cm[s, h][3], sts[h]) + _bdot(cm[s, h][10], vns[h]) for h in heads]
            news = [sts[h] * jnp.exp(cm[s, h][2]) + _bdot_tn(ks[s, h] * cm[s, h][4], vns[h]) for h in heads]
            for h in heads:
                s_ref[s, h] = sts[h].astype(bf16)
                t_ref[s, h] = invs[s, h]
                o_ref[rows[s], h * DH:(h + 1) * DH] = outs[h]
            sts = news
        for h in heads:
            state[h] = sts[h]

    tb = GDN_STEP * CH
    return pl.pallas_call(
        body, name="gdn_fwd", grid=(t // tb,),
        in_specs=[_row(tb, D), _row(tb, D), _row(tb, D), _row(tb, 128)],
        out_specs=[_row(tb, D), pl.BlockSpec((GDN_STEP, H, DH, DH), lambda i: (i, 0, 0, 0)),
                   pl.BlockSpec((GDN_STEP, H, CH, CH), lambda i: (i, 0, 0, 0))],
        out_shape=[jax.ShapeDtypeStruct((t, D), f32), jax.ShapeDtypeStruct((n_chunks, H, DH, DH), bf16),
                   jax.ShapeDtypeStruct((n_chunks, H, CH, CH), f32)],
        scratch_shapes=[pltpu.VMEM((H, DH, DH), f32)],
        compiler_params=_params(1),
    )(qn, kn, vc, gbeta)


def _gdn_bwd(qn, kn, vc, gbeta, do, s_all, t_all):
    t = qn.shape[0]

    def body(q_ref, k_ref, v_ref, gb_ref, do_ref, s_ref, t_ref, dq_ref, dk_ref, dv_ref, dgb_ref, dstate):
        @pl.when(pl.program_id(0) == 0)
        def _():
            dstate[...] = jnp.zeros_like(dstate)

        r, c, eye = _chunk_consts()
        tril = r >= c
        lane = lax.broadcasted_iota(jnp.int32, (1, 128), 1)
        hs = range(H)

        def each(fn, *lists):
            return [fn(*args) for args in zip(*lists)]

        def rsum(a):
            return jnp.sum(a, axis=1, keepdims=True)

        def before_state(s):
            rows = slice(s * CH, (s + 1) * CH)
            gb = gb_ref[rows, :]
            gall = _hdot(tril.astype(f32), gb)
            p = {"rows": rows}
            p["q"] = q = [q_ref[rows, h * DH:(h + 1) * DH] for h in hs]
            p["k"] = k = [k_ref[rows, h * DH:(h + 1) * DH] for h in hs]
            p["v"] = v = [v_ref[rows, h * DH:(h + 1) * DH] for h in hs]
            p["dout"] = dout = [do_ref[rows, h * DH:(h + 1) * DH] for h in hs]
            p["inv"] = inv = [t_ref[s, h] for h in hs]
            p["st"] = st = [s_ref[s, h] for h in hs]
            p["bcol"] = bcol = [gb[:, H + h:H + h + 1] for h in hs]
            cm = [_chunk_common(q[h], k[h], v[h], gall[:, h:h + 1], bcol[h], r, c, eye) for h in hs]
            for name, i in (("dec", 1), ("glast", 2), ("eg", 3), ("el", 4), ("kb", 5), ("vb", 6), ("low", 8), ("att", 10)):
                p[name] = [m[i] for m in cm]
            p["rcol"] = cm[0][11]
            p["elast"] = each(jnp.exp, p["glast"])
            p["kbg"] = each(jnp.multiply, p["kb"], p["eg"])
            uw = each(lambda i, a, b: _bdot(i, jnp.concatenate([a, b], axis=1)), inv, p["vb"], p["kbg"])
            p["u"] = [a[:, :DH] for a in uw]
            p["w"] = [a[:, DH:] for a in uw]
            p["vn"] = each(lambda a, b, x: a - _bdot(b, x), p["u"], p["w"], st)
            p["qd"] = each(jnp.multiply, q, p["eg"])
            p["kd"] = each(jnp.multiply, k, p["el"])
            p["dqd"] = each(_bdot_nt, dout, st)
            p["datt"] = each(lambda d, x: jnp.where(tril, _bdot_nt(d, x), 0.0), dout, p["vn"])
            p["dqk"] = each(jnp.multiply, p["datt"], p["dec"])
            p["qd_do"] = each(_bdot_tn, p["qd"], dout)
            p["att_do"] = each(_bdot_tn, p["att"], dout)
            return p

        def after_state(p, ds):
            q, k, v, st, inv, bcol = p["q"], p["k"], p["v"], p["st"], p["inv"], p["bcol"]
            eg, el, kb, u, w = p["eg"], p["el"], p["kb"], p["u"], p["w"]
            dvn = each(lambda a, kk, x: a + _bdot(kk, x), p["att_do"], p["kd"], ds)
            dkd = each(_bdot_nt, p["vn"], ds)
            dw = each(lambda a, x: -_bdot_nt(a, x), dvn, st)
            new_ds = each(lambda x, e, a, ww, dv_: x * e + a - _bdot_tn(ww, dv_), ds, p["elast"], p["qd_do"], w, dvn)
            dglast = each(lambda e, x, d: e * jnp.sum(rsum(x.astype(f32) * d), axis=0, keepdims=True), p["elast"], st, ds)
            dr = each(lambda i, a, b: _bdot_tn(i, jnp.concatenate([a, b], axis=1)), inv, dvn, dw)
            dvb = [a[:, :DH] for a in dr]
            dkbg = [a[:, DH:] for a in dr]
            dlow = each(lambda a, b, x, y: -jnp.where(r > c, _bdot_nt(a, b) + _bdot_nt(x, y), 0.0), dvb, u, dkbg, w)
            dkk = each(jnp.multiply, dlow, p["dec"])
            mm = each(lambda a, b, x, y: a * b + x * y, dlow, p["low"], p["datt"], p["att"])
            dkb = each(lambda a, kk, b, e: _bdot(a, kk) + b * e, dkk, k, dkbg, eg)
            dk = each(lambda a, b, x, y, d, e, f, g: _bdot_tn(a, b) + _bdot_tn(x, y) + d * e + f * g,
                      dkk, kb, p["dqk"], q, dkd, el, dkb, bcol)
            dq = each(lambda a, kk, d, e: _bdot(a, kk) + d * e, p["dqk"], k, p["dqd"], eg)
            dv = each(jnp.multiply, dvb, bcol)
            dbeta = each(lambda a, b, x, y: rsum(a * b) + rsum(x * y), dkb, k, dvb, v)
            deg = each(lambda a, b, x, y: rsum(a * b) + rsum(x * y), dkbg, kb, p["dqd"], q)
            delc = each(lambda a, b, e: rsum(a * b) * e, dkd, k, el)
            dgc = each(lambda m, a, e, d: rsum(m) - rsum(eye * jnp.sum(m, axis=0, keepdims=True)) + a * e - d,
                       mm, deg, eg, delc)
            dgc = each(lambda g, d, l: g + jnp.where(p["rcol"] == CH - 1, jnp.sum(d, axis=0, keepdims=True) + l, 0.0),
                       dgc, delc, dglast)
            dg_acc = jnp.zeros((CH, 128), f32)
            db_acc = jnp.zeros((CH, 128), f32)
            rows = p["rows"]
            for h in hs:
                dq_ref[rows, h * DH:(h + 1) * DH] = dq[h]
                dk_ref[rows, h * DH:(h + 1) * DH] = dk[h]
                dv_ref[rows, h * DH:(h + 1) * DH] = dv[h]
                dg_acc = dg_acc + dgc[h] * (lane == h).astype(f32)
                db_acc = db_acc + dbeta[h] * (lane == H + h).astype(f32)
            dgb_ref[rows, :] = _hdot((r <= c).astype(f32), dg_acc) + db_acc
            return new_ds

        order = list(reversed(range(GDN_STEP)))
        pre = [before_state(s) for s in order]
        ds = [dstate[h] for h in hs]
        for p in pre:
            ds = after_state(p, ds)
        for h in hs:
            dstate[h] = ds[h]

    tb = GDN_STEP * CH
    n_steps = t // tb
    rev = lambda i: (n_steps - 1 - i, 0)
    rev4 = lambda i: (n_steps - 1 - i, 0, 0, 0)
    return pl.pallas_call(
        body, name="gdn_bwd", grid=(n_steps,),
        in_specs=[pl.BlockSpec((tb, D), rev), pl.BlockSpec((tb, D), rev), pl.BlockSpec((tb, D), rev),
                  pl.BlockSpec((tb, 128), rev), pl.BlockSpec((tb, D), rev),
                  pl.BlockSpec((GDN_STEP, H, DH, DH), rev4), pl.BlockSpec((GDN_STEP, H, CH, CH), rev4)],
        out_specs=[pl.BlockSpec((tb, D), rev), pl.BlockSpec((tb, D), rev), pl.BlockSpec((tb, D), rev),
                   pl.BlockSpec((tb, 128), rev)],
        out_shape=[jax.ShapeDtypeStruct((t, D), f32)] * 3 + [jax.ShapeDtypeStruct((t, 128), f32)],
        scratch_shapes=[pltpu.VMEM((H, DH, DH), f32)],
        compiler_params=_params(1),
    )(qn, kn, vc, gbeta, do, s_all, t_all)


def _pad_rows(w, rows=8):
    return jnp.pad(w, ((0, rows - w.shape[0]), (0, 0)))


_REST = ("w_up", "w_a_out", "w_b_out", "w_o", "w_down")


def _local_step(x, tgt, w, comm=None):
    g1 = w["norm_mix_g"].reshape(1, D)
    if comm is None:
        h1 = _rms_fwd(x, g1, name="rms1_fwd")
    else:
        h1, gathered = _rms_fwd(x, g1, name="rms1_fwd", exchange=comm.gather_first())
        w = {**w, **comm.finish_first(gathered)}
    w1, w2 = w["w1"], w["w2"]
    wa = _pad_rows(w["conv_a_w"])
    wg = _pad_rows(w["gdn_conv_w"])
    wf = _pad_rows(w["ffn_conv_w"])
    alog = jnp.pad(w["gdn_A_log"].reshape(1, H), ((0, 0), (0, 128 - H)))
    dtb = jnp.pad(w["gdn_dt_bias"].reshape(1, H), ((0, 0), (0, 128 - H)))
    g2 = w["norm_ffn_g"].reshape(1, D)
    g3 = w["norm_final_g"].reshape(1, D)
    gn = w["gdn_norm_g"].reshape(1, DH)

    if comm is None:
        pg = _matmul(h1, w1, name="mm_in", cols=(0, 6 * D), out_dtype=bf16)
        pq = _matmul(h1, w1, name="mm_in_qkv", cols=(6 * D, 3 * D))
    else:
        pg, gathered = _matmul(h1, w1, name="mm_in", cols=(0, 6 * D), out_dtype=bf16, exchange=comm.gather_rest())
        pq, gathered = _matmul(h1, w1, name="mm_in_qkv", cols=(6 * D, 3 * D), exchange=_gather_forward_exchange(gathered))
        w = {**w, **comm.finish_gather(gathered)}
    ya_in, qn, kn, vc, gbeta, p2 = _pre_fwd(pg, pq, h1, w2, wa, wg, alog, dtb)
    o, s_all, t_all = _gdn_fwd(qn, kn, vc, gbeta)
    yb_in = _post_fwd(o, pg, gn)
    ya = _matmul(ya_in, w["w_a_out"], name="mm_a", out_dtype=bf16)
    yb = _matmul(yb_in, w["w_b_out"], name="mm_b", out_dtype=bf16)
    mix = _mix_fwd(ya, yb, pg)
    x2 = _matmul(mix, w["w_o"], name="mm_o", add=x, tm=1024)
    h2 = _rms_fwd(x2, g2, name="rms2_fwd")
    up = _matmul(h2, w["w_up"], nt=True, name="mm_up", tn=DFF // 2, out_dtype=bf16)
    act = _ffn_fwd(up, wf)
    x3 = _matmul(act, w["w_down"], name="mm_down", add=x2, tm=512)
    loss_p, dx3, dx3b, dg3 = _final(x3, tgt, g3)

    grads = {"norm_final_g": dg3}
    dact = _matmul(dx3b, w["w_down"], nt=True, name="mm_down_dx", tm=512, tn=DFF, out_dtype=bf16)
    grads["w_down"] = _matmul_tn(act, dx3b, name="mm_down_dw", tm=DFF // 2)
    dc, dwf = _ffn_bwd1(dact, up, wf)
    grads["ffn_conv_w"] = dwf
    dup = _ffn_bwd2(dc, wf)
    dh2 = _matmul(dup, w["w_up"], name="mm_up_dx", tm=1024, tk=DFF)
    grads["w_up"] = _matmul_tn(dup, h2, name="mm_up_dw", tm=DFF // 2)
    dx2, dx2b, dg2 = _rms_bwd(dh2, x2, g2, dx3, name="rms2_bwd")
    grads["norm_ffn_g"] = dg2
    dmix = _matmul(dx2b, w["w_o"], nt=True, name="mm_o_dx", out_dtype=bf16)
    grads["w_o"] = _matmul_tn(mix, dx2b, name="mm_o_dw")
    dya, dyb, dgates = _mix_bwd(dmix, ya, yb, pg)
    dya_in = _matmul(dya, w["w_a_out"], nt=True, name="mm_a_dx", out_dtype=bf16)
    grads["w_a_out"] = _matmul_tn(ya_in, dya, name="mm_a_dw")
    dyb_in = _matmul(dyb, w["w_b_out"], nt=True, name="mm_b_dx")
    grads["w_b_out"] = _matmul_tn(yb_in, dyb, name="mm_b_dw")
    do, dz, dgn = _post_bwd(dyb_in, o, pg, gn)
    grads["gdn_norm_g"] = dgn
    dqn, dkn, dvc, dgb = _gdn_bwd(qn, kn, vc, gbeta, do, s_all, t_all)
    dbg, dca, dc4, dp2, dwa, dwg, dal, ddt, grads["w2"] = _pre_bwd1(pg, pq, p2, dya_in, dqn, dkn, dvc, dgb, gbeta, h1,
                                                                    wa, wg, alog, dtb)
    grads["conv_a_w"] = dwa
    grads["gdn_conv_w"] = dwg
    grads["gdn_A_log"] = dal
    grads["gdn_dt_bias"] = ddt
    if comm is None:
        dp1 = _pre_bwd2(dca, dc4, pg, dbg, dz, dgates, wa, wg)
        grads["w1"] = _matmul_tn(dp1, h1, name="mm_in_dw", tt=4096)
        dh1 = _matmul(dp1, w1, nt=True, name="mm_in_dx", tm=512, tk=NW1 // 2)
    else:
        exchange, blocks = comm.reduce_halves(_REST, grads)
        dp1, recv = _pre_bwd2(dca, dc4, pg, dbg, dz, dgates, wa, wg, exchange=exchange)
        exchange, sums = comm.reduce_sums(_REST, blocks, recv)
        grads["w1"], recv = _matmul_tn(dp1, h1, name="mm_in_dw", tt=4096, exchange=exchange)
        comm.finish_reduce(_REST, sums, recv)
        exchange, blocks = comm.reduce_halves(("w_in",), grads)
        exchange, sums = comm.reduce_sums(("w_in",), blocks, _run_exchange(exchange, name="rs_sibling_w_in"))
        dh1, recv = _matmul(dp1, w1, nt=True, name="mm_in_dx", tm=512, tk=NW1 // 2, exchange=exchange)
        comm.finish_reduce(("w_in",), sums, recv)
    dx, dg1 = _rms_bwd(dh1, x, g1, dx2, name="rms1_bwd", more=(dp2, w2), bf16_copy=False)
    grads["norm_mix_g"] = dg1
    return loss_p, dx, grads


_ANY = pl.BlockSpec(memory_space=pl.ANY)


def _remote(src, dst, send_sem, recv_sem, to):
    return pltpu.make_async_remote_copy(src_ref=src, dst_ref=dst, send_sem=send_sem, recv_sem=recv_sem,
                                        device_id=to, device_id_type=MESH)


def _run_exchange(exchange, *, name):
    arrays, shapes, sems, start, wait = exchange
    n_in, n_out = len(arrays), len(shapes)

    def body(*refs):
        start(refs[:n_in], refs[n_in:n_in + n_out], refs[n_in + n_out:])
        wait(refs[:n_in], refs[n_in:n_in + n_out], refs[n_in + n_out:])

    return pl.pallas_call(body, name=name, out_shape=list(shapes), in_specs=[_ANY] * n_in, out_specs=[_ANY] * n_out,
                          scratch_shapes=list(sems))(*arrays)


def _gather_exchange(shards):
    n = len(shards)

    def copies(x_refs, out_refs, sems):
        send_sems, recv_sems, local_sems = sems
        x, y, c = lax.axis_index("x"), lax.axis_index("y"), lax.axis_index("c")

        def flip(v, b):
            return v + b - 2 * v * b

        me, sibling = (x, y, c), (x, y, 1 - c)
        chip1, chip2, diag = (flip(x, 1 - c), flip(y, c)), (flip(x, c), flip(y, 1 - c)), (1 - x, 1 - y)

        def copy(a, k, blk, to, from_input=False):
            dst = out_refs[a].at[4 * blk[0] + 2 * blk[1] + blk[2]]
            return _remote(x_refs[a] if from_input else dst, dst, send_sems.at[a, k], recv_sems.at[a, k], to)

        mine = [pltpu.make_async_copy(x_refs[a], out_refs[a].at[4 * x + 2 * y + c], local_sems.at[a]) for a in range(n)]
        first = []
        for a in range(n):
            first += [copy(a, 0, me, sibling, from_input=True), copy(a, 1, me, (*chip1, c), from_input=True),
                      copy(a, 2, me, (*chip2, c), from_input=True)]
        return copy, mine, first, me, sibling, chip1, chip2, diag, c

    def start(x_refs, out_refs, sems):
        _, mine, first, *_ = copies(x_refs, out_refs, sems)
        for cp in mine + first:
            cp.start()

    def wait(x_refs, out_refs, sems):
        copy, mine, first, me, sibling, chip1, chip2, diag, c = copies(x_refs, out_refs, sems)
        passed = []

        def pass_on(cp):
            passed.append(cp)
            cp.start()

        for a in range(n):
            copy(a, 1, (*chip1, c), me).wait_recv()
            pass_on(copy(a, 3, (*chip1, c), (*chip2, c)))
            pass_on(copy(a, 4, (*chip1, c), sibling))
        for a in range(n):
            copy(a, 2, (*chip2, c), me).wait_recv()
            pass_on(copy(a, 5, (*chip2, c), sibling))
        for a in range(n):
            copy(a, 3, (*diag, c), me).wait_recv()
            pass_on(copy(a, 6, (*diag, c), sibling))
        for a in range(n):
            copy(a, 0, sibling, me).wait_recv()
            copy(a, 4, (*chip2, 1 - c), me).wait_recv()
            copy(a, 5, (*chip1, 1 - c), me).wait_recv()
            copy(a, 6, (*diag, 1 - c), me).wait_recv()
        for cp in first + passed:
            cp.wait_send()
        for cp in mine:
            cp.wait()

    shapes = [jax.ShapeDtypeStruct((N_DEV, *s.shape), s.dtype) for s in shards]
    sems = [pltpu.SemaphoreType.DMA((n, 7)), pltpu.SemaphoreType.DMA((n, 7)), pltpu.SemaphoreType.DMA((n,))]
    return shards, shapes, sems, start, wait


def _gather_direct_exchange(shards):
    n = len(shards)

    def copies(x_refs, out_refs, sems):
        send_sems, recv_sems, local_sems = sems
        x, y, c = lax.axis_index("x"), lax.axis_index("y"), lax.axis_index("c")
        targets = [(x, y, 1 - c), (1 - x, y, c), (x, 1 - y, c), (1 - x, 1 - y, c)]
        local, sends, recvs = [], [], []
        for a in range(n):
            mine = out_refs[a].at[4 * x + 2 * y + c]
            local.append(pltpu.make_async_copy(x_refs[a], mine, local_sems.at[a]))
            for k, to in enumerate(targets):
                theirs = out_refs[a].at[4 * to[0] + 2 * to[1] + to[2]]
                sends.append(_remote(x_refs[a], mine, send_sems.at[a, k], recv_sems.at[a, k], to))
                recvs.append(_remote(theirs, theirs, send_sems.at[a, k], recv_sems.at[a, k], to))
        return local, sends, recvs

    def start(x_refs, out_refs, sems):
        local, sends, _ = copies(x_refs, out_refs, sems)
        for cp in local + sends:
            cp.start()

    def wait(x_refs, out_refs, sems):
        local, sends, recvs = copies(x_refs, out_refs, sems)
        for cp in recvs:
            cp.wait_recv()
        for cp in sends:
            cp.wait_send()
        for cp in local:
            cp.wait()

    shapes = [jax.ShapeDtypeStruct((N_DEV, *s.shape), s.dtype) for s in shards]
    sems = [pltpu.SemaphoreType.DMA((n, 4)), pltpu.SemaphoreType.DMA((n, 4)), pltpu.SemaphoreType.DMA((n,))]
    return shards, shapes, sems, start, wait


def _gather_forward_exchange(gathered):
    n = len(gathered)

    def copies(_, out_refs, sems):
        send_sems, recv_sems = sems
        x, y, c = lax.axis_index("x"), lax.axis_index("y"), lax.axis_index("c")
        sibling = (x, y, 1 - c)
        sends, recvs = [], []
        for a in range(n):
            for j, (px, py) in enumerate([(1 - x, y), (x, 1 - y), (1 - x, 1 - y)]):
                mine = out_refs[a].at[4 * px + 2 * py + c]
                theirs = out_refs[a].at[4 * px + 2 * py + 1 - c]
                sends.append(_remote(mine, mine, send_sems.at[a, j], recv_sems.at[a, j], sibling))
                recvs.append(_remote(theirs, theirs, send_sems.at[a, j], recv_sems.at[a, j], sibling))
        return sends, recvs

    def start(in_refs, out_refs, sems):
        for cp in copies(in_refs, out_refs, sems)[0]:
            cp.start()

    def wait(in_refs, out_refs, sems):
        sends, recvs = copies(in_refs, out_refs, sems)
        for cp in recvs:
            cp.wait_recv()
        for cp in sends:
            cp.wait_send()

    shapes = [jax.ShapeDtypeStruct(g.shape, g.dtype) for g in gathered]
    sems = [pltpu.SemaphoreType.DMA((n, 3)), pltpu.SemaphoreType.DMA((n, 3))]
    return gathered, shapes, sems, start, wait, True


def _chips_exchange(hsums):
    n = len(hsums)

    def copies(h_refs, out_refs, sems):
        send_sems, recv_sems = sems
        x, y, c = lax.axis_index("x"), lax.axis_index("y"), lax.axis_index("c")
        chips = [(1 - x, y), (x, 1 - y), (1 - x, 1 - y)]
        return [_remote(h_refs[a].at[2 * px + py], out_refs[a].at[k], send_sems.at[a, k], recv_sems.at[a, k], (px, py, c))
                for a in range(n) for k, (px, py) in enumerate(chips)]

    def start(h_refs, out_refs, sems):
        for cp in copies(h_refs, out_refs, sems):
            cp.start()

    def wait(h_refs, out_refs, sems):
        for cp in copies(h_refs, out_refs, sems):
            cp.wait()

    shapes = [jax.ShapeDtypeStruct((3, *h.shape[1:]), h.dtype) for h in hsums]
    sems = [pltpu.SemaphoreType.DMA((n, 3)), pltpu.SemaphoreType.DMA((n, 3))]
    return hsums, shapes, sems, start, wait


def _sibling_exchange(halves):
    n = len(halves)

    def copies(p_refs, out_refs, sems):
        send_sems, recv_sems = sems
        x, y, c = lax.axis_index("x"), lax.axis_index("y"), lax.axis_index("c")
        return [_remote(p_refs[a], out_refs[a], send_sems.at[a], recv_sems.at[a], (x, y, 1 - c)) for a in range(n)]

    def start(p_refs, out_refs, sems):
        for cp in copies(p_refs, out_refs, sems):
            cp.start()

    def wait(p_refs, out_refs, sems):
        for cp in copies(p_refs, out_refs, sems):
            cp.wait()

    shapes = [jax.ShapeDtypeStruct(h.shape, h.dtype) for h in halves]
    return halves, shapes, [pltpu.SemaphoreType.DMA((n,)), pltpu.SemaphoreType.DMA((n,))], start, wait


_IN_RANGES = ((0, 3 * D, 0, 0), (3 * D, 6 * D, 0, 6 * D), (6 * D, 7 * D, 0, 3 * D), (7 * D, 7 * D + 16, 1, 0),
              (7 * D + 16, 9 * D + 16, 0, 4 * D))


def _col_pieces(width, ranges):
    pieces = []
    for d in range(N_DEV):
        lo, hi = d * width, (d + 1) * width
        for glo, ghi, mat, mlo in ranges:
            a, b = max(lo, glo), min(hi, ghi)
            if a < b:
                pieces.append((d, a - lo, b - lo, mat, mlo + a - glo))
    return pieces


def _cols_to_matrices(g, ranges, out_widths, *, name):
    _, rows, width = g.shape
    tb = RELAYOUT_BLOCK
    pieces = _col_pieces(width, ranges)
    covered = [sum(p[2] - p[1] for p in pieces if p[3] == m) for m in range(len(out_widths))]

    def body(g_ref, *o_refs):
        for m, o_ref in enumerate(o_refs):
            if covered[m] < out_widths[m]:
                o_ref[...] = jnp.zeros_like(o_ref)
        for d, b0, b1, m, m0 in pieces:
            o_refs[m][:, m0:m0 + b1 - b0] = g_ref[d, :, b0:b1]

    return pl.pallas_call(
        body, name=name, grid=(rows // tb,), in_specs=[pl.BlockSpec((N_DEV, tb, width), lambda i: (0, i, 0))],
        out_specs=[pl.BlockSpec((tb, wo), lambda i: (i, 0)) for wo in out_widths],
        out_shape=[jax.ShapeDtypeStruct((rows, wo), g.dtype) for wo in out_widths], compiler_params=_params(1),
    )(g)


def _transposed_matrices_to_blocks(mats, ranges, width, *, name):
    rows = mats[0].shape[1]
    tb = RELAYOUT_BLOCK
    pieces = _col_pieces(width, ranges)

    def body(*refs):
        m_refs, g_ref = refs[:-1], refs[-1]
        for d, b0, b1, m, m0 in pieces:
            g_ref[d, b0:b1, :] = m_refs[m][m0:m0 + b1 - b0, :]

    return pl.pallas_call(
        body, name=name, grid=(rows // tb,),
        in_specs=[pl.BlockSpec((mt.shape[0], tb), lambda i: (0, i)) for mt in mats],
        out_specs=pl.BlockSpec((N_DEV, width, tb), lambda i: (0, 0, i)),
        out_shape=jax.ShapeDtypeStruct((N_DEV, width, rows), mats[0].dtype), compiler_params=_params(1),
    )(*mats)


def _row_block(rows):
    return 128 if rows % 128 == 0 else rows


def _half_bf16(g4, c_other, *, name):
    _, _, rows, width = g4.shape
    tb = _row_block(rows)

    def body(c_ref, p_ref, o_ref):
        o_ref[0] = p_ref[0, 0].astype(bf16)

    grid_spec = pltpu.PrefetchScalarGridSpec(
        num_scalar_prefetch=1, grid=(4, rows // tb),
        in_specs=[pl.BlockSpec((1, 1, tb, width), lambda j, i, c_ref: (j, c_ref[0], i, 0))],
        out_specs=pl.BlockSpec((1, tb, width), lambda j, i, c_ref: (j, i, 0)))
    return pl.pallas_call(
        body, name=name, grid_spec=grid_spec, out_shape=jax.ShapeDtypeStruct((4, rows, width), bf16),
        compiler_params=_params(2, _vmem_for(4 * tb * width, 2 * tb * width)),
    )(c_other, g4)


def _pair_sum(g4, recv, c_me, *, name):
    _, _, rows, width = g4.shape
    tb = _row_block(rows)

    def body(c_ref, p_ref, r_ref, o_ref, ob_ref):
        s = p_ref[0, 0] + r_ref[0].astype(f32)
        o_ref[0] = s
        ob_ref[0] = s.astype(bf16)

    blk = pl.BlockSpec((1, tb, width), lambda j, i, c_ref: (j, i, 0))
    grid_spec = pltpu.PrefetchScalarGridSpec(
        num_scalar_prefetch=1, grid=(4, rows // tb),
        in_specs=[pl.BlockSpec((1, 1, tb, width), lambda j, i, c_ref: (j, c_ref[0], i, 0)), blk],
        out_specs=[blk, blk])
    return pl.pallas_call(
        body, name=name, grid_spec=grid_spec,
        out_shape=[jax.ShapeDtypeStruct((4, rows, width), f32), jax.ShapeDtypeStruct((4, rows, width), bf16)],
        compiler_params=_params(2, _vmem_for(4 * tb * width, 2 * tb * width, 4 * tb * width, 2 * tb * width)),
    )(c_me, g4, recv)


def _adam_shard(hsum, recv, chip, w, m, v, *, name):
    _, rows, width = w.shape
    tb = _row_block(rows)

    def body(j_ref, h_ref, r_ref, w_ref, m_ref, v_ref, g_out, d_out, m_out, v_out):
        g = ((h_ref[0] + r_ref[0].astype(f32)) + r_ref[1].astype(f32)) + r_ref[2].astype(f32)
        delta, mn, vn = _adam_math(w_ref[0], g, m_ref[0], v_ref[0])
        g_out[0] = g
        d_out[0] = delta
        m_out[0] = mn
        v_out[0] = vn

    blk = pl.BlockSpec((1, tb, width), lambda i, j_ref: (0, i, 0))
    grid_spec = pltpu.PrefetchScalarGridSpec(
        num_scalar_prefetch=1, grid=(rows // tb,),
        in_specs=[pl.BlockSpec((1, tb, width), lambda i, j_ref: (j_ref[0], i, 0)),
                  pl.BlockSpec((3, tb, width), lambda i, j_ref: (0, i, 0)), blk, blk, blk],
        out_specs=[blk, blk, blk, blk])
    return pl.pallas_call(
        body, name=name, grid_spec=grid_spec, out_shape=[jax.ShapeDtypeStruct(w.shape, f32)] * 4,
        compiler_params=_params(1, _vmem_for(*[4 * tb * width] * 8, 6 * tb * width)),
    )(chip, hsum, recv, w, m, v)


def _sum_shard(hsum, recv, chip, *, name):
    _, rows, width = hsum.shape
    tb = _row_block(rows)

    def body(j_ref, h_ref, r_ref, g_out):
        g_out[...] = ((h_ref[0] + r_ref[0].astype(f32)) + r_ref[1].astype(f32)) + r_ref[2].astype(f32)

    grid_spec = pltpu.PrefetchScalarGridSpec(
        num_scalar_prefetch=1, grid=(rows // tb,),
        in_specs=[pl.BlockSpec((1, tb, width), lambda i, j_ref: (j_ref[0], i, 0)),
                  pl.BlockSpec((3, tb, width), lambda i, j_ref: (0, i, 0))],
        out_specs=pl.BlockSpec((tb, width), lambda i, j_ref: (i, 0)))
    return pl.pallas_call(body, name=name, grid_spec=grid_spec, out_shape=jax.ShapeDtypeStruct((rows, width), f32),
                          compiler_params=_params(1, _vmem_for(*[4 * tb * width] * 2, 6 * tb * width)))(chip, hsum, recv)


def _adam_columns(g, w, m, v, *, name):
    cols, _, rows = w.shape
    tb = cols // 2

    def body(g_ref, w_ref, m_ref, v_ref, d_out, m_out, v_out):
        delta, mn, vn = _adam_math(w_ref[...], g_ref[...], m_ref[...], v_ref[...])
        d_out[...] = delta
        m_out[...] = mn
        v_out[...] = vn

    blk = pl.BlockSpec((tb, 1, rows), lambda i: (i, 0, 0))
    return pl.pallas_call(
        body, name=name, grid=(cols // tb,), in_specs=[blk] * 4, out_specs=[blk] * 3,
        out_shape=[jax.ShapeDtypeStruct(w.shape, f32)] * 3,
        compiler_params=_params(1, _vmem_for(*[4 * tb * rows] * 7)),
    )(g, w, m, v)


R_SMALL = 8 + 8 * N_DEV
_SMALL_LANES = {"gdn_norm_g": (0, DH), "gdn_A_log": (DH, DH + H), "gdn_dt_bias": (2 * DH, 2 * DH + H)}
_LOSS_LANE = 3 * DH


def _pack_small(dg1, dg2, dg3, dgn, dal, ddt, loss_p, dwa, dwg, dwf):
    def body(dg1_ref, dg2_ref, dg3_ref, dgn_ref, dal_ref, ddt_ref, loss_ref, dwa_ref, dwg_ref, dwf_ref, o_ref):
        def total(ref):
            return jnp.sum(ref[...], axis=0, keepdims=True)

        o_ref[...] = jnp.zeros_like(o_ref)
        o_ref[0:1, :] = total(dg1_ref)
        o_ref[1:2, :] = total(dg2_ref)
        o_ref[2:3, :] = total(dg3_ref)
        o_ref[3:4, 0:DH] = total(dgn_ref)
        o_ref[3:4, DH:2 * DH] = total(dal_ref)
        o_ref[3:4, 2 * DH:3 * DH] = total(ddt_ref)
        o_ref[3:4, 3 * DH:4 * DH] = total(loss_ref)
        for d in range(N_DEV):
            base = 8 + 8 * d
            o_ref[base:base + 3, 0:128] = dwa_ref[0:3, 128 * d:128 * (d + 1)]
            o_ref[base:base + 4, 128:512] = dwg_ref[0:4, 384 * d:384 * (d + 1)]
            o_ref[base + 4:base + 7, 0:704] = dwf_ref[0:3, 704 * d:704 * (d + 1)]

    return pl.pallas_call(body, name="pack_small", out_shape=jax.ShapeDtypeStruct((R_SMALL, D), f32))(
        dg1, dg2, dg3, dgn, dal, ddt, loss_p, dwa, dwg, dwf)


_SMALL = ("norm_mix_g", "norm_ffn_g", "norm_final_g", "gdn_norm_g", "gdn_A_log", "gdn_dt_bias",
          "conv_a_w", "gdn_conv_w", "ffn_conv_w")


def _adam_small(gath, me, w, m, v):
    arrays = [t[n] for n in _SMALL for t in (w, m, v)]

    def body(me_ref, ga_ref, gb_ref, *refs):
        ins, outs = refs[:len(arrays)], refs[len(arrays):]
        ga, gb = ga_ref[0], gb_ref[0]
        for s in range(1, N_DEV):
            ga = ga + ga_ref[s]
            gb = gb + gb_ref[s]
        grads = {"norm_mix_g": ga[0:1, :], "norm_ffn_g": ga[1:2, :], "norm_final_g": ga[2:3, :],
                 "conv_a_w": gb[0:3, 0:128], "gdn_conv_w": gb[0:4, 128:512], "ffn_conv_w": gb[4:7, 0:704]}
        for n, (lo, hi) in _SMALL_LANES.items():
            grads[n] = ga[3:4, lo:hi]
        for i, n in enumerate(_SMALL):
            three_d = len(w[n].shape) == 3
            wv, mv, vv = (r[0] if three_d else r[...] for r in ins[3 * i:3 * i + 3])
            delta, mn, vn = _adam_math(wv, grads[n], mv, vv)
            for o_ref, val in zip(outs[4 * i:4 * i + 4], (grads[n], delta, mn, vn)):
                if three_d:
                    o_ref[0] = val
                else:
                    o_ref[...] = val
        outs[-1][...] = ga[3:4, _LOSS_LANE:_LOSS_LANE + 1]

    def whole(shape):
        return pl.BlockSpec(shape, lambda i, me_ref: (0,) * len(shape))

    grid_spec = pltpu.PrefetchScalarGridSpec(
        num_scalar_prefetch=1, grid=(1,),
        in_specs=[pl.BlockSpec((N_DEV, 8, D), lambda i, me_ref: (0, 0, 0)),
                  pl.BlockSpec((N_DEV, 8, D), lambda i, me_ref: (0, 1 + me_ref[0], 0))] + [whole(a.shape) for a in arrays],
        out_specs=[whole(w[n].shape) for n in _SMALL for _ in range(4)] + [whole((1, 1))])
    res = pl.pallas_call(
        body, name="adam_small", grid_spec=grid_spec,
        out_shape=[jax.ShapeDtypeStruct(w[n].shape, f32) for n in _SMALL for _ in range(4)]
        + [jax.ShapeDtypeStruct((1, 1), f32)],
        compiler_params=_params(1),
    )(me, gath, gath, *arrays)
    return {n: tuple(res[4 * i:4 * i + 4]) for i, n in enumerate(_SMALL)}, res[-1]


def _adam_math(w, g, m, v):
    m = ADAM_B1 * m + (1.0 - ADAM_B1) * g
    v = ADAM_B2 * v + (1.0 - ADAM_B2) * jnp.square(g)
    m_hat = m / (1.0 - ADAM_B1 ** ADAM_STEP)
    v_hat = v / (1.0 - ADAM_B2 ** ADAM_STEP)
    delta = -ADAM_LR * (m_hat / (jnp.sqrt(v_hat) + ADAM_EPS) + ADAM_WD * w)
    return delta, m, v


_WEIGHTS = ("norm_mix_g", "w_in", "conv_a_w", "gdn_conv_w", "gdn_A_log", "gdn_dt_bias", "gdn_norm_g", "w_a_out",
            "w_b_out", "w_o", "norm_ffn_g", "w_up", "ffn_conv_w", "w_down", "norm_final_g")
_CONVS = ("conv_a_w", "gdn_conv_w", "ffn_conv_w")


class _StepExchanges:
    def __init__(self, wts, mom, var, c_me, chip):
        self.wts, self.mom, self.var, self.c_me, self.chip = wts, mom, var, c_me, chip
        self.results = {}

    def gather_first(self):
        return _gather_exchange([self.wts["w_in"][0].astype(bf16)] + [self.wts[n][0] for n in _CONVS])

    def finish_first(self, gathered):
        g_in, gc_a, gc_g, gc_f = gathered
        w1, w2 = _cols_to_matrices(g_in, _IN_RANGES, (NW1, 128), name="relay_w_in")
        return {"w1": w1, "w2": w2, "conv_a_w": gc_a.transpose(1, 0, 2).reshape(3, D),
                "gdn_conv_w": gc_g.transpose(1, 0, 2).reshape(4, 3 * D),
                "ffn_conv_w": gc_f.transpose(1, 0, 2).reshape(3, 2 * DFF)}

    def gather_rest(self):
        return _gather_direct_exchange([self.wts[n][0].astype(bf16) for n in _REST])

    def finish_gather(self, gathered):
        g_up, g_a, g_b, g_o, g_down = gathered
        return {"w_up": g_up.reshape(2 * DFF, D), "w_a_out": g_a.reshape(D, D), "w_b_out": g_b.reshape(D, D),
                "w_o": g_o.reshape(D, D), "w_down": g_down.reshape(DFF, D)}

    def reduce_halves(self, names, grads):
        blocks = []
        for n in names:
            if n == "w_in":
                g = _transposed_matrices_to_blocks([grads["w1"], grads["w2"]], _IN_RANGES, R_IN, name="relay_dw_in")
                blocks.append(g.reshape(4, 2, R_IN, D))
            else:
                blocks.append(grads[n].reshape(4, 2, *self.wts[n].shape[1:]))
        return _sibling_exchange([_half_bf16(g, 1 - self.c_me, name="rs_half_" + n) for n, g in zip(names, blocks)]), blocks

    def reduce_sums(self, names, blocks, recv):
        sums = [_pair_sum(g, r, self.c_me, name="rs_sum_" + n) for n, g, r in zip(names, blocks, recv)]
        return _chips_exchange([s[1] for s in sums]), [s[0] for s in sums]

    def finish_reduce(self, names, sums, recv):
        for n, s, r in zip(names, sums, recv):
            if n == "w_in":
                g = _sum_shard(s, r, self.chip, name="rs_total_w_in")[:, None, :]
                w, m, v = (jnp.transpose(t[n], (2, 0, 1)) for t in (self.wts, self.mom, self.var))
                res = (g, *_adam_columns(g, w, m, v, name="adam_w_in"))
                self.results[n] = tuple(jnp.transpose(a, (1, 2, 0)) for a in res)
            else:
                self.results[n] = _adam_shard(s, r, self.chip, self.wts[n], self.mom[n], self.var[n], name="adam_" + n)


def kernel(x, norm_mix_g, w_in, conv_a_w, gdn_conv_w, gdn_A_log, gdn_dt_bias, gdn_norm_g, w_a_out, w_b_out, w_o, norm_ffn_g, w_up, ffn_conv_w, w_down, norm_final_g, loss_target, m_norm_mix_g, m_w_in, m_conv_a_w, m_gdn_conv_w, m_gdn_A_log, m_gdn_dt_bias, m_gdn_norm_g, m_w_a_out, m_w_b_out, m_w_o, m_norm_ffn_g, m_w_up, m_ffn_conv_w, m_w_down, m_norm_final_g, v_norm_mix_g, v_w_in, v_conv_a_w, v_gdn_conv_w, v_gdn_A_log, v_gdn_dt_bias, v_gdn_norm_g, v_w_a_out, v_w_b_out, v_w_o, v_norm_ffn_g, v_w_up, v_ffn_conv_w, v_w_down, v_norm_final_g):
    wts = dict(zip(_WEIGHTS, (norm_mix_g, w_in, conv_a_w, gdn_conv_w, gdn_A_log, gdn_dt_bias, gdn_norm_g, w_a_out,
                              w_b_out, w_o, norm_ffn_g, w_up, ffn_conv_w, w_down, norm_final_g)))
    mom = dict(zip(_WEIGHTS, (m_norm_mix_g, m_w_in, m_conv_a_w, m_gdn_conv_w, m_gdn_A_log, m_gdn_dt_bias,
                              m_gdn_norm_g, m_w_a_out, m_w_b_out, m_w_o, m_norm_ffn_g, m_w_up, m_ffn_conv_w,
                              m_w_down, m_norm_final_g)))
    var = dict(zip(_WEIGHTS, (v_norm_mix_g, v_w_in, v_conv_a_w, v_gdn_conv_w, v_gdn_A_log, v_gdn_dt_bias,
                              v_gdn_norm_g, v_w_a_out, v_w_b_out, v_w_o, v_norm_ffn_g, v_w_up, v_ffn_conv_w,
                              v_w_down, v_norm_final_g)))
    cx, cy, cc = lax.axis_index("x"), lax.axis_index("y"), lax.axis_index("c")
    c_me = jnp.reshape(cc, (1,)).astype(jnp.int32)
    chip = jnp.reshape(2 * cx + cy, (1,)).astype(jnp.int32)
    me = jnp.reshape(4 * cx + 2 * cy + cc, (1,)).astype(jnp.int32)

    def with_up_transposed(t):
        return {**t, "w_up": jnp.swapaxes(t["w_up"], 1, 2)}

    comm = _StepExchanges(with_up_transposed(wts), with_up_transposed(mom), with_up_transposed(var), c_me, chip)
    replicated = {n: wts[n] for n in ("norm_mix_g", "norm_ffn_g", "norm_final_g", "gdn_norm_g", "gdn_A_log", "gdn_dt_bias")}
    loss_p, dx, grads = _local_step(x[0], loss_target[0], replicated, comm)
    res = comm.results
    res["w_up"] = tuple(jnp.swapaxes(a, 1, 2) for a in res["w_up"])

    small = _pack_small(grads["norm_mix_g"], grads["norm_ffn_g"], grads["norm_final_g"], grads["gdn_norm_g"],
                        grads["gdn_A_log"], grads["gdn_dt_bias"], loss_p, grads["conv_a_w"], grads["gdn_conv_w"],
                        grads["ffn_conv_w"])
    (small_all,) = _run_exchange(_gather_exchange([small]), name="ag_small")

    def raw(t):
        return {n: t[n].reshape(1, D) if n == "norm_final_g" else t[n] for n in _SMALL}

    res_small, loss = _adam_small(small_all, me, raw(wts), raw(mom), raw(var))
    for n in _SMALL:
        res[n] = tuple(a.reshape(wts[n].shape) for a in res_small[n])
    outs = [[res[n][i] for n in _WEIGHTS] for i in range(4)]
    return (loss.reshape(()), dx[None], *outs[0], *outs[1], *outs[2], *outs[3])
```

```python
import jax
import jax.numpy as jnp
from jax import lax
from jax.experimental import pallas as pl
from jax.experimental.pallas import tpu as pltpu

f32 = jnp.float32
bf16 = jnp.bfloat16

D = 1024
H = 8
DH = 128
CH = 64
GDN_STEP = 2
ROW_BLOCK = 512
ELEMENTWISE_BLOCK = 1024
DFF = 2816
NW1 = 9216
EPS = 1e-6
N_DEV = 8

ADAM_LR = 0.001
ADAM_B1 = 0.9
ADAM_B2 = 0.999
ADAM_EPS = 1e-08
ADAM_WD = 0.01
ADAM_STEP = 10

VMEM_LIMIT_BYTES = 48 * 1024 * 1024
VMEM_MAX_BYTES = 56 * 1024 * 1024

R_IN, R_UP = 1154, 704

_HI = lax.Precision.HIGHEST
MESH = pl.DeviceIdType.MESH


def _params(n_grid, vmem_bytes=None):
    return pltpu.CompilerParams(dimension_semantics=("arbitrary",) * n_grid,
                                vmem_limit_bytes=VMEM_LIMIT_BYTES if vmem_bytes is None else vmem_bytes)


def _vmem_for(*block_bytes, extra=0):
    need = 2 * sum(block_bytes) + extra + 4 * 1024 * 1024
    return min(max(need, VMEM_LIMIT_BYTES), VMEM_MAX_BYTES)


def _bdot(a, b):
    return jnp.dot(a.astype(bf16), b.astype(bf16), preferred_element_type=f32)


def _bdot_nt(a, b):
    return lax.dot_general(a.astype(bf16), b.astype(bf16), (((1,), (1,)), ((), ())), preferred_element_type=f32)


def _bdot_tn(a, b):
    return lax.dot_general(a.astype(bf16), b.astype(bf16), (((0,), (0,)), ((), ())), preferred_element_type=f32)


def _hdot(a, b):
    return jnp.dot(a, b, preferred_element_type=f32, precision=_HI)


def _idot(a, b):
    return jnp.dot(a, b, preferred_element_type=f32, precision=lax.Precision.HIGH)


def _sigmoid(x):
    return 1.0 / (1.0 + jnp.exp(-x))


def _softplus(x):
    return jnp.maximum(x, 0.0) + jnp.log(1.0 + jnp.exp(-jnp.abs(x)))


def _shift_down(x, halo, j):
    if j == 0:
        return x
    xr = pltpu.roll(x, j, 0)
    hr = pltpu.roll(halo, j, 0)
    r8 = lax.broadcasted_iota(jnp.int32, hr.shape, 0)
    top = jnp.where(r8 < j, hr, xr[:8])
    return jnp.concatenate([top, xr[8:]], axis=0)


def _shift_up(x, halo, j):
    if j == 0:
        return x
    n = x.shape[0]
    xr = pltpu.roll(x, n - j, 0)
    hr = pltpu.roll(halo, 8 - j, 0)
    r8 = lax.broadcasted_iota(jnp.int32, hr.shape, 0)
    bot = jnp.where(r8 >= 8 - j, hr, xr[n - 8:])
    return jnp.concatenate([xr[:n - 8], bot], axis=0)


def _taps_down(x, halo, k):
    return [_shift_down(x, halo, k - 1 - j) for j in range(k)]


def _strip(i, base=0):
    return slice(base + i * 128, base + (i + 1) * 128)


def _strip_taps(x, halo, first, k):
    return _taps_down(x, jnp.where(first, 0.0, halo), k)


def _strip_conv(w_ref, sl, taps):
    out = w_ref[0:1, sl] * taps[0]
    for j in range(1, len(taps)):
        out = out + w_ref[j:j + 1, sl] * taps[j]
    return out


def _strip_weight_grad(dw_ref, sl, dy, taps):
    for j, tap in enumerate(taps):
        dw_ref[j:j + 1, sl] += jnp.sum(dy * tap, axis=0, keepdims=True)


def _strip_conv_up(dy, halo, last, w_ref, sl, k):
    halo = jnp.where(last, 0.0, halo)
    out = w_ref[k - 1:k, sl] * dy
    for j in range(k - 1):
        out = out + w_ref[j:j + 1, sl] * _shift_up(dy, halo, k - 1 - j)
    return out


def _row(tb, w, col=0):
    return pl.BlockSpec((tb, w), lambda i: (i, col))


def _prev(tb, w, col=0, rows=8):
    return pl.BlockSpec((rows, w), lambda i: (jnp.maximum(i * (tb // rows) - 1, 0), col))


def _next(tb, w, n_rows, col=0, rows=8):
    last = n_rows // rows - 1
    return pl.BlockSpec((rows, w), lambda i: (jnp.minimum((i + 1) * (tb // rows), last), col))


def _f32(ref, sl):
    return ref[:, sl].astype(f32)


def _halo_before(ref, sl):
    h = _f32(ref, sl)
    return h[h.shape[0] - 8:]


def _halo_after(ref, sl):
    return _f32(ref, sl)[:8]


def _fixed(shape):
    return pl.BlockSpec(shape, lambda i: (0,) * len(shape))


def _pick(n, prefs):
    for p in prefs:
        if n % p == 0:
            return p
    return n


def _matmul(a, b, *, name, nt=False, add=None, tm=2048, tn=1024, tk=None, out_dtype=f32, cols=None, exchange=None):
    m, kd = a.shape
    col0, n = cols if cols is not None else (0, b.shape[0] if nt else b.shape[1])
    tm = _pick(m, (tm, 1024, 512, 256))
    tn = _pick(n, (tn, 1024, 512, 128))
    tk = kd if tk is None else tk
    nk = kd // tk
    assert nk == 1 or out_dtype == f32
    assert col0 % tn == 0 and not (nt and cols)
    j0 = col0 // tn
    dims = (((1,), (1,)), ((), ())) if nt else (((1,), (0,)), ((), ()))

    def body(a_ref, b_ref, *rest):
        o_ref = rest[-1]
        part = lax.dot_general(a_ref[...], b_ref[...], dims, preferred_element_type=f32)
        if nk == 1:
            o_ref[...] = (part if add is None else part + rest[0][...]).astype(out_dtype)
            return
        k = pl.program_id(2)

        @pl.when(k == 0)
        def _():
            o_ref[...] = part if add is None else part + rest[0][...]

        @pl.when(k > 0)
        def _():
            o_ref[...] += part

    resident = n == tn and nk == 1
    b_mode = dict(pipeline_mode=pl.Buffered(1)) if resident else {}
    b_spec = (pl.BlockSpec((tn, tk), lambda i, j, k: (j, k), **b_mode) if nt
              else pl.BlockSpec((tk, tn), lambda i, j, k: (k, j + j0), **b_mode))
    in_specs = [pl.BlockSpec((tm, tk), lambda i, j, k: (i, k)), b_spec]
    args = [a, b]
    if add is not None:
        in_specs.append(pl.BlockSpec((tm, tn), lambda i, j, k: (i, j)))
        args.append(add)
    vmem = _vmem_for(2 * tm * tk, (1 if resident else 2) * tk * tn, tm * tn * jnp.dtype(out_dtype).itemsize,
                     4 * tm * tn if add is not None else 0, extra=4 * tm * tn + (tk * tn if resident else 0))
    return _call_with_exchange(
        body, exchange, name=name, grid=(m // tm, n // tn, nk), in_specs=in_specs,
        out_specs=pl.BlockSpec((tm, tn), lambda i, j, k: (i, j)),
        out_shape=jax.ShapeDtypeStruct((m, n), out_dtype), args=args, vmem_bytes=vmem)


def _call_with_exchange(body, exchange, *, name, grid, in_specs, out_specs, out_shape, args, vmem_bytes=None):
    if exchange is None:
        return pl.pallas_call(body, name=name, grid=grid, in_specs=in_specs, out_specs=out_specs, out_shape=out_shape,
                              compiler_params=_params(len(grid), vmem_bytes))(*args)
    x_arrays, x_shapes, x_sems, start, wait = exchange[:5]
    n_in, n_xin, n_xout = len(args), len(x_arrays), len(x_shapes)
    aliases = {n_in + i: 1 + i for i in range(n_xin)} if len(exchange) > 5 and exchange[5] else {}

    def full_body(*refs):
        c_in, x_in = refs[:n_in], refs[n_in:n_in + n_xin]
        c_out = refs[n_in + n_xin]
        x_out = refs[n_in + n_xin + 1:n_in + n_xin + 1 + n_xout]
        sems = refs[n_in + n_xin + 1 + n_xout:]
        ids = [pl.program_id(d) for d in range(len(grid))]
        first, last = ids[0] == 0, ids[0] == grid[0] - 1
        for d in range(1, len(grid)):
            first = first & (ids[d] == 0)
            last = last & (ids[d] == grid[d] - 1)

        @pl.when(first)
        def _():
            start(x_in, x_out, sems)

        body(*c_in, c_out)

        @pl.when(last)
        def _():
            wait(x_in, x_out, sems)

    res = pl.pallas_call(
        full_body, name=name, grid=grid, in_specs=list(in_specs) + [_ANY] * n_xin,
        out_specs=[out_specs] + [_ANY] * n_xout, out_shape=[out_shape] + list(x_shapes),
        scratch_shapes=list(x_sems), input_output_aliases=aliases, compiler_params=_params(len(grid), vmem_bytes),
    )(*args, *x_arrays)
    return res[0], list(res[1:])


def _matmul_tn(a, b, *, name, tm=1024, tn=1024, tt=2048, exchange=None):
    t, m = a.shape
    _, n = b.shape
    tm = _pick(m, (tm, 1024, 512, 128))
    tn = _pick(n, (tn, 1024, 512, 128))
    tt = _pick(t, (tt, 2048, 1024, 512, 256))
    nt = t // tt

    def body(a_ref, b_ref, o_ref):
        k = pl.program_id(2)
        part = lax.dot_general(a_ref[...], b_ref[...], (((0,), (0,)), ((), ())), preferred_element_type=f32)

        @pl.when(k == 0)
        def _():
            o_ref[...] = part

        @pl.when(k > 0)
        def _():
            o_ref[...] += part

    return _call_with_exchange(
        body, exchange, name=name, grid=(m // tm, n // tn, nt),
        in_specs=[pl.BlockSpec((tt, tm), lambda i, j, k: (k, i)), pl.BlockSpec((tt, tn), lambda i, j, k: (k, j))],
        out_specs=pl.BlockSpec((tm, tn), lambda i, j, k: (i, j)),
        out_shape=jax.ShapeDtypeStruct((m, n), f32), args=[a, b],
        vmem_bytes=_vmem_for(2 * tt * tm, 2 * tt * tn, 4 * tm * tn, extra=4 * tm * tn + 2 * tt * tm))


def _rms_fwd(x, g, *, name, exchange=None):
    t = x.shape[0]
    tb = _pick(t, (ELEMENTWISE_BLOCK, 256, 128))

    def body(x_ref, g_ref, h_ref):
        xv = x_ref[...]
        r = lax.rsqrt(jnp.mean(xv * xv, axis=-1, keepdims=True) + EPS)
        h_ref[...] = (xv * r * g_ref[...]).astype(bf16)

    return _call_with_exchange(
        body, exchange, name=name, grid=(t // tb,), in_specs=[_row(tb, D), _fixed((1, D))], out_specs=_row(tb, D),
        out_shape=jax.ShapeDtypeStruct((t, D), bf16), args=[x, g])


def _rms_bwd(dh, x, g, dres, *, name, more=None, bf16_copy=True):
    t = x.shape[0]
    tb = _pick(t, (ELEMENTWISE_BLOCK, 256, 128))

    def body(dh_ref, x_ref, g_ref, dres_ref, *rest):
        dx_ref, dg_ref = rest[-3 if bf16_copy else -2], rest[-1]
        xv = x_ref[...]
        r = lax.rsqrt(jnp.mean(xv * xv, axis=-1, keepdims=True) + EPS)
        xh = xv * r
        dy = dh_ref[...]
        if more is not None:
            dy = dy + lax.dot_general(rest[0][...], rest[1][...], (((1,), (1,)), ((), ())), preferred_element_type=f32)
        dyg = dy * g_ref[...]
        dx = dres_ref[...] + r * (dyg - xh * jnp.mean(dyg * xh, axis=-1, keepdims=True))
        dx_ref[...] = dx
        if bf16_copy:
            rest[-2][...] = dx.astype(bf16)

        @pl.when(pl.program_id(0) == 0)
        def _():
            dg_ref[...] = jnp.zeros_like(dg_ref)

        dg_ref[...] += jnp.sum((dy * xh).reshape(tb // 8, 8, D), axis=0)

    in_specs, args = [_row(tb, D), _row(tb, D), _fixed((1, D)), _row(tb, D)], [dh, x, g, dres]
    if more is not None:
        in_specs += [_row(tb, 128), _fixed(more[1].shape)]
        args += list(more)
    dx_dtypes = (f32, bf16) if bf16_copy else (f32,)
    return pl.pallas_call(
        body, name=name, grid=(t // tb,), in_specs=in_specs,
        out_specs=[_row(tb, D) for _ in dx_dtypes] + [_fixed((8, D))],
        out_shape=[jax.ShapeDtypeStruct((t, D), dt) for dt in dx_dtypes] + [jax.ShapeDtypeStruct((8, D), f32)],
        compiler_params=_params(1),
    )(*args)


def _gdn_gates(ab, alog, dtb):
    lane = lax.broadcasted_iota(jnp.int32, ab.shape, 1)
    g = -jnp.exp(alog) * _softplus(ab + dtb)
    beta = _sigmoid(ab)
    return jnp.where(lane < H, g, jnp.where(lane < 2 * H, beta, 0.0))


def _pre_fwd(pg, pq, h1, w2, wa, wg, alog, dtb):
    t = pg.shape[0]
    tb = _pick(t, (ROW_BLOCK // 2, 128))

    def body(p0_ref, p0h_ref, pq_ref, pqh_ref, h1_ref, w2_ref, wa_ref, wg_ref, alog_ref, dtb_ref,
             ya_ref, qn_ref, kn_ref, vc_ref, gb_ref, p2_ref):
        first = pl.program_id(0) == 0
        p2_ref[...] = jnp.dot(h1_ref[...], w2_ref[...], preferred_element_type=f32)
        for i in range(D // 128):
            sl, cg, xv = _strip(i), _strip(i, D), _strip(i, 2 * D)
            taps = _strip_taps(_f32(p0_ref, cg) * _f32(p0_ref, xv), _halo_before(p0h_ref, cg) * _halo_before(p0h_ref, xv),
                               first, 3)
            ya_ref[:, sl] = (_f32(p0_ref, sl) * _strip_conv(wa_ref, sl, taps)).astype(bf16)
        for part, out_ref, scale in ((0, qn_ref, DH ** -0.5), (1, kn_ref, 1.0), (2, vc_ref, None)):
            for h in range(H):
                sl = _strip(h, part * D)
                s = _strip_conv(wg_ref, sl, _strip_taps(pq_ref[:, sl], pqh_ref[:, sl], first, 4))
                s = s * _sigmoid(s)
                if scale is not None:
                    s = s * (lax.rsqrt(jnp.sum(s * s, axis=-1, keepdims=True) + EPS) * scale)
                out_ref[:, _strip(h)] = s
        gb_ref[...] = _gdn_gates(p2_ref[...], alog_ref[...], dtb_ref[...])

    return pl.pallas_call(
        body, name="pre_fwd", grid=(t // tb,),
        in_specs=[_row(tb, 3 * D, 0), _prev(tb, 3 * D, 0, rows=16), _row(tb, 3 * D), _prev(tb, 3 * D), _row(tb, D),
                  _fixed((D, 128)), _fixed((8, D)), _fixed((8, 3 * D)), _fixed((1, 128)), _fixed((1, 128))],
        out_specs=[_row(tb, D), _row(tb, D), _row(tb, D), _row(tb, D), _row(tb, 128), _row(tb, 128)],
        out_shape=[jax.ShapeDtypeStruct((t, D), bf16), jax.ShapeDtypeStruct((t, D), f32),
                   jax.ShapeDtypeStruct((t, D), f32), jax.ShapeDtypeStruct((t, D), f32),
                   jax.ShapeDtypeStruct((t, 128), f32), jax.ShapeDtypeStruct((t, 128), f32)],
        compiler_params=_params(1),
    )(pg, pg, pq, pq, h1, w2, wa, wg, alog, dtb)


_Z_COL, _GA_COL, _GB_COL = 3, 4, 5


def _post_fwd(o, pg, gn):
    t = o.shape[0]
    tb = _pick(t, (ELEMENTWISE_BLOCK, 256, 128))

    def body(o_ref, z_ref, gn_ref, yb_ref):
        for h in range(H):
            sl = slice(h * DH, (h + 1) * DH)
            oh = o_ref[:, sl]
            z = _f32(z_ref, sl)
            r = lax.rsqrt(jnp.mean(oh * oh, axis=-1, keepdims=True) + EPS)
            yb_ref[:, sl] = (oh * r * gn_ref[...] * (z * _sigmoid(z))).astype(bf16)

    return pl.pallas_call(
        body, name="post_fwd", grid=(t // tb,), in_specs=[_row(tb, D), _row(tb, D, _Z_COL), _fixed((1, DH))],
        out_specs=_row(tb, D), out_shape=jax.ShapeDtypeStruct((t, D), bf16), compiler_params=_params(1),
    )(o, pg, gn)


def _post_bwd(dyb, o, pg, gn):
    t = o.shape[0]
    tb = _pick(t, (ELEMENTWISE_BLOCK, 256, 128))

    def body(dyb_ref, o_ref, z_ref, gn_ref, do_ref, dz_ref, dgn_ref):
        @pl.when(pl.program_id(0) == 0)
        def _():
            dgn_ref[...] = jnp.zeros_like(dgn_ref)

        gn_v = gn_ref[...]
        acc = jnp.zeros((8, DH), f32)
        for h in range(H):
            sl = slice(h * DH, (h + 1) * DH)
            oh = o_ref[:, sl]
            z = _f32(z_ref, sl)
            dy = dyb_ref[:, sl]
            r = lax.rsqrt(jnp.mean(oh * oh, axis=-1, keepdims=True) + EPS)
            on = oh * r
            sg = _sigmoid(z)
            sz = z * sg
            don = dy * sz
            dz_ref[:, sl] = (dy * on * gn_v * (sg * (1.0 + z * (1.0 - sg)))).astype(bf16)
            acc = acc + jnp.sum((don * on).reshape(tb // 8, 8, DH), axis=0)
            doh = don * gn_v
            do_ref[:, sl] = r * (doh - on * jnp.mean(doh * on, axis=-1, keepdims=True))
        dgn_ref[...] += acc

    return pl.pallas_call(
        body, name="post_bwd", grid=(t // tb,),
        in_specs=[_row(tb, D), _row(tb, D), _row(tb, D, _Z_COL), _fixed((1, DH))],
        out_specs=[_row(tb, D), _row(tb, D), _fixed((8, DH))],
        out_shape=[jax.ShapeDtypeStruct((t, D), f32), jax.ShapeDtypeStruct((t, D), bf16),
                   jax.ShapeDtypeStruct((8, DH), f32)],
        compiler_params=_params(1),
    )(dyb, o, pg, gn)


def _mix_fwd(ya, yb, pg):
    t = ya.shape[0]
    tb = _pick(t, (ELEMENTWISE_BLOCK, 256, 128))

    def body(ya_ref, yb_ref, ga_ref, gb_ref, mix_ref):
        ya_v, yb_v = ya_ref[...].astype(f32), yb_ref[...].astype(f32)
        mix = _sigmoid(ga_ref[...].astype(f32)) * ya_v + _sigmoid(gb_ref[...].astype(f32)) * yb_v
        mix_ref[...] = mix.astype(bf16)

    return pl.pallas_call(
        body, name="mix_fwd", grid=(t // tb,),
        in_specs=[_row(tb, D), _row(tb, D), _row(tb, D, _GA_COL), _row(tb, D, _GB_COL)],
        out_specs=_row(tb, D), out_shape=jax.ShapeDtypeStruct((t, D), bf16), compiler_params=_params(1),
    )(ya, yb, pg, pg)


def _mix_bwd(dmix, ya, yb, pg):
    t = ya.shape[0]
    tb = _pick(t, (ELEMENTWISE_BLOCK, 256, 128))

    def body(dm_ref, ya_ref, yb_ref, ga_ref, gb_ref, dya_ref, dyb_ref, dg_ref):
        dm = dm_ref[...].astype(f32)
        sa = _sigmoid(ga_ref[...].astype(f32))
        sb = _sigmoid(gb_ref[...].astype(f32))
        dya_ref[...] = (dm * sa).astype(bf16)
        dyb_ref[...] = (dm * sb).astype(bf16)
        dg_ref[:, :D] = (dm * ya_ref[...].astype(f32) * sa * (1.0 - sa)).astype(bf16)
        dg_ref[:, D:] = (dm * yb_ref[...].astype(f32) * sb * (1.0 - sb)).astype(bf16)

    return pl.pallas_call(
        body, name="mix_bwd", grid=(t // tb,),
        in_specs=[_row(tb, D), _row(tb, D), _row(tb, D), _row(tb, D, _GA_COL), _row(tb, D, _GB_COL)],
        out_specs=[_row(tb, D), _row(tb, D), _row(tb, 2 * D)],
        out_shape=[jax.ShapeDtypeStruct((t, D), bf16), jax.ShapeDtypeStruct((t, D), bf16),
                   jax.ShapeDtypeStruct((t, 2 * D), bf16)],
        compiler_params=_params(1),
    )(dmix, ya, yb, pg, pg)


def _ffn_fwd(up, wf):
    t = up.shape[0]
    tb = _pick(t, (ROW_BLOCK, 128))

    def body(up_ref, uph_ref, wf_ref, act_ref):
        first = pl.program_id(0) == 0
        for i in range(DFF // 128):
            g, v = _strip(i), _strip(i, DFF)
            gate = _strip_conv(wf_ref, g, _strip_taps(_f32(up_ref, g), _halo_before(uph_ref, g), first, 3))
            val = _strip_conv(wf_ref, v, _strip_taps(_f32(up_ref, v), _halo_before(uph_ref, v), first, 3))
            act_ref[:, g] = (gate * _sigmoid(gate) * val).astype(bf16)

    return pl.pallas_call(
        body, name="ffn_fwd", grid=(t // tb,),
        in_specs=[_row(tb, 2 * DFF), _prev(tb, 2 * DFF, rows=16), _fixed((8, 2 * DFF))],
        out_specs=_row(tb, DFF), out_shape=jax.ShapeDtypeStruct((t, DFF), bf16), compiler_params=_params(1),
    )(up, up, wf)


def _ffn_bwd1(dact, up, wf):
    t = up.shape[0]
    tb = _pick(t, (ROW_BLOCK, 128))

    def body(da_ref, up_ref, uph_ref, wf_ref, dc_ref, dw_ref):
        @pl.when(pl.program_id(0) == 0)
        def _():
            dw_ref[...] = jnp.zeros_like(dw_ref)

        first = pl.program_id(0) == 0
        for i in range(DFF // 128):
            g, v = _strip(i), _strip(i, DFF)
            g_taps = _strip_taps(_f32(up_ref, g), _halo_before(uph_ref, g), first, 3)
            v_taps = _strip_taps(_f32(up_ref, v), _halo_before(uph_ref, v), first, 3)
            gate = _strip_conv(wf_ref, g, g_taps)
            val = _strip_conv(wf_ref, v, v_taps)
            sg = _sigmoid(gate)
            da = _f32(da_ref, g)
            dgate = da * val * (sg * (1.0 + gate * (1.0 - sg)))
            dval = da * (gate * sg)
            dc_ref[:, g] = dgate.astype(bf16)
            dc_ref[:, v] = dval.astype(bf16)
            _strip_weight_grad(dw_ref, g, dgate, g_taps)
            _strip_weight_grad(dw_ref, v, dval, v_taps)

    return pl.pallas_call(
        body, name="ffn_bwd1", grid=(t // tb,),
        in_specs=[_row(tb, DFF), _row(tb, 2 * DFF), _prev(tb, 2 * DFF, rows=16), _fixed((8, 2 * DFF))],
        out_specs=[_row(tb, 2 * DFF), _fixed((8, 2 * DFF))],
        out_shape=[jax.ShapeDtypeStruct((t, 2 * DFF), bf16), jax.ShapeDtypeStruct((8, 2 * DFF), f32)],
        compiler_params=_params(1),
    )(dact, up, up, wf)


def _ffn_bwd2(dc, wf):
    t = dc.shape[0]
    tb = _pick(t, (ROW_BLOCK, 128))
    nb = t // tb

    def body(dc_ref, dch_ref, wf_ref, dup_ref):
        last = pl.program_id(0) == nb - 1
        for i in range(2 * DFF // 128):
            sl = _strip(i)
            dup_ref[:, sl] = _strip_conv_up(_f32(dc_ref, sl), _halo_after(dch_ref, sl), last, wf_ref, sl, 3).astype(bf16)

    return pl.pallas_call(
        body, name="ffn_bwd2", grid=(nb,),
        in_specs=[_row(tb, 2 * DFF), _next(tb, 2 * DFF, t, rows=16), _fixed((8, 2 * DFF))],
        out_specs=_row(tb, 2 * DFF), out_shape=jax.ShapeDtypeStruct((t, 2 * DFF), bf16), compiler_params=_params(1),
    )(dc, dc, wf)


def _final(x3, tgt, g):
    t = x3.shape[0]
    tb = _pick(t, (ELEMENTWISE_BLOCK, 256, 128))

    def body(x_ref, t_ref, g_ref, loss_ref, dx_ref, dxb_ref, dg_ref):
        @pl.when(pl.program_id(0) == 0)
        def _():
            loss_ref[...] = jnp.zeros_like(loss_ref)
            dg_ref[...] = jnp.zeros_like(dg_ref)

        xv = x_ref[...]
        r = lax.rsqrt(jnp.mean(xv * xv, axis=-1, keepdims=True) + EPS)
        xh = xv * r
        gv = g_ref[...]
        e = xh * gv - t_ref[...]
        lrow = 0.5 * jnp.mean(e * e, axis=-1, keepdims=True)
        loss_ref[...] += jnp.sum(jnp.broadcast_to(lrow, (tb, 128)).reshape(tb // 8, 8, 128), axis=0)
        dy = e * (1.0 / D)
        dyg = dy * gv
        dx = r * (dyg - xh * jnp.mean(dyg * xh, axis=-1, keepdims=True))
        dx_ref[...] = dx
        dxb_ref[...] = dx.astype(bf16)
        dg_ref[...] += jnp.sum((dy * xh).reshape(tb // 8, 8, D), axis=0)

    return pl.pallas_call(
        body, name="final", grid=(t // tb,), in_specs=[_row(tb, D), _row(tb, D), _fixed((1, D))],
        out_specs=[_fixed((8, 128)), _row(tb, D), _row(tb, D), _fixed((8, D))],
        out_shape=[jax.ShapeDtypeStruct((8, 128), f32), jax.ShapeDtypeStruct((t, D), f32),
                   jax.ShapeDtypeStruct((t, D), bf16), jax.ShapeDtypeStruct((8, D), f32)],
        compiler_params=_params(1),
    )(x3, tgt, g)


def _pre_bwd1(pg, pq, p2, dya_in, dqn, dkn, dvc, dgb, gbeta, h1, wa, wg, alog, dtb):
    t = pg.shape[0]
    tb = _pick(t, (ROW_BLOCK // 2, 128))

    def body(p0_ref, p0h_ref, pq_ref, pqh_ref, p2_ref, dya_ref, dqn_ref, dkn_ref, dvc_ref, dgb_ref, gb_ref, h1_ref,
             wa_ref, wg_ref, alog_ref, dtb_ref,
             dbg_ref, dca_ref, dc4_ref, dp2_ref, dwa_ref, dwg_ref, dal_ref, ddt_ref, dw2_ref):
        @pl.when(pl.program_id(0) == 0)
        def _():
            dwa_ref[...] = jnp.zeros_like(dwa_ref)
            dwg_ref[...] = jnp.zeros_like(dwg_ref)
            dal_ref[...] = jnp.zeros_like(dal_ref)
            ddt_ref[...] = jnp.zeros_like(ddt_ref)
            dw2_ref[...] = jnp.zeros_like(dw2_ref)

        first = pl.program_id(0) == 0

        for i in range(D // 128):
            sl, cg, xv = _strip(i), _strip(i, D), _strip(i, 2 * D)
            taps = _strip_taps(_f32(p0_ref, cg) * _f32(p0_ref, xv), _halo_before(p0h_ref, cg) * _halo_before(p0h_ref, xv),
                               first, 3)
            dya = _f32(dya_ref, sl)
            dbg_ref[:, sl] = (dya * _strip_conv(wa_ref, sl, taps)).astype(bf16)
            dca = dya * _f32(p0_ref, sl)
            dca_ref[:, sl] = dca.astype(bf16)
            _strip_weight_grad(dwa_ref, sl, dca, taps)

        for part, d_ref, scale in ((0, dqn_ref, DH ** -0.5), (1, dkn_ref, 1.0), (2, dvc_ref, None)):
            for h in range(H):
                sl = _strip(h, part * D)
                taps = _strip_taps(pq_ref[:, sl], pqh_ref[:, sl], first, 4)
                c4 = _strip_conv(wg_ref, sl, taps)
                sg = _sigmoid(c4)
                dn = d_ref[:, _strip(h)]
                if scale is not None:
                    a = c4 * sg
                    r = lax.rsqrt(jnp.sum(a * a, axis=-1, keepdims=True) + EPS)
                    an = a * r
                    dn = dn * scale
                    dn = r * (dn - an * jnp.sum(dn * an, axis=-1, keepdims=True))
                dc4 = dn * (sg * (1.0 + c4 * (1.0 - sg)))
                dc4_ref[:, sl] = dc4.astype(bf16)
                _strip_weight_grad(dwg_ref, sl, dc4, taps)

        ab = p2_ref[...]
        lane = lax.broadcasted_iota(jnp.int32, ab.shape, 1)
        dgbv = dgb_ref[...]
        gbv = gb_ref[...]
        da = dgbv * (-jnp.exp(alog_ref[...])) * _sigmoid(ab + dtb_ref[...])
        db = dgbv * gbv * (1.0 - gbv)
        dp2 = jnp.where(lane < H, da, jnp.where(lane < 2 * H, db, 0.0)).astype(bf16)
        dp2_ref[...] = dp2
        dw2_ref[...] += lax.dot_general(dp2, h1_ref[...], (((0,), (0,)), ((), ())), preferred_element_type=f32)
        dal = jnp.where(lane < H, dgbv * gbv, 0.0)
        ddt = jnp.where(lane < H, da, 0.0)
        dal_ref[...] += jnp.sum(dal.reshape(tb // 8, 8, 128), axis=0)
        ddt_ref[...] += jnp.sum(ddt.reshape(tb // 8, 8, 128), axis=0)

    return pl.pallas_call(
        body, name="pre_bwd1", grid=(t // tb,),
        in_specs=[_row(tb, 3 * D, 0), _prev(tb, 3 * D, 0, rows=16), _row(tb, 3 * D), _prev(tb, 3 * D), _row(tb, 128),
                  _row(tb, D), _row(tb, D), _row(tb, D), _row(tb, D), _row(tb, 128), _row(tb, 128), _row(tb, D),
                  _fixed((8, D)), _fixed((8, 3 * D)), _fixed((1, 128)), _fixed((1, 128))],
        out_specs=[_row(tb, D), _row(tb, D), _row(tb, 3 * D), _row(tb, 128),
                   _fixed((8, D)), _fixed((8, 3 * D)), _fixed((8, 128)), _fixed((8, 128)), _fixed((128, D))],
        out_shape=[jax.ShapeDtypeStruct((t, D), bf16), jax.ShapeDtypeStruct((t, D), bf16),
                   jax.ShapeDtypeStruct((t, 3 * D), bf16), jax.ShapeDtypeStruct((t, 128), bf16),
                   jax.ShapeDtypeStruct((8, D), f32), jax.ShapeDtypeStruct((8, 3 * D), f32),
                   jax.ShapeDtypeStruct((8, 128), f32), jax.ShapeDtypeStruct((8, 128), f32),
                   jax.ShapeDtypeStruct((128, D), f32)],
        compiler_params=_params(1),
    )(pg, pg, pq, pq, p2, dya_in, dqn, dkn, dvc, dgb, gbeta, h1, wa, wg, alog, dtb)


def _pre_bwd2(dca, dc4, pg, dbg, dz, dgates, wa, wg, exchange=None):
    t = pg.shape[0]
    tb = _pick(t, (ROW_BLOCK, 128))
    nb = t // tb

    def body(dca_ref, dcah_ref, dc4_ref, dc4h_ref, p0_ref, dbg_ref, dz_ref, dgt_ref, wa_ref, wg_ref, dp_ref):
        last = pl.program_id(0) == nb - 1
        dp_ref[:, :D] = dbg_ref[...]
        for i in range(D // 128):
            sl, cg, xv = _strip(i), _strip(i, D), _strip(i, 2 * D)
            du = _strip_conv_up(_f32(dca_ref, sl), _halo_after(dcah_ref, sl), last, wa_ref, sl, 3)
            dp_ref[:, cg] = (du * _f32(p0_ref, xv)).astype(bf16)
            dp_ref[:, xv] = (du * _f32(p0_ref, cg)).astype(bf16)
        dp_ref[:, 3 * D:4 * D] = dz_ref[...]
        dp_ref[:, 4 * D:6 * D] = dgt_ref[...]
        for i in range(3 * D // 128):
            sl = _strip(i)
            dq = _strip_conv_up(_f32(dc4_ref, sl), _halo_after(dc4h_ref, sl), last, wg_ref, sl, 4)
            dp_ref[:, _strip(i, 6 * D)] = dq.astype(bf16)

    return _call_with_exchange(
        body, exchange, name="pre_bwd2", grid=(nb,),
        in_specs=[_row(tb, D), _next(tb, D, t, rows=16), _row(tb, 3 * D), _next(tb, 3 * D, t, rows=16), _row(tb, 3 * D, 0),
                  _row(tb, D), _row(tb, D), _row(tb, 2 * D), _fixed((8, D)), _fixed((8, 3 * D))],
        out_specs=_row(tb, NW1), out_shape=jax.ShapeDtypeStruct((t, NW1), bf16),
        args=[dca, dca, dc4, dc4, pg, dbg, dz, dgates, wa, wg])


def _chunk_consts():
    r = lax.broadcasted_iota(jnp.int32, (CH, CH), 0)
    c = lax.broadcasted_iota(jnp.int32, (CH, CH), 1)
    return r, c, (r == c).astype(f32)


def _tri_inverse(lows, eye, r, c):
    def same_block(b):
        return jnp.bitwise_xor(r, c) < b

    xs = [jnp.where(same_block(8), -low, 0.0) for low in lows]
    ts = [eye + x for x in xs]
    for _ in range(2):
        xs = [_idot(x, x) for x in xs]
        ts = [t + _idot(t, x) for t, x in zip(ts, xs)]
    for b in (8, 16, 32):
        below = same_block(2 * b) & jnp.logical_not(same_block(b))
        ts = [t - _idot(_idot(t, jnp.where(below, low, 0.0)), t) for t, low in zip(ts, lows)]
    return ts


def _chunk_common(q, k, v, gcol, bcol, r, c, eye):
    grow = jnp.sum(eye * gcol, axis=0, keepdims=True)
    dec = jnp.exp(jnp.where(r >= c, gcol - grow, -jnp.inf))
    rcol = lax.broadcasted_iota(jnp.int32, (CH, 1), 0)
    glast = jnp.sum(jnp.where(rcol == CH - 1, gcol, 0.0), axis=0, keepdims=True)
    eg = jnp.exp(gcol)
    el = jnp.exp(glast - gcol)
    kb = k * bcol
    vb = v * bcol
    kk = _bdot_nt(kb, k)
    low = jnp.where(r > c, kk * dec, 0.0)
    qk = _bdot_nt(q, k)
    att = qk * dec
    return grow, dec, glast, eg, el, kb, vb, kk, low, qk, att, rcol


def _gdn_fwd(qn, kn, vc, gbeta):
    t = qn.shape[0]
    n_chunks = t // CH

    def body(q_ref, k_ref, v_ref, gb_ref, o_ref, s_ref, t_ref, state):
        @pl.when(pl.program_id(0) == 0)
        def _():
            state[...] = jnp.zeros_like(state)

        r, c, eye = _chunk_consts()
        tri = (r >= c).astype(f32)
        heads = range(H)
        keys = [(s, h) for s in range(GDN_STEP) for h in heads]
        rows = [slice(s * CH, (s + 1) * CH) for s in range(GDN_STEP)]
        gbs = [gb_ref[rows[s], :] for s in range(GDN_STEP)]
        galls = [_hdot(tri, gb) for gb in gbs]
        qs = {(s, h): q_ref[rows[s], h * DH:(h + 1) * DH] for s, h in keys}
        ks = {(s, h): k_ref[rows[s], h * DH:(h + 1) * DH] for s, h in keys}
        cm = {(s, h): _chunk_common(qs[s, h], ks[s, h], v_ref[rows[s], h * DH:(h + 1) * DH], galls[s][:, h:h + 1],
                                    gbs[s][:, H + h:H + h + 1], r, c, eye) for s, h in keys}
        invs = dict(zip(keys, _tri_inverse([cm[key][8] for key in keys], eye, r, c)))
        uws = {key: _bdot(invs[key], jnp.concatenate([cm[key][6], cm[key][5] * cm[key][3]], axis=1)) for key in keys}
        sts = [state[h] for h in heads]
        for s in range(GDN_STEP):
            vns = [uws[s, h][:, :DH] - _bdot(uws[s, h][:, DH:], sts[h]) for h in heads]
            outs = [_bdot(qs[s, h] * cm[s, h][3], sts[h]) + _bdot(cm[s, h][10], vns[h]) for h in heads]
            news = [sts[h] * jnp.exp(cm[s, h][2]) + _bdot_tn(ks[s, h] * cm[s, h][4], vns[h]) for h in heads]
            for h in heads:
                s_ref[s, h] = sts[h].astype(bf16)
                t_ref[s, h] = invs[s, h]
                o_ref[rows[s], h * DH:(h + 1) * DH] = outs[h]
            sts = news
        for h in heads:
            state[h] = sts[h]

    tb = GDN_STEP * CH
    return pl.pallas_call(
        body, name="gdn_fwd", grid=(t // tb,),
        in_specs=[_row(tb, D), _row(tb, D), _row(tb, D), _row(tb, 128)],
        out_specs=[_row(tb, D), pl.BlockSpec((GDN_STEP, H, DH, DH), lambda i: (i, 0, 0, 0)),
                   pl.BlockSpec((GDN_STEP, H, CH, CH), lambda i: (i, 0, 0, 0))],
        out_shape=[jax.ShapeDtypeStruct((t, D), f32), jax.ShapeDtypeStruct((n_chunks, H, DH, DH), bf16),
                   jax.ShapeDtypeStruct((n_chunks, H, CH, CH), f32)],
        scratch_shapes=[pltpu.VMEM((H, DH, DH), f32)],
        compiler_params=_params(1),
    )(qn, kn, vc, gbeta)


def _gdn_bwd(qn, kn, vc, gbeta, do, s_all, t_all):
    t = qn.shape[0]

    def body(q_ref, k_ref, v_ref, gb_ref, do_ref, s_ref, t_ref, dq_ref, dk_ref, dv_ref, dgb_ref, dstate):
        @pl.when(pl.program_id(0) == 0)
        def _():
            dstate[...] = jnp.zeros_like(dstate)

        r, c, eye = _chunk_consts()
        tril = r >= c
        lane = lax.broadcasted_iota(jnp.int32, (1, 128), 1)
        hs = range(H)

        def each(fn, *lists):
            return [fn(*args) for args in zip(*lists)]

        def rsum(a):
            return jnp.sum(a, axis=1, keepdims=True)

        def before_state(s):
            rows = slice(s * CH, (s + 1) * CH)
            gb = gb_ref[rows, :]
            gall = _hdot(tril.astype(f32), gb)
            p = {"rows": rows}
            p["q"] = q = [q_ref[rows, h * DH:(h + 1) * DH] for h in hs]
            p["k"] = k = [k_ref[rows, h * DH:(h + 1) * DH] for h in hs]
            p["v"] = v = [v_ref[rows, h * DH:(h + 1) * DH] for h in hs]
            p["dout"] = dout = [do_ref[rows, h * DH:(h + 1) * DH] for h in hs]
            p["inv"] = inv = [t_ref[s, h] for h in hs]
            p["st"] = st = [s_ref[s, h] for h in hs]
            p["bcol"] = bcol = [gb[:, H + h:H + h + 1] for h in hs]
            cm = [_chunk_common(q[h], k[h], v[h], gall[:, h:h + 1], bcol[h], r, c, eye) for h in hs]
            for name, i in (("dec", 1), ("glast", 2), ("eg", 3), ("el", 4), ("kb", 5), ("vb", 6), ("low", 8), ("att", 10)):
                p[name] = [m[i] for m in cm]
            p["rcol"] = cm[0][11]
            p["elast"] = each(jnp.exp, p["glast"])
            p["kbg"] = each(jnp.multiply, p["kb"], p["eg"])
            uw = each(lambda i, a, b: _bdot(i, jnp.concatenate([a, b], axis=1)), inv, p["vb"], p["kbg"])
            p["u"] = [a[:, :DH] for a in uw]
            p["w"] = [a[:, DH:] for a in uw]
            p["vn"] = each(lambda a, b, x: a - _bdot(b, x), p["u"], p["w"], st)
            p["qd"] = each(jnp.multiply, q, p["eg"])
            p["kd"] = each(jnp.multiply, k, p["el"])
            p["dqd"] = each(_bdot_nt, dout, st)
            p["datt"] = each(lambda d, x: jnp.where(tril, _bdot_nt(d, x), 0.0), dout, p["vn"])
            p["dqk"] = each(jnp.multiply, p["datt"], p["dec"])
            p["qd_do"] = each(_bdot_tn, p["qd"], dout)
            p["att_do"] = each(_bdot_tn, p["att"], dout)
            return p

        def after_state(p, ds):
            q, k, v, st, inv, bcol = p["q"], p["k"], p["v"], p["st"], p["inv"], p["bcol"]
            eg, el, kb, u, w = p["eg"], p["el"], p["kb"], p["u"], p["w"]
            dvn = each(lambda a, kk, x: a + _bdot(kk, x), p["att_do"], p["kd"], ds)
            dkd = each(_bdot_nt, p["vn"], ds)
            dw = each(lambda a, x: -_bdot_nt(a, x), dvn, st)
            new_ds = each(lambda x, e, a, ww, dv_: x * e + a - _bdot_tn(ww, dv_), ds, p["elast"], p["qd_do"], w, dvn)
            dglast = each(lambda e, x, d: e * jnp.sum(rsum(x.astype(f32) * d), axis=0, keepdims=True), p["elast"], st, ds)
            dr = each(lambda i, a, b: _bdot_tn(i, jnp.concatenate([a, b], axis=1)), inv, dvn, dw)
            dvb = [a[:, :DH] for a in dr]
            dkbg = [a[:, DH:] for a in dr]
            dlow = each(lambda a, b, x, y: -jnp.where(r > c, _bdot_nt(a, b) + _bdot_nt(x, y), 0.0), dvb, u, dkbg, w)
            dkk = each(jnp.multiply, dlow, p["dec"])
            mm = each(lambda a, b, x, y: a * b + x * y, dlow, p["low"], p["datt"], p["att"])
            dkb = each(lambda a, kk, b, e: _bdot(a, kk) + b * e, dkk, k, dkbg, eg)
            dk = each(lambda a, b, x, y, d, e, f, g: _bdot_tn(a, b) + _bdot_tn(x, y) + d * e + f * g,
                      dkk, kb, p["dqk"], q, dkd, el, dkb, bcol)
            dq = each(lambda a, kk, d, e: _bdot(a, kk) + d * e, p["dqk"], k, p["dqd"], eg)
            dv = each(jnp.multiply, dvb, bcol)
            dbeta = each(lambda a, b, x, y: rsum(a * b) + rsum(x * y), dkb, k, dvb, v)
            deg = each(lambda a, b, x, y: rsum(a * b) + rsum(x * y), dkbg, kb, p["dqd"], q)
            delc = each(lambda a, b, e: rsum(a * b) * e, dkd, k, el)
            dgc = each(lambda m, a, e, d: rsum(m) - rsum(eye * jnp.sum(m, axis=0, keepdims=True)) + a * e - d,
                       mm, deg, eg, delc)
            dgc = each(lambda g, d, l: g + jnp.where(p["rcol"] == CH - 1, jnp.sum(d, axis=0, keepdims=True) + l, 0.0),
                       dgc, delc, dglast)
            dg_acc = jnp.zeros((CH, 128), f32)
            db_acc = jnp.zeros((CH, 128), f32)
            rows = p["rows"]
            for h in hs:
                dq_ref[rows, h * DH:(h + 1) * DH] = dq[h]
                dk_ref[rows, h * DH:(h + 1) * DH] = dk[h]
                dv_ref[rows, h * DH:(h + 1) * DH] = dv[h]
                dg_acc = dg_acc + dgc[h] * (lane == h).astype(f32)
                db_acc = db_acc + dbeta[h] * (lane == H + h).astype(f32)
            dgb_ref[rows, :] = _hdot((r <= c).astype(f32), dg_acc) + db_acc
            return new_ds

        order = list(reversed(range(GDN_STEP)))
        pre = [before_state(s) for s in order]
        ds = [dstate[h] for h in hs]
        for p in pre:
            ds = after_state(p, ds)
        for h in hs:
            dstate[h] = ds[h]

    tb = GDN_STEP * CH
    n_steps = t // tb
    rev = lambda i: (n_steps - 1 - i, 0)
    rev4 = lambda i: (n_steps - 1 - i, 0, 0, 0)
    return pl.pallas_call(
        body, name="gdn_bwd", grid=(n_steps,),
        in_specs=[pl.BlockSpec((tb, D), rev), pl.BlockSpec((tb, D), rev), pl.BlockSpec((tb, D), rev),
                  pl.BlockSpec((tb, 128), rev), pl.BlockSpec((tb, D), rev),
                  pl.BlockSpec((GDN_STEP, H, DH, DH), rev4), pl.BlockSpec((GDN_STEP, H, CH, CH), rev4)],
        out_specs=[pl.BlockSpec((tb, D), rev), pl.BlockSpec((tb, D), rev), pl.BlockSpec((tb, D), rev),
                   pl.BlockSpec((tb, 128), rev)],
        out_shape=[jax.ShapeDtypeStruct((t, D), f32)] * 3 + [jax.ShapeDtypeStruct((t, 128), f32)],
        scratch_shapes=[pltpu.VMEM((H, DH, DH), f32)],
        compiler_params=_params(1),
    )(qn, kn, vc, gbeta, do, s_all, t_all)


def _pad_rows(w, rows=8):
    return jnp.pad(w, ((0, rows - w.shape[0]), (0, 0)))


_REST = ("w_up", "w_a_out", "w_b_out", "w_o", "w_down")


def _local_step(x, tgt, w, comm=None):
    g1 = w["norm_mix_g"].reshape(1, D)
    if comm is None:
        h1 = _rms_fwd(x, g1, name="rms1_fwd")
    else:
        h1, gathered = _rms_fwd(x, g1, name="rms1_fwd", exchange=comm.gather_first())
        w = {**w, **comm.finish_first(gathered)}
    w1, w2 = w["w1"], w["w2"]
    wa = _pad_rows(w["conv_a_w"])
    wg = _pad_rows(w["gdn_conv_w"])
    wf = _pad_rows(w["ffn_conv_w"])
    alog = jnp.pad(w["gdn_A_log"].reshape(1, H), ((0, 0), (0, 128 - H)))
    dtb = jnp.pad(w["gdn_dt_bias"].reshape(1, H), ((0, 0), (0, 128 - H)))
    g2 = w["norm_ffn_g"].reshape(1, D)
    g3 = w["norm_final_g"].reshape(1, D)
    gn = w["gdn_norm_g"].reshape(1, DH)

    if comm is None:
        pg = _matmul(h1, w1, name="mm_in", cols=(0, 6 * D), out_dtype=bf16)
        pq = _matmul(h1, w1, name="mm_in_qkv", cols=(6 * D, 3 * D))
    else:
        pg, gathered = _matmul(h1, w1, name="mm_in", cols=(0, 6 * D), out_dtype=bf16, exchange=comm.gather_rest())
        pq, gathered = _matmul(h1, w1, name="mm_in_qkv", cols=(6 * D, 3 * D), exchange=_gather_forward_exchange(gathered))
        w = {**w, **comm.finish_gather(gathered)}
    ya_in, qn, kn, vc, gbeta, p2 = _pre_fwd(pg, pq, h1, w2, wa, wg, alog, dtb)
    o, s_all, t_all = _gdn_fwd(qn, kn, vc, gbeta)
    yb_in = _post_fwd(o, pg, gn)
    ya = _matmul(ya_in, w["w_a_out"], name="mm_a", out_dtype=bf16)
    yb = _matmul(yb_in, w["w_b_out"], name="mm_b", out_dtype=bf16)
    mix = _mix_fwd(ya, yb, pg)
    x2 = _matmul(mix, w["w_o"], name="mm_o", add=x, tm=1024)
    h2 = _rms_fwd(x2, g2, name="rms2_fwd")
    up = _matmul(h2, w["w_up"], nt=True, name="mm_up", tn=DFF // 2, out_dtype=bf16)
    act = _ffn_fwd(up, wf)
    x3 = _matmul(act, w["w_down"], name="mm_down", add=x2, tm=512)
    loss_p, dx3, dx3b, dg3 = _final(x3, tgt, g3)

    grads = {"norm_final_g": dg3}
    dact = _matmul(dx3b, w["w_down"], nt=True, name="mm_down_dx", tm=512, tn=DFF, out_dtype=bf16)
    grads["w_down"] = _matmul_tn(act, dx3b, name="mm_down_dw", tm=DFF // 2)
    dc, dwf = _ffn_bwd1(dact, up, wf)
    grads["ffn_conv_w"] = dwf
    dup = _ffn_bwd2(dc, wf)
    dh2 = _matmul(dup, w["w_up"], name="mm_up_dx", tm=1024, tk=DFF)
    grads["w_up"] = _matmul_tn(dup, h2, name="mm_up_dw", tm=DFF // 2)
    dx2, dx2b, dg2 = _rms_bwd(dh2, x2, g2, dx3, name="rms2_bwd")
    grads["norm_ffn_g"] = dg2
    dmix = _matmul(dx2b, w["w_o"], nt=True, name="mm_o_dx", out_dtype=bf16)
    grads["w_o"] = _matmul_tn(mix, dx2b, name="mm_o_dw")
    dya, dyb, dgates = _mix_bwd(dmix, ya, yb, pg)
    dya_in = _matmul(dya, w["w_a_out"], nt=True, name="mm_a_dx", out_dtype=bf16)
    grads["w_a_out"] = _matmul_tn(ya_in, dya, name="mm_a_dw")
    dyb_in = _matmul(dyb, w["w_b_out"], nt=True, name="mm_b_dx")
    grads["w_b_out"] = _matmul_tn(yb_in, dyb, name="mm_b_dw")
    do, dz, dgn = _post_bwd(dyb_in, o, pg, gn)
    grads["gdn_norm_g"] = dgn
    dqn, dkn, dvc, dgb = _gdn_bwd(qn, kn, vc, gbeta, do, s_all, t_all)
    dbg, dca, dc4, dp2, dwa, dwg, dal, ddt, grads["w2"] = _pre_bwd1(pg, pq, p2, dya_in, dqn, dkn, dvc, dgb, gbeta, h1,
                                                                    wa, wg, alog, dtb)
    grads["conv_a_w"] = dwa
    grads["gdn_conv_w"] = dwg
    grads["gdn_A_log"] = dal
    grads["gdn_dt_bias"] = ddt
    if comm is None:
        dp1 = _pre_bwd2(dca, dc4, pg, dbg, dz, dgates, wa, wg)
        grads["w1"] = _matmul_tn(dp1, h1, name="mm_in_dw", tt=4096)
        dh1 = _matmul(dp1, w1, nt=True, name="mm_in_dx", tm=512)
    else:
        exchange, blocks = comm.reduce_halves(_REST, grads)
        dp1, recv = _pre_bwd2(dca, dc4, pg, dbg, dz, dgates, wa, wg, exchange=exchange)
        exchange, sums = comm.reduce_sums(_REST, blocks, recv)
        grads["w1"], recv = _matmul_tn(dp1, h1, name="mm_in_dw", tt=4096, exchange=exchange)
        comm.finish_reduce(_REST, sums, recv)
        exchange, blocks = comm.reduce_halves(("w_in",), grads)
        exchange, sums = comm.reduce_sums(("w_in",), blocks, _run_exchange(exchange, name="rs_sibling_w_in"))
        dh1, recv = _matmul(dp1, w1, nt=True, name="mm_in_dx", tm=512, exchange=exchange)
        comm.finish_reduce(("w_in",), sums, recv)
    dx, dg1 = _rms_bwd(dh1, x, g1, dx2, name="rms1_bwd", more=(dp2, w2), bf16_copy=False)
    grads["norm_mix_g"] = dg1
    return loss_p, dx, grads


_ANY = pl.BlockSpec(memory_space=pl.ANY)


def _remote(src, dst, send_sem, recv_sem, to):
    return pltpu.make_async_remote_copy(src_ref=src, dst_ref=dst, send_sem=send_sem, recv_sem=recv_sem,
                                        device_id=to, device_id_type=MESH)


def _run_exchange(exchange, *, name):
    arrays, shapes, sems, start, wait = exchange
    n_in, n_out = len(arrays), len(shapes)

    def body(*refs):
        start(refs[:n_in], refs[n_in:n_in + n_out], refs[n_in + n_out:])
        wait(refs[:n_in], refs[n_in:n_in + n_out], refs[n_in + n_out:])

    return pl.pallas_call(body, name=name, out_shape=list(shapes), in_specs=[_ANY] * n_in, out_specs=[_ANY] * n_out,
                          scratch_shapes=list(sems))(*arrays)


def _gather_exchange(shards):
    n = len(shards)

    def copies(x_refs, out_refs, sems):
        send_sems, recv_sems, local_sems = sems
        x, y, c = lax.axis_index("x"), lax.axis_index("y"), lax.axis_index("c")

        def flip(v, b):
            return v + b - 2 * v * b

        me, sibling = (x, y, c), (x, y, 1 - c)
        chip1, chip2, diag = (flip(x, 1 - c), flip(y, c)), (flip(x, c), flip(y, 1 - c)), (1 - x, 1 - y)

        def copy(a, k, blk, to, from_input=False):
            dst = out_refs[a].at[4 * blk[0] + 2 * blk[1] + blk[2]]
            return _remote(x_refs[a] if from_input else dst, dst, send_sems.at[a, k], recv_sems.at[a, k], to)

        mine = [pltpu.make_async_copy(x_refs[a], out_refs[a].at[4 * x + 2 * y + c], local_sems.at[a]) for a in range(n)]
        first = []
        for a in range(n):
            first += [copy(a, 0, me, sibling, from_input=True), copy(a, 1, me, (*chip1, c), from_input=True),
                      copy(a, 2, me, (*chip2, c), from_input=True)]
        return copy, mine, first, me, sibling, chip1, chip2, diag, c

    def start(x_refs, out_refs, sems):
        _, mine, first, *_ = copies(x_refs, out_refs, sems)
        for cp in mine + first:
            cp.start()

    def wait(x_refs, out_refs, sems):
        copy, mine, first, me, sibling, chip1, chip2, diag, c = copies(x_refs, out_refs, sems)
        passed = []

        def pass_on(cp):
            passed.append(cp)
            cp.start()

        for a in range(n):
            copy(a, 1, (*chip1, c), me).wait_recv()
            pass_on(copy(a, 3, (*chip1, c), (*chip2, c)))
            pass_on(copy(a, 4, (*chip1, c), sibling))
        for a in range(n):
            copy(a, 2, (*chip2, c), me).wait_recv()
            pass_on(copy(a, 5, (*chip2, c), sibling))
        for a in range(n):
            copy(a, 3, (*diag, c), me).wait_recv()
            pass_on(copy(a, 6, (*diag, c), sibling))
        for a in range(n):
            copy(a, 0, sibling, me).wait_recv()
            copy(a, 4, (*chip2, 1 - c), me).wait_recv()
            copy(a, 5, (*chip1, 1 - c), me).wait_recv()
            copy(a, 6, (*diag, 1 - c), me).wait_recv()
        for cp in first + passed:
            cp.wait_send()
        for cp in mine:
            cp.wait()

    shapes = [jax.ShapeDtypeStruct((N_DEV, *s.shape), s.dtype) for s in shards]
    sems = [pltpu.SemaphoreType.DMA((n, 7)), pltpu.SemaphoreType.DMA((n, 7)), pltpu.SemaphoreType.DMA((n,))]
    return shards, shapes, sems, start, wait


def _gather_direct_exchange(shards):
    n = len(shards)

    def copies(x_refs, out_refs, sems):
        send_sems, recv_sems, local_sems = sems
        x, y, c = lax.axis_index("x"), lax.axis_index("y"), lax.axis_index("c")
        targets = [(x, y, 1 - c), (1 - x, y, c), (x, 1 - y, c), (1 - x, 1 - y, c)]
        local, sends, recvs = [], [], []
        for a in range(n):
            mine = out_refs[a].at[4 * x + 2 * y + c]
            local.append(pltpu.make_async_copy(x_refs[a], mine, local_sems.at[a]))
            for k, to in enumerate(targets):
                theirs = out_refs[a].at[4 * to[0] + 2 * to[1] + to[2]]
                sends.append(_remote(x_refs[a], mine, send_sems.at[a, k], recv_sems.at[a, k], to))
                recvs.append(_remote(theirs, theirs, send_sems.at[a, k], recv_sems.at[a, k], to))
        return local, sends, recvs

    def start(x_refs, out_refs, sems):
        local, sends, _ = copies(x_refs, out_refs, sems)
        for cp in local + sends:
            cp.start()

    def wait(x_refs, out_refs, sems):
        local, sends, recvs = copies(x_refs, out_refs, sems)
        for cp in recvs:
            cp.wait_recv()
        for cp in sends:
            cp.wait_send()
        for cp in local:
            cp.wait()

    shapes = [jax.ShapeDtypeStruct((N_DEV, *s.shape), s.dtype) for s in shards]
    sems = [pltpu.SemaphoreType.DMA((n, 4)), pltpu.SemaphoreType.DMA((n, 4)), pltpu.SemaphoreType.DMA((n,))]
    return shards, shapes, sems, start, wait


def _gather_forward_exchange(gathered):
    n = len(gathered)

    def copies(_, out_refs, sems):
        send_sems, recv_sems = sems
        x, y, c = lax.axis_index("x"), lax.axis_index("y"), lax.axis_index("c")
        sibling = (x, y, 1 - c)
        sends, recvs = [], []
        for a in range(n):
            for j, (px, py) in enumerate([(1 - x, y), (x, 1 - y), (1 - x, 1 - y)]):
                mine = out_refs[a].at[4 * px + 2 * py + c]
                theirs = out_refs[a].at[4 * px + 2 * py + 1 - c]
                sends.append(_remote(mine, mine, send_sems.at[a, j], recv_sems.at[a, j], sibling))
                recvs.append(_remote(theirs, theirs, send_sems.at[a, j], recv_sems.at[a, j], sibling))
        return sends, recvs

    def start(in_refs, out_refs, sems):
        for cp in copies(in_refs, out_refs, sems)[0]:
            cp.start()

    def wait(in_refs, out_refs, sems):
        sends, recvs = copies(in_refs, out_refs, sems)
        for cp in recvs:
            cp.wait_recv()
        for cp in sends:
            cp.wait_send()

    shapes = [jax.ShapeDtypeStruct(g.shape, g.dtype) for g in gathered]
    sems = [pltpu.SemaphoreType.DMA((n, 3)), pltpu.SemaphoreType.DMA((n, 3))]
    return gathered, shapes, sems, start, wait, True


def _chips_exchange(hsums):
    n = len(hsums)

    def copies(h_refs, out_refs, sems):
        send_sems, recv_sems = sems
        x, y, c = lax.axis_index("x"), lax.axis_index("y"), lax.axis_index("c")
        chips = [(1 - x, y), (x, 1 - y), (1 - x, 1 - y)]
        return [_remote(h_refs[a].at[2 * px + py], out_refs[a].at[k], send_sems.at[a, k], recv_sems.at[a, k], (px, py, c))
                for a in range(n) for k, (px, py) in enumerate(chips)]

    def start(h_refs, out_refs, sems):
        for cp in copies(h_refs, out_refs, sems):
            cp.start()

    def wait(h_refs, out_refs, sems):
        for cp in copies(h_refs, out_refs, sems):
            cp.wait()

    shapes = [jax.ShapeDtypeStruct((3, *h.shape[1:]), h.dtype) for h in hsums]
    sems = [pltpu.SemaphoreType.DMA((n, 3)), pltpu.SemaphoreType.DMA((n, 3))]
    return hsums, shapes, sems, start, wait


def _sibling_exchange(halves):
    n = len(halves)

    def copies(p_refs, out_refs, sems):
        send_sems, recv_sems = sems
        x, y, c = lax.axis_index("x"), lax.axis_index("y"), lax.axis_index("c")
        return [_remote(p_refs[a], out_refs[a], send_sems.at[a], recv_sems.at[a], (x, y, 1 - c)) for a in range(n)]

    def start(p_refs, out_refs, sems):
        for cp in copies(p_refs, out_refs, sems):
            cp.start()

    def wait(p_refs, out_refs, sems):
        for cp in copies(p_refs, out_refs, sems):
            cp.wait()

    shapes = [jax.ShapeDtypeStruct(h.shape, h.dtype) for h in halves]
    return halves, shapes, [pltpu.SemaphoreType.DMA((n,)), pltpu.SemaphoreType.DMA((n,))], start, wait


_IN_RANGES = ((0, 3 * D, 0, 0), (3 * D, 6 * D, 0, 6 * D), (6 * D, 7 * D, 0, 3 * D), (7 * D, 7 * D + 16, 1, 0),
              (7 * D + 16, 9 * D + 16, 0, 4 * D))


def _col_pieces(width, ranges):
    pieces = []
    for d in range(N_DEV):
        lo, hi = d * width, (d + 1) * width
        for glo, ghi, mat, mlo in ranges:
            a, b = max(lo, glo), min(hi, ghi)
            if a < b:
                pieces.append((d, a - lo, b - lo, mat, mlo + a - glo))
    return pieces


def _cols_to_matrices(g, ranges, out_widths, *, name):
    _, rows, width = g.shape
    tb = 128
    pieces = _col_pieces(width, ranges)
    covered = [sum(p[2] - p[1] for p in pieces if p[3] == m) for m in range(len(out_widths))]

    def body(g_ref, *o_refs):
        for m, o_ref in enumerate(o_refs):
            if covered[m] < out_widths[m]:
                o_ref[...] = jnp.zeros_like(o_ref)
        for d, b0, b1, m, m0 in pieces:
            o_refs[m][:, m0:m0 + b1 - b0] = g_ref[d, :, b0:b1]

    return pl.pallas_call(
        body, name=name, grid=(rows // tb,), in_specs=[pl.BlockSpec((N_DEV, tb, width), lambda i: (0, i, 0))],
        out_specs=[pl.BlockSpec((tb, wo), lambda i: (i, 0)) for wo in out_widths],
        out_shape=[jax.ShapeDtypeStruct((rows, wo), g.dtype) for wo in out_widths], compiler_params=_params(1),
    )(g)


def _transposed_matrices_to_blocks(mats, ranges, width, *, name):
    rows = mats[0].shape[1]
    pieces = _col_pieces(width, ranges)

    def body(*refs):
        m_refs, g_ref = refs[:-1], refs[-1]
        for d, b0, b1, m, m0 in pieces:
            g_ref[d, b0:b1, :] = m_refs[m][m0:m0 + b1 - b0, :]

    return pl.pallas_call(
        body, name=name, grid=(rows // 128,),
        in_specs=[pl.BlockSpec((mt.shape[0], 128), lambda i: (0, i)) for mt in mats],
        out_specs=pl.BlockSpec((N_DEV, width, 128), lambda i: (0, 0, i)),
        out_shape=jax.ShapeDtypeStruct((N_DEV, width, rows), mats[0].dtype), compiler_params=_params(1),
    )(*mats)


def _row_block(rows):
    return 128 if rows % 128 == 0 else rows


def _half_bf16(g4, c_other, *, name):
    _, _, rows, width = g4.shape
    tb = _row_block(rows)

    def body(c_ref, p_ref, o_ref):
        o_ref[0] = p_ref[0, 0].astype(bf16)

    grid_spec = pltpu.PrefetchScalarGridSpec(
        num_scalar_prefetch=1, grid=(4, rows // tb),
        in_specs=[pl.BlockSpec((1, 1, tb, width), lambda j, i, c_ref: (j, c_ref[0], i, 0))],
        out_specs=pl.BlockSpec((1, tb, width), lambda j, i, c_ref: (j, i, 0)))
    return pl.pallas_call(
        body, name=name, grid_spec=grid_spec, out_shape=jax.ShapeDtypeStruct((4, rows, width), bf16),
        compiler_params=_params(2, _vmem_for(4 * tb * width, 2 * tb * width)),
    )(c_other, g4)


def _pair_sum(g4, recv, c_me, *, name):
    _, _, rows, width = g4.shape
    tb = _row_block(rows)

    def body(c_ref, p_ref, r_ref, o_ref, ob_ref):
        s = p_ref[0, 0] + r_ref[0].astype(f32)
        o_ref[0] = s
        ob_ref[0] = s.astype(bf16)

    blk = pl.BlockSpec((1, tb, width), lambda j, i, c_ref: (j, i, 0))
    grid_spec = pltpu.PrefetchScalarGridSpec(
        num_scalar_prefetch=1, grid=(4, rows // tb),
        in_specs=[pl.BlockSpec((1, 1, tb, width), lambda j, i, c_ref: (j, c_ref[0], i, 0)), blk],
        out_specs=[blk, blk])
    return pl.pallas_call(
        body, name=name, grid_spec=grid_spec,
        out_shape=[jax.ShapeDtypeStruct((4, rows, width), f32), jax.ShapeDtypeStruct((4, rows, width), bf16)],
        compiler_params=_params(2, _vmem_for(4 * tb * width, 2 * tb * width, 4 * tb * width, 2 * tb * width)),
    )(c_me, g4, recv)


def _adam_shard(hsum, recv, chip, w, m, v, *, name):
    _, rows, width = w.shape
    tb = _row_block(rows)

    def body(j_ref, h_ref, r_ref, w_ref, m_ref, v_ref, g_out, d_out, m_out, v_out):
        g = ((h_ref[0] + r_ref[0].astype(f32)) + r_ref[1].astype(f32)) + r_ref[2].astype(f32)
        delta, mn, vn = _adam_math(w_ref[0], g, m_ref[0], v_ref[0])
        g_out[0] = g
        d_out[0] = delta
        m_out[0] = mn
        v_out[0] = vn

    blk = pl.BlockSpec((1, tb, width), lambda i, j_ref: (0, i, 0))
    grid_spec = pltpu.PrefetchScalarGridSpec(
        num_scalar_prefetch=1, grid=(rows // tb,),
        in_specs=[pl.BlockSpec((1, tb, width), lambda i, j_ref: (j_ref[0], i, 0)),
                  pl.BlockSpec((3, tb, width), lambda i, j_ref: (0, i, 0)), blk, blk, blk],
        out_specs=[blk, blk, blk, blk])
    return pl.pallas_call(
        body, name=name, grid_spec=grid_spec, out_shape=[jax.ShapeDtypeStruct(w.shape, f32)] * 4,
        compiler_params=_params(1, _vmem_for(*[4 * tb * width] * 8, 6 * tb * width)),
    )(chip, hsum, recv, w, m, v)


def _sum_shard(hsum, recv, chip, *, name):
    _, rows, width = hsum.shape
    tb = _row_block(rows)

    def body(j_ref, h_ref, r_ref, g_out):
        g_out[...] = ((h_ref[0] + r_ref[0].astype(f32)) + r_ref[1].astype(f32)) + r_ref[2].astype(f32)

    grid_spec = pltpu.PrefetchScalarGridSpec(
        num_scalar_prefetch=1, grid=(rows // tb,),
        in_specs=[pl.BlockSpec((1, tb, width), lambda i, j_ref: (j_ref[0], i, 0)),
                  pl.BlockSpec((3, tb, width), lambda i, j_ref: (0, i, 0))],
        out_specs=pl.BlockSpec((tb, width), lambda i, j_ref: (i, 0)))
    return pl.pallas_call(body, name=name, grid_spec=grid_spec, out_shape=jax.ShapeDtypeStruct((rows, width), f32),
                          compiler_params=_params(1, _vmem_for(*[4 * tb * width] * 2, 6 * tb * width)))(chip, hsum, recv)


def _adam_columns(g, w, m, v, *, name):
    cols, _, rows = w.shape
    tb = cols // 2

    def body(g_ref, w_ref, m_ref, v_ref, d_out, m_out, v_out):
        delta, mn, vn = _adam_math(w_ref[...], g_ref[...], m_ref[...], v_ref[...])
        d_out[...] = delta
        m_out[...] = mn
        v_out[...] = vn

    blk = pl.BlockSpec((tb, 1, rows), lambda i: (i, 0, 0))
    return pl.pallas_call(
        body, name=name, grid=(cols // tb,), in_specs=[blk] * 4, out_specs=[blk] * 3,
        out_shape=[jax.ShapeDtypeStruct(w.shape, f32)] * 3,
        compiler_params=_params(1, _vmem_for(*[4 * tb * rows] * 7)),
    )(g, w, m, v)


R_SMALL = 8 + 8 * N_DEV
_SMALL_LANES = {"gdn_norm_g": (0, DH), "gdn_A_log": (DH, DH + H), "gdn_dt_bias": (2 * DH, 2 * DH + H)}
_LOSS_LANE = 3 * DH


def _pack_small(dg1, dg2, dg3, dgn, dal, ddt, loss_p, dwa, dwg, dwf):
    def body(dg1_ref, dg2_ref, dg3_ref, dgn_ref, dal_ref, ddt_ref, loss_ref, dwa_ref, dwg_ref, dwf_ref, o_ref):
        def total(ref):
            return jnp.sum(ref[...], axis=0, keepdims=True)

        o_ref[...] = jnp.zeros_like(o_ref)
        o_ref[0:1, :] = total(dg1_ref)
        o_ref[1:2, :] = total(dg2_ref)
        o_ref[2:3, :] = total(dg3_ref)
        o_ref[3:4, 0:DH] = total(dgn_ref)
        o_ref[3:4, DH:2 * DH] = total(dal_ref)
        o_ref[3:4, 2 * DH:3 * DH] = total(ddt_ref)
        o_ref[3:4, 3 * DH:4 * DH] = total(loss_ref)
        for d in range(N_DEV):
            base = 8 + 8 * d
            o_ref[base:base + 3, 0:128] = dwa_ref[0:3, 128 * d:128 * (d + 1)]
            o_ref[base:base + 4, 128:512] = dwg_ref[0:4, 384 * d:384 * (d + 1)]
            o_ref[base + 4:base + 7, 0:704] = dwf_ref[0:3, 704 * d:704 * (d + 1)]

    return pl.pallas_call(body, name="pack_small", out_shape=jax.ShapeDtypeStruct((R_SMALL, D), f32))(
        dg1, dg2, dg3, dgn, dal, ddt, loss_p, dwa, dwg, dwf)


_SMALL = ("norm_mix_g", "norm_ffn_g", "norm_final_g", "gdn_norm_g", "gdn_A_log", "gdn_dt_bias",
          "conv_a_w", "gdn_conv_w", "ffn_conv_w")


def _adam_small(gath, me, w, m, v):
    arrays = [t[n] for n in _SMALL for t in (w, m, v)]

    def body(me_ref, ga_ref, gb_ref, *refs):
        ins, outs = refs[:len(arrays)], refs[len(arrays):]
        ga, gb = ga_ref[0], gb_ref[0]
        for s in range(1, N_DEV):
            ga = ga + ga_ref[s]
            gb = gb + gb_ref[s]
        grads = {"norm_mix_g": ga[0:1, :], "norm_ffn_g": ga[1:2, :], "norm_final_g": ga[2:3, :],
                 "conv_a_w": gb[0:3, 0:128], "gdn_conv_w": gb[0:4, 128:512], "ffn_conv_w": gb[4:7, 0:704]}
        for n, (lo, hi) in _SMALL_LANES.items():
            grads[n] = ga[3:4, lo:hi]
        for i, n in enumerate(_SMALL):
            three_d = len(w[n].shape) == 3
            wv, mv, vv = (r[0] if three_d else r[...] for r in ins[3 * i:3 * i + 3])
            delta, mn, vn = _adam_math(wv, grads[n], mv, vv)
            for o_ref, val in zip(outs[4 * i:4 * i + 4], (grads[n], delta, mn, vn)):
                if three_d:
                    o_ref[0] = val
                else:
                    o_ref[...] = val
        outs[-1][...] = ga[3:4, _LOSS_LANE:_LOSS_LANE + 1]

    def whole(shape):
        return pl.BlockSpec(shape, lambda i, me_ref: (0,) * len(shape))

    grid_spec = pltpu.PrefetchScalarGridSpec(
        num_scalar_prefetch=1, grid=(1,),
        in_specs=[pl.BlockSpec((N_DEV, 8, D), lambda i, me_ref: (0, 0, 0)),
                  pl.BlockSpec((N_DEV, 8, D), lambda i, me_ref: (0, 1 + me_ref[0], 0))] + [whole(a.shape) for a in arrays],
        out_specs=[whole(w[n].shape) for n in _SMALL for _ in range(4)] + [whole((1, 1))])
    res = pl.pallas_call(
        body, name="adam_small", grid_spec=grid_spec,
        out_shape=[jax.ShapeDtypeStruct(w[n].shape, f32) for n in _SMALL for _ in range(4)]
        + [jax.ShapeDtypeStruct((1, 1), f32)],
        compiler_params=_params(1),
    )(me, gath, gath, *arrays)
    return {n: tuple(res[4 * i:4 * i + 4]) for i, n in enumerate(_SMALL)}, res[-1]


def _adam_math(w, g, m, v):
    m = ADAM_B1 * m + (1.0 - ADAM_B1) * g
    v = ADAM_B2 * v + (1.0 - ADAM_B2) * jnp.square(g)
    m_hat = m / (1.0 - ADAM_B1 ** ADAM_STEP)
    v_hat = v / (1.0 - ADAM_B2 ** ADAM_STEP)
    delta = -ADAM_LR * (m_hat / (jnp.sqrt(v_hat) + ADAM_EPS) + ADAM_WD * w)
    return delta, m, v


_WEIGHTS = ("norm_mix_g", "w_in", "conv_a_w", "gdn_conv_w", "gdn_A_log", "gdn_dt_bias", "gdn_norm_g", "w_a_out",
            "w_b_out", "w_o", "norm_ffn_g", "w_up", "ffn_conv_w", "w_down", "norm_final_g")
_CONVS = ("conv_a_w", "gdn_conv_w", "ffn_conv_w")


class _StepExchanges:
    def __init__(self, wts, mom, var, c_me, chip):
        self.wts, self.mom, self.var, self.c_me, self.chip = wts, mom, var, c_me, chip
        self.results = {}

    def gather_first(self):
        return _gather_exchange([self.wts["w_in"][0].astype(bf16)] + [self.wts[n][0] for n in _CONVS])

    def finish_first(self, gathered):
        g_in, gc_a, gc_g, gc_f = gathered
        w1, w2 = _cols_to_matrices(g_in, _IN_RANGES, (NW1, 128), name="relay_w_in")
        return {"w1": w1, "w2": w2, "conv_a_w": gc_a.transpose(1, 0, 2).reshape(3, D),
                "gdn_conv_w": gc_g.transpose(1, 0, 2).reshape(4, 3 * D),
                "ffn_conv_w": gc_f.transpose(1, 0, 2).reshape(3, 2 * DFF)}

    def gather_rest(self):
        return _gather_direct_exchange([self.wts[n][0].astype(bf16) for n in _REST])

    def finish_gather(self, gathered):
        g_up, g_a, g_b, g_o, g_down = gathered
        return {"w_up": g_up.reshape(2 * DFF, D), "w_a_out": g_a.reshape(D, D), "w_b_out": g_b.reshape(D, D),
                "w_o": g_o.reshape(D, D), "w_down": g_down.reshape(DFF, D)}

    def reduce_halves(self, names, grads):
        blocks = []
        for n in names:
            if n == "w_in":
                g = _transposed_matrices_to_blocks([grads["w1"], grads["w2"]], _IN_RANGES, R_IN, name="relay_dw_in")
                blocks.append(g.reshape(4, 2, R_IN, D))
            else:
                blocks.append(grads[n].reshape(4, 2, *self.wts[n].shape[1:]))
        return _sibling_exchange([_half_bf16(g, 1 - self.c_me, name="rs_half_" + n) for n, g in zip(names, blocks)]), blocks

    def reduce_sums(self, names, blocks, recv):
        sums = [_pair_sum(g, r, self.c_me, name="rs_sum_" + n) for n, g, r in zip(names, blocks, recv)]
        return _chips_exchange([s[1] for s in sums]), [s[0] for s in sums]

    def finish_reduce(self, names, sums, recv):
        for n, s, r in zip(names, sums, recv):
            if n == "w_in":
                g = _sum_shard(s, r, self.chip, name="rs_total_w_in")[:, None, :]
                w, m, v = (jnp.transpose(t[n], (2, 0, 1)) for t in (self.wts, self.mom, self.var))
                res = (g, *_adam_columns(g, w, m, v, name="adam_w_in"))
                self.results[n] = tuple(jnp.transpose(a, (1, 2, 0)) for a in res)
            else:
                self.results[n] = _adam_shard(s, r, self.chip, self.wts[n], self.mom[n], self.var[n], name="adam_" + n)


def kernel(x, norm_mix_g, w_in, conv_a_w, gdn_conv_w, gdn_A_log, gdn_dt_bias, gdn_norm_g, w_a_out, w_b_out, w_o, norm_ffn_g, w_up, ffn_conv_w, w_down, norm_final_g, loss_target, m_norm_mix_g, m_w_in, m_conv_a_w, m_gdn_conv_w, m_gdn_A_log, m_gdn_dt_bias, m_gdn_norm_g, m_w_a_out, m_w_b_out, m_w_o, m_norm_ffn_g, m_w_up, m_ffn_conv_w, m_w_down, m_norm_final_g, v_norm_mix_g, v_w_in, v_conv_a_w, v_gdn_conv_w, v_gdn_A_log, v_gdn_dt_bias, v_gdn_norm_g, v_w_a_out, v_w_b_out, v_w_o, v_norm_ffn_g, v_w_up, v_ffn_conv_w, v_w_down, v_norm_final_g):
    wts = dict(zip(_WEIGHTS, (norm_mix_g, w_in, conv_a_w, gdn_conv_w, gdn_A_log, gdn_dt_bias, gdn_norm_g, w_a_out,
                              w_b_out, w_o, norm_ffn_g, w_up, ffn_conv_w, w_down, norm_final_g)))
    mom = dict(zip(_WEIGHTS, (m_norm_mix_g, m_w_in, m_conv_a_w, m_gdn_conv_w, m_gdn_A_log, m_gdn_dt_bias,
                              m_gdn_norm_g, m_w_a_out, m_w_b_out, m_w_o, m_norm_ffn_g, m_w_up, m_ffn_conv_w,
                              m_w_down, m_norm_final_g)))
    var = dict(zip(_WEIGHTS, (v_norm_mix_g, v_w_in, v_conv_a_w, v_gdn_conv_w, v_gdn_A_log, v_gdn_dt_bias,
                              v_gdn_norm_g, v_w_a_out, v_w_b_out, v_w_o, v_norm_ffn_g, v_w_up, v_ffn_conv_w,
                              v_w_down, v_norm_final_g)))
    cx, cy, cc = lax.axis_index("x"), lax.axis_index("y"), lax.axis_index("c")
    c_me = jnp.reshape(cc, (1,)).astype(jnp.int32)
    chip = jnp.reshape(2 * cx + cy, (1,)).astype(jnp.int32)
    me = jnp.reshape(4 * cx + 2 * cy + cc, (1,)).astype(jnp.int32)

    def with_up_transposed(t):
        return {**t, "w_up": jnp.swapaxes(t["w_up"], 1, 2)}

    comm = _StepExchanges(with_up_transposed(wts), with_up_transposed(mom), with_up_transposed(var), c_me, chip)
    replicated = {n: wts[n] for n in ("norm_mix_g", "norm_ffn_g", "norm_final_g", "gdn_norm_g", "gdn_A_log", "gdn_dt_bias")}
    loss_p, dx, grads = _local_step(x[0], loss_target[0], replicated, comm)
    res = comm.results
    res["w_up"] = tuple(jnp.swapaxes(a, 1, 2) for a in res["w_up"])

    small = _pack_small(grads["norm_mix_g"], grads["norm_ffn_g"], grads["norm_final_g"], grads["gdn_norm_g"],
                        grads["gdn_A_log"], grads["gdn_dt_bias"], loss_p, grads["conv_a_w"], grads["gdn_conv_w"],
                        grads["ffn_conv_w"])
    (small_all,) = _run_exchange(_gather_exchange([small]), name="ag_small")

    def raw(t):
        return {n: t[n].reshape(1, D) if n == "norm_final_g" else t[n] for n in _SMALL}

    res_small, loss = _adam_small(small_all, me, raw(wts), raw(mom), raw(var))
    for n in _SMALL:
        res[n] = tuple(a.reshape(wts[n].shape) for a in res_small[n])
    outs = [[res[n][i] for n in _WEIGHTS] for i in range(4)]
    return (loss.reshape(()), dx[None], *outs[0], *outs[1], *outs[2], *outs[3])
```

```python
import jax
import jax.numpy as jnp
from jax import lax
from jax.experimental import pallas as pl
from jax.experimental.pallas import tpu as pltpu

f32 = jnp.float32
bf16 = jnp.bfloat16

D = 1024
H = 8
DH = 128
CH = 64
GDN_STEP = 2
ROW_BLOCK = 512
ELEMENTWISE_BLOCK = 1024
DFF = 2816
NW1 = 9216
EPS = 1e-6
N_DEV = 8

ADAM_LR = 0.001
ADAM_B1 = 0.9
ADAM_B2 = 0.999
ADAM_EPS = 1e-08
ADAM_WD = 0.01
ADAM_STEP = 10

VMEM_LIMIT_BYTES = 48 * 1024 * 1024
VMEM_MAX_BYTES = 56 * 1024 * 1024

R_IN, R_UP = 1154, 704

_HI = lax.Precision.HIGHEST
MESH = pl.DeviceIdType.MESH


def _params(n_grid, vmem_bytes=None):
    return pltpu.CompilerParams(dimension_semantics=("arbitrary",) * n_grid,
                                vmem_limit_bytes=VMEM_LIMIT_BYTES if vmem_bytes is None else vmem_bytes)


def _vmem_for(*block_bytes, extra=0):
    need = 2 * sum(block_bytes) + extra + 4 * 1024 * 1024
    return min(max(need, VMEM_LIMIT_BYTES), VMEM_MAX_BYTES)


def _bdot(a, b):
    return jnp.dot(a.astype(bf16), b.astype(bf16), preferred_element_type=f32)


def _bdot_nt(a, b):
    return lax.dot_general(a.astype(bf16), b.astype(bf16), (((1,), (1,)), ((), ())), preferred_element_type=f32)


def _bdot_tn(a, b):
    return lax.dot_general(a.astype(bf16), b.astype(bf16), (((0,), (0,)), ((), ())), preferred_element_type=f32)


def _hdot(a, b):
    return jnp.dot(a, b, preferred_element_type=f32, precision=_HI)


def _idot(a, b):
    return jnp.dot(a, b, preferred_element_type=f32, precision=lax.Precision.HIGH)


def _sigmoid(x):
    return 1.0 / (1.0 + jnp.exp(-x))


def _softplus(x):
    return jnp.maximum(x, 0.0) + jnp.log(1.0 + jnp.exp(-jnp.abs(x)))


def _shift_down(x, halo, j):
    if j == 0:
        return x
    xr = pltpu.roll(x, j, 0)
    hr = pltpu.roll(halo, j, 0)
    r8 = lax.broadcasted_iota(jnp.int32, hr.shape, 0)
    top = jnp.where(r8 < j, hr, xr[:8])
    return jnp.concatenate([top, xr[8:]], axis=0)


def _shift_up(x, halo, j):
    if j == 0:
        return x
    n = x.shape[0]
    xr = pltpu.roll(x, n - j, 0)
    hr = pltpu.roll(halo, 8 - j, 0)
    r8 = lax.broadcasted_iota(jnp.int32, hr.shape, 0)
    bot = jnp.where(r8 >= 8 - j, hr, xr[n - 8:])
    return jnp.concatenate([xr[:n - 8], bot], axis=0)


def _taps_down(x, halo, k):
    return [_shift_down(x, halo, k - 1 - j) for j in range(k)]


def _strip(i, base=0):
    return slice(base + i * 128, base + (i + 1) * 128)


def _strip_taps(x, halo, first, k):
    return _taps_down(x, jnp.where(first, 0.0, halo), k)


def _strip_conv(w_ref, sl, taps):
    out = w_ref[0:1, sl] * taps[0]
    for j in range(1, len(taps)):
        out = out + w_ref[j:j + 1, sl] * taps[j]
    return out


def _strip_weight_grad(dw_ref, sl, dy, taps):
    for j, tap in enumerate(taps):
        dw_ref[j:j + 1, sl] += jnp.sum(dy * tap, axis=0, keepdims=True)


def _strip_conv_up(dy, halo, last, w_ref, sl, k):
    halo = jnp.where(last, 0.0, halo)
    out = w_ref[k - 1:k, sl] * dy
    for j in range(k - 1):
        out = out + w_ref[j:j + 1, sl] * _shift_up(dy, halo, k - 1 - j)
    return out


def _row(tb, w, col=0):
    return pl.BlockSpec((tb, w), lambda i: (i, col))


def _prev(tb, w, col=0, rows=8):
    return pl.BlockSpec((rows, w), lambda i: (jnp.maximum(i * (tb // rows) - 1, 0), col))


def _next(tb, w, n_rows, col=0, rows=8):
    last = n_rows // rows - 1
    return pl.BlockSpec((rows, w), lambda i: (jnp.minimum((i + 1) * (tb // rows), last), col))


def _f32(ref, sl):
    return ref[:, sl].astype(f32)


def _halo_before(ref, sl):
    h = _f32(ref, sl)
    return h[h.shape[0] - 8:]


def _halo_after(ref, sl):
    return _f32(ref, sl)[:8]


def _fixed(shape):
    return pl.BlockSpec(shape, lambda i: (0,) * len(shape))


def _pick(n, prefs):
    for p in prefs:
        if n % p == 0:
            return p
    return n


def _matmul(a, b, *, name, nt=False, add=None, tm=2048, tn=1024, tk=None, out_dtype=f32, cols=None, exchange=None):
    m, kd = a.shape
    col0, n = cols if cols is not None else (0, b.shape[0] if nt else b.shape[1])
    tm = _pick(m, (tm, 1024, 512, 256))
    tn = _pick(n, (tn, 1024, 512, 128))
    tk = kd if tk is None else tk
    nk = kd // tk
    assert nk == 1 or out_dtype == f32
    assert col0 % tn == 0 and not (nt and cols)
    j0 = col0 // tn
    dims = (((1,), (1,)), ((), ())) if nt else (((1,), (0,)), ((), ()))

    def body(a_ref, b_ref, *rest):
        o_ref = rest[-1]
        part = lax.dot_general(a_ref[...], b_ref[...], dims, preferred_element_type=f32)
        if nk == 1:
            o_ref[...] = (part if add is None else part + rest[0][...]).astype(out_dtype)
            return
        k = pl.program_id(2)

        @pl.when(k == 0)
        def _():
            o_ref[...] = part if add is None else part + rest[0][...]

        @pl.when(k > 0)
        def _():
            o_ref[...] += part

    resident = n == tn and nk == 1
    b_mode = dict(pipeline_mode=pl.Buffered(1)) if resident else {}
    b_spec = (pl.BlockSpec((tn, tk), lambda i, j, k: (j, k), **b_mode) if nt
              else pl.BlockSpec((tk, tn), lambda i, j, k: (k, j + j0), **b_mode))
    in_specs = [pl.BlockSpec((tm, tk), lambda i, j, k: (i, k)), b_spec]
    args = [a, b]
    if add is not None:
        in_specs.append(pl.BlockSpec((tm, tn), lambda i, j, k: (i, j)))
        args.append(add)
    vmem = _vmem_for(2 * tm * tk, (1 if resident else 2) * tk * tn, tm * tn * jnp.dtype(out_dtype).itemsize,
                     4 * tm * tn if add is not None else 0, extra=4 * tm * tn + (tk * tn if resident else 0))
    return _call_with_exchange(
        body, exchange, name=name, grid=(m // tm, n // tn, nk), in_specs=in_specs,
        out_specs=pl.BlockSpec((tm, tn), lambda i, j, k: (i, j)),
        out_shape=jax.ShapeDtypeStruct((m, n), out_dtype), args=args, vmem_bytes=vmem)


def _call_with_exchange(body, exchange, *, name, grid, in_specs, out_specs, out_shape, args, vmem_bytes=None):
    if exchange is None:
        return pl.pallas_call(body, name=name, grid=grid, in_specs=in_specs, out_specs=out_specs, out_shape=out_shape,
                              compiler_params=_params(len(grid), vmem_bytes))(*args)
    x_arrays, x_shapes, x_sems, start, wait = exchange[:5]
    n_in, n_xin, n_xout = len(args), len(x_arrays), len(x_shapes)
    aliases = {n_in + i: 1 + i for i in range(n_xin)} if len(exchange) > 5 and exchange[5] else {}

    def full_body(*refs):
        c_in, x_in = refs[:n_in], refs[n_in:n_in + n_xin]
        c_out = refs[n_in + n_xin]
        x_out = refs[n_in + n_xin + 1:n_in + n_xin + 1 + n_xout]
        sems = refs[n_in + n_xin + 1 + n_xout:]
        ids = [pl.program_id(d) for d in range(len(grid))]
        first, last = ids[0] == 0, ids[0] == grid[0] - 1
        for d in range(1, len(grid)):
            first = first & (ids[d] == 0)
            last = last & (ids[d] == grid[d] - 1)

        @pl.when(first)
        def _():
            start(x_in, x_out, sems)

        body(*c_in, c_out)

        @pl.when(last)
        def _():
            wait(x_in, x_out, sems)

    res = pl.pallas_call(
        full_body, name=name, grid=grid, in_specs=list(in_specs) + [_ANY] * n_xin,
        out_specs=[out_specs] + [_ANY] * n_xout, out_shape=[out_shape] + list(x_shapes),
        scratch_shapes=list(x_sems), input_output_aliases=aliases, compiler_params=_params(len(grid), vmem_bytes),
    )(*args, *x_arrays)
    return res[0], list(res[1:])


def _matmul_tn(a, b, *, name, tm=1024, tn=1024, tt=2048, exchange=None):
    t, m = a.shape
    _, n = b.shape
    tm = _pick(m, (tm, 1024, 512, 128))
    tn = _pick(n, (tn, 1024, 512, 128))
    tt = _pick(t, (tt, 2048, 1024, 512, 256))
    nt = t // tt

    def body(a_ref, b_ref, o_ref):
        k = pl.program_id(2)
        part = lax.dot_general(a_ref[...], b_ref[...], (((0,), (0,)), ((), ())), preferred_element_type=f32)

        @pl.when(k == 0)
        def _():
            o_ref[...] = part

        @pl.when(k > 0)
        def _():
            o_ref[...] += part

    return _call_with_exchange(
        body, exchange, name=name, grid=(m // tm, n // tn, nt),
        in_specs=[pl.BlockSpec((tt, tm), lambda i, j, k: (k, i)), pl.BlockSpec((tt, tn), lambda i, j, k: (k, j))],
        out_specs=pl.BlockSpec((tm, tn), lambda i, j, k: (i, j)),
        out_shape=jax.ShapeDtypeStruct((m, n), f32), args=[a, b],
        vmem_bytes=_vmem_for(2 * tt * tm, 2 * tt * tn, 4 * tm * tn, extra=4 * tm * tn + 2 * tt * tm))


def _rms_fwd(x, g, *, name, exchange=None):
    t = x.shape[0]
    tb = _pick(t, (ELEMENTWISE_BLOCK, 256, 128))

    def body(x_ref, g_ref, h_ref):
        xv = x_ref[...]
        r = lax.rsqrt(jnp.mean(xv * xv, axis=-1, keepdims=True) + EPS)
        h_ref[...] = (xv * r * g_ref[...]).astype(bf16)

    return _call_with_exchange(
        body, exchange, name=name, grid=(t // tb,), in_specs=[_row(tb, D), _fixed((1, D))], out_specs=_row(tb, D),
        out_shape=jax.ShapeDtypeStruct((t, D), bf16), args=[x, g])


def _rms_bwd(dh, x, g, dres, *, name, more=None, bf16_copy=True):
    t = x.shape[0]
    tb = _pick(t, (ELEMENTWISE_BLOCK, 256, 128))

    def body(dh_ref, x_ref, g_ref, dres_ref, *rest):
        dx_ref, dg_ref = rest[-3 if bf16_copy else -2], rest[-1]
        xv = x_ref[...]
        r = lax.rsqrt(jnp.mean(xv * xv, axis=-1, keepdims=True) + EPS)
        xh = xv * r
        dy = dh_ref[...]
        if more is not None:
            dy = dy + lax.dot_general(rest[0][...], rest[1][...], (((1,), (1,)), ((), ())), preferred_element_type=f32)
        dyg = dy * g_ref[...]
        dx = dres_ref[...] + r * (dyg - xh * jnp.mean(dyg * xh, axis=-1, keepdims=True))
        dx_ref[...] = dx
        if bf16_copy:
            rest[-2][...] = dx.astype(bf16)

        @pl.when(pl.program_id(0) == 0)
        def _():
            dg_ref[...] = jnp.zeros_like(dg_ref)

        dg_ref[...] += jnp.sum((dy * xh).reshape(tb // 8, 8, D), axis=0)

    in_specs, args = [_row(tb, D), _row(tb, D), _fixed((1, D)), _row(tb, D)], [dh, x, g, dres]
    if more is not None:
        in_specs += [_row(tb, 128), _fixed(more[1].shape)]
        args += list(more)
    dx_dtypes = (f32, bf16) if bf16_copy else (f32,)
    return pl.pallas_call(
        body, name=name, grid=(t // tb,), in_specs=in_specs,
        out_specs=[_row(tb, D) for _ in dx_dtypes] + [_fixed((8, D))],
        out_shape=[jax.ShapeDtypeStruct((t, D), dt) for dt in dx_dtypes] + [jax.ShapeDtypeStruct((8, D), f32)],
        compiler_params=_params(1),
    )(*args)


def _gdn_gates(ab, alog, dtb):
    lane = lax.broadcasted_iota(jnp.int32, ab.shape, 1)
    g = -jnp.exp(alog) * _softplus(ab + dtb)
    beta = _sigmoid(ab)
    return jnp.where(lane < H, g, jnp.where(lane < 2 * H, beta, 0.0))


def _pre_fwd(pg, pq, h1, w2, wa, wg, alog, dtb):
    t = pg.shape[0]
    tb = _pick(t, (ROW_BLOCK // 2, 128))

    def body(p0_ref, p0h_ref, pq_ref, pqh_ref, h1_ref, w2_ref, wa_ref, wg_ref, alog_ref, dtb_ref,
             ya_ref, qn_ref, kn_ref, vc_ref, gb_ref, p2_ref):
        first = pl.program_id(0) == 0
        p2_ref[...] = jnp.dot(h1_ref[...], w2_ref[...], preferred_element_type=f32)
        for i in range(D // 128):
            sl, cg, xv = _strip(i), _strip(i, D), _strip(i, 2 * D)
            taps = _strip_taps(_f32(p0_ref, cg) * _f32(p0_ref, xv), _halo_before(p0h_ref, cg) * _halo_before(p0h_ref, xv),
                               first, 3)
            ya_ref[:, sl] = (_f32(p0_ref, sl) * _strip_conv(wa_ref, sl, taps)).astype(bf16)
        for part, out_ref, scale in ((0, qn_ref, DH ** -0.5), (1, kn_ref, 1.0), (2, vc_ref, None)):
            for h in range(H):
                sl = _strip(h, part * D)
                s = _strip_conv(wg_ref, sl, _strip_taps(pq_ref[:, sl], pqh_ref[:, sl], first, 4))
                s = s * _sigmoid(s)
                if scale is not None:
                    s = s * (lax.rsqrt(jnp.sum(s * s, axis=-1, keepdims=True) + EPS) * scale)
                out_ref[:, _strip(h)] = s
        gb_ref[...] = _gdn_gates(p2_ref[...], alog_ref[...], dtb_ref[...])

    return pl.pallas_call(
        body, name="pre_fwd", grid=(t // tb,),
        in_specs=[_row(tb, 3 * D, 0), _prev(tb, 3 * D, 0, rows=16), _row(tb, 3 * D), _prev(tb, 3 * D), _row(tb, D),
                  _fixed((D, 128)), _fixed((8, D)), _fixed((8, 3 * D)), _fixed((1, 128)), _fixed((1, 128))],
        out_specs=[_row(tb, D), _row(tb, D), _row(tb, D), _row(tb, D), _row(tb, 128), _row(tb, 128)],
        out_shape=[jax.ShapeDtypeStruct((t, D), bf16), jax.ShapeDtypeStruct((t, D), f32),
                   jax.ShapeDtypeStruct((t, D), f32), jax.ShapeDtypeStruct((t, D), f32),
                   jax.ShapeDtypeStruct((t, 128), f32), jax.ShapeDtypeStruct((t, 128), f32)],
        compiler_params=_params(1),
    )(pg, pg, pq, pq, h1, w2, wa, wg, alog, dtb)


_Z_COL, _GA_COL, _GB_COL = 3, 4, 5


def _post_fwd(o, pg, gn):
    t = o.shape[0]
    tb = _pick(t, (ELEMENTWISE_BLOCK, 256, 128))

    def body(o_ref, z_ref, gn_ref, yb_ref):
        for h in range(H):
            sl = slice(h * DH, (h + 1) * DH)
            oh = o_ref[:, sl]
            z = _f32(z_ref, sl)
            r = lax.rsqrt(jnp.mean(oh * oh, axis=-1, keepdims=True) + EPS)
            yb_ref[:, sl] = (oh * r * gn_ref[...] * (z * _sigmoid(z))).astype(bf16)

    return pl.pallas_call(
        body, name="post_fwd", grid=(t // tb,), in_specs=[_row(tb, D), _row(tb, D, _Z_COL), _fixed((1, DH))],
        out_specs=_row(tb, D), out_shape=jax.ShapeDtypeStruct((t, D), bf16), compiler_params=_params(1),
    )(o, pg, gn)


def _post_bwd(dyb, o, pg, gn):
    t = o.shape[0]
    tb = _pick(t, (ELEMENTWISE_BLOCK, 256, 128))

    def body(dyb_ref, o_ref, z_ref, gn_ref, do_ref, dz_ref, dgn_ref):
        @pl.when(pl.program_id(0) == 0)
        def _():
            dgn_ref[...] = jnp.zeros_like(dgn_ref)

        gn_v = gn_ref[...]
        acc = jnp.zeros((8, DH), f32)
        for h in range(H):
            sl = slice(h * DH, (h + 1) * DH)
            oh = o_ref[:, sl]
            z = _f32(z_ref, sl)
            dy = dyb_ref[:, sl]
            r = lax.rsqrt(jnp.mean(oh * oh, axis=-1, keepdims=True) + EPS)
            on = oh * r
            sg = _sigmoid(z)
            sz = z * sg
            don = dy * sz
            dz_ref[:, sl] = (dy * on * gn_v * (sg * (1.0 + z * (1.0 - sg)))).astype(bf16)
            acc = acc + jnp.sum((don * on).reshape(tb // 8, 8, DH), axis=0)
            doh = don * gn_v
            do_ref[:, sl] = r * (doh - on * jnp.mean(doh * on, axis=-1, keepdims=True))
        dgn_ref[...] += acc

    return pl.pallas_call(
        body, name="post_bwd", grid=(t // tb,),
        in_specs=[_row(tb, D), _row(tb, D), _row(tb, D, _Z_COL), _fixed((1, DH))],
        out_specs=[_row(tb, D), _row(tb, D), _fixed((8, DH))],
        out_shape=[jax.ShapeDtypeStruct((t, D), f32), jax.ShapeDtypeStruct((t, D), bf16),
                   jax.ShapeDtypeStruct((8, DH), f32)],
        compiler_params=_params(1),
    )(dyb, o, pg, gn)


def _mix_fwd(ya, yb, pg):
    t = ya.shape[0]
    tb = _pick(t, (ELEMENTWISE_BLOCK, 256, 128))

    def body(ya_ref, yb_ref, ga_ref, gb_ref, mix_ref):
        ya_v, yb_v = ya_ref[...].astype(f32), yb_ref[...].astype(f32)
        mix = _sigmoid(ga_ref[...].astype(f32)) * ya_v + _sigmoid(gb_ref[...].astype(f32)) * yb_v
        mix_ref[...] = mix.astype(bf16)

    return pl.pallas_call(
        body, name="mix_fwd", grid=(t // tb,),
        in_specs=[_row(tb, D), _row(tb, D), _row(tb, D, _GA_COL), _row(tb, D, _GB_COL)],
        out_specs=_row(tb, D), out_shape=jax.ShapeDtypeStruct((t, D), bf16), compiler_params=_params(1),
    )(ya, yb, pg, pg)


def _mix_bwd(dmix, ya, yb, pg):
    t = ya.shape[0]
    tb = _pick(t, (ELEMENTWISE_BLOCK, 256, 128))

    def body(dm_ref, ya_ref, yb_ref, ga_ref, gb_ref, dya_ref, dyb_ref, dg_ref):
        dm = dm_ref[...].astype(f32)
        sa = _sigmoid(ga_ref[...].astype(f32))
        sb = _sigmoid(gb_ref[...].astype(f32))
        dya_ref[...] = (dm * sa).astype(bf16)
        dyb_ref[...] = (dm * sb).astype(bf16)
        dg_ref[:, :D] = (dm * ya_ref[...].astype(f32) * sa * (1.0 - sa)).astype(bf16)
        dg_ref[:, D:] = (dm * yb_ref[...].astype(f32) * sb * (1.0 - sb)).astype(bf16)

    return pl.pallas_call(
        body, name="mix_bwd", grid=(t // tb,),
        in_specs=[_row(tb, D), _row(tb, D), _row(tb, D), _row(tb, D, _GA_COL), _row(tb, D, _GB_COL)],
        out_specs=[_row(tb, D), _row(tb, D), _row(tb, 2 * D)],
        out_shape=[jax.ShapeDtypeStruct((t, D), bf16), jax.ShapeDtypeStruct((t, D), bf16),
                   jax.ShapeDtypeStruct((t, 2 * D), bf16)],
        compiler_params=_params(1),
    )(dmix, ya, yb, pg, pg)


def _ffn_fwd(up, wf):
    t = up.shape[0]
    tb = _pick(t, (ROW_BLOCK, 128))

    def body(up_ref, uph_ref, wf_ref, act_ref):
        first = pl.program_id(0) == 0
        for i in range(DFF // 128):
            g, v = _strip(i), _strip(i, DFF)
            gate = _strip_conv(wf_ref, g, _strip_taps(_f32(up_ref, g), _halo_before(uph_ref, g), first, 3))
            val = _strip_conv(wf_ref, v, _strip_taps(_f32(up_ref, v), _halo_before(uph_ref, v), first, 3))
            act_ref[:, g] = (gate * _sigmoid(gate) * val).astype(bf16)

    return pl.pallas_call(
        body, name="ffn_fwd", grid=(t // tb,),
        in_specs=[_row(tb, 2 * DFF), _prev(tb, 2 * DFF, rows=16), _fixed((8, 2 * DFF))],
        out_specs=_row(tb, DFF), out_shape=jax.ShapeDtypeStruct((t, DFF), bf16), compiler_params=_params(1),
    )(up, up, wf)


def _ffn_bwd1(dact, up, wf):
    t = up.shape[0]
    tb = _pick(t, (ROW_BLOCK, 128))

    def body(da_ref, up_ref, uph_ref, wf_ref, dc_ref, dw_ref):
        @pl.when(pl.program_id(0) == 0)
        def _():
            dw_ref[...] = jnp.zeros_like(dw_ref)

        first = pl.program_id(0) == 0
        for i in range(DFF // 128):
            g, v = _strip(i), _strip(i, DFF)
            g_taps = _strip_taps(_f32(up_ref, g), _halo_before(uph_ref, g), first, 3)
            v_taps = _strip_taps(_f32(up_ref, v), _halo_before(uph_ref, v), first, 3)
            gate = _strip_conv(wf_ref, g, g_taps)
            val = _strip_conv(wf_ref, v, v_taps)
            sg = _sigmoid(gate)
            da = _f32(da_ref, g)
            dgate = da * val * (sg * (1.0 + gate * (1.0 - sg)))
            dval = da * (gate * sg)
            dc_ref[:, g] = dgate.astype(bf16)
            dc_ref[:, v] = dval.astype(bf16)
            _strip_weight_grad(dw_ref, g, dgate, g_taps)
            _strip_weight_grad(dw_ref, v, dval, v_taps)

    return pl.pallas_call(
        body, name="ffn_bwd1", grid=(t // tb,),
        in_specs=[_row(tb, DFF), _row(tb, 2 * DFF), _prev(tb, 2 * DFF, rows=16), _fixed((8, 2 * DFF))],
        out_specs=[_row(tb, 2 * DFF), _fixed((8, 2 * DFF))],
        out_shape=[jax.ShapeDtypeStruct((t, 2 * DFF), bf16), jax.ShapeDtypeStruct((8, 2 * DFF), f32)],
        compiler_params=_params(1),
    )(dact, up, up, wf)


def _ffn_bwd2(dc, wf):
    t = dc.shape[0]
    tb = _pick(t, (ROW_BLOCK, 128))
    nb = t // tb

    def body(dc_ref, dch_ref, wf_ref, dup_ref):
        last = pl.program_id(0) == nb - 1
        for i in range(2 * DFF // 128):
            sl = _strip(i)
            dup_ref[:, sl] = _strip_conv_up(_f32(dc_ref, sl), _halo_after(dch_ref, sl), last, wf_ref, sl, 3).astype(bf16)

    return pl.pallas_call(
        body, name="ffn_bwd2", grid=(nb,),
        in_specs=[_row(tb, 2 * DFF), _next(tb, 2 * DFF, t, rows=16), _fixed((8, 2 * DFF))],
        out_specs=_row(tb, 2 * DFF), out_shape=jax.ShapeDtypeStruct((t, 2 * DFF), bf16), compiler_params=_params(1),
    )(dc, dc, wf)


def _final(x3, tgt, g):
    t = x3.shape[0]
    tb = _pick(t, (ELEMENTWISE_BLOCK, 256, 128))

    def body(x_ref, t_ref, g_ref, loss_ref, dx_ref, dxb_ref, dg_ref):
        @pl.when(pl.program_id(0) == 0)
        def _():
            loss_ref[...] = jnp.zeros_like(loss_ref)
            dg_ref[...] = jnp.zeros_like(dg_ref)

        xv = x_ref[...]
        r = lax.rsqrt(jnp.mean(xv * xv, axis=-1, keepdims=True) + EPS)
        xh = xv * r
        gv = g_ref[...]
        e = xh * gv - t_ref[...]
        lrow = 0.5 * jnp.mean(e * e, axis=-1, keepdims=True)
        loss_ref[...] += jnp.sum(jnp.broadcast_to(lrow, (tb, 128)).reshape(tb // 8, 8, 128), axis=0)
        dy = e * (1.0 / D)
        dyg = dy * gv
        dx = r * (dyg - xh * jnp.mean(dyg * xh, axis=-1, keepdims=True))
        dx_ref[...] = dx
        dxb_ref[...] = dx.astype(bf16)
        dg_ref[...] += jnp.sum((dy * xh).reshape(tb // 8, 8, D), axis=0)

    return pl.pallas_call(
        body, name="final", grid=(t // tb,), in_specs=[_row(tb, D), _row(tb, D), _fixed((1, D))],
        out_specs=[_fixed((8, 128)), _row(tb, D), _row(tb, D), _fixed((8, D))],
        out_shape=[jax.ShapeDtypeStruct((8, 128), f32), jax.ShapeDtypeStruct((t, D), f32),
                   jax.ShapeDtypeStruct((t, D), bf16), jax.ShapeDtypeStruct((8, D), f32)],
        compiler_params=_params(1),
    )(x3, tgt, g)


def _pre_bwd1(pg, pq, p2, dya_in, dqn, dkn, dvc, dgb, gbeta, h1, wa, wg, alog, dtb):
    t = pg.shape[0]
    tb = _pick(t, (ROW_BLOCK // 2, 128))

    def body(p0_ref, p0h_ref, pq_ref, pqh_ref, p2_ref, dya_ref, dqn_ref, dkn_ref, dvc_ref, dgb_ref, gb_ref, h1_ref,
             wa_ref, wg_ref, alog_ref, dtb_ref,
             dbg_ref, dca_ref, dc4_ref, dp2_ref, dwa_ref, dwg_ref, dal_ref, ddt_ref, dw2_ref):
        @pl.when(pl.program_id(0) == 0)
        def _():
            dwa_ref[...] = jnp.zeros_like(dwa_ref)
            dwg_ref[...] = jnp.zeros_like(dwg_ref)
            dal_ref[...] = jnp.zeros_like(dal_ref)
            ddt_ref[...] = jnp.zeros_like(ddt_ref)
            dw2_ref[...] = jnp.zeros_like(dw2_ref)

        first = pl.program_id(0) == 0

        for i in range(D // 128):
            sl, cg, xv = _strip(i), _strip(i, D), _strip(i, 2 * D)
            taps = _strip_taps(_f32(p0_ref, cg) * _f32(p0_ref, xv), _halo_before(p0h_ref, cg) * _halo_before(p0h_ref, xv),
                               first, 3)
            dya = _f32(dya_ref, sl)
            dbg_ref[:, sl] = (dya * _strip_conv(wa_ref, sl, taps)).astype(bf16)
            dca = dya * _f32(p0_ref, sl)
            dca_ref[:, sl] = dca.astype(bf16)
            _strip_weight_grad(dwa_ref, sl, dca, taps)

        for part, d_ref, scale in ((0, dqn_ref, DH ** -0.5), (1, dkn_ref, 1.0), (2, dvc_ref, None)):
            for h in range(H):
                sl = _strip(h, part * D)
                taps = _strip_taps(pq_ref[:, sl], pqh_ref[:, sl], first, 4)
                c4 = _strip_conv(wg_ref, sl, taps)
                sg = _sigmoid(c4)
                dn = d_ref[:, _strip(h)]
                if scale is not None:
                    a = c4 * sg
                    r = lax.rsqrt(jnp.sum(a * a, axis=-1, keepdims=True) + EPS)
                    an = a * r
                    dn = dn * scale
                    dn = r * (dn - an * jnp.sum(dn * an, axis=-1, keepdims=True))
                dc4 = dn * (sg * (1.0 + c4 * (1.0 - sg)))
                dc4_ref[:, sl] = dc4.astype(bf16)
                _strip_weight_grad(dwg_ref, sl, dc4, taps)

        ab = p2_ref[...]
        lane = lax.broadcasted_iota(jnp.int32, ab.shape, 1)
        dgbv = dgb_ref[...]
        gbv = gb_ref[...]
        da = dgbv * (-jnp.exp(alog_ref[...])) * _sigmoid(ab + dtb_ref[...])
        db = dgbv * gbv * (1.0 - gbv)
        dp2 = jnp.where(lane < H, da, jnp.where(lane < 2 * H, db, 0.0)).astype(bf16)
        dp2_ref[...] = dp2
        dw2_ref[...] += lax.dot_general(dp2, h1_ref[...], (((0,), (0,)), ((), ())), preferred_element_type=f32)
        dal = jnp.where(lane < H, dgbv * gbv, 0.0)
        ddt = jnp.where(lane < H, da, 0.0)
        dal_ref[...] += jnp.sum(dal.reshape(tb // 8, 8, 128), axis=0)
        ddt_ref[...] += jnp.sum(ddt.reshape(tb // 8, 8, 128), axis=0)

    return pl.pallas_call(
        body, name="pre_bwd1", grid=(t // tb,),
        in_specs=[_row(tb, 3 * D, 0), _prev(tb, 3 * D, 0, rows=16), _row(tb, 3 * D), _prev(tb, 3 * D), _row(tb, 128),
                  _row(tb, D), _row(tb, D), _row(tb, D), _row(tb, D), _row(tb, 128), _row(tb, 128), _row(tb, D),
                  _fixed((8, D)), _fixed((8, 3 * D)), _fixed((1, 128)), _fixed((1, 128))],
        out_specs=[_row(tb, D), _row(tb, D), _row(tb, 3 * D), _row(tb, 128),
                   _fixed((8, D)), _fixed((8, 3 * D)), _fixed((8, 128)), _fixed((8, 128)), _fixed((128, D))],
        out_shape=[jax.ShapeDtypeStruct((t, D), bf16), jax.ShapeDtypeStruct((t, D), bf16),
                   jax.ShapeDtypeStruct((t, 3 * D), bf16), jax.ShapeDtypeStruct((t, 128), bf16),
                   jax.ShapeDtypeStruct((8, D), f32), jax.ShapeDtypeStruct((8, 3 * D), f32),
                   jax.ShapeDtypeStruct((8, 128), f32), jax.ShapeDtypeStruct((8, 128), f32),
                   jax.ShapeDtypeStruct((128, D), f32)],
        compiler_params=_params(1),
    )(pg, pg, pq, pq, p2, dya_in, dqn, dkn, dvc, dgb, gbeta, h1, wa, wg, alog, dtb)


def _pre_bwd2(dca, dc4, pg, dbg, dz, dgates, wa, wg, exchange=None):
    t = pg.shape[0]
    tb = _pick(t, (ROW_BLOCK, 128))
    nb = t // tb

    def body(dca_ref, dcah_ref, dc4_ref, dc4h_ref, p0_ref, dbg_ref, dz_ref, dgt_ref, wa_ref, wg_ref, dp_ref):
        last = pl.program_id(0) == nb - 1
        dp_ref[:, :D] = dbg_ref[...]
        for i in range(D // 128):
            sl, cg, xv = _strip(i), _strip(i, D), _strip(i, 2 * D)
            du = _strip_conv_up(_f32(dca_ref, sl), _halo_after(dcah_ref, sl), last, wa_ref, sl, 3)
            dp_ref[:, cg] = (du * _f32(p0_ref, xv)).astype(bf16)
            dp_ref[:, xv] = (du * _f32(p0_ref, cg)).astype(bf16)
        dp_ref[:, 3 * D:4 * D] = dz_ref[...]
        dp_ref[:, 4 * D:6 * D] = dgt_ref[...]
        for i in range(3 * D // 128):
            sl = _strip(i)
            dq = _strip_conv_up(_f32(dc4_ref, sl), _halo_after(dc4h_ref, sl), last, wg_ref, sl, 4)
            dp_ref[:, _strip(i, 6 * D)] = dq.astype(bf16)

    return _call_with_exchange(
        body, exchange, name="pre_bwd2", grid=(nb,),
        in_specs=[_row(tb, D), _next(tb, D, t, rows=16), _row(tb, 3 * D), _next(tb, 3 * D, t, rows=16), _row(tb, 3 * D, 0),
                  _row(tb, D), _row(tb, D), _row(tb, 2 * D), _fixed((8, D)), _fixed((8, 3 * D))],
        out_specs=_row(tb, NW1), out_shape=jax.ShapeDtypeStruct((t, NW1), bf16),
        args=[dca, dca, dc4, dc4, pg, dbg, dz, dgates, wa, wg])


def _chunk_consts():
    r = lax.broadcasted_iota(jnp.int32, (CH, CH), 0)
    c = lax.broadcasted_iota(jnp.int32, (CH, CH), 1)
    return r, c, (r == c).astype(f32)


def _tri_inverse(lows, eye, r, c):
    def same_block(b):
        return jnp.bitwise_xor(r, c) < b

    xs = [jnp.where(same_block(8), -low, 0.0) for low in lows]
    ts = [eye + x for x in xs]
    for _ in range(2):
        xs = [_idot(x, x) for x in xs]
        ts = [t + _idot(t, x) for t, x in zip(ts, xs)]
    for b in (8, 16, 32):
        below = same_block(2 * b) & jnp.logical_not(same_block(b))
        ts = [t - _idot(_idot(t, jnp.where(below, low, 0.0)), t) for t, low in zip(ts, lows)]
    return ts


def _chunk_common(q, k, v, gcol, bcol, r, c, eye):
    grow = jnp.sum(eye * gcol, axis=0, keepdims=True)
    dec = jnp.exp(jnp.where(r >= c, gcol - grow, -jnp.inf))
    rcol = lax.broadcasted_iota(jnp.int32, (CH, 1), 0)
    glast = jnp.sum(jnp.where(rcol == CH - 1, gcol, 0.0), axis=0, keepdims=True)
    eg = jnp.exp(gcol)
    el = jnp.exp(glast - gcol)
    kb = k * bcol
    vb = v * bcol
    kk = _bdot_nt(kb, k)
    low = jnp.where(r > c, kk * dec, 0.0)
    qk = _bdot_nt(q, k)
    att = qk * dec
    return grow, dec, glast, eg, el, kb, vb, kk, low, qk, att, rcol


def _gdn_fwd(qn, kn, vc, gbeta):
    t = qn.shape[0]
    n_chunks = t // CH

    def body(q_ref, k_ref, v_ref, gb_ref, o_ref, s_ref, t_ref, state):
        @pl.when(pl.program_id(0) == 0)
        def _():
            state[...] = jnp.zeros_like(state)

        r, c, eye = _chunk_consts()
        tri = (r >= c).astype(f32)
        heads = range(H)
        keys = [(s, h) for s in range(GDN_STEP) for h in heads]
        rows = [slice(s * CH, (s + 1) * CH) for s in range(GDN_STEP)]
        gbs = [gb_ref[rows[s], :] for s in range(GDN_STEP)]
        galls = [_hdot(tri, gb) for gb in gbs]
        qs = {(s, h): q_ref[rows[s], h * DH:(h + 1) * DH] for s, h in keys}
        ks = {(s, h): k_ref[rows[s], h * DH:(h + 1) * DH] for s, h in keys}
        cm = {(s, h): _chunk_common(qs[s, h], ks[s, h], v_ref[rows[s], h * DH:(h + 1) * DH], galls[s][:, h:h + 1],
                                    gbs[s][:, H + h:H + h + 1], r, c, eye) for s, h in keys}
        invs = dict(zip(keys, _tri_inverse([cm[key][8] for key in keys], eye, r, c)))
        uws = {key: _bdot(invs[key], jnp.concatenate([cm[key][6], cm[key][5] * cm[key][3]], axis=1)) for key in keys}
        sts = [state[h] for h in heads]
        for s in range(GDN_STEP):
            vns = [uws[s, h][:, :DH] - _bdot(uws[s, h][:, DH:], sts[h]) for h in heads]
            outs = [_bdot(qs[s, h] * cm[s, h][3], sts[h]) + _bdot(cm[s, h][10], vns[h]) for h in heads]
            news = [sts[h] * jnp.exp(cm[s, h][2]) + _bdot_tn(ks[s, h] * cm[s, h][4], vns[h]) for h in heads]
            for h in heads:
                s_ref[s, h] = sts[h].astype(bf16)
                t_ref[s, h] = invs[s, h]
                o_ref[rows[s], h * DH:(h + 1) * DH] = outs[h]
            sts = news
        for h in heads:
            state[h] = sts[h]

    tb = GDN_STEP * CH
    return pl.pallas_call(
        body, name="gdn_fwd", grid=(t // tb,),
        in_specs=[_row(tb, D), _row(tb, D), _row(tb, D), _row(tb, 128)],
        out_specs=[_row(tb, D), pl.BlockSpec((GDN_STEP, H, DH, DH), lambda i: (i, 0, 0, 0)),
                   pl.BlockSpec((GDN_STEP, H, CH, CH), lambda i: (i, 0, 0, 0))],
        out_shape=[jax.ShapeDtypeStruct((t, D), f32), jax.ShapeDtypeStruct((n_chunks, H, DH, DH), bf16),
                   jax.ShapeDtypeStruct((n_chunks, H, CH, CH), f32)],
        scratch_shapes=[pltpu.VMEM((H, DH, DH), f32)],
        compiler_params=_params(1),
    )(qn, kn, vc, gbeta)


def _gdn_bwd(qn, kn, vc, gbeta, do, s_all, t_all):
    t = qn.shape[0]

    def body(q_ref, k_ref, v_ref, gb_ref, do_ref, s_ref, t_ref, dq_ref, dk_ref, dv_ref, dgb_ref, dstate):
        @pl.when(pl.program_id(0) == 0)
        def _():
            dstate[...] = jnp.zeros_like(dstate)

        r, c, eye = _chunk_consts()
        tril = r >= c
        lane = lax.broadcasted_iota(jnp.int32, (1, 128), 1)
        hs = range(H)

        def each(fn, *lists):
            return [fn(*args) for args in zip(*lists)]

        def rsum(a):
            return jnp.sum(a, axis=1, keepdims=True)

        def before_state(s):
            rows = slice(s * CH, (s + 1) * CH)
            gb = gb_ref[rows, :]
            gall = _hdot(tril.astype(f32), gb)
            p = {"rows": rows}
            p["q"] = q = [q_ref[rows, h * DH:(h + 1) * DH] for h in hs]
            p["k"] = k = [k_ref[rows, h * DH:(h + 1) * DH] for h in hs]
            p["v"] = v = [v_ref[rows, h * DH:(h + 1) * DH] for h in hs]
            p["dout"] = dout = [do_ref[rows, h * DH:(h + 1) * DH] for h in hs]
            p["inv"] = inv = [t_ref[s, h] for h in hs]
            p["st"] = st = [s_ref[s, h] for h in hs]
            p["bcol"] = bcol = [gb[:, H + h:H + h + 1] for h in hs]
            cm = [_chunk_common(q[h], k[h], v[h], gall[:, h:h + 1], bcol[h], r, c, eye) for h in hs]
            for name, i in (("dec", 1), ("glast", 2), ("eg", 3), ("el", 4), ("kb", 5), ("vb", 6), ("low", 8), ("att", 10)):
                p[name] = [m[i] for m in cm]
            p["rcol"] = cm[0][11]
            p["elast"] = each(jnp.exp, p["glast"])
            p["kbg"] = each(jnp.multiply, p["kb"], p["eg"])
            uw = each(lambda i, a, b: _bdot(i, jnp.concatenate([a, b], axis=1)), inv, p["vb"], p["kbg"])
            p["u"] = [a[:, :DH] for a in uw]
            p["w"] = [a[:, DH:] for a in uw]
            p["vn"] = each(lambda a, b, x: a - _bdot(b, x), p["u"], p["w"], st)
            p["qd"] = each(jnp.multiply, q, p["eg"])
            p["kd"] = each(jnp.multiply, k, p["el"])
            p["dqd"] = each(_bdot_nt, dout, st)
            p["datt"] = each(lambda d, x: jnp.where(tril, _bdot_nt(d, x), 0.0), dout, p["vn"])
            p["dqk"] = each(jnp.multiply, p["datt"], p["dec"])
            p["qd_do"] = each(_bdot_tn, p["qd"], dout)
            p["att_do"] = each(_bdot_tn, p["att"], dout)
            return p

        def after_state(p, ds):
            q, k, v, st, inv, bcol = p["q"], p["k"], p["v"], p["st"], p["inv"], p["bcol"]
            eg, el, kb, u, w = p["eg"], p["el"], p["kb"], p["u"], p["w"]
            dvn = each(lambda a, kk, x: a + _bdot(kk, x), p["att_do"], p["kd"], ds)
            dkd = each(_bdot_nt, p["vn"], ds)
            dw = each(lambda a, x: -_bdot_nt(a, x), dvn, st)
            new_ds = each(lambda x, e, a, ww, dv_: x * e + a - _bdot_tn(ww, dv_), ds, p["elast"], p["qd_do"], w, dvn)
            dglast = each(lambda e, x, d: e * jnp.sum(rsum(x.astype(f32) * d), axis=0, keepdims=True), p["elast"], st, ds)
            dr = each(lambda i, a, b: _bdot_tn(i, jnp.concatenate([a, b], axis=1)), inv, dvn, dw)
            dvb = [a[:, :DH] for a in dr]
            dkbg = [a[:, DH:] for a in dr]
            dlow = each(lambda a, b, x, y: -jnp.where(r > c, _bdot_nt(a, b) + _bdot_nt(x, y), 0.0), dvb, u, dkbg, w)
            dkk = each(jnp.multiply, dlow, p["dec"])
            mm = each(lambda a, b, x, y: a * b + x * y, dlow, p["low"], p["datt"], p["att"])
            dkb = each(lambda a, kk, b, e: _bdot(a, kk) + b * e, dkk, k, dkbg, eg)
            dk = each(lambda a, b, x, y, d, e, f, g: _bdot_tn(a, b) + _bdot_tn(x, y) + d * e + f * g,
                      dkk, kb, p["dqk"], q, dkd, el, dkb, bcol)
            dq = each(lambda a, kk, d, e: _bdot(a, kk) + d * e, p["dqk"], k, p["dqd"], eg)
            dv = each(jnp.multiply, dvb, bcol)
            dbeta = each(lambda a, b, x, y: rsum(a * b) + rsum(x * y), dkb, k, dvb, v)
            deg = each(lambda a, b, x, y: rsum(a * b) + rsum(x * y), dkbg, kb, p["dqd"], q)
            delc = each(lambda a, b, e: rsum(a * b) * e, dkd, k, el)
            dgc = each(lambda m, a, e, d: rsum(m) - rsum(eye * jnp.sum(m, axis=0, keepdims=True)) + a * e - d,
                       mm, deg, eg, delc)
            dgc = each(lambda g, d, l: g + jnp.where(p["rcol"] == CH - 1, jnp.sum(d, axis=0, keepdims=True) + l, 0.0),
                       dgc, delc, dglast)
            dg_acc = jnp.zeros((CH, 128), f32)
            db_acc = jnp.zeros((CH, 128), f32)
            rows = p["rows"]
            for h in hs:
                dq_ref[rows, h * DH:(h + 1) * DH] = dq[h]
                dk_ref[rows, h * DH:(h + 1) * DH] = dk[h]
                dv_ref[rows, h * DH:(h + 1) * DH] = dv[h]
                dg_acc = dg_acc + dgc[h] * (lane == h).astype(f32)
                db_acc = db_acc + dbeta[h] * (lane == H + h).astype(f32)
            dgb_ref[rows, :] = _hdot((r <= c).astype(f32), dg_acc) + db_acc
            return new_ds

        order = list(reversed(range(GDN_STEP)))
        pre = [before_state(s) for s in order]
        ds = [dstate[h] for h in hs]
        for p in pre:
            ds = after_state(p, ds)
        for h in hs:
            dstate[h] = ds[h]

    tb = GDN_STEP * CH
    n_steps = t // tb
    rev = lambda i: (n_steps - 1 - i, 0)
    rev4 = lambda i: (n_steps - 1 - i, 0, 0, 0)
    return pl.pallas_call(
        body, name="gdn_bwd", grid=(n_steps,),
        in_specs=[pl.BlockSpec((tb, D), rev), pl.BlockSpec((tb, D), rev), pl.BlockSpec((tb, D), rev),
                  pl.BlockSpec((tb, 128), rev), pl.BlockSpec((tb, D), rev),
                  pl.BlockSpec((GDN_STEP, H, DH, DH), rev4), pl.BlockSpec((GDN_STEP, H, CH, CH), rev4)],
        out_specs=[pl.BlockSpec((tb, D), rev), pl.BlockSpec((tb, D), rev), pl.BlockSpec((tb, D), rev),
                   pl.BlockSpec((tb, 128), rev)],
        out_shape=[jax.ShapeDtypeStruct((t, D), f32)] * 3 + [jax.ShapeDtypeStruct((t, 128), f32)],
        scratch_shapes=[pltpu.VMEM((H, DH, DH), f32)],
        compiler_params=_params(1),
    )(qn, kn, vc, gbeta, do, s_all, t_all)


def _pad_rows(w, rows=8):
    return jnp.pad(w, ((0, rows - w.shape[0]), (0, 0)))


_REST = ("w_up", "w_a_out", "w_b_out", "w_o", "w_down")


def _local_step(x, tgt, w, comm=None):
    g1 = w["norm_mix_g"].reshape(1, D)
    if comm is None:
        h1 = _rms_fwd(x, g1, name="rms1_fwd")
    else:
        h1, gathered = _rms_fwd(x, g1, name="rms1_fwd", exchange=comm.gather_first())
        w = {**w, **comm.finish_first(gathered)}
    w1, w2 = w["w1"], w["w2"]
    wa = _pad_rows(w["conv_a_w"])
    wg = _pad_rows(w["gdn_conv_w"])
    wf = _pad_rows(w["ffn_conv_w"])
    alog = jnp.pad(w["gdn_A_log"].reshape(1, H), ((0, 0), (0, 128 - H)))
    dtb = jnp.pad(w["gdn_dt_bias"].reshape(1, H), ((0, 0), (0, 128 - H)))
    g2 = w["norm_ffn_g"].reshape(1, D)
    g3 = w["norm_final_g"].reshape(1, D)
    gn = w["gdn_norm_g"].reshape(1, DH)

    if comm is None:
        pg = _matmul(h1, w1, name="mm_in", cols=(0, 6 * D), out_dtype=bf16)
        pq = _matmul(h1, w1, name="mm_in_qkv", cols=(6 * D, 3 * D))
    else:
        pg, gathered = _matmul(h1, w1, name="mm_in", cols=(0, 6 * D), out_dtype=bf16, exchange=comm.gather_rest())
        pq, gathered = _matmul(h1, w1, name="mm_in_qkv", cols=(6 * D, 3 * D), exchange=_gather_forward_exchange(gathered))
        w = {**w, **comm.finish_gather(gathered)}
    ya_in, qn, kn, vc, gbeta, p2 = _pre_fwd(pg, pq, h1, w2, wa, wg, alog, dtb)
    o, s_all, t_all = _gdn_fwd(qn, kn, vc, gbeta)
    yb_in = _post_fwd(o, pg, gn)
    ya = _matmul(ya_in, w["w_a_out"], name="mm_a", out_dtype=bf16)
    yb = _matmul(yb_in, w["w_b_out"], name="mm_b", out_dtype=bf16)
    mix = _mix_fwd(ya, yb, pg)
    x2 = _matmul(mix, w["w_o"], name="mm_o", add=x, tm=1024)
    h2 = _rms_fwd(x2, g2, name="rms2_fwd")
    up = _matmul(h2, w["w_up"], nt=True, name="mm_up", tn=DFF // 2, out_dtype=bf16)
    act = _ffn_fwd(up, wf)
    x3 = _matmul(act, w["w_down"], name="mm_down", add=x2, tm=512)
    loss_p, dx3, dx3b, dg3 = _final(x3, tgt, g3)

    grads = {"norm_final_g": dg3}
    dact = _matmul(dx3b, w["w_down"], nt=True, name="mm_down_dx", tm=512, tn=DFF, out_dtype=bf16)
    grads["w_down"] = _matmul_tn(act, dx3b, name="mm_down_dw", tm=DFF // 2)
    dc, dwf = _ffn_bwd1(dact, up, wf)
    grads["ffn_conv_w"] = dwf
    dup = _ffn_bwd2(dc, wf)
    dh2 = _matmul(dup, w["w_up"], name="mm_up_dx", tm=512)
    grads["w_up"] = _matmul_tn(dup, h2, name="mm_up_dw", tm=DFF // 2)
    dx2, dx2b, dg2 = _rms_bwd(dh2, x2, g2, dx3, name="rms2_bwd")
    grads["norm_ffn_g"] = dg2
    dmix = _matmul(dx2b, w["w_o"], nt=True, name="mm_o_dx", out_dtype=bf16)
    grads["w_o"] = _matmul_tn(mix, dx2b, name="mm_o_dw")
    dya, dyb, dgates = _mix_bwd(dmix, ya, yb, pg)
    dya_in = _matmul(dya, w["w_a_out"], nt=True, name="mm_a_dx", out_dtype=bf16)
    grads["w_a_out"] = _matmul_tn(ya_in, dya, name="mm_a_dw")
    dyb_in = _matmul(dyb, w["w_b_out"], nt=True, name="mm_b_dx")
    grads["w_b_out"] = _matmul_tn(yb_in, dyb, name="mm_b_dw")
    do, dz, dgn = _post_bwd(dyb_in, o, pg, gn)
    grads["gdn_norm_g"] = dgn
    dqn, dkn, dvc, dgb = _gdn_bwd(qn, kn, vc, gbeta, do, s_all, t_all)
    dbg, dca, dc4, dp2, dwa, dwg, dal, ddt, grads["w2"] = _pre_bwd1(pg, pq, p2, dya_in, dqn, dkn, dvc, dgb, gbeta, h1,
                                                                    wa, wg, alog, dtb)
    grads["conv_a_w"] = dwa
    grads["gdn_conv_w"] = dwg
    grads["gdn_A_log"] = dal
    grads["gdn_dt_bias"] = ddt
    if comm is None:
        dp1 = _pre_bwd2(dca, dc4, pg, dbg, dz, dgates, wa, wg)
        grads["w1"] = _matmul_tn(dp1, h1, name="mm_in_dw", tt=4096)
        dh1 = _matmul(dp1, w1, nt=True, name="mm_in_dx", tm=512)
    else:
        exchange, blocks = comm.reduce_halves(_REST, grads)
        dp1, recv = _pre_bwd2(dca, dc4, pg, dbg, dz, dgates, wa, wg, exchange=exchange)
        exchange, sums = comm.reduce_sums(_REST, blocks, recv)
        grads["w1"], recv = _matmul_tn(dp1, h1, name="mm_in_dw", tt=4096, exchange=exchange)
        comm.finish_reduce(_REST, sums, recv)
        exchange, blocks = comm.reduce_halves(("w_in",), grads)
        exchange, sums = comm.reduce_sums(("w_in",), blocks, _run_exchange(exchange, name="rs_sibling_w_in"))
        dh1, recv = _matmul(dp1, w1, nt=True, name="mm_in_dx", tm=512, exchange=exchange)
        comm.finish_reduce(("w_in",), sums, recv)
    dx, dg1 = _rms_bwd(dh1, x, g1, dx2, name="rms1_bwd", more=(dp2, w2), bf16_copy=False)
    grads["norm_mix_g"] = dg1
    return loss_p, dx, grads


_ANY = pl.BlockSpec(memory_space=pl.ANY)


def _remote(src, dst, send_sem, recv_sem, to):
    return pltpu.make_async_remote_copy(src_ref=src, dst_ref=dst, send_sem=send_sem, recv_sem=recv_sem,
                                        device_id=to, device_id_type=MESH)


def _run_exchange(exchange, *, name):
    arrays, shapes, sems, start, wait = exchange
    n_in, n_out = len(arrays), len(shapes)

    def body(*refs):
        start(refs[:n_in], refs[n_in:n_in + n_out], refs[n_in + n_out:])
        wait(refs[:n_in], refs[n_in:n_in + n_out], refs[n_in + n_out:])

    return pl.pallas_call(body, name=name, out_shape=list(shapes), in_specs=[_ANY] * n_in, out_specs=[_ANY] * n_out,
                          scratch_shapes=list(sems))(*arrays)


def _gather_exchange(shards):
    n = len(shards)

    def copies(x_refs, out_refs, sems):
        send_sems, recv_sems, local_sems = sems
        x, y, c = lax.axis_index("x"), lax.axis_index("y"), lax.axis_index("c")

        def flip(v, b):
            return v + b - 2 * v * b

        me, sibling = (x, y, c), (x, y, 1 - c)
        chip1, chip2, diag = (flip(x, 1 - c), flip(y, c)), (flip(x, c), flip(y, 1 - c)), (1 - x, 1 - y)

        def copy(a, k, blk, to, from_input=False):
            dst = out_refs[a].at[4 * blk[0] + 2 * blk[1] + blk[2]]
            return _remote(x_refs[a] if from_input else dst, dst, send_sems.at[a, k], recv_sems.at[a, k], to)

        mine = [pltpu.make_async_copy(x_refs[a], out_refs[a].at[4 * x + 2 * y + c], local_sems.at[a]) for a in range(n)]
        first = []
        for a in range(n):
            first += [copy(a, 0, me, sibling, from_input=True), copy(a, 1, me, (*chip1, c), from_input=True),
                      copy(a, 2, me, (*chip2, c), from_input=True)]
        return copy, mine, first, me, sibling, chip1, chip2, diag, c

    def start(x_refs, out_refs, sems):
        _, mine, first, *_ = copies(x_refs, out_refs, sems)
        for cp in mine + first:
            cp.start()

    def wait(x_refs, out_refs, sems):
        copy, mine, first, me, sibling, chip1, chip2, diag, c = copies(x_refs, out_refs, sems)
        passed = []

        def pass_on(cp):
            passed.append(cp)
            cp.start()

        for a in range(n):
            copy(a, 1, (*chip1, c), me).wait_recv()
            pass_on(copy(a, 3, (*chip1, c), (*chip2, c)))
            pass_on(copy(a, 4, (*chip1, c), sibling))
        for a in range(n):
            copy(a, 2, (*chip2, c), me).wait_recv()
            pass_on(copy(a, 5, (*chip2, c), sibling))
        for a in range(n):
            copy(a, 3, (*diag, c), me).wait_recv()
            pass_on(copy(a, 6, (*diag, c), sibling))
        for a in range(n):
            copy(a, 0, sibling, me).wait_recv()
            copy(a, 4, (*chip2, 1 - c), me).wait_recv()
            copy(a, 5, (*chip1, 1 - c), me).wait_recv()
            copy(a, 6, (*diag, 1 - c), me).wait_recv()
        for cp in first + passed:
            cp.wait_send()
        for cp in mine:
            cp.wait()

    shapes = [jax.ShapeDtypeStruct((N_DEV, *s.shape), s.dtype) for s in shards]
    sems = [pltpu.SemaphoreType.DMA((n, 7)), pltpu.SemaphoreType.DMA((n, 7)), pltpu.SemaphoreType.DMA((n,))]
    return shards, shapes, sems, start, wait


def _gather_direct_exchange(shards):
    n = len(shards)

    def copies(x_refs, out_refs, sems):
        send_sems, recv_sems, local_sems = sems
        x, y, c = lax.axis_index("x"), lax.axis_index("y"), lax.axis_index("c")
        targets = [(x, y, 1 - c), (1 - x, y, c), (x, 1 - y, c), (1 - x, 1 - y, c)]
        local, sends, recvs = [], [], []
        for a in range(n):
            mine = out_refs[a].at[4 * x + 2 * y + c]
            local.append(pltpu.make_async_copy(x_refs[a], mine, local_sems.at[a]))
            for k, to in enumerate(targets):
                theirs = out_refs[a].at[4 * to[0] + 2 * to[1] + to[2]]
                sends.append(_remote(x_refs[a], mine, send_sems.at[a, k], recv_sems.at[a, k], to))
                recvs.append(_remote(theirs, theirs, send_sems.at[a, k], recv_sems.at[a, k], to))
        return local, sends, recvs

    def start(x_refs, out_refs, sems):
        local, sends, _ = copies(x_refs, out_refs, sems)
        for cp in local + sends:
            cp.start()

    def wait(x_refs, out_refs, sems):
        local, sends, recvs = copies(x_refs, out_refs, sems)
        for cp in recvs:
            cp.wait_recv()
        for cp in sends:
            cp.wait_send()
        for cp in local:
            cp.wait()

    shapes = [jax.ShapeDtypeStruct((N_DEV, *s.shape), s.dtype) for s in shards]
    sems = [pltpu.SemaphoreType.DMA((n, 4)), pltpu.SemaphoreType.DMA((n, 4)), pltpu.SemaphoreType.DMA((n,))]
    return shards, shapes, sems, start, wait


def _gather_forward_exchange(gathered):
    n = len(gathered)

    def copies(_, out_refs, sems):
        send_sems, recv_sems = sems
        x, y, c = lax.axis_index("x"), lax.axis_index("y"), lax.axis_index("c")
        sibling = (x, y, 1 - c)
        sends, recvs = [], []
        for a in range(n):
            for j, (px, py) in enumerate([(1 - x, y), (x, 1 - y), (1 - x, 1 - y)]):
                mine = out_refs[a].at[4 * px + 2 * py + c]
                theirs = out_refs[a].at[4 * px + 2 * py + 1 - c]
                sends.append(_remote(mine, mine, send_sems.at[a, j], recv_sems.at[a, j], sibling))
                recvs.append(_remote(theirs, theirs, send_sems.at[a, j], recv_sems.at[a, j], sibling))
        return sends, recvs

    def start(in_refs, out_refs, sems):
        for cp in copies(in_refs, out_refs, sems)[0]:
            cp.start()

    def wait(in_refs, out_refs, sems):
        sends, recvs = copies(in_refs, out_refs, sems)
        for cp in recvs:
            cp.wait_recv()
        for cp in sends:
            cp.wait_send()

    shapes = [jax.ShapeDtypeStruct(g.shape, g.dtype) for g in gathered]
    sems = [pltpu.SemaphoreType.DMA((n, 3)), pltpu.SemaphoreType.DMA((n, 3))]
    return gathered, shapes, sems, start, wait, True


def _chips_exchange(hsums):
    n = len(hsums)

    def copies(h_refs, out_refs, sems):
        send_sems, recv_sems = sems
        x, y, c = lax.axis_index("x"), lax.axis_index("y"), lax.axis_index("c")
        chips = [(1 - x, y), (x, 1 - y), (1 - x, 1 - y)]
        return [_remote(h_refs[a].at[2 * px + py], out_refs[a].at[k], send_sems.at[a, k], recv_sems.at[a, k], (px, py, c))
                for a in range(n) for k, (px, py) in enumerate(chips)]

    def start(h_refs, out_refs, sems):
        for cp in copies(h_refs, out_refs, sems):
            cp.start()

    def wait(h_refs, out_refs, sems):
        for cp in copies(h_refs, out_refs, sems):
            cp.wait()

    shapes = [jax.ShapeDtypeStruct((3, *h.shape[1:]), h.dtype) for h in hsums]
    sems = [pltpu.SemaphoreType.DMA((n, 3)), pltpu.SemaphoreType.DMA((n, 3))]
    return hsums, shapes, sems, start, wait


def _sibling_exchange(halves):
    n = len(halves)

    def copies(p_refs, out_refs, sems):
        send_sems, recv_sems = sems
        x, y, c = lax.axis_index("x"), lax.axis_index("y"), lax.axis_index("c")
        return [_remote(p_refs[a], out_refs[a], send_sems.at[a], recv_sems.at[a], (x, y, 1 - c)) for a in range(n)]

    def start(p_refs, out_refs, sems):
        for cp in copies(p_refs, out_refs, sems):
            cp.start()

    def wait(p_refs, out_refs, sems):
        for cp in copies(p_refs, out_refs, sems):
            cp.wait()

    shapes = [jax.ShapeDtypeStruct(h.shape, h.dtype) for h in halves]
    return halves, shapes, [pltpu.SemaphoreType.DMA((n,)), pltpu.SemaphoreType.DMA((n,))], start, wait


_IN_RANGES = ((0, 3 * D, 0, 0), (3 * D, 6 * D, 0, 6 * D), (6 * D, 7 * D, 0, 3 * D), (7 * D, 7 * D + 16, 1, 0),
              (7 * D + 16, 9 * D + 16, 0, 4 * D))


def _col_pieces(width, ranges):
    pieces = []
    for d in range(N_DEV):
        lo, hi = d * width, (d + 1) * width
        for glo, ghi, mat, mlo in ranges:
            a, b = max(lo, glo), min(hi, ghi)
            if a < b:
                pieces.append((d, a - lo, b - lo, mat, mlo + a - glo))
    return pieces


def _cols_to_matrices(g, ranges, out_widths, *, name):
    _, rows, width = g.shape
    tb = 128
    pieces = _col_pieces(width, ranges)
    covered = [sum(p[2] - p[1] for p in pieces if p[3] == m) for m in range(len(out_widths))]

    def body(g_ref, *o_refs):
        for m, o_ref in enumerate(o_refs):
            if covered[m] < out_widths[m]:
                o_ref[...] = jnp.zeros_like(o_ref)
        for d, b0, b1, m, m0 in pieces:
            o_refs[m][:, m0:m0 + b1 - b0] = g_ref[d, :, b0:b1]

    return pl.pallas_call(
        body, name=name, grid=(rows // tb,), in_specs=[pl.BlockSpec((N_DEV, tb, width), lambda i: (0, i, 0))],
        out_specs=[pl.BlockSpec((tb, wo), lambda i: (i, 0)) for wo in out_widths],
        out_shape=[jax.ShapeDtypeStruct((rows, wo), g.dtype) for wo in out_widths], compiler_params=_params(1),
    )(g)


def _transposed_matrices_to_blocks(mats, ranges, width, *, name):
    rows = mats[0].shape[1]
    pieces = _col_pieces(width, ranges)

    def body(*refs):
        m_refs, g_ref = refs[:-1], refs[-1]
        for d, b0, b1, m, m0 in pieces:
            g_ref[d, b0:b1, :] = m_refs[m][m0:m0 + b1 - b0, :]

    return pl.pallas_call(
        body, name=name, grid=(rows // 128,),
        in_specs=[pl.BlockSpec((mt.shape[0], 128), lambda i: (0, i)) for mt in mats],
        out_specs=pl.BlockSpec((N_DEV, width, 128), lambda i: (0, 0, i)),
        out_shape=jax.ShapeDtypeStruct((N_DEV, width, rows), mats[0].dtype), compiler_params=_params(1),
    )(*mats)


def _row_block(rows):
    return 128 if rows % 128 == 0 else rows


def _half_bf16(g4, c_other, *, name):
    _, _, rows, width = g4.shape
    tb = _row_block(rows)

    def body(c_ref, p_ref, o_ref):
        o_ref[0] = p_ref[0, 0].astype(bf16)

    grid_spec = pltpu.PrefetchScalarGridSpec(
        num_scalar_prefetch=1, grid=(4, rows // tb),
        in_specs=[pl.BlockSpec((1, 1, tb, width), lambda j, i, c_ref: (j, c_ref[0], i, 0))],
        out_specs=pl.BlockSpec((1, tb, width), lambda j, i, c_ref: (j, i, 0)))
    return pl.pallas_call(
        body, name=name, grid_spec=grid_spec, out_shape=jax.ShapeDtypeStruct((4, rows, width), bf16),
        compiler_params=_params(2, _vmem_for(4 * tb * width, 2 * tb * width)),
    )(c_other, g4)


def _pair_sum(g4, recv, c_me, *, name):
    _, _, rows, width = g4.shape
    tb = _row_block(rows)

    def body(c_ref, p_ref, r_ref, o_ref, ob_ref):
        s = p_ref[0, 0] + r_ref[0].astype(f32)
        o_ref[0] = s
        ob_ref[0] = s.astype(bf16)

    blk = pl.BlockSpec((1, tb, width), lambda j, i, c_ref: (j, i, 0))
    grid_spec = pltpu.PrefetchScalarGridSpec(
        num_scalar_prefetch=1, grid=(4, rows // tb),
        in_specs=[pl.BlockSpec((1, 1, tb, width), lambda j, i, c_ref: (j, c_ref[0], i, 0)), blk],
        out_specs=[blk, blk])
    return pl.pallas_call(
        body, name=name, grid_spec=grid_spec,
        out_shape=[jax.ShapeDtypeStruct((4, rows, width), f32), jax.ShapeDtypeStruct((4, rows, width), bf16)],
        compiler_params=_params(2, _vmem_for(4 * tb * width, 2 * tb * width, 4 * tb * width, 2 * tb * width)),
    )(c_me, g4, recv)


def _adam_shard(hsum, recv, chip, w, m, v, *, name):
    _, rows, width = w.shape
    tb = _row_block(rows)

    def body(j_ref, h_ref, r_ref, w_ref, m_ref, v_ref, g_out, d_out, m_out, v_out):
        g = ((h_ref[0] + r_ref[0].astype(f32)) + r_ref[1].astype(f32)) + r_ref[2].astype(f32)
        delta, mn, vn = _adam_math(w_ref[0], g, m_ref[0], v_ref[0])
        g_out[0] = g
        d_out[0] = delta
        m_out[0] = mn
        v_out[0] = vn

    blk = pl.BlockSpec((1, tb, width), lambda i, j_ref: (0, i, 0))
    grid_spec = pltpu.PrefetchScalarGridSpec(
        num_scalar_prefetch=1, grid=(rows // tb,),
        in_specs=[pl.BlockSpec((1, tb, width), lambda i, j_ref: (j_ref[0], i, 0)),
                  pl.BlockSpec((3, tb, width), lambda i, j_ref: (0, i, 0)), blk, blk, blk],
        out_specs=[blk, blk, blk, blk])
    return pl.pallas_call(
        body, name=name, grid_spec=grid_spec, out_shape=[jax.ShapeDtypeStruct(w.shape, f32)] * 4,
        compiler_params=_params(1, _vmem_for(*[4 * tb * width] * 8, 6 * tb * width)),
    )(chip, hsum, recv, w, m, v)


def _sum_shard(hsum, recv, chip, *, name):
    _, rows, width = hsum.shape
    tb = _row_block(rows)

    def body(j_ref, h_ref, r_ref, g_out):
        g_out[...] = ((h_ref[0] + r_ref[0].astype(f32)) + r_ref[1].astype(f32)) + r_ref[2].astype(f32)

    grid_spec = pltpu.PrefetchScalarGridSpec(
        num_scalar_prefetch=1, grid=(rows // tb,),
        in_specs=[pl.BlockSpec((1, tb, width), lambda i, j_ref: (j_ref[0], i, 0)),
                  pl.BlockSpec((3, tb, width), lambda i, j_ref: (0, i, 0))],
        out_specs=pl.BlockSpec((tb, width), lambda i, j_ref: (i, 0)))
    return pl.pallas_call(body, name=name, grid_spec=grid_spec, out_shape=jax.ShapeDtypeStruct((rows, width), f32),
                          compiler_params=_params(1, _vmem_for(*[4 * tb * width] * 2, 6 * tb * width)))(chip, hsum, recv)


def _adam_columns(g, w, m, v, *, name):
    cols, _, rows = w.shape
    tb = cols // 2

    def body(g_ref, w_ref, m_ref, v_ref, d_out, m_out, v_out):
        delta, mn, vn = _adam_math(w_ref[...], g_ref[...], m_ref[...], v_ref[...])
        d_out[...] = delta
        m_out[...] = mn
        v_out[...] = vn

    blk = pl.BlockSpec((tb, 1, rows), lambda i: (i, 0, 0))
    return pl.pallas_call(
        body, name=name, grid=(cols // tb,), in_specs=[blk] * 4, out_specs=[blk] * 3,
        out_shape=[jax.ShapeDtypeStruct(w.shape, f32)] * 3,
        compiler_params=_params(1, _vmem_for(*[4 * tb * rows] * 7)),
    )(g, w, m, v)


R_SMALL = 8 + 8 * N_DEV
_SMALL_LANES = {"gdn_norm_g": (0, DH), "gdn_A_log": (DH, DH + H), "gdn_dt_bias": (2 * DH, 2 * DH + H)}
_LOSS_LANE = 3 * DH


def _pack_small(dg1, dg2, dg3, dgn, dal, ddt, loss_p, dwa, dwg, dwf):
    def body(dg1_ref, dg2_ref, dg3_ref, dgn_ref, dal_ref, ddt_ref, loss_ref, dwa_ref, dwg_ref, dwf_ref, o_ref):
        def total(ref):
            return jnp.sum(ref[...], axis=0, keepdims=True)

        o_ref[...] = jnp.zeros_like(o_ref)
        o_ref[0:1, :] = total(dg1_ref)
        o_ref[1:2, :] = total(dg2_ref)
        o_ref[2:3, :] = total(dg3_ref)
        o_ref[3:4, 0:DH] = total(dgn_ref)
        o_ref[3:4, DH:2 * DH] = total(dal_ref)
        o_ref[3:4, 2 * DH:3 * DH] = total(ddt_ref)
        o_ref[3:4, 3 * DH:4 * DH] = total(loss_ref)
        for d in range(N_DEV):
            base = 8 + 8 * d
            o_ref[base:base + 3, 0:128] = dwa_ref[0:3, 128 * d:128 * (d + 1)]
            o_ref[base:base + 4, 128:512] = dwg_ref[0:4, 384 * d:384 * (d + 1)]
            o_ref[base + 4:base + 7, 0:704] = dwf_ref[0:3, 704 * d:704 * (d + 1)]

    return pl.pallas_call(body, name="pack_small", out_shape=jax.ShapeDtypeStruct((R_SMALL, D), f32))(
        dg1, dg2, dg3, dgn, dal, ddt, loss_p, dwa, dwg, dwf)


_SMALL = ("norm_mix_g", "norm_ffn_g", "norm_final_g", "gdn_norm_g", "gdn_A_log", "gdn_dt_bias",
          "conv_a_w", "gdn_conv_w", "ffn_conv_w")


def _adam_small(gath, me, w, m, v):
    arrays = [t[n] for n in _SMALL for t in (w, m, v)]

    def body(me_ref, ga_ref, gb_ref, *refs):
        ins, outs = refs[:len(arrays)], refs[len(arrays):]
        ga, gb = ga_ref[0], gb_ref[0]
        for s in range(1, N_DEV):
            ga = ga + ga_ref[s]
            gb = gb + gb_ref[s]
        grads = {"norm_mix_g": ga[0:1, :], "norm_ffn_g": ga[1:2, :], "norm_final_g": ga[2:3, :],
                 "conv_a_w": gb[0:3, 0:128], "gdn_conv_w": gb[0:4, 128:512], "ffn_conv_w": gb[4:7, 0:704]}
        for n, (lo, hi) in _SMALL_LANES.items():
            grads[n] = ga[3:4, lo:hi]
        for i, n in enumerate(_SMALL):
            three_d = len(w[n].shape) == 3
            wv, mv, vv = (r[0] if three_d else r[...] for r in ins[3 * i:3 * i + 3])
            delta, mn, vn = _adam_math(wv, grads[n], mv, vv)
            for o_ref, val in zip(outs[4 * i:4 * i + 4], (grads[n], delta, mn, vn)):
                if three_d:
                    o_ref[0] = val
                else:
                    o_ref[...] = val
        outs[-1][...] = ga[3:4, _LOSS_LANE:_LOSS_LANE + 1]

    def whole(shape):
        return pl.BlockSpec(shape, lambda i, me_ref: (0,) * len(shape))

    grid_spec = pltpu.PrefetchScalarGridSpec(
        num_scalar_prefetch=1, grid=(1,),
        in_specs=[pl.BlockSpec((N_DEV, 8, D), lambda i, me_ref: (0, 0, 0)),
                  pl.BlockSpec((N_DEV, 8, D), lambda i, me_ref: (0, 1 + me_ref[0], 0))] + [whole(a.shape) for a in arrays],
        out_specs=[whole(w[n].shape) for n in _SMALL for _ in range(4)] + [whole((1, 1))])
    res = pl.pallas_call(
        body, name="adam_small", grid_spec=grid_spec,
        out_shape=[jax.ShapeDtypeStruct(w[n].shape, f32) for n in _SMALL for _ in range(4)]
        + [jax.ShapeDtypeStruct((1, 1), f32)],
        compiler_params=_params(1),
    )(me, gath, gath, *arrays)
    return {n: tuple(res[4 * i:4 * i + 4]) for i, n in enumerate(_SMALL)}, res[-1]


def _adam_math(w, g, m, v):
    m = ADAM_B1 * m + (1.0 - ADAM_B1) * g
    v = ADAM_B2 * v + (1.0 - ADAM_B2) * jnp.square(g)
    m_hat = m / (1.0 - ADAM_B1 ** ADAM_STEP)
    v_hat = v / (1.0 - ADAM_B2 ** ADAM_STEP)
    delta = -ADAM_LR * (m_hat / (jnp.sqrt(v_hat) + ADAM_EPS) + ADAM_WD * w)
    return delta, m, v


_WEIGHTS = ("norm_mix_g", "w_in", "conv_a_w", "gdn_conv_w", "gdn_A_log", "gdn_dt_bias", "gdn_norm_g", "w_a_out",
            "w_b_out", "w_o", "norm_ffn_g", "w_up", "ffn_conv_w", "w_down", "norm_final_g")
_CONVS = ("conv_a_w", "gdn_conv_w", "ffn_conv_w")


class _StepExchanges:
    def __init__(self, wts, mom, var, c_me, chip):
        self.wts, self.mom, self.var, self.c_me, self.chip = wts, mom, var, c_me, chip
        self.results = {}

    def gather_first(self):
        return _gather_exchange([self.wts["w_in"][0].astype(bf16)] + [self.wts[n][0] for n in _CONVS])

    def finish_first(self, gathered):
        g_in, gc_a, gc_g, gc_f = gathered
        w1, w2 = _cols_to_matrices(g_in, _IN_RANGES, (NW1, 128), name="relay_w_in")
        return {"w1": w1, "w2": w2, "conv_a_w": gc_a.transpose(1, 0, 2).reshape(3, D),
                "gdn_conv_w": gc_g.transpose(1, 0, 2).reshape(4, 3 * D),
                "ffn_conv_w": gc_f.transpose(1, 0, 2).reshape(3, 2 * DFF)}

    def gather_rest(self):
        return _gather_direct_exchange([self.wts[n][0].astype(bf16) for n in _REST])

    def finish_gather(self, gathered):
        g_up, g_a, g_b, g_o, g_down = gathered
        return {"w_up": g_up.reshape(2 * DFF, D), "w_a_out": g_a.reshape(D, D), "w_b_out": g_b.reshape(D, D),
                "w_o": g_o.reshape(D, D), "w_down": g_down.reshape(DFF, D)}

    def reduce_halves(self, names, grads):
        blocks = []
        for n in names:
            if n == "w_in":
                g = _transposed_matrices_to_blocks([grads["w1"], grads["w2"]], _IN_RANGES, R_IN, name="relay_dw_in")
                blocks.append(g.reshape(4, 2, R_IN, D))
            else:
                blocks.append(grads[n].reshape(4, 2, *self.wts[n].shape[1:]))
        return _sibling_exchange([_half_bf16(g, 1 - self.c_me, name="rs_half_" + n) for n, g in zip(names, blocks)]), blocks

    def reduce_sums(self, names, blocks, recv):
        sums = [_pair_sum(g, r, self.c_me, name="rs_sum_" + n) for n, g, r in zip(names, blocks, recv)]
        return _chips_exchange([s[1] for s in sums]), [s[0] for s in sums]

    def finish_reduce(self, names, sums, recv):
        for n, s, r in zip(names, sums, recv):
            if n == "w_in":
                g = _sum_shard(s, r, self.chip, name="rs_total_w_in")[:, None, :]
                w, m, v = (jnp.transpose(t[n], (2, 0, 1)) for t in (self.wts, self.mom, self.var))
                res = (g, *_adam_columns(g, w, m, v, name="adam_w_in"))
                self.results[n] = tuple(jnp.transpose(a, (1, 2, 0)) for a in res)
            else:
                self.results[n] = _adam_shard(s, r, self.chip, self.wts[n], self.mom[n], self.var[n], name="adam_" + n)


def kernel(x, norm_mix_g, w_in, conv_a_w, gdn_conv_w, gdn_A_log, gdn_dt_bias, gdn_norm_g, w_a_out, w_b_out, w_o, norm_ffn_g, w_up, ffn_conv_w, w_down, norm_final_g, loss_target, m_norm_mix_g, m_w_in, m_conv_a_w, m_gdn_conv_w, m_gdn_A_log, m_gdn_dt_bias, m_gdn_norm_g, m_w_a_out, m_w_b_out, m_w_o, m_norm_ffn_g, m_w_up, m_ffn_conv_w, m_w_down, m_norm_final_g, v_norm_mix_g, v_w_in, v_conv_a_w, v_gdn_conv_w, v_gdn_A_log, v_gdn_dt_bias, v_gdn_norm_g, v_w_a_out, v_w_b_out, v_w_o, v_norm_ffn_g, v_w_up, v_ffn_conv_w, v_w_down, v_norm_final_g):
    wts = dict(zip(_WEIGHTS, (norm_mix_g, w_in, conv_a_w, gdn_conv_w, gdn_A_log, gdn_dt_bias, gdn_norm_g, w_a_out,
                              w_b_out, w_o, norm_ffn_g, w_up, ffn_conv_w, w_down, norm_final_g)))
    mom = dict(zip(_WEIGHTS, (m_norm_mix_g, m_w_in, m_conv_a_w, m_gdn_conv_w, m_gdn_A_log, m_gdn_dt_bias,
                              m_gdn_norm_g, m_w_a_out, m_w_b_out, m_w_o, m_norm_ffn_g, m_w_up, m_ffn_conv_w,
                              m_w_down, m_norm_final_g)))
    var = dict(zip(_WEIGHTS, (v_norm_mix_g, v_w_in, v_conv_a_w, v_gdn_conv_w, v_gdn_A_log, v_gdn_dt_bias,
                              v_gdn_norm_g, v_w_a_out, v_w_b_out, v_w_o, v_norm_ffn_g, v_w_up, v_ffn_conv_w,
                              v_w_down, v_norm_final_g)))
    cx, cy, cc = lax.axis_index("x"), lax.axis_index("y"), lax.axis_index("c")
    c_me = jnp.reshape(cc, (1,)).astype(jnp.int32)
    chip = jnp.reshape(2 * cx + cy, (1,)).astype(jnp.int32)
    me = jnp.reshape(4 * cx + 2 * cy + cc, (1,)).astype(jnp.int32)

    def with_up_transposed(t):
        return {**t, "w_up": jnp.swapaxes(t["w_up"], 1, 2)}

    comm = _StepExchanges(with_up_transposed(wts), with_up_transposed(mom), with_up_transposed(var), c_me, chip)
    replicated = {n: wts[n] for n in ("norm_mix_g", "norm_ffn_g", "norm_final_g", "gdn_norm_g", "gdn_A_log", "gdn_dt_bias")}
    loss_p, dx, grads = _local_step(x[0], loss_target[0], replicated, comm)
    res = comm.results
    res["w_up"] = tuple(jnp.swapaxes(a, 1, 2) for a in res["w_up"])

    small = _pack_small(grads["norm_mix_g"], grads["norm_ffn_g"], grads["norm_final_g"], grads["gdn_norm_g"],
                        grads["gdn_A_log"], grads["gdn_dt_bias"], loss_p, grads["conv_a_w"], grads["gdn_conv_w"],
                        grads["ffn_conv_w"])
    (small_all,) = _run_exchange(_gather_exchange([small]), name="ag_small")

    def raw(t):
        return {n: t[n].reshape(1, D) if n == "norm_final_g" else t[n] for n in _SMALL}

    res_small, loss = _adam_small(small_all, me, raw(wts), raw(mom), raw(var))
    for n in _SMALL:
        res[n] = tuple(a.reshape(wts[n].shape) for a in res_small[n])
    outs = [[res[n][i] for n in _WEIGHTS] for i in range(4)]
    return (loss.reshape(()), dx[None], *outs[0], *outs[1], *outs[2], *outs[3])
```
